```python
import math
import jax, jax.numpy as jnp
from jax import lax
import numpy as np

D_MODEL = 2048
BATCH = 8
SEQ = 4096
DEPTH = 2

N_META = 16
D_MIX = D_MODEL
CONV_W = D_MIX // 4
CONV_K = 31
HEAD_DIM = 64
N_HEADS = (D_MIX // 2) // HEAD_DIM
N_KV = 4
GROUP = N_HEADS // N_KV
ATT_W = N_HEADS * HEAD_DIM
KV_W = N_KV * HEAD_DIM
WINDOW = 128
BLOCK = 128
ROT_DIM = HEAD_DIM // 4
ROPE_THETA = 500000.0
LRU_W = D_MIX // 4
LRU_HEADS = 8
LRU_HEAD_DIM = LRU_W // LRU_HEADS
LRU_CONV_K = 4
LRU_C = 8.0
IN_WIDTHS = (CONV_W, CONV_W, CONV_W,
             ATT_W, KV_W, KV_W, ATT_W,
             LRU_W, LRU_W)
IN_TOTAL = sum(IN_WIDTHS)
OUT_IN = CONV_W + ATT_W + LRU_W
LN_EPS = 1e-5
DEEPNORM_ALPHA = (2.0 * DEPTH) ** 0.25
DEEPNORM_BETA = (8.0 * DEPTH) ** -0.25
NEG_INF = -1e30

kernel_name = "hymba_style_conv_swa_rglru_deepnorm"


def _layer_norm(x, g, b):
    xf = x.astype(jnp.float32)
    mu = jnp.mean(xf, axis=-1, keepdims=True)
    var = jnp.mean(jnp.square(xf - mu), axis=-1, keepdims=True)
    y = (xf - mu) * lax.rsqrt(var + LN_EPS)
    return (y * g.astype(jnp.float32) + b.astype(jnp.float32)).astype(x.dtype)


def _causal_depthwise_conv(x, w, b):
    k = w.shape[0]
    y = lax.conv_general_dilated(
        x, w[:, None, :].astype(x.dtype), window_strides=(1,), padding=[(k - 1, 0)],
        dimension_numbers=("NWC", "WIO", "NWC"), feature_group_count=x.shape[-1])
    return y + b


def _partial_rotary(x, pos):
    half = ROT_DIM // 2
    inv_freq = ROPE_THETA ** (-jnp.arange(half, dtype=jnp.float32) / half)
    ang = pos.astype(jnp.float32)[:, None] * inv_freq[None, :]
    cos = jnp.cos(ang)[None, :, None, :]
    sin = jnp.sin(ang)[None, :, None, :]
    x1 = x[..., :half].astype(jnp.float32)
    x2 = x[..., half:ROT_DIM].astype(jnp.float32)
    rot = jnp.concatenate([x1 * cos - x2 * sin, x2 * cos + x1 * sin], axis=-1).astype(x.dtype)
    return jnp.concatenate([rot, x[..., ROT_DIM:]], axis=-1)


def _sliding_window_sink_attention(q, k, v, sinks):
    B, L, H, Dh = q.shape
    S = L - N_META
    nblk = S // BLOCK
    scale = Dh ** -0.5
    sink_kg = sinks.astype(jnp.float32).reshape(N_KV, GROUP)

    qm, qr = q[:, :N_META], q[:, N_META:]
    km, kr = k[:, :N_META], k[:, N_META:]
    vm, vr = v[:, :N_META], v[:, N_META:]

    qb = qr.reshape(B, nblk, BLOCK, N_KV, GROUP, Dh)
    kb = kr.reshape(B, nblk, BLOCK, N_KV, Dh)
    vb = vr.reshape(B, nblk, BLOCK, N_KV, Dh)
    pad = ((0, 0), (1, 0), (0, 0), (0, 0), (0, 0))
    k_band = jnp.concatenate([jnp.pad(kb, pad)[:, :-1], kb], axis=2)
    v_band = jnp.concatenate([jnp.pad(vb, pad)[:, :-1], vb], axis=2)

    s_meta = jnp.einsum("bnqkgd,bmkd->bkgnqm", qb, km).astype(jnp.float32) * scale
    s_band = jnp.einsum("bnqkgd,bnjkd->bkgnqj", qb, k_band).astype(jnp.float32) * scale
    qi = jnp.arange(BLOCK)[:, None]
    kj = jnp.arange(2 * BLOCK)[None, :]
    diff = BLOCK + qi - kj
    in_window = (diff >= 0) & (diff < WINDOW)
    blk = jnp.arange(nblk)[:, None, None]
    valid = in_window[None] & ((kj[None] >= BLOCK) | (blk > 0))
    s_band = jnp.where(valid[None, None, None], s_band, NEG_INF)
    sink_col = jnp.broadcast_to(sink_kg[None, :, :, None, None, None], s_meta.shape[:-1] + (1,))
    probs = jax.nn.softmax(jnp.concatenate([s_meta, s_band, sink_col], axis=-1), axis=-1)
    p_meta = probs[..., :N_META].astype(v.dtype)
    p_band = probs[..., N_META:N_META + 2 * BLOCK].astype(v.dtype)
    out_r = (jnp.einsum("bkgnqm,bmkd->bnqkgd", p_meta, vm)
             + jnp.einsum("bkgnqj,bnjkd->bnqkgd", p_band, v_band)).reshape(B, S, H, Dh)

    qmg = qm.reshape(B, N_META, N_KV, GROUP, Dh)
    s_mm = jnp.einsum("bqkgd,bmkd->bkgqm", qmg, km).astype(jnp.float32) * scale
    causal = jnp.tril(jnp.ones((N_META, N_META), dtype=bool))
    s_mm = jnp.where(causal[None, None, None], s_mm, NEG_INF)
    sink_mm = jnp.broadcast_to(sink_kg[None, :, :, None, None], s_mm.shape[:-1] + (1,))
    p_mm = jax.nn.softmax(jnp.concatenate([s_mm, sink_mm], axis=-1), axis=-1)[..., :N_META]
    out_m = jnp.einsum("bkgqm,bmkd->bqkgd", p_mm.astype(v.dtype), vm).reshape(B, N_META, H, Dh)

    return jnp.concatenate([out_m, out_r], axis=1)


def _rg_lru(x, w_a, b_a, w_x, b_x, lam):
    B, L, C = x.shape
    xh = x.reshape(B, L, LRU_HEADS, LRU_HEAD_DIM)
    r = jax.nn.sigmoid(jnp.einsum("blhi,hij->blhj", xh, w_a).reshape(B, L, C) + b_a)
    i = jax.nn.sigmoid(jnp.einsum("blhi,hij->blhj", xh, w_x).reshape(B, L, C) + b_x)
    log_a = -LRU_C * r.astype(jnp.float32) * jax.nn.softplus(-lam.astype(jnp.float32))
    a = jnp.exp(log_a)
    u = jnp.sqrt(-jnp.expm1(2.0 * log_a)) * (i * x).astype(jnp.float32)

    def combine(c1, c2):
        a1, b1 = c1
        a2, b2 = c2
        return a1 * a2, a2 * b1 + b2

    _, h = lax.associative_scan(combine, (a, u), axis=1)
    return h.astype(x.dtype)


def _hybrid_layer(h, pos, w_in, conv_dw_w, conv_dw_b, conv_ln_g, conv_ln_b, conv_pw_w, conv_pw_b,
                  attn_sinks, lru_conv_w, lru_conv_b, lru_wa, lru_ba, lru_wx, lru_bx, lru_lambda,
                  w_out, ln_post_g, ln_post_b):
    B, L, _ = h.shape
    proj = h @ w_in
    split_pts = [int(s) for s in np.cumsum(IN_WIDTHS)[:-1]]
    (c_val, c_glu, c_gate, q, k, v, a_gate, r_x, r_gate) = jnp.split(proj, split_pts, axis=-1)

    c = c_val * jax.nn.sigmoid(c_glu)
    c = _causal_depthwise_conv(c, conv_dw_w, conv_dw_b)
    c = jax.nn.silu(_layer_norm(c, conv_ln_g, conv_ln_b))
    c = c @ conv_pw_w + conv_pw_b
    y_conv = c * jax.nn.silu(c_gate)

    q = _partial_rotary(q.reshape(B, L, N_HEADS, HEAD_DIM), pos)
    k = _partial_rotary(k.reshape(B, L, N_KV, HEAD_DIM), pos)
    v = v.reshape(B, L, N_KV, HEAD_DIM)
    att = _sliding_window_sink_attention(q, k, v, attn_sinks).reshape(B, L, ATT_W)
    y_attn = att * jax.nn.silu(a_gate)

    r = _causal_depthwise_conv(r_x, lru_conv_w, lru_conv_b)
    r = _rg_lru(r, lru_wa, lru_ba, lru_wx, lru_bx, lru_lambda)
    y_lru = r * jax.nn.silu(r_gate)

    mixed = jnp.concatenate([y_conv, y_attn, y_lru], axis=-1) @ w_out
    return _layer_norm(DEEPNORM_ALPHA * h + mixed, ln_post_g, ln_post_b)


def _fwd_setup_inputs(seed: int = 0) -> dict:
    key = jax.random.key(seed)
    ks = jax.random.split(key, 24)
    f32 = jnp.float32
    nrm = lambda k, shape, s: jax.random.normal(k, shape, f32) * s
    u = jax.random.uniform(ks[17], (DEPTH, LRU_W), f32, 0.9, 0.999)
    s = u ** (1.0 / LRU_C)
    lam = jnp.log(s) - jnp.log1p(-s)
    return {
        "x": nrm(ks[0], (BATCH, SEQ, D_MODEL), 1.0),
        "meta_tokens": nrm(ks[1], (N_META, D_MODEL), 1.0),
        "ln_in_g": 1.0 + nrm(ks[2], (D_MODEL,), 0.02),
        "ln_in_b": nrm(ks[3], (D_MODEL,), 0.02),
        "w_in": nrm(ks[4], (DEPTH, D_MODEL, IN_TOTAL), D_MODEL ** -0.5),
        "conv_dw_w": nrm(ks[5], (DEPTH, CONV_K, CONV_W), CONV_K ** -0.5),
        "conv_dw_b": nrm(ks[6], (DEPTH, CONV_W), 0.01),
        "conv_ln_g": 1.0 + nrm(ks[7], (DEPTH, CONV_W), 0.02),
        "conv_ln_b": nrm(ks[8], (DEPTH, CONV_W), 0.02),
        "conv_pw_w": nrm(ks[9], (DEPTH, CONV_W, CONV_W), DEEPNORM_BETA * CONV_W ** -0.5),
        "conv_pw_b": nrm(ks[10], (DEPTH, CONV_W), 0.01),
        "attn_sinks": nrm(ks[11], (DEPTH, N_HEADS), 0.5),
        "lru_conv_w": nrm(ks[12], (DEPTH, LRU_CONV_K, LRU_W), LRU_CONV_K ** -0.5),
        "lru_conv_b": nrm(ks[13], (DEPTH, LRU_W), 0.01),
        "lru_wa": nrm(ks[14], (DEPTH, LRU_HEADS, LRU_HEAD_DIM, LRU_HEAD_DIM), LRU_HEAD_DIM ** -0.5),
        "lru_ba": nrm(ks[15], (DEPTH, LRU_W), 0.01),
        "lru_wx": nrm(ks[16], (DEPTH, LRU_HEADS, LRU_HEAD_DIM, LRU_HEAD_DIM), LRU_HEAD_DIM ** -0.5),
        "lru_bx": nrm(ks[18], (DEPTH, LRU_W), 0.01),
        "lru_lambda": lam,
        "w_out": nrm(ks[19], (DEPTH, OUT_IN, D_MODEL), DEEPNORM_BETA * OUT_IN ** -0.5),
        "ln_post_g": 1.0 + nrm(ks[20], (DEPTH, D_MODEL), 0.02),
        "ln_post_b": nrm(ks[21], (DEPTH, D_MODEL), 0.02),
    }


def _fwd_reference(x, meta_tokens, ln_in_g, ln_in_b, w_in, conv_dw_w, conv_dw_b, conv_ln_g, conv_ln_b,
              conv_pw_w, conv_pw_b, attn_sinks, lru_conv_w, lru_conv_b, lru_wa, lru_ba, lru_wx,
              lru_bx, lru_lambda, w_out, ln_post_g, ln_post_b):
    B = x.shape[0]
    meta = jnp.broadcast_to(meta_tokens[None].astype(x.dtype), (B, N_META, x.shape[-1]))
    h = jnp.concatenate([meta, x], axis=1)
    h = _layer_norm(h, ln_in_g, ln_in_b)
    pos = jnp.arange(h.shape[1], dtype=jnp.int32)
    for l in range(DEPTH):
        h = _hybrid_layer(h, pos, w_in[l], conv_dw_w[l], conv_dw_b[l], conv_ln_g[l], conv_ln_b[l],
                          conv_pw_w[l], conv_pw_b[l], attn_sinks[l], lru_conv_w[l], lru_conv_b[l],
                          lru_wa[l], lru_ba[l], lru_wx[l], lru_bx[l], lru_lambda[l],
                          w_out[l], ln_post_g[l], ln_post_b[l])
    return h[:, N_META:]


import jax as _jax
import jax.numpy as _jnp

TWIN_FORMAT = 'train_step'
FWD_PARAMS = ['x', 'meta_tokens', 'ln_in_g', 'ln_in_b', 'w_in', 'conv_dw_w', 'conv_dw_b', 'conv_ln_g', 'conv_ln_b', 'conv_pw_w', 'conv_pw_b', 'attn_sinks', 'lru_conv_w', 'lru_conv_b', 'lru_wa', 'lru_ba', 'lru_wx', 'lru_bx', 'lru_lambda', 'w_out', 'ln_post_g', 'ln_post_b']
TWIN_WEIGHTS = ['meta_tokens', 'ln_in_g', 'ln_in_b', 'w_in', 'conv_dw_w', 'conv_dw_b', 'conv_ln_g', 'conv_ln_b', 'conv_pw_w', 'conv_pw_b', 'attn_sinks', 'lru_conv_w', 'lru_conv_b', 'lru_wa', 'lru_ba', 'lru_wx', 'lru_bx', 'lru_lambda', 'w_out', 'ln_post_g', 'ln_post_b']
TWIN_DIFF_INPUT = 'x'
TWIN_INPUTS = ['x', 'meta_tokens', 'ln_in_g', 'ln_in_b', 'w_in', 'conv_dw_w', 'conv_dw_b', 'conv_ln_g', 'conv_ln_b', 'conv_pw_w', 'conv_pw_b', 'attn_sinks', 'lru_conv_w', 'lru_conv_b', 'lru_wa', 'lru_ba', 'lru_wx', 'lru_bx', 'lru_lambda', 'w_out', 'ln_post_g', 'ln_post_b', 'loss_target', 'm_meta_tokens', 'm_ln_in_g', 'm_ln_in_b', 'm_w_in', 'm_conv_dw_w', 'm_conv_dw_b', 'm_conv_ln_g', 'm_conv_ln_b', 'm_conv_pw_w', 'm_conv_pw_b', 'm_attn_sinks', 'm_lru_conv_w', 'm_lru_conv_b', 'm_lru_wa', 'm_lru_ba', 'm_lru_wx', 'm_lru_bx', 'm_lru_lambda', 'm_w_out', 'm_ln_post_g', 'm_ln_post_b', 'v_meta_tokens', 'v_ln_in_g', 'v_ln_in_b', 'v_w_in', 'v_conv_dw_w', 'v_conv_dw_b', 'v_conv_ln_g', 'v_conv_ln_b', 'v_conv_pw_w', 'v_conv_pw_b', 'v_attn_sinks', 'v_lru_conv_w', 'v_lru_conv_b', 'v_lru_wa', 'v_lru_ba', 'v_lru_wx', 'v_lru_bx', 'v_lru_lambda', 'v_w_out', 'v_ln_post_g', 'v_ln_post_b']
TWIN_OUTPUTS = ['loss', 'grad_x', 'grad_meta_tokens', 'grad_ln_in_g', 'grad_ln_in_b', 'grad_w_in', 'grad_conv_dw_w', 'grad_conv_dw_b', 'grad_conv_ln_g', 'grad_conv_ln_b', 'grad_conv_pw_w', 'grad_conv_pw_b', 'grad_attn_sinks', 'grad_lru_conv_w', 'grad_lru_conv_b', 'grad_lru_wa', 'grad_lru_ba', 'grad_lru_wx', 'grad_lru_bx', 'grad_lru_lambda', 'grad_w_out', 'grad_ln_post_g', 'grad_ln_post_b', 'delta_meta_tokens', 'delta_ln_in_g', 'delta_ln_in_b', 'delta_w_in', 'delta_conv_dw_w', 'delta_conv_dw_b', 'delta_conv_ln_g', 'delta_conv_ln_b', 'delta_conv_pw_w', 'delta_conv_pw_b', 'delta_attn_sinks', 'delta_lru_conv_w', 'delta_lru_conv_b', 'delta_lru_wa', 'delta_lru_ba', 'delta_lru_wx', 'delta_lru_bx', 'delta_lru_lambda', 'delta_w_out', 'delta_ln_post_g', 'delta_ln_post_b', 'new_m_meta_tokens', 'new_m_ln_in_g', 'new_m_ln_in_b', 'new_m_w_in', 'new_m_conv_dw_w', 'new_m_conv_dw_b', 'new_m_conv_ln_g', 'new_m_conv_ln_b', 'new_m_conv_pw_w', 'new_m_conv_pw_b', 'new_m_attn_sinks', 'new_m_lru_conv_w', 'new_m_lru_conv_b', 'new_m_lru_wa', 'new_m_lru_ba', 'new_m_lru_wx', 'new_m_lru_bx', 'new_m_lru_lambda', 'new_m_w_out', 'new_m_ln_post_g', 'new_m_ln_post_b', 'new_v_meta_tokens', 'new_v_ln_in_g', 'new_v_ln_in_b', 'new_v_w_in', 'new_v_conv_dw_w', 'new_v_conv_dw_b', 'new_v_conv_ln_g', 'new_v_conv_ln_b', 'new_v_conv_pw_w', 'new_v_conv_pw_b', 'new_v_attn_sinks', 'new_v_lru_conv_w', 'new_v_lru_conv_b', 'new_v_lru_wa', 'new_v_lru_ba', 'new_v_lru_wx', 'new_v_lru_bx', 'new_v_lru_lambda', 'new_v_w_out', 'new_v_ln_post_g', 'new_v_ln_post_b']
TWIN_LEAF_KINDS = {'loss': 'loss', 'grad_x': 'grad_x', 'grad_meta_tokens': 'grad_w', 'grad_ln_in_g': 'grad_w', 'grad_ln_in_b': 'grad_w', 'grad_w_in': 'grad_w', 'grad_conv_dw_w': 'grad_w', 'grad_conv_dw_b': 'grad_w', 'grad_conv_ln_g': 'grad_w', 'grad_conv_ln_b': 'grad_w', 'grad_conv_pw_w': 'grad_w', 'grad_conv_pw_b': 'grad_w', 'grad_attn_sinks': 'grad_w', 'grad_lru_conv_w': 'grad_w', 'grad_lru_conv_b': 'grad_w', 'grad_lru_wa': 'grad_w', 'grad_lru_ba': 'grad_w', 'grad_lru_wx': 'grad_w', 'grad_lru_bx': 'grad_w', 'grad_lru_lambda': 'grad_w', 'grad_w_out': 'grad_w', 'grad_ln_post_g': 'grad_w', 'grad_ln_post_b': 'grad_w', 'delta_meta_tokens': 'delta_w', 'delta_ln_in_g': 'delta_w', 'delta_ln_in_b': 'delta_w', 'delta_w_in': 'delta_w', 'delta_conv_dw_w': 'delta_w', 'delta_conv_dw_b': 'delta_w', 'delta_conv_ln_g': 'delta_w', 'delta_conv_ln_b': 'delta_w', 'delta_conv_pw_w': 'delta_w', 'delta_conv_pw_b': 'delta_w', 'delta_attn_sinks': 'delta_w', 'delta_lru_conv_w': 'delta_w', 'delta_lru_conv_b': 'delta_w', 'delta_lru_wa': 'delta_w', 'delta_lru_ba': 'delta_w', 'delta_lru_wx': 'delta_w', 'delta_lru_bx': 'delta_w', 'delta_lru_lambda': 'delta_w', 'delta_w_out': 'delta_w', 'delta_ln_post_g': 'delta_w', 'delta_ln_post_b': 'delta_w', 'new_m_meta_tokens': 'new_m', 'new_m_ln_in_g': 'new_m', 'new_m_ln_in_b': 'new_m', 'new_m_w_in': 'new_m', 'new_m_conv_dw_w': 'new_m', 'new_m_conv_dw_b': 'new_m', 'new_m_conv_ln_g': 'new_m', 'new_m_conv_ln_b': 'new_m', 'new_m_conv_pw_w': 'new_m', 'new_m_conv_pw_b': 'new_m', 'new_m_attn_sinks': 'new_m', 'new_m_lru_conv_w': 'new_m', 'new_m_lru_conv_b': 'new_m', 'new_m_lru_wa': 'new_m', 'new_m_lru_ba': 'new_m', 'new_m_lru_wx': 'new_m', 'new_m_lru_bx': 'new_m', 'new_m_lru_lambda': 'new_m', 'new_m_w_out': 'new_m', 'new_m_ln_post_g': 'new_m', 'new_m_ln_post_b': 'new_m', 'new_v_meta_tokens': 'new_v', 'new_v_ln_in_g': 'new_v', 'new_v_ln_in_b': 'new_v', 'new_v_w_in': 'new_v', 'new_v_conv_dw_w': 'new_v', 'new_v_conv_dw_b': 'new_v', 'new_v_conv_ln_g': 'new_v', 'new_v_conv_ln_b': 'new_v', 'new_v_conv_pw_w': 'new_v', 'new_v_conv_pw_b': 'new_v', 'new_v_attn_sinks': 'new_v', 'new_v_lru_conv_w': 'new_v', 'new_v_lru_conv_b': 'new_v', 'new_v_lru_wa': 'new_v', 'new_v_lru_ba': 'new_v', 'new_v_lru_wx': 'new_v', 'new_v_lru_bx': 'new_v', 'new_v_lru_lambda': 'new_v', 'new_v_w_out': 'new_v', 'new_v_ln_post_g': 'new_v', 'new_v_ln_post_b': 'new_v'}


def _forward(args):
    return _fwd_reference(*[args[k] for k in FWD_PARAMS])


def _output_shape():
    out = _jax.eval_shape(lambda: _forward(_fwd_setup_inputs(0)))
    return out.shape, out.dtype

N_MICROBATCH = 1
ADAM_LR = 0.001
ADAM_B1 = 0.9
ADAM_B2 = 0.999
ADAM_EPS = 1e-08
ADAM_WD = 0.01
ADAM_STEP = 10
PER_EXAMPLE_BATCH_AXIS = {'x': 0, 'loss_target': 0}
SHARED_INPUTS = []
_WEIGHT_DTYPES = {'meta_tokens': _jnp.float32, 'ln_in_g': _jnp.float32, 'ln_in_b': _jnp.float32, 'w_in': _jnp.float32, 'conv_dw_w': _jnp.float32, 'conv_dw_b': _jnp.float32, 'conv_ln_g': _jnp.float32, 'conv_ln_b': _jnp.float32, 'conv_pw_w': _jnp.float32, 'conv_pw_b': _jnp.float32, 'attn_sinks': _jnp.float32, 'lru_conv_w': _jnp.float32, 'lru_conv_b': _jnp.float32, 'lru_wa': _jnp.float32, 'lru_ba': _jnp.float32, 'lru_wx': _jnp.float32, 'lru_bx': _jnp.float32, 'lru_lambda': _jnp.float32, 'w_out': _jnp.float32, 'ln_post_g': _jnp.float32, 'ln_post_b': _jnp.float32}
MOMENT_SCALE = {'meta_tokens': 7.694035e-04, 'ln_in_g': 5.368339e-01, 'ln_in_b': 2.909311e-01, 'w_in': 6.849379e-03, 'conv_dw_w': 5.812405e-03, 'conv_dw_b': 1.251699e-02, 'conv_ln_g': 6.970353e-03, 'conv_ln_b': 6.835413e-03, 'conv_pw_w': 1.171769e-02, 'conv_pw_b': 2.564736e-02, 'attn_sinks': 4.956210e-04, 'lru_conv_w': 1.616884e-02, 'lru_conv_b': 2.093493e-01, 'lru_wa': 6.288263e-03, 'lru_ba': 3.790875e-03, 'lru_wx': 1.155772e-02, 'lru_bx': 5.221132e-03, 'lru_lambda': 7.363424e-03, 'w_out': 1.659306e-02, 'ln_post_g': 1.133004e+01, 'ln_post_b': 4.054418e-01}


def _to_microbatches(a, axis):
    t = _jnp.moveaxis(a, axis, 0)
    t = t.reshape((N_MICROBATCH, t.shape[0] // N_MICROBATCH) + t.shape[1:])
    return _jnp.moveaxis(t, 1, axis + 1)


def setup_inputs(seed: int = 0) -> dict:
    inp = _fwd_setup_inputs(seed)
    key = _jax.random.fold_in(_jax.random.key(seed), 7919)
    shape, _ = _output_shape()
    out = dict(inp)
    out["loss_target"] = _jax.random.normal(_jax.random.fold_in(key, 0), shape, _jnp.float32)
    for i, name in enumerate(TWIN_WEIGHTS):
        w = inp[name].astype(_jnp.float32)
        if MOMENT_SCALE is None:
            s = _jnp.sqrt(_jnp.mean(_jnp.square(w)) + 1e-30)
        else:
            s = MOMENT_SCALE[name]
        km, kv = _jax.random.split(_jax.random.fold_in(key, i + 1))
        out[name] = w
        out["m_" + name] = s * _jax.random.normal(km, w.shape, _jnp.float32)
        out["v_" + name] = (s * s) * _jax.random.uniform(kv, w.shape, _jnp.float32, 0.5, 1.5)
    if N_MICROBATCH > 1:
        for name, axis in PER_EXAMPLE_BATCH_AXIS.items():
            out[name] = _to_microbatches(out[name], axis)
    return {'x': out['x'], 'meta_tokens': out['meta_tokens'], 'ln_in_g': out['ln_in_g'], 'ln_in_b': out['ln_in_b'], 'w_in': out['w_in'], 'conv_dw_w': out['conv_dw_w'], 'conv_dw_b': out['conv_dw_b'], 'conv_ln_g': out['conv_ln_g'], 'conv_ln_b': out['conv_ln_b'], 'conv_pw_w': out['conv_pw_w'], 'conv_pw_b': out['conv_pw_b'], 'attn_sinks': out['attn_sinks'], 'lru_conv_w': out['lru_conv_w'], 'lru_conv_b': out['lru_conv_b'], 'lru_wa': out['lru_wa'], 'lru_ba': out['lru_ba'], 'lru_wx': out['lru_wx'], 'lru_bx': out['lru_bx'], 'lru_lambda': out['lru_lambda'], 'w_out': out['w_out'], 'ln_post_g': out['ln_post_g'], 'ln_post_b': out['ln_post_b'], 'loss_target': out['loss_target'], 'm_meta_tokens': out['m_meta_tokens'], 'm_ln_in_g': out['m_ln_in_g'], 'm_ln_in_b': out['m_ln_in_b'], 'm_w_in': out['m_w_in'], 'm_conv_dw_w': out['m_conv_dw_w'], 'm_conv_dw_b': out['m_conv_dw_b'], 'm_conv_ln_g': out['m_conv_ln_g'], 'm_conv_ln_b': out['m_conv_ln_b'], 'm_conv_pw_w': out['m_conv_pw_w'], 'm_conv_pw_b': out['m_conv_pw_b'], 'm_attn_sinks': out['m_attn_sinks'], 'm_lru_conv_w': out['m_lru_conv_w'], 'm_lru_conv_b': out['m_lru_conv_b'], 'm_lru_wa': out['m_lru_wa'], 'm_lru_ba': out['m_lru_ba'], 'm_lru_wx': out['m_lru_wx'], 'm_lru_bx': out['m_lru_bx'], 'm_lru_lambda': out['m_lru_lambda'], 'm_w_out': out['m_w_out'], 'm_ln_post_g': out['m_ln_post_g'], 'm_ln_post_b': out['m_ln_post_b'], 'v_meta_tokens': out['v_meta_tokens'], 'v_ln_in_g': out['v_ln_in_g'], 'v_ln_in_b': out['v_ln_in_b'], 'v_w_in': out['v_w_in'], 'v_conv_dw_w': out['v_conv_dw_w'], 'v_conv_dw_b': out['v_conv_dw_b'], 'v_conv_ln_g': out['v_conv_ln_g'], 'v_conv_ln_b': out['v_conv_ln_b'], 'v_conv_pw_w': out['v_conv_pw_w'], 'v_conv_pw_b': out['v_conv_pw_b'], 'v_attn_sinks': out['v_attn_sinks'], 'v_lru_conv_w': out['v_lru_conv_w'], 'v_lru_conv_b': out['v_lru_conv_b'], 'v_lru_wa': out['v_lru_wa'], 'v_lru_ba': out['v_lru_ba'], 'v_lru_wx': out['v_lru_wx'], 'v_lru_bx': out['v_lru_bx'], 'v_lru_lambda': out['v_lru_lambda'], 'v_w_out': out['v_w_out'], 'v_ln_post_g': out['v_ln_post_g'], 'v_ln_post_b': out['v_ln_post_b']}


def _loss(weights, diff, rest, loss_target):
    with _jax.named_scope("forward"):
        args = {**rest, TWIN_DIFF_INPUT: diff, **{k: w.astype(_WEIGHT_DTYPES[k]) for k, w in weights.items()}}
        y = _forward(args)
    with _jax.named_scope("loss_head"):
        err = _jnp.square(y.astype(_jnp.float32) - loss_target)
        return 0.5 * _jnp.sum(_jnp.mean(err, axis=-1)) if err.ndim else 0.5 * err


def _adamw(w, g, m, v):
    m = ADAM_B1 * m + (1.0 - ADAM_B1) * g
    v = ADAM_B2 * v + (1.0 - ADAM_B2) * _jnp.square(g)
    m_hat = m / (1.0 - ADAM_B1 ** ADAM_STEP)
    v_hat = v / (1.0 - ADAM_B2 ** ADAM_STEP)
    delta = -ADAM_LR * (m_hat / (_jnp.sqrt(v_hat) + ADAM_EPS) + ADAM_WD * w)
    return delta, m, v


def reference(x, meta_tokens, ln_in_g, ln_in_b, w_in, conv_dw_w, conv_dw_b, conv_ln_g, conv_ln_b, conv_pw_w, conv_pw_b, attn_sinks, lru_conv_w, lru_conv_b, lru_wa, lru_ba, lru_wx, lru_bx, lru_lambda, w_out, ln_post_g, ln_post_b, loss_target, m_meta_tokens, m_ln_in_g, m_ln_in_b, m_w_in, m_conv_dw_w, m_conv_dw_b, m_conv_ln_g, m_conv_ln_b, m_conv_pw_w, m_conv_pw_b, m_attn_sinks, m_lru_conv_w, m_lru_conv_b, m_lru_wa, m_lru_ba, m_lru_wx, m_lru_bx, m_lru_lambda, m_w_out, m_ln_post_g, m_ln_post_b, v_meta_tokens, v_ln_in_g, v_ln_in_b, v_w_in, v_conv_dw_w, v_conv_dw_b, v_conv_ln_g, v_conv_ln_b, v_conv_pw_w, v_conv_pw_b, v_attn_sinks, v_lru_conv_w, v_lru_conv_b, v_lru_wa, v_lru_ba, v_lru_wx, v_lru_bx, v_lru_lambda, v_w_out, v_ln_post_g, v_ln_post_b):
    given = dict(x=x, meta_tokens=meta_tokens, ln_in_g=ln_in_g, ln_in_b=ln_in_b, w_in=w_in, conv_dw_w=conv_dw_w, conv_dw_b=conv_dw_b, conv_ln_g=conv_ln_g, conv_ln_b=conv_ln_b, conv_pw_w=conv_pw_w, conv_pw_b=conv_pw_b, attn_sinks=attn_sinks, lru_conv_w=lru_conv_w, lru_conv_b=lru_conv_b, lru_wa=lru_wa, lru_ba=lru_ba, lru_wx=lru_wx, lru_bx=lru_bx, lru_lambda=lru_lambda, w_out=w_out, ln_post_g=ln_post_g, ln_post_b=ln_post_b, loss_target=loss_target, m_meta_tokens=m_meta_tokens, m_ln_in_g=m_ln_in_g, m_ln_in_b=m_ln_in_b, m_w_in=m_w_in, m_conv_dw_w=m_conv_dw_w, m_conv_dw_b=m_conv_dw_b, m_conv_ln_g=m_conv_ln_g, m_conv_ln_b=m_conv_ln_b, m_conv_pw_w=m_conv_pw_w, m_conv_pw_b=m_conv_pw_b, m_attn_sinks=m_attn_sinks, m_lru_conv_w=m_lru_conv_w, m_lru_conv_b=m_lru_conv_b, m_lru_wa=m_lru_wa, m_lru_ba=m_lru_ba, m_lru_wx=m_lru_wx, m_lru_bx=m_lru_bx, m_lru_lambda=m_lru_lambda, m_w_out=m_w_out, m_ln_post_g=m_ln_post_g, m_ln_post_b=m_ln_post_b, v_meta_tokens=v_meta_tokens, v_ln_in_g=v_ln_in_g, v_ln_in_b=v_ln_in_b, v_w_in=v_w_in, v_conv_dw_w=v_conv_dw_w, v_conv_dw_b=v_conv_dw_b, v_conv_ln_g=v_conv_ln_g, v_conv_ln_b=v_conv_ln_b, v_conv_pw_w=v_conv_pw_w, v_conv_pw_b=v_conv_pw_b, v_attn_sinks=v_attn_sinks, v_lru_conv_w=v_lru_conv_w, v_lru_conv_b=v_lru_conv_b, v_lru_wa=v_lru_wa, v_lru_ba=v_lru_ba, v_lru_wx=v_lru_wx, v_lru_bx=v_lru_bx, v_lru_lambda=v_lru_lambda, v_w_out=v_w_out, v_ln_post_g=v_ln_post_g, v_ln_post_b=v_ln_post_b)
    weights = {n: given[n] for n in TWIN_WEIGHTS}
    shared = {n: given[n] for n in SHARED_INPUTS}
    per_example = {n: given[n] for n in ['x']}
    grad_fn = _jax.value_and_grad(_loss, argnums=(0, 1))

    def one_microbatch(ex, loss_target):
        ex = dict(ex)
        diff = ex.pop(TWIN_DIFF_INPUT)
        return grad_fn(weights, diff, {**shared, **ex}, loss_target)

    if N_MICROBATCH == 1:
        loss, (grad_w, grad_x) = one_microbatch(per_example, given["loss_target"])
    else:
        def body(carry, xs):
            loss_sum, grad_sum = carry
            l_k, (gw_k, gx_k) = one_microbatch(xs[0], xs[1])
            with _jax.named_scope("update"):
                return (loss_sum + l_k, _jax.tree.map(_jnp.add, grad_sum, gw_k)), gx_k

        init = (_jnp.zeros((), _jnp.float32), _jax.tree.map(_jnp.zeros_like, weights))
        (loss, grad_w), grad_x = _jax.lax.scan(body, init, (per_example, given["loss_target"]))
    with _jax.named_scope("update"):
        delta_w, new_m, new_v = {}, {}, {}
        for n in TWIN_WEIGHTS:
            delta_w[n], new_m[n], new_v[n] = _adamw(weights[n], grad_w[n], given["m_" + n], given["v_" + n])
    return (loss, grad_x, *[grad_w[n] for n in TWIN_WEIGHTS], *[delta_w[n] for n in TWIN_WEIGHTS],
            *[new_m[n] for n in TWIN_WEIGHTS], *[new_v[n] for n in TWIN_WEIGHTS])
```

```python
import functools

import jax
import jax.numpy as jnp
from jax import lax
from jax.experimental import pallas as pl
from jax.experimental.pallas import tpu as pltpu

F32 = jnp.float32
BF16 = jnp.bfloat16

D = 2048
N_META = 16
CW = 512
CONV_K = 31
AW = 1024
KVW = 256
N_HEADS = 16
LW = 512
LRU_K = 4
LRU_C = 8.0
IN_TOTAL = 5120
ROT_HALF = 8
ROPE_THETA = 500000.0
LN_EPS = 1e-5
DEPTH = 2
ALPHA = (2.0 * DEPTH) ** 0.25
NEG_INF = -1e30
ADAM_LR, ADAM_B1, ADAM_B2, ADAM_EPS, ADAM_WD, ADAM_STEP = 0.001, 0.9, 0.999, 1e-08, 0.01, 10

BLK = 128
PAD = BLK - N_META
N_SHARD = 4
WIN_SH = IN_TOTAL // N_SHARD
WOUT_SH = D // N_SHARD
PW_SH = CW // N_SHARD
HALO = 32
LHALO = 8
V7X_VMEM_LIMIT = 60 * 1024 * 1024


def _cp(*sem):
    return pltpu.CompilerParams(dimension_semantics=sem if sem else None, vmem_limit_bytes=V7X_VMEM_LIMIT)


def _pick(total, prefs):
    for p in prefs:
        if total % p == 0:
            return p
    raise ValueError(f"no tile for {total}")


def _dot(a, b):
    return jnp.dot(a, b, preferred_element_type=F32)


def _dot_nt(a, b):
    return lax.dot_general(a, b, (((1,), (1,)), ((), ())), preferred_element_type=F32)


def _dot_tn(a, b):
    return lax.dot_general(a, b, (((0,), (0,)), ((), ())), preferred_element_type=F32)


def _sigmoid(x):
    return 1.0 / (1.0 + jnp.exp(-x))


def _silu_and_grad(x):
    s = _sigmoid(x)
    return x * s, s * (1.0 + x * (1.0 - s))


def _ln_rows(x, g, b):
    mu = jnp.mean(x, axis=-1, keepdims=True)
    xc = x - mu
    var = jnp.mean(xc * xc, axis=-1, keepdims=True)
    rstd = lax.rsqrt(var + LN_EPS)
    xhat = xc * rstd
    return xhat * g + b, xhat, rstd


def _ln_bwd_rows(dy, xhat, rstd, g):
    dxh = dy * g
    m1 = jnp.mean(dxh, axis=-1, keepdims=True)
    m2 = jnp.mean(dxh * xhat, axis=-1, keepdims=True)
    return rstd * (dxh - m1 - xhat * m2)


def _row_ids(n, base):
    return base + lax.broadcasted_iota(jnp.int32, (n, 1), 0)


def _colsum(x):
    return jnp.sum(x, axis=0, keepdims=True)


def _embed_fwd(x, meta, g, b):
    S = x.shape[0]
    nb = S // BLK + 1

    def body(x_ref, meta_ref, g_ref, b_ref, h_ref, hb_ref):
        n = pl.program_id(0)

        @pl.when(n == 0)
        def _():
            y, _, _ = _ln_rows(meta_ref[...], g_ref[...], b_ref[...])
            h_ref[...] = jnp.zeros_like(h_ref)
            h_ref[PAD:BLK, :] = y

        @pl.when(n > 0)
        def _():
            y, _, _ = _ln_rows(x_ref[...], g_ref[...], b_ref[...])
            h_ref[...] = y

        hb_ref[...] = h_ref[...].astype(BF16)

    return pl.pallas_call(
        body, name="embed_fwd", grid=(nb,),
        in_specs=[pl.BlockSpec((BLK, D), lambda n: (jnp.maximum(n - 1, 0), 0)),
                  pl.BlockSpec((N_META, D), lambda n: (0, 0)),
                  pl.BlockSpec((1, D), lambda n: (0, 0)),
                  pl.BlockSpec((1, D), lambda n: (0, 0))],
        out_specs=[pl.BlockSpec((BLK, D), lambda n: (n, 0)),
                   pl.BlockSpec((BLK, D), lambda n: (n, 0))],
        out_shape=[jax.ShapeDtypeStruct((nb * BLK, D), F32), jax.ShapeDtypeStruct((nb * BLK, D), BF16)],
        compiler_params=_cp("arbitrary"),
    )(x, meta, g, b)


def _embed_bwd(dh, x, meta, g, b):
    S = x.shape[0]
    nb = S // BLK + 1

    def body(dh_ref, x_ref, meta_ref, g_ref, b_ref, gx_ref, gm_ref, dg_ref, db_ref):
        n = pl.program_id(0)

        @pl.when(n == 0)
        def _():
            _, xhat, rstd = _ln_rows(meta_ref[...], g_ref[...], b_ref[...])
            dy = dh_ref[PAD:BLK, :]
            gm_ref[...] = _ln_bwd_rows(dy, xhat, rstd, g_ref[...])
            dg_ref[...] = _colsum(dy * xhat)
            db_ref[...] = _colsum(dy)

        @pl.when(n > 0)
        def _():
            _, xhat, rstd = _ln_rows(x_ref[...], g_ref[...], b_ref[...])
            dy = dh_ref[...]
            gx_ref[...] = _ln_bwd_rows(dy, xhat, rstd, g_ref[...])
            dg_ref[...] += _colsum(dy * xhat)
            db_ref[...] += _colsum(dy)

    prev = lambda n: (jnp.maximum(n - 1, 0), 0)
    const = lambda n: (0, 0)
    return pl.pallas_call(
        body, name="embed_bwd", grid=(nb,),
        in_specs=[pl.BlockSpec((BLK, D), lambda n: (n, 0)),
                  pl.BlockSpec((BLK, D), prev),
                  pl.BlockSpec((N_META, D), const),
                  pl.BlockSpec((1, D), const),
                  pl.BlockSpec((1, D), const)],
        out_specs=[pl.BlockSpec((BLK, D), prev),
                   pl.BlockSpec((N_META, D), const),
                   pl.BlockSpec((1, D), const),
                   pl.BlockSpec((1, D), const)],
        out_shape=[jax.ShapeDtypeStruct((S, D), F32), jax.ShapeDtypeStruct((N_META, D), F32),
                   jax.ShapeDtypeStruct((1, D), F32), jax.ShapeDtypeStruct((1, D), F32)],
        compiler_params=_cp("arbitrary"),
    )(dh, x, meta, g, b)


def _loss_head(h, target):
    T = h.shape[0]
    nb = T // BLK

    def body(h_ref, t_ref, part_ref, dy_ref):
        n = pl.program_id(0)

        @pl.when(n == 0)
        def _():
            part_ref[...] = jnp.zeros_like(part_ref)
            dy_ref[...] = jnp.zeros_like(dy_ref)

        @pl.when(n > 0)
        def _():
            err = h_ref[...] - t_ref[...]
            part_ref[...] += _colsum(err * err) * (0.5 / D)
            dy_ref[...] = err * (1.0 / D)

    return pl.pallas_call(
        body, name="loss_head", grid=(nb,),
        in_specs=[pl.BlockSpec((BLK, D), lambda n: (n, 0)),
                  pl.BlockSpec((BLK, D), lambda n: (jnp.maximum(n - 1, 0), 0))],
        out_specs=[pl.BlockSpec((1, D), lambda n: (0, 0)),
                   pl.BlockSpec((BLK, D), lambda n: (n, 0))],
        out_shape=[jax.ShapeDtypeStruct((1, D), F32), jax.ShapeDtypeStruct((T, D), F32)],
        compiler_params=_cp("arbitrary"),
    )(h, target)


def _proj_fwd(hb, w_in, l):
    T = hb.shape[0]
    tm = _pick(T, (1056, 384, 128))

    def body(a_ref, w_ref, o_ref):
        o_ref[...] = _dot(a_ref[...], w_ref[...])

    return pl.pallas_call(
        body, name=f"proj_fwd{l}", grid=(T // tm, N_SHARD),
        in_specs=[pl.BlockSpec((tm, D), lambda i, j: (i, 0)),
                  pl.BlockSpec((None, None, D, WIN_SH), lambda i, j: (j, l, 0, 0))],
        out_specs=pl.BlockSpec((tm, WIN_SH), lambda i, j: (i, j)),
        out_shape=jax.ShapeDtypeStruct((T, IN_TOTAL), F32),
        compiler_params=_cp("parallel", "arbitrary"),
    )(hb, w_in)


def _out_fwd(yc, ya, yl, w_out, h, g, b, l):
    T = h.shape[0]
    tm = _pick(T, (384, 128))

    def body(yc_ref, ya_ref, yl_ref, w_ref, h_ref, g_ref, b_ref, hn_ref, hnb_ref, xh_ref, rs_ref):
        acc = _dot(yc_ref[...], w_ref[0])
        acc += _dot(ya_ref[:, 0:WOUT_SH], w_ref[1])
        acc += _dot(ya_ref[:, WOUT_SH:2 * WOUT_SH], w_ref[2])
        acc += _dot(yl_ref[...], w_ref[3])
        z = ALPHA * h_ref[...] + acc
        y, xhat, rstd = _ln_rows(z, g_ref[...], b_ref[...])
        hn_ref[...] = y
        hnb_ref[...] = y.astype(BF16)
        xh_ref[...] = xhat
        rs_ref[...] = rstd

    row = lambda i: (i, 0)
    return pl.pallas_call(
        body, name=f"out_fwd{l}", grid=(T // tm,),
        in_specs=[pl.BlockSpec((tm, CW), row), pl.BlockSpec((tm, AW), row), pl.BlockSpec((tm, LW), row),
                  pl.BlockSpec((N_SHARD, None, WOUT_SH, D), lambda i: (0, l, 0, 0)),
                  pl.BlockSpec((tm, D), row),
                  pl.BlockSpec((None, 1, D), lambda i: (l, 0, 0)),
                  pl.BlockSpec((None, 1, D), lambda i: (l, 0, 0))],
        out_specs=[pl.BlockSpec((tm, D), row), pl.BlockSpec((tm, D), row), pl.BlockSpec((tm, D), row),
                   pl.BlockSpec((tm, 1), row)],
        out_shape=[jax.ShapeDtypeStruct((T, D), F32), jax.ShapeDtypeStruct((T, D), BF16),
                   jax.ShapeDtypeStruct((T, D), F32), jax.ShapeDtypeStruct((T, 1), F32)],
        compiler_params=_cp("parallel"),
    )(yc, ya, yl, w_out, h, g, b)


def _post_ln_bwd(dhn, xhat, rstd, g, l):
    T = dhn.shape[0]
    tm = _pick(T, (384, 128))

    def body(d_ref, xh_ref, rs_ref, g_ref, dz_ref, dzb_ref, dg_ref, db_ref):
        @pl.when(pl.program_id(0) == 0)
        def _():
            dg_ref[...] = jnp.zeros_like(dg_ref)
            db_ref[...] = jnp.zeros_like(db_ref)

        dy = d_ref[...]
        xhat = xh_ref[...]
        dz = _ln_bwd_rows(dy, xhat, rs_ref[...], g_ref[...])
        dz_ref[...] = dz
        dzb_ref[...] = dz.astype(BF16)
        dg_ref[...] += _colsum(dy * xhat)
        db_ref[...] += _colsum(dy)

    row = lambda i: (i, 0)
    const = lambda i: (0, 0)
    return pl.pallas_call(
        body, name=f"post_ln_bwd{l}", grid=(T // tm,),
        in_specs=[pl.BlockSpec((tm, D), row), pl.BlockSpec((tm, D), row), pl.BlockSpec((tm, 1), row),
                  pl.BlockSpec((None, 1, D), lambda i: (l, 0, 0))],
        out_specs=[pl.BlockSpec((tm, D), row), pl.BlockSpec((tm, D), row),
                   pl.BlockSpec((1, D), const), pl.BlockSpec((1, D), const)],
        out_shape=[jax.ShapeDtypeStruct((T, D), F32), jax.ShapeDtypeStruct((T, D), BF16),
                   jax.ShapeDtypeStruct((1, D), F32), jax.ShapeDtypeStruct((1, D), F32)],
        compiler_params=_cp("arbitrary"),
    )(dhn, xhat, rstd, g)


def _dcat_bwd(dzb, w_out, l):
    T = dzb.shape[0]
    tm = _pick(T, (384, 128))

    def body(dz_ref, w_ref, dc_ref, da_ref, dl_ref):
        dz = dz_ref[...]
        dc_ref[...] = _dot_nt(dz, w_ref[0])
        da_ref[:, 0:WOUT_SH] = _dot_nt(dz, w_ref[1])
        da_ref[:, WOUT_SH:2 * WOUT_SH] = _dot_nt(dz, w_ref[2])
        dl_ref[...] = _dot_nt(dz, w_ref[3])

    row = lambda i: (i, 0)
    return pl.pallas_call(
        body, name=f"dcat_bwd{l}", grid=(T // tm,),
        in_specs=[pl.BlockSpec((tm, D), row),
                  pl.BlockSpec((N_SHARD, None, WOUT_SH, D), lambda i: (0, l, 0, 0))],
        out_specs=[pl.BlockSpec((tm, CW), row), pl.BlockSpec((tm, AW), row), pl.BlockSpec((tm, LW), row)],
        out_shape=[jax.ShapeDtypeStruct((T, CW), F32), jax.ShapeDtypeStruct((T, AW), F32),
                   jax.ShapeDtypeStruct((T, LW), F32)],
        compiler_params=_cp("parallel"),
    )(dzb, w_out)


def _dwout_bwd(yc, ya, yl, dzb, l):
    T = dzb.shape[0]
    tm = _pick(T, (1056, 384, 128))
    hr = WOUT_SH // 2
    nt = T // tm

    def body(yc_ref, ya_ref, yl_ref, dz_ref, o_ref):
        j = pl.program_id(0)
        t = pl.program_id(2)

        @pl.when(t == 0)
        def _():
            o_ref[...] = jnp.zeros_like(o_ref)

        dz = dz_ref[...]

        @pl.when(j == 0)
        def _():
            o_ref[...] += _dot_tn(yc_ref[...], dz)

        @pl.when((j == 1) | (j == 2))
        def _():
            o_ref[...] += _dot_tn(ya_ref[...], dz)

        @pl.when(j == 3)
        def _():
            o_ref[...] += _dot_tn(yl_ref[...], dz)

    return pl.pallas_call(
        body, name=f"dwout_bwd{l}", grid=(N_SHARD, 2, nt),
        in_specs=[pl.BlockSpec((tm, hr), lambda j, r, t: (t, r)),
                  pl.BlockSpec((tm, hr), lambda j, r, t: (t, 2 * jnp.clip(j - 1, 0, 1) + r)),
                  pl.BlockSpec((tm, hr), lambda j, r, t: (t, r)),
                  pl.BlockSpec((tm, D), lambda j, r, t: (t, 0))],
        out_specs=pl.BlockSpec((None, None, hr, D), lambda j, r, t: (j, r, 0, 0)),
        out_shape=jax.ShapeDtypeStruct((N_SHARD, 2, hr, D), F32),
        compiler_params=_cp("parallel", "parallel", "arbitrary"),
    )(yc, ya, yl, dzb)


def _dh_bwd(dproj, w_in, dz, l):
    T = dproj.shape[0]
    tm = _pick(T, (1056, 384, 128))

    def body(dp_ref, w_ref, dz_ref, o_ref, acc_ref):
        j = pl.program_id(1)

        @pl.when(j == 0)
        def _():
            acc_ref[...] = ALPHA * dz_ref[...]

        acc_ref[...] += _dot_nt(dp_ref[...], w_ref[...])

        @pl.when(j == N_SHARD - 1)
        def _():
            o_ref[...] = acc_ref[...]

    return pl.pallas_call(
        body, name=f"dh_bwd{l}", grid=(T // tm, N_SHARD),
        in_specs=[pl.BlockSpec((tm, WIN_SH), lambda i, j: (i, j)),
                  pl.BlockSpec((None, None, D, WIN_SH), lambda i, j: (j, l, 0, 0)),
                  pl.BlockSpec((tm, D), lambda i, j: (i, 0))],
        out_specs=pl.BlockSpec((tm, D), lambda i, j: (i, 0)),
        out_shape=jax.ShapeDtypeStruct((T, D), F32),
        scratch_shapes=[pltpu.VMEM((tm, D), F32)],
        compiler_params=_cp("parallel", "arbitrary"),
    )(dproj, w_in, dz)


def _dwin_bwd(hb, dproj, l):
    T = hb.shape[0]
    tm = _pick(T, (1056, 384, 128))
    hr = D // 2

    def body(h_ref, dp_ref, o_ref):
        @pl.when(pl.program_id(2) == 0)
        def _():
            o_ref[...] = jnp.zeros_like(o_ref)

        o_ref[...] += _dot_tn(h_ref[...], dp_ref[...])

    return pl.pallas_call(
        body, name=f"dwin_bwd{l}", grid=(N_SHARD, 2, T // tm),
        in_specs=[pl.BlockSpec((tm, hr), lambda j, r, t: (t, r)),
                  pl.BlockSpec((tm, WIN_SH), lambda j, r, t: (t, j))],
        out_specs=pl.BlockSpec((None, None, hr, WIN_SH), lambda j, r, t: (j, r, 0, 0)),
        out_shape=jax.ShapeDtypeStruct((N_SHARD, 2, hr, WIN_SH), F32),
        compiler_params=_cp("parallel", "parallel", "arbitrary"),
    )(hb, dproj)


def _glu_masked(v, g, base_row):
    rows = _row_ids(v.shape[0], base_row)
    return jnp.where(rows >= PAD, v * _sigmoid(g), 0.0)


def _conv_tile(T):
    return _pick(T, (384, 128))


def _conv_fwd(proj, dw_w, dw_b, ln_g, ln_b, pw_w, pw_b, l):
    T = proj.shape[0]
    tm = _conv_tile(T)
    hb = tm // HALO

    def body(cv_ref, cg_ref, ct_ref, hv_ref, hg_ref, w_ref, b_ref, g_ref, be_ref, pw_ref, pb_ref,
             yc_ref, conv_ref, buf):
        i = pl.program_id(0)
        buf[0:HALO, :] = _glu_masked(hv_ref[...], hg_ref[...], i * tm - HALO)
        buf[HALO:HALO + tm, :] = _glu_masked(cv_ref[...], cg_ref[...], i * tm)
        acc = jnp.zeros((tm, CW), F32) + b_ref[...]
        for k in range(CONV_K):
            o = HALO - (CONV_K - 1) + k
            acc += w_ref[k:k + 1, :] * buf[o:o + tm, :]
        conv_ref[...] = acc
        u, _, _ = _ln_rows(acc, g_ref[...], be_ref[...])
        s = u * _sigmoid(u)
        cpw = _dot(s.astype(BF16), pw_ref[...]) + pb_ref[...]
        gate, _ = _silu_and_grad(ct_ref[...])
        yc_ref[...] = (cpw * gate).astype(BF16)

    vec = pl.BlockSpec((None, 1, CW), lambda i: (l, 0, 0))
    return pl.pallas_call(
        body, name=f"conv_fwd{l}", grid=(T // tm,),
        in_specs=[pl.BlockSpec((tm, CW), lambda i: (i, 0)),
                  pl.BlockSpec((tm, CW), lambda i: (i, 1)),
                  pl.BlockSpec((tm, CW), lambda i: (i, 2)),
                  pl.BlockSpec((HALO, CW), lambda i: (jnp.maximum(i * hb - 1, 0), 0)),
                  pl.BlockSpec((HALO, CW), lambda i: (jnp.maximum(i * hb - 1, 0), 1)),
                  pl.BlockSpec((None, CONV_K, CW), lambda i: (l, 0, 0)),
                  vec, vec, vec,
                  pl.BlockSpec((None, CW, CW), lambda i: (l, 0, 0)),
                  vec],
        out_specs=[pl.BlockSpec((tm, CW), lambda i: (i, 0)), pl.BlockSpec((tm, CW), lambda i: (i, 0))],
        out_shape=[jax.ShapeDtypeStruct((T, CW), BF16), jax.ShapeDtypeStruct((T, CW), F32)],
        scratch_shapes=[pltpu.VMEM((tm + HALO, CW), F32)],
        compiler_params=_cp("parallel"),
    )(proj, proj, proj, proj, proj, dw_w, dw_b, ln_g, ln_b, pw_w, pw_b)


def _conv_bwd_rows(conv, proj, d_yc, ln_g, ln_b, pw_w, pw_b, l):
    T = conv.shape[0]
    tm = _conv_tile(T)

    def body(conv_ref, ct_ref, dy_ref, g_ref, be_ref, pw_ref, pb_ref,
             dconv_ref, dct_ref, dpw_ref, dpb_ref, dg_ref, db_ref):
        @pl.when(pl.program_id(0) == 0)
        def _():
            dpw_ref[...] = jnp.zeros_like(dpw_ref)
            dpb_ref[...] = jnp.zeros_like(dpb_ref)
            dg_ref[...] = jnp.zeros_like(dg_ref)
            db_ref[...] = jnp.zeros_like(db_ref)

        u, xhat, rstd = _ln_rows(conv_ref[...], g_ref[...], be_ref[...])
        s, ds_du = _silu_and_grad(u)
        sb = s.astype(BF16)
        cpw = _dot(sb, pw_ref[...]) + pb_ref[...]
        gate, dgate = _silu_and_grad(ct_ref[...])
        dy = dy_ref[...]
        d_cpw = dy * gate
        dct_ref[...] = (dy * cpw * dgate).astype(BF16)
        d_cpw_b = d_cpw.astype(BF16)
        dpb_ref[...] += _colsum(d_cpw)
        dpw_ref[...] += _dot_tn(sb, d_cpw_b)
        du = _dot_nt(d_cpw_b, pw_ref[...]) * ds_du
        dconv_ref[...] = _ln_bwd_rows(du, xhat, rstd, g_ref[...])
        dg_ref[...] += _colsum(du * xhat)
        db_ref[...] += _colsum(du)

    vec = pl.BlockSpec((None, 1, CW), lambda i: (l, 0, 0))
    row = lambda i: (i, 0)
    const = lambda i: (0, 0)
    return pl.pallas_call(
        body, name=f"conv_bwd_rows{l}", grid=(T // tm,),
        in_specs=[pl.BlockSpec((tm, CW), row), pl.BlockSpec((tm, CW), lambda i: (i, 2)),
                  pl.BlockSpec((tm, CW), row), vec, vec,
                  pl.BlockSpec((None, CW, CW), lambda i: (l, 0, 0)), vec],
        out_specs=[pl.BlockSpec((tm, CW), row), pl.BlockSpec((tm, CW), row),
                   pl.BlockSpec((CW, CW), const), pl.BlockSpec((1, CW), const),
                   pl.BlockSpec((1, CW), const), pl.BlockSpec((1, CW), const)],
        out_shape=[jax.ShapeDtypeStruct((T, CW), F32), jax.ShapeDtypeStruct((T, CW), BF16),
                   jax.ShapeDtypeStruct((CW, CW), F32), jax.ShapeDtypeStruct((1, CW), F32),
                   jax.ShapeDtypeStruct((1, CW), F32), jax.ShapeDtypeStruct((1, CW), F32)],
        compiler_params=_cp("arbitrary"),
    )(conv, proj, d_yc, ln_g, ln_b, pw_w, pw_b)


def _conv_bwd_taps(d_conv, proj, dw_w, l):
    T = d_conv.shape[0]
    tm = _conv_tile(T)
    hb = tm // HALO
    nt = T // tm
    last_halo = T // HALO - 1

    def body(dc_ref, dh_ref, cv_ref, cg_ref, hv_ref, hg_ref, w_ref, o_ref, dw_ref, dwb_ref, cbuf, dbuf):
        i = pl.program_id(0)

        @pl.when(i == 0)
        def _():
            dw_ref[...] = jnp.zeros_like(dw_ref)
            dwb_ref[...] = jnp.zeros_like(dwb_ref)

        cbuf[0:HALO, :] = _glu_masked(hv_ref[...], hg_ref[...], i * tm - HALO)
        cbuf[HALO:HALO + tm, :] = _glu_masked(cv_ref[...], cg_ref[...], i * tm)
        dmain = dc_ref[...]
        dbuf[0:tm, :] = dmain
        dbuf[tm:tm + HALO, :] = jnp.where(i < nt - 1, dh_ref[...], 0.0)
        acc = jnp.zeros((tm, CW), F32)
        for k in range(CONV_K):
            o = CONV_K - 1 - k
            acc += w_ref[k:k + 1, :] * dbuf[o:o + tm, :]
            oc = HALO - (CONV_K - 1) + k
            dw_ref[k:k + 1, :] += _colsum(dmain * cbuf[oc:oc + tm, :])
        dwb_ref[...] += _colsum(dmain)
        d_c = jnp.where(_row_ids(tm, i * tm) >= PAD, acc, 0.0)
        sig = _sigmoid(cg_ref[...])
        o_ref[:, 0:CW] = (d_c * sig).astype(BF16)
        o_ref[:, CW:2 * CW] = (d_c * cv_ref[...] * sig * (1.0 - sig)).astype(BF16)

    const = lambda i: (0, 0)
    return pl.pallas_call(
        body, name=f"conv_bwd_taps{l}", grid=(nt,),
        in_specs=[pl.BlockSpec((tm, CW), lambda i: (i, 0)),
                  pl.BlockSpec((HALO, CW), lambda i: (jnp.minimum((i + 1) * hb, last_halo), 0)),
                  pl.BlockSpec((tm, CW), lambda i: (i, 0)),
                  pl.BlockSpec((tm, CW), lambda i: (i, 1)),
                  pl.BlockSpec((HALO, CW), lambda i: (jnp.maximum(i * hb - 1, 0), 0)),
                  pl.BlockSpec((HALO, CW), lambda i: (jnp.maximum(i * hb - 1, 0), 1)),
                  pl.BlockSpec((None, CONV_K, CW), lambda i: (l, 0, 0))],
        out_specs=[pl.BlockSpec((tm, 2 * CW), lambda i: (i, 0)),
                   pl.BlockSpec((HALO, CW), const), pl.BlockSpec((1, CW), const)],
        out_shape=[jax.ShapeDtypeStruct((T, 2 * CW), BF16), jax.ShapeDtypeStruct((HALO, CW), F32),
                   jax.ShapeDtypeStruct((1, CW), F32)],
        scratch_shapes=[pltpu.VMEM((tm + HALO, CW), F32), pltpu.VMEM((tm + HALO, CW), F32)],
        compiler_params=_cp("arbitrary"),
    )(d_conv, d_conv, proj, proj, proj, proj, dw_w)


def _log1p_small(e):
    return jnp.where(e < 1e-3, e * (1.0 - e * (0.5 - e * (1.0 / 3.0))), jnp.log(1.0 + e))


def _softplus(z):
    return jnp.maximum(z, 0.0) + _log1p_small(jnp.exp(-jnp.abs(z)))


def _neg_expm1(x):
    series = -x * (1.0 + x * (1.0 / 2.0) * (1.0 + x * (1.0 / 3.0) * (1.0 + x * (1.0 / 4.0) * (
        1.0 + x * (1.0 / 5.0) * (1.0 + x * (1.0 / 6.0) * (1.0 + x * (1.0 / 7.0)))))))
    return jnp.where(x > -0.25, series, 1.0 - jnp.exp(x))


def _lru_gates(rxbuf, tm, base_row, lw_ref, lb_ref, wa_ref, ba_ref, wx_ref, bx_ref, lam_ref):
    rc = jnp.zeros((tm, LW), F32) + lb_ref[...]
    for k in range(LRU_K):
        o = LHALO - (LRU_K - 1) + k
        rc += lw_ref[k:k + 1, :] * rxbuf[o:o + tm, :]
    rcb = rc.astype(BF16)
    r = _sigmoid(_dot(rcb, wa_ref[...]) + ba_ref[...])
    ig = _sigmoid(_dot(rcb, wx_ref[...]) + bx_ref[...])
    sp = _softplus(-lam_ref[...])
    la = -LRU_C * r * sp
    a = jnp.exp(la)
    mult = jnp.sqrt(_neg_expm1(2.0 * la))
    valid = _row_ids(tm, base_row) >= PAD
    return rc, rcb, r, ig, sp, a, mult, valid


def _mask_rows(v, base_row):
    return jnp.where(_row_ids(v.shape[0], base_row) >= PAD, v, 0.0)


def _scan_steps(tm):
    s, out = 1, []
    while s < tm:
        out.append(s)
        s *= 2
    return out


def _lru_tile(T):
    return _pick(T, (384, 128))


def _lru_fwd(proj, lw, lb, wa, ba, wx, bx, lam, l):
    T = proj.shape[0]
    tm = _lru_tile(T)
    hb = tm // LHALO

    def body(rx_ref, rg_ref, hx_ref, lw_ref, lb_ref, wa_ref, ba_ref, wx_ref, bx_ref, lam_ref,
             yl_ref, hl_ref, rxbuf, carry):
        i = pl.program_id(0)

        @pl.when(i == 0)
        def _():
            carry[...] = jnp.zeros_like(carry)

        rxbuf[0:LHALO, :] = _mask_rows(hx_ref[...], i * tm - LHALO)
        rxbuf[LHALO:LHALO + tm, :] = _mask_rows(rx_ref[...], i * tm)
        rc, _, _, ig, _, a, mult, valid = _lru_gates(rxbuf, tm, i * tm, lw_ref, lb_ref, wa_ref, ba_ref,
                                                     wx_ref, bx_ref, lam_ref)
        bb = jnp.where(valid, mult * (ig * rc), 0.0)
        aa = a
        rows = _row_ids(tm, 0)
        for s in _scan_steps(tm):
            keep = rows >= s
            a_s = jnp.where(keep, pltpu.roll(aa, s, axis=0), 1.0)
            b_s = jnp.where(keep, pltpu.roll(bb, s, axis=0), 0.0)
            bb = aa * b_s + bb
            aa = aa * a_s
        h = bb + aa * carry[0:1, :]
        hl_ref[...] = h
        carry[0:1, :] = hl_ref[tm - 1:tm, :]
        gate, _ = _silu_and_grad(rg_ref[...])
        yl_ref[...] = (h * gate).astype(BF16)

    vec = pl.BlockSpec((None, 1, LW), lambda i: (l, 0, 0))
    mat = pl.BlockSpec((None, LW, LW), lambda i: (l, 0, 0))
    return pl.pallas_call(
        body, name=f"lru_fwd{l}", grid=(T // tm,),
        in_specs=[pl.BlockSpec((tm, LW), lambda i: (i, 8)),
                  pl.BlockSpec((tm, LW), lambda i: (i, 9)),
                  pl.BlockSpec((LHALO, LW), lambda i: (jnp.maximum(i * hb - 1, 0), 8)),
                  pl.BlockSpec((None, LRU_K, LW), lambda i: (l, 0, 0)),
                  vec, mat, vec, mat, vec, vec],
        out_specs=[pl.BlockSpec((tm, LW), lambda i: (i, 0)), pl.BlockSpec((tm, LW), lambda i: (i, 0))],
        out_shape=[jax.ShapeDtypeStruct((T, LW), BF16), jax.ShapeDtypeStruct((T, LW), F32)],
        scratch_shapes=[pltpu.VMEM((tm + LHALO, LW), F32), pltpu.VMEM((8, LW), F32)],
        compiler_params=_cp("arbitrary"),
    )(proj, proj, proj, lw, lb, wa, ba, wx, bx, lam)


def _lru_bwd(proj, hl, d_yl, lw, lb, wa, ba, wx, bx, lam, l):
    T = proj.shape[0]
    tm = _lru_tile(T)
    hb = tm // LHALO
    nt = T // tm

    def body(rx_ref, rg_ref, hx_ref, hl_ref, hh_ref, dy_ref, lw_ref, lb_ref, wa_ref, ba_ref, wx_ref, bx_ref,
             lam_ref, o_ref, dlw_ref, dlb_ref, dwa_ref, dba_ref, dwx_ref, dbx_ref, dlam_ref,
             rxbuf, dbuf, carry, head):
        step = pl.program_id(0)
        i = nt - 1 - step

        @pl.when(step == 0)
        def _():
            carry[...] = jnp.zeros_like(carry)
            head[...] = jnp.zeros_like(head)
            for ref in (dlw_ref, dlb_ref, dwa_ref, dba_ref, dwx_ref, dbx_ref, dlam_ref):
                ref[...] = jnp.zeros_like(ref)

        rxbuf[0:LHALO, :] = _mask_rows(hx_ref[...], i * tm - LHALO)
        rxbuf[LHALO:LHALO + tm, :] = _mask_rows(rx_ref[...], i * tm)
        rc, rcb, r, ig, sp, a, mult, valid = _lru_gates(rxbuf, tm, i * tm, lw_ref, lb_ref, wa_ref, ba_ref,
                                                        wx_ref, bx_ref, lam_ref)
        rows = _row_ids(tm, 0)
        h = hl_ref[...]
        h_before = jnp.where(i > 0, hh_ref[LHALO - 1:LHALO, :], 0.0)
        hprev = jnp.where(rows == 0, h_before, pltpu.roll(h, 1, axis=0))
        rg = rg_ref[...]
        gate, dgate = _silu_and_grad(rg)
        dy = dy_ref[...]
        o_ref[:, LW:2 * LW] = (dy * h * dgate).astype(BF16)
        bb = dy * gate + jnp.where(rows == tm - 1, carry[0:1, :], 0.0)
        aa = jnp.where(rows == tm - 1, 0.0, pltpu.roll(a, tm - 1, axis=0))
        for s in _scan_steps(tm):
            keep = rows < tm - s
            a_s = jnp.where(keep, pltpu.roll(aa, tm - s, axis=0), 1.0)
            b_s = jnp.where(keep, pltpu.roll(bb, tm - s, axis=0), 0.0)
            bb = aa * b_s + bb
            aa = aa * a_s
        g = bb
        dbuf[0:tm, :] = a * g
        carry[0:1, :] = dbuf[0:1, :]
        du = jnp.where(valid, g, 0.0)
        da = g * hprev
        dix = du * mult
        dmult = du * (ig * rc)
        dla = jnp.where(valid, da * a - dmult * (a * a) / mult, 0.0)
        dr = dla * (-LRU_C * sp)
        dlam_ref[...] += _colsum(dla * (LRU_C * r)) * _sigmoid(-lam_ref[...])
        dpa = dr * r * (1.0 - r)
        dpx = (dix * rc) * ig * (1.0 - ig)
        dpab = dpa.astype(BF16)
        dpxb = dpx.astype(BF16)
        dba_ref[...] += _colsum(dpa)
        dbx_ref[...] += _colsum(dpx)
        dwa_ref[...] += _dot_tn(rcb, dpab)
        dwx_ref[...] += _dot_tn(rcb, dpxb)
        drc = dix * ig + _dot_nt(dpab, wa_ref[...]) + _dot_nt(dpxb, wx_ref[...])
        dbuf[0:tm, :] = drc
        dbuf[tm:tm + LHALO, :] = head[...]
        acc = jnp.zeros((tm, LW), F32)
        for k in range(LRU_K):
            o = LRU_K - 1 - k
            acc += lw_ref[k:k + 1, :] * dbuf[o:o + tm, :]
            oc = LHALO - (LRU_K - 1) + k
            dlw_ref[k:k + 1, :] += _colsum(drc * rxbuf[oc:oc + tm, :])
        dlb_ref[...] += _colsum(drc)
        head[...] = dbuf[0:LHALO, :]
        o_ref[:, 0:LW] = jnp.where(valid, acc, 0.0).astype(BF16)

    rev = lambda s: nt - 1 - s
    vec = pl.BlockSpec((None, 1, LW), lambda s: (l, 0, 0))
    mat = pl.BlockSpec((None, LW, LW), lambda s: (l, 0, 0))
    const = lambda s: (0, 0)
    halo = lambda s: jnp.maximum(rev(s) * hb - 1, 0)
    return pl.pallas_call(
        body, name=f"lru_bwd{l}", grid=(nt,),
        in_specs=[pl.BlockSpec((tm, LW), lambda s: (rev(s), 8)),
                  pl.BlockSpec((tm, LW), lambda s: (rev(s), 9)),
                  pl.BlockSpec((LHALO, LW), lambda s: (halo(s), 8)),
                  pl.BlockSpec((tm, LW), lambda s: (rev(s), 0)),
                  pl.BlockSpec((LHALO, LW), lambda s: (halo(s), 0)),
                  pl.BlockSpec((tm, LW), lambda s: (rev(s), 0)),
                  pl.BlockSpec((None, LRU_K, LW), lambda s: (l, 0, 0)),
                  vec, mat, vec, mat, vec, vec],
        out_specs=[pl.BlockSpec((tm, 2 * LW), lambda s: (rev(s), 0)),
                   pl.BlockSpec((8, LW), const), pl.BlockSpec((1, LW), const),
                   pl.BlockSpec((LW, LW), const), pl.BlockSpec((1, LW), const),
                   pl.BlockSpec((LW, LW), const), pl.BlockSpec((1, LW), const),
                   pl.BlockSpec((1, LW), const)],
        out_shape=[jax.ShapeDtypeStruct((T, 2 * LW), BF16),
                   jax.ShapeDtypeStruct((8, LW), F32), jax.ShapeDtypeStruct((1, LW), F32),
                   jax.ShapeDtypeStruct((LW, LW), F32), jax.ShapeDtypeStruct((1, LW), F32),
                   jax.ShapeDtypeStruct((LW, LW), F32), jax.ShapeDtypeStruct((1, LW), F32),
                   jax.ShapeDtypeStruct((1, LW), F32)],
        scratch_shapes=[pltpu.VMEM((tm + LHALO, LW), F32), pltpu.VMEM((tm + LHALO, LW), F32),
                        pltpu.VMEM((8, LW), F32), pltpu.VMEM((LHALO, LW), F32)],
        compiler_params=_cp("arbitrary"),
    )(proj, proj, proj, hl, hl, d_yl, lw, lb, wa, ba, wx, bx, lam)


def _rope_tables(T):
    pos = (jnp.arange(T, dtype=jnp.int32) - PAD).astype(F32)
    inv_freq = ROPE_THETA ** (-jnp.arange(ROT_HALF, dtype=F32) / ROT_HALF)
    ang = pos[:, None] * inv_freq[None, :]
    cos, sin = jnp.cos(ang), jnp.sin(ang)
    ones = jnp.ones((T, 64 - 2 * ROT_HALF), F32)
    zeros8 = jnp.zeros((T, ROT_HALF), F32)
    zrest = jnp.zeros((T, 64 - 2 * ROT_HALF), F32)
    c64 = jnp.concatenate([cos, cos, ones], axis=1)
    s1 = jnp.concatenate([-sin, zeros8, zrest], axis=1)
    s2 = jnp.concatenate([zeros8, sin, zrest], axis=1)
    two = lambda t: jnp.concatenate([t, t], axis=1)
    return two(c64), two(s1), two(s2)


def _rot_fwd(x, c, s1, s2):
    return x * c + pltpu.roll(x, 128 - ROT_HALF, axis=1) * s1 + pltpu.roll(x, ROT_HALF, axis=1) * s2


def _rot_bwd(dy, c, s1, s2):
    return dy * c + pltpu.roll(dy * s1, ROT_HALF, axis=1) + pltpu.roll(dy * s2, 128 - ROT_HALF, axis=1)


def _rope_fwd(proj, tabs, l):
    T = proj.shape[0]

    def body(ql_ref, qh_ref, k_ref, v_ref, c_ref, s1_ref, s2_ref, qr_ref, kr_ref, vb_ref):
        c, s1, s2 = c_ref[...], s1_ref[...], s2_ref[...]
        for gcol in range(AW // 128):
            src = ql_ref if gcol < 4 else qh_ref
            x = src[:, 128 * (gcol % 4):128 * (gcol % 4) + 128]
            qr_ref[:, 128 * gcol:128 * gcol + 128] = (_rot_fwd(x, c, s1, s2) * 0.125).astype(BF16)
        for gcol in range(KVW // 128):
            x = k_ref[:, 128 * gcol:128 * gcol + 128]
            kr_ref[:, 128 * gcol:128 * gcol + 128] = _rot_fwd(x, c, s1, s2).astype(BF16)
        vb_ref[...] = v_ref[...].astype(BF16)

    tab = pl.BlockSpec((BLK, 128), lambda n: (n, 0))
    return pl.pallas_call(
        body, name=f"rope_fwd{l}", grid=(T // BLK,),
        in_specs=[pl.BlockSpec((BLK, 512), lambda n: (n, 3)), pl.BlockSpec((BLK, 512), lambda n: (n, 4)),
                  pl.BlockSpec((BLK, KVW), lambda n: (n, 10)), pl.BlockSpec((BLK, KVW), lambda n: (n, 11)),
                  tab, tab, tab],
        out_specs=[pl.BlockSpec((BLK, AW), lambda n: (n, 0)), pl.BlockSpec((BLK, KVW), lambda n: (n, 0)),
                   pl.BlockSpec((BLK, KVW), lambda n: (n, 0))],
        out_shape=[jax.ShapeDtypeStruct((T, AW), BF16), jax.ShapeDtypeStruct((T, KVW), BF16),
                   jax.ShapeDtypeStruct((T, KVW), BF16)],
        compiler_params=_cp("parallel"),
    )(proj, proj, proj, proj, *tabs)


def _attn_mask(n):
    qi = lax.broadcasted_iota(jnp.int32, (BLK, BLK), 0)
    kj = lax.broadcasted_iota(jnp.int32, (BLK, BLK), 1)
    m0 = (kj >= PAD) & (n >= 1)
    mp = (kj > qi) & (n >= 2)
    mc = (kj <= qi) & ((n >= 1) | (kj >= PAD))
    return jnp.concatenate([m0, mp, mc], axis=1)


def _kv_halves(x0_ref, xp_ref, xc_ref, g):
    pg, off = g // 2, g % 2
    cols = slice(128 * pg, 128 * pg + 128)
    x = jnp.concatenate([x0_ref[:, cols], xp_ref[:, cols], xc_ref[:, cols]], axis=0).astype(F32)
    lane = lax.broadcasted_iota(jnp.int32, (1, 128), 1)
    if off == 0:
        lo = jnp.where(lane < 64, x, 0.0)
        hi = pltpu.roll(lo, 64, axis=1)
    else:
        hi = jnp.where(lane >= 64, x, 0.0)
        lo = pltpu.roll(hi, 64, axis=1)
    return lo.astype(BF16), hi.astype(BF16)


def _attn_fwd(qr, kr, vb, proj, sinks, l):
    T = qr.shape[0]

    def body(sink_ref, q_ref, k0_ref, kp_ref, kc_ref, v0_ref, vp_ref, vc_ref, ag_ref, ya_ref, att_ref, lse_ref):
        n = pl.program_id(0)
        mask = _attn_mask(n)
        lane = lax.broadcasted_iota(jnp.int32, (1, 128), 1)
        lse_acc = jnp.zeros((BLK, 128), F32)
        for g in range(4):
            k_lo, k_hi = _kv_halves(k0_ref, kp_ref, kc_ref, g)
            v_lo, v_hi = _kv_halves(v0_ref, vp_ref, vc_ref, g)
            for pp in range(2):
                cols = slice(128 * (2 * g + pp), 128 * (2 * g + pp) + 128)
                qpair = q_ref[:, cols]
                out = jnp.zeros((BLK, 128), F32)
                for hh, (kx, vx) in enumerate(((k_lo, v_lo), (k_hi, v_hi))):
                    h = 4 * g + 2 * pp + hh
                    sink = sink_ref[l, h]
                    s = jnp.where(mask, _dot_nt(qpair, kx), NEG_INF)
                    m = jnp.maximum(jnp.max(s, axis=1, keepdims=True), sink)
                    p = jnp.exp(s - m)
                    denom = jnp.sum(p, axis=1, keepdims=True) + jnp.exp(sink - m)
                    out += _dot((p / denom).astype(BF16), vx)
                    lse_acc = jnp.where(lane == h, m + jnp.log(denom), lse_acc)
                att_ref[:, cols] = out
                gate, _ = _silu_and_grad(ag_ref[:, cols])
                ya_ref[:, cols] = (out * gate).astype(BF16)
        lse_ref[...] = lse_acc

    prev = lambda n: (jnp.maximum(n - 1, 0), 0)
    cur = lambda n: (n, 0)
    zero = lambda n: (0, 0)
    kv = lambda f: pl.BlockSpec((BLK, KVW), f)
    return pl.pallas_call(
        body, name=f"attn_fwd{l}", grid=(T // BLK,),
        in_specs=[pl.BlockSpec(memory_space=pltpu.SMEM),
                  pl.BlockSpec((BLK, AW), cur), kv(zero), kv(prev), kv(cur), kv(zero), kv(prev), kv(cur),
                  pl.BlockSpec((BLK, AW), lambda n: (n, 3))],
        out_specs=[pl.BlockSpec((BLK, AW), cur), pl.BlockSpec((BLK, AW), cur), pl.BlockSpec((BLK, 128), cur)],
        out_shape=[jax.ShapeDtypeStruct((T, AW), BF16), jax.ShapeDtypeStruct((T, AW), F32),
                   jax.ShapeDtypeStruct((T, 128), F32)],
        compiler_params=_cp("parallel"),
    )(sinks, qr, kr, kr, kr, vb, vb, vb, proj)


def _attn_bwd(qr, kr, vb, proj, att, lse, d_ya, sinks, l):
    T = qr.shape[0]
    nb = T // BLK

    def body(sink_ref, q_ref, k0_ref, kp_ref, kc_ref, v0_ref, vp_ref, vc_ref, ag_ref, att_ref, lse_ref, dy_ref,
             dq_ref, dk_ref, dv_ref, dk0_ref, dv0_ref, dag_ref, dsink_ref, kcarry, vcarry):
        n = pl.program_id(0)

        @pl.when(n == 0)
        def _():
            dk0_ref[...] = jnp.zeros_like(dk0_ref)
            dv0_ref[...] = jnp.zeros_like(dv0_ref)
            dsink_ref[...] = jnp.zeros_like(dsink_ref)
            kcarry[...] = jnp.zeros_like(kcarry)
            vcarry[...] = jnp.zeros_like(vcarry)

        @pl.when(n == nb)
        def _():
            dk_ref[...] = kcarry[...]
            dv_ref[...] = vcarry[...]

        @pl.when(n < nb)
        def _():
            mask = _attn_mask(n)
            lane = lax.broadcasted_iota(jnp.int32, (1, 128), 1)
            lse = lse_ref[...]
            dsink = jnp.zeros((1, 128), F32)
            dk_pg, dv_pg = [], []
            for pg in range(2):
                dk_acc = jnp.zeros((3 * BLK, 128), F32)
                dv_acc = jnp.zeros((3 * BLK, 128), F32)
                for off in range(2):
                    g = 2 * pg + off
                    k_lo, k_hi = _kv_halves(k0_ref, kp_ref, kc_ref, g)
                    v_lo, v_hi = _kv_halves(v0_ref, vp_ref, vc_ref, g)
                    dkg = jnp.zeros((3 * BLK, 128), F32)
                    dvg = jnp.zeros((3 * BLK, 128), F32)
                    for pp in range(2):
                        cols = slice(128 * (2 * g + pp), 128 * (2 * g + pp) + 128)
                        qpair = q_ref[:, cols]
                        gate, dgate = _silu_and_grad(ag_ref[:, cols])
                        dy = dy_ref[:, cols]
                        dag_ref[:, cols] = (dy * att_ref[:, cols] * dgate).astype(BF16)
                        do = (dy * gate).astype(BF16)
                        dq = jnp.zeros((BLK, 128), F32)
                        for hh, (kx, vx) in enumerate(((k_lo, v_lo), (k_hi, v_hi))):
                            h = 4 * g + 2 * pp + hh
                            sink = sink_ref[l, h]
                            lse_h = jnp.sum(jnp.where(lane == h, lse, 0.0), axis=1, keepdims=True)
                            s = _dot_nt(qpair, kx)
                            p = jnp.where(mask, jnp.exp(s - lse_h), 0.0)
                            dp = _dot_nt(do, vx)
                            delta = jnp.sum(p * dp, axis=1, keepdims=True)
                            ds = (p * (dp - delta)).astype(BF16)
                            psink = jnp.exp(sink - lse_h)
                            dsink += jnp.where(lane == h, -jnp.sum(psink * delta), 0.0)
                            dq += _dot(ds, kx)
                            half = (lane < 64) if hh == 0 else (lane >= 64)
                            dkg += jnp.where(half, _dot_tn(ds, qpair), 0.0)
                            dvg += jnp.where(half, _dot_tn(p.astype(BF16), do), 0.0)
                        dq_ref[:, cols] = dq
                    own = (lane < 64) if off == 0 else (lane >= 64)
                    dk_acc += jnp.where(own, dkg + pltpu.roll(dkg, 64, axis=1), 0.0)
                    dv_acc += jnp.where(own, dvg + pltpu.roll(dvg, 64, axis=1), 0.0)
                dk_pg.append(dk_acc)
                dv_pg.append(dv_acc)
            dsink_ref[...] += dsink
            for pg in range(2):
                cols = slice(128 * pg, 128 * pg + 128)
                dk0_ref[:, cols] += dk_pg[pg][0:BLK]
                dv0_ref[:, cols] += dv_pg[pg][0:BLK]
                dk_ref[:, cols] = kcarry[:, cols] + dk_pg[pg][BLK:2 * BLK]
                dv_ref[:, cols] = vcarry[:, cols] + dv_pg[pg][BLK:2 * BLK]
                kcarry[:, cols] = dk_pg[pg][2 * BLK:3 * BLK]
                vcarry[:, cols] = dv_pg[pg][2 * BLK:3 * BLK]

    last = nb - 1
    cur = lambda n: (jnp.minimum(n, last), 0)
    prev = lambda n: (jnp.clip(n - 1, 0, last), 0)
    zero = lambda n: (0, 0)
    kv = lambda f: pl.BlockSpec((BLK, KVW), f)
    wide = lambda f: pl.BlockSpec((BLK, AW), f)
    return pl.pallas_call(
        body, name=f"attn_bwd{l}", grid=(nb + 1,),
        in_specs=[pl.BlockSpec(memory_space=pltpu.SMEM),
                  wide(cur), kv(zero), kv(prev), kv(cur), kv(zero), kv(prev), kv(cur),
                  pl.BlockSpec((BLK, AW), lambda n: (jnp.minimum(n, last), 3)),
                  wide(cur), pl.BlockSpec((BLK, 128), cur), wide(cur)],
        out_specs=[wide(cur), kv(prev), kv(prev), kv(zero), kv(zero), wide(cur),
                   pl.BlockSpec((1, 128), zero)],
        out_shape=[jax.ShapeDtypeStruct((T, AW), F32), jax.ShapeDtypeStruct((T, KVW), F32),
                   jax.ShapeDtypeStruct((T, KVW), F32), jax.ShapeDtypeStruct((BLK, KVW), F32),
                   jax.ShapeDtypeStruct((BLK, KVW), F32), jax.ShapeDtypeStruct((T, AW), BF16),
                   jax.ShapeDtypeStruct((1, 128), F32)],
        scratch_shapes=[pltpu.VMEM((BLK, KVW), F32), pltpu.VMEM((BLK, KVW), F32)],
        compiler_params=_cp("arbitrary"),
    )(sinks, qr, kr, kr, kr, vb, vb, vb, proj, att, lse, d_ya)


def _rope_bwd(dqr, dk, dv, dk0, dv0, tabs, l):
    T = dqr.shape[0]

    def body(dq_ref, dk_ref, dv_ref, dk0_ref, dv0_ref, c_ref, s1_ref, s2_ref, o_ref):
        n = pl.program_id(0)
        c, s1, s2 = c_ref[...], s1_ref[...], s2_ref[...]
        first = jnp.where(n == 0, 1.0, 0.0)
        for gcol in range(AW // 128):
            cols = slice(128 * gcol, 128 * gcol + 128)
            o_ref[:, cols] = (_rot_bwd(dq_ref[:, cols], c, s1, s2) * 0.125).astype(BF16)
        for gcol in range(KVW // 128):
            cols = slice(128 * gcol, 128 * gcol + 128)
            dkk = dk_ref[:, cols] + first * dk0_ref[:, cols]
            o_ref[:, AW + 128 * gcol:AW + 128 * gcol + 128] = _rot_bwd(dkk, c, s1, s2).astype(BF16)
            dvv = dv_ref[:, cols] + first * dv0_ref[:, cols]
            o_ref[:, AW + KVW + 128 * gcol:AW + KVW + 128 * gcol + 128] = dvv.astype(BF16)

    cur = lambda n: (n, 0)
    zero = lambda n: (0, 0)
    tab = pl.BlockSpec((BLK, 128), cur)
    return pl.pallas_call(
        body, name=f"rope_bwd{l}", grid=(T // BLK,),
        in_specs=[pl.BlockSpec((BLK, AW), cur), pl.BlockSpec((BLK, KVW), cur), pl.BlockSpec((BLK, KVW), cur),
                  pl.BlockSpec((BLK, KVW), zero), pl.BlockSpec((BLK, KVW), zero), tab, tab, tab],
        out_specs=pl.BlockSpec((BLK, AW + 2 * KVW), cur),
        out_shape=jax.ShapeDtypeStruct((T, AW + 2 * KVW), BF16),
        compiler_params=_cp("parallel"),
    )(dqr, dk, dv, dk0, dv0, *tabs)


def _block_diag(w):
    nl, nh, hd, _ = w.shape
    eye = jnp.eye(nh, dtype=w.dtype)
    return jnp.einsum("lhij,hg->lhigj", w, eye).reshape(nl, nh * hd, nh * hd)


def _diag_blocks(m):
    nh, hd = 8, 64
    m4 = m.reshape(nh, hd, nh, hd)
    return jnp.stack([m4[h, :, h, :] for h in range(nh)], axis=0)


def _device_step(x, target, p):
    vec = lambda a: a.reshape(DEPTH, 1, a.shape[-1])
    ln_in_g, ln_in_b = p["ln_in_g"].reshape(1, D), p["ln_in_b"].reshape(1, D)
    conv_dw_b, conv_ln_g, conv_ln_b, conv_pw_b = map(vec, (p["conv_dw_b"], p["conv_ln_g"], p["conv_ln_b"], p["conv_pw_b"]))
    lru_conv_b, lru_ba, lru_bx, lru_lambda = map(vec, (p["lru_conv_b"], p["lru_ba"], p["lru_bx"], p["lru_lambda"]))
    ln_post_g, ln_post_b = vec(p["ln_post_g"]), vec(p["ln_post_b"])
    wa_bd = _block_diag(p["lru_wa"]).astype(BF16)
    wx_bd = _block_diag(p["lru_wx"]).astype(BF16)
    w_in, w_out, pw_w = p["w_in"], p["w_out"], p["conv_pw_w"]
    sinks = p["attn_sinks"]

    h, hb = _embed_fwd(x, p["meta_tokens"], ln_in_g, ln_in_b)
    T = h.shape[0]
    tabs = _rope_tables(T)
    saved = []
    for l in range(DEPTH):
        proj = _proj_fwd(hb, w_in, l)
        yc, conv = _conv_fwd(proj, p["conv_dw_w"], conv_dw_b, conv_ln_g, conv_ln_b, pw_w, conv_pw_b, l)
        qr, kr, vb = _rope_fwd(proj, tabs, l)
        ya, att, lse = _attn_fwd(qr, kr, vb, proj, sinks, l)
        yl, hl = _lru_fwd(proj, p["lru_conv_w"], lru_conv_b, wa_bd, lru_ba, wx_bd, lru_bx, lru_lambda, l)
        hn, hnb, xhat, rstd = _out_fwd(yc, ya, yl, w_out, h, ln_post_g, ln_post_b, l)
        saved.append((hb, proj, yc, conv, qr, kr, vb, ya, att, lse, yl, hl, xhat, rstd))
        h, hb = hn, hnb

    loss_part, dh = _loss_head(h, target)
    g = {}
    for l in reversed(range(DEPTH)):
        hb_l, proj, yc, conv, qr, kr, vb, ya, att, lse, yl, hl, xhat, rstd = saved[l]
        dz, dzb, g["ln_post_g", l], g["ln_post_b", l] = _post_ln_bwd(dh, xhat, rstd, ln_post_g, l)
        d_yc, d_ya, d_yl = _dcat_bwd(dzb, w_out, l)
        g["w_out", l] = _dwout_bwd(yc, ya, yl, dzb, l)
        d_conv, d_cgate, dpw, g["conv_pw_b", l], g["conv_ln_g", l], g["conv_ln_b", l] = _conv_bwd_rows(
            conv, proj, d_yc, conv_ln_g, conv_ln_b, pw_w, conv_pw_b, l)
        g["conv_pw_w", l] = dpw.reshape(N_SHARD, 2, PW_SH // 2, CW)
        d_cvg, ddw, g["conv_dw_b", l] = _conv_bwd_taps(d_conv, proj, p["conv_dw_w"], l)
        g["conv_dw_w", l] = ddw[:CONV_K]
        dqr, dk, dv, dk0, dv0, d_ag, dsink = _attn_bwd(qr, kr, vb, proj, att, lse, d_ya, sinks, l)
        g["attn_sinks", l] = dsink[0, :N_HEADS]
        d_qkv = _rope_bwd(dqr, dk, dv, dk0, dv0, tabs, l)
        d_lru, dlw, g["lru_conv_b", l], dwa, g["lru_ba", l], dwx, g["lru_bx", l], g["lru_lambda", l] = _lru_bwd(
            proj, hl, d_yl, p["lru_conv_w"], lru_conv_b, wa_bd, lru_ba, wx_bd, lru_bx, lru_lambda, l)
        g["lru_conv_w", l] = dlw[:LRU_K]
        g["lru_wa", l] = _diag_blocks(dwa)
        g["lru_wx", l] = _diag_blocks(dwx)
        dproj = jnp.concatenate([d_cvg, d_cgate, d_qkv, d_ag, d_lru], axis=1)
        g["w_in", l] = _dwin_bwd(hb_l, dproj, l)
        dh = _dh_bwd(dproj, w_in, dz, l)
    grad_x, g["meta_tokens", -1], g["ln_in_g", -1], g["ln_in_b", -1] = _embed_bwd(
        dh, x, p["meta_tokens"], ln_in_g, ln_in_b)
    return loss_part, grad_x, g


MESH = pl.DeviceIdType.MESH
HBM_SPEC = pl.BlockSpec(memory_space=pltpu.HBM)
N_DEV = 8


def _position():
    x, y, c = lax.axis_index("x"), lax.axis_index("y"), lax.axis_index("c")
    return x, y, c


def _other_chips(x, y):
    return [(1 - x, y), (x, 1 - y), (1 - x, 1 - y)]


def _cast_bf16(a):
    na, nb_, nc = a.shape
    tb = _pick(nb_, (512, 128, nb_))

    def body(a_ref, o_ref):
        o_ref[...] = a_ref[...].astype(BF16)

    spec = pl.BlockSpec((None, tb, nc), lambda i, t: (i, t, 0))
    return pl.pallas_call(
        body, name=f"cast_bf16_{na}x{nb_}x{nc}", grid=(na, nb_ // tb),
        in_specs=[spec], out_specs=spec,
        out_shape=jax.ShapeDtypeStruct(a.shape, BF16),
        compiler_params=_cp("parallel", "parallel"),
    )(a)


def _gather_shards(shards):
    n = len(shards)

    def body(*refs):
        src, dst = refs[:n], refs[n:2 * n]
        send_sems, recv_sems, local_sems = refs[2 * n:]
        x, y, c = _position()
        mine = 2 * x + y
        chips = _other_chips(x, y)

        def copy(k, p):
            return pltpu.make_async_remote_copy(
                src_ref=src[k], dst_ref=dst[k].at[mine],
                send_sem=send_sems.at[k * 3 + p], recv_sem=recv_sems.at[k * 3 + p],
                device_id=(*chips[p], c), device_id_type=MESH)

        def arrival(k, p):
            px, py = chips[p]
            return pltpu.make_async_remote_copy(
                src_ref=src[k], dst_ref=dst[k].at[2 * px + py],
                send_sem=send_sems.at[k * 3 + p], recv_sem=recv_sems.at[k * 3 + p],
                device_id=(px, py, c), device_id_type=MESH)

        local = [pltpu.make_async_copy(src[k], dst[k].at[mine], local_sems.at[k]) for k in range(n)]
        for cp in local:
            cp.start()
        for k in range(n):
            for p in range(3):
                copy(k, p).start()
        for k in range(n):
            for p in range(3):
                arrival(k, p).wait_recv()
        for k in range(n):
            for p in range(3):
                copy(k, p).wait_send()
        for cp in local:
            cp.wait()

    return pl.pallas_call(
        body, name="gather_shards",
        in_specs=[HBM_SPEC] * n, out_specs=[HBM_SPEC] * n,
        out_shape=[jax.ShapeDtypeStruct((N_SHARD,) + s.shape, s.dtype) for s in shards],
        scratch_shapes=[pltpu.SemaphoreType.DMA((3 * n,)), pltpu.SemaphoreType.DMA((3 * n,)),
                        pltpu.SemaphoreType.DMA((n,))],
    )(*shards)


def _swap_halves(grads, l):
    n = len(grads)

    def body(*refs):
        src, dst = refs[:n], refs[n:2 * n]
        send_sems, recv_sems = refs[2 * n:]
        x, y, c = _position()
        copies = [pltpu.make_async_remote_copy(
            src_ref=src[k].at[:, 1 - c], dst_ref=dst[k],
            send_sem=send_sems.at[k], recv_sem=recv_sems.at[k],
            device_id=(x, y, 1 - c), device_id_type=MESH) for k in range(n)]
        for cp in copies:
            cp.start()
        for cp in copies:
            cp.wait()

    return pl.pallas_call(
        body, name=f"swap_halves{l}",
        in_specs=[HBM_SPEC] * n, out_specs=[HBM_SPEC] * n,
        out_shape=[jax.ShapeDtypeStruct((N_SHARD,) + g.shape[2:], F32) for g in grads],
        scratch_shapes=[pltpu.SemaphoreType.DMA((n,)), pltpu.SemaphoreType.DMA((n,))],
    )(*grads)


def _chip_partial(a, y, c, j, tag):
    _, _, R, C = a.shape
    tr = _pick(R, (256, 64))

    def body(s_ref, a_ref, y_ref, pb_ref, po_ref):
        total = a_ref[...] + y_ref[...]
        pb_ref[...] = total.astype(BF16)

        @pl.when(pl.program_id(1) == s_ref[1])
        def _():
            po_ref[...] = total

    grid_spec = pltpu.PrefetchScalarGridSpec(
        num_scalar_prefetch=1, grid=(R // tr, N_SHARD),
        in_specs=[pl.BlockSpec((None, None, tr, C), lambda t, s, sc: (s, sc[0], t, 0)),
                  pl.BlockSpec((None, tr, C), lambda t, s, sc: (s, t, 0))],
        out_specs=[pl.BlockSpec((None, tr, C), lambda t, s, sc: (s, t, 0)),
                   pl.BlockSpec((tr, C), lambda t, s, sc: (t, 0))])
    return pl.pallas_call(
        body, name=f"chip_partial_{tag}", grid_spec=grid_spec,
        out_shape=[jax.ShapeDtypeStruct((N_SHARD, R, C), BF16), jax.ShapeDtypeStruct((R, C), F32)],
        compiler_params=_cp("arbitrary", "arbitrary"),
    )(jnp.stack([c, j]).astype(jnp.int32), a, y)


def _scatter_partials(parts, l):
    n = len(parts)

    def body(*refs):
        src, dst = refs[:n], refs[n:2 * n]
        send_sems, recv_sems = refs[2 * n:]
        x, y, c = _position()
        mine = 2 * x + y
        chips = _other_chips(x, y)

        def copy(k, p):
            px, py = chips[p]
            return pltpu.make_async_remote_copy(
                src_ref=src[k].at[2 * px + py], dst_ref=dst[k].at[mine],
                send_sem=send_sems.at[k * 3 + p], recv_sem=recv_sems.at[k * 3 + p],
                device_id=(px, py, c), device_id_type=MESH)

        def arrival(k, p):
            px, py = chips[p]
            return pltpu.make_async_remote_copy(
                src_ref=src[k].at[mine], dst_ref=dst[k].at[2 * px + py],
                send_sem=send_sems.at[k * 3 + p], recv_sem=recv_sems.at[k * 3 + p],
                device_id=(px, py, c), device_id_type=MESH)

        for k in range(n):
            for p in range(3):
                copy(k, p).start()
        for k in range(n):
            for p in range(3):
                arrival(k, p).wait_recv()
        for k in range(n):
            for p in range(3):
                copy(k, p).wait_send()

    return pl.pallas_call(
        body, name=f"scatter_partials{l}",
        in_specs=[HBM_SPEC] * n, out_specs=[HBM_SPEC] * n,
        out_shape=[jax.ShapeDtypeStruct(pb.shape, BF16) for pb in parts],
        scratch_shapes=[pltpu.SemaphoreType.DMA((3 * n,)), pltpu.SemaphoreType.DMA((3 * n,))],
    )(*parts)


def _shard_total(own, z, others, tag):
    R, C = own.shape
    tr = _pick(R, (256, 64))

    def body(s_ref, o_ref, z0_ref, z1_ref, z2_ref, h_ref):
        h_ref[...] = ((o_ref[...] + z0_ref[...].astype(F32)) + z1_ref[...].astype(F32)) + z2_ref[...].astype(F32)

    zspec = lambda q: pl.BlockSpec((None, tr, C), lambda t, sc: (sc[q], t, 0))
    grid_spec = pltpu.PrefetchScalarGridSpec(
        num_scalar_prefetch=1, grid=(R // tr,),
        in_specs=[pl.BlockSpec((tr, C), lambda t, sc: (t, 0)), zspec(0), zspec(1), zspec(2)],
        out_specs=pl.BlockSpec((tr, C), lambda t, sc: (t, 0)))
    return pl.pallas_call(
        body, name=f"shard_total_{tag}", grid_spec=grid_spec,
        out_shape=jax.ShapeDtypeStruct((R, C), F32),
        compiler_params=_cp("arbitrary"),
    )(others, own, z, z, z)


def _share_halves(halves, l):
    n = len(halves)

    def body(*refs):
        src, dst = refs[:n], refs[n:2 * n]
        send_sems, recv_sems, local_sems = refs[2 * n:]
        x, y, c = _position()
        local = [pltpu.make_async_copy(src[k], dst[k].at[c], local_sems.at[k]) for k in range(n)]
        for cp in local:
            cp.start()
        sends = [pltpu.make_async_remote_copy(
            src_ref=src[k], dst_ref=dst[k].at[c], send_sem=send_sems.at[k], recv_sem=recv_sems.at[k],
            device_id=(x, y, 1 - c), device_id_type=MESH) for k in range(n)]
        for cp in sends:
            cp.start()
        for k in range(n):
            pltpu.make_async_remote_copy(
                src_ref=src[k], dst_ref=dst[k].at[1 - c], send_sem=send_sems.at[k], recv_sem=recv_sems.at[k],
                device_id=(x, y, 1 - c), device_id_type=MESH).wait_recv()
        for cp in sends:
            cp.wait_send()
        for cp in local:
            cp.wait()

    return pl.pallas_call(
        body, name=f"share_halves{l}",
        in_specs=[HBM_SPEC] * n, out_specs=[HBM_SPEC] * n,
        out_shape=[jax.ShapeDtypeStruct((2,) + h.shape, F32) for h in halves],
        scratch_shapes=[pltpu.SemaphoreType.DMA((n,)), pltpu.SemaphoreType.DMA((n,)),
                        pltpu.SemaphoreType.DMA((n,))],
    )(*halves)


def _allreduce_pack(pack):
    ns = pack.shape[0]

    def body(p_ref, o_ref, slots, send_sems, recv_sems):
        x, y, c = _position()
        me = 4 * x + 2 * y + c

        def copy(m):
            peer = (x ^ (m >> 2), y ^ ((m >> 1) & 1), c ^ (m & 1))
            return pltpu.make_async_remote_copy(
                src_ref=p_ref, dst_ref=slots.at[me], send_sem=send_sems.at[m - 1], recv_sem=recv_sems.at[m - 1],
                device_id=peer, device_id_type=MESH)

        def arrival(m):
            peer = (x ^ (m >> 2), y ^ ((m >> 1) & 1), c ^ (m & 1))
            return pltpu.make_async_remote_copy(
                src_ref=p_ref, dst_ref=slots.at[4 * peer[0] + 2 * peer[1] + peer[2]],
                send_sem=send_sems.at[m - 1], recv_sem=recv_sems.at[m - 1],
                device_id=peer, device_id_type=MESH)

        for m in range(1, N_DEV):
            copy(m).start()
        slots[me] = p_ref[...]
        for m in range(1, N_DEV):
            arrival(m).wait_recv()
        acc = slots[0]
        for d in range(1, N_DEV):
            acc = acc + slots[d]
        o_ref[...] = acc
        for m in range(1, N_DEV):
            copy(m).wait_send()

    vm = pl.BlockSpec(memory_space=pltpu.VMEM)
    return pl.pallas_call(
        body, name="allreduce_pack",
        in_specs=[vm], out_specs=vm,
        out_shape=jax.ShapeDtypeStruct(pack.shape, F32),
        scratch_shapes=[pltpu.VMEM((N_DEV, ns, 128), F32),
                        pltpu.SemaphoreType.DMA((N_DEV - 1,)), pltpu.SemaphoreType.DMA((N_DEV - 1,))],
        compiler_params=pltpu.CompilerParams(vmem_limit_bytes=V7X_VMEM_LIMIT),
    )(pack)


def _adamw_math(w, g, m, v):
    m = ADAM_B1 * m + (1.0 - ADAM_B1) * g
    v = ADAM_B2 * v + (1.0 - ADAM_B2) * (g * g)
    m_hat = m / (1.0 - ADAM_B1 ** ADAM_STEP)
    v_hat = v / (1.0 - ADAM_B2 ** ADAM_STEP)
    delta = -ADAM_LR * (m_hat / (jnp.sqrt(v_hat) + ADAM_EPS) + ADAM_WD * w)
    return delta, m, v


def _adamw_big(w, g0, g1, m, v, tag):
    _, R, C = w.shape
    tr = _pick(R, (256, 128))

    def body(w_ref, g0_ref, g1_ref, m_ref, v_ref, go_ref, d_ref, mo_ref, vo_ref):
        g = jnp.where(pl.program_id(0) == 0, g0_ref[...], g1_ref[...])
        delta, mn, vn = _adamw_math(w_ref[...], g, m_ref[...], v_ref[...])
        go_ref[...] = g
        d_ref[...] = delta
        mo_ref[...] = mn
        vo_ref[...] = vn

    s3 = pl.BlockSpec((None, tr, C), lambda l, t: (l, t, 0))
    s2 = pl.BlockSpec((tr, C), lambda l, t: (t, 0))
    shp = jax.ShapeDtypeStruct(w.shape, F32)
    return pl.pallas_call(
        body, name=f"adamw_{tag}", grid=(2, R // tr),
        in_specs=[s3, s2, s2, s3, s3], out_specs=[s3, s3, s3, s3],
        out_shape=[shp, shp, shp, shp],
        compiler_params=_cp("parallel", "parallel"),
    )(w, g0, g1, m, v)


def _adamw_small(ws, gs, ms, vs):
    n = len(ws)

    def body(*refs):
        w_r, g_r, m_r, v_r = refs[:n], refs[n:2 * n], refs[2 * n:3 * n], refs[3 * n:4 * n]
        d_o, m_o, v_o = refs[4 * n:5 * n], refs[5 * n:6 * n], refs[6 * n:7 * n]
        for k in range(n):
            delta, mn, vn = _adamw_math(w_r[k][...], g_r[k][...], m_r[k][...], v_r[k][...])
            d_o[k][...] = delta
            m_o[k][...] = mn
            v_o[k][...] = vn

    vm = pl.BlockSpec(memory_space=pltpu.VMEM)
    shapes = [jax.ShapeDtypeStruct(w.shape, F32) for w in ws]
    outs = pl.pallas_call(
        body, name="adamw_small",
        in_specs=[vm] * (4 * n), out_specs=[vm] * (3 * n),
        out_shape=shapes * 3,
    )(*ws, *gs, *ms, *vs)
    return outs[:n], outs[n:2 * n], outs[2 * n:]


_WEIGHTS = ["meta_tokens", "ln_in_g", "ln_in_b", "w_in", "conv_dw_w", "conv_dw_b", "conv_ln_g", "conv_ln_b",
            "conv_pw_w", "conv_pw_b", "attn_sinks", "lru_conv_w", "lru_conv_b", "lru_wa", "lru_ba", "lru_wx",
            "lru_bx", "lru_lambda", "w_out", "ln_post_g", "ln_post_b"]
_BIG = ("w_in", "w_out", "conv_pw_w")
_SMALL_SHARDED = {"meta_tokens": 1, "conv_dw_w": 2, "lru_conv_w": 2}
PACK_ROWS_ALIGN = 8


def _as2d(a):
    return a.reshape(1, -1) if a.ndim == 1 else a.reshape(-1, a.shape[-1])


def kernel(x, meta_tokens, ln_in_g, ln_in_b, w_in, conv_dw_w, conv_dw_b, conv_ln_g, conv_ln_b, conv_pw_w, conv_pw_b, attn_sinks, lru_conv_w, lru_conv_b, lru_wa, lru_ba, lru_wx, lru_bx, lru_lambda, w_out, ln_post_g, ln_post_b, loss_target, m_meta_tokens, m_ln_in_g, m_ln_in_b, m_w_in, m_conv_dw_w, m_conv_dw_b, m_conv_ln_g, m_conv_ln_b, m_conv_pw_w, m_conv_pw_b, m_attn_sinks, m_lru_conv_w, m_lru_conv_b, m_lru_wa, m_lru_ba, m_lru_wx, m_lru_bx, m_lru_lambda, m_w_out, m_ln_post_g, m_ln_post_b, v_meta_tokens, v_ln_in_g, v_ln_in_b, v_w_in, v_conv_dw_w, v_conv_dw_b, v_conv_ln_g, v_conv_ln_b, v_conv_pw_w, v_conv_pw_b, v_attn_sinks, v_lru_conv_w, v_lru_conv_b, v_lru_wa, v_lru_ba, v_lru_wx, v_lru_bx, v_lru_lambda, v_w_out, v_ln_post_g, v_ln_post_b):
    w = dict(meta_tokens=meta_tokens, ln_in_g=ln_in_g, ln_in_b=ln_in_b, w_in=w_in, conv_dw_w=conv_dw_w,
             conv_dw_b=conv_dw_b, conv_ln_g=conv_ln_g, conv_ln_b=conv_ln_b, conv_pw_w=conv_pw_w,
             conv_pw_b=conv_pw_b, attn_sinks=attn_sinks, lru_conv_w=lru_conv_w, lru_conv_b=lru_conv_b,
             lru_wa=lru_wa, lru_ba=lru_ba, lru_wx=lru_wx, lru_bx=lru_bx, lru_lambda=lru_lambda, w_out=w_out,
             ln_post_g=ln_post_g, ln_post_b=ln_post_b)
    mom_m = dict(zip(_WEIGHTS, (m_meta_tokens, m_ln_in_g, m_ln_in_b, m_w_in, m_conv_dw_w, m_conv_dw_b, m_conv_ln_g,
                                m_conv_ln_b, m_conv_pw_w, m_conv_pw_b, m_attn_sinks, m_lru_conv_w, m_lru_conv_b,
                                m_lru_wa, m_lru_ba, m_lru_wx, m_lru_bx, m_lru_lambda, m_w_out, m_ln_post_g,
                                m_ln_post_b)))
    mom_v = dict(zip(_WEIGHTS, (v_meta_tokens, v_ln_in_g, v_ln_in_b, v_w_in, v_conv_dw_w, v_conv_dw_b, v_conv_ln_g,
                                v_conv_ln_b, v_conv_pw_w, v_conv_pw_b, v_attn_sinks, v_lru_conv_w, v_lru_conv_b,
                                v_lru_wa, v_lru_ba, v_lru_wx, v_lru_bx, v_lru_lambda, v_w_out, v_ln_post_g,
                                v_ln_post_b)))
    xi, yi, ci = _position()
    j = 2 * xi + yi

    g_win, g_wout, g_pw, g_meta, g_dw, g_lc = _gather_shards(
        [_cast_bf16(w_in), _cast_bf16(w_out), _cast_bf16(conv_pw_w), meta_tokens, conv_dw_w, lru_conv_w])
    p = dict(w)
    p["w_in"], p["w_out"] = g_win, g_wout
    p["conv_pw_w"] = g_pw.transpose(1, 0, 2, 3).reshape(DEPTH, CW, CW)
    p["meta_tokens"] = g_meta.transpose(1, 0, 2).reshape(N_META, D)
    p["conv_dw_w"] = g_dw.transpose(1, 2, 0, 3).reshape(DEPTH, CONV_K, CW)
    p["lru_conv_w"] = g_lc.transpose(1, 2, 0, 3).reshape(DEPTH, LRU_K, LW)

    loss_part, grad_x, g = _device_step(x[0], loss_target[0], p)
    loss = lax.psum(jnp.sum(loss_part), ("x", "y", "c"))

    others = jnp.stack([jnp.where(j <= 0, 1, 0), jnp.where(j <= 1, 2, 1), jnp.where(j <= 2, 3, 2)]).astype(jnp.int32)
    big = {}
    for l in range(DEPTH):
        grads = [g[name, l] for name in _BIG]
        recv = _swap_halves(grads, l)
        parts, owns = [], []
        for name, a, r in zip(_BIG, grads, recv):
            pb, po = _chip_partial(a, r, ci, j, f"{name}{l}")
            parts.append(pb)
            owns.append(po)
        z = _scatter_partials(parts, l)
        halves = [_shard_total(po, zz, others, f"{name}{l}") for name, po, zz in zip(_BIG, owns, z)]
        full = _share_halves(halves, l)
        for name, f in zip(_BIG, full):
            big[name, l] = f.reshape(2 * f.shape[1], f.shape[2])

    small_names = [n for n in _WEIGHTS if n not in _BIG]

    def full_grad(name):
        if (name, -1) in g:
            return g[name, -1]
        return jnp.stack([g[name, l] for l in range(DEPTH)], axis=0)

    flats = [full_grad(n).reshape(-1) for n in small_names]
    sizes = [f.shape[0] for f in flats]
    total = sum(sizes)
    rows = -(-total // 128)
    rows = -(-rows // PACK_ROWS_ALIGN) * PACK_ROWS_ALIGN
    pack = jnp.concatenate(flats + [jnp.zeros((rows * 128 - total,), F32)]).reshape(rows, 128)
    red = _allreduce_pack(pack).reshape(-1)
    small_g = {}
    off = 0
    for n, sz in zip(small_names, sizes):
        full = red[off:off + sz]
        off += sz
        if n in _SMALL_SHARDED:
            ax = _SMALL_SHARDED[n]
            fshape = list(w[n].shape)
            fshape[ax] *= N_SHARD
            full = full.reshape(fshape)
            small_g[n] = lax.dynamic_slice_in_dim(full, j * w[n].shape[ax], w[n].shape[ax], axis=ax)
        else:
            small_g[n] = full.reshape(w[n].shape)

    out_g, out_d, out_m, out_v = {}, {}, {}, {}
    for name in _BIG:
        shp = w[name].shape
        to3 = lambda a: a.reshape(DEPTH, -1, shp[-1])
        go, do, mo, vo = _adamw_big(to3(w[name]), big[name, 0], big[name, 1], to3(mom_m[name]), to3(mom_v[name]), name)
        out_g[name], out_d[name], out_m[name], out_v[name] = (a.reshape(shp) for a in (go, do, mo, vo))
    ds, ms, vs = _adamw_small([_as2d(w[n]) for n in small_names], [_as2d(small_g[n]) for n in small_names],
                              [_as2d(mom_m[n]) for n in small_names], [_as2d(mom_v[n]) for n in small_names])
    for n, d_, m_, v_ in zip(small_names, ds, ms, vs):
        out_g[n] = small_g[n]
        out_d[n], out_m[n], out_v[n] = d_.reshape(w[n].shape), m_.reshape(w[n].shape), v_.reshape(w[n].shape)

    return (loss, grad_x[None], *[out_g[n] for n in _WEIGHTS], *[out_d[n] for n in _WEIGHTS],
            *[out_m[n] for n in _WEIGHTS], *[out_v[n] for n in _WEIGHTS])
```

```python
import functools

import jax
import jax.numpy as jnp
from jax import lax
from jax.experimental import pallas as pl
from jax.experimental.pallas import tpu as pltpu

F32 = jnp.float32
BF16 = jnp.bfloat16

D = 2048
N_META = 16
CW = 512
CONV_K = 31
AW = 1024
KVW = 256
N_HEADS = 16
LW = 512
LRU_K = 4
LRU_C = 8.0
IN_TOTAL = 5120
ROT_HALF = 8
ROPE_THETA = 500000.0
LN_EPS = 1e-5
DEPTH = 2
ALPHA = (2.0 * DEPTH) ** 0.25
NEG_INF = -1e30
ADAM_LR, ADAM_B1, ADAM_B2, ADAM_EPS, ADAM_WD, ADAM_STEP = 0.001, 0.9, 0.999, 1e-08, 0.01, 10

BLK = 128
PAD = BLK - N_META
N_SHARD = 4
WIN_SH = IN_TOTAL // N_SHARD
WOUT_SH = D // N_SHARD
PW_SH = CW // N_SHARD
HALO = 32
LHALO = 8
V7X_VMEM_LIMIT = 60 * 1024 * 1024


def _cp(*sem):
    return pltpu.CompilerParams(dimension_semantics=sem if sem else None, vmem_limit_bytes=V7X_VMEM_LIMIT)


def _pick(total, prefs):
    for p in prefs:
        if total % p == 0:
            return p
    raise ValueError(f"no tile for {total}")


def _dot(a, b):
    return jnp.dot(a, b, preferred_element_type=F32)


def _dot_nt(a, b):
    return lax.dot_general(a, b, (((1,), (1,)), ((), ())), preferred_element_type=F32)


def _dot_tn(a, b):
    return lax.dot_general(a, b, (((0,), (0,)), ((), ())), preferred_element_type=F32)


def _sigmoid(x):
    return 1.0 / (1.0 + jnp.exp(-x))


def _silu_and_grad(x):
    s = _sigmoid(x)
    return x * s, s * (1.0 + x * (1.0 - s))


def _ln_rows(x, g, b):
    mu = jnp.mean(x, axis=-1, keepdims=True)
    xc = x - mu
    var = jnp.mean(xc * xc, axis=-1, keepdims=True)
    rstd = lax.rsqrt(var + LN_EPS)
    xhat = xc * rstd
    return xhat * g + b, xhat, rstd


def _ln_bwd_rows(dy, xhat, rstd, g):
    dxh = dy * g
    m1 = jnp.mean(dxh, axis=-1, keepdims=True)
    m2 = jnp.mean(dxh * xhat, axis=-1, keepdims=True)
    return rstd * (dxh - m1 - xhat * m2)


def _row_ids(n, base):
    return base + lax.broadcasted_iota(jnp.int32, (n, 1), 0)


def _colsum(x):
    return jnp.sum(x, axis=0, keepdims=True)


def _embed_fwd(x, meta, g, b):
    S = x.shape[0]
    nb = S // BLK + 1

    def body(x_ref, meta_ref, g_ref, b_ref, h_ref, hb_ref):
        n = pl.program_id(0)

        @pl.when(n == 0)
        def _():
            y, _, _ = _ln_rows(meta_ref[...], g_ref[...], b_ref[...])
            h_ref[...] = jnp.zeros_like(h_ref)
            h_ref[PAD:BLK, :] = y

        @pl.when(n > 0)
        def _():
            y, _, _ = _ln_rows(x_ref[...], g_ref[...], b_ref[...])
            h_ref[...] = y

        hb_ref[...] = h_ref[...].astype(BF16)

    return pl.pallas_call(
        body, name="embed_fwd", grid=(nb,),
        in_specs=[pl.BlockSpec((BLK, D), lambda n: (jnp.maximum(n - 1, 0), 0)),
                  pl.BlockSpec((N_META, D), lambda n: (0, 0)),
                  pl.BlockSpec((1, D), lambda n: (0, 0)),
                  pl.BlockSpec((1, D), lambda n: (0, 0))],
        out_specs=[pl.BlockSpec((BLK, D), lambda n: (n, 0)),
                   pl.BlockSpec((BLK, D), lambda n: (n, 0))],
        out_shape=[jax.ShapeDtypeStruct((nb * BLK, D), F32), jax.ShapeDtypeStruct((nb * BLK, D), BF16)],
        compiler_params=_cp("arbitrary"),
    )(x, meta, g, b)


def _embed_bwd(dh, x, meta, g, b):
    S = x.shape[0]
    nb = S // BLK + 1

    def body(dh_ref, x_ref, meta_ref, g_ref, b_ref, gx_ref, gm_ref, dg_ref, db_ref):
        n = pl.program_id(0)

        @pl.when(n == 0)
        def _():
            _, xhat, rstd = _ln_rows(meta_ref[...], g_ref[...], b_ref[...])
            dy = dh_ref[PAD:BLK, :]
            gm_ref[...] = _ln_bwd_rows(dy, xhat, rstd, g_ref[...])
            dg_ref[...] = _colsum(dy * xhat)
            db_ref[...] = _colsum(dy)

        @pl.when(n > 0)
        def _():
            _, xhat, rstd = _ln_rows(x_ref[...], g_ref[...], b_ref[...])
            dy = dh_ref[...]
            gx_ref[...] = _ln_bwd_rows(dy, xhat, rstd, g_ref[...])
            dg_ref[...] += _colsum(dy * xhat)
            db_ref[...] += _colsum(dy)

    prev = lambda n: (jnp.maximum(n - 1, 0), 0)
    const = lambda n: (0, 0)
    return pl.pallas_call(
        body, name="embed_bwd", grid=(nb,),
        in_specs=[pl.BlockSpec((BLK, D), lambda n: (n, 0)),
                  pl.BlockSpec((BLK, D), prev),
                  pl.BlockSpec((N_META, D), const),
                  pl.BlockSpec((1, D), const),
                  pl.BlockSpec((1, D), const)],
        out_specs=[pl.BlockSpec((BLK, D), prev),
                   pl.BlockSpec((N_META, D), const),
                   pl.BlockSpec((1, D), const),
                   pl.BlockSpec((1, D), const)],
        out_shape=[jax.ShapeDtypeStruct((S, D), F32), jax.ShapeDtypeStruct((N_META, D), F32),
                   jax.ShapeDtypeStruct((1, D), F32), jax.ShapeDtypeStruct((1, D), F32)],
        compiler_params=_cp("arbitrary"),
    )(dh, x, meta, g, b)


def _loss_head(h, target):
    T = h.shape[0]
    nb = T // BLK

    def body(h_ref, t_ref, part_ref, dy_ref):
        n = pl.program_id(0)

        @pl.when(n == 0)
        def _():
            part_ref[...] = jnp.zeros_like(part_ref)
            dy_ref[...] = jnp.zeros_like(dy_ref)

        @pl.when(n > 0)
        def _():
            err = h_ref[...] - t_ref[...]
            part_ref[...] += _colsum(err * err) * (0.5 / D)
            dy_ref[...] = err * (1.0 / D)

    return pl.pallas_call(
        body, name="loss_head", grid=(nb,),
        in_specs=[pl.BlockSpec((BLK, D), lambda n: (n, 0)),
                  pl.BlockSpec((BLK, D), lambda n: (jnp.maximum(n - 1, 0), 0))],
        out_specs=[pl.BlockSpec((1, D), lambda n: (0, 0)),
                   pl.BlockSpec((BLK, D), lambda n: (n, 0))],
        out_shape=[jax.ShapeDtypeStruct((1, D), F32), jax.ShapeDtypeStruct((T, D), F32)],
        compiler_params=_cp("arbitrary"),
    )(h, target)


def _proj_fwd(hb, w_in, l):
    T = hb.shape[0]
    tm = _pick(T, (1056, 384, 128))

    def body(a_ref, w_ref, o_ref):
        o_ref[...] = _dot(a_ref[...], w_ref[...])

    return pl.pallas_call(
        body, name=f"proj_fwd{l}", grid=(T // tm, N_SHARD),
        in_specs=[pl.BlockSpec((tm, D), lambda i, j: (i, 0)),
                  pl.BlockSpec((None, D, WIN_SH), lambda i, j: (j, 0, 0))],
        out_specs=pl.BlockSpec((tm, WIN_SH), lambda i, j: (i, j)),
        out_shape=jax.ShapeDtypeStruct((T, IN_TOTAL), F32),
        compiler_params=_cp("parallel", "arbitrary"),
    )(hb, w_in)


def _out_fwd(yc, ya, yl, w_out, h, g, b, l):
    T = h.shape[0]
    tm = _pick(T, (384, 128))

    def body(yc_ref, ya_ref, yl_ref, w_ref, h_ref, g_ref, b_ref, hn_ref, hnb_ref, xh_ref, rs_ref):
        acc = _dot(yc_ref[...], w_ref[0])
        acc += _dot(ya_ref[:, 0:WOUT_SH], w_ref[1])
        acc += _dot(ya_ref[:, WOUT_SH:2 * WOUT_SH], w_ref[2])
        acc += _dot(yl_ref[...], w_ref[3])
        z = ALPHA * h_ref[...] + acc
        y, xhat, rstd = _ln_rows(z, g_ref[...], b_ref[...])
        hn_ref[...] = y
        hnb_ref[...] = y.astype(BF16)
        xh_ref[...] = xhat
        rs_ref[...] = rstd

    row = lambda i: (i, 0)
    return pl.pallas_call(
        body, name=f"out_fwd{l}", grid=(T // tm,),
        in_specs=[pl.BlockSpec((tm, CW), row), pl.BlockSpec((tm, AW), row), pl.BlockSpec((tm, LW), row),
                  pl.BlockSpec((N_SHARD, WOUT_SH, D), lambda i: (0, 0, 0)),
                  pl.BlockSpec((tm, D), row),
                  pl.BlockSpec((None, 1, D), lambda i: (l, 0, 0)),
                  pl.BlockSpec((None, 1, D), lambda i: (l, 0, 0))],
        out_specs=[pl.BlockSpec((tm, D), row), pl.BlockSpec((tm, D), row), pl.BlockSpec((tm, D), row),
                   pl.BlockSpec((tm, 1), row)],
        out_shape=[jax.ShapeDtypeStruct((T, D), F32), jax.ShapeDtypeStruct((T, D), BF16),
                   jax.ShapeDtypeStruct((T, D), F32), jax.ShapeDtypeStruct((T, 1), F32)],
        compiler_params=_cp("parallel"),
    )(yc, ya, yl, w_out, h, g, b)


def _post_ln_bwd(dhn, xhat, rstd, g, l):
    T = dhn.shape[0]
    tm = _pick(T, (384, 128))

    def body(d_ref, xh_ref, rs_ref, g_ref, dz_ref, dzb_ref, dg_ref, db_ref):
        @pl.when(pl.program_id(0) == 0)
        def _():
            dg_ref[...] = jnp.zeros_like(dg_ref)
            db_ref[...] = jnp.zeros_like(db_ref)

        dy = d_ref[...]
        xhat = xh_ref[...]
        dz = _ln_bwd_rows(dy, xhat, rs_ref[...], g_ref[...])
        dz_ref[...] = dz
        dzb_ref[...] = dz.astype(BF16)
        dg_ref[...] += _colsum(dy * xhat)
        db_ref[...] += _colsum(dy)

    row = lambda i: (i, 0)
    const = lambda i: (0, 0)
    return pl.pallas_call(
        body, name=f"post_ln_bwd{l}", grid=(T // tm,),
        in_specs=[pl.BlockSpec((tm, D), row), pl.BlockSpec((tm, D), row), pl.BlockSpec((tm, 1), row),
                  pl.BlockSpec((None, 1, D), lambda i: (l, 0, 0))],
        out_specs=[pl.BlockSpec((tm, D), row), pl.BlockSpec((tm, D), row),
                   pl.BlockSpec((1, D), const), pl.BlockSpec((1, D), const)],
        out_shape=[jax.ShapeDtypeStruct((T, D), F32), jax.ShapeDtypeStruct((T, D), BF16),
                   jax.ShapeDtypeStruct((1, D), F32), jax.ShapeDtypeStruct((1, D), F32)],
        compiler_params=_cp("arbitrary"),
    )(dhn, xhat, rstd, g)


def _dcat_bwd(dzb, w_out, l):
    T = dzb.shape[0]
    tm = _pick(T, (384, 128))

    def body(dz_ref, w_ref, dc_ref, da_ref, dl_ref):
        dz = dz_ref[...]
        dc_ref[...] = _dot_nt(dz, w_ref[0])
        da_ref[:, 0:WOUT_SH] = _dot_nt(dz, w_ref[1])
        da_ref[:, WOUT_SH:2 * WOUT_SH] = _dot_nt(dz, w_ref[2])
        dl_ref[...] = _dot_nt(dz, w_ref[3])

    row = lambda i: (i, 0)
    return pl.pallas_call(
        body, name=f"dcat_bwd{l}", grid=(T // tm,),
        in_specs=[pl.BlockSpec((tm, D), row),
                  pl.BlockSpec((N_SHARD, WOUT_SH, D), lambda i: (0, 0, 0))],
        out_specs=[pl.BlockSpec((tm, CW), row), pl.BlockSpec((tm, AW), row), pl.BlockSpec((tm, LW), row)],
        out_shape=[jax.ShapeDtypeStruct((T, CW), F32), jax.ShapeDtypeStruct((T, AW), F32),
                   jax.ShapeDtypeStruct((T, LW), F32)],
        compiler_params=_cp("parallel"),
    )(dzb, w_out)


def _dwout_bwd(yc, ya, yl, dzb, l):
    T = dzb.shape[0]
    tm = _pick(T, (1056, 384, 128))
    hr = WOUT_SH // 2
    nt = T // tm

    def body(yc_ref, ya_ref, yl_ref, dz_ref, o_ref):
        j = pl.program_id(0)
        t = pl.program_id(2)

        @pl.when(t == 0)
        def _():
            o_ref[...] = jnp.zeros_like(o_ref)

        dz = dz_ref[...]

        @pl.when(j == 0)
        def _():
            o_ref[...] += _dot_tn(yc_ref[...], dz)

        @pl.when((j == 1) | (j == 2))
        def _():
            o_ref[...] += _dot_tn(ya_ref[...], dz)

        @pl.when(j == 3)
        def _():
            o_ref[...] += _dot_tn(yl_ref[...], dz)

    return pl.pallas_call(
        body, name=f"dwout_bwd{l}", grid=(N_SHARD, 2, nt),
        in_specs=[pl.BlockSpec((tm, hr), lambda j, r, t: (t, r)),
                  pl.BlockSpec((tm, hr), lambda j, r, t: (t, 2 * jnp.clip(j - 1, 0, 1) + r)),
                  pl.BlockSpec((tm, hr), lambda j, r, t: (t, r)),
                  pl.BlockSpec((tm, D), lambda j, r, t: (t, 0))],
        out_specs=pl.BlockSpec((None, None, hr, D), lambda j, r, t: (j, r, 0, 0)),
        out_shape=jax.ShapeDtypeStruct((N_SHARD, 2, hr, D), F32),
        compiler_params=_cp("parallel", "parallel", "arbitrary"),
    )(yc, ya, yl, dzb)


def _dh_bwd(dproj, w_in, dz, l):
    T = dproj.shape[0]
    tm = _pick(T, (1056, 384, 128))

    def body(dp_ref, w_ref, dz_ref, o_ref, acc_ref):
        j = pl.program_id(1)

        @pl.when(j == 0)
        def _():
            acc_ref[...] = ALPHA * dz_ref[...]

        acc_ref[...] += _dot_nt(dp_ref[...], w_ref[...])

        @pl.when(j == N_SHARD - 1)
        def _():
            o_ref[...] = acc_ref[...]

    return pl.pallas_call(
        body, name=f"dh_bwd{l}", grid=(T // tm, N_SHARD),
        in_specs=[pl.BlockSpec((tm, WIN_SH), lambda i, j: (i, j)),
                  pl.BlockSpec((None, D, WIN_SH), lambda i, j: (j, 0, 0)),
                  pl.BlockSpec((tm, D), lambda i, j: (i, 0))],
        out_specs=pl.BlockSpec((tm, D), lambda i, j: (i, 0)),
        out_shape=jax.ShapeDtypeStruct((T, D), F32),
        scratch_shapes=[pltpu.VMEM((tm, D), F32)],
        compiler_params=_cp("parallel", "arbitrary"),
    )(dproj, w_in, dz)


def _dwin_bwd(hb, dproj, l):
    T = hb.shape[0]
    tm = _pick(T, (1056, 384, 128))
    hr = D // 2

    def body(h_ref, dp_ref, o_ref):
        @pl.when(pl.program_id(2) == 0)
        def _():
            o_ref[...] = jnp.zeros_like(o_ref)

        o_ref[...] += _dot_tn(h_ref[...], dp_ref[...])

    return pl.pallas_call(
        body, name=f"dwin_bwd{l}", grid=(N_SHARD, 2, T // tm),
        in_specs=[pl.BlockSpec((tm, hr), lambda j, r, t: (t, r)),
                  pl.BlockSpec((tm, WIN_SH), lambda j, r, t: (t, j))],
        out_specs=pl.BlockSpec((None, None, hr, WIN_SH), lambda j, r, t: (j, r, 0, 0)),
        out_shape=jax.ShapeDtypeStruct((N_SHARD, 2, hr, WIN_SH), F32),
        compiler_params=_cp("parallel", "parallel", "arbitrary"),
    )(hb, dproj)


def _glu_masked(v, g, base_row):
    rows = _row_ids(v.shape[0], base_row)
    return jnp.where(rows >= PAD, v * _sigmoid(g), 0.0)


def _conv_tile(T):
    return _pick(T, (384, 128))


def _conv_fwd(proj, dw_w, dw_b, ln_g, ln_b, pw_w, pw_b, l):
    T = proj.shape[0]
    tm = _conv_tile(T)
    hb = tm // HALO

    def body(cv_ref, cg_ref, ct_ref, hv_ref, hg_ref, w_ref, b_ref, g_ref, be_ref, pw_ref, pb_ref,
             yc_ref, conv_ref, buf):
        i = pl.program_id(0)
        buf[0:HALO, :] = _glu_masked(hv_ref[...], hg_ref[...], i * tm - HALO)
        buf[HALO:HALO + tm, :] = _glu_masked(cv_ref[...], cg_ref[...], i * tm)
        acc = jnp.zeros((tm, CW), F32) + b_ref[...]
        for k in range(CONV_K):
            o = HALO - (CONV_K - 1) + k
            acc += w_ref[k:k + 1, :] * buf[o:o + tm, :]
        conv_ref[...] = acc
        u, _, _ = _ln_rows(acc, g_ref[...], be_ref[...])
        s = u * _sigmoid(u)
        cpw = _dot(s.astype(BF16), pw_ref[...]) + pb_ref[...]
        gate, _ = _silu_and_grad(ct_ref[...])
        yc_ref[...] = (cpw * gate).astype(BF16)

    vec = pl.BlockSpec((None, 1, CW), lambda i: (l, 0, 0))
    return pl.pallas_call(
        body, name=f"conv_fwd{l}", grid=(T // tm,),
        in_specs=[pl.BlockSpec((tm, CW), lambda i: (i, 0)),
                  pl.BlockSpec((tm, CW), lambda i: (i, 1)),
                  pl.BlockSpec((tm, CW), lambda i: (i, 2)),
                  pl.BlockSpec((HALO, CW), lambda i: (jnp.maximum(i * hb - 1, 0), 0)),
                  pl.BlockSpec((HALO, CW), lambda i: (jnp.maximum(i * hb - 1, 0), 1)),
                  pl.BlockSpec((None, CONV_K, CW), lambda i: (l, 0, 0)),
                  vec, vec, vec,
                  pl.BlockSpec((CW, CW), lambda i: (0, 0)),
                  vec],
        out_specs=[pl.BlockSpec((tm, CW), lambda i: (i, 0)), pl.BlockSpec((tm, CW), lambda i: (i, 0))],
        out_shape=[jax.ShapeDtypeStruct((T, CW), BF16), jax.ShapeDtypeStruct((T, CW), F32)],
        scratch_shapes=[pltpu.VMEM((tm + HALO, CW), F32)],
        compiler_params=_cp("parallel"),
    )(proj, proj, proj, proj, proj, dw_w, dw_b, ln_g, ln_b, pw_w, pw_b)


def _conv_bwd_rows(conv, proj, d_yc, ln_g, ln_b, pw_w, pw_b, l):
    T = conv.shape[0]
    tm = _conv_tile(T)

    def body(conv_ref, ct_ref, dy_ref, g_ref, be_ref, pw_ref, pb_ref,
             dconv_ref, dct_ref, dpw_ref, dpb_ref, dg_ref, db_ref):
        @pl.when(pl.program_id(0) == 0)
        def _():
            dpw_ref[...] = jnp.zeros_like(dpw_ref)
            dpb_ref[...] = jnp.zeros_like(dpb_ref)
            dg_ref[...] = jnp.zeros_like(dg_ref)
            db_ref[...] = jnp.zeros_like(db_ref)

        u, xhat, rstd = _ln_rows(conv_ref[...], g_ref[...], be_ref[...])
        s, ds_du = _silu_and_grad(u)
        sb = s.astype(BF16)
        cpw = _dot(sb, pw_ref[...]) + pb_ref[...]
        gate, dgate = _silu_and_grad(ct_ref[...])
        dy = dy_ref[...]
        d_cpw = dy * gate
        dct_ref[...] = (dy * cpw * dgate).astype(BF16)
        d_cpw_b = d_cpw.astype(BF16)
        dpb_ref[...] += _colsum(d_cpw)
        dpw_ref[...] += _dot_tn(sb, d_cpw_b)
        du = _dot_nt(d_cpw_b, pw_ref[...]) * ds_du
        dconv_ref[...] = _ln_bwd_rows(du, xhat, rstd, g_ref[...])
        dg_ref[...] += _colsum(du * xhat)
        db_ref[...] += _colsum(du)

    vec = pl.BlockSpec((None, 1, CW), lambda i: (l, 0, 0))
    row = lambda i: (i, 0)
    const = lambda i: (0, 0)
    return pl.pallas_call(
        body, name=f"conv_bwd_rows{l}", grid=(T // tm,),
        in_specs=[pl.BlockSpec((tm, CW), row), pl.BlockSpec((tm, CW), lambda i: (i, 2)),
                  pl.BlockSpec((tm, CW), row), vec, vec,
                  pl.BlockSpec((CW, CW), lambda i: (0, 0)), vec],
        out_specs=[pl.BlockSpec((tm, CW), row), pl.BlockSpec((tm, CW), lambda i: (i, 2)),
                   pl.BlockSpec((CW, CW), const), pl.BlockSpec((1, CW), const),
                   pl.BlockSpec((1, CW), const), pl.BlockSpec((1, CW), const)],
        out_shape=[jax.ShapeDtypeStruct((T, CW), F32), jax.ShapeDtypeStruct((T, IN_TOTAL), BF16),
                   jax.ShapeDtypeStruct((CW, CW), F32), jax.ShapeDtypeStruct((1, CW), F32),
                   jax.ShapeDtypeStruct((1, CW), F32), jax.ShapeDtypeStruct((1, CW), F32)],
        compiler_params=_cp("arbitrary"),
    )(conv, proj, d_yc, ln_g, ln_b, pw_w, pw_b)


def _conv_bwd_taps(d_conv, proj, dw_w, dproj, l):
    T = d_conv.shape[0]
    tm = _conv_tile(T)
    hb = tm // HALO
    nt = T // tm
    last_halo = T // HALO - 1

    def body(dc_ref, dh_ref, cv_ref, cg_ref, hv_ref, hg_ref, w_ref, _, o_ref, dw_ref, dwb_ref, cbuf, dbuf):
        i = pl.program_id(0)

        @pl.when(i == 0)
        def _():
            dw_ref[...] = jnp.zeros_like(dw_ref)
            dwb_ref[...] = jnp.zeros_like(dwb_ref)

        cbuf[0:HALO, :] = _glu_masked(hv_ref[...], hg_ref[...], i * tm - HALO)
        cbuf[HALO:HALO + tm, :] = _glu_masked(cv_ref[...], cg_ref[...], i * tm)
        dmain = dc_ref[...]
        dbuf[0:tm, :] = dmain
        dbuf[tm:tm + HALO, :] = jnp.where(i < nt - 1, dh_ref[...], 0.0)
        acc = jnp.zeros((tm, CW), F32)
        for k in range(CONV_K):
            o = CONV_K - 1 - k
            acc += w_ref[k:k + 1, :] * dbuf[o:o + tm, :]
            oc = HALO - (CONV_K - 1) + k
            dw_ref[k:k + 1, :] += _colsum(dmain * cbuf[oc:oc + tm, :])
        dwb_ref[...] += _colsum(dmain)
        d_c = jnp.where(_row_ids(tm, i * tm) >= PAD, acc, 0.0)
        sig = _sigmoid(cg_ref[...])
        o_ref[:, 0:CW] = (d_c * sig).astype(BF16)
        o_ref[:, CW:2 * CW] = (d_c * cv_ref[...] * sig * (1.0 - sig)).astype(BF16)

    const = lambda i: (0, 0)
    return pl.pallas_call(
        body, name=f"conv_bwd_taps{l}", grid=(nt,),
        in_specs=[pl.BlockSpec((tm, CW), lambda i: (i, 0)),
                  pl.BlockSpec((HALO, CW), lambda i: (jnp.minimum((i + 1) * hb, last_halo), 0)),
                  pl.BlockSpec((tm, CW), lambda i: (i, 0)),
                  pl.BlockSpec((tm, CW), lambda i: (i, 1)),
                  pl.BlockSpec((HALO, CW), lambda i: (jnp.maximum(i * hb - 1, 0), 0)),
                  pl.BlockSpec((HALO, CW), lambda i: (jnp.maximum(i * hb - 1, 0), 1)),
                  pl.BlockSpec((None, CONV_K, CW), lambda i: (l, 0, 0)),
                  pl.BlockSpec(memory_space=pl.ANY)],
        out_specs=[pl.BlockSpec((tm, 2 * CW), lambda i: (i, 0)),
                   pl.BlockSpec((HALO, CW), const), pl.BlockSpec((1, CW), const)],
        out_shape=[jax.ShapeDtypeStruct(dproj.shape, BF16), jax.ShapeDtypeStruct((HALO, CW), F32),
                   jax.ShapeDtypeStruct((1, CW), F32)],
        scratch_shapes=[pltpu.VMEM((tm + HALO, CW), F32), pltpu.VMEM((tm + HALO, CW), F32)],
        input_output_aliases={7: 0},
        compiler_params=_cp("arbitrary"),
    )(d_conv, d_conv, proj, proj, proj, proj, dw_w, dproj)


def _log1p_small(e):
    return jnp.where(e < 1e-3, e * (1.0 - e * (0.5 - e * (1.0 / 3.0))), jnp.log(1.0 + e))


def _softplus(z):
    return jnp.maximum(z, 0.0) + _log1p_small(jnp.exp(-jnp.abs(z)))


def _neg_expm1(x):
    series = -x * (1.0 + x * (1.0 / 2.0) * (1.0 + x * (1.0 / 3.0) * (1.0 + x * (1.0 / 4.0) * (
        1.0 + x * (1.0 / 5.0) * (1.0 + x * (1.0 / 6.0) * (1.0 + x * (1.0 / 7.0)))))))
    return jnp.where(x > -0.25, series, 1.0 - jnp.exp(x))


def _lru_gates(rxbuf, tm, base_row, lw_ref, lb_ref, wa_ref, ba_ref, wx_ref, bx_ref, lam_ref):
    rc = jnp.zeros((tm, LW), F32) + lb_ref[...]
    for k in range(LRU_K):
        o = LHALO - (LRU_K - 1) + k
        rc += lw_ref[k:k + 1, :] * rxbuf[o:o + tm, :]
    rcb = rc.astype(BF16)
    r = _sigmoid(_dot(rcb, wa_ref[...]) + ba_ref[...])
    ig = _sigmoid(_dot(rcb, wx_ref[...]) + bx_ref[...])
    sp = _softplus(-lam_ref[...])
    la = -LRU_C * r * sp
    a = jnp.exp(la)
    mult = jnp.sqrt(_neg_expm1(2.0 * la))
    valid = _row_ids(tm, base_row) >= PAD
    return rc, rcb, r, ig, sp, a, mult, valid


def _mask_rows(v, base_row):
    return jnp.where(_row_ids(v.shape[0], base_row) >= PAD, v, 0.0)


def _scan_steps(tm):
    s, out = 1, []
    while s < tm:
        out.append(s)
        s *= 2
    return out


def _lru_tile(T):
    return _pick(T, (384, 128))


def _lru_fwd(proj, lw, lb, wa, ba, wx, bx, lam, l):
    T = proj.shape[0]
    tm = _lru_tile(T)
    hb = tm // LHALO

    def body(rx_ref, rg_ref, hx_ref, lw_ref, lb_ref, wa_ref, ba_ref, wx_ref, bx_ref, lam_ref,
             yl_ref, hl_ref, rxbuf, carry):
        i = pl.program_id(0)

        @pl.when(i == 0)
        def _():
            carry[...] = jnp.zeros_like(carry)

        rxbuf[0:LHALO, :] = _mask_rows(hx_ref[...], i * tm - LHALO)
        rxbuf[LHALO:LHALO + tm, :] = _mask_rows(rx_ref[...], i * tm)
        rc, _, _, ig, _, a, mult, valid = _lru_gates(rxbuf, tm, i * tm, lw_ref, lb_ref, wa_ref, ba_ref,
                                                     wx_ref, bx_ref, lam_ref)
        bb = jnp.where(valid, mult * (ig * rc), 0.0)
        aa = a
        rows = _row_ids(tm, 0)
        for s in _scan_steps(tm):
            keep = rows >= s
            a_s = jnp.where(keep, pltpu.roll(aa, s, axis=0), 1.0)
            b_s = jnp.where(keep, pltpu.roll(bb, s, axis=0), 0.0)
            bb = aa * b_s + bb
            aa = aa * a_s
        h = bb + aa * carry[0:1, :]
        hl_ref[...] = h
        carry[0:1, :] = hl_ref[tm - 1:tm, :]
        gate, _ = _silu_and_grad(rg_ref[...])
        yl_ref[...] = (h * gate).astype(BF16)

    vec = pl.BlockSpec((None, 1, LW), lambda i: (l, 0, 0))
    mat = pl.BlockSpec((None, LW, LW), lambda i: (l, 0, 0))
    return pl.pallas_call(
        body, name=f"lru_fwd{l}", grid=(T // tm,),
        in_specs=[pl.BlockSpec((tm, LW), lambda i: (i, 8)),
                  pl.BlockSpec((tm, LW), lambda i: (i, 9)),
                  pl.BlockSpec((LHALO, LW), lambda i: (jnp.maximum(i * hb - 1, 0), 8)),
                  pl.BlockSpec((None, LRU_K, LW), lambda i: (l, 0, 0)),
                  vec, mat, vec, mat, vec, vec],
        out_specs=[pl.BlockSpec((tm, LW), lambda i: (i, 0)), pl.BlockSpec((tm, LW), lambda i: (i, 0))],
        out_shape=[jax.ShapeDtypeStruct((T, LW), BF16), jax.ShapeDtypeStruct((T, LW), F32)],
        scratch_shapes=[pltpu.VMEM((tm + LHALO, LW), F32), pltpu.VMEM((8, LW), F32)],
        compiler_params=_cp("arbitrary"),
    )(proj, proj, proj, lw, lb, wa, ba, wx, bx, lam)


def _lru_bwd(proj, hl, d_yl, lw, lb, wa, ba, wx, bx, lam, dproj, l):
    T = proj.shape[0]
    tm = _lru_tile(T)
    hb = tm // LHALO
    nt = T // tm

    def body(rx_ref, rg_ref, hx_ref, hl_ref, hh_ref, dy_ref, lw_ref, lb_ref, wa_ref, ba_ref, wx_ref, bx_ref,
             lam_ref, _, o_ref, dlw_ref, dlb_ref, dwa_ref, dba_ref, dwx_ref, dbx_ref, dlam_ref,
             rxbuf, dbuf, carry, head):
        step = pl.program_id(0)
        i = nt - 1 - step

        @pl.when(step == 0)
        def _():
            carry[...] = jnp.zeros_like(carry)
            head[...] = jnp.zeros_like(head)
            for ref in (dlw_ref, dlb_ref, dwa_ref, dba_ref, dwx_ref, dbx_ref, dlam_ref):
                ref[...] = jnp.zeros_like(ref)

        rxbuf[0:LHALO, :] = _mask_rows(hx_ref[...], i * tm - LHALO)
        rxbuf[LHALO:LHALO + tm, :] = _mask_rows(rx_ref[...], i * tm)
        rc, rcb, r, ig, sp, a, mult, valid = _lru_gates(rxbuf, tm, i * tm, lw_ref, lb_ref, wa_ref, ba_ref,
                                                        wx_ref, bx_ref, lam_ref)
        rows = _row_ids(tm, 0)
        h = hl_ref[...]
        h_before = jnp.where(i > 0, hh_ref[LHALO - 1:LHALO, :], 0.0)
        hprev = jnp.where(rows == 0, h_before, pltpu.roll(h, 1, axis=0))
        rg = rg_ref[...]
        gate, dgate = _silu_and_grad(rg)
        dy = dy_ref[...]
        o_ref[:, LW:2 * LW] = (dy * h * dgate).astype(BF16)
        bb = dy * gate + jnp.where(rows == tm - 1, carry[0:1, :], 0.0)
        aa = jnp.where(rows == tm - 1, 0.0, pltpu.roll(a, tm - 1, axis=0))
        for s in _scan_steps(tm):
            keep = rows < tm - s
            a_s = jnp.where(keep, pltpu.roll(aa, tm - s, axis=0), 1.0)
            b_s = jnp.where(keep, pltpu.roll(bb, tm - s, axis=0), 0.0)
            bb = aa * b_s + bb
            aa = aa * a_s
        g = bb
        dbuf[0:tm, :] = a * g
        carry[0:1, :] = dbuf[0:1, :]
        du = jnp.where(valid, g, 0.0)
        da = g * hprev
        dix = du * mult
        dmult = du * (ig * rc)
        dla = jnp.where(valid, da * a - dmult * (a * a) / mult, 0.0)
        dr = dla * (-LRU_C * sp)
        dlam_ref[...] += _colsum(dla * (LRU_C * r)) * _sigmoid(-lam_ref[...])
        dpa = dr * r * (1.0 - r)
        dpx = (dix * rc) * ig * (1.0 - ig)
        dpab = dpa.astype(BF16)
        dpxb = dpx.astype(BF16)
        dba_ref[...] += _colsum(dpa)
        dbx_ref[...] += _colsum(dpx)
        dwa_ref[...] += _dot_tn(rcb, dpab)
        dwx_ref[...] += _dot_tn(rcb, dpxb)
        drc = dix * ig + _dot_nt(dpab, wa_ref[...]) + _dot_nt(dpxb, wx_ref[...])
        dbuf[0:tm, :] = drc
        dbuf[tm:tm + LHALO, :] = head[...]
        acc = jnp.zeros((tm, LW), F32)
        for k in range(LRU_K):
            o = LRU_K - 1 - k
            acc += lw_ref[k:k + 1, :] * dbuf[o:o + tm, :]
            oc = LHALO - (LRU_K - 1) + k
            dlw_ref[k:k + 1, :] += _colsum(drc * rxbuf[oc:oc + tm, :])
        dlb_ref[...] += _colsum(drc)
        head[...] = dbuf[0:LHALO, :]
        o_ref[:, 0:LW] = jnp.where(valid, acc, 0.0).astype(BF16)

    rev = lambda s: nt - 1 - s
    vec = pl.BlockSpec((None, 1, LW), lambda s: (l, 0, 0))
    mat = pl.BlockSpec((None, LW, LW), lambda s: (l, 0, 0))
    const = lambda s: (0, 0)
    halo = lambda s: jnp.maximum(rev(s) * hb - 1, 0)
    return pl.pallas_call(
        body, name=f"lru_bwd{l}", grid=(nt,),
        in_specs=[pl.BlockSpec((tm, LW), lambda s: (rev(s), 8)),
                  pl.BlockSpec((tm, LW), lambda s: (rev(s), 9)),
                  pl.BlockSpec((LHALO, LW), lambda s: (halo(s), 8)),
                  pl.BlockSpec((tm, LW), lambda s: (rev(s), 0)),
                  pl.BlockSpec((LHALO, LW), lambda s: (halo(s), 0)),
                  pl.BlockSpec((tm, LW), lambda s: (rev(s), 0)),
                  pl.BlockSpec((None, LRU_K, LW), lambda s: (l, 0, 0)),
                  vec, mat, vec, mat, vec, vec, pl.BlockSpec(memory_space=pl.ANY)],
        out_specs=[pl.BlockSpec((tm, 2 * LW), lambda s: (rev(s), 4)),
                   pl.BlockSpec((8, LW), const), pl.BlockSpec((1, LW), const),
                   pl.BlockSpec((LW, LW), const), pl.BlockSpec((1, LW), const),
                   pl.BlockSpec((LW, LW), const), pl.BlockSpec((1, LW), const),
                   pl.BlockSpec((1, LW), const)],
        out_shape=[jax.ShapeDtypeStruct(dproj.shape, BF16),
                   jax.ShapeDtypeStruct((8, LW), F32), jax.ShapeDtypeStruct((1, LW), F32),
                   jax.ShapeDtypeStruct((LW, LW), F32), jax.ShapeDtypeStruct((1, LW), F32),
                   jax.ShapeDtypeStruct((LW, LW), F32), jax.ShapeDtypeStruct((1, LW), F32),
                   jax.ShapeDtypeStruct((1, LW), F32)],
        scratch_shapes=[pltpu.VMEM((tm + LHALO, LW), F32), pltpu.VMEM((tm + LHALO, LW), F32),
                        pltpu.VMEM((8, LW), F32), pltpu.VMEM((LHALO, LW), F32)],
        input_output_aliases={13: 0},
        compiler_params=_cp("arbitrary"),
    )(proj, proj, proj, hl, hl, d_yl, lw, lb, wa, ba, wx, bx, lam, dproj)


def _rope_tables(T):
    pos = (lax.broadcasted_iota(jnp.int32, (T, 128), 0) - PAD).astype(F32)
    lane = lax.broadcasted_iota(jnp.int32, (T, 128), 1) % 64
    inv_freq = ROPE_THETA ** (-(lane % ROT_HALF).astype(F32) / ROT_HALF)
    ang = pos * inv_freq
    cos, sin = jnp.cos(ang), jnp.sin(ang)
    c = jnp.where(lane < 2 * ROT_HALF, cos, 1.0)
    s1 = jnp.where(lane < ROT_HALF, -sin, 0.0)
    s2 = jnp.where((lane >= ROT_HALF) & (lane < 2 * ROT_HALF), sin, 0.0)
    return c, s1, s2


def _rot_fwd(x, c, s1, s2):
    return x * c + pltpu.roll(x, 128 - ROT_HALF, axis=1) * s1 + pltpu.roll(x, ROT_HALF, axis=1) * s2


def _rot_bwd(dy, c, s1, s2):
    return dy * c + pltpu.roll(dy * s1, ROT_HALF, axis=1) + pltpu.roll(dy * s2, 128 - ROT_HALF, axis=1)


def _rope_fwd(proj, tabs, l):
    T = proj.shape[0]

    def body(ql_ref, qh_ref, k_ref, v_ref, c_ref, s1_ref, s2_ref, qr_ref, kr_ref, vb_ref):
        c, s1, s2 = c_ref[...], s1_ref[...], s2_ref[...]
        for gcol in range(AW // 128):
            src = ql_ref if gcol < 4 else qh_ref
            x = src[:, 128 * (gcol % 4):128 * (gcol % 4) + 128]
            qr_ref[:, 128 * gcol:128 * gcol + 128] = (_rot_fwd(x, c, s1, s2) * 0.125).astype(BF16)
        for gcol in range(KVW // 128):
            x = k_ref[:, 128 * gcol:128 * gcol + 128]
            kr_ref[:, 128 * gcol:128 * gcol + 128] = _rot_fwd(x, c, s1, s2).astype(BF16)
        vb_ref[...] = v_ref[...].astype(BF16)

    tab = pl.BlockSpec((BLK, 128), lambda n: (n, 0))
    return pl.pallas_call(
        body, name=f"rope_fwd{l}", grid=(T // BLK,),
        in_specs=[pl.BlockSpec((BLK, 512), lambda n: (n, 3)), pl.BlockSpec((BLK, 512), lambda n: (n, 4)),
                  pl.BlockSpec((BLK, KVW), lambda n: (n, 10)), pl.BlockSpec((BLK, KVW), lambda n: (n, 11)),
                  tab, tab, tab],
        out_specs=[pl.BlockSpec((BLK, AW), lambda n: (n, 0)), pl.BlockSpec((BLK, KVW), lambda n: (n, 0)),
                   pl.BlockSpec((BLK, KVW), lambda n: (n, 0))],
        out_shape=[jax.ShapeDtypeStruct((T, AW), BF16), jax.ShapeDtypeStruct((T, KVW), BF16),
                   jax.ShapeDtypeStruct((T, KVW), BF16)],
        compiler_params=_cp("parallel"),
    )(proj, proj, proj, proj, *tabs)


def _attn_mask(n):
    qi = lax.broadcasted_iota(jnp.int32, (BLK, BLK), 0)
    kj = lax.broadcasted_iota(jnp.int32, (BLK, BLK), 1)
    m0 = (kj >= PAD) & (n >= 1)
    mp = (kj > qi) & (n >= 2)
    mc = (kj <= qi) & ((n >= 1) | (kj >= PAD))
    return jnp.concatenate([m0, mp, mc], axis=1)


def _kv_halves(x0_ref, xp_ref, xc_ref, g):
    pg, off = g // 2, g % 2
    cols = slice(128 * pg, 128 * pg + 128)
    x = jnp.concatenate([x0_ref[:, cols], xp_ref[:, cols], xc_ref[:, cols]], axis=0).astype(F32)
    lane = lax.broadcasted_iota(jnp.int32, (1, 128), 1)
    if off == 0:
        lo = jnp.where(lane < 64, x, 0.0)
        hi = pltpu.roll(lo, 64, axis=1)
    else:
        hi = jnp.where(lane >= 64, x, 0.0)
        lo = pltpu.roll(hi, 64, axis=1)
    return lo.astype(BF16), hi.astype(BF16)


def _attn_fwd(qr, kr, vb, proj, sinks, l):
    T = qr.shape[0]

    def body(sink_ref, q_ref, k0_ref, kp_ref, kc_ref, v0_ref, vp_ref, vc_ref, ag_ref, ya_ref, att_ref, lse_ref):
        n = pl.program_id(0)
        mask = _attn_mask(n)
        lane = lax.broadcasted_iota(jnp.int32, (1, 128), 1)
        lse_acc = jnp.zeros((BLK, 128), F32)
        for g in range(4):
            k_lo, k_hi = _kv_halves(k0_ref, kp_ref, kc_ref, g)
            v_lo, v_hi = _kv_halves(v0_ref, vp_ref, vc_ref, g)
            for pp in range(2):
                cols = slice(128 * (2 * g + pp), 128 * (2 * g + pp) + 128)
                qpair = q_ref[:, cols]
                out = jnp.zeros((BLK, 128), F32)
                for hh, (kx, vx) in enumerate(((k_lo, v_lo), (k_hi, v_hi))):
                    h = 4 * g + 2 * pp + hh
                    sink = sink_ref[l, h]
                    s = jnp.where(mask, _dot_nt(qpair, kx), NEG_INF)
                    m = jnp.maximum(jnp.max(s, axis=1, keepdims=True), sink)
                    p = jnp.exp(s - m)
                    denom = jnp.sum(p, axis=1, keepdims=True) + jnp.exp(sink - m)
                    out += _dot((p / denom).astype(BF16), vx)
                    lse_acc = jnp.where(lane == h, m + jnp.log(denom), lse_acc)
                att_ref[:, cols] = out
                gate, _ = _silu_and_grad(ag_ref[:, cols])
                ya_ref[:, cols] = (out * gate).astype(BF16)
        lse_ref[...] = lse_acc

    prev = lambda n: (jnp.maximum(n - 1, 0), 0)
    cur = lambda n: (n, 0)
    zero = lambda n: (0, 0)
    kv = lambda f: pl.BlockSpec((BLK, KVW), f)
    return pl.pallas_call(
        body, name=f"attn_fwd{l}", grid=(T // BLK,),
        in_specs=[pl.BlockSpec(memory_space=pltpu.SMEM),
                  pl.BlockSpec((BLK, AW), cur), kv(zero), kv(prev), kv(cur), kv(zero), kv(prev), kv(cur),
                  pl.BlockSpec((BLK, AW), lambda n: (n, 3))],
        out_specs=[pl.BlockSpec((BLK, AW), cur), pl.BlockSpec((BLK, AW), cur), pl.BlockSpec((BLK, 128), cur)],
        out_shape=[jax.ShapeDtypeStruct((T, AW), BF16), jax.ShapeDtypeStruct((T, AW), F32),
                   jax.ShapeDtypeStruct((T, 128), F32)],
        compiler_params=_cp("parallel"),
    )(sinks, qr, kr, kr, kr, vb, vb, vb, proj)


def _attn_bwd(qr, kr, vb, proj, att, lse, d_ya, sinks, dproj, l):
    T = qr.shape[0]
    nb = T // BLK

    def body(sink_ref, q_ref, k0_ref, kp_ref, kc_ref, v0_ref, vp_ref, vc_ref, ag_ref, att_ref, lse_ref, dy_ref, _,
             dq_ref, dk_ref, dv_ref, dk0_ref, dv0_ref, dag_ref, dsink_ref, kcarry, vcarry):
        n = pl.program_id(0)

        @pl.when(n == 0)
        def _():
            dk0_ref[...] = jnp.zeros_like(dk0_ref)
            dv0_ref[...] = jnp.zeros_like(dv0_ref)
            dsink_ref[...] = jnp.zeros_like(dsink_ref)
            kcarry[...] = jnp.zeros_like(kcarry)
            vcarry[...] = jnp.zeros_like(vcarry)

        @pl.when(n == nb)
        def _():
            dk_ref[...] = kcarry[...]
            dv_ref[...] = vcarry[...]

        @pl.when(n < nb)
        def _():
            mask = _attn_mask(n)
            lane = lax.broadcasted_iota(jnp.int32, (1, 128), 1)
            lse = lse_ref[...]
            dsink = jnp.zeros((1, 128), F32)
            dk_pg, dv_pg = [], []
            for pg in range(2):
                dk_acc = jnp.zeros((3 * BLK, 128), F32)
                dv_acc = jnp.zeros((3 * BLK, 128), F32)
                for off in range(2):
                    g = 2 * pg + off
                    k_lo, k_hi = _kv_halves(k0_ref, kp_ref, kc_ref, g)
                    v_lo, v_hi = _kv_halves(v0_ref, vp_ref, vc_ref, g)
                    dkg = jnp.zeros((3 * BLK, 128), F32)
                    dvg = jnp.zeros((3 * BLK, 128), F32)
                    for pp in range(2):
                        cols = slice(128 * (2 * g + pp), 128 * (2 * g + pp) + 128)
                        qpair = q_ref[:, cols]
                        gate, dgate = _silu_and_grad(ag_ref[:, cols])
                        dy = dy_ref[:, cols]
                        dag_ref[:, cols] = (dy * att_ref[:, cols] * dgate).astype(BF16)
                        do = (dy * gate).astype(BF16)
                        dq = jnp.zeros((BLK, 128), F32)
                        for hh, (kx, vx) in enumerate(((k_lo, v_lo), (k_hi, v_hi))):
                            h = 4 * g + 2 * pp + hh
                            sink = sink_ref[l, h]
                            lse_h = jnp.sum(jnp.where(lane == h, lse, 0.0), axis=1, keepdims=True)
                            s = _dot_nt(qpair, kx)
                            p = jnp.where(mask, jnp.exp(s - lse_h), 0.0)
                            dp = _dot_nt(do, vx)
                            delta = jnp.sum(p * dp, axis=1, keepdims=True)
                            ds = (p * (dp - delta)).astype(BF16)
                            psink = jnp.exp(sink - lse_h)
                            dsink += jnp.where(lane == h, -jnp.sum(psink * delta), 0.0)
                            dq += _dot(ds, kx)
                            half = (lane < 64) if hh == 0 else (lane >= 64)
                            dkg += jnp.where(half, _dot_tn(ds, qpair), 0.0)
                            dvg += jnp.where(half, _dot_tn(p.astype(BF16), do), 0.0)
                        dq_ref[:, cols] = dq
                    own = (lane < 64) if off == 0 else (lane >= 64)
                    dk_acc += jnp.where(own, dkg + pltpu.roll(dkg, 64, axis=1), 0.0)
                    dv_acc += jnp.where(own, dvg + pltpu.roll(dvg, 64, axis=1), 0.0)
                dk_pg.append(dk_acc)
                dv_pg.append(dv_acc)
            dsink_ref[...] += dsink
            for pg in range(2):
                cols = slice(128 * pg, 128 * pg + 128)
                dk0_ref[:, cols] += dk_pg[pg][0:BLK]
                dv0_ref[:, cols] += dv_pg[pg][0:BLK]
                dk_ref[:, cols] = kcarry[:, cols] + dk_pg[pg][BLK:2 * BLK]
                dv_ref[:, cols] = vcarry[:, cols] + dv_pg[pg][BLK:2 * BLK]
                kcarry[:, cols] = dk_pg[pg][2 * BLK:3 * BLK]
                vcarry[:, cols] = dv_pg[pg][2 * BLK:3 * BLK]

    last = nb - 1
    cur = lambda n: (jnp.minimum(n, last), 0)
    prev = lambda n: (jnp.clip(n - 1, 0, last), 0)
    zero = lambda n: (0, 0)
    kv = lambda f: pl.BlockSpec((BLK, KVW), f)
    wide = lambda f: pl.BlockSpec((BLK, AW), f)
    return pl.pallas_call(
        body, name=f"attn_bwd{l}", grid=(nb + 1,),
        in_specs=[pl.BlockSpec(memory_space=pltpu.SMEM),
                  wide(cur), kv(zero), kv(prev), kv(cur), kv(zero), kv(prev), kv(cur),
                  pl.BlockSpec((BLK, AW), lambda n: (jnp.minimum(n, last), 3)),
                  wide(cur), pl.BlockSpec((BLK, 128), cur), wide(cur), pl.BlockSpec(memory_space=pl.ANY)],
        out_specs=[wide(cur), kv(prev), kv(prev), kv(zero), kv(zero),
                   pl.BlockSpec((BLK, AW), lambda n: (jnp.minimum(n, last), 3)),
                   pl.BlockSpec((1, 128), zero)],
        out_shape=[jax.ShapeDtypeStruct((T, AW), F32), jax.ShapeDtypeStruct((T, KVW), F32),
                   jax.ShapeDtypeStruct((T, KVW), F32), jax.ShapeDtypeStruct((BLK, KVW), F32),
                   jax.ShapeDtypeStruct((BLK, KVW), F32), jax.ShapeDtypeStruct(dproj.shape, BF16),
                   jax.ShapeDtypeStruct((1, 128), F32)],
        scratch_shapes=[pltpu.VMEM((BLK, KVW), F32), pltpu.VMEM((BLK, KVW), F32)],
        input_output_aliases={12: 5},
        compiler_params=_cp("arbitrary"),
    )(sinks, qr, kr, kr, kr, vb, vb, vb, proj, att, lse, d_ya, dproj)


def _rope_bwd(dqr, dk, dv, dk0, dv0, tabs, dproj, l):
    T = dqr.shape[0]

    def body(dq_ref, dk_ref, dv_ref, dk0_ref, dv0_ref, c_ref, s1_ref, s2_ref, _, o_ref):
        n = pl.program_id(0)
        c, s1, s2 = c_ref[...], s1_ref[...], s2_ref[...]
        first = jnp.where(n == 0, 1.0, 0.0)
        for gcol in range(AW // 128):
            cols = slice(128 * gcol, 128 * gcol + 128)
            o_ref[:, cols] = (_rot_bwd(dq_ref[:, cols], c, s1, s2) * 0.125).astype(BF16)
        for gcol in range(KVW // 128):
            cols = slice(128 * gcol, 128 * gcol + 128)
            dkk = dk_ref[:, cols] + first * dk0_ref[:, cols]
            o_ref[:, AW + 128 * gcol:AW + 128 * gcol + 128] = _rot_bwd(dkk, c, s1, s2).astype(BF16)
            dvv = dv_ref[:, cols] + first * dv0_ref[:, cols]
            o_ref[:, AW + KVW + 128 * gcol:AW + KVW + 128 * gcol + 128] = dvv.astype(BF16)

    cur = lambda n: (n, 0)
    zero = lambda n: (0, 0)
    tab = pl.BlockSpec((BLK, 128), cur)
    return pl.pallas_call(
        body, name=f"rope_bwd{l}", grid=(T // BLK,),
        in_specs=[pl.BlockSpec((BLK, AW), cur), pl.BlockSpec((BLK, KVW), cur), pl.BlockSpec((BLK, KVW), cur),
                  pl.BlockSpec((BLK, KVW), zero), pl.BlockSpec((BLK, KVW), zero), tab, tab, tab,
                  pl.BlockSpec(memory_space=pl.ANY)],
        out_specs=pl.BlockSpec((BLK, AW + 2 * KVW), lambda n: (n, 1)),
        out_shape=jax.ShapeDtypeStruct(dproj.shape, BF16),
        input_output_aliases={8: 0},
        compiler_params=_cp("parallel"),
    )(dqr, dk, dv, dk0, dv0, *tabs, dproj)


def _block_diag(w):
    nl, nh, hd, _ = w.shape
    eye = jnp.eye(nh, dtype=w.dtype)
    return jnp.einsum("lhij,hg->lhigj", w, eye).reshape(nl, nh * hd, nh * hd)


def _diag_blocks(m):
    nh, hd = 8, 64
    return jnp.einsum("hihj->hij", m.reshape(nh, hd, nh, hd))


def _device_step(x, target, p):
    vec = lambda a: a.reshape(DEPTH, 1, a.shape[-1])
    ln_in_g, ln_in_b = p["ln_in_g"].reshape(1, D), p["ln_in_b"].reshape(1, D)
    conv_dw_b, conv_ln_g, conv_ln_b, conv_pw_b = map(vec, (p["conv_dw_b"], p["conv_ln_g"], p["conv_ln_b"], p["conv_pw_b"]))
    lru_conv_b, lru_ba, lru_bx, lru_lambda = map(vec, (p["lru_conv_b"], p["lru_ba"], p["lru_bx"], p["lru_lambda"]))
    ln_post_g, ln_post_b = vec(p["ln_post_g"]), vec(p["ln_post_b"])
    wa_bd = _block_diag(p["lru_wa"]).astype(BF16)
    wx_bd = _block_diag(p["lru_wx"]).astype(BF16)
    w_in, w_out, pw_w = p["w_in"], p["w_out"], p["conv_pw_w"]
    sinks = p["attn_sinks"]

    h, hb = _embed_fwd(x, p["meta_tokens"], ln_in_g, ln_in_b)
    T = h.shape[0]
    tabs = _rope_tables(T)
    saved = []
    for l in range(DEPTH):
        proj = _proj_fwd(hb, w_in[l], l)
        yc, conv = _conv_fwd(proj, p["conv_dw_w"], conv_dw_b, conv_ln_g, conv_ln_b, pw_w[l], conv_pw_b, l)
        qr, kr, vb = _rope_fwd(proj, tabs, l)
        ya, att, lse = _attn_fwd(qr, kr, vb, proj, sinks, l)
        yl, hl = _lru_fwd(proj, p["lru_conv_w"], lru_conv_b, wa_bd, lru_ba, wx_bd, lru_bx, lru_lambda, l)
        hn, hnb, xhat, rstd = _out_fwd(yc, ya, yl, w_out[l], h, ln_post_g, ln_post_b, l)
        saved.append((hb, proj, yc, conv, qr, kr, vb, ya, att, lse, yl, hl, xhat, rstd))
        h, hb = hn, hnb

    loss_part, dh = _loss_head(h, target)
    g = {}
    for l in reversed(range(DEPTH)):
        hb_l, proj, yc, conv, qr, kr, vb, ya, att, lse, yl, hl, xhat, rstd = saved[l]
        dz, dzb, g["ln_post_g", l], g["ln_post_b", l] = _post_ln_bwd(dh, xhat, rstd, ln_post_g, l)
        d_yc, d_ya, d_yl = _dcat_bwd(dzb, w_out[l], l)
        g["w_out", l] = _dwout_bwd(yc, ya, yl, dzb, l)
        d_conv, dproj, dpw, g["conv_pw_b", l], g["conv_ln_g", l], g["conv_ln_b", l] = _conv_bwd_rows(
            conv, proj, d_yc, conv_ln_g, conv_ln_b, pw_w[l], conv_pw_b, l)
        g["conv_pw_w", l] = dpw.reshape(N_SHARD, 2, PW_SH // 2, CW)
        dproj, ddw, g["conv_dw_b", l] = _conv_bwd_taps(d_conv, proj, p["conv_dw_w"], dproj, l)
        g["conv_dw_w", l] = ddw[:CONV_K]
        dqr, dk, dv, dk0, dv0, dproj, dsink = _attn_bwd(qr, kr, vb, proj, att, lse, d_ya, sinks, dproj, l)
        g["attn_sinks", l] = dsink[0, :N_HEADS]
        dproj = _rope_bwd(dqr, dk, dv, dk0, dv0, tabs, dproj, l)
        dproj, dlw, g["lru_conv_b", l], dwa, g["lru_ba", l], dwx, g["lru_bx", l], g["lru_lambda", l] = _lru_bwd(
            proj, hl, d_yl, p["lru_conv_w"], lru_conv_b, wa_bd, lru_ba, wx_bd, lru_bx, lru_lambda, dproj, l)
        g["lru_conv_w", l] = dlw[:LRU_K]
        g["lru_wa", l] = _diag_blocks(dwa)
        g["lru_wx", l] = _diag_blocks(dwx)
        g["w_in", l] = _dwin_bwd(hb_l, dproj, l)
        dh = _dh_bwd(dproj, w_in[l], dz, l)
    grad_x, g["meta_tokens", -1], g["ln_in_g", -1], g["ln_in_b", -1] = _embed_bwd(
        dh, x, p["meta_tokens"], ln_in_g, ln_in_b)
    return loss_part, grad_x, g


MESH = pl.DeviceIdType.MESH
HBM_SPEC = pl.BlockSpec(memory_space=pltpu.HBM)
N_DEV = 8


def _position():
    x, y, c = lax.axis_index("x"), lax.axis_index("y"), lax.axis_index("c")
    return x, y, c


def _other_chips(x, y):
    return [(1 - x, y), (x, 1 - y), (1 - x, 1 - y)]


def _cast_into_slot(a, l, j, tag):
    _, R, C = a.shape
    tb = _pick(R, (512, 128))

    def body(s_ref, a_ref, o_ref):
        o_ref[...] = a_ref[...].astype(BF16)

    grid_spec = pltpu.PrefetchScalarGridSpec(
        num_scalar_prefetch=1, grid=(R // tb,),
        in_specs=[pl.BlockSpec((None, tb, C), lambda t, sc: (l, t, 0))],
        out_specs=pl.BlockSpec((None, tb, C), lambda t, sc: (sc[0], t, 0)))
    return pl.pallas_call(
        body, name=f"cast_into_slot_{tag}{l}", grid_spec=grid_spec,
        out_shape=jax.ShapeDtypeStruct((N_SHARD, R, C), BF16),
        compiler_params=_cp("arbitrary"),
    )(jnp.reshape(j, (1,)).astype(jnp.int32), a)


def _gather_layer(slots, l):
    n = len(slots)

    def body(*refs):
        buf = refs[n:2 * n]
        ici_send, ici_recv, d2d_send, d2d_recv = refs[2 * n:]
        x, y, c = _position()
        mine = 2 * x + y
        chips = _other_chips(x, y)

        def half(k, slot, which):
            hr = buf[k].shape[1] // 2
            return buf[k].at[slot, pl.ds(pl.multiple_of(which * hr, hr), hr)]

        def over_ici(k, p, slot):
            px, py = chips[p]
            return pltpu.make_async_remote_copy(
                src_ref=half(k, slot, c), dst_ref=half(k, slot, c),
                send_sem=ici_send.at[k * 3 + p], recv_sem=ici_recv.at[k * 3 + p],
                device_id=(px, py, c), device_id_type=MESH)

        def over_d2d(k, p, which):
            px, py = chips[p]
            return pltpu.make_async_remote_copy(
                src_ref=half(k, 2 * px + py, which), dst_ref=half(k, 2 * px + py, which),
                send_sem=d2d_send.at[k * 3 + p], recv_sem=d2d_recv.at[k * 3 + p],
                device_id=(x, y, 1 - c), device_id_type=MESH)

        for k in range(n):
            for p in range(3):
                over_ici(k, p, mine).start()
        for k in range(n):
            for p in range(3):
                px, py = chips[p]
                over_ici(k, p, 2 * px + py).wait_recv()
                over_d2d(k, p, c).start()
        for k in range(n):
            for p in range(3):
                over_d2d(k, p, 1 - c).wait_recv()
        for k in range(n):
            for p in range(3):
                over_ici(k, p, mine).wait_send()
                over_d2d(k, p, c).wait_send()

    return pl.pallas_call(
        body, name=f"gather_layer{l}",
        in_specs=[HBM_SPEC] * n, out_specs=[HBM_SPEC] * n,
        out_shape=[jax.ShapeDtypeStruct(s.shape, s.dtype) for s in slots],
        input_output_aliases={k: k for k in range(n)},
        scratch_shapes=[pltpu.SemaphoreType.DMA((3 * n,)), pltpu.SemaphoreType.DMA((3 * n,)),
                        pltpu.SemaphoreType.DMA((3 * n,)), pltpu.SemaphoreType.DMA((3 * n,))],
    )(*slots)


def _gather_shards(shards):
    n = len(shards)

    def body(*refs):
        src, dst = refs[:n], refs[n:2 * n]
        send_sems, recv_sems, local_sems = refs[2 * n:]
        x, y, c = _position()
        mine = 2 * x + y
        chips = _other_chips(x, y)

        def copy(k, p):
            return pltpu.make_async_remote_copy(
                src_ref=src[k], dst_ref=dst[k].at[mine],
                send_sem=send_sems.at[k * 3 + p], recv_sem=recv_sems.at[k * 3 + p],
                device_id=(*chips[p], c), device_id_type=MESH)

        def arrival(k, p):
            px, py = chips[p]
            return pltpu.make_async_remote_copy(
                src_ref=src[k], dst_ref=dst[k].at[2 * px + py],
                send_sem=send_sems.at[k * 3 + p], recv_sem=recv_sems.at[k * 3 + p],
                device_id=(px, py, c), device_id_type=MESH)

        local = [pltpu.make_async_copy(src[k], dst[k].at[mine], local_sems.at[k]) for k in range(n)]
        for cp in local:
            cp.start()
        for k in range(n):
            for p in range(3):
                copy(k, p).start()
        for k in range(n):
            for p in range(3):
                arrival(k, p).wait_recv()
        for k in range(n):
            for p in range(3):
                copy(k, p).wait_send()
        for cp in local:
            cp.wait()

    return pl.pallas_call(
        body, name="gather_shards",
        in_specs=[HBM_SPEC] * n, out_specs=[HBM_SPEC] * n,
        out_shape=[jax.ShapeDtypeStruct((N_SHARD,) + s.shape, s.dtype) for s in shards],
        scratch_shapes=[pltpu.SemaphoreType.DMA((3 * n,)), pltpu.SemaphoreType.DMA((3 * n,)),
                        pltpu.SemaphoreType.DMA((n,))],
    )(*shards)


def _swap_halves(grads, l):
    n = len(grads)

    def body(*refs):
        src, dst = refs[:n], refs[n:2 * n]
        send_sems, recv_sems = refs[2 * n:]
        x, y, c = _position()
        copies = [pltpu.make_async_remote_copy(
            src_ref=src[k].at[:, 1 - c], dst_ref=dst[k],
            send_sem=send_sems.at[k], recv_sem=recv_sems.at[k],
            device_id=(x, y, 1 - c), device_id_type=MESH) for k in range(n)]
        for cp in copies:
            cp.start()
        for cp in copies:
            cp.wait()

    return pl.pallas_call(
        body, name=f"swap_halves{l}",
        in_specs=[HBM_SPEC] * n, out_specs=[HBM_SPEC] * n,
        out_shape=[jax.ShapeDtypeStruct((N_SHARD,) + g.shape[2:], F32) for g in grads],
        scratch_shapes=[pltpu.SemaphoreType.DMA((n,)), pltpu.SemaphoreType.DMA((n,))],
    )(*grads)


def _chip_partial(a, y, c, j, tag):
    _, _, R, C = a.shape
    tr = _pick(R, (256, 64))

    def body(s_ref, a_ref, y_ref, pb_ref, po_ref):
        total = a_ref[...] + y_ref[...]
        pb_ref[...] = total.astype(BF16)

        @pl.when(pl.program_id(1) == s_ref[1])
        def _():
            po_ref[...] = total

    grid_spec = pltpu.PrefetchScalarGridSpec(
        num_scalar_prefetch=1, grid=(R // tr, N_SHARD),
        in_specs=[pl.BlockSpec((None, None, tr, C), lambda t, s, sc: (s, sc[0], t, 0)),
                  pl.BlockSpec((None, tr, C), lambda t, s, sc: (s, t, 0))],
        out_specs=[pl.BlockSpec((None, tr, C), lambda t, s, sc: (s, t, 0)),
                   pl.BlockSpec((tr, C), lambda t, s, sc: (t, 0))])
    return pl.pallas_call(
        body, name=f"chip_partial_{tag}", grid_spec=grid_spec,
        out_shape=[jax.ShapeDtypeStruct((N_SHARD, R, C), BF16), jax.ShapeDtypeStruct((R, C), F32)],
        compiler_params=_cp("arbitrary", "arbitrary"),
    )(jnp.stack([c, j]).astype(jnp.int32), a, y)


def _scatter_partials(parts, l):
    n = len(parts)

    def body(*refs):
        src, dst = refs[:n], refs[n:2 * n]
        send_sems, recv_sems = refs[2 * n:]
        x, y, c = _position()
        mine = 2 * x + y
        chips = _other_chips(x, y)

        def copy(k, p):
            px, py = chips[p]
            return pltpu.make_async_remote_copy(
                src_ref=src[k].at[2 * px + py], dst_ref=dst[k].at[mine],
                send_sem=send_sems.at[k * 3 + p], recv_sem=recv_sems.at[k * 3 + p],
                device_id=(px, py, c), device_id_type=MESH)

        def arrival(k, p):
            px, py = chips[p]
            return pltpu.make_async_remote_copy(
                src_ref=src[k].at[mine], dst_ref=dst[k].at[2 * px + py],
                send_sem=send_sems.at[k * 3 + p], recv_sem=recv_sems.at[k * 3 + p],
                device_id=(px, py, c), device_id_type=MESH)

        for k in range(n):
            for p in range(3):
                copy(k, p).start()
        for k in range(n):
            for p in range(3):
                arrival(k, p).wait_recv()
        for k in range(n):
            for p in range(3):
                copy(k, p).wait_send()

    return pl.pallas_call(
        body, name=f"scatter_partials{l}",
        in_specs=[HBM_SPEC] * n, out_specs=[HBM_SPEC] * n,
        out_shape=[jax.ShapeDtypeStruct(pb.shape, BF16) for pb in parts],
        scratch_shapes=[pltpu.SemaphoreType.DMA((3 * n,)), pltpu.SemaphoreType.DMA((3 * n,))],
    )(*parts)


def _shard_total(own, z, others_c, tag):
    R, C = own.shape
    tr = _pick(R, (256, 64))

    def body(s_ref, o_ref, z0_ref, z1_ref, z2_ref, h_ref):
        h_ref[...] = ((o_ref[...] + z0_ref[...].astype(F32)) + z1_ref[...].astype(F32)) + z2_ref[...].astype(F32)

    zspec = lambda q: pl.BlockSpec((None, tr, C), lambda t, sc: (sc[q], t, 0))
    grid_spec = pltpu.PrefetchScalarGridSpec(
        num_scalar_prefetch=1, grid=(R // tr,),
        in_specs=[pl.BlockSpec((tr, C), lambda t, sc: (t, 0)), zspec(0), zspec(1), zspec(2)],
        out_specs=pl.BlockSpec((None, tr, C), lambda t, sc: (sc[3], t, 0)))
    return pl.pallas_call(
        body, name=f"shard_total_{tag}", grid_spec=grid_spec,
        out_shape=jax.ShapeDtypeStruct((2, R, C), F32),
        compiler_params=_cp("arbitrary"),
    )(others_c, own, z, z, z)


def _share_halves(totals, l):
    n = len(totals)

    def body(*refs):
        buf = refs[n:2 * n]
        send_sems, recv_sems = refs[2 * n:]
        x, y, c = _position()

        def copy(k, which):
            return pltpu.make_async_remote_copy(
                src_ref=buf[k].at[which], dst_ref=buf[k].at[which],
                send_sem=send_sems.at[k], recv_sem=recv_sems.at[k],
                device_id=(x, y, 1 - c), device_id_type=MESH)

        for k in range(n):
            copy(k, c).start()
        for k in range(n):
            copy(k, 1 - c).wait_recv()
        for k in range(n):
            copy(k, c).wait_send()

    return pl.pallas_call(
        body, name=f"share_halves{l}",
        in_specs=[HBM_SPEC] * n, out_specs=[HBM_SPEC] * n,
        out_shape=[jax.ShapeDtypeStruct(t.shape, F32) for t in totals],
        input_output_aliases={k: k for k in range(n)},
        scratch_shapes=[pltpu.SemaphoreType.DMA((n,)), pltpu.SemaphoreType.DMA((n,))],
    )(*totals)


def _allreduce_pack(pack):
    ns = pack.shape[0]

    def body(p_ref, o_ref, slots, send_sems, recv_sems):
        x, y, c = _position()
        me = 4 * x + 2 * y + c

        def copy(m):
            peer = (x ^ (m >> 2), y ^ ((m >> 1) & 1), c ^ (m & 1))
            return pltpu.make_async_remote_copy(
                src_ref=p_ref, dst_ref=slots.at[me], send_sem=send_sems.at[m - 1], recv_sem=recv_sems.at[m - 1],
                device_id=peer, device_id_type=MESH)

        def arrival(m):
            peer = (x ^ (m >> 2), y ^ ((m >> 1) & 1), c ^ (m & 1))
            return pltpu.make_async_remote_copy(
                src_ref=p_ref, dst_ref=slots.at[4 * peer[0] + 2 * peer[1] + peer[2]],
                send_sem=send_sems.at[m - 1], recv_sem=recv_sems.at[m - 1],
                device_id=peer, device_id_type=MESH)

        for m in range(1, N_DEV):
            copy(m).start()
        slots[me] = p_ref[...]
        for m in range(1, N_DEV):
            arrival(m).wait_recv()
        acc = slots[0]
        for d in range(1, N_DEV):
            acc = acc + slots[d]
        o_ref[...] = acc
        for m in range(1, N_DEV):
            copy(m).wait_send()

    vm = pl.BlockSpec(memory_space=pltpu.VMEM)
    return pl.pallas_call(
        body, name="allreduce_pack",
        in_specs=[vm], out_specs=vm,
        out_shape=jax.ShapeDtypeStruct(pack.shape, F32),
        scratch_shapes=[pltpu.VMEM((N_DEV, ns, 128), F32),
                        pltpu.SemaphoreType.DMA((N_DEV - 1,)), pltpu.SemaphoreType.DMA((N_DEV - 1,))],
        compiler_params=pltpu.CompilerParams(vmem_limit_bytes=V7X_VMEM_LIMIT),
    )(pack)


def _adamw_math(w, g, m, v):
    m = ADAM_B1 * m + (1.0 - ADAM_B1) * g
    v = ADAM_B2 * v + (1.0 - ADAM_B2) * (g * g)
    m_hat = m / (1.0 - ADAM_B1 ** ADAM_STEP)
    v_hat = v / (1.0 - ADAM_B2 ** ADAM_STEP)
    delta = -ADAM_LR * (m_hat / (jnp.sqrt(v_hat) + ADAM_EPS) + ADAM_WD * w)
    return delta, m, v


def _adamw_big(w, g0, g1, m, v, tag):
    _, R, C = w.shape
    tr = _pick(R, (256, 128))

    def body(w_ref, g0_ref, g1_ref, m_ref, v_ref, go_ref, d_ref, mo_ref, vo_ref):
        g = jnp.where(pl.program_id(0) == 0, g0_ref[...], g1_ref[...])
        delta, mn, vn = _adamw_math(w_ref[...], g, m_ref[...], v_ref[...])
        go_ref[...] = g
        d_ref[...] = delta
        mo_ref[...] = mn
        vo_ref[...] = vn

    s3 = pl.BlockSpec((None, tr, C), lambda l, t: (l, t, 0))
    s2 = pl.BlockSpec((tr, C), lambda l, t: (t, 0))
    shp = jax.ShapeDtypeStruct(w.shape, F32)
    return pl.pallas_call(
        body, name=f"adamw_{tag}", grid=(2, R // tr),
        in_specs=[s3, s2, s2, s3, s3], out_specs=[s3, s3, s3, s3],
        out_shape=[shp, shp, shp, shp],
        compiler_params=_cp("parallel", "parallel"),
    )(w, g0, g1, m, v)


def _adamw_small(ws, gs, ms, vs):
    n = len(ws)

    def body(*refs):
        w_r, g_r, m_r, v_r = refs[:n], refs[n:2 * n], refs[2 * n:3 * n], refs[3 * n:4 * n]
        d_o, m_o, v_o = refs[4 * n:5 * n], refs[5 * n:6 * n], refs[6 * n:7 * n]
        for k in range(n):
            delta, mn, vn = _adamw_math(w_r[k][...], g_r[k][...], m_r[k][...], v_r[k][...])
            d_o[k][...] = delta
            m_o[k][...] = mn
            v_o[k][...] = vn

    vm = pl.BlockSpec(memory_space=pltpu.VMEM)
    shapes = [jax.ShapeDtypeStruct(w.shape, F32) for w in ws]
    outs = pl.pallas_call(
        body, name="adamw_small",
        in_specs=[vm] * (4 * n), out_specs=[vm] * (3 * n),
        out_shape=shapes * 3,
    )(*ws, *gs, *ms, *vs)
    return outs[:n], outs[n:2 * n], outs[2 * n:]


_WEIGHTS = ["meta_tokens", "ln_in_g", "ln_in_b", "w_in", "conv_dw_w", "conv_dw_b", "conv_ln_g", "conv_ln_b",
            "conv_pw_w", "conv_pw_b", "attn_sinks", "lru_conv_w", "lru_conv_b", "lru_wa", "lru_ba", "lru_wx",
            "lru_bx", "lru_lambda", "w_out", "ln_post_g", "ln_post_b"]
_BIG = ("w_in", "w_out", "conv_pw_w")
_SMALL_SHARDED = {"meta_tokens": 1, "conv_dw_w": 2, "lru_conv_w": 2}
PACK_ROWS_ALIGN = 8


def _as2d(a):
    return a.reshape(1, -1) if a.ndim == 1 else a.reshape(-1, a.shape[-1])


def kernel(x, meta_tokens, ln_in_g, ln_in_b, w_in, conv_dw_w, conv_dw_b, conv_ln_g, conv_ln_b, conv_pw_w, conv_pw_b, attn_sinks, lru_conv_w, lru_conv_b, lru_wa, lru_ba, lru_wx, lru_bx, lru_lambda, w_out, ln_post_g, ln_post_b, loss_target, m_meta_tokens, m_ln_in_g, m_ln_in_b, m_w_in, m_conv_dw_w, m_conv_dw_b, m_conv_ln_g, m_conv_ln_b, m_conv_pw_w, m_conv_pw_b, m_attn_sinks, m_lru_conv_w, m_lru_conv_b, m_lru_wa, m_lru_ba, m_lru_wx, m_lru_bx, m_lru_lambda, m_w_out, m_ln_post_g, m_ln_post_b, v_meta_tokens, v_ln_in_g, v_ln_in_b, v_w_in, v_conv_dw_w, v_conv_dw_b, v_conv_ln_g, v_conv_ln_b, v_conv_pw_w, v_conv_pw_b, v_attn_sinks, v_lru_conv_w, v_lru_conv_b, v_lru_wa, v_lru_ba, v_lru_wx, v_lru_bx, v_lru_lambda, v_w_out, v_ln_post_g, v_ln_post_b):
    w = dict(meta_tokens=meta_tokens, ln_in_g=ln_in_g, ln_in_b=ln_in_b, w_in=w_in, conv_dw_w=conv_dw_w,
             conv_dw_b=conv_dw_b, conv_ln_g=conv_ln_g, conv_ln_b=conv_ln_b, conv_pw_w=conv_pw_w,
             conv_pw_b=conv_pw_b, attn_sinks=attn_sinks, lru_conv_w=lru_conv_w, lru_conv_b=lru_conv_b,
             lru_wa=lru_wa, lru_ba=lru_ba, lru_wx=lru_wx, lru_bx=lru_bx, lru_lambda=lru_lambda, w_out=w_out,
             ln_post_g=ln_post_g, ln_post_b=ln_post_b)
    mom_m = dict(zip(_WEIGHTS, (m_meta_tokens, m_ln_in_g, m_ln_in_b, m_w_in, m_conv_dw_w, m_conv_dw_b, m_conv_ln_g,
                                m_conv_ln_b, m_conv_pw_w, m_conv_pw_b, m_attn_sinks, m_lru_conv_w, m_lru_conv_b,
                                m_lru_wa, m_lru_ba, m_lru_wx, m_lru_bx, m_lru_lambda, m_w_out, m_ln_post_g,
                                m_ln_post_b)))
    mom_v = dict(zip(_WEIGHTS, (v_meta_tokens, v_ln_in_g, v_ln_in_b, v_w_in, v_conv_dw_w, v_conv_dw_b, v_conv_ln_g,
                                v_conv_ln_b, v_conv_pw_w, v_conv_pw_b, v_attn_sinks, v_lru_conv_w, v_lru_conv_b,
                                v_lru_wa, v_lru_ba, v_lru_wx, v_lru_bx, v_lru_lambda, v_w_out, v_ln_post_g,
                                v_ln_post_b)))
    xi, yi, ci = _position()
    j = 2 * xi + yi

    g_meta, g_dw, g_lc = _gather_shards([meta_tokens, conv_dw_w, lru_conv_w])
    p = dict(w)
    p["w_in"], p["w_out"], p["conv_pw_w"] = [], [], []
    for l in range(DEPTH):
        g_win, g_wout, g_pw = _gather_layer(
            [_cast_into_slot(w_in, l, j, "w_in"), _cast_into_slot(w_out, l, j, "w_out"),
             _cast_into_slot(conv_pw_w, l, j, "conv_pw_w")], l)
        p["w_in"].append(g_win)
        p["w_out"].append(g_wout)
        p["conv_pw_w"].append(g_pw.reshape(CW, CW))
    p["meta_tokens"] = g_meta.transpose(1, 0, 2).reshape(N_META, D)
    p["conv_dw_w"] = g_dw.transpose(1, 2, 0, 3).reshape(DEPTH, CONV_K, CW)
    p["lru_conv_w"] = g_lc.transpose(1, 2, 0, 3).reshape(DEPTH, LRU_K, LW)

    loss_part, grad_x, g = _device_step(x[0], loss_target[0], p)
    loss = lax.psum(jnp.sum(loss_part), ("x", "y", "c"))

    others = jnp.stack([jnp.where(j <= 0, 1, 0), jnp.where(j <= 1, 2, 1), jnp.where(j <= 2, 3, 2), ci]).astype(jnp.int32)
    big = {}
    for l in range(DEPTH):
        grads = [g[name, l] for name in _BIG]
        recv = _swap_halves(grads, l)
        parts, owns = [], []
        for name, a, r in zip(_BIG, grads, recv):
            pb, po = _chip_partial(a, r, ci, j, f"{name}{l}")
            parts.append(pb)
            owns.append(po)
        z = _scatter_partials(parts, l)
        halves = [_shard_total(po, zz, others, f"{name}{l}") for name, po, zz in zip(_BIG, owns, z)]
        full = _share_halves(halves, l)
        for name, f in zip(_BIG, full):
            big[name, l] = f.reshape(2 * f.shape[1], f.shape[2])

    small_names = [n for n in _WEIGHTS if n not in _BIG]

    def full_grad(name):
        if (name, -1) in g:
            return g[name, -1]
        return jnp.stack([g[name, l] for l in range(DEPTH)], axis=0)

    flats = [full_grad(n).reshape(-1) for n in small_names]
    sizes = [f.shape[0] for f in flats]
    total = sum(sizes)
    rows = -(-total // 128)
    rows = -(-rows // PACK_ROWS_ALIGN) * PACK_ROWS_ALIGN
    pack = jnp.concatenate(flats + [jnp.zeros((rows * 128 - total,), F32)]).reshape(rows, 128)
    red = _allreduce_pack(pack).reshape(-1)
    small_g = {}
    off = 0
    for n, sz in zip(small_names, sizes):
        full = red[off:off + sz]
        off += sz
        if n in _SMALL_SHARDED:
            ax = _SMALL_SHARDED[n]
            fshape = list(w[n].shape)
            fshape[ax] *= N_SHARD
            full = full.reshape(fshape)
            small_g[n] = lax.dynamic_slice_in_dim(full, j * w[n].shape[ax], w[n].shape[ax], axis=ax)
        else:
            small_g[n] = full.reshape(w[n].shape)

    out_g, out_d, out_m, out_v = {}, {}, {}, {}
    for name in _BIG:
        shp = w[name].shape
        to3 = lambda a: a.reshape(DEPTH, -1, shp[-1])
        go, do, mo, vo = _adamw_big(to3(w[name]), big[name, 0], big[name, 1], to3(mom_m[name]), to3(mom_v[name]), name)
        out_g[name], out_d[name], out_m[name], out_v[name] = (a.reshape(shp) for a in (go, do, mo, vo))
    ds, ms, vs = _adamw_small([_as2d(w[n]) for n in small_names], [_as2d(small_g[n]) for n in small_names],
                              [_as2d(mom_m[n]) for n in small_names], [_as2d(mom_v[n]) for n in small_names])
    for n, d_, m_, v_ in zip(small_names, ds, ms, vs):
        out_g[n] = small_g[n]
        out_d[n], out_m[n], out_v[n] = d_.reshape(w[n].shape), m_.reshape(w[n].shape), v_.reshape(w[n].shape)

    return (loss, grad_x[None], *[out_g[n] for n in _WEIGHTS], *[out_d[n] for n in _WEIGHTS],
            *[out_m[n] for n in _WEIGHTS], *[out_v[n] for n in _WEIGHTS])
```

```python
import functools

import jax
import jax.numpy as jnp
from jax import lax
from jax.experimental import pallas as pl
from jax.experimental.pallas import tpu as pltpu

F32 = jnp.float32
BF16 = jnp.bfloat16

D = 2048
N_META = 16
CW = 512
CONV_K = 31
AW = 1024
KVW = 256
N_HEADS = 16
LW = 512
LRU_K = 4
LRU_C = 8.0
IN_TOTAL = 5120
ROT_HALF = 8
ROPE_THETA = 500000.0
LN_EPS = 1e-5
DEPTH = 2
ALPHA = (2.0 * DEPTH) ** 0.25
NEG_INF = -1e30
ADAM_LR, ADAM_B1, ADAM_B2, ADAM_EPS, ADAM_WD, ADAM_STEP = 0.001, 0.9, 0.999, 1e-08, 0.01, 10

BLK = 128
PAD = BLK - N_META
N_SHARD = 4
WIN_SH = IN_TOTAL // N_SHARD
WOUT_SH = D // N_SHARD
PW_SH = CW // N_SHARD
HALO = 32
LHALO = 8
V7X_VMEM_LIMIT = 60 * 1024 * 1024


def _cp(*sem):
    return pltpu.CompilerParams(dimension_semantics=sem if sem else None, vmem_limit_bytes=V7X_VMEM_LIMIT)


def _pick(total, prefs):
    for p in prefs:
        if total % p == 0:
            return p
    raise ValueError(f"no tile for {total}")


def _dot(a, b):
    return jnp.dot(a, b, preferred_element_type=F32)


def _dot_nt(a, b):
    return lax.dot_general(a, b, (((1,), (1,)), ((), ())), preferred_element_type=F32)


def _dot_tn(a, b):
    return lax.dot_general(a, b, (((0,), (0,)), ((), ())), preferred_element_type=F32)


def _sigmoid(x):
    return 1.0 / (1.0 + jnp.exp(-x))


def _silu_and_grad(x):
    s = _sigmoid(x)
    return x * s, s * (1.0 + x * (1.0 - s))


def _ln_rows(x, g, b):
    mu = jnp.mean(x, axis=-1, keepdims=True)
    xc = x - mu
    var = jnp.mean(xc * xc, axis=-1, keepdims=True)
    rstd = lax.rsqrt(var + LN_EPS)
    xhat = xc * rstd
    return xhat * g + b, xhat, rstd


def _ln_bwd_rows(dy, xhat, rstd, g):
    dxh = dy * g
    m1 = jnp.mean(dxh, axis=-1, keepdims=True)
    m2 = jnp.mean(dxh * xhat, axis=-1, keepdims=True)
    return rstd * (dxh - m1 - xhat * m2)


def _row_ids(n, base):
    return base + lax.broadcasted_iota(jnp.int32, (n, 1), 0)


def _colsum(x):
    return jnp.sum(x, axis=0, keepdims=True)


def _embed_fwd(x, meta, g, b, job=None):
    S = x.shape[0]
    nb = S // BLK + 1

    def body(x_ref, meta_ref, g_ref, b_ref, h_ref, hb_ref):
        n = pl.program_id(0)

        @pl.when(n == 0)
        def _():
            y, _, _ = _ln_rows(meta_ref[...], g_ref[...], b_ref[...])
            h_ref[...] = jnp.zeros_like(h_ref)
            h_ref[PAD:BLK, :] = y

        @pl.when(n > 0)
        def _():
            y, _, _ = _ln_rows(x_ref[...], g_ref[...], b_ref[...])
            h_ref[...] = y

        hb_ref[...] = h_ref[...].astype(BF16)

    return _side_call(
        body, job, name="embed_fwd", grid=(nb,),
        in_specs=[pl.BlockSpec((BLK, D), lambda n: (jnp.maximum(n - 1, 0), 0)),
                  pl.BlockSpec((N_META, D), lambda n: (0, 0)),
                  pl.BlockSpec((1, D), lambda n: (0, 0)),
                  pl.BlockSpec((1, D), lambda n: (0, 0))],
        out_specs=[pl.BlockSpec((BLK, D), lambda n: (n, 0)),
                   pl.BlockSpec((BLK, D), lambda n: (n, 0))],
        out_shape=[jax.ShapeDtypeStruct((nb * BLK, D), F32), jax.ShapeDtypeStruct((nb * BLK, D), BF16)],
        scratch_shapes=[], semantics=("arbitrary",), args=[x, meta, g, b])


def _embed_bwd(dh, x, meta, g, b):
    S = x.shape[0]
    nb = S // BLK + 1

    def body(dh_ref, x_ref, meta_ref, g_ref, b_ref, gx_ref, gm_ref, dg_ref, db_ref):
        n = pl.program_id(0)

        @pl.when(n == 0)
        def _():
            _, xhat, rstd = _ln_rows(meta_ref[...], g_ref[...], b_ref[...])
            dy = dh_ref[PAD:BLK, :]
            gm_ref[...] = _ln_bwd_rows(dy, xhat, rstd, g_ref[...])
            dg_ref[...] = _colsum(dy * xhat)
            db_ref[...] = _colsum(dy)

        @pl.when(n > 0)
        def _():
            _, xhat, rstd = _ln_rows(x_ref[...], g_ref[...], b_ref[...])
            dy = dh_ref[...]
            gx_ref[...] = _ln_bwd_rows(dy, xhat, rstd, g_ref[...])
            dg_ref[...] += _colsum(dy * xhat)
            db_ref[...] += _colsum(dy)

    prev = lambda n: (jnp.maximum(n - 1, 0), 0)
    const = lambda n: (0, 0)
    return pl.pallas_call(
        body, name="embed_bwd", grid=(nb,),
        in_specs=[pl.BlockSpec((BLK, D), lambda n: (n, 0)),
                  pl.BlockSpec((BLK, D), prev),
                  pl.BlockSpec((N_META, D), const),
                  pl.BlockSpec((1, D), const),
                  pl.BlockSpec((1, D), const)],
        out_specs=[pl.BlockSpec((BLK, D), prev),
                   pl.BlockSpec((N_META, D), const),
                   pl.BlockSpec((1, D), const),
                   pl.BlockSpec((1, D), const)],
        out_shape=[jax.ShapeDtypeStruct((S, D), F32), jax.ShapeDtypeStruct((N_META, D), F32),
                   jax.ShapeDtypeStruct((1, D), F32), jax.ShapeDtypeStruct((1, D), F32)],
        compiler_params=_cp("arbitrary"),
    )(dh, x, meta, g, b)


def _loss_head(h, target):
    T = h.shape[0]
    nb = T // BLK

    def body(h_ref, t_ref, part_ref, dy_ref):
        n = pl.program_id(0)

        @pl.when(n == 0)
        def _():
            part_ref[...] = jnp.zeros_like(part_ref)
            dy_ref[...] = jnp.zeros_like(dy_ref)

        @pl.when(n > 0)
        def _():
            err = h_ref[...] - t_ref[...]
            part_ref[...] += _colsum(err * err) * (0.5 / D)
            dy_ref[...] = err * (1.0 / D)

    return pl.pallas_call(
        body, name="loss_head", grid=(nb,),
        in_specs=[pl.BlockSpec((BLK, D), lambda n: (n, 0)),
                  pl.BlockSpec((BLK, D), lambda n: (jnp.maximum(n - 1, 0), 0))],
        out_specs=[pl.BlockSpec((1, D), lambda n: (0, 0)),
                   pl.BlockSpec((BLK, D), lambda n: (n, 0))],
        out_shape=[jax.ShapeDtypeStruct((1, D), F32), jax.ShapeDtypeStruct((T, D), F32)],
        compiler_params=_cp("arbitrary"),
    )(h, target)


def _proj_fwd(hb, w_in, l, job=None):
    T = hb.shape[0]
    tm = _pick(T, (1056, 384, 128))

    def body(a_ref, w_ref, o_ref):
        o_ref[...] = _dot(a_ref[...], w_ref[...])

    return _side_call(
        body, job, name=f"proj_fwd{l}", grid=(T // tm, N_SHARD),
        in_specs=[pl.BlockSpec((tm, D), lambda i, j: (i, 0)),
                  pl.BlockSpec((None, D, WIN_SH), lambda i, j: (j, 0, 0))],
        out_specs=[pl.BlockSpec((tm, WIN_SH), lambda i, j: (i, j))],
        out_shape=[jax.ShapeDtypeStruct((T, IN_TOTAL), F32)],
        scratch_shapes=[], semantics=("parallel", "arbitrary"), args=[hb, w_in])


def _out_fwd(yc, ya, yl, w_out, h, g, b, l):
    T = h.shape[0]
    tm = _pick(T, (384, 128))

    def body(yc_ref, ya_ref, yl_ref, w_ref, h_ref, g_ref, b_ref, hn_ref, hnb_ref, xh_ref, rs_ref):
        acc = _dot(yc_ref[...], w_ref[0])
        acc += _dot(ya_ref[:, 0:WOUT_SH], w_ref[1])
        acc += _dot(ya_ref[:, WOUT_SH:2 * WOUT_SH], w_ref[2])
        acc += _dot(yl_ref[...], w_ref[3])
        z = ALPHA * h_ref[...] + acc
        y, xhat, rstd = _ln_rows(z, g_ref[...], b_ref[...])
        hn_ref[...] = y
        hnb_ref[...] = y.astype(BF16)
        xh_ref[...] = xhat
        rs_ref[...] = rstd

    row = lambda i: (i, 0)
    return pl.pallas_call(
        body, name=f"out_fwd{l}", grid=(T // tm,),
        in_specs=[pl.BlockSpec((tm, CW), row), pl.BlockSpec((tm, AW), row), pl.BlockSpec((tm, LW), row),
                  pl.BlockSpec((N_SHARD, WOUT_SH, D), lambda i: (0, 0, 0)),
                  pl.BlockSpec((tm, D), row),
                  pl.BlockSpec((None, 1, D), lambda i: (l, 0, 0)),
                  pl.BlockSpec((None, 1, D), lambda i: (l, 0, 0))],
        out_specs=[pl.BlockSpec((tm, D), row), pl.BlockSpec((tm, D), row), pl.BlockSpec((tm, D), row),
                   pl.BlockSpec((tm, 1), row)],
        out_shape=[jax.ShapeDtypeStruct((T, D), F32), jax.ShapeDtypeStruct((T, D), BF16),
                   jax.ShapeDtypeStruct((T, D), F32), jax.ShapeDtypeStruct((T, 1), F32)],
        compiler_params=_cp("parallel"),
    )(yc, ya, yl, w_out, h, g, b)


def _post_ln_bwd(dhn, xhat, rstd, g, l):
    T = dhn.shape[0]
    tm = _pick(T, (384, 128))

    def body(d_ref, xh_ref, rs_ref, g_ref, dz_ref, dzb_ref, dg_ref, db_ref):
        @pl.when(pl.program_id(0) == 0)
        def _():
            dg_ref[...] = jnp.zeros_like(dg_ref)
            db_ref[...] = jnp.zeros_like(db_ref)

        dy = d_ref[...]
        xhat = xh_ref[...]
        dz = _ln_bwd_rows(dy, xhat, rs_ref[...], g_ref[...])
        dz_ref[...] = dz
        dzb_ref[...] = dz.astype(BF16)
        dg_ref[...] += _colsum(dy * xhat)
        db_ref[...] += _colsum(dy)

    row = lambda i: (i, 0)
    const = lambda i: (0, 0)
    return pl.pallas_call(
        body, name=f"post_ln_bwd{l}", grid=(T // tm,),
        in_specs=[pl.BlockSpec((tm, D), row), pl.BlockSpec((tm, D), row), pl.BlockSpec((tm, 1), row),
                  pl.BlockSpec((None, 1, D), lambda i: (l, 0, 0))],
        out_specs=[pl.BlockSpec((tm, D), row), pl.BlockSpec((tm, D), row),
                   pl.BlockSpec((1, D), const), pl.BlockSpec((1, D), const)],
        out_shape=[jax.ShapeDtypeStruct((T, D), F32), jax.ShapeDtypeStruct((T, D), BF16),
                   jax.ShapeDtypeStruct((1, D), F32), jax.ShapeDtypeStruct((1, D), F32)],
        compiler_params=_cp("arbitrary"),
    )(dhn, xhat, rstd, g)


def _dcat_bwd(dzb, w_out, l, job=None):
    T = dzb.shape[0]
    tm = _pick(T, (384, 128))

    def body(dz_ref, w_ref, dc_ref, da_ref, dl_ref):
        dz = dz_ref[...]
        dc_ref[...] = _dot_nt(dz, w_ref[0])
        da_ref[:, 0:WOUT_SH] = _dot_nt(dz, w_ref[1])
        da_ref[:, WOUT_SH:2 * WOUT_SH] = _dot_nt(dz, w_ref[2])
        dl_ref[...] = _dot_nt(dz, w_ref[3])

    row = lambda i: (i, 0)
    return _side_call(
        body, job, name=f"dcat_bwd{l}", grid=(T // tm,),
        in_specs=[pl.BlockSpec((tm, D), row),
                  pl.BlockSpec((N_SHARD, WOUT_SH, D), lambda i: (0, 0, 0))],
        out_specs=[pl.BlockSpec((tm, CW), row), pl.BlockSpec((tm, AW), row), pl.BlockSpec((tm, LW), row)],
        out_shape=[jax.ShapeDtypeStruct((T, CW), F32), jax.ShapeDtypeStruct((T, AW), F32),
                   jax.ShapeDtypeStruct((T, LW), F32)],
        scratch_shapes=[], semantics=("parallel",), args=[dzb, w_out])


def _dwout_bwd(yc, ya, yl, dzb, l):
    T = dzb.shape[0]
    tm = _pick(T, (1056, 384, 128))
    hr = WOUT_SH // 2
    nt = T // tm

    def body(yc_ref, ya_ref, yl_ref, dz_ref, o_ref):
        j = pl.program_id(0)
        t = pl.program_id(2)

        @pl.when(t == 0)
        def _():
            o_ref[...] = jnp.zeros_like(o_ref)

        dz = dz_ref[...]

        @pl.when(j == 0)
        def _():
            o_ref[...] += _dot_tn(yc_ref[...], dz)

        @pl.when((j == 1) | (j == 2))
        def _():
            o_ref[...] += _dot_tn(ya_ref[...], dz)

        @pl.when(j == 3)
        def _():
            o_ref[...] += _dot_tn(yl_ref[...], dz)

    return pl.pallas_call(
        body, name=f"dwout_bwd{l}", grid=(N_SHARD, 2, nt),
        in_specs=[pl.BlockSpec((tm, hr), lambda j, r, t: (t, r)),
                  pl.BlockSpec((tm, hr), lambda j, r, t: (t, 2 * jnp.clip(j - 1, 0, 1) + r)),
                  pl.BlockSpec((tm, hr), lambda j, r, t: (t, r)),
                  pl.BlockSpec((tm, D), lambda j, r, t: (t, 0))],
        out_specs=pl.BlockSpec((None, None, hr, D), lambda j, r, t: (j, r, 0, 0)),
        out_shape=jax.ShapeDtypeStruct((N_SHARD, 2, hr, D), F32),
        compiler_params=_cp("parallel", "parallel", "arbitrary"),
    )(yc, ya, yl, dzb)


def _dh_bwd(dproj, w_in, dz, l):
    T = dproj.shape[0]
    tm = _pick(T, (1056, 384, 128))

    def body(dp_ref, w_ref, dz_ref, o_ref, acc_ref):
        j = pl.program_id(1)

        @pl.when(j == 0)
        def _():
            acc_ref[...] = ALPHA * dz_ref[...]

        acc_ref[...] += _dot_nt(dp_ref[...], w_ref[...])

        @pl.when(j == N_SHARD - 1)
        def _():
            o_ref[...] = acc_ref[...]

    return pl.pallas_call(
        body, name=f"dh_bwd{l}", grid=(T // tm, N_SHARD),
        in_specs=[pl.BlockSpec((tm, WIN_SH), lambda i, j: (i, j)),
                  pl.BlockSpec((None, D, WIN_SH), lambda i, j: (j, 0, 0)),
                  pl.BlockSpec((tm, D), lambda i, j: (i, 0))],
        out_specs=pl.BlockSpec((tm, D), lambda i, j: (i, 0)),
        out_shape=jax.ShapeDtypeStruct((T, D), F32),
        scratch_shapes=[pltpu.VMEM((tm, D), F32)],
        compiler_params=_cp("parallel", "arbitrary"),
    )(dproj, w_in, dz)


def _dwin_bwd(hb, dproj, l):
    T = hb.shape[0]
    tm = _pick(T, (1056, 384, 128))
    hr = D // 2

    def body(h_ref, dp_ref, o_ref):
        @pl.when(pl.program_id(2) == 0)
        def _():
            o_ref[...] = jnp.zeros_like(o_ref)

        o_ref[...] += _dot_tn(h_ref[...], dp_ref[...])

    return pl.pallas_call(
        body, name=f"dwin_bwd{l}", grid=(N_SHARD, 2, T // tm),
        in_specs=[pl.BlockSpec((tm, hr), lambda j, r, t: (t, r)),
                  pl.BlockSpec((tm, WIN_SH), lambda j, r, t: (t, j))],
        out_specs=pl.BlockSpec((None, None, hr, WIN_SH), lambda j, r, t: (j, r, 0, 0)),
        out_shape=jax.ShapeDtypeStruct((N_SHARD, 2, hr, WIN_SH), F32),
        compiler_params=_cp("parallel", "parallel", "arbitrary"),
    )(hb, dproj)


def _glu_masked(v, g, base_row):
    rows = _row_ids(v.shape[0], base_row)
    return jnp.where(rows >= PAD, v * _sigmoid(g), 0.0)


def _conv_tile(T):
    return _pick(T, (384, 128))


def _conv_fwd(proj, dw_w, dw_b, ln_g, ln_b, pw_w, pw_b, l):
    T = proj.shape[0]
    tm = _conv_tile(T)
    hb = tm // HALO

    def body(cv_ref, cg_ref, ct_ref, hv_ref, hg_ref, w_ref, b_ref, g_ref, be_ref, pw_ref, pb_ref,
             yc_ref, conv_ref, buf):
        i = pl.program_id(0)
        buf[0:HALO, :] = _glu_masked(hv_ref[...], hg_ref[...], i * tm - HALO)
        buf[HALO:HALO + tm, :] = _glu_masked(cv_ref[...], cg_ref[...], i * tm)
        acc = jnp.zeros((tm, CW), F32) + b_ref[...]
        for k in range(CONV_K):
            o = HALO - (CONV_K - 1) + k
            acc += w_ref[k:k + 1, :] * buf[o:o + tm, :]
        conv_ref[...] = acc
        u, _, _ = _ln_rows(acc, g_ref[...], be_ref[...])
        s = u * _sigmoid(u)
        cpw = _dot(s.astype(BF16), pw_ref[...]) + pb_ref[...]
        gate, _ = _silu_and_grad(ct_ref[...])
        yc_ref[...] = (cpw * gate).astype(BF16)

    vec = pl.BlockSpec((None, 1, CW), lambda i: (l, 0, 0))
    return pl.pallas_call(
        body, name=f"conv_fwd{l}", grid=(T // tm,),
        in_specs=[pl.BlockSpec((tm, CW), lambda i: (i, 0)),
                  pl.BlockSpec((tm, CW), lambda i: (i, 1)),
                  pl.BlockSpec((tm, CW), lambda i: (i, 2)),
                  pl.BlockSpec((HALO, CW), lambda i: (jnp.maximum(i * hb - 1, 0), 0)),
                  pl.BlockSpec((HALO, CW), lambda i: (jnp.maximum(i * hb - 1, 0), 1)),
                  pl.BlockSpec((None, CONV_K, CW), lambda i: (l, 0, 0)),
                  vec, vec, vec,
                  pl.BlockSpec((CW, CW), lambda i: (0, 0)),
                  vec],
        out_specs=[pl.BlockSpec((tm, CW), lambda i: (i, 0)), pl.BlockSpec((tm, CW), lambda i: (i, 0))],
        out_shape=[jax.ShapeDtypeStruct((T, CW), BF16), jax.ShapeDtypeStruct((T, CW), F32)],
        scratch_shapes=[pltpu.VMEM((tm + HALO, CW), F32)],
        compiler_params=_cp("parallel"),
    )(proj, proj, proj, proj, proj, dw_w, dw_b, ln_g, ln_b, pw_w, pw_b)


def _conv_bwd_rows(conv, proj, d_yc, ln_g, ln_b, pw_w, pw_b, l):
    T = conv.shape[0]
    tm = _conv_tile(T)

    def body(conv_ref, ct_ref, dy_ref, g_ref, be_ref, pw_ref, pb_ref,
             dconv_ref, dct_ref, dpw_ref, dpb_ref, dg_ref, db_ref):
        @pl.when(pl.program_id(0) == 0)
        def _():
            dpw_ref[...] = jnp.zeros_like(dpw_ref)
            dpb_ref[...] = jnp.zeros_like(dpb_ref)
            dg_ref[...] = jnp.zeros_like(dg_ref)
            db_ref[...] = jnp.zeros_like(db_ref)

        u, xhat, rstd = _ln_rows(conv_ref[...], g_ref[...], be_ref[...])
        s, ds_du = _silu_and_grad(u)
        sb = s.astype(BF16)
        cpw = _dot(sb, pw_ref[...]) + pb_ref[...]
        gate, dgate = _silu_and_grad(ct_ref[...])
        dy = dy_ref[...]
        d_cpw = dy * gate
        dct_ref[...] = (dy * cpw * dgate).astype(BF16)
        d_cpw_b = d_cpw.astype(BF16)
        dpb_ref[...] += _colsum(d_cpw)
        dpw_ref[...] += _dot_tn(sb, d_cpw_b)
        du = _dot_nt(d_cpw_b, pw_ref[...]) * ds_du
        dconv_ref[...] = _ln_bwd_rows(du, xhat, rstd, g_ref[...])
        dg_ref[...] += _colsum(du * xhat)
        db_ref[...] += _colsum(du)

    vec = pl.BlockSpec((None, 1, CW), lambda i: (l, 0, 0))
    row = lambda i: (i, 0)
    const = lambda i: (0, 0)
    return pl.pallas_call(
        body, name=f"conv_bwd_rows{l}", grid=(T // tm,),
        in_specs=[pl.BlockSpec((tm, CW), row), pl.BlockSpec((tm, CW), lambda i: (i, 2)),
                  pl.BlockSpec((tm, CW), row), vec, vec,
                  pl.BlockSpec((CW, CW), lambda i: (0, 0)), vec],
        out_specs=[pl.BlockSpec((tm, CW), row), pl.BlockSpec((tm, CW), lambda i: (i, 2)),
                   pl.BlockSpec((CW, CW), const), pl.BlockSpec((1, CW), const),
                   pl.BlockSpec((1, CW), const), pl.BlockSpec((1, CW), const)],
        out_shape=[jax.ShapeDtypeStruct((T, CW), F32), jax.ShapeDtypeStruct((T, IN_TOTAL), BF16),
                   jax.ShapeDtypeStruct((CW, CW), F32), jax.ShapeDtypeStruct((1, CW), F32),
                   jax.ShapeDtypeStruct((1, CW), F32), jax.ShapeDtypeStruct((1, CW), F32)],
        compiler_params=_cp("arbitrary"),
    )(conv, proj, d_yc, ln_g, ln_b, pw_w, pw_b)


def _conv_bwd_taps(d_conv, proj, dw_w, dproj, l):
    T = d_conv.shape[0]
    tm = _conv_tile(T)
    hb = tm // HALO
    nt = T // tm
    last_halo = T // HALO - 1

    def body(dc_ref, dh_ref, cv_ref, cg_ref, hv_ref, hg_ref, w_ref, _, o_ref, dw_ref, dwb_ref, cbuf, dbuf):
        i = pl.program_id(0)

        @pl.when(i == 0)
        def _():
            dw_ref[...] = jnp.zeros_like(dw_ref)
            dwb_ref[...] = jnp.zeros_like(dwb_ref)

        cbuf[0:HALO, :] = _glu_masked(hv_ref[...], hg_ref[...], i * tm - HALO)
        cbuf[HALO:HALO + tm, :] = _glu_masked(cv_ref[...], cg_ref[...], i * tm)
        dmain = dc_ref[...]
        dbuf[0:tm, :] = dmain
        dbuf[tm:tm + HALO, :] = jnp.where(i < nt - 1, dh_ref[...], 0.0)
        acc = jnp.zeros((tm, CW), F32)
        for k in range(CONV_K):
            o = CONV_K - 1 - k
            acc += w_ref[k:k + 1, :] * dbuf[o:o + tm, :]
            oc = HALO - (CONV_K - 1) + k
            dw_ref[k:k + 1, :] += _colsum(dmain * cbuf[oc:oc + tm, :])
        dwb_ref[...] += _colsum(dmain)
        d_c = jnp.where(_row_ids(tm, i * tm) >= PAD, acc, 0.0)
        sig = _sigmoid(cg_ref[...])
        o_ref[:, 0:CW] = (d_c * sig).astype(BF16)
        o_ref[:, CW:2 * CW] = (d_c * cv_ref[...] * sig * (1.0 - sig)).astype(BF16)

    const = lambda i: (0, 0)
    return pl.pallas_call(
        body, name=f"conv_bwd_taps{l}", grid=(nt,),
        in_specs=[pl.BlockSpec((tm, CW), lambda i: (i, 0)),
                  pl.BlockSpec((HALO, CW), lambda i: (jnp.minimum((i + 1) * hb, last_halo), 0)),
                  pl.BlockSpec((tm, CW), lambda i: (i, 0)),
                  pl.BlockSpec((tm, CW), lambda i: (i, 1)),
                  pl.BlockSpec((HALO, CW), lambda i: (jnp.maximum(i * hb - 1, 0), 0)),
                  pl.BlockSpec((HALO, CW), lambda i: (jnp.maximum(i * hb - 1, 0), 1)),
                  pl.BlockSpec((None, CONV_K, CW), lambda i: (l, 0, 0)),
                  pl.BlockSpec(memory_space=pl.ANY)],
        out_specs=[pl.BlockSpec((tm, 2 * CW), lambda i: (i, 0)),
                   pl.BlockSpec((HALO, CW), const), pl.BlockSpec((1, CW), const)],
        out_shape=[jax.ShapeDtypeStruct(dproj.shape, BF16), jax.ShapeDtypeStruct((HALO, CW), F32),
                   jax.ShapeDtypeStruct((1, CW), F32)],
        scratch_shapes=[pltpu.VMEM((tm + HALO, CW), F32), pltpu.VMEM((tm + HALO, CW), F32)],
        input_output_aliases={7: 0},
        compiler_params=_cp("arbitrary"),
    )(d_conv, d_conv, proj, proj, proj, proj, dw_w, dproj)


def _log1p_small(e):
    return jnp.where(e < 1e-3, e * (1.0 - e * (0.5 - e * (1.0 / 3.0))), jnp.log(1.0 + e))


def _softplus(z):
    return jnp.maximum(z, 0.0) + _log1p_small(jnp.exp(-jnp.abs(z)))


def _neg_expm1(x):
    series = -x * (1.0 + x * (1.0 / 2.0) * (1.0 + x * (1.0 / 3.0) * (1.0 + x * (1.0 / 4.0) * (
        1.0 + x * (1.0 / 5.0) * (1.0 + x * (1.0 / 6.0) * (1.0 + x * (1.0 / 7.0)))))))
    return jnp.where(x > -0.25, series, 1.0 - jnp.exp(x))


def _lru_gates(rxbuf, tm, base_row, lw_ref, lb_ref, wa_ref, ba_ref, wx_ref, bx_ref, lam_ref):
    rc = jnp.zeros((tm, LW), F32) + lb_ref[...]
    for k in range(LRU_K):
        o = LHALO - (LRU_K - 1) + k
        rc += lw_ref[k:k + 1, :] * rxbuf[o:o + tm, :]
    rcb = rc.astype(BF16)
    r = _sigmoid(_dot(rcb, wa_ref[...]) + ba_ref[...])
    ig = _sigmoid(_dot(rcb, wx_ref[...]) + bx_ref[...])
    sp = _softplus(-lam_ref[...])
    la = -LRU_C * r * sp
    a = jnp.exp(la)
    mult = jnp.sqrt(_neg_expm1(2.0 * la))
    valid = _row_ids(tm, base_row) >= PAD
    return rc, rcb, r, ig, sp, a, mult, valid


def _mask_rows(v, base_row):
    return jnp.where(_row_ids(v.shape[0], base_row) >= PAD, v, 0.0)


def _scan_steps(tm):
    s, out = 1, []
    while s < tm:
        out.append(s)
        s *= 2
    return out


def _lru_tile(T):
    return _pick(T, (384, 128))


def _lru_fwd(proj, lw, lb, wa, ba, wx, bx, lam, l):
    T = proj.shape[0]
    tm = _lru_tile(T)
    hb = tm // LHALO

    def body(rx_ref, rg_ref, hx_ref, lw_ref, lb_ref, wa_ref, ba_ref, wx_ref, bx_ref, lam_ref,
             yl_ref, hl_ref, rxbuf, carry):
        i = pl.program_id(0)

        @pl.when(i == 0)
        def _():
            carry[...] = jnp.zeros_like(carry)

        rxbuf[0:LHALO, :] = _mask_rows(hx_ref[...], i * tm - LHALO)
        rxbuf[LHALO:LHALO + tm, :] = _mask_rows(rx_ref[...], i * tm)
        rc, _, _, ig, _, a, mult, valid = _lru_gates(rxbuf, tm, i * tm, lw_ref, lb_ref, wa_ref, ba_ref,
                                                     wx_ref, bx_ref, lam_ref)
        bb = jnp.where(valid, mult * (ig * rc), 0.0)
        aa = a
        rows = _row_ids(tm, 0)
        for s in _scan_steps(tm):
            keep = rows >= s
            a_s = jnp.where(keep, pltpu.roll(aa, s, axis=0), 1.0)
            b_s = jnp.where(keep, pltpu.roll(bb, s, axis=0), 0.0)
            bb = aa * b_s + bb
            aa = aa * a_s
        h = bb + aa * carry[0:1, :]
        hl_ref[...] = h
        carry[0:1, :] = hl_ref[tm - 1:tm, :]
        gate, _ = _silu_and_grad(rg_ref[...])
        yl_ref[...] = (h * gate).astype(BF16)

    vec = pl.BlockSpec((None, 1, LW), lambda i: (l, 0, 0))
    mat = pl.BlockSpec((None, LW, LW), lambda i: (l, 0, 0))
    return pl.pallas_call(
        body, name=f"lru_fwd{l}", grid=(T // tm,),
        in_specs=[pl.BlockSpec((tm, LW), lambda i: (i, 8)),
                  pl.BlockSpec((tm, LW), lambda i: (i, 9)),
                  pl.BlockSpec((LHALO, LW), lambda i: (jnp.maximum(i * hb - 1, 0), 8)),
                  pl.BlockSpec((None, LRU_K, LW), lambda i: (l, 0, 0)),
                  vec, mat, vec, mat, vec, vec],
        out_specs=[pl.BlockSpec((tm, LW), lambda i: (i, 0)), pl.BlockSpec((tm, LW), lambda i: (i, 0))],
        out_shape=[jax.ShapeDtypeStruct((T, LW), BF16), jax.ShapeDtypeStruct((T, LW), F32)],
        scratch_shapes=[pltpu.VMEM((tm + LHALO, LW), F32), pltpu.VMEM((8, LW), F32)],
        compiler_params=_cp("arbitrary"),
    )(proj, proj, proj, lw, lb, wa, ba, wx, bx, lam)


def _lru_bwd(proj, hl, d_yl, lw, lb, wa, ba, wx, bx, lam, dproj, l):
    T = proj.shape[0]
    tm = _lru_tile(T)
    hb = tm // LHALO
    nt = T // tm

    def body(rx_ref, rg_ref, hx_ref, hl_ref, hh_ref, dy_ref, lw_ref, lb_ref, wa_ref, ba_ref, wx_ref, bx_ref,
             lam_ref, _, o_ref, dlw_ref, dlb_ref, dwa_ref, dba_ref, dwx_ref, dbx_ref, dlam_ref,
             rxbuf, dbuf, carry, head):
        step = pl.program_id(0)
        i = nt - 1 - step

        @pl.when(step == 0)
        def _():
            carry[...] = jnp.zeros_like(carry)
            head[...] = jnp.zeros_like(head)
            for ref in (dlw_ref, dlb_ref, dwa_ref, dba_ref, dwx_ref, dbx_ref, dlam_ref):
                ref[...] = jnp.zeros_like(ref)

        rxbuf[0:LHALO, :] = _mask_rows(hx_ref[...], i * tm - LHALO)
        rxbuf[LHALO:LHALO + tm, :] = _mask_rows(rx_ref[...], i * tm)
        rc, rcb, r, ig, sp, a, mult, valid = _lru_gates(rxbuf, tm, i * tm, lw_ref, lb_ref, wa_ref, ba_ref,
                                                        wx_ref, bx_ref, lam_ref)
        rows = _row_ids(tm, 0)
        h = hl_ref[...]
        h_before = jnp.where(i > 0, hh_ref[LHALO - 1:LHALO, :], 0.0)
        hprev = jnp.where(rows == 0, h_before, pltpu.roll(h, 1, axis=0))
        rg = rg_ref[...]
        gate, dgate = _silu_and_grad(rg)
        dy = dy_ref[...]
        o_ref[:, LW:2 * LW] = (dy * h * dgate).astype(BF16)
        bb = dy * gate + jnp.where(rows == tm - 1, carry[0:1, :], 0.0)
        aa = jnp.where(rows == tm - 1, 0.0, pltpu.roll(a, tm - 1, axis=0))
        for s in _scan_steps(tm):
            keep = rows < tm - s
            a_s = jnp.where(keep, pltpu.roll(aa, tm - s, axis=0), 1.0)
            b_s = jnp.where(keep, pltpu.roll(bb, tm - s, axis=0), 0.0)
            bb = aa * b_s + bb
            aa = aa * a_s
        g = bb
        dbuf[0:tm, :] = a * g
        carry[0:1, :] = dbuf[0:1, :]
        du = jnp.where(valid, g, 0.0)
        da = g * hprev
        dix = du * mult
        dmult = du * (ig * rc)
        dla = jnp.where(valid, da * a - dmult * (a * a) / mult, 0.0)
        dr = dla * (-LRU_C * sp)
        dlam_ref[...] += _colsum(dla * (LRU_C * r)) * _sigmoid(-lam_ref[...])
        dpa = dr * r * (1.0 - r)
        dpx = (dix * rc) * ig * (1.0 - ig)
        dpab = dpa.astype(BF16)
        dpxb = dpx.astype(BF16)
        dba_ref[...] += _colsum(dpa)
        dbx_ref[...] += _colsum(dpx)
        dwa_ref[...] += _dot_tn(rcb, dpab)
        dwx_ref[...] += _dot_tn(rcb, dpxb)
        drc = dix * ig + _dot_nt(dpab, wa_ref[...]) + _dot_nt(dpxb, wx_ref[...])
        dbuf[0:tm, :] = drc
        dbuf[tm:tm + LHALO, :] = head[...]
        acc = jnp.zeros((tm, LW), F32)
        for k in range(LRU_K):
            o = LRU_K - 1 - k
            acc += lw_ref[k:k + 1, :] * dbuf[o:o + tm, :]
            oc = LHALO - (LRU_K - 1) + k
            dlw_ref[k:k + 1, :] += _colsum(drc * rxbuf[oc:oc + tm, :])
        dlb_ref[...] += _colsum(drc)
        head[...] = dbuf[0:LHALO, :]
        o_ref[:, 0:LW] = jnp.where(valid, acc, 0.0).astype(BF16)

    rev = lambda s: nt - 1 - s
    vec = pl.BlockSpec((None, 1, LW), lambda s: (l, 0, 0))
    mat = pl.BlockSpec((None, LW, LW), lambda s: (l, 0, 0))
    const = lambda s: (0, 0)
    halo = lambda s: jnp.maximum(rev(s) * hb - 1, 0)
    return pl.pallas_call(
        body, name=f"lru_bwd{l}", grid=(nt,),
        in_specs=[pl.BlockSpec((tm, LW), lambda s: (rev(s), 8)),
                  pl.BlockSpec((tm, LW), lambda s: (rev(s), 9)),
                  pl.BlockSpec((LHALO, LW), lambda s: (halo(s), 8)),
                  pl.BlockSpec((tm, LW), lambda s: (rev(s), 0)),
                  pl.BlockSpec((LHALO, LW), lambda s: (halo(s), 0)),
                  pl.BlockSpec((tm, LW), lambda s: (rev(s), 0)),
                  pl.BlockSpec((None, LRU_K, LW), lambda s: (l, 0, 0)),
                  vec, mat, vec, mat, vec, vec, pl.BlockSpec(memory_space=pl.ANY)],
        out_specs=[pl.BlockSpec((tm, 2 * LW), lambda s: (rev(s), 4)),
                   pl.BlockSpec((8, LW), const), pl.BlockSpec((1, LW), const),
                   pl.BlockSpec((LW, LW), const), pl.BlockSpec((1, LW), const),
                   pl.BlockSpec((LW, LW), const), pl.BlockSpec((1, LW), const),
                   pl.BlockSpec((1, LW), const)],
        out_shape=[jax.ShapeDtypeStruct(dproj.shape, BF16),
                   jax.ShapeDtypeStruct((8, LW), F32), jax.ShapeDtypeStruct((1, LW), F32),
                   jax.ShapeDtypeStruct((LW, LW), F32), jax.ShapeDtypeStruct((1, LW), F32),
                   jax.ShapeDtypeStruct((LW, LW), F32), jax.ShapeDtypeStruct((1, LW), F32),
                   jax.ShapeDtypeStruct((1, LW), F32)],
        scratch_shapes=[pltpu.VMEM((tm + LHALO, LW), F32), pltpu.VMEM((tm + LHALO, LW), F32),
                        pltpu.VMEM((8, LW), F32), pltpu.VMEM((LHALO, LW), F32)],
        input_output_aliases={13: 0},
        compiler_params=_cp("arbitrary"),
    )(proj, proj, proj, hl, hl, d_yl, lw, lb, wa, ba, wx, bx, lam, dproj)


def _rope_tables(T):
    pos = (lax.broadcasted_iota(jnp.int32, (T, 128), 0) - PAD).astype(F32)
    lane = lax.broadcasted_iota(jnp.int32, (T, 128), 1) % 64
    inv_freq = ROPE_THETA ** (-(lane % ROT_HALF).astype(F32) / ROT_HALF)
    ang = pos * inv_freq
    cos, sin = jnp.cos(ang), jnp.sin(ang)
    c = jnp.where(lane < 2 * ROT_HALF, cos, 1.0)
    s1 = jnp.where(lane < ROT_HALF, -sin, 0.0)
    s2 = jnp.where((lane >= ROT_HALF) & (lane < 2 * ROT_HALF), sin, 0.0)
    return c, s1, s2


def _rot_fwd(x, c, s1, s2):
    return x * c + pltpu.roll(x, 128 - ROT_HALF, axis=1) * s1 + pltpu.roll(x, ROT_HALF, axis=1) * s2


def _rot_bwd(dy, c, s1, s2):
    return dy * c + pltpu.roll(dy * s1, ROT_HALF, axis=1) + pltpu.roll(dy * s2, 128 - ROT_HALF, axis=1)


def _rope_fwd(proj, tabs, l):
    T = proj.shape[0]

    def body(ql_ref, qh_ref, k_ref, v_ref, c_ref, s1_ref, s2_ref, qr_ref, kr_ref, vb_ref):
        c, s1, s2 = c_ref[...], s1_ref[...], s2_ref[...]
        for gcol in range(AW // 128):
            src = ql_ref if gcol < 4 else qh_ref
            x = src[:, 128 * (gcol % 4):128 * (gcol % 4) + 128]
            qr_ref[:, 128 * gcol:128 * gcol + 128] = (_rot_fwd(x, c, s1, s2) * 0.125).astype(BF16)
        for gcol in range(KVW // 128):
            x = k_ref[:, 128 * gcol:128 * gcol + 128]
            kr_ref[:, 128 * gcol:128 * gcol + 128] = _rot_fwd(x, c, s1, s2).astype(BF16)
        vb_ref[...] = v_ref[...].astype(BF16)

    tab = pl.BlockSpec((BLK, 128), lambda n: (n, 0))
    return pl.pallas_call(
        body, name=f"rope_fwd{l}", grid=(T // BLK,),
        in_specs=[pl.BlockSpec((BLK, 512), lambda n: (n, 3)), pl.BlockSpec((BLK, 512), lambda n: (n, 4)),
                  pl.BlockSpec((BLK, KVW), lambda n: (n, 10)), pl.BlockSpec((BLK, KVW), lambda n: (n, 11)),
                  tab, tab, tab],
        out_specs=[pl.BlockSpec((BLK, AW), lambda n: (n, 0)), pl.BlockSpec((BLK, KVW), lambda n: (n, 0)),
                   pl.BlockSpec((BLK, KVW), lambda n: (n, 0))],
        out_shape=[jax.ShapeDtypeStruct((T, AW), BF16), jax.ShapeDtypeStruct((T, KVW), BF16),
                   jax.ShapeDtypeStruct((T, KVW), BF16)],
        compiler_params=_cp("parallel"),
    )(proj, proj, proj, proj, *tabs)


def _attn_mask(n):
    qi = lax.broadcasted_iota(jnp.int32, (BLK, BLK), 0)
    kj = lax.broadcasted_iota(jnp.int32, (BLK, BLK), 1)
    m0 = (kj >= PAD) & (n >= 1)
    mp = (kj > qi) & (n >= 2)
    mc = (kj <= qi) & ((n >= 1) | (kj >= PAD))
    return jnp.concatenate([m0, mp, mc], axis=1)


def _kv_halves(x0_ref, xp_ref, xc_ref, g):
    pg, off = g // 2, g % 2
    cols = slice(128 * pg, 128 * pg + 128)
    x = jnp.concatenate([x0_ref[:, cols], xp_ref[:, cols], xc_ref[:, cols]], axis=0).astype(F32)
    lane = lax.broadcasted_iota(jnp.int32, (1, 128), 1)
    if off == 0:
        lo = jnp.where(lane < 64, x, 0.0)
        hi = pltpu.roll(lo, 64, axis=1)
    else:
        hi = jnp.where(lane >= 64, x, 0.0)
        lo = pltpu.roll(hi, 64, axis=1)
    return lo.astype(BF16), hi.astype(BF16)


def _attn_fwd(qr, kr, vb, proj, sinks, l, job=None):
    T = qr.shape[0]

    def body(sink_ref, q_ref, k0_ref, kp_ref, kc_ref, v0_ref, vp_ref, vc_ref, ag_ref, ya_ref, att_ref, lse_ref):
        n = pl.program_id(0)
        mask = _attn_mask(n)
        lane = lax.broadcasted_iota(jnp.int32, (1, 128), 1)
        lse_acc = jnp.zeros((BLK, 128), F32)
        for g in range(4):
            k_lo, k_hi = _kv_halves(k0_ref, kp_ref, kc_ref, g)
            v_lo, v_hi = _kv_halves(v0_ref, vp_ref, vc_ref, g)
            for pp in range(2):
                cols = slice(128 * (2 * g + pp), 128 * (2 * g + pp) + 128)
                qpair = q_ref[:, cols]
                out = jnp.zeros((BLK, 128), F32)
                for hh, (kx, vx) in enumerate(((k_lo, v_lo), (k_hi, v_hi))):
                    h = 4 * g + 2 * pp + hh
                    sink = sink_ref[l, h]
                    s = jnp.where(mask, _dot_nt(qpair, kx), NEG_INF)
                    m = jnp.maximum(jnp.max(s, axis=1, keepdims=True), sink)
                    p = jnp.exp(s - m)
                    denom = jnp.sum(p, axis=1, keepdims=True) + jnp.exp(sink - m)
                    out += _dot((p / denom).astype(BF16), vx)
                    lse_acc = jnp.where(lane == h, m + jnp.log(denom), lse_acc)
                att_ref[:, cols] = out
                gate, _ = _silu_and_grad(ag_ref[:, cols])
                ya_ref[:, cols] = (out * gate).astype(BF16)
        lse_ref[...] = lse_acc

    prev = lambda n: (jnp.maximum(n - 1, 0), 0)
    cur = lambda n: (n, 0)
    zero = lambda n: (0, 0)
    kv = lambda f: pl.BlockSpec((BLK, KVW), f)
    return _side_call(
        body, job, name=f"attn_fwd{l}", grid=(T // BLK,),
        in_specs=[pl.BlockSpec(memory_space=pltpu.SMEM),
                  pl.BlockSpec((BLK, AW), cur), kv(zero), kv(prev), kv(cur), kv(zero), kv(prev), kv(cur),
                  pl.BlockSpec((BLK, AW), lambda n: (n, 3))],
        out_specs=[pl.BlockSpec((BLK, AW), cur), pl.BlockSpec((BLK, AW), cur), pl.BlockSpec((BLK, 128), cur)],
        out_shape=[jax.ShapeDtypeStruct((T, AW), BF16), jax.ShapeDtypeStruct((T, AW), F32),
                   jax.ShapeDtypeStruct((T, 128), F32)],
        scratch_shapes=[], semantics=("parallel",), args=[sinks, qr, kr, kr, kr, vb, vb, vb, proj])


def _attn_bwd(qr, kr, vb, proj, att, lse, d_ya, sinks, dproj, l, job=None):
    T = qr.shape[0]
    nb = T // BLK

    def body(sink_ref, q_ref, k0_ref, kp_ref, kc_ref, v0_ref, vp_ref, vc_ref, ag_ref, att_ref, lse_ref, dy_ref, _,
             dq_ref, dk_ref, dv_ref, dk0_ref, dv0_ref, dag_ref, dsink_ref, kcarry, vcarry):
        n = pl.program_id(0)

        @pl.when(n == 0)
        def _():
            dk0_ref[...] = jnp.zeros_like(dk0_ref)
            dv0_ref[...] = jnp.zeros_like(dv0_ref)
            dsink_ref[...] = jnp.zeros_like(dsink_ref)
            kcarry[...] = jnp.zeros_like(kcarry)
            vcarry[...] = jnp.zeros_like(vcarry)

        @pl.when(n == nb)
        def _():
            dk_ref[...] = kcarry[...]
            dv_ref[...] = vcarry[...]

        @pl.when(n < nb)
        def _():
            mask = _attn_mask(n)
            lane = lax.broadcasted_iota(jnp.int32, (1, 128), 1)
            lse = lse_ref[...]
            dsink = jnp.zeros((1, 128), F32)
            dk_pg, dv_pg = [], []
            for pg in range(2):
                dk_acc = jnp.zeros((3 * BLK, 128), F32)
                dv_acc = jnp.zeros((3 * BLK, 128), F32)
                for off in range(2):
                    g = 2 * pg + off
                    k_lo, k_hi = _kv_halves(k0_ref, kp_ref, kc_ref, g)
                    v_lo, v_hi = _kv_halves(v0_ref, vp_ref, vc_ref, g)
                    dkg = jnp.zeros((3 * BLK, 128), F32)
                    dvg = jnp.zeros((3 * BLK, 128), F32)
                    for pp in range(2):
                        cols = slice(128 * (2 * g + pp), 128 * (2 * g + pp) + 128)
                        qpair = q_ref[:, cols]
                        gate, dgate = _silu_and_grad(ag_ref[:, cols])
                        dy = dy_ref[:, cols]
                        dag_ref[:, cols] = (dy * att_ref[:, cols] * dgate).astype(BF16)
                        do = (dy * gate).astype(BF16)
                        dq = jnp.zeros((BLK, 128), F32)
                        for hh, (kx, vx) in enumerate(((k_lo, v_lo), (k_hi, v_hi))):
                            h = 4 * g + 2 * pp + hh
                            sink = sink_ref[l, h]
                            lse_h = jnp.sum(jnp.where(lane == h, lse, 0.0), axis=1, keepdims=True)
                            s = _dot_nt(qpair, kx)
                            p = jnp.where(mask, jnp.exp(s - lse_h), 0.0)
                            dp = _dot_nt(do, vx)
                            delta = jnp.sum(p * dp, axis=1, keepdims=True)
                            ds = (p * (dp - delta)).astype(BF16)
                            psink = jnp.exp(sink - lse_h)
                            dsink += jnp.where(lane == h, -jnp.sum(psink * delta), 0.0)
                            dq += _dot(ds, kx)
                            half = (lane < 64) if hh == 0 else (lane >= 64)
                            dkg += jnp.where(half, _dot_tn(ds, qpair), 0.0)
                            dvg += jnp.where(half, _dot_tn(p.astype(BF16), do), 0.0)
                        dq_ref[:, cols] = dq
                    own = (lane < 64) if off == 0 else (lane >= 64)
                    dk_acc += jnp.where(own, dkg + pltpu.roll(dkg, 64, axis=1), 0.0)
                    dv_acc += jnp.where(own, dvg + pltpu.roll(dvg, 64, axis=1), 0.0)
                dk_pg.append(dk_acc)
                dv_pg.append(dv_acc)
            dsink_ref[...] += dsink
            for pg in range(2):
                cols = slice(128 * pg, 128 * pg + 128)
                dk0_ref[:, cols] += dk_pg[pg][0:BLK]
                dv0_ref[:, cols] += dv_pg[pg][0:BLK]
                dk_ref[:, cols] = kcarry[:, cols] + dk_pg[pg][BLK:2 * BLK]
                dv_ref[:, cols] = vcarry[:, cols] + dv_pg[pg][BLK:2 * BLK]
                kcarry[:, cols] = dk_pg[pg][2 * BLK:3 * BLK]
                vcarry[:, cols] = dv_pg[pg][2 * BLK:3 * BLK]

    last = nb - 1
    cur = lambda n: (jnp.minimum(n, last), 0)
    prev = lambda n: (jnp.clip(n - 1, 0, last), 0)
    zero = lambda n: (0, 0)
    kv = lambda f: pl.BlockSpec((BLK, KVW), f)
    wide = lambda f: pl.BlockSpec((BLK, AW), f)
    return _side_call(
        body, job, name=f"attn_bwd{l}", grid=(nb + 1,),
        in_specs=[pl.BlockSpec(memory_space=pltpu.SMEM),
                  wide(cur), kv(zero), kv(prev), kv(cur), kv(zero), kv(prev), kv(cur),
                  pl.BlockSpec((BLK, AW), lambda n: (jnp.minimum(n, last), 3)),
                  wide(cur), pl.BlockSpec((BLK, 128), cur), wide(cur), pl.BlockSpec(memory_space=pl.ANY)],
        out_specs=[wide(cur), kv(prev), kv(prev), kv(zero), kv(zero),
                   pl.BlockSpec((BLK, AW), lambda n: (jnp.minimum(n, last), 3)),
                   pl.BlockSpec((1, 128), zero)],
        out_shape=[jax.ShapeDtypeStruct((T, AW), F32), jax.ShapeDtypeStruct((T, KVW), F32),
                   jax.ShapeDtypeStruct((T, KVW), F32), jax.ShapeDtypeStruct((BLK, KVW), F32),
                   jax.ShapeDtypeStruct((BLK, KVW), F32), jax.ShapeDtypeStruct(dproj.shape, BF16),
                   jax.ShapeDtypeStruct((1, 128), F32)],
        scratch_shapes=[pltpu.VMEM((BLK, KVW), F32), pltpu.VMEM((BLK, KVW), F32)],
        semantics=("arbitrary",), aliases={12: 5},
        args=[sinks, qr, kr, kr, kr, vb, vb, vb, proj, att, lse, d_ya, dproj])


def _rope_bwd(dqr, dk, dv, dk0, dv0, tabs, dproj, l):
    T = dqr.shape[0]

    def body(dq_ref, dk_ref, dv_ref, dk0_ref, dv0_ref, c_ref, s1_ref, s2_ref, _, o_ref):
        n = pl.program_id(0)
        c, s1, s2 = c_ref[...], s1_ref[...], s2_ref[...]
        first = jnp.where(n == 0, 1.0, 0.0)
        for gcol in range(AW // 128):
            cols = slice(128 * gcol, 128 * gcol + 128)
            o_ref[:, cols] = (_rot_bwd(dq_ref[:, cols], c, s1, s2) * 0.125).astype(BF16)
        for gcol in range(KVW // 128):
            cols = slice(128 * gcol, 128 * gcol + 128)
            dkk = dk_ref[:, cols] + first * dk0_ref[:, cols]
            o_ref[:, AW + 128 * gcol:AW + 128 * gcol + 128] = _rot_bwd(dkk, c, s1, s2).astype(BF16)
            dvv = dv_ref[:, cols] + first * dv0_ref[:, cols]
            o_ref[:, AW + KVW + 128 * gcol:AW + KVW + 128 * gcol + 128] = dvv.astype(BF16)

    cur = lambda n: (n, 0)
    zero = lambda n: (0, 0)
    tab = pl.BlockSpec((BLK, 128), cur)
    return pl.pallas_call(
        body, name=f"rope_bwd{l}", grid=(T // BLK,),
        in_specs=[pl.BlockSpec((BLK, AW), cur), pl.BlockSpec((BLK, KVW), cur), pl.BlockSpec((BLK, KVW), cur),
                  pl.BlockSpec((BLK, KVW), zero), pl.BlockSpec((BLK, KVW), zero), tab, tab, tab,
                  pl.BlockSpec(memory_space=pl.ANY)],
        out_specs=pl.BlockSpec((BLK, AW + 2 * KVW), lambda n: (n, 1)),
        out_shape=jax.ShapeDtypeStruct(dproj.shape, BF16),
        input_output_aliases={8: 0},
        compiler_params=_cp("parallel"),
    )(dqr, dk, dv, dk0, dv0, *tabs, dproj)


def _block_diag(w):
    nl, nh, hd, _ = w.shape
    eye = jnp.eye(nh, dtype=w.dtype)
    return jnp.einsum("lhij,hg->lhigj", w, eye).reshape(nl, nh * hd, nh * hd)


def _diag_blocks(m):
    nh, hd = 8, 64
    return jnp.einsum("hihj->hij", m.reshape(nh, hd, nh, hd))


def _device_step(x, target, p, dist=None):
    vec = lambda a: a.reshape(DEPTH, 1, a.shape[-1])
    ln_in_g, ln_in_b = p["ln_in_g"].reshape(1, D), p["ln_in_b"].reshape(1, D)
    conv_dw_b, conv_ln_g, conv_ln_b, conv_pw_b = map(vec, (p["conv_dw_b"], p["conv_ln_g"], p["conv_ln_b"], p["conv_pw_b"]))
    lru_conv_b, lru_ba, lru_bx, lru_lambda = map(vec, (p["lru_conv_b"], p["lru_ba"], p["lru_bx"], p["lru_lambda"]))
    ln_post_g, ln_post_b = vec(p["ln_post_g"]), vec(p["ln_post_b"])
    wa_bd = _block_diag(p["lru_wa"]).astype(BF16)
    wx_bd = _block_diag(p["lru_wx"]).astype(BF16)
    w_in, w_out, pw_w = list(p["w_in"]), list(p["w_out"]), list(p["conv_pw_w"])
    sinks = p["attn_sinks"]
    big_names = ("w_in", "w_out", "conv_pw_w")

    (h, hb), got = _embed_fwd(x, p["meta_tokens"], ln_in_g, ln_in_b, job=_gather_job([w_in[0]]) if dist else None)
    if dist:
        w_in[0] = got[0]
    T = h.shape[0]
    tabs = _rope_tables(T)
    saved = []
    for l in range(DEPTH):
        (proj,), got = _proj_fwd(hb, w_in[l], l, job=_gather_job([w_out[0], pw_w[0]]) if dist and l == 0 else None)
        if got:
            w_out[0], pw_w[0] = got
        pw_l = pw_w[l].reshape(CW, CW)
        yc, conv = _conv_fwd(proj, p["conv_dw_w"], conv_dw_b, conv_ln_g, conv_ln_b, pw_l, conv_pw_b, l)
        qr, kr, vb = _rope_fwd(proj, tabs, l)
        (ya, att, lse), got = _attn_fwd(
            qr, kr, vb, proj, sinks, l, job=_gather_job([w_in[1], w_out[1], pw_w[1]]) if dist and l == 0 else None)
        if got:
            w_in[1], w_out[1], pw_w[1] = got
        yl, hl = _lru_fwd(proj, p["lru_conv_w"], lru_conv_b, wa_bd, lru_ba, wx_bd, lru_bx, lru_lambda, l)
        hn, hnb, xhat, rstd = _out_fwd(yc, ya, yl, w_out[l], h, ln_post_g, ln_post_b, l)
        saved.append((hb, proj, yc, conv, qr, kr, vb, ya, att, lse, yl, hl, xhat, rstd, pw_l))
        h, hb = hn, hnb

    loss_part, dh = _loss_head(h, target)
    g = {}
    later = None
    for l in reversed(range(DEPTH)):
        hb_l, proj, yc, conv, qr, kr, vb, ya, att, lse, yl, hl, xhat, rstd, pw_l = saved[l]
        dz, dzb, g["ln_post_g", l], g["ln_post_b", l] = _post_ln_bwd(dh, xhat, rstd, ln_post_g, l)
        (d_yc, d_ya, d_yl), recv = _dcat_bwd(dzb, w_out[l], l, job=_swap_job(later["grads"]) if later else None)
        if later:
            later["parts"], later["owns"] = _chip_partials(later["grads"], recv, dist, later["l"])
        g["w_out", l] = _dwout_bwd(yc, ya, yl, dzb, l)
        d_conv, dproj, dpw, g["conv_pw_b", l], g["conv_ln_g", l], g["conv_ln_b", l] = _conv_bwd_rows(
            conv, proj, d_yc, conv_ln_g, conv_ln_b, pw_l, conv_pw_b, l)
        g["conv_pw_w", l] = dpw.reshape(N_SHARD, 2, PW_SH // 2, CW)
        dproj, ddw, g["conv_dw_b", l] = _conv_bwd_taps(d_conv, proj, p["conv_dw_w"], dproj, l)
        g["conv_dw_w", l] = ddw[:CONV_K]
        (dqr, dk, dv, dk0, dv0, dproj, dsink), z = _attn_bwd(
            qr, kr, vb, proj, att, lse, d_ya, sinks, dproj, l, job=_scatter_job(later["parts"]) if later else None)
        if later:
            later["z"] = z
        g["attn_sinks", l] = dsink[0, :N_HEADS]
        dproj = _rope_bwd(dqr, dk, dv, dk0, dv0, tabs, dproj, l)
        dproj, dlw, g["lru_conv_b", l], dwa, g["lru_ba", l], dwx, g["lru_bx", l], g["lru_lambda", l] = _lru_bwd(
            proj, hl, d_yl, p["lru_conv_w"], lru_conv_b, wa_bd, lru_ba, wx_bd, lru_bx, lru_lambda, dproj, l)
        g["lru_conv_w", l] = dlw[:LRU_K]
        g["lru_wa", l] = _diag_blocks(dwa)
        g["lru_wx", l] = _diag_blocks(dwx)
        g["w_in", l] = _dwin_bwd(hb_l, dproj, l)
        dh = _dh_bwd(dproj, w_in[l], dz, l)
        if dist and later:
            _finish_reduce(later, dist, g)
            later = None
        if dist and l > 0:
            later = dict(l=l, grads=[g[name, l] for name in big_names])
    grad_x, g["meta_tokens", -1], g["ln_in_g", -1], g["ln_in_b", -1] = _embed_bwd(
        dh, x, p["meta_tokens"], ln_in_g, ln_in_b)
    if dist:
        first = dict(l=0, grads=[g[name, 0] for name in big_names])
        recv = _run_job(_swap_job(first["grads"]), "swap_halves0")
        first["parts"], first["owns"] = _chip_partials(first["grads"], recv, dist, 0)
        first["z"] = _run_job(_scatter_job(first["parts"]), "scatter_partials0")
        _finish_reduce(first, dist, g)
    return loss_part, grad_x, g


def _chip_partials(grads, recv, dist, l):
    ci, j, _ = dist
    outs = [_chip_partial(a, r, ci, j, f"{name}{l}") for name, a, r in zip(("w_in", "w_out", "conv_pw_w"), grads, recv)]
    return [o[0] for o in outs], [o[1] for o in outs]


def _finish_reduce(state, dist, g):
    names = ("w_in", "w_out", "conv_pw_w")
    l = state["l"]
    totals = [_shard_total(po, zz, dist[2], f"{name}{l}") for name, po, zz in zip(names, state["owns"], state["z"])]
    full = _run_job(_share_job(totals), f"share_halves{l}")
    for name, f in zip(names, full):
        g[name, l] = f.reshape(2 * f.shape[1], f.shape[2])


MESH = pl.DeviceIdType.MESH
HBM_SPEC = pl.BlockSpec(memory_space=pltpu.HBM)
N_DEV = 8


def _position():
    x, y, c = lax.axis_index("x"), lax.axis_index("y"), lax.axis_index("c")
    return x, y, c


def _other_chips(x, y):
    return [(1 - x, y), (x, 1 - y), (1 - x, 1 - y)]


def _cast_into_slot(a, l, j, tag):
    _, R, C = a.shape
    tb = _pick(R, (512, 128))

    def body(s_ref, a_ref, o_ref):
        o_ref[...] = a_ref[...].astype(BF16)

    grid_spec = pltpu.PrefetchScalarGridSpec(
        num_scalar_prefetch=1, grid=(R // tb,),
        in_specs=[pl.BlockSpec((None, tb, C), lambda t, sc: (l, t, 0))],
        out_specs=pl.BlockSpec((None, tb, C), lambda t, sc: (sc[0], t, 0)))
    return pl.pallas_call(
        body, name=f"cast_into_slot_{tag}{l}", grid_spec=grid_spec,
        out_shape=jax.ShapeDtypeStruct((N_SHARD, R, C), BF16),
        compiler_params=_cp("arbitrary"),
    )(jnp.reshape(j, (1,)).astype(jnp.int32), a)


class _Job:
    def __init__(self, inputs, aliased, extra_out, sems, start, mid, finish):
        self.inputs, self.aliased, self.extra_out, self.sems = list(inputs), aliased, list(extra_out), list(sems)
        self.start, self.mid, self.finish = start, mid, finish

    def out_shapes(self):
        own = [jax.ShapeDtypeStruct(a.shape, a.dtype) for a in self.inputs] if self.aliased else []
        return own + self.extra_out


def _side_call(body, job, *, name, grid, in_specs, out_specs, out_shape, scratch_shapes, semantics, args,
               aliases=None):
    aliases = dict(aliases or {})
    if job is None:
        outs = pl.pallas_call(
            body, name=name, grid=grid, in_specs=in_specs, out_specs=out_specs, out_shape=out_shape,
            scratch_shapes=scratch_shapes, input_output_aliases=aliases, compiler_params=_cp(*semantics))(*args)
        return list(outs), []
    n_in, n_out, n_scr = len(in_specs), len(out_specs), len(scratch_shapes)
    j_in, j_out = len(job.inputs), len(job.out_shapes())
    steps = 1
    for gsize in grid:
        steps *= gsize

    def wrapped(*refs):
        host_in, job_in = refs[:n_in], refs[n_in:n_in + j_in]
        o0 = n_in + j_in
        host_out, job_out = refs[o0:o0 + n_out], refs[o0 + n_out:o0 + n_out + j_out]
        s0 = o0 + n_out + j_out
        host_scr, sems = refs[s0:s0 + n_scr], refs[s0 + n_scr:]
        step = pl.program_id(0)
        for d in range(1, len(grid)):
            step = step * grid[d] + pl.program_id(d)

        @pl.when(step == 0)
        def _():
            job.start(job_in, job_out, sems)

        @pl.when(step == max(steps - 2, 0))
        def _():
            job.mid(job_in, job_out, sems)

        body(*host_in, *host_out, *host_scr)

        @pl.when(step == steps - 1)
        def _():
            job.finish(job_in, job_out, sems)

    if job.aliased:
        aliases.update({n_in + k: n_out + k for k in range(j_in)})
    outs = pl.pallas_call(
        wrapped, name=name, grid=grid,
        in_specs=list(in_specs) + [HBM_SPEC] * j_in, out_specs=list(out_specs) + [HBM_SPEC] * j_out,
        out_shape=list(out_shape) + job.out_shapes(),
        scratch_shapes=list(scratch_shapes) + job.sems, input_output_aliases=aliases,
        compiler_params=_cp(*(["arbitrary"] * len(grid))))(*args, *job.inputs)
    return list(outs[:n_out]), list(outs[n_out:])


def _run_job(job, name):
    return _side_call(lambda: None, job, name=name, grid=(1,), in_specs=[], out_specs=[], out_shape=[],
                      scratch_shapes=[], semantics=("arbitrary",), args=[])[1]


def _gather_job(slots):
    n = len(slots)

    def copies(buf, sems):
        ici_send, ici_recv, d2d_send, d2d_recv = sems
        x, y, c = _position()
        chips = _other_chips(x, y)

        def half(k, slot, which):
            hr = buf[k].shape[1] // 2
            return buf[k].at[slot, pl.ds(pl.multiple_of(which * hr, hr), hr)]

        def over_ici(k, p, slot):
            px, py = chips[p]
            return pltpu.make_async_remote_copy(
                src_ref=half(k, slot, c), dst_ref=half(k, slot, c),
                send_sem=ici_send.at[k * 3 + p], recv_sem=ici_recv.at[k * 3 + p],
                device_id=(px, py, c), device_id_type=MESH)

        def over_d2d(k, p, which):
            px, py = chips[p]
            return pltpu.make_async_remote_copy(
                src_ref=half(k, 2 * px + py, which), dst_ref=half(k, 2 * px + py, which),
                send_sem=d2d_send.at[k * 3 + p], recv_sem=d2d_recv.at[k * 3 + p],
                device_id=(x, y, 1 - c), device_id_type=MESH)

        return over_ici, over_d2d, 2 * x + y, chips, c

    pairs = [(k, p) for k in range(n) for p in range(3)]

    def start(_, buf, sems):
        over_ici, _, mine, _, _ = copies(buf, sems)
        for k, p in pairs:
            over_ici(k, p, mine).start()

    def mid(_, buf, sems):
        over_ici, over_d2d, _, chips, c = copies(buf, sems)
        for k, p in pairs:
            px, py = chips[p]
            over_ici(k, p, 2 * px + py).wait_recv()
            over_d2d(k, p, c).start()

    def finish(_, buf, sems):
        over_ici, over_d2d, mine, _, c = copies(buf, sems)
        for k, p in pairs:
            over_d2d(k, p, 1 - c).wait_recv()
        for k, p in pairs:
            over_ici(k, p, mine).wait_send()
            over_d2d(k, p, c).wait_send()

    return _Job(slots, True, [], [pltpu.SemaphoreType.DMA((3 * n,))] * 4, start, mid, finish)


def _gather_shards(shards):
    n = len(shards)

    def body(*refs):
        src, dst = refs[:n], refs[n:2 * n]
        send_sems, recv_sems, local_sems = refs[2 * n:]
        x, y, c = _position()
        mine = 2 * x + y
        chips = _other_chips(x, y)

        def copy(k, p):
            return pltpu.make_async_remote_copy(
                src_ref=src[k], dst_ref=dst[k].at[mine],
                send_sem=send_sems.at[k * 3 + p], recv_sem=recv_sems.at[k * 3 + p],
                device_id=(*chips[p], c), device_id_type=MESH)

        def arrival(k, p):
            px, py = chips[p]
            return pltpu.make_async_remote_copy(
                src_ref=src[k], dst_ref=dst[k].at[2 * px + py],
                send_sem=send_sems.at[k * 3 + p], recv_sem=recv_sems.at[k * 3 + p],
                device_id=(px, py, c), device_id_type=MESH)

        local = [pltpu.make_async_copy(src[k], dst[k].at[mine], local_sems.at[k]) for k in range(n)]
        for cp in local:
            cp.start()
        for k in range(n):
            for p in range(3):
                copy(k, p).start()
        for k in range(n):
            for p in range(3):
                arrival(k, p).wait_recv()
        for k in range(n):
            for p in range(3):
                copy(k, p).wait_send()
        for cp in local:
            cp.wait()

    return pl.pallas_call(
        body, name="gather_shards",
        in_specs=[HBM_SPEC] * n, out_specs=[HBM_SPEC] * n,
        out_shape=[jax.ShapeDtypeStruct((N_SHARD,) + s.shape, s.dtype) for s in shards],
        scratch_shapes=[pltpu.SemaphoreType.DMA((3 * n,)), pltpu.SemaphoreType.DMA((3 * n,)),
                        pltpu.SemaphoreType.DMA((n,))],
    )(*shards)


def _swap_job(grads):
    n = len(grads)

    def copies(src, dst, sems):
        x, y, c = _position()
        return [pltpu.make_async_remote_copy(
            src_ref=src[k].at[:, 1 - c], dst_ref=dst[k],
            send_sem=sems[0].at[k], recv_sem=sems[1].at[k],
            device_id=(x, y, 1 - c), device_id_type=MESH) for k in range(n)]

    def start(src, dst, sems):
        for cp in copies(src, dst, sems):
            cp.start()

    def finish(src, dst, sems):
        for cp in copies(src, dst, sems):
            cp.wait()

    return _Job(grads, False, [jax.ShapeDtypeStruct((N_SHARD,) + g.shape[2:], F32) for g in grads],
                [pltpu.SemaphoreType.DMA((n,))] * 2, start, lambda *_: None, finish)


def _chip_partial(a, y, c, j, tag):
    _, _, R, C = a.shape
    tr = _pick(R, (256, 64))

    def body(s_ref, a_ref, y_ref, pb_ref, po_ref):
        total = a_ref[...] + y_ref[...]
        pb_ref[...] = total.astype(BF16)

        @pl.when(pl.program_id(1) == s_ref[1])
        def _():
            po_ref[...] = total

    grid_spec = pltpu.PrefetchScalarGridSpec(
        num_scalar_prefetch=1, grid=(R // tr, N_SHARD),
        in_specs=[pl.BlockSpec((None, None, tr, C), lambda t, s, sc: (s, sc[0], t, 0)),
                  pl.BlockSpec((None, tr, C), lambda t, s, sc: (s, t, 0))],
        out_specs=[pl.BlockSpec((None, tr, C), lambda t, s, sc: (s, t, 0)),
                   pl.BlockSpec((tr, C), lambda t, s, sc: (t, 0))])
    return pl.pallas_call(
        body, name=f"chip_partial_{tag}", grid_spec=grid_spec,
        out_shape=[jax.ShapeDtypeStruct((N_SHARD, R, C), BF16), jax.ShapeDtypeStruct((R, C), F32)],
        compiler_params=_cp("arbitrary", "arbitrary"),
    )(jnp.stack([c, j]).astype(jnp.int32), a, y)


def _scatter_job(parts):
    n = len(parts)
    pairs = [(k, p) for k in range(n) for p in range(3)]

    def copy(src, dst, sems, k, p, outgoing):
        x, y, c = _position()
        mine = 2 * x + y
        px, py = _other_chips(x, y)[p]
        theirs = 2 * px + py
        return pltpu.make_async_remote_copy(
            src_ref=src[k].at[theirs if outgoing else mine], dst_ref=dst[k].at[mine if outgoing else theirs],
            send_sem=sems[0].at[k * 3 + p], recv_sem=sems[1].at[k * 3 + p],
            device_id=(px, py, c), device_id_type=MESH)

    def start(src, dst, sems):
        for k, p in pairs:
            copy(src, dst, sems, k, p, True).start()

    def finish(src, dst, sems):
        for k, p in pairs:
            copy(src, dst, sems, k, p, False).wait_recv()
        for k, p in pairs:
            copy(src, dst, sems, k, p, True).wait_send()

    return _Job(parts, False, [jax.ShapeDtypeStruct(pb.shape, BF16) for pb in parts],
                [pltpu.SemaphoreType.DMA((3 * n,))] * 2, start, lambda *_: None, finish)


def _shard_total(own, z, others_c, tag):
    R, C = own.shape
    tr = _pick(R, (256, 64))

    def body(s_ref, o_ref, z0_ref, z1_ref, z2_ref, h_ref):
        h_ref[...] = ((o_ref[...] + z0_ref[...].astype(F32)) + z1_ref[...].astype(F32)) + z2_ref[...].astype(F32)

    zspec = lambda q: pl.BlockSpec((None, tr, C), lambda t, sc: (sc[q], t, 0))
    grid_spec = pltpu.PrefetchScalarGridSpec(
        num_scalar_prefetch=1, grid=(R // tr,),
        in_specs=[pl.BlockSpec((tr, C), lambda t, sc: (t, 0)), zspec(0), zspec(1), zspec(2)],
        out_specs=pl.BlockSpec((None, tr, C), lambda t, sc: (sc[3], t, 0)))
    return pl.pallas_call(
        body, name=f"shard_total_{tag}", grid_spec=grid_spec,
        out_shape=jax.ShapeDtypeStruct((2, R, C), F32),
        compiler_params=_cp("arbitrary"),
    )(others_c, own, z, z, z)


def _share_job(totals):
    n = len(totals)

    def copy(buf, sems, k, which):
        x, y, c = _position()
        return pltpu.make_async_remote_copy(
            src_ref=buf[k].at[which], dst_ref=buf[k].at[which],
            send_sem=sems[0].at[k], recv_sem=sems[1].at[k],
            device_id=(x, y, 1 - c), device_id_type=MESH)

    def start(_, buf, sems):
        c = lax.axis_index("c")
        for k in range(n):
            copy(buf, sems, k, c).start()

    def finish(_, buf, sems):
        c = lax.axis_index("c")
        for k in range(n):
            copy(buf, sems, k, 1 - c).wait_recv()
        for k in range(n):
            copy(buf, sems, k, c).wait_send()

    return _Job(totals, True, [], [pltpu.SemaphoreType.DMA((n,))] * 2, start, lambda *_: None, finish)


def _allreduce_pack(pack):
    ns = pack.shape[0]

    def body(p_ref, o_ref, slots, send_sems, recv_sems):
        x, y, c = _position()
        me = 4 * x + 2 * y + c

        def copy(m):
            peer = (x ^ (m >> 2), y ^ ((m >> 1) & 1), c ^ (m & 1))
            return pltpu.make_async_remote_copy(
                src_ref=p_ref, dst_ref=slots.at[me], send_sem=send_sems.at[m - 1], recv_sem=recv_sems.at[m - 1],
                device_id=peer, device_id_type=MESH)

        def arrival(m):
            peer = (x ^ (m >> 2), y ^ ((m >> 1) & 1), c ^ (m & 1))
            return pltpu.make_async_remote_copy(
                src_ref=p_ref, dst_ref=slots.at[4 * peer[0] + 2 * peer[1] + peer[2]],
                send_sem=send_sems.at[m - 1], recv_sem=recv_sems.at[m - 1],
                device_id=peer, device_id_type=MESH)

        for m in range(1, N_DEV):
            copy(m).start()
        slots[me] = p_ref[...]
        for m in range(1, N_DEV):
            arrival(m).wait_recv()
        acc = slots[0]
        for d in range(1, N_DEV):
            acc = acc + slots[d]
        o_ref[...] = acc
        for m in range(1, N_DEV):
            copy(m).wait_send()

    vm = pl.BlockSpec(memory_space=pltpu.VMEM)
    return pl.pallas_call(
        body, name="allreduce_pack",
        in_specs=[vm], out_specs=vm,
        out_shape=jax.ShapeDtypeStruct(pack.shape, F32),
        scratch_shapes=[pltpu.VMEM((N_DEV, ns, 128), F32),
                        pltpu.SemaphoreType.DMA((N_DEV - 1,)), pltpu.SemaphoreType.DMA((N_DEV - 1,))],
        compiler_params=pltpu.CompilerParams(vmem_limit_bytes=V7X_VMEM_LIMIT),
    )(pack)


def _adamw_math(w, g, m, v):
    m = ADAM_B1 * m + (1.0 - ADAM_B1) * g
    v = ADAM_B2 * v + (1.0 - ADAM_B2) * (g * g)
    m_hat = m / (1.0 - ADAM_B1 ** ADAM_STEP)
    v_hat = v / (1.0 - ADAM_B2 ** ADAM_STEP)
    delta = -ADAM_LR * (m_hat / (jnp.sqrt(v_hat) + ADAM_EPS) + ADAM_WD * w)
    return delta, m, v


def _adamw_big(w, g0, g1, m, v, tag):
    _, R, C = w.shape
    tr = _pick(R, (256, 128))

    def body(w_ref, g0_ref, g1_ref, m_ref, v_ref, go_ref, d_ref, mo_ref, vo_ref):
        g = jnp.where(pl.program_id(0) == 0, g0_ref[...], g1_ref[...])
        delta, mn, vn = _adamw_math(w_ref[...], g, m_ref[...], v_ref[...])
        go_ref[...] = g
        d_ref[...] = delta
        mo_ref[...] = mn
        vo_ref[...] = vn

    s3 = pl.BlockSpec((None, tr, C), lambda l, t: (l, t, 0))
    s2 = pl.BlockSpec((tr, C), lambda l, t: (t, 0))
    shp = jax.ShapeDtypeStruct(w.shape, F32)
    return pl.pallas_call(
        body, name=f"adamw_{tag}", grid=(2, R // tr),
        in_specs=[s3, s2, s2, s3, s3], out_specs=[s3, s3, s3, s3],
        out_shape=[shp, shp, shp, shp],
        compiler_params=_cp("parallel", "parallel"),
    )(w, g0, g1, m, v)


def _adamw_small(ws, gs, ms, vs):
    n = len(ws)

    def body(*refs):
        w_r, g_r, m_r, v_r = refs[:n], refs[n:2 * n], refs[2 * n:3 * n], refs[3 * n:4 * n]
        d_o, m_o, v_o = refs[4 * n:5 * n], refs[5 * n:6 * n], refs[6 * n:7 * n]
        for k in range(n):
            delta, mn, vn = _adamw_math(w_r[k][...], g_r[k][...], m_r[k][...], v_r[k][...])
            d_o[k][...] = delta
            m_o[k][...] = mn
            v_o[k][...] = vn

    vm = pl.BlockSpec(memory_space=pltpu.VMEM)
    shapes = [jax.ShapeDtypeStruct(w.shape, F32) for w in ws]
    outs = pl.pallas_call(
        body, name="adamw_small",
        in_specs=[vm] * (4 * n), out_specs=[vm] * (3 * n),
        out_shape=shapes * 3,
    )(*ws, *gs, *ms, *vs)
    return outs[:n], outs[n:2 * n], outs[2 * n:]


_WEIGHTS = ["meta_tokens", "ln_in_g", "ln_in_b", "w_in", "conv_dw_w", "conv_dw_b", "conv_ln_g", "conv_ln_b",
            "conv_pw_w", "conv_pw_b", "attn_sinks", "lru_conv_w", "lru_conv_b", "lru_wa", "lru_ba", "lru_wx",
            "lru_bx", "lru_lambda", "w_out", "ln_post_g", "ln_post_b"]
_BIG = ("w_in", "w_out", "conv_pw_w")
_SMALL_SHARDED = {"meta_tokens": 1, "conv_dw_w": 2, "lru_conv_w": 2}
PACK_ROWS_ALIGN = 8


def _as2d(a):
    return a.reshape(1, -1) if a.ndim == 1 else a.reshape(-1, a.shape[-1])


def kernel(x, meta_tokens, ln_in_g, ln_in_b, w_in, conv_dw_w, conv_dw_b, conv_ln_g, conv_ln_b, conv_pw_w, conv_pw_b, attn_sinks, lru_conv_w, lru_conv_b, lru_wa, lru_ba, lru_wx, lru_bx, lru_lambda, w_out, ln_post_g, ln_post_b, loss_target, m_meta_tokens, m_ln_in_g, m_ln_in_b, m_w_in, m_conv_dw_w, m_conv_dw_b, m_conv_ln_g, m_conv_ln_b, m_conv_pw_w, m_conv_pw_b, m_attn_sinks, m_lru_conv_w, m_lru_conv_b, m_lru_wa, m_lru_ba, m_lru_wx, m_lru_bx, m_lru_lambda, m_w_out, m_ln_post_g, m_ln_post_b, v_meta_tokens, v_ln_in_g, v_ln_in_b, v_w_in, v_conv_dw_w, v_conv_dw_b, v_conv_ln_g, v_conv_ln_b, v_conv_pw_w, v_conv_pw_b, v_attn_sinks, v_lru_conv_w, v_lru_conv_b, v_lru_wa, v_lru_ba, v_lru_wx, v_lru_bx, v_lru_lambda, v_w_out, v_ln_post_g, v_ln_post_b):
    w = dict(meta_tokens=meta_tokens, ln_in_g=ln_in_g, ln_in_b=ln_in_b, w_in=w_in, conv_dw_w=conv_dw_w,
             conv_dw_b=conv_dw_b, conv_ln_g=conv_ln_g, conv_ln_b=conv_ln_b, conv_pw_w=conv_pw_w,
             conv_pw_b=conv_pw_b, attn_sinks=attn_sinks, lru_conv_w=lru_conv_w, lru_conv_b=lru_conv_b,
             lru_wa=lru_wa, lru_ba=lru_ba, lru_wx=lru_wx, lru_bx=lru_bx, lru_lambda=lru_lambda, w_out=w_out,
             ln_post_g=ln_post_g, ln_post_b=ln_post_b)
    mom_m = dict(zip(_WEIGHTS, (m_meta_tokens, m_ln_in_g, m_ln_in_b, m_w_in, m_conv_dw_w, m_conv_dw_b, m_conv_ln_g,
                                m_conv_ln_b, m_conv_pw_w, m_conv_pw_b, m_attn_sinks, m_lru_conv_w, m_lru_conv_b,
                                m_lru_wa, m_lru_ba, m_lru_wx, m_lru_bx, m_lru_lambda, m_w_out, m_ln_post_g,
                                m_ln_post_b)))
    mom_v = dict(zip(_WEIGHTS, (v_meta_tokens, v_ln_in_g, v_ln_in_b, v_w_in, v_conv_dw_w, v_conv_dw_b, v_conv_ln_g,
                                v_conv_ln_b, v_conv_pw_w, v_conv_pw_b, v_attn_sinks, v_lru_conv_w, v_lru_conv_b,
                                v_lru_wa, v_lru_ba, v_lru_wx, v_lru_bx, v_lru_lambda, v_w_out, v_ln_post_g,
                                v_ln_post_b)))
    xi, yi, ci = _position()
    j = 2 * xi + yi

    g_meta, g_dw, g_lc = _gather_shards([meta_tokens, conv_dw_w, lru_conv_w])
    p = dict(w)
    p["w_in"] = [_cast_into_slot(w_in, l, j, "w_in") for l in range(DEPTH)]
    p["w_out"] = [_cast_into_slot(w_out, l, j, "w_out") for l in range(DEPTH)]
    p["conv_pw_w"] = [_cast_into_slot(conv_pw_w, l, j, "conv_pw_w") for l in range(DEPTH)]
    p["meta_tokens"] = g_meta.transpose(1, 0, 2).reshape(N_META, D)
    p["conv_dw_w"] = g_dw.transpose(1, 2, 0, 3).reshape(DEPTH, CONV_K, CW)
    p["lru_conv_w"] = g_lc.transpose(1, 2, 0, 3).reshape(DEPTH, LRU_K, LW)

    others = jnp.stack([jnp.where(j <= 0, 1, 0), jnp.where(j <= 1, 2, 1), jnp.where(j <= 2, 3, 2), ci]).astype(jnp.int32)
    loss_part, grad_x, g = _device_step(x[0], loss_target[0], p, dist=(ci, j, others))
    loss = lax.psum(jnp.sum(loss_part), ("x", "y", "c"))
    big = {(name, l): g[name, l] for name in _BIG for l in range(DEPTH)}

    small_names = [n for n in _WEIGHTS if n not in _BIG]

    def full_grad(name):
        if (name, -1) in g:
            return g[name, -1]
        return jnp.stack([g[name, l] for l in range(DEPTH)], axis=0)

    flats = [full_grad(n).reshape(-1) for n in small_names]
    sizes = [f.shape[0] for f in flats]
    total = sum(sizes)
    rows = -(-total // 128)
    rows = -(-rows // PACK_ROWS_ALIGN) * PACK_ROWS_ALIGN
    pack = jnp.concatenate(flats + [jnp.zeros((rows * 128 - total,), F32)]).reshape(rows, 128)
    red = _allreduce_pack(pack).reshape(-1)
    small_g = {}
    off = 0
    for n, sz in zip(small_names, sizes):
        full = red[off:off + sz]
        off += sz
        if n in _SMALL_SHARDED:
            ax = _SMALL_SHARDED[n]
            fshape = list(w[n].shape)
            fshape[ax] *= N_SHARD
            full = full.reshape(fshape)
            small_g[n] = lax.dynamic_slice_in_dim(full, j * w[n].shape[ax], w[n].shape[ax], axis=ax)
        else:
            small_g[n] = full.reshape(w[n].shape)

    out_g, out_d, out_m, out_v = {}, {}, {}, {}
    for name in _BIG:
        shp = w[name].shape
        to3 = lambda a: a.reshape(DEPTH, -1, shp[-1])
        go, do, mo, vo = _adamw_big(to3(w[name]), big[name, 0], big[name, 1], to3(mom_m[name]), to3(mom_v[name]), name)
        out_g[name], out_d[name], out_m[name], out_v[name] = (a.reshape(shp) for a in (go, do, mo, vo))
    ds, ms, vs = _adamw_small([_as2d(w[n]) for n in small_names], [_as2d(small_g[n]) for n in small_names],
                              [_as2d(mom_m[n]) for n in small_names], [_as2d(mom_v[n]) for n in small_names])
    for n, d_, m_, v_ in zip(small_names, ds, ms, vs):
        out_g[n] = small_g[n]
        out_d[n], out_m[n], out_v[n] = d_.reshape(w[n].shape), m_.reshape(w[n].shape), v_.reshape(w[n].shape)

    return (loss, grad_x[None], *[out_g[n] for n in _WEIGHTS], *[out_d[n] for n in _WEIGHTS],
            *[out_m[n] for n in _WEIGHTS], *[out_v[n] for n in _WEIGHTS])
```

```python
import functools

import jax
import jax.numpy as jnp
from jax import lax
from jax.experimental import pallas as pl
from jax.experimental.pallas import tpu as pltpu

F32 = jnp.float32
BF16 = jnp.bfloat16

D = 2048
N_META = 16
CW = 512
CONV_K = 31
AW = 1024
KVW = 256
N_HEADS = 16
LW = 512
LRU_K = 4
LRU_C = 8.0
IN_TOTAL = 5120
ROT_HALF = 8
ROPE_THETA = 500000.0
LN_EPS = 1e-5
DEPTH = 2
ALPHA = (2.0 * DEPTH) ** 0.25
NEG_INF = -1e30
ADAM_LR, ADAM_B1, ADAM_B2, ADAM_EPS, ADAM_WD, ADAM_STEP = 0.001, 0.9, 0.999, 1e-08, 0.01, 10

BLK = 128
PAD = BLK - N_META
N_SHARD = 4
WIN_SH = IN_TOTAL // N_SHARD
WOUT_SH = D // N_SHARD
PW_SH = CW // N_SHARD
HALO = 32
LHALO = 8
V7X_VMEM_LIMIT = 60 * 1024 * 1024


def _cp(*sem):
    return pltpu.CompilerParams(dimension_semantics=sem if sem else None, vmem_limit_bytes=V7X_VMEM_LIMIT)


def _pick(total, prefs):
    for p in prefs:
        if total % p == 0:
            return p
    raise ValueError(f"no tile for {total}")


def _dot(a, b):
    return jnp.dot(a, b, preferred_element_type=F32)


def _dot_nt(a, b):
    return lax.dot_general(a, b, (((1,), (1,)), ((), ())), preferred_element_type=F32)


def _dot_tn(a, b):
    return lax.dot_general(a, b, (((0,), (0,)), ((), ())), preferred_element_type=F32)


def _sigmoid(x):
    return 1.0 / (1.0 + jnp.exp(-x))


def _silu_and_grad(x):
    s = _sigmoid(x)
    return x * s, s * (1.0 + x * (1.0 - s))


def _ln_rows(x, g, b):
    mu = jnp.mean(x, axis=-1, keepdims=True)
    xc = x - mu
    var = jnp.mean(xc * xc, axis=-1, keepdims=True)
    rstd = lax.rsqrt(var + LN_EPS)
    xhat = xc * rstd
    return xhat * g + b, xhat, rstd


def _ln_bwd_rows(dy, xhat, rstd, g):
    dxh = dy * g
    m1 = jnp.mean(dxh, axis=-1, keepdims=True)
    m2 = jnp.mean(dxh * xhat, axis=-1, keepdims=True)
    return rstd * (dxh - m1 - xhat * m2)


def _row_ids(n, base):
    return base + lax.broadcasted_iota(jnp.int32, (n, 1), 0)


def _colsum(x):
    return jnp.sum(x, axis=0, keepdims=True)


def _embed_fwd(x, meta, g, b, job=None):
    S = x.shape[0]
    nb = S // BLK + 1

    def body(x_ref, meta_ref, g_ref, b_ref, h_ref, hb_ref):
        n = pl.program_id(0)

        @pl.when(n == 0)
        def _():
            y, _, _ = _ln_rows(meta_ref[...], g_ref[...], b_ref[...])
            h_ref[...] = jnp.zeros_like(h_ref)
            h_ref[PAD:BLK, :] = y

        @pl.when(n > 0)
        def _():
            y, _, _ = _ln_rows(x_ref[...], g_ref[...], b_ref[...])
            h_ref[...] = y

        hb_ref[...] = h_ref[...].astype(BF16)

    return _side_call(
        body, job, name="embed_fwd", grid=(nb,),
        in_specs=[pl.BlockSpec((BLK, D), lambda n: (jnp.maximum(n - 1, 0), 0)),
                  pl.BlockSpec((N_META, D), lambda n: (0, 0)),
                  pl.BlockSpec((1, D), lambda n: (0, 0)),
                  pl.BlockSpec((1, D), lambda n: (0, 0))],
        out_specs=[pl.BlockSpec((BLK, D), lambda n: (n, 0)),
                   pl.BlockSpec((BLK, D), lambda n: (n, 0))],
        out_shape=[jax.ShapeDtypeStruct((nb * BLK, D), F32), jax.ShapeDtypeStruct((nb * BLK, D), BF16)],
        scratch_shapes=[], semantics=("arbitrary",), args=[x, meta, g, b])


def _embed_bwd(dh, x, meta, g, b):
    S = x.shape[0]
    nb = S // BLK + 1

    def body(dh_ref, x_ref, meta_ref, g_ref, b_ref, gx_ref, gm_ref, dg_ref, db_ref):
        n = pl.program_id(0)

        @pl.when(n == 0)
        def _():
            _, xhat, rstd = _ln_rows(meta_ref[...], g_ref[...], b_ref[...])
            dy = dh_ref[PAD:BLK, :]
            gm_ref[...] = _ln_bwd_rows(dy, xhat, rstd, g_ref[...])
            dg_ref[...] = _colsum(dy * xhat)
            db_ref[...] = _colsum(dy)

        @pl.when(n > 0)
        def _():
            _, xhat, rstd = _ln_rows(x_ref[...], g_ref[...], b_ref[...])
            dy = dh_ref[...]
            gx_ref[...] = _ln_bwd_rows(dy, xhat, rstd, g_ref[...])
            dg_ref[...] += _colsum(dy * xhat)
            db_ref[...] += _colsum(dy)

    prev = lambda n: (jnp.maximum(n - 1, 0), 0)
    const = lambda n: (0, 0)
    return pl.pallas_call(
        body, name="embed_bwd", grid=(nb,),
        in_specs=[pl.BlockSpec((BLK, D), lambda n: (n, 0)),
                  pl.BlockSpec((BLK, D), prev),
                  pl.BlockSpec((N_META, D), const),
                  pl.BlockSpec((1, D), const),
                  pl.BlockSpec((1, D), const)],
        out_specs=[pl.BlockSpec((BLK, D), prev),
                   pl.BlockSpec((N_META, D), const),
                   pl.BlockSpec((1, D), const),
                   pl.BlockSpec((1, D), const)],
        out_shape=[jax.ShapeDtypeStruct((S, D), F32), jax.ShapeDtypeStruct((N_META, D), F32),
                   jax.ShapeDtypeStruct((1, D), F32), jax.ShapeDtypeStruct((1, D), F32)],
        compiler_params=_cp("arbitrary"),
    )(dh, x, meta, g, b)


def _loss_head(h, target):
    T = h.shape[0]
    nb = T // BLK

    def body(h_ref, t_ref, part_ref, dy_ref):
        n = pl.program_id(0)

        @pl.when(n == 0)
        def _():
            part_ref[...] = jnp.zeros_like(part_ref)
            dy_ref[...] = jnp.zeros_like(dy_ref)

        @pl.when(n > 0)
        def _():
            err = h_ref[...] - t_ref[...]
            part_ref[...] += _colsum(err * err) * (0.5 / D)
            dy_ref[...] = err * (1.0 / D)

    return pl.pallas_call(
        body, name="loss_head", grid=(nb,),
        in_specs=[pl.BlockSpec((BLK, D), lambda n: (n, 0)),
                  pl.BlockSpec((BLK, D), lambda n: (jnp.maximum(n - 1, 0), 0))],
        out_specs=[pl.BlockSpec((1, D), lambda n: (0, 0)),
                   pl.BlockSpec((BLK, D), lambda n: (n, 0))],
        out_shape=[jax.ShapeDtypeStruct((1, D), F32), jax.ShapeDtypeStruct((T, D), F32)],
        compiler_params=_cp("arbitrary"),
    )(h, target)


def _proj_fwd(hb, w_in, l, job=None):
    T = hb.shape[0]
    tm = _pick(T, (1056, 384, 128))

    def body(a_ref, w_ref, o_ref):
        o_ref[...] = _dot(a_ref[...], w_ref[...])

    return _side_call(
        body, job, name=f"proj_fwd{l}", grid=(T // tm, N_SHARD),
        in_specs=[pl.BlockSpec((tm, D), lambda i, j: (i, 0)),
                  pl.BlockSpec((None, D, WIN_SH), lambda i, j: (j, 0, 0))],
        out_specs=[pl.BlockSpec((tm, WIN_SH), lambda i, j: (i, j))],
        out_shape=[jax.ShapeDtypeStruct((T, IN_TOTAL), F32)],
        scratch_shapes=[], semantics=("parallel", "arbitrary"), args=[hb, w_in])


def _out_fwd(yc, ya, yl, w_out, h, g, b, l):
    T = h.shape[0]
    tm = _pick(T, (384, 128))

    def body(yc_ref, ya_ref, yl_ref, w_ref, h_ref, g_ref, b_ref, hn_ref, hnb_ref, xh_ref, rs_ref):
        acc = _dot(yc_ref[...], w_ref[0])
        acc += _dot(ya_ref[:, 0:WOUT_SH], w_ref[1])
        acc += _dot(ya_ref[:, WOUT_SH:2 * WOUT_SH], w_ref[2])
        acc += _dot(yl_ref[...], w_ref[3])
        z = ALPHA * h_ref[...] + acc
        y, xhat, rstd = _ln_rows(z, g_ref[...], b_ref[...])
        hn_ref[...] = y
        hnb_ref[...] = y.astype(BF16)
        xh_ref[...] = xhat
        rs_ref[...] = rstd

    row = lambda i: (i, 0)
    return pl.pallas_call(
        body, name=f"out_fwd{l}", grid=(T // tm,),
        in_specs=[pl.BlockSpec((tm, CW), row), pl.BlockSpec((tm, AW), row), pl.BlockSpec((tm, LW), row),
                  pl.BlockSpec((N_SHARD, WOUT_SH, D), lambda i: (0, 0, 0)),
                  pl.BlockSpec((tm, D), row),
                  pl.BlockSpec((None, 1, D), lambda i: (l, 0, 0)),
                  pl.BlockSpec((None, 1, D), lambda i: (l, 0, 0))],
        out_specs=[pl.BlockSpec((tm, D), row), pl.BlockSpec((tm, D), row), pl.BlockSpec((tm, D), row),
                   pl.BlockSpec((tm, 1), row)],
        out_shape=[jax.ShapeDtypeStruct((T, D), F32), jax.ShapeDtypeStruct((T, D), BF16),
                   jax.ShapeDtypeStruct((T, D), F32), jax.ShapeDtypeStruct((T, 1), F32)],
        compiler_params=_cp("parallel"),
    )(yc, ya, yl, w_out, h, g, b)


def _post_ln_bwd(dhn, xhat, rstd, g, l):
    T = dhn.shape[0]
    tm = _pick(T, (384, 128))

    def body(d_ref, xh_ref, rs_ref, g_ref, dz_ref, dzb_ref, dg_ref, db_ref):
        @pl.when(pl.program_id(0) == 0)
        def _():
            dg_ref[...] = jnp.zeros_like(dg_ref)
            db_ref[...] = jnp.zeros_like(db_ref)

        dy = d_ref[...]
        xhat = xh_ref[...]
        dz = _ln_bwd_rows(dy, xhat, rs_ref[...], g_ref[...])
        dz_ref[...] = dz
        dzb_ref[...] = dz.astype(BF16)
        dg_ref[...] += _colsum(dy * xhat)
        db_ref[...] += _colsum(dy)

    row = lambda i: (i, 0)
    const = lambda i: (0, 0)
    return pl.pallas_call(
        body, name=f"post_ln_bwd{l}", grid=(T // tm,),
        in_specs=[pl.BlockSpec((tm, D), row), pl.BlockSpec((tm, D), row), pl.BlockSpec((tm, 1), row),
                  pl.BlockSpec((None, 1, D), lambda i: (l, 0, 0))],
        out_specs=[pl.BlockSpec((tm, D), row), pl.BlockSpec((tm, D), row),
                   pl.BlockSpec((1, D), const), pl.BlockSpec((1, D), const)],
        out_shape=[jax.ShapeDtypeStruct((T, D), F32), jax.ShapeDtypeStruct((T, D), BF16),
                   jax.ShapeDtypeStruct((1, D), F32), jax.ShapeDtypeStruct((1, D), F32)],
        compiler_params=_cp("arbitrary"),
    )(dhn, xhat, rstd, g)


def _dcat_bwd(dzb, w_out, l, job=None):
    T = dzb.shape[0]
    tm = _pick(T, (384, 128))

    def body(dz_ref, w_ref, dc_ref, da_ref, dl_ref):
        dz = dz_ref[...]
        dc_ref[...] = _dot_nt(dz, w_ref[0])
        da_ref[:, 0:WOUT_SH] = _dot_nt(dz, w_ref[1])
        da_ref[:, WOUT_SH:2 * WOUT_SH] = _dot_nt(dz, w_ref[2])
        dl_ref[...] = _dot_nt(dz, w_ref[3])

    row = lambda i: (i, 0)
    return _side_call(
        body, job, name=f"dcat_bwd{l}", grid=(T // tm,),
        in_specs=[pl.BlockSpec((tm, D), row),
                  pl.BlockSpec((N_SHARD, WOUT_SH, D), lambda i: (0, 0, 0))],
        out_specs=[pl.BlockSpec((tm, CW), row), pl.BlockSpec((tm, AW), row), pl.BlockSpec((tm, LW), row)],
        out_shape=[jax.ShapeDtypeStruct((T, CW), F32), jax.ShapeDtypeStruct((T, AW), F32),
                   jax.ShapeDtypeStruct((T, LW), F32)],
        scratch_shapes=[], semantics=("parallel",), args=[dzb, w_out])


def _dwout_bwd(yc, ya, yl, dzb, l):
    T = dzb.shape[0]
    tm = _pick(T, (1056, 384, 128))
    hr = WOUT_SH // 2
    nt = T // tm

    def body(yc_ref, ya_ref, yl_ref, dz_ref, o_ref):
        j = pl.program_id(0)
        t = pl.program_id(2)

        @pl.when(t == 0)
        def _():
            o_ref[...] = jnp.zeros_like(o_ref)

        dz = dz_ref[...]

        @pl.when(j == 0)
        def _():
            o_ref[...] += _dot_tn(yc_ref[...], dz)

        @pl.when((j == 1) | (j == 2))
        def _():
            o_ref[...] += _dot_tn(ya_ref[...], dz)

        @pl.when(j == 3)
        def _():
            o_ref[...] += _dot_tn(yl_ref[...], dz)

    return pl.pallas_call(
        body, name=f"dwout_bwd{l}", grid=(N_SHARD, 2, nt),
        in_specs=[pl.BlockSpec((tm, hr), lambda j, r, t: (t, r)),
                  pl.BlockSpec((tm, hr), lambda j, r, t: (t, 2 * jnp.clip(j - 1, 0, 1) + r)),
                  pl.BlockSpec((tm, hr), lambda j, r, t: (t, r)),
                  pl.BlockSpec((tm, D), lambda j, r, t: (t, 0))],
        out_specs=pl.BlockSpec((None, None, hr, D), lambda j, r, t: (j, r, 0, 0)),
        out_shape=jax.ShapeDtypeStruct((N_SHARD, 2, hr, D), F32),
        compiler_params=_cp("parallel", "parallel", "arbitrary"),
    )(yc, ya, yl, dzb)


def _dh_bwd(dproj, w_in, dz, l, job=None):
    T = dproj.shape[0]
    tm = _pick(T, (1056, 384, 128))

    def body(dp_ref, w_ref, dz_ref, o_ref, acc_ref):
        j = pl.program_id(1)

        @pl.when(j == 0)
        def _():
            acc_ref[...] = ALPHA * dz_ref[...]

        acc_ref[...] += _dot_nt(dp_ref[...], w_ref[...])

        @pl.when(j == N_SHARD - 1)
        def _():
            o_ref[...] = acc_ref[...]

    return _side_call(
        body, job, name=f"dh_bwd{l}", grid=(T // tm, N_SHARD),
        in_specs=[pl.BlockSpec((tm, WIN_SH), lambda i, j: (i, j)),
                  pl.BlockSpec((None, D, WIN_SH), lambda i, j: (j, 0, 0)),
                  pl.BlockSpec((tm, D), lambda i, j: (i, 0))],
        out_specs=[pl.BlockSpec((tm, D), lambda i, j: (i, 0))],
        out_shape=[jax.ShapeDtypeStruct((T, D), F32)],
        scratch_shapes=[pltpu.VMEM((tm, D), F32)],
        semantics=("parallel", "arbitrary"), args=[dproj, w_in, dz])


def _dwin_bwd(hb, dproj, l):
    T = hb.shape[0]
    tm = _pick(T, (1056, 384, 128))
    hr = D // 2

    def body(h_ref, dp_ref, o_ref):
        @pl.when(pl.program_id(2) == 0)
        def _():
            o_ref[...] = jnp.zeros_like(o_ref)

        o_ref[...] += _dot_tn(h_ref[...], dp_ref[...])

    return pl.pallas_call(
        body, name=f"dwin_bwd{l}", grid=(N_SHARD, 2, T // tm),
        in_specs=[pl.BlockSpec((tm, hr), lambda j, r, t: (t, r)),
                  pl.BlockSpec((tm, WIN_SH), lambda j, r, t: (t, j))],
        out_specs=pl.BlockSpec((None, None, hr, WIN_SH), lambda j, r, t: (j, r, 0, 0)),
        out_shape=jax.ShapeDtypeStruct((N_SHARD, 2, hr, WIN_SH), F32),
        compiler_params=_cp("parallel", "parallel", "arbitrary"),
    )(hb, dproj)


def _glu_masked(v, g, base_row):
    rows = _row_ids(v.shape[0], base_row)
    return jnp.where(rows >= PAD, v * _sigmoid(g), 0.0)


def _conv_tile(T):
    return _pick(T, (384, 128))


def _conv_fwd(proj, dw_w, dw_b, ln_g, ln_b, pw_w, pw_b, l):
    T = proj.shape[0]
    tm = _conv_tile(T)
    hb = tm // HALO

    def body(cv_ref, cg_ref, ct_ref, hv_ref, hg_ref, w_ref, b_ref, g_ref, be_ref, pw_ref, pb_ref,
             yc_ref, conv_ref, buf):
        i = pl.program_id(0)
        buf[0:HALO, :] = _glu_masked(hv_ref[...], hg_ref[...], i * tm - HALO)
        buf[HALO:HALO + tm, :] = _glu_masked(cv_ref[...], cg_ref[...], i * tm)
        acc = jnp.zeros((tm, CW), F32) + b_ref[...]
        for k in range(CONV_K):
            o = HALO - (CONV_K - 1) + k
            acc += w_ref[k:k + 1, :] * buf[o:o + tm, :]
        conv_ref[...] = acc
        u, _, _ = _ln_rows(acc, g_ref[...], be_ref[...])
        s = u * _sigmoid(u)
        cpw = _dot(s.astype(BF16), pw_ref[...]) + pb_ref[...]
        gate, _ = _silu_and_grad(ct_ref[...])
        yc_ref[...] = (cpw * gate).astype(BF16)

    vec = pl.BlockSpec((None, 1, CW), lambda i: (l, 0, 0))
    return pl.pallas_call(
        body, name=f"conv_fwd{l}", grid=(T // tm,),
        in_specs=[pl.BlockSpec((tm, CW), lambda i: (i, 0)),
                  pl.BlockSpec((tm, CW), lambda i: (i, 1)),
                  pl.BlockSpec((tm, CW), lambda i: (i, 2)),
                  pl.BlockSpec((HALO, CW), lambda i: (jnp.maximum(i * hb - 1, 0), 0)),
                  pl.BlockSpec((HALO, CW), lambda i: (jnp.maximum(i * hb - 1, 0), 1)),
                  pl.BlockSpec((None, CONV_K, CW), lambda i: (l, 0, 0)),
                  vec, vec, vec,
                  pl.BlockSpec((CW, CW), lambda i: (0, 0)),
                  vec],
        out_specs=[pl.BlockSpec((tm, CW), lambda i: (i, 0)), pl.BlockSpec((tm, CW), lambda i: (i, 0))],
        out_shape=[jax.ShapeDtypeStruct((T, CW), BF16), jax.ShapeDtypeStruct((T, CW), F32)],
        scratch_shapes=[pltpu.VMEM((tm + HALO, CW), F32)],
        compiler_params=_cp("parallel"),
    )(proj, proj, proj, proj, proj, dw_w, dw_b, ln_g, ln_b, pw_w, pw_b)


def _conv_bwd_rows(conv, proj, d_yc, ln_g, ln_b, pw_w, pw_b, l):
    T = conv.shape[0]
    tm = _conv_tile(T)

    def body(conv_ref, ct_ref, dy_ref, g_ref, be_ref, pw_ref, pb_ref,
             dconv_ref, dct_ref, dpw_ref, dpb_ref, dg_ref, db_ref):
        @pl.when(pl.program_id(0) == 0)
        def _():
            dpw_ref[...] = jnp.zeros_like(dpw_ref)
            dpb_ref[...] = jnp.zeros_like(dpb_ref)
            dg_ref[...] = jnp.zeros_like(dg_ref)
            db_ref[...] = jnp.zeros_like(db_ref)

        u, xhat, rstd = _ln_rows(conv_ref[...], g_ref[...], be_ref[...])
        s, ds_du = _silu_and_grad(u)
        sb = s.astype(BF16)
        cpw = _dot(sb, pw_ref[...]) + pb_ref[...]
        gate, dgate = _silu_and_grad(ct_ref[...])
        dy = dy_ref[...]
        d_cpw = dy * gate
        dct_ref[...] = (dy * cpw * dgate).astype(BF16)
        d_cpw_b = d_cpw.astype(BF16)
        dpb_ref[...] += _colsum(d_cpw)
        dpw_ref[...] += _dot_tn(sb, d_cpw_b)
        du = _dot_nt(d_cpw_b, pw_ref[...]) * ds_du
        dconv_ref[...] = _ln_bwd_rows(du, xhat, rstd, g_ref[...])
        dg_ref[...] += _colsum(du * xhat)
        db_ref[...] += _colsum(du)

    vec = pl.BlockSpec((None, 1, CW), lambda i: (l, 0, 0))
    row = lambda i: (i, 0)
    const = lambda i: (0, 0)
    return pl.pallas_call(
        body, name=f"conv_bwd_rows{l}", grid=(T // tm,),
        in_specs=[pl.BlockSpec((tm, CW), row), pl.BlockSpec((tm, CW), lambda i: (i, 2)),
                  pl.BlockSpec((tm, CW), row), vec, vec,
                  pl.BlockSpec((CW, CW), lambda i: (0, 0)), vec],
        out_specs=[pl.BlockSpec((tm, CW), row), pl.BlockSpec((tm, CW), lambda i: (i, 2)),
                   pl.BlockSpec((CW, CW), const), pl.BlockSpec((1, CW), const),
                   pl.BlockSpec((1, CW), const), pl.BlockSpec((1, CW), const)],
        out_shape=[jax.ShapeDtypeStruct((T, CW), F32), jax.ShapeDtypeStruct((T, IN_TOTAL), BF16),
                   jax.ShapeDtypeStruct((CW, CW), F32), jax.ShapeDtypeStruct((1, CW), F32),
                   jax.ShapeDtypeStruct((1, CW), F32), jax.ShapeDtypeStruct((1, CW), F32)],
        compiler_params=_cp("arbitrary"),
    )(conv, proj, d_yc, ln_g, ln_b, pw_w, pw_b)


def _conv_bwd_taps(d_conv, proj, dw_w, dproj, l, job=None):
    T = d_conv.shape[0]
    tm = _conv_tile(T)
    hb = tm // HALO
    nt = T // tm
    last_halo = T // HALO - 1

    def body(dc_ref, dh_ref, cv_ref, cg_ref, hv_ref, hg_ref, w_ref, _, o_ref, dw_ref, dwb_ref, cbuf, dbuf):
        i = pl.program_id(0)

        @pl.when(i == 0)
        def _():
            dw_ref[...] = jnp.zeros_like(dw_ref)
            dwb_ref[...] = jnp.zeros_like(dwb_ref)

        cbuf[0:HALO, :] = _glu_masked(hv_ref[...], hg_ref[...], i * tm - HALO)
        cbuf[HALO:HALO + tm, :] = _glu_masked(cv_ref[...], cg_ref[...], i * tm)
        dmain = dc_ref[...]
        dbuf[0:tm, :] = dmain
        dbuf[tm:tm + HALO, :] = jnp.where(i < nt - 1, dh_ref[...], 0.0)
        acc = jnp.zeros((tm, CW), F32)
        for k in range(CONV_K):
            o = CONV_K - 1 - k
            acc += w_ref[k:k + 1, :] * dbuf[o:o + tm, :]
            oc = HALO - (CONV_K - 1) + k
            dw_ref[k:k + 1, :] += _colsum(dmain * cbuf[oc:oc + tm, :])
        dwb_ref[...] += _colsum(dmain)
        d_c = jnp.where(_row_ids(tm, i * tm) >= PAD, acc, 0.0)
        sig = _sigmoid(cg_ref[...])
        o_ref[:, 0:CW] = (d_c * sig).astype(BF16)
        o_ref[:, CW:2 * CW] = (d_c * cv_ref[...] * sig * (1.0 - sig)).astype(BF16)

    const = lambda i: (0, 0)
    return _side_call(
        body, job, name=f"conv_bwd_taps{l}", grid=(nt,),
        in_specs=[pl.BlockSpec((tm, CW), lambda i: (i, 0)),
                  pl.BlockSpec((HALO, CW), lambda i: (jnp.minimum((i + 1) * hb, last_halo), 0)),
                  pl.BlockSpec((tm, CW), lambda i: (i, 0)),
                  pl.BlockSpec((tm, CW), lambda i: (i, 1)),
                  pl.BlockSpec((HALO, CW), lambda i: (jnp.maximum(i * hb - 1, 0), 0)),
                  pl.BlockSpec((HALO, CW), lambda i: (jnp.maximum(i * hb - 1, 0), 1)),
                  pl.BlockSpec((None, CONV_K, CW), lambda i: (l, 0, 0)),
                  pl.BlockSpec(memory_space=pl.ANY)],
        out_specs=[pl.BlockSpec((tm, 2 * CW), lambda i: (i, 0)),
                   pl.BlockSpec((HALO, CW), const), pl.BlockSpec((1, CW), const)],
        out_shape=[jax.ShapeDtypeStruct(dproj.shape, BF16), jax.ShapeDtypeStruct((HALO, CW), F32),
                   jax.ShapeDtypeStruct((1, CW), F32)],
        scratch_shapes=[pltpu.VMEM((tm + HALO, CW), F32), pltpu.VMEM((tm + HALO, CW), F32)],
        semantics=("arbitrary",), aliases={7: 0},
        args=[d_conv, d_conv, proj, proj, proj, proj, dw_w, dproj])


def _log1p_small(e):
    return jnp.where(e < 1e-3, e * (1.0 - e * (0.5 - e * (1.0 / 3.0))), jnp.log(1.0 + e))


def _softplus(z):
    return jnp.maximum(z, 0.0) + _log1p_small(jnp.exp(-jnp.abs(z)))


def _neg_expm1(x):
    series = -x * (1.0 + x * (1.0 / 2.0) * (1.0 + x * (1.0 / 3.0) * (1.0 + x * (1.0 / 4.0) * (
        1.0 + x * (1.0 / 5.0) * (1.0 + x * (1.0 / 6.0) * (1.0 + x * (1.0 / 7.0)))))))
    return jnp.where(x > -0.25, series, 1.0 - jnp.exp(x))


def _lru_gates(rxbuf, tm, base_row, lw_ref, lb_ref, wa_ref, ba_ref, wx_ref, bx_ref, lam_ref):
    rc = jnp.zeros((tm, LW), F32) + lb_ref[...]
    for k in range(LRU_K):
        o = LHALO - (LRU_K - 1) + k
        rc += lw_ref[k:k + 1, :] * rxbuf[o:o + tm, :]
    rcb = rc.astype(BF16)
    r = _sigmoid(_dot(rcb, wa_ref[...]) + ba_ref[...])
    ig = _sigmoid(_dot(rcb, wx_ref[...]) + bx_ref[...])
    sp = _softplus(-lam_ref[...])
    la = -LRU_C * r * sp
    a = jnp.exp(la)
    mult = jnp.sqrt(_neg_expm1(2.0 * la))
    valid = _row_ids(tm, base_row) >= PAD
    return rc, rcb, r, ig, sp, a, mult, valid


def _mask_rows(v, base_row):
    return jnp.where(_row_ids(v.shape[0], base_row) >= PAD, v, 0.0)


def _scan_steps(tm):
    s, out = 1, []
    while s < tm:
        out.append(s)
        s *= 2
    return out


def _lru_tile(T):
    return _pick(T, (384, 128))


def _lru_fwd(proj, lw, lb, wa, ba, wx, bx, lam, l):
    T = proj.shape[0]
    tm = _lru_tile(T)
    hb = tm // LHALO

    def body(rx_ref, rg_ref, hx_ref, lw_ref, lb_ref, wa_ref, ba_ref, wx_ref, bx_ref, lam_ref,
             yl_ref, hl_ref, rxbuf, carry):
        i = pl.program_id(0)

        @pl.when(i == 0)
        def _():
            carry[...] = jnp.zeros_like(carry)

        rxbuf[0:LHALO, :] = _mask_rows(hx_ref[...], i * tm - LHALO)
        rxbuf[LHALO:LHALO + tm, :] = _mask_rows(rx_ref[...], i * tm)
        rc, _, _, ig, _, a, mult, valid = _lru_gates(rxbuf, tm, i * tm, lw_ref, lb_ref, wa_ref, ba_ref,
                                                     wx_ref, bx_ref, lam_ref)
        bb = jnp.where(valid, mult * (ig * rc), 0.0)
        aa = a
        rows = _row_ids(tm, 0)
        for s in _scan_steps(tm):
            keep = rows >= s
            a_s = jnp.where(keep, pltpu.roll(aa, s, axis=0), 1.0)
            b_s = jnp.where(keep, pltpu.roll(bb, s, axis=0), 0.0)
            bb = aa * b_s + bb
            aa = aa * a_s
        h = bb + aa * carry[0:1, :]
        hl_ref[...] = h
        carry[0:1, :] = hl_ref[tm - 1:tm, :]
        gate, _ = _silu_and_grad(rg_ref[...])
        yl_ref[...] = (h * gate).astype(BF16)

    vec = pl.BlockSpec((None, 1, LW), lambda i: (l, 0, 0))
    mat = pl.BlockSpec((None, LW, LW), lambda i: (l, 0, 0))
    return pl.pallas_call(
        body, name=f"lru_fwd{l}", grid=(T // tm,),
        in_specs=[pl.BlockSpec((tm, LW), lambda i: (i, 8)),
                  pl.BlockSpec((tm, LW), lambda i: (i, 9)),
                  pl.BlockSpec((LHALO, LW), lambda i: (jnp.maximum(i * hb - 1, 0), 8)),
                  pl.BlockSpec((None, LRU_K, LW), lambda i: (l, 0, 0)),
                  vec, mat, vec, mat, vec, vec],
        out_specs=[pl.BlockSpec((tm, LW), lambda i: (i, 0)), pl.BlockSpec((tm, LW), lambda i: (i, 0))],
        out_shape=[jax.ShapeDtypeStruct((T, LW), BF16), jax.ShapeDtypeStruct((T, LW), F32)],
        scratch_shapes=[pltpu.VMEM((tm + LHALO, LW), F32), pltpu.VMEM((8, LW), F32)],
        compiler_params=_cp("arbitrary"),
    )(proj, proj, proj, lw, lb, wa, ba, wx, bx, lam)


def _lru_bwd(proj, hl, d_yl, lw, lb, wa, ba, wx, bx, lam, dproj, l, job=None):
    T = proj.shape[0]
    tm = _lru_tile(T)
    hb = tm // LHALO
    nt = T // tm

    def body(rx_ref, rg_ref, hx_ref, hl_ref, hh_ref, dy_ref, lw_ref, lb_ref, wa_ref, ba_ref, wx_ref, bx_ref,
             lam_ref, _, o_ref, dlw_ref, dlb_ref, dwa_ref, dba_ref, dwx_ref, dbx_ref, dlam_ref,
             rxbuf, dbuf, carry, head):
        step = pl.program_id(0)
        i = nt - 1 - step

        @pl.when(step == 0)
        def _():
            carry[...] = jnp.zeros_like(carry)
            head[...] = jnp.zeros_like(head)
            for ref in (dlw_ref, dlb_ref, dwa_ref, dba_ref, dwx_ref, dbx_ref, dlam_ref):
                ref[...] = jnp.zeros_like(ref)

        rxbuf[0:LHALO, :] = _mask_rows(hx_ref[...], i * tm - LHALO)
        rxbuf[LHALO:LHALO + tm, :] = _mask_rows(rx_ref[...], i * tm)
        rc, rcb, r, ig, sp, a, mult, valid = _lru_gates(rxbuf, tm, i * tm, lw_ref, lb_ref, wa_ref, ba_ref,
                                                        wx_ref, bx_ref, lam_ref)
        rows = _row_ids(tm, 0)
        h = hl_ref[...]
        h_before = jnp.where(i > 0, hh_ref[LHALO - 1:LHALO, :], 0.0)
        hprev = jnp.where(rows == 0, h_before, pltpu.roll(h, 1, axis=0))
        rg = rg_ref[...]
        gate, dgate = _silu_and_grad(rg)
        dy = dy_ref[...]
        o_ref[:, LW:2 * LW] = (dy * h * dgate).astype(BF16)
        bb = dy * gate + jnp.where(rows == tm - 1, carry[0:1, :], 0.0)
        aa = jnp.where(rows == tm - 1, 0.0, pltpu.roll(a, tm - 1, axis=0))
        for s in _scan_steps(tm):
            keep = rows < tm - s
            a_s = jnp.where(keep, pltpu.roll(aa, tm - s, axis=0), 1.0)
            b_s = jnp.where(keep, pltpu.roll(bb, tm - s, axis=0), 0.0)
            bb = aa * b_s + bb
            aa = aa * a_s
        g = bb
        dbuf[0:tm, :] = a * g
        carry[0:1, :] = dbuf[0:1, :]
        du = jnp.where(valid, g, 0.0)
        da = g * hprev
        dix = du * mult
        dmult = du * (ig * rc)
        dla = jnp.where(valid, da * a - dmult * (a * a) / mult, 0.0)
        dr = dla * (-LRU_C * sp)
        dlam_ref[...] += _colsum(dla * (LRU_C * r)) * _sigmoid(-lam_ref[...])
        dpa = dr * r * (1.0 - r)
        dpx = (dix * rc) * ig * (1.0 - ig)
        dpab = dpa.astype(BF16)
        dpxb = dpx.astype(BF16)
        dba_ref[...] += _colsum(dpa)
        dbx_ref[...] += _colsum(dpx)
        dwa_ref[...] += _dot_tn(rcb, dpab)
        dwx_ref[...] += _dot_tn(rcb, dpxb)
        drc = dix * ig + _dot_nt(dpab, wa_ref[...]) + _dot_nt(dpxb, wx_ref[...])
        dbuf[0:tm, :] = drc
        dbuf[tm:tm + LHALO, :] = head[...]
        acc = jnp.zeros((tm, LW), F32)
        for k in range(LRU_K):
            o = LRU_K - 1 - k
            acc += lw_ref[k:k + 1, :] * dbuf[o:o + tm, :]
            oc = LHALO - (LRU_K - 1) + k
            dlw_ref[k:k + 1, :] += _colsum(drc * rxbuf[oc:oc + tm, :])
        dlb_ref[...] += _colsum(drc)
        head[...] = dbuf[0:LHALO, :]
        o_ref[:, 0:LW] = jnp.where(valid, acc, 0.0).astype(BF16)

    rev = lambda s: nt - 1 - s
    vec = pl.BlockSpec((None, 1, LW), lambda s: (l, 0, 0))
    mat = pl.BlockSpec((None, LW, LW), lambda s: (l, 0, 0))
    const = lambda s: (0, 0)
    halo = lambda s: jnp.maximum(rev(s) * hb - 1, 0)
    return _side_call(
        body, job, name=f"lru_bwd{l}", grid=(nt,),
        in_specs=[pl.BlockSpec((tm, LW), lambda s: (rev(s), 8)),
                  pl.BlockSpec((tm, LW), lambda s: (rev(s), 9)),
                  pl.BlockSpec((LHALO, LW), lambda s: (halo(s), 8)),
                  pl.BlockSpec((tm, LW), lambda s: (rev(s), 0)),
                  pl.BlockSpec((LHALO, LW), lambda s: (halo(s), 0)),
                  pl.BlockSpec((tm, LW), lambda s: (rev(s), 0)),
                  pl.BlockSpec((None, LRU_K, LW), lambda s: (l, 0, 0)),
                  vec, mat, vec, mat, vec, vec, pl.BlockSpec(memory_space=pl.ANY)],
        out_specs=[pl.BlockSpec((tm, 2 * LW), lambda s: (rev(s), 4)),
                   pl.BlockSpec((8, LW), const), pl.BlockSpec((1, LW), const),
                   pl.BlockSpec((LW, LW), const), pl.BlockSpec((1, LW), const),
                   pl.BlockSpec((LW, LW), const), pl.BlockSpec((1, LW), const),
                   pl.BlockSpec((1, LW), const)],
        out_shape=[jax.ShapeDtypeStruct(dproj.shape, BF16),
                   jax.ShapeDtypeStruct((8, LW), F32), jax.ShapeDtypeStruct((1, LW), F32),
                   jax.ShapeDtypeStruct((LW, LW), F32), jax.ShapeDtypeStruct((1, LW), F32),
                   jax.ShapeDtypeStruct((LW, LW), F32), jax.ShapeDtypeStruct((1, LW), F32),
                   jax.ShapeDtypeStruct((1, LW), F32)],
        scratch_shapes=[pltpu.VMEM((tm + LHALO, LW), F32), pltpu.VMEM((tm + LHALO, LW), F32),
                        pltpu.VMEM((8, LW), F32), pltpu.VMEM((LHALO, LW), F32)],
        semantics=("arbitrary",), aliases={13: 0},
        args=[proj, proj, proj, hl, hl, d_yl, lw, lb, wa, ba, wx, bx, lam, dproj])


def _rope_tables(T):
    pos = (lax.broadcasted_iota(jnp.int32, (T, 128), 0) - PAD).astype(F32)
    lane = lax.broadcasted_iota(jnp.int32, (T, 128), 1) % 64
    inv_freq = ROPE_THETA ** (-(lane % ROT_HALF).astype(F32) / ROT_HALF)
    ang = pos * inv_freq
    cos, sin = jnp.cos(ang), jnp.sin(ang)
    c = jnp.where(lane < 2 * ROT_HALF, cos, 1.0)
    s1 = jnp.where(lane < ROT_HALF, -sin, 0.0)
    s2 = jnp.where((lane >= ROT_HALF) & (lane < 2 * ROT_HALF), sin, 0.0)
    return c, s1, s2


def _rot_fwd(x, c, s1, s2):
    return x * c + pltpu.roll(x, 128 - ROT_HALF, axis=1) * s1 + pltpu.roll(x, ROT_HALF, axis=1) * s2


def _rot_bwd(dy, c, s1, s2):
    return dy * c + pltpu.roll(dy * s1, ROT_HALF, axis=1) + pltpu.roll(dy * s2, 128 - ROT_HALF, axis=1)


def _rope_fwd(proj, tabs, l):
    T = proj.shape[0]

    def body(ql_ref, qh_ref, k_ref, v_ref, c_ref, s1_ref, s2_ref, qr_ref, kr_ref, vb_ref):
        c, s1, s2 = c_ref[...], s1_ref[...], s2_ref[...]
        for gcol in range(AW // 128):
            src = ql_ref if gcol < 4 else qh_ref
            x = src[:, 128 * (gcol % 4):128 * (gcol % 4) + 128]
            qr_ref[:, 128 * gcol:128 * gcol + 128] = (_rot_fwd(x, c, s1, s2) * 0.125).astype(BF16)
        for gcol in range(KVW // 128):
            x = k_ref[:, 128 * gcol:128 * gcol + 128]
            kr_ref[:, 128 * gcol:128 * gcol + 128] = _rot_fwd(x, c, s1, s2).astype(BF16)
        vb_ref[...] = v_ref[...].astype(BF16)

    tab = pl.BlockSpec((BLK, 128), lambda n: (n, 0))
    return pl.pallas_call(
        body, name=f"rope_fwd{l}", grid=(T // BLK,),
        in_specs=[pl.BlockSpec((BLK, 512), lambda n: (n, 3)), pl.BlockSpec((BLK, 512), lambda n: (n, 4)),
                  pl.BlockSpec((BLK, KVW), lambda n: (n, 10)), pl.BlockSpec((BLK, KVW), lambda n: (n, 11)),
                  tab, tab, tab],
        out_specs=[pl.BlockSpec((BLK, AW), lambda n: (n, 0)), pl.BlockSpec((BLK, KVW), lambda n: (n, 0)),
                   pl.BlockSpec((BLK, KVW), lambda n: (n, 0))],
        out_shape=[jax.ShapeDtypeStruct((T, AW), BF16), jax.ShapeDtypeStruct((T, KVW), BF16),
                   jax.ShapeDtypeStruct((T, KVW), BF16)],
        compiler_params=_cp("parallel"),
    )(proj, proj, proj, proj, *tabs)


def _attn_mask(n):
    qi = lax.broadcasted_iota(jnp.int32, (BLK, BLK), 0)
    kj = lax.broadcasted_iota(jnp.int32, (BLK, BLK), 1)
    m0 = (kj >= PAD) & (n >= 1)
    mp = (kj > qi) & (n >= 2)
    mc = (kj <= qi) & ((n >= 1) | (kj >= PAD))
    return jnp.concatenate([m0, mp, mc], axis=1)


def _kv_halves(x0_ref, xp_ref, xc_ref, g):
    pg, off = g // 2, g % 2
    cols = slice(128 * pg, 128 * pg + 128)
    x = jnp.concatenate([x0_ref[:, cols], xp_ref[:, cols], xc_ref[:, cols]], axis=0).astype(F32)
    lane = lax.broadcasted_iota(jnp.int32, (1, 128), 1)
    if off == 0:
        lo = jnp.where(lane < 64, x, 0.0)
        hi = pltpu.roll(lo, 64, axis=1)
    else:
        hi = jnp.where(lane >= 64, x, 0.0)
        lo = pltpu.roll(hi, 64, axis=1)
    return lo.astype(BF16), hi.astype(BF16)


def _attn_fwd(qr, kr, vb, proj, sinks, l, job=None):
    T = qr.shape[0]

    def body(sink_ref, q_ref, k0_ref, kp_ref, kc_ref, v0_ref, vp_ref, vc_ref, ag_ref, ya_ref, att_ref, lse_ref):
        n = pl.program_id(0)
        mask = _attn_mask(n)
        lane = lax.broadcasted_iota(jnp.int32, (1, 128), 1)
        lse_acc = jnp.zeros((BLK, 128), F32)
        for g in range(4):
            k_lo, k_hi = _kv_halves(k0_ref, kp_ref, kc_ref, g)
            v_lo, v_hi = _kv_halves(v0_ref, vp_ref, vc_ref, g)
            for pp in range(2):
                cols = slice(128 * (2 * g + pp), 128 * (2 * g + pp) + 128)
                qpair = q_ref[:, cols]
                out = jnp.zeros((BLK, 128), F32)
                for hh, (kx, vx) in enumerate(((k_lo, v_lo), (k_hi, v_hi))):
                    h = 4 * g + 2 * pp + hh
                    sink = sink_ref[l, h]
                    s = jnp.where(mask, _dot_nt(qpair, kx), NEG_INF)
                    m = jnp.maximum(jnp.max(s, axis=1, keepdims=True), sink)
                    p = jnp.exp(s - m)
                    denom = jnp.sum(p, axis=1, keepdims=True) + jnp.exp(sink - m)
                    out += _dot((p / denom).astype(BF16), vx)
                    lse_acc = jnp.where(lane == h, m + jnp.log(denom), lse_acc)
                att_ref[:, cols] = out
                gate, _ = _silu_and_grad(ag_ref[:, cols])
                ya_ref[:, cols] = (out * gate).astype(BF16)
        lse_ref[...] = lse_acc

    prev = lambda n: (jnp.maximum(n - 1, 0), 0)
    cur = lambda n: (n, 0)
    zero = lambda n: (0, 0)
    kv = lambda f: pl.BlockSpec((BLK, KVW), f)
    return _side_call(
        body, job, name=f"attn_fwd{l}", grid=(T // BLK,),
        in_specs=[pl.BlockSpec(memory_space=pltpu.SMEM),
                  pl.BlockSpec((BLK, AW), cur), kv(zero), kv(prev), kv(cur), kv(zero), kv(prev), kv(cur),
                  pl.BlockSpec((BLK, AW), lambda n: (n, 3))],
        out_specs=[pl.BlockSpec((BLK, AW), cur), pl.BlockSpec((BLK, AW), cur), pl.BlockSpec((BLK, 128), cur)],
        out_shape=[jax.ShapeDtypeStruct((T, AW), BF16), jax.ShapeDtypeStruct((T, AW), F32),
                   jax.ShapeDtypeStruct((T, 128), F32)],
        scratch_shapes=[], semantics=("parallel",), args=[sinks, qr, kr, kr, kr, vb, vb, vb, proj])


def _attn_bwd(qr, kr, vb, proj, att, lse, d_ya, sinks, dproj, l, job=None):
    T = qr.shape[0]
    nb = T // BLK

    def body(sink_ref, q_ref, k0_ref, kp_ref, kc_ref, v0_ref, vp_ref, vc_ref, ag_ref, att_ref, lse_ref, dy_ref, _,
             dq_ref, dk_ref, dv_ref, dk0_ref, dv0_ref, dag_ref, dsink_ref, kcarry, vcarry):
        n = pl.program_id(0)

        @pl.when(n == 0)
        def _():
            dk0_ref[...] = jnp.zeros_like(dk0_ref)
            dv0_ref[...] = jnp.zeros_like(dv0_ref)
            dsink_ref[...] = jnp.zeros_like(dsink_ref)
            kcarry[...] = jnp.zeros_like(kcarry)
            vcarry[...] = jnp.zeros_like(vcarry)

        @pl.when(n == nb)
        def _():
            dk_ref[...] = kcarry[...]
            dv_ref[...] = vcarry[...]

        @pl.when(n < nb)
        def _():
            mask = _attn_mask(n)
            lane = lax.broadcasted_iota(jnp.int32, (1, 128), 1)
            lse = lse_ref[...]
            dsink = jnp.zeros((1, 128), F32)
            dk_pg, dv_pg = [], []
            for pg in range(2):
                dk_acc = jnp.zeros((3 * BLK, 128), F32)
                dv_acc = jnp.zeros((3 * BLK, 128), F32)
                for off in range(2):
                    g = 2 * pg + off
                    k_lo, k_hi = _kv_halves(k0_ref, kp_ref, kc_ref, g)
                    v_lo, v_hi = _kv_halves(v0_ref, vp_ref, vc_ref, g)
                    dkg = jnp.zeros((3 * BLK, 128), F32)
                    dvg = jnp.zeros((3 * BLK, 128), F32)
                    for pp in range(2):
                        cols = slice(128 * (2 * g + pp), 128 * (2 * g + pp) + 128)
                        qpair = q_ref[:, cols]
                        gate, dgate = _silu_and_grad(ag_ref[:, cols])
                        dy = dy_ref[:, cols]
                        dag_ref[:, cols] = (dy * att_ref[:, cols] * dgate).astype(BF16)
                        do = (dy * gate).astype(BF16)
                        dq = jnp.zeros((BLK, 128), F32)
                        for hh, (kx, vx) in enumerate(((k_lo, v_lo), (k_hi, v_hi))):
                            h = 4 * g + 2 * pp + hh
                            sink = sink_ref[l, h]
                            lse_h = jnp.sum(jnp.where(lane == h, lse, 0.0), axis=1, keepdims=True)
                            s = _dot_nt(qpair, kx)
                            p = jnp.where(mask, jnp.exp(s - lse_h), 0.0)
                            dp = _dot_nt(do, vx)
                            delta = jnp.sum(p * dp, axis=1, keepdims=True)
                            ds = (p * (dp - delta)).astype(BF16)
                            psink = jnp.exp(sink - lse_h)
                            dsink += jnp.where(lane == h, -jnp.sum(psink * delta), 0.0)
                            dq += _dot(ds, kx)
                            half = (lane < 64) if hh == 0 else (lane >= 64)
                            dkg += jnp.where(half, _dot_tn(ds, qpair), 0.0)
                            dvg += jnp.where(half, _dot_tn(p.astype(BF16), do), 0.0)
                        dq_ref[:, cols] = dq
                    own = (lane < 64) if off == 0 else (lane >= 64)
                    dk_acc += jnp.where(own, dkg + pltpu.roll(dkg, 64, axis=1), 0.0)
                    dv_acc += jnp.where(own, dvg + pltpu.roll(dvg, 64, axis=1), 0.0)
                dk_pg.append(dk_acc)
                dv_pg.append(dv_acc)
            dsink_ref[...] += dsink
            for pg in range(2):
                cols = slice(128 * pg, 128 * pg + 128)
                dk0_ref[:, cols] += dk_pg[pg][0:BLK]
                dv0_ref[:, cols] += dv_pg[pg][0:BLK]
                dk_ref[:, cols] = kcarry[:, cols] + dk_pg[pg][BLK:2 * BLK]
                dv_ref[:, cols] = vcarry[:, cols] + dv_pg[pg][BLK:2 * BLK]
                kcarry[:, cols] = dk_pg[pg][2 * BLK:3 * BLK]
                vcarry[:, cols] = dv_pg[pg][2 * BLK:3 * BLK]

    last = nb - 1
    cur = lambda n: (jnp.minimum(n, last), 0)
    prev = lambda n: (jnp.clip(n - 1, 0, last), 0)
    zero = lambda n: (0, 0)
    kv = lambda f: pl.BlockSpec((BLK, KVW), f)
    wide = lambda f: pl.BlockSpec((BLK, AW), f)
    return _side_call(
        body, job, name=f"attn_bwd{l}", grid=(nb + 1,),
        in_specs=[pl.BlockSpec(memory_space=pltpu.SMEM),
                  wide(cur), kv(zero), kv(prev), kv(cur), kv(zero), kv(prev), kv(cur),
                  pl.BlockSpec((BLK, AW), lambda n: (jnp.minimum(n, last), 3)),
                  wide(cur), pl.BlockSpec((BLK, 128), cur), wide(cur), pl.BlockSpec(memory_space=pl.ANY)],
        out_specs=[wide(cur), kv(prev), kv(prev), kv(zero), kv(zero),
                   pl.BlockSpec((BLK, AW), lambda n: (jnp.minimum(n, last), 3)),
                   pl.BlockSpec((1, 128), zero)],
        out_shape=[jax.ShapeDtypeStruct((T, AW), F32), jax.ShapeDtypeStruct((T, KVW), F32),
                   jax.ShapeDtypeStruct((T, KVW), F32), jax.ShapeDtypeStruct((BLK, KVW), F32),
                   jax.ShapeDtypeStruct((BLK, KVW), F32), jax.ShapeDtypeStruct(dproj.shape, BF16),
                   jax.ShapeDtypeStruct((1, 128), F32)],
        scratch_shapes=[pltpu.VMEM((BLK, KVW), F32), pltpu.VMEM((BLK, KVW), F32)],
        semantics=("arbitrary",), aliases={12: 5},
        args=[sinks, qr, kr, kr, kr, vb, vb, vb, proj, att, lse, d_ya, dproj])


def _rope_bwd(dqr, dk, dv, dk0, dv0, tabs, dproj, l):
    T = dqr.shape[0]

    def body(dq_ref, dk_ref, dv_ref, dk0_ref, dv0_ref, c_ref, s1_ref, s2_ref, _, o_ref):
        n = pl.program_id(0)
        c, s1, s2 = c_ref[...], s1_ref[...], s2_ref[...]
        first = jnp.where(n == 0, 1.0, 0.0)
        for gcol in range(AW // 128):
            cols = slice(128 * gcol, 128 * gcol + 128)
            o_ref[:, cols] = (_rot_bwd(dq_ref[:, cols], c, s1, s2) * 0.125).astype(BF16)
        for gcol in range(KVW // 128):
            cols = slice(128 * gcol, 128 * gcol + 128)
            dkk = dk_ref[:, cols] + first * dk0_ref[:, cols]
            o_ref[:, AW + 128 * gcol:AW + 128 * gcol + 128] = _rot_bwd(dkk, c, s1, s2).astype(BF16)
            dvv = dv_ref[:, cols] + first * dv0_ref[:, cols]
            o_ref[:, AW + KVW + 128 * gcol:AW + KVW + 128 * gcol + 128] = dvv.astype(BF16)

    cur = lambda n: (n, 0)
    zero = lambda n: (0, 0)
    tab = pl.BlockSpec((BLK, 128), cur)
    return pl.pallas_call(
        body, name=f"rope_bwd{l}", grid=(T // BLK,),
        in_specs=[pl.BlockSpec((BLK, AW), cur), pl.BlockSpec((BLK, KVW), cur), pl.BlockSpec((BLK, KVW), cur),
                  pl.BlockSpec((BLK, KVW), zero), pl.BlockSpec((BLK, KVW), zero), tab, tab, tab,
                  pl.BlockSpec(memory_space=pl.ANY)],
        out_specs=pl.BlockSpec((BLK, AW + 2 * KVW), lambda n: (n, 1)),
        out_shape=jax.ShapeDtypeStruct(dproj.shape, BF16),
        input_output_aliases={8: 0},
        compiler_params=_cp("parallel"),
    )(dqr, dk, dv, dk0, dv0, *tabs, dproj)


def _block_diag(w):
    nl, nh, hd, _ = w.shape
    eye = jnp.eye(nh, dtype=w.dtype)
    return jnp.einsum("lhij,hg->lhigj", w, eye).reshape(nl, nh * hd, nh * hd)


def _diag_blocks(m):
    nh, hd = 8, 64
    return jnp.einsum("hihj->hij", m.reshape(nh, hd, nh, hd))


def _device_step(x, target, p, dist=None):
    vec = lambda a: a.reshape(DEPTH, 1, a.shape[-1])
    ln_in_g, ln_in_b = p["ln_in_g"].reshape(1, D), p["ln_in_b"].reshape(1, D)
    conv_dw_b, conv_ln_g, conv_ln_b, conv_pw_b = map(vec, (p["conv_dw_b"], p["conv_ln_g"], p["conv_ln_b"], p["conv_pw_b"]))
    lru_conv_b, lru_ba, lru_bx, lru_lambda = map(vec, (p["lru_conv_b"], p["lru_ba"], p["lru_bx"], p["lru_lambda"]))
    ln_post_g, ln_post_b = vec(p["ln_post_g"]), vec(p["ln_post_b"])
    wa_bd = _block_diag(p["lru_wa"]).astype(BF16)
    wx_bd = _block_diag(p["lru_wx"]).astype(BF16)
    w_in, w_out, pw_w = list(p["w_in"]), list(p["w_out"]), list(p["conv_pw_w"])
    sinks = p["attn_sinks"]
    big_names = ("w_in", "w_out", "conv_pw_w")

    (h, hb), got = _embed_fwd(x, p["meta_tokens"], ln_in_g, ln_in_b, job=_gather_job([w_in[0]]) if dist else None)
    if dist:
        w_in[0] = got[0]
    T = h.shape[0]
    tabs = _rope_tables(T)
    saved = []
    for l in range(DEPTH):
        (proj,), got = _proj_fwd(hb, w_in[l], l, job=_gather_job([w_out[0], pw_w[0]]) if dist and l == 0 else None)
        if got:
            w_out[0], pw_w[0] = got
        pw_l = pw_w[l].reshape(CW, CW)
        yc, conv = _conv_fwd(proj, p["conv_dw_w"], conv_dw_b, conv_ln_g, conv_ln_b, pw_l, conv_pw_b, l)
        qr, kr, vb = _rope_fwd(proj, tabs, l)
        (ya, att, lse), got = _attn_fwd(
            qr, kr, vb, proj, sinks, l, job=_gather_job([w_in[1], w_out[1], pw_w[1]]) if dist and l == 0 else None)
        if got:
            w_in[1], w_out[1], pw_w[1] = got
        yl, hl = _lru_fwd(proj, p["lru_conv_w"], lru_conv_b, wa_bd, lru_ba, wx_bd, lru_bx, lru_lambda, l)
        hn, hnb, xhat, rstd = _out_fwd(yc, ya, yl, w_out[l], h, ln_post_g, ln_post_b, l)
        saved.append((hb, proj, yc, conv, qr, kr, vb, ya, att, lse, yl, hl, xhat, rstd, pw_l))
        h, hb = hn, hnb

    loss_part, dh = _loss_head(h, target)
    g = {}
    later = None
    early, last = ("w_out", "conv_pw_w"), ("w_in",)
    own = {}
    for l in reversed(range(DEPTH)):
        hb_l, proj, yc, conv, qr, kr, vb, ya, att, lse, yl, hl, xhat, rstd, pw_l = saved[l]
        tail = dist is not None and l == 0
        dz, dzb, g["ln_post_g", l], g["ln_post_b", l] = _post_ln_bwd(dh, xhat, rstd, ln_post_g, l)
        (d_yc, d_ya, d_yl), recv = _dcat_bwd(dzb, w_out[l], l, job=_swap_job(later["grads"]) if later else None)
        if later:
            later["parts"], later["owns"] = _chip_partials(big_names, later["grads"], recv, dist, later["l"])
        g["w_out", l] = _dwout_bwd(yc, ya, yl, dzb, l)
        d_conv, dproj, dpw, g["conv_pw_b", l], g["conv_ln_g", l], g["conv_ln_b", l] = _conv_bwd_rows(
            conv, proj, d_yc, conv_ln_g, conv_ln_b, pw_l, conv_pw_b, l)
        g["conv_pw_w", l] = dpw.reshape(N_SHARD, 2, PW_SH // 2, CW)
        if tail:
            own["early"] = dict(l=0, grads=[g[name, 0] for name in early])
        (dproj, ddw, g["conv_dw_b", l]), recv = _conv_bwd_taps(
            d_conv, proj, p["conv_dw_w"], dproj, l, job=_swap_job(own["early"]["grads"]) if tail else None)
        if tail:
            own["early"]["parts"], own["early"]["owns"] = _chip_partials(early, own["early"]["grads"], recv, dist, 0)
        g["conv_dw_w", l] = ddw[:CONV_K]
        (dqr, dk, dv, dk0, dv0, dproj, dsink), z = _attn_bwd(
            qr, kr, vb, proj, att, lse, d_ya, sinks, dproj, l, job=_scatter_job(later["parts"]) if later else None)
        if later:
            later["z"] = z
        g["attn_sinks", l] = dsink[0, :N_HEADS]
        dproj = _rope_bwd(dqr, dk, dv, dk0, dv0, tabs, dproj, l)
        (dproj, dlw, g["lru_conv_b", l], dwa, g["lru_ba", l], dwx, g["lru_bx", l], g["lru_lambda", l]), z = _lru_bwd(
            proj, hl, d_yl, p["lru_conv_w"], lru_conv_b, wa_bd, lru_ba, wx_bd, lru_bx, lru_lambda, dproj, l,
            job=_scatter_job(own["early"]["parts"]) if tail else None)
        if tail:
            own["early"]["z"] = z
        g["lru_conv_w", l] = dlw[:LRU_K]
        g["lru_wa", l] = _diag_blocks(dwa)
        g["lru_wx", l] = _diag_blocks(dwx)
        g["w_in", l] = _dwin_bwd(hb_l, dproj, l)
        job = None
        if tail:
            own["last"] = dict(l=0, grads=[g["w_in", 0]])
            pack_a = _pack_rows([_layer_stack(g, name) for name in _SMALL_LAYERED])
            job = _join_jobs(_swap_job(own["last"]["grads"]), _spread_job(pack_a))
        (dh,), got = _dh_bwd(dproj, w_in[l], dz, l, job=job)
        if tail:
            own["last"]["parts"], own["last"]["owns"] = _chip_partials(last, own["last"]["grads"], got[:1], dist, 0)
            g["pack_layered", -1] = _sum_slots(pack_a, got[1], dist[3], "layered")
        if later:
            _finish_reduce(big_names, later, dist, g)
            later = None
        if dist and l > 0:
            later = dict(l=l, grads=[g[name, l] for name in big_names])
    grad_x, g["meta_tokens", -1], g["ln_in_g", -1], g["ln_in_b", -1] = _embed_bwd(
        dh, x, p["meta_tokens"], ln_in_g, ln_in_b)
    if dist:
        pack_b = _pack_rows([g[name, -1] for name in _SMALL_EMBED])
        got = _run_job(_join_jobs(_scatter_job(own["last"]["parts"]), _spread_job(pack_b)), "scatter_and_spread")
        own["last"]["z"] = got[:1]
        g["pack_embed", -1] = _sum_slots(pack_b, got[1], dist[3], "embed")
        state = dict(l=0, owns=own["last"]["owns"] + own["early"]["owns"], z=own["last"]["z"] + own["early"]["z"])
        _finish_reduce(last + early, state, dist, g)
    return loss_part, grad_x, g


_SMALL_EMBED = ("meta_tokens", "ln_in_g", "ln_in_b")
_SMALL_LAYERED = ("conv_dw_w", "conv_dw_b", "conv_ln_g", "conv_ln_b", "conv_pw_b", "attn_sinks", "lru_conv_w",
                  "lru_conv_b", "lru_wa", "lru_ba", "lru_wx", "lru_bx", "lru_lambda", "ln_post_g", "ln_post_b")


def _layer_stack(g, name):
    return jnp.stack([g[name, l] for l in range(DEPTH)], axis=0)


def _chip_partials(names, grads, recv, dist, l):
    outs = [_chip_partial(a, r, dist[0], dist[1], f"{name}{l}") for name, a, r in zip(names, grads, recv)]
    return [o[0] for o in outs], [o[1] for o in outs]


def _finish_reduce(names, state, dist, g):
    l = state["l"]
    totals = [_shard_total(po, zz, dist[2], f"{name}{l}") for name, po, zz in zip(names, state["owns"], state["z"])]
    full = _run_job(_share_job(totals), f"share_halves{l}")
    for name, f in zip(names, full):
        g[name, l] = f.reshape(2 * f.shape[1], f.shape[2])


MESH = pl.DeviceIdType.MESH
HBM_SPEC = pl.BlockSpec(memory_space=pltpu.HBM)
N_DEV = 8


def _position():
    x, y, c = lax.axis_index("x"), lax.axis_index("y"), lax.axis_index("c")
    return x, y, c


def _other_chips(x, y):
    return [(1 - x, y), (x, 1 - y), (1 - x, 1 - y)]


def _cast_into_slot(a, l, j, tag):
    _, R, C = a.shape
    tb = _pick(R, (512, 128))

    def body(s_ref, a_ref, o_ref):
        o_ref[...] = a_ref[...].astype(BF16)

    grid_spec = pltpu.PrefetchScalarGridSpec(
        num_scalar_prefetch=1, grid=(R // tb,),
        in_specs=[pl.BlockSpec((None, tb, C), lambda t, sc: (l, t, 0))],
        out_specs=pl.BlockSpec((None, tb, C), lambda t, sc: (sc[0], t, 0)))
    return pl.pallas_call(
        body, name=f"cast_into_slot_{tag}{l}", grid_spec=grid_spec,
        out_shape=jax.ShapeDtypeStruct((N_SHARD, R, C), BF16),
        compiler_params=_cp("arbitrary"),
    )(jnp.reshape(j, (1,)).astype(jnp.int32), a)


class _Job:
    def __init__(self, inputs, aliased, extra_out, sems, start, mid, finish):
        self.inputs, self.aliased, self.extra_out, self.sems = list(inputs), aliased, list(extra_out), list(sems)
        self.start, self.mid, self.finish = start, mid, finish

    def out_shapes(self):
        own = [jax.ShapeDtypeStruct(a.shape, a.dtype) for a in self.inputs] if self.aliased else []
        return own + self.extra_out


def _side_call(body, job, *, name, grid, in_specs, out_specs, out_shape, scratch_shapes, semantics, args,
               aliases=None):
    aliases = dict(aliases or {})
    if job is None:
        outs = pl.pallas_call(
            body, name=name, grid=grid, in_specs=in_specs, out_specs=out_specs, out_shape=out_shape,
            scratch_shapes=scratch_shapes, input_output_aliases=aliases, compiler_params=_cp(*semantics))(*args)
        return list(outs), []
    n_in, n_out, n_scr = len(in_specs), len(out_specs), len(scratch_shapes)
    j_in, j_out = len(job.inputs), len(job.out_shapes())
    steps = 1
    for gsize in grid:
        steps *= gsize

    def wrapped(*refs):
        host_in, job_in = refs[:n_in], refs[n_in:n_in + j_in]
        o0 = n_in + j_in
        host_out, job_out = refs[o0:o0 + n_out], refs[o0 + n_out:o0 + n_out + j_out]
        s0 = o0 + n_out + j_out
        host_scr, sems = refs[s0:s0 + n_scr], refs[s0 + n_scr:]
        step = pl.program_id(0)
        for d in range(1, len(grid)):
            step = step * grid[d] + pl.program_id(d)

        @pl.when(step == 0)
        def _():
            job.start(job_in, job_out, sems)

        @pl.when(step == max(steps - 2, 0))
        def _():
            job.mid(job_in, job_out, sems)

        body(*host_in, *host_out, *host_scr)

        @pl.when(step == steps - 1)
        def _():
            job.finish(job_in, job_out, sems)

    if job.aliased:
        aliases.update({n_in + k: n_out + k for k in range(j_in)})
    outs = pl.pallas_call(
        wrapped, name=name, grid=grid,
        in_specs=list(in_specs) + [HBM_SPEC] * j_in, out_specs=list(out_specs) + [HBM_SPEC] * j_out,
        out_shape=list(out_shape) + job.out_shapes(),
        scratch_shapes=list(scratch_shapes) + job.sems, input_output_aliases=aliases,
        compiler_params=_cp(*(["arbitrary"] * len(grid))))(*args, *job.inputs)
    return list(outs[:n_out]), list(outs[n_out:])


def _run_job(job, name):
    return _side_call(lambda: None, job, name=name, grid=(1,), in_specs=[], out_specs=[], out_shape=[],
                      scratch_shapes=[], semantics=("arbitrary",), args=[])[1]


def _gather_job(slots):
    n = len(slots)

    def copies(buf, sems):
        ici_send, ici_recv, d2d_send, d2d_recv = sems
        x, y, c = _position()
        chips = _other_chips(x, y)

        def half(k, slot, which):
            hr = buf[k].shape[1] // 2
            return buf[k].at[slot, pl.ds(pl.multiple_of(which * hr, hr), hr)]

        def over_ici(k, p, slot):
            px, py = chips[p]
            return pltpu.make_async_remote_copy(
                src_ref=half(k, slot, c), dst_ref=half(k, slot, c),
                send_sem=ici_send.at[k * 3 + p], recv_sem=ici_recv.at[k * 3 + p],
                device_id=(px, py, c), device_id_type=MESH)

        def over_d2d(k, p, which):
            px, py = chips[p]
            return pltpu.make_async_remote_copy(
                src_ref=half(k, 2 * px + py, which), dst_ref=half(k, 2 * px + py, which),
                send_sem=d2d_send.at[k * 3 + p], recv_sem=d2d_recv.at[k * 3 + p],
                device_id=(x, y, 1 - c), device_id_type=MESH)

        return over_ici, over_d2d, 2 * x + y, chips, c

    pairs = [(k, p) for k in range(n) for p in range(3)]

    def start(_, buf, sems):
        over_ici, _, mine, _, _ = copies(buf, sems)
        for k, p in pairs:
            over_ici(k, p, mine).start()

    def mid(_, buf, sems):
        over_ici, over_d2d, _, chips, c = copies(buf, sems)
        for k, p in pairs:
            px, py = chips[p]
            over_ici(k, p, 2 * px + py).wait_recv()
            over_d2d(k, p, c).start()

    def finish(_, buf, sems):
        over_ici, over_d2d, mine, _, c = copies(buf, sems)
        for k, p in pairs:
            over_d2d(k, p, 1 - c).wait_recv()
        for k, p in pairs:
            over_ici(k, p, mine).wait_send()
            over_d2d(k, p, c).wait_send()

    return _Job(slots, True, [], [pltpu.SemaphoreType.DMA((3 * n,))] * 4, start, mid, finish)


def _gather_shards(shards):
    n = len(shards)

    def body(*refs):
        src, dst = refs[:n], refs[n:2 * n]
        send_sems, recv_sems, local_sems = refs[2 * n:]
        x, y, c = _position()
        mine = 2 * x + y
        chips = _other_chips(x, y)

        def copy(k, p):
            return pltpu.make_async_remote_copy(
                src_ref=src[k], dst_ref=dst[k].at[mine],
                send_sem=send_sems.at[k * 3 + p], recv_sem=recv_sems.at[k * 3 + p],
                device_id=(*chips[p], c), device_id_type=MESH)

        def arrival(k, p):
            px, py = chips[p]
            return pltpu.make_async_remote_copy(
                src_ref=src[k], dst_ref=dst[k].at[2 * px + py],
                send_sem=send_sems.at[k * 3 + p], recv_sem=recv_sems.at[k * 3 + p],
                device_id=(px, py, c), device_id_type=MESH)

        local = [pltpu.make_async_copy(src[k], dst[k].at[mine], local_sems.at[k]) for k in range(n)]
        for cp in local:
            cp.start()
        for k in range(n):
            for p in range(3):
                copy(k, p).start()
        for k in range(n):
            for p in range(3):
                arrival(k, p).wait_recv()
        for k in range(n):
            for p in range(3):
                copy(k, p).wait_send()
        for cp in local:
            cp.wait()

    return pl.pallas_call(
        body, name="gather_shards",
        in_specs=[HBM_SPEC] * n, out_specs=[HBM_SPEC] * n,
        out_shape=[jax.ShapeDtypeStruct((N_SHARD,) + s.shape, s.dtype) for s in shards],
        scratch_shapes=[pltpu.SemaphoreType.DMA((3 * n,)), pltpu.SemaphoreType.DMA((3 * n,)),
                        pltpu.SemaphoreType.DMA((n,))],
    )(*shards)


def _swap_job(grads):
    n = len(grads)

    def copies(src, dst, sems):
        x, y, c = _position()
        return [pltpu.make_async_remote_copy(
            src_ref=src[k].at[:, 1 - c], dst_ref=dst[k],
            send_sem=sems[0].at[k], recv_sem=sems[1].at[k],
            device_id=(x, y, 1 - c), device_id_type=MESH) for k in range(n)]

    def start(src, dst, sems):
        for cp in copies(src, dst, sems):
            cp.start()

    def finish(src, dst, sems):
        for cp in copies(src, dst, sems):
            cp.wait()

    return _Job(grads, False, [jax.ShapeDtypeStruct((N_SHARD,) + g.shape[2:], F32) for g in grads],
                [pltpu.SemaphoreType.DMA((n,))] * 2, start, lambda *_: None, finish)


def _chip_partial(a, y, c, j, tag):
    _, _, R, C = a.shape
    tr = _pick(R, (256, 64))

    def body(s_ref, a_ref, y_ref, pb_ref, po_ref):
        total = a_ref[...] + y_ref[...]
        pb_ref[...] = total.astype(BF16)

        @pl.when(pl.program_id(1) == s_ref[1])
        def _():
            po_ref[...] = total

    grid_spec = pltpu.PrefetchScalarGridSpec(
        num_scalar_prefetch=1, grid=(R // tr, N_SHARD),
        in_specs=[pl.BlockSpec((None, None, tr, C), lambda t, s, sc: (s, sc[0], t, 0)),
                  pl.BlockSpec((None, tr, C), lambda t, s, sc: (s, t, 0))],
        out_specs=[pl.BlockSpec((None, tr, C), lambda t, s, sc: (s, t, 0)),
                   pl.BlockSpec((tr, C), lambda t, s, sc: (t, 0))])
    return pl.pallas_call(
        body, name=f"chip_partial_{tag}", grid_spec=grid_spec,
        out_shape=[jax.ShapeDtypeStruct((N_SHARD, R, C), BF16), jax.ShapeDtypeStruct((R, C), F32)],
        compiler_params=_cp("arbitrary", "arbitrary"),
    )(jnp.stack([c, j]).astype(jnp.int32), a, y)


def _scatter_job(parts):
    n = len(parts)
    pairs = [(k, p) for k in range(n) for p in range(3)]

    def copy(src, dst, sems, k, p, outgoing):
        x, y, c = _position()
        mine = 2 * x + y
        px, py = _other_chips(x, y)[p]
        theirs = 2 * px + py
        return pltpu.make_async_remote_copy(
            src_ref=src[k].at[theirs if outgoing else mine], dst_ref=dst[k].at[mine if outgoing else theirs],
            send_sem=sems[0].at[k * 3 + p], recv_sem=sems[1].at[k * 3 + p],
            device_id=(px, py, c), device_id_type=MESH)

    def start(src, dst, sems):
        for k, p in pairs:
            copy(src, dst, sems, k, p, True).start()

    def finish(src, dst, sems):
        for k, p in pairs:
            copy(src, dst, sems, k, p, False).wait_recv()
        for k, p in pairs:
            copy(src, dst, sems, k, p, True).wait_send()

    return _Job(parts, False, [jax.ShapeDtypeStruct(pb.shape, BF16) for pb in parts],
                [pltpu.SemaphoreType.DMA((3 * n,))] * 2, start, lambda *_: None, finish)


def _shard_total(own, z, others_c, tag):
    R, C = own.shape
    tr = _pick(R, (256, 64))

    def body(s_ref, o_ref, z0_ref, z1_ref, z2_ref, h_ref):
        h_ref[...] = ((o_ref[...] + z0_ref[...].astype(F32)) + z1_ref[...].astype(F32)) + z2_ref[...].astype(F32)

    zspec = lambda q: pl.BlockSpec((None, tr, C), lambda t, sc: (sc[q], t, 0))
    grid_spec = pltpu.PrefetchScalarGridSpec(
        num_scalar_prefetch=1, grid=(R // tr,),
        in_specs=[pl.BlockSpec((tr, C), lambda t, sc: (t, 0)), zspec(0), zspec(1), zspec(2)],
        out_specs=pl.BlockSpec((None, tr, C), lambda t, sc: (sc[3], t, 0)))
    return pl.pallas_call(
        body, name=f"shard_total_{tag}", grid_spec=grid_spec,
        out_shape=jax.ShapeDtypeStruct((2, R, C), F32),
        compiler_params=_cp("arbitrary"),
    )(others_c, own, z, z, z)


def _share_job(totals):
    n = len(totals)

    def copy(buf, sems, k, which):
        x, y, c = _position()
        return pltpu.make_async_remote_copy(
            src_ref=buf[k].at[which], dst_ref=buf[k].at[which],
            send_sem=sems[0].at[k], recv_sem=sems[1].at[k],
            device_id=(x, y, 1 - c), device_id_type=MESH)

    def start(_, buf, sems):
        c = lax.axis_index("c")
        for k in range(n):
            copy(buf, sems, k, c).start()

    def finish(_, buf, sems):
        c = lax.axis_index("c")
        for k in range(n):
            copy(buf, sems, k, 1 - c).wait_recv()
        for k in range(n):
            copy(buf, sems, k, c).wait_send()

    return _Job(totals, True, [], [pltpu.SemaphoreType.DMA((n,))] * 2, start, lambda *_: None, finish)


def _spread_job(pack):
    def copy(src, dst, sems, m, outgoing):
        x, y, c = _position()
        peer = (x ^ (m >> 2), y ^ ((m >> 1) & 1), c ^ (m & 1))
        slot = 4 * x + 2 * y + c if outgoing else 4 * peer[0] + 2 * peer[1] + peer[2]
        return pltpu.make_async_remote_copy(
            src_ref=src[0], dst_ref=dst[0].at[slot], send_sem=sems[0].at[m - 1], recv_sem=sems[1].at[m - 1],
            device_id=peer, device_id_type=MESH)

    def start(src, dst, sems):
        for m in range(1, N_DEV):
            copy(src, dst, sems, m, True).start()

    def finish(src, dst, sems):
        for m in range(1, N_DEV):
            copy(src, dst, sems, m, False).wait_recv()
        for m in range(1, N_DEV):
            copy(src, dst, sems, m, True).wait_send()

    return _Job([pack], False, [jax.ShapeDtypeStruct((N_DEV,) + pack.shape, F32)],
                [pltpu.SemaphoreType.DMA((N_DEV - 1,))] * 2, start, lambda *_: None, finish)


def _join_jobs(a, b):
    assert not a.aliased and not b.aliased
    n_in, n_out, n_sem = len(a.inputs), len(a.extra_out), len(a.sems)

    def phase(name):
        def run(ins, outs, sems):
            getattr(a, name)(ins[:n_in], outs[:n_out], sems[:n_sem])
            getattr(b, name)(ins[n_in:], outs[n_out:], sems[n_sem:])
        return run

    return _Job(a.inputs + b.inputs, False, a.extra_out + b.extra_out, a.sems + b.sems,
                phase("start"), phase("mid"), phase("finish"))


def _sum_slots(pack, slots, me, tag):
    def body(me_ref, p_ref, s_ref, o_ref):
        acc = None
        for d in range(N_DEV):
            term = jnp.where(me_ref[0] == d, p_ref[...], s_ref[d])
            acc = term if acc is None else acc + term
        o_ref[...] = acc

    vm = pl.BlockSpec(memory_space=pltpu.VMEM)
    return pl.pallas_call(
        body, name=f"sum_slots_{tag}",
        in_specs=[pl.BlockSpec(memory_space=pltpu.SMEM), vm, vm], out_specs=vm,
        out_shape=jax.ShapeDtypeStruct(pack.shape, F32),
        compiler_params=pltpu.CompilerParams(vmem_limit_bytes=V7X_VMEM_LIMIT),
    )(jnp.reshape(me, (1,)).astype(jnp.int32), pack, slots)


def _pack_rows(arrays):
    total = sum(a.size for a in arrays)
    rows = -(-total // 128)
    rows = -(-rows // PACK_ROWS_ALIGN) * PACK_ROWS_ALIGN
    flat = [a.reshape(-1) for a in arrays] + [jnp.zeros((rows * 128 - total,), F32)]
    return jnp.concatenate(flat).reshape(rows, 128)


def _adamw_math(w, g, m, v):
    m = ADAM_B1 * m + (1.0 - ADAM_B1) * g
    v = ADAM_B2 * v + (1.0 - ADAM_B2) * (g * g)
    m_hat = m / (1.0 - ADAM_B1 ** ADAM_STEP)
    v_hat = v / (1.0 - ADAM_B2 ** ADAM_STEP)
    delta = -ADAM_LR * (m_hat / (jnp.sqrt(v_hat) + ADAM_EPS) + ADAM_WD * w)
    return delta, m, v


def _adamw_big(w, g0, g1, m, v, tag):
    _, R, C = w.shape
    tr = _pick(R, (256, 128))

    def body(w_ref, g0_ref, g1_ref, m_ref, v_ref, go_ref, d_ref, mo_ref, vo_ref):
        g = jnp.where(pl.program_id(0) == 0, g0_ref[...], g1_ref[...])
        delta, mn, vn = _adamw_math(w_ref[...], g, m_ref[...], v_ref[...])
        go_ref[...] = g
        d_ref[...] = delta
        mo_ref[...] = mn
        vo_ref[...] = vn

    s3 = pl.BlockSpec((None, tr, C), lambda l, t: (l, t, 0))
    s2 = pl.BlockSpec((tr, C), lambda l, t: (t, 0))
    shp = jax.ShapeDtypeStruct(w.shape, F32)
    return pl.pallas_call(
        body, name=f"adamw_{tag}", grid=(2, R // tr),
        in_specs=[s3, s2, s2, s3, s3], out_specs=[s3, s3, s3, s3],
        out_shape=[shp, shp, shp, shp],
        compiler_params=_cp("parallel", "parallel"),
    )(w, g0, g1, m, v)


def _adamw_small(ws, gs, ms, vs):
    n = len(ws)

    def body(*refs):
        w_r, g_r, m_r, v_r = refs[:n], refs[n:2 * n], refs[2 * n:3 * n], refs[3 * n:4 * n]
        d_o, m_o, v_o = refs[4 * n:5 * n], refs[5 * n:6 * n], refs[6 * n:7 * n]
        for k in range(n):
            delta, mn, vn = _adamw_math(w_r[k][...], g_r[k][...], m_r[k][...], v_r[k][...])
            d_o[k][...] = delta
            m_o[k][...] = mn
            v_o[k][...] = vn

    vm = pl.BlockSpec(memory_space=pltpu.VMEM)
    shapes = [jax.ShapeDtypeStruct(w.shape, F32) for w in ws]
    outs = pl.pallas_call(
        body, name="adamw_small",
        in_specs=[vm] * (4 * n), out_specs=[vm] * (3 * n),
        out_shape=shapes * 3,
    )(*ws, *gs, *ms, *vs)
    return outs[:n], outs[n:2 * n], outs[2 * n:]


_WEIGHTS = ["meta_tokens", "ln_in_g", "ln_in_b", "w_in", "conv_dw_w", "conv_dw_b", "conv_ln_g", "conv_ln_b",
            "conv_pw_w", "conv_pw_b", "attn_sinks", "lru_conv_w", "lru_conv_b", "lru_wa", "lru_ba", "lru_wx",
            "lru_bx", "lru_lambda", "w_out", "ln_post_g", "ln_post_b"]
_BIG = ("w_in", "w_out", "conv_pw_w")
_SMALL_SHARDED = {"meta_tokens": 1, "conv_dw_w": 2, "lru_conv_w": 2}
PACK_ROWS_ALIGN = 8


def _as2d(a):
    return a.reshape(1, -1) if a.ndim == 1 else a.reshape(-1, a.shape[-1])


def kernel(x, meta_tokens, ln_in_g, ln_in_b, w_in, conv_dw_w, conv_dw_b, conv_ln_g, conv_ln_b, conv_pw_w, conv_pw_b, attn_sinks, lru_conv_w, lru_conv_b, lru_wa, lru_ba, lru_wx, lru_bx, lru_lambda, w_out, ln_post_g, ln_post_b, loss_target, m_meta_tokens, m_ln_in_g, m_ln_in_b, m_w_in, m_conv_dw_w, m_conv_dw_b, m_conv_ln_g, m_conv_ln_b, m_conv_pw_w, m_conv_pw_b, m_attn_sinks, m_lru_conv_w, m_lru_conv_b, m_lru_wa, m_lru_ba, m_lru_wx, m_lru_bx, m_lru_lambda, m_w_out, m_ln_post_g, m_ln_post_b, v_meta_tokens, v_ln_in_g, v_ln_in_b, v_w_in, v_conv_dw_w, v_conv_dw_b, v_conv_ln_g, v_conv_ln_b, v_conv_pw_w, v_conv_pw_b, v_attn_sinks, v_lru_conv_w, v_lru_conv_b, v_lru_wa, v_lru_ba, v_lru_wx, v_lru_bx, v_lru_lambda, v_w_out, v_ln_post_g, v_ln_post_b):
    w = dict(meta_tokens=meta_tokens, ln_in_g=ln_in_g, ln_in_b=ln_in_b, w_in=w_in, conv_dw_w=conv_dw_w,
             conv_dw_b=conv_dw_b, conv_ln_g=conv_ln_g, conv_ln_b=conv_ln_b, conv_pw_w=conv_pw_w,
             conv_pw_b=conv_pw_b, attn_sinks=attn_sinks, lru_conv_w=lru_conv_w, lru_conv_b=lru_conv_b,
             lru_wa=lru_wa, lru_ba=lru_ba, lru_wx=lru_wx, lru_bx=lru_bx, lru_lambda=lru_lambda, w_out=w_out,
             ln_post_g=ln_post_g, ln_post_b=ln_post_b)
    mom_m = dict(zip(_WEIGHTS, (m_meta_tokens, m_ln_in_g, m_ln_in_b, m_w_in, m_conv_dw_w, m_conv_dw_b, m_conv_ln_g,
                                m_conv_ln_b, m_conv_pw_w, m_conv_pw_b, m_attn_sinks, m_lru_conv_w, m_lru_conv_b,
                                m_lru_wa, m_lru_ba, m_lru_wx, m_lru_bx, m_lru_lambda, m_w_out, m_ln_post_g,
                                m_ln_post_b)))
    mom_v = dict(zip(_WEIGHTS, (v_meta_tokens, v_ln_in_g, v_ln_in_b, v_w_in, v_conv_dw_w, v_conv_dw_b, v_conv_ln_g,
                                v_conv_ln_b, v_conv_pw_w, v_conv_pw_b, v_attn_sinks, v_lru_conv_w, v_lru_conv_b,
                                v_lru_wa, v_lru_ba, v_lru_wx, v_lru_bx, v_lru_lambda, v_w_out, v_ln_post_g,
                                v_ln_post_b)))
    xi, yi, ci = _position()
    j = 2 * xi + yi

    g_meta, g_dw, g_lc = _gather_shards([meta_tokens, conv_dw_w, lru_conv_w])
    p = dict(w)
    p["w_in"] = [_cast_into_slot(w_in, l, j, "w_in") for l in range(DEPTH)]
    p["w_out"] = [_cast_into_slot(w_out, l, j, "w_out") for l in range(DEPTH)]
    p["conv_pw_w"] = [_cast_into_slot(conv_pw_w, l, j, "conv_pw_w") for l in range(DEPTH)]
    p["meta_tokens"] = g_meta.transpose(1, 0, 2).reshape(N_META, D)
    p["conv_dw_w"] = g_dw.transpose(1, 2, 0, 3).reshape(DEPTH, CONV_K, CW)
    p["lru_conv_w"] = g_lc.transpose(1, 2, 0, 3).reshape(DEPTH, LRU_K, LW)

    others = jnp.stack([jnp.where(j <= 0, 1, 0), jnp.where(j <= 1, 2, 1), jnp.where(j <= 2, 3, 2), ci]).astype(jnp.int32)
    me = 4 * xi + 2 * yi + ci
    loss_part, grad_x, g = _device_step(x[0], loss_target[0], p, dist=(ci, j, others, me))
    loss = lax.psum(jnp.sum(loss_part), ("x", "y", "c"))
    big = {(name, l): g[name, l] for name in _BIG for l in range(DEPTH)}

    small_names = [n for n in _WEIGHTS if n not in _BIG]
    small_g = {}
    for names, red in ((_SMALL_LAYERED, g["pack_layered", -1]), (_SMALL_EMBED, g["pack_embed", -1])):
        red = red.reshape(-1)
        off = 0
        for n in names:
            fshape = list(w[n].shape)
            if n in _SMALL_SHARDED:
                fshape[_SMALL_SHARDED[n]] *= N_SHARD
            sz = 1
            for dim in fshape:
                sz *= dim
            full = red[off:off + sz].reshape(fshape)
            off += sz
            if n in _SMALL_SHARDED:
                ax = _SMALL_SHARDED[n]
                full = lax.dynamic_slice_in_dim(full, j * w[n].shape[ax], w[n].shape[ax], axis=ax)
            small_g[n] = full

    out_g, out_d, out_m, out_v = {}, {}, {}, {}
    for name in _BIG:
        shp = w[name].shape
        to3 = lambda a: a.reshape(DEPTH, -1, shp[-1])
        go, do, mo, vo = _adamw_big(to3(w[name]), big[name, 0], big[name, 1], to3(mom_m[name]), to3(mom_v[name]), name)
        out_g[name], out_d[name], out_m[name], out_v[name] = (a.reshape(shp) for a in (go, do, mo, vo))
    ds, ms, vs = _adamw_small([_as2d(w[n]) for n in small_names], [_as2d(small_g[n]) for n in small_names],
                              [_as2d(mom_m[n]) for n in small_names], [_as2d(mom_v[n]) for n in small_names])
    for n, d_, m_, v_ in zip(small_names, ds, ms, vs):
        out_g[n] = small_g[n]
        out_d[n], out_m[n], out_v[n] = d_.reshape(w[n].shape), m_.reshape(w[n].shape), v_.reshape(w[n].shape)

    return (loss, grad_x[None], *[out_g[n] for n in _WEIGHTS], *[out_d[n] for n in _WEIGHTS],
            *[out_m[n] for n in _WEIGHTS], *[out_v[n] for n in _WEIGHTS])
```

```python
import functools

import jax
import jax.numpy as jnp
from jax import lax
from jax.experimental import pallas as pl
from jax.experimental.pallas import tpu as pltpu

F32 = jnp.float32
BF16 = jnp.bfloat16

D = 2048
N_META = 16
CW = 512
CONV_K = 31
AW = 1024
KVW = 256
N_HEADS = 16
LW = 512
LRU_K = 4
LRU_C = 8.0
IN_TOTAL = 5120
ROT_HALF = 8
ROPE_THETA = 500000.0
LN_EPS = 1e-5
DEPTH = 2
ALPHA = (2.0 * DEPTH) ** 0.25
NEG_INF = -1e30
ADAM_LR, ADAM_B1, ADAM_B2, ADAM_EPS, ADAM_WD, ADAM_STEP = 0.001, 0.9, 0.999, 1e-08, 0.01, 10

BLK = 128
PAD = BLK - N_META
N_SHARD = 4
WIN_SH = IN_TOTAL // N_SHARD
WOUT_SH = D // N_SHARD
PW_SH = CW // N_SHARD
HALO = 32
LHALO = 8
V7X_VMEM_LIMIT = 60 * 1024 * 1024


def _cp(*sem):
    return pltpu.CompilerParams(dimension_semantics=sem if sem else None, vmem_limit_bytes=V7X_VMEM_LIMIT)


def _pick(total, prefs):
    for p in prefs:
        if total % p == 0:
            return p
    raise ValueError(f"no tile for {total}")


def _dot(a, b):
    return jnp.dot(a, b, preferred_element_type=F32)


def _dot_nt(a, b):
    return lax.dot_general(a, b, (((1,), (1,)), ((), ())), preferred_element_type=F32)


def _dot_tn(a, b):
    return lax.dot_general(a, b, (((0,), (0,)), ((), ())), preferred_element_type=F32)


def _sigmoid(x):
    return 1.0 / (1.0 + jnp.exp(-x))


def _silu_and_grad(x):
    s = _sigmoid(x)
    return x * s, s * (1.0 + x * (1.0 - s))


def _ln_rows(x, g, b):
    mu = jnp.mean(x, axis=-1, keepdims=True)
    xc = x - mu
    var = jnp.mean(xc * xc, axis=-1, keepdims=True)
    rstd = lax.rsqrt(var + LN_EPS)
    xhat = xc * rstd
    return xhat * g + b, xhat, rstd


def _ln_bwd_rows(dy, xhat, rstd, g):
    dxh = dy * g
    m1 = jnp.mean(dxh, axis=-1, keepdims=True)
    m2 = jnp.mean(dxh * xhat, axis=-1, keepdims=True)
    return rstd * (dxh - m1 - xhat * m2)


def _row_ids(n, base):
    return base + lax.broadcasted_iota(jnp.int32, (n, 1), 0)


def _colsum(x):
    return jnp.sum(x, axis=0, keepdims=True)


def _embed_fwd(x, meta, g, b, job=None):
    S = x.shape[0]
    nb = S // BLK + 1

    def body(x_ref, meta_ref, g_ref, b_ref, h_ref, hb_ref):
        n = pl.program_id(0)

        @pl.when(n == 0)
        def _():
            y, _, _ = _ln_rows(meta_ref[...], g_ref[...], b_ref[...])
            h_ref[...] = jnp.zeros_like(h_ref)
            h_ref[PAD:BLK, :] = y

        @pl.when(n > 0)
        def _():
            y, _, _ = _ln_rows(x_ref[...], g_ref[...], b_ref[...])
            h_ref[...] = y

        hb_ref[...] = h_ref[...].astype(BF16)

    return _side_call(
        body, job, name="embed_fwd", grid=(nb,),
        in_specs=[pl.BlockSpec((BLK, D), lambda n: (jnp.maximum(n - 1, 0), 0)),
                  pl.BlockSpec((N_META, D), lambda n: (0, 0)),
                  pl.BlockSpec((1, D), lambda n: (0, 0)),
                  pl.BlockSpec((1, D), lambda n: (0, 0))],
        out_specs=[pl.BlockSpec((BLK, D), lambda n: (n, 0)),
                   pl.BlockSpec((BLK, D), lambda n: (n, 0))],
        out_shape=[jax.ShapeDtypeStruct((nb * BLK, D), F32), jax.ShapeDtypeStruct((nb * BLK, D), BF16)],
        scratch_shapes=[], semantics=("arbitrary",), args=[x, meta, g, b])


def _embed_bwd(dh, x, meta, g, b):
    S = x.shape[0]
    nb = S // BLK + 1

    def body(dh_ref, x_ref, meta_ref, g_ref, b_ref, gx_ref, gm_ref, dg_ref, db_ref):
        n = pl.program_id(0)

        @pl.when(n == 0)
        def _():
            _, xhat, rstd = _ln_rows(meta_ref[...], g_ref[...], b_ref[...])
            dy = dh_ref[PAD:BLK, :]
            gm_ref[...] = _ln_bwd_rows(dy, xhat, rstd, g_ref[...])
            dg_ref[...] = _colsum(dy * xhat)
            db_ref[...] = _colsum(dy)

        @pl.when(n > 0)
        def _():
            _, xhat, rstd = _ln_rows(x_ref[...], g_ref[...], b_ref[...])
            dy = dh_ref[...]
            gx_ref[...] = _ln_bwd_rows(dy, xhat, rstd, g_ref[...])
            dg_ref[...] += _colsum(dy * xhat)
            db_ref[...] += _colsum(dy)

    prev = lambda n: (jnp.maximum(n - 1, 0), 0)
    const = lambda n: (0, 0)
    return pl.pallas_call(
        body, name="embed_bwd", grid=(nb,),
        in_specs=[pl.BlockSpec((BLK, D), lambda n: (n, 0)),
                  pl.BlockSpec((BLK, D), prev),
                  pl.BlockSpec((N_META, D), const),
                  pl.BlockSpec((1, D), const),
                  pl.BlockSpec((1, D), const)],
        out_specs=[pl.BlockSpec((BLK, D), prev),
                   pl.BlockSpec((N_META, D), const),
                   pl.BlockSpec((1, D), const),
                   pl.BlockSpec((1, D), const)],
        out_shape=[jax.ShapeDtypeStruct((S, D), F32), jax.ShapeDtypeStruct((N_META, D), F32),
                   jax.ShapeDtypeStruct((1, D), F32), jax.ShapeDtypeStruct((1, D), F32)],
        compiler_params=_cp("arbitrary"),
    )(dh, x, meta, g, b)


def _loss_head(h, target):
    T = h.shape[0]
    nb = T // BLK

    def body(h_ref, t_ref, part_ref, dy_ref):
        n = pl.program_id(0)

        @pl.when(n == 0)
        def _():
            part_ref[...] = jnp.zeros_like(part_ref)
            dy_ref[...] = jnp.zeros_like(dy_ref)

        @pl.when(n > 0)
        def _():
            err = h_ref[...] - t_ref[...]
            part_ref[...] += _colsum(err * err) * (0.5 / D)
            dy_ref[...] = err * (1.0 / D)

    return pl.pallas_call(
        body, name="loss_head", grid=(nb,),
        in_specs=[pl.BlockSpec((BLK, D), lambda n: (n, 0)),
                  pl.BlockSpec((BLK, D), lambda n: (jnp.maximum(n - 1, 0), 0))],
        out_specs=[pl.BlockSpec((1, D), lambda n: (0, 0)),
                   pl.BlockSpec((BLK, D), lambda n: (n, 0))],
        out_shape=[jax.ShapeDtypeStruct((1, D), F32), jax.ShapeDtypeStruct((T, D), F32)],
        compiler_params=_cp("arbitrary"),
    )(h, target)


def _proj_fwd(hb, w_in, l, job=None):
    T = hb.shape[0]
    tm = _pick(T, (1056, 384, 128))

    def body(a_ref, w_ref, o_ref):
        o_ref[...] = _dot(a_ref[...], w_ref[...])

    return _side_call(
        body, job, name=f"proj_fwd{l}", grid=(T // tm, N_SHARD),
        in_specs=[pl.BlockSpec((tm, D), lambda i, j: (i, 0)),
                  pl.BlockSpec((None, D, WIN_SH), lambda i, j: (j, 0, 0))],
        out_specs=[pl.BlockSpec((tm, WIN_SH), lambda i, j: (i, j))],
        out_shape=[jax.ShapeDtypeStruct((T, IN_TOTAL), F32)],
        scratch_shapes=[], semantics=("parallel", "arbitrary"), args=[hb, w_in])


def _out_fwd(yc, ya, yl, w_out, h, g, b, l):
    T = h.shape[0]
    tm = _pick(T, (384, 128))

    def body(yc_ref, ya_ref, yl_ref, w_ref, h_ref, g_ref, b_ref, hn_ref, hnb_ref, xh_ref, rs_ref):
        acc = _dot(yc_ref[...], w_ref[0])
        acc += _dot(ya_ref[:, 0:WOUT_SH], w_ref[1])
        acc += _dot(ya_ref[:, WOUT_SH:2 * WOUT_SH], w_ref[2])
        acc += _dot(yl_ref[...], w_ref[3])
        z = ALPHA * h_ref[...] + acc
        y, xhat, rstd = _ln_rows(z, g_ref[...], b_ref[...])
        hn_ref[...] = y
        hnb_ref[...] = y.astype(BF16)
        xh_ref[...] = xhat
        rs_ref[...] = rstd

    row = lambda i: (i, 0)
    return pl.pallas_call(
        body, name=f"out_fwd{l}", grid=(T // tm,),
        in_specs=[pl.BlockSpec((tm, CW), row), pl.BlockSpec((tm, AW), row), pl.BlockSpec((tm, LW), row),
                  pl.BlockSpec((N_SHARD, WOUT_SH, D), lambda i: (0, 0, 0)),
                  pl.BlockSpec((tm, D), row),
                  pl.BlockSpec((None, 1, D), lambda i: (l, 0, 0)),
                  pl.BlockSpec((None, 1, D), lambda i: (l, 0, 0))],
        out_specs=[pl.BlockSpec((tm, D), row), pl.BlockSpec((tm, D), row), pl.BlockSpec((tm, D), row),
                   pl.BlockSpec((tm, 1), row)],
        out_shape=[jax.ShapeDtypeStruct((T, D), F32), jax.ShapeDtypeStruct((T, D), BF16),
                   jax.ShapeDtypeStruct((T, D), F32), jax.ShapeDtypeStruct((T, 1), F32)],
        compiler_params=_cp("parallel"),
    )(yc, ya, yl, w_out, h, g, b)


def _post_ln_bwd(dhn, xhat, rstd, g, l):
    T = dhn.shape[0]
    tm = _pick(T, (384, 128))

    def body(d_ref, xh_ref, rs_ref, g_ref, dz_ref, dzb_ref, dg_ref, db_ref):
        @pl.when(pl.program_id(0) == 0)
        def _():
            dg_ref[...] = jnp.zeros_like(dg_ref)
            db_ref[...] = jnp.zeros_like(db_ref)

        dy = d_ref[...]
        xhat = xh_ref[...]
        dz = _ln_bwd_rows(dy, xhat, rs_ref[...], g_ref[...])
        dz_ref[...] = dz
        dzb_ref[...] = dz.astype(BF16)
        dg_ref[...] += _colsum(dy * xhat)
        db_ref[...] += _colsum(dy)

    row = lambda i: (i, 0)
    const = lambda i: (0, 0)
    return pl.pallas_call(
        body, name=f"post_ln_bwd{l}", grid=(T // tm,),
        in_specs=[pl.BlockSpec((tm, D), row), pl.BlockSpec((tm, D), row), pl.BlockSpec((tm, 1), row),
                  pl.BlockSpec((None, 1, D), lambda i: (l, 0, 0))],
        out_specs=[pl.BlockSpec((tm, D), row), pl.BlockSpec((tm, D), row),
                   pl.BlockSpec((1, D), const), pl.BlockSpec((1, D), const)],
        out_shape=[jax.ShapeDtypeStruct((T, D), F32), jax.ShapeDtypeStruct((T, D), BF16),
                   jax.ShapeDtypeStruct((1, D), F32), jax.ShapeDtypeStruct((1, D), F32)],
        compiler_params=_cp("arbitrary"),
    )(dhn, xhat, rstd, g)


def _dcat_bwd(dzb, w_out, l, job=None):
    T = dzb.shape[0]
    tm = _pick(T, (384, 128))

    def body(dz_ref, w_ref, dc_ref, da_ref, dl_ref):
        dz = dz_ref[...]
        dc_ref[...] = _dot_nt(dz, w_ref[0])
        da_ref[:, 0:WOUT_SH] = _dot_nt(dz, w_ref[1])
        da_ref[:, WOUT_SH:2 * WOUT_SH] = _dot_nt(dz, w_ref[2])
        dl_ref[...] = _dot_nt(dz, w_ref[3])

    row = lambda i: (i, 0)
    return _side_call(
        body, job, name=f"dcat_bwd{l}", grid=(T // tm,),
        in_specs=[pl.BlockSpec((tm, D), row),
                  pl.BlockSpec((N_SHARD, WOUT_SH, D), lambda i: (0, 0, 0))],
        out_specs=[pl.BlockSpec((tm, CW), row), pl.BlockSpec((tm, AW), row), pl.BlockSpec((tm, LW), row)],
        out_shape=[jax.ShapeDtypeStruct((T, CW), F32), jax.ShapeDtypeStruct((T, AW), F32),
                   jax.ShapeDtypeStruct((T, LW), F32)],
        scratch_shapes=[], semantics=("parallel",), args=[dzb, w_out])


def _dwout_bwd(yc, ya, yl, dzb, l):
    T = dzb.shape[0]
    tm = _pick(T, (1056, 384, 128))
    hr = WOUT_SH // 2
    nt = T // tm

    def body(yc_ref, ya_ref, yl_ref, dz_ref, o_ref):
        j = pl.program_id(0)
        t = pl.program_id(2)

        @pl.when(t == 0)
        def _():
            o_ref[...] = jnp.zeros_like(o_ref)

        dz = dz_ref[...]

        @pl.when(j == 0)
        def _():
            o_ref[...] += _dot_tn(yc_ref[...], dz)

        @pl.when((j == 1) | (j == 2))
        def _():
            o_ref[...] += _dot_tn(ya_ref[...], dz)

        @pl.when(j == 3)
        def _():
            o_ref[...] += _dot_tn(yl_ref[...], dz)

    return pl.pallas_call(
        body, name=f"dwout_bwd{l}", grid=(N_SHARD, 2, nt),
        in_specs=[pl.BlockSpec((tm, hr), lambda j, r, t: (t, r)),
                  pl.BlockSpec((tm, hr), lambda j, r, t: (t, 2 * jnp.clip(j - 1, 0, 1) + r)),
                  pl.BlockSpec((tm, hr), lambda j, r, t: (t, r)),
                  pl.BlockSpec((tm, D), lambda j, r, t: (t, 0))],
        out_specs=pl.BlockSpec((None, None, hr, D), lambda j, r, t: (j, r, 0, 0)),
        out_shape=jax.ShapeDtypeStruct((N_SHARD, 2, hr, D), F32),
        compiler_params=_cp("parallel", "parallel", "arbitrary"),
    )(yc, ya, yl, dzb)


def _dh_bwd(dproj, w_in, dz, l, job=None):
    T = dproj.shape[0]
    tm = _pick(T, (1056, 384, 128))

    def body(dp_ref, w_ref, dz_ref, o_ref, acc_ref):
        j = pl.program_id(1)

        @pl.when(j == 0)
        def _():
            acc_ref[...] = ALPHA * dz_ref[...]

        acc_ref[...] += _dot_nt(dp_ref[...], w_ref[...])

        @pl.when(j == N_SHARD - 1)
        def _():
            o_ref[...] = acc_ref[...]

    return _side_call(
        body, job, name=f"dh_bwd{l}", grid=(T // tm, N_SHARD),
        in_specs=[pl.BlockSpec((tm, WIN_SH), lambda i, j: (i, j)),
                  pl.BlockSpec((None, D, WIN_SH), lambda i, j: (j, 0, 0)),
                  pl.BlockSpec((tm, D), lambda i, j: (i, 0))],
        out_specs=[pl.BlockSpec((tm, D), lambda i, j: (i, 0))],
        out_shape=[jax.ShapeDtypeStruct((T, D), F32)],
        scratch_shapes=[pltpu.VMEM((tm, D), F32)],
        semantics=("parallel", "arbitrary"), args=[dproj, w_in, dz])


def _dwin_bwd(hb, dproj, l):
    T = hb.shape[0]
    tm = _pick(T, (1056, 384, 128))
    hr = D // 2

    def body(h_ref, dp_ref, o_ref):
        @pl.when(pl.program_id(2) == 0)
        def _():
            o_ref[...] = jnp.zeros_like(o_ref)

        o_ref[...] += _dot_tn(h_ref[...], dp_ref[...])

    return pl.pallas_call(
        body, name=f"dwin_bwd{l}", grid=(N_SHARD, 2, T // tm),
        in_specs=[pl.BlockSpec((tm, hr), lambda j, r, t: (t, r)),
                  pl.BlockSpec((tm, WIN_SH), lambda j, r, t: (t, j))],
        out_specs=pl.BlockSpec((None, None, hr, WIN_SH), lambda j, r, t: (j, r, 0, 0)),
        out_shape=jax.ShapeDtypeStruct((N_SHARD, 2, hr, WIN_SH), F32),
        compiler_params=_cp("parallel", "parallel", "arbitrary"),
    )(hb, dproj)


def _glu_masked(v, g, base_row):
    rows = _row_ids(v.shape[0], base_row)
    return jnp.where(rows >= PAD, v * _sigmoid(g), 0.0)


def _conv_tile(T):
    return _pick(T, (384, 128))


SUBLANES = 8


def _for_each_shift(buf, rot, tm, offsets, fn):
    for r in range(SUBLANES):
        group = [o for o in offsets if o % SUBLANES == r]
        if not group:
            continue
        if r == 0:
            src = buf
        else:
            n = tm + max(group) - r
            rot[0:n, :] = buf[r:r + n, :]
            src = rot
        for o in group:
            fn(o, src[o - r:o - r + tm, :])


def _conv_fwd(proj, dw_w, dw_b, ln_g, ln_b, pw_w, pw_b, l):
    T = proj.shape[0]
    tm = _conv_tile(T)
    hb = tm // HALO

    def body(cv_ref, cg_ref, ct_ref, hv_ref, hg_ref, w_ref, b_ref, g_ref, be_ref, pw_ref, pb_ref,
             yc_ref, conv_ref, buf, rot):
        i = pl.program_id(0)
        buf[0:HALO, :] = _glu_masked(hv_ref[...], hg_ref[...], i * tm - HALO)
        buf[HALO:HALO + tm, :] = _glu_masked(cv_ref[...], cg_ref[...], i * tm)
        first = HALO - (CONV_K - 1)
        total = [jnp.zeros((tm, CW), F32) + b_ref[...]]

        def tap(o, tile):
            k = o - first
            total[0] = total[0] + w_ref[k:k + 1, :] * tile

        _for_each_shift(buf, rot, tm, [first + k for k in range(CONV_K)], tap)
        acc = total[0]
        conv_ref[...] = acc
        u, _, _ = _ln_rows(acc, g_ref[...], be_ref[...])
        s = u * _sigmoid(u)
        cpw = _dot(s.astype(BF16), pw_ref[...]) + pb_ref[...]
        gate, _ = _silu_and_grad(ct_ref[...])
        yc_ref[...] = (cpw * gate).astype(BF16)

    vec = pl.BlockSpec((None, 1, CW), lambda i: (l, 0, 0))
    return pl.pallas_call(
        body, name=f"conv_fwd{l}", grid=(T // tm,),
        in_specs=[pl.BlockSpec((tm, CW), lambda i: (i, 0)),
                  pl.BlockSpec((tm, CW), lambda i: (i, 1)),
                  pl.BlockSpec((tm, CW), lambda i: (i, 2)),
                  pl.BlockSpec((HALO, CW), lambda i: (jnp.maximum(i * hb - 1, 0), 0)),
                  pl.BlockSpec((HALO, CW), lambda i: (jnp.maximum(i * hb - 1, 0), 1)),
                  pl.BlockSpec((None, CONV_K, CW), lambda i: (l, 0, 0)),
                  vec, vec, vec,
                  pl.BlockSpec((CW, CW), lambda i: (0, 0)),
                  vec],
        out_specs=[pl.BlockSpec((tm, CW), lambda i: (i, 0)), pl.BlockSpec((tm, CW), lambda i: (i, 0))],
        out_shape=[jax.ShapeDtypeStruct((T, CW), BF16), jax.ShapeDtypeStruct((T, CW), F32)],
        scratch_shapes=[pltpu.VMEM((tm + HALO, CW), F32), pltpu.VMEM((tm + HALO, CW), F32)],
        compiler_params=_cp("parallel"),
    )(proj, proj, proj, proj, proj, dw_w, dw_b, ln_g, ln_b, pw_w, pw_b)


def _conv_bwd_rows(conv, proj, d_yc, ln_g, ln_b, pw_w, pw_b, l):
    T = conv.shape[0]
    tm = _conv_tile(T)

    def body(conv_ref, ct_ref, dy_ref, g_ref, be_ref, pw_ref, pb_ref,
             dconv_ref, dct_ref, dpw_ref, dpb_ref, dg_ref, db_ref):
        @pl.when(pl.program_id(0) == 0)
        def _():
            dpw_ref[...] = jnp.zeros_like(dpw_ref)
            dpb_ref[...] = jnp.zeros_like(dpb_ref)
            dg_ref[...] = jnp.zeros_like(dg_ref)
            db_ref[...] = jnp.zeros_like(db_ref)

        u, xhat, rstd = _ln_rows(conv_ref[...], g_ref[...], be_ref[...])
        s, ds_du = _silu_and_grad(u)
        sb = s.astype(BF16)
        cpw = _dot(sb, pw_ref[...]) + pb_ref[...]
        gate, dgate = _silu_and_grad(ct_ref[...])
        dy = dy_ref[...]
        d_cpw = dy * gate
        dct_ref[...] = (dy * cpw * dgate).astype(BF16)
        d_cpw_b = d_cpw.astype(BF16)
        dpb_ref[...] += _colsum(d_cpw)
        dpw_ref[...] += _dot_tn(sb, d_cpw_b)
        du = _dot_nt(d_cpw_b, pw_ref[...]) * ds_du
        dconv_ref[...] = _ln_bwd_rows(du, xhat, rstd, g_ref[...])
        dg_ref[...] += _colsum(du * xhat)
        db_ref[...] += _colsum(du)

    vec = pl.BlockSpec((None, 1, CW), lambda i: (l, 0, 0))
    row = lambda i: (i, 0)
    const = lambda i: (0, 0)
    return pl.pallas_call(
        body, name=f"conv_bwd_rows{l}", grid=(T // tm,),
        in_specs=[pl.BlockSpec((tm, CW), row), pl.BlockSpec((tm, CW), lambda i: (i, 2)),
                  pl.BlockSpec((tm, CW), row), vec, vec,
                  pl.BlockSpec((CW, CW), lambda i: (0, 0)), vec],
        out_specs=[pl.BlockSpec((tm, CW), row), pl.BlockSpec((tm, CW), lambda i: (i, 2)),
                   pl.BlockSpec((CW, CW), const), pl.BlockSpec((1, CW), const),
                   pl.BlockSpec((1, CW), const), pl.BlockSpec((1, CW), const)],
        out_shape=[jax.ShapeDtypeStruct((T, CW), F32), jax.ShapeDtypeStruct((T, IN_TOTAL), BF16),
                   jax.ShapeDtypeStruct((CW, CW), F32), jax.ShapeDtypeStruct((1, CW), F32),
                   jax.ShapeDtypeStruct((1, CW), F32), jax.ShapeDtypeStruct((1, CW), F32)],
        compiler_params=_cp("arbitrary"),
    )(conv, proj, d_yc, ln_g, ln_b, pw_w, pw_b)


def _conv_bwd_taps(d_conv, proj, dw_w, dproj, l, job=None):
    T = d_conv.shape[0]
    tm = _conv_tile(T)
    hb = tm // HALO
    nt = T // tm
    last_halo = T // HALO - 1

    def body(dc_ref, dh_ref, cv_ref, cg_ref, hv_ref, hg_ref, w_ref, _, o_ref, dw_ref, dwb_ref, cbuf, dbuf, rot):
        i = pl.program_id(0)

        @pl.when(i == 0)
        def _():
            dw_ref[...] = jnp.zeros_like(dw_ref)
            dwb_ref[...] = jnp.zeros_like(dwb_ref)

        cbuf[0:HALO, :] = _glu_masked(hv_ref[...], hg_ref[...], i * tm - HALO)
        cbuf[HALO:HALO + tm, :] = _glu_masked(cv_ref[...], cg_ref[...], i * tm)
        dmain = dc_ref[...]
        dbuf[0:tm, :] = dmain
        dbuf[tm:tm + HALO, :] = jnp.where(i < nt - 1, dh_ref[...], 0.0)
        total = [jnp.zeros((tm, CW), F32)]

        def tap_back(o, tile):
            k = CONV_K - 1 - o
            total[0] = total[0] + w_ref[k:k + 1, :] * tile

        _for_each_shift(dbuf, rot, tm, list(range(CONV_K)), tap_back)
        acc = total[0]
        first = HALO - (CONV_K - 1)

        def tap_weight(o, tile):
            k = o - first
            dw_ref[k:k + 1, :] += _colsum(dmain * tile)

        _for_each_shift(cbuf, rot, tm, [first + k for k in range(CONV_K)], tap_weight)
        dwb_ref[...] += _colsum(dmain)
        d_c = jnp.where(_row_ids(tm, i * tm) >= PAD, acc, 0.0)
        sig = _sigmoid(cg_ref[...])
        o_ref[:, 0:CW] = (d_c * sig).astype(BF16)
        o_ref[:, CW:2 * CW] = (d_c * cv_ref[...] * sig * (1.0 - sig)).astype(BF16)

    const = lambda i: (0, 0)
    return _side_call(
        body, job, name=f"conv_bwd_taps{l}", grid=(nt,),
        in_specs=[pl.BlockSpec((tm, CW), lambda i: (i, 0)),
                  pl.BlockSpec((HALO, CW), lambda i: (jnp.minimum((i + 1) * hb, last_halo), 0)),
                  pl.BlockSpec((tm, CW), lambda i: (i, 0)),
                  pl.BlockSpec((tm, CW), lambda i: (i, 1)),
                  pl.BlockSpec((HALO, CW), lambda i: (jnp.maximum(i * hb - 1, 0), 0)),
                  pl.BlockSpec((HALO, CW), lambda i: (jnp.maximum(i * hb - 1, 0), 1)),
                  pl.BlockSpec((None, CONV_K, CW), lambda i: (l, 0, 0)),
                  pl.BlockSpec(memory_space=pl.ANY)],
        out_specs=[pl.BlockSpec((tm, 2 * CW), lambda i: (i, 0)),
                   pl.BlockSpec((HALO, CW), const), pl.BlockSpec((1, CW), const)],
        out_shape=[jax.ShapeDtypeStruct(dproj.shape, BF16), jax.ShapeDtypeStruct((HALO, CW), F32),
                   jax.ShapeDtypeStruct((1, CW), F32)],
        scratch_shapes=[pltpu.VMEM((tm + HALO, CW), F32), pltpu.VMEM((tm + HALO, CW), F32),
                        pltpu.VMEM((tm + HALO, CW), F32)],
        semantics=("arbitrary",), aliases={7: 0},
        args=[d_conv, d_conv, proj, proj, proj, proj, dw_w, dproj])


def _log1p_small(e):
    return jnp.where(e < 1e-3, e * (1.0 - e * (0.5 - e * (1.0 / 3.0))), jnp.log(1.0 + e))


def _softplus(z):
    return jnp.maximum(z, 0.0) + _log1p_small(jnp.exp(-jnp.abs(z)))


def _neg_expm1(x):
    series = -x * (1.0 + x * (1.0 / 2.0) * (1.0 + x * (1.0 / 3.0) * (1.0 + x * (1.0 / 4.0) * (
        1.0 + x * (1.0 / 5.0) * (1.0 + x * (1.0 / 6.0) * (1.0 + x * (1.0 / 7.0)))))))
    return jnp.where(x > -0.25, series, 1.0 - jnp.exp(x))


def _lru_gates(rxbuf, tm, base_row, lw_ref, lb_ref, wa_ref, ba_ref, wx_ref, bx_ref, lam_ref):
    rc = jnp.zeros((tm, LW), F32) + lb_ref[...]
    for k in range(LRU_K):
        o = LHALO - (LRU_K - 1) + k
        rc += lw_ref[k:k + 1, :] * rxbuf[o:o + tm, :]
    rcb = rc.astype(BF16)
    r = _sigmoid(_dot(rcb, wa_ref[...]) + ba_ref[...])
    ig = _sigmoid(_dot(rcb, wx_ref[...]) + bx_ref[...])
    sp = _softplus(-lam_ref[...])
    la = -LRU_C * r * sp
    a = jnp.exp(la)
    mult = jnp.sqrt(_neg_expm1(2.0 * la))
    valid = _row_ids(tm, base_row) >= PAD
    return rc, rcb, r, ig, sp, a, mult, valid


def _mask_rows(v, base_row):
    return jnp.where(_row_ids(v.shape[0], base_row) >= PAD, v, 0.0)


def _scan_steps(tm):
    s, out = 1, []
    while s < tm:
        out.append(s)
        s *= 2
    return out


def _lru_tile(T):
    return _pick(T, (384, 128))


def _lru_fwd(proj, lw, lb, wa, ba, wx, bx, lam, l):
    T = proj.shape[0]
    tm = _lru_tile(T)
    hb = tm // LHALO

    def body(rx_ref, rg_ref, hx_ref, lw_ref, lb_ref, wa_ref, ba_ref, wx_ref, bx_ref, lam_ref,
             yl_ref, hl_ref, rxbuf, carry):
        i = pl.program_id(0)

        @pl.when(i == 0)
        def _():
            carry[...] = jnp.zeros_like(carry)

        rxbuf[0:LHALO, :] = _mask_rows(hx_ref[...], i * tm - LHALO)
        rxbuf[LHALO:LHALO + tm, :] = _mask_rows(rx_ref[...], i * tm)
        rc, _, _, ig, _, a, mult, valid = _lru_gates(rxbuf, tm, i * tm, lw_ref, lb_ref, wa_ref, ba_ref,
                                                     wx_ref, bx_ref, lam_ref)
        bb = jnp.where(valid, mult * (ig * rc), 0.0)
        aa = a
        rows = _row_ids(tm, 0)
        for s in _scan_steps(tm):
            keep = rows >= s
            a_s = jnp.where(keep, pltpu.roll(aa, s, axis=0), 1.0)
            b_s = jnp.where(keep, pltpu.roll(bb, s, axis=0), 0.0)
            bb = aa * b_s + bb
            aa = aa * a_s
        h = bb + aa * carry[0:1, :]
        hl_ref[...] = h
        carry[0:1, :] = hl_ref[tm - 1:tm, :]
        gate, _ = _silu_and_grad(rg_ref[...])
        yl_ref[...] = (h * gate).astype(BF16)

    vec = pl.BlockSpec((None, 1, LW), lambda i: (l, 0, 0))
    mat = pl.BlockSpec((None, LW, LW), lambda i: (l, 0, 0))
    return pl.pallas_call(
        body, name=f"lru_fwd{l}", grid=(T // tm,),
        in_specs=[pl.BlockSpec((tm, LW), lambda i: (i, 8)),
                  pl.BlockSpec((tm, LW), lambda i: (i, 9)),
                  pl.BlockSpec((LHALO, LW), lambda i: (jnp.maximum(i * hb - 1, 0), 8)),
                  pl.BlockSpec((None, LRU_K, LW), lambda i: (l, 0, 0)),
                  vec, mat, vec, mat, vec, vec],
        out_specs=[pl.BlockSpec((tm, LW), lambda i: (i, 0)), pl.BlockSpec((tm, LW), lambda i: (i, 0))],
        out_shape=[jax.ShapeDtypeStruct((T, LW), BF16), jax.ShapeDtypeStruct((T, LW), F32)],
        scratch_shapes=[pltpu.VMEM((tm + LHALO, LW), F32), pltpu.VMEM((8, LW), F32)],
        compiler_params=_cp("arbitrary"),
    )(proj, proj, proj, lw, lb, wa, ba, wx, bx, lam)


def _lru_bwd(proj, hl, d_yl, lw, lb, wa, ba, wx, bx, lam, dproj, l, job=None):
    T = proj.shape[0]
    tm = _lru_tile(T)
    hb = tm // LHALO
    nt = T // tm

    def body(rx_ref, rg_ref, hx_ref, hl_ref, hh_ref, dy_ref, lw_ref, lb_ref, wa_ref, ba_ref, wx_ref, bx_ref,
             lam_ref, _, o_ref, dlw_ref, dlb_ref, dwa_ref, dba_ref, dwx_ref, dbx_ref, dlam_ref,
             rxbuf, dbuf, carry, head):
        step = pl.program_id(0)
        i = nt - 1 - step

        @pl.when(step == 0)
        def _():
            carry[...] = jnp.zeros_like(carry)
            head[...] = jnp.zeros_like(head)
            for ref in (dlw_ref, dlb_ref, dwa_ref, dba_ref, dwx_ref, dbx_ref, dlam_ref):
                ref[...] = jnp.zeros_like(ref)

        rxbuf[0:LHALO, :] = _mask_rows(hx_ref[...], i * tm - LHALO)
        rxbuf[LHALO:LHALO + tm, :] = _mask_rows(rx_ref[...], i * tm)
        rc, rcb, r, ig, sp, a, mult, valid = _lru_gates(rxbuf, tm, i * tm, lw_ref, lb_ref, wa_ref, ba_ref,
                                                        wx_ref, bx_ref, lam_ref)
        rows = _row_ids(tm, 0)
        h = hl_ref[...]
        h_before = jnp.where(i > 0, hh_ref[LHALO - 1:LHALO, :], 0.0)
        hprev = jnp.where(rows == 0, h_before, pltpu.roll(h, 1, axis=0))
        rg = rg_ref[...]
        gate, dgate = _silu_and_grad(rg)
        dy = dy_ref[...]
        o_ref[:, LW:2 * LW] = (dy * h * dgate).astype(BF16)
        bb = dy * gate + jnp.where(rows == tm - 1, carry[0:1, :], 0.0)
        aa = jnp.where(rows == tm - 1, 0.0, pltpu.roll(a, tm - 1, axis=0))
        for s in _scan_steps(tm):
            keep = rows < tm - s
            a_s = jnp.where(keep, pltpu.roll(aa, tm - s, axis=0), 1.0)
            b_s = jnp.where(keep, pltpu.roll(bb, tm - s, axis=0), 0.0)
            bb = aa * b_s + bb
            aa = aa * a_s
        g = bb
        dbuf[0:tm, :] = a * g
        carry[0:1, :] = dbuf[0:1, :]
        du = jnp.where(valid, g, 0.0)
        da = g * hprev
        dix = du * mult
        dmult = du * (ig * rc)
        dla = jnp.where(valid, da * a - dmult * (a * a) / mult, 0.0)
        dr = dla * (-LRU_C * sp)
        dlam_ref[...] += _colsum(dla * (LRU_C * r)) * _sigmoid(-lam_ref[...])
        dpa = dr * r * (1.0 - r)
        dpx = (dix * rc) * ig * (1.0 - ig)
        dpab = dpa.astype(BF16)
        dpxb = dpx.astype(BF16)
        dba_ref[...] += _colsum(dpa)
        dbx_ref[...] += _colsum(dpx)
        dwa_ref[...] += _dot_tn(rcb, dpab)
        dwx_ref[...] += _dot_tn(rcb, dpxb)
        drc = dix * ig + _dot_nt(dpab, wa_ref[...]) + _dot_nt(dpxb, wx_ref[...])
        dbuf[0:tm, :] = drc
        dbuf[tm:tm + LHALO, :] = head[...]
        acc = jnp.zeros((tm, LW), F32)
        for k in range(LRU_K):
            o = LRU_K - 1 - k
            acc += lw_ref[k:k + 1, :] * dbuf[o:o + tm, :]
            oc = LHALO - (LRU_K - 1) + k
            dlw_ref[k:k + 1, :] += _colsum(drc * rxbuf[oc:oc + tm, :])
        dlb_ref[...] += _colsum(drc)
        head[...] = dbuf[0:LHALO, :]
        o_ref[:, 0:LW] = jnp.where(valid, acc, 0.0).astype(BF16)

    rev = lambda s: nt - 1 - s
    vec = pl.BlockSpec((None, 1, LW), lambda s: (l, 0, 0))
    mat = pl.BlockSpec((None, LW, LW), lambda s: (l, 0, 0))
    const = lambda s: (0, 0)
    halo = lambda s: jnp.maximum(rev(s) * hb - 1, 0)
    return _side_call(
        body, job, name=f"lru_bwd{l}", grid=(nt,),
        in_specs=[pl.BlockSpec((tm, LW), lambda s: (rev(s), 8)),
                  pl.BlockSpec((tm, LW), lambda s: (rev(s), 9)),
                  pl.BlockSpec((LHALO, LW), lambda s: (halo(s), 8)),
                  pl.BlockSpec((tm, LW), lambda s: (rev(s), 0)),
                  pl.BlockSpec((LHALO, LW), lambda s: (halo(s), 0)),
                  pl.BlockSpec((tm, LW), lambda s: (rev(s), 0)),
                  pl.BlockSpec((None, LRU_K, LW), lambda s: (l, 0, 0)),
                  vec, mat, vec, mat, vec, vec, pl.BlockSpec(memory_space=pl.ANY)],
        out_specs=[pl.BlockSpec((tm, 2 * LW), lambda s: (rev(s), 4)),
                   pl.BlockSpec((8, LW), const), pl.BlockSpec((1, LW), const),
                   pl.BlockSpec((LW, LW), const), pl.BlockSpec((1, LW), const),
                   pl.BlockSpec((LW, LW), const), pl.BlockSpec((1, LW), const),
                   pl.BlockSpec((1, LW), const)],
        out_shape=[jax.ShapeDtypeStruct(dproj.shape, BF16),
                   jax.ShapeDtypeStruct((8, LW), F32), jax.ShapeDtypeStruct((1, LW), F32),
                   jax.ShapeDtypeStruct((LW, LW), F32), jax.ShapeDtypeStruct((1, LW), F32),
                   jax.ShapeDtypeStruct((LW, LW), F32), jax.ShapeDtypeStruct((1, LW), F32),
                   jax.ShapeDtypeStruct((1, LW), F32)],
        scratch_shapes=[pltpu.VMEM((tm + LHALO, LW), F32), pltpu.VMEM((tm + LHALO, LW), F32),
                        pltpu.VMEM((8, LW), F32), pltpu.VMEM((LHALO, LW), F32)],
        semantics=("arbitrary",), aliases={13: 0},
        args=[proj, proj, proj, hl, hl, d_yl, lw, lb, wa, ba, wx, bx, lam, dproj])


def _rope_tables(T):
    pos = (lax.broadcasted_iota(jnp.int32, (T, 128), 0) - PAD).astype(F32)
    lane = lax.broadcasted_iota(jnp.int32, (T, 128), 1) % 64
    inv_freq = ROPE_THETA ** (-(lane % ROT_HALF).astype(F32) / ROT_HALF)
    ang = pos * inv_freq
    cos, sin = jnp.cos(ang), jnp.sin(ang)
    c = jnp.where(lane < 2 * ROT_HALF, cos, 1.0)
    s1 = jnp.where(lane < ROT_HALF, -sin, 0.0)
    s2 = jnp.where((lane >= ROT_HALF) & (lane < 2 * ROT_HALF), sin, 0.0)
    return c, s1, s2


def _rot_fwd(x, c, s1, s2):
    return x * c + pltpu.roll(x, 128 - ROT_HALF, axis=1) * s1 + pltpu.roll(x, ROT_HALF, axis=1) * s2


def _rot_bwd(dy, c, s1, s2):
    return dy * c + pltpu.roll(dy * s1, ROT_HALF, axis=1) + pltpu.roll(dy * s2, 128 - ROT_HALF, axis=1)


def _rope_fwd(proj, tabs, l):
    T = proj.shape[0]

    def body(ql_ref, qh_ref, k_ref, v_ref, c_ref, s1_ref, s2_ref, qr_ref, kr_ref, vb_ref):
        c, s1, s2 = c_ref[...], s1_ref[...], s2_ref[...]
        for gcol in range(AW // 128):
            src = ql_ref if gcol < 4 else qh_ref
            x = src[:, 128 * (gcol % 4):128 * (gcol % 4) + 128]
            qr_ref[:, 128 * gcol:128 * gcol + 128] = (_rot_fwd(x, c, s1, s2) * 0.125).astype(BF16)
        for gcol in range(KVW // 128):
            x = k_ref[:, 128 * gcol:128 * gcol + 128]
            kr_ref[:, 128 * gcol:128 * gcol + 128] = _rot_fwd(x, c, s1, s2).astype(BF16)
        vb_ref[...] = v_ref[...].astype(BF16)

    tab = pl.BlockSpec((BLK, 128), lambda n: (n, 0))
    return pl.pallas_call(
        body, name=f"rope_fwd{l}", grid=(T // BLK,),
        in_specs=[pl.BlockSpec((BLK, 512), lambda n: (n, 3)), pl.BlockSpec((BLK, 512), lambda n: (n, 4)),
                  pl.BlockSpec((BLK, KVW), lambda n: (n, 10)), pl.BlockSpec((BLK, KVW), lambda n: (n, 11)),
                  tab, tab, tab],
        out_specs=[pl.BlockSpec((BLK, AW), lambda n: (n, 0)), pl.BlockSpec((BLK, KVW), lambda n: (n, 0)),
                   pl.BlockSpec((BLK, KVW), lambda n: (n, 0))],
        out_shape=[jax.ShapeDtypeStruct((T, AW), BF16), jax.ShapeDtypeStruct((T, KVW), BF16),
                   jax.ShapeDtypeStruct((T, KVW), BF16)],
        compiler_params=_cp("parallel"),
    )(proj, proj, proj, proj, *tabs)


GROUP = 4


def _attn_mask(n, reps):
    qi = lax.broadcasted_iota(jnp.int32, (reps * BLK, BLK), 0) & (BLK - 1)
    kj = lax.broadcasted_iota(jnp.int32, (reps * BLK, BLK), 1)
    m0 = (kj >= PAD) & (n >= 1)
    mp = (kj > qi) & (n >= 2)
    mc = (kj <= qi) & ((n >= 1) | (kj >= PAD))
    return jnp.concatenate([m0, mp, mc], axis=1)


def _kv_both(x0_ref, xp_ref, xc_ref, g):
    pg, off = g // 2, g % 2
    cols = slice(128 * pg, 128 * pg + 128)
    x = jnp.concatenate([x0_ref[:, cols], xp_ref[:, cols], xc_ref[:, cols]], axis=0).astype(F32)
    lane = lax.broadcasted_iota(jnp.int32, (1, 128), 1)
    half = jnp.where((lane < 64) if off == 0 else (lane >= 64), x, 0.0)
    return (half + pltpu.roll(half, 64, axis=1)).astype(BF16)


def _kv_halves(x0_ref, xp_ref, xc_ref, g):
    pg, off = g // 2, g % 2
    cols = slice(128 * pg, 128 * pg + 128)
    x = jnp.concatenate([x0_ref[:, cols], xp_ref[:, cols], xc_ref[:, cols]], axis=0).astype(F32)
    lane = lax.broadcasted_iota(jnp.int32, (1, 128), 1)
    if off == 0:
        lo = jnp.where(lane < 64, x, 0.0)
        hi = pltpu.roll(lo, 64, axis=1)
    else:
        hi = jnp.where(lane >= 64, x, 0.0)
        lo = pltpu.roll(hi, 64, axis=1)
    return lo.astype(BF16), hi.astype(BF16)


def _stack_heads(a, b):
    lo = lax.broadcasted_iota(jnp.int32, (1, 128), 1) < 64
    a, b = a.astype(F32), b.astype(F32)
    return jnp.concatenate([jnp.where(lo, a, 0.0), jnp.where(lo, 0.0, a),
                            jnp.where(lo, b, 0.0), jnp.where(lo, 0.0, b)], axis=0).astype(BF16)


def _unstack_heads(x):
    lo = lax.broadcasted_iota(jnp.int32, (1, 128), 1) < 64
    return (jnp.where(lo, x[0:BLK], x[BLK:2 * BLK]), jnp.where(lo, x[2 * BLK:3 * BLK], x[3 * BLK:4 * BLK]))


def _per_head_column(values):
    return jnp.concatenate([jnp.zeros((BLK, 1), F32) + v for v in values], axis=0)


def _attn_fwd(qr, kr, vb, proj, sinks, l, job=None):
    T = qr.shape[0]

    def body(sink_ref, q_ref, k0_ref, kp_ref, kc_ref, v0_ref, vp_ref, vc_ref, ag_ref, ya_ref, att_ref, lse_ref):
        n = pl.program_id(0)
        mask = _attn_mask(n, 1)
        lane = lax.broadcasted_iota(jnp.int32, (1, 128), 1)
        lse_acc = jnp.zeros((BLK, 128), F32)
        for g in range(4):
            k_lo, k_hi = _kv_halves(k0_ref, kp_ref, kc_ref, g)
            v_lo, v_hi = _kv_halves(v0_ref, vp_ref, vc_ref, g)
            for pp in range(2):
                cols = slice(128 * (2 * g + pp), 128 * (2 * g + pp) + 128)
                qpair = q_ref[:, cols]
                out = jnp.zeros((BLK, 128), F32)
                for hh, (kx, vx) in enumerate(((k_lo, v_lo), (k_hi, v_hi))):
                    h = 4 * g + 2 * pp + hh
                    sink = sink_ref[l, h]
                    s = jnp.where(mask, _dot_nt(qpair, kx), NEG_INF)
                    m = jnp.maximum(jnp.max(s, axis=1, keepdims=True), sink)
                    p = jnp.exp(s - m)
                    denom = jnp.sum(p, axis=1, keepdims=True) + jnp.exp(sink - m)
                    out += _dot((p / denom).astype(BF16), vx)
                    lse_acc = jnp.where(lane == h, m + jnp.log(denom), lse_acc)
                att_ref[:, cols] = out
                gate, _ = _silu_and_grad(ag_ref[:, cols])
                ya_ref[:, cols] = (out * gate).astype(BF16)
        lse_ref[...] = lse_acc

    prev = lambda n: (jnp.maximum(n - 1, 0), 0)
    cur = lambda n: (n, 0)
    zero = lambda n: (0, 0)
    kv = lambda f: pl.BlockSpec((BLK, KVW), f)
    return _side_call(
        body, job, name=f"attn_fwd{l}", grid=(T // BLK,),
        in_specs=[pl.BlockSpec(memory_space=pltpu.SMEM),
                  pl.BlockSpec((BLK, AW), cur), kv(zero), kv(prev), kv(cur), kv(zero), kv(prev), kv(cur),
                  pl.BlockSpec((BLK, AW), lambda n: (n, 3))],
        out_specs=[pl.BlockSpec((BLK, AW), cur), pl.BlockSpec((BLK, AW), cur), pl.BlockSpec((BLK, 128), cur)],
        out_shape=[jax.ShapeDtypeStruct((T, AW), BF16), jax.ShapeDtypeStruct((T, AW), F32),
                   jax.ShapeDtypeStruct((T, 128), F32)],
        scratch_shapes=[], semantics=("parallel",), args=[sinks, qr, kr, kr, kr, vb, vb, vb, proj])


def _attn_bwd(qr, kr, vb, proj, att, lse, d_ya, sinks, dproj, l, job=None):
    T = qr.shape[0]
    nb = T // BLK

    def body(sink_ref, q_ref, k0_ref, kp_ref, kc_ref, v0_ref, vp_ref, vc_ref, ag_ref, att_ref, lse_ref, dy_ref, _,
             dq_ref, dk_ref, dv_ref, dk0_ref, dv0_ref, dag_ref, dsink_ref, kcarry, vcarry):
        n = pl.program_id(0)

        @pl.when(n == 0)
        def _():
            dk0_ref[...] = jnp.zeros_like(dk0_ref)
            dv0_ref[...] = jnp.zeros_like(dv0_ref)
            dsink_ref[...] = jnp.zeros_like(dsink_ref)
            kcarry[...] = jnp.zeros_like(kcarry)
            vcarry[...] = jnp.zeros_like(vcarry)

        @pl.when(n == nb)
        def _():
            dk_ref[...] = kcarry[...]
            dv_ref[...] = vcarry[...]

        @pl.when(n < nb)
        def _():
            mask = _attn_mask(n, GROUP)
            lane = lax.broadcasted_iota(jnp.int32, (1, 128), 1)
            lse = lse_ref[...]
            dsink = jnp.zeros((1, 128), F32)
            dk_pg, dv_pg = [], []
            for pg in range(2):
                dk_acc = jnp.zeros((3 * BLK, 128), F32)
                dv_acc = jnp.zeros((3 * BLK, 128), F32)
                for off in range(2):
                    g = 2 * pg + off
                    kx = _kv_both(k0_ref, kp_ref, kc_ref, g)
                    vx = _kv_both(v0_ref, vp_ref, vc_ref, g)
                    pair_cols = [slice(128 * (2 * g + pp), 128 * (2 * g + pp) + 128) for pp in range(2)]
                    q4 = _stack_heads(q_ref[:, pair_cols[0]], q_ref[:, pair_cols[1]])
                    d_out = []
                    for cols in pair_cols:
                        gate, dgate = _silu_and_grad(ag_ref[:, cols])
                        dy = dy_ref[:, cols]
                        dag_ref[:, cols] = (dy * att_ref[:, cols] * dgate).astype(BF16)
                        d_out.append(dy * gate)
                    do4 = _stack_heads(d_out[0], d_out[1])
                    heads = [GROUP * g + r for r in range(GROUP)]
                    sink = _per_head_column([sink_ref[l, h] for h in heads])
                    lse4 = _per_head_column(
                        [jnp.sum(jnp.where(lane == h, lse, 0.0), axis=1, keepdims=True) for h in heads])
                    p = jnp.where(mask, jnp.exp(_dot_nt(q4, kx) - lse4), 0.0)
                    dp = _dot_nt(do4, vx)
                    delta = jnp.sum(p * dp, axis=1, keepdims=True)
                    ds = (p * (dp - delta)).astype(BF16)
                    sink_term = jnp.exp(sink - lse4) * delta
                    for r, h in enumerate(heads):
                        dsink += jnp.where(lane == h, -jnp.sum(sink_term[BLK * r:BLK * r + BLK]), 0.0)
                    for cols, dq in zip(pair_cols, _unstack_heads(_dot(ds, kx))):
                        dq_ref[:, cols] = dq
                    dkg = _dot_tn(ds, q4)
                    dvg = _dot_tn(p.astype(BF16), do4)
                    own = (lane < 64) if off == 0 else (lane >= 64)
                    dk_acc += jnp.where(own, dkg + pltpu.roll(dkg, 64, axis=1), 0.0)
                    dv_acc += jnp.where(own, dvg + pltpu.roll(dvg, 64, axis=1), 0.0)
                dk_pg.append(dk_acc)
                dv_pg.append(dv_acc)
            dsink_ref[...] += dsink
            for pg in range(2):
                cols = slice(128 * pg, 128 * pg + 128)
                dk0_ref[:, cols] += dk_pg[pg][0:BLK]
                dv0_ref[:, cols] += dv_pg[pg][0:BLK]
                dk_ref[:, cols] = kcarry[:, cols] + dk_pg[pg][BLK:2 * BLK]
                dv_ref[:, cols] = vcarry[:, cols] + dv_pg[pg][BLK:2 * BLK]
                kcarry[:, cols] = dk_pg[pg][2 * BLK:3 * BLK]
                vcarry[:, cols] = dv_pg[pg][2 * BLK:3 * BLK]

    last = nb - 1
    cur = lambda n: (jnp.minimum(n, last), 0)
    prev = lambda n: (jnp.clip(n - 1, 0, last), 0)
    zero = lambda n: (0, 0)
    kv = lambda f: pl.BlockSpec((BLK, KVW), f)
    wide = lambda f: pl.BlockSpec((BLK, AW), f)
    return _side_call(
        body, job, name=f"attn_bwd{l}", grid=(nb + 1,),
        in_specs=[pl.BlockSpec(memory_space=pltpu.SMEM),
                  wide(cur), kv(zero), kv(prev), kv(cur), kv(zero), kv(prev), kv(cur),
                  pl.BlockSpec((BLK, AW), lambda n: (jnp.minimum(n, last), 3)),
                  wide(cur), pl.BlockSpec((BLK, 128), cur), wide(cur), pl.BlockSpec(memory_space=pl.ANY)],
        out_specs=[wide(cur), kv(prev), kv(prev), kv(zero), kv(zero),
                   pl.BlockSpec((BLK, AW), lambda n: (jnp.minimum(n, last), 3)),
                   pl.BlockSpec((1, 128), zero)],
        out_shape=[jax.ShapeDtypeStruct((T, AW), F32), jax.ShapeDtypeStruct((T, KVW), F32),
                   jax.ShapeDtypeStruct((T, KVW), F32), jax.ShapeDtypeStruct((BLK, KVW), F32),
                   jax.ShapeDtypeStruct((BLK, KVW), F32), jax.ShapeDtypeStruct(dproj.shape, BF16),
                   jax.ShapeDtypeStruct((1, 128), F32)],
        scratch_shapes=[pltpu.VMEM((BLK, KVW), F32), pltpu.VMEM((BLK, KVW), F32)],
        semantics=("arbitrary",), aliases={12: 5},
        args=[sinks, qr, kr, kr, kr, vb, vb, vb, proj, att, lse, d_ya, dproj])


def _rope_bwd(dqr, dk, dv, dk0, dv0, tabs, dproj, l):
    T = dqr.shape[0]

    def body(dq_ref, dk_ref, dv_ref, dk0_ref, dv0_ref, c_ref, s1_ref, s2_ref, _, o_ref):
        n = pl.program_id(0)
        c, s1, s2 = c_ref[...], s1_ref[...], s2_ref[...]
        first = jnp.where(n == 0, 1.0, 0.0)
        for gcol in range(AW // 128):
            cols = slice(128 * gcol, 128 * gcol + 128)
            o_ref[:, cols] = (_rot_bwd(dq_ref[:, cols], c, s1, s2) * 0.125).astype(BF16)
        for gcol in range(KVW // 128):
            cols = slice(128 * gcol, 128 * gcol + 128)
            dkk = dk_ref[:, cols] + first * dk0_ref[:, cols]
            o_ref[:, AW + 128 * gcol:AW + 128 * gcol + 128] = _rot_bwd(dkk, c, s1, s2).astype(BF16)
            dvv = dv_ref[:, cols] + first * dv0_ref[:, cols]
            o_ref[:, AW + KVW + 128 * gcol:AW + KVW + 128 * gcol + 128] = dvv.astype(BF16)

    cur = lambda n: (n, 0)
    zero = lambda n: (0, 0)
    tab = pl.BlockSpec((BLK, 128), cur)
    return pl.pallas_call(
        body, name=f"rope_bwd{l}", grid=(T // BLK,),
        in_specs=[pl.BlockSpec((BLK, AW), cur), pl.BlockSpec((BLK, KVW), cur), pl.BlockSpec((BLK, KVW), cur),
                  pl.BlockSpec((BLK, KVW), zero), pl.BlockSpec((BLK, KVW), zero), tab, tab, tab,
                  pl.BlockSpec(memory_space=pl.ANY)],
        out_specs=pl.BlockSpec((BLK, AW + 2 * KVW), lambda n: (n, 1)),
        out_shape=jax.ShapeDtypeStruct(dproj.shape, BF16),
        input_output_aliases={8: 0},
        compiler_params=_cp("parallel"),
    )(dqr, dk, dv, dk0, dv0, *tabs, dproj)


def _block_diag(w):
    nl, nh, hd, _ = w.shape
    eye = jnp.eye(nh, dtype=w.dtype)
    return jnp.einsum("lhij,hg->lhigj", w, eye).reshape(nl, nh * hd, nh * hd)


def _diag_blocks(m):
    nh, hd = 8, 64
    return jnp.einsum("hihj->hij", m.reshape(nh, hd, nh, hd))


def _device_step(x, target, p, dist=None):
    vec = lambda a: a.reshape(DEPTH, 1, a.shape[-1])
    ln_in_g, ln_in_b = p["ln_in_g"].reshape(1, D), p["ln_in_b"].reshape(1, D)
    conv_dw_b, conv_ln_g, conv_ln_b, conv_pw_b = map(vec, (p["conv_dw_b"], p["conv_ln_g"], p["conv_ln_b"], p["conv_pw_b"]))
    lru_conv_b, lru_ba, lru_bx, lru_lambda = map(vec, (p["lru_conv_b"], p["lru_ba"], p["lru_bx"], p["lru_lambda"]))
    ln_post_g, ln_post_b = vec(p["ln_post_g"]), vec(p["ln_post_b"])
    wa_bd = _block_diag(p["lru_wa"]).astype(BF16)
    wx_bd = _block_diag(p["lru_wx"]).astype(BF16)
    w_in, w_out, pw_w = list(p["w_in"]), list(p["w_out"]), list(p["conv_pw_w"])
    sinks = p["attn_sinks"]
    big_names = ("w_in", "w_out", "conv_pw_w")

    (h, hb), got = _embed_fwd(x, p["meta_tokens"], ln_in_g, ln_in_b, job=_gather_job([w_in[0]]) if dist else None)
    if dist:
        w_in[0] = got[0]
    T = h.shape[0]
    tabs = _rope_tables(T)
    saved = []
    for l in range(DEPTH):
        (proj,), got = _proj_fwd(hb, w_in[l], l, job=_gather_job([w_out[0], pw_w[0]]) if dist and l == 0 else None)
        if got:
            w_out[0], pw_w[0] = got
        pw_l = pw_w[l].reshape(CW, CW)
        yc, conv = _conv_fwd(proj, p["conv_dw_w"], conv_dw_b, conv_ln_g, conv_ln_b, pw_l, conv_pw_b, l)
        qr, kr, vb = _rope_fwd(proj, tabs, l)
        (ya, att, lse), got = _attn_fwd(
            qr, kr, vb, proj, sinks, l, job=_gather_job([w_in[1], w_out[1], pw_w[1]]) if dist and l == 0 else None)
        if got:
            w_in[1], w_out[1], pw_w[1] = got
        yl, hl = _lru_fwd(proj, p["lru_conv_w"], lru_conv_b, wa_bd, lru_ba, wx_bd, lru_bx, lru_lambda, l)
        hn, hnb, xhat, rstd = _out_fwd(yc, ya, yl, w_out[l], h, ln_post_g, ln_post_b, l)
        saved.append((hb, proj, yc, conv, qr, kr, vb, ya, att, lse, yl, hl, xhat, rstd, pw_l))
        h, hb = hn, hnb

    loss_part, dh = _loss_head(h, target)
    g = {}
    later = None
    early, last = ("w_out", "conv_pw_w"), ("w_in",)
    own = {}
    for l in reversed(range(DEPTH)):
        hb_l, proj, yc, conv, qr, kr, vb, ya, att, lse, yl, hl, xhat, rstd, pw_l = saved[l]
        tail = dist is not None and l == 0
        dz, dzb, g["ln_post_g", l], g["ln_post_b", l] = _post_ln_bwd(dh, xhat, rstd, ln_post_g, l)
        (d_yc, d_ya, d_yl), recv = _dcat_bwd(dzb, w_out[l], l, job=_swap_job(later["grads"]) if later else None)
        if later:
            later["parts"], later["owns"] = _chip_partials(big_names, later["grads"], recv, dist, later["l"])
        g["w_out", l] = _dwout_bwd(yc, ya, yl, dzb, l)
        d_conv, dproj, dpw, g["conv_pw_b", l], g["conv_ln_g", l], g["conv_ln_b", l] = _conv_bwd_rows(
            conv, proj, d_yc, conv_ln_g, conv_ln_b, pw_l, conv_pw_b, l)
        g["conv_pw_w", l] = dpw.reshape(N_SHARD, 2, PW_SH // 2, CW)
        if tail:
            own["early"] = dict(l=0, grads=[g[name, 0] for name in early])
        (dproj, ddw, g["conv_dw_b", l]), recv = _conv_bwd_taps(
            d_conv, proj, p["conv_dw_w"], dproj, l, job=_swap_job(own["early"]["grads"]) if tail else None)
        if tail:
            own["early"]["parts"], own["early"]["owns"] = _chip_partials(early, own["early"]["grads"], recv, dist, 0)
        g["conv_dw_w", l] = ddw[:CONV_K]
        (dqr, dk, dv, dk0, dv0, dproj, dsink), z = _attn_bwd(
            qr, kr, vb, proj, att, lse, d_ya, sinks, dproj, l, job=_scatter_job(later["parts"]) if later else None)
        if later:
            later["z"] = z
        g["attn_sinks", l] = dsink[0, :N_HEADS]
        dproj = _rope_bwd(dqr, dk, dv, dk0, dv0, tabs, dproj, l)
        (dproj, dlw, g["lru_conv_b", l], dwa, g["lru_ba", l], dwx, g["lru_bx", l], g["lru_lambda", l]), z = _lru_bwd(
            proj, hl, d_yl, p["lru_conv_w"], lru_conv_b, wa_bd, lru_ba, wx_bd, lru_bx, lru_lambda, dproj, l,
            job=_scatter_job(own["early"]["parts"]) if tail else None)
        if tail:
            own["early"]["z"] = z
        g["lru_conv_w", l] = dlw[:LRU_K]
        g["lru_wa", l] = _diag_blocks(dwa)
        g["lru_wx", l] = _diag_blocks(dwx)
        g["w_in", l] = _dwin_bwd(hb_l, dproj, l)
        job = None
        if tail:
            own["last"] = dict(l=0, grads=[g["w_in", 0]])
            pack_a = _pack_rows([_layer_stack(g, name) for name in _SMALL_LAYERED])
            job = _join_jobs(_swap_job(own["last"]["grads"]), _spread_job(pack_a))
        (dh,), got = _dh_bwd(dproj, w_in[l], dz, l, job=job)
        if tail:
            own["last"]["parts"], own["last"]["owns"] = _chip_partials(last, own["last"]["grads"], got[:1], dist, 0)
            g["pack_layered", -1] = _sum_slots(pack_a, got[1], dist[3], "layered")
        if later:
            _finish_reduce(big_names, later, dist, g)
            later = None
        if dist and l > 0:
            later = dict(l=l, grads=[g[name, l] for name in big_names])
    grad_x, g["meta_tokens", -1], g["ln_in_g", -1], g["ln_in_b", -1] = _embed_bwd(
        dh, x, p["meta_tokens"], ln_in_g, ln_in_b)
    if dist:
        pack_b = _pack_rows([g[name, -1] for name in _SMALL_EMBED])
        got = _run_job(_join_jobs(_scatter_job(own["last"]["parts"]), _spread_job(pack_b)), "scatter_and_spread")
        own["last"]["z"] = got[:1]
        g["pack_embed", -1] = _sum_slots(pack_b, got[1], dist[3], "embed")
        state = dict(l=0, owns=own["last"]["owns"] + own["early"]["owns"], z=own["last"]["z"] + own["early"]["z"])
        _finish_reduce(last + early, state, dist, g)
    return loss_part, grad_x, g


_SMALL_EMBED = ("meta_tokens", "ln_in_g", "ln_in_b")
_SMALL_LAYERED = ("conv_dw_w", "conv_dw_b", "conv_ln_g", "conv_ln_b", "conv_pw_b", "attn_sinks", "lru_conv_w",
                  "lru_conv_b", "lru_wa", "lru_ba", "lru_wx", "lru_bx", "lru_lambda", "ln_post_g", "ln_post_b")


def _layer_stack(g, name):
    return jnp.stack([g[name, l] for l in range(DEPTH)], axis=0)


def _chip_partials(names, grads, recv, dist, l):
    outs = [_chip_partial(a, r, dist[0], dist[1], f"{name}{l}") for name, a, r in zip(names, grads, recv)]
    return [o[0] for o in outs], [o[1] for o in outs]


def _finish_reduce(names, state, dist, g):
    l = state["l"]
    totals = [_shard_total(po, zz, dist[2], f"{name}{l}") for name, po, zz in zip(names, state["owns"], state["z"])]
    full = _run_job(_share_job(totals), f"share_halves{l}")
    for name, f in zip(names, full):
        g[name, l] = f.reshape(2 * f.shape[1], f.shape[2])


MESH = pl.DeviceIdType.MESH
HBM_SPEC = pl.BlockSpec(memory_space=pltpu.HBM)
N_DEV = 8


def _position():
    x, y, c = lax.axis_index("x"), lax.axis_index("y"), lax.axis_index("c")
    return x, y, c


def _other_chips(x, y):
    return [(1 - x, y), (x, 1 - y), (1 - x, 1 - y)]


def _cast_into_slot(a, l, j, tag):
    _, R, C = a.shape
    tb = _pick(R, (512, 128))

    def body(s_ref, a_ref, o_ref):
        o_ref[...] = a_ref[...].astype(BF16)

    grid_spec = pltpu.PrefetchScalarGridSpec(
        num_scalar_prefetch=1, grid=(R // tb,),
        in_specs=[pl.BlockSpec((None, tb, C), lambda t, sc: (l, t, 0))],
        out_specs=pl.BlockSpec((None, tb, C), lambda t, sc: (sc[0], t, 0)))
    return pl.pallas_call(
        body, name=f"cast_into_slot_{tag}{l}", grid_spec=grid_spec,
        out_shape=jax.ShapeDtypeStruct((N_SHARD, R, C), BF16),
        compiler_params=_cp("arbitrary"),
    )(jnp.reshape(j, (1,)).astype(jnp.int32), a)


class _Job:
    def __init__(self, inputs, aliased, extra_out, sems, start, mid, finish):
        self.inputs, self.aliased, self.extra_out, self.sems = list(inputs), aliased, list(extra_out), list(sems)
        self.start, self.mid, self.finish = start, mid, finish

    def out_shapes(self):
        own = [jax.ShapeDtypeStruct(a.shape, a.dtype) for a in self.inputs] if self.aliased else []
        return own + self.extra_out


def _side_call(body, job, *, name, grid, in_specs, out_specs, out_shape, scratch_shapes, semantics, args,
               aliases=None):
    aliases = dict(aliases or {})
    if job is None:
        outs = pl.pallas_call(
            body, name=name, grid=grid, in_specs=in_specs, out_specs=out_specs, out_shape=out_shape,
            scratch_shapes=scratch_shapes, input_output_aliases=aliases, compiler_params=_cp(*semantics))(*args)
        return list(outs), []
    n_in, n_out, n_scr = len(in_specs), len(out_specs), len(scratch_shapes)
    j_in, j_out = len(job.inputs), len(job.out_shapes())
    steps = 1
    for gsize in grid:
        steps *= gsize

    def wrapped(*refs):
        host_in, job_in = refs[:n_in], refs[n_in:n_in + j_in]
        o0 = n_in + j_in
        host_out, job_out = refs[o0:o0 + n_out], refs[o0 + n_out:o0 + n_out + j_out]
        s0 = o0 + n_out + j_out
        host_scr, sems = refs[s0:s0 + n_scr], refs[s0 + n_scr:]
        step = pl.program_id(0)
        for d in range(1, len(grid)):
            step = step * grid[d] + pl.program_id(d)

        @pl.when(step == 0)
        def _():
            job.start(job_in, job_out, sems)

        @pl.when(step == max(steps - 2, 0))
        def _():
            job.mid(job_in, job_out, sems)

        body(*host_in, *host_out, *host_scr)

        @pl.when(step == steps - 1)
        def _():
            job.finish(job_in, job_out, sems)

    if job.aliased:
        aliases.update({n_in + k: n_out + k for k in range(j_in)})
    outs = pl.pallas_call(
        wrapped, name=name, grid=grid,
        in_specs=list(in_specs) + [HBM_SPEC] * j_in, out_specs=list(out_specs) + [HBM_SPEC] * j_out,
        out_shape=list(out_shape) + job.out_shapes(),
        scratch_shapes=list(scratch_shapes) + job.sems, input_output_aliases=aliases,
        compiler_params=_cp(*(["arbitrary"] * len(grid))))(*args, *job.inputs)
    return list(outs[:n_out]), list(outs[n_out:])


def _run_job(job, name):
    return _side_call(lambda: None, job, name=name, grid=(1,), in_specs=[], out_specs=[], out_shape=[],
                      scratch_shapes=[], semantics=("arbitrary",), args=[])[1]


def _gather_job(slots):
    n = len(slots)

    def copies(buf, sems):
        ici_send, ici_recv, d2d_send, d2d_recv = sems
        x, y, c = _position()
        chips = _other_chips(x, y)

        def half(k, slot, which):
            hr = buf[k].shape[1] // 2
            return buf[k].at[slot, pl.ds(pl.multiple_of(which * hr, hr), hr)]

        def over_ici(k, p, slot):
            px, py = chips[p]
            return pltpu.make_async_remote_copy(
                src_ref=half(k, slot, c), dst_ref=half(k, slot, c),
                send_sem=ici_send.at[k * 3 + p], recv_sem=ici_recv.at[k * 3 + p],
                device_id=(px, py, c), device_id_type=MESH)

        def over_d2d(k, p, which):
            px, py = chips[p]
            return pltpu.make_async_remote_copy(
                src_ref=half(k, 2 * px + py, which), dst_ref=half(k, 2 * px + py, which),
                send_sem=d2d_send.at[k * 3 + p], recv_sem=d2d_recv.at[k * 3 + p],
                device_id=(x, y, 1 - c), device_id_type=MESH)

        return over_ici, over_d2d, 2 * x + y, chips, c

    pairs = [(k, p) for k in range(n) for p in range(3)]

    def start(_, buf, sems):
        over_ici, _, mine, _, _ = copies(buf, sems)
        for k, p in pairs:
            over_ici(k, p, mine).start()

    def mid(_, buf, sems):
        over_ici, over_d2d, _, chips, c = copies(buf, sems)
        for k, p in pairs:
            px, py = chips[p]
            over_ici(k, p, 2 * px + py).wait_recv()
            over_d2d(k, p, c).start()

    def finish(_, buf, sems):
        over_ici, over_d2d, mine, _, c = copies(buf, sems)
        for k, p in pairs:
            over_d2d(k, p, 1 - c).wait_recv()
        for k, p in pairs:
            over_ici(k, p, mine).wait_send()
            over_d2d(k, p, c).wait_send()

    return _Job(slots, True, [], [pltpu.SemaphoreType.DMA((3 * n,))] * 4, start, mid, finish)


def _gather_shards(shards):
    n = len(shards)

    def body(*refs):
        src, dst = refs[:n], refs[n:2 * n]
        send_sems, recv_sems, local_sems = refs[2 * n:]
        x, y, c = _position()
        mine = 2 * x + y
        chips = _other_chips(x, y)

        def copy(k, p):
            return pltpu.make_async_remote_copy(
                src_ref=src[k], dst_ref=dst[k].at[mine],
                send_sem=send_sems.at[k * 3 + p], recv_sem=recv_sems.at[k * 3 + p],
                device_id=(*chips[p], c), device_id_type=MESH)

        def arrival(k, p):
            px, py = chips[p]
            return pltpu.make_async_remote_copy(
                src_ref=src[k], dst_ref=dst[k].at[2 * px + py],
                send_sem=send_sems.at[k * 3 + p], recv_sem=recv_sems.at[k * 3 + p],
                device_id=(px, py, c), device_id_type=MESH)

        local = [pltpu.make_async_copy(src[k], dst[k].at[mine], local_sems.at[k]) for k in range(n)]
        for cp in local:
            cp.start()
        for k in range(n):
            for p in range(3):
                copy(k, p).start()
        for k in range(n):
            for p in range(3):
                arrival(k, p).wait_recv()
        for k in range(n):
            for p in range(3):
                copy(k, p).wait_send()
        for cp in local:
            cp.wait()

    return pl.pallas_call(
        body, name="gather_shards",
        in_specs=[HBM_SPEC] * n, out_specs=[HBM_SPEC] * n,
        out_shape=[jax.ShapeDtypeStruct((N_SHARD,) + s.shape, s.dtype) for s in shards],
        scratch_shapes=[pltpu.SemaphoreType.DMA((3 * n,)), pltpu.SemaphoreType.DMA((3 * n,)),
                        pltpu.SemaphoreType.DMA((n,))],
    )(*shards)


def _swap_job(grads):
    n = len(grads)

    def copies(src, dst, sems):
        x, y, c = _position()
        return [pltpu.make_async_remote_copy(
            src_ref=src[k].at[:, 1 - c], dst_ref=dst[k],
            send_sem=sems[0].at[k], recv_sem=sems[1].at[k],
            device_id=(x, y, 1 - c), device_id_type=MESH) for k in range(n)]

    def start(src, dst, sems):
        for cp in copies(src, dst, sems):
            cp.start()

    def finish(src, dst, sems):
        for cp in copies(src, dst, sems):
            cp.wait()

    return _Job(grads, False, [jax.ShapeDtypeStruct((N_SHARD,) + g.shape[2:], F32) for g in grads],
                [pltpu.SemaphoreType.DMA((n,))] * 2, start, lambda *_: None, finish)


def _chip_partial(a, y, c, j, tag):
    _, _, R, C = a.shape
    tr = _pick(R, (256, 64))

    def body(s_ref, a_ref, y_ref, pb_ref, po_ref):
        total = a_ref[...] + y_ref[...]
        pb_ref[...] = total.astype(BF16)

        @pl.when(pl.program_id(1) == s_ref[1])
        def _():
            po_ref[...] = total

    grid_spec = pltpu.PrefetchScalarGridSpec(
        num_scalar_prefetch=1, grid=(R // tr, N_SHARD),
        in_specs=[pl.BlockSpec((None, None, tr, C), lambda t, s, sc: (s, sc[0], t, 0)),
                  pl.BlockSpec((None, tr, C), lambda t, s, sc: (s, t, 0))],
        out_specs=[pl.BlockSpec((None, tr, C), lambda t, s, sc: (s, t, 0)),
                   pl.BlockSpec((tr, C), lambda t, s, sc: (t, 0))])
    return pl.pallas_call(
        body, name=f"chip_partial_{tag}", grid_spec=grid_spec,
        out_shape=[jax.ShapeDtypeStruct((N_SHARD, R, C), BF16), jax.ShapeDtypeStruct((R, C), F32)],
        compiler_params=_cp("arbitrary", "arbitrary"),
    )(jnp.stack([c, j]).astype(jnp.int32), a, y)


def _scatter_job(parts):
    n = len(parts)
    pairs = [(k, p) for k in range(n) for p in range(3)]

    def copy(src, dst, sems, k, p, outgoing):
        x, y, c = _position()
        mine = 2 * x + y
        px, py = _other_chips(x, y)[p]
        theirs = 2 * px + py
        return pltpu.make_async_remote_copy(
            src_ref=src[k].at[theirs if outgoing else mine], dst_ref=dst[k].at[mine if outgoing else theirs],
            send_sem=sems[0].at[k * 3 + p], recv_sem=sems[1].at[k * 3 + p],
            device_id=(px, py, c), device_id_type=MESH)

    def start(src, dst, sems):
        for k, p in pairs:
            copy(src, dst, sems, k, p, True).start()

    def finish(src, dst, sems):
        for k, p in pairs:
            copy(src, dst, sems, k, p, False).wait_recv()
        for k, p in pairs:
            copy(src, dst, sems, k, p, True).wait_send()

    return _Job(parts, False, [jax.ShapeDtypeStruct(pb.shape, BF16) for pb in parts],
                [pltpu.SemaphoreType.DMA((3 * n,))] * 2, start, lambda *_: None, finish)


def _shard_total(own, z, others_c, tag):
    R, C = own.shape
    tr = _pick(R, (256, 64))

    def body(s_ref, o_ref, z0_ref, z1_ref, z2_ref, h_ref):
        h_ref[...] = ((o_ref[...] + z0_ref[...].astype(F32)) + z1_ref[...].astype(F32)) + z2_ref[...].astype(F32)

    zspec = lambda q: pl.BlockSpec((None, tr, C), lambda t, sc: (sc[q], t, 0))
    grid_spec = pltpu.PrefetchScalarGridSpec(
        num_scalar_prefetch=1, grid=(R // tr,),
        in_specs=[pl.BlockSpec((tr, C), lambda t, sc: (t, 0)), zspec(0), zspec(1), zspec(2)],
        out_specs=pl.BlockSpec((None, tr, C), lambda t, sc: (sc[3], t, 0)))
    return pl.pallas_call(
        body, name=f"shard_total_{tag}", grid_spec=grid_spec,
        out_shape=jax.ShapeDtypeStruct((2, R, C), F32),
        compiler_params=_cp("arbitrary"),
    )(others_c, own, z, z, z)


def _share_job(totals):
    n = len(totals)

    def copy(buf, sems, k, which):
        x, y, c = _position()
        return pltpu.make_async_remote_copy(
            src_ref=buf[k].at[which], dst_ref=buf[k].at[which],
            send_sem=sems[0].at[k], recv_sem=sems[1].at[k],
            device_id=(x, y, 1 - c), device_id_type=MESH)

    def start(_, buf, sems):
        c = lax.axis_index("c")
        for k in range(n):
            copy(buf, sems, k, c).start()

    def finish(_, buf, sems):
        c = lax.axis_index("c")
        for k in range(n):
            copy(buf, sems, k, 1 - c).wait_recv()
        for k in range(n):
            copy(buf, sems, k, c).wait_send()

    return _Job(totals, True, [], [pltpu.SemaphoreType.DMA((n,))] * 2, start, lambda *_: None, finish)


def _spread_job(pack):
    def copy(src, dst, sems, m, outgoing):
        x, y, c = _position()
        peer = (x ^ (m >> 2), y ^ ((m >> 1) & 1), c ^ (m & 1))
        slot = 4 * x + 2 * y + c if outgoing else 4 * peer[0] + 2 * peer[1] + peer[2]
        return pltpu.make_async_remote_copy(
            src_ref=src[0], dst_ref=dst[0].at[slot], send_sem=sems[0].at[m - 1], recv_sem=sems[1].at[m - 1],
            device_id=peer, device_id_type=MESH)

    def start(src, dst, sems):
        for m in range(1, N_DEV):
            copy(src, dst, sems, m, True).start()

    def finish(src, dst, sems):
        for m in range(1, N_DEV):
            copy(src, dst, sems, m, False).wait_recv()
        for m in range(1, N_DEV):
            copy(src, dst, sems, m, True).wait_send()

    return _Job([pack], False, [jax.ShapeDtypeStruct((N_DEV,) + pack.shape, F32)],
                [pltpu.SemaphoreType.DMA((N_DEV - 1,))] * 2, start, lambda *_: None, finish)


def _join_jobs(a, b):
    assert not a.aliased and not b.aliased
    n_in, n_out, n_sem = len(a.inputs), len(a.extra_out), len(a.sems)

    def phase(name):
        def run(ins, outs, sems):
            getattr(a, name)(ins[:n_in], outs[:n_out], sems[:n_sem])
            getattr(b, name)(ins[n_in:], outs[n_out:], sems[n_sem:])
        return run

    return _Job(a.inputs + b.inputs, False, a.extra_out + b.extra_out, a.sems + b.sems,
                phase("start"), phase("mid"), phase("finish"))


def _sum_slots(pack, slots, me, tag):
    def body(me_ref, p_ref, s_ref, o_ref):
        acc = None
        for d in range(N_DEV):
            term = jnp.where(me_ref[0] == d, p_ref[...], s_ref[d])
            acc = term if acc is None else acc + term
        o_ref[...] = acc

    vm = pl.BlockSpec(memory_space=pltpu.VMEM)
    return pl.pallas_call(
        body, name=f"sum_slots_{tag}",
        in_specs=[pl.BlockSpec(memory_space=pltpu.SMEM), vm, vm], out_specs=vm,
        out_shape=jax.ShapeDtypeStruct(pack.shape, F32),
        compiler_params=pltpu.CompilerParams(vmem_limit_bytes=V7X_VMEM_LIMIT),
    )(jnp.reshape(me, (1,)).astype(jnp.int32), pack, slots)


def _pack_rows(arrays):
    total = sum(a.size for a in arrays)
    rows = -(-total // 128)
    rows = -(-rows // PACK_ROWS_ALIGN) * PACK_ROWS_ALIGN
    flat = [a.reshape(-1) for a in arrays] + [jnp.zeros((rows * 128 - total,), F32)]
    return jnp.concatenate(flat).reshape(rows, 128)


def _adamw_math(w, g, m, v):
    m = ADAM_B1 * m + (1.0 - ADAM_B1) * g
    v = ADAM_B2 * v + (1.0 - ADAM_B2) * (g * g)
    m_hat = m / (1.0 - ADAM_B1 ** ADAM_STEP)
    v_hat = v / (1.0 - ADAM_B2 ** ADAM_STEP)
    delta = -ADAM_LR * (m_hat / (jnp.sqrt(v_hat) + ADAM_EPS) + ADAM_WD * w)
    return delta, m, v


def _adamw_big(w, g0, g1, m, v, tag):
    _, R, C = w.shape
    tr = _pick(R, (256, 128))

    def body(w_ref, g0_ref, g1_ref, m_ref, v_ref, go_ref, d_ref, mo_ref, vo_ref):
        g = jnp.where(pl.program_id(0) == 0, g0_ref[...], g1_ref[...])
        delta, mn, vn = _adamw_math(w_ref[...], g, m_ref[...], v_ref[...])
        go_ref[...] = g
        d_ref[...] = delta
        mo_ref[...] = mn
        vo_ref[...] = vn

    s3 = pl.BlockSpec((None, tr, C), lambda l, t: (l, t, 0))
    s2 = pl.BlockSpec((tr, C), lambda l, t: (t, 0))
    shp = jax.ShapeDtypeStruct(w.shape, F32)
    return pl.pallas_call(
        body, name=f"adamw_{tag}", grid=(2, R // tr),
        in_specs=[s3, s2, s2, s3, s3], out_specs=[s3, s3, s3, s3],
        out_shape=[shp, shp, shp, shp],
        compiler_params=_cp("parallel", "parallel"),
    )(w, g0, g1, m, v)


def _adamw_small(ws, gs, ms, vs):
    n = len(ws)

    def body(*refs):
        w_r, g_r, m_r, v_r = refs[:n], refs[n:2 * n], refs[2 * n:3 * n], refs[3 * n:4 * n]
        d_o, m_o, v_o = refs[4 * n:5 * n], refs[5 * n:6 * n], refs[6 * n:7 * n]
        for k in range(n):
            delta, mn, vn = _adamw_math(w_r[k][...], g_r[k][...], m_r[k][...], v_r[k][...])
            d_o[k][...] = delta
            m_o[k][...] = mn
            v_o[k][...] = vn

    vm = pl.BlockSpec(memory_space=pltpu.VMEM)
    shapes = [jax.ShapeDtypeStruct(w.shape, F32) for w in ws]
    outs = pl.pallas_call(
        body, name="adamw_small",
        in_specs=[vm] * (4 * n), out_specs=[vm] * (3 * n),
        out_shape=shapes * 3,
    )(*ws, *gs, *ms, *vs)
    return outs[:n], outs[n:2 * n], outs[2 * n:]


_WEIGHTS = ["meta_tokens", "ln_in_g", "ln_in_b", "w_in", "conv_dw_w", "conv_dw_b", "conv_ln_g", "conv_ln_b",
            "conv_pw_w", "conv_pw_b", "attn_sinks", "lru_conv_w", "lru_conv_b", "lru_wa", "lru_ba", "lru_wx",
            "lru_bx", "lru_lambda", "w_out", "ln_post_g", "ln_post_b"]
_BIG = ("w_in", "w_out", "conv_pw_w")
_SMALL_SHARDED = {"meta_tokens": 1, "conv_dw_w": 2, "lru_conv_w": 2}
PACK_ROWS_ALIGN = 8


def _as2d(a):
    return a.reshape(1, -1) if a.ndim == 1 else a.reshape(-1, a.shape[-1])


def kernel(x, meta_tokens, ln_in_g, ln_in_b, w_in, conv_dw_w, conv_dw_b, conv_ln_g, conv_ln_b, conv_pw_w, conv_pw_b, attn_sinks, lru_conv_w, lru_conv_b, lru_wa, lru_ba, lru_wx, lru_bx, lru_lambda, w_out, ln_post_g, ln_post_b, loss_target, m_meta_tokens, m_ln_in_g, m_ln_in_b, m_w_in, m_conv_dw_w, m_conv_dw_b, m_conv_ln_g, m_conv_ln_b, m_conv_pw_w, m_conv_pw_b, m_attn_sinks, m_lru_conv_w, m_lru_conv_b, m_lru_wa, m_lru_ba, m_lru_wx, m_lru_bx, m_lru_lambda, m_w_out, m_ln_post_g, m_ln_post_b, v_meta_tokens, v_ln_in_g, v_ln_in_b, v_w_in, v_conv_dw_w, v_conv_dw_b, v_conv_ln_g, v_conv_ln_b, v_conv_pw_w, v_conv_pw_b, v_attn_sinks, v_lru_conv_w, v_lru_conv_b, v_lru_wa, v_lru_ba, v_lru_wx, v_lru_bx, v_lru_lambda, v_w_out, v_ln_post_g, v_ln_post_b):
    w = dict(meta_tokens=meta_tokens, ln_in_g=ln_in_g, ln_in_b=ln_in_b, w_in=w_in, conv_dw_w=conv_dw_w,
             conv_dw_b=conv_dw_b, conv_ln_g=conv_ln_g, conv_ln_b=conv_ln_b, conv_pw_w=conv_pw_w,
             conv_pw_b=conv_pw_b, attn_sinks=attn_sinks, lru_conv_w=lru_conv_w, lru_conv_b=lru_conv_b,
             lru_wa=lru_wa, lru_ba=lru_ba, lru_wx=lru_wx, lru_bx=lru_bx, lru_lambda=lru_lambda, w_out=w_out,
             ln_post_g=ln_post_g, ln_post_b=ln_post_b)
    mom_m = dict(zip(_WEIGHTS, (m_meta_tokens, m_ln_in_g, m_ln_in_b, m_w_in, m_conv_dw_w, m_conv_dw_b, m_conv_ln_g,
                                m_conv_ln_b, m_conv_pw_w, m_conv_pw_b, m_attn_sinks, m_lru_conv_w, m_lru_conv_b,
                                m_lru_wa, m_lru_ba, m_lru_wx, m_lru_bx, m_lru_lambda, m_w_out, m_ln_post_g,
                                m_ln_post_b)))
    mom_v = dict(zip(_WEIGHTS, (v_meta_tokens, v_ln_in_g, v_ln_in_b, v_w_in, v_conv_dw_w, v_conv_dw_b, v_conv_ln_g,
                                v_conv_ln_b, v_conv_pw_w, v_conv_pw_b, v_attn_sinks, v_lru_conv_w, v_lru_conv_b,
                                v_lru_wa, v_lru_ba, v_lru_wx, v_lru_bx, v_lru_lambda, v_w_out, v_ln_post_g,
                                v_ln_post_b)))
    xi, yi, ci = _position()
    j = 2 * xi + yi

    g_meta, g_dw, g_lc = _gather_shards([meta_tokens, conv_dw_w, lru_conv_w])
    p = dict(w)
    p["w_in"] = [_cast_into_slot(w_in, l, j, "w_in") for l in range(DEPTH)]
    p["w_out"] = [_cast_into_slot(w_out, l, j, "w_out") for l in range(DEPTH)]
    p["conv_pw_w"] = [_cast_into_slot(conv_pw_w, l, j, "conv_pw_w") for l in range(DEPTH)]
    p["meta_tokens"] = g_meta.transpose(1, 0, 2).reshape(N_META, D)
    p["conv_dw_w"] = g_dw.transpose(1, 2, 0, 3).reshape(DEPTH, CONV_K, CW)
    p["lru_conv_w"] = g_lc.transpose(1, 2, 0, 3).reshape(DEPTH, LRU_K, LW)

    others = jnp.stack([jnp.where(j <= 0, 1, 0), jnp.where(j <= 1, 2, 1), jnp.where(j <= 2, 3, 2), ci]).astype(jnp.int32)
    me = 4 * xi + 2 * yi + ci
    loss_part, grad_x, g = _device_step(x[0], loss_target[0], p, dist=(ci, j, others, me))
    loss = lax.psum(jnp.sum(loss_part), ("x", "y", "c"))
    big = {(name, l): g[name, l] for name in _BIG for l in range(DEPTH)}

    small_names = [n for n in _WEIGHTS if n not in _BIG]
    small_g = {}
    for names, red in ((_SMALL_LAYERED, g["pack_layered", -1]), (_SMALL_EMBED, g["pack_embed", -1])):
        red = red.reshape(-1)
        off = 0
        for n in names:
            fshape = list(w[n].shape)
            if n in _SMALL_SHARDED:
                fshape[_SMALL_SHARDED[n]] *= N_SHARD
            sz = 1
            for dim in fshape:
                sz *= dim
            full = red[off:off + sz].reshape(fshape)
            off += sz
            if n in _SMALL_SHARDED:
                ax = _SMALL_SHARDED[n]
                full = lax.dynamic_slice_in_dim(full, j * w[n].shape[ax], w[n].shape[ax], axis=ax)
            small_g[n] = full

    out_g, out_d, out_m, out_v = {}, {}, {}, {}
    for name in _BIG:
        shp = w[name].shape
        to3 = lambda a: a.reshape(DEPTH, -1, shp[-1])
        go, do, mo, vo = _adamw_big(to3(w[name]), big[name, 0], big[name, 1], to3(mom_m[name]), to3(mom_v[name]), name)
        out_g[name], out_d[name], out_m[name], out_v[name] = (a.reshape(shp) for a in (go, do, mo, vo))
    ds, ms, vs = _adamw_small([_as2d(w[n]) for n in small_names], [_as2d(small_g[n]) for n in small_names],
                              [_as2d(mom_m[n]) for n in small_names], [_as2d(mom_v[n]) for n in small_names])
    for n, d_, m_, v_ in zip(small_names, ds, ms, vs):
        out_g[n] = small_g[n]
        out_d[n], out_m[n], out_v[n] = d_.reshape(w[n].shape), m_.reshape(w[n].shape), v_.reshape(w[n].shape)

    return (loss, grad_x[None], *[out_g[n] for n in _WEIGHTS], *[out_d[n] for n in _WEIGHTS],
            *[out_m[n] for n in _WEIGHTS], *[out_v[n] for n in _WEIGHTS])
```

```python
import functools

import jax
import jax.numpy as jnp
from jax import lax
from jax.experimental import pallas as pl
from jax.experimental.pallas import tpu as pltpu

F32 = jnp.float32
BF16 = jnp.bfloat16

D = 2048
N_META = 16
CW = 512
CONV_K = 31
AW = 1024
KVW = 256
N_HEADS = 16
LW = 512
LRU_K = 4
LRU_C = 8.0
IN_TOTAL = 5120
ROT_HALF = 8
ROPE_THETA = 500000.0
LN_EPS = 1e-5
DEPTH = 2
ALPHA = (2.0 * DEPTH) ** 0.25
NEG_INF = -1e30
ADAM_LR, ADAM_B1, ADAM_B2, ADAM_EPS, ADAM_WD, ADAM_STEP = 0.001, 0.9, 0.999, 1e-08, 0.01, 10

BLK = 128
PAD = BLK - N_META
N_SHARD = 4
WIN_SH = IN_TOTAL // N_SHARD
WOUT_SH = D // N_SHARD
PW_SH = CW // N_SHARD
HALO = 32
LHALO = 8
V7X_VMEM_LIMIT = 60 * 1024 * 1024


def _cp(*sem):
    return pltpu.CompilerParams(dimension_semantics=sem if sem else None, vmem_limit_bytes=V7X_VMEM_LIMIT)


def _pick(total, prefs):
    for p in prefs:
        if total % p == 0:
            return p
    raise ValueError(f"no tile for {total}")


def _dot(a, b):
    return jnp.dot(a, b, preferred_element_type=F32)


def _dot_nt(a, b):
    return lax.dot_general(a, b, (((1,), (1,)), ((), ())), preferred_element_type=F32)


def _dot_tn(a, b):
    return lax.dot_general(a, b, (((0,), (0,)), ((), ())), preferred_element_type=F32)


def _sigmoid(x):
    return 1.0 / (1.0 + jnp.exp(-x))


def _silu_and_grad(x):
    s = _sigmoid(x)
    return x * s, s * (1.0 + x * (1.0 - s))


def _ln_rows(x, g, b):
    mu = jnp.mean(x, axis=-1, keepdims=True)
    xc = x - mu
    var = jnp.mean(xc * xc, axis=-1, keepdims=True)
    rstd = lax.rsqrt(var + LN_EPS)
    xhat = xc * rstd
    return xhat * g + b, xhat, rstd


def _ln_bwd_rows(dy, xhat, rstd, g):
    dxh = dy * g
    m1 = jnp.mean(dxh, axis=-1, keepdims=True)
    m2 = jnp.mean(dxh * xhat, axis=-1, keepdims=True)
    return rstd * (dxh - m1 - xhat * m2)


def _row_ids(n, base):
    return base + lax.broadcasted_iota(jnp.int32, (n, 1), 0)


def _colsum(x):
    return jnp.sum(x, axis=0, keepdims=True)


def _embed_fwd(x, meta, g, b, job=None):
    S = x.shape[0]
    nb = S // BLK + 1

    def body(x_ref, meta_ref, g_ref, b_ref, h_ref, hb_ref):
        n = pl.program_id(0)

        @pl.when(n == 0)
        def _():
            y, _, _ = _ln_rows(meta_ref[...], g_ref[...], b_ref[...])
            h_ref[...] = jnp.zeros_like(h_ref)
            h_ref[PAD:BLK, :] = y

        @pl.when(n > 0)
        def _():
            y, _, _ = _ln_rows(x_ref[...], g_ref[...], b_ref[...])
            h_ref[...] = y

        hb_ref[...] = h_ref[...].astype(BF16)

    return _side_call(
        body, job, name="embed_fwd", grid=(nb,),
        in_specs=[pl.BlockSpec((BLK, D), lambda n: (jnp.maximum(n - 1, 0), 0)),
                  pl.BlockSpec((N_META, D), lambda n: (0, 0)),
                  pl.BlockSpec((1, D), lambda n: (0, 0)),
                  pl.BlockSpec((1, D), lambda n: (0, 0))],
        out_specs=[pl.BlockSpec((BLK, D), lambda n: (n, 0)),
                   pl.BlockSpec((BLK, D), lambda n: (n, 0))],
        out_shape=[jax.ShapeDtypeStruct((nb * BLK, D), F32), jax.ShapeDtypeStruct((nb * BLK, D), BF16)],
        scratch_shapes=[], semantics=("arbitrary",), args=[x, meta, g, b])


def _embed_bwd(dh, x, meta, g, b):
    S = x.shape[0]
    nb = S // BLK + 1

    def body(dh_ref, x_ref, meta_ref, g_ref, b_ref, gx_ref, gm_ref, dg_ref, db_ref):
        n = pl.program_id(0)

        @pl.when(n == 0)
        def _():
            _, xhat, rstd = _ln_rows(meta_ref[...], g_ref[...], b_ref[...])
            dy = dh_ref[PAD:BLK, :]
            gm_ref[...] = _ln_bwd_rows(dy, xhat, rstd, g_ref[...])
            dg_ref[...] = _colsum(dy * xhat)
            db_ref[...] = _colsum(dy)

        @pl.when(n > 0)
        def _():
            _, xhat, rstd = _ln_rows(x_ref[...], g_ref[...], b_ref[...])
            dy = dh_ref[...]
            gx_ref[...] = _ln_bwd_rows(dy, xhat, rstd, g_ref[...])
            dg_ref[...] += _colsum(dy * xhat)
            db_ref[...] += _colsum(dy)

    prev = lambda n: (jnp.maximum(n - 1, 0), 0)
    const = lambda n: (0, 0)
    return pl.pallas_call(
        body, name="embed_bwd", grid=(nb,),
        in_specs=[pl.BlockSpec((BLK, D), lambda n: (n, 0)),
                  pl.BlockSpec((BLK, D), prev),
                  pl.BlockSpec((N_META, D), const),
                  pl.BlockSpec((1, D), const),
                  pl.BlockSpec((1, D), const)],
        out_specs=[pl.BlockSpec((BLK, D), prev),
                   pl.BlockSpec((N_META, D), const),
                   pl.BlockSpec((1, D), const),
                   pl.BlockSpec((1, D), const)],
        out_shape=[jax.ShapeDtypeStruct((S, D), F32), jax.ShapeDtypeStruct((N_META, D), F32),
                   jax.ShapeDtypeStruct((1, D), F32), jax.ShapeDtypeStruct((1, D), F32)],
        compiler_params=_cp("arbitrary"),
    )(dh, x, meta, g, b)


def _proj_fwd(hb, w_in, l, job=None):
    T = hb.shape[0]
    tm = _pick(T, (1056, 384, 128))

    def body(a_ref, w_ref, o_ref):
        o_ref[...] = _dot(a_ref[...], w_ref[...])

    return _side_call(
        body, job, name=f"proj_fwd{l}", grid=(T // tm, N_SHARD),
        in_specs=[pl.BlockSpec((tm, D), lambda i, j: (i, 0)),
                  pl.BlockSpec((None, D, WIN_SH), lambda i, j: (j, 0, 0))],
        out_specs=[pl.BlockSpec((tm, WIN_SH), lambda i, j: (i, j))],
        out_shape=[jax.ShapeDtypeStruct((T, IN_TOTAL), F32)],
        scratch_shapes=[], semantics=("parallel", "arbitrary"), args=[hb, w_in])


def _out_fwd(yc, ya, yl, w_out, h, g, b, l):
    T = h.shape[0]
    tm = _pick(T, (384, 128))

    def body(yc_ref, ya_ref, yl_ref, w_ref, h_ref, g_ref, b_ref, hn_ref, hnb_ref, xh_ref, rs_ref):
        acc = _dot(yc_ref[...], w_ref[0])
        acc += _dot(ya_ref[:, 0:WOUT_SH], w_ref[1])
        acc += _dot(ya_ref[:, WOUT_SH:2 * WOUT_SH], w_ref[2])
        acc += _dot(yl_ref[...], w_ref[3])
        z = ALPHA * h_ref[...] + acc
        y, xhat, rstd = _ln_rows(z, g_ref[...], b_ref[...])
        hn_ref[...] = y
        hnb_ref[...] = y.astype(BF16)
        xh_ref[...] = xhat
        rs_ref[...] = rstd

    row = lambda i: (i, 0)
    return pl.pallas_call(
        body, name=f"out_fwd{l}", grid=(T // tm,),
        in_specs=[pl.BlockSpec((tm, CW), row), pl.BlockSpec((tm, AW), row), pl.BlockSpec((tm, LW), row),
                  pl.BlockSpec((N_SHARD, WOUT_SH, D), lambda i: (0, 0, 0)),
                  pl.BlockSpec((tm, D), row),
                  pl.BlockSpec((None, 1, D), lambda i: (l, 0, 0)),
                  pl.BlockSpec((None, 1, D), lambda i: (l, 0, 0))],
        out_specs=[pl.BlockSpec((tm, D), row), pl.BlockSpec((tm, D), row), pl.BlockSpec((tm, D), row),
                   pl.BlockSpec((tm, 1), row)],
        out_shape=[jax.ShapeDtypeStruct((T, D), F32), jax.ShapeDtypeStruct((T, D), BF16),
                   jax.ShapeDtypeStruct((T, D), F32), jax.ShapeDtypeStruct((T, 1), F32)],
        compiler_params=_cp("parallel"),
    )(yc, ya, yl, w_out, h, g, b)


def _post_ln_bwd(dhn, xhat, rstd, g, l):
    T = dhn.shape[0]
    tm = _pick(T, (384, 128))

    def body(d_ref, xh_ref, rs_ref, g_ref, dz_ref, dzb_ref, dg_ref, db_ref):
        @pl.when(pl.program_id(0) == 0)
        def _():
            dg_ref[...] = jnp.zeros_like(dg_ref)
            db_ref[...] = jnp.zeros_like(db_ref)

        dy = d_ref[...]
        xhat = xh_ref[...]
        dz = _ln_bwd_rows(dy, xhat, rs_ref[...], g_ref[...])
        dz_ref[...] = dz
        dzb_ref[...] = dz.astype(BF16)
        dg_ref[...] += _colsum(dy * xhat)
        db_ref[...] += _colsum(dy)

    row = lambda i: (i, 0)
    const = lambda i: (0, 0)
    return pl.pallas_call(
        body, name=f"post_ln_bwd{l}", grid=(T // tm,),
        in_specs=[pl.BlockSpec((tm, D), row), pl.BlockSpec((tm, D), row), pl.BlockSpec((tm, 1), row),
                  pl.BlockSpec((None, 1, D), lambda i: (l, 0, 0))],
        out_specs=[pl.BlockSpec((tm, D), row), pl.BlockSpec((tm, D), row),
                   pl.BlockSpec((1, D), const), pl.BlockSpec((1, D), const)],
        out_shape=[jax.ShapeDtypeStruct((T, D), F32), jax.ShapeDtypeStruct((T, D), BF16),
                   jax.ShapeDtypeStruct((1, D), F32), jax.ShapeDtypeStruct((1, D), F32)],
        compiler_params=_cp("arbitrary"),
    )(dhn, xhat, rstd, g)


def _loss_post_ln_bwd(h, target, xhat, rstd, g, l):
    T = h.shape[0]
    tm = _pick(T, (384, 128))
    per = tm // BLK
    last_blk = target.shape[0] // BLK - 1

    def body(h_ref, *refs):
        t_refs, (xh_ref, rs_ref, g_ref, part_ref, dz_ref, dzb_ref, dg_ref, db_ref) = refs[:per], refs[per:]
        i = pl.program_id(0)

        @pl.when(i == 0)
        def _():
            part_ref[...] = jnp.zeros_like(part_ref)
            dg_ref[...] = jnp.zeros_like(dg_ref)
            db_ref[...] = jnp.zeros_like(db_ref)

        tgt = jnp.concatenate([r[...] for r in t_refs], axis=0) if per > 1 else t_refs[0][...]
        real = _row_ids(tm, i * tm) >= BLK
        err = jnp.where(real, h_ref[...] - tgt, 0.0)
        part_ref[...] += _colsum(err * err) * (0.5 / D)
        dy = err * (1.0 / D)
        xhat = xh_ref[...]
        dz = _ln_bwd_rows(dy, xhat, rs_ref[...], g_ref[...])
        dz_ref[...] = dz
        dzb_ref[...] = dz.astype(BF16)
        dg_ref[...] += _colsum(dy * xhat)
        db_ref[...] += _colsum(dy)

    row = lambda i: (i, 0)
    const = lambda i: (0, 0)
    t_specs = [pl.BlockSpec((BLK, D), functools.partial(lambda i, q: (jnp.clip(i * per - 1 + q, 0, last_blk), 0), q=q))
               for q in range(per)]
    return pl.pallas_call(
        body, name=f"loss_post_ln_bwd{l}", grid=(T // tm,),
        in_specs=[pl.BlockSpec((tm, D), row)] + t_specs + [
            pl.BlockSpec((tm, D), row), pl.BlockSpec((tm, 1), row), pl.BlockSpec((None, 1, D), lambda i: (l, 0, 0))],
        out_specs=[pl.BlockSpec((1, D), const), pl.BlockSpec((tm, D), row), pl.BlockSpec((tm, D), row),
                   pl.BlockSpec((1, D), const), pl.BlockSpec((1, D), const)],
        out_shape=[jax.ShapeDtypeStruct((1, D), F32), jax.ShapeDtypeStruct((T, D), F32),
                   jax.ShapeDtypeStruct((T, D), BF16), jax.ShapeDtypeStruct((1, D), F32),
                   jax.ShapeDtypeStruct((1, D), F32)],
        compiler_params=_cp("arbitrary"),
    )(h, *([target] * per), xhat, rstd, g)


def _dcat_bwd(dzb, w_out, l, job=None):
    T = dzb.shape[0]
    tm = _pick(T, (384, 128))

    def body(dz_ref, w_ref, dc_ref, da_ref, dl_ref):
        dz = dz_ref[...]
        dc_ref[...] = _dot_nt(dz, w_ref[0])
        da_ref[:, 0:WOUT_SH] = _dot_nt(dz, w_ref[1])
        da_ref[:, WOUT_SH:2 * WOUT_SH] = _dot_nt(dz, w_ref[2])
        dl_ref[...] = _dot_nt(dz, w_ref[3])

    row = lambda i: (i, 0)
    return _side_call(
        body, job, name=f"dcat_bwd{l}", grid=(T // tm,),
        in_specs=[pl.BlockSpec((tm, D), row),
                  pl.BlockSpec((N_SHARD, WOUT_SH, D), lambda i: (0, 0, 0))],
        out_specs=[pl.BlockSpec((tm, CW), row), pl.BlockSpec((tm, AW), row), pl.BlockSpec((tm, LW), row)],
        out_shape=[jax.ShapeDtypeStruct((T, CW), F32), jax.ShapeDtypeStruct((T, AW), F32),
                   jax.ShapeDtypeStruct((T, LW), F32)],
        scratch_shapes=[], semantics=("parallel",), args=[dzb, w_out])


def _dwout_bwd(yc, ya, yl, dzb, l):
    T = dzb.shape[0]
    tm = _pick(T, (384, 128))

    def body(yc_ref, ya_ref, yl_ref, dz_ref, o_ref):
        @pl.when(pl.program_id(0) == 0)
        def _():
            o_ref[...] = jnp.zeros_like(o_ref)

        cat = jnp.concatenate([yc_ref[...], ya_ref[...], yl_ref[...]], axis=1)
        o_ref[...] += _dot_tn(cat, dz_ref[...])

    row = lambda t: (t, 0)
    out = pl.pallas_call(
        body, name=f"dwout_bwd{l}", grid=(T // tm,),
        in_specs=[pl.BlockSpec((tm, CW), row), pl.BlockSpec((tm, AW), row), pl.BlockSpec((tm, LW), row),
                  pl.BlockSpec((tm, D), row)],
        out_specs=pl.BlockSpec((D, D), lambda t: (0, 0)),
        out_shape=jax.ShapeDtypeStruct((D, D), F32),
        compiler_params=_cp("arbitrary"),
    )(yc, ya, yl, dzb)
    return out.reshape(N_SHARD, 2, WOUT_SH // 2, D)


def _dh_bwd(dproj, w_in, dz, l, job=None):
    T = dproj.shape[0]
    tm = _pick(T, (1056, 384, 128))

    def body(dp_ref, w_ref, dz_ref, o_ref, acc_ref):
        j = pl.program_id(1)

        @pl.when(j == 0)
        def _():
            acc_ref[...] = ALPHA * dz_ref[...]

        acc_ref[...] += _dot_nt(dp_ref[...], w_ref[...])

        @pl.when(j == N_SHARD - 1)
        def _():
            o_ref[...] = acc_ref[...]

    return _side_call(
        body, job, name=f"dh_bwd{l}", grid=(T // tm, N_SHARD),
        in_specs=[pl.BlockSpec((tm, WIN_SH), lambda i, j: (i, j)),
                  pl.BlockSpec((None, D, WIN_SH), lambda i, j: (j, 0, 0)),
                  pl.BlockSpec((tm, D), lambda i, j: (i, 0))],
        out_specs=[pl.BlockSpec((tm, D), lambda i, j: (i, 0))],
        out_shape=[jax.ShapeDtypeStruct((T, D), F32)],
        scratch_shapes=[pltpu.VMEM((tm, D), F32)],
        semantics=("parallel", "arbitrary"), args=[dproj, w_in, dz])


def _dwin_bwd(hb, dproj, l):
    T = hb.shape[0]
    tm = _pick(T, (1056, 384, 128))

    def body(h_ref, dp_ref, o_ref):
        @pl.when(pl.program_id(1) == 0)
        def _():
            o_ref[...] = jnp.zeros_like(o_ref)

        o_ref[...] += _dot_tn(h_ref[...], dp_ref[...])

    out = pl.pallas_call(
        body, name=f"dwin_bwd{l}", grid=(N_SHARD, T // tm),
        in_specs=[pl.BlockSpec((tm, D), lambda j, t: (t, 0)),
                  pl.BlockSpec((tm, WIN_SH), lambda j, t: (t, j))],
        out_specs=pl.BlockSpec((None, D, WIN_SH), lambda j, t: (j, 0, 0)),
        out_shape=jax.ShapeDtypeStruct((N_SHARD, D, WIN_SH), F32),
        compiler_params=_cp("parallel", "arbitrary"),
    )(hb, dproj)
    return out.reshape(N_SHARD, 2, D // 2, WIN_SH)


def _glu_masked(v, g, base_row):
    rows = _row_ids(v.shape[0], base_row)
    return jnp.where(rows >= PAD, v * _sigmoid(g), 0.0)


def _conv_tile(T):
    return _pick(T, (384, 128))


SUBLANES = 8


def _for_each_shift(buf, rot, tm, offsets, fn):
    for r in range(SUBLANES):
        group = [o for o in offsets if o % SUBLANES == r]
        if not group:
            continue
        if r == 0:
            src = buf
        else:
            n = tm + max(group) - r
            rot[0:n, :] = buf[r:r + n, :]
            src = rot
        for o in group:
            fn(o, src[o - r:o - r + tm, :])


def _conv_fwd(proj, dw_w, dw_b, ln_g, ln_b, pw_w, pw_b, l):
    T = proj.shape[0]
    tm = _conv_tile(T)
    hb = tm // HALO

    def body(cv_ref, cg_ref, ct_ref, hv_ref, hg_ref, w_ref, b_ref, g_ref, be_ref, pw_ref, pb_ref,
             yc_ref, conv_ref, buf, rot):
        i = pl.program_id(0)
        buf[0:HALO, :] = _glu_masked(hv_ref[...], hg_ref[...], i * tm - HALO)
        buf[HALO:HALO + tm, :] = _glu_masked(cv_ref[...], cg_ref[...], i * tm)
        first = HALO - (CONV_K - 1)
        total = [jnp.zeros((tm, CW), F32) + b_ref[...]]

        def tap(o, tile):
            k = o - first
            total[0] = total[0] + w_ref[k:k + 1, :] * tile

        _for_each_shift(buf, rot, tm, [first + k for k in range(CONV_K)], tap)
        acc = total[0]
        conv_ref[...] = acc
        u, _, _ = _ln_rows(acc, g_ref[...], be_ref[...])
        s = u * _sigmoid(u)
        cpw = _dot(s.astype(BF16), pw_ref[...]) + pb_ref[...]
        gate, _ = _silu_and_grad(ct_ref[...])
        yc_ref[...] = (cpw * gate).astype(BF16)

    vec = pl.BlockSpec((None, 1, CW), lambda i: (l, 0, 0))
    return pl.pallas_call(
        body, name=f"conv_fwd{l}", grid=(T // tm,),
        in_specs=[pl.BlockSpec((tm, CW), lambda i: (i, 0)),
                  pl.BlockSpec((tm, CW), lambda i: (i, 1)),
                  pl.BlockSpec((tm, CW), lambda i: (i, 2)),
                  pl.BlockSpec((HALO, CW), lambda i: (jnp.maximum(i * hb - 1, 0), 0)),
                  pl.BlockSpec((HALO, CW), lambda i: (jnp.maximum(i * hb - 1, 0), 1)),
                  pl.BlockSpec((None, CONV_K, CW), lambda i: (l, 0, 0)),
                  vec, vec, vec,
                  pl.BlockSpec((CW, CW), lambda i: (0, 0)),
                  vec],
        out_specs=[pl.BlockSpec((tm, CW), lambda i: (i, 0)), pl.BlockSpec((tm, CW), lambda i: (i, 0))],
        out_shape=[jax.ShapeDtypeStruct((T, CW), BF16), jax.ShapeDtypeStruct((T, CW), F32)],
        scratch_shapes=[pltpu.VMEM((tm + HALO, CW), F32), pltpu.VMEM((tm + HALO, CW), F32)],
        compiler_params=_cp("parallel"),
    )(proj, proj, proj, proj, proj, dw_w, dw_b, ln_g, ln_b, pw_w, pw_b)


def _conv_bwd_rows(conv, proj, d_yc, ln_g, ln_b, pw_w, pw_b, l):
    T = conv.shape[0]
    tm = _conv_tile(T)

    def body(conv_ref, ct_ref, dy_ref, g_ref, be_ref, pw_ref, pb_ref,
             dconv_ref, dct_ref, dpw_ref, dpb_ref, dg_ref, db_ref):
        @pl.when(pl.program_id(0) == 0)
        def _():
            dpw_ref[...] = jnp.zeros_like(dpw_ref)
            dpb_ref[...] = jnp.zeros_like(dpb_ref)
            dg_ref[...] = jnp.zeros_like(dg_ref)
            db_ref[...] = jnp.zeros_like(db_ref)

        u, xhat, rstd = _ln_rows(conv_ref[...], g_ref[...], be_ref[...])
        s, ds_du = _silu_and_grad(u)
        sb = s.astype(BF16)
        cpw = _dot(sb, pw_ref[...]) + pb_ref[...]
        gate, dgate = _silu_and_grad(ct_ref[...])
        dy = dy_ref[...]
        d_cpw = dy * gate
        dct_ref[...] = (dy * cpw * dgate).astype(BF16)
        d_cpw_b = d_cpw.astype(BF16)
        dpb_ref[...] += _colsum(d_cpw)
        dpw_ref[...] += _dot_tn(sb, d_cpw_b)
        du = _dot_nt(d_cpw_b, pw_ref[...]) * ds_du
        dconv_ref[...] = _ln_bwd_rows(du, xhat, rstd, g_ref[...])
        dg_ref[...] += _colsum(du * xhat)
        db_ref[...] += _colsum(du)

    vec = pl.BlockSpec((None, 1, CW), lambda i: (l, 0, 0))
    row = lambda i: (i, 0)
    const = lambda i: (0, 0)
    return pl.pallas_call(
        body, name=f"conv_bwd_rows{l}", grid=(T // tm,),
        in_specs=[pl.BlockSpec((tm, CW), row), pl.BlockSpec((tm, CW), lambda i: (i, 2)),
                  pl.BlockSpec((tm, CW), row), vec, vec,
                  pl.BlockSpec((CW, CW), lambda i: (0, 0)), vec],
        out_specs=[pl.BlockSpec((tm, CW), row), pl.BlockSpec((tm, CW), lambda i: (i, 2)),
                   pl.BlockSpec((CW, CW), const), pl.BlockSpec((1, CW), const),
                   pl.BlockSpec((1, CW), const), pl.BlockSpec((1, CW), const)],
        out_shape=[jax.ShapeDtypeStruct((T, CW), F32), jax.ShapeDtypeStruct((T, IN_TOTAL), BF16),
                   jax.ShapeDtypeStruct((CW, CW), F32), jax.ShapeDtypeStruct((1, CW), F32),
                   jax.ShapeDtypeStruct((1, CW), F32), jax.ShapeDtypeStruct((1, CW), F32)],
        compiler_params=_cp("arbitrary"),
    )(conv, proj, d_yc, ln_g, ln_b, pw_w, pw_b)


def _conv_bwd_taps(d_conv, proj, dw_w, dproj, l, job=None):
    T = d_conv.shape[0]
    tm = _conv_tile(T)
    hb = tm // HALO
    nt = T // tm
    last_halo = T // HALO - 1

    def body(dc_ref, dh_ref, cv_ref, cg_ref, hv_ref, hg_ref, w_ref, _, o_ref, dw_ref, dwb_ref, cbuf, dbuf, rot):
        i = pl.program_id(0)

        @pl.when(i == 0)
        def _():
            dw_ref[...] = jnp.zeros_like(dw_ref)
            dwb_ref[...] = jnp.zeros_like(dwb_ref)

        cbuf[0:HALO, :] = _glu_masked(hv_ref[...], hg_ref[...], i * tm - HALO)
        cbuf[HALO:HALO + tm, :] = _glu_masked(cv_ref[...], cg_ref[...], i * tm)
        dmain = dc_ref[...]
        dbuf[0:tm, :] = dmain
        dbuf[tm:tm + HALO, :] = jnp.where(i < nt - 1, dh_ref[...], 0.0)
        total = [jnp.zeros((tm, CW), F32)]

        def tap_back(o, tile):
            k = CONV_K - 1 - o
            total[0] = total[0] + w_ref[k:k + 1, :] * tile

        _for_each_shift(dbuf, rot, tm, list(range(CONV_K)), tap_back)
        acc = total[0]
        first = HALO - (CONV_K - 1)

        def tap_weight(o, tile):
            k = o - first
            dw_ref[k:k + 1, :] += _colsum(dmain * tile)

        _for_each_shift(cbuf, rot, tm, [first + k for k in range(CONV_K)], tap_weight)
        dwb_ref[...] += _colsum(dmain)
        d_c = jnp.where(_row_ids(tm, i * tm) >= PAD, acc, 0.0)
        sig = _sigmoid(cg_ref[...])
        o_ref[:, 0:CW] = (d_c * sig).astype(BF16)
        o_ref[:, CW:2 * CW] = (d_c * cv_ref[...] * sig * (1.0 - sig)).astype(BF16)

    const = lambda i: (0, 0)
    return _side_call(
        body, job, name=f"conv_bwd_taps{l}", grid=(nt,),
        in_specs=[pl.BlockSpec((tm, CW), lambda i: (i, 0)),
                  pl.BlockSpec((HALO, CW), lambda i: (jnp.minimum((i + 1) * hb, last_halo), 0)),
                  pl.BlockSpec((tm, CW), lambda i: (i, 0)),
                  pl.BlockSpec((tm, CW), lambda i: (i, 1)),
                  pl.BlockSpec((HALO, CW), lambda i: (jnp.maximum(i * hb - 1, 0), 0)),
                  pl.BlockSpec((HALO, CW), lambda i: (jnp.maximum(i * hb - 1, 0), 1)),
                  pl.BlockSpec((None, CONV_K, CW), lambda i: (l, 0, 0)),
                  pl.BlockSpec(memory_space=pl.ANY)],
        out_specs=[pl.BlockSpec((tm, 2 * CW), lambda i: (i, 0)),
                   pl.BlockSpec((HALO, CW), const), pl.BlockSpec((1, CW), const)],
        out_shape=[jax.ShapeDtypeStruct(dproj.shape, BF16), jax.ShapeDtypeStruct((HALO, CW), F32),
                   jax.ShapeDtypeStruct((1, CW), F32)],
        scratch_shapes=[pltpu.VMEM((tm + HALO, CW), F32), pltpu.VMEM((tm + HALO, CW), F32),
                        pltpu.VMEM((tm + HALO, CW), F32)],
        semantics=("arbitrary",), aliases={7: 0},
        args=[d_conv, d_conv, proj, proj, proj, proj, dw_w, dproj])


def _log1p_small(e):
    return jnp.where(e < 1e-3, e * (1.0 - e * (0.5 - e * (1.0 / 3.0))), jnp.log(1.0 + e))


def _softplus(z):
    return jnp.maximum(z, 0.0) + _log1p_small(jnp.exp(-jnp.abs(z)))


def _neg_expm1(x):
    series = -x * (1.0 + x * (1.0 / 2.0) * (1.0 + x * (1.0 / 3.0) * (1.0 + x * (1.0 / 4.0) * (
        1.0 + x * (1.0 / 5.0) * (1.0 + x * (1.0 / 6.0) * (1.0 + x * (1.0 / 7.0)))))))
    return jnp.where(x > -0.25, series, 1.0 - jnp.exp(x))


def _lru_gates(rxbuf, tm, base_row, lw_ref, lb_ref, wa_ref, ba_ref, wx_ref, bx_ref, lam_ref):
    rc = jnp.zeros((tm, LW), F32) + lb_ref[...]
    for k in range(LRU_K):
        o = LHALO - (LRU_K - 1) + k
        rc += lw_ref[k:k + 1, :] * rxbuf[o:o + tm, :]
    rcb = rc.astype(BF16)
    r = _sigmoid(_dot(rcb, wa_ref[...]) + ba_ref[...])
    ig = _sigmoid(_dot(rcb, wx_ref[...]) + bx_ref[...])
    sp = _softplus(-lam_ref[...])
    la = -LRU_C * r * sp
    a = jnp.exp(la)
    mult = jnp.sqrt(_neg_expm1(2.0 * la))
    valid = _row_ids(tm, base_row) >= PAD
    return rc, rcb, r, ig, sp, a, mult, valid


def _mask_rows(v, base_row):
    return jnp.where(_row_ids(v.shape[0], base_row) >= PAD, v, 0.0)


def _scan_steps(tm):
    s, out = 1, []
    while s < tm:
        out.append(s)
        s *= 2
    return out


def _lru_tile(T):
    return _pick(T, (384, 128))


def _lru_fwd(proj, lw, lb, wa, ba, wx, bx, lam, l):
    T = proj.shape[0]
    tm = _lru_tile(T)
    hb = tm // LHALO

    def body(rx_ref, rg_ref, hx_ref, lw_ref, lb_ref, wa_ref, ba_ref, wx_ref, bx_ref, lam_ref,
             yl_ref, hl_ref, rxbuf, carry):
        i = pl.program_id(0)

        @pl.when(i == 0)
        def _():
            carry[...] = jnp.zeros_like(carry)

        rxbuf[0:LHALO, :] = _mask_rows(hx_ref[...], i * tm - LHALO)
        rxbuf[LHALO:LHALO + tm, :] = _mask_rows(rx_ref[...], i * tm)
        rc, _, _, ig, _, a, mult, valid = _lru_gates(rxbuf, tm, i * tm, lw_ref, lb_ref, wa_ref, ba_ref,
                                                     wx_ref, bx_ref, lam_ref)
        bb = jnp.where(valid, mult * (ig * rc), 0.0)
        aa = a
        rows = _row_ids(tm, 0)
        for s in _scan_steps(tm):
            keep = rows >= s
            a_s = jnp.where(keep, pltpu.roll(aa, s, axis=0), 1.0)
            b_s = jnp.where(keep, pltpu.roll(bb, s, axis=0), 0.0)
            bb = aa * b_s + bb
            aa = aa * a_s
        h = bb + aa * carry[0:1, :]
        hl_ref[...] = h
        carry[0:1, :] = hl_ref[tm - 1:tm, :]
        gate, _ = _silu_and_grad(rg_ref[...])
        yl_ref[...] = (h * gate).astype(BF16)

    vec = pl.BlockSpec((None, 1, LW), lambda i: (l, 0, 0))
    mat = pl.BlockSpec((None, LW, LW), lambda i: (l, 0, 0))
    return pl.pallas_call(
        body, name=f"lru_fwd{l}", grid=(T // tm,),
        in_specs=[pl.BlockSpec((tm, LW), lambda i: (i, 8)),
                  pl.BlockSpec((tm, LW), lambda i: (i, 9)),
                  pl.BlockSpec((LHALO, LW), lambda i: (jnp.maximum(i * hb - 1, 0), 8)),
                  pl.BlockSpec((None, LRU_K, LW), lambda i: (l, 0, 0)),
                  vec, mat, vec, mat, vec, vec],
        out_specs=[pl.BlockSpec((tm, LW), lambda i: (i, 0)), pl.BlockSpec((tm, LW), lambda i: (i, 0))],
        out_shape=[jax.ShapeDtypeStruct((T, LW), BF16), jax.ShapeDtypeStruct((T, LW), F32)],
        scratch_shapes=[pltpu.VMEM((tm + LHALO, LW), F32), pltpu.VMEM((8, LW), F32)],
        compiler_params=_cp("arbitrary"),
    )(proj, proj, proj, lw, lb, wa, ba, wx, bx, lam)


def _lru_bwd(proj, hl, d_yl, lw, lb, wa, ba, wx, bx, lam, dproj, l, job=None):
    T = proj.shape[0]
    tm = _lru_tile(T)
    hb = tm // LHALO
    nt = T // tm

    def body(rx_ref, rg_ref, hx_ref, hl_ref, hh_ref, dy_ref, lw_ref, lb_ref, wa_ref, ba_ref, wx_ref, bx_ref,
             lam_ref, _, o_ref, dlw_ref, dlb_ref, dwa_ref, dba_ref, dwx_ref, dbx_ref, dlam_ref,
             rxbuf, dbuf, carry, head):
        step = pl.program_id(0)
        i = nt - 1 - step

        @pl.when(step == 0)
        def _():
            carry[...] = jnp.zeros_like(carry)
            head[...] = jnp.zeros_like(head)
            for ref in (dlw_ref, dlb_ref, dwa_ref, dba_ref, dwx_ref, dbx_ref, dlam_ref):
                ref[...] = jnp.zeros_like(ref)

        rxbuf[0:LHALO, :] = _mask_rows(hx_ref[...], i * tm - LHALO)
        rxbuf[LHALO:LHALO + tm, :] = _mask_rows(rx_ref[...], i * tm)
        rc, rcb, r, ig, sp, a, mult, valid = _lru_gates(rxbuf, tm, i * tm, lw_ref, lb_ref, wa_ref, ba_ref,
                                                        wx_ref, bx_ref, lam_ref)
        rows = _row_ids(tm, 0)
        h = hl_ref[...]
        h_before = jnp.where(i > 0, hh_ref[LHALO - 1:LHALO, :], 0.0)
        hprev = jnp.where(rows == 0, h_before, pltpu.roll(h, 1, axis=0))
        rg = rg_ref[...]
        gate, dgate = _silu_and_grad(rg)
        dy = dy_ref[...]
        o_ref[:, LW:2 * LW] = (dy * h * dgate).astype(BF16)
        bb = dy * gate + jnp.where(rows == tm - 1, carry[0:1, :], 0.0)
        aa = jnp.where(rows == tm - 1, 0.0, pltpu.roll(a, tm - 1, axis=0))
        for s in _scan_steps(tm):
            keep = rows < tm - s
            a_s = jnp.where(keep, pltpu.roll(aa, tm - s, axis=0), 1.0)
            b_s = jnp.where(keep, pltpu.roll(bb, tm - s, axis=0), 0.0)
            bb = aa * b_s + bb
            aa = aa * a_s
        g = bb
        dbuf[0:tm, :] = a * g
        carry[0:1, :] = dbuf[0:1, :]
        du = jnp.where(valid, g, 0.0)
        da = g * hprev
        dix = du * mult
        dmult = du * (ig * rc)
        dla = jnp.where(valid, da * a - dmult * (a * a) / mult, 0.0)
        dr = dla * (-LRU_C * sp)
        dlam_ref[...] += _colsum(dla * (LRU_C * r)) * _sigmoid(-lam_ref[...])
        dpa = dr * r * (1.0 - r)
        dpx = (dix * rc) * ig * (1.0 - ig)
        dpab = dpa.astype(BF16)
        dpxb = dpx.astype(BF16)
        dba_ref[...] += _colsum(dpa)
        dbx_ref[...] += _colsum(dpx)
        dwa_ref[...] += _dot_tn(rcb, dpab)
        dwx_ref[...] += _dot_tn(rcb, dpxb)
        drc = dix * ig + _dot_nt(dpab, wa_ref[...]) + _dot_nt(dpxb, wx_ref[...])
        dbuf[0:tm, :] = drc
        dbuf[tm:tm + LHALO, :] = head[...]
        acc = jnp.zeros((tm, LW), F32)
        for k in range(LRU_K):
            o = LRU_K - 1 - k
            acc += lw_ref[k:k + 1, :] * dbuf[o:o + tm, :]
            oc = LHALO - (LRU_K - 1) + k
            dlw_ref[k:k + 1, :] += _colsum(drc * rxbuf[oc:oc + tm, :])
        dlb_ref[...] += _colsum(drc)
        head[...] = dbuf[0:LHALO, :]
        o_ref[:, 0:LW] = jnp.where(valid, acc, 0.0).astype(BF16)

    rev = lambda s: nt - 1 - s
    vec = pl.BlockSpec((None, 1, LW), lambda s: (l, 0, 0))
    mat = pl.BlockSpec((None, LW, LW), lambda s: (l, 0, 0))
    const = lambda s: (0, 0)
    halo = lambda s: jnp.maximum(rev(s) * hb - 1, 0)
    return _side_call(
        body, job, name=f"lru_bwd{l}", grid=(nt,),
        in_specs=[pl.BlockSpec((tm, LW), lambda s: (rev(s), 8)),
                  pl.BlockSpec((tm, LW), lambda s: (rev(s), 9)),
                  pl.BlockSpec((LHALO, LW), lambda s: (halo(s), 8)),
                  pl.BlockSpec((tm, LW), lambda s: (rev(s), 0)),
                  pl.BlockSpec((LHALO, LW), lambda s: (halo(s), 0)),
                  pl.BlockSpec((tm, LW), lambda s: (rev(s), 0)),
                  pl.BlockSpec((None, LRU_K, LW), lambda s: (l, 0, 0)),
                  vec, mat, vec, mat, vec, vec, pl.BlockSpec(memory_space=pl.ANY)],
        out_specs=[pl.BlockSpec((tm, 2 * LW), lambda s: (rev(s), 4)),
                   pl.BlockSpec((8, LW), const), pl.BlockSpec((1, LW), const),
                   pl.BlockSpec((LW, LW), const), pl.BlockSpec((1, LW), const),
                   pl.BlockSpec((LW, LW), const), pl.BlockSpec((1, LW), const),
                   pl.BlockSpec((1, LW), const)],
        out_shape=[jax.ShapeDtypeStruct(dproj.shape, BF16),
                   jax.ShapeDtypeStruct((8, LW), F32), jax.ShapeDtypeStruct((1, LW), F32),
                   jax.ShapeDtypeStruct((LW, LW), F32), jax.ShapeDtypeStruct((1, LW), F32),
                   jax.ShapeDtypeStruct((LW, LW), F32), jax.ShapeDtypeStruct((1, LW), F32),
                   jax.ShapeDtypeStruct((1, LW), F32)],
        scratch_shapes=[pltpu.VMEM((tm + LHALO, LW), F32), pltpu.VMEM((tm + LHALO, LW), F32),
                        pltpu.VMEM((8, LW), F32), pltpu.VMEM((LHALO, LW), F32)],
        semantics=("arbitrary",), aliases={13: 0},
        args=[proj, proj, proj, hl, hl, d_yl, lw, lb, wa, ba, wx, bx, lam, dproj])


def _rope_tables(T):
    pos = (lax.broadcasted_iota(jnp.int32, (T, 128), 0) - PAD).astype(F32)
    lane = lax.broadcasted_iota(jnp.int32, (T, 128), 1) % 64
    inv_freq = ROPE_THETA ** (-(lane % ROT_HALF).astype(F32) / ROT_HALF)
    ang = pos * inv_freq
    cos, sin = jnp.cos(ang), jnp.sin(ang)
    c = jnp.where(lane < 2 * ROT_HALF, cos, 1.0)
    s1 = jnp.where(lane < ROT_HALF, -sin, 0.0)
    s2 = jnp.where((lane >= ROT_HALF) & (lane < 2 * ROT_HALF), sin, 0.0)
    return c, s1, s2


def _rot_fwd(x, c, s1, s2):
    return x * c + pltpu.roll(x, 128 - ROT_HALF, axis=1) * s1 + pltpu.roll(x, ROT_HALF, axis=1) * s2


def _rot_bwd(dy, c, s1, s2):
    return dy * c + pltpu.roll(dy * s1, ROT_HALF, axis=1) + pltpu.roll(dy * s2, 128 - ROT_HALF, axis=1)


def _rope_fwd(proj, tabs, l):
    T = proj.shape[0]

    def body(ql_ref, qh_ref, k_ref, v_ref, c_ref, s1_ref, s2_ref, qr_ref, kr_ref, vb_ref):
        c, s1, s2 = c_ref[...], s1_ref[...], s2_ref[...]
        for gcol in range(AW // 128):
            src = ql_ref if gcol < 4 else qh_ref
            x = src[:, 128 * (gcol % 4):128 * (gcol % 4) + 128]
            qr_ref[:, 128 * gcol:128 * gcol + 128] = (_rot_fwd(x, c, s1, s2) * 0.125).astype(BF16)
        for gcol in range(KVW // 128):
            x = k_ref[:, 128 * gcol:128 * gcol + 128]
            kr_ref[:, 128 * gcol:128 * gcol + 128] = _rot_fwd(x, c, s1, s2).astype(BF16)
        vb_ref[...] = v_ref[...].astype(BF16)

    tr = _pick(T, (384, 128))
    tab = pl.BlockSpec((tr, 128), lambda n: (n, 0))
    return pl.pallas_call(
        body, name=f"rope_fwd{l}", grid=(T // tr,),
        in_specs=[pl.BlockSpec((tr, 512), lambda n: (n, 3)), pl.BlockSpec((tr, 512), lambda n: (n, 4)),
                  pl.BlockSpec((tr, KVW), lambda n: (n, 10)), pl.BlockSpec((tr, KVW), lambda n: (n, 11)),
                  tab, tab, tab],
        out_specs=[pl.BlockSpec((tr, AW), lambda n: (n, 0)), pl.BlockSpec((tr, KVW), lambda n: (n, 0)),
                   pl.BlockSpec((tr, KVW), lambda n: (n, 0))],
        out_shape=[jax.ShapeDtypeStruct((T, AW), BF16), jax.ShapeDtypeStruct((T, KVW), BF16),
                   jax.ShapeDtypeStruct((T, KVW), BF16)],
        compiler_params=_cp("parallel"),
    )(proj, proj, proj, proj, *tabs)


GROUP = 4


def _attn_mask(n, reps):
    qi = lax.broadcasted_iota(jnp.int32, (reps * BLK, BLK), 0) & (BLK - 1)
    kj = lax.broadcasted_iota(jnp.int32, (reps * BLK, BLK), 1)
    m0 = (kj >= PAD) & (n >= 1)
    mp = (kj > qi) & (n >= 2)
    mc = (kj <= qi) & ((n >= 1) | (kj >= PAD))
    return jnp.concatenate([m0, mp, mc], axis=1)


def _kv_both(x0_ref, xp_ref, xc_ref, g):
    pg, off = g // 2, g % 2
    cols = slice(128 * pg, 128 * pg + 128)
    x = jnp.concatenate([x0_ref[:, cols], xp_ref[:, cols], xc_ref[:, cols]], axis=0).astype(F32)
    lane = lax.broadcasted_iota(jnp.int32, (1, 128), 1)
    half = jnp.where((lane < 64) if off == 0 else (lane >= 64), x, 0.0)
    return (half + pltpu.roll(half, 64, axis=1)).astype(BF16)


def _kv_halves(x0_ref, xp_ref, xc_ref, g):
    pg, off = g // 2, g % 2
    cols = slice(128 * pg, 128 * pg + 128)
    x = jnp.concatenate([x0_ref[:, cols], xp_ref[:, cols], xc_ref[:, cols]], axis=0).astype(F32)
    lane = lax.broadcasted_iota(jnp.int32, (1, 128), 1)
    if off == 0:
        lo = jnp.where(lane < 64, x, 0.0)
        hi = pltpu.roll(lo, 64, axis=1)
    else:
        hi = jnp.where(lane >= 64, x, 0.0)
        lo = pltpu.roll(hi, 64, axis=1)
    return lo.astype(BF16), hi.astype(BF16)


def _stack_heads(a, b):
    lo = lax.broadcasted_iota(jnp.int32, (1, 128), 1) < 64
    a, b = a.astype(F32), b.astype(F32)
    return jnp.concatenate([jnp.where(lo, a, 0.0), jnp.where(lo, 0.0, a),
                            jnp.where(lo, b, 0.0), jnp.where(lo, 0.0, b)], axis=0).astype(BF16)


def _unstack_heads(x):
    lo = lax.broadcasted_iota(jnp.int32, (1, 128), 1) < 64
    return (jnp.where(lo, x[0:BLK], x[BLK:2 * BLK]), jnp.where(lo, x[2 * BLK:3 * BLK], x[3 * BLK:4 * BLK]))


def _per_head_column(values):
    return jnp.concatenate([jnp.zeros((BLK, 1), F32) + v for v in values], axis=0)


def _attn_fwd(qr, kr, vb, proj, sinks, l, job=None):
    T = qr.shape[0]

    def body(sink_ref, q_ref, k0_ref, kp_ref, kc_ref, v0_ref, vp_ref, vc_ref, ag_ref, ya_ref, att_ref, lse_ref):
        n = pl.program_id(0)
        mask = _attn_mask(n, 1)
        lane = lax.broadcasted_iota(jnp.int32, (1, 128), 1)
        lse_acc = jnp.zeros((BLK, 128), F32)
        for g in range(4):
            k_lo, k_hi = _kv_halves(k0_ref, kp_ref, kc_ref, g)
            v_lo, v_hi = _kv_halves(v0_ref, vp_ref, vc_ref, g)
            for pp in range(2):
                cols = slice(128 * (2 * g + pp), 128 * (2 * g + pp) + 128)
                qpair = q_ref[:, cols]
                out = jnp.zeros((BLK, 128), F32)
                for hh, (kx, vx) in enumerate(((k_lo, v_lo), (k_hi, v_hi))):
                    h = 4 * g + 2 * pp + hh
                    sink = sink_ref[l, h]
                    s = jnp.where(mask, _dot_nt(qpair, kx), NEG_INF)
                    m = jnp.maximum(jnp.max(s, axis=1, keepdims=True), sink)
                    p = jnp.exp(s - m)
                    denom = jnp.sum(p, axis=1, keepdims=True) + jnp.exp(sink - m)
                    out += _dot((p / denom).astype(BF16), vx)
                    lse_acc = jnp.where(lane == h, m + jnp.log(denom), lse_acc)
                att_ref[:, cols] = out
                gate, _ = _silu_and_grad(ag_ref[:, cols])
                ya_ref[:, cols] = (out * gate).astype(BF16)
        lse_ref[...] = lse_acc

    prev = lambda n: (jnp.maximum(n - 1, 0), 0)
    cur = lambda n: (n, 0)
    zero = lambda n: (0, 0)
    kv = lambda f: pl.BlockSpec((BLK, KVW), f)
    return _side_call(
        body, job, name=f"attn_fwd{l}", grid=(T // BLK,),
        in_specs=[pl.BlockSpec(memory_space=pltpu.SMEM),
                  pl.BlockSpec((BLK, AW), cur), kv(zero), kv(prev), kv(cur), kv(zero), kv(prev), kv(cur),
                  pl.BlockSpec((BLK, AW), lambda n: (n, 3))],
        out_specs=[pl.BlockSpec((BLK, AW), cur), pl.BlockSpec((BLK, AW), cur), pl.BlockSpec((BLK, 128), cur)],
        out_shape=[jax.ShapeDtypeStruct((T, AW), BF16), jax.ShapeDtypeStruct((T, AW), F32),
                   jax.ShapeDtypeStruct((T, 128), F32)],
        scratch_shapes=[], semantics=("parallel",), args=[sinks, qr, kr, kr, kr, vb, vb, vb, proj])


def _attn_bwd(qr, kr, vb, proj, att, lse, d_ya, sinks, dproj, l, job=None):
    T = qr.shape[0]
    nb = T // BLK

    def body(sink_ref, q_ref, k0_ref, kp_ref, kc_ref, v0_ref, vp_ref, vc_ref, ag_ref, att_ref, lse_ref, dy_ref, _,
             dq_ref, dk_ref, dv_ref, dk0_ref, dv0_ref, dag_ref, dsink_ref, kcarry, vcarry):
        n = pl.program_id(0)

        @pl.when(n == 0)
        def _():
            dk0_ref[...] = jnp.zeros_like(dk0_ref)
            dv0_ref[...] = jnp.zeros_like(dv0_ref)
            dsink_ref[...] = jnp.zeros_like(dsink_ref)
            kcarry[...] = jnp.zeros_like(kcarry)
            vcarry[...] = jnp.zeros_like(vcarry)

        @pl.when(n == nb)
        def _():
            dk_ref[...] = kcarry[...]
            dv_ref[...] = vcarry[...]

        @pl.when(n < nb)
        def _():
            mask = _attn_mask(n, GROUP)
            lane = lax.broadcasted_iota(jnp.int32, (1, 128), 1)
            lse = lse_ref[...]
            dsink = jnp.zeros((1, 128), F32)
            dk_pg, dv_pg = [], []
            for pg in range(2):
                dk_acc = jnp.zeros((3 * BLK, 128), F32)
                dv_acc = jnp.zeros((3 * BLK, 128), F32)
                for off in range(2):
                    g = 2 * pg + off
                    kx = _kv_both(k0_ref, kp_ref, kc_ref, g)
                    vx = _kv_both(v0_ref, vp_ref, vc_ref, g)
                    pair_cols = [slice(128 * (2 * g + pp), 128 * (2 * g + pp) + 128) for pp in range(2)]
                    q4 = _stack_heads(q_ref[:, pair_cols[0]], q_ref[:, pair_cols[1]])
                    d_out = []
                    for cols in pair_cols:
                        gate, dgate = _silu_and_grad(ag_ref[:, cols])
                        dy = dy_ref[:, cols]
                        dag_ref[:, cols] = (dy * att_ref[:, cols] * dgate).astype(BF16)
                        d_out.append(dy * gate)
                    do4 = _stack_heads(d_out[0], d_out[1])
                    heads = [GROUP * g + r for r in range(GROUP)]
                    sink = _per_head_column([sink_ref[l, h] for h in heads])
                    lse4 = _per_head_column(
                        [jnp.sum(jnp.where(lane == h, lse, 0.0), axis=1, keepdims=True) for h in heads])
                    p = jnp.where(mask, jnp.exp(_dot_nt(q4, kx) - lse4), 0.0)
                    dp = _dot_nt(do4, vx)
                    delta = jnp.sum(p * dp, axis=1, keepdims=True)
                    ds = (p * (dp - delta)).astype(BF16)
                    sink_term = jnp.exp(sink - lse4) * delta
                    for r, h in enumerate(heads):
                        dsink += jnp.where(lane == h, -jnp.sum(sink_term[BLK * r:BLK * r + BLK]), 0.0)
                    for cols, dq in zip(pair_cols, _unstack_heads(_dot(ds, kx))):
                        dq_ref[:, cols] = dq
                    dkg = _dot_tn(ds, q4)
                    dvg = _dot_tn(p.astype(BF16), do4)
                    own = (lane < 64) if off == 0 else (lane >= 64)
                    dk_acc += jnp.where(own, dkg + pltpu.roll(dkg, 64, axis=1), 0.0)
                    dv_acc += jnp.where(own, dvg + pltpu.roll(dvg, 64, axis=1), 0.0)
                dk_pg.append(dk_acc)
                dv_pg.append(dv_acc)
            dsink_ref[...] += dsink
            for pg in range(2):
                cols = slice(128 * pg, 128 * pg + 128)
                dk0_ref[:, cols] += dk_pg[pg][0:BLK]
                dv0_ref[:, cols] += dv_pg[pg][0:BLK]
                dk_ref[:, cols] = kcarry[:, cols] + dk_pg[pg][BLK:2 * BLK]
                dv_ref[:, cols] = vcarry[:, cols] + dv_pg[pg][BLK:2 * BLK]
                kcarry[:, cols] = dk_pg[pg][2 * BLK:3 * BLK]
                vcarry[:, cols] = dv_pg[pg][2 * BLK:3 * BLK]

    last = nb - 1
    cur = lambda n: (jnp.minimum(n, last), 0)
    prev = lambda n: (jnp.clip(n - 1, 0, last), 0)
    zero = lambda n: (0, 0)
    kv = lambda f: pl.BlockSpec((BLK, KVW), f)
    wide = lambda f: pl.BlockSpec((BLK, AW), f)
    return _side_call(
        body, job, name=f"attn_bwd{l}", grid=(nb + 1,),
        in_specs=[pl.BlockSpec(memory_space=pltpu.SMEM),
                  wide(cur), kv(zero), kv(prev), kv(cur), kv(zero), kv(prev), kv(cur),
                  pl.BlockSpec((BLK, AW), lambda n: (jnp.minimum(n, last), 3)),
                  wide(cur), pl.BlockSpec((BLK, 128), cur), wide(cur), pl.BlockSpec(memory_space=pl.ANY)],
        out_specs=[wide(cur), kv(prev), kv(prev), kv(zero), kv(zero),
                   pl.BlockSpec((BLK, AW), lambda n: (jnp.minimum(n, last), 3)),
                   pl.BlockSpec((1, 128), zero)],
        out_shape=[jax.ShapeDtypeStruct((T, AW), F32), jax.ShapeDtypeStruct((T, KVW), F32),
                   jax.ShapeDtypeStruct((T, KVW), F32), jax.ShapeDtypeStruct((BLK, KVW), F32),
                   jax.ShapeDtypeStruct((BLK, KVW), F32), jax.ShapeDtypeStruct(dproj.shape, BF16),
                   jax.ShapeDtypeStruct((1, 128), F32)],
        scratch_shapes=[pltpu.VMEM((BLK, KVW), F32), pltpu.VMEM((BLK, KVW), F32)],
        semantics=("arbitrary",), aliases={12: 5},
        args=[sinks, qr, kr, kr, kr, vb, vb, vb, proj, att, lse, d_ya, dproj])


def _rope_bwd(dqr, dk, dv, dk0, dv0, tabs, dproj, l):
    T = dqr.shape[0]

    def body(dq_ref, dk_ref, dv_ref, dk0_ref, dv0_ref, c_ref, s1_ref, s2_ref, _, o_ref):
        n = pl.program_id(0)
        c, s1, s2 = c_ref[...], s1_ref[...], s2_ref[...]
        for gcol in range(AW // 128):
            cols = slice(128 * gcol, 128 * gcol + 128)
            o_ref[:, cols] = (_rot_bwd(dq_ref[:, cols], c, s1, s2) * 0.125).astype(BF16)
        for gcol in range(KVW // 128):
            cols = slice(128 * gcol, 128 * gcol + 128)
            kcols = slice(AW + 128 * gcol, AW + 128 * gcol + 128)
            vcols = slice(AW + KVW + 128 * gcol, AW + KVW + 128 * gcol + 128)
            o_ref[:, kcols] = _rot_bwd(dk_ref[:, cols], c, s1, s2).astype(BF16)
            o_ref[:, vcols] = dv_ref[:, cols].astype(BF16)

            @pl.when(n == 0)
            def _():
                dkk = dk_ref[0:BLK, cols] + dk0_ref[:, cols]
                o_ref[0:BLK, kcols] = _rot_bwd(dkk, c[0:BLK], s1[0:BLK], s2[0:BLK]).astype(BF16)
                o_ref[0:BLK, vcols] = (dv_ref[0:BLK, cols] + dv0_ref[:, cols]).astype(BF16)

    tr = _pick(T, (384, 128))
    cur = lambda n: (n, 0)
    zero = lambda n: (0, 0)
    tab = pl.BlockSpec((tr, 128), cur)
    return pl.pallas_call(
        body, name=f"rope_bwd{l}", grid=(T // tr,),
        in_specs=[pl.BlockSpec((tr, AW), cur), pl.BlockSpec((tr, KVW), cur), pl.BlockSpec((tr, KVW), cur),
                  pl.BlockSpec((BLK, KVW), zero), pl.BlockSpec((BLK, KVW), zero), tab, tab, tab,
                  pl.BlockSpec(memory_space=pl.ANY)],
        out_specs=pl.BlockSpec((tr, AW + 2 * KVW), lambda n: (n, 1)),
        out_shape=jax.ShapeDtypeStruct(dproj.shape, BF16),
        input_output_aliases={8: 0},
        compiler_params=_cp("parallel"),
    )(dqr, dk, dv, dk0, dv0, *tabs, dproj)


def _block_diag(w):
    nl, nh, hd, _ = w.shape
    eye = jnp.eye(nh, dtype=w.dtype)
    return jnp.einsum("lhij,hg->lhigj", w, eye).reshape(nl, nh * hd, nh * hd)


def _diag_blocks(m):
    nh, hd = 8, 64
    return jnp.einsum("hihj->hij", m.reshape(nh, hd, nh, hd))


def _device_step(x, target, p, dist=None):
    vec = lambda a: a.reshape(DEPTH, 1, a.shape[-1])
    ln_in_g, ln_in_b = p["ln_in_g"].reshape(1, D), p["ln_in_b"].reshape(1, D)
    conv_dw_b, conv_ln_g, conv_ln_b, conv_pw_b = map(vec, (p["conv_dw_b"], p["conv_ln_g"], p["conv_ln_b"], p["conv_pw_b"]))
    lru_conv_b, lru_ba, lru_bx, lru_lambda = map(vec, (p["lru_conv_b"], p["lru_ba"], p["lru_bx"], p["lru_lambda"]))
    ln_post_g, ln_post_b = vec(p["ln_post_g"]), vec(p["ln_post_b"])
    wa_bd = _block_diag(p["lru_wa"]).astype(BF16)
    wx_bd = _block_diag(p["lru_wx"]).astype(BF16)
    w_in, w_out, pw_w = list(p["w_in"]), list(p["w_out"]), list(p["conv_pw_w"])
    sinks = p["attn_sinks"]
    big_names = ("w_in", "w_out", "conv_pw_w")

    (h, hb), got = _embed_fwd(x, p["meta_tokens"], ln_in_g, ln_in_b, job=_gather_job([w_in[0]]) if dist else None)
    if dist:
        w_in[0] = got[0]
    T = h.shape[0]
    tabs = _rope_tables(T)
    saved = []
    for l in range(DEPTH):
        (proj,), got = _proj_fwd(hb, w_in[l], l, job=_gather_job([w_out[0], pw_w[0]]) if dist and l == 0 else None)
        if got:
            w_out[0], pw_w[0] = got
        pw_l = pw_w[l].reshape(CW, CW)
        yc, conv = _conv_fwd(proj, p["conv_dw_w"], conv_dw_b, conv_ln_g, conv_ln_b, pw_l, conv_pw_b, l)
        qr, kr, vb = _rope_fwd(proj, tabs, l)
        (ya, att, lse), got = _attn_fwd(
            qr, kr, vb, proj, sinks, l, job=_gather_job([w_in[1], w_out[1], pw_w[1]]) if dist and l == 0 else None)
        if got:
            w_in[1], w_out[1], pw_w[1] = got
        yl, hl = _lru_fwd(proj, p["lru_conv_w"], lru_conv_b, wa_bd, lru_ba, wx_bd, lru_bx, lru_lambda, l)
        hn, hnb, xhat, rstd = _out_fwd(yc, ya, yl, w_out[l], h, ln_post_g, ln_post_b, l)
        saved.append((hb, proj, yc, conv, qr, kr, vb, ya, att, lse, yl, hl, xhat, rstd, pw_l))
        h, hb = hn, hnb

    dh = None
    g = {}
    later = None
    early, last = ("w_out", "conv_pw_w"), ("w_in",)
    own = {}
    for l in reversed(range(DEPTH)):
        hb_l, proj, yc, conv, qr, kr, vb, ya, att, lse, yl, hl, xhat, rstd, pw_l = saved[l]
        tail = dist is not None and l == 0
        if l == DEPTH - 1:
            loss_part, dz, dzb, g["ln_post_g", l], g["ln_post_b", l] = _loss_post_ln_bwd(
                h, target, xhat, rstd, ln_post_g, l)
        else:
            dz, dzb, g["ln_post_g", l], g["ln_post_b", l] = _post_ln_bwd(dh, xhat, rstd, ln_post_g, l)
        (d_yc, d_ya, d_yl), recv = _dcat_bwd(dzb, w_out[l], l, job=_swap_job(later["grads"]) if later else None)
        if later:
            later["parts"], later["owns"] = _chip_partials(big_names, later["grads"], recv, dist, later["l"])
        g["w_out", l] = _dwout_bwd(yc, ya, yl, dzb, l)
        d_conv, dproj, dpw, g["conv_pw_b", l], g["conv_ln_g", l], g["conv_ln_b", l] = _conv_bwd_rows(
            conv, proj, d_yc, conv_ln_g, conv_ln_b, pw_l, conv_pw_b, l)
        g["conv_pw_w", l] = dpw.reshape(N_SHARD, 2, PW_SH // 2, CW)
        if tail:
            own["early"] = dict(l=0, grads=[g[name, 0] for name in early])
        job = None
        if tail:
            job = _join_jobs(_swap_job(own["early"]["grads"]), _scatter_job(later["parts"][1:]))
        (dproj, ddw, g["conv_dw_b", l]), got = _conv_bwd_taps(d_conv, proj, p["conv_dw_w"], dproj, l, job=job)
        if tail:
            n_early = len(early)
            own["early"]["parts"], own["early"]["owns"] = _chip_partials(
                early, own["early"]["grads"], got[:n_early], dist, 0)
            later["z"] = got[n_early:]
        g["conv_dw_w", l] = ddw[:CONV_K]
        (dqr, dk, dv, dk0, dv0, dproj, dsink), z = _attn_bwd(
            qr, kr, vb, proj, att, lse, d_ya, sinks, dproj, l,
            job=_scatter_job(later["parts"][:1]) if later else None)
        if later:
            later["z"] = z + later["z"]
        g["attn_sinks", l] = dsink[0, :N_HEADS]
        dproj = _rope_bwd(dqr, dk, dv, dk0, dv0, tabs, dproj, l)
        (dproj, dlw, g["lru_conv_b", l], dwa, g["lru_ba", l], dwx, g["lru_bx", l], g["lru_lambda", l]), z = _lru_bwd(
            proj, hl, d_yl, p["lru_conv_w"], lru_conv_b, wa_bd, lru_ba, wx_bd, lru_bx, lru_lambda, dproj, l,
            job=_scatter_job(own["early"]["parts"]) if tail else None)
        if tail:
            own["early"]["z"] = z
        g["lru_conv_w", l] = dlw[:LRU_K]
        g["lru_wa", l] = _diag_blocks(dwa)
        g["lru_wx", l] = _diag_blocks(dwx)
        g["w_in", l] = _dwin_bwd(hb_l, dproj, l)
        job = None
        if tail:
            own["last"] = dict(l=0, grads=[g["w_in", 0]])
            pack_a = _pack_rows([_layer_stack(g, name) for name in _SMALL_LAYERED])
            job = _join_jobs(_swap_job(own["last"]["grads"]), _spread_job(pack_a))
        (dh,), got = _dh_bwd(dproj, w_in[l], dz, l, job=job)
        if tail:
            own["last"]["parts"], own["last"]["owns"] = _chip_partials(last, own["last"]["grads"], got[:1], dist, 0)
            g["pack_layered", -1] = _sum_slots(pack_a, got[1], dist[3], "layered")
        if later:
            _finish_reduce(big_names, later, dist, g)
            later = None
        if dist and l > 0:
            later = dict(l=l, grads=[g[name, l] for name in big_names])
    grad_x, g["meta_tokens", -1], g["ln_in_g", -1], g["ln_in_b", -1] = _embed_bwd(
        dh, x, p["meta_tokens"], ln_in_g, ln_in_b)
    if dist:
        pack_b = _pack_rows([g[name, -1] for name in _SMALL_EMBED])
        got = _run_job(_join_jobs(_scatter_job(own["last"]["parts"]), _spread_job(pack_b)), "scatter_and_spread")
        own["last"]["z"] = got[:1]
        g["pack_embed", -1] = _sum_slots(pack_b, got[1], dist[3], "embed")
        state = dict(l=0, owns=own["last"]["owns"] + own["early"]["owns"], z=own["last"]["z"] + own["early"]["z"])
        _finish_reduce(last + early, state, dist, g)
    return loss_part, grad_x, g


_SMALL_EMBED = ("meta_tokens", "ln_in_g", "ln_in_b")
_SMALL_LAYERED = ("conv_dw_w", "conv_dw_b", "conv_ln_g", "conv_ln_b", "conv_pw_b", "attn_sinks", "lru_conv_w",
                  "lru_conv_b", "lru_wa", "lru_ba", "lru_wx", "lru_bx", "lru_lambda", "ln_post_g", "ln_post_b")


def _layer_stack(g, name):
    return jnp.stack([g[name, l] for l in range(DEPTH)], axis=0)


def _chip_partials(names, grads, recv, dist, l):
    outs = [_chip_partial(a, r, dist[0], dist[1], f"{name}{l}") for name, a, r in zip(names, grads, recv)]
    return [o[0] for o in outs], [o[1] for o in outs]


def _finish_reduce(names, state, dist, g):
    l = state["l"]
    totals = [_shard_total(po, zz, dist[2], f"{name}{l}") for name, po, zz in zip(names, state["owns"], state["z"])]
    full = _run_job(_share_job(totals), f"share_halves{l}")
    for name, f in zip(names, full):
        g[name, l] = f.reshape(2 * f.shape[1], f.shape[2])


MESH = pl.DeviceIdType.MESH
HBM_SPEC = pl.BlockSpec(memory_space=pltpu.HBM)
N_DEV = 8


def _position():
    x, y, c = lax.axis_index("x"), lax.axis_index("y"), lax.axis_index("c")
    return x, y, c


def _other_chips(x, y):
    return [(1 - x, y), (x, 1 - y), (1 - x, 1 - y)]


def _cast_into_slot(a, l, j, tag):
    _, R, C = a.shape
    tb = _pick(R, (512, 128))

    def body(s_ref, a_ref, o_ref):
        o_ref[...] = a_ref[...].astype(BF16)

    grid_spec = pltpu.PrefetchScalarGridSpec(
        num_scalar_prefetch=1, grid=(R // tb,),
        in_specs=[pl.BlockSpec((None, tb, C), lambda t, sc: (l, t, 0))],
        out_specs=pl.BlockSpec((None, tb, C), lambda t, sc: (sc[0], t, 0)))
    return pl.pallas_call(
        body, name=f"cast_into_slot_{tag}{l}", grid_spec=grid_spec,
        out_shape=jax.ShapeDtypeStruct((N_SHARD, R, C), BF16),
        compiler_params=_cp("arbitrary"),
    )(jnp.reshape(j, (1,)).astype(jnp.int32), a)


class _Job:
    def __init__(self, inputs, aliased, extra_out, sems, start, mid, finish):
        self.inputs, self.aliased, self.extra_out, self.sems = list(inputs), aliased, list(extra_out), list(sems)
        self.start, self.mid, self.finish = start, mid, finish

    def out_shapes(self):
        own = [jax.ShapeDtypeStruct(a.shape, a.dtype) for a in self.inputs] if self.aliased else []
        return own + self.extra_out


def _side_call(body, job, *, name, grid, in_specs, out_specs, out_shape, scratch_shapes, semantics, args,
               aliases=None):
    aliases = dict(aliases or {})
    if job is None:
        outs = pl.pallas_call(
            body, name=name, grid=grid, in_specs=in_specs, out_specs=out_specs, out_shape=out_shape,
            scratch_shapes=scratch_shapes, input_output_aliases=aliases, compiler_params=_cp(*semantics))(*args)
        return list(outs), []
    n_in, n_out, n_scr = len(in_specs), len(out_specs), len(scratch_shapes)
    j_in, j_out = len(job.inputs), len(job.out_shapes())
    steps = 1
    for gsize in grid:
        steps *= gsize

    def wrapped(*refs):
        host_in, job_in = refs[:n_in], refs[n_in:n_in + j_in]
        o0 = n_in + j_in
        host_out, job_out = refs[o0:o0 + n_out], refs[o0 + n_out:o0 + n_out + j_out]
        s0 = o0 + n_out + j_out
        host_scr, sems = refs[s0:s0 + n_scr], refs[s0 + n_scr:]
        step = pl.program_id(0)
        for d in range(1, len(grid)):
            step = step * grid[d] + pl.program_id(d)

        @pl.when(step == 0)
        def _():
            job.start(job_in, job_out, sems)

        @pl.when(step == max(steps - 2, 0))
        def _():
            job.mid(job_in, job_out, sems)

        body(*host_in, *host_out, *host_scr)

        @pl.when(step == steps - 1)
        def _():
            job.finish(job_in, job_out, sems)

    if job.aliased:
        aliases.update({n_in + k: n_out + k for k in range(j_in)})
    outs = pl.pallas_call(
        wrapped, name=name, grid=grid,
        in_specs=list(in_specs) + [HBM_SPEC] * j_in, out_specs=list(out_specs) + [HBM_SPEC] * j_out,
        out_shape=list(out_shape) + job.out_shapes(),
        scratch_shapes=list(scratch_shapes) + job.sems, input_output_aliases=aliases,
        compiler_params=_cp(*(["arbitrary"] * len(grid))))(*args, *job.inputs)
    return list(outs[:n_out]), list(outs[n_out:])


def _run_job(job, name):
    return _side_call(lambda: None, job, name=name, grid=(1,), in_specs=[], out_specs=[], out_shape=[],
                      scratch_shapes=[], semantics=("arbitrary",), args=[])[1]


def _gather_job(slots):
    n = len(slots)

    def copies(buf, sems):
        ici_send, ici_recv, d2d_send, d2d_recv = sems
        x, y, c = _position()
        chips = _other_chips(x, y)

        def half(k, slot, which):
            hr = buf[k].shape[1] // 2
            return buf[k].at[slot, pl.ds(pl.multiple_of(which * hr, hr), hr)]

        def over_ici(k, p, slot):
            px, py = chips[p]
            return pltpu.make_async_remote_copy(
                src_ref=half(k, slot, c), dst_ref=half(k, slot, c),
                send_sem=ici_send.at[k * 3 + p], recv_sem=ici_recv.at[k * 3 + p],
                device_id=(px, py, c), device_id_type=MESH)

        def over_d2d(k, p, which):
            px, py = chips[p]
            return pltpu.make_async_remote_copy(
                src_ref=half(k, 2 * px + py, which), dst_ref=half(k, 2 * px + py, which),
                send_sem=d2d_send.at[k * 3 + p], recv_sem=d2d_recv.at[k * 3 + p],
                device_id=(x, y, 1 - c), device_id_type=MESH)

        return over_ici, over_d2d, 2 * x + y, chips, c

    pairs = [(k, p) for k in range(n) for p in range(3)]

    def start(_, buf, sems):
        over_ici, _, mine, _, _ = copies(buf, sems)
        for k, p in pairs:
            over_ici(k, p, mine).start()

    def mid(_, buf, sems):
        over_ici, over_d2d, _, chips, c = copies(buf, sems)
        for k, p in pairs:
            px, py = chips[p]
            over_ici(k, p, 2 * px + py).wait_recv()
            over_d2d(k, p, c).start()

    def finish(_, buf, sems):
        over_ici, over_d2d, mine, _, c = copies(buf, sems)
        for k, p in pairs:
            over_d2d(k, p, 1 - c).wait_recv()
        for k, p in pairs:
            over_ici(k, p, mine).wait_send()
            over_d2d(k, p, c).wait_send()

    return _Job(slots, True, [], [pltpu.SemaphoreType.DMA((3 * n,))] * 4, start, mid, finish)


def _gather_shards(shards):
    n = len(shards)

    def body(*refs):
        src, dst = refs[:n], refs[n:2 * n]
        send_sems, recv_sems, local_sems = refs[2 * n:]
        x, y, c = _position()
        mine = 2 * x + y
        chips = _other_chips(x, y)

        def copy(k, p):
            return pltpu.make_async_remote_copy(
                src_ref=src[k], dst_ref=dst[k].at[mine],
                send_sem=send_sems.at[k * 3 + p], recv_sem=recv_sems.at[k * 3 + p],
                device_id=(*chips[p], c), device_id_type=MESH)

        def arrival(k, p):
            px, py = chips[p]
            return pltpu.make_async_remote_copy(
                src_ref=src[k], dst_ref=dst[k].at[2 * px + py],
                send_sem=send_sems.at[k * 3 + p], recv_sem=recv_sems.at[k * 3 + p],
                device_id=(px, py, c), device_id_type=MESH)

        local = [pltpu.make_async_copy(src[k], dst[k].at[mine], local_sems.at[k]) for k in range(n)]
        for cp in local:
            cp.start()
        for k in range(n):
            for p in range(3):
                copy(k, p).start()
        for k in range(n):
            for p in range(3):
                arrival(k, p).wait_recv()
        for k in range(n):
            for p in range(3):
                copy(k, p).wait_send()
        for cp in local:
            cp.wait()

    return pl.pallas_call(
        body, name="gather_shards",
        in_specs=[HBM_SPEC] * n, out_specs=[HBM_SPEC] * n,
        out_shape=[jax.ShapeDtypeStruct((N_SHARD,) + s.shape, s.dtype) for s in shards],
        scratch_shapes=[pltpu.SemaphoreType.DMA((3 * n,)), pltpu.SemaphoreType.DMA((3 * n,)),
                        pltpu.SemaphoreType.DMA((n,))],
    )(*shards)


def _swap_job(grads):
    n = len(grads)

    def copies(src, dst, sems):
        x, y, c = _position()
        return [pltpu.make_async_remote_copy(
            src_ref=src[k].at[:, 1 - c], dst_ref=dst[k],
            send_sem=sems[0].at[k], recv_sem=sems[1].at[k],
            device_id=(x, y, 1 - c), device_id_type=MESH) for k in range(n)]

    def start(src, dst, sems):
        for cp in copies(src, dst, sems):
            cp.start()

    def finish(src, dst, sems):
        for cp in copies(src, dst, sems):
            cp.wait()

    return _Job(grads, False, [jax.ShapeDtypeStruct((N_SHARD,) + g.shape[2:], F32) for g in grads],
                [pltpu.SemaphoreType.DMA((n,))] * 2, start, lambda *_: None, finish)


def _chip_partial(a, y, c, j, tag):
    _, _, R, C = a.shape
    tr = _pick(R, (256, 64))

    def body(s_ref, a_ref, y_ref, pb_ref, po_ref):
        total = a_ref[...] + y_ref[...]
        pb_ref[...] = total.astype(BF16)

        @pl.when(pl.program_id(1) == s_ref[1])
        def _():
            po_ref[...] = total

    grid_spec = pltpu.PrefetchScalarGridSpec(
        num_scalar_prefetch=1, grid=(R // tr, N_SHARD),
        in_specs=[pl.BlockSpec((None, None, tr, C), lambda t, s, sc: (s, sc[0], t, 0)),
                  pl.BlockSpec((None, tr, C), lambda t, s, sc: (s, t, 0))],
        out_specs=[pl.BlockSpec((None, tr, C), lambda t, s, sc: (s, t, 0)),
                   pl.BlockSpec((tr, C), lambda t, s, sc: (t, 0))])
    return pl.pallas_call(
        body, name=f"chip_partial_{tag}", grid_spec=grid_spec,
        out_shape=[jax.ShapeDtypeStruct((N_SHARD, R, C), BF16), jax.ShapeDtypeStruct((R, C), F32)],
        compiler_params=_cp("arbitrary", "arbitrary"),
    )(jnp.stack([c, j]).astype(jnp.int32), a, y)


def _scatter_job(parts):
    n = len(parts)
    pairs = [(k, p) for k in range(n) for p in range(3)]

    def copy(src, dst, sems, k, p, outgoing):
        x, y, c = _position()
        mine = 2 * x + y
        px, py = _other_chips(x, y)[p]
        theirs = 2 * px + py
        return pltpu.make_async_remote_copy(
            src_ref=src[k].at[theirs if outgoing else mine], dst_ref=dst[k].at[mine if outgoing else theirs],
            send_sem=sems[0].at[k * 3 + p], recv_sem=sems[1].at[k * 3 + p],
            device_id=(px, py, c), device_id_type=MESH)

    def start(src, dst, sems):
        for k, p in pairs:
            copy(src, dst, sems, k, p, True).start()

    def finish(src, dst, sems):
        for k, p in pairs:
            copy(src, dst, sems, k, p, False).wait_recv()
        for k, p in pairs:
            copy(src, dst, sems, k, p, True).wait_send()

    return _Job(parts, False, [jax.ShapeDtypeStruct(pb.shape, BF16) for pb in parts],
                [pltpu.SemaphoreType.DMA((3 * n,))] * 2, start, lambda *_: None, finish)


def _shard_total(own, z, others_c, tag):
    R, C = own.shape
    tr = _pick(R, (256, 64))

    def body(s_ref, o_ref, z0_ref, z1_ref, z2_ref, h_ref):
        h_ref[...] = ((o_ref[...] + z0_ref[...].astype(F32)) + z1_ref[...].astype(F32)) + z2_ref[...].astype(F32)

    zspec = lambda q: pl.BlockSpec((None, tr, C), lambda t, sc: (sc[q], t, 0))
    grid_spec = pltpu.PrefetchScalarGridSpec(
        num_scalar_prefetch=1, grid=(R // tr,),
        in_specs=[pl.BlockSpec((tr, C), lambda t, sc: (t, 0)), zspec(0), zspec(1), zspec(2)],
        out_specs=pl.BlockSpec((None, tr, C), lambda t, sc: (sc[3], t, 0)))
    return pl.pallas_call(
        body, name=f"shard_total_{tag}", grid_spec=grid_spec,
        out_shape=jax.ShapeDtypeStruct((2, R, C), F32),
        compiler_params=_cp("arbitrary"),
    )(others_c, own, z, z, z)


def _share_job(totals):
    n = len(totals)

    def copy(buf, sems, k, which):
        x, y, c = _position()
        return pltpu.make_async_remote_copy(
            src_ref=buf[k].at[which], dst_ref=buf[k].at[which],
            send_sem=sems[0].at[k], recv_sem=sems[1].at[k],
            device_id=(x, y, 1 - c), device_id_type=MESH)

    def start(_, buf, sems):
        c = lax.axis_index("c")
        for k in range(n):
            copy(buf, sems, k, c).start()

    def finish(_, buf, sems):
        c = lax.axis_index("c")
        for k in range(n):
            copy(buf, sems, k, 1 - c).wait_recv()
        for k in range(n):
            copy(buf, sems, k, c).wait_send()

    return _Job(totals, True, [], [pltpu.SemaphoreType.DMA((n,))] * 2, start, lambda *_: None, finish)


def _spread_job(pack):
    def copy(src, dst, sems, m, outgoing):
        x, y, c = _position()
        peer = (x ^ (m >> 2), y ^ ((m >> 1) & 1), c ^ (m & 1))
        slot = 4 * x + 2 * y + c if outgoing else 4 * peer[0] + 2 * peer[1] + peer[2]
        return pltpu.make_async_remote_copy(
            src_ref=src[0], dst_ref=dst[0].at[slot], send_sem=sems[0].at[m - 1], recv_sem=sems[1].at[m - 1],
            device_id=peer, device_id_type=MESH)

    def start(src, dst, sems):
        for m in range(1, N_DEV):
            copy(src, dst, sems, m, True).start()

    def finish(src, dst, sems):
        for m in range(1, N_DEV):
            copy(src, dst, sems, m, False).wait_recv()
        for m in range(1, N_DEV):
            copy(src, dst, sems, m, True).wait_send()

    return _Job([pack], False, [jax.ShapeDtypeStruct((N_DEV,) + pack.shape, F32)],
                [pltpu.SemaphoreType.DMA((N_DEV - 1,))] * 2, start, lambda *_: None, finish)


def _join_jobs(a, b):
    assert not a.aliased and not b.aliased
    n_in, n_out, n_sem = len(a.inputs), len(a.extra_out), len(a.sems)

    def phase(name):
        def run(ins, outs, sems):
            getattr(a, name)(ins[:n_in], outs[:n_out], sems[:n_sem])
            getattr(b, name)(ins[n_in:], outs[n_out:], sems[n_sem:])
        return run

    return _Job(a.inputs + b.inputs, False, a.extra_out + b.extra_out, a.sems + b.sems,
                phase("start"), phase("mid"), phase("finish"))


def _sum_slots(pack, slots, me, tag):
    def body(me_ref, p_ref, s_ref, o_ref):
        acc = None
        for d in range(N_DEV):
            term = jnp.where(me_ref[0] == d, p_ref[...], s_ref[d])
            acc = term if acc is None else acc + term
        o_ref[...] = acc

    vm = pl.BlockSpec(memory_space=pltpu.VMEM)
    return pl.pallas_call(
        body, name=f"sum_slots_{tag}",
        in_specs=[pl.BlockSpec(memory_space=pltpu.SMEM), vm, vm], out_specs=vm,
        out_shape=jax.ShapeDtypeStruct(pack.shape, F32),
        compiler_params=pltpu.CompilerParams(vmem_limit_bytes=V7X_VMEM_LIMIT),
    )(jnp.reshape(me, (1,)).astype(jnp.int32), pack, slots)


def _pack_rows(arrays):
    total = sum(a.size for a in arrays)
    rows = -(-total // 128)
    rows = -(-rows // PACK_ROWS_ALIGN) * PACK_ROWS_ALIGN
    flat = [a.reshape(-1) for a in arrays] + [jnp.zeros((rows * 128 - total,), F32)]
    return jnp.concatenate(flat).reshape(rows, 128)


def _adamw_math(w, g, m, v):
    m = ADAM_B1 * m + (1.0 - ADAM_B1) * g
    v = ADAM_B2 * v + (1.0 - ADAM_B2) * (g * g)
    m_hat = m / (1.0 - ADAM_B1 ** ADAM_STEP)
    v_hat = v / (1.0 - ADAM_B2 ** ADAM_STEP)
    delta = -ADAM_LR * (m_hat / (jnp.sqrt(v_hat) + ADAM_EPS) + ADAM_WD * w)
    return delta, m, v


def _adamw_big(w, g0, g1, m, v, tag):
    _, R, C = w.shape
    tr = _pick(R, (256, 128))

    def body(w_ref, g0_ref, g1_ref, m_ref, v_ref, go_ref, d_ref, mo_ref, vo_ref):
        g = jnp.where(pl.program_id(0) == 0, g0_ref[...], g1_ref[...])
        delta, mn, vn = _adamw_math(w_ref[...], g, m_ref[...], v_ref[...])
        go_ref[...] = g
        d_ref[...] = delta
        mo_ref[...] = mn
        vo_ref[...] = vn

    s3 = pl.BlockSpec((None, tr, C), lambda l, t: (l, t, 0))
    s2 = pl.BlockSpec((tr, C), lambda l, t: (t, 0))
    shp = jax.ShapeDtypeStruct(w.shape, F32)
    return pl.pallas_call(
        body, name=f"adamw_{tag}", grid=(2, R // tr),
        in_specs=[s3, s2, s2, s3, s3], out_specs=[s3, s3, s3, s3],
        out_shape=[shp, shp, shp, shp],
        compiler_params=_cp("parallel", "parallel"),
    )(w, g0, g1, m, v)


def _adamw_small(ws, gs, ms, vs):
    n = len(ws)

    def body(*refs):
        w_r, g_r, m_r, v_r = refs[:n], refs[n:2 * n], refs[2 * n:3 * n], refs[3 * n:4 * n]
        d_o, m_o, v_o = refs[4 * n:5 * n], refs[5 * n:6 * n], refs[6 * n:7 * n]
        for k in range(n):
            delta, mn, vn = _adamw_math(w_r[k][...], g_r[k][...], m_r[k][...], v_r[k][...])
            d_o[k][...] = delta
            m_o[k][...] = mn
            v_o[k][...] = vn

    vm = pl.BlockSpec(memory_space=pltpu.VMEM)
    shapes = [jax.ShapeDtypeStruct(w.shape, F32) for w in ws]
    outs = pl.pallas_call(
        body, name="adamw_small",
        in_specs=[vm] * (4 * n), out_specs=[vm] * (3 * n),
        out_shape=shapes * 3,
    )(*ws, *gs, *ms, *vs)
    return outs[:n], outs[n:2 * n], outs[2 * n:]


_WEIGHTS = ["meta_tokens", "ln_in_g", "ln_in_b", "w_in", "conv_dw_w", "conv_dw_b", "conv_ln_g", "conv_ln_b",
            "conv_pw_w", "conv_pw_b", "attn_sinks", "lru_conv_w", "lru_conv_b", "lru_wa", "lru_ba", "lru_wx",
            "lru_bx", "lru_lambda", "w_out", "ln_post_g", "ln_post_b"]
_BIG = ("w_in", "w_out", "conv_pw_w")
_SMALL_SHARDED = {"meta_tokens": 1, "conv_dw_w": 2, "lru_conv_w": 2}
PACK_ROWS_ALIGN = 8


def _as2d(a):
    return a.reshape(1, -1) if a.ndim == 1 else a.reshape(-1, a.shape[-1])


def kernel(x, meta_tokens, ln_in_g, ln_in_b, w_in, conv_dw_w, conv_dw_b, conv_ln_g, conv_ln_b, conv_pw_w, conv_pw_b, attn_sinks, lru_conv_w, lru_conv_b, lru_wa, lru_ba, lru_wx, lru_bx, lru_lambda, w_out, ln_post_g, ln_post_b, loss_target, m_meta_tokens, m_ln_in_g, m_ln_in_b, m_w_in, m_conv_dw_w, m_conv_dw_b, m_conv_ln_g, m_conv_ln_b, m_conv_pw_w, m_conv_pw_b, m_attn_sinks, m_lru_conv_w, m_lru_conv_b, m_lru_wa, m_lru_ba, m_lru_wx, m_lru_bx, m_lru_lambda, m_w_out, m_ln_post_g, m_ln_post_b, v_meta_tokens, v_ln_in_g, v_ln_in_b, v_w_in, v_conv_dw_w, v_conv_dw_b, v_conv_ln_g, v_conv_ln_b, v_conv_pw_w, v_conv_pw_b, v_attn_sinks, v_lru_conv_w, v_lru_conv_b, v_lru_wa, v_lru_ba, v_lru_wx, v_lru_bx, v_lru_lambda, v_w_out, v_ln_post_g, v_ln_post_b):
    w = dict(meta_tokens=meta_tokens, ln_in_g=ln_in_g, ln_in_b=ln_in_b, w_in=w_in, conv_dw_w=conv_dw_w,
             conv_dw_b=conv_dw_b, conv_ln_g=conv_ln_g, conv_ln_b=conv_ln_b, conv_pw_w=conv_pw_w,
             conv_pw_b=conv_pw_b, attn_sinks=attn_sinks, lru_conv_w=lru_conv_w, lru_conv_b=lru_conv_b,
             lru_wa=lru_wa, lru_ba=lru_ba, lru_wx=lru_wx, lru_bx=lru_bx, lru_lambda=lru_lambda, w_out=w_out,
             ln_post_g=ln_post_g, ln_post_b=ln_post_b)
    mom_m = dict(zip(_WEIGHTS, (m_meta_tokens, m_ln_in_g, m_ln_in_b, m_w_in, m_conv_dw_w, m_conv_dw_b, m_conv_ln_g,
                                m_conv_ln_b, m_conv_pw_w, m_conv_pw_b, m_attn_sinks, m_lru_conv_w, m_lru_conv_b,
                                m_lru_wa, m_lru_ba, m_lru_wx, m_lru_bx, m_lru_lambda, m_w_out, m_ln_post_g,
                                m_ln_post_b)))
    mom_v = dict(zip(_WEIGHTS, (v_meta_tokens, v_ln_in_g, v_ln_in_b, v_w_in, v_conv_dw_w, v_conv_dw_b, v_conv_ln_g,
                                v_conv_ln_b, v_conv_pw_w, v_conv_pw_b, v_attn_sinks, v_lru_conv_w, v_lru_conv_b,
                                v_lru_wa, v_lru_ba, v_lru_wx, v_lru_bx, v_lru_lambda, v_w_out, v_ln_post_g,
                                v_ln_post_b)))
    xi, yi, ci = _position()
    j = 2 * xi + yi

    g_meta, g_dw, g_lc = _gather_shards([meta_tokens, conv_dw_w, lru_conv_w])
    p = dict(w)
    p["w_in"] = [_cast_into_slot(w_in, l, j, "w_in") for l in range(DEPTH)]
    p["w_out"] = [_cast_into_slot(w_out, l, j, "w_out") for l in range(DEPTH)]
    p["conv_pw_w"] = [_cast_into_slot(conv_pw_w, l, j, "conv_pw_w") for l in range(DEPTH)]
    p["meta_tokens"] = g_meta.transpose(1, 0, 2).reshape(N_META, D)
    p["conv_dw_w"] = g_dw.transpose(1, 2, 0, 3).reshape(DEPTH, CONV_K, CW)
    p["lru_conv_w"] = g_lc.transpose(1, 2, 0, 3).reshape(DEPTH, LRU_K, LW)

    others = jnp.stack([jnp.where(j <= 0, 1, 0), jnp.where(j <= 1, 2, 1), jnp.where(j <= 2, 3, 2), ci]).astype(jnp.int32)
    me = 4 * xi + 2 * yi + ci
    loss_part, grad_x, g = _device_step(x[0], loss_target[0], p, dist=(ci, j, others, me))
    loss = lax.psum(jnp.sum(loss_part), ("x", "y", "c"))
    big = {(name, l): g[name, l] for name in _BIG for l in range(DEPTH)}

    small_names = [n for n in _WEIGHTS if n not in _BIG]
    small_g = {}
    for names, red in ((_SMALL_LAYERED, g["pack_layered", -1]), (_SMALL_EMBED, g["pack_embed", -1])):
        red = red.reshape(-1)
        off = 0
        for n in names:
            fshape = list(w[n].shape)
            if n in _SMALL_SHARDED:
                fshape[_SMALL_SHARDED[n]] *= N_SHARD
            sz = 1
            for dim in fshape:
                sz *= dim
            full = red[off:off + sz].reshape(fshape)
            off += sz
            if n in _SMALL_SHARDED:
                ax = _SMALL_SHARDED[n]
                full = lax.dynamic_slice_in_dim(full, j * w[n].shape[ax], w[n].shape[ax], axis=ax)
            small_g[n] = full

    out_g, out_d, out_m, out_v = {}, {}, {}, {}
    for name in _BIG:
        shp = w[name].shape
        to3 = lambda a: a.reshape(DEPTH, -1, shp[-1])
        go, do, mo, vo = _adamw_big(to3(w[name]), big[name, 0], big[name, 1], to3(mom_m[name]), to3(mom_v[name]), name)
        out_g[name], out_d[name], out_m[name], out_v[name] = (a.reshape(shp) for a in (go, do, mo, vo))
    ds, ms, vs = _adamw_small([_as2d(w[n]) for n in small_names], [_as2d(small_g[n]) for n in small_names],
                              [_as2d(mom_m[n]) for n in small_names], [_as2d(mom_v[n]) for n in small_names])
    for n, d_, m_, v_ in zip(small_names, ds, ms, vs):
        out_g[n] = small_g[n]
        out_d[n], out_m[n], out_v[n] = d_.reshape(w[n].shape), m_.reshape(w[n].shape), v_.reshape(w[n].shape)

    return (loss, grad_x[None], *[out_g[n] for n in _WEIGHTS], *[out_d[n] for n in _WEIGHTS],
            *[out_m[n] for n in _WEIGHTS], *[out_v[n] for n in _WEIGHTS])
```

```python
import functools

import jax
import jax.numpy as jnp
from jax import lax
from jax.experimental import pallas as pl
from jax.experimental.pallas import tpu as pltpu

F32 = jnp.float32
BF16 = jnp.bfloat16

D = 2048
N_META = 16
CW = 512
CONV_K = 31
AW = 1024
KVW = 256
N_HEADS = 16
LW = 512
LRU_K = 4
LRU_C = 8.0
IN_TOTAL = 5120
ROT_HALF = 8
ROPE_THETA = 500000.0
LN_EPS = 1e-5
DEPTH = 2
ALPHA = (2.0 * DEPTH) ** 0.25
NEG_INF = -1e30
ADAM_LR, ADAM_B1, ADAM_B2, ADAM_EPS, ADAM_WD, ADAM_STEP = 0.001, 0.9, 0.999, 1e-08, 0.01, 10

BLK = 128
PAD = BLK - N_META
N_SHARD = 4
WIN_SH = IN_TOTAL // N_SHARD
WOUT_SH = D // N_SHARD
PW_SH = CW // N_SHARD
HALO = 32
LHALO = 8
V7X_VMEM_LIMIT = 60 * 1024 * 1024


def _cp(*sem):
    return pltpu.CompilerParams(dimension_semantics=sem if sem else None, vmem_limit_bytes=V7X_VMEM_LIMIT)


def _pick(total, prefs):
    for p in prefs:
        if total % p == 0:
            return p
    raise ValueError(f"no tile for {total}")


def _dot(a, b):
    return jnp.dot(a, b, preferred_element_type=F32)


def _dot_nt(a, b):
    return lax.dot_general(a, b, (((1,), (1,)), ((), ())), preferred_element_type=F32)


def _dot_tn(a, b):
    return lax.dot_general(a, b, (((0,), (0,)), ((), ())), preferred_element_type=F32)


def _sigmoid(x):
    return 1.0 / (1.0 + jnp.exp(-x))


def _silu_and_grad(x):
    s = _sigmoid(x)
    return x * s, s * (1.0 + x * (1.0 - s))


def _ln_rows(x, g, b):
    mu = jnp.mean(x, axis=-1, keepdims=True)
    xc = x - mu
    var = jnp.mean(xc * xc, axis=-1, keepdims=True)
    rstd = lax.rsqrt(var + LN_EPS)
    xhat = xc * rstd
    return xhat * g + b, xhat, rstd


def _ln_bwd_rows(dy, xhat, rstd, g):
    dxh = dy * g
    m1 = jnp.mean(dxh, axis=-1, keepdims=True)
    m2 = jnp.mean(dxh * xhat, axis=-1, keepdims=True)
    return rstd * (dxh - m1 - xhat * m2)


def _row_ids(n, base):
    return base + lax.broadcasted_iota(jnp.int32, (n, 1), 0)


def _colsum(x):
    return jnp.sum(x, axis=0, keepdims=True)


def _embed_fwd(x, meta, g, b, job=None):
    S = x.shape[0]
    nb = S // BLK + 1

    def body(x_ref, meta_ref, g_ref, b_ref, h_ref, hb_ref):
        n = pl.program_id(0)

        @pl.when(n == 0)
        def _():
            y, _, _ = _ln_rows(meta_ref[...], g_ref[...], b_ref[...])
            h_ref[...] = jnp.zeros_like(h_ref)
            h_ref[PAD:BLK, :] = y

        @pl.when(n > 0)
        def _():
            y, _, _ = _ln_rows(x_ref[...], g_ref[...], b_ref[...])
            h_ref[...] = y

        hb_ref[...] = h_ref[...].astype(BF16)

    return _side_call(
        body, job, name="embed_fwd", grid=(nb,),
        in_specs=[pl.BlockSpec((BLK, D), lambda n: (jnp.maximum(n - 1, 0), 0)),
                  pl.BlockSpec((N_META, D), lambda n: (0, 0)),
                  pl.BlockSpec((1, D), lambda n: (0, 0)),
                  pl.BlockSpec((1, D), lambda n: (0, 0))],
        out_specs=[pl.BlockSpec((BLK, D), lambda n: (n, 0)),
                   pl.BlockSpec((BLK, D), lambda n: (n, 0))],
        out_shape=[jax.ShapeDtypeStruct((nb * BLK, D), F32), jax.ShapeDtypeStruct((nb * BLK, D), BF16)],
        scratch_shapes=[], semantics=("arbitrary",), args=[x, meta, g, b])


def _embed_bwd(dh, x, meta, g, b):
    S = x.shape[0]
    nb = S // BLK + 1

    def body(dh_ref, x_ref, meta_ref, g_ref, b_ref, gx_ref, gm_ref, dg_ref, db_ref):
        n = pl.program_id(0)

        @pl.when(n == 0)
        def _():
            _, xhat, rstd = _ln_rows(meta_ref[...], g_ref[...], b_ref[...])
            dy = dh_ref[PAD:BLK, :]
            gm_ref[...] = _ln_bwd_rows(dy, xhat, rstd, g_ref[...])
            dg_ref[...] = _colsum(dy * xhat)
            db_ref[...] = _colsum(dy)

        @pl.when(n > 0)
        def _():
            _, xhat, rstd = _ln_rows(x_ref[...], g_ref[...], b_ref[...])
            dy = dh_ref[...]
            gx_ref[...] = _ln_bwd_rows(dy, xhat, rstd, g_ref[...])
            dg_ref[...] += _colsum(dy * xhat)
            db_ref[...] += _colsum(dy)

    prev = lambda n: (jnp.maximum(n - 1, 0), 0)
    const = lambda n: (0, 0)
    return pl.pallas_call(
        body, name="embed_bwd", grid=(nb,),
        in_specs=[pl.BlockSpec((BLK, D), lambda n: (n, 0)),
                  pl.BlockSpec((BLK, D), prev),
                  pl.BlockSpec((N_META, D), const),
                  pl.BlockSpec((1, D), const),
                  pl.BlockSpec((1, D), const)],
        out_specs=[pl.BlockSpec((BLK, D), prev),
                   pl.BlockSpec((N_META, D), const),
                   pl.BlockSpec((1, D), const),
                   pl.BlockSpec((1, D), const)],
        out_shape=[jax.ShapeDtypeStruct((S, D), F32), jax.ShapeDtypeStruct((N_META, D), F32),
                   jax.ShapeDtypeStruct((1, D), F32), jax.ShapeDtypeStruct((1, D), F32)],
        compiler_params=_cp("arbitrary"),
    )(dh, x, meta, g, b)


def _proj_fwd(hb, w_in, l, job=None):
    T = hb.shape[0]
    tm = _pick(T, (1056, 384, 128))

    def body(a_ref, w_ref, o_ref):
        o_ref[...] = _dot(a_ref[...], w_ref[...])

    return _side_call(
        body, job, name=f"proj_fwd{l}", grid=(T // tm, N_SHARD),
        in_specs=[pl.BlockSpec((tm, D), lambda i, j: (i, 0)),
                  pl.BlockSpec((None, D, WIN_SH), lambda i, j: (j, 0, 0))],
        out_specs=[pl.BlockSpec((tm, WIN_SH), lambda i, j: (i, j))],
        out_shape=[jax.ShapeDtypeStruct((T, IN_TOTAL), F32)],
        scratch_shapes=[], semantics=("parallel", "arbitrary"), args=[hb, w_in])


def _out_fwd(yc, ya, yl, w_out, h, g, b, l, job=None):
    T = h.shape[0]
    tm = _pick(T, (384, 128))

    def body(yc_ref, ya_ref, yl_ref, w_ref, h_ref, g_ref, b_ref, hn_ref, hnb_ref, xh_ref, rs_ref):
        acc = _dot(yc_ref[...], w_ref[0])
        acc += _dot(ya_ref[:, 0:WOUT_SH], w_ref[1])
        acc += _dot(ya_ref[:, WOUT_SH:2 * WOUT_SH], w_ref[2])
        acc += _dot(yl_ref[...], w_ref[3])
        z = ALPHA * h_ref[...] + acc
        y, xhat, rstd = _ln_rows(z, g_ref[...], b_ref[...])
        hn_ref[...] = y
        hnb_ref[...] = y.astype(BF16)
        xh_ref[...] = xhat
        rs_ref[...] = rstd

    row = lambda i: (i, 0)
    return _side_call(
        body, job, name=f"out_fwd{l}", grid=(T // tm,),
        in_specs=[pl.BlockSpec((tm, CW), row), pl.BlockSpec((tm, AW), row), pl.BlockSpec((tm, LW), row),
                  pl.BlockSpec((N_SHARD, WOUT_SH, D), lambda i: (0, 0, 0)),
                  pl.BlockSpec((tm, D), row),
                  pl.BlockSpec((None, 1, D), lambda i: (l, 0, 0)),
                  pl.BlockSpec((None, 1, D), lambda i: (l, 0, 0))],
        out_specs=[pl.BlockSpec((tm, D), row), pl.BlockSpec((tm, D), row), pl.BlockSpec((tm, D), row),
                   pl.BlockSpec((tm, 1), row)],
        out_shape=[jax.ShapeDtypeStruct((T, D), F32), jax.ShapeDtypeStruct((T, D), BF16),
                   jax.ShapeDtypeStruct((T, D), F32), jax.ShapeDtypeStruct((T, 1), F32)],
        scratch_shapes=[], semantics=("parallel",), args=[yc, ya, yl, w_out, h, g, b])


def _post_ln_bwd(dhn, xhat, rstd, g, l):
    T = dhn.shape[0]
    tm = _pick(T, (384, 128))

    def body(d_ref, xh_ref, rs_ref, g_ref, dz_ref, dzb_ref, dg_ref, db_ref):
        @pl.when(pl.program_id(0) == 0)
        def _():
            dg_ref[...] = jnp.zeros_like(dg_ref)
            db_ref[...] = jnp.zeros_like(db_ref)

        dy = d_ref[...]
        xhat = xh_ref[...]
        dz = _ln_bwd_rows(dy, xhat, rs_ref[...], g_ref[...])
        dz_ref[...] = dz
        dzb_ref[...] = dz.astype(BF16)
        dg_ref[...] += _colsum(dy * xhat)
        db_ref[...] += _colsum(dy)

    row = lambda i: (i, 0)
    const = lambda i: (0, 0)
    return pl.pallas_call(
        body, name=f"post_ln_bwd{l}", grid=(T // tm,),
        in_specs=[pl.BlockSpec((tm, D), row), pl.BlockSpec((tm, D), row), pl.BlockSpec((tm, 1), row),
                  pl.BlockSpec((None, 1, D), lambda i: (l, 0, 0))],
        out_specs=[pl.BlockSpec((tm, D), row), pl.BlockSpec((tm, D), row),
                   pl.BlockSpec((1, D), const), pl.BlockSpec((1, D), const)],
        out_shape=[jax.ShapeDtypeStruct((T, D), F32), jax.ShapeDtypeStruct((T, D), BF16),
                   jax.ShapeDtypeStruct((1, D), F32), jax.ShapeDtypeStruct((1, D), F32)],
        compiler_params=_cp("arbitrary"),
    )(dhn, xhat, rstd, g)


def _loss_post_ln_bwd(h, target, xhat, rstd, g, l):
    T = h.shape[0]
    tm = _pick(T, (384, 128))
    per = tm // BLK
    last_blk = target.shape[0] // BLK - 1

    def body(h_ref, *refs):
        t_refs, (xh_ref, rs_ref, g_ref, part_ref, dz_ref, dzb_ref, dg_ref, db_ref) = refs[:per], refs[per:]
        i = pl.program_id(0)

        @pl.when(i == 0)
        def _():
            part_ref[...] = jnp.zeros_like(part_ref)
            dg_ref[...] = jnp.zeros_like(dg_ref)
            db_ref[...] = jnp.zeros_like(db_ref)

        tgt = jnp.concatenate([r[...] for r in t_refs], axis=0) if per > 1 else t_refs[0][...]
        real = _row_ids(tm, i * tm) >= BLK
        err = jnp.where(real, h_ref[...] - tgt, 0.0)
        part_ref[...] += _colsum(err * err) * (0.5 / D)
        dy = err * (1.0 / D)
        xhat = xh_ref[...]
        dz = _ln_bwd_rows(dy, xhat, rs_ref[...], g_ref[...])
        dz_ref[...] = dz
        dzb_ref[...] = dz.astype(BF16)
        dg_ref[...] += _colsum(dy * xhat)
        db_ref[...] += _colsum(dy)

    row = lambda i: (i, 0)
    const = lambda i: (0, 0)
    t_specs = [pl.BlockSpec((BLK, D), functools.partial(lambda i, q: (jnp.clip(i * per - 1 + q, 0, last_blk), 0), q=q))
               for q in range(per)]
    return pl.pallas_call(
        body, name=f"loss_post_ln_bwd{l}", grid=(T // tm,),
        in_specs=[pl.BlockSpec((tm, D), row)] + t_specs + [
            pl.BlockSpec((tm, D), row), pl.BlockSpec((tm, 1), row), pl.BlockSpec((None, 1, D), lambda i: (l, 0, 0))],
        out_specs=[pl.BlockSpec((1, D), const), pl.BlockSpec((tm, D), row), pl.BlockSpec((tm, D), row),
                   pl.BlockSpec((1, D), const), pl.BlockSpec((1, D), const)],
        out_shape=[jax.ShapeDtypeStruct((1, D), F32), jax.ShapeDtypeStruct((T, D), F32),
                   jax.ShapeDtypeStruct((T, D), BF16), jax.ShapeDtypeStruct((1, D), F32),
                   jax.ShapeDtypeStruct((1, D), F32)],
        compiler_params=_cp("arbitrary"),
    )(h, *([target] * per), xhat, rstd, g)


def _dcat_bwd(dzb, w_out, l, job=None):
    T = dzb.shape[0]
    tm = _pick(T, (384, 128))

    def body(dz_ref, w_ref, dc_ref, da_ref, dl_ref):
        dz = dz_ref[...]
        dc_ref[...] = _dot_nt(dz, w_ref[0])
        da_ref[:, 0:WOUT_SH] = _dot_nt(dz, w_ref[1])
        da_ref[:, WOUT_SH:2 * WOUT_SH] = _dot_nt(dz, w_ref[2])
        dl_ref[...] = _dot_nt(dz, w_ref[3])

    row = lambda i: (i, 0)
    return _side_call(
        body, job, name=f"dcat_bwd{l}", grid=(T // tm,),
        in_specs=[pl.BlockSpec((tm, D), row),
                  pl.BlockSpec((N_SHARD, WOUT_SH, D), lambda i: (0, 0, 0))],
        out_specs=[pl.BlockSpec((tm, CW), row), pl.BlockSpec((tm, AW), row), pl.BlockSpec((tm, LW), row)],
        out_shape=[jax.ShapeDtypeStruct((T, CW), F32), jax.ShapeDtypeStruct((T, AW), F32),
                   jax.ShapeDtypeStruct((T, LW), F32)],
        scratch_shapes=[], semantics=("parallel",), args=[dzb, w_out])


def _dwout_bwd(yc, ya, yl, dzb, l):
    T = dzb.shape[0]
    tm = _pick(T, (384, 128))

    def body(yc_ref, ya_ref, yl_ref, dz_ref, o_ref):
        @pl.when(pl.program_id(0) == 0)
        def _():
            o_ref[...] = jnp.zeros_like(o_ref)

        cat = jnp.concatenate([yc_ref[...], ya_ref[...], yl_ref[...]], axis=1)
        o_ref[...] += _dot_tn(cat, dz_ref[...])

    row = lambda t: (t, 0)
    out = pl.pallas_call(
        body, name=f"dwout_bwd{l}", grid=(T // tm,),
        in_specs=[pl.BlockSpec((tm, CW), row), pl.BlockSpec((tm, AW), row), pl.BlockSpec((tm, LW), row),
                  pl.BlockSpec((tm, D), row)],
        out_specs=pl.BlockSpec((D, D), lambda t: (0, 0)),
        out_shape=jax.ShapeDtypeStruct((D, D), F32),
        compiler_params=_cp("arbitrary"),
    )(yc, ya, yl, dzb)
    return out.reshape(N_SHARD, 2, WOUT_SH // 2, D)


def _dh_bwd(dproj, w_in, dz, l, job=None):
    T = dproj.shape[0]
    tm = _pick(T, (1056, 384, 128))

    def body(dp_ref, w_ref, dz_ref, o_ref, acc_ref):
        j = pl.program_id(1)

        @pl.when(j == 0)
        def _():
            acc_ref[...] = ALPHA * dz_ref[...]

        acc_ref[...] += _dot_nt(dp_ref[...], w_ref[...])

        @pl.when(j == N_SHARD - 1)
        def _():
            o_ref[...] = acc_ref[...]

    return _side_call(
        body, job, name=f"dh_bwd{l}", grid=(T // tm, N_SHARD),
        in_specs=[pl.BlockSpec((tm, WIN_SH), lambda i, j: (i, j)),
                  pl.BlockSpec((None, D, WIN_SH), lambda i, j: (j, 0, 0)),
                  pl.BlockSpec((tm, D), lambda i, j: (i, 0))],
        out_specs=[pl.BlockSpec((tm, D), lambda i, j: (i, 0))],
        out_shape=[jax.ShapeDtypeStruct((T, D), F32)],
        scratch_shapes=[pltpu.VMEM((tm, D), F32)],
        semantics=("parallel", "arbitrary"), args=[dproj, w_in, dz])


def _dwin_bwd(hb, dproj, l):
    T = hb.shape[0]
    tm = _pick(T, (1056, 384, 128))

    def body(h_ref, dp_ref, o_ref):
        @pl.when(pl.program_id(1) == 0)
        def _():
            o_ref[...] = jnp.zeros_like(o_ref)

        o_ref[...] += _dot_tn(h_ref[...], dp_ref[...])

    out = pl.pallas_call(
        body, name=f"dwin_bwd{l}", grid=(N_SHARD, T // tm),
        in_specs=[pl.BlockSpec((tm, D), lambda j, t: (t, 0)),
                  pl.BlockSpec((tm, WIN_SH), lambda j, t: (t, j))],
        out_specs=pl.BlockSpec((None, D, WIN_SH), lambda j, t: (j, 0, 0)),
        out_shape=jax.ShapeDtypeStruct((N_SHARD, D, WIN_SH), F32),
        compiler_params=_cp("parallel", "arbitrary"),
    )(hb, dproj)
    return out.reshape(N_SHARD, 2, D // 2, WIN_SH)


def _dwin_half(hb, dproj, which, l, tag, job=None):
    T = hb.shape[0]
    tm = _pick(T, (1056, 384, 128))
    hr = D // 2

    def body(w_ref, h_ref, dp_ref, o_ref):
        @pl.when(pl.program_id(1) == 0)
        def _():
            o_ref[...] = jnp.zeros_like(o_ref)

        o_ref[...] += _dot_tn(h_ref[...], dp_ref[...])

    return _side_call(
        body, job, name=f"dwin_{tag}{l}", grid=(N_SHARD, T // tm),
        in_specs=[pl.BlockSpec((tm, hr), lambda j, t, w: (t, w[0])),
                  pl.BlockSpec((tm, WIN_SH), lambda j, t, w: (t, j))],
        out_specs=[pl.BlockSpec((None, hr, WIN_SH), lambda j, t, w: (j, 0, 0))],
        out_shape=[jax.ShapeDtypeStruct((N_SHARD, hr, WIN_SH), F32)],
        scratch_shapes=[], semantics=("parallel", "arbitrary"), args=[hb, dproj],
        prefetch=[jnp.reshape(which, (1,)).astype(jnp.int32)])


def _glu_masked(v, g, base_row):
    rows = _row_ids(v.shape[0], base_row)
    return jnp.where(rows >= PAD, v * _sigmoid(g), 0.0)


def _conv_tile(T):
    return _pick(T, (384, 128))


SUBLANES = 8


def _for_each_shift(buf, rot, tm, offsets, fn):
    for r in range(SUBLANES):
        group = [o for o in offsets if o % SUBLANES == r]
        if not group:
            continue
        if r == 0:
            src = buf
        else:
            n = tm + max(group) - r
            rot[0:n, :] = buf[r:r + n, :]
            src = rot
        for o in group:
            fn(o, src[o - r:o - r + tm, :])


def _conv_fwd(proj, dw_w, dw_b, ln_g, ln_b, pw_w, pw_b, l):
    T = proj.shape[0]
    tm = _conv_tile(T)
    hb = tm // HALO

    def body(cv_ref, cg_ref, ct_ref, hv_ref, hg_ref, w_ref, b_ref, g_ref, be_ref, pw_ref, pb_ref,
             yc_ref, conv_ref, buf, rot):
        i = pl.program_id(0)
        buf[0:HALO, :] = _glu_masked(hv_ref[...], hg_ref[...], i * tm - HALO)
        buf[HALO:HALO + tm, :] = _glu_masked(cv_ref[...], cg_ref[...], i * tm)
        first = HALO - (CONV_K - 1)
        total = [jnp.zeros((tm, CW), F32) + b_ref[...]]

        def tap(o, tile):
            k = o - first
            total[0] = total[0] + w_ref[k:k + 1, :] * tile

        _for_each_shift(buf, rot, tm, [first + k for k in range(CONV_K)], tap)
        acc = total[0]
        conv_ref[...] = acc
        u, _, _ = _ln_rows(acc, g_ref[...], be_ref[...])
        s = u * _sigmoid(u)
        cpw = _dot(s.astype(BF16), pw_ref[...]) + pb_ref[...]
        gate, _ = _silu_and_grad(ct_ref[...])
        yc_ref[...] = (cpw * gate).astype(BF16)

    vec = pl.BlockSpec((None, 1, CW), lambda i: (l, 0, 0))
    return pl.pallas_call(
        body, name=f"conv_fwd{l}", grid=(T // tm,),
        in_specs=[pl.BlockSpec((tm, CW), lambda i: (i, 0)),
                  pl.BlockSpec((tm, CW), lambda i: (i, 1)),
                  pl.BlockSpec((tm, CW), lambda i: (i, 2)),
                  pl.BlockSpec((HALO, CW), lambda i: (jnp.maximum(i * hb - 1, 0), 0)),
                  pl.BlockSpec((HALO, CW), lambda i: (jnp.maximum(i * hb - 1, 0), 1)),
                  pl.BlockSpec((None, CONV_K, CW), lambda i: (l, 0, 0)),
                  vec, vec, vec,
                  pl.BlockSpec((CW, CW), lambda i: (0, 0)),
                  vec],
        out_specs=[pl.BlockSpec((tm, CW), lambda i: (i, 0)), pl.BlockSpec((tm, CW), lambda i: (i, 0))],
        out_shape=[jax.ShapeDtypeStruct((T, CW), BF16), jax.ShapeDtypeStruct((T, CW), F32)],
        scratch_shapes=[pltpu.VMEM((tm + HALO, CW), F32), pltpu.VMEM((tm + HALO, CW), F32)],
        compiler_params=_cp("parallel"),
    )(proj, proj, proj, proj, proj, dw_w, dw_b, ln_g, ln_b, pw_w, pw_b)


def _conv_bwd_rows(conv, proj, d_yc, ln_g, ln_b, pw_w, pw_b, l):
    T = conv.shape[0]
    tm = _conv_tile(T)

    def body(conv_ref, ct_ref, dy_ref, g_ref, be_ref, pw_ref, pb_ref,
             dconv_ref, dct_ref, dpw_ref, dpb_ref, dg_ref, db_ref):
        @pl.when(pl.program_id(0) == 0)
        def _():
            dpw_ref[...] = jnp.zeros_like(dpw_ref)
            dpb_ref[...] = jnp.zeros_like(dpb_ref)
            dg_ref[...] = jnp.zeros_like(dg_ref)
            db_ref[...] = jnp.zeros_like(db_ref)

        u, xhat, rstd = _ln_rows(conv_ref[...], g_ref[...], be_ref[...])
        s, ds_du = _silu_and_grad(u)
        sb = s.astype(BF16)
        cpw = _dot(sb, pw_ref[...]) + pb_ref[...]
        gate, dgate = _silu_and_grad(ct_ref[...])
        dy = dy_ref[...]
        d_cpw = dy * gate
        dct_ref[...] = (dy * cpw * dgate).astype(BF16)
        d_cpw_b = d_cpw.astype(BF16)
        dpb_ref[...] += _colsum(d_cpw)
        dpw_ref[...] += _dot_tn(sb, d_cpw_b)
        du = _dot_nt(d_cpw_b, pw_ref[...]) * ds_du
        dconv_ref[...] = _ln_bwd_rows(du, xhat, rstd, g_ref[...])
        dg_ref[...] += _colsum(du * xhat)
        db_ref[...] += _colsum(du)

    vec = pl.BlockSpec((None, 1, CW), lambda i: (l, 0, 0))
    row = lambda i: (i, 0)
    const = lambda i: (0, 0)
    return pl.pallas_call(
        body, name=f"conv_bwd_rows{l}", grid=(T // tm,),
        in_specs=[pl.BlockSpec((tm, CW), row), pl.BlockSpec((tm, CW), lambda i: (i, 2)),
                  pl.BlockSpec((tm, CW), row), vec, vec,
                  pl.BlockSpec((CW, CW), lambda i: (0, 0)), vec],
        out_specs=[pl.BlockSpec((tm, CW), row), pl.BlockSpec((tm, CW), lambda i: (i, 2)),
                   pl.BlockSpec((CW, CW), const), pl.BlockSpec((1, CW), const),
                   pl.BlockSpec((1, CW), const), pl.BlockSpec((1, CW), const)],
        out_shape=[jax.ShapeDtypeStruct((T, CW), F32), jax.ShapeDtypeStruct((T, IN_TOTAL), BF16),
                   jax.ShapeDtypeStruct((CW, CW), F32), jax.ShapeDtypeStruct((1, CW), F32),
                   jax.ShapeDtypeStruct((1, CW), F32), jax.ShapeDtypeStruct((1, CW), F32)],
        compiler_params=_cp("arbitrary"),
    )(conv, proj, d_yc, ln_g, ln_b, pw_w, pw_b)


def _conv_bwd_taps(d_conv, proj, dw_w, dproj, l, job=None):
    T = d_conv.shape[0]
    tm = _conv_tile(T)
    hb = tm // HALO
    nt = T // tm
    last_halo = T // HALO - 1

    def body(dc_ref, dh_ref, cv_ref, cg_ref, hv_ref, hg_ref, w_ref, _, o_ref, dw_ref, dwb_ref, cbuf, dbuf, rot):
        i = pl.program_id(0)

        @pl.when(i == 0)
        def _():
            dw_ref[...] = jnp.zeros_like(dw_ref)
            dwb_ref[...] = jnp.zeros_like(dwb_ref)

        cbuf[0:HALO, :] = _glu_masked(hv_ref[...], hg_ref[...], i * tm - HALO)
        cbuf[HALO:HALO + tm, :] = _glu_masked(cv_ref[...], cg_ref[...], i * tm)
        dmain = dc_ref[...]
        dbuf[0:tm, :] = dmain
        dbuf[tm:tm + HALO, :] = jnp.where(i < nt - 1, dh_ref[...], 0.0)
        total = [jnp.zeros((tm, CW), F32)]

        def tap_back(o, tile):
            k = CONV_K - 1 - o
            total[0] = total[0] + w_ref[k:k + 1, :] * tile

        _for_each_shift(dbuf, rot, tm, list(range(CONV_K)), tap_back)
        acc = total[0]
        first = HALO - (CONV_K - 1)

        def tap_weight(o, tile):
            k = o - first
            dw_ref[k:k + 1, :] += _colsum(dmain * tile)

        _for_each_shift(cbuf, rot, tm, [first + k for k in range(CONV_K)], tap_weight)
        dwb_ref[...] += _colsum(dmain)
        d_c = jnp.where(_row_ids(tm, i * tm) >= PAD, acc, 0.0)
        sig = _sigmoid(cg_ref[...])
        o_ref[:, 0:CW] = (d_c * sig).astype(BF16)
        o_ref[:, CW:2 * CW] = (d_c * cv_ref[...] * sig * (1.0 - sig)).astype(BF16)

    const = lambda i: (0, 0)
    return _side_call(
        body, job, name=f"conv_bwd_taps{l}", grid=(nt,),
        in_specs=[pl.BlockSpec((tm, CW), lambda i: (i, 0)),
                  pl.BlockSpec((HALO, CW), lambda i: (jnp.minimum((i + 1) * hb, last_halo), 0)),
                  pl.BlockSpec((tm, CW), lambda i: (i, 0)),
                  pl.BlockSpec((tm, CW), lambda i: (i, 1)),
                  pl.BlockSpec((HALO, CW), lambda i: (jnp.maximum(i * hb - 1, 0), 0)),
                  pl.BlockSpec((HALO, CW), lambda i: (jnp.maximum(i * hb - 1, 0), 1)),
                  pl.BlockSpec((None, CONV_K, CW), lambda i: (l, 0, 0)),
                  pl.BlockSpec(memory_space=pl.ANY)],
        out_specs=[pl.BlockSpec((tm, 2 * CW), lambda i: (i, 0)),
                   pl.BlockSpec((HALO, CW), const), pl.BlockSpec((1, CW), const)],
        out_shape=[jax.ShapeDtypeStruct(dproj.shape, BF16), jax.ShapeDtypeStruct((HALO, CW), F32),
                   jax.ShapeDtypeStruct((1, CW), F32)],
        scratch_shapes=[pltpu.VMEM((tm + HALO, CW), F32), pltpu.VMEM((tm + HALO, CW), F32),
                        pltpu.VMEM((tm + HALO, CW), F32)],
        semantics=("arbitrary",), aliases={7: 0},
        args=[d_conv, d_conv, proj, proj, proj, proj, dw_w, dproj])


def _log1p_small(e):
    return jnp.where(e < 1e-3, e * (1.0 - e * (0.5 - e * (1.0 / 3.0))), jnp.log(1.0 + e))


def _softplus(z):
    return jnp.maximum(z, 0.0) + _log1p_small(jnp.exp(-jnp.abs(z)))


def _neg_expm1(x):
    series = -x * (1.0 + x * (1.0 / 2.0) * (1.0 + x * (1.0 / 3.0) * (1.0 + x * (1.0 / 4.0) * (
        1.0 + x * (1.0 / 5.0) * (1.0 + x * (1.0 / 6.0) * (1.0 + x * (1.0 / 7.0)))))))
    return jnp.where(x > -0.25, series, 1.0 - jnp.exp(x))


def _lru_gates(rxbuf, tm, base_row, lw_ref, lb_ref, wa_ref, ba_ref, wx_ref, bx_ref, lam_ref):
    rc = jnp.zeros((tm, LW), F32) + lb_ref[...]
    for k in range(LRU_K):
        o = LHALO - (LRU_K - 1) + k
        rc += lw_ref[k:k + 1, :] * rxbuf[o:o + tm, :]
    rcb = rc.astype(BF16)
    r = _sigmoid(_dot(rcb, wa_ref[...]) + ba_ref[...])
    ig = _sigmoid(_dot(rcb, wx_ref[...]) + bx_ref[...])
    sp = _softplus(-lam_ref[...])
    la = -LRU_C * r * sp
    a = jnp.exp(la)
    mult = jnp.sqrt(_neg_expm1(2.0 * la))
    valid = _row_ids(tm, base_row) >= PAD
    return rc, rcb, r, ig, sp, a, mult, valid


def _mask_rows(v, base_row):
    return jnp.where(_row_ids(v.shape[0], base_row) >= PAD, v, 0.0)


def _scan_steps(tm):
    s, out = 1, []
    while s < tm:
        out.append(s)
        s *= 2
    return out


def _lru_tile(T):
    return _pick(T, (384, 128))


def _lru_fwd(proj, lw, lb, wa, ba, wx, bx, lam, l):
    T = proj.shape[0]
    tm = _lru_tile(T)
    hb = tm // LHALO

    def body(rx_ref, rg_ref, hx_ref, lw_ref, lb_ref, wa_ref, ba_ref, wx_ref, bx_ref, lam_ref,
             yl_ref, hl_ref, rxbuf, carry):
        i = pl.program_id(0)

        @pl.when(i == 0)
        def _():
            carry[...] = jnp.zeros_like(carry)

        rxbuf[0:LHALO, :] = _mask_rows(hx_ref[...], i * tm - LHALO)
        rxbuf[LHALO:LHALO + tm, :] = _mask_rows(rx_ref[...], i * tm)
        rc, _, _, ig, _, a, mult, valid = _lru_gates(rxbuf, tm, i * tm, lw_ref, lb_ref, wa_ref, ba_ref,
                                                     wx_ref, bx_ref, lam_ref)
        bb = jnp.where(valid, mult * (ig * rc), 0.0)
        aa = a
        rows = _row_ids(tm, 0)
        for s in _scan_steps(tm):
            keep = rows >= s
            a_s = jnp.where(keep, pltpu.roll(aa, s, axis=0), 1.0)
            b_s = jnp.where(keep, pltpu.roll(bb, s, axis=0), 0.0)
            bb = aa * b_s + bb
            aa = aa * a_s
        h = bb + aa * carry[0:1, :]
        hl_ref[...] = h
        carry[0:1, :] = hl_ref[tm - 1:tm, :]
        gate, _ = _silu_and_grad(rg_ref[...])
        yl_ref[...] = (h * gate).astype(BF16)

    vec = pl.BlockSpec((None, 1, LW), lambda i: (l, 0, 0))
    mat = pl.BlockSpec((None, LW, LW), lambda i: (l, 0, 0))
    return pl.pallas_call(
        body, name=f"lru_fwd{l}", grid=(T // tm,),
        in_specs=[pl.BlockSpec((tm, LW), lambda i: (i, 8)),
                  pl.BlockSpec((tm, LW), lambda i: (i, 9)),
                  pl.BlockSpec((LHALO, LW), lambda i: (jnp.maximum(i * hb - 1, 0), 8)),
                  pl.BlockSpec((None, LRU_K, LW), lambda i: (l, 0, 0)),
                  vec, mat, vec, mat, vec, vec],
        out_specs=[pl.BlockSpec((tm, LW), lambda i: (i, 0)), pl.BlockSpec((tm, LW), lambda i: (i, 0))],
        out_shape=[jax.ShapeDtypeStruct((T, LW), BF16), jax.ShapeDtypeStruct((T, LW), F32)],
        scratch_shapes=[pltpu.VMEM((tm + LHALO, LW), F32), pltpu.VMEM((8, LW), F32)],
        compiler_params=_cp("arbitrary"),
    )(proj, proj, proj, lw, lb, wa, ba, wx, bx, lam)


def _lru_bwd(proj, hl, d_yl, lw, lb, wa, ba, wx, bx, lam, dproj, l, job=None):
    T = proj.shape[0]
    tm = _lru_tile(T)
    hb = tm // LHALO
    nt = T // tm

    def body(rx_ref, rg_ref, hx_ref, hl_ref, hh_ref, dy_ref, lw_ref, lb_ref, wa_ref, ba_ref, wx_ref, bx_ref,
             lam_ref, _, o_ref, dlw_ref, dlb_ref, dwa_ref, dba_ref, dwx_ref, dbx_ref, dlam_ref,
             rxbuf, dbuf, carry, head):
        step = pl.program_id(0)
        i = nt - 1 - step

        @pl.when(step == 0)
        def _():
            carry[...] = jnp.zeros_like(carry)
            head[...] = jnp.zeros_like(head)
            for ref in (dlw_ref, dlb_ref, dwa_ref, dba_ref, dwx_ref, dbx_ref, dlam_ref):
                ref[...] = jnp.zeros_like(ref)

        rxbuf[0:LHALO, :] = _mask_rows(hx_ref[...], i * tm - LHALO)
        rxbuf[LHALO:LHALO + tm, :] = _mask_rows(rx_ref[...], i * tm)
        rc, rcb, r, ig, sp, a, mult, valid = _lru_gates(rxbuf, tm, i * tm, lw_ref, lb_ref, wa_ref, ba_ref,
                                                        wx_ref, bx_ref, lam_ref)
        rows = _row_ids(tm, 0)
        h = hl_ref[...]
        h_before = jnp.where(i > 0, hh_ref[LHALO - 1:LHALO, :], 0.0)
        hprev = jnp.where(rows == 0, h_before, pltpu.roll(h, 1, axis=0))
        rg = rg_ref[...]
        gate, dgate = _silu_and_grad(rg)
        dy = dy_ref[...]
        o_ref[:, LW:2 * LW] = (dy * h * dgate).astype(BF16)
        bb = dy * gate + jnp.where(rows == tm - 1, carry[0:1, :], 0.0)
        aa = jnp.where(rows == tm - 1, 0.0, pltpu.roll(a, tm - 1, axis=0))
        for s in _scan_steps(tm):
            keep = rows < tm - s
            a_s = jnp.where(keep, pltpu.roll(aa, tm - s, axis=0), 1.0)
            b_s = jnp.where(keep, pltpu.roll(bb, tm - s, axis=0), 0.0)
            bb = aa * b_s + bb
            aa = aa * a_s
        g = bb
        dbuf[0:tm, :] = a * g
        carry[0:1, :] = dbuf[0:1, :]
        du = jnp.where(valid, g, 0.0)
        da = g * hprev
        dix = du * mult
        dmult = du * (ig * rc)
        dla = jnp.where(valid, da * a - dmult * (a * a) / mult, 0.0)
        dr = dla * (-LRU_C * sp)
        dlam_ref[...] += _colsum(dla * (LRU_C * r)) * _sigmoid(-lam_ref[...])
        dpa = dr * r * (1.0 - r)
        dpx = (dix * rc) * ig * (1.0 - ig)
        dpab = dpa.astype(BF16)
        dpxb = dpx.astype(BF16)
        dba_ref[...] += _colsum(dpa)
        dbx_ref[...] += _colsum(dpx)
        dwa_ref[...] += _dot_tn(rcb, dpab)
        dwx_ref[...] += _dot_tn(rcb, dpxb)
        drc = dix * ig + _dot_nt(dpab, wa_ref[...]) + _dot_nt(dpxb, wx_ref[...])
        dbuf[0:tm, :] = drc
        dbuf[tm:tm + LHALO, :] = head[...]
        acc = jnp.zeros((tm, LW), F32)
        for k in range(LRU_K):
            o = LRU_K - 1 - k
            acc += lw_ref[k:k + 1, :] * dbuf[o:o + tm, :]
            oc = LHALO - (LRU_K - 1) + k
            dlw_ref[k:k + 1, :] += _colsum(drc * rxbuf[oc:oc + tm, :])
        dlb_ref[...] += _colsum(drc)
        head[...] = dbuf[0:LHALO, :]
        o_ref[:, 0:LW] = jnp.where(valid, acc, 0.0).astype(BF16)

    rev = lambda s: nt - 1 - s
    vec = pl.BlockSpec((None, 1, LW), lambda s: (l, 0, 0))
    mat = pl.BlockSpec((None, LW, LW), lambda s: (l, 0, 0))
    const = lambda s: (0, 0)
    halo = lambda s: jnp.maximum(rev(s) * hb - 1, 0)
    return _side_call(
        body, job, name=f"lru_bwd{l}", grid=(nt,),
        in_specs=[pl.BlockSpec((tm, LW), lambda s: (rev(s), 8)),
                  pl.BlockSpec((tm, LW), lambda s: (rev(s), 9)),
                  pl.BlockSpec((LHALO, LW), lambda s: (halo(s), 8)),
                  pl.BlockSpec((tm, LW), lambda s: (rev(s), 0)),
                  pl.BlockSpec((LHALO, LW), lambda s: (halo(s), 0)),
                  pl.BlockSpec((tm, LW), lambda s: (rev(s), 0)),
                  pl.BlockSpec((None, LRU_K, LW), lambda s: (l, 0, 0)),
                  vec, mat, vec, mat, vec, vec, pl.BlockSpec(memory_space=pl.ANY)],
        out_specs=[pl.BlockSpec((tm, 2 * LW), lambda s: (rev(s), 4)),
                   pl.BlockSpec((8, LW), const), pl.BlockSpec((1, LW), const),
                   pl.BlockSpec((LW, LW), const), pl.BlockSpec((1, LW), const),
                   pl.BlockSpec((LW, LW), const), pl.BlockSpec((1, LW), const),
                   pl.BlockSpec((1, LW), const)],
        out_shape=[jax.ShapeDtypeStruct(dproj.shape, BF16),
                   jax.ShapeDtypeStruct((8, LW), F32), jax.ShapeDtypeStruct((1, LW), F32),
                   jax.ShapeDtypeStruct((LW, LW), F32), jax.ShapeDtypeStruct((1, LW), F32),
                   jax.ShapeDtypeStruct((LW, LW), F32), jax.ShapeDtypeStruct((1, LW), F32),
                   jax.ShapeDtypeStruct((1, LW), F32)],
        scratch_shapes=[pltpu.VMEM((tm + LHALO, LW), F32), pltpu.VMEM((tm + LHALO, LW), F32),
                        pltpu.VMEM((8, LW), F32), pltpu.VMEM((LHALO, LW), F32)],
        semantics=("arbitrary",), aliases={13: 0},
        args=[proj, proj, proj, hl, hl, d_yl, lw, lb, wa, ba, wx, bx, lam, dproj])


def _rope_tables(T):
    pos = (lax.broadcasted_iota(jnp.int32, (T, 128), 0) - PAD).astype(F32)
    lane = lax.broadcasted_iota(jnp.int32, (T, 128), 1) % 64
    inv_freq = ROPE_THETA ** (-(lane % ROT_HALF).astype(F32) / ROT_HALF)
    ang = pos * inv_freq
    cos, sin = jnp.cos(ang), jnp.sin(ang)
    c = jnp.where(lane < 2 * ROT_HALF, cos, 1.0)
    s1 = jnp.where(lane < ROT_HALF, -sin, 0.0)
    s2 = jnp.where((lane >= ROT_HALF) & (lane < 2 * ROT_HALF), sin, 0.0)
    return c, s1, s2


def _rot_fwd(x, c, s1, s2):
    return x * c + pltpu.roll(x, 128 - ROT_HALF, axis=1) * s1 + pltpu.roll(x, ROT_HALF, axis=1) * s2


def _rot_bwd(dy, c, s1, s2):
    return dy * c + pltpu.roll(dy * s1, ROT_HALF, axis=1) + pltpu.roll(dy * s2, 128 - ROT_HALF, axis=1)


def _rope_fwd(proj, tabs, l):
    T = proj.shape[0]

    def body(ql_ref, qh_ref, k_ref, v_ref, c_ref, s1_ref, s2_ref, qr_ref, kr_ref, vb_ref):
        c, s1, s2 = c_ref[...], s1_ref[...], s2_ref[...]
        for gcol in range(AW // 128):
            src = ql_ref if gcol < 4 else qh_ref
            x = src[:, 128 * (gcol % 4):128 * (gcol % 4) + 128]
            qr_ref[:, 128 * gcol:128 * gcol + 128] = (_rot_fwd(x, c, s1, s2) * 0.125).astype(BF16)
        for gcol in range(KVW // 128):
            x = k_ref[:, 128 * gcol:128 * gcol + 128]
            kr_ref[:, 128 * gcol:128 * gcol + 128] = _rot_fwd(x, c, s1, s2).astype(BF16)
        vb_ref[...] = v_ref[...].astype(BF16)

    tr = _pick(T, (384, 128))
    tab = pl.BlockSpec((tr, 128), lambda n: (n, 0))
    return pl.pallas_call(
        body, name=f"rope_fwd{l}", grid=(T // tr,),
        in_specs=[pl.BlockSpec((tr, 512), lambda n: (n, 3)), pl.BlockSpec((tr, 512), lambda n: (n, 4)),
                  pl.BlockSpec((tr, KVW), lambda n: (n, 10)), pl.BlockSpec((tr, KVW), lambda n: (n, 11)),
                  tab, tab, tab],
        out_specs=[pl.BlockSpec((tr, AW), lambda n: (n, 0)), pl.BlockSpec((tr, KVW), lambda n: (n, 0)),
                   pl.BlockSpec((tr, KVW), lambda n: (n, 0))],
        out_shape=[jax.ShapeDtypeStruct((T, AW), BF16), jax.ShapeDtypeStruct((T, KVW), BF16),
                   jax.ShapeDtypeStruct((T, KVW), BF16)],
        compiler_params=_cp("parallel"),
    )(proj, proj, proj, proj, *tabs)


GROUP = 4


def _attn_mask(n, reps):
    qi = lax.broadcasted_iota(jnp.int32, (reps * BLK, BLK), 0) & (BLK - 1)
    kj = lax.broadcasted_iota(jnp.int32, (reps * BLK, BLK), 1)
    m0 = (kj >= PAD) & (n >= 1)
    mp = (kj > qi) & (n >= 2)
    mc = (kj <= qi) & ((n >= 1) | (kj >= PAD))
    return jnp.concatenate([m0, mp, mc], axis=1)


def _kv_both(x0_ref, xp_ref, xc_ref, g):
    pg, off = g // 2, g % 2
    cols = slice(128 * pg, 128 * pg + 128)
    x = jnp.concatenate([x0_ref[:, cols], xp_ref[:, cols], xc_ref[:, cols]], axis=0).astype(F32)
    lane = lax.broadcasted_iota(jnp.int32, (1, 128), 1)
    half = jnp.where((lane < 64) if off == 0 else (lane >= 64), x, 0.0)
    return (half + pltpu.roll(half, 64, axis=1)).astype(BF16)


def _kv_halves(x0_ref, xp_ref, xc_ref, g):
    pg, off = g // 2, g % 2
    cols = slice(128 * pg, 128 * pg + 128)
    x = jnp.concatenate([x0_ref[:, cols], xp_ref[:, cols], xc_ref[:, cols]], axis=0).astype(F32)
    lane = lax.broadcasted_iota(jnp.int32, (1, 128), 1)
    if off == 0:
        lo = jnp.where(lane < 64, x, 0.0)
        hi = pltpu.roll(lo, 64, axis=1)
    else:
        hi = jnp.where(lane >= 64, x, 0.0)
        lo = pltpu.roll(hi, 64, axis=1)
    return lo.astype(BF16), hi.astype(BF16)


def _stack_heads(a, b):
    lo = lax.broadcasted_iota(jnp.int32, (1, 128), 1) < 64
    a, b = a.astype(F32), b.astype(F32)
    return jnp.concatenate([jnp.where(lo, a, 0.0), jnp.where(lo, 0.0, a),
                            jnp.where(lo, b, 0.0), jnp.where(lo, 0.0, b)], axis=0).astype(BF16)


def _unstack_heads(x):
    lo = lax.broadcasted_iota(jnp.int32, (1, 128), 1) < 64
    return (jnp.where(lo, x[0:BLK], x[BLK:2 * BLK]), jnp.where(lo, x[2 * BLK:3 * BLK], x[3 * BLK:4 * BLK]))


def _per_head_column(values):
    return jnp.concatenate([jnp.zeros((BLK, 1), F32) + v for v in values], axis=0)


def _attn_fwd(qr, kr, vb, proj, sinks, l, job=None):
    T = qr.shape[0]

    def body(sink_ref, q_ref, k0_ref, kp_ref, kc_ref, v0_ref, vp_ref, vc_ref, ag_ref, ya_ref, att_ref, lse_ref):
        n = pl.program_id(0)
        mask = _attn_mask(n, 1)
        lane = lax.broadcasted_iota(jnp.int32, (1, 128), 1)
        lse_acc = jnp.zeros((BLK, 128), F32)
        for g in range(4):
            k_lo, k_hi = _kv_halves(k0_ref, kp_ref, kc_ref, g)
            v_lo, v_hi = _kv_halves(v0_ref, vp_ref, vc_ref, g)
            for pp in range(2):
                cols = slice(128 * (2 * g + pp), 128 * (2 * g + pp) + 128)
                qpair = q_ref[:, cols]
                out = jnp.zeros((BLK, 128), F32)
                for hh, (kx, vx) in enumerate(((k_lo, v_lo), (k_hi, v_hi))):
                    h = 4 * g + 2 * pp + hh
                    sink = sink_ref[l, h]
                    s = jnp.where(mask, _dot_nt(qpair, kx), NEG_INF)
                    m = jnp.maximum(jnp.max(s, axis=1, keepdims=True), sink)
                    p = jnp.exp(s - m)
                    denom = jnp.sum(p, axis=1, keepdims=True) + jnp.exp(sink - m)
                    out += _dot((p / denom).astype(BF16), vx)
                    lse_acc = jnp.where(lane == h, m + jnp.log(denom), lse_acc)
                att_ref[:, cols] = out
                gate, _ = _silu_and_grad(ag_ref[:, cols])
                ya_ref[:, cols] = (out * gate).astype(BF16)
        lse_ref[...] = lse_acc

    prev = lambda n: (jnp.maximum(n - 1, 0), 0)
    cur = lambda n: (n, 0)
    zero = lambda n: (0, 0)
    kv = lambda f: pl.BlockSpec((BLK, KVW), f)
    return _side_call(
        body, job, name=f"attn_fwd{l}", grid=(T // BLK,),
        in_specs=[pl.BlockSpec(memory_space=pltpu.SMEM),
                  pl.BlockSpec((BLK, AW), cur), kv(zero), kv(prev), kv(cur), kv(zero), kv(prev), kv(cur),
                  pl.BlockSpec((BLK, AW), lambda n: (n, 3))],
        out_specs=[pl.BlockSpec((BLK, AW), cur), pl.BlockSpec((BLK, AW), cur), pl.BlockSpec((BLK, 128), cur)],
        out_shape=[jax.ShapeDtypeStruct((T, AW), BF16), jax.ShapeDtypeStruct((T, AW), F32),
                   jax.ShapeDtypeStruct((T, 128), F32)],
        scratch_shapes=[], semantics=("parallel",), args=[sinks, qr, kr, kr, kr, vb, vb, vb, proj])


def _attn_bwd(qr, kr, vb, proj, att, lse, d_ya, sinks, dproj, l, job=None):
    T = qr.shape[0]
    nb = T // BLK

    def body(sink_ref, q_ref, k0_ref, kp_ref, kc_ref, v0_ref, vp_ref, vc_ref, ag_ref, att_ref, lse_ref, dy_ref, _,
             dq_ref, dk_ref, dv_ref, dk0_ref, dv0_ref, dag_ref, dsink_ref, kcarry, vcarry):
        n = pl.program_id(0)

        @pl.when(n == 0)
        def _():
            dk0_ref[...] = jnp.zeros_like(dk0_ref)
            dv0_ref[...] = jnp.zeros_like(dv0_ref)
            dsink_ref[...] = jnp.zeros_like(dsink_ref)
            kcarry[...] = jnp.zeros_like(kcarry)
            vcarry[...] = jnp.zeros_like(vcarry)

        @pl.when(n == nb)
        def _():
            dk_ref[...] = kcarry[...]
            dv_ref[...] = vcarry[...]

        @pl.when(n < nb)
        def _():
            mask = _attn_mask(n, GROUP)
            lane = lax.broadcasted_iota(jnp.int32, (1, 128), 1)
            lse = lse_ref[...]
            dsink = jnp.zeros((1, 128), F32)
            dk_pg, dv_pg = [], []
            for pg in range(2):
                dk_acc = jnp.zeros((3 * BLK, 128), F32)
                dv_acc = jnp.zeros((3 * BLK, 128), F32)
                for off in range(2):
                    g = 2 * pg + off
                    kx = _kv_both(k0_ref, kp_ref, kc_ref, g)
                    vx = _kv_both(v0_ref, vp_ref, vc_ref, g)
                    pair_cols = [slice(128 * (2 * g + pp), 128 * (2 * g + pp) + 128) for pp in range(2)]
                    q4 = _stack_heads(q_ref[:, pair_cols[0]], q_ref[:, pair_cols[1]])
                    d_out = []
                    for cols in pair_cols:
                        gate, dgate = _silu_and_grad(ag_ref[:, cols])
                        dy = dy_ref[:, cols]
                        dag_ref[:, cols] = (dy * att_ref[:, cols] * dgate).astype(BF16)
                        d_out.append(dy * gate)
                    do4 = _stack_heads(d_out[0], d_out[1])
                    heads = [GROUP * g + r for r in range(GROUP)]
                    sink = _per_head_column([sink_ref[l, h] for h in heads])
                    lse4 = _per_head_column(
                        [jnp.sum(jnp.where(lane == h, lse, 0.0), axis=1, keepdims=True) for h in heads])
                    p = jnp.where(mask, jnp.exp(_dot_nt(q4, kx) - lse4), 0.0)
                    dp = _dot_nt(do4, vx)
                    delta = jnp.sum(p * dp, axis=1, keepdims=True)
                    ds = (p * (dp - delta)).astype(BF16)
                    sink_term = jnp.exp(sink - lse4) * delta
                    for r, h in enumerate(heads):
                        dsink += jnp.where(lane == h, -jnp.sum(sink_term[BLK * r:BLK * r + BLK]), 0.0)
                    for cols, dq in zip(pair_cols, _unstack_heads(_dot(ds, kx))):
                        dq_ref[:, cols] = dq
                    dkg = _dot_tn(ds, q4)
                    dvg = _dot_tn(p.astype(BF16), do4)
                    own = (lane < 64) if off == 0 else (lane >= 64)
                    dk_acc += jnp.where(own, dkg + pltpu.roll(dkg, 64, axis=1), 0.0)
                    dv_acc += jnp.where(own, dvg + pltpu.roll(dvg, 64, axis=1), 0.0)
                dk_pg.append(dk_acc)
                dv_pg.append(dv_acc)
            dsink_ref[...] += dsink
            for pg in range(2):
                cols = slice(128 * pg, 128 * pg + 128)
                dk0_ref[:, cols] += dk_pg[pg][0:BLK]
                dv0_ref[:, cols] += dv_pg[pg][0:BLK]
                dk_ref[:, cols] = kcarry[:, cols] + dk_pg[pg][BLK:2 * BLK]
                dv_ref[:, cols] = vcarry[:, cols] + dv_pg[pg][BLK:2 * BLK]
                kcarry[:, cols] = dk_pg[pg][2 * BLK:3 * BLK]
                vcarry[:, cols] = dv_pg[pg][2 * BLK:3 * BLK]

    last = nb - 1
    cur = lambda n: (jnp.minimum(n, last), 0)
    prev = lambda n: (jnp.clip(n - 1, 0, last), 0)
    zero = lambda n: (0, 0)
    kv = lambda f: pl.BlockSpec((BLK, KVW), f)
    wide = lambda f: pl.BlockSpec((BLK, AW), f)
    return _side_call(
        body, job, name=f"attn_bwd{l}", grid=(nb + 1,),
        in_specs=[pl.BlockSpec(memory_space=pltpu.SMEM),
                  wide(cur), kv(zero), kv(prev), kv(cur), kv(zero), kv(prev), kv(cur),
                  pl.BlockSpec((BLK, AW), lambda n: (jnp.minimum(n, last), 3)),
                  wide(cur), pl.BlockSpec((BLK, 128), cur), wide(cur), pl.BlockSpec(memory_space=pl.ANY)],
        out_specs=[wide(cur), kv(prev), kv(prev), kv(zero), kv(zero),
                   pl.BlockSpec((BLK, AW), lambda n: (jnp.minimum(n, last), 3)),
                   pl.BlockSpec((1, 128), zero)],
        out_shape=[jax.ShapeDtypeStruct((T, AW), F32), jax.ShapeDtypeStruct((T, KVW), F32),
                   jax.ShapeDtypeStruct((T, KVW), F32), jax.ShapeDtypeStruct((BLK, KVW), F32),
                   jax.ShapeDtypeStruct((BLK, KVW), F32), jax.ShapeDtypeStruct(dproj.shape, BF16),
                   jax.ShapeDtypeStruct((1, 128), F32)],
        scratch_shapes=[pltpu.VMEM((BLK, KVW), F32), pltpu.VMEM((BLK, KVW), F32)],
        semantics=("arbitrary",), aliases={12: 5},
        args=[sinks, qr, kr, kr, kr, vb, vb, vb, proj, att, lse, d_ya, dproj])


def _rope_bwd(dqr, dk, dv, dk0, dv0, tabs, dproj, l):
    T = dqr.shape[0]

    def body(dq_ref, dk_ref, dv_ref, dk0_ref, dv0_ref, c_ref, s1_ref, s2_ref, _, o_ref):
        n = pl.program_id(0)
        c, s1, s2 = c_ref[...], s1_ref[...], s2_ref[...]
        for gcol in range(AW // 128):
            cols = slice(128 * gcol, 128 * gcol + 128)
            o_ref[:, cols] = (_rot_bwd(dq_ref[:, cols], c, s1, s2) * 0.125).astype(BF16)
        for gcol in range(KVW // 128):
            cols = slice(128 * gcol, 128 * gcol + 128)
            kcols = slice(AW + 128 * gcol, AW + 128 * gcol + 128)
            vcols = slice(AW + KVW + 128 * gcol, AW + KVW + 128 * gcol + 128)
            o_ref[:, kcols] = _rot_bwd(dk_ref[:, cols], c, s1, s2).astype(BF16)
            o_ref[:, vcols] = dv_ref[:, cols].astype(BF16)

            @pl.when(n == 0)
            def _():
                dkk = dk_ref[0:BLK, cols] + dk0_ref[:, cols]
                o_ref[0:BLK, kcols] = _rot_bwd(dkk, c[0:BLK], s1[0:BLK], s2[0:BLK]).astype(BF16)
                o_ref[0:BLK, vcols] = (dv_ref[0:BLK, cols] + dv0_ref[:, cols]).astype(BF16)

    tr = _pick(T, (384, 128))
    cur = lambda n: (n, 0)
    zero = lambda n: (0, 0)
    tab = pl.BlockSpec((tr, 128), cur)
    return pl.pallas_call(
        body, name=f"rope_bwd{l}", grid=(T // tr,),
        in_specs=[pl.BlockSpec((tr, AW), cur), pl.BlockSpec((tr, KVW), cur), pl.BlockSpec((tr, KVW), cur),
                  pl.BlockSpec((BLK, KVW), zero), pl.BlockSpec((BLK, KVW), zero), tab, tab, tab,
                  pl.BlockSpec(memory_space=pl.ANY)],
        out_specs=pl.BlockSpec((tr, AW + 2 * KVW), lambda n: (n, 1)),
        out_shape=jax.ShapeDtypeStruct(dproj.shape, BF16),
        input_output_aliases={8: 0},
        compiler_params=_cp("parallel"),
    )(dqr, dk, dv, dk0, dv0, *tabs, dproj)


def _block_diag(w):
    nl, nh, hd, _ = w.shape
    eye = jnp.eye(nh, dtype=w.dtype)
    return jnp.einsum("lhij,hg->lhigj", w, eye).reshape(nl, nh * hd, nh * hd)


def _diag_blocks(m):
    nh, hd = 8, 64
    return jnp.einsum("hihj->hij", m.reshape(nh, hd, nh, hd))


def _device_step(x, target, p, dist=None):
    vec = lambda a: a.reshape(DEPTH, 1, a.shape[-1])
    ln_in_g, ln_in_b = p["ln_in_g"].reshape(1, D), p["ln_in_b"].reshape(1, D)
    conv_dw_b, conv_ln_g, conv_ln_b, conv_pw_b = map(vec, (p["conv_dw_b"], p["conv_ln_g"], p["conv_ln_b"], p["conv_pw_b"]))
    lru_conv_b, lru_ba, lru_bx, lru_lambda = map(vec, (p["lru_conv_b"], p["lru_ba"], p["lru_bx"], p["lru_lambda"]))
    ln_post_g, ln_post_b = vec(p["ln_post_g"]), vec(p["ln_post_b"])
    wa_bd = _block_diag(p["lru_wa"]).astype(BF16)
    wx_bd = _block_diag(p["lru_wx"]).astype(BF16)
    w_in, w_out, pw_w = list(p["w_in"]), list(p["w_out"]), list(p["conv_pw_w"])
    sinks = p["attn_sinks"]
    big_names = ("w_in", "w_out", "conv_pw_w")

    (h, hb), got = _embed_fwd(x, p["meta_tokens"], ln_in_g, ln_in_b, job=_gather_job([w_in[0]]) if dist else None)
    if dist:
        w_in[0] = got[0]
    T = h.shape[0]
    tabs = _rope_tables(T)
    saved = []
    for l in range(DEPTH):
        (proj,), got = _proj_fwd(hb, w_in[l], l, job=_gather_job([w_out[0], pw_w[0]]) if dist and l == 0 else None)
        if got:
            w_out[0], pw_w[0] = got
        pw_l = pw_w[l].reshape(CW, CW)
        yc, conv = _conv_fwd(proj, p["conv_dw_w"], conv_dw_b, conv_ln_g, conv_ln_b, pw_l, conv_pw_b, l)
        qr, kr, vb = _rope_fwd(proj, tabs, l)
        (ya, att, lse), got = _attn_fwd(
            qr, kr, vb, proj, sinks, l, job=_gather_job([w_in[1]]) if dist and l == 0 else None)
        if got:
            w_in[1] = got[0]
        yl, hl = _lru_fwd(proj, p["lru_conv_w"], lru_conv_b, wa_bd, lru_ba, wx_bd, lru_bx, lru_lambda, l)
        (hn, hnb, xhat, rstd), got = _out_fwd(
            yc, ya, yl, w_out[l], h, ln_post_g, ln_post_b, l,
            job=_gather_job([w_out[1], pw_w[1]]) if dist and l == 0 else None)
        if got:
            w_out[1], pw_w[1] = got
        saved.append((hb, proj, yc, conv, qr, kr, vb, ya, att, lse, yl, hl, xhat, rstd, pw_l))
        h, hb = hn, hnb

    dh = None
    g = {}
    later = None
    early, last = ("w_out", "conv_pw_w"), ("w_in",)
    own = {}
    for l in reversed(range(DEPTH)):
        hb_l, proj, yc, conv, qr, kr, vb, ya, att, lse, yl, hl, xhat, rstd, pw_l = saved[l]
        tail = dist is not None and l == 0
        if l == DEPTH - 1:
            loss_part, dz, dzb, g["ln_post_g", l], g["ln_post_b", l] = _loss_post_ln_bwd(
                h, target, xhat, rstd, ln_post_g, l)
        else:
            dz, dzb, g["ln_post_g", l], g["ln_post_b", l] = _post_ln_bwd(dh, xhat, rstd, ln_post_g, l)
        (d_yc, d_ya, d_yl), recv = _dcat_bwd(dzb, w_out[l], l, job=_swap_job(later["grads"]) if later else None)
        if later:
            later["parts"], later["owns"] = _chip_partials(big_names, later["grads"], recv, dist, later["l"])
        g["w_out", l] = _dwout_bwd(yc, ya, yl, dzb, l)
        d_conv, dproj, dpw, g["conv_pw_b", l], g["conv_ln_g", l], g["conv_ln_b", l] = _conv_bwd_rows(
            conv, proj, d_yc, conv_ln_g, conv_ln_b, pw_l, conv_pw_b, l)
        g["conv_pw_w", l] = dpw.reshape(N_SHARD, 2, PW_SH // 2, CW)
        if tail:
            own["early"] = dict(l=0, grads=[g[name, 0] for name in early])
        job = None
        if tail:
            job = _join_jobs(_swap_job(own["early"]["grads"]), _scatter_job(later["parts"][1:]))
        (dproj, ddw, g["conv_dw_b", l]), got = _conv_bwd_taps(d_conv, proj, p["conv_dw_w"], dproj, l, job=job)
        if tail:
            n_early = len(early)
            own["early"]["parts"], own["early"]["owns"] = _chip_partials(
                early, own["early"]["grads"], got[:n_early], dist, 0)
            later["z"] = got[n_early:]
        g["conv_dw_w", l] = ddw[:CONV_K]
        (dqr, dk, dv, dk0, dv0, dproj, dsink), z = _attn_bwd(
            qr, kr, vb, proj, att, lse, d_ya, sinks, dproj, l,
            job=_scatter_job(later["parts"][:1]) if later else None)
        if later:
            later["z"] = z + later["z"]
        g["attn_sinks", l] = dsink[0, :N_HEADS]
        dproj = _rope_bwd(dqr, dk, dv, dk0, dv0, tabs, dproj, l)
        (dproj, dlw, g["lru_conv_b", l], dwa, g["lru_ba", l], dwx, g["lru_bx", l], g["lru_lambda", l]), z = _lru_bwd(
            proj, hl, d_yl, p["lru_conv_w"], lru_conv_b, wa_bd, lru_ba, wx_bd, lru_bx, lru_lambda, dproj, l,
            job=_scatter_job(own["early"]["parts"]) if tail else None)
        if tail:
            own["early"]["z"] = z
        g["lru_conv_w", l] = dlw[:LRU_K]
        g["lru_wa", l] = _diag_blocks(dwa)
        g["lru_wx", l] = _diag_blocks(dwx)
        job = None
        if l > 0:
            g["w_in", l] = _dwin_bwd(hb_l, dproj, l)
        else:
            c = dist[0] if dist else jnp.int32(0)
            (give,), _ = _dwin_half(hb_l, dproj, 1 - c, l, "give")
            (keep,), recv = _dwin_half(hb_l, dproj, c, l, "keep", job=_send_job([give]) if dist else None)
            if dist:
                own["last"] = dict(l=0)
                own["last"]["parts"], own["last"]["owns"] = _chip_partials(
                    last, [keep.reshape(N_SHARD, 1, D // 2, WIN_SH)], recv, (jnp.int32(0),) + tuple(dist[1:]), 0)
                pack_a = _pack_rows([_layer_stack(g, name) for name in _SMALL_LAYERED])
                job = _join_jobs(_scatter_job(own["last"]["parts"]), _spread_job(pack_a))
            else:
                g["w_in", l] = jnp.stack([keep, give], axis=1)
        (dh,), got = _dh_bwd(dproj, w_in[l], dz, l, job=job)
        if tail:
            own["last"]["z"] = got[:1]
            g["pack_layered", -1] = _sum_slots(pack_a, got[1], dist[3], "layered")
        if later:
            _finish_reduce(big_names, later, dist, g)
            later = None
        if dist and l > 0:
            later = dict(l=l, grads=[g[name, l] for name in big_names])
    grad_x, g["meta_tokens", -1], g["ln_in_g", -1], g["ln_in_b", -1] = _embed_bwd(
        dh, x, p["meta_tokens"], ln_in_g, ln_in_b)
    if dist:
        pack_b = _pack_rows([g[name, -1] for name in _SMALL_EMBED])
        slots_b = _run_job(_spread_job(pack_b), "spread_embed")[0]
        g["pack_embed", -1] = _sum_slots(pack_b, slots_b, dist[3], "embed")
        state = dict(l=0, owns=own["last"]["owns"] + own["early"]["owns"], z=own["last"]["z"] + own["early"]["z"])
        _finish_reduce(last + early, state, dist, g)
    return loss_part, grad_x, g


_SMALL_EMBED = ("meta_tokens", "ln_in_g", "ln_in_b")
_SMALL_LAYERED = ("conv_dw_w", "conv_dw_b", "conv_ln_g", "conv_ln_b", "conv_pw_b", "attn_sinks", "lru_conv_w",
                  "lru_conv_b", "lru_wa", "lru_ba", "lru_wx", "lru_bx", "lru_lambda", "ln_post_g", "ln_post_b")


def _layer_stack(g, name):
    return jnp.stack([g[name, l] for l in range(DEPTH)], axis=0)


def _chip_partials(names, grads, recv, dist, l):
    outs = [_chip_partial(a, r, dist[0], dist[1], f"{name}{l}") for name, a, r in zip(names, grads, recv)]
    return [o[0] for o in outs], [o[1] for o in outs]


def _finish_reduce(names, state, dist, g):
    l = state["l"]
    totals = [_shard_total(po, zz, dist[2], f"{name}{l}") for name, po, zz in zip(names, state["owns"], state["z"])]
    full = _run_job(_share_job(totals), f"share_halves{l}")
    for name, f in zip(names, full):
        g[name, l] = f.reshape(2 * f.shape[1], f.shape[2])


MESH = pl.DeviceIdType.MESH
HBM_SPEC = pl.BlockSpec(memory_space=pltpu.HBM)
N_DEV = 8


def _position():
    x, y, c = lax.axis_index("x"), lax.axis_index("y"), lax.axis_index("c")
    return x, y, c


def _other_chips(x, y):
    return [(1 - x, y), (x, 1 - y), (1 - x, 1 - y)]


def _cast_into_slot(a, l, j, tag):
    _, R, C = a.shape
    tb = _pick(R, (512, 128))

    def body(s_ref, a_ref, o_ref):
        o_ref[...] = a_ref[...].astype(BF16)

    grid_spec = pltpu.PrefetchScalarGridSpec(
        num_scalar_prefetch=1, grid=(R // tb,),
        in_specs=[pl.BlockSpec((None, tb, C), lambda t, sc: (l, t, 0))],
        out_specs=pl.BlockSpec((None, tb, C), lambda t, sc: (sc[0], t, 0)))
    return pl.pallas_call(
        body, name=f"cast_into_slot_{tag}{l}", grid_spec=grid_spec,
        out_shape=jax.ShapeDtypeStruct((N_SHARD, R, C), BF16),
        compiler_params=_cp("arbitrary"),
    )(jnp.reshape(j, (1,)).astype(jnp.int32), a)


class _Job:
    def __init__(self, inputs, aliased, extra_out, sems, start, mid, finish):
        self.inputs, self.aliased, self.extra_out, self.sems = list(inputs), aliased, list(extra_out), list(sems)
        self.start, self.mid, self.finish = start, mid, finish

    def out_shapes(self):
        own = [jax.ShapeDtypeStruct(a.shape, a.dtype) for a in self.inputs] if self.aliased else []
        return own + self.extra_out


def _side_call(body, job, *, name, grid, in_specs, out_specs, out_shape, scratch_shapes, semantics, args,
               aliases=None, prefetch=()):
    aliases = dict(aliases or {})
    n_pre = len(prefetch)
    assert not (n_pre and (aliases or (job is not None and job.aliased)))

    def call(fn, ins, outs, shapes, scratch, sem, operands):
        if n_pre:
            spec = pltpu.PrefetchScalarGridSpec(num_scalar_prefetch=n_pre, grid=grid, in_specs=ins, out_specs=outs,
                                                scratch_shapes=scratch)
            return pl.pallas_call(fn, name=name, grid_spec=spec, out_shape=shapes,
                                  compiler_params=_cp(*sem))(*prefetch, *operands)
        return pl.pallas_call(fn, name=name, grid=grid, in_specs=ins, out_specs=outs, out_shape=shapes,
                              scratch_shapes=scratch, input_output_aliases=aliases,
                              compiler_params=_cp(*sem))(*operands)

    if job is None:
        return list(call(body, list(in_specs), list(out_specs), list(out_shape), list(scratch_shapes),
                         semantics, args)), []
    n_in, n_out, n_scr = len(in_specs), len(out_specs), len(scratch_shapes)
    j_in, j_out = len(job.inputs), len(job.out_shapes())
    steps = 1
    for gsize in grid:
        steps *= gsize

    def wrapped(*refs):
        pre, refs = refs[:n_pre], refs[n_pre:]
        host_in, job_in = refs[:n_in], refs[n_in:n_in + j_in]
        o0 = n_in + j_in
        host_out, job_out = refs[o0:o0 + n_out], refs[o0 + n_out:o0 + n_out + j_out]
        s0 = o0 + n_out + j_out
        host_scr, sems = refs[s0:s0 + n_scr], refs[s0 + n_scr:]
        step = pl.program_id(0)
        for d in range(1, len(grid)):
            step = step * grid[d] + pl.program_id(d)

        @pl.when(step == 0)
        def _():
            job.start(job_in, job_out, sems)

        @pl.when(step == max(steps - 2, 0))
        def _():
            job.mid(job_in, job_out, sems)

        body(*pre, *host_in, *host_out, *host_scr)

        @pl.when(step == steps - 1)
        def _():
            job.finish(job_in, job_out, sems)

    if job.aliased:
        aliases.update({n_in + k: n_out + k for k in range(j_in)})
    outs = call(wrapped, list(in_specs) + [HBM_SPEC] * j_in, list(out_specs) + [HBM_SPEC] * j_out,
                list(out_shape) + job.out_shapes(), list(scratch_shapes) + job.sems,
                ["arbitrary"] * len(grid), [*args, *job.inputs])
    return list(outs[:n_out]), list(outs[n_out:])


def _run_job(job, name):
    return _side_call(lambda: None, job, name=name, grid=(1,), in_specs=[], out_specs=[], out_shape=[],
                      scratch_shapes=[], semantics=("arbitrary",), args=[])[1]


def _gather_job(slots):
    n = len(slots)

    def copies(buf, sems):
        ici_send, ici_recv, d2d_send, d2d_recv = sems
        x, y, c = _position()
        chips = _other_chips(x, y)

        def half(k, slot, which):
            hr = buf[k].shape[1] // 2
            return buf[k].at[slot, pl.ds(pl.multiple_of(which * hr, hr), hr)]

        def over_ici(k, p, slot):
            px, py = chips[p]
            return pltpu.make_async_remote_copy(
                src_ref=half(k, slot, c), dst_ref=half(k, slot, c),
                send_sem=ici_send.at[k * 3 + p], recv_sem=ici_recv.at[k * 3 + p],
                device_id=(px, py, c), device_id_type=MESH)

        def over_d2d(k, p, which):
            px, py = chips[p]
            return pltpu.make_async_remote_copy(
                src_ref=half(k, 2 * px + py, which), dst_ref=half(k, 2 * px + py, which),
                send_sem=d2d_send.at[k * 3 + p], recv_sem=d2d_recv.at[k * 3 + p],
                device_id=(x, y, 1 - c), device_id_type=MESH)

        return over_ici, over_d2d, 2 * x + y, chips, c

    pairs = [(k, p) for k in range(n) for p in range(3)]

    def start(_, buf, sems):
        over_ici, _, mine, _, _ = copies(buf, sems)
        for k, p in pairs:
            over_ici(k, p, mine).start()

    def mid(_, buf, sems):
        over_ici, over_d2d, _, chips, c = copies(buf, sems)
        for k, p in pairs:
            px, py = chips[p]
            over_ici(k, p, 2 * px + py).wait_recv()
            over_d2d(k, p, c).start()

    def finish(_, buf, sems):
        over_ici, over_d2d, mine, _, c = copies(buf, sems)
        for k, p in pairs:
            over_d2d(k, p, 1 - c).wait_recv()
        for k, p in pairs:
            over_ici(k, p, mine).wait_send()
            over_d2d(k, p, c).wait_send()

    return _Job(slots, True, [], [pltpu.SemaphoreType.DMA((3 * n,))] * 4, start, mid, finish)


def _gather_shards(shards):
    n = len(shards)

    def body(*refs):
        src, dst = refs[:n], refs[n:2 * n]
        send_sems, recv_sems, local_sems = refs[2 * n:]
        x, y, c = _position()
        mine = 2 * x + y
        chips = _other_chips(x, y)

        def copy(k, p):
            return pltpu.make_async_remote_copy(
                src_ref=src[k], dst_ref=dst[k].at[mine],
                send_sem=send_sems.at[k * 3 + p], recv_sem=recv_sems.at[k * 3 + p],
                device_id=(*chips[p], c), device_id_type=MESH)

        def arrival(k, p):
            px, py = chips[p]
            return pltpu.make_async_remote_copy(
                src_ref=src[k], dst_ref=dst[k].at[2 * px + py],
                send_sem=send_sems.at[k * 3 + p], recv_sem=recv_sems.at[k * 3 + p],
                device_id=(px, py, c), device_id_type=MESH)

        local = [pltpu.make_async_copy(src[k], dst[k].at[mine], local_sems.at[k]) for k in range(n)]
        for cp in local:
            cp.start()
        for k in range(n):
            for p in range(3):
                copy(k, p).start()
        for k in range(n):
            for p in range(3):
                arrival(k, p).wait_recv()
        for k in range(n):
            for p in range(3):
                copy(k, p).wait_send()
        for cp in local:
            cp.wait()

    return pl.pallas_call(
        body, name="gather_shards",
        in_specs=[HBM_SPEC] * n, out_specs=[HBM_SPEC] * n,
        out_shape=[jax.ShapeDtypeStruct((N_SHARD,) + s.shape, s.dtype) for s in shards],
        scratch_shapes=[pltpu.SemaphoreType.DMA((3 * n,)), pltpu.SemaphoreType.DMA((3 * n,)),
                        pltpu.SemaphoreType.DMA((n,))],
    )(*shards)


def _swap_job(grads):
    n = len(grads)

    def copies(src, dst, sems):
        x, y, c = _position()
        return [pltpu.make_async_remote_copy(
            src_ref=src[k].at[:, 1 - c], dst_ref=dst[k],
            send_sem=sems[0].at[k], recv_sem=sems[1].at[k],
            device_id=(x, y, 1 - c), device_id_type=MESH) for k in range(n)]

    def start(src, dst, sems):
        for cp in copies(src, dst, sems):
            cp.start()

    def finish(src, dst, sems):
        for cp in copies(src, dst, sems):
            cp.wait()

    return _Job(grads, False, [jax.ShapeDtypeStruct((N_SHARD,) + g.shape[2:], F32) for g in grads],
                [pltpu.SemaphoreType.DMA((n,))] * 2, start, lambda *_: None, finish)


def _send_job(arrays):
    n = len(arrays)

    def copies(src, dst, sems):
        x, y, c = _position()
        return [pltpu.make_async_remote_copy(
            src_ref=src[k], dst_ref=dst[k], send_sem=sems[0].at[k], recv_sem=sems[1].at[k],
            device_id=(x, y, 1 - c), device_id_type=MESH) for k in range(n)]

    def start(src, dst, sems):
        for cp in copies(src, dst, sems):
            cp.start()

    def finish(src, dst, sems):
        for cp in copies(src, dst, sems):
            cp.wait()

    return _Job(arrays, False, [jax.ShapeDtypeStruct(a.shape, a.dtype) for a in arrays],
                [pltpu.SemaphoreType.DMA((n,))] * 2, start, lambda *_: None, finish)


def _chip_partial(a, y, c, j, tag):
    _, _, R, C = a.shape
    tr = _pick(R, (256, 64))

    def body(s_ref, a_ref, y_ref, pb_ref, po_ref):
        total = a_ref[...] + y_ref[...]
        pb_ref[...] = total.astype(BF16)

        @pl.when(pl.program_id(1) == s_ref[1])
        def _():
            po_ref[...] = total

    grid_spec = pltpu.PrefetchScalarGridSpec(
        num_scalar_prefetch=1, grid=(R // tr, N_SHARD),
        in_specs=[pl.BlockSpec((None, None, tr, C), lambda t, s, sc: (s, sc[0], t, 0)),
                  pl.BlockSpec((None, tr, C), lambda t, s, sc: (s, t, 0))],
        out_specs=[pl.BlockSpec((None, tr, C), lambda t, s, sc: (s, t, 0)),
                   pl.BlockSpec((tr, C), lambda t, s, sc: (t, 0))])
    return pl.pallas_call(
        body, name=f"chip_partial_{tag}", grid_spec=grid_spec,
        out_shape=[jax.ShapeDtypeStruct((N_SHARD, R, C), BF16), jax.ShapeDtypeStruct((R, C), F32)],
        compiler_params=_cp("arbitrary", "arbitrary"),
    )(jnp.stack([c, j]).astype(jnp.int32), a, y)


def _scatter_job(parts):
    n = len(parts)
    pairs = [(k, p) for k in range(n) for p in range(3)]

    def copy(src, dst, sems, k, p, outgoing):
        x, y, c = _position()
        mine = 2 * x + y
        px, py = _other_chips(x, y)[p]
        theirs = 2 * px + py
        return pltpu.make_async_remote_copy(
            src_ref=src[k].at[theirs if outgoing else mine], dst_ref=dst[k].at[mine if outgoing else theirs],
            send_sem=sems[0].at[k * 3 + p], recv_sem=sems[1].at[k * 3 + p],
            device_id=(px, py, c), device_id_type=MESH)

    def start(src, dst, sems):
        for k, p in pairs:
            copy(src, dst, sems, k, p, True).start()

    def finish(src, dst, sems):
        for k, p in pairs:
            copy(src, dst, sems, k, p, False).wait_recv()
        for k, p in pairs:
            copy(src, dst, sems, k, p, True).wait_send()

    return _Job(parts, False, [jax.ShapeDtypeStruct(pb.shape, BF16) for pb in parts],
                [pltpu.SemaphoreType.DMA((3 * n,))] * 2, start, lambda *_: None, finish)


def _shard_total(own, z, others_c, tag):
    R, C = own.shape
    tr = _pick(R, (256, 64))

    def body(s_ref, o_ref, z0_ref, z1_ref, z2_ref, h_ref):
        h_ref[...] = ((o_ref[...] + z0_ref[...].astype(F32)) + z1_ref[...].astype(F32)) + z2_ref[...].astype(F32)

    zspec = lambda q: pl.BlockSpec((None, tr, C), lambda t, sc: (sc[q], t, 0))
    grid_spec = pltpu.PrefetchScalarGridSpec(
        num_scalar_prefetch=1, grid=(R // tr,),
        in_specs=[pl.BlockSpec((tr, C), lambda t, sc: (t, 0)), zspec(0), zspec(1), zspec(2)],
        out_specs=pl.BlockSpec((None, tr, C), lambda t, sc: (sc[3], t, 0)))
    return pl.pallas_call(
        body, name=f"shard_total_{tag}", grid_spec=grid_spec,
        out_shape=jax.ShapeDtypeStruct((2, R, C), F32),
        compiler_params=_cp("arbitrary"),
    )(others_c, own, z, z, z)


def _share_job(totals):
    n = len(totals)

    def copy(buf, sems, k, which):
        x, y, c = _position()
        return pltpu.make_async_remote_copy(
            src_ref=buf[k].at[which], dst_ref=buf[k].at[which],
            send_sem=sems[0].at[k], recv_sem=sems[1].at[k],
            device_id=(x, y, 1 - c), device_id_type=MESH)

    def start(_, buf, sems):
        c = lax.axis_index("c")
        for k in range(n):
            copy(buf, sems, k, c).start()

    def finish(_, buf, sems):
        c = lax.axis_index("c")
        for k in range(n):
            copy(buf, sems, k, 1 - c).wait_recv()
        for k in range(n):
            copy(buf, sems, k, c).wait_send()

    return _Job(totals, True, [], [pltpu.SemaphoreType.DMA((n,))] * 2, start, lambda *_: None, finish)


def _spread_job(pack):
    def copy(src, dst, sems, m, outgoing):
        x, y, c = _position()
        peer = (x ^ (m >> 2), y ^ ((m >> 1) & 1), c ^ (m & 1))
        slot = 4 * x + 2 * y + c if outgoing else 4 * peer[0] + 2 * peer[1] + peer[2]
        return pltpu.make_async_remote_copy(
            src_ref=src[0], dst_ref=dst[0].at[slot], send_sem=sems[0].at[m - 1], recv_sem=sems[1].at[m - 1],
            device_id=peer, device_id_type=MESH)

    def start(src, dst, sems):
        for m in range(1, N_DEV):
            copy(src, dst, sems, m, True).start()

    def finish(src, dst, sems):
        for m in range(1, N_DEV):
            copy(src, dst, sems, m, False).wait_recv()
        for m in range(1, N_DEV):
            copy(src, dst, sems, m, True).wait_send()

    return _Job([pack], False, [jax.ShapeDtypeStruct((N_DEV,) + pack.shape, F32)],
                [pltpu.SemaphoreType.DMA((N_DEV - 1,))] * 2, start, lambda *_: None, finish)


def _join_jobs(a, b):
    assert not a.aliased and not b.aliased
    n_in, n_out, n_sem = len(a.inputs), len(a.extra_out), len(a.sems)

    def phase(name):
        def run(ins, outs, sems):
            getattr(a, name)(ins[:n_in], outs[:n_out], sems[:n_sem])
            getattr(b, name)(ins[n_in:], outs[n_out:], sems[n_sem:])
        return run

    return _Job(a.inputs + b.inputs, False, a.extra_out + b.extra_out, a.sems + b.sems,
                phase("start"), phase("mid"), phase("finish"))


def _sum_slots(pack, slots, me, tag):
    def body(me_ref, p_ref, s_ref, o_ref):
        acc = None
        for d in range(N_DEV):
            term = jnp.where(me_ref[0] == d, p_ref[...], s_ref[d])
            acc = term if acc is None else acc + term
        o_ref[...] = acc

    vm = pl.BlockSpec(memory_space=pltpu.VMEM)
    return pl.pallas_call(
        body, name=f"sum_slots_{tag}",
        in_specs=[pl.BlockSpec(memory_space=pltpu.SMEM), vm, vm], out_specs=vm,
        out_shape=jax.ShapeDtypeStruct(pack.shape, F32),
        compiler_params=pltpu.CompilerParams(vmem_limit_bytes=V7X_VMEM_LIMIT),
    )(jnp.reshape(me, (1,)).astype(jnp.int32), pack, slots)


def _pack_rows(arrays):
    total = sum(a.size for a in arrays)
    rows = -(-total // 128)
    rows = -(-rows // PACK_ROWS_ALIGN) * PACK_ROWS_ALIGN
    flat = [a.reshape(-1) for a in arrays] + [jnp.zeros((rows * 128 - total,), F32)]
    return jnp.concatenate(flat).reshape(rows, 128)


def _adamw_math(w, g, m, v):
    m = ADAM_B1 * m + (1.0 - ADAM_B1) * g
    v = ADAM_B2 * v + (1.0 - ADAM_B2) * (g * g)
    m_hat = m / (1.0 - ADAM_B1 ** ADAM_STEP)
    v_hat = v / (1.0 - ADAM_B2 ** ADAM_STEP)
    delta = -ADAM_LR * (m_hat / (jnp.sqrt(v_hat) + ADAM_EPS) + ADAM_WD * w)
    return delta, m, v


def _adamw_big(w, g0, g1, m, v, tag):
    _, R, C = w.shape
    tr = _pick(R, (256, 128))

    def body(w_ref, g0_ref, g1_ref, m_ref, v_ref, go_ref, d_ref, mo_ref, vo_ref):
        g = jnp.where(pl.program_id(0) == 0, g0_ref[...], g1_ref[...])
        delta, mn, vn = _adamw_math(w_ref[...], g, m_ref[...], v_ref[...])
        go_ref[...] = g
        d_ref[...] = delta
        mo_ref[...] = mn
        vo_ref[...] = vn

    s3 = pl.BlockSpec((None, tr, C), lambda l, t: (l, t, 0))
    s2 = pl.BlockSpec((tr, C), lambda l, t: (t, 0))
    shp = jax.ShapeDtypeStruct(w.shape, F32)
    return pl.pallas_call(
        body, name=f"adamw_{tag}", grid=(2, R // tr),
        in_specs=[s3, s2, s2, s3, s3], out_specs=[s3, s3, s3, s3],
        out_shape=[shp, shp, shp, shp],
        compiler_params=_cp("parallel", "parallel"),
    )(w, g0, g1, m, v)


def _adamw_small(ws, gs, ms, vs):
    n = len(ws)

    def body(*refs):
        w_r, g_r, m_r, v_r = refs[:n], refs[n:2 * n], refs[2 * n:3 * n], refs[3 * n:4 * n]
        d_o, m_o, v_o = refs[4 * n:5 * n], refs[5 * n:6 * n], refs[6 * n:7 * n]
        for k in range(n):
            delta, mn, vn = _adamw_math(w_r[k][...], g_r[k][...], m_r[k][...], v_r[k][...])
            d_o[k][...] = delta
            m_o[k][...] = mn
            v_o[k][...] = vn

    vm = pl.BlockSpec(memory_space=pltpu.VMEM)
    shapes = [jax.ShapeDtypeStruct(w.shape, F32) for w in ws]
    outs = pl.pallas_call(
        body, name="adamw_small",
        in_specs=[vm] * (4 * n), out_specs=[vm] * (3 * n),
        out_shape=shapes * 3,
    )(*ws, *gs, *ms, *vs)
    return outs[:n], outs[n:2 * n], outs[2 * n:]


_WEIGHTS = ["meta_tokens", "ln_in_g", "ln_in_b", "w_in", "conv_dw_w", "conv_dw_b", "conv_ln_g", "conv_ln_b",
            "conv_pw_w", "conv_pw_b", "attn_sinks", "lru_conv_w", "lru_conv_b", "lru_wa", "lru_ba", "lru_wx",
            "lru_bx", "lru_lambda", "w_out", "ln_post_g", "ln_post_b"]
_BIG = ("w_in", "w_out", "conv_pw_w")
_SMALL_SHARDED = {"meta_tokens": 1, "conv_dw_w": 2, "lru_conv_w": 2}
PACK_ROWS_ALIGN = 8


def _as2d(a):
    return a.reshape(1, -1) if a.ndim == 1 else a.reshape(-1, a.shape[-1])


def kernel(x, meta_tokens, ln_in_g, ln_in_b, w_in, conv_dw_w, conv_dw_b, conv_ln_g, conv_ln_b, conv_pw_w, conv_pw_b, attn_sinks, lru_conv_w, lru_conv_b, lru_wa, lru_ba, lru_wx, lru_bx, lru_lambda, w_out, ln_post_g, ln_post_b, loss_target, m_meta_tokens, m_ln_in_g, m_ln_in_b, m_w_in, m_conv_dw_w, m_conv_dw_b, m_conv_ln_g, m_conv_ln_b, m_conv_pw_w, m_conv_pw_b, m_attn_sinks, m_lru_conv_w, m_lru_conv_b, m_lru_wa, m_lru_ba, m_lru_wx, m_lru_bx, m_lru_lambda, m_w_out, m_ln_post_g, m_ln_post_b, v_meta_tokens, v_ln_in_g, v_ln_in_b, v_w_in, v_conv_dw_w, v_conv_dw_b, v_conv_ln_g, v_conv_ln_b, v_conv_pw_w, v_conv_pw_b, v_attn_sinks, v_lru_conv_w, v_lru_conv_b, v_lru_wa, v_lru_ba, v_lru_wx, v_lru_bx, v_lru_lambda, v_w_out, v_ln_post_g, v_ln_post_b):
    w = dict(meta_tokens=meta_tokens, ln_in_g=ln_in_g, ln_in_b=ln_in_b, w_in=w_in, conv_dw_w=conv_dw_w,
             conv_dw_b=conv_dw_b, conv_ln_g=conv_ln_g, conv_ln_b=conv_ln_b, conv_pw_w=conv_pw_w,
             conv_pw_b=conv_pw_b, attn_sinks=attn_sinks, lru_conv_w=lru_conv_w, lru_conv_b=lru_conv_b,
             lru_wa=lru_wa, lru_ba=lru_ba, lru_wx=lru_wx, lru_bx=lru_bx, lru_lambda=lru_lambda, w_out=w_out,
             ln_post_g=ln_post_g, ln_post_b=ln_post_b)
    mom_m = dict(zip(_WEIGHTS, (m_meta_tokens, m_ln_in_g, m_ln_in_b, m_w_in, m_conv_dw_w, m_conv_dw_b, m_conv_ln_g,
                                m_conv_ln_b, m_conv_pw_w, m_conv_pw_b, m_attn_sinks, m_lru_conv_w, m_lru_conv_b,
                                m_lru_wa, m_lru_ba, m_lru_wx, m_lru_bx, m_lru_lambda, m_w_out, m_ln_post_g,
                                m_ln_post_b)))
    mom_v = dict(zip(_WEIGHTS, (v_meta_tokens, v_ln_in_g, v_ln_in_b, v_w_in, v_conv_dw_w, v_conv_dw_b, v_conv_ln_g,
                                v_conv_ln_b, v_conv_pw_w, v_conv_pw_b, v_attn_sinks, v_lru_conv_w, v_lru_conv_b,
                                v_lru_wa, v_lru_ba, v_lru_wx, v_lru_bx, v_lru_lambda, v_w_out, v_ln_post_g,
                                v_ln_post_b)))
    xi, yi, ci = _position()
    j = 2 * xi + yi

    g_meta, g_dw, g_lc = _gather_shards([meta_tokens, conv_dw_w, lru_conv_w])
    p = dict(w)
    p["w_in"] = [_cast_into_slot(w_in, l, j, "w_in") for l in range(DEPTH)]
    p["w_out"] = [_cast_into_slot(w_out, l, j, "w_out") for l in range(DEPTH)]
    p["conv_pw_w"] = [_cast_into_slot(conv_pw_w, l, j, "conv_pw_w") for l in range(DEPTH)]
    p["meta_tokens"] = g_meta.transpose(1, 0, 2).reshape(N_META, D)
    p["conv_dw_w"] = g_dw.transpose(1, 2, 0, 3).reshape(DEPTH, CONV_K, CW)
    p["lru_conv_w"] = g_lc.transpose(1, 2, 0, 3).reshape(DEPTH, LRU_K, LW)

    others = jnp.stack([jnp.where(j <= 0, 1, 0), jnp.where(j <= 1, 2, 1), jnp.where(j <= 2, 3, 2), ci]).astype(jnp.int32)
    me = 4 * xi + 2 * yi + ci
    loss_part, grad_x, g = _device_step(x[0], loss_target[0], p, dist=(ci, j, others, me))
    loss = lax.psum(jnp.sum(loss_part), ("x", "y", "c"))
    big = {(name, l): g[name, l] for name in _BIG for l in range(DEPTH)}

    small_names = [n for n in _WEIGHTS if n not in _BIG]
    small_g = {}
    for names, red in ((_SMALL_LAYERED, g["pack_layered", -1]), (_SMALL_EMBED, g["pack_embed", -1])):
        red = red.reshape(-1)
        off = 0
        for n in names:
            fshape = list(w[n].shape)
            if n in _SMALL_SHARDED:
                fshape[_SMALL_SHARDED[n]] *= N_SHARD
            sz = 1
            for dim in fshape:
                sz *= dim
            full = red[off:off + sz].reshape(fshape)
            off += sz
            if n in _SMALL_SHARDED:
                ax = _SMALL_SHARDED[n]
                full = lax.dynamic_slice_in_dim(full, j * w[n].shape[ax], w[n].shape[ax], axis=ax)
            small_g[n] = full

    out_g, out_d, out_m, out_v = {}, {}, {}, {}
    for name in _BIG:
        shp = w[name].shape
        to3 = lambda a: a.reshape(DEPTH, -1, shp[-1])
        go, do, mo, vo = _adamw_big(to3(w[name]), big[name, 0], big[name, 1], to3(mom_m[name]), to3(mom_v[name]), name)
        out_g[name], out_d[name], out_m[name], out_v[name] = (a.reshape(shp) for a in (go, do, mo, vo))
    ds, ms, vs = _adamw_small([_as2d(w[n]) for n in small_names], [_as2d(small_g[n]) for n in small_names],
                              [_as2d(mom_m[n]) for n in small_names], [_as2d(mom_v[n]) for n in small_names])
    for n, d_, m_, v_ in zip(small_names, ds, ms, vs):
        out_g[n] = small_g[n]
        out_d[n], out_m[n], out_v[n] = d_.reshape(w[n].shape), m_.reshape(w[n].shape), v_.reshape(w[n].shape)

    return (loss, grad_x[None], *[out_g[n] for n in _WEIGHTS], *[out_d[n] for n in _WEIGHTS],
            *[out_m[n] for n in _WEIGHTS], *[out_v[n] for n in _WEIGHTS])
```

```python
import functools

import jax
import jax.numpy as jnp
from jax import lax
from jax.experimental import pallas as pl
from jax.experimental.pallas import tpu as pltpu

F32 = jnp.float32
BF16 = jnp.bfloat16

D = 2048
N_META = 16
CW = 512
CONV_K = 31
AW = 1024
KVW = 256
N_HEADS = 16
LW = 512
LRU_K = 4
LRU_C = 8.0
IN_TOTAL = 5120
ROT_HALF = 8
ROPE_THETA = 500000.0
LN_EPS = 1e-5
DEPTH = 2
ALPHA = (2.0 * DEPTH) ** 0.25
NEG_INF = -1e30
ADAM_LR, ADAM_B1, ADAM_B2, ADAM_EPS, ADAM_WD, ADAM_STEP = 0.001, 0.9, 0.999, 1e-08, 0.01, 10

BLK = 128
PAD = BLK - N_META
N_SHARD = 4
WIN_SH = IN_TOTAL // N_SHARD
WOUT_SH = D // N_SHARD
PW_SH = CW // N_SHARD
HALO = 32
LHALO = 8
V7X_VMEM_LIMIT = 60 * 1024 * 1024


def _cp(*sem):
    return pltpu.CompilerParams(dimension_semantics=sem if sem else None, vmem_limit_bytes=V7X_VMEM_LIMIT)


def _pick(total, prefs):
    for p in prefs:
        if total % p == 0:
            return p
    raise ValueError(f"no tile for {total}")


def _dot(a, b):
    return jnp.dot(a, b, preferred_element_type=F32)


def _dot_nt(a, b):
    return lax.dot_general(a, b, (((1,), (1,)), ((), ())), preferred_element_type=F32)


def _dot_tn(a, b):
    return lax.dot_general(a, b, (((0,), (0,)), ((), ())), preferred_element_type=F32)


def _sigmoid(x):
    return 1.0 / (1.0 + jnp.exp(-x))


def _silu_and_grad(x):
    s = _sigmoid(x)
    return x * s, s * (1.0 + x * (1.0 - s))


def _ln_rows(x, g, b):
    mu = jnp.mean(x, axis=-1, keepdims=True)
    xc = x - mu
    var = jnp.mean(xc * xc, axis=-1, keepdims=True)
    rstd = lax.rsqrt(var + LN_EPS)
    xhat = xc * rstd
    return xhat * g + b, xhat, rstd


def _ln_bwd_rows(dy, xhat, rstd, g):
    dxh = dy * g
    m1 = jnp.mean(dxh, axis=-1, keepdims=True)
    m2 = jnp.mean(dxh * xhat, axis=-1, keepdims=True)
    return rstd * (dxh - m1 - xhat * m2)


def _row_ids(n, base):
    return base + lax.broadcasted_iota(jnp.int32, (n, 1), 0)


def _colsum(x):
    return jnp.sum(x, axis=0, keepdims=True)


def _embed_fwd(x, meta, g, b, job=None):
    S = x.shape[0]
    nb = S // BLK + 1

    def body(x_ref, meta_ref, g_ref, b_ref, h_ref, hb_ref):
        n = pl.program_id(0)

        @pl.when(n == 0)
        def _():
            y, _, _ = _ln_rows(meta_ref[...], g_ref[...], b_ref[...])
            h_ref[...] = jnp.zeros_like(h_ref)
            h_ref[PAD:BLK, :] = y

        @pl.when(n > 0)
        def _():
            y, _, _ = _ln_rows(x_ref[...], g_ref[...], b_ref[...])
            h_ref[...] = y

        hb_ref[...] = h_ref[...].astype(BF16)

    return _side_call(
        body, job, name="embed_fwd", grid=(nb,),
        in_specs=[pl.BlockSpec((BLK, D), lambda n: (jnp.maximum(n - 1, 0), 0)),
                  pl.BlockSpec((N_META, D), lambda n: (0, 0)),
                  pl.BlockSpec((1, D), lambda n: (0, 0)),
                  pl.BlockSpec((1, D), lambda n: (0, 0))],
        out_specs=[pl.BlockSpec((BLK, D), lambda n: (n, 0)),
                   pl.BlockSpec((BLK, D), lambda n: (n, 0))],
        out_shape=[jax.ShapeDtypeStruct((nb * BLK, D), F32), jax.ShapeDtypeStruct((nb * BLK, D), BF16)],
        scratch_shapes=[], semantics=("arbitrary",), args=[x, meta, g, b])


def _embed_bwd(dh, x, meta, g, b):
    S = x.shape[0]
    nb = S // BLK + 1

    def body(dh_ref, x_ref, meta_ref, g_ref, b_ref, gx_ref, gm_ref, dg_ref, db_ref):
        n = pl.program_id(0)

        @pl.when(n == 0)
        def _():
            _, xhat, rstd = _ln_rows(meta_ref[...], g_ref[...], b_ref[...])
            dy = dh_ref[PAD:BLK, :]
            gm_ref[...] = _ln_bwd_rows(dy, xhat, rstd, g_ref[...])
            dg_ref[...] = _colsum(dy * xhat)
            db_ref[...] = _colsum(dy)

        @pl.when(n > 0)
        def _():
            _, xhat, rstd = _ln_rows(x_ref[...], g_ref[...], b_ref[...])
            dy = dh_ref[...]
            gx_ref[...] = _ln_bwd_rows(dy, xhat, rstd, g_ref[...])
            dg_ref[...] += _colsum(dy * xhat)
            db_ref[...] += _colsum(dy)

    prev = lambda n: (jnp.maximum(n - 1, 0), 0)
    const = lambda n: (0, 0)
    return pl.pallas_call(
        body, name="embed_bwd", grid=(nb,),
        in_specs=[pl.BlockSpec((BLK, D), lambda n: (n, 0)),
                  pl.BlockSpec((BLK, D), prev),
                  pl.BlockSpec((N_META, D), const),
                  pl.BlockSpec((1, D), const),
                  pl.BlockSpec((1, D), const)],
        out_specs=[pl.BlockSpec((BLK, D), prev),
                   pl.BlockSpec((N_META, D), const),
                   pl.BlockSpec((1, D), const),
                   pl.BlockSpec((1, D), const)],
        out_shape=[jax.ShapeDtypeStruct((S, D), F32), jax.ShapeDtypeStruct((N_META, D), F32),
                   jax.ShapeDtypeStruct((1, D), F32), jax.ShapeDtypeStruct((1, D), F32)],
        compiler_params=_cp("arbitrary"),
    )(dh, x, meta, g, b)


def _proj_fwd(hb, w_in, l, job=None):
    T = hb.shape[0]
    tm = _pick(T, (1056, 384, 128))

    def body(a_ref, w_ref, o_ref):
        o_ref[...] = _dot(a_ref[...], w_ref[...])

    return _side_call(
        body, job, name=f"proj_fwd{l}", grid=(T // tm, N_SHARD),
        in_specs=[pl.BlockSpec((tm, D), lambda i, j: (i, 0)),
                  pl.BlockSpec((None, D, WIN_SH), lambda i, j: (j, 0, 0))],
        out_specs=[pl.BlockSpec((tm, WIN_SH), lambda i, j: (i, j))],
        out_shape=[jax.ShapeDtypeStruct((T, IN_TOTAL), F32)],
        scratch_shapes=[], semantics=("parallel", "arbitrary"), args=[hb, w_in])


def _out_fwd(yc, ya, yl, w_out, h, g, b, l, job=None):
    T = h.shape[0]
    tm = _pick(T, (384, 128))

    def body(yc_ref, ya_ref, yl_ref, w_ref, h_ref, g_ref, b_ref, hn_ref, hnb_ref, xh_ref, rs_ref):
        acc = _dot(yc_ref[...], w_ref[0])
        acc += _dot(ya_ref[:, 0:WOUT_SH], w_ref[1])
        acc += _dot(ya_ref[:, WOUT_SH:2 * WOUT_SH], w_ref[2])
        acc += _dot(yl_ref[...], w_ref[3])
        z = ALPHA * h_ref[...] + acc
        y, xhat, rstd = _ln_rows(z, g_ref[...], b_ref[...])
        hn_ref[...] = y
        hnb_ref[...] = y.astype(BF16)
        xh_ref[...] = xhat
        rs_ref[...] = rstd

    row = lambda i: (i, 0)
    return _side_call(
        body, job, name=f"out_fwd{l}", grid=(T // tm,),
        in_specs=[pl.BlockSpec((tm, CW), row), pl.BlockSpec((tm, AW), row), pl.BlockSpec((tm, LW), row),
                  pl.BlockSpec((N_SHARD, WOUT_SH, D), lambda i: (0, 0, 0)),
                  pl.BlockSpec((tm, D), row),
                  pl.BlockSpec((None, 1, D), lambda i: (l, 0, 0)),
                  pl.BlockSpec((None, 1, D), lambda i: (l, 0, 0))],
        out_specs=[pl.BlockSpec((tm, D), row), pl.BlockSpec((tm, D), row), pl.BlockSpec((tm, D), row),
                   pl.BlockSpec((tm, 1), row)],
        out_shape=[jax.ShapeDtypeStruct((T, D), F32), jax.ShapeDtypeStruct((T, D), BF16),
                   jax.ShapeDtypeStruct((T, D), F32), jax.ShapeDtypeStruct((T, 1), F32)],
        scratch_shapes=[], semantics=("parallel",), args=[yc, ya, yl, w_out, h, g, b])


def _post_ln_bwd(dhn, xhat, rstd, g, l):
    T = dhn.shape[0]
    tm = _pick(T, (384, 128))

    def body(d_ref, xh_ref, rs_ref, g_ref, dz_ref, dzb_ref, dg_ref, db_ref):
        @pl.when(pl.program_id(0) == 0)
        def _():
            dg_ref[...] = jnp.zeros_like(dg_ref)
            db_ref[...] = jnp.zeros_like(db_ref)

        dy = d_ref[...]
        xhat = xh_ref[...]
        dz = _ln_bwd_rows(dy, xhat, rs_ref[...], g_ref[...])
        dz_ref[...] = dz
        dzb_ref[...] = dz.astype(BF16)
        dg_ref[...] += _colsum(dy * xhat)
        db_ref[...] += _colsum(dy)

    row = lambda i: (i, 0)
    const = lambda i: (0, 0)
    return pl.pallas_call(
        body, name=f"post_ln_bwd{l}", grid=(T // tm,),
        in_specs=[pl.BlockSpec((tm, D), row), pl.BlockSpec((tm, D), row), pl.BlockSpec((tm, 1), row),
                  pl.BlockSpec((None, 1, D), lambda i: (l, 0, 0))],
        out_specs=[pl.BlockSpec((tm, D), row), pl.BlockSpec((tm, D), row),
                   pl.BlockSpec((1, D), const), pl.BlockSpec((1, D), const)],
        out_shape=[jax.ShapeDtypeStruct((T, D), F32), jax.ShapeDtypeStruct((T, D), BF16),
                   jax.ShapeDtypeStruct((1, D), F32), jax.ShapeDtypeStruct((1, D), F32)],
        compiler_params=_cp("arbitrary"),
    )(dhn, xhat, rstd, g)


def _loss_post_ln_bwd(h, target, xhat, rstd, g, l):
    T = h.shape[0]
    tm = _pick(T, (384, 128))
    per = tm // BLK
    last_blk = target.shape[0] // BLK - 1

    def body(h_ref, *refs):
        t_refs, (xh_ref, rs_ref, g_ref, part_ref, dz_ref, dzb_ref, dg_ref, db_ref) = refs[:per], refs[per:]
        i = pl.program_id(0)

        @pl.when(i == 0)
        def _():
            part_ref[...] = jnp.zeros_like(part_ref)
            dg_ref[...] = jnp.zeros_like(dg_ref)
            db_ref[...] = jnp.zeros_like(db_ref)

        tgt = jnp.concatenate([r[...] for r in t_refs], axis=0) if per > 1 else t_refs[0][...]
        real = _row_ids(tm, i * tm) >= BLK
        err = jnp.where(real, h_ref[...] - tgt, 0.0)
        part_ref[...] += _colsum(err * err) * (0.5 / D)
        dy = err * (1.0 / D)
        xhat = xh_ref[...]
        dz = _ln_bwd_rows(dy, xhat, rs_ref[...], g_ref[...])
        dz_ref[...] = dz
        dzb_ref[...] = dz.astype(BF16)
        dg_ref[...] += _colsum(dy * xhat)
        db_ref[...] += _colsum(dy)

    row = lambda i: (i, 0)
    const = lambda i: (0, 0)
    t_specs = [pl.BlockSpec((BLK, D), functools.partial(lambda i, q: (jnp.clip(i * per - 1 + q, 0, last_blk), 0), q=q))
               for q in range(per)]
    return pl.pallas_call(
        body, name=f"loss_post_ln_bwd{l}", grid=(T // tm,),
        in_specs=[pl.BlockSpec((tm, D), row)] + t_specs + [
            pl.BlockSpec((tm, D), row), pl.BlockSpec((tm, 1), row), pl.BlockSpec((None, 1, D), lambda i: (l, 0, 0))],
        out_specs=[pl.BlockSpec((1, D), const), pl.BlockSpec((tm, D), row), pl.BlockSpec((tm, D), row),
                   pl.BlockSpec((1, D), const), pl.BlockSpec((1, D), const)],
        out_shape=[jax.ShapeDtypeStruct((1, D), F32), jax.ShapeDtypeStruct((T, D), F32),
                   jax.ShapeDtypeStruct((T, D), BF16), jax.ShapeDtypeStruct((1, D), F32),
                   jax.ShapeDtypeStruct((1, D), F32)],
        compiler_params=_cp("arbitrary"),
    )(h, *([target] * per), xhat, rstd, g)


def _dcat_bwd(dzb, w_out, l, job=None):
    T = dzb.shape[0]
    tm = _pick(T, (384, 128))

    def body(dz_ref, w_ref, dc_ref, da_ref, dl_ref):
        dz = dz_ref[...]
        dc_ref[...] = _dot_nt(dz, w_ref[0])
        da_ref[:, 0:WOUT_SH] = _dot_nt(dz, w_ref[1])
        da_ref[:, WOUT_SH:2 * WOUT_SH] = _dot_nt(dz, w_ref[2])
        dl_ref[...] = _dot_nt(dz, w_ref[3])

    row = lambda i: (i, 0)
    return _side_call(
        body, job, name=f"dcat_bwd{l}", grid=(T // tm,),
        in_specs=[pl.BlockSpec((tm, D), row),
                  pl.BlockSpec((N_SHARD, WOUT_SH, D), lambda i: (0, 0, 0))],
        out_specs=[pl.BlockSpec((tm, CW), row), pl.BlockSpec((tm, AW), row), pl.BlockSpec((tm, LW), row)],
        out_shape=[jax.ShapeDtypeStruct((T, CW), F32), jax.ShapeDtypeStruct((T, AW), F32),
                   jax.ShapeDtypeStruct((T, LW), F32)],
        scratch_shapes=[], semantics=("parallel",), args=[dzb, w_out])


def _dwout_bwd(yc, ya, yl, dzb, l):
    T = dzb.shape[0]
    tm = _pick(T, (384, 128))

    def body(yc_ref, ya_ref, yl_ref, dz_ref, o_ref):
        @pl.when(pl.program_id(0) == 0)
        def _():
            o_ref[...] = jnp.zeros_like(o_ref)

        cat = jnp.concatenate([yc_ref[...], ya_ref[...], yl_ref[...]], axis=1)
        o_ref[...] += _dot_tn(cat, dz_ref[...])

    row = lambda t: (t, 0)
    out = pl.pallas_call(
        body, name=f"dwout_bwd{l}", grid=(T // tm,),
        in_specs=[pl.BlockSpec((tm, CW), row), pl.BlockSpec((tm, AW), row), pl.BlockSpec((tm, LW), row),
                  pl.BlockSpec((tm, D), row)],
        out_specs=pl.BlockSpec((D, D), lambda t: (0, 0)),
        out_shape=jax.ShapeDtypeStruct((D, D), F32),
        compiler_params=_cp("arbitrary"),
    )(yc, ya, yl, dzb)
    return out.reshape(N_SHARD, 2, WOUT_SH // 2, D)


def _dh_bwd(dproj, w_in, dz, l, job=None):
    T = dproj.shape[0]
    tm = _pick(T, (1056, 384, 128))

    def body(dp_ref, w_ref, dz_ref, o_ref, acc_ref):
        j = pl.program_id(1)

        @pl.when(j == 0)
        def _():
            acc_ref[...] = ALPHA * dz_ref[...]

        acc_ref[...] += _dot_nt(dp_ref[...], w_ref[...])

        @pl.when(j == N_SHARD - 1)
        def _():
            o_ref[...] = acc_ref[...]

    return _side_call(
        body, job, name=f"dh_bwd{l}", grid=(T // tm, N_SHARD),
        in_specs=[pl.BlockSpec((tm, WIN_SH), lambda i, j: (i, j)),
                  pl.BlockSpec((None, D, WIN_SH), lambda i, j: (j, 0, 0)),
                  pl.BlockSpec((tm, D), lambda i, j: (i, 0))],
        out_specs=[pl.BlockSpec((tm, D), lambda i, j: (i, 0))],
        out_shape=[jax.ShapeDtypeStruct((T, D), F32)],
        scratch_shapes=[pltpu.VMEM((tm, D), F32)],
        semantics=("parallel", "arbitrary"), args=[dproj, w_in, dz])


def _dwin_bwd(hb, dproj, l):
    T = hb.shape[0]
    tm = _pick(T, (1056, 384, 128))

    def body(h_ref, dp_ref, o_ref):
        @pl.when(pl.program_id(1) == 0)
        def _():
            o_ref[...] = jnp.zeros_like(o_ref)

        o_ref[...] += _dot_tn(h_ref[...], dp_ref[...])

    out = pl.pallas_call(
        body, name=f"dwin_bwd{l}", grid=(N_SHARD, T // tm),
        in_specs=[pl.BlockSpec((tm, D), lambda j, t: (t, 0)),
                  pl.BlockSpec((tm, WIN_SH), lambda j, t: (t, j))],
        out_specs=pl.BlockSpec((None, D, WIN_SH), lambda j, t: (j, 0, 0)),
        out_shape=jax.ShapeDtypeStruct((N_SHARD, D, WIN_SH), F32),
        compiler_params=_cp("parallel", "arbitrary"),
    )(hb, dproj)
    return out.reshape(N_SHARD, 2, D // 2, WIN_SH)


def _dwin_half(hb, dproj, which, l, tag, job=None):
    T = hb.shape[0]
    tm = _pick(T, (1056, 384, 128))
    hr = D // 2

    def body(w_ref, h_ref, dp_ref, o_ref):
        @pl.when(pl.program_id(1) == 0)
        def _():
            o_ref[...] = jnp.zeros_like(o_ref)

        o_ref[...] += _dot_tn(h_ref[...], dp_ref[...])

    return _side_call(
        body, job, name=f"dwin_{tag}{l}", grid=(N_SHARD, T // tm),
        in_specs=[pl.BlockSpec((tm, hr), lambda j, t, w: (t, w[0])),
                  pl.BlockSpec((tm, WIN_SH), lambda j, t, w: (t, j))],
        out_specs=[pl.BlockSpec((None, hr, WIN_SH), lambda j, t, w: (j, 0, 0))],
        out_shape=[jax.ShapeDtypeStruct((N_SHARD, hr, WIN_SH), F32)],
        scratch_shapes=[], semantics=("parallel", "arbitrary"), args=[hb, dproj],
        prefetch=[jnp.reshape(which, (1,)).astype(jnp.int32)])


def _glu_masked(v, g, base_row):
    rows = _row_ids(v.shape[0], base_row)
    return jnp.where(rows >= PAD, v * _sigmoid(g), 0.0)


def _conv_tile(T):
    return _pick(T, (384, 128))


SUBLANES = 8


def _for_each_shift(buf, rot, tm, offsets, fn):
    for r in range(SUBLANES):
        group = [o for o in offsets if o % SUBLANES == r]
        if not group:
            continue
        if r == 0:
            src = buf
        else:
            n = tm + max(group) - r
            rot[0:n, :] = buf[r:r + n, :]
            src = rot
        for o in group:
            fn(o, src[o - r:o - r + tm, :])


def _conv_fwd(proj, dw_w, dw_b, ln_g, ln_b, pw_w, pw_b, l):
    T = proj.shape[0]
    tm = _conv_tile(T)
    hb = tm // HALO

    def body(cv_ref, cg_ref, ct_ref, hv_ref, hg_ref, w_ref, b_ref, g_ref, be_ref, pw_ref, pb_ref,
             yc_ref, conv_ref, buf, rot):
        i = pl.program_id(0)
        buf[0:HALO, :] = _glu_masked(hv_ref[...], hg_ref[...], i * tm - HALO)
        buf[HALO:HALO + tm, :] = _glu_masked(cv_ref[...], cg_ref[...], i * tm)
        first = HALO - (CONV_K - 1)
        total = [jnp.zeros((tm, CW), F32) + b_ref[...]]

        def tap(o, tile):
            k = o - first
            total[0] = total[0] + w_ref[k:k + 1, :] * tile

        _for_each_shift(buf, rot, tm, [first + k for k in range(CONV_K)], tap)
        acc = total[0]
        conv_ref[...] = acc
        u, _, _ = _ln_rows(acc, g_ref[...], be_ref[...])
        s = u * _sigmoid(u)
        cpw = _dot(s.astype(BF16), pw_ref[...]) + pb_ref[...]
        gate, _ = _silu_and_grad(ct_ref[...])
        yc_ref[...] = (cpw * gate).astype(BF16)

    vec = pl.BlockSpec((None, 1, CW), lambda i: (l, 0, 0))
    return pl.pallas_call(
        body, name=f"conv_fwd{l}", grid=(T // tm,),
        in_specs=[pl.BlockSpec((tm, CW), lambda i: (i, 0)),
                  pl.BlockSpec((tm, CW), lambda i: (i, 1)),
                  pl.BlockSpec((tm, CW), lambda i: (i, 2)),
                  pl.BlockSpec((HALO, CW), lambda i: (jnp.maximum(i * hb - 1, 0), 0)),
                  pl.BlockSpec((HALO, CW), lambda i: (jnp.maximum(i * hb - 1, 0), 1)),
                  pl.BlockSpec((None, CONV_K, CW), lambda i: (l, 0, 0)),
                  vec, vec, vec,
                  pl.BlockSpec((CW, CW), lambda i: (0, 0)),
                  vec],
        out_specs=[pl.BlockSpec((tm, CW), lambda i: (i, 0)), pl.BlockSpec((tm, CW), lambda i: (i, 0))],
        out_shape=[jax.ShapeDtypeStruct((T, CW), BF16), jax.ShapeDtypeStruct((T, CW), F32)],
        scratch_shapes=[pltpu.VMEM((tm + HALO, CW), F32), pltpu.VMEM((tm + HALO, CW), F32)],
        compiler_params=_cp("parallel"),
    )(proj, proj, proj, proj, proj, dw_w, dw_b, ln_g, ln_b, pw_w, pw_b)


def _conv_bwd_rows(conv, proj, d_yc, ln_g, ln_b, pw_w, pw_b, l):
    T = conv.shape[0]
    tm = _conv_tile(T)

    def body(conv_ref, ct_ref, dy_ref, g_ref, be_ref, pw_ref, pb_ref,
             dconv_ref, dct_ref, dpw_ref, dpb_ref, dg_ref, db_ref):
        @pl.when(pl.program_id(0) == 0)
        def _():
            dpw_ref[...] = jnp.zeros_like(dpw_ref)
            dpb_ref[...] = jnp.zeros_like(dpb_ref)
            dg_ref[...] = jnp.zeros_like(dg_ref)
            db_ref[...] = jnp.zeros_like(db_ref)

        u, xhat, rstd = _ln_rows(conv_ref[...], g_ref[...], be_ref[...])
        s, ds_du = _silu_and_grad(u)
        sb = s.astype(BF16)
        cpw = _dot(sb, pw_ref[...]) + pb_ref[...]
        gate, dgate = _silu_and_grad(ct_ref[...])
        dy = dy_ref[...]
        d_cpw = dy * gate
        dct_ref[...] = (dy * cpw * dgate).astype(BF16)
        d_cpw_b = d_cpw.astype(BF16)
        dpb_ref[...] += _colsum(d_cpw)
        dpw_ref[...] += _dot_tn(sb, d_cpw_b)
        du = _dot_nt(d_cpw_b, pw_ref[...]) * ds_du
        dconv_ref[...] = _ln_bwd_rows(du, xhat, rstd, g_ref[...])
        dg_ref[...] += _colsum(du * xhat)
        db_ref[...] += _colsum(du)

    vec = pl.BlockSpec((None, 1, CW), lambda i: (l, 0, 0))
    row = lambda i: (i, 0)
    const = lambda i: (0, 0)
    return pl.pallas_call(
        body, name=f"conv_bwd_rows{l}", grid=(T // tm,),
        in_specs=[pl.BlockSpec((tm, CW), row), pl.BlockSpec((tm, CW), lambda i: (i, 2)),
                  pl.BlockSpec((tm, CW), row), vec, vec,
                  pl.BlockSpec((CW, CW), lambda i: (0, 0)), vec],
        out_specs=[pl.BlockSpec((tm, CW), row), pl.BlockSpec((tm, CW), lambda i: (i, 2)),
                   pl.BlockSpec((CW, CW), const), pl.BlockSpec((1, CW), const),
                   pl.BlockSpec((1, CW), const), pl.BlockSpec((1, CW), const)],
        out_shape=[jax.ShapeDtypeStruct((T, CW), F32), jax.ShapeDtypeStruct((T, IN_TOTAL), BF16),
                   jax.ShapeDtypeStruct((CW, CW), F32), jax.ShapeDtypeStruct((1, CW), F32),
                   jax.ShapeDtypeStruct((1, CW), F32), jax.ShapeDtypeStruct((1, CW), F32)],
        compiler_params=_cp("arbitrary"),
    )(conv, proj, d_yc, ln_g, ln_b, pw_w, pw_b)


def _conv_bwd_taps(d_conv, proj, dw_w, dproj, l, job=None):
    T = d_conv.shape[0]
    tm = _conv_tile(T)
    hb = tm // HALO
    nt = T // tm
    last_halo = T // HALO - 1

    def body(dc_ref, dh_ref, cv_ref, cg_ref, hv_ref, hg_ref, w_ref, _, o_ref, dw_ref, dwb_ref, cbuf, dbuf, rot):
        i = pl.program_id(0)

        @pl.when(i == 0)
        def _():
            dw_ref[...] = jnp.zeros_like(dw_ref)
            dwb_ref[...] = jnp.zeros_like(dwb_ref)

        cbuf[0:HALO, :] = _glu_masked(hv_ref[...], hg_ref[...], i * tm - HALO)
        cbuf[HALO:HALO + tm, :] = _glu_masked(cv_ref[...], cg_ref[...], i * tm)
        dmain = dc_ref[...]
        dbuf[0:tm, :] = dmain
        dbuf[tm:tm + HALO, :] = jnp.where(i < nt - 1, dh_ref[...], 0.0)
        total = [jnp.zeros((tm, CW), F32)]

        def tap_back(o, tile):
            k = CONV_K - 1 - o
            total[0] = total[0] + w_ref[k:k + 1, :] * tile

        _for_each_shift(dbuf, rot, tm, list(range(CONV_K)), tap_back)
        acc = total[0]
        first = HALO - (CONV_K - 1)

        def tap_weight(o, tile):
            k = o - first
            dw_ref[k:k + 1, :] += _colsum(dmain * tile)

        _for_each_shift(cbuf, rot, tm, [first + k for k in range(CONV_K)], tap_weight)
        dwb_ref[...] += _colsum(dmain)
        d_c = jnp.where(_row_ids(tm, i * tm) >= PAD, acc, 0.0)
        sig = _sigmoid(cg_ref[...])
        o_ref[:, 0:CW] = (d_c * sig).astype(BF16)
        o_ref[:, CW:2 * CW] = (d_c * cv_ref[...] * sig * (1.0 - sig)).astype(BF16)

    const = lambda i: (0, 0)
    return _side_call(
        body, job, name=f"conv_bwd_taps{l}", grid=(nt,),
        in_specs=[pl.BlockSpec((tm, CW), lambda i: (i, 0)),
                  pl.BlockSpec((HALO, CW), lambda i: (jnp.minimum((i + 1) * hb, last_halo), 0)),
                  pl.BlockSpec((tm, CW), lambda i: (i, 0)),
                  pl.BlockSpec((tm, CW), lambda i: (i, 1)),
                  pl.BlockSpec((HALO, CW), lambda i: (jnp.maximum(i * hb - 1, 0), 0)),
                  pl.BlockSpec((HALO, CW), lambda i: (jnp.maximum(i * hb - 1, 0), 1)),
                  pl.BlockSpec((None, CONV_K, CW), lambda i: (l, 0, 0)),
                  pl.BlockSpec(memory_space=pl.ANY)],
        out_specs=[pl.BlockSpec((tm, 2 * CW), lambda i: (i, 0)),
                   pl.BlockSpec((HALO, CW), const), pl.BlockSpec((1, CW), const)],
        out_shape=[jax.ShapeDtypeStruct(dproj.shape, BF16), jax.ShapeDtypeStruct((HALO, CW), F32),
                   jax.ShapeDtypeStruct((1, CW), F32)],
        scratch_shapes=[pltpu.VMEM((tm + HALO, CW), F32), pltpu.VMEM((tm + HALO, CW), F32),
                        pltpu.VMEM((tm + HALO, CW), F32)],
        semantics=("arbitrary",), aliases={7: 0},
        args=[d_conv, d_conv, proj, proj, proj, proj, dw_w, dproj])


def _log1p_small(e):
    return jnp.where(e < 1e-3, e * (1.0 - e * (0.5 - e * (1.0 / 3.0))), jnp.log(1.0 + e))


def _softplus(z):
    return jnp.maximum(z, 0.0) + _log1p_small(jnp.exp(-jnp.abs(z)))


def _neg_expm1(x):
    series = -x * (1.0 + x * (1.0 / 2.0) * (1.0 + x * (1.0 / 3.0) * (1.0 + x * (1.0 / 4.0) * (
        1.0 + x * (1.0 / 5.0) * (1.0 + x * (1.0 / 6.0) * (1.0 + x * (1.0 / 7.0)))))))
    return jnp.where(x > -0.25, series, 1.0 - jnp.exp(x))


def _lru_gates(rxbuf, tm, base_row, lw_ref, lb_ref, wa_ref, ba_ref, wx_ref, bx_ref, lam_ref):
    rc = jnp.zeros((tm, LW), F32) + lb_ref[...]
    for k in range(LRU_K):
        o = LHALO - (LRU_K - 1) + k
        rc += lw_ref[k:k + 1, :] * rxbuf[o:o + tm, :]
    rcb = rc.astype(BF16)
    r = _sigmoid(_dot(rcb, wa_ref[...]) + ba_ref[...])
    ig = _sigmoid(_dot(rcb, wx_ref[...]) + bx_ref[...])
    sp = _softplus(-lam_ref[...])
    la = -LRU_C * r * sp
    a = jnp.exp(la)
    mult = jnp.sqrt(_neg_expm1(2.0 * la))
    valid = _row_ids(tm, base_row) >= PAD
    return rc, rcb, r, ig, sp, a, mult, valid


def _mask_rows(v, base_row):
    return jnp.where(_row_ids(v.shape[0], base_row) >= PAD, v, 0.0)


def _scan_rows(aa, bb, carry, out_ref, reverse):
    tm = aa.shape[0]
    sub = _row_ids(tm, 0) & (SUBLANES - 1)
    s = 1
    while s < SUBLANES:
        keep = (sub < SUBLANES - s) if reverse else (sub >= s)
        shift = tm - s if reverse else s
        a_s = jnp.where(keep, pltpu.roll(aa, shift, axis=0), 1.0)
        b_s = jnp.where(keep, pltpu.roll(bb, shift, axis=0), 0.0)
        bb = aa * b_s + bb
        aa = aa * a_s
        s *= 2
    groups = range(tm // SUBLANES)
    edge = 0 if reverse else SUBLANES - 1
    for j in (reversed(groups) if reverse else groups):
        rows = slice(SUBLANES * j, SUBLANES * j + SUBLANES)
        x = bb[rows] + aa[rows] * carry
        out_ref[rows, :] = x
        carry = x[edge:edge + 1]


def _lru_tile(T):
    return _pick(T, (384, 128))


def _lru_fwd(proj, lw, lb, wa, ba, wx, bx, lam, l):
    T = proj.shape[0]
    tm = _lru_tile(T)
    hb = tm // LHALO

    def body(rx_ref, rg_ref, hx_ref, lw_ref, lb_ref, wa_ref, ba_ref, wx_ref, bx_ref, lam_ref,
             yl_ref, hl_ref, rxbuf, carry):
        i = pl.program_id(0)

        @pl.when(i == 0)
        def _():
            carry[...] = jnp.zeros_like(carry)

        rxbuf[0:LHALO, :] = _mask_rows(hx_ref[...], i * tm - LHALO)
        rxbuf[LHALO:LHALO + tm, :] = _mask_rows(rx_ref[...], i * tm)
        rc, _, _, ig, _, a, mult, valid = _lru_gates(rxbuf, tm, i * tm, lw_ref, lb_ref, wa_ref, ba_ref,
                                                     wx_ref, bx_ref, lam_ref)
        bb = jnp.where(valid, mult * (ig * rc), 0.0)
        _scan_rows(a, bb, carry[0:1, :], hl_ref, reverse=False)
        carry[0:1, :] = hl_ref[tm - 1:tm, :]
        gate, _ = _silu_and_grad(rg_ref[...])
        yl_ref[...] = (hl_ref[...] * gate).astype(BF16)

    vec = pl.BlockSpec((None, 1, LW), lambda i: (l, 0, 0))
    mat = pl.BlockSpec((None, LW, LW), lambda i: (l, 0, 0))
    return pl.pallas_call(
        body, name=f"lru_fwd{l}", grid=(T // tm,),
        in_specs=[pl.BlockSpec((tm, LW), lambda i: (i, 8)),
                  pl.BlockSpec((tm, LW), lambda i: (i, 9)),
                  pl.BlockSpec((LHALO, LW), lambda i: (jnp.maximum(i * hb - 1, 0), 8)),
                  pl.BlockSpec((None, LRU_K, LW), lambda i: (l, 0, 0)),
                  vec, mat, vec, mat, vec, vec],
        out_specs=[pl.BlockSpec((tm, LW), lambda i: (i, 0)), pl.BlockSpec((tm, LW), lambda i: (i, 0))],
        out_shape=[jax.ShapeDtypeStruct((T, LW), BF16), jax.ShapeDtypeStruct((T, LW), F32)],
        scratch_shapes=[pltpu.VMEM((tm + LHALO, LW), F32), pltpu.VMEM((8, LW), F32)],
        compiler_params=_cp("arbitrary"),
    )(proj, proj, proj, lw, lb, wa, ba, wx, bx, lam)


def _lru_bwd(proj, hl, d_yl, lw, lb, wa, ba, wx, bx, lam, dproj, l, job=None):
    T = proj.shape[0]
    tm = _lru_tile(T)
    hb = tm // LHALO
    nt = T // tm

    def body(rx_ref, rg_ref, hx_ref, hl_ref, hh_ref, dy_ref, lw_ref, lb_ref, wa_ref, ba_ref, wx_ref, bx_ref,
             lam_ref, _, o_ref, dlw_ref, dlb_ref, dwa_ref, dba_ref, dwx_ref, dbx_ref, dlam_ref,
             rxbuf, dbuf, carry, head, gbuf):
        step = pl.program_id(0)
        i = nt - 1 - step

        @pl.when(step == 0)
        def _():
            carry[...] = jnp.zeros_like(carry)
            head[...] = jnp.zeros_like(head)
            for ref in (dlw_ref, dlb_ref, dwa_ref, dba_ref, dwx_ref, dbx_ref, dlam_ref):
                ref[...] = jnp.zeros_like(ref)

        rxbuf[0:LHALO, :] = _mask_rows(hx_ref[...], i * tm - LHALO)
        rxbuf[LHALO:LHALO + tm, :] = _mask_rows(rx_ref[...], i * tm)
        rc, rcb, r, ig, sp, a, mult, valid = _lru_gates(rxbuf, tm, i * tm, lw_ref, lb_ref, wa_ref, ba_ref,
                                                        wx_ref, bx_ref, lam_ref)
        rows = _row_ids(tm, 0)
        h = hl_ref[...]
        h_before = jnp.where(i > 0, hh_ref[LHALO - 1:LHALO, :], 0.0)
        hprev = jnp.where(rows == 0, h_before, pltpu.roll(h, 1, axis=0))
        rg = rg_ref[...]
        gate, dgate = _silu_and_grad(rg)
        dy = dy_ref[...]
        o_ref[:, LW:2 * LW] = (dy * h * dgate).astype(BF16)
        bb = dy * gate + jnp.where(rows == tm - 1, carry[0:1, :], 0.0)
        aa = jnp.where(rows == tm - 1, 0.0, pltpu.roll(a, tm - 1, axis=0))
        _scan_rows(aa, bb, jnp.zeros((1, LW), F32), gbuf, reverse=True)
        g = gbuf[...]
        dbuf[0:tm, :] = a * g
        carry[0:1, :] = dbuf[0:1, :]
        du = jnp.where(valid, g, 0.0)
        da = g * hprev
        dix = du * mult
        dmult = du * (ig * rc)
        dla = jnp.where(valid, da * a - dmult * (a * a) / mult, 0.0)
        dr = dla * (-LRU_C * sp)
        dlam_ref[...] += _colsum(dla * (LRU_C * r)) * _sigmoid(-lam_ref[...])
        dpa = dr * r * (1.0 - r)
        dpx = (dix * rc) * ig * (1.0 - ig)
        dpab = dpa.astype(BF16)
        dpxb = dpx.astype(BF16)
        dba_ref[...] += _colsum(dpa)
        dbx_ref[...] += _colsum(dpx)
        dwa_ref[...] += _dot_tn(rcb, dpab)
        dwx_ref[...] += _dot_tn(rcb, dpxb)
        drc = dix * ig + _dot_nt(dpab, wa_ref[...]) + _dot_nt(dpxb, wx_ref[...])
        dbuf[0:tm, :] = drc
        dbuf[tm:tm + LHALO, :] = head[...]
        acc = jnp.zeros((tm, LW), F32)
        for k in range(LRU_K):
            o = LRU_K - 1 - k
            acc += lw_ref[k:k + 1, :] * dbuf[o:o + tm, :]
            oc = LHALO - (LRU_K - 1) + k
            dlw_ref[k:k + 1, :] += _colsum(drc * rxbuf[oc:oc + tm, :])
        dlb_ref[...] += _colsum(drc)
        head[...] = dbuf[0:LHALO, :]
        o_ref[:, 0:LW] = jnp.where(valid, acc, 0.0).astype(BF16)

    rev = lambda s: nt - 1 - s
    vec = pl.BlockSpec((None, 1, LW), lambda s: (l, 0, 0))
    mat = pl.BlockSpec((None, LW, LW), lambda s: (l, 0, 0))
    const = lambda s: (0, 0)
    halo = lambda s: jnp.maximum(rev(s) * hb - 1, 0)
    return _side_call(
        body, job, name=f"lru_bwd{l}", grid=(nt,),
        in_specs=[pl.BlockSpec((tm, LW), lambda s: (rev(s), 8)),
                  pl.BlockSpec((tm, LW), lambda s: (rev(s), 9)),
                  pl.BlockSpec((LHALO, LW), lambda s: (halo(s), 8)),
                  pl.BlockSpec((tm, LW), lambda s: (rev(s), 0)),
                  pl.BlockSpec((LHALO, LW), lambda s: (halo(s), 0)),
                  pl.BlockSpec((tm, LW), lambda s: (rev(s), 0)),
                  pl.BlockSpec((None, LRU_K, LW), lambda s: (l, 0, 0)),
                  vec, mat, vec, mat, vec, vec, pl.BlockSpec(memory_space=pl.ANY)],
        out_specs=[pl.BlockSpec((tm, 2 * LW), lambda s: (rev(s), 4)),
                   pl.BlockSpec((8, LW), const), pl.BlockSpec((1, LW), const),
                   pl.BlockSpec((LW, LW), const), pl.BlockSpec((1, LW), const),
                   pl.BlockSpec((LW, LW), const), pl.BlockSpec((1, LW), const),
                   pl.BlockSpec((1, LW), const)],
        out_shape=[jax.ShapeDtypeStruct(dproj.shape, BF16),
                   jax.ShapeDtypeStruct((8, LW), F32), jax.ShapeDtypeStruct((1, LW), F32),
                   jax.ShapeDtypeStruct((LW, LW), F32), jax.ShapeDtypeStruct((1, LW), F32),
                   jax.ShapeDtypeStruct((LW, LW), F32), jax.ShapeDtypeStruct((1, LW), F32),
                   jax.ShapeDtypeStruct((1, LW), F32)],
        scratch_shapes=[pltpu.VMEM((tm + LHALO, LW), F32), pltpu.VMEM((tm + LHALO, LW), F32),
                        pltpu.VMEM((8, LW), F32), pltpu.VMEM((LHALO, LW), F32), pltpu.VMEM((tm, LW), F32)],
        semantics=("arbitrary",), aliases={13: 0},
        args=[proj, proj, proj, hl, hl, d_yl, lw, lb, wa, ba, wx, bx, lam, dproj])


def _rope_tables(T):
    pos = (lax.broadcasted_iota(jnp.int32, (T, 128), 0) - PAD).astype(F32)
    lane = lax.broadcasted_iota(jnp.int32, (T, 128), 1) % 64
    inv_freq = ROPE_THETA ** (-(lane % ROT_HALF).astype(F32) / ROT_HALF)
    ang = pos * inv_freq
    cos, sin = jnp.cos(ang), jnp.sin(ang)
    c = jnp.where(lane < 2 * ROT_HALF, cos, 1.0)
    s1 = jnp.where(lane < ROT_HALF, -sin, 0.0)
    s2 = jnp.where((lane >= ROT_HALF) & (lane < 2 * ROT_HALF), sin, 0.0)
    return c, s1, s2


def _rot_fwd(x, c, s1, s2):
    return x * c + pltpu.roll(x, 128 - ROT_HALF, axis=1) * s1 + pltpu.roll(x, ROT_HALF, axis=1) * s2


def _rot_bwd(dy, c, s1, s2):
    return dy * c + pltpu.roll(dy * s1, ROT_HALF, axis=1) + pltpu.roll(dy * s2, 128 - ROT_HALF, axis=1)


def _rope_fwd(proj, tabs, l):
    T = proj.shape[0]

    def body(ql_ref, qh_ref, k_ref, v_ref, c_ref, s1_ref, s2_ref, qr_ref, kr_ref, vb_ref):
        c, s1, s2 = c_ref[...], s1_ref[...], s2_ref[...]
        for gcol in range(AW // 128):
            src = ql_ref if gcol < 4 else qh_ref
            x = src[:, 128 * (gcol % 4):128 * (gcol % 4) + 128]
            qr_ref[:, 128 * gcol:128 * gcol + 128] = (_rot_fwd(x, c, s1, s2) * 0.125).astype(BF16)
        for gcol in range(KVW // 128):
            x = k_ref[:, 128 * gcol:128 * gcol + 128]
            kr_ref[:, 128 * gcol:128 * gcol + 128] = _rot_fwd(x, c, s1, s2).astype(BF16)
        vb_ref[...] = v_ref[...].astype(BF16)

    tr = _pick(T, (384, 128))
    tab = pl.BlockSpec((tr, 128), lambda n: (n, 0))
    return pl.pallas_call(
        body, name=f"rope_fwd{l}", grid=(T // tr,),
        in_specs=[pl.BlockSpec((tr, 512), lambda n: (n, 3)), pl.BlockSpec((tr, 512), lambda n: (n, 4)),
                  pl.BlockSpec((tr, KVW), lambda n: (n, 10)), pl.BlockSpec((tr, KVW), lambda n: (n, 11)),
                  tab, tab, tab],
        out_specs=[pl.BlockSpec((tr, AW), lambda n: (n, 0)), pl.BlockSpec((tr, KVW), lambda n: (n, 0)),
                   pl.BlockSpec((tr, KVW), lambda n: (n, 0))],
        out_shape=[jax.ShapeDtypeStruct((T, AW), BF16), jax.ShapeDtypeStruct((T, KVW), BF16),
                   jax.ShapeDtypeStruct((T, KVW), BF16)],
        compiler_params=_cp("parallel"),
    )(proj, proj, proj, proj, *tabs)


GROUP = 4


def _attn_mask(n, reps):
    qi = lax.broadcasted_iota(jnp.int32, (reps * BLK, BLK), 0) & (BLK - 1)
    kj = lax.broadcasted_iota(jnp.int32, (reps * BLK, BLK), 1)
    m0 = (kj >= PAD) & (n >= 1)
    mp = (kj > qi) & (n >= 2)
    mc = (kj <= qi) & ((n >= 1) | (kj >= PAD))
    return jnp.concatenate([m0, mp, mc], axis=1)


def _kv_both(x0_ref, xp_ref, xc_ref, g):
    pg, off = g // 2, g % 2
    cols = slice(128 * pg, 128 * pg + 128)
    x = jnp.concatenate([x0_ref[:, cols], xp_ref[:, cols], xc_ref[:, cols]], axis=0).astype(F32)
    lane = lax.broadcasted_iota(jnp.int32, (1, 128), 1)
    half = jnp.where((lane < 64) if off == 0 else (lane >= 64), x, 0.0)
    return (half + pltpu.roll(half, 64, axis=1)).astype(BF16)


def _kv_halves(x0_ref, xp_ref, xc_ref, g):
    pg, off = g // 2, g % 2
    cols = slice(128 * pg, 128 * pg + 128)
    x = jnp.concatenate([x0_ref[:, cols], xp_ref[:, cols], xc_ref[:, cols]], axis=0).astype(F32)
    lane = lax.broadcasted_iota(jnp.int32, (1, 128), 1)
    if off == 0:
        lo = jnp.where(lane < 64, x, 0.0)
        hi = pltpu.roll(lo, 64, axis=1)
    else:
        hi = jnp.where(lane >= 64, x, 0.0)
        lo = pltpu.roll(hi, 64, axis=1)
    return lo.astype(BF16), hi.astype(BF16)


def _stack_heads(a, b):
    lo = lax.broadcasted_iota(jnp.int32, (1, 128), 1) < 64
    a, b = a.astype(F32), b.astype(F32)
    return jnp.concatenate([jnp.where(lo, a, 0.0), jnp.where(lo, 0.0, a),
                            jnp.where(lo, b, 0.0), jnp.where(lo, 0.0, b)], axis=0).astype(BF16)


def _unstack_heads(x):
    lo = lax.broadcasted_iota(jnp.int32, (1, 128), 1) < 64
    return (jnp.where(lo, x[0:BLK], x[BLK:2 * BLK]), jnp.where(lo, x[2 * BLK:3 * BLK], x[3 * BLK:4 * BLK]))


def _per_head_column(values):
    return jnp.concatenate([jnp.zeros((BLK, 1), F32) + v for v in values], axis=0)


def _attn_fwd(qr, kr, vb, proj, sinks, l, job=None):
    T = qr.shape[0]

    def body(sink_ref, q_ref, k0_ref, kp_ref, kc_ref, v0_ref, vp_ref, vc_ref, ag_ref, ya_ref, att_ref, lse_ref):
        n = pl.program_id(0)
        mask = _attn_mask(n, 1)
        lane = lax.broadcasted_iota(jnp.int32, (1, 128), 1)
        lse_acc = jnp.zeros((BLK, 128), F32)
        for g in range(4):
            k_lo, k_hi = _kv_halves(k0_ref, kp_ref, kc_ref, g)
            v_lo, v_hi = _kv_halves(v0_ref, vp_ref, vc_ref, g)
            for pp in range(2):
                cols = slice(128 * (2 * g + pp), 128 * (2 * g + pp) + 128)
                qpair = q_ref[:, cols]
                out = jnp.zeros((BLK, 128), F32)
                for hh, (kx, vx) in enumerate(((k_lo, v_lo), (k_hi, v_hi))):
                    h = 4 * g + 2 * pp + hh
                    sink = sink_ref[l, h]
                    s = jnp.where(mask, _dot_nt(qpair, kx), NEG_INF)
                    m = jnp.maximum(jnp.max(s, axis=1, keepdims=True), sink)
                    p = jnp.exp(s - m)
                    denom = jnp.sum(p, axis=1, keepdims=True) + jnp.exp(sink - m)
                    out += _dot((p * (1.0 / denom)).astype(BF16), vx)
                    lse_acc = jnp.where(lane == h, m + jnp.log(denom), lse_acc)
                att_ref[:, cols] = out
                gate, _ = _silu_and_grad(ag_ref[:, cols])
                ya_ref[:, cols] = (out * gate).astype(BF16)
        lse_ref[...] = lse_acc

    prev = lambda n: (jnp.maximum(n - 1, 0), 0)
    cur = lambda n: (n, 0)
    zero = lambda n: (0, 0)
    kv = lambda f: pl.BlockSpec((BLK, KVW), f)
    return _side_call(
        body, job, name=f"attn_fwd{l}", grid=(T // BLK,),
        in_specs=[pl.BlockSpec(memory_space=pltpu.SMEM),
                  pl.BlockSpec((BLK, AW), cur), kv(zero), kv(prev), kv(cur), kv(zero), kv(prev), kv(cur),
                  pl.BlockSpec((BLK, AW), lambda n: (n, 3))],
        out_specs=[pl.BlockSpec((BLK, AW), cur), pl.BlockSpec((BLK, AW), cur), pl.BlockSpec((BLK, 128), cur)],
        out_shape=[jax.ShapeDtypeStruct((T, AW), BF16), jax.ShapeDtypeStruct((T, AW), F32),
                   jax.ShapeDtypeStruct((T, 128), F32)],
        scratch_shapes=[], semantics=("parallel",), args=[sinks, qr, kr, kr, kr, vb, vb, vb, proj])


def _attn_bwd(qr, kr, vb, proj, att, lse, d_ya, sinks, dproj, l, job=None):
    T = qr.shape[0]
    nb = T // BLK

    def body(sink_ref, q_ref, k0_ref, kp_ref, kc_ref, v0_ref, vp_ref, vc_ref, ag_ref, att_ref, lse_ref, dy_ref, _,
             dq_ref, dk_ref, dv_ref, dk0_ref, dv0_ref, dag_ref, dsink_ref, kcarry, vcarry):
        n = pl.program_id(0)

        @pl.when(n == 0)
        def _():
            dk0_ref[...] = jnp.zeros_like(dk0_ref)
            dv0_ref[...] = jnp.zeros_like(dv0_ref)
            dsink_ref[...] = jnp.zeros_like(dsink_ref)
            kcarry[...] = jnp.zeros_like(kcarry)
            vcarry[...] = jnp.zeros_like(vcarry)

        @pl.when(n == nb)
        def _():
            dk_ref[...] = kcarry[...]
            dv_ref[...] = vcarry[...]

        @pl.when(n < nb)
        def _():
            mask = _attn_mask(n, GROUP)
            lane = lax.broadcasted_iota(jnp.int32, (1, 128), 1)
            lse = lse_ref[...]
            dsink = jnp.zeros((1, 128), F32)
            dk_pg, dv_pg = [], []
            for pg in range(2):
                dk_acc = jnp.zeros((3 * BLK, 128), F32)
                dv_acc = jnp.zeros((3 * BLK, 128), F32)
                for off in range(2):
                    g = 2 * pg + off
                    kx = _kv_both(k0_ref, kp_ref, kc_ref, g)
                    vx = _kv_both(v0_ref, vp_ref, vc_ref, g)
                    pair_cols = [slice(128 * (2 * g + pp), 128 * (2 * g + pp) + 128) for pp in range(2)]
                    q4 = _stack_heads(q_ref[:, pair_cols[0]], q_ref[:, pair_cols[1]])
                    d_out = []
                    for cols in pair_cols:
                        gate, dgate = _silu_and_grad(ag_ref[:, cols])
                        dy = dy_ref[:, cols]
                        dag_ref[:, cols] = (dy * att_ref[:, cols] * dgate).astype(BF16)
                        d_out.append(dy * gate)
                    do4 = _stack_heads(d_out[0], d_out[1])
                    heads = [GROUP * g + r for r in range(GROUP)]
                    sink = _per_head_column([sink_ref[l, h] for h in heads])
                    lse4 = _per_head_column(
                        [jnp.sum(jnp.where(lane == h, lse, 0.0), axis=1, keepdims=True) for h in heads])
                    p = jnp.where(mask, jnp.exp(_dot_nt(q4, kx) - lse4), 0.0)
                    dp = _dot_nt(do4, vx)
                    delta = jnp.sum(p * dp, axis=1, keepdims=True)
                    ds = (p * (dp - delta)).astype(BF16)
                    sink_term = jnp.exp(sink - lse4) * delta
                    for r, h in enumerate(heads):
                        dsink += jnp.where(lane == h, -jnp.sum(sink_term[BLK * r:BLK * r + BLK]), 0.0)
                    for cols, dq in zip(pair_cols, _unstack_heads(_dot(ds, kx))):
                        dq_ref[:, cols] = dq
                    dkg = _dot_tn(ds, q4)
                    dvg = _dot_tn(p.astype(BF16), do4)
                    own = (lane < 64) if off == 0 else (lane >= 64)
                    dk_acc += jnp.where(own, dkg + pltpu.roll(dkg, 64, axis=1), 0.0)
                    dv_acc += jnp.where(own, dvg + pltpu.roll(dvg, 64, axis=1), 0.0)
                dk_pg.append(dk_acc)
                dv_pg.append(dv_acc)
            dsink_ref[...] += dsink
            for pg in range(2):
                cols = slice(128 * pg, 128 * pg + 128)
                dk0_ref[:, cols] += dk_pg[pg][0:BLK]
                dv0_ref[:, cols] += dv_pg[pg][0:BLK]
                dk_ref[:, cols] = kcarry[:, cols] + dk_pg[pg][BLK:2 * BLK]
                dv_ref[:, cols] = vcarry[:, cols] + dv_pg[pg][BLK:2 * BLK]
                kcarry[:, cols] = dk_pg[pg][2 * BLK:3 * BLK]
                vcarry[:, cols] = dv_pg[pg][2 * BLK:3 * BLK]

    last = nb - 1
    cur = lambda n: (jnp.minimum(n, last), 0)
    prev = lambda n: (jnp.clip(n - 1, 0, last), 0)
    zero = lambda n: (0, 0)
    kv = lambda f: pl.BlockSpec((BLK, KVW), f)
    wide = lambda f: pl.BlockSpec((BLK, AW), f)
    return _side_call(
        body, job, name=f"attn_bwd{l}", grid=(nb + 1,),
        in_specs=[pl.BlockSpec(memory_space=pltpu.SMEM),
                  wide(cur), kv(zero), kv(prev), kv(cur), kv(zero), kv(prev), kv(cur),
                  pl.BlockSpec((BLK, AW), lambda n: (jnp.minimum(n, last), 3)),
                  wide(cur), pl.BlockSpec((BLK, 128), cur), wide(cur), pl.BlockSpec(memory_space=pl.ANY)],
        out_specs=[wide(cur), kv(prev), kv(prev), kv(zero), kv(zero),
                   pl.BlockSpec((BLK, AW), lambda n: (jnp.minimum(n, last), 3)),
                   pl.BlockSpec((1, 128), zero)],
        out_shape=[jax.ShapeDtypeStruct((T, AW), F32), jax.ShapeDtypeStruct((T, KVW), F32),
                   jax.ShapeDtypeStruct((T, KVW), F32), jax.ShapeDtypeStruct((BLK, KVW), F32),
                   jax.ShapeDtypeStruct((BLK, KVW), F32), jax.ShapeDtypeStruct(dproj.shape, BF16),
                   jax.ShapeDtypeStruct((1, 128), F32)],
        scratch_shapes=[pltpu.VMEM((BLK, KVW), F32), pltpu.VMEM((BLK, KVW), F32)],
        semantics=("arbitrary",), aliases={12: 5},
        args=[sinks, qr, kr, kr, kr, vb, vb, vb, proj, att, lse, d_ya, dproj])


def _rope_bwd(dqr, dk, dv, dk0, dv0, tabs, dproj, l):
    T = dqr.shape[0]

    def body(dq_ref, dk_ref, dv_ref, dk0_ref, dv0_ref, c_ref, s1_ref, s2_ref, _, o_ref):
        n = pl.program_id(0)
        c, s1, s2 = c_ref[...], s1_ref[...], s2_ref[...]
        for gcol in range(AW // 128):
            cols = slice(128 * gcol, 128 * gcol + 128)
            o_ref[:, cols] = (_rot_bwd(dq_ref[:, cols], c, s1, s2) * 0.125).astype(BF16)
        for gcol in range(KVW // 128):
            cols = slice(128 * gcol, 128 * gcol + 128)
            kcols = slice(AW + 128 * gcol, AW + 128 * gcol + 128)
            vcols = slice(AW + KVW + 128 * gcol, AW + KVW + 128 * gcol + 128)
            o_ref[:, kcols] = _rot_bwd(dk_ref[:, cols], c, s1, s2).astype(BF16)
            o_ref[:, vcols] = dv_ref[:, cols].astype(BF16)

            @pl.when(n == 0)
            def _():
                dkk = dk_ref[0:BLK, cols] + dk0_ref[:, cols]
                o_ref[0:BLK, kcols] = _rot_bwd(dkk, c[0:BLK], s1[0:BLK], s2[0:BLK]).astype(BF16)
                o_ref[0:BLK, vcols] = (dv_ref[0:BLK, cols] + dv0_ref[:, cols]).astype(BF16)

    tr = _pick(T, (384, 128))
    cur = lambda n: (n, 0)
    zero = lambda n: (0, 0)
    tab = pl.BlockSpec((tr, 128), cur)
    return pl.pallas_call(
        body, name=f"rope_bwd{l}", grid=(T // tr,),
        in_specs=[pl.BlockSpec((tr, AW), cur), pl.BlockSpec((tr, KVW), cur), pl.BlockSpec((tr, KVW), cur),
                  pl.BlockSpec((BLK, KVW), zero), pl.BlockSpec((BLK, KVW), zero), tab, tab, tab,
                  pl.BlockSpec(memory_space=pl.ANY)],
        out_specs=pl.BlockSpec((tr, AW + 2 * KVW), lambda n: (n, 1)),
        out_shape=jax.ShapeDtypeStruct(dproj.shape, BF16),
        input_output_aliases={8: 0},
        compiler_params=_cp("parallel"),
    )(dqr, dk, dv, dk0, dv0, *tabs, dproj)


def _block_diag(w):
    nl, nh, hd, _ = w.shape
    eye = jnp.eye(nh, dtype=w.dtype)
    return jnp.einsum("lhij,hg->lhigj", w, eye).reshape(nl, nh * hd, nh * hd)


def _diag_blocks(m):
    nh, hd = 8, 64
    return jnp.einsum("hihj->hij", m.reshape(nh, hd, nh, hd))


def _device_step(x, target, p, dist=None):
    vec = lambda a: a.reshape(DEPTH, 1, a.shape[-1])
    ln_in_g, ln_in_b = p["ln_in_g"].reshape(1, D), p["ln_in_b"].reshape(1, D)
    conv_dw_b, conv_ln_g, conv_ln_b, conv_pw_b = map(vec, (p["conv_dw_b"], p["conv_ln_g"], p["conv_ln_b"], p["conv_pw_b"]))
    lru_conv_b, lru_ba, lru_bx, lru_lambda = map(vec, (p["lru_conv_b"], p["lru_ba"], p["lru_bx"], p["lru_lambda"]))
    ln_post_g, ln_post_b = vec(p["ln_post_g"]), vec(p["ln_post_b"])
    wa_bd = _block_diag(p["lru_wa"]).astype(BF16)
    wx_bd = _block_diag(p["lru_wx"]).astype(BF16)
    w_in, w_out, pw_w = list(p["w_in"]), list(p["w_out"]), list(p["conv_pw_w"])
    sinks = p["attn_sinks"]
    big_names = ("w_in", "w_out", "conv_pw_w")

    (h, hb), got = _embed_fwd(x, p["meta_tokens"], ln_in_g, ln_in_b, job=_gather_job([w_in[0]]) if dist else None)
    if dist:
        w_in[0] = got[0]
    T = h.shape[0]
    tabs = _rope_tables(T)
    saved = []
    for l in range(DEPTH):
        (proj,), got = _proj_fwd(hb, w_in[l], l, job=_gather_job([w_out[0], pw_w[0]]) if dist and l == 0 else None)
        if got:
            w_out[0], pw_w[0] = got
        pw_l = pw_w[l].reshape(CW, CW)
        yc, conv = _conv_fwd(proj, p["conv_dw_w"], conv_dw_b, conv_ln_g, conv_ln_b, pw_l, conv_pw_b, l)
        qr, kr, vb = _rope_fwd(proj, tabs, l)
        (ya, att, lse), got = _attn_fwd(
            qr, kr, vb, proj, sinks, l, job=_gather_job([w_in[1]]) if dist and l == 0 else None)
        if got:
            w_in[1] = got[0]
        yl, hl = _lru_fwd(proj, p["lru_conv_w"], lru_conv_b, wa_bd, lru_ba, wx_bd, lru_bx, lru_lambda, l)
        (hn, hnb, xhat, rstd), got = _out_fwd(
            yc, ya, yl, w_out[l], h, ln_post_g, ln_post_b, l,
            job=_gather_job([w_out[1], pw_w[1]]) if dist and l == 0 else None)
        if got:
            w_out[1], pw_w[1] = got
        saved.append((hb, proj, yc, conv, qr, kr, vb, ya, att, lse, yl, hl, xhat, rstd, pw_l))
        h, hb = hn, hnb

    dh = None
    g = {}
    later = None
    early, last = ("w_out", "conv_pw_w"), ("w_in",)
    own = {}
    for l in reversed(range(DEPTH)):
        hb_l, proj, yc, conv, qr, kr, vb, ya, att, lse, yl, hl, xhat, rstd, pw_l = saved[l]
        tail = dist is not None and l == 0
        if l == DEPTH - 1:
            loss_part, dz, dzb, g["ln_post_g", l], g["ln_post_b", l] = _loss_post_ln_bwd(
                h, target, xhat, rstd, ln_post_g, l)
        else:
            dz, dzb, g["ln_post_g", l], g["ln_post_b", l] = _post_ln_bwd(dh, xhat, rstd, ln_post_g, l)
        (d_yc, d_ya, d_yl), recv = _dcat_bwd(dzb, w_out[l], l, job=_swap_job(later["grads"]) if later else None)
        if later:
            later["parts"], later["owns"] = _chip_partials(big_names, later["grads"], recv, dist, later["l"])
        g["w_out", l] = _dwout_bwd(yc, ya, yl, dzb, l)
        d_conv, dproj, dpw, g["conv_pw_b", l], g["conv_ln_g", l], g["conv_ln_b", l] = _conv_bwd_rows(
            conv, proj, d_yc, conv_ln_g, conv_ln_b, pw_l, conv_pw_b, l)
        g["conv_pw_w", l] = dpw.reshape(N_SHARD, 2, PW_SH // 2, CW)
        if tail:
            own["early"] = dict(l=0, grads=[g[name, 0] for name in early])
        job = None
        if tail:
            job = _join_jobs(_swap_job(own["early"]["grads"]), _scatter_job(later["parts"][1:]))
        (dproj, ddw, g["conv_dw_b", l]), got = _conv_bwd_taps(d_conv, proj, p["conv_dw_w"], dproj, l, job=job)
        if tail:
            n_early = len(early)
            own["early"]["parts"], own["early"]["owns"] = _chip_partials(
                early, own["early"]["grads"], got[:n_early], dist, 0)
            later["z"] = got[n_early:]
        g["conv_dw_w", l] = ddw[:CONV_K]
        (dqr, dk, dv, dk0, dv0, dproj, dsink), z = _attn_bwd(
            qr, kr, vb, proj, att, lse, d_ya, sinks, dproj, l,
            job=_scatter_job(later["parts"][:1]) if later else None)
        if later:
            later["z"] = z + later["z"]
        g["attn_sinks", l] = dsink[0, :N_HEADS]
        dproj = _rope_bwd(dqr, dk, dv, dk0, dv0, tabs, dproj, l)
        (dproj, dlw, g["lru_conv_b", l], dwa, g["lru_ba", l], dwx, g["lru_bx", l], g["lru_lambda", l]), z = _lru_bwd(
            proj, hl, d_yl, p["lru_conv_w"], lru_conv_b, wa_bd, lru_ba, wx_bd, lru_bx, lru_lambda, dproj, l,
            job=_scatter_job(own["early"]["parts"]) if tail else None)
        if tail:
            own["early"]["z"] = z
        g["lru_conv_w", l] = dlw[:LRU_K]
        g["lru_wa", l] = _diag_blocks(dwa)
        g["lru_wx", l] = _diag_blocks(dwx)
        job = None
        if l > 0:
            g["w_in", l] = _dwin_bwd(hb_l, dproj, l)
        else:
            c = dist[0] if dist else jnp.int32(0)
            pack_a = _pack_rows([_layer_stack(g, name) for name in _SMALL_LAYERED]) if dist else None
            (give,), slots_a = _dwin_half(hb_l, dproj, 1 - c, l, "give", job=_spread_job(pack_a) if dist else None)
            (keep,), recv = _dwin_half(hb_l, dproj, c, l, "keep", job=_send_job([give]) if dist else None)
            if dist:
                g["pack_layered", -1] = _sum_slots(pack_a, slots_a[0], dist[3], "layered")
                own["last"] = dict(l=0)
                own["last"]["parts"], own["last"]["owns"] = _chip_partials(
                    last, [keep.reshape(N_SHARD, 1, D // 2, WIN_SH)], recv, (jnp.int32(0),) + tuple(dist[1:]), 0)
                job = _scatter_job(own["last"]["parts"])
            else:
                g["w_in", l] = jnp.stack([keep, give], axis=1)
        (dh,), got = _dh_bwd(dproj, w_in[l], dz, l, job=job)
        if tail:
            own["last"]["z"] = got
        if later:
            _finish_reduce(big_names, later, dist, g)
            later = None
        if dist and l > 0:
            later = dict(l=l, grads=[g[name, l] for name in big_names])
    grad_x, g["meta_tokens", -1], g["ln_in_g", -1], g["ln_in_b", -1] = _embed_bwd(
        dh, x, p["meta_tokens"], ln_in_g, ln_in_b)
    if dist:
        pack_b = _pack_rows([g[name, -1] for name in _SMALL_EMBED])
        slots_b = _run_job(_spread_job(pack_b), "spread_embed")[0]
        g["pack_embed", -1] = _sum_slots(pack_b, slots_b, dist[3], "embed")
        state = dict(l=0, owns=own["last"]["owns"] + own["early"]["owns"], z=own["last"]["z"] + own["early"]["z"])
        _finish_reduce(last + early, state, dist, g)
    return loss_part, grad_x, g


_SMALL_EMBED = ("meta_tokens", "ln_in_g", "ln_in_b")
_SMALL_LAYERED = ("conv_dw_w", "conv_dw_b", "conv_ln_g", "conv_ln_b", "conv_pw_b", "attn_sinks", "lru_conv_w",
                  "lru_conv_b", "lru_wa", "lru_ba", "lru_wx", "lru_bx", "lru_lambda", "ln_post_g", "ln_post_b")


def _layer_stack(g, name):
    return jnp.stack([g[name, l] for l in range(DEPTH)], axis=0)


def _chip_partials(names, grads, recv, dist, l):
    outs = [_chip_partial(a, r, dist[0], dist[1], f"{name}{l}") for name, a, r in zip(names, grads, recv)]
    return [o[0] for o in outs], [o[1] for o in outs]


def _finish_reduce(names, state, dist, g):
    l = state["l"]
    totals = [_shard_total(po, zz, dist[2], f"{name}{l}") for name, po, zz in zip(names, state["owns"], state["z"])]
    full = _run_job(_share_job(totals), f"share_halves{l}")
    for name, f in zip(names, full):
        g[name, l] = f.reshape(2 * f.shape[1], f.shape[2])


MESH = pl.DeviceIdType.MESH
HBM_SPEC = pl.BlockSpec(memory_space=pltpu.HBM)
N_DEV = 8


def _position():
    x, y, c = lax.axis_index("x"), lax.axis_index("y"), lax.axis_index("c")
    return x, y, c


def _other_chips(x, y):
    return [(1 - x, y), (x, 1 - y), (1 - x, 1 - y)]


def _cast_into_slot(a, l, j, tag):
    _, R, C = a.shape
    tb = _pick(R, (512, 128))

    def body(s_ref, a_ref, o_ref):
        o_ref[...] = a_ref[...].astype(BF16)

    grid_spec = pltpu.PrefetchScalarGridSpec(
        num_scalar_prefetch=1, grid=(R // tb,),
        in_specs=[pl.BlockSpec((None, tb, C), lambda t, sc: (l, t, 0))],
        out_specs=pl.BlockSpec((None, tb, C), lambda t, sc: (sc[0], t, 0)))
    return pl.pallas_call(
        body, name=f"cast_into_slot_{tag}{l}", grid_spec=grid_spec,
        out_shape=jax.ShapeDtypeStruct((N_SHARD, R, C), BF16),
        compiler_params=_cp("arbitrary"),
    )(jnp.reshape(j, (1,)).astype(jnp.int32), a)


class _Job:
    def __init__(self, inputs, aliased, extra_out, sems, start, mid, finish):
        self.inputs, self.aliased, self.extra_out, self.sems = list(inputs), aliased, list(extra_out), list(sems)
        self.start, self.mid, self.finish = start, mid, finish

    def out_shapes(self):
        own = [jax.ShapeDtypeStruct(a.shape, a.dtype) for a in self.inputs] if self.aliased else []
        return own + self.extra_out


def _side_call(body, job, *, name, grid, in_specs, out_specs, out_shape, scratch_shapes, semantics, args,
               aliases=None, prefetch=()):
    aliases = dict(aliases or {})
    n_pre = len(prefetch)
    assert not (n_pre and (aliases or (job is not None and job.aliased)))

    def call(fn, ins, outs, shapes, scratch, sem, operands):
        if n_pre:
            spec = pltpu.PrefetchScalarGridSpec(num_scalar_prefetch=n_pre, grid=grid, in_specs=ins, out_specs=outs,
                                                scratch_shapes=scratch)
            return pl.pallas_call(fn, name=name, grid_spec=spec, out_shape=shapes,
                                  compiler_params=_cp(*sem))(*prefetch, *operands)
        return pl.pallas_call(fn, name=name, grid=grid, in_specs=ins, out_specs=outs, out_shape=shapes,
                              scratch_shapes=scratch, input_output_aliases=aliases,
                              compiler_params=_cp(*sem))(*operands)

    if job is None:
        return list(call(body, list(in_specs), list(out_specs), list(out_shape), list(scratch_shapes),
                         semantics, args)), []
    n_in, n_out, n_scr = len(in_specs), len(out_specs), len(scratch_shapes)
    j_in, j_out = len(job.inputs), len(job.out_shapes())
    steps = 1
    for gsize in grid:
        steps *= gsize

    def wrapped(*refs):
        pre, refs = refs[:n_pre], refs[n_pre:]
        host_in, job_in = refs[:n_in], refs[n_in:n_in + j_in]
        o0 = n_in + j_in
        host_out, job_out = refs[o0:o0 + n_out], refs[o0 + n_out:o0 + n_out + j_out]
        s0 = o0 + n_out + j_out
        host_scr, sems = refs[s0:s0 + n_scr], refs[s0 + n_scr:]
        step = pl.program_id(0)
        for d in range(1, len(grid)):
            step = step * grid[d] + pl.program_id(d)

        @pl.when(step == 0)
        def _():
            job.start(job_in, job_out, sems)

        @pl.when(step == max(steps - 2, 0))
        def _():
            job.mid(job_in, job_out, sems)

        body(*pre, *host_in, *host_out, *host_scr)

        @pl.when(step == steps - 1)
        def _():
            job.finish(job_in, job_out, sems)

    if job.aliased:
        aliases.update({n_in + k: n_out + k for k in range(j_in)})
    outs = call(wrapped, list(in_specs) + [HBM_SPEC] * j_in, list(out_specs) + [HBM_SPEC] * j_out,
                list(out_shape) + job.out_shapes(), list(scratch_shapes) + job.sems,
                ["arbitrary"] * len(grid), [*args, *job.inputs])
    return list(outs[:n_out]), list(outs[n_out:])


def _run_job(job, name):
    return _side_call(lambda: None, job, name=name, grid=(1,), in_specs=[], out_specs=[], out_shape=[],
                      scratch_shapes=[], semantics=("arbitrary",), args=[])[1]


def _gather_job(slots):
    n = len(slots)

    def copies(buf, sems):
        ici_send, ici_recv, d2d_send, d2d_recv = sems
        x, y, c = _position()
        chips = _other_chips(x, y)

        def half(k, slot, which):
            hr = buf[k].shape[1] // 2
            return buf[k].at[slot, pl.ds(pl.multiple_of(which * hr, hr), hr)]

        def over_ici(k, p, slot):
            px, py = chips[p]
            return pltpu.make_async_remote_copy(
                src_ref=half(k, slot, c), dst_ref=half(k, slot, c),
                send_sem=ici_send.at[k * 3 + p], recv_sem=ici_recv.at[k * 3 + p],
                device_id=(px, py, c), device_id_type=MESH)

        def over_d2d(k, p, which):
            px, py = chips[p]
            return pltpu.make_async_remote_copy(
                src_ref=half(k, 2 * px + py, which), dst_ref=half(k, 2 * px + py, which),
                send_sem=d2d_send.at[k * 3 + p], recv_sem=d2d_recv.at[k * 3 + p],
                device_id=(x, y, 1 - c), device_id_type=MESH)

        return over_ici, over_d2d, 2 * x + y, chips, c

    pairs = [(k, p) for k in range(n) for p in range(3)]

    def start(_, buf, sems):
        over_ici, _, mine, _, _ = copies(buf, sems)
        for k, p in pairs:
            over_ici(k, p, mine).start()

    def mid(_, buf, sems):
        over_ici, over_d2d, _, chips, c = copies(buf, sems)
        for k, p in pairs:
            px, py = chips[p]
            over_ici(k, p, 2 * px + py).wait_recv()
            over_d2d(k, p, c).start()

    def finish(_, buf, sems):
        over_ici, over_d2d, mine, _, c = copies(buf, sems)
        for k, p in pairs:
            over_d2d(k, p, 1 - c).wait_recv()
        for k, p in pairs:
            over_ici(k, p, mine).wait_send()
            over_d2d(k, p, c).wait_send()

    return _Job(slots, True, [], [pltpu.SemaphoreType.DMA((3 * n,))] * 4, start, mid, finish)


def _gather_shards(shards):
    n = len(shards)

    def body(*refs):
        src, dst = refs[:n], refs[n:2 * n]
        send_sems, recv_sems, local_sems = refs[2 * n:]
        x, y, c = _position()
        mine = 2 * x + y
        chips = _other_chips(x, y)

        def copy(k, p):
            return pltpu.make_async_remote_copy(
                src_ref=src[k], dst_ref=dst[k].at[mine],
                send_sem=send_sems.at[k * 3 + p], recv_sem=recv_sems.at[k * 3 + p],
                device_id=(*chips[p], c), device_id_type=MESH)

        def arrival(k, p):
            px, py = chips[p]
            return pltpu.make_async_remote_copy(
                src_ref=src[k], dst_ref=dst[k].at[2 * px + py],
                send_sem=send_sems.at[k * 3 + p], recv_sem=recv_sems.at[k * 3 + p],
                device_id=(px, py, c), device_id_type=MESH)

        local = [pltpu.make_async_copy(src[k], dst[k].at[mine], local_sems.at[k]) for k in range(n)]
        for cp in local:
            cp.start()
        for k in range(n):
            for p in range(3):
                copy(k, p).start()
        for k in range(n):
            for p in range(3):
                arrival(k, p).wait_recv()
        for k in range(n):
            for p in range(3):
                copy(k, p).wait_send()
        for cp in local:
            cp.wait()

    return pl.pallas_call(
        body, name="gather_shards",
        in_specs=[HBM_SPEC] * n, out_specs=[HBM_SPEC] * n,
        out_shape=[jax.ShapeDtypeStruct((N_SHARD,) + s.shape, s.dtype) for s in shards],
        scratch_shapes=[pltpu.SemaphoreType.DMA((3 * n,)), pltpu.SemaphoreType.DMA((3 * n,)),
                        pltpu.SemaphoreType.DMA((n,))],
    )(*shards)


def _swap_job(grads):
    n = len(grads)

    def copies(src, dst, sems):
        x, y, c = _position()
        return [pltpu.make_async_remote_copy(
            src_ref=src[k].at[:, 1 - c], dst_ref=dst[k],
            send_sem=sems[0].at[k], recv_sem=sems[1].at[k],
            device_id=(x, y, 1 - c), device_id_type=MESH) for k in range(n)]

    def start(src, dst, sems):
        for cp in copies(src, dst, sems):
            cp.start()

    def finish(src, dst, sems):
        for cp in copies(src, dst, sems):
            cp.wait()

    return _Job(grads, False, [jax.ShapeDtypeStruct((N_SHARD,) + g.shape[2:], F32) for g in grads],
                [pltpu.SemaphoreType.DMA((n,))] * 2, start, lambda *_: None, finish)


def _send_job(arrays):
    n = len(arrays)

    def copies(src, dst, sems):
        x, y, c = _position()
        return [pltpu.make_async_remote_copy(
            src_ref=src[k], dst_ref=dst[k], send_sem=sems[0].at[k], recv_sem=sems[1].at[k],
            device_id=(x, y, 1 - c), device_id_type=MESH) for k in range(n)]

    def start(src, dst, sems):
        for cp in copies(src, dst, sems):
            cp.start()

    def finish(src, dst, sems):
        for cp in copies(src, dst, sems):
            cp.wait()

    return _Job(arrays, False, [jax.ShapeDtypeStruct(a.shape, a.dtype) for a in arrays],
                [pltpu.SemaphoreType.DMA((n,))] * 2, start, lambda *_: None, finish)


def _chip_partial(a, y, c, j, tag):
    _, _, R, C = a.shape
    tr = _pick(R, (256, 64))

    def body(s_ref, a_ref, y_ref, pb_ref, po_ref):
        total = a_ref[...] + y_ref[...]
        pb_ref[...] = total.astype(BF16)

        @pl.when(pl.program_id(1) == s_ref[1])
        def _():
            po_ref[...] = total

    grid_spec = pltpu.PrefetchScalarGridSpec(
        num_scalar_prefetch=1, grid=(R // tr, N_SHARD),
        in_specs=[pl.BlockSpec((None, None, tr, C), lambda t, s, sc: (s, sc[0], t, 0)),
                  pl.BlockSpec((None, tr, C), lambda t, s, sc: (s, t, 0))],
        out_specs=[pl.BlockSpec((None, tr, C), lambda t, s, sc: (s, t, 0)),
                   pl.BlockSpec((tr, C), lambda t, s, sc: (t, 0))])
    return pl.pallas_call(
        body, name=f"chip_partial_{tag}", grid_spec=grid_spec,
        out_shape=[jax.ShapeDtypeStruct((N_SHARD, R, C), BF16), jax.ShapeDtypeStruct((R, C), F32)],
        compiler_params=_cp("arbitrary", "arbitrary"),
    )(jnp.stack([c, j]).astype(jnp.int32), a, y)


def _scatter_job(parts):
    n = len(parts)
    pairs = [(k, p) for k in range(n) for p in range(3)]

    def copy(src, dst, sems, k, p, outgoing):
        x, y, c = _position()
        mine = 2 * x + y
        px, py = _other_chips(x, y)[p]
        theirs = 2 * px + py
        return pltpu.make_async_remote_copy(
            src_ref=src[k].at[theirs if outgoing else mine], dst_ref=dst[k].at[mine if outgoing else theirs],
            send_sem=sems[0].at[k * 3 + p], recv_sem=sems[1].at[k * 3 + p],
            device_id=(px, py, c), device_id_type=MESH)

    def start(src, dst, sems):
        for k, p in pairs:
            copy(src, dst, sems, k, p, True).start()

    def finish(src, dst, sems):
        for k, p in pairs:
            copy(src, dst, sems, k, p, False).wait_recv()
        for k, p in pairs:
            copy(src, dst, sems, k, p, True).wait_send()

    return _Job(parts, False, [jax.ShapeDtypeStruct(pb.shape, BF16) for pb in parts],
                [pltpu.SemaphoreType.DMA((3 * n,))] * 2, start, lambda *_: None, finish)


def _shard_total(own, z, others_c, tag):
    R, C = own.shape
    tr = _pick(R, (256, 64))

    def body(s_ref, o_ref, z0_ref, z1_ref, z2_ref, h_ref):
        h_ref[...] = ((o_ref[...] + z0_ref[...].astype(F32)) + z1_ref[...].astype(F32)) + z2_ref[...].astype(F32)

    zspec = lambda q: pl.BlockSpec((None, tr, C), lambda t, sc: (sc[q], t, 0))
    grid_spec = pltpu.PrefetchScalarGridSpec(
        num_scalar_prefetch=1, grid=(R // tr,),
        in_specs=[pl.BlockSpec((tr, C), lambda t, sc: (t, 0)), zspec(0), zspec(1), zspec(2)],
        out_specs=pl.BlockSpec((None, tr, C), lambda t, sc: (sc[3], t, 0)))
    return pl.pallas_call(
        body, name=f"shard_total_{tag}", grid_spec=grid_spec,
        out_shape=jax.ShapeDtypeStruct((2, R, C), F32),
        compiler_params=_cp("arbitrary"),
    )(others_c, own, z, z, z)


def _share_job(totals):
    n = len(totals)

    def copy(buf, sems, k, which):
        x, y, c = _position()
        return pltpu.make_async_remote_copy(
            src_ref=buf[k].at[which], dst_ref=buf[k].at[which],
            send_sem=sems[0].at[k], recv_sem=sems[1].at[k],
            device_id=(x, y, 1 - c), device_id_type=MESH)

    def start(_, buf, sems):
        c = lax.axis_index("c")
        for k in range(n):
            copy(buf, sems, k, c).start()

    def finish(_, buf, sems):
        c = lax.axis_index("c")
        for k in range(n):
            copy(buf, sems, k, 1 - c).wait_recv()
        for k in range(n):
            copy(buf, sems, k, c).wait_send()

    return _Job(totals, True, [], [pltpu.SemaphoreType.DMA((n,))] * 2, start, lambda *_: None, finish)


def _spread_job(pack):
    def copy(src, dst, sems, m, outgoing):
        x, y, c = _position()
        peer = (x ^ (m >> 2), y ^ ((m >> 1) & 1), c ^ (m & 1))
        slot = 4 * x + 2 * y + c if outgoing else 4 * peer[0] + 2 * peer[1] + peer[2]
        return pltpu.make_async_remote_copy(
            src_ref=src[0], dst_ref=dst[0].at[slot], send_sem=sems[0].at[m - 1], recv_sem=sems[1].at[m - 1],
            device_id=peer, device_id_type=MESH)

    def start(src, dst, sems):
        for m in range(1, N_DEV):
            copy(src, dst, sems, m, True).start()

    def finish(src, dst, sems):
        for m in range(1, N_DEV):
            copy(src, dst, sems, m, False).wait_recv()
        for m in range(1, N_DEV):
            copy(src, dst, sems, m, True).wait_send()

    return _Job([pack], False, [jax.ShapeDtypeStruct((N_DEV,) + pack.shape, F32)],
                [pltpu.SemaphoreType.DMA((N_DEV - 1,))] * 2, start, lambda *_: None, finish)


def _join_jobs(a, b):
    assert not a.aliased and not b.aliased
    n_in, n_out, n_sem = len(a.inputs), len(a.extra_out), len(a.sems)

    def phase(name):
        def run(ins, outs, sems):
            getattr(a, name)(ins[:n_in], outs[:n_out], sems[:n_sem])
            getattr(b, name)(ins[n_in:], outs[n_out:], sems[n_sem:])
        return run

    return _Job(a.inputs + b.inputs, False, a.extra_out + b.extra_out, a.sems + b.sems,
                phase("start"), phase("mid"), phase("finish"))


def _sum_slots(pack, slots, me, tag):
    def body(me_ref, p_ref, s_ref, o_ref):
        acc = None
        for d in range(N_DEV):
            term = jnp.where(me_ref[0] == d, p_ref[...], s_ref[d])
            acc = term if acc is None else acc + term
        o_ref[...] = acc

    vm = pl.BlockSpec(memory_space=pltpu.VMEM)
    return pl.pallas_call(
        body, name=f"sum_slots_{tag}",
        in_specs=[pl.BlockSpec(memory_space=pltpu.SMEM), vm, vm], out_specs=vm,
        out_shape=jax.ShapeDtypeStruct(pack.shape, F32),
        compiler_params=pltpu.CompilerParams(vmem_limit_bytes=V7X_VMEM_LIMIT),
    )(jnp.reshape(me, (1,)).astype(jnp.int32), pack, slots)


def _pack_rows(arrays):
    total = sum(a.size for a in arrays)
    rows = -(-total // 128)
    rows = -(-rows // PACK_ROWS_ALIGN) * PACK_ROWS_ALIGN
    flat = [a.reshape(-1) for a in arrays] + [jnp.zeros((rows * 128 - total,), F32)]
    return jnp.concatenate(flat).reshape(rows, 128)


def _adamw_math(w, g, m, v):
    m = ADAM_B1 * m + (1.0 - ADAM_B1) * g
    v = ADAM_B2 * v + (1.0 - ADAM_B2) * (g * g)
    m_hat = m / (1.0 - ADAM_B1 ** ADAM_STEP)
    v_hat = v / (1.0 - ADAM_B2 ** ADAM_STEP)
    delta = -ADAM_LR * (m_hat / (jnp.sqrt(v_hat) + ADAM_EPS) + ADAM_WD * w)
    return delta, m, v


def _adamw_big(w, g0, g1, m, v, tag):
    _, R, C = w.shape
    tr = _pick(R, (256, 128))

    def body(w_ref, g0_ref, g1_ref, m_ref, v_ref, go_ref, d_ref, mo_ref, vo_ref):
        g = jnp.where(pl.program_id(0) == 0, g0_ref[...], g1_ref[...])
        delta, mn, vn = _adamw_math(w_ref[...], g, m_ref[...], v_ref[...])
        go_ref[...] = g
        d_ref[...] = delta
        mo_ref[...] = mn
        vo_ref[...] = vn

    s3 = pl.BlockSpec((None, tr, C), lambda l, t: (l, t, 0))
    s2 = pl.BlockSpec((tr, C), lambda l, t: (t, 0))
    shp = jax.ShapeDtypeStruct(w.shape, F32)
    return pl.pallas_call(
        body, name=f"adamw_{tag}", grid=(2, R // tr),
        in_specs=[s3, s2, s2, s3, s3], out_specs=[s3, s3, s3, s3],
        out_shape=[shp, shp, shp, shp],
        compiler_params=_cp("parallel", "parallel"),
    )(w, g0, g1, m, v)


def _adamw_small(ws, gs, ms, vs):
    n = len(ws)

    def body(*refs):
        w_r, g_r, m_r, v_r = refs[:n], refs[n:2 * n], refs[2 * n:3 * n], refs[3 * n:4 * n]
        d_o, m_o, v_o = refs[4 * n:5 * n], refs[5 * n:6 * n], refs[6 * n:7 * n]
        for k in range(n):
            delta, mn, vn = _adamw_math(w_r[k][...], g_r[k][...], m_r[k][...], v_r[k][...])
            d_o[k][...] = delta
            m_o[k][...] = mn
            v_o[k][...] = vn

    vm = pl.BlockSpec(memory_space=pltpu.VMEM)
    shapes = [jax.ShapeDtypeStruct(w.shape, F32) for w in ws]
    outs = pl.pallas_call(
        body, name="adamw_small",
        in_specs=[vm] * (4 * n), out_specs=[vm] * (3 * n),
        out_shape=shapes * 3,
    )(*ws, *gs, *ms, *vs)
    return outs[:n], outs[n:2 * n], outs[2 * n:]


_WEIGHTS = ["meta_tokens", "ln_in_g", "ln_in_b", "w_in", "conv_dw_w", "conv_dw_b", "conv_ln_g", "conv_ln_b",
            "conv_pw_w", "conv_pw_b", "attn_sinks", "lru_conv_w", "lru_conv_b", "lru_wa", "lru_ba", "lru_wx",
            "lru_bx", "lru_lambda", "w_out", "ln_post_g", "ln_post_b"]
_BIG = ("w_in", "w_out", "conv_pw_w")
_SMALL_SHARDED = {"meta_tokens": 1, "conv_dw_w": 2, "lru_conv_w": 2}
PACK_ROWS_ALIGN = 8


def _as2d(a):
    return a.reshape(1, -1) if a.ndim == 1 else a.reshape(-1, a.shape[-1])


def kernel(x, meta_tokens, ln_in_g, ln_in_b, w_in, conv_dw_w, conv_dw_b, conv_ln_g, conv_ln_b, conv_pw_w, conv_pw_b, attn_sinks, lru_conv_w, lru_conv_b, lru_wa, lru_ba, lru_wx, lru_bx, lru_lambda, w_out, ln_post_g, ln_post_b, loss_target, m_meta_tokens, m_ln_in_g, m_ln_in_b, m_w_in, m_conv_dw_w, m_conv_dw_b, m_conv_ln_g, m_conv_ln_b, m_conv_pw_w, m_conv_pw_b, m_attn_sinks, m_lru_conv_w, m_lru_conv_b, m_lru_wa, m_lru_ba, m_lru_wx, m_lru_bx, m_lru_lambda, m_w_out, m_ln_post_g, m_ln_post_b, v_meta_tokens, v_ln_in_g, v_ln_in_b, v_w_in, v_conv_dw_w, v_conv_dw_b, v_conv_ln_g, v_conv_ln_b, v_conv_pw_w, v_conv_pw_b, v_attn_sinks, v_lru_conv_w, v_lru_conv_b, v_lru_wa, v_lru_ba, v_lru_wx, v_lru_bx, v_lru_lambda, v_w_out, v_ln_post_g, v_ln_post_b):
    w = dict(meta_tokens=meta_tokens, ln_in_g=ln_in_g, ln_in_b=ln_in_b, w_in=w_in, conv_dw_w=conv_dw_w,
             conv_dw_b=conv_dw_b, conv_ln_g=conv_ln_g, conv_ln_b=conv_ln_b, conv_pw_w=conv_pw_w,
             conv_pw_b=conv_pw_b, attn_sinks=attn_sinks, lru_conv_w=lru_conv_w, lru_conv_b=lru_conv_b,
             lru_wa=lru_wa, lru_ba=lru_ba, lru_wx=lru_wx, lru_bx=lru_bx, lru_lambda=lru_lambda, w_out=w_out,
             ln_post_g=ln_post_g, ln_post_b=ln_post_b)
    mom_m = dict(zip(_WEIGHTS, (m_meta_tokens, m_ln_in_g, m_ln_in_b, m_w_in, m_conv_dw_w, m_conv_dw_b, m_conv_ln_g,
                                m_conv_ln_b, m_conv_pw_w, m_conv_pw_b, m_attn_sinks, m_lru_conv_w, m_lru_conv_b,
                                m_lru_wa, m_lru_ba, m_lru_wx, m_lru_bx, m_lru_lambda, m_w_out, m_ln_post_g,
                                m_ln_post_b)))
    mom_v = dict(zip(_WEIGHTS, (v_meta_tokens, v_ln_in_g, v_ln_in_b, v_w_in, v_conv_dw_w, v_conv_dw_b, v_conv_ln_g,
                                v_conv_ln_b, v_conv_pw_w, v_conv_pw_b, v_attn_sinks, v_lru_conv_w, v_lru_conv_b,
                                v_lru_wa, v_lru_ba, v_lru_wx, v_lru_bx, v_lru_lambda, v_w_out, v_ln_post_g,
                                v_ln_post_b)))
    xi, yi, ci = _position()
    j = 2 * xi + yi

    g_meta, g_dw, g_lc = _gather_shards([meta_tokens, conv_dw_w, lru_conv_w])
    p = dict(w)
    p["w_in"] = [_cast_into_slot(w_in, l, j, "w_in") for l in range(DEPTH)]
    p["w_out"] = [_cast_into_slot(w_out, l, j, "w_out") for l in range(DEPTH)]
    p["conv_pw_w"] = [_cast_into_slot(conv_pw_w, l, j, "conv_pw_w") for l in range(DEPTH)]
    p["meta_tokens"] = g_meta.transpose(1, 0, 2).reshape(N_META, D)
    p["conv_dw_w"] = g_dw.transpose(1, 2, 0, 3).reshape(DEPTH, CONV_K, CW)
    p["lru_conv_w"] = g_lc.transpose(1, 2, 0, 3).reshape(DEPTH, LRU_K, LW)

    others = jnp.stack([jnp.where(j <= 0, 1, 0), jnp.where(j <= 1, 2, 1), jnp.where(j <= 2, 3, 2), ci]).astype(jnp.int32)
    me = 4 * xi + 2 * yi + ci
    loss_part, grad_x, g = _device_step(x[0], loss_target[0], p, dist=(ci, j, others, me))
    loss = lax.psum(jnp.sum(loss_part), ("x", "y", "c"))
    big = {(name, l): g[name, l] for name in _BIG for l in range(DEPTH)}

    small_names = [n for n in _WEIGHTS if n not in _BIG]
    small_g = {}
    for names, red in ((_SMALL_LAYERED, g["pack_layered", -1]), (_SMALL_EMBED, g["pack_embed", -1])):
        red = red.reshape(-1)
        off = 0
        for n in names:
            fshape = list(w[n].shape)
            if n in _SMALL_SHARDED:
                fshape[_SMALL_SHARDED[n]] *= N_SHARD
            sz = 1
            for dim in fshape:
                sz *= dim
            full = red[off:off + sz].reshape(fshape)
            off += sz
            if n in _SMALL_SHARDED:
                ax = _SMALL_SHARDED[n]
                full = lax.dynamic_slice_in_dim(full, j * w[n].shape[ax], w[n].shape[ax], axis=ax)
            small_g[n] = full

    out_g, out_d, out_m, out_v = {}, {}, {}, {}
    for name in _BIG:
        shp = w[name].shape
        to3 = lambda a: a.reshape(DEPTH, -1, shp[-1])
        go, do, mo, vo = _adamw_big(to3(w[name]), big[name, 0], big[name, 1], to3(mom_m[name]), to3(mom_v[name]), name)
        out_g[name], out_d[name], out_m[name], out_v[name] = (a.reshape(shp) for a in (go, do, mo, vo))
    ds, ms, vs = _adamw_small([_as2d(w[n]) for n in small_names], [_as2d(small_g[n]) for n in small_names],
                              [_as2d(mom_m[n]) for n in small_names], [_as2d(mom_v[n]) for n in small_names])
    for n, d_, m_, v_ in zip(small_names, ds, ms, vs):
        out_g[n] = small_g[n]
        out_d[n], out_m[n], out_v[n] = d_.reshape(w[n].shape), m_.reshape(w[n].shape), v_.reshape(w[n].shape)

    return (loss, grad_x[None], *[out_g[n] for n in _WEIGHTS], *[out_d[n] for n in _WEIGHTS],
            *[out_m[n] for n in _WEIGHTS], *[out_v[n] for n in _WEIGHTS])
```

```python
import functools

import jax
import jax.numpy as jnp
from jax import lax
from jax.experimental import pallas as pl
from jax.experimental.pallas import tpu as pltpu

F32 = jnp.float32
BF16 = jnp.bfloat16

D = 2048
N_META = 16
CW = 512
CONV_K = 31
AW = 1024
KVW = 256
N_HEADS = 16
LW = 512
LRU_K = 4
LRU_C = 8.0
IN_TOTAL = 5120
ROT_HALF = 8
ROPE_THETA = 500000.0
LN_EPS = 1e-5
DEPTH = 2
ALPHA = (2.0 * DEPTH) ** 0.25
NEG_INF = -1e30
ADAM_LR, ADAM_B1, ADAM_B2, ADAM_EPS, ADAM_WD, ADAM_STEP = 0.001, 0.9, 0.999, 1e-08, 0.01, 10

BLK = 128
PAD = BLK - N_META
N_SHARD = 4
WIN_SH = IN_TOTAL // N_SHARD
WOUT_SH = D // N_SHARD
PW_SH = CW // N_SHARD
HALO = 32
LHALO = 8
V7X_VMEM_LIMIT = 60 * 1024 * 1024


def _cp(*sem):
    return pltpu.CompilerParams(dimension_semantics=sem if sem else None, vmem_limit_bytes=V7X_VMEM_LIMIT)


def _pick(total, prefs):
    for p in prefs:
        if total % p == 0:
            return p
    raise ValueError(f"no tile for {total}")


def _dot(a, b):
    return jnp.dot(a, b, preferred_element_type=F32)


def _dot_nt(a, b):
    return lax.dot_general(a, b, (((1,), (1,)), ((), ())), preferred_element_type=F32)


def _dot_tn(a, b):
    return lax.dot_general(a, b, (((0,), (0,)), ((), ())), preferred_element_type=F32)


def _sigmoid(x):
    return 1.0 / (1.0 + jnp.exp(-x))


def _silu_and_grad(x):
    s = _sigmoid(x)
    return x * s, s * (1.0 + x * (1.0 - s))


def _ln_rows(x, g, b):
    mu = jnp.mean(x, axis=-1, keepdims=True)
    xc = x - mu
    var = jnp.mean(xc * xc, axis=-1, keepdims=True)
    rstd = lax.rsqrt(var + LN_EPS)
    xhat = xc * rstd
    return xhat * g + b, xhat, rstd


def _ln_bwd_rows(dy, xhat, rstd, g):
    dxh = dy * g
    m1 = jnp.mean(dxh, axis=-1, keepdims=True)
    m2 = jnp.mean(dxh * xhat, axis=-1, keepdims=True)
    return rstd * (dxh - m1 - xhat * m2)


def _row_ids(n, base):
    return base + lax.broadcasted_iota(jnp.int32, (n, 1), 0)


def _colsum(x):
    return jnp.sum(x, axis=0, keepdims=True)


def _embed_fwd(x, meta, g, b, job=None):
    S = x.shape[0]
    nb = S // BLK + 1

    def body(x_ref, meta_ref, g_ref, b_ref, h_ref, hb_ref):
        n = pl.program_id(0)

        @pl.when(n == 0)
        def _():
            y, _, _ = _ln_rows(meta_ref[...], g_ref[...], b_ref[...])
            h_ref[...] = jnp.zeros_like(h_ref)
            h_ref[PAD:BLK, :] = y

        @pl.when(n > 0)
        def _():
            y, _, _ = _ln_rows(x_ref[...], g_ref[...], b_ref[...])
            h_ref[...] = y

        hb_ref[...] = h_ref[...].astype(BF16)

    return _side_call(
        body, job, name="embed_fwd", grid=(nb,),
        in_specs=[pl.BlockSpec((BLK, D), lambda n: (jnp.maximum(n - 1, 0), 0)),
                  pl.BlockSpec((N_META, D), lambda n: (0, 0)),
                  pl.BlockSpec((1, D), lambda n: (0, 0)),
                  pl.BlockSpec((1, D), lambda n: (0, 0))],
        out_specs=[pl.BlockSpec((BLK, D), lambda n: (n, 0)),
                   pl.BlockSpec((BLK, D), lambda n: (n, 0))],
        out_shape=[jax.ShapeDtypeStruct((nb * BLK, D), F32), jax.ShapeDtypeStruct((nb * BLK, D), BF16)],
        scratch_shapes=[], semantics=("arbitrary",), args=[x, meta, g, b])


def _embed_bwd(dh, x, meta, g, b):
    S = x.shape[0]
    nb = S // BLK + 1

    def body(dh_ref, x_ref, meta_ref, g_ref, b_ref, gx_ref, gm_ref, dg_ref, db_ref):
        n = pl.program_id(0)

        @pl.when(n == 0)
        def _():
            _, xhat, rstd = _ln_rows(meta_ref[...], g_ref[...], b_ref[...])
            dy = dh_ref[PAD:BLK, :]
            gm_ref[...] = _ln_bwd_rows(dy, xhat, rstd, g_ref[...])
            dg_ref[...] = _colsum(dy * xhat)
            db_ref[...] = _colsum(dy)

        @pl.when(n > 0)
        def _():
            _, xhat, rstd = _ln_rows(x_ref[...], g_ref[...], b_ref[...])
            dy = dh_ref[...]
            gx_ref[...] = _ln_bwd_rows(dy, xhat, rstd, g_ref[...])
            dg_ref[...] += _colsum(dy * xhat)
            db_ref[...] += _colsum(dy)

    prev = lambda n: (jnp.maximum(n - 1, 0), 0)
    const = lambda n: (0, 0)
    return pl.pallas_call(
        body, name="embed_bwd", grid=(nb,),
        in_specs=[pl.BlockSpec((BLK, D), lambda n: (n, 0)),
                  pl.BlockSpec((BLK, D), prev),
                  pl.BlockSpec((N_META, D), const),
                  pl.BlockSpec((1, D), const),
                  pl.BlockSpec((1, D), const)],
        out_specs=[pl.BlockSpec((BLK, D), prev),
                   pl.BlockSpec((N_META, D), const),
                   pl.BlockSpec((1, D), const),
                   pl.BlockSpec((1, D), const)],
        out_shape=[jax.ShapeDtypeStruct((S, D), F32), jax.ShapeDtypeStruct((N_META, D), F32),
                   jax.ShapeDtypeStruct((1, D), F32), jax.ShapeDtypeStruct((1, D), F32)],
        compiler_params=_cp("arbitrary"),
    )(dh, x, meta, g, b)


def _proj_fwd(hb, w_in, l, job=None):
    T = hb.shape[0]
    tm = _pick(T, (1056, 384, 128))

    def body(a_ref, w_ref, o_ref):
        o_ref[...] = _dot(a_ref[...], w_ref[...])

    return _side_call(
        body, job, name=f"proj_fwd{l}", grid=(T // tm, N_SHARD),
        in_specs=[pl.BlockSpec((tm, D), lambda i, j: (i, 0)),
                  pl.BlockSpec((None, D, WIN_SH), lambda i, j: (j, 0, 0))],
        out_specs=[pl.BlockSpec((tm, WIN_SH), lambda i, j: (i, j))],
        out_shape=[jax.ShapeDtypeStruct((T, IN_TOTAL), F32)],
        scratch_shapes=[], semantics=("parallel", "arbitrary"), args=[hb, w_in])


def _out_fwd(yc, ya, yl, w_out, h, g, b, l, job=None):
    T = h.shape[0]
    tm = _pick(T, (384, 128))

    def body(yc_ref, ya_ref, yl_ref, w_ref, h_ref, g_ref, b_ref, hn_ref, hnb_ref, xh_ref, rs_ref):
        acc = _dot(yc_ref[...], w_ref[0])
        acc += _dot(ya_ref[:, 0:WOUT_SH], w_ref[1])
        acc += _dot(ya_ref[:, WOUT_SH:2 * WOUT_SH], w_ref[2])
        acc += _dot(yl_ref[...], w_ref[3])
        z = ALPHA * h_ref[...] + acc
        y, xhat, rstd = _ln_rows(z, g_ref[...], b_ref[...])
        hn_ref[...] = y
        hnb_ref[...] = y.astype(BF16)
        xh_ref[...] = xhat
        rs_ref[...] = rstd

    row = lambda i: (i, 0)
    return _side_call(
        body, job, name=f"out_fwd{l}", grid=(T // tm,),
        in_specs=[pl.BlockSpec((tm, CW), row), pl.BlockSpec((tm, AW), row), pl.BlockSpec((tm, LW), row),
                  pl.BlockSpec((N_SHARD, WOUT_SH, D), lambda i: (0, 0, 0)),
                  pl.BlockSpec((tm, D), row),
                  pl.BlockSpec((None, 1, D), lambda i: (l, 0, 0)),
                  pl.BlockSpec((None, 1, D), lambda i: (l, 0, 0))],
        out_specs=[pl.BlockSpec((tm, D), row), pl.BlockSpec((tm, D), row), pl.BlockSpec((tm, D), row),
                   pl.BlockSpec((tm, 1), row)],
        out_shape=[jax.ShapeDtypeStruct((T, D), F32), jax.ShapeDtypeStruct((T, D), BF16),
                   jax.ShapeDtypeStruct((T, D), F32), jax.ShapeDtypeStruct((T, 1), F32)],
        scratch_shapes=[], semantics=("parallel",), args=[yc, ya, yl, w_out, h, g, b])


def _post_ln_dcat_bwd(src, target, xhat, rstd, g, w_out, l, job=None):
    T = src.shape[0]
    tm = _pick(T, (384, 128))
    per = tm // BLK if target is not None else 0
    last_blk = target.shape[0] // BLK - 1 if target is not None else 0

    def body(s_ref, *refs):
        t_refs = refs[:per]
        (xh_ref, rs_ref, g_ref, w_ref, part_ref, dz_ref, dzb_ref, dg_ref, db_ref, dc_ref, da_ref, dl_ref) = refs[per:]
        i = pl.program_id(0)

        @pl.when(i == 0)
        def _():
            part_ref[...] = jnp.zeros_like(part_ref)
            dg_ref[...] = jnp.zeros_like(dg_ref)
            db_ref[...] = jnp.zeros_like(db_ref)

        if per:
            tgt = jnp.concatenate([r[...] for r in t_refs], axis=0) if per > 1 else t_refs[0][...]
            real = _row_ids(tm, i * tm) >= BLK
            err = jnp.where(real, s_ref[...] - tgt, 0.0)
            part_ref[...] += _colsum(err * err) * (0.5 / D)
            dy = err * (1.0 / D)
        else:
            dy = s_ref[...]
        xhat = xh_ref[...]
        dz = _ln_bwd_rows(dy, xhat, rs_ref[...], g_ref[...])
        dzb = dz.astype(BF16)
        dz_ref[...] = dz
        dzb_ref[...] = dzb
        dg_ref[...] += _colsum(dy * xhat)
        db_ref[...] += _colsum(dy)
        dc_ref[...] = _dot_nt(dzb, w_ref[0])
        da_ref[:, 0:WOUT_SH] = _dot_nt(dzb, w_ref[1])
        da_ref[:, WOUT_SH:2 * WOUT_SH] = _dot_nt(dzb, w_ref[2])
        dl_ref[...] = _dot_nt(dzb, w_ref[3])

    row = lambda i: (i, 0)
    const = lambda i: (0, 0)
    t_specs = [pl.BlockSpec((BLK, D), functools.partial(lambda i, q: (jnp.clip(i * per - 1 + q, 0, last_blk), 0), q=q))
               for q in range(per)]
    return _side_call(
        body, job, name=f"post_ln_dcat_bwd{l}", grid=(T // tm,),
        in_specs=[pl.BlockSpec((tm, D), row)] + t_specs + [
            pl.BlockSpec((tm, D), row), pl.BlockSpec((tm, 1), row), pl.BlockSpec((None, 1, D), lambda i: (l, 0, 0)),
            pl.BlockSpec((N_SHARD, WOUT_SH, D), lambda i: (0, 0, 0))],
        out_specs=[pl.BlockSpec((1, D), const), pl.BlockSpec((tm, D), row), pl.BlockSpec((tm, D), row),
                   pl.BlockSpec((1, D), const), pl.BlockSpec((1, D), const),
                   pl.BlockSpec((tm, CW), row), pl.BlockSpec((tm, AW), row), pl.BlockSpec((tm, LW), row)],
        out_shape=[jax.ShapeDtypeStruct((1, D), F32), jax.ShapeDtypeStruct((T, D), F32),
                   jax.ShapeDtypeStruct((T, D), BF16), jax.ShapeDtypeStruct((1, D), F32),
                   jax.ShapeDtypeStruct((1, D), F32), jax.ShapeDtypeStruct((T, CW), F32),
                   jax.ShapeDtypeStruct((T, AW), F32), jax.ShapeDtypeStruct((T, LW), F32)],
        scratch_shapes=[], semantics=("arbitrary",),
        args=[src] + [target] * per + [xhat, rstd, g, w_out])


def _dwout_bwd(yc, ya, yl, dzb, l):
    T = dzb.shape[0]
    tm = _pick(T, (384, 128))

    def body(yc_ref, ya_ref, yl_ref, dz_ref, o_ref):
        @pl.when(pl.program_id(0) == 0)
        def _():
            o_ref[...] = jnp.zeros_like(o_ref)

        cat = jnp.concatenate([yc_ref[...], ya_ref[...], yl_ref[...]], axis=1)
        o_ref[...] += _dot_tn(cat, dz_ref[...])

    row = lambda t: (t, 0)
    out = pl.pallas_call(
        body, name=f"dwout_bwd{l}", grid=(T // tm,),
        in_specs=[pl.BlockSpec((tm, CW), row), pl.BlockSpec((tm, AW), row), pl.BlockSpec((tm, LW), row),
                  pl.BlockSpec((tm, D), row)],
        out_specs=pl.BlockSpec((D, D), lambda t: (0, 0)),
        out_shape=jax.ShapeDtypeStruct((D, D), F32),
        compiler_params=_cp("arbitrary"),
    )(yc, ya, yl, dzb)
    return out.reshape(N_SHARD, 2, WOUT_SH // 2, D)


def _dh_bwd(dproj, w_in, dz, l, job=None):
    T = dproj.shape[0]
    tm = _pick(T, (1056, 384, 128))

    def body(dp_ref, w_ref, dz_ref, o_ref, acc_ref):
        j = pl.program_id(1)

        @pl.when(j == 0)
        def _():
            acc_ref[...] = ALPHA * dz_ref[...]

        acc_ref[...] += _dot_nt(dp_ref[...], w_ref[...])

        @pl.when(j == N_SHARD - 1)
        def _():
            o_ref[...] = acc_ref[...]

    return _side_call(
        body, job, name=f"dh_bwd{l}", grid=(T // tm, N_SHARD),
        in_specs=[pl.BlockSpec((tm, WIN_SH), lambda i, j: (i, j)),
                  pl.BlockSpec((None, D, WIN_SH), lambda i, j: (j, 0, 0)),
                  pl.BlockSpec((tm, D), lambda i, j: (i, 0))],
        out_specs=[pl.BlockSpec((tm, D), lambda i, j: (i, 0))],
        out_shape=[jax.ShapeDtypeStruct((T, D), F32)],
        scratch_shapes=[pltpu.VMEM((tm, D), F32)],
        semantics=("parallel", "arbitrary"), args=[dproj, w_in, dz])


def _dwin_bwd(hb, dproj, l):
    T = hb.shape[0]
    tm = _pick(T, (1056, 384, 128))

    def body(h_ref, dp_ref, o_ref):
        @pl.when(pl.program_id(1) == 0)
        def _():
            o_ref[...] = jnp.zeros_like(o_ref)

        o_ref[...] += _dot_tn(h_ref[...], dp_ref[...])

    out = pl.pallas_call(
        body, name=f"dwin_bwd{l}", grid=(N_SHARD, T // tm),
        in_specs=[pl.BlockSpec((tm, D), lambda j, t: (t, 0)),
                  pl.BlockSpec((tm, WIN_SH), lambda j, t: (t, j))],
        out_specs=pl.BlockSpec((None, D, WIN_SH), lambda j, t: (j, 0, 0)),
        out_shape=jax.ShapeDtypeStruct((N_SHARD, D, WIN_SH), F32),
        compiler_params=_cp("parallel", "arbitrary"),
    )(hb, dproj)
    return out.reshape(N_SHARD, 2, D // 2, WIN_SH)


def _dwin_half(hb, dproj, which, l, tag, job=None):
    T = hb.shape[0]
    tm = _pick(T, (1056, 384, 128))
    hr = D // 2

    def body(w_ref, h_ref, dp_ref, o_ref):
        @pl.when(pl.program_id(1) == 0)
        def _():
            o_ref[...] = jnp.zeros_like(o_ref)

        o_ref[...] += _dot_tn(h_ref[...], dp_ref[...])

    return _side_call(
        body, job, name=f"dwin_{tag}{l}", grid=(N_SHARD, T // tm),
        in_specs=[pl.BlockSpec((tm, hr), lambda j, t, w: (t, w[0])),
                  pl.BlockSpec((tm, WIN_SH), lambda j, t, w: (t, j))],
        out_specs=[pl.BlockSpec((None, hr, WIN_SH), lambda j, t, w: (j, 0, 0))],
        out_shape=[jax.ShapeDtypeStruct((N_SHARD, hr, WIN_SH), F32)],
        scratch_shapes=[], semantics=("parallel", "arbitrary"), args=[hb, dproj],
        prefetch=[jnp.reshape(which, (1,)).astype(jnp.int32)])


def _glu_masked(v, g, base_row):
    rows = _row_ids(v.shape[0], base_row)
    return jnp.where(rows >= PAD, v * _sigmoid(g), 0.0)


def _conv_tile(T):
    return _pick(T, (384, 128))


SUBLANES = 8


def _for_each_shift(buf, rot, tm, offsets, fn):
    for r in range(SUBLANES):
        group = [o for o in offsets if o % SUBLANES == r]
        if not group:
            continue
        if r == 0:
            src = buf
        else:
            n = tm + max(group) - r
            rot[0:n, :] = buf[r:r + n, :]
            src = rot
        for o in group:
            fn(o, src[o - r:o - r + tm, :])


def _conv_fwd(proj, dw_w, dw_b, ln_g, ln_b, pw_w, pw_b, l):
    T = proj.shape[0]
    tm = _conv_tile(T)
    hb = tm // HALO

    def body(cv_ref, cg_ref, ct_ref, hv_ref, hg_ref, w_ref, b_ref, g_ref, be_ref, pw_ref, pb_ref,
             yc_ref, conv_ref, buf, rot):
        i = pl.program_id(0)
        buf[0:HALO, :] = _glu_masked(hv_ref[...], hg_ref[...], i * tm - HALO)
        buf[HALO:HALO + tm, :] = _glu_masked(cv_ref[...], cg_ref[...], i * tm)
        first = HALO - (CONV_K - 1)
        total = [jnp.zeros((tm, CW), F32) + b_ref[...]]

        def tap(o, tile):
            k = o - first
            total[0] = total[0] + w_ref[k:k + 1, :] * tile

        _for_each_shift(buf, rot, tm, [first + k for k in range(CONV_K)], tap)
        acc = total[0]
        conv_ref[...] = acc
        u, _, _ = _ln_rows(acc, g_ref[...], be_ref[...])
        s = u * _sigmoid(u)
        cpw = _dot(s.astype(BF16), pw_ref[...]) + pb_ref[...]
        gate, _ = _silu_and_grad(ct_ref[...])
        yc_ref[...] = (cpw * gate).astype(BF16)

    vec = pl.BlockSpec((None, 1, CW), lambda i: (l, 0, 0))
    return pl.pallas_call(
        body, name=f"conv_fwd{l}", grid=(T // tm,),
        in_specs=[pl.BlockSpec((tm, CW), lambda i: (i, 0)),
                  pl.BlockSpec((tm, CW), lambda i: (i, 1)),
                  pl.BlockSpec((tm, CW), lambda i: (i, 2)),
                  pl.BlockSpec((HALO, CW), lambda i: (jnp.maximum(i * hb - 1, 0), 0)),
                  pl.BlockSpec((HALO, CW), lambda i: (jnp.maximum(i * hb - 1, 0), 1)),
                  pl.BlockSpec((None, CONV_K, CW), lambda i: (l, 0, 0)),
                  vec, vec, vec,
                  pl.BlockSpec((CW, CW), lambda i: (0, 0)),
                  vec],
        out_specs=[pl.BlockSpec((tm, CW), lambda i: (i, 0)), pl.BlockSpec((tm, CW), lambda i: (i, 0))],
        out_shape=[jax.ShapeDtypeStruct((T, CW), BF16), jax.ShapeDtypeStruct((T, CW), F32)],
        scratch_shapes=[pltpu.VMEM((tm + HALO, CW), F32), pltpu.VMEM((tm + HALO, CW), F32)],
        compiler_params=_cp("parallel"),
    )(proj, proj, proj, proj, proj, dw_w, dw_b, ln_g, ln_b, pw_w, pw_b)


def _conv_bwd_rows(conv, proj, d_yc, ln_g, ln_b, pw_w, pw_b, l):
    T = conv.shape[0]
    tm = _conv_tile(T)

    def body(conv_ref, ct_ref, dy_ref, g_ref, be_ref, pw_ref, pb_ref,
             dconv_ref, dct_ref, dpw_ref, dpb_ref, dg_ref, db_ref):
        @pl.when(pl.program_id(0) == 0)
        def _():
            dpw_ref[...] = jnp.zeros_like(dpw_ref)
            dpb_ref[...] = jnp.zeros_like(dpb_ref)
            dg_ref[...] = jnp.zeros_like(dg_ref)
            db_ref[...] = jnp.zeros_like(db_ref)

        u, xhat, rstd = _ln_rows(conv_ref[...], g_ref[...], be_ref[...])
        s, ds_du = _silu_and_grad(u)
        sb = s.astype(BF16)
        cpw = _dot(sb, pw_ref[...]) + pb_ref[...]
        gate, dgate = _silu_and_grad(ct_ref[...])
        dy = dy_ref[...]
        d_cpw = dy * gate
        dct_ref[...] = (dy * cpw * dgate).astype(BF16)
        d_cpw_b = d_cpw.astype(BF16)
        dpb_ref[...] += _colsum(d_cpw)
        dpw_ref[...] += _dot_tn(sb, d_cpw_b)
        du = _dot_nt(d_cpw_b, pw_ref[...]) * ds_du
        dconv_ref[...] = _ln_bwd_rows(du, xhat, rstd, g_ref[...])
        dg_ref[...] += _colsum(du * xhat)
        db_ref[...] += _colsum(du)

    vec = pl.BlockSpec((None, 1, CW), lambda i: (l, 0, 0))
    row = lambda i: (i, 0)
    const = lambda i: (0, 0)
    return pl.pallas_call(
        body, name=f"conv_bwd_rows{l}", grid=(T // tm,),
        in_specs=[pl.BlockSpec((tm, CW), row), pl.BlockSpec((tm, CW), lambda i: (i, 2)),
                  pl.BlockSpec((tm, CW), row), vec, vec,
                  pl.BlockSpec((CW, CW), lambda i: (0, 0)), vec],
        out_specs=[pl.BlockSpec((tm, CW), row), pl.BlockSpec((tm, CW), lambda i: (i, 2)),
                   pl.BlockSpec((CW, CW), const), pl.BlockSpec((1, CW), const),
                   pl.BlockSpec((1, CW), const), pl.BlockSpec((1, CW), const)],
        out_shape=[jax.ShapeDtypeStruct((T, CW), F32), jax.ShapeDtypeStruct((T, IN_TOTAL), BF16),
                   jax.ShapeDtypeStruct((CW, CW), F32), jax.ShapeDtypeStruct((1, CW), F32),
                   jax.ShapeDtypeStruct((1, CW), F32), jax.ShapeDtypeStruct((1, CW), F32)],
        compiler_params=_cp("arbitrary"),
    )(conv, proj, d_yc, ln_g, ln_b, pw_w, pw_b)


def _conv_bwd_taps(d_conv, proj, dw_w, dproj, l, job=None):
    T = d_conv.shape[0]
    tm = _conv_tile(T)
    hb = tm // HALO
    nt = T // tm
    last_halo = T // HALO - 1

    def body(dc_ref, dh_ref, cv_ref, cg_ref, hv_ref, hg_ref, w_ref, _, o_ref, dw_ref, dwb_ref, cbuf, dbuf, rot):
        i = pl.program_id(0)

        @pl.when(i == 0)
        def _():
            dw_ref[...] = jnp.zeros_like(dw_ref)
            dwb_ref[...] = jnp.zeros_like(dwb_ref)

        cbuf[0:HALO, :] = _glu_masked(hv_ref[...], hg_ref[...], i * tm - HALO)
        cbuf[HALO:HALO + tm, :] = _glu_masked(cv_ref[...], cg_ref[...], i * tm)
        dmain = dc_ref[...]
        dbuf[0:tm, :] = dmain
        dbuf[tm:tm + HALO, :] = jnp.where(i < nt - 1, dh_ref[...], 0.0)
        total = [jnp.zeros((tm, CW), F32)]

        def tap_back(o, tile):
            k = CONV_K - 1 - o
            total[0] = total[0] + w_ref[k:k + 1, :] * tile

        _for_each_shift(dbuf, rot, tm, list(range(CONV_K)), tap_back)
        acc = total[0]
        first = HALO - (CONV_K - 1)

        def tap_weight(o, tile):
            k = o - first
            dw_ref[k:k + 1, :] += _colsum(dmain * tile)

        _for_each_shift(cbuf, rot, tm, [first + k for k in range(CONV_K)], tap_weight)
        dwb_ref[...] += _colsum(dmain)
        d_c = jnp.where(_row_ids(tm, i * tm) >= PAD, acc, 0.0)
        sig = _sigmoid(cg_ref[...])
        o_ref[:, 0:CW] = (d_c * sig).astype(BF16)
        o_ref[:, CW:2 * CW] = (d_c * cv_ref[...] * sig * (1.0 - sig)).astype(BF16)

    const = lambda i: (0, 0)
    return _side_call(
        body, job, name=f"conv_bwd_taps{l}", grid=(nt,),
        in_specs=[pl.BlockSpec((tm, CW), lambda i: (i, 0)),
                  pl.BlockSpec((HALO, CW), lambda i: (jnp.minimum((i + 1) * hb, last_halo), 0)),
                  pl.BlockSpec((tm, CW), lambda i: (i, 0)),
                  pl.BlockSpec((tm, CW), lambda i: (i, 1)),
                  pl.BlockSpec((HALO, CW), lambda i: (jnp.maximum(i * hb - 1, 0), 0)),
                  pl.BlockSpec((HALO, CW), lambda i: (jnp.maximum(i * hb - 1, 0), 1)),
                  pl.BlockSpec((None, CONV_K, CW), lambda i: (l, 0, 0)),
                  pl.BlockSpec(memory_space=pl.ANY)],
        out_specs=[pl.BlockSpec((tm, 2 * CW), lambda i: (i, 0)),
                   pl.BlockSpec((HALO, CW), const), pl.BlockSpec((1, CW), const)],
        out_shape=[jax.ShapeDtypeStruct(dproj.shape, BF16), jax.ShapeDtypeStruct((HALO, CW), F32),
                   jax.ShapeDtypeStruct((1, CW), F32)],
        scratch_shapes=[pltpu.VMEM((tm + HALO, CW), F32), pltpu.VMEM((tm + HALO, CW), F32),
                        pltpu.VMEM((tm + HALO, CW), F32)],
        semantics=("arbitrary",), aliases={7: 0},
        args=[d_conv, d_conv, proj, proj, proj, proj, dw_w, dproj])


def _log1p_small(e):
    return jnp.where(e < 1e-3, e * (1.0 - e * (0.5 - e * (1.0 / 3.0))), jnp.log(1.0 + e))


def _softplus(z):
    return jnp.maximum(z, 0.0) + _log1p_small(jnp.exp(-jnp.abs(z)))


def _neg_expm1(x):
    series = -x * (1.0 + x * (1.0 / 2.0) * (1.0 + x * (1.0 / 3.0) * (1.0 + x * (1.0 / 4.0) * (
        1.0 + x * (1.0 / 5.0) * (1.0 + x * (1.0 / 6.0) * (1.0 + x * (1.0 / 7.0)))))))
    return jnp.where(x > -0.25, series, 1.0 - jnp.exp(x))


def _lru_gates(rxbuf, tm, base_row, lw_ref, lb_ref, wa_ref, ba_ref, wx_ref, bx_ref, lam_ref):
    rc = jnp.zeros((tm, LW), F32) + lb_ref[...]
    for k in range(LRU_K):
        o = LHALO - (LRU_K - 1) + k
        rc += lw_ref[k:k + 1, :] * rxbuf[o:o + tm, :]
    rcb = rc.astype(BF16)
    r = _sigmoid(_dot(rcb, wa_ref[...]) + ba_ref[...])
    ig = _sigmoid(_dot(rcb, wx_ref[...]) + bx_ref[...])
    sp = _softplus(-lam_ref[...])
    la = -LRU_C * r * sp
    a = jnp.exp(la)
    mult = jnp.sqrt(_neg_expm1(2.0 * la))
    valid = _row_ids(tm, base_row) >= PAD
    return rc, rcb, r, ig, sp, a, mult, valid


def _mask_rows(v, base_row):
    return jnp.where(_row_ids(v.shape[0], base_row) >= PAD, v, 0.0)


def _scan_rows(aa, bb, carry, out_ref, reverse):
    tm = aa.shape[0]
    sub = _row_ids(tm, 0) & (SUBLANES - 1)
    s = 1
    while s < SUBLANES:
        keep = (sub < SUBLANES - s) if reverse else (sub >= s)
        shift = tm - s if reverse else s
        a_s = jnp.where(keep, pltpu.roll(aa, shift, axis=0), 1.0)
        b_s = jnp.where(keep, pltpu.roll(bb, shift, axis=0), 0.0)
        bb = aa * b_s + bb
        aa = aa * a_s
        s *= 2
    groups = range(tm // SUBLANES)
    edge = 0 if reverse else SUBLANES - 1
    for j in (reversed(groups) if reverse else groups):
        rows = slice(SUBLANES * j, SUBLANES * j + SUBLANES)
        x = bb[rows] + aa[rows] * carry
        out_ref[rows, :] = x
        carry = x[edge:edge + 1]


def _lru_tile(T):
    return _pick(T, (384, 128))


def _lru_fwd(proj, lw, lb, wa, ba, wx, bx, lam, l):
    T = proj.shape[0]
    tm = _lru_tile(T)
    hb = tm // LHALO

    def body(rx_ref, rg_ref, hx_ref, lw_ref, lb_ref, wa_ref, ba_ref, wx_ref, bx_ref, lam_ref,
             yl_ref, hl_ref, rxbuf, carry):
        i = pl.program_id(0)

        @pl.when(i == 0)
        def _():
            carry[...] = jnp.zeros_like(carry)

        rxbuf[0:LHALO, :] = _mask_rows(hx_ref[...], i * tm - LHALO)
        rxbuf[LHALO:LHALO + tm, :] = _mask_rows(rx_ref[...], i * tm)
        rc, _, _, ig, _, a, mult, valid = _lru_gates(rxbuf, tm, i * tm, lw_ref, lb_ref, wa_ref, ba_ref,
                                                     wx_ref, bx_ref, lam_ref)
        bb = jnp.where(valid, mult * (ig * rc), 0.0)
        _scan_rows(a, bb, carry[0:1, :], hl_ref, reverse=False)
        carry[0:1, :] = hl_ref[tm - 1:tm, :]
        gate, _ = _silu_and_grad(rg_ref[...])
        yl_ref[...] = (hl_ref[...] * gate).astype(BF16)

    vec = pl.BlockSpec((None, 1, LW), lambda i: (l, 0, 0))
    mat = pl.BlockSpec((None, LW, LW), lambda i: (l, 0, 0))
    return pl.pallas_call(
        body, name=f"lru_fwd{l}", grid=(T // tm,),
        in_specs=[pl.BlockSpec((tm, LW), lambda i: (i, 8)),
                  pl.BlockSpec((tm, LW), lambda i: (i, 9)),
                  pl.BlockSpec((LHALO, LW), lambda i: (jnp.maximum(i * hb - 1, 0), 8)),
                  pl.BlockSpec((None, LRU_K, LW), lambda i: (l, 0, 0)),
                  vec, mat, vec, mat, vec, vec],
        out_specs=[pl.BlockSpec((tm, LW), lambda i: (i, 0)), pl.BlockSpec((tm, LW), lambda i: (i, 0))],
        out_shape=[jax.ShapeDtypeStruct((T, LW), BF16), jax.ShapeDtypeStruct((T, LW), F32)],
        scratch_shapes=[pltpu.VMEM((tm + LHALO, LW), F32), pltpu.VMEM((8, LW), F32)],
        compiler_params=_cp("arbitrary"),
    )(proj, proj, proj, lw, lb, wa, ba, wx, bx, lam)


def _lru_bwd(proj, hl, d_yl, lw, lb, wa, ba, wx, bx, lam, dproj, l, job=None):
    T = proj.shape[0]
    tm = _lru_tile(T)
    hb = tm // LHALO
    nt = T // tm

    def body(rx_ref, rg_ref, hx_ref, hl_ref, hh_ref, dy_ref, lw_ref, lb_ref, wa_ref, ba_ref, wx_ref, bx_ref,
             lam_ref, _, o_ref, dlw_ref, dlb_ref, dwa_ref, dba_ref, dwx_ref, dbx_ref, dlam_ref,
             rxbuf, dbuf, carry, head, gbuf):
        step = pl.program_id(0)
        i = nt - 1 - step

        @pl.when(step == 0)
        def _():
            carry[...] = jnp.zeros_like(carry)
            head[...] = jnp.zeros_like(head)
            for ref in (dlw_ref, dlb_ref, dwa_ref, dba_ref, dwx_ref, dbx_ref, dlam_ref):
                ref[...] = jnp.zeros_like(ref)

        rxbuf[0:LHALO, :] = _mask_rows(hx_ref[...], i * tm - LHALO)
        rxbuf[LHALO:LHALO + tm, :] = _mask_rows(rx_ref[...], i * tm)
        rc, rcb, r, ig, sp, a, mult, valid = _lru_gates(rxbuf, tm, i * tm, lw_ref, lb_ref, wa_ref, ba_ref,
                                                        wx_ref, bx_ref, lam_ref)
        rows = _row_ids(tm, 0)
        h = hl_ref[...]
        h_before = jnp.where(i > 0, hh_ref[LHALO - 1:LHALO, :], 0.0)
        hprev = jnp.where(rows == 0, h_before, pltpu.roll(h, 1, axis=0))
        rg = rg_ref[...]
        gate, dgate = _silu_and_grad(rg)
        dy = dy_ref[...]
        o_ref[:, LW:2 * LW] = (dy * h * dgate).astype(BF16)
        bb = dy * gate + jnp.where(rows == tm - 1, carry[0:1, :], 0.0)
        aa = jnp.where(rows == tm - 1, 0.0, pltpu.roll(a, tm - 1, axis=0))
        _scan_rows(aa, bb, jnp.zeros((1, LW), F32), gbuf, reverse=True)
        g = gbuf[...]
        dbuf[0:tm, :] = a * g
        carry[0:1, :] = dbuf[0:1, :]
        du = jnp.where(valid, g, 0.0)
        da = g * hprev
        dix = du * mult
        dmult = du * (ig * rc)
        dla = jnp.where(valid, da * a - dmult * (a * a) / mult, 0.0)
        dr = dla * (-LRU_C * sp)
        dlam_ref[...] += _colsum(dla * (LRU_C * r)) * _sigmoid(-lam_ref[...])
        dpa = dr * r * (1.0 - r)
        dpx = (dix * rc) * ig * (1.0 - ig)
        dpab = dpa.astype(BF16)
        dpxb = dpx.astype(BF16)
        dba_ref[...] += _colsum(dpa)
        dbx_ref[...] += _colsum(dpx)
        dwa_ref[...] += _dot_tn(rcb, dpab)
        dwx_ref[...] += _dot_tn(rcb, dpxb)
        drc = dix * ig + _dot_nt(dpab, wa_ref[...]) + _dot_nt(dpxb, wx_ref[...])
        dbuf[0:tm, :] = drc
        dbuf[tm:tm + LHALO, :] = head[...]
        acc = jnp.zeros((tm, LW), F32)
        for k in range(LRU_K):
            o = LRU_K - 1 - k
            acc += lw_ref[k:k + 1, :] * dbuf[o:o + tm, :]
            oc = LHALO - (LRU_K - 1) + k
            dlw_ref[k:k + 1, :] += _colsum(drc * rxbuf[oc:oc + tm, :])
        dlb_ref[...] += _colsum(drc)
        head[...] = dbuf[0:LHALO, :]
        o_ref[:, 0:LW] = jnp.where(valid, acc, 0.0).astype(BF16)

    rev = lambda s: nt - 1 - s
    vec = pl.BlockSpec((None, 1, LW), lambda s: (l, 0, 0))
    mat = pl.BlockSpec((None, LW, LW), lambda s: (l, 0, 0))
    const = lambda s: (0, 0)
    halo = lambda s: jnp.maximum(rev(s) * hb - 1, 0)
    return _side_call(
        body, job, name=f"lru_bwd{l}", grid=(nt,),
        in_specs=[pl.BlockSpec((tm, LW), lambda s: (rev(s), 8)),
                  pl.BlockSpec((tm, LW), lambda s: (rev(s), 9)),
                  pl.BlockSpec((LHALO, LW), lambda s: (halo(s), 8)),
                  pl.BlockSpec((tm, LW), lambda s: (rev(s), 0)),
                  pl.BlockSpec((LHALO, LW), lambda s: (halo(s), 0)),
                  pl.BlockSpec((tm, LW), lambda s: (rev(s), 0)),
                  pl.BlockSpec((None, LRU_K, LW), lambda s: (l, 0, 0)),
                  vec, mat, vec, mat, vec, vec, pl.BlockSpec(memory_space=pl.ANY)],
        out_specs=[pl.BlockSpec((tm, 2 * LW), lambda s: (rev(s), 4)),
                   pl.BlockSpec((8, LW), const), pl.BlockSpec((1, LW), const),
                   pl.BlockSpec((LW, LW), const), pl.BlockSpec((1, LW), const),
                   pl.BlockSpec((LW, LW), const), pl.BlockSpec((1, LW), const),
                   pl.BlockSpec((1, LW), const)],
        out_shape=[jax.ShapeDtypeStruct(dproj.shape, BF16),
                   jax.ShapeDtypeStruct((8, LW), F32), jax.ShapeDtypeStruct((1, LW), F32),
                   jax.ShapeDtypeStruct((LW, LW), F32), jax.ShapeDtypeStruct((1, LW), F32),
                   jax.ShapeDtypeStruct((LW, LW), F32), jax.ShapeDtypeStruct((1, LW), F32),
                   jax.ShapeDtypeStruct((1, LW), F32)],
        scratch_shapes=[pltpu.VMEM((tm + LHALO, LW), F32), pltpu.VMEM((tm + LHALO, LW), F32),
                        pltpu.VMEM((8, LW), F32), pltpu.VMEM((LHALO, LW), F32), pltpu.VMEM((tm, LW), F32)],
        semantics=("arbitrary",), aliases={13: 0},
        args=[proj, proj, proj, hl, hl, d_yl, lw, lb, wa, ba, wx, bx, lam, dproj])


def _rope_tables(T):
    pos = (lax.broadcasted_iota(jnp.int32, (T, 128), 0) - PAD).astype(F32)
    lane = lax.broadcasted_iota(jnp.int32, (T, 128), 1) % 64
    inv_freq = ROPE_THETA ** (-(lane % ROT_HALF).astype(F32) / ROT_HALF)
    ang = pos * inv_freq
    cos, sin = jnp.cos(ang), jnp.sin(ang)
    c = jnp.where(lane < 2 * ROT_HALF, cos, 1.0)
    s1 = jnp.where(lane < ROT_HALF, -sin, 0.0)
    s2 = jnp.where((lane >= ROT_HALF) & (lane < 2 * ROT_HALF), sin, 0.0)
    return c, s1, s2


def _rot_fwd(x, c, s1, s2):
    return x * c + pltpu.roll(x, 128 - ROT_HALF, axis=1) * s1 + pltpu.roll(x, ROT_HALF, axis=1) * s2


def _rot_bwd(dy, c, s1, s2):
    return dy * c + pltpu.roll(dy * s1, ROT_HALF, axis=1) + pltpu.roll(dy * s2, 128 - ROT_HALF, axis=1)


def _rope_fwd(proj, tabs, l):
    T = proj.shape[0]

    def body(ql_ref, qh_ref, k_ref, v_ref, c_ref, s1_ref, s2_ref, qr_ref, kr_ref, vb_ref):
        c, s1, s2 = c_ref[...], s1_ref[...], s2_ref[...]
        for gcol in range(AW // 128):
            src = ql_ref if gcol < 4 else qh_ref
            x = src[:, 128 * (gcol % 4):128 * (gcol % 4) + 128]
            qr_ref[:, 128 * gcol:128 * gcol + 128] = (_rot_fwd(x, c, s1, s2) * 0.125).astype(BF16)
        for gcol in range(KVW // 128):
            x = k_ref[:, 128 * gcol:128 * gcol + 128]
            kr_ref[:, 128 * gcol:128 * gcol + 128] = _rot_fwd(x, c, s1, s2).astype(BF16)
        vb_ref[...] = v_ref[...].astype(BF16)

    tr = _pick(T, (384, 128))
    tab = pl.BlockSpec((tr, 128), lambda n: (n, 0))
    return pl.pallas_call(
        body, name=f"rope_fwd{l}", grid=(T // tr,),
        in_specs=[pl.BlockSpec((tr, 512), lambda n: (n, 3)), pl.BlockSpec((tr, 512), lambda n: (n, 4)),
                  pl.BlockSpec((tr, KVW), lambda n: (n, 10)), pl.BlockSpec((tr, KVW), lambda n: (n, 11)),
                  tab, tab, tab],
        out_specs=[pl.BlockSpec((tr, AW), lambda n: (n, 0)), pl.BlockSpec((tr, KVW), lambda n: (n, 0)),
                   pl.BlockSpec((tr, KVW), lambda n: (n, 0))],
        out_shape=[jax.ShapeDtypeStruct((T, AW), BF16), jax.ShapeDtypeStruct((T, KVW), BF16),
                   jax.ShapeDtypeStruct((T, KVW), BF16)],
        compiler_params=_cp("parallel"),
    )(proj, proj, proj, proj, *tabs)


GROUP = 4


def _attn_mask(n, reps):
    qi = lax.broadcasted_iota(jnp.int32, (reps * BLK, BLK), 0) & (BLK - 1)
    kj = lax.broadcasted_iota(jnp.int32, (reps * BLK, BLK), 1)
    m0 = (kj >= PAD) & (n >= 1)
    mp = (kj > qi) & (n >= 2)
    mc = (kj <= qi) & ((n >= 1) | (kj >= PAD))
    return jnp.concatenate([m0, mp, mc], axis=1)


def _kv_both(x0_ref, xp_ref, xc_ref, g):
    pg, off = g // 2, g % 2
    cols = slice(128 * pg, 128 * pg + 128)
    x = jnp.concatenate([x0_ref[:, cols], xp_ref[:, cols], xc_ref[:, cols]], axis=0).astype(F32)
    lane = lax.broadcasted_iota(jnp.int32, (1, 128), 1)
    half = jnp.where((lane < 64) if off == 0 else (lane >= 64), x, 0.0)
    return (half + pltpu.roll(half, 64, axis=1)).astype(BF16)


def _kv_halves(x0_ref, xp_ref, xc_ref, g):
    pg, off = g // 2, g % 2
    cols = slice(128 * pg, 128 * pg + 128)
    x = jnp.concatenate([x0_ref[:, cols], xp_ref[:, cols], xc_ref[:, cols]], axis=0).astype(F32)
    lane = lax.broadcasted_iota(jnp.int32, (1, 128), 1)
    if off == 0:
        lo = jnp.where(lane < 64, x, 0.0)
        hi = pltpu.roll(lo, 64, axis=1)
    else:
        hi = jnp.where(lane >= 64, x, 0.0)
        lo = pltpu.roll(hi, 64, axis=1)
    return lo.astype(BF16), hi.astype(BF16)


def _stack_heads(a, b):
    lo = lax.broadcasted_iota(jnp.int32, (1, 128), 1) < 64
    a, b = a.astype(F32), b.astype(F32)
    return jnp.concatenate([jnp.where(lo, a, 0.0), jnp.where(lo, 0.0, a),
                            jnp.where(lo, b, 0.0), jnp.where(lo, 0.0, b)], axis=0).astype(BF16)


def _unstack_heads(x):
    lo = lax.broadcasted_iota(jnp.int32, (1, 128), 1) < 64
    return (jnp.where(lo, x[0:BLK], x[BLK:2 * BLK]), jnp.where(lo, x[2 * BLK:3 * BLK], x[3 * BLK:4 * BLK]))


def _per_head_column(values):
    return jnp.concatenate([jnp.zeros((BLK, 1), F32) + v for v in values], axis=0)


def _attn_fwd(qr, kr, vb, proj, sinks, l, job=None):
    T = qr.shape[0]

    def body(sink_ref, q_ref, k0_ref, kp_ref, kc_ref, v0_ref, vp_ref, vc_ref, ag_ref, ya_ref, att_ref, lse_ref):
        n = pl.program_id(0)
        mask = _attn_mask(n, 1)
        lane = lax.broadcasted_iota(jnp.int32, (1, 128), 1)
        lse_acc = jnp.zeros((BLK, 128), F32)
        for g in range(4):
            k_lo, k_hi = _kv_halves(k0_ref, kp_ref, kc_ref, g)
            v_lo, v_hi = _kv_halves(v0_ref, vp_ref, vc_ref, g)
            for pp in range(2):
                cols = slice(128 * (2 * g + pp), 128 * (2 * g + pp) + 128)
                qpair = q_ref[:, cols]
                out = jnp.zeros((BLK, 128), F32)
                for hh, (kx, vx) in enumerate(((k_lo, v_lo), (k_hi, v_hi))):
                    h = 4 * g + 2 * pp + hh
                    sink = sink_ref[l, h]
                    s = jnp.where(mask, _dot_nt(qpair, kx), NEG_INF)
                    m = jnp.maximum(jnp.max(s, axis=1, keepdims=True), sink)
                    p = jnp.exp(s - m)
                    denom = jnp.sum(p, axis=1, keepdims=True) + jnp.exp(sink - m)
                    out += _dot((p * (1.0 / denom)).astype(BF16), vx)
                    lse_acc = jnp.where(lane == h, m + jnp.log(denom), lse_acc)
                att_ref[:, cols] = out
                gate, _ = _silu_and_grad(ag_ref[:, cols])
                ya_ref[:, cols] = (out * gate).astype(BF16)
        lse_ref[...] = lse_acc

    prev = lambda n: (jnp.maximum(n - 1, 0), 0)
    cur = lambda n: (n, 0)
    zero = lambda n: (0, 0)
    kv = lambda f: pl.BlockSpec((BLK, KVW), f)
    return _side_call(
        body, job, name=f"attn_fwd{l}", grid=(T // BLK,),
        in_specs=[pl.BlockSpec(memory_space=pltpu.SMEM),
                  pl.BlockSpec((BLK, AW), cur), kv(zero), kv(prev), kv(cur), kv(zero), kv(prev), kv(cur),
                  pl.BlockSpec((BLK, AW), lambda n: (n, 3))],
        out_specs=[pl.BlockSpec((BLK, AW), cur), pl.BlockSpec((BLK, AW), cur), pl.BlockSpec((BLK, 128), cur)],
        out_shape=[jax.ShapeDtypeStruct((T, AW), BF16), jax.ShapeDtypeStruct((T, AW), F32),
                   jax.ShapeDtypeStruct((T, 128), F32)],
        scratch_shapes=[], semantics=("parallel",), args=[sinks, qr, kr, kr, kr, vb, vb, vb, proj])


def _attn_bwd(qr, kr, vb, proj, att, lse, d_ya, sinks, dproj, l, job=None):
    T = qr.shape[0]
    nb = T // BLK

    def body(sink_ref, q_ref, k0_ref, kp_ref, kc_ref, v0_ref, vp_ref, vc_ref, ag_ref, att_ref, lse_ref, dy_ref, _,
             dq_ref, dk_ref, dv_ref, dk0_ref, dv0_ref, dag_ref, dsink_ref, kcarry, vcarry):
        n = pl.program_id(0)

        @pl.when(n == 0)
        def _():
            dk0_ref[...] = jnp.zeros_like(dk0_ref)
            dv0_ref[...] = jnp.zeros_like(dv0_ref)
            dsink_ref[...] = jnp.zeros_like(dsink_ref)
            kcarry[...] = jnp.zeros_like(kcarry)
            vcarry[...] = jnp.zeros_like(vcarry)

        @pl.when(n == nb)
        def _():
            dk_ref[...] = kcarry[...]
            dv_ref[...] = vcarry[...]

        @pl.when(n < nb)
        def _():
            mask = _attn_mask(n, GROUP)
            lane = lax.broadcasted_iota(jnp.int32, (1, 128), 1)
            lse = lse_ref[...]
            dsink = jnp.zeros((1, 128), F32)
            dk_pg, dv_pg = [], []
            for pg in range(2):
                dk_acc = jnp.zeros((3 * BLK, 128), F32)
                dv_acc = jnp.zeros((3 * BLK, 128), F32)
                for off in range(2):
                    g = 2 * pg + off
                    kx = _kv_both(k0_ref, kp_ref, kc_ref, g)
                    vx = _kv_both(v0_ref, vp_ref, vc_ref, g)
                    pair_cols = [slice(128 * (2 * g + pp), 128 * (2 * g + pp) + 128) for pp in range(2)]
                    q4 = _stack_heads(q_ref[:, pair_cols[0]], q_ref[:, pair_cols[1]])
                    d_out = []
                    for cols in pair_cols:
                        gate, dgate = _silu_and_grad(ag_ref[:, cols])
                        dy = dy_ref[:, cols]
                        dag_ref[:, cols] = (dy * att_ref[:, cols] * dgate).astype(BF16)
                        d_out.append(dy * gate)
                    do4 = _stack_heads(d_out[0], d_out[1])
                    heads = [GROUP * g + r for r in range(GROUP)]
                    sink = _per_head_column([sink_ref[l, h] for h in heads])
                    lse4 = _per_head_column(
                        [jnp.sum(jnp.where(lane == h, lse, 0.0), axis=1, keepdims=True) for h in heads])
                    p = jnp.where(mask, jnp.exp(_dot_nt(q4, kx) - lse4), 0.0)
                    dp = _dot_nt(do4, vx)
                    delta = jnp.sum(p * dp, axis=1, keepdims=True)
                    ds = (p * (dp - delta)).astype(BF16)
                    sink_term = jnp.exp(sink - lse4) * delta
                    for r, h in enumerate(heads):
                        dsink += jnp.where(lane == h, -jnp.sum(sink_term[BLK * r:BLK * r + BLK]), 0.0)
                    for cols, dq in zip(pair_cols, _unstack_heads(_dot(ds, kx))):
                        dq_ref[:, cols] = dq
                    dkg = _dot_tn(ds, q4)
                    dvg = _dot_tn(p.astype(BF16), do4)
                    own = (lane < 64) if off == 0 else (lane >= 64)
                    dk_acc += jnp.where(own, dkg + pltpu.roll(dkg, 64, axis=1), 0.0)
                    dv_acc += jnp.where(own, dvg + pltpu.roll(dvg, 64, axis=1), 0.0)
                dk_pg.append(dk_acc)
                dv_pg.append(dv_acc)
            dsink_ref[...] += dsink
            for pg in range(2):
                cols = slice(128 * pg, 128 * pg + 128)
                dk0_ref[:, cols] += dk_pg[pg][0:BLK]
                dv0_ref[:, cols] += dv_pg[pg][0:BLK]
                dk_ref[:, cols] = kcarry[:, cols] + dk_pg[pg][BLK:2 * BLK]
                dv_ref[:, cols] = vcarry[:, cols] + dv_pg[pg][BLK:2 * BLK]
                kcarry[:, cols] = dk_pg[pg][2 * BLK:3 * BLK]
                vcarry[:, cols] = dv_pg[pg][2 * BLK:3 * BLK]

    last = nb - 1
    cur = lambda n: (jnp.minimum(n, last), 0)
    prev = lambda n: (jnp.clip(n - 1, 0, last), 0)
    zero = lambda n: (0, 0)
    kv = lambda f: pl.BlockSpec((BLK, KVW), f)
    wide = lambda f: pl.BlockSpec((BLK, AW), f)
    return _side_call(
        body, job, name=f"attn_bwd{l}", grid=(nb + 1,),
        in_specs=[pl.BlockSpec(memory_space=pltpu.SMEM),
                  wide(cur), kv(zero), kv(prev), kv(cur), kv(zero), kv(prev), kv(cur),
                  pl.BlockSpec((BLK, AW), lambda n: (jnp.minimum(n, last), 3)),
                  wide(cur), pl.BlockSpec((BLK, 128), cur), wide(cur), pl.BlockSpec(memory_space=pl.ANY)],
        out_specs=[wide(cur), kv(prev), kv(prev), kv(zero), kv(zero),
                   pl.BlockSpec((BLK, AW), lambda n: (jnp.minimum(n, last), 3)),
                   pl.BlockSpec((1, 128), zero)],
        out_shape=[jax.ShapeDtypeStruct((T, AW), F32), jax.ShapeDtypeStruct((T, KVW), F32),
                   jax.ShapeDtypeStruct((T, KVW), F32), jax.ShapeDtypeStruct((BLK, KVW), F32),
                   jax.ShapeDtypeStruct((BLK, KVW), F32), jax.ShapeDtypeStruct(dproj.shape, BF16),
                   jax.ShapeDtypeStruct((1, 128), F32)],
        scratch_shapes=[pltpu.VMEM((BLK, KVW), F32), pltpu.VMEM((BLK, KVW), F32)],
        semantics=("arbitrary",), aliases={12: 5},
        args=[sinks, qr, kr, kr, kr, vb, vb, vb, proj, att, lse, d_ya, dproj])


def _rope_bwd(dqr, dk, dv, dk0, dv0, tabs, dproj, l):
    T = dqr.shape[0]

    def body(dq_ref, dk_ref, dv_ref, dk0_ref, dv0_ref, c_ref, s1_ref, s2_ref, _, o_ref):
        n = pl.program_id(0)
        c, s1, s2 = c_ref[...], s1_ref[...], s2_ref[...]
        for gcol in range(AW // 128):
            cols = slice(128 * gcol, 128 * gcol + 128)
            o_ref[:, cols] = (_rot_bwd(dq_ref[:, cols], c, s1, s2) * 0.125).astype(BF16)
        for gcol in range(KVW // 128):
            cols = slice(128 * gcol, 128 * gcol + 128)
            kcols = slice(AW + 128 * gcol, AW + 128 * gcol + 128)
            vcols = slice(AW + KVW + 128 * gcol, AW + KVW + 128 * gcol + 128)
            o_ref[:, kcols] = _rot_bwd(dk_ref[:, cols], c, s1, s2).astype(BF16)
            o_ref[:, vcols] = dv_ref[:, cols].astype(BF16)

            @pl.when(n == 0)
            def _():
                dkk = dk_ref[0:BLK, cols] + dk0_ref[:, cols]
                o_ref[0:BLK, kcols] = _rot_bwd(dkk, c[0:BLK], s1[0:BLK], s2[0:BLK]).astype(BF16)
                o_ref[0:BLK, vcols] = (dv_ref[0:BLK, cols] + dv0_ref[:, cols]).astype(BF16)

    tr = _pick(T, (384, 128))
    cur = lambda n: (n, 0)
    zero = lambda n: (0, 0)
    tab = pl.BlockSpec((tr, 128), cur)
    return pl.pallas_call(
        body, name=f"rope_bwd{l}", grid=(T // tr,),
        in_specs=[pl.BlockSpec((tr, AW), cur), pl.BlockSpec((tr, KVW), cur), pl.BlockSpec((tr, KVW), cur),
                  pl.BlockSpec((BLK, KVW), zero), pl.BlockSpec((BLK, KVW), zero), tab, tab, tab,
                  pl.BlockSpec(memory_space=pl.ANY)],
        out_specs=pl.BlockSpec((tr, AW + 2 * KVW), lambda n: (n, 1)),
        out_shape=jax.ShapeDtypeStruct(dproj.shape, BF16),
        input_output_aliases={8: 0},
        compiler_params=_cp("parallel"),
    )(dqr, dk, dv, dk0, dv0, *tabs, dproj)


def _block_diag(w):
    nl, nh, hd, _ = w.shape
    eye = jnp.eye(nh, dtype=w.dtype)
    return jnp.einsum("lhij,hg->lhigj", w, eye).reshape(nl, nh * hd, nh * hd)


def _diag_blocks(m):
    nh, hd = 8, 64
    return jnp.einsum("hihj->hij", m.reshape(nh, hd, nh, hd))


def _device_step(x, target, p, dist=None):
    vec = lambda a: a.reshape(DEPTH, 1, a.shape[-1])
    ln_in_g, ln_in_b = p["ln_in_g"].reshape(1, D), p["ln_in_b"].reshape(1, D)
    conv_dw_b, conv_ln_g, conv_ln_b, conv_pw_b = map(vec, (p["conv_dw_b"], p["conv_ln_g"], p["conv_ln_b"], p["conv_pw_b"]))
    lru_conv_b, lru_ba, lru_bx, lru_lambda = map(vec, (p["lru_conv_b"], p["lru_ba"], p["lru_bx"], p["lru_lambda"]))
    ln_post_g, ln_post_b = vec(p["ln_post_g"]), vec(p["ln_post_b"])
    wa_bd = _block_diag(p["lru_wa"]).astype(BF16)
    wx_bd = _block_diag(p["lru_wx"]).astype(BF16)
    w_in, w_out, pw_w = list(p["w_in"]), list(p["w_out"]), list(p["conv_pw_w"])
    sinks = p["attn_sinks"]
    big_names = ("w_in", "w_out", "conv_pw_w")

    (h, hb), got = _embed_fwd(x, p["meta_tokens"], ln_in_g, ln_in_b, job=_gather_job([w_in[0]]) if dist else None)
    if dist:
        w_in[0] = got[0]
    T = h.shape[0]
    tabs = _rope_tables(T)
    saved = []
    for l in range(DEPTH):
        (proj,), got = _proj_fwd(hb, w_in[l], l, job=_gather_job([w_out[0], pw_w[0]]) if dist and l == 0 else None)
        if got:
            w_out[0], pw_w[0] = got
        pw_l = pw_w[l].reshape(CW, CW)
        yc, conv = _conv_fwd(proj, p["conv_dw_w"], conv_dw_b, conv_ln_g, conv_ln_b, pw_l, conv_pw_b, l)
        qr, kr, vb = _rope_fwd(proj, tabs, l)
        (ya, att, lse), got = _attn_fwd(
            qr, kr, vb, proj, sinks, l, job=_gather_job([w_in[1]]) if dist and l == 0 else None)
        if got:
            w_in[1] = got[0]
        yl, hl = _lru_fwd(proj, p["lru_conv_w"], lru_conv_b, wa_bd, lru_ba, wx_bd, lru_bx, lru_lambda, l)
        (hn, hnb, xhat, rstd), got = _out_fwd(
            yc, ya, yl, w_out[l], h, ln_post_g, ln_post_b, l,
            job=_gather_job([w_out[1], pw_w[1]]) if dist and l == 0 else None)
        if got:
            w_out[1], pw_w[1] = got
        saved.append((hb, proj, yc, conv, qr, kr, vb, ya, att, lse, yl, hl, xhat, rstd, pw_l))
        h, hb = hn, hnb

    dh = None
    g = {}
    later = None
    early, last = ("w_out", "conv_pw_w"), ("w_in",)
    own = {}
    for l in reversed(range(DEPTH)):
        hb_l, proj, yc, conv, qr, kr, vb, ya, att, lse, yl, hl, xhat, rstd, pw_l = saved[l]
        tail = dist is not None and l == 0
        top = l == DEPTH - 1
        (part, dz, dzb, g["ln_post_g", l], g["ln_post_b", l], d_yc, d_ya, d_yl), recv = _post_ln_dcat_bwd(
            h if top else dh, target if top else None, xhat, rstd, ln_post_g, w_out[l], l,
            job=_swap_job(later["grads"]) if later else None)
        if top:
            loss_part = part
        if later:
            later["parts"], later["owns"] = _chip_partials(big_names, later["grads"], recv, dist, later["l"])
        g["w_out", l] = _dwout_bwd(yc, ya, yl, dzb, l)
        d_conv, dproj, dpw, g["conv_pw_b", l], g["conv_ln_g", l], g["conv_ln_b", l] = _conv_bwd_rows(
            conv, proj, d_yc, conv_ln_g, conv_ln_b, pw_l, conv_pw_b, l)
        g["conv_pw_w", l] = dpw.reshape(N_SHARD, 2, PW_SH // 2, CW)
        if tail:
            own["early"] = dict(l=0, grads=[g[name, 0] for name in early])
        job = None
        if tail:
            job = _join_jobs(_swap_job(own["early"]["grads"]), _scatter_job(later["parts"][1:]))
        (dproj, ddw, g["conv_dw_b", l]), got = _conv_bwd_taps(d_conv, proj, p["conv_dw_w"], dproj, l, job=job)
        if tail:
            n_early = len(early)
            own["early"]["parts"], own["early"]["owns"] = _chip_partials(
                early, own["early"]["grads"], got[:n_early], dist, 0)
            later["z"] = got[n_early:]
        g["conv_dw_w", l] = ddw[:CONV_K]
        (dqr, dk, dv, dk0, dv0, dproj, dsink), z = _attn_bwd(
            qr, kr, vb, proj, att, lse, d_ya, sinks, dproj, l,
            job=_scatter_job(later["parts"][:1]) if later else None)
        if later:
            later["z"] = z + later["z"]
        g["attn_sinks", l] = dsink[0, :N_HEADS]
        dproj = _rope_bwd(dqr, dk, dv, dk0, dv0, tabs, dproj, l)
        (dproj, dlw, g["lru_conv_b", l], dwa, g["lru_ba", l], dwx, g["lru_bx", l], g["lru_lambda", l]), z = _lru_bwd(
            proj, hl, d_yl, p["lru_conv_w"], lru_conv_b, wa_bd, lru_ba, wx_bd, lru_bx, lru_lambda, dproj, l,
            job=_scatter_job(own["early"]["parts"]) if tail else None)
        if tail:
            own["early"]["z"] = z
        g["lru_conv_w", l] = dlw[:LRU_K]
        g["lru_wa", l] = _diag_blocks(dwa)
        g["lru_wx", l] = _diag_blocks(dwx)
        job = None
        if l > 0:
            g["w_in", l] = _dwin_bwd(hb_l, dproj, l)
        else:
            c = dist[0] if dist else jnp.int32(0)
            pack_a = _pack_rows([_layer_stack(g, name) for name in _SMALL_LAYERED]) if dist else None
            (give,), slots_a = _dwin_half(hb_l, dproj, 1 - c, l, "give", job=_spread_job(pack_a) if dist else None)
            (keep,), recv = _dwin_half(hb_l, dproj, c, l, "keep", job=_send_job([give]) if dist else None)
            if dist:
                g["pack_layered", -1] = _sum_slots(pack_a, slots_a[0], dist[3], "layered")
                own["last"] = dict(l=0)
                own["last"]["parts"], own["last"]["owns"] = _chip_partials(
                    last, [keep.reshape(N_SHARD, 1, D // 2, WIN_SH)], recv, (jnp.int32(0),) + tuple(dist[1:]), 0)
                job = _scatter_job(own["last"]["parts"])
            else:
                g["w_in", l] = jnp.stack([keep, give], axis=1)
        (dh,), got = _dh_bwd(dproj, w_in[l], dz, l, job=job)
        if tail:
            own["last"]["z"] = got
        if later:
            _finish_reduce(big_names, later, dist, g)
            later = None
        if dist and l > 0:
            later = dict(l=l, grads=[g[name, l] for name in big_names])
    grad_x, g["meta_tokens", -1], g["ln_in_g", -1], g["ln_in_b", -1] = _embed_bwd(
        dh, x, p["meta_tokens"], ln_in_g, ln_in_b)
    if dist:
        pack_b = _pack_rows([g[name, -1] for name in _SMALL_EMBED])
        slots_b = _run_job(_spread_job(pack_b), "spread_embed")[0]
        g["pack_embed", -1] = _sum_slots(pack_b, slots_b, dist[3], "embed")
        state = dict(l=0, owns=own["last"]["owns"] + own["early"]["owns"], z=own["last"]["z"] + own["early"]["z"])
        _finish_reduce(last + early, state, dist, g)
    return loss_part, grad_x, g


_SMALL_EMBED = ("meta_tokens", "ln_in_g", "ln_in_b")
_SMALL_LAYERED = ("conv_dw_w", "conv_dw_b", "conv_ln_g", "conv_ln_b", "conv_pw_b", "attn_sinks", "lru_conv_w",
                  "lru_conv_b", "lru_wa", "lru_ba", "lru_wx", "lru_bx", "lru_lambda", "ln_post_g", "ln_post_b")


def _layer_stack(g, name):
    return jnp.stack([g[name, l] for l in range(DEPTH)], axis=0)


def _chip_partials(names, grads, recv, dist, l):
    outs = [_chip_partial(a, r, dist[0], dist[1], f"{name}{l}") for name, a, r in zip(names, grads, recv)]
    return [o[0] for o in outs], [o[1] for o in outs]


def _finish_reduce(names, state, dist, g):
    l = state["l"]
    totals = [_shard_total(po, zz, dist[2], f"{name}{l}") for name, po, zz in zip(names, state["owns"], state["z"])]
    full = _run_job(_share_job(totals), f"share_halves{l}")
    for name, f in zip(names, full):
        g[name, l] = f.reshape(2 * f.shape[1], f.shape[2])


MESH = pl.DeviceIdType.MESH
HBM_SPEC = pl.BlockSpec(memory_space=pltpu.HBM)
N_DEV = 8


def _position():
    x, y, c = lax.axis_index("x"), lax.axis_index("y"), lax.axis_index("c")
    return x, y, c


def _other_chips(x, y):
    return [(1 - x, y), (x, 1 - y), (1 - x, 1 - y)]


def _cast_into_slot(a, l, j, tag):
    _, R, C = a.shape
    tb = _pick(R, (512, 128))

    def body(s_ref, a_ref, o_ref):
        o_ref[...] = a_ref[...].astype(BF16)

    grid_spec = pltpu.PrefetchScalarGridSpec(
        num_scalar_prefetch=1, grid=(R // tb,),
        in_specs=[pl.BlockSpec((None, tb, C), lambda t, sc: (l, t, 0))],
        out_specs=pl.BlockSpec((None, tb, C), lambda t, sc: (sc[0], t, 0)))
    return pl.pallas_call(
        body, name=f"cast_into_slot_{tag}{l}", grid_spec=grid_spec,
        out_shape=jax.ShapeDtypeStruct((N_SHARD, R, C), BF16),
        compiler_params=_cp("arbitrary"),
    )(jnp.reshape(j, (1,)).astype(jnp.int32), a)


class _Job:
    def __init__(self, inputs, aliased, extra_out, sems, start, mid, finish):
        self.inputs, self.aliased, self.extra_out, self.sems = list(inputs), aliased, list(extra_out), list(sems)
        self.start, self.mid, self.finish = start, mid, finish

    def out_shapes(self):
        own = [jax.ShapeDtypeStruct(a.shape, a.dtype) for a in self.inputs] if self.aliased else []
        return own + self.extra_out


def _side_call(body, job, *, name, grid, in_specs, out_specs, out_shape, scratch_shapes, semantics, args,
               aliases=None, prefetch=()):
    aliases = dict(aliases or {})
    n_pre = len(prefetch)
    assert not (n_pre and (aliases or (job is not None and job.aliased)))

    def call(fn, ins, outs, shapes, scratch, sem, operands):
        if n_pre:
            spec = pltpu.PrefetchScalarGridSpec(num_scalar_prefetch=n_pre, grid=grid, in_specs=ins, out_specs=outs,
                                                scratch_shapes=scratch)
            return pl.pallas_call(fn, name=name, grid_spec=spec, out_shape=shapes,
                                  compiler_params=_cp(*sem))(*prefetch, *operands)
        return pl.pallas_call(fn, name=name, grid=grid, in_specs=ins, out_specs=outs, out_shape=shapes,
                              scratch_shapes=scratch, input_output_aliases=aliases,
                              compiler_params=_cp(*sem))(*operands)

    if job is None:
        return list(call(body, list(in_specs), list(out_specs), list(out_shape), list(scratch_shapes),
                         semantics, args)), []
    n_in, n_out, n_scr = len(in_specs), len(out_specs), len(scratch_shapes)
    j_in, j_out = len(job.inputs), len(job.out_shapes())
    steps = 1
    for gsize in grid:
        steps *= gsize

    def wrapped(*refs):
        pre, refs = refs[:n_pre], refs[n_pre:]
        host_in, job_in = refs[:n_in], refs[n_in:n_in + j_in]
        o0 = n_in + j_in
        host_out, job_out = refs[o0:o0 + n_out], refs[o0 + n_out:o0 + n_out + j_out]
        s0 = o0 + n_out + j_out
        host_scr, sems = refs[s0:s0 + n_scr], refs[s0 + n_scr:]
        step = pl.program_id(0)
        for d in range(1, len(grid)):
            step = step * grid[d] + pl.program_id(d)

        @pl.when(step == 0)
        def _():
            job.start(job_in, job_out, sems)

        @pl.when(step == max(steps - 2, 0))
        def _():
            job.mid(job_in, job_out, sems)

        body(*pre, *host_in, *host_out, *host_scr)

        @pl.when(step == steps - 1)
        def _():
            job.finish(job_in, job_out, sems)

    if job.aliased:
        aliases.update({n_in + k: n_out + k for k in range(j_in)})
    outs = call(wrapped, list(in_specs) + [HBM_SPEC] * j_in, list(out_specs) + [HBM_SPEC] * j_out,
                list(out_shape) + job.out_shapes(), list(scratch_shapes) + job.sems,
                ["arbitrary"] * len(grid), [*args, *job.inputs])
    return list(outs[:n_out]), list(outs[n_out:])


def _run_job(job, name):
    return _side_call(lambda: None, job, name=name, grid=(1,), in_specs=[], out_specs=[], out_shape=[],
                      scratch_shapes=[], semantics=("arbitrary",), args=[])[1]


def _gather_job(slots):
    n = len(slots)

    def copies(buf, sems):
        ici_send, ici_recv, d2d_send, d2d_recv = sems
        x, y, c = _position()
        chips = _other_chips(x, y)

        def half(k, slot, which):
            hr = buf[k].shape[1] // 2
            return buf[k].at[slot, pl.ds(pl.multiple_of(which * hr, hr), hr)]

        def over_ici(k, p, slot):
            px, py = chips[p]
            return pltpu.make_async_remote_copy(
                src_ref=half(k, slot, c), dst_ref=half(k, slot, c),
                send_sem=ici_send.at[k * 3 + p], recv_sem=ici_recv.at[k * 3 + p],
                device_id=(px, py, c), device_id_type=MESH)

        def over_d2d(k, p, which):
            px, py = chips[p]
            return pltpu.make_async_remote_copy(
                src_ref=half(k, 2 * px + py, which), dst_ref=half(k, 2 * px + py, which),
                send_sem=d2d_send.at[k * 3 + p], recv_sem=d2d_recv.at[k * 3 + p],
                device_id=(x, y, 1 - c), device_id_type=MESH)

        return over_ici, over_d2d, 2 * x + y, chips, c

    pairs = [(k, p) for k in range(n) for p in range(3)]

    def start(_, buf, sems):
        over_ici, _, mine, _, _ = copies(buf, sems)
        for k, p in pairs:
            over_ici(k, p, mine).start()

    def mid(_, buf, sems):
        over_ici, over_d2d, _, chips, c = copies(buf, sems)
        for k, p in pairs:
            px, py = chips[p]
            over_ici(k, p, 2 * px + py).wait_recv()
            over_d2d(k, p, c).start()

    def finish(_, buf, sems):
        over_ici, over_d2d, mine, _, c = copies(buf, sems)
        for k, p in pairs:
            over_d2d(k, p, 1 - c).wait_recv()
        for k, p in pairs:
            over_ici(k, p, mine).wait_send()
            over_d2d(k, p, c).wait_send()

    return _Job(slots, True, [], [pltpu.SemaphoreType.DMA((3 * n,))] * 4, start, mid, finish)


def _gather_shards(shards):
    n = len(shards)

    def body(*refs):
        src, dst = refs[:n], refs[n:2 * n]
        send_sems, recv_sems, local_sems = refs[2 * n:]
        x, y, c = _position()
        mine = 2 * x + y
        chips = _other_chips(x, y)

        def copy(k, p):
            return pltpu.make_async_remote_copy(
                src_ref=src[k], dst_ref=dst[k].at[mine],
                send_sem=send_sems.at[k * 3 + p], recv_sem=recv_sems.at[k * 3 + p],
                device_id=(*chips[p], c), device_id_type=MESH)

        def arrival(k, p):
            px, py = chips[p]
            return pltpu.make_async_remote_copy(
                src_ref=src[k], dst_ref=dst[k].at[2 * px + py],
                send_sem=send_sems.at[k * 3 + p], recv_sem=recv_sems.at[k * 3 + p],
                device_id=(px, py, c), device_id_type=MESH)

        local = [pltpu.make_async_copy(src[k], dst[k].at[mine], local_sems.at[k]) for k in range(n)]
        for cp in local:
            cp.start()
        for k in range(n):
            for p in range(3):
                copy(k, p).start()
        for k in range(n):
            for p in range(3):
                arrival(k, p).wait_recv()
        for k in range(n):
            for p in range(3):
                copy(k, p).wait_send()
        for cp in local:
            cp.wait()

    return pl.pallas_call(
        body, name="gather_shards",
        in_specs=[HBM_SPEC] * n, out_specs=[HBM_SPEC] * n,
        out_shape=[jax.ShapeDtypeStruct((N_SHARD,) + s.shape, s.dtype) for s in shards],
        scratch_shapes=[pltpu.SemaphoreType.DMA((3 * n,)), pltpu.SemaphoreType.DMA((3 * n,)),
                        pltpu.SemaphoreType.DMA((n,))],
    )(*shards)


def _swap_job(grads):
    n = len(grads)

    def copies(src, dst, sems):
        x, y, c = _position()
        return [pltpu.make_async_remote_copy(
            src_ref=src[k].at[:, 1 - c], dst_ref=dst[k],
            send_sem=sems[0].at[k], recv_sem=sems[1].at[k],
            device_id=(x, y, 1 - c), device_id_type=MESH) for k in range(n)]

    def start(src, dst, sems):
        for cp in copies(src, dst, sems):
            cp.start()

    def finish(src, dst, sems):
        for cp in copies(src, dst, sems):
            cp.wait()

    return _Job(grads, False, [jax.ShapeDtypeStruct((N_SHARD,) + g.shape[2:], F32) for g in grads],
                [pltpu.SemaphoreType.DMA((n,))] * 2, start, lambda *_: None, finish)


def _send_job(arrays):
    n = len(arrays)

    def copies(src, dst, sems):
        x, y, c = _position()
        return [pltpu.make_async_remote_copy(
            src_ref=src[k], dst_ref=dst[k], send_sem=sems[0].at[k], recv_sem=sems[1].at[k],
            device_id=(x, y, 1 - c), device_id_type=MESH) for k in range(n)]

    def start(src, dst, sems):
        for cp in copies(src, dst, sems):
            cp.start()

    def finish(src, dst, sems):
        for cp in copies(src, dst, sems):
            cp.wait()

    return _Job(arrays, False, [jax.ShapeDtypeStruct(a.shape, a.dtype) for a in arrays],
                [pltpu.SemaphoreType.DMA((n,))] * 2, start, lambda *_: None, finish)


def _chip_partial(a, y, c, j, tag):
    _, _, R, C = a.shape
    tr = _pick(R, (256, 64))

    def body(s_ref, a_ref, y_ref, pb_ref, po_ref):
        total = a_ref[...] + y_ref[...]
        pb_ref[...] = total.astype(BF16)

        @pl.when(pl.program_id(1) == s_ref[1])
        def _():
            po_ref[...] = total

    grid_spec = pltpu.PrefetchScalarGridSpec(
        num_scalar_prefetch=1, grid=(R // tr, N_SHARD),
        in_specs=[pl.BlockSpec((None, None, tr, C), lambda t, s, sc: (s, sc[0], t, 0)),
                  pl.BlockSpec((None, tr, C), lambda t, s, sc: (s, t, 0))],
        out_specs=[pl.BlockSpec((None, tr, C), lambda t, s, sc: (s, t, 0)),
                   pl.BlockSpec((tr, C), lambda t, s, sc: (t, 0))])
    return pl.pallas_call(
        body, name=f"chip_partial_{tag}", grid_spec=grid_spec,
        out_shape=[jax.ShapeDtypeStruct((N_SHARD, R, C), BF16), jax.ShapeDtypeStruct((R, C), F32)],
        compiler_params=_cp("arbitrary", "arbitrary"),
    )(jnp.stack([c, j]).astype(jnp.int32), a, y)


def _scatter_job(parts):
    n = len(parts)
    pairs = [(k, p) for k in range(n) for p in range(3)]

    def copy(src, dst, sems, k, p, outgoing):
        x, y, c = _position()
        mine = 2 * x + y
        px, py = _other_chips(x, y)[p]
        theirs = 2 * px + py
        return pltpu.make_async_remote_copy(
            src_ref=src[k].at[theirs if outgoing else mine], dst_ref=dst[k].at[mine if outgoing else theirs],
            send_sem=sems[0].at[k * 3 + p], recv_sem=sems[1].at[k * 3 + p],
            device_id=(px, py, c), device_id_type=MESH)

    def start(src, dst, sems):
        for k, p in pairs:
            copy(src, dst, sems, k, p, True).start()

    def finish(src, dst, sems):
        for k, p in pairs:
            copy(src, dst, sems, k, p, False).wait_recv()
        for k, p in pairs:
            copy(src, dst, sems, k, p, True).wait_send()

    return _Job(parts, False, [jax.ShapeDtypeStruct(pb.shape, BF16) for pb in parts],
                [pltpu.SemaphoreType.DMA((3 * n,))] * 2, start, lambda *_: None, finish)


def _shard_total(own, z, others_c, tag):
    R, C = own.shape
    tr = _pick(R, (256, 64))

    def body(s_ref, o_ref, z0_ref, z1_ref, z2_ref, h_ref):
        h_ref[...] = ((o_ref[...] + z0_ref[...].astype(F32)) + z1_ref[...].astype(F32)) + z2_ref[...].astype(F32)

    zspec = lambda q: pl.BlockSpec((None, tr, C), lambda t, sc: (sc[q], t, 0))
    grid_spec = pltpu.PrefetchScalarGridSpec(
        num_scalar_prefetch=1, grid=(R // tr,),
        in_specs=[pl.BlockSpec((tr, C), lambda t, sc: (t, 0)), zspec(0), zspec(1), zspec(2)],
        out_specs=pl.BlockSpec((None, tr, C), lambda t, sc: (sc[3], t, 0)))
    return pl.pallas_call(
        body, name=f"shard_total_{tag}", grid_spec=grid_spec,
        out_shape=jax.ShapeDtypeStruct((2, R, C), F32),
        compiler_params=_cp("arbitrary"),
    )(others_c, own, z, z, z)


def _share_job(totals):
    n = len(totals)

    def copy(buf, sems, k, which):
        x, y, c = _position()
        return pltpu.make_async_remote_copy(
            src_ref=buf[k].at[which], dst_ref=buf[k].at[which],
            send_sem=sems[0].at[k], recv_sem=sems[1].at[k],
            device_id=(x, y, 1 - c), device_id_type=MESH)

    def start(_, buf, sems):
        c = lax.axis_index("c")
        for k in range(n):
            copy(buf, sems, k, c).start()

    def finish(_, buf, sems):
        c = lax.axis_index("c")
        for k in range(n):
            copy(buf, sems, k, 1 - c).wait_recv()
        for k in range(n):
            copy(buf, sems, k, c).wait_send()

    return _Job(totals, True, [], [pltpu.SemaphoreType.DMA((n,))] * 2, start, lambda *_: None, finish)


def _spread_job(pack):
    def copy(src, dst, sems, m, outgoing):
        x, y, c = _position()
        peer = (x ^ (m >> 2), y ^ ((m >> 1) & 1), c ^ (m & 1))
        slot = 4 * x + 2 * y + c if outgoing else 4 * peer[0] + 2 * peer[1] + peer[2]
        return pltpu.make_async_remote_copy(
            src_ref=src[0], dst_ref=dst[0].at[slot], send_sem=sems[0].at[m - 1], recv_sem=sems[1].at[m - 1],
            device_id=peer, device_id_type=MESH)

    def start(src, dst, sems):
        for m in range(1, N_DEV):
            copy(src, dst, sems, m, True).start()

    def finish(src, dst, sems):
        for m in range(1, N_DEV):
            copy(src, dst, sems, m, False).wait_recv()
        for m in range(1, N_DEV):
            copy(src, dst, sems, m, True).wait_send()

    return _Job([pack], False, [jax.ShapeDtypeStruct((N_DEV,) + pack.shape, F32)],
                [pltpu.SemaphoreType.DMA((N_DEV - 1,))] * 2, start, lambda *_: None, finish)


def _join_jobs(a, b):
    assert not a.aliased and not b.aliased
    n_in, n_out, n_sem = len(a.inputs), len(a.extra_out), len(a.sems)

    def phase(name):
        def run(ins, outs, sems):
            getattr(a, name)(ins[:n_in], outs[:n_out], sems[:n_sem])
            getattr(b, name)(ins[n_in:], outs[n_out:], sems[n_sem:])
        return run

    return _Job(a.inputs + b.inputs, False, a.extra_out + b.extra_out, a.sems + b.sems,
                phase("start"), phase("mid"), phase("finish"))


def _sum_slots(pack, slots, me, tag):
    def body(me_ref, p_ref, s_ref, o_ref):
        acc = None
        for d in range(N_DEV):
            term = jnp.where(me_ref[0] == d, p_ref[...], s_ref[d])
            acc = term if acc is None else acc + term
        o_ref[...] = acc

    vm = pl.BlockSpec(memory_space=pltpu.VMEM)
    return pl.pallas_call(
        body, name=f"sum_slots_{tag}",
        in_specs=[pl.BlockSpec(memory_space=pltpu.SMEM), vm, vm], out_specs=vm,
        out_shape=jax.ShapeDtypeStruct(pack.shape, F32),
        compiler_params=pltpu.CompilerParams(vmem_limit_bytes=V7X_VMEM_LIMIT),
    )(jnp.reshape(me, (1,)).astype(jnp.int32), pack, slots)


def _pack_rows(arrays):
    total = sum(a.size for a in arrays)
    rows = -(-total // 128)
    rows = -(-rows // PACK_ROWS_ALIGN) * PACK_ROWS_ALIGN
    flat = [a.reshape(-1) for a in arrays] + [jnp.zeros((rows * 128 - total,), F32)]
    return jnp.concatenate(flat).reshape(rows, 128)


def _adamw_math(w, g, m, v):
    m = ADAM_B1 * m + (1.0 - ADAM_B1) * g
    v = ADAM_B2 * v + (1.0 - ADAM_B2) * (g * g)
    m_hat = m / (1.0 - ADAM_B1 ** ADAM_STEP)
    v_hat = v / (1.0 - ADAM_B2 ** ADAM_STEP)
    delta = -ADAM_LR * (m_hat / (jnp.sqrt(v_hat) + ADAM_EPS) + ADAM_WD * w)
    return delta, m, v


def _adamw_big(w, g0, g1, m, v, tag):
    _, R, C = w.shape
    tr = _pick(R, (256, 128))

    def body(w_ref, g0_ref, g1_ref, m_ref, v_ref, go_ref, d_ref, mo_ref, vo_ref):
        g = jnp.where(pl.program_id(0) == 0, g0_ref[...], g1_ref[...])
        delta, mn, vn = _adamw_math(w_ref[...], g, m_ref[...], v_ref[...])
        go_ref[...] = g
        d_ref[...] = delta
        mo_ref[...] = mn
        vo_ref[...] = vn

    s3 = pl.BlockSpec((None, tr, C), lambda l, t: (l, t, 0))
    s2 = pl.BlockSpec((tr, C), lambda l, t: (t, 0))
    shp = jax.ShapeDtypeStruct(w.shape, F32)
    return pl.pallas_call(
        body, name=f"adamw_{tag}", grid=(2, R // tr),
        in_specs=[s3, s2, s2, s3, s3], out_specs=[s3, s3, s3, s3],
        out_shape=[shp, shp, shp, shp],
        compiler_params=_cp("parallel", "parallel"),
    )(w, g0, g1, m, v)


def _adamw_small(ws, gs, ms, vs):
    n = len(ws)

    def body(*refs):
        w_r, g_r, m_r, v_r = refs[:n], refs[n:2 * n], refs[2 * n:3 * n], refs[3 * n:4 * n]
        d_o, m_o, v_o = refs[4 * n:5 * n], refs[5 * n:6 * n], refs[6 * n:7 * n]
        for k in range(n):
            delta, mn, vn = _adamw_math(w_r[k][...], g_r[k][...], m_r[k][...], v_r[k][...])
            d_o[k][...] = delta
            m_o[k][...] = mn
            v_o[k][...] = vn

    vm = pl.BlockSpec(memory_space=pltpu.VMEM)
    shapes = [jax.ShapeDtypeStruct(w.shape, F32) for w in ws]
    outs = pl.pallas_call(
        body, name="adamw_small",
        in_specs=[vm] * (4 * n), out_specs=[vm] * (3 * n),
        out_shape=shapes * 3,
    )(*ws, *gs, *ms, *vs)
    return outs[:n], outs[n:2 * n], outs[2 * n:]


_WEIGHTS = ["meta_tokens", "ln_in_g", "ln_in_b", "w_in", "conv_dw_w", "conv_dw_b", "conv_ln_g", "conv_ln_b",
            "conv_pw_w", "conv_pw_b", "attn_sinks", "lru_conv_w", "lru_conv_b", "lru_wa", "lru_ba", "lru_wx",
            "lru_bx", "lru_lambda", "w_out", "ln_post_g", "ln_post_b"]
_BIG = ("w_in", "w_out", "conv_pw_w")
_SMALL_SHARDED = {"meta_tokens": 1, "conv_dw_w": 2, "lru_conv_w": 2}
PACK_ROWS_ALIGN = 8


def _as2d(a):
    return a.reshape(1, -1) if a.ndim == 1 else a.reshape(-1, a.shape[-1])


def kernel(x, meta_tokens, ln_in_g, ln_in_b, w_in, conv_dw_w, conv_dw_b, conv_ln_g, conv_ln_b, conv_pw_w, conv_pw_b, attn_sinks, lru_conv_w, lru_conv_b, lru_wa, lru_ba, lru_wx, lru_bx, lru_lambda, w_out, ln_post_g, ln_post_b, loss_target, m_meta_tokens, m_ln_in_g, m_ln_in_b, m_w_in, m_conv_dw_w, m_conv_dw_b, m_conv_ln_g, m_conv_ln_b, m_conv_pw_w, m_conv_pw_b, m_attn_sinks, m_lru_conv_w, m_lru_conv_b, m_lru_wa, m_lru_ba, m_lru_wx, m_lru_bx, m_lru_lambda, m_w_out, m_ln_post_g, m_ln_post_b, v_meta_tokens, v_ln_in_g, v_ln_in_b, v_w_in, v_conv_dw_w, v_conv_dw_b, v_conv_ln_g, v_conv_ln_b, v_conv_pw_w, v_conv_pw_b, v_attn_sinks, v_lru_conv_w, v_lru_conv_b, v_lru_wa, v_lru_ba, v_lru_wx, v_lru_bx, v_lru_lambda, v_w_out, v_ln_post_g, v_ln_post_b):
    w = dict(meta_tokens=meta_tokens, ln_in_g=ln_in_g, ln_in_b=ln_in_b, w_in=w_in, conv_dw_w=conv_dw_w,
             conv_dw_b=conv_dw_b, conv_ln_g=conv_ln_g, conv_ln_b=conv_ln_b, conv_pw_w=conv_pw_w,
             conv_pw_b=conv_pw_b, attn_sinks=attn_sinks, lru_conv_w=lru_conv_w, lru_conv_b=lru_conv_b,
             lru_wa=lru_wa, lru_ba=lru_ba, lru_wx=lru_wx, lru_bx=lru_bx, lru_lambda=lru_lambda, w_out=w_out,
             ln_post_g=ln_post_g, ln_post_b=ln_post_b)
    mom_m = dict(zip(_WEIGHTS, (m_meta_tokens, m_ln_in_g, m_ln_in_b, m_w_in, m_conv_dw_w, m_conv_dw_b, m_conv_ln_g,
                                m_conv_ln_b, m_conv_pw_w, m_conv_pw_b, m_attn_sinks, m_lru_conv_w, m_lru_conv_b,
                                m_lru_wa, m_lru_ba, m_lru_wx, m_lru_bx, m_lru_lambda, m_w_out, m_ln_post_g,
                                m_ln_post_b)))
    mom_v = dict(zip(_WEIGHTS, (v_meta_tokens, v_ln_in_g, v_ln_in_b, v_w_in, v_conv_dw_w, v_conv_dw_b, v_conv_ln_g,
                                v_conv_ln_b, v_conv_pw_w, v_conv_pw_b, v_attn_sinks, v_lru_conv_w, v_lru_conv_b,
                                v_lru_wa, v_lru_ba, v_lru_wx, v_lru_bx, v_lru_lambda, v_w_out, v_ln_post_g,
                                v_ln_post_b)))
    xi, yi, ci = _position()
    j = 2 * xi + yi

    g_meta, g_dw, g_lc = _gather_shards([meta_tokens, conv_dw_w, lru_conv_w])
    p = dict(w)
    p["w_in"] = [_cast_into_slot(w_in, l, j, "w_in") for l in range(DEPTH)]
    p["w_out"] = [_cast_into_slot(w_out, l, j, "w_out") for l in range(DEPTH)]
    p["conv_pw_w"] = [_cast_into_slot(conv_pw_w, l, j, "conv_pw_w") for l in range(DEPTH)]
    p["meta_tokens"] = g_meta.transpose(1, 0, 2).reshape(N_META, D)
    p["conv_dw_w"] = g_dw.transpose(1, 2, 0, 3).reshape(DEPTH, CONV_K, CW)
    p["lru_conv_w"] = g_lc.transpose(1, 2, 0, 3).reshape(DEPTH, LRU_K, LW)

    others = jnp.stack([jnp.where(j <= 0, 1, 0), jnp.where(j <= 1, 2, 1), jnp.where(j <= 2, 3, 2), ci]).astype(jnp.int32)
    me = 4 * xi + 2 * yi + ci
    loss_part, grad_x, g = _device_step(x[0], loss_target[0], p, dist=(ci, j, others, me))
    loss = lax.psum(jnp.sum(loss_part), ("x", "y", "c"))
    big = {(name, l): g[name, l] for name in _BIG for l in range(DEPTH)}

    small_names = [n for n in _WEIGHTS if n not in _BIG]
    small_g = {}
    for names, red in ((_SMALL_LAYERED, g["pack_layered", -1]), (_SMALL_EMBED, g["pack_embed", -1])):
        red = red.reshape(-1)
        off = 0
        for n in names:
            fshape = list(w[n].shape)
            if n in _SMALL_SHARDED:
                fshape[_SMALL_SHARDED[n]] *= N_SHARD
            sz = 1
            for dim in fshape:
                sz *= dim
            full = red[off:off + sz].reshape(fshape)
            off += sz
            if n in _SMALL_SHARDED:
                ax = _SMALL_SHARDED[n]
                full = lax.dynamic_slice_in_dim(full, j * w[n].shape[ax], w[n].shape[ax], axis=ax)
            small_g[n] = full

    out_g, out_d, out_m, out_v = {}, {}, {}, {}
    for name in _BIG:
        shp = w[name].shape
        to3 = lambda a: a.reshape(DEPTH, -1, shp[-1])
        go, do, mo, vo = _adamw_big(to3(w[name]), big[name, 0], big[name, 1], to3(mom_m[name]), to3(mom_v[name]), name)
        out_g[name], out_d[name], out_m[name], out_v[name] = (a.reshape(shp) for a in (go, do, mo, vo))
    ds, ms, vs = _adamw_small([_as2d(w[n]) for n in small_names], [_as2d(small_g[n]) for n in small_names],
                              [_as2d(mom_m[n]) for n in small_names], [_as2d(mom_v[n]) for n in small_names])
    for n, d_, m_, v_ in zip(small_names, ds, ms, vs):
        out_g[n] = small_g[n]
        out_d[n], out_m[n], out_v[n] = d_.reshape(w[n].shape), m_.reshape(w[n].shape), v_.reshape(w[n].shape)

    return (loss, grad_x[None], *[out_g[n] for n in _WEIGHTS], *[out_d[n] for n in _WEIGHTS],
            *[out_m[n] for n in _WEIGHTS], *[out_v[n] for n in _WEIGHTS])
```

```python
import functools

import jax
import jax.numpy as jnp
from jax import lax
from jax.experimental import pallas as pl
from jax.experimental.pallas import tpu as pltpu

F32 = jnp.float32
BF16 = jnp.bfloat16

D = 2048
N_META = 16
CW = 512
CONV_K = 31
AW = 1024
KVW = 256
N_HEADS = 16
LW = 512
LRU_K = 4
LRU_C = 8.0
IN_TOTAL = 5120
ROT_HALF = 8
ROPE_THETA = 500000.0
LN_EPS = 1e-5
DEPTH = 2
ALPHA = (2.0 * DEPTH) ** 0.25
NEG_INF = -1e30
ADAM_LR, ADAM_B1, ADAM_B2, ADAM_EPS, ADAM_WD, ADAM_STEP = 0.001, 0.9, 0.999, 1e-08, 0.01, 10

BLK = 128
PAD = BLK - N_META
N_SHARD = 4
WIN_SH = IN_TOTAL // N_SHARD
WOUT_SH = D // N_SHARD
PW_SH = CW // N_SHARD
HALO = 32
LHALO = 8
V7X_VMEM_LIMIT = 60 * 1024 * 1024


def _cp(*sem):
    return pltpu.CompilerParams(dimension_semantics=sem if sem else None, vmem_limit_bytes=V7X_VMEM_LIMIT)


def _pick(total, prefs):
    for p in prefs:
        if total % p == 0:
            return p
    raise ValueError(f"no tile for {total}")


def _dot(a, b):
    return jnp.dot(a, b, preferred_element_type=F32)


def _dot_nt(a, b):
    return lax.dot_general(a, b, (((1,), (1,)), ((), ())), preferred_element_type=F32)


def _dot_tn(a, b):
    return lax.dot_general(a, b, (((0,), (0,)), ((), ())), preferred_element_type=F32)


def _sigmoid(x):
    return 1.0 / (1.0 + jnp.exp(-x))


def _silu_and_grad(x):
    s = _sigmoid(x)
    return x * s, s * (1.0 + x * (1.0 - s))


def _ln_rows(x, g, b):
    mu = jnp.mean(x, axis=-1, keepdims=True)
    xc = x - mu
    var = jnp.mean(xc * xc, axis=-1, keepdims=True)
    rstd = lax.rsqrt(var + LN_EPS)
    xhat = xc * rstd
    return xhat * g + b, xhat, rstd


def _ln_bwd_rows(dy, xhat, rstd, g):
    dxh = dy * g
    m1 = jnp.mean(dxh, axis=-1, keepdims=True)
    m2 = jnp.mean(dxh * xhat, axis=-1, keepdims=True)
    return rstd * (dxh - m1 - xhat * m2)


def _row_ids(n, base):
    return base + lax.broadcasted_iota(jnp.int32, (n, 1), 0)


def _colsum(x):
    return jnp.sum(x, axis=0, keepdims=True)


def _embed_fwd(x, meta, g, b, job=None):
    S = x.shape[0]
    nb = S // BLK + 1

    def body(x_ref, meta_ref, g_ref, b_ref, h_ref, hb_ref):
        n = pl.program_id(0)

        @pl.when(n == 0)
        def _():
            y, _, _ = _ln_rows(meta_ref[...], g_ref[...], b_ref[...])
            h_ref[...] = jnp.zeros_like(h_ref)
            h_ref[PAD:BLK, :] = y

        @pl.when(n > 0)
        def _():
            y, _, _ = _ln_rows(x_ref[...], g_ref[...], b_ref[...])
            h_ref[...] = y

        hb_ref[...] = h_ref[...].astype(BF16)

    return _side_call(
        body, job, name="embed_fwd", grid=(nb,),
        in_specs=[pl.BlockSpec((BLK, D), lambda n: (jnp.maximum(n - 1, 0), 0)),
                  pl.BlockSpec((N_META, D), lambda n: (0, 0)),
                  pl.BlockSpec((1, D), lambda n: (0, 0)),
                  pl.BlockSpec((1, D), lambda n: (0, 0))],
        out_specs=[pl.BlockSpec((BLK, D), lambda n: (n, 0)),
                   pl.BlockSpec((BLK, D), lambda n: (n, 0))],
        out_shape=[jax.ShapeDtypeStruct((nb * BLK, D), F32), jax.ShapeDtypeStruct((nb * BLK, D), BF16)],
        scratch_shapes=[], semantics=("arbitrary",), args=[x, meta, g, b])


def _embed_bwd(dh, x, meta, g, b):
    S = x.shape[0]
    nb = S // BLK + 1

    def body(dh_ref, x_ref, meta_ref, g_ref, b_ref, gx_ref, gm_ref, dg_ref, db_ref):
        n = pl.program_id(0)

        @pl.when(n == 0)
        def _():
            _, xhat, rstd = _ln_rows(meta_ref[...], g_ref[...], b_ref[...])
            dy = dh_ref[PAD:BLK, :]
            gm_ref[...] = _ln_bwd_rows(dy, xhat, rstd, g_ref[...])
            dg_ref[...] = _colsum(dy * xhat)
            db_ref[...] = _colsum(dy)

        @pl.when(n > 0)
        def _():
            _, xhat, rstd = _ln_rows(x_ref[...], g_ref[...], b_ref[...])
            dy = dh_ref[...]
            gx_ref[...] = _ln_bwd_rows(dy, xhat, rstd, g_ref[...])
            dg_ref[...] += _colsum(dy * xhat)
            db_ref[...] += _colsum(dy)

    prev = lambda n: (jnp.maximum(n - 1, 0), 0)
    const = lambda n: (0, 0)
    return pl.pallas_call(
        body, name="embed_bwd", grid=(nb,),
        in_specs=[pl.BlockSpec((BLK, D), lambda n: (n, 0)),
                  pl.BlockSpec((BLK, D), prev),
                  pl.BlockSpec((N_META, D), const),
                  pl.BlockSpec((1, D), const),
                  pl.BlockSpec((1, D), const)],
        out_specs=[pl.BlockSpec((BLK, D), prev),
                   pl.BlockSpec((N_META, D), const),
                   pl.BlockSpec((1, D), const),
                   pl.BlockSpec((1, D), const)],
        out_shape=[jax.ShapeDtypeStruct((S, D), F32), jax.ShapeDtypeStruct((N_META, D), F32),
                   jax.ShapeDtypeStruct((1, D), F32), jax.ShapeDtypeStruct((1, D), F32)],
        compiler_params=_cp("arbitrary"),
    )(dh, x, meta, g, b)


def _proj_fwd(hb, w_piece, piece, partial, l, job=None):
    T = hb.shape[0]
    rows = w_piece.shape[1]
    tm = _pick(T, (1056, 384, 128))

    def body(a_ref, w_ref, *rest):
        o_ref = rest[-1]
        acc = _dot(a_ref[...], w_ref[...])
        o_ref[...] = acc if partial is None else rest[0][...] + acc

    out_spec = pl.BlockSpec((tm, WIN_SH), lambda i, j: (i, j))
    return _side_call(
        body, job, name=f"proj_fwd{l}_{piece}", grid=(T // tm, N_SHARD),
        in_specs=[pl.BlockSpec((tm, rows), lambda i, j: (i, piece)),
                  pl.BlockSpec((None, rows, WIN_SH), lambda i, j: (j, 0, 0))] + ([] if partial is None else [out_spec]),
        out_specs=[out_spec],
        out_shape=[jax.ShapeDtypeStruct((T, IN_TOTAL), F32)],
        scratch_shapes=[], semantics=("parallel", "arbitrary"),
        args=[hb, w_piece] + ([] if partial is None else [partial]),
        aliases=None if partial is None else {2: 0})


def _out_fwd(yc, ya, yl, w_out, h, g, b, l, job=None):
    T = h.shape[0]
    tm = _pick(T, (384, 128))

    def body(yc_ref, ya_ref, yl_ref, w_ref, h_ref, g_ref, b_ref, hn_ref, hnb_ref, xh_ref, rs_ref):
        acc = _dot(yc_ref[...], w_ref[0])
        acc += _dot(ya_ref[:, 0:WOUT_SH], w_ref[1])
        acc += _dot(ya_ref[:, WOUT_SH:2 * WOUT_SH], w_ref[2])
        acc += _dot(yl_ref[...], w_ref[3])
        z = ALPHA * h_ref[...] + acc
        y, xhat, rstd = _ln_rows(z, g_ref[...], b_ref[...])
        hn_ref[...] = y
        hnb_ref[...] = y.astype(BF16)
        xh_ref[...] = xhat
        rs_ref[...] = rstd

    row = lambda i: (i, 0)
    return _side_call(
        body, job, name=f"out_fwd{l}", grid=(T // tm,),
        in_specs=[pl.BlockSpec((tm, CW), row), pl.BlockSpec((tm, AW), row), pl.BlockSpec((tm, LW), row),
                  pl.BlockSpec((N_SHARD, WOUT_SH, D), lambda i: (0, 0, 0)),
                  pl.BlockSpec((tm, D), row),
                  pl.BlockSpec((None, 1, D), lambda i: (l, 0, 0)),
                  pl.BlockSpec((None, 1, D), lambda i: (l, 0, 0))],
        out_specs=[pl.BlockSpec((tm, D), row), pl.BlockSpec((tm, D), row), pl.BlockSpec((tm, D), row),
                   pl.BlockSpec((tm, 1), row)],
        out_shape=[jax.ShapeDtypeStruct((T, D), F32), jax.ShapeDtypeStruct((T, D), BF16),
                   jax.ShapeDtypeStruct((T, D), F32), jax.ShapeDtypeStruct((T, 1), F32)],
        scratch_shapes=[], semantics=("parallel",), args=[yc, ya, yl, w_out, h, g, b])


def _post_ln_dcat_bwd(src, target, xhat, rstd, g, w_out, l, job=None):
    T = src.shape[0]
    tm = _pick(T, (384, 128))
    per = tm // BLK if target is not None else 0
    last_blk = target.shape[0] // BLK - 1 if target is not None else 0

    def body(s_ref, *refs):
        t_refs = refs[:per]
        (xh_ref, rs_ref, g_ref, w_ref, part_ref, dz_ref, dzb_ref, dg_ref, db_ref, dc_ref, da_ref, dl_ref) = refs[per:]
        i = pl.program_id(0)

        @pl.when(i == 0)
        def _():
            part_ref[...] = jnp.zeros_like(part_ref)
            dg_ref[...] = jnp.zeros_like(dg_ref)
            db_ref[...] = jnp.zeros_like(db_ref)

        if per:
            tgt = jnp.concatenate([r[...] for r in t_refs], axis=0) if per > 1 else t_refs[0][...]
            real = _row_ids(tm, i * tm) >= BLK
            err = jnp.where(real, s_ref[...] - tgt, 0.0)
            part_ref[...] += _colsum(err * err) * (0.5 / D)
            dy = err * (1.0 / D)
        else:
            dy = s_ref[...]
        xhat = xh_ref[...]
        dz = _ln_bwd_rows(dy, xhat, rs_ref[...], g_ref[...])
        dzb = dz.astype(BF16)
        dz_ref[...] = dz
        dzb_ref[...] = dzb
        dg_ref[...] += _colsum(dy * xhat)
        db_ref[...] += _colsum(dy)
        dc_ref[...] = _dot_nt(dzb, w_ref[0])
        da_ref[:, 0:WOUT_SH] = _dot_nt(dzb, w_ref[1])
        da_ref[:, WOUT_SH:2 * WOUT_SH] = _dot_nt(dzb, w_ref[2])
        dl_ref[...] = _dot_nt(dzb, w_ref[3])

    row = lambda i: (i, 0)
    const = lambda i: (0, 0)
    t_specs = [pl.BlockSpec((BLK, D), functools.partial(lambda i, q: (jnp.clip(i * per - 1 + q, 0, last_blk), 0), q=q))
               for q in range(per)]
    return _side_call(
        body, job, name=f"post_ln_dcat_bwd{l}", grid=(T // tm,),
        in_specs=[pl.BlockSpec((tm, D), row)] + t_specs + [
            pl.BlockSpec((tm, D), row), pl.BlockSpec((tm, 1), row), pl.BlockSpec((None, 1, D), lambda i: (l, 0, 0)),
            pl.BlockSpec((N_SHARD, WOUT_SH, D), lambda i: (0, 0, 0))],
        out_specs=[pl.BlockSpec((1, D), const), pl.BlockSpec((tm, D), row), pl.BlockSpec((tm, D), row),
                   pl.BlockSpec((1, D), const), pl.BlockSpec((1, D), const),
                   pl.BlockSpec((tm, CW), row), pl.BlockSpec((tm, AW), row), pl.BlockSpec((tm, LW), row)],
        out_shape=[jax.ShapeDtypeStruct((1, D), F32), jax.ShapeDtypeStruct((T, D), F32),
                   jax.ShapeDtypeStruct((T, D), BF16), jax.ShapeDtypeStruct((1, D), F32),
                   jax.ShapeDtypeStruct((1, D), F32), jax.ShapeDtypeStruct((T, CW), F32),
                   jax.ShapeDtypeStruct((T, AW), F32), jax.ShapeDtypeStruct((T, LW), F32)],
        scratch_shapes=[], semantics=("arbitrary",),
        args=[src] + [target] * per + [xhat, rstd, g, w_out])


def _dwout_bwd(yc, ya, yl, dzb, l):
    T = dzb.shape[0]
    tm = _pick(T, (384, 128))

    def body(yc_ref, ya_ref, yl_ref, dz_ref, o_ref):
        @pl.when(pl.program_id(0) == 0)
        def _():
            o_ref[...] = jnp.zeros_like(o_ref)

        cat = jnp.concatenate([yc_ref[...], ya_ref[...], yl_ref[...]], axis=1)
        o_ref[...] += _dot_tn(cat, dz_ref[...])

    row = lambda t: (t, 0)
    out = pl.pallas_call(
        body, name=f"dwout_bwd{l}", grid=(T // tm,),
        in_specs=[pl.BlockSpec((tm, CW), row), pl.BlockSpec((tm, AW), row), pl.BlockSpec((tm, LW), row),
                  pl.BlockSpec((tm, D), row)],
        out_specs=pl.BlockSpec((D, D), lambda t: (0, 0)),
        out_shape=jax.ShapeDtypeStruct((D, D), F32),
        compiler_params=_cp("arbitrary"),
    )(yc, ya, yl, dzb)
    return out.reshape(N_SHARD, 2, WOUT_SH // 2, D)


def _dh_bwd(dproj, w_in, dz, l, job=None):
    T = dproj.shape[0]
    tm = _pick(T, (1056, 384, 128))

    n_w = len(w_in)

    def body(dp_ref, *refs):
        w_refs, (dz_ref, o_ref, acc_ref) = refs[:n_w], refs[n_w:]
        j = pl.program_id(1)

        @pl.when(j == 0)
        def _():
            acc_ref[...] = ALPHA * dz_ref[...]

        dp = dp_ref[...]
        off = 0
        for w_ref in w_refs:
            rows = w_ref.shape[0]
            acc_ref[:, off:off + rows] += _dot_nt(dp, w_ref[...])
            off += rows

        @pl.when(j == N_SHARD - 1)
        def _():
            o_ref[...] = acc_ref[...]

    return _side_call(
        body, job, name=f"dh_bwd{l}", grid=(T // tm, N_SHARD),
        in_specs=[pl.BlockSpec((tm, WIN_SH), lambda i, j: (i, j))]
        + [pl.BlockSpec((None, w.shape[1], WIN_SH), lambda i, j: (j, 0, 0)) for w in w_in]
        + [pl.BlockSpec((tm, D), lambda i, j: (i, 0))],
        out_specs=[pl.BlockSpec((tm, D), lambda i, j: (i, 0))],
        out_shape=[jax.ShapeDtypeStruct((T, D), F32)],
        scratch_shapes=[pltpu.VMEM((tm, D), F32)],
        semantics=("parallel", "arbitrary"), args=[dproj, *w_in, dz])


def _dwin_bwd(hb, dproj, l):
    T = hb.shape[0]
    tm = _pick(T, (1056, 384, 128))

    def body(h_ref, dp_ref, o_ref):
        @pl.when(pl.program_id(1) == 0)
        def _():
            o_ref[...] = jnp.zeros_like(o_ref)

        o_ref[...] += _dot_tn(h_ref[...], dp_ref[...])

    out = pl.pallas_call(
        body, name=f"dwin_bwd{l}", grid=(N_SHARD, T // tm),
        in_specs=[pl.BlockSpec((tm, D), lambda j, t: (t, 0)),
                  pl.BlockSpec((tm, WIN_SH), lambda j, t: (t, j))],
        out_specs=pl.BlockSpec((None, D, WIN_SH), lambda j, t: (j, 0, 0)),
        out_shape=jax.ShapeDtypeStruct((N_SHARD, D, WIN_SH), F32),
        compiler_params=_cp("parallel", "arbitrary"),
    )(hb, dproj)
    return out.reshape(N_SHARD, 2, D // 2, WIN_SH)


def _dwin_half(hb, dproj, which, l, tag, job=None):
    T = hb.shape[0]
    tm = _pick(T, (1056, 384, 128))
    hr = D // 2

    def body(w_ref, h_ref, dp_ref, o_ref):
        @pl.when(pl.program_id(1) == 0)
        def _():
            o_ref[...] = jnp.zeros_like(o_ref)

        o_ref[...] += _dot_tn(h_ref[...], dp_ref[...])

    return _side_call(
        body, job, name=f"dwin_{tag}{l}", grid=(N_SHARD, T // tm),
        in_specs=[pl.BlockSpec((tm, hr), lambda j, t, w: (t, w[0])),
                  pl.BlockSpec((tm, WIN_SH), lambda j, t, w: (t, j))],
        out_specs=[pl.BlockSpec((None, hr, WIN_SH), lambda j, t, w: (j, 0, 0))],
        out_shape=[jax.ShapeDtypeStruct((N_SHARD, hr, WIN_SH), F32)],
        scratch_shapes=[], semantics=("parallel", "arbitrary"), args=[hb, dproj],
        prefetch=[jnp.reshape(which, (1,)).astype(jnp.int32)])


def _glu_masked(v, g, base_row):
    rows = _row_ids(v.shape[0], base_row)
    return jnp.where(rows >= PAD, v * _sigmoid(g), 0.0)


def _conv_tile(T):
    return _pick(T, (384, 128))


SUBLANES = 8


def _for_each_shift(buf, rot, tm, offsets, fn):
    for r in range(SUBLANES):
        group = [o for o in offsets if o % SUBLANES == r]
        if not group:
            continue
        if r == 0:
            src = buf
        else:
            n = tm + max(group) - r
            rot[0:n, :] = buf[r:r + n, :]
            src = rot
        for o in group:
            fn(o, src[o - r:o - r + tm, :])


def _conv_fwd(proj, dw_w, dw_b, ln_g, ln_b, pw_w, pw_b, l):
    T = proj.shape[0]
    tm = _conv_tile(T)
    hb = tm // HALO

    def body(cv_ref, cg_ref, ct_ref, hv_ref, hg_ref, w_ref, b_ref, g_ref, be_ref, pw_ref, pb_ref,
             yc_ref, conv_ref, buf, rot):
        i = pl.program_id(0)
        buf[0:HALO, :] = _glu_masked(hv_ref[...], hg_ref[...], i * tm - HALO)
        buf[HALO:HALO + tm, :] = _glu_masked(cv_ref[...], cg_ref[...], i * tm)
        first = HALO - (CONV_K - 1)
        total = [jnp.zeros((tm, CW), F32) + b_ref[...]]

        def tap(o, tile):
            k = o - first
            total[0] = total[0] + w_ref[k:k + 1, :] * tile

        _for_each_shift(buf, rot, tm, [first + k for k in range(CONV_K)], tap)
        acc = total[0]
        conv_ref[...] = acc
        u, _, _ = _ln_rows(acc, g_ref[...], be_ref[...])
        s = u * _sigmoid(u)
        cpw = _dot(s.astype(BF16), pw_ref[...]) + pb_ref[...]
        gate, _ = _silu_and_grad(ct_ref[...])
        yc_ref[...] = (cpw * gate).astype(BF16)

    vec = pl.BlockSpec((None, 1, CW), lambda i: (l, 0, 0))
    return pl.pallas_call(
        body, name=f"conv_fwd{l}", grid=(T // tm,),
        in_specs=[pl.BlockSpec((tm, CW), lambda i: (i, 0)),
                  pl.BlockSpec((tm, CW), lambda i: (i, 1)),
                  pl.BlockSpec((tm, CW), lambda i: (i, 2)),
                  pl.BlockSpec((HALO, CW), lambda i: (jnp.maximum(i * hb - 1, 0), 0)),
                  pl.BlockSpec((HALO, CW), lambda i: (jnp.maximum(i * hb - 1, 0), 1)),
                  pl.BlockSpec((None, CONV_K, CW), lambda i: (l, 0, 0)),
                  vec, vec, vec,
                  pl.BlockSpec((CW, CW), lambda i: (0, 0)),
                  vec],
        out_specs=[pl.BlockSpec((tm, CW), lambda i: (i, 0)), pl.BlockSpec((tm, CW), lambda i: (i, 0))],
        out_shape=[jax.ShapeDtypeStruct((T, CW), BF16), jax.ShapeDtypeStruct((T, CW), F32)],
        scratch_shapes=[pltpu.VMEM((tm + HALO, CW), F32), pltpu.VMEM((tm + HALO, CW), F32)],
        compiler_params=_cp("parallel"),
    )(proj, proj, proj, proj, proj, dw_w, dw_b, ln_g, ln_b, pw_w, pw_b)


def _conv_bwd_rows(conv, proj, d_yc, ln_g, ln_b, pw_w, pw_b, l):
    T = conv.shape[0]
    tm = _conv_tile(T)

    def body(conv_ref, ct_ref, dy_ref, g_ref, be_ref, pw_ref, pb_ref,
             dconv_ref, dct_ref, dpw_ref, dpb_ref, dg_ref, db_ref):
        @pl.when(pl.program_id(0) == 0)
        def _():
            dpw_ref[...] = jnp.zeros_like(dpw_ref)
            dpb_ref[...] = jnp.zeros_like(dpb_ref)
            dg_ref[...] = jnp.zeros_like(dg_ref)
            db_ref[...] = jnp.zeros_like(db_ref)

        u, xhat, rstd = _ln_rows(conv_ref[...], g_ref[...], be_ref[...])
        s, ds_du = _silu_and_grad(u)
        sb = s.astype(BF16)
        cpw = _dot(sb, pw_ref[...]) + pb_ref[...]
        gate, dgate = _silu_and_grad(ct_ref[...])
        dy = dy_ref[...]
        d_cpw = dy * gate
        dct_ref[...] = (dy * cpw * dgate).astype(BF16)
        d_cpw_b = d_cpw.astype(BF16)
        dpb_ref[...] += _colsum(d_cpw)
        dpw_ref[...] += _dot_tn(sb, d_cpw_b)
        du = _dot_nt(d_cpw_b, pw_ref[...]) * ds_du
        dconv_ref[...] = _ln_bwd_rows(du, xhat, rstd, g_ref[...])
        dg_ref[...] += _colsum(du * xhat)
        db_ref[...] += _colsum(du)

    vec = pl.BlockSpec((None, 1, CW), lambda i: (l, 0, 0))
    row = lambda i: (i, 0)
    const = lambda i: (0, 0)
    return pl.pallas_call(
        body, name=f"conv_bwd_rows{l}", grid=(T // tm,),
        in_specs=[pl.BlockSpec((tm, CW), row), pl.BlockSpec((tm, CW), lambda i: (i, 2)),
                  pl.BlockSpec((tm, CW), row), vec, vec,
                  pl.BlockSpec((CW, CW), lambda i: (0, 0)), vec],
        out_specs=[pl.BlockSpec((tm, CW), row), pl.BlockSpec((tm, CW), lambda i: (i, 2)),
                   pl.BlockSpec((CW, CW), const), pl.BlockSpec((1, CW), const),
                   pl.BlockSpec((1, CW), const), pl.BlockSpec((1, CW), const)],
        out_shape=[jax.ShapeDtypeStruct((T, CW), F32), jax.ShapeDtypeStruct((T, IN_TOTAL), BF16),
                   jax.ShapeDtypeStruct((CW, CW), F32), jax.ShapeDtypeStruct((1, CW), F32),
                   jax.ShapeDtypeStruct((1, CW), F32), jax.ShapeDtypeStruct((1, CW), F32)],
        compiler_params=_cp("arbitrary"),
    )(conv, proj, d_yc, ln_g, ln_b, pw_w, pw_b)


def _conv_bwd_taps(d_conv, proj, dw_w, dproj, l, job=None):
    T = d_conv.shape[0]
    tm = _conv_tile(T)
    hb = tm // HALO
    nt = T // tm
    last_halo = T // HALO - 1

    def body(dc_ref, dh_ref, cv_ref, cg_ref, hv_ref, hg_ref, w_ref, _, o_ref, dw_ref, dwb_ref, cbuf, dbuf, rot):
        i = pl.program_id(0)

        @pl.when(i == 0)
        def _():
            dw_ref[...] = jnp.zeros_like(dw_ref)
            dwb_ref[...] = jnp.zeros_like(dwb_ref)

        cbuf[0:HALO, :] = _glu_masked(hv_ref[...], hg_ref[...], i * tm - HALO)
        cbuf[HALO:HALO + tm, :] = _glu_masked(cv_ref[...], cg_ref[...], i * tm)
        dmain = dc_ref[...]
        dbuf[0:tm, :] = dmain
        dbuf[tm:tm + HALO, :] = jnp.where(i < nt - 1, dh_ref[...], 0.0)
        total = [jnp.zeros((tm, CW), F32)]

        def tap_back(o, tile):
            k = CONV_K - 1 - o
            total[0] = total[0] + w_ref[k:k + 1, :] * tile

        _for_each_shift(dbuf, rot, tm, list(range(CONV_K)), tap_back)
        acc = total[0]
        first = HALO - (CONV_K - 1)

        def tap_weight(o, tile):
            k = o - first
            dw_ref[k:k + 1, :] += _colsum(dmain * tile)

        _for_each_shift(cbuf, rot, tm, [first + k for k in range(CONV_K)], tap_weight)
        dwb_ref[...] += _colsum(dmain)
        d_c = jnp.where(_row_ids(tm, i * tm) >= PAD, acc, 0.0)
        sig = _sigmoid(cg_ref[...])
        o_ref[:, 0:CW] = (d_c * sig).astype(BF16)
        o_ref[:, CW:2 * CW] = (d_c * cv_ref[...] * sig * (1.0 - sig)).astype(BF16)

    const = lambda i: (0, 0)
    return _side_call(
        body, job, name=f"conv_bwd_taps{l}", grid=(nt,),
        in_specs=[pl.BlockSpec((tm, CW), lambda i: (i, 0)),
                  pl.BlockSpec((HALO, CW), lambda i: (jnp.minimum((i + 1) * hb, last_halo), 0)),
                  pl.BlockSpec((tm, CW), lambda i: (i, 0)),
                  pl.BlockSpec((tm, CW), lambda i: (i, 1)),
                  pl.BlockSpec((HALO, CW), lambda i: (jnp.maximum(i * hb - 1, 0), 0)),
                  pl.BlockSpec((HALO, CW), lambda i: (jnp.maximum(i * hb - 1, 0), 1)),
                  pl.BlockSpec((None, CONV_K, CW), lambda i: (l, 0, 0)),
                  pl.BlockSpec(memory_space=pl.ANY)],
        out_specs=[pl.BlockSpec((tm, 2 * CW), lambda i: (i, 0)),
                   pl.BlockSpec((HALO, CW), const), pl.BlockSpec((1, CW), const)],
        out_shape=[jax.ShapeDtypeStruct(dproj.shape, BF16), jax.ShapeDtypeStruct((HALO, CW), F32),
                   jax.ShapeDtypeStruct((1, CW), F32)],
        scratch_shapes=[pltpu.VMEM((tm + HALO, CW), F32), pltpu.VMEM((tm + HALO, CW), F32),
                        pltpu.VMEM((tm + HALO, CW), F32)],
        semantics=("arbitrary",), aliases={7: 0},
        args=[d_conv, d_conv, proj, proj, proj, proj, dw_w, dproj])


def _log1p_small(e):
    return jnp.where(e < 1e-3, e * (1.0 - e * (0.5 - e * (1.0 / 3.0))), jnp.log(1.0 + e))


def _softplus(z):
    return jnp.maximum(z, 0.0) + _log1p_small(jnp.exp(-jnp.abs(z)))


def _neg_expm1(x):
    series = -x * (1.0 + x * (1.0 / 2.0) * (1.0 + x * (1.0 / 3.0) * (1.0 + x * (1.0 / 4.0) * (
        1.0 + x * (1.0 / 5.0) * (1.0 + x * (1.0 / 6.0) * (1.0 + x * (1.0 / 7.0)))))))
    return jnp.where(x > -0.25, series, 1.0 - jnp.exp(x))


def _lru_gates(rxbuf, tm, base_row, lw_ref, lb_ref, wa_ref, ba_ref, wx_ref, bx_ref, lam_ref):
    rc = jnp.zeros((tm, LW), F32) + lb_ref[...]
    for k in range(LRU_K):
        o = LHALO - (LRU_K - 1) + k
        rc += lw_ref[k:k + 1, :] * rxbuf[o:o + tm, :]
    rcb = rc.astype(BF16)
    r = _sigmoid(_dot(rcb, wa_ref[...]) + ba_ref[...])
    ig = _sigmoid(_dot(rcb, wx_ref[...]) + bx_ref[...])
    sp = _softplus(-lam_ref[...])
    la = -LRU_C * r * sp
    a = jnp.exp(la)
    mult = jnp.sqrt(_neg_expm1(2.0 * la))
    valid = _row_ids(tm, base_row) >= PAD
    return rc, rcb, r, ig, sp, a, mult, valid


def _mask_rows(v, base_row):
    return jnp.where(_row_ids(v.shape[0], base_row) >= PAD, v, 0.0)


def _scan_rows(aa, bb, carry, out_ref, reverse):
    tm = aa.shape[0]
    sub = _row_ids(tm, 0) & (SUBLANES - 1)
    s = 1
    while s < SUBLANES:
        keep = (sub < SUBLANES - s) if reverse else (sub >= s)
        shift = tm - s if reverse else s
        a_s = jnp.where(keep, pltpu.roll(aa, shift, axis=0), 1.0)
        b_s = jnp.where(keep, pltpu.roll(bb, shift, axis=0), 0.0)
        bb = aa * b_s + bb
        aa = aa * a_s
        s *= 2
    groups = range(tm // SUBLANES)
    edge = 0 if reverse else SUBLANES - 1
    for j in (reversed(groups) if reverse else groups):
        rows = slice(SUBLANES * j, SUBLANES * j + SUBLANES)
        x = bb[rows] + aa[rows] * carry
        out_ref[rows, :] = x
        carry = x[edge:edge + 1]


def _lru_tile(T):
    return _pick(T, (384, 128))


def _lru_fwd(proj, lw, lb, wa, ba, wx, bx, lam, l):
    T = proj.shape[0]
    tm = _lru_tile(T)
    hb = tm // LHALO

    def body(rx_ref, rg_ref, hx_ref, lw_ref, lb_ref, wa_ref, ba_ref, wx_ref, bx_ref, lam_ref,
             yl_ref, hl_ref, rxbuf, carry):
        i = pl.program_id(0)

        @pl.when(i == 0)
        def _():
            carry[...] = jnp.zeros_like(carry)

        rxbuf[0:LHALO, :] = _mask_rows(hx_ref[...], i * tm - LHALO)
        rxbuf[LHALO:LHALO + tm, :] = _mask_rows(rx_ref[...], i * tm)
        rc, _, _, ig, _, a, mult, valid = _lru_gates(rxbuf, tm, i * tm, lw_ref, lb_ref, wa_ref, ba_ref,
                                                     wx_ref, bx_ref, lam_ref)
        bb = jnp.where(valid, mult * (ig * rc), 0.0)
        _scan_rows(a, bb, carry[0:1, :], hl_ref, reverse=False)
        carry[0:1, :] = hl_ref[tm - 1:tm, :]
        gate, _ = _silu_and_grad(rg_ref[...])
        yl_ref[...] = (hl_ref[...] * gate).astype(BF16)

    vec = pl.BlockSpec((None, 1, LW), lambda i: (l, 0, 0))
    mat = pl.BlockSpec((None, LW, LW), lambda i: (l, 0, 0))
    return pl.pallas_call(
        body, name=f"lru_fwd{l}", grid=(T // tm,),
        in_specs=[pl.BlockSpec((tm, LW), lambda i: (i, 8)),
                  pl.BlockSpec((tm, LW), lambda i: (i, 9)),
                  pl.BlockSpec((LHALO, LW), lambda i: (jnp.maximum(i * hb - 1, 0), 8)),
                  pl.BlockSpec((None, LRU_K, LW), lambda i: (l, 0, 0)),
                  vec, mat, vec, mat, vec, vec],
        out_specs=[pl.BlockSpec((tm, LW), lambda i: (i, 0)), pl.BlockSpec((tm, LW), lambda i: (i, 0))],
        out_shape=[jax.ShapeDtypeStruct((T, LW), BF16), jax.ShapeDtypeStruct((T, LW), F32)],
        scratch_shapes=[pltpu.VMEM((tm + LHALO, LW), F32), pltpu.VMEM((8, LW), F32)],
        compiler_params=_cp("arbitrary"),
    )(proj, proj, proj, lw, lb, wa, ba, wx, bx, lam)


def _lru_bwd(proj, hl, d_yl, lw, lb, wa, ba, wx, bx, lam, dproj, l, job=None):
    T = proj.shape[0]
    tm = _lru_tile(T)
    hb = tm // LHALO
    nt = T // tm

    def body(rx_ref, rg_ref, hx_ref, hl_ref, hh_ref, dy_ref, lw_ref, lb_ref, wa_ref, ba_ref, wx_ref, bx_ref,
             lam_ref, _, o_ref, dlw_ref, dlb_ref, dwa_ref, dba_ref, dwx_ref, dbx_ref, dlam_ref,
             rxbuf, dbuf, carry, head, gbuf):
        step = pl.program_id(0)
        i = nt - 1 - step

        @pl.when(step == 0)
        def _():
            carry[...] = jnp.zeros_like(carry)
            head[...] = jnp.zeros_like(head)
            for ref in (dlw_ref, dlb_ref, dwa_ref, dba_ref, dwx_ref, dbx_ref, dlam_ref):
                ref[...] = jnp.zeros_like(ref)

        rxbuf[0:LHALO, :] = _mask_rows(hx_ref[...], i * tm - LHALO)
        rxbuf[LHALO:LHALO + tm, :] = _mask_rows(rx_ref[...], i * tm)
        rc, rcb, r, ig, sp, a, mult, valid = _lru_gates(rxbuf, tm, i * tm, lw_ref, lb_ref, wa_ref, ba_ref,
                                                        wx_ref, bx_ref, lam_ref)
        rows = _row_ids(tm, 0)
        h = hl_ref[...]
        h_before = jnp.where(i > 0, hh_ref[LHALO - 1:LHALO, :], 0.0)
        hprev = jnp.where(rows == 0, h_before, pltpu.roll(h, 1, axis=0))
        rg = rg_ref[...]
        gate, dgate = _silu_and_grad(rg)
        dy = dy_ref[...]
        o_ref[:, LW:2 * LW] = (dy * h * dgate).astype(BF16)
        bb = dy * gate + jnp.where(rows == tm - 1, carry[0:1, :], 0.0)
        aa = jnp.where(rows == tm - 1, 0.0, pltpu.roll(a, tm - 1, axis=0))
        _scan_rows(aa, bb, jnp.zeros((1, LW), F32), gbuf, reverse=True)
        g = gbuf[...]
        dbuf[0:tm, :] = a * g
        carry[0:1, :] = dbuf[0:1, :]
        du = jnp.where(valid, g, 0.0)
        da = g * hprev
        dix = du * mult
        dmult = du * (ig * rc)
        dla = jnp.where(valid, da * a - dmult * (a * a) / mult, 0.0)
        dr = dla * (-LRU_C * sp)
        dlam_ref[...] += _colsum(dla * (LRU_C * r)) * _sigmoid(-lam_ref[...])
        dpa = dr * r * (1.0 - r)
        dpx = (dix * rc) * ig * (1.0 - ig)
        dpab = dpa.astype(BF16)
        dpxb = dpx.astype(BF16)
        dba_ref[...] += _colsum(dpa)
        dbx_ref[...] += _colsum(dpx)
        dwa_ref[...] += _dot_tn(rcb, dpab)
        dwx_ref[...] += _dot_tn(rcb, dpxb)
        drc = dix * ig + _dot_nt(dpab, wa_ref[...]) + _dot_nt(dpxb, wx_ref[...])
        dbuf[0:tm, :] = drc
        dbuf[tm:tm + LHALO, :] = head[...]
        acc = jnp.zeros((tm, LW), F32)
        for k in range(LRU_K):
            o = LRU_K - 1 - k
            acc += lw_ref[k:k + 1, :] * dbuf[o:o + tm, :]
            oc = LHALO - (LRU_K - 1) + k
            dlw_ref[k:k + 1, :] += _colsum(drc * rxbuf[oc:oc + tm, :])
        dlb_ref[...] += _colsum(drc)
        head[...] = dbuf[0:LHALO, :]
        o_ref[:, 0:LW] = jnp.where(valid, acc, 0.0).astype(BF16)

    rev = lambda s: nt - 1 - s
    vec = pl.BlockSpec((None, 1, LW), lambda s: (l, 0, 0))
    mat = pl.BlockSpec((None, LW, LW), lambda s: (l, 0, 0))
    const = lambda s: (0, 0)
    halo = lambda s: jnp.maximum(rev(s) * hb - 1, 0)
    return _side_call(
        body, job, name=f"lru_bwd{l}", grid=(nt,),
        in_specs=[pl.BlockSpec((tm, LW), lambda s: (rev(s), 8)),
                  pl.BlockSpec((tm, LW), lambda s: (rev(s), 9)),
                  pl.BlockSpec((LHALO, LW), lambda s: (halo(s), 8)),
                  pl.BlockSpec((tm, LW), lambda s: (rev(s), 0)),
                  pl.BlockSpec((LHALO, LW), lambda s: (halo(s), 0)),
                  pl.BlockSpec((tm, LW), lambda s: (rev(s), 0)),
                  pl.BlockSpec((None, LRU_K, LW), lambda s: (l, 0, 0)),
                  vec, mat, vec, mat, vec, vec, pl.BlockSpec(memory_space=pl.ANY)],
        out_specs=[pl.BlockSpec((tm, 2 * LW), lambda s: (rev(s), 4)),
                   pl.BlockSpec((8, LW), const), pl.BlockSpec((1, LW), const),
                   pl.BlockSpec((LW, LW), const), pl.BlockSpec((1, LW), const),
                   pl.BlockSpec((LW, LW), const), pl.BlockSpec((1, LW), const),
                   pl.BlockSpec((1, LW), const)],
        out_shape=[jax.ShapeDtypeStruct(dproj.shape, BF16),
                   jax.ShapeDtypeStruct((8, LW), F32), jax.ShapeDtypeStruct((1, LW), F32),
                   jax.ShapeDtypeStruct((LW, LW), F32), jax.ShapeDtypeStruct((1, LW), F32),
                   jax.ShapeDtypeStruct((LW, LW), F32), jax.ShapeDtypeStruct((1, LW), F32),
                   jax.ShapeDtypeStruct((1, LW), F32)],
        scratch_shapes=[pltpu.VMEM((tm + LHALO, LW), F32), pltpu.VMEM((tm + LHALO, LW), F32),
                        pltpu.VMEM((8, LW), F32), pltpu.VMEM((LHALO, LW), F32), pltpu.VMEM((tm, LW), F32)],
        semantics=("arbitrary",), aliases={13: 0},
        args=[proj, proj, proj, hl, hl, d_yl, lw, lb, wa, ba, wx, bx, lam, dproj])


def _rope_tables(T):
    pos = (lax.broadcasted_iota(jnp.int32, (T, 128), 0) - PAD).astype(F32)
    lane = lax.broadcasted_iota(jnp.int32, (T, 128), 1) % 64
    inv_freq = ROPE_THETA ** (-(lane % ROT_HALF).astype(F32) / ROT_HALF)
    ang = pos * inv_freq
    cos, sin = jnp.cos(ang), jnp.sin(ang)
    c = jnp.where(lane < 2 * ROT_HALF, cos, 1.0)
    s1 = jnp.where(lane < ROT_HALF, -sin, 0.0)
    s2 = jnp.where((lane >= ROT_HALF) & (lane < 2 * ROT_HALF), sin, 0.0)
    return c, s1, s2


def _rot_fwd(x, c, s1, s2):
    return x * c + pltpu.roll(x, 128 - ROT_HALF, axis=1) * s1 + pltpu.roll(x, ROT_HALF, axis=1) * s2


def _rot_bwd(dy, c, s1, s2):
    return dy * c + pltpu.roll(dy * s1, ROT_HALF, axis=1) + pltpu.roll(dy * s2, 128 - ROT_HALF, axis=1)


def _rope_fwd(proj, tabs, l):
    T = proj.shape[0]

    def body(ql_ref, qh_ref, k_ref, v_ref, c_ref, s1_ref, s2_ref, qr_ref, kr_ref, vb_ref):
        c, s1, s2 = c_ref[...], s1_ref[...], s2_ref[...]
        for gcol in range(AW // 128):
            src = ql_ref if gcol < 4 else qh_ref
            x = src[:, 128 * (gcol % 4):128 * (gcol % 4) + 128]
            qr_ref[:, 128 * gcol:128 * gcol + 128] = (_rot_fwd(x, c, s1, s2) * 0.125).astype(BF16)
        for gcol in range(KVW // 128):
            x = k_ref[:, 128 * gcol:128 * gcol + 128]
            kr_ref[:, 128 * gcol:128 * gcol + 128] = _rot_fwd(x, c, s1, s2).astype(BF16)
        vb_ref[...] = v_ref[...].astype(BF16)

    tr = _pick(T, (384, 128))
    tab = pl.BlockSpec((tr, 128), lambda n: (n, 0))
    return pl.pallas_call(
        body, name=f"rope_fwd{l}", grid=(T // tr,),
        in_specs=[pl.BlockSpec((tr, 512), lambda n: (n, 3)), pl.BlockSpec((tr, 512), lambda n: (n, 4)),
                  pl.BlockSpec((tr, KVW), lambda n: (n, 10)), pl.BlockSpec((tr, KVW), lambda n: (n, 11)),
                  tab, tab, tab],
        out_specs=[pl.BlockSpec((tr, AW), lambda n: (n, 0)), pl.BlockSpec((tr, KVW), lambda n: (n, 0)),
                   pl.BlockSpec((tr, KVW), lambda n: (n, 0))],
        out_shape=[jax.ShapeDtypeStruct((T, AW), BF16), jax.ShapeDtypeStruct((T, KVW), BF16),
                   jax.ShapeDtypeStruct((T, KVW), BF16)],
        compiler_params=_cp("parallel"),
    )(proj, proj, proj, proj, *tabs)


GROUP = 4


def _attn_mask(n, reps):
    qi = lax.broadcasted_iota(jnp.int32, (reps * BLK, BLK), 0) & (BLK - 1)
    kj = lax.broadcasted_iota(jnp.int32, (reps * BLK, BLK), 1)
    m0 = (kj >= PAD) & (n >= 1)
    mp = (kj > qi) & (n >= 2)
    mc = (kj <= qi) & ((n >= 1) | (kj >= PAD))
    return jnp.concatenate([m0, mp, mc], axis=1)


def _kv_both(x0_ref, xp_ref, xc_ref, g):
    pg, off = g // 2, g % 2
    cols = slice(128 * pg, 128 * pg + 128)
    x = jnp.concatenate([x0_ref[:, cols], xp_ref[:, cols], xc_ref[:, cols]], axis=0).astype(F32)
    lane = lax.broadcasted_iota(jnp.int32, (1, 128), 1)
    half = jnp.where((lane < 64) if off == 0 else (lane >= 64), x, 0.0)
    return (half + pltpu.roll(half, 64, axis=1)).astype(BF16)


def _kv_halves(x0_ref, xp_ref, xc_ref, g):
    pg, off = g // 2, g % 2
    cols = slice(128 * pg, 128 * pg + 128)
    x = jnp.concatenate([x0_ref[:, cols], xp_ref[:, cols], xc_ref[:, cols]], axis=0).astype(F32)
    lane = lax.broadcasted_iota(jnp.int32, (1, 128), 1)
    if off == 0:
        lo = jnp.where(lane < 64, x, 0.0)
        hi = pltpu.roll(lo, 64, axis=1)
    else:
        hi = jnp.where(lane >= 64, x, 0.0)
        lo = pltpu.roll(hi, 64, axis=1)
    return lo.astype(BF16), hi.astype(BF16)


def _stack_heads(a, b):
    lo = lax.broadcasted_iota(jnp.int32, (1, 128), 1) < 64
    a, b = a.astype(F32), b.astype(F32)
    return jnp.concatenate([jnp.where(lo, a, 0.0), jnp.where(lo, 0.0, a),
                            jnp.where(lo, b, 0.0), jnp.where(lo, 0.0, b)], axis=0).astype(BF16)


def _unstack_heads(x):
    lo = lax.broadcasted_iota(jnp.int32, (1, 128), 1) < 64
    return (jnp.where(lo, x[0:BLK], x[BLK:2 * BLK]), jnp.where(lo, x[2 * BLK:3 * BLK], x[3 * BLK:4 * BLK]))


def _per_head_column(values):
    return jnp.concatenate([jnp.zeros((BLK, 1), F32) + v for v in values], axis=0)


def _attn_fwd(qr, kr, vb, proj, sinks, l, job=None):
    T = qr.shape[0]

    def body(sink_ref, q_ref, k0_ref, kp_ref, kc_ref, v0_ref, vp_ref, vc_ref, ag_ref, ya_ref, att_ref, lse_ref):
        n = pl.program_id(0)
        mask = _attn_mask(n, 1)
        lane = lax.broadcasted_iota(jnp.int32, (1, 128), 1)
        lse_acc = jnp.zeros((BLK, 128), F32)
        for g in range(4):
            k_lo, k_hi = _kv_halves(k0_ref, kp_ref, kc_ref, g)
            v_lo, v_hi = _kv_halves(v0_ref, vp_ref, vc_ref, g)
            for pp in range(2):
                cols = slice(128 * (2 * g + pp), 128 * (2 * g + pp) + 128)
                qpair = q_ref[:, cols]
                out = jnp.zeros((BLK, 128), F32)
                for hh, (kx, vx) in enumerate(((k_lo, v_lo), (k_hi, v_hi))):
                    h = 4 * g + 2 * pp + hh
                    sink = sink_ref[l, h]
                    s = jnp.where(mask, _dot_nt(qpair, kx), NEG_INF)
                    m = jnp.maximum(jnp.max(s, axis=1, keepdims=True), sink)
                    p = jnp.exp(s - m)
                    denom = jnp.sum(p, axis=1, keepdims=True) + jnp.exp(sink - m)
                    out += _dot((p * (1.0 / denom)).astype(BF16), vx)
                    lse_acc = jnp.where(lane == h, m + jnp.log(denom), lse_acc)
                att_ref[:, cols] = out
                gate, _ = _silu_and_grad(ag_ref[:, cols])
                ya_ref[:, cols] = (out * gate).astype(BF16)
        lse_ref[...] = lse_acc

    prev = lambda n: (jnp.maximum(n - 1, 0), 0)
    cur = lambda n: (n, 0)
    zero = lambda n: (0, 0)
    kv = lambda f: pl.BlockSpec((BLK, KVW), f)
    return _side_call(
        body, job, name=f"attn_fwd{l}", grid=(T // BLK,),
        in_specs=[pl.BlockSpec(memory_space=pltpu.SMEM),
                  pl.BlockSpec((BLK, AW), cur), kv(zero), kv(prev), kv(cur), kv(zero), kv(prev), kv(cur),
                  pl.BlockSpec((BLK, AW), lambda n: (n, 3))],
        out_specs=[pl.BlockSpec((BLK, AW), cur), pl.BlockSpec((BLK, AW), cur), pl.BlockSpec((BLK, 128), cur)],
        out_shape=[jax.ShapeDtypeStruct((T, AW), BF16), jax.ShapeDtypeStruct((T, AW), F32),
                   jax.ShapeDtypeStruct((T, 128), F32)],
        scratch_shapes=[], semantics=("parallel",), args=[sinks, qr, kr, kr, kr, vb, vb, vb, proj])


def _attn_bwd(qr, kr, vb, proj, att, lse, d_ya, sinks, dproj, l, job=None):
    T = qr.shape[0]
    nb = T // BLK

    def body(sink_ref, q_ref, k0_ref, kp_ref, kc_ref, v0_ref, vp_ref, vc_ref, ag_ref, att_ref, lse_ref, dy_ref, _,
             dq_ref, dk_ref, dv_ref, dk0_ref, dv0_ref, dag_ref, dsink_ref, kcarry, vcarry):
        n = pl.program_id(0)

        @pl.when(n == 0)
        def _():
            dk0_ref[...] = jnp.zeros_like(dk0_ref)
            dv0_ref[...] = jnp.zeros_like(dv0_ref)
            dsink_ref[...] = jnp.zeros_like(dsink_ref)
            kcarry[...] = jnp.zeros_like(kcarry)
            vcarry[...] = jnp.zeros_like(vcarry)

        @pl.when(n == nb)
        def _():
            dk_ref[...] = kcarry[...]
            dv_ref[...] = vcarry[...]

        @pl.when(n < nb)
        def _():
            mask = _attn_mask(n, GROUP)
            lane = lax.broadcasted_iota(jnp.int32, (1, 128), 1)
            lse = lse_ref[...]
            dsink = jnp.zeros((1, 128), F32)
            dk_pg, dv_pg = [], []
            for pg in range(2):
                dk_acc = jnp.zeros((3 * BLK, 128), F32)
                dv_acc = jnp.zeros((3 * BLK, 128), F32)
                for off in range(2):
                    g = 2 * pg + off
                    kx = _kv_both(k0_ref, kp_ref, kc_ref, g)
                    vx = _kv_both(v0_ref, vp_ref, vc_ref, g)
                    pair_cols = [slice(128 * (2 * g + pp), 128 * (2 * g + pp) + 128) for pp in range(2)]
                    q4 = _stack_heads(q_ref[:, pair_cols[0]], q_ref[:, pair_cols[1]])
                    d_out = []
                    for cols in pair_cols:
                        gate, dgate = _silu_and_grad(ag_ref[:, cols])
                        dy = dy_ref[:, cols]
                        dag_ref[:, cols] = (dy * att_ref[:, cols] * dgate).astype(BF16)
                        d_out.append(dy * gate)
                    do4 = _stack_heads(d_out[0], d_out[1])
                    heads = [GROUP * g + r for r in range(GROUP)]
                    sink = _per_head_column([sink_ref[l, h] for h in heads])
                    lse4 = _per_head_column(
                        [jnp.sum(jnp.where(lane == h, lse, 0.0), axis=1, keepdims=True) for h in heads])
                    p = jnp.where(mask, jnp.exp(_dot_nt(q4, kx) - lse4), 0.0)
                    dp = _dot_nt(do4, vx)
                    delta = jnp.sum(p * dp, axis=1, keepdims=True)
                    ds = (p * (dp - delta)).astype(BF16)
                    sink_term = jnp.exp(sink - lse4) * delta
                    for r, h in enumerate(heads):
                        dsink += jnp.where(lane == h, -jnp.sum(sink_term[BLK * r:BLK * r + BLK]), 0.0)
                    for cols, dq in zip(pair_cols, _unstack_heads(_dot(ds, kx))):
                        dq_ref[:, cols] = dq
                    dkg = _dot_tn(ds, q4)
                    dvg = _dot_tn(p.astype(BF16), do4)
                    own = (lane < 64) if off == 0 else (lane >= 64)
                    dk_acc += jnp.where(own, dkg + pltpu.roll(dkg, 64, axis=1), 0.0)
                    dv_acc += jnp.where(own, dvg + pltpu.roll(dvg, 64, axis=1), 0.0)
                dk_pg.append(dk_acc)
                dv_pg.append(dv_acc)
            dsink_ref[...] += dsink
            for pg in range(2):
                cols = slice(128 * pg, 128 * pg + 128)
                dk0_ref[:, cols] += dk_pg[pg][0:BLK]
                dv0_ref[:, cols] += dv_pg[pg][0:BLK]
                dk_ref[:, cols] = kcarry[:, cols] + dk_pg[pg][BLK:2 * BLK]
                dv_ref[:, cols] = vcarry[:, cols] + dv_pg[pg][BLK:2 * BLK]
                kcarry[:, cols] = dk_pg[pg][2 * BLK:3 * BLK]
                vcarry[:, cols] = dv_pg[pg][2 * BLK:3 * BLK]

    last = nb - 1
    cur = lambda n: (jnp.minimum(n, last), 0)
    prev = lambda n: (jnp.clip(n - 1, 0, last), 0)
    zero = lambda n: (0, 0)
    kv = lambda f: pl.BlockSpec((BLK, KVW), f)
    wide = lambda f: pl.BlockSpec((BLK, AW), f)
    return _side_call(
        body, job, name=f"attn_bwd{l}", grid=(nb + 1,),
        in_specs=[pl.BlockSpec(memory_space=pltpu.SMEM),
                  wide(cur), kv(zero), kv(prev), kv(cur), kv(zero), kv(prev), kv(cur),
                  pl.BlockSpec((BLK, AW), lambda n: (jnp.minimum(n, last), 3)),
                  wide(cur), pl.BlockSpec((BLK, 128), cur), wide(cur), pl.BlockSpec(memory_space=pl.ANY)],
        out_specs=[wide(cur), kv(prev), kv(prev), kv(zero), kv(zero),
                   pl.BlockSpec((BLK, AW), lambda n: (jnp.minimum(n, last), 3)),
                   pl.BlockSpec((1, 128), zero)],
        out_shape=[jax.ShapeDtypeStruct((T, AW), F32), jax.ShapeDtypeStruct((T, KVW), F32),
                   jax.ShapeDtypeStruct((T, KVW), F32), jax.ShapeDtypeStruct((BLK, KVW), F32),
                   jax.ShapeDtypeStruct((BLK, KVW), F32), jax.ShapeDtypeStruct(dproj.shape, BF16),
                   jax.ShapeDtypeStruct((1, 128), F32)],
        scratch_shapes=[pltpu.VMEM((BLK, KVW), F32), pltpu.VMEM((BLK, KVW), F32)],
        semantics=("arbitrary",), aliases={12: 5},
        args=[sinks, qr, kr, kr, kr, vb, vb, vb, proj, att, lse, d_ya, dproj])


def _rope_bwd(dqr, dk, dv, dk0, dv0, tabs, dproj, l):
    T = dqr.shape[0]

    def body(dq_ref, dk_ref, dv_ref, dk0_ref, dv0_ref, c_ref, s1_ref, s2_ref, _, o_ref):
        n = pl.program_id(0)
        c, s1, s2 = c_ref[...], s1_ref[...], s2_ref[...]
        for gcol in range(AW // 128):
            cols = slice(128 * gcol, 128 * gcol + 128)
            o_ref[:, cols] = (_rot_bwd(dq_ref[:, cols], c, s1, s2) * 0.125).astype(BF16)
        for gcol in range(KVW // 128):
            cols = slice(128 * gcol, 128 * gcol + 128)
            kcols = slice(AW + 128 * gcol, AW + 128 * gcol + 128)
            vcols = slice(AW + KVW + 128 * gcol, AW + KVW + 128 * gcol + 128)
            o_ref[:, kcols] = _rot_bwd(dk_ref[:, cols], c, s1, s2).astype(BF16)
            o_ref[:, vcols] = dv_ref[:, cols].astype(BF16)

            @pl.when(n == 0)
            def _():
                dkk = dk_ref[0:BLK, cols] + dk0_ref[:, cols]
                o_ref[0:BLK, kcols] = _rot_bwd(dkk, c[0:BLK], s1[0:BLK], s2[0:BLK]).astype(BF16)
                o_ref[0:BLK, vcols] = (dv_ref[0:BLK, cols] + dv0_ref[:, cols]).astype(BF16)

    tr = _pick(T, (384, 128))
    cur = lambda n: (n, 0)
    zero = lambda n: (0, 0)
    tab = pl.BlockSpec((tr, 128), cur)
    return pl.pallas_call(
        body, name=f"rope_bwd{l}", grid=(T // tr,),
        in_specs=[pl.BlockSpec((tr, AW), cur), pl.BlockSpec((tr, KVW), cur), pl.BlockSpec((tr, KVW), cur),
                  pl.BlockSpec((BLK, KVW), zero), pl.BlockSpec((BLK, KVW), zero), tab, tab, tab,
                  pl.BlockSpec(memory_space=pl.ANY)],
        out_specs=pl.BlockSpec((tr, AW + 2 * KVW), lambda n: (n, 1)),
        out_shape=jax.ShapeDtypeStruct(dproj.shape, BF16),
        input_output_aliases={8: 0},
        compiler_params=_cp("parallel"),
    )(dqr, dk, dv, dk0, dv0, *tabs, dproj)


def _block_diag(w):
    nl, nh, hd, _ = w.shape
    eye = jnp.eye(nh, dtype=w.dtype)
    return jnp.einsum("lhij,hg->lhigj", w, eye).reshape(nl, nh * hd, nh * hd)


def _diag_blocks(m):
    nh, hd = 8, 64
    return jnp.einsum("hihj->hij", m.reshape(nh, hd, nh, hd))


def _device_step(x, target, p, dist=None):
    vec = lambda a: a.reshape(DEPTH, 1, a.shape[-1])
    ln_in_g, ln_in_b = p["ln_in_g"].reshape(1, D), p["ln_in_b"].reshape(1, D)
    conv_dw_b, conv_ln_g, conv_ln_b, conv_pw_b = map(vec, (p["conv_dw_b"], p["conv_ln_g"], p["conv_ln_b"], p["conv_pw_b"]))
    lru_conv_b, lru_ba, lru_bx, lru_lambda = map(vec, (p["lru_conv_b"], p["lru_ba"], p["lru_bx"], p["lru_lambda"]))
    ln_post_g, ln_post_b = vec(p["ln_post_g"]), vec(p["ln_post_b"])
    wa_bd = _block_diag(p["lru_wa"]).astype(BF16)
    wx_bd = _block_diag(p["lru_wx"]).astype(BF16)
    w_in, w_out, pw_w = list(p["w_in"]), list(p["w_out"]), list(p["conv_pw_w"])
    sinks = p["attn_sinks"]
    big_names = ("w_in", "w_out", "conv_pw_w")

    w_in = [list(pieces) for pieces in w_in]
    (h, hb), got = _embed_fwd(x, p["meta_tokens"], ln_in_g, ln_in_b,
                              job=_gather_job([w_in[0][0]]) if dist else None)
    if dist:
        w_in[0][0] = got[0]
    T = h.shape[0]
    tabs = _rope_tables(T)
    saved = []
    for l in range(DEPTH):
        proj = None
        for k in range(len(w_in[l])):
            job = None
            if dist and l == 0:
                rest = w_in[0][k + 1:k + 2]
                job = _gather_job(rest + [pw_w[0]] if k == 0 else rest + [w_out[0]] if k == 1 else rest)
            (proj,), got = _proj_fwd(hb, w_in[l][k], k, proj, l, job=job)
            if job is not None:
                if rest:
                    w_in[0][k + 1] = got[0]
                if k == 0:
                    pw_w[0] = got[-1]
                if k == 1:
                    w_out[0] = got[-1]
        pw_l = pw_w[l].reshape(CW, CW)
        yc, conv = _conv_fwd(proj, p["conv_dw_w"], conv_dw_b, conv_ln_g, conv_ln_b, pw_l, conv_pw_b, l)
        qr, kr, vb = _rope_fwd(proj, tabs, l)
        (ya, att, lse), got = _attn_fwd(
            qr, kr, vb, proj, sinks, l, job=_gather_job(w_in[1]) if dist and l == 0 else None)
        if got:
            w_in[1] = list(got)
        yl, hl = _lru_fwd(proj, p["lru_conv_w"], lru_conv_b, wa_bd, lru_ba, wx_bd, lru_bx, lru_lambda, l)
        (hn, hnb, xhat, rstd), got = _out_fwd(
            yc, ya, yl, w_out[l], h, ln_post_g, ln_post_b, l,
            job=_gather_job([w_out[1], pw_w[1]]) if dist and l == 0 else None)
        if got:
            w_out[1], pw_w[1] = got
        saved.append((hb, proj, yc, conv, qr, kr, vb, ya, att, lse, yl, hl, xhat, rstd, pw_l))
        h, hb = hn, hnb

    dh = None
    g = {}
    later = None
    early, last = ("w_out", "conv_pw_w"), ("w_in",)
    own = {}
    for l in reversed(range(DEPTH)):
        hb_l, proj, yc, conv, qr, kr, vb, ya, att, lse, yl, hl, xhat, rstd, pw_l = saved[l]
        tail = dist is not None and l == 0
        top = l == DEPTH - 1
        (part, dz, dzb, g["ln_post_g", l], g["ln_post_b", l], d_yc, d_ya, d_yl), recv = _post_ln_dcat_bwd(
            h if top else dh, target if top else None, xhat, rstd, ln_post_g, w_out[l], l,
            job=_swap_job(later["grads"]) if later else None)
        if top:
            loss_part = part
        if later:
            later["parts"], later["owns"] = _chip_partials(big_names, later["grads"], recv, dist, later["l"])
        g["w_out", l] = _dwout_bwd(yc, ya, yl, dzb, l)
        d_conv, dproj, dpw, g["conv_pw_b", l], g["conv_ln_g", l], g["conv_ln_b", l] = _conv_bwd_rows(
            conv, proj, d_yc, conv_ln_g, conv_ln_b, pw_l, conv_pw_b, l)
        g["conv_pw_w", l] = dpw.reshape(N_SHARD, 2, PW_SH // 2, CW)
        if tail:
            own["early"] = dict(l=0, grads=[g[name, 0] for name in early])
        job = None
        if tail:
            job = _join_jobs(_swap_job(own["early"]["grads"]), _scatter_job(later["parts"][1:]))
        (dproj, ddw, g["conv_dw_b", l]), got = _conv_bwd_taps(d_conv, proj, p["conv_dw_w"], dproj, l, job=job)
        if tail:
            n_early = len(early)
            own["early"]["parts"], own["early"]["owns"] = _chip_partials(
                early, own["early"]["grads"], got[:n_early], dist, 0)
            later["z"] = got[n_early:]
        g["conv_dw_w", l] = ddw[:CONV_K]
        (dqr, dk, dv, dk0, dv0, dproj, dsink), z = _attn_bwd(
            qr, kr, vb, proj, att, lse, d_ya, sinks, dproj, l,
            job=_scatter_job(later["parts"][:1]) if later else None)
        if later:
            later["z"] = z + later["z"]
        g["attn_sinks", l] = dsink[0, :N_HEADS]
        dproj = _rope_bwd(dqr, dk, dv, dk0, dv0, tabs, dproj, l)
        (dproj, dlw, g["lru_conv_b", l], dwa, g["lru_ba", l], dwx, g["lru_bx", l], g["lru_lambda", l]), z = _lru_bwd(
            proj, hl, d_yl, p["lru_conv_w"], lru_conv_b, wa_bd, lru_ba, wx_bd, lru_bx, lru_lambda, dproj, l,
            job=_scatter_job(own["early"]["parts"]) if tail else None)
        if tail:
            own["early"]["z"] = z
        g["lru_conv_w", l] = dlw[:LRU_K]
        g["lru_wa", l] = _diag_blocks(dwa)
        g["lru_wx", l] = _diag_blocks(dwx)
        job = None
        if l > 0:
            g["w_in", l] = _dwin_bwd(hb_l, dproj, l)
        else:
            c = dist[0] if dist else jnp.int32(0)
            pack_a = _pack_rows([_layer_stack(g, name) for name in _SMALL_LAYERED]) if dist else None
            (give,), slots_a = _dwin_half(hb_l, dproj, 1 - c, l, "give", job=_spread_job(pack_a) if dist else None)
            (keep,), recv = _dwin_half(hb_l, dproj, c, l, "keep", job=_send_job([give]) if dist else None)
            if dist:
                g["pack_layered", -1] = _sum_slots(pack_a, slots_a[0], dist[3], "layered")
                own["last"] = dict(l=0)
                own["last"]["parts"], own["last"]["owns"] = _chip_partials(
                    last, [keep.reshape(N_SHARD, 1, D // 2, WIN_SH)], recv, (jnp.int32(0),) + tuple(dist[1:]), 0)
                job = _scatter_job(own["last"]["parts"])
            else:
                g["w_in", l] = jnp.stack([keep, give], axis=1)
        (dh,), got = _dh_bwd(dproj, w_in[l], dz, l, job=job)
        if tail:
            own["last"]["z"] = got
        if later:
            _finish_reduce(big_names, later, dist, g)
            later = None
        if dist and l > 0:
            later = dict(l=l, grads=[g[name, l] for name in big_names])
    grad_x, g["meta_tokens", -1], g["ln_in_g", -1], g["ln_in_b", -1] = _embed_bwd(
        dh, x, p["meta_tokens"], ln_in_g, ln_in_b)
    if dist:
        pack_b = _pack_rows([g[name, -1] for name in _SMALL_EMBED])
        slots_b = _run_job(_spread_job(pack_b), "spread_embed")[0]
        g["pack_embed", -1] = _sum_slots(pack_b, slots_b, dist[3], "embed")
        state = dict(l=0, owns=own["last"]["owns"] + own["early"]["owns"], z=own["last"]["z"] + own["early"]["z"])
        _finish_reduce(last + early, state, dist, g)
    return loss_part, grad_x, g


_SMALL_EMBED = ("meta_tokens", "ln_in_g", "ln_in_b")
_SMALL_LAYERED = ("conv_dw_w", "conv_dw_b", "conv_ln_g", "conv_ln_b", "conv_pw_b", "attn_sinks", "lru_conv_w",
                  "lru_conv_b", "lru_wa", "lru_ba", "lru_wx", "lru_bx", "lru_lambda", "ln_post_g", "ln_post_b")


def _layer_stack(g, name):
    return jnp.stack([g[name, l] for l in range(DEPTH)], axis=0)


def _chip_partials(names, grads, recv, dist, l):
    outs = [_chip_partial(a, r, dist[0], dist[1], f"{name}{l}") for name, a, r in zip(names, grads, recv)]
    return [o[0] for o in outs], [o[1] for o in outs]


def _finish_reduce(names, state, dist, g):
    l = state["l"]
    totals = [_shard_total(po, zz, dist[2], f"{name}{l}") for name, po, zz in zip(names, state["owns"], state["z"])]
    full = _run_job(_share_job(totals), f"share_halves{l}")
    for name, f in zip(names, full):
        g[name, l] = f.reshape(2 * f.shape[1], f.shape[2])


MESH = pl.DeviceIdType.MESH
HBM_SPEC = pl.BlockSpec(memory_space=pltpu.HBM)
N_DEV = 8


def _position():
    x, y, c = lax.axis_index("x"), lax.axis_index("y"), lax.axis_index("c")
    return x, y, c


def _other_chips(x, y):
    return [(1 - x, y), (x, 1 - y), (1 - x, 1 - y)]


def _cast_into_slot(a, l, j, tag, piece=0, pieces=1):
    _, R, C = a.shape
    rows = R // pieces
    tb = _pick(rows, (512, 128))
    first = piece * rows // tb

    def body(s_ref, a_ref, o_ref):
        o_ref[...] = a_ref[...].astype(BF16)

    grid_spec = pltpu.PrefetchScalarGridSpec(
        num_scalar_prefetch=1, grid=(rows // tb,),
        in_specs=[pl.BlockSpec((None, tb, C), lambda t, sc: (l, first + t, 0))],
        out_specs=pl.BlockSpec((None, tb, C), lambda t, sc: (sc[0], t, 0)))
    return pl.pallas_call(
        body, name=f"cast_into_slot_{tag}{l}_{piece}", grid_spec=grid_spec,
        out_shape=jax.ShapeDtypeStruct((N_SHARD, rows, C), BF16),
        compiler_params=_cp("arbitrary"),
    )(jnp.reshape(j, (1,)).astype(jnp.int32), a)


class _Job:
    def __init__(self, inputs, aliased, extra_out, sems, start, mid, finish):
        self.inputs, self.aliased, self.extra_out, self.sems = list(inputs), aliased, list(extra_out), list(sems)
        self.start, self.mid, self.finish = start, mid, finish

    def out_shapes(self):
        own = [jax.ShapeDtypeStruct(a.shape, a.dtype) for a in self.inputs] if self.aliased else []
        return own + self.extra_out


def _side_call(body, job, *, name, grid, in_specs, out_specs, out_shape, scratch_shapes, semantics, args,
               aliases=None, prefetch=()):
    aliases = dict(aliases or {})
    n_pre = len(prefetch)
    assert not (n_pre and (aliases or (job is not None and job.aliased)))

    def call(fn, ins, outs, shapes, scratch, sem, operands):
        if n_pre:
            spec = pltpu.PrefetchScalarGridSpec(num_scalar_prefetch=n_pre, grid=grid, in_specs=ins, out_specs=outs,
                                                scratch_shapes=scratch)
            return pl.pallas_call(fn, name=name, grid_spec=spec, out_shape=shapes,
                                  compiler_params=_cp(*sem))(*prefetch, *operands)
        return pl.pallas_call(fn, name=name, grid=grid, in_specs=ins, out_specs=outs, out_shape=shapes,
                              scratch_shapes=scratch, input_output_aliases=aliases,
                              compiler_params=_cp(*sem))(*operands)

    if job is None:
        return list(call(body, list(in_specs), list(out_specs), list(out_shape), list(scratch_shapes),
                         semantics, args)), []
    n_in, n_out, n_scr = len(in_specs), len(out_specs), len(scratch_shapes)
    j_in, j_out = len(job.inputs), len(job.out_shapes())
    steps = 1
    for gsize in grid:
        steps *= gsize

    def wrapped(*refs):
        pre, refs = refs[:n_pre], refs[n_pre:]
        host_in, job_in = refs[:n_in], refs[n_in:n_in + j_in]
        o0 = n_in + j_in
        host_out, job_out = refs[o0:o0 + n_out], refs[o0 + n_out:o0 + n_out + j_out]
        s0 = o0 + n_out + j_out
        host_scr, sems = refs[s0:s0 + n_scr], refs[s0 + n_scr:]
        step = pl.program_id(0)
        for d in range(1, len(grid)):
            step = step * grid[d] + pl.program_id(d)

        @pl.when(step == 0)
        def _():
            job.start(job_in, job_out, sems)

        @pl.when(step == max(steps - 2, 0))
        def _():
            job.mid(job_in, job_out, sems)

        body(*pre, *host_in, *host_out, *host_scr)

        @pl.when(step == steps - 1)
        def _():
            job.finish(job_in, job_out, sems)

    if job.aliased:
        aliases.update({n_in + k: n_out + k for k in range(j_in)})
    outs = call(wrapped, list(in_specs) + [HBM_SPEC] * j_in, list(out_specs) + [HBM_SPEC] * j_out,
                list(out_shape) + job.out_shapes(), list(scratch_shapes) + job.sems,
                ["arbitrary"] * len(grid), [*args, *job.inputs])
    return list(outs[:n_out]), list(outs[n_out:])


def _run_job(job, name):
    return _side_call(lambda: None, job, name=name, grid=(1,), in_specs=[], out_specs=[], out_shape=[],
                      scratch_shapes=[], semantics=("arbitrary",), args=[])[1]


def _gather_job(slots):
    n = len(slots)

    def copies(buf, sems):
        ici_send, ici_recv, d2d_send, d2d_recv = sems
        x, y, c = _position()
        chips = _other_chips(x, y)

        def half(k, slot, which):
            hr = buf[k].shape[1] // 2
            return buf[k].at[slot, pl.ds(pl.multiple_of(which * hr, hr), hr)]

        def over_ici(k, p, slot):
            px, py = chips[p]
            return pltpu.make_async_remote_copy(
                src_ref=half(k, slot, c), dst_ref=half(k, slot, c),
                send_sem=ici_send.at[k * 3 + p], recv_sem=ici_recv.at[k * 3 + p],
                device_id=(px, py, c), device_id_type=MESH)

        def over_d2d(k, p, which):
            px, py = chips[p]
            return pltpu.make_async_remote_copy(
                src_ref=half(k, 2 * px + py, which), dst_ref=half(k, 2 * px + py, which),
                send_sem=d2d_send.at[k * 3 + p], recv_sem=d2d_recv.at[k * 3 + p],
                device_id=(x, y, 1 - c), device_id_type=MESH)

        return over_ici, over_d2d, 2 * x + y, chips, c

    pairs = [(k, p) for k in range(n) for p in range(3)]

    def start(_, buf, sems):
        over_ici, _, mine, _, _ = copies(buf, sems)
        for k, p in pairs:
            over_ici(k, p, mine).start()

    def mid(_, buf, sems):
        over_ici, over_d2d, _, chips, c = copies(buf, sems)
        for k, p in pairs:
            px, py = chips[p]
            over_ici(k, p, 2 * px + py).wait_recv()
            over_d2d(k, p, c).start()

    def finish(_, buf, sems):
        over_ici, over_d2d, mine, _, c = copies(buf, sems)
        for k, p in pairs:
            over_d2d(k, p, 1 - c).wait_recv()
        for k, p in pairs:
            over_ici(k, p, mine).wait_send()
            over_d2d(k, p, c).wait_send()

    return _Job(slots, True, [], [pltpu.SemaphoreType.DMA((3 * n,))] * 4, start, mid, finish)


def _gather_shards(shards):
    n = len(shards)

    def body(*refs):
        src, dst = refs[:n], refs[n:2 * n]
        send_sems, recv_sems, local_sems = refs[2 * n:]
        x, y, c = _position()
        mine = 2 * x + y
        chips = _other_chips(x, y)

        def copy(k, p):
            return pltpu.make_async_remote_copy(
                src_ref=src[k], dst_ref=dst[k].at[mine],
                send_sem=send_sems.at[k * 3 + p], recv_sem=recv_sems.at[k * 3 + p],
                device_id=(*chips[p], c), device_id_type=MESH)

        def arrival(k, p):
            px, py = chips[p]
            return pltpu.make_async_remote_copy(
                src_ref=src[k], dst_ref=dst[k].at[2 * px + py],
                send_sem=send_sems.at[k * 3 + p], recv_sem=recv_sems.at[k * 3 + p],
                device_id=(px, py, c), device_id_type=MESH)

        local = [pltpu.make_async_copy(src[k], dst[k].at[mine], local_sems.at[k]) for k in range(n)]
        for cp in local:
            cp.start()
        for k in range(n):
            for p in range(3):
                copy(k, p).start()
        for k in range(n):
            for p in range(3):
                arrival(k, p).wait_recv()
        for k in range(n):
            for p in range(3):
                copy(k, p).wait_send()
        for cp in local:
            cp.wait()

    return pl.pallas_call(
        body, name="gather_shards",
        in_specs=[HBM_SPEC] * n, out_specs=[HBM_SPEC] * n,
        out_shape=[jax.ShapeDtypeStruct((N_SHARD,) + s.shape, s.dtype) for s in shards],
        scratch_shapes=[pltpu.SemaphoreType.DMA((3 * n,)), pltpu.SemaphoreType.DMA((3 * n,)),
                        pltpu.SemaphoreType.DMA((n,))],
    )(*shards)


def _swap_job(grads):
    n = len(grads)

    def copies(src, dst, sems):
        x, y, c = _position()
        return [pltpu.make_async_remote_copy(
            src_ref=src[k].at[:, 1 - c], dst_ref=dst[k],
            send_sem=sems[0].at[k], recv_sem=sems[1].at[k],
            device_id=(x, y, 1 - c), device_id_type=MESH) for k in range(n)]

    def start(src, dst, sems):
        for cp in copies(src, dst, sems):
            cp.start()

    def finish(src, dst, sems):
        for cp in copies(src, dst, sems):
            cp.wait()

    return _Job(grads, False, [jax.ShapeDtypeStruct((N_SHARD,) + g.shape[2:], F32) for g in grads],
                [pltpu.SemaphoreType.DMA((n,))] * 2, start, lambda *_: None, finish)


def _send_job(arrays):
    n = len(arrays)

    def copies(src, dst, sems):
        x, y, c = _position()
        return [pltpu.make_async_remote_copy(
            src_ref=src[k], dst_ref=dst[k], send_sem=sems[0].at[k], recv_sem=sems[1].at[k],
            device_id=(x, y, 1 - c), device_id_type=MESH) for k in range(n)]

    def start(src, dst, sems):
        for cp in copies(src, dst, sems):
            cp.start()

    def finish(src, dst, sems):
        for cp in copies(src, dst, sems):
            cp.wait()

    return _Job(arrays, False, [jax.ShapeDtypeStruct(a.shape, a.dtype) for a in arrays],
                [pltpu.SemaphoreType.DMA((n,))] * 2, start, lambda *_: None, finish)


def _chip_partial(a, y, c, j, tag):
    _, _, R, C = a.shape
    tr = _pick(R, (256, 64))

    def body(s_ref, a_ref, y_ref, pb_ref, po_ref):
        total = a_ref[...] + y_ref[...]
        pb_ref[...] = total.astype(BF16)

        @pl.when(pl.program_id(1) == s_ref[1])
        def _():
            po_ref[...] = total

    grid_spec = pltpu.PrefetchScalarGridSpec(
        num_scalar_prefetch=1, grid=(R // tr, N_SHARD),
        in_specs=[pl.BlockSpec((None, None, tr, C), lambda t, s, sc: (s, sc[0], t, 0)),
                  pl.BlockSpec((None, tr, C), lambda t, s, sc: (s, t, 0))],
        out_specs=[pl.BlockSpec((None, tr, C), lambda t, s, sc: (s, t, 0)),
                   pl.BlockSpec((tr, C), lambda t, s, sc: (t, 0))])
    return pl.pallas_call(
        body, name=f"chip_partial_{tag}", grid_spec=grid_spec,
        out_shape=[jax.ShapeDtypeStruct((N_SHARD, R, C), BF16), jax.ShapeDtypeStruct((R, C), F32)],
        compiler_params=_cp("arbitrary", "arbitrary"),
    )(jnp.stack([c, j]).astype(jnp.int32), a, y)


def _scatter_job(parts):
    n = len(parts)
    pairs = [(k, p) for k in range(n) for p in range(3)]

    def copy(src, dst, sems, k, p, outgoing):
        x, y, c = _position()
        mine = 2 * x + y
        px, py = _other_chips(x, y)[p]
        theirs = 2 * px + py
        return pltpu.make_async_remote_copy(
            src_ref=src[k].at[theirs if outgoing else mine], dst_ref=dst[k].at[mine if outgoing else theirs],
            send_sem=sems[0].at[k * 3 + p], recv_sem=sems[1].at[k * 3 + p],
            device_id=(px, py, c), device_id_type=MESH)

    def start(src, dst, sems):
        for k, p in pairs:
            copy(src, dst, sems, k, p, True).start()

    def finish(src, dst, sems):
        for k, p in pairs:
            copy(src, dst, sems, k, p, False).wait_recv()
        for k, p in pairs:
            copy(src, dst, sems, k, p, True).wait_send()

    return _Job(parts, False, [jax.ShapeDtypeStruct(pb.shape, BF16) for pb in parts],
                [pltpu.SemaphoreType.DMA((3 * n,))] * 2, start, lambda *_: None, finish)


def _shard_total(own, z, others_c, tag):
    R, C = own.shape
    tr = _pick(R, (256, 64))

    def body(s_ref, o_ref, z0_ref, z1_ref, z2_ref, h_ref):
        h_ref[...] = ((o_ref[...] + z0_ref[...].astype(F32)) + z1_ref[...].astype(F32)) + z2_ref[...].astype(F32)

    zspec = lambda q: pl.BlockSpec((None, tr, C), lambda t, sc: (sc[q], t, 0))
    grid_spec = pltpu.PrefetchScalarGridSpec(
        num_scalar_prefetch=1, grid=(R // tr,),
        in_specs=[pl.BlockSpec((tr, C), lambda t, sc: (t, 0)), zspec(0), zspec(1), zspec(2)],
        out_specs=pl.BlockSpec((None, tr, C), lambda t, sc: (sc[3], t, 0)))
    return pl.pallas_call(
        body, name=f"shard_total_{tag}", grid_spec=grid_spec,
        out_shape=jax.ShapeDtypeStruct((2, R, C), F32),
        compiler_params=_cp("arbitrary"),
    )(others_c, own, z, z, z)


def _share_job(totals):
    n = len(totals)

    def copy(buf, sems, k, which):
        x, y, c = _position()
        return pltpu.make_async_remote_copy(
            src_ref=buf[k].at[which], dst_ref=buf[k].at[which],
            send_sem=sems[0].at[k], recv_sem=sems[1].at[k],
            device_id=(x, y, 1 - c), device_id_type=MESH)

    def start(_, buf, sems):
        c = lax.axis_index("c")
        for k in range(n):
            copy(buf, sems, k, c).start()

    def finish(_, buf, sems):
        c = lax.axis_index("c")
        for k in range(n):
            copy(buf, sems, k, 1 - c).wait_recv()
        for k in range(n):
            copy(buf, sems, k, c).wait_send()

    return _Job(totals, True, [], [pltpu.SemaphoreType.DMA((n,))] * 2, start, lambda *_: None, finish)


def _spread_job(pack):
    def copy(src, dst, sems, m, outgoing):
        x, y, c = _position()
        peer = (x ^ (m >> 2), y ^ ((m >> 1) & 1), c ^ (m & 1))
        slot = 4 * x + 2 * y + c if outgoing else 4 * peer[0] + 2 * peer[1] + peer[2]
        return pltpu.make_async_remote_copy(
            src_ref=src[0], dst_ref=dst[0].at[slot], send_sem=sems[0].at[m - 1], recv_sem=sems[1].at[m - 1],
            device_id=peer, device_id_type=MESH)

    def start(src, dst, sems):
        for m in range(1, N_DEV):
            copy(src, dst, sems, m, True).start()

    def finish(src, dst, sems):
        for m in range(1, N_DEV):
            copy(src, dst, sems, m, False).wait_recv()
        for m in range(1, N_DEV):
            copy(src, dst, sems, m, True).wait_send()

    return _Job([pack], False, [jax.ShapeDtypeStruct((N_DEV,) + pack.shape, F32)],
                [pltpu.SemaphoreType.DMA((N_DEV - 1,))] * 2, start, lambda *_: None, finish)


def _join_jobs(a, b):
    assert not a.aliased and not b.aliased
    n_in, n_out, n_sem = len(a.inputs), len(a.extra_out), len(a.sems)

    def phase(name):
        def run(ins, outs, sems):
            getattr(a, name)(ins[:n_in], outs[:n_out], sems[:n_sem])
            getattr(b, name)(ins[n_in:], outs[n_out:], sems[n_sem:])
        return run

    return _Job(a.inputs + b.inputs, False, a.extra_out + b.extra_out, a.sems + b.sems,
                phase("start"), phase("mid"), phase("finish"))


def _sum_slots(pack, slots, me, tag):
    def body(me_ref, p_ref, s_ref, o_ref):
        acc = None
        for d in range(N_DEV):
            term = jnp.where(me_ref[0] == d, p_ref[...], s_ref[d])
            acc = term if acc is None else acc + term
        o_ref[...] = acc

    vm = pl.BlockSpec(memory_space=pltpu.VMEM)
    return pl.pallas_call(
        body, name=f"sum_slots_{tag}",
        in_specs=[pl.BlockSpec(memory_space=pltpu.SMEM), vm, vm], out_specs=vm,
        out_shape=jax.ShapeDtypeStruct(pack.shape, F32),
        compiler_params=pltpu.CompilerParams(vmem_limit_bytes=V7X_VMEM_LIMIT),
    )(jnp.reshape(me, (1,)).astype(jnp.int32), pack, slots)


def _pack_rows(arrays):
    total = sum(a.size for a in arrays)
    rows = -(-total // 128)
    rows = -(-rows // PACK_ROWS_ALIGN) * PACK_ROWS_ALIGN
    flat = [a.reshape(-1) for a in arrays] + [jnp.zeros((rows * 128 - total,), F32)]
    return jnp.concatenate(flat).reshape(rows, 128)


def _adamw_math(w, g, m, v):
    m = ADAM_B1 * m + (1.0 - ADAM_B1) * g
    v = ADAM_B2 * v + (1.0 - ADAM_B2) * (g * g)
    m_hat = m / (1.0 - ADAM_B1 ** ADAM_STEP)
    v_hat = v / (1.0 - ADAM_B2 ** ADAM_STEP)
    delta = -ADAM_LR * (m_hat / (jnp.sqrt(v_hat) + ADAM_EPS) + ADAM_WD * w)
    return delta, m, v


def _adamw_big(w, g0, g1, m, v, tag):
    _, R, C = w.shape
    tr = _pick(R, (256, 128))

    def body(w_ref, g0_ref, g1_ref, m_ref, v_ref, go_ref, d_ref, mo_ref, vo_ref):
        g = jnp.where(pl.program_id(0) == 0, g0_ref[...], g1_ref[...])
        delta, mn, vn = _adamw_math(w_ref[...], g, m_ref[...], v_ref[...])
        go_ref[...] = g
        d_ref[...] = delta
        mo_ref[...] = mn
        vo_ref[...] = vn

    s3 = pl.BlockSpec((None, tr, C), lambda l, t: (l, t, 0))
    s2 = pl.BlockSpec((tr, C), lambda l, t: (t, 0))
    shp = jax.ShapeDtypeStruct(w.shape, F32)
    return pl.pallas_call(
        body, name=f"adamw_{tag}", grid=(2, R // tr),
        in_specs=[s3, s2, s2, s3, s3], out_specs=[s3, s3, s3, s3],
        out_shape=[shp, shp, shp, shp],
        compiler_params=_cp("parallel", "parallel"),
    )(w, g0, g1, m, v)


def _adamw_small(ws, gs, ms, vs):
    n = len(ws)

    def body(*refs):
        w_r, g_r, m_r, v_r = refs[:n], refs[n:2 * n], refs[2 * n:3 * n], refs[3 * n:4 * n]
        d_o, m_o, v_o = refs[4 * n:5 * n], refs[5 * n:6 * n], refs[6 * n:7 * n]
        for k in range(n):
            delta, mn, vn = _adamw_math(w_r[k][...], g_r[k][...], m_r[k][...], v_r[k][...])
            d_o[k][...] = delta
            m_o[k][...] = mn
            v_o[k][...] = vn

    vm = pl.BlockSpec(memory_space=pltpu.VMEM)
    shapes = [jax.ShapeDtypeStruct(w.shape, F32) for w in ws]
    outs = pl.pallas_call(
        body, name="adamw_small",
        in_specs=[vm] * (4 * n), out_specs=[vm] * (3 * n),
        out_shape=shapes * 3,
    )(*ws, *gs, *ms, *vs)
    return outs[:n], outs[n:2 * n], outs[2 * n:]


_WEIGHTS = ["meta_tokens", "ln_in_g", "ln_in_b", "w_in", "conv_dw_w", "conv_dw_b", "conv_ln_g", "conv_ln_b",
            "conv_pw_w", "conv_pw_b", "attn_sinks", "lru_conv_w", "lru_conv_b", "lru_wa", "lru_ba", "lru_wx",
            "lru_bx", "lru_lambda", "w_out", "ln_post_g", "ln_post_b"]
_BIG = ("w_in", "w_out", "conv_pw_w")
_SMALL_SHARDED = {"meta_tokens": 1, "conv_dw_w": 2, "lru_conv_w": 2}
PACK_ROWS_ALIGN = 8


def _as2d(a):
    return a.reshape(1, -1) if a.ndim == 1 else a.reshape(-1, a.shape[-1])


def kernel(x, meta_tokens, ln_in_g, ln_in_b, w_in, conv_dw_w, conv_dw_b, conv_ln_g, conv_ln_b, conv_pw_w, conv_pw_b, attn_sinks, lru_conv_w, lru_conv_b, lru_wa, lru_ba, lru_wx, lru_bx, lru_lambda, w_out, ln_post_g, ln_post_b, loss_target, m_meta_tokens, m_ln_in_g, m_ln_in_b, m_w_in, m_conv_dw_w, m_conv_dw_b, m_conv_ln_g, m_conv_ln_b, m_conv_pw_w, m_conv_pw_b, m_attn_sinks, m_lru_conv_w, m_lru_conv_b, m_lru_wa, m_lru_ba, m_lru_wx, m_lru_bx, m_lru_lambda, m_w_out, m_ln_post_g, m_ln_post_b, v_meta_tokens, v_ln_in_g, v_ln_in_b, v_w_in, v_conv_dw_w, v_conv_dw_b, v_conv_ln_g, v_conv_ln_b, v_conv_pw_w, v_conv_pw_b, v_attn_sinks, v_lru_conv_w, v_lru_conv_b, v_lru_wa, v_lru_ba, v_lru_wx, v_lru_bx, v_lru_lambda, v_w_out, v_ln_post_g, v_ln_post_b):
    w = dict(meta_tokens=meta_tokens, ln_in_g=ln_in_g, ln_in_b=ln_in_b, w_in=w_in, conv_dw_w=conv_dw_w,
             conv_dw_b=conv_dw_b, conv_ln_g=conv_ln_g, conv_ln_b=conv_ln_b, conv_pw_w=conv_pw_w,
             conv_pw_b=conv_pw_b, attn_sinks=attn_sinks, lru_conv_w=lru_conv_w, lru_conv_b=lru_conv_b,
             lru_wa=lru_wa, lru_ba=lru_ba, lru_wx=lru_wx, lru_bx=lru_bx, lru_lambda=lru_lambda, w_out=w_out,
             ln_post_g=ln_post_g, ln_post_b=ln_post_b)
    mom_m = dict(zip(_WEIGHTS, (m_meta_tokens, m_ln_in_g, m_ln_in_b, m_w_in, m_conv_dw_w, m_conv_dw_b, m_conv_ln_g,
                                m_conv_ln_b, m_conv_pw_w, m_conv_pw_b, m_attn_sinks, m_lru_conv_w, m_lru_conv_b,
                                m_lru_wa, m_lru_ba, m_lru_wx, m_lru_bx, m_lru_lambda, m_w_out, m_ln_post_g,
                                m_ln_post_b)))
    mom_v = dict(zip(_WEIGHTS, (v_meta_tokens, v_ln_in_g, v_ln_in_b, v_w_in, v_conv_dw_w, v_conv_dw_b, v_conv_ln_g,
                                v_conv_ln_b, v_conv_pw_w, v_conv_pw_b, v_attn_sinks, v_lru_conv_w, v_lru_conv_b,
                                v_lru_wa, v_lru_ba, v_lru_wx, v_lru_bx, v_lru_lambda, v_w_out, v_ln_post_g,
                                v_ln_post_b)))
    xi, yi, ci = _position()
    j = 2 * xi + yi

    g_meta, g_dw, g_lc = _gather_shards([meta_tokens, conv_dw_w, lru_conv_w])
    p = dict(w)
    p["w_in"] = [[_cast_into_slot(w_in, 0, j, "w_in", piece, 2) for piece in range(2)],
                 [_cast_into_slot(w_in, 1, j, "w_in")]]
    p["w_out"] = [_cast_into_slot(w_out, l, j, "w_out") for l in range(DEPTH)]
    p["conv_pw_w"] = [_cast_into_slot(conv_pw_w, l, j, "conv_pw_w") for l in range(DEPTH)]
    p["meta_tokens"] = g_meta.transpose(1, 0, 2).reshape(N_META, D)
    p["conv_dw_w"] = g_dw.transpose(1, 2, 0, 3).reshape(DEPTH, CONV_K, CW)
    p["lru_conv_w"] = g_lc.transpose(1, 2, 0, 3).reshape(DEPTH, LRU_K, LW)

    others = jnp.stack([jnp.where(j <= 0, 1, 0), jnp.where(j <= 1, 2, 1), jnp.where(j <= 2, 3, 2), ci]).astype(jnp.int32)
    me = 4 * xi + 2 * yi + ci
    loss_part, grad_x, g = _device_step(x[0], loss_target[0], p, dist=(ci, j, others, me))
    loss = lax.psum(jnp.sum(loss_part), ("x", "y", "c"))
    big = {(name, l): g[name, l] for name in _BIG for l in range(DEPTH)}

    small_names = [n for n in _WEIGHTS if n not in _BIG]
    small_g = {}
    for names, red in ((_SMALL_LAYERED, g["pack_layered", -1]), (_SMALL_EMBED, g["pack_embed", -1])):
        red = red.reshape(-1)
        off = 0
        for n in names:
            fshape = list(w[n].shape)
            if n in _SMALL_SHARDED:
                fshape[_SMALL_SHARDED[n]] *= N_SHARD
            sz = 1
            for dim in fshape:
                sz *= dim
            full = red[off:off + sz].reshape(fshape)
            off += sz
            if n in _SMALL_SHARDED:
                ax = _SMALL_SHARDED[n]
                full = lax.dynamic_slice_in_dim(full, j * w[n].shape[ax], w[n].shape[ax], axis=ax)
            small_g[n] = full

    out_g, out_d, out_m, out_v = {}, {}, {}, {}
    for name in _BIG:
        shp = w[name].shape
        to3 = lambda a: a.reshape(DEPTH, -1, shp[-1])
        go, do, mo, vo = _adamw_big(to3(w[name]), big[name, 0], big[name, 1], to3(mom_m[name]), to3(mom_v[name]), name)
        out_g[name], out_d[name], out_m[name], out_v[name] = (a.reshape(shp) for a in (go, do, mo, vo))
    ds, ms, vs = _adamw_small([_as2d(w[n]) for n in small_names], [_as2d(small_g[n]) for n in small_names],
                              [_as2d(mom_m[n]) for n in small_names], [_as2d(mom_v[n]) for n in small_names])
    for n, d_, m_, v_ in zip(small_names, ds, ms, vs):
        out_g[n] = small_g[n]
        out_d[n], out_m[n], out_v[n] = d_.reshape(w[n].shape), m_.reshape(w[n].shape), v_.reshape(w[n].shape)

    return (loss, grad_x[None], *[out_g[n] for n in _WEIGHTS], *[out_d[n] for n in _WEIGHTS],
            *[out_m[n] for n in _WEIGHTS], *[out_v[n] for n in _WEIGHTS])
```

```python
import functools

import jax
import jax.numpy as jnp
from jax import lax
from jax.experimental import pallas as pl
from jax.experimental.pallas import tpu as pltpu

F32 = jnp.float32
BF16 = jnp.bfloat16

D = 2048
N_META = 16
CW = 512
CONV_K = 31
AW = 1024
KVW = 256
N_HEADS = 16
LW = 512
LRU_K = 4
LRU_C = 8.0
IN_TOTAL = 5120
ROT_HALF = 8
ROPE_THETA = 500000.0
LN_EPS = 1e-5
DEPTH = 2
ALPHA = (2.0 * DEPTH) ** 0.25
NEG_INF = -1e30
ADAM_LR, ADAM_B1, ADAM_B2, ADAM_EPS, ADAM_WD, ADAM_STEP = 0.001, 0.9, 0.999, 1e-08, 0.01, 10

BLK = 128
PAD = BLK - N_META
N_SHARD = 4
WIN_SH = IN_TOTAL // N_SHARD
WOUT_SH = D // N_SHARD
PW_SH = CW // N_SHARD
HALO = 32
LHALO = 8
V7X_VMEM_LIMIT = 60 * 1024 * 1024


def _cp(*sem):
    return pltpu.CompilerParams(dimension_semantics=sem if sem else None, vmem_limit_bytes=V7X_VMEM_LIMIT)


def _pick(total, prefs):
    for p in prefs:
        if total % p == 0:
            return p
    raise ValueError(f"no tile for {total}")


def _dot(a, b):
    return jnp.dot(a, b, preferred_element_type=F32)


def _dot_nt(a, b):
    return lax.dot_general(a, b, (((1,), (1,)), ((), ())), preferred_element_type=F32)


def _dot_tn(a, b):
    return lax.dot_general(a, b, (((0,), (0,)), ((), ())), preferred_element_type=F32)


def _sigmoid(x):
    return 1.0 / (1.0 + jnp.exp(-x))


def _silu_and_grad(x):
    s = _sigmoid(x)
    return x * s, s * (1.0 + x * (1.0 - s))


def _ln_rows(x, g, b):
    mu = jnp.mean(x, axis=-1, keepdims=True)
    xc = x - mu
    var = jnp.mean(xc * xc, axis=-1, keepdims=True)
    rstd = lax.rsqrt(var + LN_EPS)
    xhat = xc * rstd
    return xhat * g + b, xhat, rstd


def _ln_bwd_rows(dy, xhat, rstd, g):
    dxh = dy * g
    m1 = jnp.mean(dxh, axis=-1, keepdims=True)
    m2 = jnp.mean(dxh * xhat, axis=-1, keepdims=True)
    return rstd * (dxh - m1 - xhat * m2)


def _row_ids(n, base):
    return base + lax.broadcasted_iota(jnp.int32, (n, 1), 0)


def _colsum(x):
    return jnp.sum(x, axis=0, keepdims=True)


def _embed_fwd(x, meta, g, b, job=None):
    S = x.shape[0]
    nb = S // BLK + 1

    def body(x_ref, meta_ref, g_ref, b_ref, h_ref, hb_ref):
        n = pl.program_id(0)

        @pl.when(n == 0)
        def _():
            y, _, _ = _ln_rows(meta_ref[...], g_ref[...], b_ref[...])
            h_ref[...] = jnp.zeros_like(h_ref)
            h_ref[PAD:BLK, :] = y

        @pl.when(n > 0)
        def _():
            y, _, _ = _ln_rows(x_ref[...], g_ref[...], b_ref[...])
            h_ref[...] = y

        hb_ref[...] = h_ref[...].astype(BF16)

    return _side_call(
        body, job, name="embed_fwd", grid=(nb,),
        in_specs=[pl.BlockSpec((BLK, D), lambda n: (jnp.maximum(n - 1, 0), 0)),
                  pl.BlockSpec((N_META, D), lambda n: (0, 0)),
                  pl.BlockSpec((1, D), lambda n: (0, 0)),
                  pl.BlockSpec((1, D), lambda n: (0, 0))],
        out_specs=[pl.BlockSpec((BLK, D), lambda n: (n, 0)),
                   pl.BlockSpec((BLK, D), lambda n: (n, 0))],
        out_shape=[jax.ShapeDtypeStruct((nb * BLK, D), F32), jax.ShapeDtypeStruct((nb * BLK, D), BF16)],
        scratch_shapes=[], semantics=("arbitrary",), args=[x, meta, g, b])


def _embed_bwd(dh, x, meta, g, b):
    S = x.shape[0]
    nb = S // BLK + 1

    def body(dh_ref, x_ref, meta_ref, g_ref, b_ref, gx_ref, gm_ref, dg_ref, db_ref):
        n = pl.program_id(0)

        @pl.when(n == 0)
        def _():
            _, xhat, rstd = _ln_rows(meta_ref[...], g_ref[...], b_ref[...])
            dy = dh_ref[PAD:BLK, :]
            gm_ref[...] = _ln_bwd_rows(dy, xhat, rstd, g_ref[...])
            dg_ref[...] = _colsum(dy * xhat)
            db_ref[...] = _colsum(dy)

        @pl.when(n > 0)
        def _():
            _, xhat, rstd = _ln_rows(x_ref[...], g_ref[...], b_ref[...])
            dy = dh_ref[...]
            gx_ref[...] = _ln_bwd_rows(dy, xhat, rstd, g_ref[...])
            dg_ref[...] += _colsum(dy * xhat)
            db_ref[...] += _colsum(dy)

    prev = lambda n: (jnp.maximum(n - 1, 0), 0)
    const = lambda n: (0, 0)
    return pl.pallas_call(
        body, name="embed_bwd", grid=(nb,),
        in_specs=[pl.BlockSpec((BLK, D), lambda n: (n, 0)),
                  pl.BlockSpec((BLK, D), prev),
                  pl.BlockSpec((N_META, D), const),
                  pl.BlockSpec((1, D), const),
                  pl.BlockSpec((1, D), const)],
        out_specs=[pl.BlockSpec((BLK, D), prev),
                   pl.BlockSpec((N_META, D), const),
                   pl.BlockSpec((1, D), const),
                   pl.BlockSpec((1, D), const)],
        out_shape=[jax.ShapeDtypeStruct((S, D), F32), jax.ShapeDtypeStruct((N_META, D), F32),
                   jax.ShapeDtypeStruct((1, D), F32), jax.ShapeDtypeStruct((1, D), F32)],
        compiler_params=_cp("arbitrary"),
    )(dh, x, meta, g, b)


def _proj_fwd(hb, w_in, order, first, count, prev, l, job=None):
    T = hb.shape[0]
    tm = _pick(T, (1056, 384, 128))

    def body(o_sc, a_ref, w_ref, *rest):
        rest[-1][...] = _dot(a_ref[...], w_ref[...])

    return _side_call(
        body, job, name=f"proj_fwd{l}_{first}", grid=(T // tm, count),
        in_specs=[pl.BlockSpec((tm, D), lambda i, j, o: (i, 0)),
                  pl.BlockSpec((None, D, WIN_SH), lambda i, j, o: (o[first + j], 0, 0))]
        + ([] if prev is None else [pl.BlockSpec(memory_space=pl.ANY)]),
        out_specs=[pl.BlockSpec((tm, WIN_SH), lambda i, j, o: (i, o[first + j]))],
        out_shape=[jax.ShapeDtypeStruct((T, IN_TOTAL), F32)],
        scratch_shapes=[], semantics=("parallel", "arbitrary"),
        args=[hb, w_in] + ([] if prev is None else [prev]),
        aliases=None if prev is None else {2: 0}, prefetch=[order])


def _out_fwd(yc, ya, yl, w_out, h, g, b, l, job=None):
    T = h.shape[0]
    tm = _pick(T, (384, 128))

    def body(yc_ref, ya_ref, yl_ref, w_ref, h_ref, g_ref, b_ref, hn_ref, hnb_ref, xh_ref, rs_ref):
        acc = _dot(yc_ref[...], w_ref[0])
        acc += _dot(ya_ref[:, 0:WOUT_SH], w_ref[1])
        acc += _dot(ya_ref[:, WOUT_SH:2 * WOUT_SH], w_ref[2])
        acc += _dot(yl_ref[...], w_ref[3])
        z = ALPHA * h_ref[...] + acc
        y, xhat, rstd = _ln_rows(z, g_ref[...], b_ref[...])
        hn_ref[...] = y
        hnb_ref[...] = y.astype(BF16)
        xh_ref[...] = xhat
        rs_ref[...] = rstd

    row = lambda i: (i, 0)
    return _side_call(
        body, job, name=f"out_fwd{l}", grid=(T // tm,),
        in_specs=[pl.BlockSpec((tm, CW), row), pl.BlockSpec((tm, AW), row), pl.BlockSpec((tm, LW), row),
                  pl.BlockSpec((N_SHARD, WOUT_SH, D), lambda i: (0, 0, 0)),
                  pl.BlockSpec((tm, D), row),
                  pl.BlockSpec((None, 1, D), lambda i: (l, 0, 0)),
                  pl.BlockSpec((None, 1, D), lambda i: (l, 0, 0))],
        out_specs=[pl.BlockSpec((tm, D), row), pl.BlockSpec((tm, D), row), pl.BlockSpec((tm, D), row),
                   pl.BlockSpec((tm, 1), row)],
        out_shape=[jax.ShapeDtypeStruct((T, D), F32), jax.ShapeDtypeStruct((T, D), BF16),
                   jax.ShapeDtypeStruct((T, D), F32), jax.ShapeDtypeStruct((T, 1), F32)],
        scratch_shapes=[], semantics=("parallel",), args=[yc, ya, yl, w_out, h, g, b])


def _post_ln_dcat_bwd(src, target, xhat, rstd, g, w_out, l, job=None):
    T = src.shape[0]
    tm = _pick(T, (384, 128))
    per = tm // BLK if target is not None else 0
    last_blk = target.shape[0] // BLK - 1 if target is not None else 0

    def body(s_ref, *refs):
        t_refs = refs[:per]
        (xh_ref, rs_ref, g_ref, w_ref, part_ref, dz_ref, dzb_ref, dg_ref, db_ref, dc_ref, da_ref, dl_ref) = refs[per:]
        i = pl.program_id(0)

        @pl.when(i == 0)
        def _():
            part_ref[...] = jnp.zeros_like(part_ref)
            dg_ref[...] = jnp.zeros_like(dg_ref)
            db_ref[...] = jnp.zeros_like(db_ref)

        if per:
            tgt = jnp.concatenate([r[...] for r in t_refs], axis=0) if per > 1 else t_refs[0][...]
            real = _row_ids(tm, i * tm) >= BLK
            err = jnp.where(real, s_ref[...] - tgt, 0.0)
            part_ref[...] += _colsum(err * err) * (0.5 / D)
            dy = err * (1.0 / D)
        else:
            dy = s_ref[...]
        xhat = xh_ref[...]
        dz = _ln_bwd_rows(dy, xhat, rs_ref[...], g_ref[...])
        dzb = dz.astype(BF16)
        dz_ref[...] = dz
        dzb_ref[...] = dzb
        dg_ref[...] += _colsum(dy * xhat)
        db_ref[...] += _colsum(dy)
        dc_ref[...] = _dot_nt(dzb, w_ref[0])
        da_ref[:, 0:WOUT_SH] = _dot_nt(dzb, w_ref[1])
        da_ref[:, WOUT_SH:2 * WOUT_SH] = _dot_nt(dzb, w_ref[2])
        dl_ref[...] = _dot_nt(dzb, w_ref[3])

    row = lambda i: (i, 0)
    const = lambda i: (0, 0)
    t_specs = [pl.BlockSpec((BLK, D), functools.partial(lambda i, q: (jnp.clip(i * per - 1 + q, 0, last_blk), 0), q=q))
               for q in range(per)]
    return _side_call(
        body, job, name=f"post_ln_dcat_bwd{l}", grid=(T // tm,),
        in_specs=[pl.BlockSpec((tm, D), row)] + t_specs + [
            pl.BlockSpec((tm, D), row), pl.BlockSpec((tm, 1), row), pl.BlockSpec((None, 1, D), lambda i: (l, 0, 0)),
            pl.BlockSpec((N_SHARD, WOUT_SH, D), lambda i: (0, 0, 0))],
        out_specs=[pl.BlockSpec((1, D), const), pl.BlockSpec((tm, D), row), pl.BlockSpec((tm, D), row),
                   pl.BlockSpec((1, D), const), pl.BlockSpec((1, D), const),
                   pl.BlockSpec((tm, CW), row), pl.BlockSpec((tm, AW), row), pl.BlockSpec((tm, LW), row)],
        out_shape=[jax.ShapeDtypeStruct((1, D), F32), jax.ShapeDtypeStruct((T, D), F32),
                   jax.ShapeDtypeStruct((T, D), BF16), jax.ShapeDtypeStruct((1, D), F32),
                   jax.ShapeDtypeStruct((1, D), F32), jax.ShapeDtypeStruct((T, CW), F32),
                   jax.ShapeDtypeStruct((T, AW), F32), jax.ShapeDtypeStruct((T, LW), F32)],
        scratch_shapes=[], semantics=("arbitrary",),
        args=[src] + [target] * per + [xhat, rstd, g, w_out])


def _dwout_bwd(yc, ya, yl, dzb, l):
    T = dzb.shape[0]
    tm = _pick(T, (384, 128))

    def body(yc_ref, ya_ref, yl_ref, dz_ref, o_ref):
        @pl.when(pl.program_id(0) == 0)
        def _():
            o_ref[...] = jnp.zeros_like(o_ref)

        cat = jnp.concatenate([yc_ref[...], ya_ref[...], yl_ref[...]], axis=1)
        o_ref[...] += _dot_tn(cat, dz_ref[...])

    row = lambda t: (t, 0)
    out = pl.pallas_call(
        body, name=f"dwout_bwd{l}", grid=(T // tm,),
        in_specs=[pl.BlockSpec((tm, CW), row), pl.BlockSpec((tm, AW), row), pl.BlockSpec((tm, LW), row),
                  pl.BlockSpec((tm, D), row)],
        out_specs=pl.BlockSpec((D, D), lambda t: (0, 0)),
        out_shape=jax.ShapeDtypeStruct((D, D), F32),
        compiler_params=_cp("arbitrary"),
    )(yc, ya, yl, dzb)
    return out.reshape(N_SHARD, 2, WOUT_SH // 2, D)


def _dh_bwd(dproj, w_in, dz, l, job=None):
    T = dproj.shape[0]
    tm = _pick(T, (1056, 384, 128))

    n_w = len(w_in)

    def body(dp_ref, *refs):
        w_refs, (dz_ref, o_ref, acc_ref) = refs[:n_w], refs[n_w:]
        j = pl.program_id(1)

        @pl.when(j == 0)
        def _():
            acc_ref[...] = ALPHA * dz_ref[...]

        dp = dp_ref[...]
        off = 0
        for w_ref in w_refs:
            rows = w_ref.shape[0]
            acc_ref[:, off:off + rows] += _dot_nt(dp, w_ref[...])
            off += rows

        @pl.when(j == N_SHARD - 1)
        def _():
            o_ref[...] = acc_ref[...]

    return _side_call(
        body, job, name=f"dh_bwd{l}", grid=(T // tm, N_SHARD),
        in_specs=[pl.BlockSpec((tm, WIN_SH), lambda i, j: (i, j))]
        + [pl.BlockSpec((None, w.shape[1], WIN_SH), lambda i, j: (j, 0, 0)) for w in w_in]
        + [pl.BlockSpec((tm, D), lambda i, j: (i, 0))],
        out_specs=[pl.BlockSpec((tm, D), lambda i, j: (i, 0))],
        out_shape=[jax.ShapeDtypeStruct((T, D), F32)],
        scratch_shapes=[pltpu.VMEM((tm, D), F32)],
        semantics=("parallel", "arbitrary"), args=[dproj, *w_in, dz])


def _dwin_bwd(hb, dproj, l):
    T = hb.shape[0]
    tm = _pick(T, (1056, 384, 128))

    def body(h_ref, dp_ref, o_ref):
        @pl.when(pl.program_id(1) == 0)
        def _():
            o_ref[...] = jnp.zeros_like(o_ref)

        o_ref[...] += _dot_tn(h_ref[...], dp_ref[...])

    out = pl.pallas_call(
        body, name=f"dwin_bwd{l}", grid=(N_SHARD, T // tm),
        in_specs=[pl.BlockSpec((tm, D), lambda j, t: (t, 0)),
                  pl.BlockSpec((tm, WIN_SH), lambda j, t: (t, j))],
        out_specs=pl.BlockSpec((None, D, WIN_SH), lambda j, t: (j, 0, 0)),
        out_shape=jax.ShapeDtypeStruct((N_SHARD, D, WIN_SH), F32),
        compiler_params=_cp("parallel", "arbitrary"),
    )(hb, dproj)
    return out.reshape(N_SHARD, 2, D // 2, WIN_SH)


def _dwin_half(hb, dproj, which, l, tag, job=None):
    T = hb.shape[0]
    tm = _pick(T, (1056, 384, 128))
    hr = D // 2

    def body(w_ref, h_ref, dp_ref, o_ref):
        @pl.when(pl.program_id(1) == 0)
        def _():
            o_ref[...] = jnp.zeros_like(o_ref)

        o_ref[...] += _dot_tn(h_ref[...], dp_ref[...])

    return _side_call(
        body, job, name=f"dwin_{tag}{l}", grid=(N_SHARD, T // tm),
        in_specs=[pl.BlockSpec((tm, hr), lambda j, t, w: (t, w[0])),
                  pl.BlockSpec((tm, WIN_SH), lambda j, t, w: (t, j))],
        out_specs=[pl.BlockSpec((None, hr, WIN_SH), lambda j, t, w: (j, 0, 0))],
        out_shape=[jax.ShapeDtypeStruct((N_SHARD, hr, WIN_SH), F32)],
        scratch_shapes=[], semantics=("parallel", "arbitrary"), args=[hb, dproj],
        prefetch=[jnp.reshape(which, (1,)).astype(jnp.int32)])


def _glu_masked(v, g, base_row):
    rows = _row_ids(v.shape[0], base_row)
    return jnp.where(rows >= PAD, v * _sigmoid(g), 0.0)


def _conv_tile(T):
    return _pick(T, (384, 128))


SUBLANES = 8


def _for_each_shift(buf, rot, tm, offsets, fn):
    for r in range(SUBLANES):
        group = [o for o in offsets if o % SUBLANES == r]
        if not group:
            continue
        if r == 0:
            src = buf
        else:
            n = tm + max(group) - r
            rot[0:n, :] = buf[r:r + n, :]
            src = rot
        for o in group:
            fn(o, src[o - r:o - r + tm, :])


def _conv_fwd(proj, dw_w, dw_b, ln_g, ln_b, pw_w, pw_b, l):
    T = proj.shape[0]
    tm = _conv_tile(T)
    hb = tm // HALO

    def body(cv_ref, cg_ref, ct_ref, hv_ref, hg_ref, w_ref, b_ref, g_ref, be_ref, pw_ref, pb_ref,
             yc_ref, conv_ref, buf, rot):
        i = pl.program_id(0)
        buf[0:HALO, :] = _glu_masked(hv_ref[...], hg_ref[...], i * tm - HALO)
        buf[HALO:HALO + tm, :] = _glu_masked(cv_ref[...], cg_ref[...], i * tm)
        first = HALO - (CONV_K - 1)
        total = [jnp.zeros((tm, CW), F32) + b_ref[...]]

        def tap(o, tile):
            k = o - first
            total[0] = total[0] + w_ref[k:k + 1, :] * tile

        _for_each_shift(buf, rot, tm, [first + k for k in range(CONV_K)], tap)
        acc = total[0]
        conv_ref[...] = acc
        u, _, _ = _ln_rows(acc, g_ref[...], be_ref[...])
        s = u * _sigmoid(u)
        cpw = _dot(s.astype(BF16), pw_ref[...]) + pb_ref[...]
        gate, _ = _silu_and_grad(ct_ref[...])
        yc_ref[...] = (cpw * gate).astype(BF16)

    vec = pl.BlockSpec((None, 1, CW), lambda i: (l, 0, 0))
    return pl.pallas_call(
        body, name=f"conv_fwd{l}", grid=(T // tm,),
        in_specs=[pl.BlockSpec((tm, CW), lambda i: (i, 0)),
                  pl.BlockSpec((tm, CW), lambda i: (i, 1)),
                  pl.BlockSpec((tm, CW), lambda i: (i, 2)),
                  pl.BlockSpec((HALO, CW), lambda i: (jnp.maximum(i * hb - 1, 0), 0)),
                  pl.BlockSpec((HALO, CW), lambda i: (jnp.maximum(i * hb - 1, 0), 1)),
                  pl.BlockSpec((None, CONV_K, CW), lambda i: (l, 0, 0)),
                  vec, vec, vec,
                  pl.BlockSpec((CW, CW), lambda i: (0, 0)),
                  vec],
        out_specs=[pl.BlockSpec((tm, CW), lambda i: (i, 0)), pl.BlockSpec((tm, CW), lambda i: (i, 0))],
        out_shape=[jax.ShapeDtypeStruct((T, CW), BF16), jax.ShapeDtypeStruct((T, CW), F32)],
        scratch_shapes=[pltpu.VMEM((tm + HALO, CW), F32), pltpu.VMEM((tm + HALO, CW), F32)],
        compiler_params=_cp("parallel"),
    )(proj, proj, proj, proj, proj, dw_w, dw_b, ln_g, ln_b, pw_w, pw_b)


def _conv_bwd_rows(conv, proj, d_yc, ln_g, ln_b, pw_w, pw_b, l):
    T = conv.shape[0]
    tm = _conv_tile(T)

    def body(conv_ref, ct_ref, dy_ref, g_ref, be_ref, pw_ref, pb_ref,
             dconv_ref, dct_ref, dpw_ref, dpb_ref, dg_ref, db_ref):
        @pl.when(pl.program_id(0) == 0)
        def _():
            dpw_ref[...] = jnp.zeros_like(dpw_ref)
            dpb_ref[...] = jnp.zeros_like(dpb_ref)
            dg_ref[...] = jnp.zeros_like(dg_ref)
            db_ref[...] = jnp.zeros_like(db_ref)

        u, xhat, rstd = _ln_rows(conv_ref[...], g_ref[...], be_ref[...])
        s, ds_du = _silu_and_grad(u)
        sb = s.astype(BF16)
        cpw = _dot(sb, pw_ref[...]) + pb_ref[...]
        gate, dgate = _silu_and_grad(ct_ref[...])
        dy = dy_ref[...]
        d_cpw = dy * gate
        dct_ref[...] = (dy * cpw * dgate).astype(BF16)
        d_cpw_b = d_cpw.astype(BF16)
        dpb_ref[...] += _colsum(d_cpw)
        dpw_ref[...] += _dot_tn(sb, d_cpw_b)
        du = _dot_nt(d_cpw_b, pw_ref[...]) * ds_du
        dconv_ref[...] = _ln_bwd_rows(du, xhat, rstd, g_ref[...])
        dg_ref[...] += _colsum(du * xhat)
        db_ref[...] += _colsum(du)

    vec = pl.BlockSpec((None, 1, CW), lambda i: (l, 0, 0))
    row = lambda i: (i, 0)
    const = lambda i: (0, 0)
    return pl.pallas_call(
        body, name=f"conv_bwd_rows{l}", grid=(T // tm,),
        in_specs=[pl.BlockSpec((tm, CW), row), pl.BlockSpec((tm, CW), lambda i: (i, 2)),
                  pl.BlockSpec((tm, CW), row), vec, vec,
                  pl.BlockSpec((CW, CW), lambda i: (0, 0)), vec],
        out_specs=[pl.BlockSpec((tm, CW), row), pl.BlockSpec((tm, CW), lambda i: (i, 2)),
                   pl.BlockSpec((CW, CW), const), pl.BlockSpec((1, CW), const),
                   pl.BlockSpec((1, CW), const), pl.BlockSpec((1, CW), const)],
        out_shape=[jax.ShapeDtypeStruct((T, CW), F32), jax.ShapeDtypeStruct((T, IN_TOTAL), BF16),
                   jax.ShapeDtypeStruct((CW, CW), F32), jax.ShapeDtypeStruct((1, CW), F32),
                   jax.ShapeDtypeStruct((1, CW), F32), jax.ShapeDtypeStruct((1, CW), F32)],
        compiler_params=_cp("arbitrary"),
    )(conv, proj, d_yc, ln_g, ln_b, pw_w, pw_b)


def _conv_bwd_taps(d_conv, proj, dw_w, dproj, l, job=None):
    T = d_conv.shape[0]
    tm = _conv_tile(T)
    hb = tm // HALO
    nt = T // tm
    last_halo = T // HALO - 1

    def body(dc_ref, dh_ref, cv_ref, cg_ref, hv_ref, hg_ref, w_ref, _, o_ref, dw_ref, dwb_ref, cbuf, dbuf, rot):
        i = pl.program_id(0)

        @pl.when(i == 0)
        def _():
            dw_ref[...] = jnp.zeros_like(dw_ref)
            dwb_ref[...] = jnp.zeros_like(dwb_ref)

        cbuf[0:HALO, :] = _glu_masked(hv_ref[...], hg_ref[...], i * tm - HALO)
        cbuf[HALO:HALO + tm, :] = _glu_masked(cv_ref[...], cg_ref[...], i * tm)
        dmain = dc_ref[...]
        dbuf[0:tm, :] = dmain
        dbuf[tm:tm + HALO, :] = jnp.where(i < nt - 1, dh_ref[...], 0.0)
        total = [jnp.zeros((tm, CW), F32)]

        def tap_back(o, tile):
            k = CONV_K - 1 - o
            total[0] = total[0] + w_ref[k:k + 1, :] * tile

        _for_each_shift(dbuf, rot, tm, list(range(CONV_K)), tap_back)
        acc = total[0]
        first = HALO - (CONV_K - 1)

        def tap_weight(o, tile):
            k = o - first
            dw_ref[k:k + 1, :] += _colsum(dmain * tile)

        _for_each_shift(cbuf, rot, tm, [first + k for k in range(CONV_K)], tap_weight)
        dwb_ref[...] += _colsum(dmain)
        d_c = jnp.where(_row_ids(tm, i * tm) >= PAD, acc, 0.0)
        sig = _sigmoid(cg_ref[...])
        o_ref[:, 0:CW] = (d_c * sig).astype(BF16)
        o_ref[:, CW:2 * CW] = (d_c * cv_ref[...] * sig * (1.0 - sig)).astype(BF16)

    const = lambda i: (0, 0)
    return _side_call(
        body, job, name=f"conv_bwd_taps{l}", grid=(nt,),
        in_specs=[pl.BlockSpec((tm, CW), lambda i: (i, 0)),
                  pl.BlockSpec((HALO, CW), lambda i: (jnp.minimum((i + 1) * hb, last_halo), 0)),
                  pl.BlockSpec((tm, CW), lambda i: (i, 0)),
                  pl.BlockSpec((tm, CW), lambda i: (i, 1)),
                  pl.BlockSpec((HALO, CW), lambda i: (jnp.maximum(i * hb - 1, 0), 0)),
                  pl.BlockSpec((HALO, CW), lambda i: (jnp.maximum(i * hb - 1, 0), 1)),
                  pl.BlockSpec((None, CONV_K, CW), lambda i: (l, 0, 0)),
                  pl.BlockSpec(memory_space=pl.ANY)],
        out_specs=[pl.BlockSpec((tm, 2 * CW), lambda i: (i, 0)),
                   pl.BlockSpec((HALO, CW), const), pl.BlockSpec((1, CW), const)],
        out_shape=[jax.ShapeDtypeStruct(dproj.shape, BF16), jax.ShapeDtypeStruct((HALO, CW), F32),
                   jax.ShapeDtypeStruct((1, CW), F32)],
        scratch_shapes=[pltpu.VMEM((tm + HALO, CW), F32), pltpu.VMEM((tm + HALO, CW), F32),
                        pltpu.VMEM((tm + HALO, CW), F32)],
        semantics=("arbitrary",), aliases={7: 0},
        args=[d_conv, d_conv, proj, proj, proj, proj, dw_w, dproj])


def _log1p_small(e):
    return jnp.where(e < 1e-3, e * (1.0 - e * (0.5 - e * (1.0 / 3.0))), jnp.log(1.0 + e))


def _softplus(z):
    return jnp.maximum(z, 0.0) + _log1p_small(jnp.exp(-jnp.abs(z)))


def _neg_expm1(x):
    series = -x * (1.0 + x * (1.0 / 2.0) * (1.0 + x * (1.0 / 3.0) * (1.0 + x * (1.0 / 4.0) * (
        1.0 + x * (1.0 / 5.0) * (1.0 + x * (1.0 / 6.0) * (1.0 + x * (1.0 / 7.0)))))))
    return jnp.where(x > -0.25, series, 1.0 - jnp.exp(x))


def _lru_gates(rxbuf, tm, base_row, lw_ref, lb_ref, wa_ref, ba_ref, wx_ref, bx_ref, lam_ref):
    rc = jnp.zeros((tm, LW), F32) + lb_ref[...]
    for k in range(LRU_K):
        o = LHALO - (LRU_K - 1) + k
        rc += lw_ref[k:k + 1, :] * rxbuf[o:o + tm, :]
    rcb = rc.astype(BF16)
    r = _sigmoid(_dot(rcb, wa_ref[...]) + ba_ref[...])
    ig = _sigmoid(_dot(rcb, wx_ref[...]) + bx_ref[...])
    sp = _softplus(-lam_ref[...])
    la = -LRU_C * r * sp
    a = jnp.exp(la)
    mult = jnp.sqrt(_neg_expm1(2.0 * la))
    valid = _row_ids(tm, base_row) >= PAD
    return rc, rcb, r, ig, sp, a, mult, valid


def _mask_rows(v, base_row):
    return jnp.where(_row_ids(v.shape[0], base_row) >= PAD, v, 0.0)


def _scan_rows(aa, bb, carry, out_ref, reverse):
    tm = aa.shape[0]
    sub = _row_ids(tm, 0) & (SUBLANES - 1)
    s = 1
    while s < SUBLANES:
        keep = (sub < SUBLANES - s) if reverse else (sub >= s)
        shift = tm - s if reverse else s
        a_s = jnp.where(keep, pltpu.roll(aa, shift, axis=0), 1.0)
        b_s = jnp.where(keep, pltpu.roll(bb, shift, axis=0), 0.0)
        bb = aa * b_s + bb
        aa = aa * a_s
        s *= 2
    groups = range(tm // SUBLANES)
    edge = 0 if reverse else SUBLANES - 1
    for j in (reversed(groups) if reverse else groups):
        rows = slice(SUBLANES * j, SUBLANES * j + SUBLANES)
        x = bb[rows] + aa[rows] * carry
        out_ref[rows, :] = x
        carry = x[edge:edge + 1]


def _lru_tile(T):
    return _pick(T, (384, 128))


def _lru_fwd(proj, lw, lb, wa, ba, wx, bx, lam, l):
    T = proj.shape[0]
    tm = _lru_tile(T)
    hb = tm // LHALO

    def body(rx_ref, rg_ref, hx_ref, lw_ref, lb_ref, wa_ref, ba_ref, wx_ref, bx_ref, lam_ref,
             yl_ref, hl_ref, rxbuf, carry):
        i = pl.program_id(0)

        @pl.when(i == 0)
        def _():
            carry[...] = jnp.zeros_like(carry)

        rxbuf[0:LHALO, :] = _mask_rows(hx_ref[...], i * tm - LHALO)
        rxbuf[LHALO:LHALO + tm, :] = _mask_rows(rx_ref[...], i * tm)
        rc, _, _, ig, _, a, mult, valid = _lru_gates(rxbuf, tm, i * tm, lw_ref, lb_ref, wa_ref, ba_ref,
                                                     wx_ref, bx_ref, lam_ref)
        bb = jnp.where(valid, mult * (ig * rc), 0.0)
        _scan_rows(a, bb, carry[0:1, :], hl_ref, reverse=False)
        carry[0:1, :] = hl_ref[tm - 1:tm, :]
        gate, _ = _silu_and_grad(rg_ref[...])
        yl_ref[...] = (hl_ref[...] * gate).astype(BF16)

    vec = pl.BlockSpec((None, 1, LW), lambda i: (l, 0, 0))
    mat = pl.BlockSpec((None, LW, LW), lambda i: (l, 0, 0))
    return pl.pallas_call(
        body, name=f"lru_fwd{l}", grid=(T // tm,),
        in_specs=[pl.BlockSpec((tm, LW), lambda i: (i, 8)),
                  pl.BlockSpec((tm, LW), lambda i: (i, 9)),
                  pl.BlockSpec((LHALO, LW), lambda i: (jnp.maximum(i * hb - 1, 0), 8)),
                  pl.BlockSpec((None, LRU_K, LW), lambda i: (l, 0, 0)),
                  vec, mat, vec, mat, vec, vec],
        out_specs=[pl.BlockSpec((tm, LW), lambda i: (i, 0)), pl.BlockSpec((tm, LW), lambda i: (i, 0))],
        out_shape=[jax.ShapeDtypeStruct((T, LW), BF16), jax.ShapeDtypeStruct((T, LW), F32)],
        scratch_shapes=[pltpu.VMEM((tm + LHALO, LW), F32), pltpu.VMEM((8, LW), F32)],
        compiler_params=_cp("arbitrary"),
    )(proj, proj, proj, lw, lb, wa, ba, wx, bx, lam)


def _lru_bwd(proj, hl, d_yl, lw, lb, wa, ba, wx, bx, lam, dproj, l, job=None):
    T = proj.shape[0]
    tm = _lru_tile(T)
    hb = tm // LHALO
    nt = T // tm

    def body(rx_ref, rg_ref, hx_ref, hl_ref, hh_ref, dy_ref, lw_ref, lb_ref, wa_ref, ba_ref, wx_ref, bx_ref,
             lam_ref, _, o_ref, dlw_ref, dlb_ref, dwa_ref, dba_ref, dwx_ref, dbx_ref, dlam_ref,
             rxbuf, dbuf, carry, head, gbuf):
        step = pl.program_id(0)
        i = nt - 1 - step

        @pl.when(step == 0)
        def _():
            carry[...] = jnp.zeros_like(carry)
            head[...] = jnp.zeros_like(head)
            for ref in (dlw_ref, dlb_ref, dwa_ref, dba_ref, dwx_ref, dbx_ref, dlam_ref):
                ref[...] = jnp.zeros_like(ref)

        rxbuf[0:LHALO, :] = _mask_rows(hx_ref[...], i * tm - LHALO)
        rxbuf[LHALO:LHALO + tm, :] = _mask_rows(rx_ref[...], i * tm)
        rc, rcb, r, ig, sp, a, mult, valid = _lru_gates(rxbuf, tm, i * tm, lw_ref, lb_ref, wa_ref, ba_ref,
                                                        wx_ref, bx_ref, lam_ref)
        rows = _row_ids(tm, 0)
        h = hl_ref[...]
        h_before = jnp.where(i > 0, hh_ref[LHALO - 1:LHALO, :], 0.0)
        hprev = jnp.where(rows == 0, h_before, pltpu.roll(h, 1, axis=0))
        rg = rg_ref[...]
        gate, dgate = _silu_and_grad(rg)
        dy = dy_ref[...]
        o_ref[:, LW:2 * LW] = (dy * h * dgate).astype(BF16)
        bb = dy * gate + jnp.where(rows == tm - 1, carry[0:1, :], 0.0)
        aa = jnp.where(rows == tm - 1, 0.0, pltpu.roll(a, tm - 1, axis=0))
        _scan_rows(aa, bb, jnp.zeros((1, LW), F32), gbuf, reverse=True)
        g = gbuf[...]
        dbuf[0:tm, :] = a * g
        carry[0:1, :] = dbuf[0:1, :]
        du = jnp.where(valid, g, 0.0)
        da = g * hprev
        dix = du * mult
        dmult = du * (ig * rc)
        dla = jnp.where(valid, da * a - dmult * (a * a) / mult, 0.0)
        dr = dla * (-LRU_C * sp)
        dlam_ref[...] += _colsum(dla * (LRU_C * r)) * _sigmoid(-lam_ref[...])
        dpa = dr * r * (1.0 - r)
        dpx = (dix * rc) * ig * (1.0 - ig)
        dpab = dpa.astype(BF16)
        dpxb = dpx.astype(BF16)
        dba_ref[...] += _colsum(dpa)
        dbx_ref[...] += _colsum(dpx)
        dwa_ref[...] += _dot_tn(rcb, dpab)
        dwx_ref[...] += _dot_tn(rcb, dpxb)
        drc = dix * ig + _dot_nt(dpab, wa_ref[...]) + _dot_nt(dpxb, wx_ref[...])
        dbuf[0:tm, :] = drc
        dbuf[tm:tm + LHALO, :] = head[...]
        acc = jnp.zeros((tm, LW), F32)
        for k in range(LRU_K):
            o = LRU_K - 1 - k
            acc += lw_ref[k:k + 1, :] * dbuf[o:o + tm, :]
            oc = LHALO - (LRU_K - 1) + k
            dlw_ref[k:k + 1, :] += _colsum(drc * rxbuf[oc:oc + tm, :])
        dlb_ref[...] += _colsum(drc)
        head[...] = dbuf[0:LHALO, :]
        o_ref[:, 0:LW] = jnp.where(valid, acc, 0.0).astype(BF16)

    rev = lambda s: nt - 1 - s
    vec = pl.BlockSpec((None, 1, LW), lambda s: (l, 0, 0))
    mat = pl.BlockSpec((None, LW, LW), lambda s: (l, 0, 0))
    const = lambda s: (0, 0)
    halo = lambda s: jnp.maximum(rev(s) * hb - 1, 0)
    return _side_call(
        body, job, name=f"lru_bwd{l}", grid=(nt,),
        in_specs=[pl.BlockSpec((tm, LW), lambda s: (rev(s), 8)),
                  pl.BlockSpec((tm, LW), lambda s: (rev(s), 9)),
                  pl.BlockSpec((LHALO, LW), lambda s: (halo(s), 8)),
                  pl.BlockSpec((tm, LW), lambda s: (rev(s), 0)),
                  pl.BlockSpec((LHALO, LW), lambda s: (halo(s), 0)),
                  pl.BlockSpec((tm, LW), lambda s: (rev(s), 0)),
                  pl.BlockSpec((None, LRU_K, LW), lambda s: (l, 0, 0)),
                  vec, mat, vec, mat, vec, vec, pl.BlockSpec(memory_space=pl.ANY)],
        out_specs=[pl.BlockSpec((tm, 2 * LW), lambda s: (rev(s), 4)),
                   pl.BlockSpec((8, LW), const), pl.BlockSpec((1, LW), const),
                   pl.BlockSpec((LW, LW), const), pl.BlockSpec((1, LW), const),
                   pl.BlockSpec((LW, LW), const), pl.BlockSpec((1, LW), const),
                   pl.BlockSpec((1, LW), const)],
        out_shape=[jax.ShapeDtypeStruct(dproj.shape, BF16),
                   jax.ShapeDtypeStruct((8, LW), F32), jax.ShapeDtypeStruct((1, LW), F32),
                   jax.ShapeDtypeStruct((LW, LW), F32), jax.ShapeDtypeStruct((1, LW), F32),
                   jax.ShapeDtypeStruct((LW, LW), F32), jax.ShapeDtypeStruct((1, LW), F32),
                   jax.ShapeDtypeStruct((1, LW), F32)],
        scratch_shapes=[pltpu.VMEM((tm + LHALO, LW), F32), pltpu.VMEM((tm + LHALO, LW), F32),
                        pltpu.VMEM((8, LW), F32), pltpu.VMEM((LHALO, LW), F32), pltpu.VMEM((tm, LW), F32)],
        semantics=("arbitrary",), aliases={13: 0},
        args=[proj, proj, proj, hl, hl, d_yl, lw, lb, wa, ba, wx, bx, lam, dproj])


def _rope_tables(T):
    pos = (lax.broadcasted_iota(jnp.int32, (T, 128), 0) - PAD).astype(F32)
    lane = lax.broadcasted_iota(jnp.int32, (T, 128), 1) % 64
    inv_freq = ROPE_THETA ** (-(lane % ROT_HALF).astype(F32) / ROT_HALF)
    ang = pos * inv_freq
    cos, sin = jnp.cos(ang), jnp.sin(ang)
    c = jnp.where(lane < 2 * ROT_HALF, cos, 1.0)
    s1 = jnp.where(lane < ROT_HALF, -sin, 0.0)
    s2 = jnp.where((lane >= ROT_HALF) & (lane < 2 * ROT_HALF), sin, 0.0)
    return c, s1, s2


def _rot_fwd(x, c, s1, s2):
    return x * c + pltpu.roll(x, 128 - ROT_HALF, axis=1) * s1 + pltpu.roll(x, ROT_HALF, axis=1) * s2


def _rot_bwd(dy, c, s1, s2):
    return dy * c + pltpu.roll(dy * s1, ROT_HALF, axis=1) + pltpu.roll(dy * s2, 128 - ROT_HALF, axis=1)


def _rope_fwd(proj, tabs, l):
    T = proj.shape[0]

    def body(ql_ref, qh_ref, k_ref, v_ref, c_ref, s1_ref, s2_ref, qr_ref, kr_ref, vb_ref):
        c, s1, s2 = c_ref[...], s1_ref[...], s2_ref[...]
        for gcol in range(AW // 128):
            src = ql_ref if gcol < 4 else qh_ref
            x = src[:, 128 * (gcol % 4):128 * (gcol % 4) + 128]
            qr_ref[:, 128 * gcol:128 * gcol + 128] = (_rot_fwd(x, c, s1, s2) * 0.125).astype(BF16)
        for gcol in range(KVW // 128):
            x = k_ref[:, 128 * gcol:128 * gcol + 128]
            kr_ref[:, 128 * gcol:128 * gcol + 128] = _rot_fwd(x, c, s1, s2).astype(BF16)
        vb_ref[...] = v_ref[...].astype(BF16)

    tr = _pick(T, (384, 128))
    tab = pl.BlockSpec((tr, 128), lambda n: (n, 0))
    return pl.pallas_call(
        body, name=f"rope_fwd{l}", grid=(T // tr,),
        in_specs=[pl.BlockSpec((tr, 512), lambda n: (n, 3)), pl.BlockSpec((tr, 512), lambda n: (n, 4)),
                  pl.BlockSpec((tr, KVW), lambda n: (n, 10)), pl.BlockSpec((tr, KVW), lambda n: (n, 11)),
                  tab, tab, tab],
        out_specs=[pl.BlockSpec((tr, AW), lambda n: (n, 0)), pl.BlockSpec((tr, KVW), lambda n: (n, 0)),
                   pl.BlockSpec((tr, KVW), lambda n: (n, 0))],
        out_shape=[jax.ShapeDtypeStruct((T, AW), BF16), jax.ShapeDtypeStruct((T, KVW), BF16),
                   jax.ShapeDtypeStruct((T, KVW), BF16)],
        compiler_params=_cp("parallel"),
    )(proj, proj, proj, proj, *tabs)


GROUP = 4


def _attn_mask(n, reps):
    qi = lax.broadcasted_iota(jnp.int32, (reps * BLK, BLK), 0) & (BLK - 1)
    kj = lax.broadcasted_iota(jnp.int32, (reps * BLK, BLK), 1)
    m0 = (kj >= PAD) & (n >= 1)
    mp = (kj > qi) & (n >= 2)
    mc = (kj <= qi) & ((n >= 1) | (kj >= PAD))
    return jnp.concatenate([m0, mp, mc], axis=1)


def _kv_both(x0_ref, xp_ref, xc_ref, g):
    pg, off = g // 2, g % 2
    cols = slice(128 * pg, 128 * pg + 128)
    x = jnp.concatenate([x0_ref[:, cols], xp_ref[:, cols], xc_ref[:, cols]], axis=0).astype(F32)
    lane = lax.broadcasted_iota(jnp.int32, (1, 128), 1)
    half = jnp.where((lane < 64) if off == 0 else (lane >= 64), x, 0.0)
    return (half + pltpu.roll(half, 64, axis=1)).astype(BF16)


def _kv_halves(x0_ref, xp_ref, xc_ref, g):
    pg, off = g // 2, g % 2
    cols = slice(128 * pg, 128 * pg + 128)
    x = jnp.concatenate([x0_ref[:, cols], xp_ref[:, cols], xc_ref[:, cols]], axis=0).astype(F32)
    lane = lax.broadcasted_iota(jnp.int32, (1, 128), 1)
    if off == 0:
        lo = jnp.where(lane < 64, x, 0.0)
        hi = pltpu.roll(lo, 64, axis=1)
    else:
        hi = jnp.where(lane >= 64, x, 0.0)
        lo = pltpu.roll(hi, 64, axis=1)
    return lo.astype(BF16), hi.astype(BF16)


def _stack_heads(a, b):
    lo = lax.broadcasted_iota(jnp.int32, (1, 128), 1) < 64
    a, b = a.astype(F32), b.astype(F32)
    return jnp.concatenate([jnp.where(lo, a, 0.0), jnp.where(lo, 0.0, a),
                            jnp.where(lo, b, 0.0), jnp.where(lo, 0.0, b)], axis=0).astype(BF16)


def _unstack_heads(x):
    lo = lax.broadcasted_iota(jnp.int32, (1, 128), 1) < 64
    return (jnp.where(lo, x[0:BLK], x[BLK:2 * BLK]), jnp.where(lo, x[2 * BLK:3 * BLK], x[3 * BLK:4 * BLK]))


def _per_head_column(values):
    return jnp.concatenate([jnp.zeros((BLK, 1), F32) + v for v in values], axis=0)


def _attn_fwd(qr, kr, vb, proj, sinks, l, job=None):
    T = qr.shape[0]

    def body(sink_ref, q_ref, k0_ref, kp_ref, kc_ref, v0_ref, vp_ref, vc_ref, ag_ref, ya_ref, att_ref, lse_ref):
        n = pl.program_id(0)
        mask = _attn_mask(n, 1)
        lane = lax.broadcasted_iota(jnp.int32, (1, 128), 1)
        lse_acc = jnp.zeros((BLK, 128), F32)
        for g in range(4):
            kx = _kv_both(k0_ref, kp_ref, kc_ref, g)
            vx = _kv_both(v0_ref, vp_ref, vc_ref, g)
            pair_cols = [slice(128 * (2 * g + pp), 128 * (2 * g + pp) + 128) for pp in range(2)]
            s4 = _dot_nt(_stack_heads(q_ref[:, pair_cols[0]], q_ref[:, pair_cols[1]]), kx)
            probs = []
            for r in range(GROUP):
                h = GROUP * g + r
                sink = sink_ref[l, h]
                s = jnp.where(mask, s4[BLK * r:BLK * r + BLK], NEG_INF)
                m = jnp.maximum(jnp.max(s, axis=1, keepdims=True), sink)
                p = jnp.exp(s - m)
                denom = jnp.sum(p, axis=1, keepdims=True) + jnp.exp(sink - m)
                probs.append((p * (1.0 / denom)).astype(BF16))
                lse_acc = jnp.where(lane == h, m + jnp.log(denom), lse_acc)
            outs = _unstack_heads(_dot(jnp.concatenate(probs, axis=0), vx))
            for cols, out in zip(pair_cols, outs):
                att_ref[:, cols] = out
                gate, _ = _silu_and_grad(ag_ref[:, cols])
                ya_ref[:, cols] = (out * gate).astype(BF16)
        lse_ref[...] = lse_acc

    prev = lambda n: (jnp.maximum(n - 1, 0), 0)
    cur = lambda n: (n, 0)
    zero = lambda n: (0, 0)
    kv = lambda f: pl.BlockSpec((BLK, KVW), f)
    return _side_call(
        body, job, name=f"attn_fwd{l}", grid=(T // BLK,),
        in_specs=[pl.BlockSpec(memory_space=pltpu.SMEM),
                  pl.BlockSpec((BLK, AW), cur), kv(zero), kv(prev), kv(cur), kv(zero), kv(prev), kv(cur),
                  pl.BlockSpec((BLK, AW), lambda n: (n, 3))],
        out_specs=[pl.BlockSpec((BLK, AW), cur), pl.BlockSpec((BLK, AW), cur), pl.BlockSpec((BLK, 128), cur)],
        out_shape=[jax.ShapeDtypeStruct((T, AW), BF16), jax.ShapeDtypeStruct((T, AW), F32),
                   jax.ShapeDtypeStruct((T, 128), F32)],
        scratch_shapes=[], semantics=("parallel",), args=[sinks, qr, kr, kr, kr, vb, vb, vb, proj])


def _attn_bwd(qr, kr, vb, proj, att, lse, d_ya, sinks, dproj, l, job=None):
    T = qr.shape[0]
    nb = T // BLK

    def body(sink_ref, q_ref, k0_ref, kp_ref, kc_ref, v0_ref, vp_ref, vc_ref, ag_ref, att_ref, lse_ref, dy_ref, _,
             dq_ref, dk_ref, dv_ref, dk0_ref, dv0_ref, dag_ref, dsink_ref, kcarry, vcarry):
        n = pl.program_id(0)

        @pl.when(n == 0)
        def _():
            dk0_ref[...] = jnp.zeros_like(dk0_ref)
            dv0_ref[...] = jnp.zeros_like(dv0_ref)
            dsink_ref[...] = jnp.zeros_like(dsink_ref)
            kcarry[...] = jnp.zeros_like(kcarry)
            vcarry[...] = jnp.zeros_like(vcarry)

        @pl.when(n == nb)
        def _():
            dk_ref[...] = kcarry[...]
            dv_ref[...] = vcarry[...]

        @pl.when(n < nb)
        def _():
            mask = _attn_mask(n, GROUP)
            lane = lax.broadcasted_iota(jnp.int32, (1, 128), 1)
            lse = lse_ref[...]
            dsink = jnp.zeros((1, 128), F32)
            dk_pg, dv_pg = [], []
            for pg in range(2):
                dk_acc = jnp.zeros((3 * BLK, 128), F32)
                dv_acc = jnp.zeros((3 * BLK, 128), F32)
                for off in range(2):
                    g = 2 * pg + off
                    kx = _kv_both(k0_ref, kp_ref, kc_ref, g)
                    vx = _kv_both(v0_ref, vp_ref, vc_ref, g)
                    pair_cols = [slice(128 * (2 * g + pp), 128 * (2 * g + pp) + 128) for pp in range(2)]
                    q4 = _stack_heads(q_ref[:, pair_cols[0]], q_ref[:, pair_cols[1]])
                    d_out = []
                    for cols in pair_cols:
                        gate, dgate = _silu_and_grad(ag_ref[:, cols])
                        dy = dy_ref[:, cols]
                        dag_ref[:, cols] = (dy * att_ref[:, cols] * dgate).astype(BF16)
                        d_out.append(dy * gate)
                    do4 = _stack_heads(d_out[0], d_out[1])
                    heads = [GROUP * g + r for r in range(GROUP)]
                    sink = _per_head_column([sink_ref[l, h] for h in heads])
                    lse4 = _per_head_column(
                        [jnp.sum(jnp.where(lane == h, lse, 0.0), axis=1, keepdims=True) for h in heads])
                    p = jnp.where(mask, jnp.exp(_dot_nt(q4, kx) - lse4), 0.0)
                    dp = _dot_nt(do4, vx)
                    delta = jnp.sum(p * dp, axis=1, keepdims=True)
                    ds = (p * (dp - delta)).astype(BF16)
                    sink_term = jnp.exp(sink - lse4) * delta
                    for r, h in enumerate(heads):
                        dsink += jnp.where(lane == h, -jnp.sum(sink_term[BLK * r:BLK * r + BLK]), 0.0)
                    for cols, dq in zip(pair_cols, _unstack_heads(_dot(ds, kx))):
                        dq_ref[:, cols] = dq
                    dkg = _dot_tn(ds, q4)
                    dvg = _dot_tn(p.astype(BF16), do4)
                    own = (lane < 64) if off == 0 else (lane >= 64)
                    dk_acc += jnp.where(own, dkg + pltpu.roll(dkg, 64, axis=1), 0.0)
                    dv_acc += jnp.where(own, dvg + pltpu.roll(dvg, 64, axis=1), 0.0)
                dk_pg.append(dk_acc)
                dv_pg.append(dv_acc)
            dsink_ref[...] += dsink
            for pg in range(2):
                cols = slice(128 * pg, 128 * pg + 128)
                dk0_ref[:, cols] += dk_pg[pg][0:BLK]
                dv0_ref[:, cols] += dv_pg[pg][0:BLK]
                dk_ref[:, cols] = kcarry[:, cols] + dk_pg[pg][BLK:2 * BLK]
                dv_ref[:, cols] = vcarry[:, cols] + dv_pg[pg][BLK:2 * BLK]
                kcarry[:, cols] = dk_pg[pg][2 * BLK:3 * BLK]
                vcarry[:, cols] = dv_pg[pg][2 * BLK:3 * BLK]

    last = nb - 1
    cur = lambda n: (jnp.minimum(n, last), 0)
    prev = lambda n: (jnp.clip(n - 1, 0, last), 0)
    zero = lambda n: (0, 0)
    kv = lambda f: pl.BlockSpec((BLK, KVW), f)
    wide = lambda f: pl.BlockSpec((BLK, AW), f)
    return _side_call(
        body, job, name=f"attn_bwd{l}", grid=(nb + 1,),
        in_specs=[pl.BlockSpec(memory_space=pltpu.SMEM),
                  wide(cur), kv(zero), kv(prev), kv(cur), kv(zero), kv(prev), kv(cur),
                  pl.BlockSpec((BLK, AW), lambda n: (jnp.minimum(n, last), 3)),
                  wide(cur), pl.BlockSpec((BLK, 128), cur), wide(cur), pl.BlockSpec(memory_space=pl.ANY)],
        out_specs=[wide(cur), kv(prev), kv(prev), kv(zero), kv(zero),
                   pl.BlockSpec((BLK, AW), lambda n: (jnp.minimum(n, last), 3)),
                   pl.BlockSpec((1, 128), zero)],
        out_shape=[jax.ShapeDtypeStruct((T, AW), F32), jax.ShapeDtypeStruct((T, KVW), F32),
                   jax.ShapeDtypeStruct((T, KVW), F32), jax.ShapeDtypeStruct((BLK, KVW), F32),
                   jax.ShapeDtypeStruct((BLK, KVW), F32), jax.ShapeDtypeStruct(dproj.shape, BF16),
                   jax.ShapeDtypeStruct((1, 128), F32)],
        scratch_shapes=[pltpu.VMEM((BLK, KVW), F32), pltpu.VMEM((BLK, KVW), F32)],
        semantics=("arbitrary",), aliases={12: 5},
        args=[sinks, qr, kr, kr, kr, vb, vb, vb, proj, att, lse, d_ya, dproj])


def _rope_bwd(dqr, dk, dv, dk0, dv0, tabs, dproj, l):
    T = dqr.shape[0]

    def body(dq_ref, dk_ref, dv_ref, dk0_ref, dv0_ref, c_ref, s1_ref, s2_ref, _, o_ref):
        n = pl.program_id(0)
        c, s1, s2 = c_ref[...], s1_ref[...], s2_ref[...]
        for gcol in range(AW // 128):
            cols = slice(128 * gcol, 128 * gcol + 128)
            o_ref[:, cols] = (_rot_bwd(dq_ref[:, cols], c, s1, s2) * 0.125).astype(BF16)
        for gcol in range(KVW // 128):
            cols = slice(128 * gcol, 128 * gcol + 128)
            kcols = slice(AW + 128 * gcol, AW + 128 * gcol + 128)
            vcols = slice(AW + KVW + 128 * gcol, AW + KVW + 128 * gcol + 128)
            o_ref[:, kcols] = _rot_bwd(dk_ref[:, cols], c, s1, s2).astype(BF16)
            o_ref[:, vcols] = dv_ref[:, cols].astype(BF16)

            @pl.when(n == 0)
            def _():
                dkk = dk_ref[0:BLK, cols] + dk0_ref[:, cols]
                o_ref[0:BLK, kcols] = _rot_bwd(dkk, c[0:BLK], s1[0:BLK], s2[0:BLK]).astype(BF16)
                o_ref[0:BLK, vcols] = (dv_ref[0:BLK, cols] + dv0_ref[:, cols]).astype(BF16)

    tr = _pick(T, (384, 128))
    cur = lambda n: (n, 0)
    zero = lambda n: (0, 0)
    tab = pl.BlockSpec((tr, 128), cur)
    return pl.pallas_call(
        body, name=f"rope_bwd{l}", grid=(T // tr,),
        in_specs=[pl.BlockSpec((tr, AW), cur), pl.BlockSpec((tr, KVW), cur), pl.BlockSpec((tr, KVW), cur),
                  pl.BlockSpec((BLK, KVW), zero), pl.BlockSpec((BLK, KVW), zero), tab, tab, tab,
                  pl.BlockSpec(memory_space=pl.ANY)],
        out_specs=pl.BlockSpec((tr, AW + 2 * KVW), lambda n: (n, 1)),
        out_shape=jax.ShapeDtypeStruct(dproj.shape, BF16),
        input_output_aliases={8: 0},
        compiler_params=_cp("parallel"),
    )(dqr, dk, dv, dk0, dv0, *tabs, dproj)


def _block_diag(w):
    nl, nh, hd, _ = w.shape
    eye = jnp.eye(nh, dtype=w.dtype)
    return jnp.einsum("lhij,hg->lhigj", w, eye).reshape(nl, nh * hd, nh * hd)


def _diag_blocks(m):
    nh, hd = 8, 64
    return jnp.einsum("hihj->hij", m.reshape(nh, hd, nh, hd))


def _device_step(x, target, p, dist=None):
    vec = lambda a: a.reshape(DEPTH, 1, a.shape[-1])
    ln_in_g, ln_in_b = p["ln_in_g"].reshape(1, D), p["ln_in_b"].reshape(1, D)
    conv_dw_b, conv_ln_g, conv_ln_b, conv_pw_b = map(vec, (p["conv_dw_b"], p["conv_ln_g"], p["conv_ln_b"], p["conv_pw_b"]))
    lru_conv_b, lru_ba, lru_bx, lru_lambda = map(vec, (p["lru_conv_b"], p["lru_ba"], p["lru_bx"], p["lru_lambda"]))
    ln_post_g, ln_post_b = vec(p["ln_post_g"]), vec(p["ln_post_b"])
    wa_bd = _block_diag(p["lru_wa"]).astype(BF16)
    wx_bd = _block_diag(p["lru_wx"]).astype(BF16)
    w_in, w_out, pw_w = list(p["w_in"]), list(p["w_out"]), list(p["conv_pw_w"])
    sinks = p["attn_sinks"]
    big_names = ("w_in", "w_out", "conv_pw_w")

    order = dist[4] if dist else jnp.arange(N_SHARD, dtype=jnp.int32)
    (h, hb), got = _embed_fwd(x, p["meta_tokens"], ln_in_g, ln_in_b,
                              job=_gather_job([w_in[0]], peers=(0, 1)) if dist else None)
    if dist:
        w_in[0] = got[0]
    T = h.shape[0]
    tabs = _rope_tables(T)
    saved = []
    for l in range(DEPTH):
        if l == 0:
            job = _join_jobs(_gather_job([w_in[0]], peers=(2,)), _gather_job([pw_w[0]])) if dist else None
            (proj,), got = _proj_fwd(hb, w_in[0], order, 0, N_SHARD - 1, None, l, job=job)
            if dist:
                w_in[0], pw_w[0] = got
            (proj,), got = _proj_fwd(hb, w_in[0], order, N_SHARD - 1, 1, proj, l,
                                     job=_gather_job([w_out[0]]) if dist else None)
            if dist:
                w_out[0] = got[0]
        else:
            (proj,), _ = _proj_fwd(hb, w_in[l], order, 0, N_SHARD, None, l)
        pw_l = pw_w[l].reshape(CW, CW)
        yc, conv = _conv_fwd(proj, p["conv_dw_w"], conv_dw_b, conv_ln_g, conv_ln_b, pw_l, conv_pw_b, l)
        qr, kr, vb = _rope_fwd(proj, tabs, l)
        (ya, att, lse), got = _attn_fwd(
            qr, kr, vb, proj, sinks, l, job=_gather_job([w_in[1]]) if dist and l == 0 else None)
        if got:
            w_in[1] = got[0]
        yl, hl = _lru_fwd(proj, p["lru_conv_w"], lru_conv_b, wa_bd, lru_ba, wx_bd, lru_bx, lru_lambda, l)
        (hn, hnb, xhat, rstd), got = _out_fwd(
            yc, ya, yl, w_out[l], h, ln_post_g, ln_post_b, l,
            job=_gather_job([w_out[1], pw_w[1]]) if dist and l == 0 else None)
        if got:
            w_out[1], pw_w[1] = got
        saved.append((hb, proj, yc, conv, qr, kr, vb, ya, att, lse, yl, hl, xhat, rstd, pw_l))
        h, hb = hn, hnb

    dh = None
    g = {}
    later = None
    early, last = ("w_out", "conv_pw_w"), ("w_in",)
    own = {}
    for l in reversed(range(DEPTH)):
        hb_l, proj, yc, conv, qr, kr, vb, ya, att, lse, yl, hl, xhat, rstd, pw_l = saved[l]
        tail = dist is not None and l == 0
        top = l == DEPTH - 1
        (part, dz, dzb, g["ln_post_g", l], g["ln_post_b", l], d_yc, d_ya, d_yl), recv = _post_ln_dcat_bwd(
            h if top else dh, target if top else None, xhat, rstd, ln_post_g, w_out[l], l,
            job=_swap_job(later["grads"]) if later else None)
        if top:
            loss_part = part
        if later:
            later["parts"], later["owns"] = _chip_partials(big_names, later["grads"], recv, dist, later["l"])
        g["w_out", l] = _dwout_bwd(yc, ya, yl, dzb, l)
        d_conv, dproj, dpw, g["conv_pw_b", l], g["conv_ln_g", l], g["conv_ln_b", l] = _conv_bwd_rows(
            conv, proj, d_yc, conv_ln_g, conv_ln_b, pw_l, conv_pw_b, l)
        g["conv_pw_w", l] = dpw.reshape(N_SHARD, 2, PW_SH // 2, CW)
        if tail:
            own["early"] = dict(l=0, grads=[g[name, 0] for name in early])
        job = None
        if tail:
            job = _join_jobs(_swap_job(own["early"]["grads"]), _scatter_job(later["parts"][1:]))
        (dproj, ddw, g["conv_dw_b", l]), got = _conv_bwd_taps(d_conv, proj, p["conv_dw_w"], dproj, l, job=job)
        if tail:
            n_early = len(early)
            own["early"]["parts"], own["early"]["owns"] = _chip_partials(
                early, own["early"]["grads"], got[:n_early], dist, 0)
            later["z"] = got[n_early:]
        g["conv_dw_w", l] = ddw[:CONV_K]
        (dqr, dk, dv, dk0, dv0, dproj, dsink), z = _attn_bwd(
            qr, kr, vb, proj, att, lse, d_ya, sinks, dproj, l,
            job=_scatter_job(later["parts"][:1]) if later else None)
        if later:
            later["z"] = z + later["z"]
        g["attn_sinks", l] = dsink[0, :N_HEADS]
        dproj = _rope_bwd(dqr, dk, dv, dk0, dv0, tabs, dproj, l)
        (dproj, dlw, g["lru_conv_b", l], dwa, g["lru_ba", l], dwx, g["lru_bx", l], g["lru_lambda", l]), z = _lru_bwd(
            proj, hl, d_yl, p["lru_conv_w"], lru_conv_b, wa_bd, lru_ba, wx_bd, lru_bx, lru_lambda, dproj, l,
            job=_scatter_job(own["early"]["parts"]) if tail else None)
        if tail:
            own["early"]["z"] = z
        g["lru_conv_w", l] = dlw[:LRU_K]
        g["lru_wa", l] = _diag_blocks(dwa)
        g["lru_wx", l] = _diag_blocks(dwx)
        job = None
        if l > 0:
            g["w_in", l] = _dwin_bwd(hb_l, dproj, l)
        else:
            c = dist[0] if dist else jnp.int32(0)
            pack_a = _pack_rows([_layer_stack(g, name) for name in _SMALL_LAYERED]) if dist else None
            (give,), slots_a = _dwin_half(hb_l, dproj, 1 - c, l, "give", job=_spread_job(pack_a) if dist else None)
            (keep,), recv = _dwin_half(hb_l, dproj, c, l, "keep", job=_send_job([give]) if dist else None)
            if dist:
                g["pack_layered", -1] = _sum_slots(pack_a, slots_a[0], dist[3], "layered")
                own["last"] = dict(l=0)
                own["last"]["parts"], own["last"]["owns"] = _chip_partials(
                    last, [keep.reshape(N_SHARD, 1, D // 2, WIN_SH)], recv, (jnp.int32(0),) + tuple(dist[1:]), 0)
                job = _scatter_job(own["last"]["parts"])
            else:
                g["w_in", l] = jnp.stack([keep, give], axis=1)
        (dh,), got = _dh_bwd(dproj, [w_in[l]], dz, l, job=job)
        if tail:
            own["last"]["z"] = got
        if later:
            _finish_reduce(big_names, later, dist, g)
            later = None
        if dist and l > 0:
            later = dict(l=l, grads=[g[name, l] for name in big_names])
    grad_x, g["meta_tokens", -1], g["ln_in_g", -1], g["ln_in_b", -1] = _embed_bwd(
        dh, x, p["meta_tokens"], ln_in_g, ln_in_b)
    if dist:
        pack_b = _pack_rows([g[name, -1] for name in _SMALL_EMBED])
        slots_b = _run_job(_spread_job(pack_b), "spread_embed")[0]
        g["pack_embed", -1] = _sum_slots(pack_b, slots_b, dist[3], "embed")
        state = dict(l=0, owns=own["last"]["owns"] + own["early"]["owns"], z=own["last"]["z"] + own["early"]["z"])
        _finish_reduce(last + early, state, dist, g)
    return loss_part, grad_x, g


_SMALL_EMBED = ("meta_tokens", "ln_in_g", "ln_in_b")
_SMALL_LAYERED = ("conv_dw_w", "conv_dw_b", "conv_ln_g", "conv_ln_b", "conv_pw_b", "attn_sinks", "lru_conv_w",
                  "lru_conv_b", "lru_wa", "lru_ba", "lru_wx", "lru_bx", "lru_lambda", "ln_post_g", "ln_post_b")


def _layer_stack(g, name):
    return jnp.stack([g[name, l] for l in range(DEPTH)], axis=0)


def _chip_partials(names, grads, recv, dist, l):
    outs = [_chip_partial(a, r, dist[0], dist[1], f"{name}{l}") for name, a, r in zip(names, grads, recv)]
    return [o[0] for o in outs], [o[1] for o in outs]


def _finish_reduce(names, state, dist, g):
    l = state["l"]
    totals = [_shard_total(po, zz, dist[2], f"{name}{l}") for name, po, zz in zip(names, state["owns"], state["z"])]
    full = _run_job(_share_job(totals), f"share_halves{l}")
    for name, f in zip(names, full):
        g[name, l] = f.reshape(2 * f.shape[1], f.shape[2])


MESH = pl.DeviceIdType.MESH
HBM_SPEC = pl.BlockSpec(memory_space=pltpu.HBM)
N_DEV = 8


def _position():
    x, y, c = lax.axis_index("x"), lax.axis_index("y"), lax.axis_index("c")
    return x, y, c


def _other_chips(x, y):
    return [(1 - x, y), (x, 1 - y), (1 - x, 1 - y)]


def _cast_into_slot(a, l, j, tag, piece=0, pieces=1):
    _, R, C = a.shape
    rows = R // pieces
    tb = _pick(rows, (512, 128))
    first = piece * rows // tb

    def body(s_ref, a_ref, o_ref):
        o_ref[...] = a_ref[...].astype(BF16)

    grid_spec = pltpu.PrefetchScalarGridSpec(
        num_scalar_prefetch=1, grid=(rows // tb,),
        in_specs=[pl.BlockSpec((None, tb, C), lambda t, sc: (l, first + t, 0))],
        out_specs=pl.BlockSpec((None, tb, C), lambda t, sc: (sc[0], t, 0)))
    return pl.pallas_call(
        body, name=f"cast_into_slot_{tag}{l}_{piece}", grid_spec=grid_spec,
        out_shape=jax.ShapeDtypeStruct((N_SHARD, rows, C), BF16),
        compiler_params=_cp("arbitrary"),
    )(jnp.reshape(j, (1,)).astype(jnp.int32), a)


class _Job:
    def __init__(self, inputs, aliased, extra_out, sems, start, mid, finish):
        self.inputs, self.aliased, self.extra_out, self.sems = list(inputs), aliased, list(extra_out), list(sems)
        self.start, self.mid, self.finish = start, mid, finish

    def out_shapes(self):
        own = [jax.ShapeDtypeStruct(a.shape, a.dtype) for a in self.inputs] if self.aliased else []
        return own + self.extra_out


def _side_call(body, job, *, name, grid, in_specs, out_specs, out_shape, scratch_shapes, semantics, args,
               aliases=None, prefetch=()):
    aliases = dict(aliases or {})
    n_pre = len(prefetch)

    def call(fn, ins, outs, shapes, scratch, sem, operands):
        if n_pre:
            spec = pltpu.PrefetchScalarGridSpec(num_scalar_prefetch=n_pre, grid=grid, in_specs=ins, out_specs=outs,
                                                scratch_shapes=scratch)
            return pl.pallas_call(fn, name=name, grid_spec=spec, out_shape=shapes,
                                  input_output_aliases={k + n_pre: v for k, v in aliases.items()},
                                  compiler_params=_cp(*sem))(*prefetch, *operands)
        return pl.pallas_call(fn, name=name, grid=grid, in_specs=ins, out_specs=outs, out_shape=shapes,
                              scratch_shapes=scratch, input_output_aliases=aliases,
                              compiler_params=_cp(*sem))(*operands)

    if job is None:
        return list(call(body, list(in_specs), list(out_specs), list(out_shape), list(scratch_shapes),
                         semantics, args)), []
    n_in, n_out, n_scr = len(in_specs), len(out_specs), len(scratch_shapes)
    j_in, j_out = len(job.inputs), len(job.out_shapes())
    steps = 1
    for gsize in grid:
        steps *= gsize

    def wrapped(*refs):
        pre, refs = refs[:n_pre], refs[n_pre:]
        host_in, job_in = refs[:n_in], refs[n_in:n_in + j_in]
        o0 = n_in + j_in
        host_out, job_out = refs[o0:o0 + n_out], refs[o0 + n_out:o0 + n_out + j_out]
        s0 = o0 + n_out + j_out
        host_scr, sems = refs[s0:s0 + n_scr], refs[s0 + n_scr:]
        step = pl.program_id(0)
        for d in range(1, len(grid)):
            step = step * grid[d] + pl.program_id(d)

        @pl.when(step == 0)
        def _():
            job.start(job_in, job_out, sems)

        @pl.when(step == max(steps - 2, 0))
        def _():
            job.mid(job_in, job_out, sems)

        body(*pre, *host_in, *host_out, *host_scr)

        @pl.when(step == steps - 1)
        def _():
            job.finish(job_in, job_out, sems)

    if job.aliased:
        aliases.update({n_in + k: n_out + k for k in range(j_in)})
    outs = call(wrapped, list(in_specs) + [HBM_SPEC] * j_in, list(out_specs) + [HBM_SPEC] * j_out,
                list(out_shape) + job.out_shapes(), list(scratch_shapes) + job.sems,
                ["arbitrary"] * len(grid), [*args, *job.inputs])
    return list(outs[:n_out]), list(outs[n_out:])


def _run_job(job, name):
    return _side_call(lambda: None, job, name=name, grid=(1,), in_specs=[], out_specs=[], out_shape=[],
                      scratch_shapes=[], semantics=("arbitrary",), args=[])[1]


def _gather_job(slots, peers=(0, 1, 2)):
    n = len(slots)

    def copies(buf, sems):
        ici_send, ici_recv, d2d_send, d2d_recv = sems
        x, y, c = _position()
        chips = _other_chips(x, y)

        def half(k, slot, which):
            hr = buf[k].shape[1] // 2
            return buf[k].at[slot, pl.ds(pl.multiple_of(which * hr, hr), hr)]

        def over_ici(k, p, slot):
            px, py = chips[p]
            return pltpu.make_async_remote_copy(
                src_ref=half(k, slot, c), dst_ref=half(k, slot, c),
                send_sem=ici_send.at[k * 3 + p], recv_sem=ici_recv.at[k * 3 + p],
                device_id=(px, py, c), device_id_type=MESH)

        def over_d2d(k, p, which):
            px, py = chips[p]
            return pltpu.make_async_remote_copy(
                src_ref=half(k, 2 * px + py, which), dst_ref=half(k, 2 * px + py, which),
                send_sem=d2d_send.at[k * 3 + p], recv_sem=d2d_recv.at[k * 3 + p],
                device_id=(x, y, 1 - c), device_id_type=MESH)

        return over_ici, over_d2d, 2 * x + y, chips, c

    pairs = [(k, p) for k in range(n) for p in peers]

    def start(_, buf, sems):
        over_ici, _, mine, _, _ = copies(buf, sems)
        for k, p in pairs:
            over_ici(k, p, mine).start()

    def mid(_, buf, sems):
        over_ici, over_d2d, _, chips, c = copies(buf, sems)
        for k, p in pairs:
            px, py = chips[p]
            over_ici(k, p, 2 * px + py).wait_recv()
            over_d2d(k, p, c).start()

    def finish(_, buf, sems):
        over_ici, over_d2d, mine, _, c = copies(buf, sems)
        for k, p in pairs:
            over_d2d(k, p, 1 - c).wait_recv()
        for k, p in pairs:
            over_ici(k, p, mine).wait_send()
            over_d2d(k, p, c).wait_send()

    return _Job(slots, True, [], [pltpu.SemaphoreType.DMA((3 * n,))] * 4, start, mid, finish)


def _gather_shards(shards):
    n = len(shards)

    def body(*refs):
        src, dst = refs[:n], refs[n:2 * n]
        send_sems, recv_sems, local_sems = refs[2 * n:]
        x, y, c = _position()
        mine = 2 * x + y
        chips = _other_chips(x, y)

        def copy(k, p):
            return pltpu.make_async_remote_copy(
                src_ref=src[k], dst_ref=dst[k].at[mine],
                send_sem=send_sems.at[k * 3 + p], recv_sem=recv_sems.at[k * 3 + p],
                device_id=(*chips[p], c), device_id_type=MESH)

        def arrival(k, p):
            px, py = chips[p]
            return pltpu.make_async_remote_copy(
                src_ref=src[k], dst_ref=dst[k].at[2 * px + py],
                send_sem=send_sems.at[k * 3 + p], recv_sem=recv_sems.at[k * 3 + p],
                device_id=(px, py, c), device_id_type=MESH)

        local = [pltpu.make_async_copy(src[k], dst[k].at[mine], local_sems.at[k]) for k in range(n)]
        for cp in local:
            cp.start()
        for k in range(n):
            for p in range(3):
                copy(k, p).start()
        for k in range(n):
            for p in range(3):
                arrival(k, p).wait_recv()
        for k in range(n):
            for p in range(3):
                copy(k, p).wait_send()
        for cp in local:
            cp.wait()

    return pl.pallas_call(
        body, name="gather_shards",
        in_specs=[HBM_SPEC] * n, out_specs=[HBM_SPEC] * n,
        out_shape=[jax.ShapeDtypeStruct((N_SHARD,) + s.shape, s.dtype) for s in shards],
        scratch_shapes=[pltpu.SemaphoreType.DMA((3 * n,)), pltpu.SemaphoreType.DMA((3 * n,)),
                        pltpu.SemaphoreType.DMA((n,))],
    )(*shards)


def _swap_job(grads):
    n = len(grads)

    def copies(src, dst, sems):
        x, y, c = _position()
        return [pltpu.make_async_remote_copy(
            src_ref=src[k].at[:, 1 - c], dst_ref=dst[k],
            send_sem=sems[0].at[k], recv_sem=sems[1].at[k],
            device_id=(x, y, 1 - c), device_id_type=MESH) for k in range(n)]

    def start(src, dst, sems):
        for cp in copies(src, dst, sems):
            cp.start()

    def finish(src, dst, sems):
        for cp in copies(src, dst, sems):
            cp.wait()

    return _Job(grads, False, [jax.ShapeDtypeStruct((N_SHARD,) + g.shape[2:], F32) for g in grads],
                [pltpu.SemaphoreType.DMA((n,))] * 2, start, lambda *_: None, finish)


def _send_job(arrays):
    n = len(arrays)

    def copies(src, dst, sems):
        x, y, c = _position()
        return [pltpu.make_async_remote_copy(
            src_ref=src[k], dst_ref=dst[k], send_sem=sems[0].at[k], recv_sem=sems[1].at[k],
            device_id=(x, y, 1 - c), device_id_type=MESH) for k in range(n)]

    def start(src, dst, sems):
        for cp in copies(src, dst, sems):
            cp.start()

    def finish(src, dst, sems):
        for cp in copies(src, dst, sems):
            cp.wait()

    return _Job(arrays, False, [jax.ShapeDtypeStruct(a.shape, a.dtype) for a in arrays],
                [pltpu.SemaphoreType.DMA((n,))] * 2, start, lambda *_: None, finish)


def _chip_partial(a, y, c, j, tag):
    _, _, R, C = a.shape
    tr = _pick(R, (256, 64))

    def body(s_ref, a_ref, y_ref, pb_ref, po_ref):
        total = a_ref[...] + y_ref[...]
        pb_ref[...] = total.astype(BF16)

        @pl.when(pl.program_id(1) == s_ref[1])
        def _():
            po_ref[...] = total

    grid_spec = pltpu.PrefetchScalarGridSpec(
        num_scalar_prefetch=1, grid=(R // tr, N_SHARD),
        in_specs=[pl.BlockSpec((None, None, tr, C), lambda t, s, sc: (s, sc[0], t, 0)),
                  pl.BlockSpec((None, tr, C), lambda t, s, sc: (s, t, 0))],
        out_specs=[pl.BlockSpec((None, tr, C), lambda t, s, sc: (s, t, 0)),
                   pl.BlockSpec((tr, C), lambda t, s, sc: (t, 0))])
    return pl.pallas_call(
        body, name=f"chip_partial_{tag}", grid_spec=grid_spec,
        out_shape=[jax.ShapeDtypeStruct((N_SHARD, R, C), BF16), jax.ShapeDtypeStruct((R, C), F32)],
        compiler_params=_cp("arbitrary", "arbitrary"),
    )(jnp.stack([c, j]).astype(jnp.int32), a, y)


def _scatter_job(parts):
    n = len(parts)
    pairs = [(k, p) for k in range(n) for p in range(3)]

    def copy(src, dst, sems, k, p, outgoing):
        x, y, c = _position()
        mine = 2 * x + y
        px, py = _other_chips(x, y)[p]
        theirs = 2 * px + py
        return pltpu.make_async_remote_copy(
            src_ref=src[k].at[theirs if outgoing else mine], dst_ref=dst[k].at[mine if outgoing else theirs],
            send_sem=sems[0].at[k * 3 + p], recv_sem=sems[1].at[k * 3 + p],
            device_id=(px, py, c), device_id_type=MESH)

    def start(src, dst, sems):
        for k, p in pairs:
            copy(src, dst, sems, k, p, True).start()

    def finish(src, dst, sems):
        for k, p in pairs:
            copy(src, dst, sems, k, p, False).wait_recv()
        for k, p in pairs:
            copy(src, dst, sems, k, p, True).wait_send()

    return _Job(parts, False, [jax.ShapeDtypeStruct(pb.shape, BF16) for pb in parts],
                [pltpu.SemaphoreType.DMA((3 * n,))] * 2, start, lambda *_: None, finish)


def _shard_total(own, z, others_c, tag):
    R, C = own.shape
    tr = _pick(R, (256, 64))

    def body(s_ref, o_ref, z0_ref, z1_ref, z2_ref, h_ref):
        h_ref[...] = ((o_ref[...] + z0_ref[...].astype(F32)) + z1_ref[...].astype(F32)) + z2_ref[...].astype(F32)

    zspec = lambda q: pl.BlockSpec((None, tr, C), lambda t, sc: (sc[q], t, 0))
    grid_spec = pltpu.PrefetchScalarGridSpec(
        num_scalar_prefetch=1, grid=(R // tr,),
        in_specs=[pl.BlockSpec((tr, C), lambda t, sc: (t, 0)), zspec(0), zspec(1), zspec(2)],
        out_specs=pl.BlockSpec((None, tr, C), lambda t, sc: (sc[3], t, 0)))
    return pl.pallas_call(
        body, name=f"shard_total_{tag}", grid_spec=grid_spec,
        out_shape=jax.ShapeDtypeStruct((2, R, C), F32),
        compiler_params=_cp("arbitrary"),
    )(others_c, own, z, z, z)


def _share_job(totals):
    n = len(totals)

    def copy(buf, sems, k, which):
        x, y, c = _position()
        return pltpu.make_async_remote_copy(
            src_ref=buf[k].at[which], dst_ref=buf[k].at[which],
            send_sem=sems[0].at[k], recv_sem=sems[1].at[k],
            device_id=(x, y, 1 - c), device_id_type=MESH)

    def start(_, buf, sems):
        c = lax.axis_index("c")
        for k in range(n):
            copy(buf, sems, k, c).start()

    def finish(_, buf, sems):
        c = lax.axis_index("c")
        for k in range(n):
            copy(buf, sems, k, 1 - c).wait_recv()
        for k in range(n):
            copy(buf, sems, k, c).wait_send()

    return _Job(totals, True, [], [pltpu.SemaphoreType.DMA((n,))] * 2, start, lambda *_: None, finish)


def _spread_job(pack):
    def copy(src, dst, sems, m, outgoing):
        x, y, c = _position()
        peer = (x ^ (m >> 2), y ^ ((m >> 1) & 1), c ^ (m & 1))
        slot = 4 * x + 2 * y + c if outgoing else 4 * peer[0] + 2 * peer[1] + peer[2]
        return pltpu.make_async_remote_copy(
            src_ref=src[0], dst_ref=dst[0].at[slot], send_sem=sems[0].at[m - 1], recv_sem=sems[1].at[m - 1],
            device_id=peer, device_id_type=MESH)

    def start(src, dst, sems):
        for m in range(1, N_DEV):
            copy(src, dst, sems, m, True).start()

    def finish(src, dst, sems):
        for m in range(1, N_DEV):
            copy(src, dst, sems, m, False).wait_recv()
        for m in range(1, N_DEV):
            copy(src, dst, sems, m, True).wait_send()

    return _Job([pack], False, [jax.ShapeDtypeStruct((N_DEV,) + pack.shape, F32)],
                [pltpu.SemaphoreType.DMA((N_DEV - 1,))] * 2, start, lambda *_: None, finish)


def _join_jobs(a, b):
    assert a.aliased == b.aliased and not (a.aliased and (a.extra_out or b.extra_out))
    n_in, n_out, n_sem = len(a.inputs), len(a.out_shapes()), len(a.sems)

    def phase(name):
        def run(ins, outs, sems):
            getattr(a, name)(ins[:n_in], outs[:n_out], sems[:n_sem])
            getattr(b, name)(ins[n_in:], outs[n_out:], sems[n_sem:])
        return run

    return _Job(a.inputs + b.inputs, a.aliased, a.extra_out + b.extra_out, a.sems + b.sems,
                phase("start"), phase("mid"), phase("finish"))


def _sum_slots(pack, slots, me, tag):
    def body(me_ref, p_ref, s_ref, o_ref):
        acc = None
        for d in range(N_DEV):
            term = jnp.where(me_ref[0] == d, p_ref[...], s_ref[d])
            acc = term if acc is None else acc + term
        o_ref[...] = acc

    vm = pl.BlockSpec(memory_space=pltpu.VMEM)
    return pl.pallas_call(
        body, name=f"sum_slots_{tag}",
        in_specs=[pl.BlockSpec(memory_space=pltpu.SMEM), vm, vm], out_specs=vm,
        out_shape=jax.ShapeDtypeStruct(pack.shape, F32),
        compiler_params=pltpu.CompilerParams(vmem_limit_bytes=V7X_VMEM_LIMIT),
    )(jnp.reshape(me, (1,)).astype(jnp.int32), pack, slots)


def _pack_rows(arrays):
    total = sum(a.size for a in arrays)
    rows = -(-total // 128)
    rows = -(-rows // PACK_ROWS_ALIGN) * PACK_ROWS_ALIGN
    flat = [a.reshape(-1) for a in arrays] + [jnp.zeros((rows * 128 - total,), F32)]
    return jnp.concatenate(flat).reshape(rows, 128)


def _adamw_math(w, g, m, v):
    m = ADAM_B1 * m + (1.0 - ADAM_B1) * g
    v = ADAM_B2 * v + (1.0 - ADAM_B2) * (g * g)
    m_hat = m / (1.0 - ADAM_B1 ** ADAM_STEP)
    v_hat = v / (1.0 - ADAM_B2 ** ADAM_STEP)
    delta = -ADAM_LR * (m_hat / (jnp.sqrt(v_hat) + ADAM_EPS) + ADAM_WD * w)
    return delta, m, v


def _adamw_big(w, g0, g1, m, v, tag):
    _, R, C = w.shape
    tr = _pick(R, (256, 128))

    def body(w_ref, g0_ref, g1_ref, m_ref, v_ref, go_ref, d_ref, mo_ref, vo_ref):
        g = jnp.where(pl.program_id(0) == 0, g0_ref[...], g1_ref[...])
        delta, mn, vn = _adamw_math(w_ref[...], g, m_ref[...], v_ref[...])
        go_ref[...] = g
        d_ref[...] = delta
        mo_ref[...] = mn
        vo_ref[...] = vn

    s3 = pl.BlockSpec((None, tr, C), lambda l, t: (l, t, 0))
    s2 = pl.BlockSpec((tr, C), lambda l, t: (t, 0))
    shp = jax.ShapeDtypeStruct(w.shape, F32)
    return pl.pallas_call(
        body, name=f"adamw_{tag}", grid=(2, R // tr),
        in_specs=[s3, s2, s2, s3, s3], out_specs=[s3, s3, s3, s3],
        out_shape=[shp, shp, shp, shp],
        compiler_params=_cp("parallel", "parallel"),
    )(w, g0, g1, m, v)


def _adamw_small(ws, gs, ms, vs):
    n = len(ws)

    def body(*refs):
        w_r, g_r, m_r, v_r = refs[:n], refs[n:2 * n], refs[2 * n:3 * n], refs[3 * n:4 * n]
        d_o, m_o, v_o = refs[4 * n:5 * n], refs[5 * n:6 * n], refs[6 * n:7 * n]
        for k in range(n):
            delta, mn, vn = _adamw_math(w_r[k][...], g_r[k][...], m_r[k][...], v_r[k][...])
            d_o[k][...] = delta
            m_o[k][...] = mn
            v_o[k][...] = vn

    vm = pl.BlockSpec(memory_space=pltpu.VMEM)
    shapes = [jax.ShapeDtypeStruct(w.shape, F32) for w in ws]
    outs = pl.pallas_call(
        body, name="adamw_small",
        in_specs=[vm] * (4 * n), out_specs=[vm] * (3 * n),
        out_shape=shapes * 3,
    )(*ws, *gs, *ms, *vs)
    return outs[:n], outs[n:2 * n], outs[2 * n:]


_WEIGHTS = ["meta_tokens", "ln_in_g", "ln_in_b", "w_in", "conv_dw_w", "conv_dw_b", "conv_ln_g", "conv_ln_b",
            "conv_pw_w", "conv_pw_b", "attn_sinks", "lru_conv_w", "lru_conv_b", "lru_wa", "lru_ba", "lru_wx",
            "lru_bx", "lru_lambda", "w_out", "ln_post_g", "ln_post_b"]
_BIG = ("w_in", "w_out", "conv_pw_w")
_SMALL_SHARDED = {"meta_tokens": 1, "conv_dw_w": 2, "lru_conv_w": 2}
PACK_ROWS_ALIGN = 8


def _as2d(a):
    return a.reshape(1, -1) if a.ndim == 1 else a.reshape(-1, a.shape[-1])


def kernel(x, meta_tokens, ln_in_g, ln_in_b, w_in, conv_dw_w, conv_dw_b, conv_ln_g, conv_ln_b, conv_pw_w, conv_pw_b, attn_sinks, lru_conv_w, lru_conv_b, lru_wa, lru_ba, lru_wx, lru_bx, lru_lambda, w_out, ln_post_g, ln_post_b, loss_target, m_meta_tokens, m_ln_in_g, m_ln_in_b, m_w_in, m_conv_dw_w, m_conv_dw_b, m_conv_ln_g, m_conv_ln_b, m_conv_pw_w, m_conv_pw_b, m_attn_sinks, m_lru_conv_w, m_lru_conv_b, m_lru_wa, m_lru_ba, m_lru_wx, m_lru_bx, m_lru_lambda, m_w_out, m_ln_post_g, m_ln_post_b, v_meta_tokens, v_ln_in_g, v_ln_in_b, v_w_in, v_conv_dw_w, v_conv_dw_b, v_conv_ln_g, v_conv_ln_b, v_conv_pw_w, v_conv_pw_b, v_attn_sinks, v_lru_conv_w, v_lru_conv_b, v_lru_wa, v_lru_ba, v_lru_wx, v_lru_bx, v_lru_lambda, v_w_out, v_ln_post_g, v_ln_post_b):
    w = dict(meta_tokens=meta_tokens, ln_in_g=ln_in_g, ln_in_b=ln_in_b, w_in=w_in, conv_dw_w=conv_dw_w,
             conv_dw_b=conv_dw_b, conv_ln_g=conv_ln_g, conv_ln_b=conv_ln_b, conv_pw_w=conv_pw_w,
             conv_pw_b=conv_pw_b, attn_sinks=attn_sinks, lru_conv_w=lru_conv_w, lru_conv_b=lru_conv_b,
             lru_wa=lru_wa, lru_ba=lru_ba, lru_wx=lru_wx, lru_bx=lru_bx, lru_lambda=lru_lambda, w_out=w_out,
             ln_post_g=ln_post_g, ln_post_b=ln_post_b)
    mom_m = dict(zip(_WEIGHTS, (m_meta_tokens, m_ln_in_g, m_ln_in_b, m_w_in, m_conv_dw_w, m_conv_dw_b, m_conv_ln_g,
                                m_conv_ln_b, m_conv_pw_w, m_conv_pw_b, m_attn_sinks, m_lru_conv_w, m_lru_conv_b,
                                m_lru_wa, m_lru_ba, m_lru_wx, m_lru_bx, m_lru_lambda, m_w_out, m_ln_post_g,
                                m_ln_post_b)))
    mom_v = dict(zip(_WEIGHTS, (v_meta_tokens, v_ln_in_g, v_ln_in_b, v_w_in, v_conv_dw_w, v_conv_dw_b, v_conv_ln_g,
                                v_conv_ln_b, v_conv_pw_w, v_conv_pw_b, v_attn_sinks, v_lru_conv_w, v_lru_conv_b,
                                v_lru_wa, v_lru_ba, v_lru_wx, v_lru_bx, v_lru_lambda, v_w_out, v_ln_post_g,
                                v_ln_post_b)))
    xi, yi, ci = _position()
    j = 2 * xi + yi

    g_meta, g_dw, g_lc = _gather_shards([meta_tokens, conv_dw_w, lru_conv_w])
    p = dict(w)
    p["w_in"] = [_cast_into_slot(w_in, l, j, "w_in") for l in range(DEPTH)]
    p["w_out"] = [_cast_into_slot(w_out, l, j, "w_out") for l in range(DEPTH)]
    p["conv_pw_w"] = [_cast_into_slot(conv_pw_w, l, j, "conv_pw_w") for l in range(DEPTH)]
    p["meta_tokens"] = g_meta.transpose(1, 0, 2).reshape(N_META, D)
    p["conv_dw_w"] = g_dw.transpose(1, 2, 0, 3).reshape(DEPTH, CONV_K, CW)
    p["lru_conv_w"] = g_lc.transpose(1, 2, 0, 3).reshape(DEPTH, LRU_K, LW)

    others = jnp.stack([jnp.where(j <= 0, 1, 0), jnp.where(j <= 1, 2, 1), jnp.where(j <= 2, 3, 2), ci]).astype(jnp.int32)
    me = 4 * xi + 2 * yi + ci
    order = jnp.stack([j, 2 * (1 - xi) + yi, 2 * xi + (1 - yi), 2 * (1 - xi) + (1 - yi)]).astype(jnp.int32)
    loss_part, grad_x, g = _device_step(x[0], loss_target[0], p, dist=(ci, j, others, me, order))
    loss = lax.psum(jnp.sum(loss_part), ("x", "y", "c"))
    big = {(name, l): g[name, l] for name in _BIG for l in range(DEPTH)}

    small_names = [n for n in _WEIGHTS if n not in _BIG]
    small_g = {}
    for names, red in ((_SMALL_LAYERED, g["pack_layered", -1]), (_SMALL_EMBED, g["pack_embed", -1])):
        red = red.reshape(-1)
        off = 0
        for n in names:
            fshape = list(w[n].shape)
            if n in _SMALL_SHARDED:
                fshape[_SMALL_SHARDED[n]] *= N_SHARD
            sz = 1
            for dim in fshape:
                sz *= dim
            full = red[off:off + sz].reshape(fshape)
            off += sz
            if n in _SMALL_SHARDED:
                ax = _SMALL_SHARDED[n]
                full = lax.dynamic_slice_in_dim(full, j * w[n].shape[ax], w[n].shape[ax], axis=ax)
            small_g[n] = full

    out_g, out_d, out_m, out_v = {}, {}, {}, {}
    for name in _BIG:
        shp = w[name].shape
        to3 = lambda a: a.reshape(DEPTH, -1, shp[-1])
        go, do, mo, vo = _adamw_big(to3(w[name]), big[name, 0], big[name, 1], to3(mom_m[name]), to3(mom_v[name]), name)
        out_g[name], out_d[name], out_m[name], out_v[name] = (a.reshape(shp) for a in (go, do, mo, vo))
    ds, ms, vs = _adamw_small([_as2d(w[n]) for n in small_names], [_as2d(small_g[n]) for n in small_names],
                              [_as2d(mom_m[n]) for n in small_names], [_as2d(mom_v[n]) for n in small_names])
    for n, d_, m_, v_ in zip(small_names, ds, ms, vs):
        out_g[n] = small_g[n]
        out_d[n], out_m[n], out_v[n] = d_.reshape(w[n].shape), m_.reshape(w[n].shape), v_.reshape(w[n].shape)

    return (loss, grad_x[None], *[out_g[n] for n in _WEIGHTS], *[out_d[n] for n in _WEIGHTS],
            *[out_m[n] for n in _WEIGHTS], *[out_v[n] for n in _WEIGHTS])
```

```python
import functools

import jax
import jax.numpy as jnp
from jax import lax
from jax.experimental import pallas as pl
from jax.experimental.pallas import tpu as pltpu

F32 = jnp.float32
BF16 = jnp.bfloat16

D = 2048
N_META = 16
CW = 512
CONV_K = 31
AW = 1024
KVW = 256
N_HEADS = 16
LW = 512
LRU_K = 4
LRU_C = 8.0
IN_TOTAL = 5120
ROT_HALF = 8
ROPE_THETA = 500000.0
LN_EPS = 1e-5
DEPTH = 2
ALPHA = (2.0 * DEPTH) ** 0.25
NEG_INF = -1e30
ADAM_LR, ADAM_B1, ADAM_B2, ADAM_EPS, ADAM_WD, ADAM_STEP = 0.001, 0.9, 0.999, 1e-08, 0.01, 10

BLK = 128
PAD = BLK - N_META
N_SHARD = 4
WIN_SH = IN_TOTAL // N_SHARD
WOUT_SH = D // N_SHARD
PW_SH = CW // N_SHARD
HALO = 32
LHALO = 8
V7X_VMEM_LIMIT = 60 * 1024 * 1024


def _cp(*sem):
    return pltpu.CompilerParams(dimension_semantics=sem if sem else None, vmem_limit_bytes=V7X_VMEM_LIMIT)


def _pick(total, prefs):
    for p in prefs:
        if total % p == 0:
            return p
    raise ValueError(f"no tile for {total}")


def _dot(a, b):
    return jnp.dot(a, b, preferred_element_type=F32)


def _dot_nt(a, b):
    return lax.dot_general(a, b, (((1,), (1,)), ((), ())), preferred_element_type=F32)


def _dot_tn(a, b):
    return lax.dot_general(a, b, (((0,), (0,)), ((), ())), preferred_element_type=F32)


def _sigmoid(x):
    return 1.0 / (1.0 + jnp.exp(-x))


def _silu_and_grad(x):
    s = _sigmoid(x)
    return x * s, s * (1.0 + x * (1.0 - s))


def _ln_rows(x, g, b):
    mu = jnp.mean(x, axis=-1, keepdims=True)
    xc = x - mu
    var = jnp.mean(xc * xc, axis=-1, keepdims=True)
    rstd = lax.rsqrt(var + LN_EPS)
    xhat = xc * rstd
    return xhat * g + b, xhat, rstd


def _ln_bwd_rows(dy, xhat, rstd, g):
    dxh = dy * g
    m1 = jnp.mean(dxh, axis=-1, keepdims=True)
    m2 = jnp.mean(dxh * xhat, axis=-1, keepdims=True)
    return rstd * (dxh - m1 - xhat * m2)


def _row_ids(n, base):
    return base + lax.broadcasted_iota(jnp.int32, (n, 1), 0)


def _colsum(x):
    return jnp.sum(x, axis=0, keepdims=True)


def _embed_fwd(x, meta, g, b, job=None):
    S = x.shape[0]
    nb = S // BLK + 1

    def body(x_ref, meta_ref, g_ref, b_ref, h_ref, hb_ref):
        n = pl.program_id(0)

        @pl.when(n == 0)
        def _():
            y, _, _ = _ln_rows(meta_ref[...], g_ref[...], b_ref[...])
            h_ref[...] = jnp.zeros_like(h_ref)
            h_ref[PAD:BLK, :] = y

        @pl.when(n > 0)
        def _():
            y, _, _ = _ln_rows(x_ref[...], g_ref[...], b_ref[...])
            h_ref[...] = y

        hb_ref[...] = h_ref[...].astype(BF16)

    return _side_call(
        body, job, name="embed_fwd", grid=(nb,),
        in_specs=[pl.BlockSpec((BLK, D), lambda n: (jnp.maximum(n - 1, 0), 0)),
                  pl.BlockSpec((N_META, D), lambda n: (0, 0)),
                  pl.BlockSpec((1, D), lambda n: (0, 0)),
                  pl.BlockSpec((1, D), lambda n: (0, 0))],
        out_specs=[pl.BlockSpec((BLK, D), lambda n: (n, 0)),
                   pl.BlockSpec((BLK, D), lambda n: (n, 0))],
        out_shape=[jax.ShapeDtypeStruct((nb * BLK, D), F32), jax.ShapeDtypeStruct((nb * BLK, D), BF16)],
        scratch_shapes=[], semantics=("arbitrary",), args=[x, meta, g, b])


def _embed_bwd(dh, x, meta, g, b):
    S = x.shape[0]
    nb = S // BLK + 1

    def body(dh_ref, x_ref, meta_ref, g_ref, b_ref, gx_ref, gm_ref, dg_ref, db_ref):
        n = pl.program_id(0)

        @pl.when(n == 0)
        def _():
            _, xhat, rstd = _ln_rows(meta_ref[...], g_ref[...], b_ref[...])
            dy = dh_ref[PAD:BLK, :]
            gm_ref[...] = _ln_bwd_rows(dy, xhat, rstd, g_ref[...])
            dg_ref[...] = _colsum(dy * xhat)
            db_ref[...] = _colsum(dy)

        @pl.when(n > 0)
        def _():
            _, xhat, rstd = _ln_rows(x_ref[...], g_ref[...], b_ref[...])
            dy = dh_ref[...]
            gx_ref[...] = _ln_bwd_rows(dy, xhat, rstd, g_ref[...])
            dg_ref[...] += _colsum(dy * xhat)
            db_ref[...] += _colsum(dy)

    prev = lambda n: (jnp.maximum(n - 1, 0), 0)
    const = lambda n: (0, 0)
    return pl.pallas_call(
        body, name="embed_bwd", grid=(nb,),
        in_specs=[pl.BlockSpec((BLK, D), lambda n: (n, 0)),
                  pl.BlockSpec((BLK, D), prev),
                  pl.BlockSpec((N_META, D), const),
                  pl.BlockSpec((1, D), const),
                  pl.BlockSpec((1, D), const)],
        out_specs=[pl.BlockSpec((BLK, D), prev),
                   pl.BlockSpec((N_META, D), const),
                   pl.BlockSpec((1, D), const),
                   pl.BlockSpec((1, D), const)],
        out_shape=[jax.ShapeDtypeStruct((S, D), F32), jax.ShapeDtypeStruct((N_META, D), F32),
                   jax.ShapeDtypeStruct((1, D), F32), jax.ShapeDtypeStruct((1, D), F32)],
        compiler_params=_cp("arbitrary"),
    )(dh, x, meta, g, b)


def _proj_fwd(hb, w_in, order, first, count, prev, l, job=None):
    T = hb.shape[0]
    tm = _pick(T, (1056, 384, 128))

    def body(o_sc, a_ref, w_ref, *rest):
        rest[-1][...] = _dot(a_ref[...], w_ref[...])

    return _side_call(
        body, job, name=f"proj_fwd{l}_{first}", grid=(T // tm, count),
        in_specs=[pl.BlockSpec((tm, D), lambda i, j, o: (i, 0)),
                  pl.BlockSpec((None, D, WIN_SH), lambda i, j, o: (o[first + j], 0, 0))]
        + ([] if prev is None else [pl.BlockSpec(memory_space=pl.ANY)]),
        out_specs=[pl.BlockSpec((tm, WIN_SH), lambda i, j, o: (i, o[first + j]))],
        out_shape=[jax.ShapeDtypeStruct((T, IN_TOTAL), F32)],
        scratch_shapes=[], semantics=("parallel", "arbitrary"),
        args=[hb, w_in] + ([] if prev is None else [prev]),
        aliases=None if prev is None else {2: 0}, prefetch=[order])


def _out_fwd(yc, ya, yl, w_out, h, g, b, l, job=None):
    T = h.shape[0]
    tm = _pick(T, (384, 128))

    def body(yc_ref, ya_ref, yl_ref, w_ref, h_ref, g_ref, b_ref, hn_ref, hnb_ref, xh_ref, rs_ref):
        acc = _dot(yc_ref[...], w_ref[0])
        acc += _dot(ya_ref[:, 0:WOUT_SH], w_ref[1])
        acc += _dot(ya_ref[:, WOUT_SH:2 * WOUT_SH], w_ref[2])
        acc += _dot(yl_ref[...], w_ref[3])
        z = ALPHA * h_ref[...] + acc
        y, xhat, rstd = _ln_rows(z, g_ref[...], b_ref[...])
        hn_ref[...] = y
        hnb_ref[...] = y.astype(BF16)
        xh_ref[...] = xhat
        rs_ref[...] = rstd

    row = lambda i: (i, 0)
    return _side_call(
        body, job, name=f"out_fwd{l}", grid=(T // tm,),
        in_specs=[pl.BlockSpec((tm, CW), row), pl.BlockSpec((tm, AW), row), pl.BlockSpec((tm, LW), row),
                  pl.BlockSpec((N_SHARD, WOUT_SH, D), lambda i: (0, 0, 0)),
                  pl.BlockSpec((tm, D), row),
                  pl.BlockSpec((None, 1, D), lambda i: (l, 0, 0)),
                  pl.BlockSpec((None, 1, D), lambda i: (l, 0, 0))],
        out_specs=[pl.BlockSpec((tm, D), row), pl.BlockSpec((tm, D), row), pl.BlockSpec((tm, D), row),
                   pl.BlockSpec((tm, 1), row)],
        out_shape=[jax.ShapeDtypeStruct((T, D), F32), jax.ShapeDtypeStruct((T, D), BF16),
                   jax.ShapeDtypeStruct((T, D), F32), jax.ShapeDtypeStruct((T, 1), F32)],
        scratch_shapes=[], semantics=("parallel",), args=[yc, ya, yl, w_out, h, g, b])


def _post_ln_dcat_bwd(src, target, xhat, rstd, g, w_out, l, job=None):
    T = src.shape[0]
    tm = _pick(T, (384, 128))
    per = tm // BLK if target is not None else 0
    last_blk = target.shape[0] // BLK - 1 if target is not None else 0

    def body(s_ref, *refs):
        t_refs = refs[:per]
        (xh_ref, rs_ref, g_ref, w_ref, part_ref, dz_ref, dzb_ref, dg_ref, db_ref, dc_ref, da_ref, dl_ref) = refs[per:]
        i = pl.program_id(0)

        @pl.when(i == 0)
        def _():
            part_ref[...] = jnp.zeros_like(part_ref)
            dg_ref[...] = jnp.zeros_like(dg_ref)
            db_ref[...] = jnp.zeros_like(db_ref)

        if per:
            tgt = jnp.concatenate([r[...] for r in t_refs], axis=0) if per > 1 else t_refs[0][...]
            real = _row_ids(tm, i * tm) >= BLK
            err = jnp.where(real, s_ref[...] - tgt, 0.0)
            part_ref[...] += _colsum(err * err) * (0.5 / D)
            dy = err * (1.0 / D)
        else:
            dy = s_ref[...]
        xhat = xh_ref[...]
        dz = _ln_bwd_rows(dy, xhat, rs_ref[...], g_ref[...])
        dzb = dz.astype(BF16)
        dz_ref[...] = dz
        dzb_ref[...] = dzb
        dg_ref[...] += _colsum(dy * xhat)
        db_ref[...] += _colsum(dy)
        dc_ref[...] = _dot_nt(dzb, w_ref[0])
        da_ref[:, 0:WOUT_SH] = _dot_nt(dzb, w_ref[1])
        da_ref[:, WOUT_SH:2 * WOUT_SH] = _dot_nt(dzb, w_ref[2])
        dl_ref[...] = _dot_nt(dzb, w_ref[3])

    row = lambda i: (i, 0)
    const = lambda i: (0, 0)
    t_specs = [pl.BlockSpec((BLK, D), functools.partial(lambda i, q: (jnp.clip(i * per - 1 + q, 0, last_blk), 0), q=q))
               for q in range(per)]
    return _side_call(
        body, job, name=f"post_ln_dcat_bwd{l}", grid=(T // tm,),
        in_specs=[pl.BlockSpec((tm, D), row)] + t_specs + [
            pl.BlockSpec((tm, D), row), pl.BlockSpec((tm, 1), row), pl.BlockSpec((None, 1, D), lambda i: (l, 0, 0)),
            pl.BlockSpec((N_SHARD, WOUT_SH, D), lambda i: (0, 0, 0))],
        out_specs=[pl.BlockSpec((1, D), const), pl.BlockSpec((tm, D), row), pl.BlockSpec((tm, D), row),
                   pl.BlockSpec((1, D), const), pl.BlockSpec((1, D), const),
                   pl.BlockSpec((tm, CW), row), pl.BlockSpec((tm, AW), row), pl.BlockSpec((tm, LW), row)],
        out_shape=[jax.ShapeDtypeStruct((1, D), F32), jax.ShapeDtypeStruct((T, D), F32),
                   jax.ShapeDtypeStruct((T, D), BF16), jax.ShapeDtypeStruct((1, D), F32),
                   jax.ShapeDtypeStruct((1, D), F32), jax.ShapeDtypeStruct((T, CW), F32),
                   jax.ShapeDtypeStruct((T, AW), F32), jax.ShapeDtypeStruct((T, LW), F32)],
        scratch_shapes=[], semantics=("arbitrary",),
        args=[src] + [target] * per + [xhat, rstd, g, w_out])


def _dwout_bwd(yc, ya, yl, dzb, l):
    T = dzb.shape[0]
    tm = _pick(T, (384, 128))

    def body(yc_ref, ya_ref, yl_ref, dz_ref, o_ref):
        @pl.when(pl.program_id(0) == 0)
        def _():
            o_ref[...] = jnp.zeros_like(o_ref)

        cat = jnp.concatenate([yc_ref[...], ya_ref[...], yl_ref[...]], axis=1)
        o_ref[...] += _dot_tn(cat, dz_ref[...])

    row = lambda t: (t, 0)
    out = pl.pallas_call(
        body, name=f"dwout_bwd{l}", grid=(T // tm,),
        in_specs=[pl.BlockSpec((tm, CW), row), pl.BlockSpec((tm, AW), row), pl.BlockSpec((tm, LW), row),
                  pl.BlockSpec((tm, D), row)],
        out_specs=pl.BlockSpec((D, D), lambda t: (0, 0)),
        out_shape=jax.ShapeDtypeStruct((D, D), F32),
        compiler_params=_cp("arbitrary"),
    )(yc, ya, yl, dzb)
    return out.reshape(N_SHARD, 2, WOUT_SH // 2, D)


def _dh_bwd(dproj, w_in, dz, l, job=None):
    T = dproj.shape[0]
    tm = _pick(T, (1056, 384, 128))

    n_w = len(w_in)

    def body(dp_ref, *refs):
        w_refs, (dz_ref, o_ref, acc_ref) = refs[:n_w], refs[n_w:]
        j = pl.program_id(1)

        @pl.when(j == 0)
        def _():
            acc_ref[...] = ALPHA * dz_ref[...]

        dp = dp_ref[...]
        off = 0
        for w_ref in w_refs:
            rows = w_ref.shape[0]
            acc_ref[:, off:off + rows] += _dot_nt(dp, w_ref[...])
            off += rows

        @pl.when(j == N_SHARD - 1)
        def _():
            o_ref[...] = acc_ref[...]

    return _side_call(
        body, job, name=f"dh_bwd{l}", grid=(T // tm, N_SHARD),
        in_specs=[pl.BlockSpec((tm, WIN_SH), lambda i, j: (i, j))]
        + [pl.BlockSpec((None, w.shape[1], WIN_SH), lambda i, j: (j, 0, 0)) for w in w_in]
        + [pl.BlockSpec((tm, D), lambda i, j: (i, 0))],
        out_specs=[pl.BlockSpec((tm, D), lambda i, j: (i, 0))],
        out_shape=[jax.ShapeDtypeStruct((T, D), F32)],
        scratch_shapes=[pltpu.VMEM((tm, D), F32)],
        semantics=("parallel", "arbitrary"), args=[dproj, *w_in, dz])


def _dwin_bwd(hb, dproj, l):
    T = hb.shape[0]
    tm = _pick(T, (1056, 384, 128))

    def body(h_ref, dp_ref, o_ref):
        @pl.when(pl.program_id(1) == 0)
        def _():
            o_ref[...] = jnp.zeros_like(o_ref)

        o_ref[...] += _dot_tn(h_ref[...], dp_ref[...])

    out = pl.pallas_call(
        body, name=f"dwin_bwd{l}", grid=(N_SHARD, T // tm),
        in_specs=[pl.BlockSpec((tm, D), lambda j, t: (t, 0)),
                  pl.BlockSpec((tm, WIN_SH), lambda j, t: (t, j))],
        out_specs=pl.BlockSpec((None, D, WIN_SH), lambda j, t: (j, 0, 0)),
        out_shape=jax.ShapeDtypeStruct((N_SHARD, D, WIN_SH), F32),
        compiler_params=_cp("parallel", "arbitrary"),
    )(hb, dproj)
    return out.reshape(N_SHARD, 2, D // 2, WIN_SH)


def _dwin_half(hb, dproj, which, l, tag, job=None):
    T = hb.shape[0]
    tm = _pick(T, (1056, 384, 128))
    hr = D // 2

    def body(w_ref, h_ref, dp_ref, o_ref):
        @pl.when(pl.program_id(1) == 0)
        def _():
            o_ref[...] = jnp.zeros_like(o_ref)

        o_ref[...] += _dot_tn(h_ref[...], dp_ref[...])

    return _side_call(
        body, job, name=f"dwin_{tag}{l}", grid=(N_SHARD, T // tm),
        in_specs=[pl.BlockSpec((tm, hr), lambda j, t, w: (t, w[0])),
                  pl.BlockSpec((tm, WIN_SH), lambda j, t, w: (t, j))],
        out_specs=[pl.BlockSpec((None, hr, WIN_SH), lambda j, t, w: (j, 0, 0))],
        out_shape=[jax.ShapeDtypeStruct((N_SHARD, hr, WIN_SH), F32)],
        scratch_shapes=[], semantics=("parallel", "arbitrary"), args=[hb, dproj],
        prefetch=[jnp.reshape(which, (1,)).astype(jnp.int32)])


def _glu_masked(v, g, base_row):
    rows = _row_ids(v.shape[0], base_row)
    return jnp.where(rows >= PAD, v * _sigmoid(g), 0.0)


def _conv_tile(T):
    return _pick(T, (384, 128))


SUBLANES = 8


def _for_each_shift(buf, rot, tm, offsets, fn):
    for r in range(SUBLANES):
        group = [o for o in offsets if o % SUBLANES == r]
        if not group:
            continue
        if r == 0:
            src = buf
        else:
            n = tm + max(group) - r
            rot[0:n, :] = buf[r:r + n, :]
            src = rot
        for o in group:
            fn(o, src[o - r:o - r + tm, :])


def _conv_fwd(proj, dw_w, dw_b, ln_g, ln_b, pw_w, pw_b, l, job=None):
    T = proj.shape[0]
    tm = _conv_tile(T)
    hb = tm // HALO

    def body(cv_ref, cg_ref, ct_ref, hv_ref, hg_ref, w_ref, b_ref, g_ref, be_ref, pw_ref, pb_ref,
             yc_ref, conv_ref, buf, rot):
        i = pl.program_id(0)
        buf[0:HALO, :] = _glu_masked(hv_ref[...], hg_ref[...], i * tm - HALO)
        buf[HALO:HALO + tm, :] = _glu_masked(cv_ref[...], cg_ref[...], i * tm)
        first = HALO - (CONV_K - 1)
        total = [jnp.zeros((tm, CW), F32) + b_ref[...]]

        def tap(o, tile):
            k = o - first
            total[0] = total[0] + w_ref[k:k + 1, :] * tile

        _for_each_shift(buf, rot, tm, [first + k for k in range(CONV_K)], tap)
        acc = total[0]
        conv_ref[...] = acc
        u, _, _ = _ln_rows(acc, g_ref[...], be_ref[...])
        s = u * _sigmoid(u)
        cpw = _dot(s.astype(BF16), pw_ref[...]) + pb_ref[...]
        gate, _ = _silu_and_grad(ct_ref[...])
        yc_ref[...] = (cpw * gate).astype(BF16)

    vec = pl.BlockSpec((None, 1, CW), lambda i: (l, 0, 0))
    return _side_call(
        body, job, name=f"conv_fwd{l}", grid=(T // tm,),
        in_specs=[pl.BlockSpec((tm, CW), lambda i: (i, 0)),
                  pl.BlockSpec((tm, CW), lambda i: (i, 1)),
                  pl.BlockSpec((tm, CW), lambda i: (i, 2)),
                  pl.BlockSpec((HALO, CW), lambda i: (jnp.maximum(i * hb - 1, 0), 0)),
                  pl.BlockSpec((HALO, CW), lambda i: (jnp.maximum(i * hb - 1, 0), 1)),
                  pl.BlockSpec((None, CONV_K, CW), lambda i: (l, 0, 0)),
                  vec, vec, vec,
                  pl.BlockSpec((CW, CW), lambda i: (0, 0)),
                  vec],
        out_specs=[pl.BlockSpec((tm, CW), lambda i: (i, 0)), pl.BlockSpec((tm, CW), lambda i: (i, 0))],
        out_shape=[jax.ShapeDtypeStruct((T, CW), BF16), jax.ShapeDtypeStruct((T, CW), F32)],
        scratch_shapes=[pltpu.VMEM((tm + HALO, CW), F32), pltpu.VMEM((tm + HALO, CW), F32)],
        semantics=("parallel",), args=[proj, proj, proj, proj, proj, dw_w, dw_b, ln_g, ln_b, pw_w, pw_b])


def _conv_bwd_rows(conv, proj, d_yc, ln_g, ln_b, pw_w, pw_b, l):
    T = conv.shape[0]
    tm = _conv_tile(T)

    def body(conv_ref, ct_ref, dy_ref, g_ref, be_ref, pw_ref, pb_ref,
             dconv_ref, dct_ref, dpw_ref, dpb_ref, dg_ref, db_ref):
        @pl.when(pl.program_id(0) == 0)
        def _():
            dpw_ref[...] = jnp.zeros_like(dpw_ref)
            dpb_ref[...] = jnp.zeros_like(dpb_ref)
            dg_ref[...] = jnp.zeros_like(dg_ref)
            db_ref[...] = jnp.zeros_like(db_ref)

        u, xhat, rstd = _ln_rows(conv_ref[...], g_ref[...], be_ref[...])
        s, ds_du = _silu_and_grad(u)
        sb = s.astype(BF16)
        cpw = _dot(sb, pw_ref[...]) + pb_ref[...]
        gate, dgate = _silu_and_grad(ct_ref[...])
        dy = dy_ref[...]
        d_cpw = dy * gate
        dct_ref[...] = (dy * cpw * dgate).astype(BF16)
        d_cpw_b = d_cpw.astype(BF16)
        dpb_ref[...] += _colsum(d_cpw)
        dpw_ref[...] += _dot_tn(sb, d_cpw_b)
        du = _dot_nt(d_cpw_b, pw_ref[...]) * ds_du
        dconv_ref[...] = _ln_bwd_rows(du, xhat, rstd, g_ref[...])
        dg_ref[...] += _colsum(du * xhat)
        db_ref[...] += _colsum(du)

    vec = pl.BlockSpec((None, 1, CW), lambda i: (l, 0, 0))
    row = lambda i: (i, 0)
    const = lambda i: (0, 0)
    return pl.pallas_call(
        body, name=f"conv_bwd_rows{l}", grid=(T // tm,),
        in_specs=[pl.BlockSpec((tm, CW), row), pl.BlockSpec((tm, CW), lambda i: (i, 2)),
                  pl.BlockSpec((tm, CW), row), vec, vec,
                  pl.BlockSpec((CW, CW), lambda i: (0, 0)), vec],
        out_specs=[pl.BlockSpec((tm, CW), row), pl.BlockSpec((tm, CW), lambda i: (i, 2)),
                   pl.BlockSpec((CW, CW), const), pl.BlockSpec((1, CW), const),
                   pl.BlockSpec((1, CW), const), pl.BlockSpec((1, CW), const)],
        out_shape=[jax.ShapeDtypeStruct((T, CW), F32), jax.ShapeDtypeStruct((T, IN_TOTAL), BF16),
                   jax.ShapeDtypeStruct((CW, CW), F32), jax.ShapeDtypeStruct((1, CW), F32),
                   jax.ShapeDtypeStruct((1, CW), F32), jax.ShapeDtypeStruct((1, CW), F32)],
        compiler_params=_cp("arbitrary"),
    )(conv, proj, d_yc, ln_g, ln_b, pw_w, pw_b)


def _conv_bwd_taps(d_conv, proj, dw_w, dproj, l, job=None):
    T = d_conv.shape[0]
    tm = _conv_tile(T)
    hb = tm // HALO
    nt = T // tm
    last_halo = T // HALO - 1

    def body(dc_ref, dh_ref, cv_ref, cg_ref, hv_ref, hg_ref, w_ref, _, o_ref, dw_ref, dwb_ref, cbuf, dbuf, rot):
        i = pl.program_id(0)

        @pl.when(i == 0)
        def _():
            dw_ref[...] = jnp.zeros_like(dw_ref)
            dwb_ref[...] = jnp.zeros_like(dwb_ref)

        cbuf[0:HALO, :] = _glu_masked(hv_ref[...], hg_ref[...], i * tm - HALO)
        cbuf[HALO:HALO + tm, :] = _glu_masked(cv_ref[...], cg_ref[...], i * tm)
        dmain = dc_ref[...]
        dbuf[0:tm, :] = dmain
        dbuf[tm:tm + HALO, :] = jnp.where(i < nt - 1, dh_ref[...], 0.0)
        total = [jnp.zeros((tm, CW), F32)]

        def tap_back(o, tile):
            k = CONV_K - 1 - o
            total[0] = total[0] + w_ref[k:k + 1, :] * tile

        _for_each_shift(dbuf, rot, tm, list(range(CONV_K)), tap_back)
        acc = total[0]
        first = HALO - (CONV_K - 1)

        def tap_weight(o, tile):
            k = o - first
            dw_ref[k:k + 1, :] += _colsum(dmain * tile)

        _for_each_shift(cbuf, rot, tm, [first + k for k in range(CONV_K)], tap_weight)
        dwb_ref[...] += _colsum(dmain)
        d_c = jnp.where(_row_ids(tm, i * tm) >= PAD, acc, 0.0)
        sig = _sigmoid(cg_ref[...])
        o_ref[:, 0:CW] = (d_c * sig).astype(BF16)
        o_ref[:, CW:2 * CW] = (d_c * cv_ref[...] * sig * (1.0 - sig)).astype(BF16)

    const = lambda i: (0, 0)
    return _side_call(
        body, job, name=f"conv_bwd_taps{l}", grid=(nt,),
        in_specs=[pl.BlockSpec((tm, CW), lambda i: (i, 0)),
                  pl.BlockSpec((HALO, CW), lambda i: (jnp.minimum((i + 1) * hb, last_halo), 0)),
                  pl.BlockSpec((tm, CW), lambda i: (i, 0)),
                  pl.BlockSpec((tm, CW), lambda i: (i, 1)),
                  pl.BlockSpec((HALO, CW), lambda i: (jnp.maximum(i * hb - 1, 0), 0)),
                  pl.BlockSpec((HALO, CW), lambda i: (jnp.maximum(i * hb - 1, 0), 1)),
                  pl.BlockSpec((None, CONV_K, CW), lambda i: (l, 0, 0)),
                  pl.BlockSpec(memory_space=pl.ANY)],
        out_specs=[pl.BlockSpec((tm, 2 * CW), lambda i: (i, 0)),
                   pl.BlockSpec((HALO, CW), const), pl.BlockSpec((1, CW), const)],
        out_shape=[jax.ShapeDtypeStruct(dproj.shape, BF16), jax.ShapeDtypeStruct((HALO, CW), F32),
                   jax.ShapeDtypeStruct((1, CW), F32)],
        scratch_shapes=[pltpu.VMEM((tm + HALO, CW), F32), pltpu.VMEM((tm + HALO, CW), F32),
                        pltpu.VMEM((tm + HALO, CW), F32)],
        semantics=("arbitrary",), aliases={7: 0},
        args=[d_conv, d_conv, proj, proj, proj, proj, dw_w, dproj])


def _log1p_small(e):
    return jnp.where(e < 1e-3, e * (1.0 - e * (0.5 - e * (1.0 / 3.0))), jnp.log(1.0 + e))


def _softplus(z):
    return jnp.maximum(z, 0.0) + _log1p_small(jnp.exp(-jnp.abs(z)))


def _neg_expm1(x):
    series = -x * (1.0 + x * (1.0 / 2.0) * (1.0 + x * (1.0 / 3.0) * (1.0 + x * (1.0 / 4.0) * (
        1.0 + x * (1.0 / 5.0) * (1.0 + x * (1.0 / 6.0) * (1.0 + x * (1.0 / 7.0)))))))
    return jnp.where(x > -0.25, series, 1.0 - jnp.exp(x))


def _lru_gates(rxbuf, tm, base_row, lw_ref, lb_ref, wa_ref, ba_ref, wx_ref, bx_ref, lam_ref):
    rc = jnp.zeros((tm, LW), F32) + lb_ref[...]
    for k in range(LRU_K):
        o = LHALO - (LRU_K - 1) + k
        rc += lw_ref[k:k + 1, :] * rxbuf[o:o + tm, :]
    rcb = rc.astype(BF16)
    r = _sigmoid(_dot(rcb, wa_ref[...]) + ba_ref[...])
    ig = _sigmoid(_dot(rcb, wx_ref[...]) + bx_ref[...])
    sp = _softplus(-lam_ref[...])
    la = -LRU_C * r * sp
    a = jnp.exp(la)
    mult = jnp.sqrt(_neg_expm1(2.0 * la))
    valid = _row_ids(tm, base_row) >= PAD
    return rc, rcb, r, ig, sp, a, mult, valid


def _mask_rows(v, base_row):
    return jnp.where(_row_ids(v.shape[0], base_row) >= PAD, v, 0.0)


def _scan_rows(aa, bb, carry, out_ref, reverse):
    tm = aa.shape[0]
    sub = _row_ids(tm, 0) & (SUBLANES - 1)
    s = 1
    while s < SUBLANES:
        keep = (sub < SUBLANES - s) if reverse else (sub >= s)
        shift = tm - s if reverse else s
        a_s = jnp.where(keep, pltpu.roll(aa, shift, axis=0), 1.0)
        b_s = jnp.where(keep, pltpu.roll(bb, shift, axis=0), 0.0)
        bb = aa * b_s + bb
        aa = aa * a_s
        s *= 2
    groups = range(tm // SUBLANES)
    edge = 0 if reverse else SUBLANES - 1
    for j in (reversed(groups) if reverse else groups):
        rows = slice(SUBLANES * j, SUBLANES * j + SUBLANES)
        x = bb[rows] + aa[rows] * carry
        out_ref[rows, :] = x
        carry = x[edge:edge + 1]


def _lru_tile(T):
    return _pick(T, (384, 128))


def _lru_fwd(proj, lw, lb, wa, ba, wx, bx, lam, l):
    T = proj.shape[0]
    tm = _lru_tile(T)
    hb = tm // LHALO

    def body(rx_ref, rg_ref, hx_ref, lw_ref, lb_ref, wa_ref, ba_ref, wx_ref, bx_ref, lam_ref,
             yl_ref, hl_ref, rxbuf, carry):
        i = pl.program_id(0)

        @pl.when(i == 0)
        def _():
            carry[...] = jnp.zeros_like(carry)

        rxbuf[0:LHALO, :] = _mask_rows(hx_ref[...], i * tm - LHALO)
        rxbuf[LHALO:LHALO + tm, :] = _mask_rows(rx_ref[...], i * tm)
        rc, _, _, ig, _, a, mult, valid = _lru_gates(rxbuf, tm, i * tm, lw_ref, lb_ref, wa_ref, ba_ref,
                                                     wx_ref, bx_ref, lam_ref)
        bb = jnp.where(valid, mult * (ig * rc), 0.0)
        _scan_rows(a, bb, carry[0:1, :], hl_ref, reverse=False)
        carry[0:1, :] = hl_ref[tm - 1:tm, :]
        gate, _ = _silu_and_grad(rg_ref[...])
        yl_ref[...] = (hl_ref[...] * gate).astype(BF16)

    vec = pl.BlockSpec((None, 1, LW), lambda i: (l, 0, 0))
    mat = pl.BlockSpec((None, LW, LW), lambda i: (l, 0, 0))
    return pl.pallas_call(
        body, name=f"lru_fwd{l}", grid=(T // tm,),
        in_specs=[pl.BlockSpec((tm, LW), lambda i: (i, 8)),
                  pl.BlockSpec((tm, LW), lambda i: (i, 9)),
                  pl.BlockSpec((LHALO, LW), lambda i: (jnp.maximum(i * hb - 1, 0), 8)),
                  pl.BlockSpec((None, LRU_K, LW), lambda i: (l, 0, 0)),
                  vec, mat, vec, mat, vec, vec],
        out_specs=[pl.BlockSpec((tm, LW), lambda i: (i, 0)), pl.BlockSpec((tm, LW), lambda i: (i, 0))],
        out_shape=[jax.ShapeDtypeStruct((T, LW), BF16), jax.ShapeDtypeStruct((T, LW), F32)],
        scratch_shapes=[pltpu.VMEM((tm + LHALO, LW), F32), pltpu.VMEM((8, LW), F32)],
        compiler_params=_cp("arbitrary"),
    )(proj, proj, proj, lw, lb, wa, ba, wx, bx, lam)


def _lru_bwd(proj, hl, d_yl, lw, lb, wa, ba, wx, bx, lam, dproj, l, job=None):
    T = proj.shape[0]
    tm = _lru_tile(T)
    hb = tm // LHALO
    nt = T // tm

    def body(rx_ref, rg_ref, hx_ref, hl_ref, hh_ref, dy_ref, lw_ref, lb_ref, wa_ref, ba_ref, wx_ref, bx_ref,
             lam_ref, _, o_ref, dlw_ref, dlb_ref, dwa_ref, dba_ref, dwx_ref, dbx_ref, dlam_ref,
             rxbuf, dbuf, carry, head, gbuf):
        step = pl.program_id(0)
        i = nt - 1 - step

        @pl.when(step == 0)
        def _():
            carry[...] = jnp.zeros_like(carry)
            head[...] = jnp.zeros_like(head)
            for ref in (dlw_ref, dlb_ref, dwa_ref, dba_ref, dwx_ref, dbx_ref, dlam_ref):
                ref[...] = jnp.zeros_like(ref)

        rxbuf[0:LHALO, :] = _mask_rows(hx_ref[...], i * tm - LHALO)
        rxbuf[LHALO:LHALO + tm, :] = _mask_rows(rx_ref[...], i * tm)
        rc, rcb, r, ig, sp, a, mult, valid = _lru_gates(rxbuf, tm, i * tm, lw_ref, lb_ref, wa_ref, ba_ref,
                                                        wx_ref, bx_ref, lam_ref)
        rows = _row_ids(tm, 0)
        h = hl_ref[...]
        h_before = jnp.where(i > 0, hh_ref[LHALO - 1:LHALO, :], 0.0)
        hprev = jnp.where(rows == 0, h_before, pltpu.roll(h, 1, axis=0))
        rg = rg_ref[...]
        gate, dgate = _silu_and_grad(rg)
        dy = dy_ref[...]
        o_ref[:, LW:2 * LW] = (dy * h * dgate).astype(BF16)
        bb = dy * gate + jnp.where(rows == tm - 1, carry[0:1, :], 0.0)
        aa = jnp.where(rows == tm - 1, 0.0, pltpu.roll(a, tm - 1, axis=0))
        _scan_rows(aa, bb, jnp.zeros((1, LW), F32), gbuf, reverse=True)
        g = gbuf[...]
        dbuf[0:tm, :] = a * g
        carry[0:1, :] = dbuf[0:1, :]
        du = jnp.where(valid, g, 0.0)
        da = g * hprev
        dix = du * mult
        dmult = du * (ig * rc)
        dla = jnp.where(valid, da * a - dmult * (a * a) / mult, 0.0)
        dr = dla * (-LRU_C * sp)
        dlam_ref[...] += _colsum(dla * (LRU_C * r)) * _sigmoid(-lam_ref[...])
        dpa = dr * r * (1.0 - r)
        dpx = (dix * rc) * ig * (1.0 - ig)
        dpab = dpa.astype(BF16)
        dpxb = dpx.astype(BF16)
        dba_ref[...] += _colsum(dpa)
        dbx_ref[...] += _colsum(dpx)
        dwa_ref[...] += _dot_tn(rcb, dpab)
        dwx_ref[...] += _dot_tn(rcb, dpxb)
        drc = dix * ig + _dot_nt(dpab, wa_ref[...]) + _dot_nt(dpxb, wx_ref[...])
        dbuf[0:tm, :] = drc
        dbuf[tm:tm + LHALO, :] = head[...]
        acc = jnp.zeros((tm, LW), F32)
        for k in range(LRU_K):
            o = LRU_K - 1 - k
            acc += lw_ref[k:k + 1, :] * dbuf[o:o + tm, :]
            oc = LHALO - (LRU_K - 1) + k
            dlw_ref[k:k + 1, :] += _colsum(drc * rxbuf[oc:oc + tm, :])
        dlb_ref[...] += _colsum(drc)
        head[...] = dbuf[0:LHALO, :]
        o_ref[:, 0:LW] = jnp.where(valid, acc, 0.0).astype(BF16)

    rev = lambda s: nt - 1 - s
    vec = pl.BlockSpec((None, 1, LW), lambda s: (l, 0, 0))
    mat = pl.BlockSpec((None, LW, LW), lambda s: (l, 0, 0))
    const = lambda s: (0, 0)
    halo = lambda s: jnp.maximum(rev(s) * hb - 1, 0)
    return _side_call(
        body, job, name=f"lru_bwd{l}", grid=(nt,),
        in_specs=[pl.BlockSpec((tm, LW), lambda s: (rev(s), 8)),
                  pl.BlockSpec((tm, LW), lambda s: (rev(s), 9)),
                  pl.BlockSpec((LHALO, LW), lambda s: (halo(s), 8)),
                  pl.BlockSpec((tm, LW), lambda s: (rev(s), 0)),
                  pl.BlockSpec((LHALO, LW), lambda s: (halo(s), 0)),
                  pl.BlockSpec((tm, LW), lambda s: (rev(s), 0)),
                  pl.BlockSpec((None, LRU_K, LW), lambda s: (l, 0, 0)),
                  vec, mat, vec, mat, vec, vec, pl.BlockSpec(memory_space=pl.ANY)],
        out_specs=[pl.BlockSpec((tm, 2 * LW), lambda s: (rev(s), 4)),
                   pl.BlockSpec((8, LW), const), pl.BlockSpec((1, LW), const),
                   pl.BlockSpec((LW, LW), const), pl.BlockSpec((1, LW), const),
                   pl.BlockSpec((LW, LW), const), pl.BlockSpec((1, LW), const),
                   pl.BlockSpec((1, LW), const)],
        out_shape=[jax.ShapeDtypeStruct(dproj.shape, BF16),
                   jax.ShapeDtypeStruct((8, LW), F32), jax.ShapeDtypeStruct((1, LW), F32),
                   jax.ShapeDtypeStruct((LW, LW), F32), jax.ShapeDtypeStruct((1, LW), F32),
                   jax.ShapeDtypeStruct((LW, LW), F32), jax.ShapeDtypeStruct((1, LW), F32),
                   jax.ShapeDtypeStruct((1, LW), F32)],
        scratch_shapes=[pltpu.VMEM((tm + LHALO, LW), F32), pltpu.VMEM((tm + LHALO, LW), F32),
                        pltpu.VMEM((8, LW), F32), pltpu.VMEM((LHALO, LW), F32), pltpu.VMEM((tm, LW), F32)],
        semantics=("arbitrary",), aliases={13: 0},
        args=[proj, proj, proj, hl, hl, d_yl, lw, lb, wa, ba, wx, bx, lam, dproj])


def _rope_tables(T):
    pos = (lax.broadcasted_iota(jnp.int32, (T, 128), 0) - PAD).astype(F32)
    lane = lax.broadcasted_iota(jnp.int32, (T, 128), 1) % 64
    inv_freq = ROPE_THETA ** (-(lane % ROT_HALF).astype(F32) / ROT_HALF)
    ang = pos * inv_freq
    cos, sin = jnp.cos(ang), jnp.sin(ang)
    c = jnp.where(lane < 2 * ROT_HALF, cos, 1.0)
    s1 = jnp.where(lane < ROT_HALF, -sin, 0.0)
    s2 = jnp.where((lane >= ROT_HALF) & (lane < 2 * ROT_HALF), sin, 0.0)
    return c, s1, s2


def _rot_fwd(x, c, s1, s2):
    return x * c + pltpu.roll(x, 128 - ROT_HALF, axis=1) * s1 + pltpu.roll(x, ROT_HALF, axis=1) * s2


def _rot_bwd(dy, c, s1, s2):
    return dy * c + pltpu.roll(dy * s1, ROT_HALF, axis=1) + pltpu.roll(dy * s2, 128 - ROT_HALF, axis=1)


def _rope_fwd(proj, tabs, l):
    T = proj.shape[0]

    def body(ql_ref, qh_ref, k_ref, v_ref, c_ref, s1_ref, s2_ref, qr_ref, kr_ref, vb_ref):
        c, s1, s2 = c_ref[...], s1_ref[...], s2_ref[...]
        for gcol in range(AW // 128):
            src = ql_ref if gcol < 4 else qh_ref
            x = src[:, 128 * (gcol % 4):128 * (gcol % 4) + 128]
            qr_ref[:, 128 * gcol:128 * gcol + 128] = (_rot_fwd(x, c, s1, s2) * 0.125).astype(BF16)
        for gcol in range(KVW // 128):
            x = k_ref[:, 128 * gcol:128 * gcol + 128]
            kr_ref[:, 128 * gcol:128 * gcol + 128] = _rot_fwd(x, c, s1, s2).astype(BF16)
        vb_ref[...] = v_ref[...].astype(BF16)

    tr = _pick(T, (384, 128))
    tab = pl.BlockSpec((tr, 128), lambda n: (n, 0))
    return pl.pallas_call(
        body, name=f"rope_fwd{l}", grid=(T // tr,),
        in_specs=[pl.BlockSpec((tr, 512), lambda n: (n, 3)), pl.BlockSpec((tr, 512), lambda n: (n, 4)),
                  pl.BlockSpec((tr, KVW), lambda n: (n, 10)), pl.BlockSpec((tr, KVW), lambda n: (n, 11)),
                  tab, tab, tab],
        out_specs=[pl.BlockSpec((tr, AW), lambda n: (n, 0)), pl.BlockSpec((tr, KVW), lambda n: (n, 0)),
                   pl.BlockSpec((tr, KVW), lambda n: (n, 0))],
        out_shape=[jax.ShapeDtypeStruct((T, AW), BF16), jax.ShapeDtypeStruct((T, KVW), BF16),
                   jax.ShapeDtypeStruct((T, KVW), BF16)],
        compiler_params=_cp("parallel"),
    )(proj, proj, proj, proj, *tabs)


GROUP = 4


def _attn_mask(n, reps):
    qi = lax.broadcasted_iota(jnp.int32, (reps * BLK, BLK), 0) & (BLK - 1)
    kj = lax.broadcasted_iota(jnp.int32, (reps * BLK, BLK), 1)
    m0 = (kj >= PAD) & (n >= 1)
    mp = (kj > qi) & (n >= 2)
    mc = (kj <= qi) & ((n >= 1) | (kj >= PAD))
    return jnp.concatenate([m0, mp, mc], axis=1)


def _kv_both(x0_ref, xp_ref, xc_ref, g):
    pg, off = g // 2, g % 2
    cols = slice(128 * pg, 128 * pg + 128)
    x = jnp.concatenate([x0_ref[:, cols], xp_ref[:, cols], xc_ref[:, cols]], axis=0).astype(F32)
    lane = lax.broadcasted_iota(jnp.int32, (1, 128), 1)
    half = jnp.where((lane < 64) if off == 0 else (lane >= 64), x, 0.0)
    return (half + pltpu.roll(half, 64, axis=1)).astype(BF16)


def _kv_halves(x0_ref, xp_ref, xc_ref, g):
    pg, off = g // 2, g % 2
    cols = slice(128 * pg, 128 * pg + 128)
    x = jnp.concatenate([x0_ref[:, cols], xp_ref[:, cols], xc_ref[:, cols]], axis=0).astype(F32)
    lane = lax.broadcasted_iota(jnp.int32, (1, 128), 1)
    if off == 0:
        lo = jnp.where(lane < 64, x, 0.0)
        hi = pltpu.roll(lo, 64, axis=1)
    else:
        hi = jnp.where(lane >= 64, x, 0.0)
        lo = pltpu.roll(hi, 64, axis=1)
    return lo.astype(BF16), hi.astype(BF16)


def _stack_heads(a, b):
    lo = lax.broadcasted_iota(jnp.int32, (1, 128), 1) < 64
    a, b = a.astype(F32), b.astype(F32)
    return jnp.concatenate([jnp.where(lo, a, 0.0), jnp.where(lo, 0.0, a),
                            jnp.where(lo, b, 0.0), jnp.where(lo, 0.0, b)], axis=0).astype(BF16)


def _unstack_heads(x):
    lo = lax.broadcasted_iota(jnp.int32, (1, 128), 1) < 64
    return (jnp.where(lo, x[0:BLK], x[BLK:2 * BLK]), jnp.where(lo, x[2 * BLK:3 * BLK], x[3 * BLK:4 * BLK]))


def _per_head_column(values):
    return jnp.concatenate([jnp.zeros((BLK, 1), F32) + v for v in values], axis=0)


def _attn_fwd(qr, kr, vb, proj, sinks, l, job=None):
    T = qr.shape[0]

    def body(sink_ref, q_ref, k0_ref, kp_ref, kc_ref, v0_ref, vp_ref, vc_ref, ag_ref, ya_ref, att_ref, lse_ref):
        n = pl.program_id(0)
        mask = _attn_mask(n, 1)
        lane = lax.broadcasted_iota(jnp.int32, (1, 128), 1)
        lse_acc = jnp.zeros((BLK, 128), F32)
        for g in range(4):
            kx = _kv_both(k0_ref, kp_ref, kc_ref, g)
            vx = _kv_both(v0_ref, vp_ref, vc_ref, g)
            pair_cols = [slice(128 * (2 * g + pp), 128 * (2 * g + pp) + 128) for pp in range(2)]
            s4 = _dot_nt(_stack_heads(q_ref[:, pair_cols[0]], q_ref[:, pair_cols[1]]), kx)
            probs = []
            for r in range(GROUP):
                h = GROUP * g + r
                sink = sink_ref[l, h]
                s = jnp.where(mask, s4[BLK * r:BLK * r + BLK], NEG_INF)
                m = jnp.maximum(jnp.max(s, axis=1, keepdims=True), sink)
                p = jnp.exp(s - m)
                denom = jnp.sum(p, axis=1, keepdims=True) + jnp.exp(sink - m)
                probs.append((p * (1.0 / denom)).astype(BF16))
                lse_acc = jnp.where(lane == h, m + jnp.log(denom), lse_acc)
            outs = _unstack_heads(_dot(jnp.concatenate(probs, axis=0), vx))
            for cols, out in zip(pair_cols, outs):
                att_ref[:, cols] = out
                gate, _ = _silu_and_grad(ag_ref[:, cols])
                ya_ref[:, cols] = (out * gate).astype(BF16)
        lse_ref[...] = lse_acc

    prev = lambda n: (jnp.maximum(n - 1, 0), 0)
    cur = lambda n: (n, 0)
    zero = lambda n: (0, 0)
    kv = lambda f: pl.BlockSpec((BLK, KVW), f)
    return _side_call(
        body, job, name=f"attn_fwd{l}", grid=(T // BLK,),
        in_specs=[pl.BlockSpec(memory_space=pltpu.SMEM),
                  pl.BlockSpec((BLK, AW), cur), kv(zero), kv(prev), kv(cur), kv(zero), kv(prev), kv(cur),
                  pl.BlockSpec((BLK, AW), lambda n: (n, 3))],
        out_specs=[pl.BlockSpec((BLK, AW), cur), pl.BlockSpec((BLK, AW), cur), pl.BlockSpec((BLK, 128), cur)],
        out_shape=[jax.ShapeDtypeStruct((T, AW), BF16), jax.ShapeDtypeStruct((T, AW), F32),
                   jax.ShapeDtypeStruct((T, 128), F32)],
        scratch_shapes=[], semantics=("parallel",), args=[sinks, qr, kr, kr, kr, vb, vb, vb, proj])


def _attn_bwd(qr, kr, vb, proj, att, lse, d_ya, sinks, dproj, l, job=None):
    T = qr.shape[0]
    nb = T // BLK

    def body(sink_ref, q_ref, k0_ref, kp_ref, kc_ref, v0_ref, vp_ref, vc_ref, ag_ref, att_ref, lse_ref, dy_ref, _,
             dq_ref, dk_ref, dv_ref, dk0_ref, dv0_ref, dag_ref, dsink_ref, kcarry, vcarry):
        n = pl.program_id(0)

        @pl.when(n == 0)
        def _():
            dk0_ref[...] = jnp.zeros_like(dk0_ref)
            dv0_ref[...] = jnp.zeros_like(dv0_ref)
            dsink_ref[...] = jnp.zeros_like(dsink_ref)
            kcarry[...] = jnp.zeros_like(kcarry)
            vcarry[...] = jnp.zeros_like(vcarry)

        @pl.when(n == nb)
        def _():
            dk_ref[...] = kcarry[...]
            dv_ref[...] = vcarry[...]

        @pl.when(n < nb)
        def _():
            mask = _attn_mask(n, GROUP)
            lane = lax.broadcasted_iota(jnp.int32, (1, 128), 1)
            lse = lse_ref[...]
            dsink = jnp.zeros((1, 128), F32)
            dk_pg, dv_pg = [], []
            for pg in range(2):
                dk_acc = jnp.zeros((3 * BLK, 128), F32)
                dv_acc = jnp.zeros((3 * BLK, 128), F32)
                for off in range(2):
                    g = 2 * pg + off
                    kx = _kv_both(k0_ref, kp_ref, kc_ref, g)
                    vx = _kv_both(v0_ref, vp_ref, vc_ref, g)
                    pair_cols = [slice(128 * (2 * g + pp), 128 * (2 * g + pp) + 128) for pp in range(2)]
                    q4 = _stack_heads(q_ref[:, pair_cols[0]], q_ref[:, pair_cols[1]])
                    d_out = []
                    for cols in pair_cols:
                        gate, dgate = _silu_and_grad(ag_ref[:, cols])
                        dy = dy_ref[:, cols]
                        dag_ref[:, cols] = (dy * att_ref[:, cols] * dgate).astype(BF16)
                        d_out.append(dy * gate)
                    do4 = _stack_heads(d_out[0], d_out[1])
                    heads = [GROUP * g + r for r in range(GROUP)]
                    sink = _per_head_column([sink_ref[l, h] for h in heads])
                    lse4 = _per_head_column(
                        [jnp.sum(jnp.where(lane == h, lse, 0.0), axis=1, keepdims=True) for h in heads])
                    p = jnp.where(mask, jnp.exp(_dot_nt(q4, kx) - lse4), 0.0)
                    dp = _dot_nt(do4, vx)
                    delta = jnp.sum(p * dp, axis=1, keepdims=True)
                    ds = (p * (dp - delta)).astype(BF16)
                    sink_term = jnp.exp(sink - lse4) * delta
                    for r, h in enumerate(heads):
                        dsink += jnp.where(lane == h, -jnp.sum(sink_term[BLK * r:BLK * r + BLK]), 0.0)
                    for cols, dq in zip(pair_cols, _unstack_heads(_dot(ds, kx))):
                        dq_ref[:, cols] = dq
                    dkg = _dot_tn(ds, q4)
                    dvg = _dot_tn(p.astype(BF16), do4)
                    own = (lane < 64) if off == 0 else (lane >= 64)
                    dk_acc += jnp.where(own, dkg + pltpu.roll(dkg, 64, axis=1), 0.0)
                    dv_acc += jnp.where(own, dvg + pltpu.roll(dvg, 64, axis=1), 0.0)
                dk_pg.append(dk_acc)
                dv_pg.append(dv_acc)
            dsink_ref[...] += dsink
            for pg in range(2):
                cols = slice(128 * pg, 128 * pg + 128)
                dk0_ref[:, cols] += dk_pg[pg][0:BLK]
                dv0_ref[:, cols] += dv_pg[pg][0:BLK]
                dk_ref[:, cols] = kcarry[:, cols] + dk_pg[pg][BLK:2 * BLK]
                dv_ref[:, cols] = vcarry[:, cols] + dv_pg[pg][BLK:2 * BLK]
                kcarry[:, cols] = dk_pg[pg][2 * BLK:3 * BLK]
                vcarry[:, cols] = dv_pg[pg][2 * BLK:3 * BLK]

    last = nb - 1
    cur = lambda n: (jnp.minimum(n, last), 0)
    prev = lambda n: (jnp.clip(n - 1, 0, last), 0)
    zero = lambda n: (0, 0)
    kv = lambda f: pl.BlockSpec((BLK, KVW), f)
    wide = lambda f: pl.BlockSpec((BLK, AW), f)
    return _side_call(
        body, job, name=f"attn_bwd{l}", grid=(nb + 1,),
        in_specs=[pl.BlockSpec(memory_space=pltpu.SMEM),
                  wide(cur), kv(zero), kv(prev), kv(cur), kv(zero), kv(prev), kv(cur),
                  pl.BlockSpec((BLK, AW), lambda n: (jnp.minimum(n, last), 3)),
                  wide(cur), pl.BlockSpec((BLK, 128), cur), wide(cur), pl.BlockSpec(memory_space=pl.ANY)],
        out_specs=[wide(cur), kv(prev), kv(prev), kv(zero), kv(zero),
                   pl.BlockSpec((BLK, AW), lambda n: (jnp.minimum(n, last), 3)),
                   pl.BlockSpec((1, 128), zero)],
        out_shape=[jax.ShapeDtypeStruct((T, AW), F32), jax.ShapeDtypeStruct((T, KVW), F32),
                   jax.ShapeDtypeStruct((T, KVW), F32), jax.ShapeDtypeStruct((BLK, KVW), F32),
                   jax.ShapeDtypeStruct((BLK, KVW), F32), jax.ShapeDtypeStruct(dproj.shape, BF16),
                   jax.ShapeDtypeStruct((1, 128), F32)],
        scratch_shapes=[pltpu.VMEM((BLK, KVW), F32), pltpu.VMEM((BLK, KVW), F32)],
        semantics=("arbitrary",), aliases={12: 5},
        args=[sinks, qr, kr, kr, kr, vb, vb, vb, proj, att, lse, d_ya, dproj])


def _rope_bwd(dqr, dk, dv, dk0, dv0, tabs, dproj, l):
    T = dqr.shape[0]

    def body(dq_ref, dk_ref, dv_ref, dk0_ref, dv0_ref, c_ref, s1_ref, s2_ref, _, o_ref):
        n = pl.program_id(0)
        c, s1, s2 = c_ref[...], s1_ref[...], s2_ref[...]
        for gcol in range(AW // 128):
            cols = slice(128 * gcol, 128 * gcol + 128)
            o_ref[:, cols] = (_rot_bwd(dq_ref[:, cols], c, s1, s2) * 0.125).astype(BF16)
        for gcol in range(KVW // 128):
            cols = slice(128 * gcol, 128 * gcol + 128)
            kcols = slice(AW + 128 * gcol, AW + 128 * gcol + 128)
            vcols = slice(AW + KVW + 128 * gcol, AW + KVW + 128 * gcol + 128)
            o_ref[:, kcols] = _rot_bwd(dk_ref[:, cols], c, s1, s2).astype(BF16)
            o_ref[:, vcols] = dv_ref[:, cols].astype(BF16)

            @pl.when(n == 0)
            def _():
                dkk = dk_ref[0:BLK, cols] + dk0_ref[:, cols]
                o_ref[0:BLK, kcols] = _rot_bwd(dkk, c[0:BLK], s1[0:BLK], s2[0:BLK]).astype(BF16)
                o_ref[0:BLK, vcols] = (dv_ref[0:BLK, cols] + dv0_ref[:, cols]).astype(BF16)

    tr = _pick(T, (384, 128))
    cur = lambda n: (n, 0)
    zero = lambda n: (0, 0)
    tab = pl.BlockSpec((tr, 128), cur)
    return pl.pallas_call(
        body, name=f"rope_bwd{l}", grid=(T // tr,),
        in_specs=[pl.BlockSpec((tr, AW), cur), pl.BlockSpec((tr, KVW), cur), pl.BlockSpec((tr, KVW), cur),
                  pl.BlockSpec((BLK, KVW), zero), pl.BlockSpec((BLK, KVW), zero), tab, tab, tab,
                  pl.BlockSpec(memory_space=pl.ANY)],
        out_specs=pl.BlockSpec((tr, AW + 2 * KVW), lambda n: (n, 1)),
        out_shape=jax.ShapeDtypeStruct(dproj.shape, BF16),
        input_output_aliases={8: 0},
        compiler_params=_cp("parallel"),
    )(dqr, dk, dv, dk0, dv0, *tabs, dproj)


def _block_diag(w):
    nl, nh, hd, _ = w.shape
    eye = jnp.eye(nh, dtype=w.dtype)
    return jnp.einsum("lhij,hg->lhigj", w, eye).reshape(nl, nh * hd, nh * hd)


def _diag_blocks(m):
    nh, hd = 8, 64
    return jnp.einsum("hihj->hij", m.reshape(nh, hd, nh, hd))


def _device_step(x, target, p, dist=None):
    vec = lambda a: a.reshape(DEPTH, 1, a.shape[-1])
    ln_in_g, ln_in_b = p["ln_in_g"].reshape(1, D), p["ln_in_b"].reshape(1, D)
    conv_dw_b, conv_ln_g, conv_ln_b, conv_pw_b = map(vec, (p["conv_dw_b"], p["conv_ln_g"], p["conv_ln_b"], p["conv_pw_b"]))
    lru_conv_b, lru_ba, lru_bx, lru_lambda = map(vec, (p["lru_conv_b"], p["lru_ba"], p["lru_bx"], p["lru_lambda"]))
    ln_post_g, ln_post_b = vec(p["ln_post_g"]), vec(p["ln_post_b"])
    wa_bd = _block_diag(p["lru_wa"]).astype(BF16)
    wx_bd = _block_diag(p["lru_wx"]).astype(BF16)
    w_in, w_out, pw_w = list(p["w_in"]), list(p["w_out"]), list(p["conv_pw_w"])
    sinks = p["attn_sinks"]
    big_names = ("w_in", "w_out", "conv_pw_w")

    order = dist[4] if dist else jnp.arange(N_SHARD, dtype=jnp.int32)
    (h, hb), got = _embed_fwd(x, p["meta_tokens"], ln_in_g, ln_in_b,
                              job=_gather_job([w_in[0]], peers=(0, 1)) if dist else None)
    if dist:
        w_in[0] = got[0]
    T = h.shape[0]
    tabs = _rope_tables(T)
    saved = []
    for l in range(DEPTH):
        if l == 0:
            job = _join_jobs(_gather_job([w_in[0]], peers=(2,)), _gather_job([pw_w[0]])) if dist else None
            (proj,), got = _proj_fwd(hb, w_in[0], order, 0, N_SHARD - 1, None, l, job=job)
            if dist:
                w_in[0], pw_w[0] = got
            (proj,), _ = _proj_fwd(hb, w_in[0], order, N_SHARD - 1, 1, proj, l)
        else:
            (proj,), _ = _proj_fwd(hb, w_in[l], order, 0, N_SHARD, None, l)
        pw_l = pw_w[l].reshape(CW, CW)
        (yc, conv), got = _conv_fwd(proj, p["conv_dw_w"], conv_dw_b, conv_ln_g, conv_ln_b, pw_l, conv_pw_b, l,
                                    job=_gather_job([w_out[0]]) if dist and l == 0 else None)
        if got:
            w_out[0] = got[0]
        qr, kr, vb = _rope_fwd(proj, tabs, l)
        (ya, att, lse), got = _attn_fwd(
            qr, kr, vb, proj, sinks, l, job=_gather_job([w_in[1]]) if dist and l == 0 else None)
        if got:
            w_in[1] = got[0]
        yl, hl = _lru_fwd(proj, p["lru_conv_w"], lru_conv_b, wa_bd, lru_ba, wx_bd, lru_bx, lru_lambda, l)
        (hn, hnb, xhat, rstd), got = _out_fwd(
            yc, ya, yl, w_out[l], h, ln_post_g, ln_post_b, l,
            job=_gather_job([w_out[1], pw_w[1]]) if dist and l == 0 else None)
        if got:
            w_out[1], pw_w[1] = got
        saved.append((hb, proj, yc, conv, qr, kr, vb, ya, att, lse, yl, hl, xhat, rstd, pw_l))
        h, hb = hn, hnb

    dh = None
    g = {}
    later = None
    early, last = ("w_out", "conv_pw_w"), ("w_in",)
    own = {}
    for l in reversed(range(DEPTH)):
        hb_l, proj, yc, conv, qr, kr, vb, ya, att, lse, yl, hl, xhat, rstd, pw_l = saved[l]
        tail = dist is not None and l == 0
        top = l == DEPTH - 1
        (part, dz, dzb, g["ln_post_g", l], g["ln_post_b", l], d_yc, d_ya, d_yl), recv = _post_ln_dcat_bwd(
            h if top else dh, target if top else None, xhat, rstd, ln_post_g, w_out[l], l,
            job=_swap_job(later["grads"]) if later else None)
        if top:
            loss_part = part
        if later:
            later["parts"], later["owns"] = _chip_partials(big_names, later["grads"], recv, dist, later["l"])
        g["w_out", l] = _dwout_bwd(yc, ya, yl, dzb, l)
        d_conv, dproj, dpw, g["conv_pw_b", l], g["conv_ln_g", l], g["conv_ln_b", l] = _conv_bwd_rows(
            conv, proj, d_yc, conv_ln_g, conv_ln_b, pw_l, conv_pw_b, l)
        g["conv_pw_w", l] = dpw.reshape(N_SHARD, 2, PW_SH // 2, CW)
        if tail:
            own["early"] = dict(l=0, grads=[g[name, 0] for name in early])
        job = None
        if tail:
            job = _join_jobs(_swap_job(own["early"]["grads"]), _scatter_job(later["parts"][1:]))
        (dproj, ddw, g["conv_dw_b", l]), got = _conv_bwd_taps(d_conv, proj, p["conv_dw_w"], dproj, l, job=job)
        if tail:
            n_early = len(early)
            own["early"]["parts"], own["early"]["owns"] = _chip_partials(
                early, own["early"]["grads"], got[:n_early], dist, 0)
            later["z"] = got[n_early:]
        g["conv_dw_w", l] = ddw[:CONV_K]
        (dqr, dk, dv, dk0, dv0, dproj, dsink), z = _attn_bwd(
            qr, kr, vb, proj, att, lse, d_ya, sinks, dproj, l,
            job=_scatter_job(later["parts"][:1]) if later else None)
        if later:
            later["z"] = z + later["z"]
        g["attn_sinks", l] = dsink[0, :N_HEADS]
        dproj = _rope_bwd(dqr, dk, dv, dk0, dv0, tabs, dproj, l)
        (dproj, dlw, g["lru_conv_b", l], dwa, g["lru_ba", l], dwx, g["lru_bx", l], g["lru_lambda", l]), z = _lru_bwd(
            proj, hl, d_yl, p["lru_conv_w"], lru_conv_b, wa_bd, lru_ba, wx_bd, lru_bx, lru_lambda, dproj, l,
            job=_scatter_job(own["early"]["parts"]) if tail else None)
        if tail:
            own["early"]["z"] = z
        g["lru_conv_w", l] = dlw[:LRU_K]
        g["lru_wa", l] = _diag_blocks(dwa)
        g["lru_wx", l] = _diag_blocks(dwx)
        job = None
        if l > 0:
            g["w_in", l] = _dwin_bwd(hb_l, dproj, l)
        else:
            c = dist[0] if dist else jnp.int32(0)
            pack_a = _pack_rows([_layer_stack(g, name) for name in _SMALL_LAYERED]) if dist else None
            (give,), slots_a = _dwin_half(hb_l, dproj, 1 - c, l, "give", job=_spread_job(pack_a) if dist else None)
            (keep,), recv = _dwin_half(hb_l, dproj, c, l, "keep", job=_send_job([give]) if dist else None)
            if dist:
                g["pack_layered", -1] = _sum_slots(pack_a, slots_a[0], dist[3], "layered")
                own["last"] = dict(l=0)
                own["last"]["parts"], own["last"]["owns"] = _chip_partials(
                    last, [keep.reshape(N_SHARD, 1, D // 2, WIN_SH)], recv, (jnp.int32(0),) + tuple(dist[1:]), 0)
                job = _scatter_job(own["last"]["parts"])
            else:
                g["w_in", l] = jnp.stack([keep, give], axis=1)
        (dh,), got = _dh_bwd(dproj, [w_in[l]], dz, l, job=job)
        if tail:
            own["last"]["z"] = got
        if later:
            _finish_reduce(big_names, later, dist, g)
            later = None
        if dist and l > 0:
            later = dict(l=l, grads=[g[name, l] for name in big_names])
    grad_x, g["meta_tokens", -1], g["ln_in_g", -1], g["ln_in_b", -1] = _embed_bwd(
        dh, x, p["meta_tokens"], ln_in_g, ln_in_b)
    if dist:
        pack_b = _pack_rows([g[name, -1] for name in _SMALL_EMBED])
        slots_b = _run_job(_spread_job(pack_b), "spread_embed")[0]
        g["pack_embed", -1] = _sum_slots(pack_b, slots_b, dist[3], "embed")
        state = dict(l=0, owns=own["last"]["owns"] + own["early"]["owns"], z=own["last"]["z"] + own["early"]["z"])
        _finish_reduce(last + early, state, dist, g)
    return loss_part, grad_x, g


_SMALL_EMBED = ("meta_tokens", "ln_in_g", "ln_in_b")
_SMALL_LAYERED = ("conv_dw_w", "conv_dw_b", "conv_ln_g", "conv_ln_b", "conv_pw_b", "attn_sinks", "lru_conv_w",
                  "lru_conv_b", "lru_wa", "lru_ba", "lru_wx", "lru_bx", "lru_lambda", "ln_post_g", "ln_post_b")


def _layer_stack(g, name):
    return jnp.stack([g[name, l] for l in range(DEPTH)], axis=0)


def _chip_partials(names, grads, recv, dist, l):
    outs = [_chip_partial(a, r, dist[0], dist[1], f"{name}{l}") for name, a, r in zip(names, grads, recv)]
    return [o[0] for o in outs], [o[1] for o in outs]


def _finish_reduce(names, state, dist, g):
    l = state["l"]
    totals = [_shard_total(po, zz, dist[2], f"{name}{l}") for name, po, zz in zip(names, state["owns"], state["z"])]
    full = _run_job(_share_job(totals), f"share_halves{l}")
    for name, f in zip(names, full):
        g[name, l] = f.reshape(2 * f.shape[1], f.shape[2])


MESH = pl.DeviceIdType.MESH
HBM_SPEC = pl.BlockSpec(memory_space=pltpu.HBM)
N_DEV = 8


def _position():
    x, y, c = lax.axis_index("x"), lax.axis_index("y"), lax.axis_index("c")
    return x, y, c


def _other_chips(x, y):
    return [(1 - x, y), (x, 1 - y), (1 - x, 1 - y)]


def _cast_into_slot(a, l, j, tag, piece=0, pieces=1):
    _, R, C = a.shape
    rows = R // pieces
    tb = _pick(rows, (512, 128))
    first = piece * rows // tb

    def body(s_ref, a_ref, o_ref):
        o_ref[...] = a_ref[...].astype(BF16)

    grid_spec = pltpu.PrefetchScalarGridSpec(
        num_scalar_prefetch=1, grid=(rows // tb,),
        in_specs=[pl.BlockSpec((None, tb, C), lambda t, sc: (l, first + t, 0))],
        out_specs=pl.BlockSpec((None, tb, C), lambda t, sc: (sc[0], t, 0)))
    return pl.pallas_call(
        body, name=f"cast_into_slot_{tag}{l}_{piece}", grid_spec=grid_spec,
        out_shape=jax.ShapeDtypeStruct((N_SHARD, rows, C), BF16),
        compiler_params=_cp("arbitrary"),
    )(jnp.reshape(j, (1,)).astype(jnp.int32), a)


class _Job:
    def __init__(self, inputs, aliased, extra_out, sems, start, mid, finish):
        self.inputs, self.aliased, self.extra_out, self.sems = list(inputs), aliased, list(extra_out), list(sems)
        self.start, self.mid, self.finish = start, mid, finish

    def out_shapes(self):
        own = [jax.ShapeDtypeStruct(a.shape, a.dtype) for a in self.inputs] if self.aliased else []
        return own + self.extra_out


def _side_call(body, job, *, name, grid, in_specs, out_specs, out_shape, scratch_shapes, semantics, args,
               aliases=None, prefetch=()):
    aliases = dict(aliases or {})
    n_pre = len(prefetch)

    def call(fn, ins, outs, shapes, scratch, sem, operands):
        if n_pre:
            spec = pltpu.PrefetchScalarGridSpec(num_scalar_prefetch=n_pre, grid=grid, in_specs=ins, out_specs=outs,
                                                scratch_shapes=scratch)
            return pl.pallas_call(fn, name=name, grid_spec=spec, out_shape=shapes,
                                  input_output_aliases={k + n_pre: v for k, v in aliases.items()},
                                  compiler_params=_cp(*sem))(*prefetch, *operands)
        return pl.pallas_call(fn, name=name, grid=grid, in_specs=ins, out_specs=outs, out_shape=shapes,
                              scratch_shapes=scratch, input_output_aliases=aliases,
                              compiler_params=_cp(*sem))(*operands)

    if job is None:
        return list(call(body, list(in_specs), list(out_specs), list(out_shape), list(scratch_shapes),
                         semantics, args)), []
    n_in, n_out, n_scr = len(in_specs), len(out_specs), len(scratch_shapes)
    j_in, j_out = len(job.inputs), len(job.out_shapes())
    steps = 1
    for gsize in grid:
        steps *= gsize

    def wrapped(*refs):
        pre, refs = refs[:n_pre], refs[n_pre:]
        host_in, job_in = refs[:n_in], refs[n_in:n_in + j_in]
        o0 = n_in + j_in
        host_out, job_out = refs[o0:o0 + n_out], refs[o0 + n_out:o0 + n_out + j_out]
        s0 = o0 + n_out + j_out
        host_scr, sems = refs[s0:s0 + n_scr], refs[s0 + n_scr:]
        step = pl.program_id(0)
        for d in range(1, len(grid)):
            step = step * grid[d] + pl.program_id(d)

        @pl.when(step == 0)
        def _():
            job.start(job_in, job_out, sems)

        @pl.when(step == max(steps - 2, 0))
        def _():
            job.mid(job_in, job_out, sems)

        body(*pre, *host_in, *host_out, *host_scr)

        @pl.when(step == steps - 1)
        def _():
            job.finish(job_in, job_out, sems)

    if job.aliased:
        aliases.update({n_in + k: n_out + k for k in range(j_in)})
    outs = call(wrapped, list(in_specs) + [HBM_SPEC] * j_in, list(out_specs) + [HBM_SPEC] * j_out,
                list(out_shape) + job.out_shapes(), list(scratch_shapes) + job.sems,
                ["arbitrary"] * len(grid), [*args, *job.inputs])
    return list(outs[:n_out]), list(outs[n_out:])


def _run_job(job, name):
    return _side_call(lambda: None, job, name=name, grid=(1,), in_specs=[], out_specs=[], out_shape=[],
                      scratch_shapes=[], semantics=("arbitrary",), args=[])[1]


def _gather_job(slots, peers=(0, 1, 2)):
    n = len(slots)

    def copies(buf, sems):
        ici_send, ici_recv, d2d_send, d2d_recv = sems
        x, y, c = _position()
        chips = _other_chips(x, y)

        def half(k, slot, which):
            hr = buf[k].shape[1] // 2
            return buf[k].at[slot, pl.ds(pl.multiple_of(which * hr, hr), hr)]

        def over_ici(k, p, slot):
            px, py = chips[p]
            return pltpu.make_async_remote_copy(
                src_ref=half(k, slot, c), dst_ref=half(k, slot, c),
                send_sem=ici_send.at[k * 3 + p], recv_sem=ici_recv.at[k * 3 + p],
                device_id=(px, py, c), device_id_type=MESH)

        def over_d2d(k, p, which):
            px, py = chips[p]
            return pltpu.make_async_remote_copy(
                src_ref=half(k, 2 * px + py, which), dst_ref=half(k, 2 * px + py, which),
                send_sem=d2d_send.at[k * 3 + p], recv_sem=d2d_recv.at[k * 3 + p],
                device_id=(x, y, 1 - c), device_id_type=MESH)

        return over_ici, over_d2d, 2 * x + y, chips, c

    pairs = [(k, p) for k in range(n) for p in peers]

    def start(_, buf, sems):
        over_ici, _, mine, _, _ = copies(buf, sems)
        for k, p in pairs:
            over_ici(k, p, mine).start()

    def mid(_, buf, sems):
        over_ici, over_d2d, _, chips, c = copies(buf, sems)
        for k, p in pairs:
            px, py = chips[p]
            over_ici(k, p, 2 * px + py).wait_recv()
            over_d2d(k, p, c).start()

    def finish(_, buf, sems):
        over_ici, over_d2d, mine, _, c = copies(buf, sems)
        for k, p in pairs:
            over_d2d(k, p, 1 - c).wait_recv()
        for k, p in pairs:
            over_ici(k, p, mine).wait_send()
            over_d2d(k, p, c).wait_send()

    return _Job(slots, True, [], [pltpu.SemaphoreType.DMA((3 * n,))] * 4, start, mid, finish)


def _gather_shards(shards):
    n = len(shards)

    def body(*refs):
        src, dst = refs[:n], refs[n:2 * n]
        send_sems, recv_sems, local_sems = refs[2 * n:]
        x, y, c = _position()
        mine = 2 * x + y
        chips = _other_chips(x, y)

        def copy(k, p):
            return pltpu.make_async_remote_copy(
                src_ref=src[k], dst_ref=dst[k].at[mine],
                send_sem=send_sems.at[k * 3 + p], recv_sem=recv_sems.at[k * 3 + p],
                device_id=(*chips[p], c), device_id_type=MESH)

        def arrival(k, p):
            px, py = chips[p]
            return pltpu.make_async_remote_copy(
                src_ref=src[k], dst_ref=dst[k].at[2 * px + py],
                send_sem=send_sems.at[k * 3 + p], recv_sem=recv_sems.at[k * 3 + p],
                device_id=(px, py, c), device_id_type=MESH)

        local = [pltpu.make_async_copy(src[k], dst[k].at[mine], local_sems.at[k]) for k in range(n)]
        for cp in local:
            cp.start()
        for k in range(n):
            for p in range(3):
                copy(k, p).start()
        for k in range(n):
            for p in range(3):
                arrival(k, p).wait_recv()
        for k in range(n):
            for p in range(3):
                copy(k, p).wait_send()
        for cp in local:
            cp.wait()

    return pl.pallas_call(
        body, name="gather_shards",
        in_specs=[HBM_SPEC] * n, out_specs=[HBM_SPEC] * n,
        out_shape=[jax.ShapeDtypeStruct((N_SHARD,) + s.shape, s.dtype) for s in shards],
        scratch_shapes=[pltpu.SemaphoreType.DMA((3 * n,)), pltpu.SemaphoreType.DMA((3 * n,)),
                        pltpu.SemaphoreType.DMA((n,))],
    )(*shards)


def _swap_job(grads):
    n = len(grads)

    def copies(src, dst, sems):
        x, y, c = _position()
        return [pltpu.make_async_remote_copy(
            src_ref=src[k].at[:, 1 - c], dst_ref=dst[k],
            send_sem=sems[0].at[k], recv_sem=sems[1].at[k],
            device_id=(x, y, 1 - c), device_id_type=MESH) for k in range(n)]

    def start(src, dst, sems):
        for cp in copies(src, dst, sems):
            cp.start()

    def finish(src, dst, sems):
        for cp in copies(src, dst, sems):
            cp.wait()

    return _Job(grads, False, [jax.ShapeDtypeStruct((N_SHARD,) + g.shape[2:], F32) for g in grads],
                [pltpu.SemaphoreType.DMA((n,))] * 2, start, lambda *_: None, finish)


def _send_job(arrays):
    n = len(arrays)

    def copies(src, dst, sems):
        x, y, c = _position()
        return [pltpu.make_async_remote_copy(
            src_ref=src[k], dst_ref=dst[k], send_sem=sems[0].at[k], recv_sem=sems[1].at[k],
            device_id=(x, y, 1 - c), device_id_type=MESH) for k in range(n)]

    def start(src, dst, sems):
        for cp in copies(src, dst, sems):
            cp.start()

    def finish(src, dst, sems):
        for cp in copies(src, dst, sems):
            cp.wait()

    return _Job(arrays, False, [jax.ShapeDtypeStruct(a.shape, a.dtype) for a in arrays],
                [pltpu.SemaphoreType.DMA((n,))] * 2, start, lambda *_: None, finish)


def _chip_partial(a, y, c, j, tag):
    _, _, R, C = a.shape
    tr = _pick(R, (256, 64))

    def body(s_ref, a_ref, y_ref, pb_ref, po_ref):
        total = a_ref[...] + y_ref[...]
        pb_ref[...] = total.astype(BF16)

        @pl.when(pl.program_id(1) == s_ref[1])
        def _():
            po_ref[...] = total

    grid_spec = pltpu.PrefetchScalarGridSpec(
        num_scalar_prefetch=1, grid=(R // tr, N_SHARD),
        in_specs=[pl.BlockSpec((None, None, tr, C), lambda t, s, sc: (s, sc[0], t, 0)),
                  pl.BlockSpec((None, tr, C), lambda t, s, sc: (s, t, 0))],
        out_specs=[pl.BlockSpec((None, tr, C), lambda t, s, sc: (s, t, 0)),
                   pl.BlockSpec((tr, C), lambda t, s, sc: (t, 0))])
    return pl.pallas_call(
        body, name=f"chip_partial_{tag}", grid_spec=grid_spec,
        out_shape=[jax.ShapeDtypeStruct((N_SHARD, R, C), BF16), jax.ShapeDtypeStruct((R, C), F32)],
        compiler_params=_cp("arbitrary", "arbitrary"),
    )(jnp.stack([c, j]).astype(jnp.int32), a, y)


def _scatter_job(parts):
    n = len(parts)
    pairs = [(k, p) for k in range(n) for p in range(3)]

    def copy(src, dst, sems, k, p, outgoing):
        x, y, c = _position()
        mine = 2 * x + y
        px, py = _other_chips(x, y)[p]
        theirs = 2 * px + py
        return pltpu.make_async_remote_copy(
            src_ref=src[k].at[theirs if outgoing else mine], dst_ref=dst[k].at[mine if outgoing else theirs],
            send_sem=sems[0].at[k * 3 + p], recv_sem=sems[1].at[k * 3 + p],
            device_id=(px, py, c), device_id_type=MESH)

    def start(src, dst, sems):
        for k, p in pairs:
            copy(src, dst, sems, k, p, True).start()

    def finish(src, dst, sems):
        for k, p in pairs:
            copy(src, dst, sems, k, p, False).wait_recv()
        for k, p in pairs:
            copy(src, dst, sems, k, p, True).wait_send()

    return _Job(parts, False, [jax.ShapeDtypeStruct(pb.shape, BF16) for pb in parts],
                [pltpu.SemaphoreType.DMA((3 * n,))] * 2, start, lambda *_: None, finish)


def _shard_total(own, z, others_c, tag):
    R, C = own.shape
    tr = _pick(R, (256, 64))

    def body(s_ref, o_ref, z0_ref, z1_ref, z2_ref, h_ref):
        h_ref[...] = ((o_ref[...] + z0_ref[...].astype(F32)) + z1_ref[...].astype(F32)) + z2_ref[...].astype(F32)

    zspec = lambda q: pl.BlockSpec((None, tr, C), lambda t, sc: (sc[q], t, 0))
    grid_spec = pltpu.PrefetchScalarGridSpec(
        num_scalar_prefetch=1, grid=(R // tr,),
        in_specs=[pl.BlockSpec((tr, C), lambda t, sc: (t, 0)), zspec(0), zspec(1), zspec(2)],
        out_specs=pl.BlockSpec((None, tr, C), lambda t, sc: (sc[3], t, 0)))
    return pl.pallas_call(
        body, name=f"shard_total_{tag}", grid_spec=grid_spec,
        out_shape=jax.ShapeDtypeStruct((2, R, C), F32),
        compiler_params=_cp("arbitrary"),
    )(others_c, own, z, z, z)


def _share_job(totals):
    n = len(totals)

    def copy(buf, sems, k, which):
        x, y, c = _position()
        return pltpu.make_async_remote_copy(
            src_ref=buf[k].at[which], dst_ref=buf[k].at[which],
            send_sem=sems[0].at[k], recv_sem=sems[1].at[k],
            device_id=(x, y, 1 - c), device_id_type=MESH)

    def start(_, buf, sems):
        c = lax.axis_index("c")
        for k in range(n):
            copy(buf, sems, k, c).start()

    def finish(_, buf, sems):
        c = lax.axis_index("c")
        for k in range(n):
            copy(buf, sems, k, 1 - c).wait_recv()
        for k in range(n):
            copy(buf, sems, k, c).wait_send()

    return _Job(totals, True, [], [pltpu.SemaphoreType.DMA((n,))] * 2, start, lambda *_: None, finish)


def _spread_job(pack):
    def copy(src, dst, sems, m, outgoing):
        x, y, c = _position()
        peer = (x ^ (m >> 2), y ^ ((m >> 1) & 1), c ^ (m & 1))
        slot = 4 * x + 2 * y + c if outgoing else 4 * peer[0] + 2 * peer[1] + peer[2]
        return pltpu.make_async_remote_copy(
            src_ref=src[0], dst_ref=dst[0].at[slot], send_sem=sems[0].at[m - 1], recv_sem=sems[1].at[m - 1],
            device_id=peer, device_id_type=MESH)

    def start(src, dst, sems):
        for m in range(1, N_DEV):
            copy(src, dst, sems, m, True).start()

    def finish(src, dst, sems):
        for m in range(1, N_DEV):
            copy(src, dst, sems, m, False).wait_recv()
        for m in range(1, N_DEV):
            copy(src, dst, sems, m, True).wait_send()

    return _Job([pack], False, [jax.ShapeDtypeStruct((N_DEV,) + pack.shape, F32)],
                [pltpu.SemaphoreType.DMA((N_DEV - 1,))] * 2, start, lambda *_: None, finish)


def _join_jobs(a, b):
    assert a.aliased == b.aliased and not (a.aliased and (a.extra_out or b.extra_out))
    n_in, n_out, n_sem = len(a.inputs), len(a.out_shapes()), len(a.sems)

    def phase(name):
        def run(ins, outs, sems):
            getattr(a, name)(ins[:n_in], outs[:n_out], sems[:n_sem])
            getattr(b, name)(ins[n_in:], outs[n_out:], sems[n_sem:])
        return run

    return _Job(a.inputs + b.inputs, a.aliased, a.extra_out + b.extra_out, a.sems + b.sems,
                phase("start"), phase("mid"), phase("finish"))


def _sum_slots(pack, slots, me, tag):
    def body(me_ref, p_ref, s_ref, o_ref):
        acc = None
        for d in range(N_DEV):
            term = jnp.where(me_ref[0] == d, p_ref[...], s_ref[d])
            acc = term if acc is None else acc + term
        o_ref[...] = acc

    vm = pl.BlockSpec(memory_space=pltpu.VMEM)
    return pl.pallas_call(
        body, name=f"sum_slots_{tag}",
        in_specs=[pl.BlockSpec(memory_space=pltpu.SMEM), vm, vm], out_specs=vm,
        out_shape=jax.ShapeDtypeStruct(pack.shape, F32),
        compiler_params=pltpu.CompilerParams(vmem_limit_bytes=V7X_VMEM_LIMIT),
    )(jnp.reshape(me, (1,)).astype(jnp.int32), pack, slots)


def _pack_rows(arrays):
    total = sum(a.size for a in arrays)
    rows = -(-total // 128)
    rows = -(-rows // PACK_ROWS_ALIGN) * PACK_ROWS_ALIGN
    flat = [a.reshape(-1) for a in arrays] + [jnp.zeros((rows * 128 - total,), F32)]
    return jnp.concatenate(flat).reshape(rows, 128)


def _adamw_math(w, g, m, v):
    m = ADAM_B1 * m + (1.0 - ADAM_B1) * g
    v = ADAM_B2 * v + (1.0 - ADAM_B2) * (g * g)
    m_hat = m / (1.0 - ADAM_B1 ** ADAM_STEP)
    v_hat = v / (1.0 - ADAM_B2 ** ADAM_STEP)
    delta = -ADAM_LR * (m_hat / (jnp.sqrt(v_hat) + ADAM_EPS) + ADAM_WD * w)
    return delta, m, v


def _adamw_big(w, g0, g1, m, v, tag):
    _, R, C = w.shape
    tr = _pick(R, (256, 128))

    def body(w_ref, g0_ref, g1_ref, m_ref, v_ref, go_ref, d_ref, mo_ref, vo_ref):
        g = jnp.where(pl.program_id(0) == 0, g0_ref[...], g1_ref[...])
        delta, mn, vn = _adamw_math(w_ref[...], g, m_ref[...], v_ref[...])
        go_ref[...] = g
        d_ref[...] = delta
        mo_ref[...] = mn
        vo_ref[...] = vn

    s3 = pl.BlockSpec((None, tr, C), lambda l, t: (l, t, 0))
    s2 = pl.BlockSpec((tr, C), lambda l, t: (t, 0))
    shp = jax.ShapeDtypeStruct(w.shape, F32)
    return pl.pallas_call(
        body, name=f"adamw_{tag}", grid=(2, R // tr),
        in_specs=[s3, s2, s2, s3, s3], out_specs=[s3, s3, s3, s3],
        out_shape=[shp, shp, shp, shp],
        compiler_params=_cp("parallel", "parallel"),
    )(w, g0, g1, m, v)


def _adamw_small(ws, gs, ms, vs):
    n = len(ws)

    def body(*refs):
        w_r, g_r, m_r, v_r = refs[:n], refs[n:2 * n], refs[2 * n:3 * n], refs[3 * n:4 * n]
        d_o, m_o, v_o = refs[4 * n:5 * n], refs[5 * n:6 * n], refs[6 * n:7 * n]
        for k in range(n):
            delta, mn, vn = _adamw_math(w_r[k][...], g_r[k][...], m_r[k][...], v_r[k][...])
            d_o[k][...] = delta
            m_o[k][...] = mn
            v_o[k][...] = vn

    vm = pl.BlockSpec(memory_space=pltpu.VMEM)
    shapes = [jax.ShapeDtypeStruct(w.shape, F32) for w in ws]
    outs = pl.pallas_call(
        body, name="adamw_small",
        in_specs=[vm] * (4 * n), out_specs=[vm] * (3 * n),
        out_shape=shapes * 3,
    )(*ws, *gs, *ms, *vs)
    return outs[:n], outs[n:2 * n], outs[2 * n:]


_WEIGHTS = ["meta_tokens", "ln_in_g", "ln_in_b", "w_in", "conv_dw_w", "conv_dw_b", "conv_ln_g", "conv_ln_b",
            "conv_pw_w", "conv_pw_b", "attn_sinks", "lru_conv_w", "lru_conv_b", "lru_wa", "lru_ba", "lru_wx",
            "lru_bx", "lru_lambda", "w_out", "ln_post_g", "ln_post_b"]
_BIG = ("w_in", "w_out", "conv_pw_w")
_SMALL_SHARDED = {"meta_tokens": 1, "conv_dw_w": 2, "lru_conv_w": 2}
PACK_ROWS_ALIGN = 8


def _as2d(a):
    return a.reshape(1, -1) if a.ndim == 1 else a.reshape(-1, a.shape[-1])


def kernel(x, meta_tokens, ln_in_g, ln_in_b, w_in, conv_dw_w, conv_dw_b, conv_ln_g, conv_ln_b, conv_pw_w, conv_pw_b, attn_sinks, lru_conv_w, lru_conv_b, lru_wa, lru_ba, lru_wx, lru_bx, lru_lambda, w_out, ln_post_g, ln_post_b, loss_target, m_meta_tokens, m_ln_in_g, m_ln_in_b, m_w_in, m_conv_dw_w, m_conv_dw_b, m_conv_ln_g, m_conv_ln_b, m_conv_pw_w, m_conv_pw_b, m_attn_sinks, m_lru_conv_w, m_lru_conv_b, m_lru_wa, m_lru_ba, m_lru_wx, m_lru_bx, m_lru_lambda, m_w_out, m_ln_post_g, m_ln_post_b, v_meta_tokens, v_ln_in_g, v_ln_in_b, v_w_in, v_conv_dw_w, v_conv_dw_b, v_conv_ln_g, v_conv_ln_b, v_conv_pw_w, v_conv_pw_b, v_attn_sinks, v_lru_conv_w, v_lru_conv_b, v_lru_wa, v_lru_ba, v_lru_wx, v_lru_bx, v_lru_lambda, v_w_out, v_ln_post_g, v_ln_post_b):
    w = dict(meta_tokens=meta_tokens, ln_in_g=ln_in_g, ln_in_b=ln_in_b, w_in=w_in, conv_dw_w=conv_dw_w,
             conv_dw_b=conv_dw_b, conv_ln_g=conv_ln_g, conv_ln_b=conv_ln_b, conv_pw_w=conv_pw_w,
             conv_pw_b=conv_pw_b, attn_sinks=attn_sinks, lru_conv_w=lru_conv_w, lru_conv_b=lru_conv_b,
             lru_wa=lru_wa, lru_ba=lru_ba, lru_wx=lru_wx, lru_bx=lru_bx, lru_lambda=lru_lambda, w_out=w_out,
             ln_post_g=ln_post_g, ln_post_b=ln_post_b)
    mom_m = dict(zip(_WEIGHTS, (m_meta_tokens, m_ln_in_g, m_ln_in_b, m_w_in, m_conv_dw_w, m_conv_dw_b, m_conv_ln_g,
                                m_conv_ln_b, m_conv_pw_w, m_conv_pw_b, m_attn_sinks, m_lru_conv_w, m_lru_conv_b,
                                m_lru_wa, m_lru_ba, m_lru_wx, m_lru_bx, m_lru_lambda, m_w_out, m_ln_post_g,
                                m_ln_post_b)))
    mom_v = dict(zip(_WEIGHTS, (v_meta_tokens, v_ln_in_g, v_ln_in_b, v_w_in, v_conv_dw_w, v_conv_dw_b, v_conv_ln_g,
                                v_conv_ln_b, v_conv_pw_w, v_conv_pw_b, v_attn_sinks, v_lru_conv_w, v_lru_conv_b,
                                v_lru_wa, v_lru_ba, v_lru_wx, v_lru_bx, v_lru_lambda, v_w_out, v_ln_post_g,
                                v_ln_post_b)))
    xi, yi, ci = _position()
    j = 2 * xi + yi

    g_meta, g_dw, g_lc = _gather_shards([meta_tokens, conv_dw_w, lru_conv_w])
    p = dict(w)
    p["w_in"] = [_cast_into_slot(w_in, l, j, "w_in") for l in range(DEPTH)]
    p["w_out"] = [_cast_into_slot(w_out, l, j, "w_out") for l in range(DEPTH)]
    p["conv_pw_w"] = [_cast_into_slot(conv_pw_w, l, j, "conv_pw_w") for l in range(DEPTH)]
    p["meta_tokens"] = g_meta.transpose(1, 0, 2).reshape(N_META, D)
    p["conv_dw_w"] = g_dw.transpose(1, 2, 0, 3).reshape(DEPTH, CONV_K, CW)
    p["lru_conv_w"] = g_lc.transpose(1, 2, 0, 3).reshape(DEPTH, LRU_K, LW)

    others = jnp.stack([jnp.where(j <= 0, 1, 0), jnp.where(j <= 1, 2, 1), jnp.where(j <= 2, 3, 2), ci]).astype(jnp.int32)
    me = 4 * xi + 2 * yi + ci
    order = jnp.stack([j, 2 * (1 - xi) + yi, 2 * xi + (1 - yi), 2 * (1 - xi) + (1 - yi)]).astype(jnp.int32)
    loss_part, grad_x, g = _device_step(x[0], loss_target[0], p, dist=(ci, j, others, me, order))
    loss = lax.psum(jnp.sum(loss_part), ("x", "y", "c"))
    big = {(name, l): g[name, l] for name in _BIG for l in range(DEPTH)}

    small_names = [n for n in _WEIGHTS if n not in _BIG]
    small_g = {}
    for names, red in ((_SMALL_LAYERED, g["pack_layered", -1]), (_SMALL_EMBED, g["pack_embed", -1])):
        red = red.reshape(-1)
        off = 0
        for n in names:
            fshape = list(w[n].shape)
            if n in _SMALL_SHARDED:
                fshape[_SMALL_SHARDED[n]] *= N_SHARD
            sz = 1
            for dim in fshape:
                sz *= dim
            full = red[off:off + sz].reshape(fshape)
            off += sz
            if n in _SMALL_SHARDED:
                ax = _SMALL_SHARDED[n]
                full = lax.dynamic_slice_in_dim(full, j * w[n].shape[ax], w[n].shape[ax], axis=ax)
            small_g[n] = full

    out_g, out_d, out_m, out_v = {}, {}, {}, {}
    for name in _BIG:
        shp = w[name].shape
        to3 = lambda a: a.reshape(DEPTH, -1, shp[-1])
        go, do, mo, vo = _adamw_big(to3(w[name]), big[name, 0], big[name, 1], to3(mom_m[name]), to3(mom_v[name]), name)
        out_g[name], out_d[name], out_m[name], out_v[name] = (a.reshape(shp) for a in (go, do, mo, vo))
    ds, ms, vs = _adamw_small([_as2d(w[n]) for n in small_names], [_as2d(small_g[n]) for n in small_names],
                              [_as2d(mom_m[n]) for n in small_names], [_as2d(mom_v[n]) for n in small_names])
    for n, d_, m_, v_ in zip(small_names, ds, ms, vs):
        out_g[n] = small_g[n]
        out_d[n], out_m[n], out_v[n] = d_.reshape(w[n].shape), m_.reshape(w[n].shape), v_.reshape(w[n].shape)

    return (loss, grad_x[None], *[out_g[n] for n in _WEIGHTS], *[out_d[n] for n in _WEIGHTS],
            *[out_m[n] for n in _WEIGHTS], *[out_v[n] for n in _WEIGHTS])
```

```python
import functools

import jax
import jax.numpy as jnp
from jax import lax
from jax.experimental import pallas as pl
from jax.experimental.pallas import tpu as pltpu

F32 = jnp.float32
BF16 = jnp.bfloat16

D = 2048
N_META = 16
CW = 512
CONV_K = 31
AW = 1024
KVW = 256
N_HEADS = 16
LW = 512
LRU_K = 4
LRU_C = 8.0
IN_TOTAL = 5120
ROT_HALF = 8
ROPE_THETA = 500000.0
LN_EPS = 1e-5
DEPTH = 2
ALPHA = (2.0 * DEPTH) ** 0.25
NEG_INF = -1e30
ADAM_LR, ADAM_B1, ADAM_B2, ADAM_EPS, ADAM_WD, ADAM_STEP = 0.001, 0.9, 0.999, 1e-08, 0.01, 10

BLK = 128
PAD = BLK - N_META
N_SHARD = 4
WIN_SH = IN_TOTAL // N_SHARD
WOUT_SH = D // N_SHARD
PW_SH = CW // N_SHARD
HALO = 32
LHALO = 8
V7X_VMEM_LIMIT = 60 * 1024 * 1024


def _cp(*sem):
    return pltpu.CompilerParams(dimension_semantics=sem if sem else None, vmem_limit_bytes=V7X_VMEM_LIMIT)


def _pick(total, prefs):
    for p in prefs:
        if total % p == 0:
            return p
    raise ValueError(f"no tile for {total}")


def _dot(a, b):
    return jnp.dot(a, b, preferred_element_type=F32)


def _dot_nt(a, b):
    return lax.dot_general(a, b, (((1,), (1,)), ((), ())), preferred_element_type=F32)


def _dot_tn(a, b):
    return lax.dot_general(a, b, (((0,), (0,)), ((), ())), preferred_element_type=F32)


def _sigmoid(x):
    return 1.0 / (1.0 + jnp.exp(-x))


def _silu_and_grad(x):
    s = _sigmoid(x)
    return x * s, s * (1.0 + x * (1.0 - s))


def _ln_rows(x, g, b):
    mu = jnp.mean(x, axis=-1, keepdims=True)
    xc = x - mu
    var = jnp.mean(xc * xc, axis=-1, keepdims=True)
    rstd = lax.rsqrt(var + LN_EPS)
    xhat = xc * rstd
    return xhat * g + b, xhat, rstd


def _ln_bwd_rows(dy, xhat, rstd, g):
    dxh = dy * g
    m1 = jnp.mean(dxh, axis=-1, keepdims=True)
    m2 = jnp.mean(dxh * xhat, axis=-1, keepdims=True)
    return rstd * (dxh - m1 - xhat * m2)


def _row_ids(n, base):
    return base + lax.broadcasted_iota(jnp.int32, (n, 1), 0)


def _colsum(x):
    return jnp.sum(x, axis=0, keepdims=True)


def _embed_fwd(x, meta, g, b, job=None):
    S = x.shape[0]
    nb = S // BLK + 1

    def body(x_ref, meta_ref, g_ref, b_ref, h_ref, hb_ref):
        n = pl.program_id(0)

        @pl.when(n == 0)
        def _():
            y, _, _ = _ln_rows(meta_ref[...], g_ref[...], b_ref[...])
            h_ref[...] = jnp.zeros_like(h_ref)
            h_ref[PAD:BLK, :] = y

        @pl.when(n > 0)
        def _():
            y, _, _ = _ln_rows(x_ref[...], g_ref[...], b_ref[...])
            h_ref[...] = y

        hb_ref[...] = h_ref[...].astype(BF16)

    return _side_call(
        body, job, name="embed_fwd", grid=(nb,),
        in_specs=[pl.BlockSpec((BLK, D), lambda n: (jnp.maximum(n - 1, 0), 0)),
                  pl.BlockSpec((N_META, D), lambda n: (0, 0)),
                  pl.BlockSpec((1, D), lambda n: (0, 0)),
                  pl.BlockSpec((1, D), lambda n: (0, 0))],
        out_specs=[pl.BlockSpec((BLK, D), lambda n: (n, 0)),
                   pl.BlockSpec((BLK, D), lambda n: (n, 0))],
        out_shape=[jax.ShapeDtypeStruct((nb * BLK, D), F32), jax.ShapeDtypeStruct((nb * BLK, D), BF16)],
        scratch_shapes=[], semantics=("arbitrary",), args=[x, meta, g, b])


def _embed_bwd(dh, x, meta, g, b):
    S = x.shape[0]
    nb = S // BLK + 1

    def body(dh_ref, x_ref, meta_ref, g_ref, b_ref, gx_ref, gm_ref, dg_ref, db_ref):
        n = pl.program_id(0)

        @pl.when(n == 0)
        def _():
            _, xhat, rstd = _ln_rows(meta_ref[...], g_ref[...], b_ref[...])
            dy = dh_ref[PAD:BLK, :]
            gm_ref[...] = _ln_bwd_rows(dy, xhat, rstd, g_ref[...])
            dg_ref[...] = _colsum(dy * xhat)
            db_ref[...] = _colsum(dy)

        @pl.when(n > 0)
        def _():
            _, xhat, rstd = _ln_rows(x_ref[...], g_ref[...], b_ref[...])
            dy = dh_ref[...]
            gx_ref[...] = _ln_bwd_rows(dy, xhat, rstd, g_ref[...])
            dg_ref[...] += _colsum(dy * xhat)
            db_ref[...] += _colsum(dy)

    prev = lambda n: (jnp.maximum(n - 1, 0), 0)
    const = lambda n: (0, 0)
    return pl.pallas_call(
        body, name="embed_bwd", grid=(nb,),
        in_specs=[pl.BlockSpec((BLK, D), lambda n: (n, 0)),
                  pl.BlockSpec((BLK, D), prev),
                  pl.BlockSpec((N_META, D), const),
                  pl.BlockSpec((1, D), const),
                  pl.BlockSpec((1, D), const)],
        out_specs=[pl.BlockSpec((BLK, D), prev),
                   pl.BlockSpec((N_META, D), const),
                   pl.BlockSpec((1, D), const),
                   pl.BlockSpec((1, D), const)],
        out_shape=[jax.ShapeDtypeStruct((S, D), F32), jax.ShapeDtypeStruct((N_META, D), F32),
                   jax.ShapeDtypeStruct((1, D), F32), jax.ShapeDtypeStruct((1, D), F32)],
        compiler_params=_cp("arbitrary"),
    )(dh, x, meta, g, b)


def _proj_fwd(hb, w_in, order, first, count, prev, l, job=None):
    T = hb.shape[0]
    tm = _pick(T, (1056, 384, 128))

    def body(o_sc, a_ref, w_ref, *rest):
        rest[-1][...] = _dot(a_ref[...], w_ref[...])

    return _side_call(
        body, job, name=f"proj_fwd{l}_{first}", grid=(T // tm, count),
        in_specs=[pl.BlockSpec((tm, D), lambda i, j, o: (i, 0)),
                  pl.BlockSpec((None, D, WIN_SH), lambda i, j, o: (o[first + j], 0, 0))]
        + ([] if prev is None else [pl.BlockSpec(memory_space=pl.ANY)]),
        out_specs=[pl.BlockSpec((tm, WIN_SH), lambda i, j, o: (i, o[first + j]))],
        out_shape=[jax.ShapeDtypeStruct((T, IN_TOTAL), F32)],
        scratch_shapes=[], semantics=("parallel", "arbitrary"),
        args=[hb, w_in] + ([] if prev is None else [prev]),
        aliases=None if prev is None else {2: 0}, prefetch=[order])


def _out_fwd(yc, ya, yl, w_out, h, g, b, l, job=None):
    T = h.shape[0]
    tm = _pick(T, (384, 128))

    def body(yc_ref, ya_ref, yl_ref, w_ref, h_ref, g_ref, b_ref, hn_ref, hnb_ref, xh_ref, rs_ref):
        acc = _dot(yc_ref[...], w_ref[0])
        acc += _dot(ya_ref[:, 0:WOUT_SH], w_ref[1])
        acc += _dot(ya_ref[:, WOUT_SH:2 * WOUT_SH], w_ref[2])
        acc += _dot(yl_ref[...], w_ref[3])
        z = ALPHA * h_ref[...] + acc
        y, xhat, rstd = _ln_rows(z, g_ref[...], b_ref[...])
        hn_ref[...] = y
        hnb_ref[...] = y.astype(BF16)
        xh_ref[...] = xhat
        rs_ref[...] = rstd

    row = lambda i: (i, 0)
    return _side_call(
        body, job, name=f"out_fwd{l}", grid=(T // tm,),
        in_specs=[pl.BlockSpec((tm, CW), row), pl.BlockSpec((tm, AW), row), pl.BlockSpec((tm, LW), row),
                  pl.BlockSpec((N_SHARD, WOUT_SH, D), lambda i: (0, 0, 0)),
                  pl.BlockSpec((tm, D), row),
                  pl.BlockSpec((None, 1, D), lambda i: (l, 0, 0)),
                  pl.BlockSpec((None, 1, D), lambda i: (l, 0, 0))],
        out_specs=[pl.BlockSpec((tm, D), row), pl.BlockSpec((tm, D), row), pl.BlockSpec((tm, D), row),
                   pl.BlockSpec((tm, 1), row)],
        out_shape=[jax.ShapeDtypeStruct((T, D), F32), jax.ShapeDtypeStruct((T, D), BF16),
                   jax.ShapeDtypeStruct((T, D), F32), jax.ShapeDtypeStruct((T, 1), F32)],
        scratch_shapes=[], semantics=("parallel",), args=[yc, ya, yl, w_out, h, g, b])


def _post_ln_dcat_bwd(src, target, xhat, rstd, g, w_out, l, job=None):
    T = src.shape[0]
    tm = _pick(T, (384, 128))
    per = tm // BLK if target is not None else 0
    last_blk = target.shape[0] // BLK - 1 if target is not None else 0

    def body(s_ref, *refs):
        t_refs = refs[:per]
        (xh_ref, rs_ref, g_ref, w_ref, part_ref, dz_ref, dzb_ref, dg_ref, db_ref, dc_ref, da_ref, dl_ref) = refs[per:]
        i = pl.program_id(0)

        @pl.when(i == 0)
        def _():
            part_ref[...] = jnp.zeros_like(part_ref)
            dg_ref[...] = jnp.zeros_like(dg_ref)
            db_ref[...] = jnp.zeros_like(db_ref)

        if per:
            tgt = jnp.concatenate([r[...] for r in t_refs], axis=0) if per > 1 else t_refs[0][...]
            real = _row_ids(tm, i * tm) >= BLK
            err = jnp.where(real, s_ref[...] - tgt, 0.0)
            part_ref[...] += _colsum(err * err) * (0.5 / D)
            dy = err * (1.0 / D)
        else:
            dy = s_ref[...]
        xhat = xh_ref[...]
        dz = _ln_bwd_rows(dy, xhat, rs_ref[...], g_ref[...])
        dzb = dz.astype(BF16)
        dz_ref[...] = dz
        dzb_ref[...] = dzb
        dg_ref[...] += _colsum(dy * xhat)
        db_ref[...] += _colsum(dy)
        dc_ref[...] = _dot_nt(dzb, w_ref[0])
        da_ref[:, 0:WOUT_SH] = _dot_nt(dzb, w_ref[1])
        da_ref[:, WOUT_SH:2 * WOUT_SH] = _dot_nt(dzb, w_ref[2])
        dl_ref[...] = _dot_nt(dzb, w_ref[3])

    row = lambda i: (i, 0)
    const = lambda i: (0, 0)
    t_specs = [pl.BlockSpec((BLK, D), functools.partial(lambda i, q: (jnp.clip(i * per - 1 + q, 0, last_blk), 0), q=q))
               for q in range(per)]
    return _side_call(
        body, job, name=f"post_ln_dcat_bwd{l}", grid=(T // tm,),
        in_specs=[pl.BlockSpec((tm, D), row)] + t_specs + [
            pl.BlockSpec((tm, D), row), pl.BlockSpec((tm, 1), row), pl.BlockSpec((None, 1, D), lambda i: (l, 0, 0)),
            pl.BlockSpec((N_SHARD, WOUT_SH, D), lambda i: (0, 0, 0))],
        out_specs=[pl.BlockSpec((1, D), const), pl.BlockSpec((tm, D), row), pl.BlockSpec((tm, D), row),
                   pl.BlockSpec((1, D), const), pl.BlockSpec((1, D), const),
                   pl.BlockSpec((tm, CW), row), pl.BlockSpec((tm, AW), row), pl.BlockSpec((tm, LW), row)],
        out_shape=[jax.ShapeDtypeStruct((1, D), F32), jax.ShapeDtypeStruct((T, D), F32),
                   jax.ShapeDtypeStruct((T, D), BF16), jax.ShapeDtypeStruct((1, D), F32),
                   jax.ShapeDtypeStruct((1, D), F32), jax.ShapeDtypeStruct((T, CW), F32),
                   jax.ShapeDtypeStruct((T, AW), F32), jax.ShapeDtypeStruct((T, LW), F32)],
        scratch_shapes=[], semantics=("arbitrary",),
        args=[src] + [target] * per + [xhat, rstd, g, w_out])


def _dwout_bwd(yc, ya, yl, dzb, l):
    T = dzb.shape[0]
    tm = _pick(T, (384, 128))

    def body(yc_ref, ya_ref, yl_ref, dz_ref, o_ref):
        @pl.when(pl.program_id(0) == 0)
        def _():
            o_ref[...] = jnp.zeros_like(o_ref)

        cat = jnp.concatenate([yc_ref[...], ya_ref[...], yl_ref[...]], axis=1)
        o_ref[...] += _dot_tn(cat, dz_ref[...])

    row = lambda t: (t, 0)
    out = pl.pallas_call(
        body, name=f"dwout_bwd{l}", grid=(T // tm,),
        in_specs=[pl.BlockSpec((tm, CW), row), pl.BlockSpec((tm, AW), row), pl.BlockSpec((tm, LW), row),
                  pl.BlockSpec((tm, D), row)],
        out_specs=pl.BlockSpec((D, D), lambda t: (0, 0)),
        out_shape=jax.ShapeDtypeStruct((D, D), F32),
        compiler_params=_cp("arbitrary"),
    )(yc, ya, yl, dzb)
    return out.reshape(N_SHARD, 2, WOUT_SH // 2, D)


def _dh_bwd(dproj, w_in, dz, l, job=None):
    T = dproj.shape[0]
    tm = _pick(T, (1056, 384, 128))

    n_w = len(w_in)

    def body(dp_ref, *refs):
        w_refs, (dz_ref, o_ref, acc_ref) = refs[:n_w], refs[n_w:]
        j = pl.program_id(1)

        @pl.when(j == 0)
        def _():
            acc_ref[...] = ALPHA * dz_ref[...]

        dp = dp_ref[...]
        off = 0
        for w_ref in w_refs:
            rows = w_ref.shape[0]
            acc_ref[:, off:off + rows] += _dot_nt(dp, w_ref[...])
            off += rows

        @pl.when(j == N_SHARD - 1)
        def _():
            o_ref[...] = acc_ref[...]

    return _side_call(
        body, job, name=f"dh_bwd{l}", grid=(T // tm, N_SHARD),
        in_specs=[pl.BlockSpec((tm, WIN_SH), lambda i, j: (i, j))]
        + [pl.BlockSpec((None, w.shape[1], WIN_SH), lambda i, j: (j, 0, 0)) for w in w_in]
        + [pl.BlockSpec((tm, D), lambda i, j: (i, 0))],
        out_specs=[pl.BlockSpec((tm, D), lambda i, j: (i, 0))],
        out_shape=[jax.ShapeDtypeStruct((T, D), F32)],
        scratch_shapes=[pltpu.VMEM((tm, D), F32)],
        semantics=("parallel", "arbitrary"), args=[dproj, *w_in, dz])


def _dwin_bwd(hb, dproj, l):
    T = hb.shape[0]
    tm = _pick(T, (1056, 384, 128))

    def body(h_ref, dp_ref, o_ref):
        @pl.when(pl.program_id(1) == 0)
        def _():
            o_ref[...] = jnp.zeros_like(o_ref)

        o_ref[...] += _dot_tn(h_ref[...], dp_ref[...])

    out = pl.pallas_call(
        body, name=f"dwin_bwd{l}", grid=(N_SHARD, T // tm),
        in_specs=[pl.BlockSpec((tm, D), lambda j, t: (t, 0)),
                  pl.BlockSpec((tm, WIN_SH), lambda j, t: (t, j))],
        out_specs=pl.BlockSpec((None, D, WIN_SH), lambda j, t: (j, 0, 0)),
        out_shape=jax.ShapeDtypeStruct((N_SHARD, D, WIN_SH), F32),
        compiler_params=_cp("parallel", "arbitrary"),
    )(hb, dproj)
    return out.reshape(N_SHARD, 2, D // 2, WIN_SH)


def _dwin_half(hb, dproj, which, l, tag, job=None):
    T = hb.shape[0]
    tm = _pick(T, (1056, 384, 128))
    hr = D // 2

    def body(w_ref, h_ref, dp_ref, o_ref):
        @pl.when(pl.program_id(1) == 0)
        def _():
            o_ref[...] = jnp.zeros_like(o_ref)

        o_ref[...] += _dot_tn(h_ref[...], dp_ref[...])

    return _side_call(
        body, job, name=f"dwin_{tag}{l}", grid=(N_SHARD, T // tm),
        in_specs=[pl.BlockSpec((tm, hr), lambda j, t, w: (t, w[0])),
                  pl.BlockSpec((tm, WIN_SH), lambda j, t, w: (t, j))],
        out_specs=[pl.BlockSpec((None, hr, WIN_SH), lambda j, t, w: (j, 0, 0))],
        out_shape=[jax.ShapeDtypeStruct((N_SHARD, hr, WIN_SH), F32)],
        scratch_shapes=[], semantics=("parallel", "arbitrary"), args=[hb, dproj],
        prefetch=[jnp.reshape(which, (1,)).astype(jnp.int32)])


def _glu_masked(v, g, base_row):
    rows = _row_ids(v.shape[0], base_row)
    return jnp.where(rows >= PAD, v * _sigmoid(g), 0.0)


def _conv_tile(T):
    return _pick(T, (384, 128))


SUBLANES = 8


def _for_each_shift(buf, rot, tm, offsets, fn):
    for r in range(SUBLANES):
        group = [o for o in offsets if o % SUBLANES == r]
        if not group:
            continue
        if r == 0:
            src = buf
        else:
            n = tm + max(group) - r
            rot[0:n, :] = buf[r:r + n, :]
            src = rot
        for o in group:
            fn(o, src[o - r:o - r + tm, :])


def _conv_fwd(proj, dw_w, dw_b, ln_g, ln_b, pw_w, pw_b, l, job=None):
    T = proj.shape[0]
    tm = _conv_tile(T)
    hb = tm // HALO

    def body(cv_ref, cg_ref, ct_ref, hv_ref, hg_ref, w_ref, b_ref, g_ref, be_ref, pw_ref, pb_ref,
             yc_ref, conv_ref, buf, rot):
        i = pl.program_id(0)
        buf[0:HALO, :] = _glu_masked(hv_ref[...], hg_ref[...], i * tm - HALO)
        buf[HALO:HALO + tm, :] = _glu_masked(cv_ref[...], cg_ref[...], i * tm)
        first = HALO - (CONV_K - 1)
        total = [jnp.zeros((tm, CW), F32) + b_ref[...]]

        def tap(o, tile):
            k = o - first
            total[0] = total[0] + w_ref[k:k + 1, :] * tile

        _for_each_shift(buf, rot, tm, [first + k for k in range(CONV_K)], tap)
        acc = total[0]
        conv_ref[...] = acc
        u, _, _ = _ln_rows(acc, g_ref[...], be_ref[...])
        s = u * _sigmoid(u)
        cpw = _dot(s.astype(BF16), pw_ref[...]) + pb_ref[...]
        gate, _ = _silu_and_grad(ct_ref[...])
        yc_ref[...] = (cpw * gate).astype(BF16)

    vec = pl.BlockSpec((None, 1, CW), lambda i: (l, 0, 0))
    return _side_call(
        body, job, name=f"conv_fwd{l}", grid=(T // tm,),
        in_specs=[pl.BlockSpec((tm, CW), lambda i: (i, 0)),
                  pl.BlockSpec((tm, CW), lambda i: (i, 1)),
                  pl.BlockSpec((tm, CW), lambda i: (i, 2)),
                  pl.BlockSpec((HALO, CW), lambda i: (jnp.maximum(i * hb - 1, 0), 0)),
                  pl.BlockSpec((HALO, CW), lambda i: (jnp.maximum(i * hb - 1, 0), 1)),
                  pl.BlockSpec((None, CONV_K, CW), lambda i: (l, 0, 0)),
                  vec, vec, vec,
                  pl.BlockSpec((CW, CW), lambda i: (0, 0)),
                  vec],
        out_specs=[pl.BlockSpec((tm, CW), lambda i: (i, 0)), pl.BlockSpec((tm, CW), lambda i: (i, 0))],
        out_shape=[jax.ShapeDtypeStruct((T, CW), BF16), jax.ShapeDtypeStruct((T, CW), F32)],
        scratch_shapes=[pltpu.VMEM((tm + HALO, CW), F32), pltpu.VMEM((tm + HALO, CW), F32)],
        semantics=("parallel",), args=[proj, proj, proj, proj, proj, dw_w, dw_b, ln_g, ln_b, pw_w, pw_b])


def _conv_bwd_rows(conv, proj, d_yc, ln_g, ln_b, pw_w, pw_b, l):
    T = conv.shape[0]
    tm = _conv_tile(T)

    def body(conv_ref, ct_ref, dy_ref, g_ref, be_ref, pw_ref, pb_ref,
             dconv_ref, dct_ref, dpw_ref, dpb_ref, dg_ref, db_ref):
        @pl.when(pl.program_id(0) == 0)
        def _():
            dpw_ref[...] = jnp.zeros_like(dpw_ref)
            dpb_ref[...] = jnp.zeros_like(dpb_ref)
            dg_ref[...] = jnp.zeros_like(dg_ref)
            db_ref[...] = jnp.zeros_like(db_ref)

        u, xhat, rstd = _ln_rows(conv_ref[...], g_ref[...], be_ref[...])
        s, ds_du = _silu_and_grad(u)
        sb = s.astype(BF16)
        cpw = _dot(sb, pw_ref[...]) + pb_ref[...]
        gate, dgate = _silu_and_grad(ct_ref[...])
        dy = dy_ref[...]
        d_cpw = dy * gate
        dct_ref[...] = (dy * cpw * dgate).astype(BF16)
        d_cpw_b = d_cpw.astype(BF16)
        dpb_ref[...] += _colsum(d_cpw)
        dpw_ref[...] += _dot_tn(sb, d_cpw_b)
        du = _dot_nt(d_cpw_b, pw_ref[...]) * ds_du
        dconv_ref[...] = _ln_bwd_rows(du, xhat, rstd, g_ref[...])
        dg_ref[...] += _colsum(du * xhat)
        db_ref[...] += _colsum(du)

    vec = pl.BlockSpec((None, 1, CW), lambda i: (l, 0, 0))
    row = lambda i: (i, 0)
    const = lambda i: (0, 0)
    return pl.pallas_call(
        body, name=f"conv_bwd_rows{l}", grid=(T // tm,),
        in_specs=[pl.BlockSpec((tm, CW), row), pl.BlockSpec((tm, CW), lambda i: (i, 2)),
                  pl.BlockSpec((tm, CW), row), vec, vec,
                  pl.BlockSpec((CW, CW), lambda i: (0, 0)), vec],
        out_specs=[pl.BlockSpec((tm, CW), row), pl.BlockSpec((tm, CW), lambda i: (i, 2)),
                   pl.BlockSpec((CW, CW), const), pl.BlockSpec((1, CW), const),
                   pl.BlockSpec((1, CW), const), pl.BlockSpec((1, CW), const)],
        out_shape=[jax.ShapeDtypeStruct((T, CW), F32), jax.ShapeDtypeStruct((T, IN_TOTAL), BF16),
                   jax.ShapeDtypeStruct((CW, CW), F32), jax.ShapeDtypeStruct((1, CW), F32),
                   jax.ShapeDtypeStruct((1, CW), F32), jax.ShapeDtypeStruct((1, CW), F32)],
        compiler_params=_cp("arbitrary"),
    )(conv, proj, d_yc, ln_g, ln_b, pw_w, pw_b)


def _conv_bwd_taps(d_conv, proj, dw_w, dproj, l, job=None):
    T = d_conv.shape[0]
    tm = _conv_tile(T)
    hb = tm // HALO
    nt = T // tm
    last_halo = T // HALO - 1

    def body(dc_ref, dh_ref, cv_ref, cg_ref, hv_ref, hg_ref, w_ref, _, o_ref, dw_ref, dwb_ref, cbuf, dbuf, rot):
        i = pl.program_id(0)

        @pl.when(i == 0)
        def _():
            dw_ref[...] = jnp.zeros_like(dw_ref)
            dwb_ref[...] = jnp.zeros_like(dwb_ref)

        cbuf[0:HALO, :] = _glu_masked(hv_ref[...], hg_ref[...], i * tm - HALO)
        cbuf[HALO:HALO + tm, :] = _glu_masked(cv_ref[...], cg_ref[...], i * tm)
        dmain = dc_ref[...]
        dbuf[0:tm, :] = dmain
        dbuf[tm:tm + HALO, :] = jnp.where(i < nt - 1, dh_ref[...], 0.0)
        total = [jnp.zeros((tm, CW), F32)]

        def tap_back(o, tile):
            k = CONV_K - 1 - o
            total[0] = total[0] + w_ref[k:k + 1, :] * tile

        _for_each_shift(dbuf, rot, tm, list(range(CONV_K)), tap_back)
        acc = total[0]
        first = HALO - (CONV_K - 1)

        def tap_weight(o, tile):
            k = o - first
            dw_ref[k:k + 1, :] += _colsum(dmain * tile)

        _for_each_shift(cbuf, rot, tm, [first + k for k in range(CONV_K)], tap_weight)
        dwb_ref[...] += _colsum(dmain)
        d_c = jnp.where(_row_ids(tm, i * tm) >= PAD, acc, 0.0)
        sig = _sigmoid(cg_ref[...])
        o_ref[:, 0:CW] = (d_c * sig).astype(BF16)
        o_ref[:, CW:2 * CW] = (d_c * cv_ref[...] * sig * (1.0 - sig)).astype(BF16)

    const = lambda i: (0, 0)
    return _side_call(
        body, job, name=f"conv_bwd_taps{l}", grid=(nt,),
        in_specs=[pl.BlockSpec((tm, CW), lambda i: (i, 0)),
                  pl.BlockSpec((HALO, CW), lambda i: (jnp.minimum((i + 1) * hb, last_halo), 0)),
                  pl.BlockSpec((tm, CW), lambda i: (i, 0)),
                  pl.BlockSpec((tm, CW), lambda i: (i, 1)),
                  pl.BlockSpec((HALO, CW), lambda i: (jnp.maximum(i * hb - 1, 0), 0)),
                  pl.BlockSpec((HALO, CW), lambda i: (jnp.maximum(i * hb - 1, 0), 1)),
                  pl.BlockSpec((None, CONV_K, CW), lambda i: (l, 0, 0)),
                  pl.BlockSpec(memory_space=pl.ANY)],
        out_specs=[pl.BlockSpec((tm, 2 * CW), lambda i: (i, 0)),
                   pl.BlockSpec((HALO, CW), const), pl.BlockSpec((1, CW), const)],
        out_shape=[jax.ShapeDtypeStruct(dproj.shape, BF16), jax.ShapeDtypeStruct((HALO, CW), F32),
                   jax.ShapeDtypeStruct((1, CW), F32)],
        scratch_shapes=[pltpu.VMEM((tm + HALO, CW), F32), pltpu.VMEM((tm + HALO, CW), F32),
                        pltpu.VMEM((tm + HALO, CW), F32)],
        semantics=("arbitrary",), aliases={7: 0},
        args=[d_conv, d_conv, proj, proj, proj, proj, dw_w, dproj])


def _log1p_small(e):
    return jnp.where(e < 1e-3, e * (1.0 - e * (0.5 - e * (1.0 / 3.0))), jnp.log(1.0 + e))


def _softplus(z):
    return jnp.maximum(z, 0.0) + _log1p_small(jnp.exp(-jnp.abs(z)))


def _neg_expm1(x):
    series = -x * (1.0 + x * (1.0 / 2.0) * (1.0 + x * (1.0 / 3.0) * (1.0 + x * (1.0 / 4.0) * (
        1.0 + x * (1.0 / 5.0) * (1.0 + x * (1.0 / 6.0) * (1.0 + x * (1.0 / 7.0)))))))
    return jnp.where(x > -0.25, series, 1.0 - jnp.exp(x))


def _lru_gates(rxbuf, tm, base_row, lw_ref, lb_ref, wa_ref, ba_ref, wx_ref, bx_ref, lam_ref):
    rc = jnp.zeros((tm, LW), F32) + lb_ref[...]
    for k in range(LRU_K):
        o = LHALO - (LRU_K - 1) + k
        rc += lw_ref[k:k + 1, :] * rxbuf[o:o + tm, :]
    rcb = rc.astype(BF16)
    r = _sigmoid(_dot(rcb, wa_ref[...]) + ba_ref[...])
    ig = _sigmoid(_dot(rcb, wx_ref[...]) + bx_ref[...])
    sp = _softplus(-lam_ref[...])
    la = -LRU_C * r * sp
    a = jnp.exp(la)
    mult = jnp.sqrt(_neg_expm1(2.0 * la))
    valid = _row_ids(tm, base_row) >= PAD
    return rc, rcb, r, ig, sp, a, mult, valid


def _mask_rows(v, base_row):
    return jnp.where(_row_ids(v.shape[0], base_row) >= PAD, v, 0.0)


def _scan_rows(aa, bb, carry, out_ref, reverse):
    tm = aa.shape[0]
    sub = _row_ids(tm, 0) & (SUBLANES - 1)
    s = 1
    while s < SUBLANES:
        keep = (sub < SUBLANES - s) if reverse else (sub >= s)
        shift = tm - s if reverse else s
        a_s = jnp.where(keep, pltpu.roll(aa, shift, axis=0), 1.0)
        b_s = jnp.where(keep, pltpu.roll(bb, shift, axis=0), 0.0)
        bb = aa * b_s + bb
        aa = aa * a_s
        s *= 2
    groups = range(tm // SUBLANES)
    edge = 0 if reverse else SUBLANES - 1
    for j in (reversed(groups) if reverse else groups):
        rows = slice(SUBLANES * j, SUBLANES * j + SUBLANES)
        x = bb[rows] + aa[rows] * carry
        out_ref[rows, :] = x
        carry = x[edge:edge + 1]


def _lru_tile(T):
    return _pick(T, (384, 128))


def _lru_fwd(proj, lw, lb, wa, ba, wx, bx, lam, l, job=None):
    T = proj.shape[0]
    tm = _lru_tile(T)
    hb = tm // LHALO

    def body(rx_ref, rg_ref, hx_ref, lw_ref, lb_ref, wa_ref, ba_ref, wx_ref, bx_ref, lam_ref,
             yl_ref, hl_ref, rxbuf, carry):
        i = pl.program_id(0)

        @pl.when(i == 0)
        def _():
            carry[...] = jnp.zeros_like(carry)

        rxbuf[0:LHALO, :] = _mask_rows(hx_ref[...], i * tm - LHALO)
        rxbuf[LHALO:LHALO + tm, :] = _mask_rows(rx_ref[...], i * tm)
        rc, _, _, ig, _, a, mult, valid = _lru_gates(rxbuf, tm, i * tm, lw_ref, lb_ref, wa_ref, ba_ref,
                                                     wx_ref, bx_ref, lam_ref)
        bb = jnp.where(valid, mult * (ig * rc), 0.0)
        _scan_rows(a, bb, carry[0:1, :], hl_ref, reverse=False)
        carry[0:1, :] = hl_ref[tm - 1:tm, :]
        gate, _ = _silu_and_grad(rg_ref[...])
        yl_ref[...] = (hl_ref[...] * gate).astype(BF16)

    vec = pl.BlockSpec((None, 1, LW), lambda i: (l, 0, 0))
    mat = pl.BlockSpec((None, LW, LW), lambda i: (l, 0, 0))
    return _side_call(
        body, job, name=f"lru_fwd{l}", grid=(T // tm,),
        in_specs=[pl.BlockSpec((tm, LW), lambda i: (i, 8)),
                  pl.BlockSpec((tm, LW), lambda i: (i, 9)),
                  pl.BlockSpec((LHALO, LW), lambda i: (jnp.maximum(i * hb - 1, 0), 8)),
                  pl.BlockSpec((None, LRU_K, LW), lambda i: (l, 0, 0)),
                  vec, mat, vec, mat, vec, vec],
        out_specs=[pl.BlockSpec((tm, LW), lambda i: (i, 0)), pl.BlockSpec((tm, LW), lambda i: (i, 0))],
        out_shape=[jax.ShapeDtypeStruct((T, LW), BF16), jax.ShapeDtypeStruct((T, LW), F32)],
        scratch_shapes=[pltpu.VMEM((tm + LHALO, LW), F32), pltpu.VMEM((8, LW), F32)],
        semantics=("arbitrary",), args=[proj, proj, proj, lw, lb, wa, ba, wx, bx, lam])


def _lru_bwd(proj, hl, d_yl, lw, lb, wa, ba, wx, bx, lam, dproj, l, job=None):
    T = proj.shape[0]
    tm = _lru_tile(T)
    hb = tm // LHALO
    nt = T // tm

    def body(rx_ref, rg_ref, hx_ref, hl_ref, hh_ref, dy_ref, lw_ref, lb_ref, wa_ref, ba_ref, wx_ref, bx_ref,
             lam_ref, _, o_ref, dlw_ref, dlb_ref, dwa_ref, dba_ref, dwx_ref, dbx_ref, dlam_ref,
             rxbuf, dbuf, carry, head, gbuf):
        step = pl.program_id(0)
        i = nt - 1 - step

        @pl.when(step == 0)
        def _():
            carry[...] = jnp.zeros_like(carry)
            head[...] = jnp.zeros_like(head)
            for ref in (dlw_ref, dlb_ref, dwa_ref, dba_ref, dwx_ref, dbx_ref, dlam_ref):
                ref[...] = jnp.zeros_like(ref)

        rxbuf[0:LHALO, :] = _mask_rows(hx_ref[...], i * tm - LHALO)
        rxbuf[LHALO:LHALO + tm, :] = _mask_rows(rx_ref[...], i * tm)
        rc, rcb, r, ig, sp, a, mult, valid = _lru_gates(rxbuf, tm, i * tm, lw_ref, lb_ref, wa_ref, ba_ref,
                                                        wx_ref, bx_ref, lam_ref)
        rows = _row_ids(tm, 0)
        h = hl_ref[...]
        h_before = jnp.where(i > 0, hh_ref[LHALO - 1:LHALO, :], 0.0)
        hprev = jnp.where(rows == 0, h_before, pltpu.roll(h, 1, axis=0))
        rg = rg_ref[...]
        gate, dgate = _silu_and_grad(rg)
        dy = dy_ref[...]
        o_ref[:, LW:2 * LW] = (dy * h * dgate).astype(BF16)
        bb = dy * gate + jnp.where(rows == tm - 1, carry[0:1, :], 0.0)
        aa = jnp.where(rows == tm - 1, 0.0, pltpu.roll(a, tm - 1, axis=0))
        _scan_rows(aa, bb, jnp.zeros((1, LW), F32), gbuf, reverse=True)
        g = gbuf[...]
        dbuf[0:tm, :] = a * g
        carry[0:1, :] = dbuf[0:1, :]
        du = jnp.where(valid, g, 0.0)
        da = g * hprev
        dix = du * mult
        dmult = du * (ig * rc)
        dla = jnp.where(valid, da * a - dmult * (a * a) / mult, 0.0)
        dr = dla * (-LRU_C * sp)
        dlam_ref[...] += _colsum(dla * (LRU_C * r)) * _sigmoid(-lam_ref[...])
        dpa = dr * r * (1.0 - r)
        dpx = (dix * rc) * ig * (1.0 - ig)
        dpab = dpa.astype(BF16)
        dpxb = dpx.astype(BF16)
        dba_ref[...] += _colsum(dpa)
        dbx_ref[...] += _colsum(dpx)
        dwa_ref[...] += _dot_tn(rcb, dpab)
        dwx_ref[...] += _dot_tn(rcb, dpxb)
        drc = dix * ig + _dot_nt(dpab, wa_ref[...]) + _dot_nt(dpxb, wx_ref[...])
        dbuf[0:tm, :] = drc
        dbuf[tm:tm + LHALO, :] = head[...]
        acc = jnp.zeros((tm, LW), F32)
        for k in range(LRU_K):
            o = LRU_K - 1 - k
            acc += lw_ref[k:k + 1, :] * dbuf[o:o + tm, :]
            oc = LHALO - (LRU_K - 1) + k
            dlw_ref[k:k + 1, :] += _colsum(drc * rxbuf[oc:oc + tm, :])
        dlb_ref[...] += _colsum(drc)
        head[...] = dbuf[0:LHALO, :]
        o_ref[:, 0:LW] = jnp.where(valid, acc, 0.0).astype(BF16)

    rev = lambda s: nt - 1 - s
    vec = pl.BlockSpec((None, 1, LW), lambda s: (l, 0, 0))
    mat = pl.BlockSpec((None, LW, LW), lambda s: (l, 0, 0))
    const = lambda s: (0, 0)
    halo = lambda s: jnp.maximum(rev(s) * hb - 1, 0)
    return _side_call(
        body, job, name=f"lru_bwd{l}", grid=(nt,),
        in_specs=[pl.BlockSpec((tm, LW), lambda s: (rev(s), 8)),
                  pl.BlockSpec((tm, LW), lambda s: (rev(s), 9)),
                  pl.BlockSpec((LHALO, LW), lambda s: (halo(s), 8)),
                  pl.BlockSpec((tm, LW), lambda s: (rev(s), 0)),
                  pl.BlockSpec((LHALO, LW), lambda s: (halo(s), 0)),
                  pl.BlockSpec((tm, LW), lambda s: (rev(s), 0)),
                  pl.BlockSpec((None, LRU_K, LW), lambda s: (l, 0, 0)),
                  vec, mat, vec, mat, vec, vec, pl.BlockSpec(memory_space=pl.ANY)],
        out_specs=[pl.BlockSpec((tm, 2 * LW), lambda s: (rev(s), 4)),
                   pl.BlockSpec((8, LW), const), pl.BlockSpec((1, LW), const),
                   pl.BlockSpec((LW, LW), const), pl.BlockSpec((1, LW), const),
                   pl.BlockSpec((LW, LW), const), pl.BlockSpec((1, LW), const),
                   pl.BlockSpec((1, LW), const)],
        out_shape=[jax.ShapeDtypeStruct(dproj.shape, BF16),
                   jax.ShapeDtypeStruct((8, LW), F32), jax.ShapeDtypeStruct((1, LW), F32),
                   jax.ShapeDtypeStruct((LW, LW), F32), jax.ShapeDtypeStruct((1, LW), F32),
                   jax.ShapeDtypeStruct((LW, LW), F32), jax.ShapeDtypeStruct((1, LW), F32),
                   jax.ShapeDtypeStruct((1, LW), F32)],
        scratch_shapes=[pltpu.VMEM((tm + LHALO, LW), F32), pltpu.VMEM((tm + LHALO, LW), F32),
                        pltpu.VMEM((8, LW), F32), pltpu.VMEM((LHALO, LW), F32), pltpu.VMEM((tm, LW), F32)],
        semantics=("arbitrary",), aliases={13: 0},
        args=[proj, proj, proj, hl, hl, d_yl, lw, lb, wa, ba, wx, bx, lam, dproj])


def _rope_tables(T):
    pos = (lax.broadcasted_iota(jnp.int32, (T, 128), 0) - PAD).astype(F32)
    lane = lax.broadcasted_iota(jnp.int32, (T, 128), 1) % 64
    inv_freq = ROPE_THETA ** (-(lane % ROT_HALF).astype(F32) / ROT_HALF)
    ang = pos * inv_freq
    cos, sin = jnp.cos(ang), jnp.sin(ang)
    c = jnp.where(lane < 2 * ROT_HALF, cos, 1.0)
    s1 = jnp.where(lane < ROT_HALF, -sin, 0.0)
    s2 = jnp.where((lane >= ROT_HALF) & (lane < 2 * ROT_HALF), sin, 0.0)
    return c, s1, s2


def _rot_fwd(x, c, s1, s2):
    return x * c + pltpu.roll(x, 128 - ROT_HALF, axis=1) * s1 + pltpu.roll(x, ROT_HALF, axis=1) * s2


def _rot_bwd(dy, c, s1, s2):
    return dy * c + pltpu.roll(dy * s1, ROT_HALF, axis=1) + pltpu.roll(dy * s2, 128 - ROT_HALF, axis=1)


def _rope_fwd(proj, tabs, l):
    T = proj.shape[0]

    def body(ql_ref, qh_ref, k_ref, v_ref, c_ref, s1_ref, s2_ref, qr_ref, kr_ref, vb_ref):
        c, s1, s2 = c_ref[...], s1_ref[...], s2_ref[...]
        for gcol in range(AW // 128):
            src = ql_ref if gcol < 4 else qh_ref
            x = src[:, 128 * (gcol % 4):128 * (gcol % 4) + 128]
            qr_ref[:, 128 * gcol:128 * gcol + 128] = (_rot_fwd(x, c, s1, s2) * 0.125).astype(BF16)
        for gcol in range(KVW // 128):
            x = k_ref[:, 128 * gcol:128 * gcol + 128]
            kr_ref[:, 128 * gcol:128 * gcol + 128] = _rot_fwd(x, c, s1, s2).astype(BF16)
        vb_ref[...] = v_ref[...].astype(BF16)

    tr = _pick(T, (384, 128))
    tab = pl.BlockSpec((tr, 128), lambda n: (n, 0))
    return pl.pallas_call(
        body, name=f"rope_fwd{l}", grid=(T // tr,),
        in_specs=[pl.BlockSpec((tr, 512), lambda n: (n, 3)), pl.BlockSpec((tr, 512), lambda n: (n, 4)),
                  pl.BlockSpec((tr, KVW), lambda n: (n, 10)), pl.BlockSpec((tr, KVW), lambda n: (n, 11)),
                  tab, tab, tab],
        out_specs=[pl.BlockSpec((tr, AW), lambda n: (n, 0)), pl.BlockSpec((tr, KVW), lambda n: (n, 0)),
                   pl.BlockSpec((tr, KVW), lambda n: (n, 0))],
        out_shape=[jax.ShapeDtypeStruct((T, AW), BF16), jax.ShapeDtypeStruct((T, KVW), BF16),
                   jax.ShapeDtypeStruct((T, KVW), BF16)],
        compiler_params=_cp("parallel"),
    )(proj, proj, proj, proj, *tabs)


GROUP = 4


def _attn_mask(n, reps):
    qi = lax.broadcasted_iota(jnp.int32, (reps * BLK, BLK), 0) & (BLK - 1)
    kj = lax.broadcasted_iota(jnp.int32, (reps * BLK, BLK), 1)
    m0 = (kj >= PAD) & (n >= 1)
    mp = (kj > qi) & (n >= 2)
    mc = (kj <= qi) & ((n >= 1) | (kj >= PAD))
    return jnp.concatenate([m0, mp, mc], axis=1)


def _kv_both(x0_ref, xp_ref, xc_ref, g):
    pg, off = g // 2, g % 2
    cols = slice(128 * pg, 128 * pg + 128)
    x = jnp.concatenate([x0_ref[:, cols], xp_ref[:, cols], xc_ref[:, cols]], axis=0).astype(F32)
    lane = lax.broadcasted_iota(jnp.int32, (1, 128), 1)
    half = jnp.where((lane < 64) if off == 0 else (lane >= 64), x, 0.0)
    return (half + pltpu.roll(half, 64, axis=1)).astype(BF16)


def _kv_halves(x0_ref, xp_ref, xc_ref, g):
    pg, off = g // 2, g % 2
    cols = slice(128 * pg, 128 * pg + 128)
    x = jnp.concatenate([x0_ref[:, cols], xp_ref[:, cols], xc_ref[:, cols]], axis=0).astype(F32)
    lane = lax.broadcasted_iota(jnp.int32, (1, 128), 1)
    if off == 0:
        lo = jnp.where(lane < 64, x, 0.0)
        hi = pltpu.roll(lo, 64, axis=1)
    else:
        hi = jnp.where(lane >= 64, x, 0.0)
        lo = pltpu.roll(hi, 64, axis=1)
    return lo.astype(BF16), hi.astype(BF16)


def _stack_heads(a, b):
    lo = lax.broadcasted_iota(jnp.int32, (1, 128), 1) < 64
    a, b = a.astype(F32), b.astype(F32)
    return jnp.concatenate([jnp.where(lo, a, 0.0), jnp.where(lo, 0.0, a),
                            jnp.where(lo, b, 0.0), jnp.where(lo, 0.0, b)], axis=0).astype(BF16)


def _unstack_heads(x):
    lo = lax.broadcasted_iota(jnp.int32, (1, 128), 1) < 64
    return (jnp.where(lo, x[0:BLK], x[BLK:2 * BLK]), jnp.where(lo, x[2 * BLK:3 * BLK], x[3 * BLK:4 * BLK]))


def _per_head_column(values):
    return jnp.concatenate([jnp.zeros((BLK, 1), F32) + v for v in values], axis=0)


def _attn_fwd(qr, kr, vb, proj, sinks, l, job=None):
    T = qr.shape[0]

    def body(sink_ref, q_ref, k0_ref, kp_ref, kc_ref, v0_ref, vp_ref, vc_ref, ag_ref, ya_ref, att_ref, lse_ref):
        n = pl.program_id(0)
        mask = _attn_mask(n, 1)
        lane = lax.broadcasted_iota(jnp.int32, (1, 128), 1)
        lse_acc = jnp.zeros((BLK, 128), F32)
        for g in range(4):
            kx = _kv_both(k0_ref, kp_ref, kc_ref, g)
            vx = _kv_both(v0_ref, vp_ref, vc_ref, g)
            pair_cols = [slice(128 * (2 * g + pp), 128 * (2 * g + pp) + 128) for pp in range(2)]
            s4 = _dot_nt(_stack_heads(q_ref[:, pair_cols[0]], q_ref[:, pair_cols[1]]), kx)
            probs = []
            for r in range(GROUP):
                h = GROUP * g + r
                sink = sink_ref[l, h]
                s = jnp.where(mask, s4[BLK * r:BLK * r + BLK], NEG_INF)
                m = jnp.maximum(jnp.max(s, axis=1, keepdims=True), sink)
                p = jnp.exp(s - m)
                denom = jnp.sum(p, axis=1, keepdims=True) + jnp.exp(sink - m)
                probs.append((p * (1.0 / denom)).astype(BF16))
                lse_acc = jnp.where(lane == h, m + jnp.log(denom), lse_acc)
            outs = _unstack_heads(_dot(jnp.concatenate(probs, axis=0), vx))
            for cols, out in zip(pair_cols, outs):
                att_ref[:, cols] = out
                gate, _ = _silu_and_grad(ag_ref[:, cols])
                ya_ref[:, cols] = (out * gate).astype(BF16)
        lse_ref[...] = lse_acc

    prev = lambda n: (jnp.maximum(n - 1, 0), 0)
    cur = lambda n: (n, 0)
    zero = lambda n: (0, 0)
    kv = lambda f: pl.BlockSpec((BLK, KVW), f)
    return _side_call(
        body, job, name=f"attn_fwd{l}", grid=(T // BLK,),
        in_specs=[pl.BlockSpec(memory_space=pltpu.SMEM),
                  pl.BlockSpec((BLK, AW), cur), kv(zero), kv(prev), kv(cur), kv(zero), kv(prev), kv(cur),
                  pl.BlockSpec((BLK, AW), lambda n: (n, 3))],
        out_specs=[pl.BlockSpec((BLK, AW), cur), pl.BlockSpec((BLK, AW), cur), pl.BlockSpec((BLK, 128), cur)],
        out_shape=[jax.ShapeDtypeStruct((T, AW), BF16), jax.ShapeDtypeStruct((T, AW), F32),
                   jax.ShapeDtypeStruct((T, 128), F32)],
        scratch_shapes=[], semantics=("parallel",), args=[sinks, qr, kr, kr, kr, vb, vb, vb, proj])


def _attn_bwd(qr, kr, vb, proj, att, lse, d_ya, sinks, dproj, l, job=None):
    T = qr.shape[0]
    nb = T // BLK

    def body(sink_ref, q_ref, k0_ref, kp_ref, kc_ref, v0_ref, vp_ref, vc_ref, ag_ref, att_ref, lse_ref, dy_ref, _,
             dq_ref, dk_ref, dv_ref, dk0_ref, dv0_ref, dag_ref, dsink_ref, kcarry, vcarry):
        n = pl.program_id(0)

        @pl.when(n == 0)
        def _():
            dk0_ref[...] = jnp.zeros_like(dk0_ref)
            dv0_ref[...] = jnp.zeros_like(dv0_ref)
            dsink_ref[...] = jnp.zeros_like(dsink_ref)
            kcarry[...] = jnp.zeros_like(kcarry)
            vcarry[...] = jnp.zeros_like(vcarry)

        @pl.when(n == nb)
        def _():
            dk_ref[...] = kcarry[...]
            dv_ref[...] = vcarry[...]

        @pl.when(n < nb)
        def _():
            mask = _attn_mask(n, GROUP)
            lane = lax.broadcasted_iota(jnp.int32, (1, 128), 1)
            lse = lse_ref[...]
            dsink = jnp.zeros((1, 128), F32)
            dk_pg, dv_pg = [], []
            for pg in range(2):
                dk_acc = jnp.zeros((3 * BLK, 128), F32)
                dv_acc = jnp.zeros((3 * BLK, 128), F32)
                for off in range(2):
                    g = 2 * pg + off
                    kx = _kv_both(k0_ref, kp_ref, kc_ref, g)
                    vx = _kv_both(v0_ref, vp_ref, vc_ref, g)
                    pair_cols = [slice(128 * (2 * g + pp), 128 * (2 * g + pp) + 128) for pp in range(2)]
                    q4 = _stack_heads(q_ref[:, pair_cols[0]], q_ref[:, pair_cols[1]])
                    d_out = []
                    for cols in pair_cols:
                        gate, dgate = _silu_and_grad(ag_ref[:, cols])
                        dy = dy_ref[:, cols]
                        dag_ref[:, cols] = (dy * att_ref[:, cols] * dgate).astype(BF16)
                        d_out.append(dy * gate)
                    do4 = _stack_heads(d_out[0], d_out[1])
                    heads = [GROUP * g + r for r in range(GROUP)]
                    sink = _per_head_column([sink_ref[l, h] for h in heads])
                    lse4 = _per_head_column(
                        [jnp.sum(jnp.where(lane == h, lse, 0.0), axis=1, keepdims=True) for h in heads])
                    p = jnp.where(mask, jnp.exp(_dot_nt(q4, kx) - lse4), 0.0)
                    dp = _dot_nt(do4, vx)
                    delta = jnp.sum(p * dp, axis=1, keepdims=True)
                    ds = (p * (dp - delta)).astype(BF16)
                    sink_term = jnp.exp(sink - lse4) * delta
                    for r, h in enumerate(heads):
                        dsink += jnp.where(lane == h, -jnp.sum(sink_term[BLK * r:BLK * r + BLK]), 0.0)
                    for cols, dq in zip(pair_cols, _unstack_heads(_dot(ds, kx))):
                        dq_ref[:, cols] = dq
                    dkg = _dot_tn(ds, q4)
                    dvg = _dot_tn(p.astype(BF16), do4)
                    own = (lane < 64) if off == 0 else (lane >= 64)
                    dk_acc += jnp.where(own, dkg + pltpu.roll(dkg, 64, axis=1), 0.0)
                    dv_acc += jnp.where(own, dvg + pltpu.roll(dvg, 64, axis=1), 0.0)
                dk_pg.append(dk_acc)
                dv_pg.append(dv_acc)
            dsink_ref[...] += dsink
            for pg in range(2):
                cols = slice(128 * pg, 128 * pg + 128)
                dk0_ref[:, cols] += dk_pg[pg][0:BLK]
                dv0_ref[:, cols] += dv_pg[pg][0:BLK]
                dk_ref[:, cols] = kcarry[:, cols] + dk_pg[pg][BLK:2 * BLK]
                dv_ref[:, cols] = vcarry[:, cols] + dv_pg[pg][BLK:2 * BLK]
                kcarry[:, cols] = dk_pg[pg][2 * BLK:3 * BLK]
                vcarry[:, cols] = dv_pg[pg][2 * BLK:3 * BLK]

    last = nb - 1
    cur = lambda n: (jnp.minimum(n, last), 0)
    prev = lambda n: (jnp.clip(n - 1, 0, last), 0)
    zero = lambda n: (0, 0)
    kv = lambda f: pl.BlockSpec((BLK, KVW), f)
    wide = lambda f: pl.BlockSpec((BLK, AW), f)
    return _side_call(
        body, job, name=f"attn_bwd{l}", grid=(nb + 1,),
        in_specs=[pl.BlockSpec(memory_space=pltpu.SMEM),
                  wide(cur), kv(zero), kv(prev), kv(cur), kv(zero), kv(prev), kv(cur),
                  pl.BlockSpec((BLK, AW), lambda n: (jnp.minimum(n, last), 3)),
                  wide(cur), pl.BlockSpec((BLK, 128), cur), wide(cur), pl.BlockSpec(memory_space=pl.ANY)],
        out_specs=[wide(cur), kv(prev), kv(prev), kv(zero), kv(zero),
                   pl.BlockSpec((BLK, AW), lambda n: (jnp.minimum(n, last), 3)),
                   pl.BlockSpec((1, 128), zero)],
        out_shape=[jax.ShapeDtypeStruct((T, AW), F32), jax.ShapeDtypeStruct((T, KVW), F32),
                   jax.ShapeDtypeStruct((T, KVW), F32), jax.ShapeDtypeStruct((BLK, KVW), F32),
                   jax.ShapeDtypeStruct((BLK, KVW), F32), jax.ShapeDtypeStruct(dproj.shape, BF16),
                   jax.ShapeDtypeStruct((1, 128), F32)],
        scratch_shapes=[pltpu.VMEM((BLK, KVW), F32), pltpu.VMEM((BLK, KVW), F32)],
        semantics=("arbitrary",), aliases={12: 5},
        args=[sinks, qr, kr, kr, kr, vb, vb, vb, proj, att, lse, d_ya, dproj])


def _rope_bwd(dqr, dk, dv, dk0, dv0, tabs, dproj, l):
    T = dqr.shape[0]

    def body(dq_ref, dk_ref, dv_ref, dk0_ref, dv0_ref, c_ref, s1_ref, s2_ref, _, o_ref):
        n = pl.program_id(0)
        c, s1, s2 = c_ref[...], s1_ref[...], s2_ref[...]
        for gcol in range(AW // 128):
            cols = slice(128 * gcol, 128 * gcol + 128)
            o_ref[:, cols] = (_rot_bwd(dq_ref[:, cols], c, s1, s2) * 0.125).astype(BF16)
        for gcol in range(KVW // 128):
            cols = slice(128 * gcol, 128 * gcol + 128)
            kcols = slice(AW + 128 * gcol, AW + 128 * gcol + 128)
            vcols = slice(AW + KVW + 128 * gcol, AW + KVW + 128 * gcol + 128)
            o_ref[:, kcols] = _rot_bwd(dk_ref[:, cols], c, s1, s2).astype(BF16)
            o_ref[:, vcols] = dv_ref[:, cols].astype(BF16)

            @pl.when(n == 0)
            def _():
                dkk = dk_ref[0:BLK, cols] + dk0_ref[:, cols]
                o_ref[0:BLK, kcols] = _rot_bwd(dkk, c[0:BLK], s1[0:BLK], s2[0:BLK]).astype(BF16)
                o_ref[0:BLK, vcols] = (dv_ref[0:BLK, cols] + dv0_ref[:, cols]).astype(BF16)

    tr = _pick(T, (384, 128))
    cur = lambda n: (n, 0)
    zero = lambda n: (0, 0)
    tab = pl.BlockSpec((tr, 128), cur)
    return pl.pallas_call(
        body, name=f"rope_bwd{l}", grid=(T // tr,),
        in_specs=[pl.BlockSpec((tr, AW), cur), pl.BlockSpec((tr, KVW), cur), pl.BlockSpec((tr, KVW), cur),
                  pl.BlockSpec((BLK, KVW), zero), pl.BlockSpec((BLK, KVW), zero), tab, tab, tab,
                  pl.BlockSpec(memory_space=pl.ANY)],
        out_specs=pl.BlockSpec((tr, AW + 2 * KVW), lambda n: (n, 1)),
        out_shape=jax.ShapeDtypeStruct(dproj.shape, BF16),
        input_output_aliases={8: 0},
        compiler_params=_cp("parallel"),
    )(dqr, dk, dv, dk0, dv0, *tabs, dproj)


def _block_diag(w):
    nl, nh, hd, _ = w.shape
    eye = jnp.eye(nh, dtype=w.dtype)
    return jnp.einsum("lhij,hg->lhigj", w, eye).reshape(nl, nh * hd, nh * hd)


def _diag_blocks(m):
    nh, hd = 8, 64
    return jnp.einsum("hihj->hij", m.reshape(nh, hd, nh, hd))


def _device_step(x, target, p, dist=None):
    vec = lambda a: a.reshape(DEPTH, 1, a.shape[-1])
    ln_in_g, ln_in_b = p["ln_in_g"].reshape(1, D), p["ln_in_b"].reshape(1, D)
    conv_dw_b, conv_ln_g, conv_ln_b, conv_pw_b = map(vec, (p["conv_dw_b"], p["conv_ln_g"], p["conv_ln_b"], p["conv_pw_b"]))
    lru_conv_b, lru_ba, lru_bx, lru_lambda = map(vec, (p["lru_conv_b"], p["lru_ba"], p["lru_bx"], p["lru_lambda"]))
    ln_post_g, ln_post_b = vec(p["ln_post_g"]), vec(p["ln_post_b"])
    wa_bd = _block_diag(p["lru_wa"]).astype(BF16)
    wx_bd = _block_diag(p["lru_wx"]).astype(BF16)
    w_in, w_out, pw_w = list(p["w_in"]), list(p["w_out"]), list(p["conv_pw_w"])
    sinks = p["attn_sinks"]
    big_names = ("w_in", "w_out", "conv_pw_w")

    order = dist[4] if dist else jnp.arange(N_SHARD, dtype=jnp.int32)
    (h, hb), got = _embed_fwd(x, p["meta_tokens"], ln_in_g, ln_in_b,
                              job=_gather_job([w_in[0]], peers=(0, 1)) if dist else None)
    if dist:
        w_in[0] = got[0]
    T = h.shape[0]
    tabs = _rope_tables(T)
    saved = []
    for l in range(DEPTH):
        if l == 0:
            job = _join_jobs(_gather_job([w_in[0]], peers=(2,)), _gather_job([pw_w[0]])) if dist else None
            (proj,), got = _proj_fwd(hb, w_in[0], order, 0, N_SHARD - 1, None, l, job=job)
            if dist:
                w_in[0], pw_w[0] = got
            (proj,), _ = _proj_fwd(hb, w_in[0], order, N_SHARD - 1, 1, proj, l)
        else:
            (proj,), _ = _proj_fwd(hb, w_in[l], order, 0, N_SHARD, None, l)
        pw_l = pw_w[l].reshape(CW, CW)
        (yc, conv), got = _conv_fwd(proj, p["conv_dw_w"], conv_dw_b, conv_ln_g, conv_ln_b, pw_l, conv_pw_b, l,
                                    job=_gather_job([w_out[0]]) if dist and l == 0 else None)
        if got:
            w_out[0] = got[0]
        qr, kr, vb = _rope_fwd(proj, tabs, l)
        (ya, att, lse), got = _attn_fwd(
            qr, kr, vb, proj, sinks, l, job=_gather_job([w_in[1]]) if dist and l == 0 else None)
        if got:
            w_in[1] = got[0]
        (yl, hl), _ = _lru_fwd(proj, p["lru_conv_w"], lru_conv_b, wa_bd, lru_ba, wx_bd, lru_bx, lru_lambda, l)
        (hn, hnb, xhat, rstd), got = _out_fwd(
            yc, ya, yl, w_out[l], h, ln_post_g, ln_post_b, l,
            job=_gather_job([w_out[1], pw_w[1]]) if dist and l == 0 else None)
        if got:
            w_out[1], pw_w[1] = got
        saved.append((hb, proj, yc, conv, qr, kr, vb, ya, att, lse, yl, hl, xhat, rstd, pw_l))
        h, hb = hn, hnb

    dh = None
    g = {}
    later = None
    early, last = ("w_out", "conv_pw_w"), ("w_in",)
    own = {}
    for l in reversed(range(DEPTH)):
        hb_l, proj, yc, conv, qr, kr, vb, ya, att, lse, yl, hl, xhat, rstd, pw_l = saved[l]
        tail = dist is not None and l == 0
        top = l == DEPTH - 1
        (part, dz, dzb, g["ln_post_g", l], g["ln_post_b", l], d_yc, d_ya, d_yl), recv = _post_ln_dcat_bwd(
            h if top else dh, target if top else None, xhat, rstd, ln_post_g, w_out[l], l,
            job=_swap_job(later["grads"]) if later else None)
        if top:
            loss_part = part
        if later:
            later["parts"], later["owns"] = _chip_partials(big_names, later["grads"], recv, dist, later["l"])
        g["w_out", l] = _dwout_bwd(yc, ya, yl, dzb, l)
        d_conv, dproj, dpw, g["conv_pw_b", l], g["conv_ln_g", l], g["conv_ln_b", l] = _conv_bwd_rows(
            conv, proj, d_yc, conv_ln_g, conv_ln_b, pw_l, conv_pw_b, l)
        g["conv_pw_w", l] = dpw.reshape(N_SHARD, 2, PW_SH // 2, CW)
        if tail:
            own["early"] = dict(l=0, grads=[g[name, 0] for name in early])
        job = None
        if tail:
            job = _join_jobs(_swap_job(own["early"]["grads"]), _scatter_job(later["parts"][1:]))
        (dproj, ddw, g["conv_dw_b", l]), got = _conv_bwd_taps(d_conv, proj, p["conv_dw_w"], dproj, l, job=job)
        if tail:
            n_early = len(early)
            own["early"]["parts"], own["early"]["owns"] = _chip_partials(
                early, own["early"]["grads"], got[:n_early], dist, 0)
            later["z"] = got[n_early:]
        g["conv_dw_w", l] = ddw[:CONV_K]
        (dqr, dk, dv, dk0, dv0, dproj, dsink), z = _attn_bwd(
            qr, kr, vb, proj, att, lse, d_ya, sinks, dproj, l,
            job=_scatter_job(later["parts"][:1]) if later else None)
        if later:
            later["z"] = z + later["z"]
        g["attn_sinks", l] = dsink[0, :N_HEADS]
        dproj = _rope_bwd(dqr, dk, dv, dk0, dv0, tabs, dproj, l)
        (dproj, dlw, g["lru_conv_b", l], dwa, g["lru_ba", l], dwx, g["lru_bx", l], g["lru_lambda", l]), z = _lru_bwd(
            proj, hl, d_yl, p["lru_conv_w"], lru_conv_b, wa_bd, lru_ba, wx_bd, lru_bx, lru_lambda, dproj, l,
            job=_scatter_job(own["early"]["parts"]) if tail else None)
        if tail:
            own["early"]["z"] = z
        g["lru_conv_w", l] = dlw[:LRU_K]
        g["lru_wa", l] = _diag_blocks(dwa)
        g["lru_wx", l] = _diag_blocks(dwx)
        job = None
        if l > 0:
            g["w_in", l] = _dwin_bwd(hb_l, dproj, l)
        else:
            c = dist[0] if dist else jnp.int32(0)
            job = None
            if dist:
                pack_a = _pack_rows([_layer_stack(g, name) for name in _SMALL_LAYERED])
                totals = _shard_totals(big_names, later, dist)
                job = _join_jobs(_share_job(totals), _spread_job(pack_a))
            (give,), got = _dwin_half(hb_l, dproj, 1 - c, l, "give", job=job)
            (keep,), recv = _dwin_half(hb_l, dproj, c, l, "keep", job=_send_job([give]) if dist else None)
            job = None
            if dist:
                _store_reduced(big_names, later["l"], got[:-1], g)
                later = None
                g["pack_layered", -1] = _sum_slots(pack_a, got[-1], dist[3], "layered")
                own["last"] = dict(l=0)
                own["last"]["parts"], own["last"]["owns"] = _chip_partials(
                    last, [keep.reshape(N_SHARD, 1, D // 2, WIN_SH)], recv, (jnp.int32(0),) + tuple(dist[1:]), 0)
                job = _scatter_job(own["last"]["parts"])
            else:
                g["w_in", l] = jnp.stack([keep, give], axis=1)
        (dh,), got = _dh_bwd(dproj, [w_in[l]], dz, l, job=job)
        if tail:
            own["last"]["z"] = got
        if dist and l > 0:
            later = dict(l=l, grads=[g[name, l] for name in big_names])
    grad_x, g["meta_tokens", -1], g["ln_in_g", -1], g["ln_in_b", -1] = _embed_bwd(
        dh, x, p["meta_tokens"], ln_in_g, ln_in_b)
    if dist:
        pack_b = _pack_rows([g[name, -1] for name in _SMALL_EMBED])
        state = dict(l=0, owns=own["last"]["owns"] + own["early"]["owns"], z=own["last"]["z"] + own["early"]["z"])
        totals = _shard_totals(last + early, state, dist)
        got = _run_job(_join_jobs(_share_job(totals), _spread_job(pack_b)), "share_and_spread")
        _store_reduced(last + early, 0, got[:-1], g)
        g["pack_embed", -1] = _sum_slots(pack_b, got[-1], dist[3], "embed")
    return loss_part, grad_x, g


_SMALL_EMBED = ("meta_tokens", "ln_in_g", "ln_in_b")
_SMALL_LAYERED = ("conv_dw_w", "conv_dw_b", "conv_ln_g", "conv_ln_b", "conv_pw_b", "attn_sinks", "lru_conv_w",
                  "lru_conv_b", "lru_wa", "lru_ba", "lru_wx", "lru_bx", "lru_lambda", "ln_post_g", "ln_post_b")


def _layer_stack(g, name):
    return jnp.stack([g[name, l] for l in range(DEPTH)], axis=0)


def _chip_partials(names, grads, recv, dist, l):
    outs = [_chip_partial(a, r, dist[0], dist[1], f"{name}{l}") for name, a, r in zip(names, grads, recv)]
    return [o[0] for o in outs], [o[1] for o in outs]


def _shard_totals(names, state, dist):
    l = state["l"]
    return [_shard_total(po, zz, dist[2], f"{name}{l}") for name, po, zz in zip(names, state["owns"], state["z"])]


def _store_reduced(names, l, full, g):
    for name, f in zip(names, full):
        g[name, l] = f.reshape(2 * f.shape[1], f.shape[2])


MESH = pl.DeviceIdType.MESH
HBM_SPEC = pl.BlockSpec(memory_space=pltpu.HBM)
N_DEV = 8


def _position():
    x, y, c = lax.axis_index("x"), lax.axis_index("y"), lax.axis_index("c")
    return x, y, c


def _other_chips(x, y):
    return [(1 - x, y), (x, 1 - y), (1 - x, 1 - y)]


def _cast_into_slot(a, l, j, tag, piece=0, pieces=1):
    _, R, C = a.shape
    rows = R // pieces
    tb = _pick(rows, (512, 128))
    first = piece * rows // tb

    def body(s_ref, a_ref, o_ref):
        o_ref[...] = a_ref[...].astype(BF16)

    grid_spec = pltpu.PrefetchScalarGridSpec(
        num_scalar_prefetch=1, grid=(rows // tb,),
        in_specs=[pl.BlockSpec((None, tb, C), lambda t, sc: (l, first + t, 0))],
        out_specs=pl.BlockSpec((None, tb, C), lambda t, sc: (sc[0], t, 0)))
    return pl.pallas_call(
        body, name=f"cast_into_slot_{tag}{l}_{piece}", grid_spec=grid_spec,
        out_shape=jax.ShapeDtypeStruct((N_SHARD, rows, C), BF16),
        compiler_params=_cp("arbitrary"),
    )(jnp.reshape(j, (1,)).astype(jnp.int32), a)


class _Job:
    def __init__(self, inputs, aliased, extra_out, sems, start, mid, finish):
        self.inputs, self.extra_out, self.sems = list(inputs), list(extra_out), list(sems)
        self.n_aliased = len(self.inputs) if aliased is True else int(aliased)
        self.start, self.mid, self.finish = start, mid, finish

    def out_shapes(self):
        return [jax.ShapeDtypeStruct(a.shape, a.dtype) for a in self.inputs[:self.n_aliased]] + self.extra_out


def _side_call(body, job, *, name, grid, in_specs, out_specs, out_shape, scratch_shapes, semantics, args,
               aliases=None, prefetch=()):
    aliases = dict(aliases or {})
    n_pre = len(prefetch)

    def call(fn, ins, outs, shapes, scratch, sem, operands):
        if n_pre:
            spec = pltpu.PrefetchScalarGridSpec(num_scalar_prefetch=n_pre, grid=grid, in_specs=ins, out_specs=outs,
                                                scratch_shapes=scratch)
            return pl.pallas_call(fn, name=name, grid_spec=spec, out_shape=shapes,
                                  input_output_aliases={k + n_pre: v for k, v in aliases.items()},
                                  compiler_params=_cp(*sem))(*prefetch, *operands)
        return pl.pallas_call(fn, name=name, grid=grid, in_specs=ins, out_specs=outs, out_shape=shapes,
                              scratch_shapes=scratch, input_output_aliases=aliases,
                              compiler_params=_cp(*sem))(*operands)

    if job is None:
        return list(call(body, list(in_specs), list(out_specs), list(out_shape), list(scratch_shapes),
                         semantics, args)), []
    n_in, n_out, n_scr = len(in_specs), len(out_specs), len(scratch_shapes)
    j_in, j_out = len(job.inputs), len(job.out_shapes())
    steps = 1
    for gsize in grid:
        steps *= gsize

    def wrapped(*refs):
        pre, refs = refs[:n_pre], refs[n_pre:]
        host_in, job_in = refs[:n_in], refs[n_in:n_in + j_in]
        o0 = n_in + j_in
        host_out, job_out = refs[o0:o0 + n_out], refs[o0 + n_out:o0 + n_out + j_out]
        s0 = o0 + n_out + j_out
        host_scr, sems = refs[s0:s0 + n_scr], refs[s0 + n_scr:]
        step = pl.program_id(0)
        for d in range(1, len(grid)):
            step = step * grid[d] + pl.program_id(d)

        @pl.when(step == 0)
        def _():
            job.start(job_in, job_out, sems)

        @pl.when(step == max(steps - 2, 0))
        def _():
            job.mid(job_in, job_out, sems)

        body(*pre, *host_in, *host_out, *host_scr)

        @pl.when(step == steps - 1)
        def _():
            job.finish(job_in, job_out, sems)

    aliases.update({n_in + k: n_out + k for k in range(job.n_aliased)})
    outs = call(wrapped, list(in_specs) + [HBM_SPEC] * j_in, list(out_specs) + [HBM_SPEC] * j_out,
                list(out_shape) + job.out_shapes(), list(scratch_shapes) + job.sems,
                ["arbitrary"] * len(grid), [*args, *job.inputs])
    return list(outs[:n_out]), list(outs[n_out:])


def _run_job(job, name):
    return _side_call(lambda: None, job, name=name, grid=(1,), in_specs=[], out_specs=[], out_shape=[],
                      scratch_shapes=[], semantics=("arbitrary",), args=[])[1]


def _gather_job(slots, peers=(0, 1, 2)):
    n = len(slots)

    def copies(buf, sems):
        ici_send, ici_recv, d2d_send, d2d_recv = sems
        x, y, c = _position()
        chips = _other_chips(x, y)

        def half(k, slot, which):
            hr = buf[k].shape[1] // 2
            return buf[k].at[slot, pl.ds(pl.multiple_of(which * hr, hr), hr)]

        def over_ici(k, p, slot):
            px, py = chips[p]
            return pltpu.make_async_remote_copy(
                src_ref=half(k, slot, c), dst_ref=half(k, slot, c),
                send_sem=ici_send.at[k * 3 + p], recv_sem=ici_recv.at[k * 3 + p],
                device_id=(px, py, c), device_id_type=MESH)

        def over_d2d(k, p, which):
            px, py = chips[p]
            return pltpu.make_async_remote_copy(
                src_ref=half(k, 2 * px + py, which), dst_ref=half(k, 2 * px + py, which),
                send_sem=d2d_send.at[k * 3 + p], recv_sem=d2d_recv.at[k * 3 + p],
                device_id=(x, y, 1 - c), device_id_type=MESH)

        return over_ici, over_d2d, 2 * x + y, chips, c

    pairs = [(k, p) for k in range(n) for p in peers]

    def start(_, buf, sems):
        over_ici, _, mine, _, _ = copies(buf, sems)
        for k, p in pairs:
            over_ici(k, p, mine).start()

    def mid(_, buf, sems):
        over_ici, over_d2d, _, chips, c = copies(buf, sems)
        for k, p in pairs:
            px, py = chips[p]
            over_ici(k, p, 2 * px + py).wait_recv()
            over_d2d(k, p, c).start()

    def finish(_, buf, sems):
        over_ici, over_d2d, mine, _, c = copies(buf, sems)
        for k, p in pairs:
            over_d2d(k, p, 1 - c).wait_recv()
        for k, p in pairs:
            over_ici(k, p, mine).wait_send()
            over_d2d(k, p, c).wait_send()

    return _Job(slots, True, [], [pltpu.SemaphoreType.DMA((3 * n,))] * 4, start, mid, finish)


def _gather_shards(shards):
    n = len(shards)

    def body(*refs):
        src, dst = refs[:n], refs[n:2 * n]
        send_sems, recv_sems, local_sems = refs[2 * n:]
        x, y, c = _position()
        mine = 2 * x + y
        chips = _other_chips(x, y)

        def copy(k, p):
            return pltpu.make_async_remote_copy(
                src_ref=src[k], dst_ref=dst[k].at[mine],
                send_sem=send_sems.at[k * 3 + p], recv_sem=recv_sems.at[k * 3 + p],
                device_id=(*chips[p], c), device_id_type=MESH)

        def arrival(k, p):
            px, py = chips[p]
            return pltpu.make_async_remote_copy(
                src_ref=src[k], dst_ref=dst[k].at[2 * px + py],
                send_sem=send_sems.at[k * 3 + p], recv_sem=recv_sems.at[k * 3 + p],
                device_id=(px, py, c), device_id_type=MESH)

        local = [pltpu.make_async_copy(src[k], dst[k].at[mine], local_sems.at[k]) for k in range(n)]
        for cp in local:
            cp.start()
        for k in range(n):
            for p in range(3):
                copy(k, p).start()
        for k in range(n):
            for p in range(3):
                arrival(k, p).wait_recv()
        for k in range(n):
            for p in range(3):
                copy(k, p).wait_send()
        for cp in local:
            cp.wait()

    return pl.pallas_call(
        body, name="gather_shards",
        in_specs=[HBM_SPEC] * n, out_specs=[HBM_SPEC] * n,
        out_shape=[jax.ShapeDtypeStruct((N_SHARD,) + s.shape, s.dtype) for s in shards],
        scratch_shapes=[pltpu.SemaphoreType.DMA((3 * n,)), pltpu.SemaphoreType.DMA((3 * n,)),
                        pltpu.SemaphoreType.DMA((n,))],
    )(*shards)


def _swap_job(grads):
    n = len(grads)

    def copies(src, dst, sems):
        x, y, c = _position()
        return [pltpu.make_async_remote_copy(
            src_ref=src[k].at[:, 1 - c], dst_ref=dst[k],
            send_sem=sems[0].at[k], recv_sem=sems[1].at[k],
            device_id=(x, y, 1 - c), device_id_type=MESH) for k in range(n)]

    def start(src, dst, sems):
        for cp in copies(src, dst, sems):
            cp.start()

    def finish(src, dst, sems):
        for cp in copies(src, dst, sems):
            cp.wait()

    return _Job(grads, False, [jax.ShapeDtypeStruct((N_SHARD,) + g.shape[2:], F32) for g in grads],
                [pltpu.SemaphoreType.DMA((n,))] * 2, start, lambda *_: None, finish)


def _send_job(arrays):
    n = len(arrays)

    def copies(src, dst, sems):
        x, y, c = _position()
        return [pltpu.make_async_remote_copy(
            src_ref=src[k], dst_ref=dst[k], send_sem=sems[0].at[k], recv_sem=sems[1].at[k],
            device_id=(x, y, 1 - c), device_id_type=MESH) for k in range(n)]

    def start(src, dst, sems):
        for cp in copies(src, dst, sems):
            cp.start()

    def finish(src, dst, sems):
        for cp in copies(src, dst, sems):
            cp.wait()

    return _Job(arrays, False, [jax.ShapeDtypeStruct(a.shape, a.dtype) for a in arrays],
                [pltpu.SemaphoreType.DMA((n,))] * 2, start, lambda *_: None, finish)


def _chip_partial(a, y, c, j, tag):
    _, _, R, C = a.shape
    tr = _pick(R, (256, 64))

    def body(s_ref, a_ref, y_ref, pb_ref, po_ref):
        total = a_ref[...] + y_ref[...]
        pb_ref[...] = total.astype(BF16)

        @pl.when(pl.program_id(1) == s_ref[1])
        def _():
            po_ref[...] = total

    grid_spec = pltpu.PrefetchScalarGridSpec(
        num_scalar_prefetch=1, grid=(R // tr, N_SHARD),
        in_specs=[pl.BlockSpec((None, None, tr, C), lambda t, s, sc: (s, sc[0], t, 0)),
                  pl.BlockSpec((None, tr, C), lambda t, s, sc: (s, t, 0))],
        out_specs=[pl.BlockSpec((None, tr, C), lambda t, s, sc: (s, t, 0)),
                   pl.BlockSpec((tr, C), lambda t, s, sc: (t, 0))])
    return pl.pallas_call(
        body, name=f"chip_partial_{tag}", grid_spec=grid_spec,
        out_shape=[jax.ShapeDtypeStruct((N_SHARD, R, C), BF16), jax.ShapeDtypeStruct((R, C), F32)],
        compiler_params=_cp("arbitrary", "arbitrary"),
    )(jnp.stack([c, j]).astype(jnp.int32), a, y)


def _scatter_job(parts):
    n = len(parts)
    pairs = [(k, p) for k in range(n) for p in range(3)]

    def copy(src, dst, sems, k, p, outgoing):
        x, y, c = _position()
        mine = 2 * x + y
        px, py = _other_chips(x, y)[p]
        theirs = 2 * px + py
        return pltpu.make_async_remote_copy(
            src_ref=src[k].at[theirs if outgoing else mine], dst_ref=dst[k].at[mine if outgoing else theirs],
            send_sem=sems[0].at[k * 3 + p], recv_sem=sems[1].at[k * 3 + p],
            device_id=(px, py, c), device_id_type=MESH)

    def start(src, dst, sems):
        for k, p in pairs:
            copy(src, dst, sems, k, p, True).start()

    def finish(src, dst, sems):
        for k, p in pairs:
            copy(src, dst, sems, k, p, False).wait_recv()
        for k, p in pairs:
            copy(src, dst, sems, k, p, True).wait_send()

    return _Job(parts, False, [jax.ShapeDtypeStruct(pb.shape, BF16) for pb in parts],
                [pltpu.SemaphoreType.DMA((3 * n,))] * 2, start, lambda *_: None, finish)


def _shard_total(own, z, others_c, tag):
    R, C = own.shape
    tr = _pick(R, (256, 64))

    def body(s_ref, o_ref, z0_ref, z1_ref, z2_ref, h_ref):
        h_ref[...] = ((o_ref[...] + z0_ref[...].astype(F32)) + z1_ref[...].astype(F32)) + z2_ref[...].astype(F32)

    zspec = lambda q: pl.BlockSpec((None, tr, C), lambda t, sc: (sc[q], t, 0))
    grid_spec = pltpu.PrefetchScalarGridSpec(
        num_scalar_prefetch=1, grid=(R // tr,),
        in_specs=[pl.BlockSpec((tr, C), lambda t, sc: (t, 0)), zspec(0), zspec(1), zspec(2)],
        out_specs=pl.BlockSpec((None, tr, C), lambda t, sc: (sc[3], t, 0)))
    return pl.pallas_call(
        body, name=f"shard_total_{tag}", grid_spec=grid_spec,
        out_shape=jax.ShapeDtypeStruct((2, R, C), F32),
        compiler_params=_cp("arbitrary"),
    )(others_c, own, z, z, z)


def _share_job(totals):
    n = len(totals)

    def copy(buf, sems, k, which):
        x, y, c = _position()
        return pltpu.make_async_remote_copy(
            src_ref=buf[k].at[which], dst_ref=buf[k].at[which],
            send_sem=sems[0].at[k], recv_sem=sems[1].at[k],
            device_id=(x, y, 1 - c), device_id_type=MESH)

    def start(_, buf, sems):
        c = lax.axis_index("c")
        for k in range(n):
            copy(buf, sems, k, c).start()

    def finish(_, buf, sems):
        c = lax.axis_index("c")
        for k in range(n):
            copy(buf, sems, k, 1 - c).wait_recv()
        for k in range(n):
            copy(buf, sems, k, c).wait_send()

    return _Job(totals, True, [], [pltpu.SemaphoreType.DMA((n,))] * 2, start, lambda *_: None, finish)


def _spread_job(pack):
    def copy(src, dst, sems, m, outgoing):
        x, y, c = _position()
        peer = (x ^ (m >> 2), y ^ ((m >> 1) & 1), c ^ (m & 1))
        slot = 4 * x + 2 * y + c if outgoing else 4 * peer[0] + 2 * peer[1] + peer[2]
        return pltpu.make_async_remote_copy(
            src_ref=src[0], dst_ref=dst[0].at[slot], send_sem=sems[0].at[m - 1], recv_sem=sems[1].at[m - 1],
            device_id=peer, device_id_type=MESH)

    def start(src, dst, sems):
        for m in range(1, N_DEV):
            copy(src, dst, sems, m, True).start()

    def finish(src, dst, sems):
        for m in range(1, N_DEV):
            copy(src, dst, sems, m, False).wait_recv()
        for m in range(1, N_DEV):
            copy(src, dst, sems, m, True).wait_send()

    return _Job([pack], False, [jax.ShapeDtypeStruct((N_DEV,) + pack.shape, F32)],
                [pltpu.SemaphoreType.DMA((N_DEV - 1,))] * 2, start, lambda *_: None, finish)


def _join_jobs(a, b):
    for job in (a, b):
        assert job.n_aliased in (0, len(job.inputs)) and not (job.n_aliased and job.extra_out)
    assert a.n_aliased or not b.n_aliased
    n_in, n_out, n_sem = len(a.inputs), len(a.out_shapes()), len(a.sems)

    def phase(name):
        def run(ins, outs, sems):
            getattr(a, name)(ins[:n_in], outs[:n_out], sems[:n_sem])
            getattr(b, name)(ins[n_in:], outs[n_out:], sems[n_sem:])
        return run

    return _Job(a.inputs + b.inputs, a.n_aliased + b.n_aliased, a.extra_out + b.extra_out, a.sems + b.sems,
                phase("start"), phase("mid"), phase("finish"))


def _sum_slots(pack, slots, me, tag):
    def body(me_ref, p_ref, s_ref, o_ref):
        acc = None
        for d in range(N_DEV):
            term = jnp.where(me_ref[0] == d, p_ref[...], s_ref[d])
            acc = term if acc is None else acc + term
        o_ref[...] = acc

    vm = pl.BlockSpec(memory_space=pltpu.VMEM)
    return pl.pallas_call(
        body, name=f"sum_slots_{tag}",
        in_specs=[pl.BlockSpec(memory_space=pltpu.SMEM), vm, vm], out_specs=vm,
        out_shape=jax.ShapeDtypeStruct(pack.shape, F32),
        compiler_params=pltpu.CompilerParams(vmem_limit_bytes=V7X_VMEM_LIMIT),
    )(jnp.reshape(me, (1,)).astype(jnp.int32), pack, slots)


def _pack_rows(arrays):
    total = sum(a.size for a in arrays)
    rows = -(-total // 128)
    rows = -(-rows // PACK_ROWS_ALIGN) * PACK_ROWS_ALIGN
    flat = [a.reshape(-1) for a in arrays] + [jnp.zeros((rows * 128 - total,), F32)]
    return jnp.concatenate(flat).reshape(rows, 128)


def _adamw_math(w, g, m, v):
    m = ADAM_B1 * m + (1.0 - ADAM_B1) * g
    v = ADAM_B2 * v + (1.0 - ADAM_B2) * (g * g)
    m_hat = m / (1.0 - ADAM_B1 ** ADAM_STEP)
    v_hat = v / (1.0 - ADAM_B2 ** ADAM_STEP)
    delta = -ADAM_LR * (m_hat / (jnp.sqrt(v_hat) + ADAM_EPS) + ADAM_WD * w)
    return delta, m, v


def _adamw_big(w, g0, g1, m, v, tag):
    _, R, C = w.shape
    tr = _pick(R, (256, 128))

    def body(w_ref, g0_ref, g1_ref, m_ref, v_ref, go_ref, d_ref, mo_ref, vo_ref):
        g = jnp.where(pl.program_id(0) == 0, g0_ref[...], g1_ref[...])
        delta, mn, vn = _adamw_math(w_ref[...], g, m_ref[...], v_ref[...])
        go_ref[...] = g
        d_ref[...] = delta
        mo_ref[...] = mn
        vo_ref[...] = vn

    s3 = pl.BlockSpec((None, tr, C), lambda l, t: (l, t, 0))
    s2 = pl.BlockSpec((tr, C), lambda l, t: (t, 0))
    shp = jax.ShapeDtypeStruct(w.shape, F32)
    return pl.pallas_call(
        body, name=f"adamw_{tag}", grid=(2, R // tr),
        in_specs=[s3, s2, s2, s3, s3], out_specs=[s3, s3, s3, s3],
        out_shape=[shp, shp, shp, shp],
        compiler_params=_cp("parallel", "parallel"),
    )(w, g0, g1, m, v)


def _adamw_small(ws, gs, ms, vs):
    n = len(ws)

    def body(*refs):
        w_r, g_r, m_r, v_r = refs[:n], refs[n:2 * n], refs[2 * n:3 * n], refs[3 * n:4 * n]
        d_o, m_o, v_o = refs[4 * n:5 * n], refs[5 * n:6 * n], refs[6 * n:7 * n]
        for k in range(n):
            delta, mn, vn = _adamw_math(w_r[k][...], g_r[k][...], m_r[k][...], v_r[k][...])
            d_o[k][...] = delta
            m_o[k][...] = mn
            v_o[k][...] = vn

    vm = pl.BlockSpec(memory_space=pltpu.VMEM)
    shapes = [jax.ShapeDtypeStruct(w.shape, F32) for w in ws]
    outs = pl.pallas_call(
        body, name="adamw_small",
        in_specs=[vm] * (4 * n), out_specs=[vm] * (3 * n),
        out_shape=shapes * 3,
    )(*ws, *gs, *ms, *vs)
    return outs[:n], outs[n:2 * n], outs[2 * n:]


_WEIGHTS = ["meta_tokens", "ln_in_g", "ln_in_b", "w_in", "conv_dw_w", "conv_dw_b", "conv_ln_g", "conv_ln_b",
            "conv_pw_w", "conv_pw_b", "attn_sinks", "lru_conv_w", "lru_conv_b", "lru_wa", "lru_ba", "lru_wx",
            "lru_bx", "lru_lambda", "w_out", "ln_post_g", "ln_post_b"]
_BIG = ("w_in", "w_out", "conv_pw_w")
_SMALL_SHARDED = {"meta_tokens": 1, "conv_dw_w": 2, "lru_conv_w": 2}
PACK_ROWS_ALIGN = 8


def _as2d(a):
    return a.reshape(1, -1) if a.ndim == 1 else a.reshape(-1, a.shape[-1])


def kernel(x, meta_tokens, ln_in_g, ln_in_b, w_in, conv_dw_w, conv_dw_b, conv_ln_g, conv_ln_b, conv_pw_w, conv_pw_b, attn_sinks, lru_conv_w, lru_conv_b, lru_wa, lru_ba, lru_wx, lru_bx, lru_lambda, w_out, ln_post_g, ln_post_b, loss_target, m_meta_tokens, m_ln_in_g, m_ln_in_b, m_w_in, m_conv_dw_w, m_conv_dw_b, m_conv_ln_g, m_conv_ln_b, m_conv_pw_w, m_conv_pw_b, m_attn_sinks, m_lru_conv_w, m_lru_conv_b, m_lru_wa, m_lru_ba, m_lru_wx, m_lru_bx, m_lru_lambda, m_w_out, m_ln_post_g, m_ln_post_b, v_meta_tokens, v_ln_in_g, v_ln_in_b, v_w_in, v_conv_dw_w, v_conv_dw_b, v_conv_ln_g, v_conv_ln_b, v_conv_pw_w, v_conv_pw_b, v_attn_sinks, v_lru_conv_w, v_lru_conv_b, v_lru_wa, v_lru_ba, v_lru_wx, v_lru_bx, v_lru_lambda, v_w_out, v_ln_post_g, v_ln_post_b):
    w = dict(meta_tokens=meta_tokens, ln_in_g=ln_in_g, ln_in_b=ln_in_b, w_in=w_in, conv_dw_w=conv_dw_w,
             conv_dw_b=conv_dw_b, conv_ln_g=conv_ln_g, conv_ln_b=conv_ln_b, conv_pw_w=conv_pw_w,
             conv_pw_b=conv_pw_b, attn_sinks=attn_sinks, lru_conv_w=lru_conv_w, lru_conv_b=lru_conv_b,
             lru_wa=lru_wa, lru_ba=lru_ba, lru_wx=lru_wx, lru_bx=lru_bx, lru_lambda=lru_lambda, w_out=w_out,
             ln_post_g=ln_post_g, ln_post_b=ln_post_b)
    mom_m = dict(zip(_WEIGHTS, (m_meta_tokens, m_ln_in_g, m_ln_in_b, m_w_in, m_conv_dw_w, m_conv_dw_b, m_conv_ln_g,
                                m_conv_ln_b, m_conv_pw_w, m_conv_pw_b, m_attn_sinks, m_lru_conv_w, m_lru_conv_b,
                                m_lru_wa, m_lru_ba, m_lru_wx, m_lru_bx, m_lru_lambda, m_w_out, m_ln_post_g,
                                m_ln_post_b)))
    mom_v = dict(zip(_WEIGHTS, (v_meta_tokens, v_ln_in_g, v_ln_in_b, v_w_in, v_conv_dw_w, v_conv_dw_b, v_conv_ln_g,
                                v_conv_ln_b, v_conv_pw_w, v_conv_pw_b, v_attn_sinks, v_lru_conv_w, v_lru_conv_b,
                                v_lru_wa, v_lru_ba, v_lru_wx, v_lru_bx, v_lru_lambda, v_w_out, v_ln_post_g,
                                v_ln_post_b)))
    xi, yi, ci = _position()
    j = 2 * xi + yi

    g_meta, g_dw, g_lc = _gather_shards([meta_tokens, conv_dw_w, lru_conv_w])
    p = dict(w)
    p["w_in"] = [_cast_into_slot(w_in, l, j, "w_in") for l in range(DEPTH)]
    p["w_out"] = [_cast_into_slot(w_out, l, j, "w_out") for l in range(DEPTH)]
    p["conv_pw_w"] = [_cast_into_slot(conv_pw_w, l, j, "conv_pw_w") for l in range(DEPTH)]
    p["meta_tokens"] = g_meta.transpose(1, 0, 2).reshape(N_META, D)
    p["conv_dw_w"] = g_dw.transpose(1, 2, 0, 3).reshape(DEPTH, CONV_K, CW)
    p["lru_conv_w"] = g_lc.transpose(1, 2, 0, 3).reshape(DEPTH, LRU_K, LW)

    others = jnp.stack([jnp.where(j <= 0, 1, 0), jnp.where(j <= 1, 2, 1), jnp.where(j <= 2, 3, 2), ci]).astype(jnp.int32)
    me = 4 * xi + 2 * yi + ci
    order = jnp.stack([j, 2 * (1 - xi) + yi, 2 * xi + (1 - yi), 2 * (1 - xi) + (1 - yi)]).astype(jnp.int32)
    loss_part, grad_x, g = _device_step(x[0], loss_target[0], p, dist=(ci, j, others, me, order))
    loss = lax.psum(jnp.sum(loss_part), ("x", "y", "c"))
    big = {(name, l): g[name, l] for name in _BIG for l in range(DEPTH)}

    small_names = [n for n in _WEIGHTS if n not in _BIG]
    small_g = {}
    for names, red in ((_SMALL_LAYERED, g["pack_layered", -1]), (_SMALL_EMBED, g["pack_embed", -1])):
        red = red.reshape(-1)
        off = 0
        for n in names:
            fshape = list(w[n].shape)
            if n in _SMALL_SHARDED:
                fshape[_SMALL_SHARDED[n]] *= N_SHARD
            sz = 1
            for dim in fshape:
                sz *= dim
            full = red[off:off + sz].reshape(fshape)
            off += sz
            if n in _SMALL_SHARDED:
                ax = _SMALL_SHARDED[n]
                full = lax.dynamic_slice_in_dim(full, j * w[n].shape[ax], w[n].shape[ax], axis=ax)
            small_g[n] = full

    out_g, out_d, out_m, out_v = {}, {}, {}, {}
    for name in _BIG:
        shp = w[name].shape
        to3 = lambda a: a.reshape(DEPTH, -1, shp[-1])
        go, do, mo, vo = _adamw_big(to3(w[name]), big[name, 0], big[name, 1], to3(mom_m[name]), to3(mom_v[name]), name)
        out_g[name], out_d[name], out_m[name], out_v[name] = (a.reshape(shp) for a in (go, do, mo, vo))
    ds, ms, vs = _adamw_small([_as2d(w[n]) for n in small_names], [_as2d(small_g[n]) for n in small_names],
                              [_as2d(mom_m[n]) for n in small_names], [_as2d(mom_v[n]) for n in small_names])
    for n, d_, m_, v_ in zip(small_names, ds, ms, vs):
        out_g[n] = small_g[n]
        out_d[n], out_m[n], out_v[n] = d_.reshape(w[n].shape), m_.reshape(w[n].shape), v_.reshape(w[n].shape)

    return (loss, grad_x[None], *[out_g[n] for n in _WEIGHTS], *[out_d[n] for n in _WEIGHTS],
            *[out_m[n] for n in _WEIGHTS], *[out_v[n] for n in _WEIGHTS])
```

```python
import functools

import jax
import jax.numpy as jnp
from jax import lax
from jax.experimental import pallas as pl
from jax.experimental.pallas import tpu as pltpu

F32 = jnp.float32
BF16 = jnp.bfloat16

D = 2048
N_META = 16
CW = 512
CONV_K = 31
AW = 1024
KVW = 256
N_HEADS = 16
LW = 512
LRU_K = 4
LRU_C = 8.0
IN_TOTAL = 5120
ROT_HALF = 8
ROPE_THETA = 500000.0
LN_EPS = 1e-5
DEPTH = 2
ALPHA = (2.0 * DEPTH) ** 0.25
NEG_INF = -1e30
ADAM_LR, ADAM_B1, ADAM_B2, ADAM_EPS, ADAM_WD, ADAM_STEP = 0.001, 0.9, 0.999, 1e-08, 0.01, 10

BLK = 128
PAD = BLK - N_META
N_SHARD = 4
WIN_SH = IN_TOTAL // N_SHARD
WOUT_SH = D // N_SHARD
PW_SH = CW // N_SHARD
HALO = 32
LHALO = 8
V7X_VMEM_LIMIT = 60 * 1024 * 1024


def _cp(*sem):
    return pltpu.CompilerParams(dimension_semantics=sem if sem else None, vmem_limit_bytes=V7X_VMEM_LIMIT)


def _pick(total, prefs):
    for p in prefs:
        if total % p == 0:
            return p
    raise ValueError(f"no tile for {total}")


def _dot(a, b):
    return jnp.dot(a, b, preferred_element_type=F32)


def _dot_nt(a, b):
    return lax.dot_general(a, b, (((1,), (1,)), ((), ())), preferred_element_type=F32)


def _dot_tn(a, b):
    return lax.dot_general(a, b, (((0,), (0,)), ((), ())), preferred_element_type=F32)


def _sigmoid(x):
    return 1.0 / (1.0 + jnp.exp(-x))


def _silu_and_grad(x):
    s = _sigmoid(x)
    return x * s, s * (1.0 + x * (1.0 - s))


def _ln_rows(x, g, b):
    mu = jnp.mean(x, axis=-1, keepdims=True)
    xc = x - mu
    var = jnp.mean(xc * xc, axis=-1, keepdims=True)
    rstd = lax.rsqrt(var + LN_EPS)
    xhat = xc * rstd
    return xhat * g + b, xhat, rstd


def _ln_bwd_rows(dy, xhat, rstd, g):
    dxh = dy * g
    m1 = jnp.mean(dxh, axis=-1, keepdims=True)
    m2 = jnp.mean(dxh * xhat, axis=-1, keepdims=True)
    return rstd * (dxh - m1 - xhat * m2)


def _row_ids(n, base):
    return base + lax.broadcasted_iota(jnp.int32, (n, 1), 0)


def _colsum(x):
    return jnp.sum(x, axis=0, keepdims=True)


def _embed_fwd(x, meta, g, b, job=None):
    S = x.shape[0]
    nb = S // BLK + 1

    def body(x_ref, meta_ref, g_ref, b_ref, h_ref, hb_ref):
        n = pl.program_id(0)

        @pl.when(n == 0)
        def _():
            y, _, _ = _ln_rows(meta_ref[...], g_ref[...], b_ref[...])
            h_ref[...] = jnp.zeros_like(h_ref)
            h_ref[PAD:BLK, :] = y

        @pl.when(n > 0)
        def _():
            y, _, _ = _ln_rows(x_ref[...], g_ref[...], b_ref[...])
            h_ref[...] = y

        hb_ref[...] = h_ref[...].astype(BF16)

    return _side_call(
        body, job, name="embed_fwd", grid=(nb,),
        in_specs=[pl.BlockSpec((BLK, D), lambda n: (jnp.maximum(n - 1, 0), 0)),
                  pl.BlockSpec((N_META, D), lambda n: (0, 0)),
                  pl.BlockSpec((1, D), lambda n: (0, 0)),
                  pl.BlockSpec((1, D), lambda n: (0, 0))],
        out_specs=[pl.BlockSpec((BLK, D), lambda n: (n, 0)),
                   pl.BlockSpec((BLK, D), lambda n: (n, 0))],
        out_shape=[jax.ShapeDtypeStruct((nb * BLK, D), F32), jax.ShapeDtypeStruct((nb * BLK, D), BF16)],
        scratch_shapes=[], semantics=("arbitrary",), args=[x, meta, g, b])


def _embed_bwd(dh, x, meta, g, b):
    S = x.shape[0]
    nb = S // BLK + 1

    def body(dh_ref, x_ref, meta_ref, g_ref, b_ref, gx_ref, gm_ref, dg_ref, db_ref):
        n = pl.program_id(0)

        @pl.when(n == 0)
        def _():
            _, xhat, rstd = _ln_rows(meta_ref[...], g_ref[...], b_ref[...])
            dy = dh_ref[PAD:BLK, :]
            gm_ref[...] = _ln_bwd_rows(dy, xhat, rstd, g_ref[...])
            dg_ref[...] = _colsum(dy * xhat)
            db_ref[...] = _colsum(dy)

        @pl.when(n > 0)
        def _():
            _, xhat, rstd = _ln_rows(x_ref[...], g_ref[...], b_ref[...])
            dy = dh_ref[...]
            gx_ref[...] = _ln_bwd_rows(dy, xhat, rstd, g_ref[...])
            dg_ref[...] += _colsum(dy * xhat)
            db_ref[...] += _colsum(dy)

    prev = lambda n: (jnp.maximum(n - 1, 0), 0)
    const = lambda n: (0, 0)
    return pl.pallas_call(
        body, name="embed_bwd", grid=(nb,),
        in_specs=[pl.BlockSpec((BLK, D), lambda n: (n, 0)),
                  pl.BlockSpec((BLK, D), prev),
                  pl.BlockSpec((N_META, D), const),
                  pl.BlockSpec((1, D), const),
                  pl.BlockSpec((1, D), const)],
        out_specs=[pl.BlockSpec((BLK, D), prev),
                   pl.BlockSpec((N_META, D), const),
                   pl.BlockSpec((1, D), const),
                   pl.BlockSpec((1, D), const)],
        out_shape=[jax.ShapeDtypeStruct((S, D), F32), jax.ShapeDtypeStruct((N_META, D), F32),
                   jax.ShapeDtypeStruct((1, D), F32), jax.ShapeDtypeStruct((1, D), F32)],
        compiler_params=_cp("arbitrary"),
    )(dh, x, meta, g, b)


def _proj_fwd(hb, w_in, order, first, count, prev, l, job=None):
    T = hb.shape[0]
    tm = _pick(T, (1056, 384, 128))

    def body(o_sc, a_ref, w_ref, *rest):
        rest[-1][...] = _dot(a_ref[...], w_ref[...])

    return _side_call(
        body, job, name=f"proj_fwd{l}_{first}", grid=(T // tm, count),
        in_specs=[pl.BlockSpec((tm, D), lambda i, j, o: (i, 0)),
                  pl.BlockSpec((None, D, WIN_SH), lambda i, j, o: (o[first + j], 0, 0))]
        + ([] if prev is None else [pl.BlockSpec(memory_space=pl.ANY)]),
        out_specs=[pl.BlockSpec((tm, WIN_SH), lambda i, j, o: (i, o[first + j]))],
        out_shape=[jax.ShapeDtypeStruct((T, IN_TOTAL), F32)],
        scratch_shapes=[], semantics=("parallel", "arbitrary"),
        args=[hb, w_in] + ([] if prev is None else [prev]),
        aliases=None if prev is None else {2: 0}, prefetch=[order])


def _out_fwd(yc, ya, yl, w_out, h, g, b, l, job=None):
    T = h.shape[0]
    tm = _pick(T, (384, 128))

    def body(yc_ref, ya_ref, yl_ref, w_ref, h_ref, g_ref, b_ref, hn_ref, hnb_ref, xh_ref, rs_ref):
        acc = _dot(yc_ref[...], w_ref[0])
        acc += _dot(ya_ref[:, 0:WOUT_SH], w_ref[1])
        acc += _dot(ya_ref[:, WOUT_SH:2 * WOUT_SH], w_ref[2])
        acc += _dot(yl_ref[...], w_ref[3])
        z = ALPHA * h_ref[...] + acc
        y, xhat, rstd = _ln_rows(z, g_ref[...], b_ref[...])
        hn_ref[...] = y
        hnb_ref[...] = y.astype(BF16)
        xh_ref[...] = xhat
        rs_ref[...] = rstd

    row = lambda i: (i, 0)
    return _side_call(
        body, job, name=f"out_fwd{l}", grid=(T // tm,),
        in_specs=[pl.BlockSpec((tm, CW), row), pl.BlockSpec((tm, AW), row), pl.BlockSpec((tm, LW), row),
                  pl.BlockSpec((N_SHARD, WOUT_SH, D), lambda i: (0, 0, 0)),
                  pl.BlockSpec((tm, D), row),
                  pl.BlockSpec((None, 1, D), lambda i: (l, 0, 0)),
                  pl.BlockSpec((None, 1, D), lambda i: (l, 0, 0))],
        out_specs=[pl.BlockSpec((tm, D), row), pl.BlockSpec((tm, D), row), pl.BlockSpec((tm, D), row),
                   pl.BlockSpec((tm, 1), row)],
        out_shape=[jax.ShapeDtypeStruct((T, D), F32), jax.ShapeDtypeStruct((T, D), BF16),
                   jax.ShapeDtypeStruct((T, D), F32), jax.ShapeDtypeStruct((T, 1), F32)],
        scratch_shapes=[], semantics=("parallel",), args=[yc, ya, yl, w_out, h, g, b])


def _post_ln_dcat_bwd(src, target, xhat, rstd, g, w_out, conv, proj, cln_g, cln_b, pw_w, pw_b, l, job=None):
    T = src.shape[0]
    tm = _pick(T, (384, 128))
    per = tm // BLK if target is not None else 0
    last_blk = target.shape[0] // BLK - 1 if target is not None else 0

    def body(s_ref, *refs):
        t_refs = refs[:per]
        (xh_ref, rs_ref, g_ref, w_ref, conv_ref, ct_ref, cg_ref, cb_ref, pw_ref, pb_ref,
         part_ref, dz_ref, dzb_ref, dg_ref, db_ref, da_ref, dl_ref,
         dconv_ref, dct_ref, dpw_ref, dpb_ref, dcg_ref, dcb_ref) = refs[per:]
        i = pl.program_id(0)

        @pl.when(i == 0)
        def _():
            for ref in (part_ref, dg_ref, db_ref, dpw_ref, dpb_ref, dcg_ref, dcb_ref):
                ref[...] = jnp.zeros_like(ref)

        if per:
            tgt = jnp.concatenate([r[...] for r in t_refs], axis=0) if per > 1 else t_refs[0][...]
            real = _row_ids(tm, i * tm) >= BLK
            err = jnp.where(real, s_ref[...] - tgt, 0.0)
            part_ref[...] += _colsum(err * err) * (0.5 / D)
            dy = err * (1.0 / D)
        else:
            dy = s_ref[...]
        xhat = xh_ref[...]
        dz = _ln_bwd_rows(dy, xhat, rs_ref[...], g_ref[...])
        dzb = dz.astype(BF16)
        dz_ref[...] = dz
        dzb_ref[...] = dzb
        dg_ref[...] += _colsum(dy * xhat)
        db_ref[...] += _colsum(dy)
        da_ref[:, 0:WOUT_SH] = _dot_nt(dzb, w_ref[1])
        da_ref[:, WOUT_SH:2 * WOUT_SH] = _dot_nt(dzb, w_ref[2])
        dl_ref[...] = _dot_nt(dzb, w_ref[3])

        d_yc = _dot_nt(dzb, w_ref[0])
        u, chat, crstd = _ln_rows(conv_ref[...], cg_ref[...], cb_ref[...])
        s, ds_du = _silu_and_grad(u)
        sb = s.astype(BF16)
        cpw = _dot(sb, pw_ref[...]) + pb_ref[...]
        gate, dgate = _silu_and_grad(ct_ref[...])
        d_cpw = d_yc * gate
        dct_ref[...] = (d_yc * cpw * dgate).astype(BF16)
        d_cpw_b = d_cpw.astype(BF16)
        dpb_ref[...] += _colsum(d_cpw)
        dpw_ref[...] += _dot_tn(sb, d_cpw_b)
        du = _dot_nt(d_cpw_b, pw_ref[...]) * ds_du
        dconv_ref[...] = _ln_bwd_rows(du, chat, crstd, cg_ref[...])
        dcg_ref[...] += _colsum(du * chat)
        dcb_ref[...] += _colsum(du)

    row = lambda i: (i, 0)
    const = lambda i: (0, 0)
    vec = pl.BlockSpec((None, 1, CW), lambda i: (l, 0, 0))
    t_specs = [pl.BlockSpec((BLK, D), functools.partial(lambda i, q: (jnp.clip(i * per - 1 + q, 0, last_blk), 0), q=q))
               for q in range(per)]
    return _side_call(
        body, job, name=f"post_ln_dcat_bwd{l}", grid=(T // tm,),
        in_specs=[pl.BlockSpec((tm, D), row)] + t_specs + [
            pl.BlockSpec((tm, D), row), pl.BlockSpec((tm, 1), row), pl.BlockSpec((None, 1, D), lambda i: (l, 0, 0)),
            pl.BlockSpec((N_SHARD, WOUT_SH, D), lambda i: (0, 0, 0)),
            pl.BlockSpec((tm, CW), row), pl.BlockSpec((tm, CW), lambda i: (i, 2)), vec, vec,
            pl.BlockSpec((CW, CW), const), vec],
        out_specs=[pl.BlockSpec((1, D), const), pl.BlockSpec((tm, D), row), pl.BlockSpec((tm, D), row),
                   pl.BlockSpec((1, D), const), pl.BlockSpec((1, D), const),
                   pl.BlockSpec((tm, AW), row), pl.BlockSpec((tm, LW), row),
                   pl.BlockSpec((tm, CW), row), pl.BlockSpec((tm, CW), lambda i: (i, 2)),
                   pl.BlockSpec((CW, CW), const), pl.BlockSpec((1, CW), const),
                   pl.BlockSpec((1, CW), const), pl.BlockSpec((1, CW), const)],
        out_shape=[jax.ShapeDtypeStruct((1, D), F32), jax.ShapeDtypeStruct((T, D), F32),
                   jax.ShapeDtypeStruct((T, D), BF16), jax.ShapeDtypeStruct((1, D), F32),
                   jax.ShapeDtypeStruct((1, D), F32),
                   jax.ShapeDtypeStruct((T, AW), F32), jax.ShapeDtypeStruct((T, LW), F32),
                   jax.ShapeDtypeStruct((T, CW), F32), jax.ShapeDtypeStruct((T, IN_TOTAL), BF16),
                   jax.ShapeDtypeStruct((CW, CW), F32), jax.ShapeDtypeStruct((1, CW), F32),
                   jax.ShapeDtypeStruct((1, CW), F32), jax.ShapeDtypeStruct((1, CW), F32)],
        scratch_shapes=[], semantics=("arbitrary",),
        args=[src] + [target] * per + [xhat, rstd, g, w_out, conv, proj, cln_g, cln_b, pw_w, pw_b])


def _dwout_bwd(yc, ya, yl, dzb, l):
    T = dzb.shape[0]
    tm = _pick(T, (384, 128))

    def body(yc_ref, ya_ref, yl_ref, dz_ref, o_ref):
        @pl.when(pl.program_id(0) == 0)
        def _():
            o_ref[...] = jnp.zeros_like(o_ref)

        cat = jnp.concatenate([yc_ref[...], ya_ref[...], yl_ref[...]], axis=1)
        o_ref[...] += _dot_tn(cat, dz_ref[...])

    row = lambda t: (t, 0)
    out = pl.pallas_call(
        body, name=f"dwout_bwd{l}", grid=(T // tm,),
        in_specs=[pl.BlockSpec((tm, CW), row), pl.BlockSpec((tm, AW), row), pl.BlockSpec((tm, LW), row),
                  pl.BlockSpec((tm, D), row)],
        out_specs=pl.BlockSpec((D, D), lambda t: (0, 0)),
        out_shape=jax.ShapeDtypeStruct((D, D), F32),
        compiler_params=_cp("arbitrary"),
    )(yc, ya, yl, dzb)
    return out.reshape(N_SHARD, 2, WOUT_SH // 2, D)


def _dh_bwd(dproj, w_in, dz, l, job=None):
    T = dproj.shape[0]
    tm = _pick(T, (1056, 384, 128))

    n_w = len(w_in)

    def body(dp_ref, *refs):
        w_refs, (dz_ref, o_ref, acc_ref) = refs[:n_w], refs[n_w:]
        j = pl.program_id(1)

        @pl.when(j == 0)
        def _():
            acc_ref[...] = ALPHA * dz_ref[...]

        dp = dp_ref[...]
        off = 0
        for w_ref in w_refs:
            rows = w_ref.shape[0]
            acc_ref[:, off:off + rows] += _dot_nt(dp, w_ref[...])
            off += rows

        @pl.when(j == N_SHARD - 1)
        def _():
            o_ref[...] = acc_ref[...]

    return _side_call(
        body, job, name=f"dh_bwd{l}", grid=(T // tm, N_SHARD),
        in_specs=[pl.BlockSpec((tm, WIN_SH), lambda i, j: (i, j))]
        + [pl.BlockSpec((None, w.shape[1], WIN_SH), lambda i, j: (j, 0, 0)) for w in w_in]
        + [pl.BlockSpec((tm, D), lambda i, j: (i, 0))],
        out_specs=[pl.BlockSpec((tm, D), lambda i, j: (i, 0))],
        out_shape=[jax.ShapeDtypeStruct((T, D), F32)],
        scratch_shapes=[pltpu.VMEM((tm, D), F32)],
        semantics=("parallel", "arbitrary"), args=[dproj, *w_in, dz])


def _dwin_bwd(hb, dproj, l):
    T = hb.shape[0]
    tm = _pick(T, (1056, 384, 128))

    def body(h_ref, dp_ref, o_ref):
        @pl.when(pl.program_id(1) == 0)
        def _():
            o_ref[...] = jnp.zeros_like(o_ref)

        o_ref[...] += _dot_tn(h_ref[...], dp_ref[...])

    out = pl.pallas_call(
        body, name=f"dwin_bwd{l}", grid=(N_SHARD, T // tm),
        in_specs=[pl.BlockSpec((tm, D), lambda j, t: (t, 0)),
                  pl.BlockSpec((tm, WIN_SH), lambda j, t: (t, j))],
        out_specs=pl.BlockSpec((None, D, WIN_SH), lambda j, t: (j, 0, 0)),
        out_shape=jax.ShapeDtypeStruct((N_SHARD, D, WIN_SH), F32),
        compiler_params=_cp("parallel", "arbitrary"),
    )(hb, dproj)
    return out.reshape(N_SHARD, 2, D // 2, WIN_SH)


def _dwin_half(hb, dproj, which, l, tag, job=None):
    T = hb.shape[0]
    tm = _pick(T, (1056, 384, 128))
    hr = D // 2

    def body(w_ref, h_ref, dp_ref, o_ref):
        @pl.when(pl.program_id(1) == 0)
        def _():
            o_ref[...] = jnp.zeros_like(o_ref)

        o_ref[...] += _dot_tn(h_ref[...], dp_ref[...])

    return _side_call(
        body, job, name=f"dwin_{tag}{l}", grid=(N_SHARD, T // tm),
        in_specs=[pl.BlockSpec((tm, hr), lambda j, t, w: (t, w[0])),
                  pl.BlockSpec((tm, WIN_SH), lambda j, t, w: (t, j))],
        out_specs=[pl.BlockSpec((None, hr, WIN_SH), lambda j, t, w: (j, 0, 0))],
        out_shape=[jax.ShapeDtypeStruct((N_SHARD, hr, WIN_SH), F32)],
        scratch_shapes=[], semantics=("parallel", "arbitrary"), args=[hb, dproj],
        prefetch=[jnp.reshape(which, (1,)).astype(jnp.int32)])


def _glu_masked(v, g, base_row):
    rows = _row_ids(v.shape[0], base_row)
    return jnp.where(rows >= PAD, v * _sigmoid(g), 0.0)


def _conv_tile(T):
    return _pick(T, (384, 128))


SUBLANES = 8


def _for_each_shift(buf, rot, tm, offsets, fn):
    for r in range(SUBLANES):
        group = [o for o in offsets if o % SUBLANES == r]
        if not group:
            continue
        if r == 0:
            src = buf
        else:
            n = tm + max(group) - r
            rot[0:n, :] = buf[r:r + n, :]
            src = rot
        for o in group:
            fn(o, src[o - r:o - r + tm, :])


def _conv_fwd(proj, dw_w, dw_b, ln_g, ln_b, pw_w, pw_b, l, job=None):
    T = proj.shape[0]
    tm = _conv_tile(T)
    hb = tm // HALO

    def body(cv_ref, cg_ref, ct_ref, hv_ref, hg_ref, w_ref, b_ref, g_ref, be_ref, pw_ref, pb_ref,
             yc_ref, conv_ref, buf, rot):
        i = pl.program_id(0)
        buf[0:HALO, :] = _glu_masked(hv_ref[...], hg_ref[...], i * tm - HALO)
        buf[HALO:HALO + tm, :] = _glu_masked(cv_ref[...], cg_ref[...], i * tm)
        first = HALO - (CONV_K - 1)
        total = [jnp.zeros((tm, CW), F32) + b_ref[...]]

        def tap(o, tile):
            k = o - first
            total[0] = total[0] + w_ref[k:k + 1, :] * tile

        _for_each_shift(buf, rot, tm, [first + k for k in range(CONV_K)], tap)
        acc = total[0]
        conv_ref[...] = acc
        u, _, _ = _ln_rows(acc, g_ref[...], be_ref[...])
        s = u * _sigmoid(u)
        cpw = _dot(s.astype(BF16), pw_ref[...]) + pb_ref[...]
        gate, _ = _silu_and_grad(ct_ref[...])
        yc_ref[...] = (cpw * gate).astype(BF16)

    vec = pl.BlockSpec((None, 1, CW), lambda i: (l, 0, 0))
    return _side_call(
        body, job, name=f"conv_fwd{l}", grid=(T // tm,),
        in_specs=[pl.BlockSpec((tm, CW), lambda i: (i, 0)),
                  pl.BlockSpec((tm, CW), lambda i: (i, 1)),
                  pl.BlockSpec((tm, CW), lambda i: (i, 2)),
                  pl.BlockSpec((HALO, CW), lambda i: (jnp.maximum(i * hb - 1, 0), 0)),
                  pl.BlockSpec((HALO, CW), lambda i: (jnp.maximum(i * hb - 1, 0), 1)),
                  pl.BlockSpec((None, CONV_K, CW), lambda i: (l, 0, 0)),
                  vec, vec, vec,
                  pl.BlockSpec((CW, CW), lambda i: (0, 0)),
                  vec],
        out_specs=[pl.BlockSpec((tm, CW), lambda i: (i, 0)), pl.BlockSpec((tm, CW), lambda i: (i, 0))],
        out_shape=[jax.ShapeDtypeStruct((T, CW), BF16), jax.ShapeDtypeStruct((T, CW), F32)],
        scratch_shapes=[pltpu.VMEM((tm + HALO, CW), F32), pltpu.VMEM((tm + HALO, CW), F32)],
        semantics=("parallel",), args=[proj, proj, proj, proj, proj, dw_w, dw_b, ln_g, ln_b, pw_w, pw_b])


def _conv_bwd_taps(d_conv, proj, dw_w, dproj, l, job=None):
    T = d_conv.shape[0]
    tm = _conv_tile(T)
    hb = tm // HALO
    nt = T // tm
    last_halo = T // HALO - 1

    def body(dc_ref, dh_ref, cv_ref, cg_ref, hv_ref, hg_ref, w_ref, _, o_ref, dw_ref, dwb_ref, cbuf, dbuf, rot):
        i = pl.program_id(0)

        @pl.when(i == 0)
        def _():
            dw_ref[...] = jnp.zeros_like(dw_ref)
            dwb_ref[...] = jnp.zeros_like(dwb_ref)

        cbuf[0:HALO, :] = _glu_masked(hv_ref[...], hg_ref[...], i * tm - HALO)
        cbuf[HALO:HALO + tm, :] = _glu_masked(cv_ref[...], cg_ref[...], i * tm)
        dmain = dc_ref[...]
        dbuf[0:tm, :] = dmain
        dbuf[tm:tm + HALO, :] = jnp.where(i < nt - 1, dh_ref[...], 0.0)
        total = [jnp.zeros((tm, CW), F32)]

        def tap_back(o, tile):
            k = CONV_K - 1 - o
            total[0] = total[0] + w_ref[k:k + 1, :] * tile

        _for_each_shift(dbuf, rot, tm, list(range(CONV_K)), tap_back)
        acc = total[0]
        first = HALO - (CONV_K - 1)

        def tap_weight(o, tile):
            k = o - first
            dw_ref[k:k + 1, :] += _colsum(dmain * tile)

        _for_each_shift(cbuf, rot, tm, [first + k for k in range(CONV_K)], tap_weight)
        dwb_ref[...] += _colsum(dmain)
        d_c = jnp.where(_row_ids(tm, i * tm) >= PAD, acc, 0.0)
        sig = _sigmoid(cg_ref[...])
        o_ref[:, 0:CW] = (d_c * sig).astype(BF16)
        o_ref[:, CW:2 * CW] = (d_c * cv_ref[...] * sig * (1.0 - sig)).astype(BF16)

    const = lambda i: (0, 0)
    return _side_call(
        body, job, name=f"conv_bwd_taps{l}", grid=(nt,),
        in_specs=[pl.BlockSpec((tm, CW), lambda i: (i, 0)),
                  pl.BlockSpec((HALO, CW), lambda i: (jnp.minimum((i + 1) * hb, last_halo), 0)),
                  pl.BlockSpec((tm, CW), lambda i: (i, 0)),
                  pl.BlockSpec((tm, CW), lambda i: (i, 1)),
                  pl.BlockSpec((HALO, CW), lambda i: (jnp.maximum(i * hb - 1, 0), 0)),
                  pl.BlockSpec((HALO, CW), lambda i: (jnp.maximum(i * hb - 1, 0), 1)),
                  pl.BlockSpec((None, CONV_K, CW), lambda i: (l, 0, 0)),
                  pl.BlockSpec(memory_space=pl.ANY)],
        out_specs=[pl.BlockSpec((tm, 2 * CW), lambda i: (i, 0)),
                   pl.BlockSpec((HALO, CW), const), pl.BlockSpec((1, CW), const)],
        out_shape=[jax.ShapeDtypeStruct(dproj.shape, BF16), jax.ShapeDtypeStruct((HALO, CW), F32),
                   jax.ShapeDtypeStruct((1, CW), F32)],
        scratch_shapes=[pltpu.VMEM((tm + HALO, CW), F32), pltpu.VMEM((tm + HALO, CW), F32),
                        pltpu.VMEM((tm + HALO, CW), F32)],
        semantics=("arbitrary",), aliases={7: 0},
        args=[d_conv, d_conv, proj, proj, proj, proj, dw_w, dproj])


def _log1p_small(e):
    return jnp.where(e < 1e-3, e * (1.0 - e * (0.5 - e * (1.0 / 3.0))), jnp.log(1.0 + e))


def _softplus(z):
    return jnp.maximum(z, 0.0) + _log1p_small(jnp.exp(-jnp.abs(z)))


def _neg_expm1(x):
    series = -x * (1.0 + x * (1.0 / 2.0) * (1.0 + x * (1.0 / 3.0) * (1.0 + x * (1.0 / 4.0) * (
        1.0 + x * (1.0 / 5.0) * (1.0 + x * (1.0 / 6.0) * (1.0 + x * (1.0 / 7.0)))))))
    return jnp.where(x > -0.25, series, 1.0 - jnp.exp(x))


def _lru_gates(rxbuf, tm, base_row, lw_ref, lb_ref, wa_ref, ba_ref, wx_ref, bx_ref, lam_ref):
    rc = jnp.zeros((tm, LW), F32) + lb_ref[...]
    for k in range(LRU_K):
        o = LHALO - (LRU_K - 1) + k
        rc += lw_ref[k:k + 1, :] * rxbuf[o:o + tm, :]
    rcb = rc.astype(BF16)
    r = _sigmoid(_dot(rcb, wa_ref[...]) + ba_ref[...])
    ig = _sigmoid(_dot(rcb, wx_ref[...]) + bx_ref[...])
    sp = _softplus(-lam_ref[...])
    la = -LRU_C * r * sp
    a = jnp.exp(la)
    mult = jnp.sqrt(_neg_expm1(2.0 * la))
    valid = _row_ids(tm, base_row) >= PAD
    return rc, rcb, r, ig, sp, a, mult, valid


def _mask_rows(v, base_row):
    return jnp.where(_row_ids(v.shape[0], base_row) >= PAD, v, 0.0)


def _scan_rows(aa, bb, carry, out_ref, reverse):
    tm = aa.shape[0]
    sub = _row_ids(tm, 0) & (SUBLANES - 1)
    s = 1
    while s < SUBLANES:
        keep = (sub < SUBLANES - s) if reverse else (sub >= s)
        shift = tm - s if reverse else s
        a_s = jnp.where(keep, pltpu.roll(aa, shift, axis=0), 1.0)
        b_s = jnp.where(keep, pltpu.roll(bb, shift, axis=0), 0.0)
        bb = aa * b_s + bb
        aa = aa * a_s
        s *= 2
    groups = range(tm // SUBLANES)
    edge = 0 if reverse else SUBLANES - 1
    for j in (reversed(groups) if reverse else groups):
        rows = slice(SUBLANES * j, SUBLANES * j + SUBLANES)
        x = bb[rows] + aa[rows] * carry
        out_ref[rows, :] = x
        carry = x[edge:edge + 1]


def _lru_tile(T):
    return _pick(T, (384, 128))


def _lru_fwd(proj, lw, lb, wa, ba, wx, bx, lam, l, job=None):
    T = proj.shape[0]
    tm = _lru_tile(T)
    hb = tm // LHALO

    def body(rx_ref, rg_ref, hx_ref, lw_ref, lb_ref, wa_ref, ba_ref, wx_ref, bx_ref, lam_ref,
             yl_ref, hl_ref, rxbuf, carry):
        i = pl.program_id(0)

        @pl.when(i == 0)
        def _():
            carry[...] = jnp.zeros_like(carry)

        rxbuf[0:LHALO, :] = _mask_rows(hx_ref[...], i * tm - LHALO)
        rxbuf[LHALO:LHALO + tm, :] = _mask_rows(rx_ref[...], i * tm)
        rc, _, _, ig, _, a, mult, valid = _lru_gates(rxbuf, tm, i * tm, lw_ref, lb_ref, wa_ref, ba_ref,
                                                     wx_ref, bx_ref, lam_ref)
        bb = jnp.where(valid, mult * (ig * rc), 0.0)
        _scan_rows(a, bb, carry[0:1, :], hl_ref, reverse=False)
        carry[0:1, :] = hl_ref[tm - 1:tm, :]
        gate, _ = _silu_and_grad(rg_ref[...])
        yl_ref[...] = (hl_ref[...] * gate).astype(BF16)

    vec = pl.BlockSpec((None, 1, LW), lambda i: (l, 0, 0))
    mat = pl.BlockSpec((None, LW, LW), lambda i: (l, 0, 0))
    return _side_call(
        body, job, name=f"lru_fwd{l}", grid=(T // tm,),
        in_specs=[pl.BlockSpec((tm, LW), lambda i: (i, 8)),
                  pl.BlockSpec((tm, LW), lambda i: (i, 9)),
                  pl.BlockSpec((LHALO, LW), lambda i: (jnp.maximum(i * hb - 1, 0), 8)),
                  pl.BlockSpec((None, LRU_K, LW), lambda i: (l, 0, 0)),
                  vec, mat, vec, mat, vec, vec],
        out_specs=[pl.BlockSpec((tm, LW), lambda i: (i, 0)), pl.BlockSpec((tm, LW), lambda i: (i, 0))],
        out_shape=[jax.ShapeDtypeStruct((T, LW), BF16), jax.ShapeDtypeStruct((T, LW), F32)],
        scratch_shapes=[pltpu.VMEM((tm + LHALO, LW), F32), pltpu.VMEM((8, LW), F32)],
        semantics=("arbitrary",), args=[proj, proj, proj, lw, lb, wa, ba, wx, bx, lam])


def _lru_bwd(proj, hl, d_yl, lw, lb, wa, ba, wx, bx, lam, dproj, l, job=None):
    T = proj.shape[0]
    tm = _lru_tile(T)
    hb = tm // LHALO
    nt = T // tm

    def body(rx_ref, rg_ref, hx_ref, hl_ref, hh_ref, dy_ref, lw_ref, lb_ref, wa_ref, ba_ref, wx_ref, bx_ref,
             lam_ref, _, o_ref, dlw_ref, dlb_ref, dwa_ref, dba_ref, dwx_ref, dbx_ref, dlam_ref,
             rxbuf, dbuf, carry, head, gbuf):
        step = pl.program_id(0)
        i = nt - 1 - step

        @pl.when(step == 0)
        def _():
            carry[...] = jnp.zeros_like(carry)
            head[...] = jnp.zeros_like(head)
            for ref in (dlw_ref, dlb_ref, dwa_ref, dba_ref, dwx_ref, dbx_ref, dlam_ref):
                ref[...] = jnp.zeros_like(ref)

        rxbuf[0:LHALO, :] = _mask_rows(hx_ref[...], i * tm - LHALO)
        rxbuf[LHALO:LHALO + tm, :] = _mask_rows(rx_ref[...], i * tm)
        rc, rcb, r, ig, sp, a, mult, valid = _lru_gates(rxbuf, tm, i * tm, lw_ref, lb_ref, wa_ref, ba_ref,
                                                        wx_ref, bx_ref, lam_ref)
        rows = _row_ids(tm, 0)
        h = hl_ref[...]
        h_before = jnp.where(i > 0, hh_ref[LHALO - 1:LHALO, :], 0.0)
        hprev = jnp.where(rows == 0, h_before, pltpu.roll(h, 1, axis=0))
        rg = rg_ref[...]
        gate, dgate = _silu_and_grad(rg)
        dy = dy_ref[...]
        o_ref[:, LW:2 * LW] = (dy * h * dgate).astype(BF16)
        bb = dy * gate + jnp.where(rows == tm - 1, carry[0:1, :], 0.0)
        aa = jnp.where(rows == tm - 1, 0.0, pltpu.roll(a, tm - 1, axis=0))
        _scan_rows(aa, bb, jnp.zeros((1, LW), F32), gbuf, reverse=True)
        g = gbuf[...]
        dbuf[0:tm, :] = a * g
        carry[0:1, :] = dbuf[0:1, :]
        du = jnp.where(valid, g, 0.0)
        da = g * hprev
        dix = du * mult
        dmult = du * (ig * rc)
        dla = jnp.where(valid, da * a - dmult * (a * a) / mult, 0.0)
        dr = dla * (-LRU_C * sp)
        dlam_ref[...] += _colsum(dla * (LRU_C * r)) * _sigmoid(-lam_ref[...])
        dpa = dr * r * (1.0 - r)
        dpx = (dix * rc) * ig * (1.0 - ig)
        dpab = dpa.astype(BF16)
        dpxb = dpx.astype(BF16)
        dba_ref[...] += _colsum(dpa)
        dbx_ref[...] += _colsum(dpx)
        dwa_ref[...] += _dot_tn(rcb, dpab)
        dwx_ref[...] += _dot_tn(rcb, dpxb)
        drc = dix * ig + _dot_nt(dpab, wa_ref[...]) + _dot_nt(dpxb, wx_ref[...])
        dbuf[0:tm, :] = drc
        dbuf[tm:tm + LHALO, :] = head[...]
        acc = jnp.zeros((tm, LW), F32)
        for k in range(LRU_K):
            o = LRU_K - 1 - k
            acc += lw_ref[k:k + 1, :] * dbuf[o:o + tm, :]
            oc = LHALO - (LRU_K - 1) + k
            dlw_ref[k:k + 1, :] += _colsum(drc * rxbuf[oc:oc + tm, :])
        dlb_ref[...] += _colsum(drc)
        head[...] = dbuf[0:LHALO, :]
        o_ref[:, 0:LW] = jnp.where(valid, acc, 0.0).astype(BF16)

    rev = lambda s: nt - 1 - s
    vec = pl.BlockSpec((None, 1, LW), lambda s: (l, 0, 0))
    mat = pl.BlockSpec((None, LW, LW), lambda s: (l, 0, 0))
    const = lambda s: (0, 0)
    halo = lambda s: jnp.maximum(rev(s) * hb - 1, 0)
    return _side_call(
        body, job, name=f"lru_bwd{l}", grid=(nt,),
        in_specs=[pl.BlockSpec((tm, LW), lambda s: (rev(s), 8)),
                  pl.BlockSpec((tm, LW), lambda s: (rev(s), 9)),
                  pl.BlockSpec((LHALO, LW), lambda s: (halo(s), 8)),
                  pl.BlockSpec((tm, LW), lambda s: (rev(s), 0)),
                  pl.BlockSpec((LHALO, LW), lambda s: (halo(s), 0)),
                  pl.BlockSpec((tm, LW), lambda s: (rev(s), 0)),
                  pl.BlockSpec((None, LRU_K, LW), lambda s: (l, 0, 0)),
                  vec, mat, vec, mat, vec, vec, pl.BlockSpec(memory_space=pl.ANY)],
        out_specs=[pl.BlockSpec((tm, 2 * LW), lambda s: (rev(s), 4)),
                   pl.BlockSpec((8, LW), const), pl.BlockSpec((1, LW), const),
                   pl.BlockSpec((LW, LW), const), pl.BlockSpec((1, LW), const),
                   pl.BlockSpec((LW, LW), const), pl.BlockSpec((1, LW), const),
                   pl.BlockSpec((1, LW), const)],
        out_shape=[jax.ShapeDtypeStruct(dproj.shape, BF16),
                   jax.ShapeDtypeStruct((8, LW), F32), jax.ShapeDtypeStruct((1, LW), F32),
                   jax.ShapeDtypeStruct((LW, LW), F32), jax.ShapeDtypeStruct((1, LW), F32),
                   jax.ShapeDtypeStruct((LW, LW), F32), jax.ShapeDtypeStruct((1, LW), F32),
                   jax.ShapeDtypeStruct((1, LW), F32)],
        scratch_shapes=[pltpu.VMEM((tm + LHALO, LW), F32), pltpu.VMEM((tm + LHALO, LW), F32),
                        pltpu.VMEM((8, LW), F32), pltpu.VMEM((LHALO, LW), F32), pltpu.VMEM((tm, LW), F32)],
        semantics=("arbitrary",), aliases={13: 0},
        args=[proj, proj, proj, hl, hl, d_yl, lw, lb, wa, ba, wx, bx, lam, dproj])


def _rope_tables(T):
    pos = (lax.broadcasted_iota(jnp.int32, (T, 128), 0) - PAD).astype(F32)
    lane = lax.broadcasted_iota(jnp.int32, (T, 128), 1) % 64
    inv_freq = ROPE_THETA ** (-(lane % ROT_HALF).astype(F32) / ROT_HALF)
    ang = pos * inv_freq
    cos, sin = jnp.cos(ang), jnp.sin(ang)
    c = jnp.where(lane < 2 * ROT_HALF, cos, 1.0)
    s1 = jnp.where(lane < ROT_HALF, -sin, 0.0)
    s2 = jnp.where((lane >= ROT_HALF) & (lane < 2 * ROT_HALF), sin, 0.0)
    return c, s1, s2


def _rot_fwd(x, c, s1, s2):
    return x * c + pltpu.roll(x, 128 - ROT_HALF, axis=1) * s1 + pltpu.roll(x, ROT_HALF, axis=1) * s2


def _rot_bwd(dy, c, s1, s2):
    return dy * c + pltpu.roll(dy * s1, ROT_HALF, axis=1) + pltpu.roll(dy * s2, 128 - ROT_HALF, axis=1)


def _rope_fwd(proj, tabs, l):
    T = proj.shape[0]

    def body(ql_ref, qh_ref, k_ref, v_ref, c_ref, s1_ref, s2_ref, qr_ref, kr_ref, vb_ref):
        c, s1, s2 = c_ref[...], s1_ref[...], s2_ref[...]
        for gcol in range(AW // 128):
            src = ql_ref if gcol < 4 else qh_ref
            x = src[:, 128 * (gcol % 4):128 * (gcol % 4) + 128]
            qr_ref[:, 128 * gcol:128 * gcol + 128] = (_rot_fwd(x, c, s1, s2) * 0.125).astype(BF16)
        for gcol in range(KVW // 128):
            x = k_ref[:, 128 * gcol:128 * gcol + 128]
            kr_ref[:, 128 * gcol:128 * gcol + 128] = _rot_fwd(x, c, s1, s2).astype(BF16)
        vb_ref[...] = v_ref[...].astype(BF16)

    tr = _pick(T, (384, 128))
    tab = pl.BlockSpec((tr, 128), lambda n: (n, 0))
    return pl.pallas_call(
        body, name=f"rope_fwd{l}", grid=(T // tr,),
        in_specs=[pl.BlockSpec((tr, 512), lambda n: (n, 3)), pl.BlockSpec((tr, 512), lambda n: (n, 4)),
                  pl.BlockSpec((tr, KVW), lambda n: (n, 10)), pl.BlockSpec((tr, KVW), lambda n: (n, 11)),
                  tab, tab, tab],
        out_specs=[pl.BlockSpec((tr, AW), lambda n: (n, 0)), pl.BlockSpec((tr, KVW), lambda n: (n, 0)),
                   pl.BlockSpec((tr, KVW), lambda n: (n, 0))],
        out_shape=[jax.ShapeDtypeStruct((T, AW), BF16), jax.ShapeDtypeStruct((T, KVW), BF16),
                   jax.ShapeDtypeStruct((T, KVW), BF16)],
        compiler_params=_cp("parallel"),
    )(proj, proj, proj, proj, *tabs)


GROUP = 4


def _attn_mask(n, reps):
    qi = lax.broadcasted_iota(jnp.int32, (reps * BLK, BLK), 0) & (BLK - 1)
    kj = lax.broadcasted_iota(jnp.int32, (reps * BLK, BLK), 1)
    m0 = (kj >= PAD) & (n >= 1)
    mp = (kj > qi) & (n >= 2)
    mc = (kj <= qi) & ((n >= 1) | (kj >= PAD))
    return jnp.concatenate([m0, mp, mc], axis=1)


def _kv_both(x0_ref, xp_ref, xc_ref, g):
    pg, off = g // 2, g % 2
    cols = slice(128 * pg, 128 * pg + 128)
    x = jnp.concatenate([x0_ref[:, cols], xp_ref[:, cols], xc_ref[:, cols]], axis=0).astype(F32)
    lane = lax.broadcasted_iota(jnp.int32, (1, 128), 1)
    half = jnp.where((lane < 64) if off == 0 else (lane >= 64), x, 0.0)
    return (half + pltpu.roll(half, 64, axis=1)).astype(BF16)


def _kv_halves(x0_ref, xp_ref, xc_ref, g):
    pg, off = g // 2, g % 2
    cols = slice(128 * pg, 128 * pg + 128)
    x = jnp.concatenate([x0_ref[:, cols], xp_ref[:, cols], xc_ref[:, cols]], axis=0).astype(F32)
    lane = lax.broadcasted_iota(jnp.int32, (1, 128), 1)
    if off == 0:
        lo = jnp.where(lane < 64, x, 0.0)
        hi = pltpu.roll(lo, 64, axis=1)
    else:
        hi = jnp.where(lane >= 64, x, 0.0)
        lo = pltpu.roll(hi, 64, axis=1)
    return lo.astype(BF16), hi.astype(BF16)


def _stack_heads(a, b):
    lo = lax.broadcasted_iota(jnp.int32, (1, 128), 1) < 64
    a, b = a.astype(F32), b.astype(F32)
    return jnp.concatenate([jnp.where(lo, a, 0.0), jnp.where(lo, 0.0, a),
                            jnp.where(lo, b, 0.0), jnp.where(lo, 0.0, b)], axis=0).astype(BF16)


def _unstack_heads(x):
    lo = lax.broadcasted_iota(jnp.int32, (1, 128), 1) < 64
    return (jnp.where(lo, x[0:BLK], x[BLK:2 * BLK]), jnp.where(lo, x[2 * BLK:3 * BLK], x[3 * BLK:4 * BLK]))


def _per_head_column(values):
    return jnp.concatenate([jnp.zeros((BLK, 1), F32) + v for v in values], axis=0)


def _attn_fwd(qr, kr, vb, proj, sinks, l, job=None):
    T = qr.shape[0]

    def body(sink_ref, q_ref, k0_ref, kp_ref, kc_ref, v0_ref, vp_ref, vc_ref, ag_ref, ya_ref, att_ref, lse_ref):
        n = pl.program_id(0)
        mask = _attn_mask(n, 1)
        lane = lax.broadcasted_iota(jnp.int32, (1, 128), 1)
        lse_acc = jnp.zeros((BLK, 128), F32)
        for g in range(4):
            kx = _kv_both(k0_ref, kp_ref, kc_ref, g)
            vx = _kv_both(v0_ref, vp_ref, vc_ref, g)
            pair_cols = [slice(128 * (2 * g + pp), 128 * (2 * g + pp) + 128) for pp in range(2)]
            s4 = _dot_nt(_stack_heads(q_ref[:, pair_cols[0]], q_ref[:, pair_cols[1]]), kx)
            probs = []
            for r in range(GROUP):
                h = GROUP * g + r
                sink = sink_ref[l, h]
                s = jnp.where(mask, s4[BLK * r:BLK * r + BLK], NEG_INF)
                m = jnp.maximum(jnp.max(s, axis=1, keepdims=True), sink)
                p = jnp.exp(s - m)
                denom = jnp.sum(p, axis=1, keepdims=True) + jnp.exp(sink - m)
                probs.append((p * (1.0 / denom)).astype(BF16))
                lse_acc = jnp.where(lane == h, m + jnp.log(denom), lse_acc)
            outs = _unstack_heads(_dot(jnp.concatenate(probs, axis=0), vx))
            for cols, out in zip(pair_cols, outs):
                att_ref[:, cols] = out
                gate, _ = _silu_and_grad(ag_ref[:, cols])
                ya_ref[:, cols] = (out * gate).astype(BF16)
        lse_ref[...] = lse_acc

    prev = lambda n: (jnp.maximum(n - 1, 0), 0)
    cur = lambda n: (n, 0)
    zero = lambda n: (0, 0)
    kv = lambda f: pl.BlockSpec((BLK, KVW), f)
    return _side_call(
        body, job, name=f"attn_fwd{l}", grid=(T // BLK,),
        in_specs=[pl.BlockSpec(memory_space=pltpu.SMEM),
                  pl.BlockSpec((BLK, AW), cur), kv(zero), kv(prev), kv(cur), kv(zero), kv(prev), kv(cur),
                  pl.BlockSpec((BLK, AW), lambda n: (n, 3))],
        out_specs=[pl.BlockSpec((BLK, AW), cur), pl.BlockSpec((BLK, AW), cur), pl.BlockSpec((BLK, 128), cur)],
        out_shape=[jax.ShapeDtypeStruct((T, AW), BF16), jax.ShapeDtypeStruct((T, AW), F32),
                   jax.ShapeDtypeStruct((T, 128), F32)],
        scratch_shapes=[], semantics=("parallel",), args=[sinks, qr, kr, kr, kr, vb, vb, vb, proj])


def _attn_bwd(qr, kr, vb, proj, att, lse, d_ya, sinks, dproj, l, job=None):
    T = qr.shape[0]
    nb = T // BLK

    def body(sink_ref, q_ref, k0_ref, kp_ref, kc_ref, v0_ref, vp_ref, vc_ref, ag_ref, att_ref, lse_ref, dy_ref, _,
             dq_ref, dk_ref, dv_ref, dk0_ref, dv0_ref, dag_ref, dsink_ref, kcarry, vcarry):
        n = pl.program_id(0)

        @pl.when(n == 0)
        def _():
            dk0_ref[...] = jnp.zeros_like(dk0_ref)
            dv0_ref[...] = jnp.zeros_like(dv0_ref)
            dsink_ref[...] = jnp.zeros_like(dsink_ref)
            kcarry[...] = jnp.zeros_like(kcarry)
            vcarry[...] = jnp.zeros_like(vcarry)

        @pl.when(n == nb)
        def _():
            dk_ref[...] = kcarry[...]
            dv_ref[...] = vcarry[...]

        @pl.when(n < nb)
        def _():
            mask = _attn_mask(n, GROUP)
            lane = lax.broadcasted_iota(jnp.int32, (1, 128), 1)
            lse = lse_ref[...]
            dsink = jnp.zeros((1, 128), F32)
            dk_pg, dv_pg = [], []
            for pg in range(2):
                dk_acc = jnp.zeros((3 * BLK, 128), F32)
                dv_acc = jnp.zeros((3 * BLK, 128), F32)
                for off in range(2):
                    g = 2 * pg + off
                    kx = _kv_both(k0_ref, kp_ref, kc_ref, g)
                    vx = _kv_both(v0_ref, vp_ref, vc_ref, g)
                    pair_cols = [slice(128 * (2 * g + pp), 128 * (2 * g + pp) + 128) for pp in range(2)]
                    q4 = _stack_heads(q_ref[:, pair_cols[0]], q_ref[:, pair_cols[1]])
                    d_out = []
                    for cols in pair_cols:
                        gate, dgate = _silu_and_grad(ag_ref[:, cols])
                        dy = dy_ref[:, cols]
                        dag_ref[:, cols] = (dy * att_ref[:, cols] * dgate).astype(BF16)
                        d_out.append(dy * gate)
                    do4 = _stack_heads(d_out[0], d_out[1])
                    heads = [GROUP * g + r for r in range(GROUP)]
                    sink = _per_head_column([sink_ref[l, h] for h in heads])
                    lse4 = _per_head_column(
                        [jnp.sum(jnp.where(lane == h, lse, 0.0), axis=1, keepdims=True) for h in heads])
                    p = jnp.where(mask, jnp.exp(_dot_nt(q4, kx) - lse4), 0.0)
                    dp = _dot_nt(do4, vx)
                    delta = jnp.sum(p * dp, axis=1, keepdims=True)
                    ds = (p * (dp - delta)).astype(BF16)
                    sink_term = jnp.exp(sink - lse4) * delta
                    for r, h in enumerate(heads):
                        dsink += jnp.where(lane == h, -jnp.sum(sink_term[BLK * r:BLK * r + BLK]), 0.0)
                    for cols, dq in zip(pair_cols, _unstack_heads(_dot(ds, kx))):
                        dq_ref[:, cols] = dq
                    dkg = _dot_tn(ds, q4)
                    dvg = _dot_tn(p.astype(BF16), do4)
                    own = (lane < 64) if off == 0 else (lane >= 64)
                    dk_acc += jnp.where(own, dkg + pltpu.roll(dkg, 64, axis=1), 0.0)
                    dv_acc += jnp.where(own, dvg + pltpu.roll(dvg, 64, axis=1), 0.0)
                dk_pg.append(dk_acc)
                dv_pg.append(dv_acc)
            dsink_ref[...] += dsink
            for pg in range(2):
                cols = slice(128 * pg, 128 * pg + 128)
                dk0_ref[:, cols] += dk_pg[pg][0:BLK]
                dv0_ref[:, cols] += dv_pg[pg][0:BLK]
                dk_ref[:, cols] = kcarry[:, cols] + dk_pg[pg][BLK:2 * BLK]
                dv_ref[:, cols] = vcarry[:, cols] + dv_pg[pg][BLK:2 * BLK]
                kcarry[:, cols] = dk_pg[pg][2 * BLK:3 * BLK]
                vcarry[:, cols] = dv_pg[pg][2 * BLK:3 * BLK]

    last = nb - 1
    cur = lambda n: (jnp.minimum(n, last), 0)
    prev = lambda n: (jnp.clip(n - 1, 0, last), 0)
    zero = lambda n: (0, 0)
    kv = lambda f: pl.BlockSpec((BLK, KVW), f)
    wide = lambda f: pl.BlockSpec((BLK, AW), f)
    return _side_call(
        body, job, name=f"attn_bwd{l}", grid=(nb + 1,),
        in_specs=[pl.BlockSpec(memory_space=pltpu.SMEM),
                  wide(cur), kv(zero), kv(prev), kv(cur), kv(zero), kv(prev), kv(cur),
                  pl.BlockSpec((BLK, AW), lambda n: (jnp.minimum(n, last), 3)),
                  wide(cur), pl.BlockSpec((BLK, 128), cur), wide(cur), pl.BlockSpec(memory_space=pl.ANY)],
        out_specs=[wide(cur), kv(prev), kv(prev), kv(zero), kv(zero),
                   pl.BlockSpec((BLK, AW), lambda n: (jnp.minimum(n, last), 3)),
                   pl.BlockSpec((1, 128), zero)],
        out_shape=[jax.ShapeDtypeStruct((T, AW), F32), jax.ShapeDtypeStruct((T, KVW), F32),
                   jax.ShapeDtypeStruct((T, KVW), F32), jax.ShapeDtypeStruct((BLK, KVW), F32),
                   jax.ShapeDtypeStruct((BLK, KVW), F32), jax.ShapeDtypeStruct(dproj.shape, BF16),
                   jax.ShapeDtypeStruct((1, 128), F32)],
        scratch_shapes=[pltpu.VMEM((BLK, KVW), F32), pltpu.VMEM((BLK, KVW), F32)],
        semantics=("arbitrary",), aliases={12: 5},
        args=[sinks, qr, kr, kr, kr, vb, vb, vb, proj, att, lse, d_ya, dproj])


def _rope_bwd(dqr, dk, dv, dk0, dv0, tabs, dproj, l):
    T = dqr.shape[0]

    def body(dq_ref, dk_ref, dv_ref, dk0_ref, dv0_ref, c_ref, s1_ref, s2_ref, _, o_ref):
        n = pl.program_id(0)
        c, s1, s2 = c_ref[...], s1_ref[...], s2_ref[...]
        for gcol in range(AW // 128):
            cols = slice(128 * gcol, 128 * gcol + 128)
            o_ref[:, cols] = (_rot_bwd(dq_ref[:, cols], c, s1, s2) * 0.125).astype(BF16)
        for gcol in range(KVW // 128):
            cols = slice(128 * gcol, 128 * gcol + 128)
            kcols = slice(AW + 128 * gcol, AW + 128 * gcol + 128)
            vcols = slice(AW + KVW + 128 * gcol, AW + KVW + 128 * gcol + 128)
            o_ref[:, kcols] = _rot_bwd(dk_ref[:, cols], c, s1, s2).astype(BF16)
            o_ref[:, vcols] = dv_ref[:, cols].astype(BF16)

            @pl.when(n == 0)
            def _():
                dkk = dk_ref[0:BLK, cols] + dk0_ref[:, cols]
                o_ref[0:BLK, kcols] = _rot_bwd(dkk, c[0:BLK], s1[0:BLK], s2[0:BLK]).astype(BF16)
                o_ref[0:BLK, vcols] = (dv_ref[0:BLK, cols] + dv0_ref[:, cols]).astype(BF16)

    tr = _pick(T, (384, 128))
    cur = lambda n: (n, 0)
    zero = lambda n: (0, 0)
    tab = pl.BlockSpec((tr, 128), cur)
    return pl.pallas_call(
        body, name=f"rope_bwd{l}", grid=(T // tr,),
        in_specs=[pl.BlockSpec((tr, AW), cur), pl.BlockSpec((tr, KVW), cur), pl.BlockSpec((tr, KVW), cur),
                  pl.BlockSpec((BLK, KVW), zero), pl.BlockSpec((BLK, KVW), zero), tab, tab, tab,
                  pl.BlockSpec(memory_space=pl.ANY)],
        out_specs=pl.BlockSpec((tr, AW + 2 * KVW), lambda n: (n, 1)),
        out_shape=jax.ShapeDtypeStruct(dproj.shape, BF16),
        input_output_aliases={8: 0},
        compiler_params=_cp("parallel"),
    )(dqr, dk, dv, dk0, dv0, *tabs, dproj)


def _block_diag(w):
    nl, nh, hd, _ = w.shape
    eye = jnp.eye(nh, dtype=w.dtype)
    return jnp.einsum("lhij,hg->lhigj", w, eye).reshape(nl, nh * hd, nh * hd)


def _diag_blocks(m):
    nh, hd = 8, 64
    return jnp.einsum("hihj->hij", m.reshape(nh, hd, nh, hd))


def _device_step(x, target, p, dist=None):
    vec = lambda a: a.reshape(DEPTH, 1, a.shape[-1])
    ln_in_g, ln_in_b = p["ln_in_g"].reshape(1, D), p["ln_in_b"].reshape(1, D)
    conv_dw_b, conv_ln_g, conv_ln_b, conv_pw_b = map(vec, (p["conv_dw_b"], p["conv_ln_g"], p["conv_ln_b"], p["conv_pw_b"]))
    lru_conv_b, lru_ba, lru_bx, lru_lambda = map(vec, (p["lru_conv_b"], p["lru_ba"], p["lru_bx"], p["lru_lambda"]))
    ln_post_g, ln_post_b = vec(p["ln_post_g"]), vec(p["ln_post_b"])
    wa_bd = _block_diag(p["lru_wa"]).astype(BF16)
    wx_bd = _block_diag(p["lru_wx"]).astype(BF16)
    w_in, w_out, pw_w = list(p["w_in"]), list(p["w_out"]), list(p["conv_pw_w"])
    sinks = p["attn_sinks"]
    big_names = ("w_in", "w_out", "conv_pw_w")

    order = dist[4] if dist else jnp.arange(N_SHARD, dtype=jnp.int32)
    (h, hb), got = _embed_fwd(x, p["meta_tokens"], ln_in_g, ln_in_b,
                              job=_gather_job([w_in[0]], peers=(0, 1)) if dist else None)
    if dist:
        w_in[0] = got[0]
    T = h.shape[0]
    tabs = _rope_tables(T)
    saved = []
    for l in range(DEPTH):
        if l == 0:
            job = _join_jobs(_gather_job([w_in[0]], peers=(2,)), _gather_job([pw_w[0]])) if dist else None
            (proj,), got = _proj_fwd(hb, w_in[0], order, 0, N_SHARD - 1, None, l, job=job)
            if dist:
                w_in[0], pw_w[0] = got
            (proj,), _ = _proj_fwd(hb, w_in[0], order, N_SHARD - 1, 1, proj, l)
        else:
            (proj,), _ = _proj_fwd(hb, w_in[l], order, 0, N_SHARD, None, l)
        pw_l = pw_w[l].reshape(CW, CW)
        (yc, conv), got = _conv_fwd(proj, p["conv_dw_w"], conv_dw_b, conv_ln_g, conv_ln_b, pw_l, conv_pw_b, l,
                                    job=_gather_job([w_out[0]]) if dist and l == 0 else None)
        if got:
            w_out[0] = got[0]
        qr, kr, vb = _rope_fwd(proj, tabs, l)
        (ya, att, lse), got = _attn_fwd(
            qr, kr, vb, proj, sinks, l, job=_gather_job([w_in[1]]) if dist and l == 0 else None)
        if got:
            w_in[1] = got[0]
        (yl, hl), _ = _lru_fwd(proj, p["lru_conv_w"], lru_conv_b, wa_bd, lru_ba, wx_bd, lru_bx, lru_lambda, l)
        (hn, hnb, xhat, rstd), got = _out_fwd(
            yc, ya, yl, w_out[l], h, ln_post_g, ln_post_b, l,
            job=_gather_job([w_out[1], pw_w[1]]) if dist and l == 0 else None)
        if got:
            w_out[1], pw_w[1] = got
        saved.append((hb, proj, yc, conv, qr, kr, vb, ya, att, lse, yl, hl, xhat, rstd, pw_l))
        h, hb = hn, hnb

    dh = None
    g = {}
    later = None
    early, last = ("w_out", "conv_pw_w"), ("w_in",)
    own = {}
    for l in reversed(range(DEPTH)):
        hb_l, proj, yc, conv, qr, kr, vb, ya, att, lse, yl, hl, xhat, rstd, pw_l = saved[l]
        tail = dist is not None and l == 0
        top = l == DEPTH - 1
        (part, dz, dzb, g["ln_post_g", l], g["ln_post_b", l], d_ya, d_yl, d_conv, dproj, dpw, g["conv_pw_b", l],
         g["conv_ln_g", l], g["conv_ln_b", l]), recv = _post_ln_dcat_bwd(
            h if top else dh, target if top else None, xhat, rstd, ln_post_g, w_out[l],
            conv, proj, conv_ln_g, conv_ln_b, pw_l, conv_pw_b, l,
            job=_swap_job(later["grads"]) if later else None)
        if top:
            loss_part = part
        if later:
            later["parts"], later["owns"] = _chip_partials(big_names, later["grads"], recv, dist, later["l"])
        g["w_out", l] = _dwout_bwd(yc, ya, yl, dzb, l)
        g["conv_pw_w", l] = dpw.reshape(N_SHARD, 2, PW_SH // 2, CW)
        if tail:
            own["early"] = dict(l=0, grads=[g[name, 0] for name in early])
        job = None
        if tail:
            job = _join_jobs(_swap_job(own["early"]["grads"]), _scatter_job(later["parts"][1:]))
        (dproj, ddw, g["conv_dw_b", l]), got = _conv_bwd_taps(d_conv, proj, p["conv_dw_w"], dproj, l, job=job)
        if tail:
            n_early = len(early)
            own["early"]["parts"], own["early"]["owns"] = _chip_partials(
                early, own["early"]["grads"], got[:n_early], dist, 0)
            later["z"] = got[n_early:]
        g["conv_dw_w", l] = ddw[:CONV_K]
        (dqr, dk, dv, dk0, dv0, dproj, dsink), z = _attn_bwd(
            qr, kr, vb, proj, att, lse, d_ya, sinks, dproj, l,
            job=_scatter_job(later["parts"][:1]) if later else None)
        if later:
            later["z"] = z + later["z"]
        g["attn_sinks", l] = dsink[0, :N_HEADS]
        dproj = _rope_bwd(dqr, dk, dv, dk0, dv0, tabs, dproj, l)
        (dproj, dlw, g["lru_conv_b", l], dwa, g["lru_ba", l], dwx, g["lru_bx", l], g["lru_lambda", l]), z = _lru_bwd(
            proj, hl, d_yl, p["lru_conv_w"], lru_conv_b, wa_bd, lru_ba, wx_bd, lru_bx, lru_lambda, dproj, l,
            job=_scatter_job(own["early"]["parts"]) if tail else None)
        if tail:
            own["early"]["z"] = z
        g["lru_conv_w", l] = dlw[:LRU_K]
        g["lru_wa", l] = _diag_blocks(dwa)
        g["lru_wx", l] = _diag_blocks(dwx)
        job = None
        if l > 0:
            g["w_in", l] = _dwin_bwd(hb_l, dproj, l)
        else:
            c = dist[0] if dist else jnp.int32(0)
            job = None
            if dist:
                pack_a = _pack_rows([_layer_stack(g, name) for name in _SMALL_LAYERED])
                totals = _shard_totals(big_names, later, dist)
                job = _join_jobs(_share_job(totals), _spread_job(pack_a))
            (give,), got = _dwin_half(hb_l, dproj, 1 - c, l, "give", job=job)
            (keep,), recv = _dwin_half(hb_l, dproj, c, l, "keep", job=_send_job([give]) if dist else None)
            job = None
            if dist:
                _store_reduced(big_names, later["l"], got[:-1], g)
                later = None
                g["pack_layered", -1] = _sum_slots(pack_a, got[-1], dist[3], "layered")
                own["last"] = dict(l=0)
                own["last"]["parts"], own["last"]["owns"] = _chip_partials(
                    last, [keep.reshape(N_SHARD, 1, D // 2, WIN_SH)], recv, (jnp.int32(0),) + tuple(dist[1:]), 0)
                job = _scatter_job(own["last"]["parts"])
            else:
                g["w_in", l] = jnp.stack([keep, give], axis=1)
        (dh,), got = _dh_bwd(dproj, [w_in[l]], dz, l, job=job)
        if tail:
            own["last"]["z"] = got
        if dist and l > 0:
            later = dict(l=l, grads=[g[name, l] for name in big_names])
    grad_x, g["meta_tokens", -1], g["ln_in_g", -1], g["ln_in_b", -1] = _embed_bwd(
        dh, x, p["meta_tokens"], ln_in_g, ln_in_b)
    if dist:
        pack_b = _pack_rows([g[name, -1] for name in _SMALL_EMBED])
        state = dict(l=0, owns=own["last"]["owns"] + own["early"]["owns"], z=own["last"]["z"] + own["early"]["z"])
        totals = _shard_totals(last + early, state, dist)
        got = _run_job(_join_jobs(_share_job(totals), _spread_job(pack_b)), "share_and_spread")
        _store_reduced(last + early, 0, got[:-1], g)
        g["pack_embed", -1] = _sum_slots(pack_b, got[-1], dist[3], "embed")
    return loss_part, grad_x, g


_SMALL_EMBED = ("meta_tokens", "ln_in_g", "ln_in_b")
_SMALL_LAYERED = ("conv_dw_w", "conv_dw_b", "conv_ln_g", "conv_ln_b", "conv_pw_b", "attn_sinks", "lru_conv_w",
                  "lru_conv_b", "lru_wa", "lru_ba", "lru_wx", "lru_bx", "lru_lambda", "ln_post_g", "ln_post_b")


def _layer_stack(g, name):
    return jnp.stack([g[name, l] for l in range(DEPTH)], axis=0)


def _chip_partials(names, grads, recv, dist, l):
    outs = [_chip_partial(a, r, dist[0], dist[1], f"{name}{l}") for name, a, r in zip(names, grads, recv)]
    return [o[0] for o in outs], [o[1] for o in outs]


def _shard_totals(names, state, dist):
    l = state["l"]
    return [_shard_total(po, zz, dist[2], f"{name}{l}") for name, po, zz in zip(names, state["owns"], state["z"])]


def _store_reduced(names, l, full, g):
    for name, f in zip(names, full):
        g[name, l] = f.reshape(2 * f.shape[1], f.shape[2])


MESH = pl.DeviceIdType.MESH
HBM_SPEC = pl.BlockSpec(memory_space=pltpu.HBM)
N_DEV = 8


def _position():
    x, y, c = lax.axis_index("x"), lax.axis_index("y"), lax.axis_index("c")
    return x, y, c


def _other_chips(x, y):
    return [(1 - x, y), (x, 1 - y), (1 - x, 1 - y)]


def _cast_into_slot(a, l, j, tag, piece=0, pieces=1):
    _, R, C = a.shape
    rows = R // pieces
    tb = _pick(rows, (512, 128))
    first = piece * rows // tb

    def body(s_ref, a_ref, o_ref):
        o_ref[...] = a_ref[...].astype(BF16)

    grid_spec = pltpu.PrefetchScalarGridSpec(
        num_scalar_prefetch=1, grid=(rows // tb,),
        in_specs=[pl.BlockSpec((None, tb, C), lambda t, sc: (l, first + t, 0))],
        out_specs=pl.BlockSpec((None, tb, C), lambda t, sc: (sc[0], t, 0)))
    return pl.pallas_call(
        body, name=f"cast_into_slot_{tag}{l}_{piece}", grid_spec=grid_spec,
        out_shape=jax.ShapeDtypeStruct((N_SHARD, rows, C), BF16),
        compiler_params=_cp("arbitrary"),
    )(jnp.reshape(j, (1,)).astype(jnp.int32), a)


class _Job:
    def __init__(self, inputs, aliased, extra_out, sems, start, mid, finish):
        self.inputs, self.extra_out, self.sems = list(inputs), list(extra_out), list(sems)
        self.n_aliased = len(self.inputs) if aliased is True else int(aliased)
        self.start, self.mid, self.finish = start, mid, finish

    def out_shapes(self):
        return [jax.ShapeDtypeStruct(a.shape, a.dtype) for a in self.inputs[:self.n_aliased]] + self.extra_out


def _side_call(body, job, *, name, grid, in_specs, out_specs, out_shape, scratch_shapes, semantics, args,
               aliases=None, prefetch=()):
    aliases = dict(aliases or {})
    n_pre = len(prefetch)

    def call(fn, ins, outs, shapes, scratch, sem, operands):
        if n_pre:
            spec = pltpu.PrefetchScalarGridSpec(num_scalar_prefetch=n_pre, grid=grid, in_specs=ins, out_specs=outs,
                                                scratch_shapes=scratch)
            return pl.pallas_call(fn, name=name, grid_spec=spec, out_shape=shapes,
                                  input_output_aliases={k + n_pre: v for k, v in aliases.items()},
                                  compiler_params=_cp(*sem))(*prefetch, *operands)
        return pl.pallas_call(fn, name=name, grid=grid, in_specs=ins, out_specs=outs, out_shape=shapes,
                              scratch_shapes=scratch, input_output_aliases=aliases,
                              compiler_params=_cp(*sem))(*operands)

    if job is None:
        return list(call(body, list(in_specs), list(out_specs), list(out_shape), list(scratch_shapes),
                         semantics, args)), []
    n_in, n_out, n_scr = len(in_specs), len(out_specs), len(scratch_shapes)
    j_in, j_out = len(job.inputs), len(job.out_shapes())
    steps = 1
    for gsize in grid:
        steps *= gsize

    def wrapped(*refs):
        pre, refs = refs[:n_pre], refs[n_pre:]
        host_in, job_in = refs[:n_in], refs[n_in:n_in + j_in]
        o0 = n_in + j_in
        host_out, job_out = refs[o0:o0 + n_out], refs[o0 + n_out:o0 + n_out + j_out]
        s0 = o0 + n_out + j_out
        host_scr, sems = refs[s0:s0 + n_scr], refs[s0 + n_scr:]
        step = pl.program_id(0)
        for d in range(1, len(grid)):
            step = step * grid[d] + pl.program_id(d)

        @pl.when(step == 0)
        def _():
            job.start(job_in, job_out, sems)

        @pl.when(step == max(steps - 2, 0))
        def _():
            job.mid(job_in, job_out, sems)

        body(*pre, *host_in, *host_out, *host_scr)

        @pl.when(step == steps - 1)
        def _():
            job.finish(job_in, job_out, sems)

    aliases.update({n_in + k: n_out + k for k in range(job.n_aliased)})
    outs = call(wrapped, list(in_specs) + [HBM_SPEC] * j_in, list(out_specs) + [HBM_SPEC] * j_out,
                list(out_shape) + job.out_shapes(), list(scratch_shapes) + job.sems,
                ["arbitrary"] * len(grid), [*args, *job.inputs])
    return list(outs[:n_out]), list(outs[n_out:])


def _run_job(job, name):
    return _side_call(lambda: None, job, name=name, grid=(1,), in_specs=[], out_specs=[], out_shape=[],
                      scratch_shapes=[], semantics=("arbitrary",), args=[])[1]


def _gather_job(slots, peers=(0, 1, 2)):
    n = len(slots)

    def copies(buf, sems):
        ici_send, ici_recv, d2d_send, d2d_recv = sems
        x, y, c = _position()
        chips = _other_chips(x, y)

        def half(k, slot, which):
            hr = buf[k].shape[1] // 2
            return buf[k].at[slot, pl.ds(pl.multiple_of(which * hr, hr), hr)]

        def over_ici(k, p, slot):
            px, py = chips[p]
            return pltpu.make_async_remote_copy(
                src_ref=half(k, slot, c), dst_ref=half(k, slot, c),
                send_sem=ici_send.at[k * 3 + p], recv_sem=ici_recv.at[k * 3 + p],
                device_id=(px, py, c), device_id_type=MESH)

        def over_d2d(k, p, which):
            px, py = chips[p]
            return pltpu.make_async_remote_copy(
                src_ref=half(k, 2 * px + py, which), dst_ref=half(k, 2 * px + py, which),
                send_sem=d2d_send.at[k * 3 + p], recv_sem=d2d_recv.at[k * 3 + p],
                device_id=(x, y, 1 - c), device_id_type=MESH)

        return over_ici, over_d2d, 2 * x + y, chips, c

    pairs = [(k, p) for k in range(n) for p in peers]

    def start(_, buf, sems):
        over_ici, _, mine, _, _ = copies(buf, sems)
        for k, p in pairs:
            over_ici(k, p, mine).start()

    def mid(_, buf, sems):
        over_ici, over_d2d, _, chips, c = copies(buf, sems)
        for k, p in pairs:
            px, py = chips[p]
            over_ici(k, p, 2 * px + py).wait_recv()
            over_d2d(k, p, c).start()

    def finish(_, buf, sems):
        over_ici, over_d2d, mine, _, c = copies(buf, sems)
        for k, p in pairs:
            over_d2d(k, p, 1 - c).wait_recv()
        for k, p in pairs:
            over_ici(k, p, mine).wait_send()
            over_d2d(k, p, c).wait_send()

    return _Job(slots, True, [], [pltpu.SemaphoreType.DMA((3 * n,))] * 4, start, mid, finish)


def _gather_shards(shards):
    n = len(shards)

    def body(*refs):
        src, dst = refs[:n], refs[n:2 * n]
        send_sems, recv_sems, local_sems = refs[2 * n:]
        x, y, c = _position()
        mine = 2 * x + y
        chips = _other_chips(x, y)

        def copy(k, p):
            return pltpu.make_async_remote_copy(
                src_ref=src[k], dst_ref=dst[k].at[mine],
                send_sem=send_sems.at[k * 3 + p], recv_sem=recv_sems.at[k * 3 + p],
                device_id=(*chips[p], c), device_id_type=MESH)

        def arrival(k, p):
            px, py = chips[p]
            return pltpu.make_async_remote_copy(
                src_ref=src[k], dst_ref=dst[k].at[2 * px + py],
                send_sem=send_sems.at[k * 3 + p], recv_sem=recv_sems.at[k * 3 + p],
                device_id=(px, py, c), device_id_type=MESH)

        local = [pltpu.make_async_copy(src[k], dst[k].at[mine], local_sems.at[k]) for k in range(n)]
        for cp in local:
            cp.start()
        for k in range(n):
            for p in range(3):
                copy(k, p).start()
        for k in range(n):
            for p in range(3):
                arrival(k, p).wait_recv()
        for k in range(n):
            for p in range(3):
                copy(k, p).wait_send()
        for cp in local:
            cp.wait()

    return pl.pallas_call(
        body, name="gather_shards",
        in_specs=[HBM_SPEC] * n, out_specs=[HBM_SPEC] * n,
        out_shape=[jax.ShapeDtypeStruct((N_SHARD,) + s.shape, s.dtype) for s in shards],
        scratch_shapes=[pltpu.SemaphoreType.DMA((3 * n,)), pltpu.SemaphoreType.DMA((3 * n,)),
                        pltpu.SemaphoreType.DMA((n,))],
    )(*shards)


def _swap_job(grads):
    n = len(grads)

    def copies(src, dst, sems):
        x, y, c = _position()
        return [pltpu.make_async_remote_copy(
            src_ref=src[k].at[:, 1 - c], dst_ref=dst[k],
            send_sem=sems[0].at[k], recv_sem=sems[1].at[k],
            device_id=(x, y, 1 - c), device_id_type=MESH) for k in range(n)]

    def start(src, dst, sems):
        for cp in copies(src, dst, sems):
            cp.start()

    def finish(src, dst, sems):
        for cp in copies(src, dst, sems):
            cp.wait()

    return _Job(grads, False, [jax.ShapeDtypeStruct((N_SHARD,) + g.shape[2:], F32) for g in grads],
                [pltpu.SemaphoreType.DMA((n,))] * 2, start, lambda *_: None, finish)


def _send_job(arrays):
    n = len(arrays)

    def copies(src, dst, sems):
        x, y, c = _position()
        return [pltpu.make_async_remote_copy(
            src_ref=src[k], dst_ref=dst[k], send_sem=sems[0].at[k], recv_sem=sems[1].at[k],
            device_id=(x, y, 1 - c), device_id_type=MESH) for k in range(n)]

    def start(src, dst, sems):
        for cp in copies(src, dst, sems):
            cp.start()

    def finish(src, dst, sems):
        for cp in copies(src, dst, sems):
            cp.wait()

    return _Job(arrays, False, [jax.ShapeDtypeStruct(a.shape, a.dtype) for a in arrays],
                [pltpu.SemaphoreType.DMA((n,))] * 2, start, lambda *_: None, finish)


def _chip_partial(a, y, c, j, tag):
    _, _, R, C = a.shape
    tr = _pick(R, (256, 64))

    def body(s_ref, a_ref, y_ref, pb_ref, po_ref):
        total = a_ref[...] + y_ref[...]
        pb_ref[...] = total.astype(BF16)

        @pl.when(pl.program_id(1) == s_ref[1])
        def _():
            po_ref[...] = total

    grid_spec = pltpu.PrefetchScalarGridSpec(
        num_scalar_prefetch=1, grid=(R // tr, N_SHARD),
        in_specs=[pl.BlockSpec((None, None, tr, C), lambda t, s, sc: (s, sc[0], t, 0)),
                  pl.BlockSpec((None, tr, C), lambda t, s, sc: (s, t, 0))],
        out_specs=[pl.BlockSpec((None, tr, C), lambda t, s, sc: (s, t, 0)),
                   pl.BlockSpec((tr, C), lambda t, s, sc: (t, 0))])
    return pl.pallas_call(
        body, name=f"chip_partial_{tag}", grid_spec=grid_spec,
        out_shape=[jax.ShapeDtypeStruct((N_SHARD, R, C), BF16), jax.ShapeDtypeStruct((R, C), F32)],
        compiler_params=_cp("arbitrary", "arbitrary"),
    )(jnp.stack([c, j]).astype(jnp.int32), a, y)


def _scatter_job(parts):
    n = len(parts)
    pairs = [(k, p) for k in range(n) for p in range(3)]

    def copy(src, dst, sems, k, p, outgoing):
        x, y, c = _position()
        mine = 2 * x + y
        px, py = _other_chips(x, y)[p]
        theirs = 2 * px + py
        return pltpu.make_async_remote_copy(
            src_ref=src[k].at[theirs if outgoing else mine], dst_ref=dst[k].at[mine if outgoing else theirs],
            send_sem=sems[0].at[k * 3 + p], recv_sem=sems[1].at[k * 3 + p],
            device_id=(px, py, c), device_id_type=MESH)

    def start(src, dst, sems):
        for k, p in pairs:
            copy(src, dst, sems, k, p, True).start()

    def finish(src, dst, sems):
        for k, p in pairs:
            copy(src, dst, sems, k, p, False).wait_recv()
        for k, p in pairs:
            copy(src, dst, sems, k, p, True).wait_send()

    return _Job(parts, False, [jax.ShapeDtypeStruct(pb.shape, BF16) for pb in parts],
                [pltpu.SemaphoreType.DMA((3 * n,))] * 2, start, lambda *_: None, finish)


def _shard_total(own, z, others_c, tag):
    R, C = own.shape
    tr = _pick(R, (256, 64))

    def body(s_ref, o_ref, z0_ref, z1_ref, z2_ref, h_ref):
        h_ref[...] = ((o_ref[...] + z0_ref[...].astype(F32)) + z1_ref[...].astype(F32)) + z2_ref[...].astype(F32)

    zspec = lambda q: pl.BlockSpec((None, tr, C), lambda t, sc: (sc[q], t, 0))
    grid_spec = pltpu.PrefetchScalarGridSpec(
        num_scalar_prefetch=1, grid=(R // tr,),
        in_specs=[pl.BlockSpec((tr, C), lambda t, sc: (t, 0)), zspec(0), zspec(1), zspec(2)],
        out_specs=pl.BlockSpec((None, tr, C), lambda t, sc: (sc[3], t, 0)))
    return pl.pallas_call(
        body, name=f"shard_total_{tag}", grid_spec=grid_spec,
        out_shape=jax.ShapeDtypeStruct((2, R, C), F32),
        compiler_params=_cp("arbitrary"),
    )(others_c, own, z, z, z)


def _share_job(totals):
    n = len(totals)

    def copy(buf, sems, k, which):
        x, y, c = _position()
        return pltpu.make_async_remote_copy(
            src_ref=buf[k].at[which], dst_ref=buf[k].at[which],
            send_sem=sems[0].at[k], recv_sem=sems[1].at[k],
            device_id=(x, y, 1 - c), device_id_type=MESH)

    def start(_, buf, sems):
        c = lax.axis_index("c")
        for k in range(n):
            copy(buf, sems, k, c).start()

    def finish(_, buf, sems):
        c = lax.axis_index("c")
        for k in range(n):
            copy(buf, sems, k, 1 - c).wait_recv()
        for k in range(n):
            copy(buf, sems, k, c).wait_send()

    return _Job(totals, True, [], [pltpu.SemaphoreType.DMA((n,))] * 2, start, lambda *_: None, finish)


def _spread_job(pack):
    def copy(src, dst, sems, m, outgoing):
        x, y, c = _position()
        peer = (x ^ (m >> 2), y ^ ((m >> 1) & 1), c ^ (m & 1))
        slot = 4 * x + 2 * y + c if outgoing else 4 * peer[0] + 2 * peer[1] + peer[2]
        return pltpu.make_async_remote_copy(
            src_ref=src[0], dst_ref=dst[0].at[slot], send_sem=sems[0].at[m - 1], recv_sem=sems[1].at[m - 1],
            device_id=peer, device_id_type=MESH)

    def start(src, dst, sems):
        for m in range(1, N_DEV):
            copy(src, dst, sems, m, True).start()

    def finish(src, dst, sems):
        for m in range(1, N_DEV):
            copy(src, dst, sems, m, False).wait_recv()
        for m in range(1, N_DEV):
            copy(src, dst, sems, m, True).wait_send()

    return _Job([pack], False, [jax.ShapeDtypeStruct((N_DEV,) + pack.shape, F32)],
                [pltpu.SemaphoreType.DMA((N_DEV - 1,))] * 2, start, lambda *_: None, finish)


def _join_jobs(a, b):
    for job in (a, b):
        assert job.n_aliased in (0, len(job.inputs)) and not (job.n_aliased and job.extra_out)
    assert a.n_aliased or not b.n_aliased
    n_in, n_out, n_sem = len(a.inputs), len(a.out_shapes()), len(a.sems)

    def phase(name):
        def run(ins, outs, sems):
            getattr(a, name)(ins[:n_in], outs[:n_out], sems[:n_sem])
            getattr(b, name)(ins[n_in:], outs[n_out:], sems[n_sem:])
        return run

    return _Job(a.inputs + b.inputs, a.n_aliased + b.n_aliased, a.extra_out + b.extra_out, a.sems + b.sems,
                phase("start"), phase("mid"), phase("finish"))


def _sum_slots(pack, slots, me, tag):
    def body(me_ref, p_ref, s_ref, o_ref):
        acc = None
        for d in range(N_DEV):
            term = jnp.where(me_ref[0] == d, p_ref[...], s_ref[d])
            acc = term if acc is None else acc + term
        o_ref[...] = acc

    vm = pl.BlockSpec(memory_space=pltpu.VMEM)
    return pl.pallas_call(
        body, name=f"sum_slots_{tag}",
        in_specs=[pl.BlockSpec(memory_space=pltpu.SMEM), vm, vm], out_specs=vm,
        out_shape=jax.ShapeDtypeStruct(pack.shape, F32),
        compiler_params=pltpu.CompilerParams(vmem_limit_bytes=V7X_VMEM_LIMIT),
    )(jnp.reshape(me, (1,)).astype(jnp.int32), pack, slots)


def _pack_rows(arrays):
    total = sum(a.size for a in arrays)
    rows = -(-total // 128)
    rows = -(-rows // PACK_ROWS_ALIGN) * PACK_ROWS_ALIGN
    flat = [a.reshape(-1) for a in arrays] + [jnp.zeros((rows * 128 - total,), F32)]
    return jnp.concatenate(flat).reshape(rows, 128)


def _adamw_math(w, g, m, v):
    m = ADAM_B1 * m + (1.0 - ADAM_B1) * g
    v = ADAM_B2 * v + (1.0 - ADAM_B2) * (g * g)
    m_hat = m / (1.0 - ADAM_B1 ** ADAM_STEP)
    v_hat = v / (1.0 - ADAM_B2 ** ADAM_STEP)
    delta = -ADAM_LR * (m_hat / (jnp.sqrt(v_hat) + ADAM_EPS) + ADAM_WD * w)
    return delta, m, v


def _adamw_big(w, g0, g1, m, v, tag):
    _, R, C = w.shape
    tr = _pick(R, (256, 128))

    def body(w_ref, g0_ref, g1_ref, m_ref, v_ref, go_ref, d_ref, mo_ref, vo_ref):
        g = jnp.where(pl.program_id(0) == 0, g0_ref[...], g1_ref[...])
        delta, mn, vn = _adamw_math(w_ref[...], g, m_ref[...], v_ref[...])
        go_ref[...] = g
        d_ref[...] = delta
        mo_ref[...] = mn
        vo_ref[...] = vn

    s3 = pl.BlockSpec((None, tr, C), lambda l, t: (l, t, 0))
    g_spec = lambda layer: pl.BlockSpec((tr, C), lambda l, t: (jnp.where(l == layer, t, 0), 0))
    shp = jax.ShapeDtypeStruct(w.shape, F32)
    return pl.pallas_call(
        body, name=f"adamw_{tag}", grid=(2, R // tr),
        in_specs=[s3, g_spec(0), g_spec(1), s3, s3], out_specs=[s3, s3, s3, s3],
        out_shape=[shp, shp, shp, shp],
        compiler_params=_cp("parallel", "parallel"),
    )(w, g0, g1, m, v)


def _adamw_small(ws, gs, ms, vs):
    n = len(ws)

    def body(*refs):
        w_r, g_r, m_r, v_r = refs[:n], refs[n:2 * n], refs[2 * n:3 * n], refs[3 * n:4 * n]
        d_o, m_o, v_o = refs[4 * n:5 * n], refs[5 * n:6 * n], refs[6 * n:7 * n]
        for k in range(n):
            delta, mn, vn = _adamw_math(w_r[k][...], g_r[k][...], m_r[k][...], v_r[k][...])
            d_o[k][...] = delta
            m_o[k][...] = mn
            v_o[k][...] = vn

    vm = pl.BlockSpec(memory_space=pltpu.VMEM)
    shapes = [jax.ShapeDtypeStruct(w.shape, F32) for w in ws]
    outs = pl.pallas_call(
        body, name="adamw_small",
        in_specs=[vm] * (4 * n), out_specs=[vm] * (3 * n),
        out_shape=shapes * 3,
    )(*ws, *gs, *ms, *vs)
    return outs[:n], outs[n:2 * n], outs[2 * n:]


_WEIGHTS = ["meta_tokens", "ln_in_g", "ln_in_b", "w_in", "conv_dw_w", "conv_dw_b", "conv_ln_g", "conv_ln_b",
            "conv_pw_w", "conv_pw_b", "attn_sinks", "lru_conv_w", "lru_conv_b", "lru_wa", "lru_ba", "lru_wx",
            "lru_bx", "lru_lambda", "w_out", "ln_post_g", "ln_post_b"]
_BIG = ("w_in", "w_out", "conv_pw_w")
_SMALL_SHARDED = {"meta_tokens": 1, "conv_dw_w": 2, "lru_conv_w": 2}
PACK_ROWS_ALIGN = 8


def _as2d(a):
    return a.reshape(1, -1) if a.ndim == 1 else a.reshape(-1, a.shape[-1])


def kernel(x, meta_tokens, ln_in_g, ln_in_b, w_in, conv_dw_w, conv_dw_b, conv_ln_g, conv_ln_b, conv_pw_w, conv_pw_b, attn_sinks, lru_conv_w, lru_conv_b, lru_wa, lru_ba, lru_wx, lru_bx, lru_lambda, w_out, ln_post_g, ln_post_b, loss_target, m_meta_tokens, m_ln_in_g, m_ln_in_b, m_w_in, m_conv_dw_w, m_conv_dw_b, m_conv_ln_g, m_conv_ln_b, m_conv_pw_w, m_conv_pw_b, m_attn_sinks, m_lru_conv_w, m_lru_conv_b, m_lru_wa, m_lru_ba, m_lru_wx, m_lru_bx, m_lru_lambda, m_w_out, m_ln_post_g, m_ln_post_b, v_meta_tokens, v_ln_in_g, v_ln_in_b, v_w_in, v_conv_dw_w, v_conv_dw_b, v_conv_ln_g, v_conv_ln_b, v_conv_pw_w, v_conv_pw_b, v_attn_sinks, v_lru_conv_w, v_lru_conv_b, v_lru_wa, v_lru_ba, v_lru_wx, v_lru_bx, v_lru_lambda, v_w_out, v_ln_post_g, v_ln_post_b):
    w = dict(meta_tokens=meta_tokens, ln_in_g=ln_in_g, ln_in_b=ln_in_b, w_in=w_in, conv_dw_w=conv_dw_w,
             conv_dw_b=conv_dw_b, conv_ln_g=conv_ln_g, conv_ln_b=conv_ln_b, conv_pw_w=conv_pw_w,
             conv_pw_b=conv_pw_b, attn_sinks=attn_sinks, lru_conv_w=lru_conv_w, lru_conv_b=lru_conv_b,
             lru_wa=lru_wa, lru_ba=lru_ba, lru_wx=lru_wx, lru_bx=lru_bx, lru_lambda=lru_lambda, w_out=w_out,
             ln_post_g=ln_post_g, ln_post_b=ln_post_b)
    mom_m = dict(zip(_WEIGHTS, (m_meta_tokens, m_ln_in_g, m_ln_in_b, m_w_in, m_conv_dw_w, m_conv_dw_b, m_conv_ln_g,
                                m_conv_ln_b, m_conv_pw_w, m_conv_pw_b, m_attn_sinks, m_lru_conv_w, m_lru_conv_b,
                                m_lru_wa, m_lru_ba, m_lru_wx, m_lru_bx, m_lru_lambda, m_w_out, m_ln_post_g,
                                m_ln_post_b)))
    mom_v = dict(zip(_WEIGHTS, (v_meta_tokens, v_ln_in_g, v_ln_in_b, v_w_in, v_conv_dw_w, v_conv_dw_b, v_conv_ln_g,
                                v_conv_ln_b, v_conv_pw_w, v_conv_pw_b, v_attn_sinks, v_lru_conv_w, v_lru_conv_b,
                                v_lru_wa, v_lru_ba, v_lru_wx, v_lru_bx, v_lru_lambda, v_w_out, v_ln_post_g,
                                v_ln_post_b)))
    xi, yi, ci = _position()
    j = 2 * xi + yi

    g_meta, g_dw, g_lc = _gather_shards([meta_tokens, conv_dw_w, lru_conv_w])
    p = dict(w)
    p["w_in"] = [_cast_into_slot(w_in, l, j, "w_in") for l in range(DEPTH)]
    p["w_out"] = [_cast_into_slot(w_out, l, j, "w_out") for l in range(DEPTH)]
    p["conv_pw_w"] = [_cast_into_slot(conv_pw_w, l, j, "conv_pw_w") for l in range(DEPTH)]
    p["meta_tokens"] = g_meta.transpose(1, 0, 2).reshape(N_META, D)
    p["conv_dw_w"] = g_dw.transpose(1, 2, 0, 3).reshape(DEPTH, CONV_K, CW)
    p["lru_conv_w"] = g_lc.transpose(1, 2, 0, 3).reshape(DEPTH, LRU_K, LW)

    others = jnp.stack([jnp.where(j <= 0, 1, 0), jnp.where(j <= 1, 2, 1), jnp.where(j <= 2, 3, 2), ci]).astype(jnp.int32)
    me = 4 * xi + 2 * yi + ci
    order = jnp.stack([j, 2 * (1 - xi) + yi, 2 * xi + (1 - yi), 2 * (1 - xi) + (1 - yi)]).astype(jnp.int32)
    loss_part, grad_x, g = _device_step(x[0], loss_target[0], p, dist=(ci, j, others, me, order))
    loss = lax.psum(jnp.sum(loss_part), ("x", "y", "c"))
    big = {(name, l): g[name, l] for name in _BIG for l in range(DEPTH)}

    small_names = [n for n in _WEIGHTS if n not in _BIG]
    small_g = {}
    for names, red in ((_SMALL_LAYERED, g["pack_layered", -1]), (_SMALL_EMBED, g["pack_embed", -1])):
        red = red.reshape(-1)
        off = 0
        for n in names:
            fshape = list(w[n].shape)
            if n in _SMALL_SHARDED:
                fshape[_SMALL_SHARDED[n]] *= N_SHARD
            sz = 1
            for dim in fshape:
                sz *= dim
            full = red[off:off + sz].reshape(fshape)
            off += sz
            if n in _SMALL_SHARDED:
                ax = _SMALL_SHARDED[n]
                full = lax.dynamic_slice_in_dim(full, j * w[n].shape[ax], w[n].shape[ax], axis=ax)
            small_g[n] = full

    out_g, out_d, out_m, out_v = {}, {}, {}, {}
    for name in _BIG:
        shp = w[name].shape
        to3 = lambda a: a.reshape(DEPTH, -1, shp[-1])
        go, do, mo, vo = _adamw_big(to3(w[name]), big[name, 0], big[name, 1], to3(mom_m[name]), to3(mom_v[name]), name)
        out_g[name], out_d[name], out_m[name], out_v[name] = (a.reshape(shp) for a in (go, do, mo, vo))
    ds, ms, vs = _adamw_small([_as2d(w[n]) for n in small_names], [_as2d(small_g[n]) for n in small_names],
                              [_as2d(mom_m[n]) for n in small_names], [_as2d(mom_v[n]) for n in small_names])
    for n, d_, m_, v_ in zip(small_names, ds, ms, vs):
        out_g[n] = small_g[n]
        out_d[n], out_m[n], out_v[n] = d_.reshape(w[n].shape), m_.reshape(w[n].shape), v_.reshape(w[n].shape)

    return (loss, grad_x[None], *[out_g[n] for n in _WEIGHTS], *[out_d[n] for n in _WEIGHTS],
            *[out_m[n] for n in _WEIGHTS], *[out_v[n] for n in _WEIGHTS])
```

```python
import functools

import jax
import jax.numpy as jnp
from jax import lax
from jax.experimental import pallas as pl
from jax.experimental.pallas import tpu as pltpu

F32 = jnp.float32
BF16 = jnp.bfloat16

D = 2048
N_META = 16
CW = 512
CONV_K = 31
AW = 1024
KVW = 256
N_HEADS = 16
LW = 512
LRU_K = 4
LRU_C = 8.0
IN_TOTAL = 5120
ROT_HALF = 8
ROPE_THETA = 500000.0
LN_EPS = 1e-5
DEPTH = 2
ALPHA = (2.0 * DEPTH) ** 0.25
NEG_INF = -1e30
ADAM_LR, ADAM_B1, ADAM_B2, ADAM_EPS, ADAM_WD, ADAM_STEP = 0.001, 0.9, 0.999, 1e-08, 0.01, 10

BLK = 128
PAD = BLK - N_META
N_SHARD = 4
WIN_SH = IN_TOTAL // N_SHARD
WOUT_SH = D // N_SHARD
PW_SH = CW // N_SHARD
HALO = 32
LHALO = 8
V7X_VMEM_LIMIT = 60 * 1024 * 1024


def _cp(*sem):
    return pltpu.CompilerParams(dimension_semantics=sem if sem else None, vmem_limit_bytes=V7X_VMEM_LIMIT)


def _pick(total, prefs):
    for p in prefs:
        if total % p == 0:
            return p
    raise ValueError(f"no tile for {total}")


def _dot(a, b):
    return jnp.dot(a, b, preferred_element_type=F32)


def _dot_nt(a, b):
    return lax.dot_general(a, b, (((1,), (1,)), ((), ())), preferred_element_type=F32)


def _dot_tn(a, b):
    return lax.dot_general(a, b, (((0,), (0,)), ((), ())), preferred_element_type=F32)


def _sigmoid(x):
    return 1.0 / (1.0 + jnp.exp(-x))


def _silu_and_grad(x):
    s = _sigmoid(x)
    return x * s, s * (1.0 + x * (1.0 - s))


def _ln_rows(x, g, b):
    mu = jnp.mean(x, axis=-1, keepdims=True)
    xc = x - mu
    var = jnp.mean(xc * xc, axis=-1, keepdims=True)
    rstd = lax.rsqrt(var + LN_EPS)
    xhat = xc * rstd
    return xhat * g + b, xhat, rstd


def _ln_bwd_rows(dy, xhat, rstd, g):
    dxh = dy * g
    m1 = jnp.mean(dxh, axis=-1, keepdims=True)
    m2 = jnp.mean(dxh * xhat, axis=-1, keepdims=True)
    return rstd * (dxh - m1 - xhat * m2)


def _row_ids(n, base):
    return base + lax.broadcasted_iota(jnp.int32, (n, 1), 0)


def _colsum(x):
    return jnp.sum(x, axis=0, keepdims=True)


def _embed_fwd(x, meta, g, b, job=None):
    S = x.shape[0]
    nb = S // BLK + 1

    def body(x_ref, meta_ref, g_ref, b_ref, h_ref, hb_ref):
        n = pl.program_id(0)

        @pl.when(n == 0)
        def _():
            y, _, _ = _ln_rows(meta_ref[...], g_ref[...], b_ref[...])
            h_ref[...] = jnp.zeros_like(h_ref)
            h_ref[PAD:BLK, :] = y

        @pl.when(n > 0)
        def _():
            y, _, _ = _ln_rows(x_ref[...], g_ref[...], b_ref[...])
            h_ref[...] = y

        hb_ref[...] = h_ref[...].astype(BF16)

    return _side_call(
        body, job, name="embed_fwd", grid=(nb,),
        in_specs=[pl.BlockSpec((BLK, D), lambda n: (jnp.maximum(n - 1, 0), 0)),
                  pl.BlockSpec((N_META, D), lambda n: (0, 0)),
                  pl.BlockSpec((1, D), lambda n: (0, 0)),
                  pl.BlockSpec((1, D), lambda n: (0, 0))],
        out_specs=[pl.BlockSpec((BLK, D), lambda n: (n, 0)),
                   pl.BlockSpec((BLK, D), lambda n: (n, 0))],
        out_shape=[jax.ShapeDtypeStruct((nb * BLK, D), F32), jax.ShapeDtypeStruct((nb * BLK, D), BF16)],
        scratch_shapes=[], semantics=("arbitrary",), args=[x, meta, g, b])


def _embed_bwd(dh, x, meta, g, b):
    S = x.shape[0]
    nb = S // BLK + 1

    def body(dh_ref, x_ref, meta_ref, g_ref, b_ref, gx_ref, gm_ref, dg_ref, db_ref):
        n = pl.program_id(0)

        @pl.when(n == 0)
        def _():
            _, xhat, rstd = _ln_rows(meta_ref[...], g_ref[...], b_ref[...])
            dy = dh_ref[PAD:BLK, :]
            gm_ref[...] = _ln_bwd_rows(dy, xhat, rstd, g_ref[...])
            dg_ref[...] = _colsum(dy * xhat)
            db_ref[...] = _colsum(dy)

        @pl.when(n > 0)
        def _():
            _, xhat, rstd = _ln_rows(x_ref[...], g_ref[...], b_ref[...])
            dy = dh_ref[...]
            gx_ref[...] = _ln_bwd_rows(dy, xhat, rstd, g_ref[...])
            dg_ref[...] += _colsum(dy * xhat)
            db_ref[...] += _colsum(dy)

    prev = lambda n: (jnp.maximum(n - 1, 0), 0)
    const = lambda n: (0, 0)
    return pl.pallas_call(
        body, name="embed_bwd", grid=(nb,),
        in_specs=[pl.BlockSpec((BLK, D), lambda n: (n, 0)),
                  pl.BlockSpec((BLK, D), prev),
                  pl.BlockSpec((N_META, D), const),
                  pl.BlockSpec((1, D), const),
                  pl.BlockSpec((1, D), const)],
        out_specs=[pl.BlockSpec((BLK, D), prev),
                   pl.BlockSpec((N_META, D), const),
                   pl.BlockSpec((1, D), const),
                   pl.BlockSpec((1, D), const)],
        out_shape=[jax.ShapeDtypeStruct((S, D), F32), jax.ShapeDtypeStruct((N_META, D), F32),
                   jax.ShapeDtypeStruct((1, D), F32), jax.ShapeDtypeStruct((1, D), F32)],
        compiler_params=_cp("arbitrary"),
    )(dh, x, meta, g, b)


def _proj_fwd(hb, w_in, order, first, count, prev, l, job=None):
    T = hb.shape[0]
    tm = _pick(T, (1056, 384, 128))

    def body(o_sc, a_ref, w_ref, *rest):
        rest[-1][...] = _dot(a_ref[...], w_ref[...])

    return _side_call(
        body, job, name=f"proj_fwd{l}_{first}", grid=(T // tm, count),
        in_specs=[pl.BlockSpec((tm, D), lambda i, j, o: (i, 0)),
                  pl.BlockSpec((None, D, WIN_SH), lambda i, j, o: (o[first + j], 0, 0))]
        + ([] if prev is None else [pl.BlockSpec(memory_space=pl.ANY)]),
        out_specs=[pl.BlockSpec((tm, WIN_SH), lambda i, j, o: (i, o[first + j]))],
        out_shape=[jax.ShapeDtypeStruct((T, IN_TOTAL), F32)],
        scratch_shapes=[], semantics=("parallel", "arbitrary"),
        args=[hb, w_in] + ([] if prev is None else [prev]),
        aliases=None if prev is None else {2: 0}, prefetch=[order])


def _out_fwd(yc, ya, yl, w_out, h, g, b, l, job=None):
    T = h.shape[0]
    tm = _pick(T, (384, 128))

    def body(yc_ref, ya_ref, yl_ref, w_ref, h_ref, g_ref, b_ref, hn_ref, hnb_ref, xh_ref, rs_ref):
        acc = _dot(yc_ref[...], w_ref[0])
        acc += _dot(ya_ref[:, 0:WOUT_SH], w_ref[1])
        acc += _dot(ya_ref[:, WOUT_SH:2 * WOUT_SH], w_ref[2])
        acc += _dot(yl_ref[...], w_ref[3])
        z = ALPHA * h_ref[...] + acc
        y, xhat, rstd = _ln_rows(z, g_ref[...], b_ref[...])
        hn_ref[...] = y
        hnb_ref[...] = y.astype(BF16)
        xh_ref[...] = xhat
        rs_ref[...] = rstd

    row = lambda i: (i, 0)
    return _side_call(
        body, job, name=f"out_fwd{l}", grid=(T // tm,),
        in_specs=[pl.BlockSpec((tm, CW), row), pl.BlockSpec((tm, AW), row), pl.BlockSpec((tm, LW), row),
                  pl.BlockSpec((N_SHARD, WOUT_SH, D), lambda i: (0, 0, 0)),
                  pl.BlockSpec((tm, D), row),
                  pl.BlockSpec((None, 1, D), lambda i: (l, 0, 0)),
                  pl.BlockSpec((None, 1, D), lambda i: (l, 0, 0))],
        out_specs=[pl.BlockSpec((tm, D), row), pl.BlockSpec((tm, D), row), pl.BlockSpec((tm, D), row),
                   pl.BlockSpec((tm, 1), row)],
        out_shape=[jax.ShapeDtypeStruct((T, D), F32), jax.ShapeDtypeStruct((T, D), BF16),
                   jax.ShapeDtypeStruct((T, D), F32), jax.ShapeDtypeStruct((T, 1), F32)],
        scratch_shapes=[], semantics=("parallel",), args=[yc, ya, yl, w_out, h, g, b])


def _post_ln_dcat_bwd(src, target, xhat, rstd, g, w_out, conv, proj, cln_g, cln_b, pw_w, pw_b, l, job=None):
    T = src.shape[0]
    tm = _pick(T, (384, 128))
    per = tm // BLK if target is not None else 0
    last_blk = target.shape[0] // BLK - 1 if target is not None else 0

    def body(s_ref, *refs):
        t_refs = refs[:per]
        (xh_ref, rs_ref, g_ref, w_ref, conv_ref, ct_ref, cg_ref, cb_ref, pw_ref, pb_ref,
         part_ref, dz_ref, dzb_ref, dg_ref, db_ref, da_ref, dl_ref,
         dconv_ref, dct_ref, dpw_ref, dpb_ref, dcg_ref, dcb_ref) = refs[per:]
        i = pl.program_id(0)

        @pl.when(i == 0)
        def _():
            for ref in (part_ref, dg_ref, db_ref, dpw_ref, dpb_ref, dcg_ref, dcb_ref):
                ref[...] = jnp.zeros_like(ref)

        if per:
            tgt = jnp.concatenate([r[...] for r in t_refs], axis=0) if per > 1 else t_refs[0][...]
            real = _row_ids(tm, i * tm) >= BLK
            err = jnp.where(real, s_ref[...] - tgt, 0.0)
            part_ref[...] += _colsum(err * err) * (0.5 / D)
            dy = err * (1.0 / D)
        else:
            dy = s_ref[...]
        xhat = xh_ref[...]
        dz = _ln_bwd_rows(dy, xhat, rs_ref[...], g_ref[...])
        dzb = dz.astype(BF16)
        dz_ref[...] = dz
        dzb_ref[...] = dzb
        dg_ref[...] += _colsum(dy * xhat)
        db_ref[...] += _colsum(dy)
        da_ref[:, 0:WOUT_SH] = _dot_nt(dzb, w_ref[1])
        da_ref[:, WOUT_SH:2 * WOUT_SH] = _dot_nt(dzb, w_ref[2])
        dl_ref[...] = _dot_nt(dzb, w_ref[3])

        d_yc = _dot_nt(dzb, w_ref[0])
        u, chat, crstd = _ln_rows(conv_ref[...], cg_ref[...], cb_ref[...])
        s, ds_du = _silu_and_grad(u)
        sb = s.astype(BF16)
        cpw = _dot(sb, pw_ref[...]) + pb_ref[...]
        gate, dgate = _silu_and_grad(ct_ref[...])
        d_cpw = d_yc * gate
        dct_ref[...] = (d_yc * cpw * dgate).astype(BF16)
        d_cpw_b = d_cpw.astype(BF16)
        dpb_ref[...] += _colsum(d_cpw)
        dpw_ref[...] += _dot_tn(sb, d_cpw_b)
        du = _dot_nt(d_cpw_b, pw_ref[...]) * ds_du
        dconv_ref[...] = _ln_bwd_rows(du, chat, crstd, cg_ref[...])
        dcg_ref[...] += _colsum(du * chat)
        dcb_ref[...] += _colsum(du)

    row = lambda i: (i, 0)
    const = lambda i: (0, 0)
    vec = pl.BlockSpec((None, 1, CW), lambda i: (l, 0, 0))
    t_specs = [pl.BlockSpec((BLK, D), functools.partial(lambda i, q: (jnp.clip(i * per - 1 + q, 0, last_blk), 0), q=q))
               for q in range(per)]
    return _side_call(
        body, job, name=f"post_ln_dcat_bwd{l}", grid=(T // tm,),
        in_specs=[pl.BlockSpec((tm, D), row)] + t_specs + [
            pl.BlockSpec((tm, D), row), pl.BlockSpec((tm, 1), row), pl.BlockSpec((None, 1, D), lambda i: (l, 0, 0)),
            pl.BlockSpec((N_SHARD, WOUT_SH, D), lambda i: (0, 0, 0)),
            pl.BlockSpec((tm, CW), row), pl.BlockSpec((tm, CW), lambda i: (i, 2)), vec, vec,
            pl.BlockSpec((CW, CW), const), vec],
        out_specs=[pl.BlockSpec((1, D), const), pl.BlockSpec((tm, D), row), pl.BlockSpec((tm, D), row),
                   pl.BlockSpec((1, D), const), pl.BlockSpec((1, D), const),
                   pl.BlockSpec((tm, AW), row), pl.BlockSpec((tm, LW), row),
                   pl.BlockSpec((tm, CW), row), pl.BlockSpec((tm, CW), lambda i: (i, 2)),
                   pl.BlockSpec((CW, CW), const), pl.BlockSpec((1, CW), const),
                   pl.BlockSpec((1, CW), const), pl.BlockSpec((1, CW), const)],
        out_shape=[jax.ShapeDtypeStruct((1, D), F32), jax.ShapeDtypeStruct((T, D), F32),
                   jax.ShapeDtypeStruct((T, D), BF16), jax.ShapeDtypeStruct((1, D), F32),
                   jax.ShapeDtypeStruct((1, D), F32),
                   jax.ShapeDtypeStruct((T, AW), F32), jax.ShapeDtypeStruct((T, LW), F32),
                   jax.ShapeDtypeStruct((T, CW), F32), jax.ShapeDtypeStruct((T, IN_TOTAL), BF16),
                   jax.ShapeDtypeStruct((CW, CW), F32), jax.ShapeDtypeStruct((1, CW), F32),
                   jax.ShapeDtypeStruct((1, CW), F32), jax.ShapeDtypeStruct((1, CW), F32)],
        scratch_shapes=[], semantics=("arbitrary",),
        args=[src] + [target] * per + [xhat, rstd, g, w_out, conv, proj, cln_g, cln_b, pw_w, pw_b])


def _dwout_bwd(yc, ya, yl, dzb, l):
    T = dzb.shape[0]
    tm = _pick(T, (384, 128))

    def body(yc_ref, ya_ref, yl_ref, dz_ref, o_ref):
        @pl.when(pl.program_id(0) == 0)
        def _():
            o_ref[...] = jnp.zeros_like(o_ref)

        cat = jnp.concatenate([yc_ref[...], ya_ref[...], yl_ref[...]], axis=1)
        o_ref[...] += _dot_tn(cat, dz_ref[...])

    row = lambda t: (t, 0)
    out = pl.pallas_call(
        body, name=f"dwout_bwd{l}", grid=(T // tm,),
        in_specs=[pl.BlockSpec((tm, CW), row), pl.BlockSpec((tm, AW), row), pl.BlockSpec((tm, LW), row),
                  pl.BlockSpec((tm, D), row)],
        out_specs=pl.BlockSpec((D, D), lambda t: (0, 0)),
        out_shape=jax.ShapeDtypeStruct((D, D), F32),
        compiler_params=_cp("arbitrary"),
    )(yc, ya, yl, dzb)
    return out.reshape(N_SHARD, 2, WOUT_SH // 2, D)


def _dh_bwd(dproj, w_in, dz, l, job=None):
    T = dproj.shape[0]
    tm = _pick(T, (1056, 384, 128))

    n_w = len(w_in)

    def body(dp_ref, *refs):
        w_refs, (dz_ref, o_ref, acc_ref) = refs[:n_w], refs[n_w:]
        j = pl.program_id(1)

        @pl.when(j == 0)
        def _():
            acc_ref[...] = ALPHA * dz_ref[...]

        dp = dp_ref[...]
        off = 0
        for w_ref in w_refs:
            rows = w_ref.shape[0]
            acc_ref[:, off:off + rows] += _dot_nt(dp, w_ref[...])
            off += rows

        @pl.when(j == N_SHARD - 1)
        def _():
            o_ref[...] = acc_ref[...]

    return _side_call(
        body, job, name=f"dh_bwd{l}", grid=(T // tm, N_SHARD),
        in_specs=[pl.BlockSpec((tm, WIN_SH), lambda i, j: (i, j))]
        + [pl.BlockSpec((None, w.shape[1], WIN_SH), lambda i, j: (j, 0, 0)) for w in w_in]
        + [pl.BlockSpec((tm, D), lambda i, j: (i, 0))],
        out_specs=[pl.BlockSpec((tm, D), lambda i, j: (i, 0))],
        out_shape=[jax.ShapeDtypeStruct((T, D), F32)],
        scratch_shapes=[pltpu.VMEM((tm, D), F32)],
        semantics=("parallel", "arbitrary"), args=[dproj, *w_in, dz])


def _dwin_bwd(hb, dproj, l):
    T = hb.shape[0]
    tm = _pick(T, (1056, 384, 128))

    def body(h_ref, dp_ref, o_ref):
        @pl.when(pl.program_id(1) == 0)
        def _():
            o_ref[...] = jnp.zeros_like(o_ref)

        o_ref[...] += _dot_tn(h_ref[...], dp_ref[...])

    out = pl.pallas_call(
        body, name=f"dwin_bwd{l}", grid=(N_SHARD, T // tm),
        in_specs=[pl.BlockSpec((tm, D), lambda j, t: (t, 0)),
                  pl.BlockSpec((tm, WIN_SH), lambda j, t: (t, j))],
        out_specs=pl.BlockSpec((None, D, WIN_SH), lambda j, t: (j, 0, 0)),
        out_shape=jax.ShapeDtypeStruct((N_SHARD, D, WIN_SH), F32),
        compiler_params=_cp("parallel", "arbitrary"),
    )(hb, dproj)
    return out.reshape(N_SHARD, 2, D // 2, WIN_SH)


def _dwin_half(hb, dproj, which, l, tag, job=None):
    T = hb.shape[0]
    tm = _pick(T, (1056, 384, 128))
    hr = D // 2

    def body(w_ref, h_ref, dp_ref, o_ref):
        @pl.when(pl.program_id(1) == 0)
        def _():
            o_ref[...] = jnp.zeros_like(o_ref)

        o_ref[...] += _dot_tn(h_ref[...], dp_ref[...])

    return _side_call(
        body, job, name=f"dwin_{tag}{l}", grid=(N_SHARD, T // tm),
        in_specs=[pl.BlockSpec((tm, hr), lambda j, t, w: (t, w[0])),
                  pl.BlockSpec((tm, WIN_SH), lambda j, t, w: (t, j))],
        out_specs=[pl.BlockSpec((None, hr, WIN_SH), lambda j, t, w: (j, 0, 0))],
        out_shape=[jax.ShapeDtypeStruct((N_SHARD, hr, WIN_SH), F32)],
        scratch_shapes=[], semantics=("parallel", "arbitrary"), args=[hb, dproj],
        prefetch=[jnp.reshape(which, (1,)).astype(jnp.int32)])


def _glu_masked(v, g, base_row):
    rows = _row_ids(v.shape[0], base_row)
    return jnp.where(rows >= PAD, v * _sigmoid(g), 0.0)


def _conv_tile(T):
    return _pick(T, (384, 128))


SUBLANES = 8


def _for_each_shift(buf, rot, tm, offsets, fn):
    for r in range(SUBLANES):
        group = [o for o in offsets if o % SUBLANES == r]
        if not group:
            continue
        if r == 0:
            src = buf
        else:
            n = tm + max(group) - r
            rot[0:n, :] = buf[r:r + n, :]
            src = rot
        for o in group:
            fn(o, src[o - r:o - r + tm, :])


def _conv_fwd(proj, dw_w, dw_b, ln_g, ln_b, pw_w, pw_b, l, job=None):
    T = proj.shape[0]
    tm = _conv_tile(T)
    hb = tm // HALO

    def body(cv_ref, cg_ref, ct_ref, hv_ref, hg_ref, w_ref, b_ref, g_ref, be_ref, pw_ref, pb_ref,
             yc_ref, conv_ref, buf, rot):
        i = pl.program_id(0)
        buf[0:HALO, :] = _glu_masked(hv_ref[...], hg_ref[...], i * tm - HALO)
        buf[HALO:HALO + tm, :] = _glu_masked(cv_ref[...], cg_ref[...], i * tm)
        first = HALO - (CONV_K - 1)
        total = [jnp.zeros((tm, CW), F32) + b_ref[...]]

        def tap(o, tile):
            k = o - first
            total[0] = total[0] + w_ref[k:k + 1, :] * tile

        _for_each_shift(buf, rot, tm, [first + k for k in range(CONV_K)], tap)
        acc = total[0]
        conv_ref[...] = acc
        u, _, _ = _ln_rows(acc, g_ref[...], be_ref[...])
        s = u * _sigmoid(u)
        cpw = _dot(s.astype(BF16), pw_ref[...]) + pb_ref[...]
        gate, _ = _silu_and_grad(ct_ref[...])
        yc_ref[...] = (cpw * gate).astype(BF16)

    vec = pl.BlockSpec((None, 1, CW), lambda i: (l, 0, 0))
    return _side_call(
        body, job, name=f"conv_fwd{l}", grid=(T // tm,),
        in_specs=[pl.BlockSpec((tm, CW), lambda i: (i, 0)),
                  pl.BlockSpec((tm, CW), lambda i: (i, 1)),
                  pl.BlockSpec((tm, CW), lambda i: (i, 2)),
                  pl.BlockSpec((HALO, CW), lambda i: (jnp.maximum(i * hb - 1, 0), 0)),
                  pl.BlockSpec((HALO, CW), lambda i: (jnp.maximum(i * hb - 1, 0), 1)),
                  pl.BlockSpec((None, CONV_K, CW), lambda i: (l, 0, 0)),
                  vec, vec, vec,
                  pl.BlockSpec((CW, CW), lambda i: (0, 0)),
                  vec],
        out_specs=[pl.BlockSpec((tm, CW), lambda i: (i, 0)), pl.BlockSpec((tm, CW), lambda i: (i, 0))],
        out_shape=[jax.ShapeDtypeStruct((T, CW), BF16), jax.ShapeDtypeStruct((T, CW), F32)],
        scratch_shapes=[pltpu.VMEM((tm + HALO, CW), F32), pltpu.VMEM((tm + HALO, CW), F32)],
        semantics=("parallel",), args=[proj, proj, proj, proj, proj, dw_w, dw_b, ln_g, ln_b, pw_w, pw_b])


def _conv_bwd_taps(d_conv, proj, dw_w, dproj, l, job=None):
    T = d_conv.shape[0]
    tm = _conv_tile(T)
    hb = tm // HALO
    nt = T // tm
    last_halo = T // HALO - 1

    def body(dc_ref, dh_ref, cv_ref, cg_ref, hv_ref, hg_ref, w_ref, _, o_ref, dw_ref, dwb_ref, cbuf, dbuf, rot):
        i = pl.program_id(0)

        @pl.when(i == 0)
        def _():
            dw_ref[...] = jnp.zeros_like(dw_ref)
            dwb_ref[...] = jnp.zeros_like(dwb_ref)

        cbuf[0:HALO, :] = _glu_masked(hv_ref[...], hg_ref[...], i * tm - HALO)
        cbuf[HALO:HALO + tm, :] = _glu_masked(cv_ref[...], cg_ref[...], i * tm)
        dmain = dc_ref[...]
        dbuf[0:tm, :] = dmain
        dbuf[tm:tm + HALO, :] = jnp.where(i < nt - 1, dh_ref[...], 0.0)
        total = [jnp.zeros((tm, CW), F32)]

        def tap_back(o, tile):
            k = CONV_K - 1 - o
            total[0] = total[0] + w_ref[k:k + 1, :] * tile

        _for_each_shift(dbuf, rot, tm, list(range(CONV_K)), tap_back)
        acc = total[0]
        first = HALO - (CONV_K - 1)

        def tap_weight(o, tile):
            k = o - first
            dw_ref[k:k + 1, :] += _colsum(dmain * tile)

        _for_each_shift(cbuf, rot, tm, [first + k for k in range(CONV_K)], tap_weight)
        dwb_ref[...] += _colsum(dmain)
        d_c = jnp.where(_row_ids(tm, i * tm) >= PAD, acc, 0.0)
        sig = _sigmoid(cg_ref[...])
        o_ref[:, 0:CW] = (d_c * sig).astype(BF16)
        o_ref[:, CW:2 * CW] = (d_c * cv_ref[...] * sig * (1.0 - sig)).astype(BF16)

    const = lambda i: (0, 0)
    return _side_call(
        body, job, name=f"conv_bwd_taps{l}", grid=(nt,),
        in_specs=[pl.BlockSpec((tm, CW), lambda i: (i, 0)),
                  pl.BlockSpec((HALO, CW), lambda i: (jnp.minimum((i + 1) * hb, last_halo), 0)),
                  pl.BlockSpec((tm, CW), lambda i: (i, 0)),
                  pl.BlockSpec((tm, CW), lambda i: (i, 1)),
                  pl.BlockSpec((HALO, CW), lambda i: (jnp.maximum(i * hb - 1, 0), 0)),
                  pl.BlockSpec((HALO, CW), lambda i: (jnp.maximum(i * hb - 1, 0), 1)),
                  pl.BlockSpec((None, CONV_K, CW), lambda i: (l, 0, 0)),
                  pl.BlockSpec(memory_space=pl.ANY)],
        out_specs=[pl.BlockSpec((tm, 2 * CW), lambda i: (i, 0)),
                   pl.BlockSpec((HALO, CW), const), pl.BlockSpec((1, CW), const)],
        out_shape=[jax.ShapeDtypeStruct(dproj.shape, BF16), jax.ShapeDtypeStruct((HALO, CW), F32),
                   jax.ShapeDtypeStruct((1, CW), F32)],
        scratch_shapes=[pltpu.VMEM((tm + HALO, CW), F32), pltpu.VMEM((tm + HALO, CW), F32),
                        pltpu.VMEM((tm + HALO, CW), F32)],
        semantics=("arbitrary",), aliases={7: 0},
        args=[d_conv, d_conv, proj, proj, proj, proj, dw_w, dproj])


def _log1p_small(e):
    return jnp.where(e < 1e-3, e * (1.0 - e * (0.5 - e * (1.0 / 3.0))), jnp.log(1.0 + e))


def _softplus(z):
    return jnp.maximum(z, 0.0) + _log1p_small(jnp.exp(-jnp.abs(z)))


def _neg_expm1(x):
    series = -x * (1.0 + x * (1.0 / 2.0) * (1.0 + x * (1.0 / 3.0) * (1.0 + x * (1.0 / 4.0) * (
        1.0 + x * (1.0 / 5.0) * (1.0 + x * (1.0 / 6.0) * (1.0 + x * (1.0 / 7.0)))))))
    return jnp.where(x > -0.25, series, 1.0 - jnp.exp(x))


def _lru_gates(rxbuf, tm, base_row, lw_ref, lb_ref, wa_ref, ba_ref, wx_ref, bx_ref, lam_ref):
    rc = jnp.zeros((tm, LW), F32) + lb_ref[...]
    for k in range(LRU_K):
        o = LHALO - (LRU_K - 1) + k
        rc += lw_ref[k:k + 1, :] * rxbuf[o:o + tm, :]
    rcb = rc.astype(BF16)
    r = _sigmoid(_dot(rcb, wa_ref[...]) + ba_ref[...])
    ig = _sigmoid(_dot(rcb, wx_ref[...]) + bx_ref[...])
    sp = _softplus(-lam_ref[...])
    la = -LRU_C * r * sp
    a = jnp.exp(la)
    mult = jnp.sqrt(_neg_expm1(2.0 * la))
    valid = _row_ids(tm, base_row) >= PAD
    return rc, rcb, r, ig, sp, a, mult, valid


def _mask_rows(v, base_row):
    return jnp.where(_row_ids(v.shape[0], base_row) >= PAD, v, 0.0)


def _scan_rows(aa, bb, carry, out_ref, reverse):
    tm = aa.shape[0]
    sub = _row_ids(tm, 0) & (SUBLANES - 1)
    s = 1
    while s < SUBLANES:
        keep = (sub < SUBLANES - s) if reverse else (sub >= s)
        shift = tm - s if reverse else s
        a_s = jnp.where(keep, pltpu.roll(aa, shift, axis=0), 1.0)
        b_s = jnp.where(keep, pltpu.roll(bb, shift, axis=0), 0.0)
        bb = aa * b_s + bb
        aa = aa * a_s
        s *= 2
    groups = range(tm // SUBLANES)
    edge = 0 if reverse else SUBLANES - 1
    for j in (reversed(groups) if reverse else groups):
        rows = slice(SUBLANES * j, SUBLANES * j + SUBLANES)
        x = bb[rows] + aa[rows] * carry
        out_ref[rows, :] = x
        carry = x[edge:edge + 1]


def _lru_tile(T):
    return _pick(T, (384, 128))


def _lru_fwd(proj, lw, lb, wa, ba, wx, bx, lam, l, job=None):
    T = proj.shape[0]
    tm = _lru_tile(T)
    hb = tm // LHALO

    def body(rx_ref, rg_ref, hx_ref, lw_ref, lb_ref, wa_ref, ba_ref, wx_ref, bx_ref, lam_ref,
             yl_ref, hl_ref, rxbuf, carry):
        i = pl.program_id(0)

        @pl.when(i == 0)
        def _():
            carry[...] = jnp.zeros_like(carry)

        rxbuf[0:LHALO, :] = _mask_rows(hx_ref[...], i * tm - LHALO)
        rxbuf[LHALO:LHALO + tm, :] = _mask_rows(rx_ref[...], i * tm)
        rc, _, _, ig, _, a, mult, valid = _lru_gates(rxbuf, tm, i * tm, lw_ref, lb_ref, wa_ref, ba_ref,
                                                     wx_ref, bx_ref, lam_ref)
        bb = jnp.where(valid, mult * (ig * rc), 0.0)
        _scan_rows(a, bb, carry[0:1, :], hl_ref, reverse=False)
        carry[0:1, :] = hl_ref[tm - 1:tm, :]
        gate, _ = _silu_and_grad(rg_ref[...])
        yl_ref[...] = (hl_ref[...] * gate).astype(BF16)

    vec = pl.BlockSpec((None, 1, LW), lambda i: (l, 0, 0))
    mat = pl.BlockSpec((None, LW, LW), lambda i: (l, 0, 0))
    return _side_call(
        body, job, name=f"lru_fwd{l}", grid=(T // tm,),
        in_specs=[pl.BlockSpec((tm, LW), lambda i: (i, 8)),
                  pl.BlockSpec((tm, LW), lambda i: (i, 9)),
                  pl.BlockSpec((LHALO, LW), lambda i: (jnp.maximum(i * hb - 1, 0), 8)),
                  pl.BlockSpec((None, LRU_K, LW), lambda i: (l, 0, 0)),
                  vec, mat, vec, mat, vec, vec],
        out_specs=[pl.BlockSpec((tm, LW), lambda i: (i, 0)), pl.BlockSpec((tm, LW), lambda i: (i, 0))],
        out_shape=[jax.ShapeDtypeStruct((T, LW), BF16), jax.ShapeDtypeStruct((T, LW), F32)],
        scratch_shapes=[pltpu.VMEM((tm + LHALO, LW), F32), pltpu.VMEM((8, LW), F32)],
        semantics=("arbitrary",), args=[proj, proj, proj, lw, lb, wa, ba, wx, bx, lam])


def _lru_bwd(proj, hl, d_yl, lw, lb, wa, ba, wx, bx, lam, dproj, l, job=None):
    T = proj.shape[0]
    tm = _lru_tile(T)
    hb = tm // LHALO
    nt = T // tm

    def body(rx_ref, rg_ref, hx_ref, hl_ref, hh_ref, dy_ref, lw_ref, lb_ref, wa_ref, ba_ref, wx_ref, bx_ref,
             lam_ref, _, o_ref, dlw_ref, dlb_ref, dwa_ref, dba_ref, dwx_ref, dbx_ref, dlam_ref,
             rxbuf, dbuf, carry, head, gbuf):
        step = pl.program_id(0)
        i = nt - 1 - step

        @pl.when(step == 0)
        def _():
            carry[...] = jnp.zeros_like(carry)
            head[...] = jnp.zeros_like(head)
            for ref in (dlw_ref, dlb_ref, dwa_ref, dba_ref, dwx_ref, dbx_ref, dlam_ref):
                ref[...] = jnp.zeros_like(ref)

        rxbuf[0:LHALO, :] = _mask_rows(hx_ref[...], i * tm - LHALO)
        rxbuf[LHALO:LHALO + tm, :] = _mask_rows(rx_ref[...], i * tm)
        rc, rcb, r, ig, sp, a, mult, valid = _lru_gates(rxbuf, tm, i * tm, lw_ref, lb_ref, wa_ref, ba_ref,
                                                        wx_ref, bx_ref, lam_ref)
        rows = _row_ids(tm, 0)
        h = hl_ref[...]
        h_before = jnp.where(i > 0, hh_ref[LHALO - 1:LHALO, :], 0.0)
        hprev = jnp.where(rows == 0, h_before, pltpu.roll(h, 1, axis=0))
        rg = rg_ref[...]
        gate, dgate = _silu_and_grad(rg)
        dy = dy_ref[...]
        o_ref[:, LW:2 * LW] = (dy * h * dgate).astype(BF16)
        bb = dy * gate + jnp.where(rows == tm - 1, carry[0:1, :], 0.0)
        aa = jnp.where(rows == tm - 1, 0.0, pltpu.roll(a, tm - 1, axis=0))
        _scan_rows(aa, bb, jnp.zeros((1, LW), F32), gbuf, reverse=True)
        g = gbuf[...]
        dbuf[0:tm, :] = a * g
        carry[0:1, :] = dbuf[0:1, :]
        du = jnp.where(valid, g, 0.0)
        da = g * hprev
        dix = du * mult
        dmult = du * (ig * rc)
        dla = jnp.where(valid, da * a - dmult * (a * a) / mult, 0.0)
        dr = dla * (-LRU_C * sp)
        dlam_ref[...] += _colsum(dla * (LRU_C * r)) * _sigmoid(-lam_ref[...])
        dpa = dr * r * (1.0 - r)
        dpx = (dix * rc) * ig * (1.0 - ig)
        dpab = dpa.astype(BF16)
        dpxb = dpx.astype(BF16)
        dba_ref[...] += _colsum(dpa)
        dbx_ref[...] += _colsum(dpx)
        dwa_ref[...] += _dot_tn(rcb, dpab)
        dwx_ref[...] += _dot_tn(rcb, dpxb)
        drc = dix * ig + _dot_nt(dpab, wa_ref[...]) + _dot_nt(dpxb, wx_ref[...])
        dbuf[0:tm, :] = drc
        dbuf[tm:tm + LHALO, :] = head[...]
        acc = jnp.zeros((tm, LW), F32)
        for k in range(LRU_K):
            o = LRU_K - 1 - k
            acc += lw_ref[k:k + 1, :] * dbuf[o:o + tm, :]
            oc = LHALO - (LRU_K - 1) + k
            dlw_ref[k:k + 1, :] += _colsum(drc * rxbuf[oc:oc + tm, :])
        dlb_ref[...] += _colsum(drc)
        head[...] = dbuf[0:LHALO, :]
        o_ref[:, 0:LW] = jnp.where(valid, acc, 0.0).astype(BF16)

    rev = lambda s: nt - 1 - s
    vec = pl.BlockSpec((None, 1, LW), lambda s: (l, 0, 0))
    mat = pl.BlockSpec((None, LW, LW), lambda s: (l, 0, 0))
    const = lambda s: (0, 0)
    halo = lambda s: jnp.maximum(rev(s) * hb - 1, 0)
    return _side_call(
        body, job, name=f"lru_bwd{l}", grid=(nt,),
        in_specs=[pl.BlockSpec((tm, LW), lambda s: (rev(s), 8)),
                  pl.BlockSpec((tm, LW), lambda s: (rev(s), 9)),
                  pl.BlockSpec((LHALO, LW), lambda s: (halo(s), 8)),
                  pl.BlockSpec((tm, LW), lambda s: (rev(s), 0)),
                  pl.BlockSpec((LHALO, LW), lambda s: (halo(s), 0)),
                  pl.BlockSpec((tm, LW), lambda s: (rev(s), 0)),
                  pl.BlockSpec((None, LRU_K, LW), lambda s: (l, 0, 0)),
                  vec, mat, vec, mat, vec, vec, pl.BlockSpec(memory_space=pl.ANY)],
        out_specs=[pl.BlockSpec((tm, 2 * LW), lambda s: (rev(s), 4)),
                   pl.BlockSpec((8, LW), const), pl.BlockSpec((1, LW), const),
                   pl.BlockSpec((LW, LW), const), pl.BlockSpec((1, LW), const),
                   pl.BlockSpec((LW, LW), const), pl.BlockSpec((1, LW), const),
                   pl.BlockSpec((1, LW), const)],
        out_shape=[jax.ShapeDtypeStruct(dproj.shape, BF16),
                   jax.ShapeDtypeStruct((8, LW), F32), jax.ShapeDtypeStruct((1, LW), F32),
                   jax.ShapeDtypeStruct((LW, LW), F32), jax.ShapeDtypeStruct((1, LW), F32),
                   jax.ShapeDtypeStruct((LW, LW), F32), jax.ShapeDtypeStruct((1, LW), F32),
                   jax.ShapeDtypeStruct((1, LW), F32)],
        scratch_shapes=[pltpu.VMEM((tm + LHALO, LW), F32), pltpu.VMEM((tm + LHALO, LW), F32),
                        pltpu.VMEM((8, LW), F32), pltpu.VMEM((LHALO, LW), F32), pltpu.VMEM((tm, LW), F32)],
        semantics=("arbitrary",), aliases={13: 0},
        args=[proj, proj, proj, hl, hl, d_yl, lw, lb, wa, ba, wx, bx, lam, dproj])


def _rope_tables(T):
    pos = (lax.broadcasted_iota(jnp.int32, (T, 128), 0) - PAD).astype(F32)
    lane = lax.broadcasted_iota(jnp.int32, (T, 128), 1) % 64
    inv_freq = ROPE_THETA ** (-(lane % ROT_HALF).astype(F32) / ROT_HALF)
    ang = pos * inv_freq
    cos, sin = jnp.cos(ang), jnp.sin(ang)
    c = jnp.where(lane < 2 * ROT_HALF, cos, 1.0)
    s1 = jnp.where(lane < ROT_HALF, -sin, 0.0)
    s2 = jnp.where((lane >= ROT_HALF) & (lane < 2 * ROT_HALF), sin, 0.0)
    return c, s1, s2


def _rot_fwd(x, c, s1, s2):
    return x * c + pltpu.roll(x, 128 - ROT_HALF, axis=1) * s1 + pltpu.roll(x, ROT_HALF, axis=1) * s2


def _rot_bwd(dy, c, s1, s2):
    return dy * c + pltpu.roll(dy * s1, ROT_HALF, axis=1) + pltpu.roll(dy * s2, 128 - ROT_HALF, axis=1)


def _rope_fwd(proj, tabs, l):
    T = proj.shape[0]

    def body(ql_ref, qh_ref, k_ref, v_ref, c_ref, s1_ref, s2_ref, qr_ref, kr_ref, vb_ref):
        c, s1, s2 = c_ref[...], s1_ref[...], s2_ref[...]
        for gcol in range(AW // 128):
            src = ql_ref if gcol < 4 else qh_ref
            x = src[:, 128 * (gcol % 4):128 * (gcol % 4) + 128]
            qr_ref[:, 128 * gcol:128 * gcol + 128] = (_rot_fwd(x, c, s1, s2) * 0.125).astype(BF16)
        for gcol in range(KVW // 128):
            x = k_ref[:, 128 * gcol:128 * gcol + 128]
            kr_ref[:, 128 * gcol:128 * gcol + 128] = _rot_fwd(x, c, s1, s2).astype(BF16)
        vb_ref[...] = v_ref[...].astype(BF16)

    tr = _pick(T, (384, 128))
    tab = pl.BlockSpec((tr, 128), lambda n: (n, 0))
    return pl.pallas_call(
        body, name=f"rope_fwd{l}", grid=(T // tr,),
        in_specs=[pl.BlockSpec((tr, 512), lambda n: (n, 3)), pl.BlockSpec((tr, 512), lambda n: (n, 4)),
                  pl.BlockSpec((tr, KVW), lambda n: (n, 10)), pl.BlockSpec((tr, KVW), lambda n: (n, 11)),
                  tab, tab, tab],
        out_specs=[pl.BlockSpec((tr, AW), lambda n: (n, 0)), pl.BlockSpec((tr, KVW), lambda n: (n, 0)),
                   pl.BlockSpec((tr, KVW), lambda n: (n, 0))],
        out_shape=[jax.ShapeDtypeStruct((T, AW), BF16), jax.ShapeDtypeStruct((T, KVW), BF16),
                   jax.ShapeDtypeStruct((T, KVW), BF16)],
        compiler_params=_cp("parallel"),
    )(proj, proj, proj, proj, *tabs)


GROUP = 4


def _attn_mask(n, reps):
    qi = lax.broadcasted_iota(jnp.int32, (reps * BLK, BLK), 0) & (BLK - 1)
    kj = lax.broadcasted_iota(jnp.int32, (reps * BLK, BLK), 1)
    m0 = (kj >= PAD) & (n >= 1)
    mp = (kj > qi) & (n >= 2)
    mc = (kj <= qi) & ((n >= 1) | (kj >= PAD))
    return jnp.concatenate([m0, mp, mc], axis=1)


def _kv_both(x0_ref, xp_ref, xc_ref, g):
    pg, off = g // 2, g % 2
    cols = slice(128 * pg, 128 * pg + 128)
    x = jnp.concatenate([x0_ref[:, cols], xp_ref[:, cols], xc_ref[:, cols]], axis=0).astype(F32)
    lane = lax.broadcasted_iota(jnp.int32, (1, 128), 1)
    half = jnp.where((lane < 64) if off == 0 else (lane >= 64), x, 0.0)
    return (half + pltpu.roll(half, 64, axis=1)).astype(BF16)


def _stack_heads(a, b):
    lo = lax.broadcasted_iota(jnp.int32, (1, 128), 1) < 64
    a, b = a.astype(F32), b.astype(F32)
    return jnp.concatenate([jnp.where(lo, a, 0.0), jnp.where(lo, 0.0, a),
                            jnp.where(lo, b, 0.0), jnp.where(lo, 0.0, b)], axis=0).astype(BF16)


def _unstack_heads(x):
    lo = lax.broadcasted_iota(jnp.int32, (1, 128), 1) < 64
    return (jnp.where(lo, x[0:BLK], x[BLK:2 * BLK]), jnp.where(lo, x[2 * BLK:3 * BLK], x[3 * BLK:4 * BLK]))


def _per_head_column(values):
    return jnp.concatenate([jnp.zeros((BLK, 1), F32) + v for v in values], axis=0)


def _attn_fwd(qr, kr, vb, proj, sinks, l, job=None):
    T = qr.shape[0]

    def body(sink_ref, q_ref, k0_ref, kp_ref, kc_ref, v0_ref, vp_ref, vc_ref, ag_ref, ya_ref, att_ref, lse_ref):
        n = pl.program_id(0)
        mask = _attn_mask(n, 1)
        lane = lax.broadcasted_iota(jnp.int32, (1, 128), 1)
        lse_acc = jnp.zeros((BLK, 128), F32)
        for g in range(4):
            kx = _kv_both(k0_ref, kp_ref, kc_ref, g)
            vx = _kv_both(v0_ref, vp_ref, vc_ref, g)
            pair_cols = [slice(128 * (2 * g + pp), 128 * (2 * g + pp) + 128) for pp in range(2)]
            s4 = _dot_nt(_stack_heads(q_ref[:, pair_cols[0]], q_ref[:, pair_cols[1]]), kx)
            probs = []
            for r in range(GROUP):
                h = GROUP * g + r
                sink = sink_ref[l, h]
                s = jnp.where(mask, s4[BLK * r:BLK * r + BLK], NEG_INF)
                m = jnp.maximum(jnp.max(s, axis=1, keepdims=True), sink)
                p = jnp.exp(s - m)
                denom = jnp.sum(p, axis=1, keepdims=True) + jnp.exp(sink - m)
                probs.append((p * (1.0 / denom)).astype(BF16))
                lse_acc = jnp.where(lane == h, m + jnp.log(denom), lse_acc)
            outs = _unstack_heads(_dot(jnp.concatenate(probs, axis=0), vx))
            for cols, out in zip(pair_cols, outs):
                att_ref[:, cols] = out
                gate, _ = _silu_and_grad(ag_ref[:, cols])
                ya_ref[:, cols] = (out * gate).astype(BF16)
        lse_ref[...] = lse_acc

    prev = lambda n: (jnp.maximum(n - 1, 0), 0)
    cur = lambda n: (n, 0)
    zero = lambda n: (0, 0)
    kv = lambda f: pl.BlockSpec((BLK, KVW), f)
    return _side_call(
        body, job, name=f"attn_fwd{l}", grid=(T // BLK,),
        in_specs=[pl.BlockSpec(memory_space=pltpu.SMEM),
                  pl.BlockSpec((BLK, AW), cur), kv(zero), kv(prev), kv(cur), kv(zero), kv(prev), kv(cur),
                  pl.BlockSpec((BLK, AW), lambda n: (n, 3))],
        out_specs=[pl.BlockSpec((BLK, AW), cur), pl.BlockSpec((BLK, AW), cur), pl.BlockSpec((BLK, 128), cur)],
        out_shape=[jax.ShapeDtypeStruct((T, AW), BF16), jax.ShapeDtypeStruct((T, AW), F32),
                   jax.ShapeDtypeStruct((T, 128), F32)],
        scratch_shapes=[], semantics=("parallel",), args=[sinks, qr, kr, kr, kr, vb, vb, vb, proj])


def _attn_bwd(qr, kr, vb, proj, att, lse, d_ya, sinks, dproj, l, job=None):
    T = qr.shape[0]
    nb = T // BLK

    def body(sink_ref, q_ref, k0_ref, kp_ref, kc_ref, v0_ref, vp_ref, vc_ref, ag_ref, att_ref, lse_ref, dy_ref, _,
             dq_ref, dk_ref, dv_ref, dk0_ref, dv0_ref, dag_ref, dsink_ref, kcarry, vcarry):
        n = pl.program_id(0)

        @pl.when(n == 0)
        def _():
            dk0_ref[...] = jnp.zeros_like(dk0_ref)
            dv0_ref[...] = jnp.zeros_like(dv0_ref)
            dsink_ref[...] = jnp.zeros_like(dsink_ref)
            kcarry[...] = jnp.zeros_like(kcarry)
            vcarry[...] = jnp.zeros_like(vcarry)

        @pl.when(n == nb)
        def _():
            dk_ref[...] = kcarry[...]
            dv_ref[...] = vcarry[...]

        @pl.when(n < nb)
        def _():
            mask = _attn_mask(n, GROUP)
            lane = lax.broadcasted_iota(jnp.int32, (1, 128), 1)
            lse = lse_ref[...]
            dsink = jnp.zeros((1, 128), F32)
            dk_pg, dv_pg = [], []
            for pg in range(2):
                dk_acc = jnp.zeros((3 * BLK, 128), F32)
                dv_acc = jnp.zeros((3 * BLK, 128), F32)
                for off in range(2):
                    g = 2 * pg + off
                    kx = _kv_both(k0_ref, kp_ref, kc_ref, g)
                    vx = _kv_both(v0_ref, vp_ref, vc_ref, g)
                    pair_cols = [slice(128 * (2 * g + pp), 128 * (2 * g + pp) + 128) for pp in range(2)]
                    q4 = _stack_heads(q_ref[:, pair_cols[0]], q_ref[:, pair_cols[1]])
                    d_out = []
                    for cols in pair_cols:
                        gate, dgate = _silu_and_grad(ag_ref[:, cols])
                        dy = dy_ref[:, cols]
                        dag_ref[:, cols] = (dy * att_ref[:, cols] * dgate).astype(BF16)
                        d_out.append(dy * gate)
                    do4 = _stack_heads(d_out[0], d_out[1])
                    heads = [GROUP * g + r for r in range(GROUP)]
                    sink = _per_head_column([sink_ref[l, h] for h in heads])
                    lse4 = _per_head_column(
                        [jnp.sum(jnp.where(lane == h, lse, 0.0), axis=1, keepdims=True) for h in heads])
                    p = jnp.where(mask, jnp.exp(_dot_nt(q4, kx) - lse4), 0.0)
                    dp = _dot_nt(do4, vx)
                    delta = jnp.sum(p * dp, axis=1, keepdims=True)
                    ds = (p * (dp - delta)).astype(BF16)
                    sink_term = jnp.exp(sink - lse4) * delta
                    for r, h in enumerate(heads):
                        dsink += jnp.where(lane == h, -jnp.sum(sink_term[BLK * r:BLK * r + BLK]), 0.0)
                    for cols, dq in zip(pair_cols, _unstack_heads(_dot(ds, kx))):
                        dq_ref[:, cols] = dq
                    dkg = _dot_tn(ds, q4)
                    dvg = _dot_tn(p.astype(BF16), do4)
                    own = (lane < 64) if off == 0 else (lane >= 64)
                    dk_acc += jnp.where(own, dkg + pltpu.roll(dkg, 64, axis=1), 0.0)
                    dv_acc += jnp.where(own, dvg + pltpu.roll(dvg, 64, axis=1), 0.0)
                dk_pg.append(dk_acc)
                dv_pg.append(dv_acc)
            dsink_ref[...] += dsink
            for pg in range(2):
                cols = slice(128 * pg, 128 * pg + 128)
                dk0_ref[:, cols] += dk_pg[pg][0:BLK]
                dv0_ref[:, cols] += dv_pg[pg][0:BLK]
                dk_ref[:, cols] = kcarry[:, cols] + dk_pg[pg][BLK:2 * BLK]
                dv_ref[:, cols] = vcarry[:, cols] + dv_pg[pg][BLK:2 * BLK]
                kcarry[:, cols] = dk_pg[pg][2 * BLK:3 * BLK]
                vcarry[:, cols] = dv_pg[pg][2 * BLK:3 * BLK]

    last = nb - 1
    cur = lambda n: (jnp.minimum(n, last), 0)
    prev = lambda n: (jnp.clip(n - 1, 0, last), 0)
    zero = lambda n: (0, 0)
    kv = lambda f: pl.BlockSpec((BLK, KVW), f)
    wide = lambda f: pl.BlockSpec((BLK, AW), f)
    return _side_call(
        body, job, name=f"attn_bwd{l}", grid=(nb + 1,),
        in_specs=[pl.BlockSpec(memory_space=pltpu.SMEM),
                  wide(cur), kv(zero), kv(prev), kv(cur), kv(zero), kv(prev), kv(cur),
                  pl.BlockSpec((BLK, AW), lambda n: (jnp.minimum(n, last), 3)),
                  wide(cur), pl.BlockSpec((BLK, 128), cur), wide(cur), pl.BlockSpec(memory_space=pl.ANY)],
        out_specs=[wide(cur), kv(prev), kv(prev), kv(zero), kv(zero),
                   pl.BlockSpec((BLK, AW), lambda n: (jnp.minimum(n, last), 3)),
                   pl.BlockSpec((1, 128), zero)],
        out_shape=[jax.ShapeDtypeStruct((T, AW), F32), jax.ShapeDtypeStruct((T, KVW), F32),
                   jax.ShapeDtypeStruct((T, KVW), F32), jax.ShapeDtypeStruct((BLK, KVW), F32),
                   jax.ShapeDtypeStruct((BLK, KVW), F32), jax.ShapeDtypeStruct(dproj.shape, BF16),
                   jax.ShapeDtypeStruct((1, 128), F32)],
        scratch_shapes=[pltpu.VMEM((BLK, KVW), F32), pltpu.VMEM((BLK, KVW), F32)],
        semantics=("arbitrary",), aliases={12: 5},
        args=[sinks, qr, kr, kr, kr, vb, vb, vb, proj, att, lse, d_ya, dproj])


def _rope_bwd(dqr, dk, dv, dk0, dv0, tabs, dproj, l):
    T = dqr.shape[0]

    def body(dq_ref, dk_ref, dv_ref, dk0_ref, dv0_ref, c_ref, s1_ref, s2_ref, _, o_ref):
        n = pl.program_id(0)
        c, s1, s2 = c_ref[...], s1_ref[...], s2_ref[...]
        for gcol in range(AW // 128):
            cols = slice(128 * gcol, 128 * gcol + 128)
            o_ref[:, cols] = (_rot_bwd(dq_ref[:, cols], c, s1, s2) * 0.125).astype(BF16)
        for gcol in range(KVW // 128):
            cols = slice(128 * gcol, 128 * gcol + 128)
            kcols = slice(AW + 128 * gcol, AW + 128 * gcol + 128)
            vcols = slice(AW + KVW + 128 * gcol, AW + KVW + 128 * gcol + 128)
            o_ref[:, kcols] = _rot_bwd(dk_ref[:, cols], c, s1, s2).astype(BF16)
            o_ref[:, vcols] = dv_ref[:, cols].astype(BF16)

            @pl.when(n == 0)
            def _():
                dkk = dk_ref[0:BLK, cols] + dk0_ref[:, cols]
                o_ref[0:BLK, kcols] = _rot_bwd(dkk, c[0:BLK], s1[0:BLK], s2[0:BLK]).astype(BF16)
                o_ref[0:BLK, vcols] = (dv_ref[0:BLK, cols] + dv0_ref[:, cols]).astype(BF16)

    tr = _pick(T, (384, 128))
    cur = lambda n: (n, 0)
    zero = lambda n: (0, 0)
    tab = pl.BlockSpec((tr, 128), cur)
    return pl.pallas_call(
        body, name=f"rope_bwd{l}", grid=(T // tr,),
        in_specs=[pl.BlockSpec((tr, AW), cur), pl.BlockSpec((tr, KVW), cur), pl.BlockSpec((tr, KVW), cur),
                  pl.BlockSpec((BLK, KVW), zero), pl.BlockSpec((BLK, KVW), zero), tab, tab, tab,
                  pl.BlockSpec(memory_space=pl.ANY)],
        out_specs=pl.BlockSpec((tr, AW + 2 * KVW), lambda n: (n, 1)),
        out_shape=jax.ShapeDtypeStruct(dproj.shape, BF16),
        input_output_aliases={8: 0},
        compiler_params=_cp("parallel"),
    )(dqr, dk, dv, dk0, dv0, *tabs, dproj)


def _block_diag(w):
    nl, nh, hd, _ = w.shape
    eye = jnp.eye(nh, dtype=w.dtype)
    return jnp.einsum("lhij,hg->lhigj", w, eye).reshape(nl, nh * hd, nh * hd)


def _diag_blocks(m):
    nh, hd = 8, 64
    return jnp.einsum("hihj->hij", m.reshape(nh, hd, nh, hd))


def _device_step(x, target, p, dist=None):
    vec = lambda a: a.reshape(DEPTH, 1, a.shape[-1])
    ln_in_g, ln_in_b = p["ln_in_g"].reshape(1, D), p["ln_in_b"].reshape(1, D)
    conv_dw_b, conv_ln_g, conv_ln_b, conv_pw_b = map(vec, (p["conv_dw_b"], p["conv_ln_g"], p["conv_ln_b"], p["conv_pw_b"]))
    lru_conv_b, lru_ba, lru_bx, lru_lambda = map(vec, (p["lru_conv_b"], p["lru_ba"], p["lru_bx"], p["lru_lambda"]))
    ln_post_g, ln_post_b = vec(p["ln_post_g"]), vec(p["ln_post_b"])
    wa_bd = _block_diag(p["lru_wa"]).astype(BF16)
    wx_bd = _block_diag(p["lru_wx"]).astype(BF16)
    w_in, w_out, pw_w = list(p["w_in"]), list(p["w_out"]), list(p["conv_pw_w"])
    sinks = p["attn_sinks"]
    big_names = ("w_in", "w_out", "conv_pw_w")

    order = dist[4] if dist else jnp.arange(N_SHARD, dtype=jnp.int32)
    (h, hb), got = _embed_fwd(x, p["meta_tokens"], ln_in_g, ln_in_b,
                              job=_gather_job([w_in[0]], peers=(0, 1)) if dist else None)
    if dist:
        w_in[0] = got[0]
    T = h.shape[0]
    tabs = _rope_tables(T)
    saved = []
    for l in range(DEPTH):
        if l == 0:
            job = _join_jobs(_gather_job([w_in[0]], peers=(2,)), _gather_job([pw_w[0]])) if dist else None
            (proj,), got = _proj_fwd(hb, w_in[0], order, 0, N_SHARD - 1, None, l, job=job)
            if dist:
                w_in[0], pw_w[0] = got
            (proj,), _ = _proj_fwd(hb, w_in[0], order, N_SHARD - 1, 1, proj, l)
        else:
            (proj,), _ = _proj_fwd(hb, w_in[l], order, 0, N_SHARD, None, l)
        pw_l = pw_w[l].reshape(CW, CW)
        (yc, conv), got = _conv_fwd(proj, p["conv_dw_w"], conv_dw_b, conv_ln_g, conv_ln_b, pw_l, conv_pw_b, l,
                                    job=_gather_job([w_out[0]]) if dist and l == 0 else None)
        if got:
            w_out[0] = got[0]
        qr, kr, vb = _rope_fwd(proj, tabs, l)
        (ya, att, lse), got = _attn_fwd(
            qr, kr, vb, proj, sinks, l, job=_gather_job([w_in[1]]) if dist and l == 0 else None)
        if got:
            w_in[1] = got[0]
        (yl, hl), _ = _lru_fwd(proj, p["lru_conv_w"], lru_conv_b, wa_bd, lru_ba, wx_bd, lru_bx, lru_lambda, l)
        (hn, hnb, xhat, rstd), got = _out_fwd(
            yc, ya, yl, w_out[l], h, ln_post_g, ln_post_b, l,
            job=_gather_job([w_out[1], pw_w[1]]) if dist and l == 0 else None)
        if got:
            w_out[1], pw_w[1] = got
        saved.append((hb, proj, yc, conv, qr, kr, vb, ya, att, lse, yl, hl, xhat, rstd, pw_l))
        h, hb = hn, hnb

    dh = None
    g = {}
    later = None
    early, last = ("w_out", "conv_pw_w"), ("w_in",)
    own = {}
    for l in reversed(range(DEPTH)):
        hb_l, proj, yc, conv, qr, kr, vb, ya, att, lse, yl, hl, xhat, rstd, pw_l = saved[l]
        tail = dist is not None and l == 0
        top = l == DEPTH - 1
        (part, dz, dzb, g["ln_post_g", l], g["ln_post_b", l], d_ya, d_yl, d_conv, dproj, dpw, g["conv_pw_b", l],
         g["conv_ln_g", l], g["conv_ln_b", l]), recv = _post_ln_dcat_bwd(
            h if top else dh, target if top else None, xhat, rstd, ln_post_g, w_out[l],
            conv, proj, conv_ln_g, conv_ln_b, pw_l, conv_pw_b, l,
            job=_swap_job(later["grads"]) if later else None)
        if top:
            loss_part = part
        if later:
            later["parts"], later["owns"] = _chip_partials(big_names, later["grads"], recv, dist, later["l"])
        g["w_out", l] = _dwout_bwd(yc, ya, yl, dzb, l)
        g["conv_pw_w", l] = dpw.reshape(N_SHARD, 2, PW_SH // 2, CW)
        if tail:
            own["early"] = dict(l=0, grads=[g[name, 0] for name in early])
        job = None
        if tail:
            job = _join_jobs(_swap_job(own["early"]["grads"]), _scatter_job(later["parts"][1:]))
        (dproj, ddw, g["conv_dw_b", l]), got = _conv_bwd_taps(d_conv, proj, p["conv_dw_w"], dproj, l, job=job)
        if tail:
            n_early = len(early)
            own["early"]["parts"], own["early"]["owns"] = _chip_partials(
                early, own["early"]["grads"], got[:n_early], dist, 0)
            later["z"] = got[n_early:]
        g["conv_dw_w", l] = ddw[:CONV_K]
        (dqr, dk, dv, dk0, dv0, dproj, dsink), z = _attn_bwd(
            qr, kr, vb, proj, att, lse, d_ya, sinks, dproj, l,
            job=_scatter_job(later["parts"][:1]) if later else None)
        if later:
            later["z"] = z + later["z"]
        g["attn_sinks", l] = dsink[0, :N_HEADS]
        dproj = _rope_bwd(dqr, dk, dv, dk0, dv0, tabs, dproj, l)
        (dproj, dlw, g["lru_conv_b", l], dwa, g["lru_ba", l], dwx, g["lru_bx", l], g["lru_lambda", l]), z = _lru_bwd(
            proj, hl, d_yl, p["lru_conv_w"], lru_conv_b, wa_bd, lru_ba, wx_bd, lru_bx, lru_lambda, dproj, l,
            job=_scatter_job(own["early"]["parts"]) if tail else None)
        if tail:
            own["early"]["z"] = z
        g["lru_conv_w", l] = dlw[:LRU_K]
        g["lru_wa", l] = _diag_blocks(dwa)
        g["lru_wx", l] = _diag_blocks(dwx)
        job = None
        if l > 0:
            g["w_in", l] = _dwin_bwd(hb_l, dproj, l)
        else:
            c = dist[0] if dist else jnp.int32(0)
            job = None
            if dist:
                pack_a = _pack_rows([_layer_stack(g, name) for name in _SMALL_LAYERED])
                totals = _shard_totals(big_names, later, dist)
                job = _join_jobs(_share_job(totals), _spread_job(pack_a))
            (give,), got = _dwin_half(hb_l, dproj, 1 - c, l, "give", job=job)
            (keep,), recv = _dwin_half(hb_l, dproj, c, l, "keep", job=_send_job([give]) if dist else None)
            job = None
            if dist:
                _store_reduced(big_names, later["l"], got[:-1], g)
                later = None
                g["pack_layered", -1] = _sum_slots(pack_a, got[-1], dist[3], "layered")
                own["last"] = dict(l=0)
                own["last"]["parts"], own["last"]["owns"] = _chip_partials(
                    last, [keep.reshape(N_SHARD, 1, D // 2, WIN_SH)], recv, (jnp.int32(0),) + tuple(dist[1:]), 0)
                job = _scatter_job(own["last"]["parts"])
            else:
                g["w_in", l] = jnp.stack([keep, give], axis=1)
        (dh,), got = _dh_bwd(dproj, [w_in[l]], dz, l, job=job)
        if tail:
            own["last"]["z"] = got
        if dist and l > 0:
            later = dict(l=l, grads=[g[name, l] for name in big_names])
    grad_x, g["meta_tokens", -1], g["ln_in_g", -1], g["ln_in_b", -1] = _embed_bwd(
        dh, x, p["meta_tokens"], ln_in_g, ln_in_b)
    if dist:
        pack_b = _pack_rows([g[name, -1] for name in _SMALL_EMBED])
        state = dict(l=0, owns=own["last"]["owns"] + own["early"]["owns"], z=own["last"]["z"] + own["early"]["z"])
        totals = _shard_totals(last + early, state, dist)
        got = _run_job(_join_jobs(_share_job(totals), _spread_job(pack_b)), "share_and_spread")
        _store_reduced(last + early, 0, got[:-1], g)
        g["pack_embed", -1] = _sum_slots(pack_b, got[-1], dist[3], "embed")
    return loss_part, grad_x, g


_SMALL_EMBED = ("meta_tokens", "ln_in_g", "ln_in_b")
_SMALL_LAYERED = ("conv_dw_w", "conv_dw_b", "conv_ln_g", "conv_ln_b", "conv_pw_b", "attn_sinks", "lru_conv_w",
                  "lru_conv_b", "lru_wa", "lru_ba", "lru_wx", "lru_bx", "lru_lambda", "ln_post_g", "ln_post_b")


def _layer_stack(g, name):
    return jnp.stack([g[name, l] for l in range(DEPTH)], axis=0)


def _chip_partials(names, grads, recv, dist, l):
    outs = [_chip_partial(a, r, dist[0], dist[1], f"{name}{l}") for name, a, r in zip(names, grads, recv)]
    return [o[0] for o in outs], [o[1] for o in outs]


def _shard_totals(names, state, dist):
    l = state["l"]
    return [_shard_total(po, zz, dist[2], f"{name}{l}") for name, po, zz in zip(names, state["owns"], state["z"])]


def _store_reduced(names, l, full, g):
    for name, f in zip(names, full):
        g[name, l] = f.reshape(2 * f.shape[1], f.shape[2])


MESH = pl.DeviceIdType.MESH
HBM_SPEC = pl.BlockSpec(memory_space=pltpu.HBM)
N_DEV = 8


def _position():
    x, y, c = lax.axis_index("x"), lax.axis_index("y"), lax.axis_index("c")
    return x, y, c


def _other_chips(x, y):
    return [(1 - x, y), (x, 1 - y), (1 - x, 1 - y)]


def _cast_into_slot(a, l, j, tag):
    _, R, C = a.shape
    tb = _pick(R, (512, 128))

    def body(s_ref, a_ref, o_ref):
        o_ref[...] = a_ref[...].astype(BF16)

    grid_spec = pltpu.PrefetchScalarGridSpec(
        num_scalar_prefetch=1, grid=(R // tb,),
        in_specs=[pl.BlockSpec((None, tb, C), lambda t, sc: (l, t, 0))],
        out_specs=pl.BlockSpec((None, tb, C), lambda t, sc: (sc[0], t, 0)))
    return pl.pallas_call(
        body, name=f"cast_into_slot_{tag}{l}", grid_spec=grid_spec,
        out_shape=jax.ShapeDtypeStruct((N_SHARD, R, C), BF16),
        compiler_params=_cp("arbitrary"),
    )(jnp.reshape(j, (1,)).astype(jnp.int32), a)


class _Job:
    def __init__(self, inputs, aliased, extra_out, sems, start, mid, finish):
        self.inputs, self.extra_out, self.sems = list(inputs), list(extra_out), list(sems)
        self.n_aliased = len(self.inputs) if aliased is True else int(aliased)
        self.start, self.mid, self.finish = start, mid, finish

    def out_shapes(self):
        return [jax.ShapeDtypeStruct(a.shape, a.dtype) for a in self.inputs[:self.n_aliased]] + self.extra_out


def _side_call(body, job, *, name, grid, in_specs, out_specs, out_shape, scratch_shapes, semantics, args,
               aliases=None, prefetch=()):
    aliases = dict(aliases or {})
    n_pre = len(prefetch)

    def call(fn, ins, outs, shapes, scratch, sem, operands):
        if n_pre:
            spec = pltpu.PrefetchScalarGridSpec(num_scalar_prefetch=n_pre, grid=grid, in_specs=ins, out_specs=outs,
                                                scratch_shapes=scratch)
            return pl.pallas_call(fn, name=name, grid_spec=spec, out_shape=shapes,
                                  input_output_aliases={k + n_pre: v for k, v in aliases.items()},
                                  compiler_params=_cp(*sem))(*prefetch, *operands)
        return pl.pallas_call(fn, name=name, grid=grid, in_specs=ins, out_specs=outs, out_shape=shapes,
                              scratch_shapes=scratch, input_output_aliases=aliases,
                              compiler_params=_cp(*sem))(*operands)

    if job is None:
        return list(call(body, list(in_specs), list(out_specs), list(out_shape), list(scratch_shapes),
                         semantics, args)), []
    n_in, n_out, n_scr = len(in_specs), len(out_specs), len(scratch_shapes)
    j_in, j_out = len(job.inputs), len(job.out_shapes())
    steps = 1
    for gsize in grid:
        steps *= gsize

    def wrapped(*refs):
        pre, refs = refs[:n_pre], refs[n_pre:]
        host_in, job_in = refs[:n_in], refs[n_in:n_in + j_in]
        o0 = n_in + j_in
        host_out, job_out = refs[o0:o0 + n_out], refs[o0 + n_out:o0 + n_out + j_out]
        s0 = o0 + n_out + j_out
        host_scr, sems = refs[s0:s0 + n_scr], refs[s0 + n_scr:]
        step = pl.program_id(0)
        for d in range(1, len(grid)):
            step = step * grid[d] + pl.program_id(d)

        @pl.when(step == 0)
        def _():
            job.start(job_in, job_out, sems)

        @pl.when(step == max(steps - 2, 0))
        def _():
            job.mid(job_in, job_out, sems)

        body(*pre, *host_in, *host_out, *host_scr)

        @pl.when(step == steps - 1)
        def _():
            job.finish(job_in, job_out, sems)

    aliases.update({n_in + k: n_out + k for k in range(job.n_aliased)})
    outs = call(wrapped, list(in_specs) + [HBM_SPEC] * j_in, list(out_specs) + [HBM_SPEC] * j_out,
                list(out_shape) + job.out_shapes(), list(scratch_shapes) + job.sems,
                ["arbitrary"] * len(grid), [*args, *job.inputs])
    return list(outs[:n_out]), list(outs[n_out:])


def _run_job(job, name):
    return _side_call(lambda: None, job, name=name, grid=(1,), in_specs=[], out_specs=[], out_shape=[],
                      scratch_shapes=[], semantics=("arbitrary",), args=[])[1]


def _gather_job(slots, peers=(0, 1, 2)):
    n = len(slots)

    def copies(buf, sems):
        ici_send, ici_recv, d2d_send, d2d_recv = sems
        x, y, c = _position()
        chips = _other_chips(x, y)

        def half(k, slot, which):
            hr = buf[k].shape[1] // 2
            return buf[k].at[slot, pl.ds(pl.multiple_of(which * hr, hr), hr)]

        def over_ici(k, p, slot):
            px, py = chips[p]
            return pltpu.make_async_remote_copy(
                src_ref=half(k, slot, c), dst_ref=half(k, slot, c),
                send_sem=ici_send.at[k * 3 + p], recv_sem=ici_recv.at[k * 3 + p],
                device_id=(px, py, c), device_id_type=MESH)

        def over_d2d(k, p, which):
            px, py = chips[p]
            return pltpu.make_async_remote_copy(
                src_ref=half(k, 2 * px + py, which), dst_ref=half(k, 2 * px + py, which),
                send_sem=d2d_send.at[k * 3 + p], recv_sem=d2d_recv.at[k * 3 + p],
                device_id=(x, y, 1 - c), device_id_type=MESH)

        return over_ici, over_d2d, 2 * x + y, chips, c

    pairs = [(k, p) for k in range(n) for p in peers]

    def start(_, buf, sems):
        over_ici, _, mine, _, _ = copies(buf, sems)
        for k, p in pairs:
            over_ici(k, p, mine).start()

    def mid(_, buf, sems):
        over_ici, over_d2d, _, chips, c = copies(buf, sems)
        for k, p in pairs:
            px, py = chips[p]
            over_ici(k, p, 2 * px + py).wait_recv()
            over_d2d(k, p, c).start()

    def finish(_, buf, sems):
        over_ici, over_d2d, mine, _, c = copies(buf, sems)
        for k, p in pairs:
            over_d2d(k, p, 1 - c).wait_recv()
        for k, p in pairs:
            over_ici(k, p, mine).wait_send()
            over_d2d(k, p, c).wait_send()

    return _Job(slots, True, [], [pltpu.SemaphoreType.DMA((3 * n,))] * 4, start, mid, finish)


def _gather_shards(shards):
    n = len(shards)

    def body(*refs):
        src, dst = refs[:n], refs[n:2 * n]
        send_sems, recv_sems, local_sems = refs[2 * n:]
        x, y, c = _position()
        mine = 2 * x + y
        chips = _other_chips(x, y)

        def copy(k, p):
            return pltpu.make_async_remote_copy(
                src_ref=src[k], dst_ref=dst[k].at[mine],
                send_sem=send_sems.at[k * 3 + p], recv_sem=recv_sems.at[k * 3 + p],
                device_id=(*chips[p], c), device_id_type=MESH)

        def arrival(k, p):
            px, py = chips[p]
            return pltpu.make_async_remote_copy(
                src_ref=src[k], dst_ref=dst[k].at[2 * px + py],
                send_sem=send_sems.at[k * 3 + p], recv_sem=recv_sems.at[k * 3 + p],
                device_id=(px, py, c), device_id_type=MESH)

        local = [pltpu.make_async_copy(src[k], dst[k].at[mine], local_sems.at[k]) for k in range(n)]
        for cp in local:
            cp.start()
        for k in range(n):
            for p in range(3):
                copy(k, p).start()
        for k in range(n):
            for p in range(3):
                arrival(k, p).wait_recv()
        for k in range(n):
            for p in range(3):
                copy(k, p).wait_send()
        for cp in local:
            cp.wait()

    return pl.pallas_call(
        body, name="gather_shards",
        in_specs=[HBM_SPEC] * n, out_specs=[HBM_SPEC] * n,
        out_shape=[jax.ShapeDtypeStruct((N_SHARD,) + s.shape, s.dtype) for s in shards],
        scratch_shapes=[pltpu.SemaphoreType.DMA((3 * n,)), pltpu.SemaphoreType.DMA((3 * n,)),
                        pltpu.SemaphoreType.DMA((n,))],
    )(*shards)


def _swap_job(grads):
    n = len(grads)

    def copies(src, dst, sems):
        x, y, c = _position()
        return [pltpu.make_async_remote_copy(
            src_ref=src[k].at[:, 1 - c], dst_ref=dst[k],
            send_sem=sems[0].at[k], recv_sem=sems[1].at[k],
            device_id=(x, y, 1 - c), device_id_type=MESH) for k in range(n)]

    def start(src, dst, sems):
        for cp in copies(src, dst, sems):
            cp.start()

    def finish(src, dst, sems):
        for cp in copies(src, dst, sems):
            cp.wait()

    return _Job(grads, False, [jax.ShapeDtypeStruct((N_SHARD,) + g.shape[2:], F32) for g in grads],
                [pltpu.SemaphoreType.DMA((n,))] * 2, start, lambda *_: None, finish)


def _send_job(arrays):
    n = len(arrays)

    def copies(src, dst, sems):
        x, y, c = _position()
        return [pltpu.make_async_remote_copy(
            src_ref=src[k], dst_ref=dst[k], send_sem=sems[0].at[k], recv_sem=sems[1].at[k],
            device_id=(x, y, 1 - c), device_id_type=MESH) for k in range(n)]

    def start(src, dst, sems):
        for cp in copies(src, dst, sems):
            cp.start()

    def finish(src, dst, sems):
        for cp in copies(src, dst, sems):
            cp.wait()

    return _Job(arrays, False, [jax.ShapeDtypeStruct(a.shape, a.dtype) for a in arrays],
                [pltpu.SemaphoreType.DMA((n,))] * 2, start, lambda *_: None, finish)


def _chip_partial(a, y, c, j, tag):
    _, _, R, C = a.shape
    tr = _pick(R, (256, 64))

    def body(s_ref, a_ref, y_ref, pb_ref, po_ref):
        total = a_ref[...] + y_ref[...]
        pb_ref[...] = total.astype(BF16)

        @pl.when(pl.program_id(1) == s_ref[1])
        def _():
            po_ref[...] = total

    grid_spec = pltpu.PrefetchScalarGridSpec(
        num_scalar_prefetch=1, grid=(R // tr, N_SHARD),
        in_specs=[pl.BlockSpec((None, None, tr, C), lambda t, s, sc: (s, sc[0], t, 0)),
                  pl.BlockSpec((None, tr, C), lambda t, s, sc: (s, t, 0))],
        out_specs=[pl.BlockSpec((None, tr, C), lambda t, s, sc: (s, t, 0)),
                   pl.BlockSpec((tr, C), lambda t, s, sc: (t, 0))])
    return pl.pallas_call(
        body, name=f"chip_partial_{tag}", grid_spec=grid_spec,
        out_shape=[jax.ShapeDtypeStruct((N_SHARD, R, C), BF16), jax.ShapeDtypeStruct((R, C), F32)],
        compiler_params=_cp("arbitrary", "arbitrary"),
    )(jnp.stack([c, j]).astype(jnp.int32), a, y)


def _scatter_job(parts):
    n = len(parts)
    pairs = [(k, p) for k in range(n) for p in range(3)]

    def copy(src, dst, sems, k, p, outgoing):
        x, y, c = _position()
        mine = 2 * x + y
        px, py = _other_chips(x, y)[p]
        theirs = 2 * px + py
        return pltpu.make_async_remote_copy(
            src_ref=src[k].at[theirs if outgoing else mine], dst_ref=dst[k].at[mine if outgoing else theirs],
            send_sem=sems[0].at[k * 3 + p], recv_sem=sems[1].at[k * 3 + p],
            device_id=(px, py, c), device_id_type=MESH)

    def start(src, dst, sems):
        for k, p in pairs:
            copy(src, dst, sems, k, p, True).start()

    def finish(src, dst, sems):
        for k, p in pairs:
            copy(src, dst, sems, k, p, False).wait_recv()
        for k, p in pairs:
            copy(src, dst, sems, k, p, True).wait_send()

    return _Job(parts, False, [jax.ShapeDtypeStruct(pb.shape, BF16) for pb in parts],
                [pltpu.SemaphoreType.DMA((3 * n,))] * 2, start, lambda *_: None, finish)


def _shard_total(own, z, others_c, tag):
    R, C = own.shape
    tr = _pick(R, (256, 64))

    def body(s_ref, o_ref, z0_ref, z1_ref, z2_ref, h_ref):
        h_ref[...] = ((o_ref[...] + z0_ref[...].astype(F32)) + z1_ref[...].astype(F32)) + z2_ref[...].astype(F32)

    zspec = lambda q: pl.BlockSpec((None, tr, C), lambda t, sc: (sc[q], t, 0))
    grid_spec = pltpu.PrefetchScalarGridSpec(
        num_scalar_prefetch=1, grid=(R // tr,),
        in_specs=[pl.BlockSpec((tr, C), lambda t, sc: (t, 0)), zspec(0), zspec(1), zspec(2)],
        out_specs=pl.BlockSpec((None, tr, C), lambda t, sc: (sc[3], t, 0)))
    return pl.pallas_call(
        body, name=f"shard_total_{tag}", grid_spec=grid_spec,
        out_shape=jax.ShapeDtypeStruct((2, R, C), F32),
        compiler_params=_cp("arbitrary"),
    )(others_c, own, z, z, z)


def _share_job(totals):
    n = len(totals)

    def copy(buf, sems, k, which):
        x, y, c = _position()
        return pltpu.make_async_remote_copy(
            src_ref=buf[k].at[which], dst_ref=buf[k].at[which],
            send_sem=sems[0].at[k], recv_sem=sems[1].at[k],
            device_id=(x, y, 1 - c), device_id_type=MESH)

    def start(_, buf, sems):
        c = lax.axis_index("c")
        for k in range(n):
            copy(buf, sems, k, c).start()

    def finish(_, buf, sems):
        c = lax.axis_index("c")
        for k in range(n):
            copy(buf, sems, k, 1 - c).wait_recv()
        for k in range(n):
            copy(buf, sems, k, c).wait_send()

    return _Job(totals, True, [], [pltpu.SemaphoreType.DMA((n,))] * 2, start, lambda *_: None, finish)


def _spread_job(pack):
    def copy(src, dst, sems, m, outgoing):
        x, y, c = _position()
        peer = (x ^ (m >> 2), y ^ ((m >> 1) & 1), c ^ (m & 1))
        slot = 4 * x + 2 * y + c if outgoing else 4 * peer[0] + 2 * peer[1] + peer[2]
        return pltpu.make_async_remote_copy(
            src_ref=src[0], dst_ref=dst[0].at[slot], send_sem=sems[0].at[m - 1], recv_sem=sems[1].at[m - 1],
            device_id=peer, device_id_type=MESH)

    def start(src, dst, sems):
        for m in range(1, N_DEV):
            copy(src, dst, sems, m, True).start()

    def finish(src, dst, sems):
        for m in range(1, N_DEV):
            copy(src, dst, sems, m, False).wait_recv()
        for m in range(1, N_DEV):
            copy(src, dst, sems, m, True).wait_send()

    return _Job([pack], False, [jax.ShapeDtypeStruct((N_DEV,) + pack.shape, F32)],
                [pltpu.SemaphoreType.DMA((N_DEV - 1,))] * 2, start, lambda *_: None, finish)


def _join_jobs(a, b):
    for job in (a, b):
        assert job.n_aliased in (0, len(job.inputs)) and not (job.n_aliased and job.extra_out)
    assert a.n_aliased or not b.n_aliased
    n_in, n_out, n_sem = len(a.inputs), len(a.out_shapes()), len(a.sems)

    def phase(name):
        def run(ins, outs, sems):
            getattr(a, name)(ins[:n_in], outs[:n_out], sems[:n_sem])
            getattr(b, name)(ins[n_in:], outs[n_out:], sems[n_sem:])
        return run

    return _Job(a.inputs + b.inputs, a.n_aliased + b.n_aliased, a.extra_out + b.extra_out, a.sems + b.sems,
                phase("start"), phase("mid"), phase("finish"))


def _sum_slots(pack, slots, me, tag):
    def body(me_ref, p_ref, s_ref, o_ref):
        acc = None
        for d in range(N_DEV):
            term = jnp.where(me_ref[0] == d, p_ref[...], s_ref[d])
            acc = term if acc is None else acc + term
        o_ref[...] = acc

    vm = pl.BlockSpec(memory_space=pltpu.VMEM)
    return pl.pallas_call(
        body, name=f"sum_slots_{tag}",
        in_specs=[pl.BlockSpec(memory_space=pltpu.SMEM), vm, vm], out_specs=vm,
        out_shape=jax.ShapeDtypeStruct(pack.shape, F32),
        compiler_params=pltpu.CompilerParams(vmem_limit_bytes=V7X_VMEM_LIMIT),
    )(jnp.reshape(me, (1,)).astype(jnp.int32), pack, slots)


def _pack_rows(arrays):
    total = sum(a.size for a in arrays)
    rows = -(-total // 128)
    rows = -(-rows // PACK_ROWS_ALIGN) * PACK_ROWS_ALIGN
    flat = [a.reshape(-1) for a in arrays] + [jnp.zeros((rows * 128 - total,), F32)]
    return jnp.concatenate(flat).reshape(rows, 128)


def _adamw_math(w, g, m, v):
    m = ADAM_B1 * m + (1.0 - ADAM_B1) * g
    v = ADAM_B2 * v + (1.0 - ADAM_B2) * (g * g)
    m_hat = m / (1.0 - ADAM_B1 ** ADAM_STEP)
    v_hat = v / (1.0 - ADAM_B2 ** ADAM_STEP)
    delta = -ADAM_LR * (m_hat / (jnp.sqrt(v_hat) + ADAM_EPS) + ADAM_WD * w)
    return delta, m, v


def _adamw_big(w, g0, g1, m, v, tag):
    _, R, C = w.shape
    tr = _pick(R, (256, 128))

    def body(w_ref, g0_ref, g1_ref, m_ref, v_ref, go_ref, d_ref, mo_ref, vo_ref):
        g = jnp.where(pl.program_id(0) == 0, g0_ref[...], g1_ref[...])
        delta, mn, vn = _adamw_math(w_ref[...], g, m_ref[...], v_ref[...])
        go_ref[...] = g
        d_ref[...] = delta
        mo_ref[...] = mn
        vo_ref[...] = vn

    s3 = pl.BlockSpec((None, tr, C), lambda l, t: (l, t, 0))
    g_spec = lambda layer: pl.BlockSpec((tr, C), lambda l, t: (jnp.where(l == layer, t, 0), 0))
    shp = jax.ShapeDtypeStruct(w.shape, F32)
    return pl.pallas_call(
        body, name=f"adamw_{tag}", grid=(2, R // tr),
        in_specs=[s3, g_spec(0), g_spec(1), s3, s3], out_specs=[s3, s3, s3, s3],
        out_shape=[shp, shp, shp, shp],
        compiler_params=_cp("parallel", "parallel"),
    )(w, g0, g1, m, v)


def _adamw_small(ws, gs, ms, vs):
    n = len(ws)

    def body(*refs):
        w_r, g_r, m_r, v_r = refs[:n], refs[n:2 * n], refs[2 * n:3 * n], refs[3 * n:4 * n]
        d_o, m_o, v_o = refs[4 * n:5 * n], refs[5 * n:6 * n], refs[6 * n:7 * n]
        for k in range(n):
            delta, mn, vn = _adamw_math(w_r[k][...], g_r[k][...], m_r[k][...], v_r[k][...])
            d_o[k][...] = delta
            m_o[k][...] = mn
            v_o[k][...] = vn

    vm = pl.BlockSpec(memory_space=pltpu.VMEM)
    shapes = [jax.ShapeDtypeStruct(w.shape, F32) for w in ws]
    outs = pl.pallas_call(
        body, name="adamw_small",
        in_specs=[vm] * (4 * n), out_specs=[vm] * (3 * n),
        out_shape=shapes * 3,
    )(*ws, *gs, *ms, *vs)
    return outs[:n], outs[n:2 * n], outs[2 * n:]


_WEIGHTS = ["meta_tokens", "ln_in_g", "ln_in_b", "w_in", "conv_dw_w", "conv_dw_b", "conv_ln_g", "conv_ln_b",
            "conv_pw_w", "conv_pw_b", "attn_sinks", "lru_conv_w", "lru_conv_b", "lru_wa", "lru_ba", "lru_wx",
            "lru_bx", "lru_lambda", "w_out", "ln_post_g", "ln_post_b"]
_BIG = ("w_in", "w_out", "conv_pw_w")
_SMALL_SHARDED = {"meta_tokens": 1, "conv_dw_w": 2, "lru_conv_w": 2}
PACK_ROWS_ALIGN = 8


def _as2d(a):
    return a.reshape(1, -1) if a.ndim == 1 else a.reshape(-1, a.shape[-1])


def kernel(x, meta_tokens, ln_in_g, ln_in_b, w_in, conv_dw_w, conv_dw_b, conv_ln_g, conv_ln_b, conv_pw_w, conv_pw_b, attn_sinks, lru_conv_w, lru_conv_b, lru_wa, lru_ba, lru_wx, lru_bx, lru_lambda, w_out, ln_post_g, ln_post_b, loss_target, m_meta_tokens, m_ln_in_g, m_ln_in_b, m_w_in, m_conv_dw_w, m_conv_dw_b, m_conv_ln_g, m_conv_ln_b, m_conv_pw_w, m_conv_pw_b, m_attn_sinks, m_lru_conv_w, m_lru_conv_b, m_lru_wa, m_lru_ba, m_lru_wx, m_lru_bx, m_lru_lambda, m_w_out, m_ln_post_g, m_ln_post_b, v_meta_tokens, v_ln_in_g, v_ln_in_b, v_w_in, v_conv_dw_w, v_conv_dw_b, v_conv_ln_g, v_conv_ln_b, v_conv_pw_w, v_conv_pw_b, v_attn_sinks, v_lru_conv_w, v_lru_conv_b, v_lru_wa, v_lru_ba, v_lru_wx, v_lru_bx, v_lru_lambda, v_w_out, v_ln_post_g, v_ln_post_b):
    w = dict(meta_tokens=meta_tokens, ln_in_g=ln_in_g, ln_in_b=ln_in_b, w_in=w_in, conv_dw_w=conv_dw_w,
             conv_dw_b=conv_dw_b, conv_ln_g=conv_ln_g, conv_ln_b=conv_ln_b, conv_pw_w=conv_pw_w,
             conv_pw_b=conv_pw_b, attn_sinks=attn_sinks, lru_conv_w=lru_conv_w, lru_conv_b=lru_conv_b,
             lru_wa=lru_wa, lru_ba=lru_ba, lru_wx=lru_wx, lru_bx=lru_bx, lru_lambda=lru_lambda, w_out=w_out,
             ln_post_g=ln_post_g, ln_post_b=ln_post_b)
    mom_m = dict(zip(_WEIGHTS, (m_meta_tokens, m_ln_in_g, m_ln_in_b, m_w_in, m_conv_dw_w, m_conv_dw_b, m_conv_ln_g,
                                m_conv_ln_b, m_conv_pw_w, m_conv_pw_b, m_attn_sinks, m_lru_conv_w, m_lru_conv_b,
                                m_lru_wa, m_lru_ba, m_lru_wx, m_lru_bx, m_lru_lambda, m_w_out, m_ln_post_g,
                                m_ln_post_b)))
    mom_v = dict(zip(_WEIGHTS, (v_meta_tokens, v_ln_in_g, v_ln_in_b, v_w_in, v_conv_dw_w, v_conv_dw_b, v_conv_ln_g,
                                v_conv_ln_b, v_conv_pw_w, v_conv_pw_b, v_attn_sinks, v_lru_conv_w, v_lru_conv_b,
                                v_lru_wa, v_lru_ba, v_lru_wx, v_lru_bx, v_lru_lambda, v_w_out, v_ln_post_g,
                                v_ln_post_b)))
    xi, yi, ci = _position()
    j = 2 * xi + yi

    g_meta, g_dw, g_lc = _gather_shards([meta_tokens, conv_dw_w, lru_conv_w])
    p = dict(w)
    p["w_in"] = [_cast_into_slot(w_in, l, j, "w_in") for l in range(DEPTH)]
    p["w_out"] = [_cast_into_slot(w_out, l, j, "w_out") for l in range(DEPTH)]
    p["conv_pw_w"] = [_cast_into_slot(conv_pw_w, l, j, "conv_pw_w") for l in range(DEPTH)]
    p["meta_tokens"] = g_meta.transpose(1, 0, 2).reshape(N_META, D)
    p["conv_dw_w"] = g_dw.transpose(1, 2, 0, 3).reshape(DEPTH, CONV_K, CW)
    p["lru_conv_w"] = g_lc.transpose(1, 2, 0, 3).reshape(DEPTH, LRU_K, LW)

    others = jnp.stack([jnp.where(j <= 0, 1, 0), jnp.where(j <= 1, 2, 1), jnp.where(j <= 2, 3, 2), ci]).astype(jnp.int32)
    me = 4 * xi + 2 * yi + ci
    order = jnp.stack([j, 2 * (1 - xi) + yi, 2 * xi + (1 - yi), 2 * (1 - xi) + (1 - yi)]).astype(jnp.int32)
    loss_part, grad_x, g = _device_step(x[0], loss_target[0], p, dist=(ci, j, others, me, order))
    loss = lax.psum(jnp.sum(loss_part), ("x", "y", "c"))
    big = {(name, l): g[name, l] for name in _BIG for l in range(DEPTH)}

    small_names = [n for n in _WEIGHTS if n not in _BIG]
    small_g = {}
    for names, red in ((_SMALL_LAYERED, g["pack_layered", -1]), (_SMALL_EMBED, g["pack_embed", -1])):
        red = red.reshape(-1)
        off = 0
        for n in names:
            fshape = list(w[n].shape)
            if n in _SMALL_SHARDED:
                fshape[_SMALL_SHARDED[n]] *= N_SHARD
            sz = 1
            for dim in fshape:
                sz *= dim
            full = red[off:off + sz].reshape(fshape)
            off += sz
            if n in _SMALL_SHARDED:
                ax = _SMALL_SHARDED[n]
                full = lax.dynamic_slice_in_dim(full, j * w[n].shape[ax], w[n].shape[ax], axis=ax)
            small_g[n] = full

    out_g, out_d, out_m, out_v = {}, {}, {}, {}
    for name in _BIG:
        shp = w[name].shape
        to3 = lambda a: a.reshape(DEPTH, -1, shp[-1])
        go, do, mo, vo = _adamw_big(to3(w[name]), big[name, 0], big[name, 1], to3(mom_m[name]), to3(mom_v[name]), name)
        out_g[name], out_d[name], out_m[name], out_v[name] = (a.reshape(shp) for a in (go, do, mo, vo))
    ds, ms, vs = _adamw_small([_as2d(w[n]) for n in small_names], [_as2d(small_g[n]) for n in small_names],
                              [_as2d(mom_m[n]) for n in small_names], [_as2d(mom_v[n]) for n in small_names])
    for n, d_, m_, v_ in zip(small_names, ds, ms, vs):
        out_g[n] = small_g[n]
        out_d[n], out_m[n], out_v[n] = d_.reshape(w[n].shape), m_.reshape(w[n].shape), v_.reshape(w[n].shape)

    return (loss, grad_x[None], *[out_g[n] for n in _WEIGHTS], *[out_d[n] for n in _WEIGHTS],
            *[out_m[n] for n in _WEIGHTS], *[out_v[n] for n in _WEIGHTS])
```

```python
import functools

import jax
import jax.numpy as jnp
from jax import lax
from jax.experimental import pallas as pl
from jax.experimental.pallas import tpu as pltpu

F32 = jnp.float32
BF16 = jnp.bfloat16

D = 2048
N_META = 16
CW = 512
CONV_K = 31
AW = 1024
KVW = 256
N_HEADS = 16
LW = 512
LRU_K = 4
LRU_C = 8.0
IN_TOTAL = 5120
ROT_HALF = 8
ROPE_THETA = 500000.0
LN_EPS = 1e-5
DEPTH = 2
ALPHA = (2.0 * DEPTH) ** 0.25
NEG_INF = -1e30
ADAM_LR, ADAM_B1, ADAM_B2, ADAM_EPS, ADAM_WD, ADAM_STEP = 0.001, 0.9, 0.999, 1e-08, 0.01, 10

BLK = 128
PAD = BLK - N_META
N_SHARD = 4
WIN_SH = IN_TOTAL // N_SHARD
WOUT_SH = D // N_SHARD
PW_SH = CW // N_SHARD
HALO = 32
LHALO = 8
V7X_VMEM_LIMIT = 60 * 1024 * 1024


def _cp(*sem):
    return pltpu.CompilerParams(dimension_semantics=sem if sem else None, vmem_limit_bytes=V7X_VMEM_LIMIT)


def _pick(total, prefs):
    for p in prefs:
        if total % p == 0:
            return p
    raise ValueError(f"no tile for {total}")


def _dot(a, b):
    return jnp.dot(a, b, preferred_element_type=F32)


def _dot_nt(a, b):
    return lax.dot_general(a, b, (((1,), (1,)), ((), ())), preferred_element_type=F32)


def _dot_tn(a, b):
    return lax.dot_general(a, b, (((0,), (0,)), ((), ())), preferred_element_type=F32)


def _sigmoid(x):
    return 1.0 / (1.0 + jnp.exp(-x))


def _silu_and_grad(x):
    s = _sigmoid(x)
    return x * s, s * (1.0 + x * (1.0 - s))


def _ln_rows(x, g, b):
    mu = jnp.mean(x, axis=-1, keepdims=True)
    xc = x - mu
    var = jnp.mean(xc * xc, axis=-1, keepdims=True)
    rstd = lax.rsqrt(var + LN_EPS)
    xhat = xc * rstd
    return xhat * g + b, xhat, rstd


def _ln_bwd_rows(dy, xhat, rstd, g):
    dxh = dy * g
    m1 = jnp.mean(dxh, axis=-1, keepdims=True)
    m2 = jnp.mean(dxh * xhat, axis=-1, keepdims=True)
    return rstd * (dxh - m1 - xhat * m2)


def _row_ids(n, base):
    return base + lax.broadcasted_iota(jnp.int32, (n, 1), 0)


def _colsum(x):
    return jnp.sum(x, axis=0, keepdims=True)


def _embed_fwd(x, meta, g, b, job=None):
    S = x.shape[0]
    nb = S // BLK + 1

    def body(x_ref, meta_ref, g_ref, b_ref, h_ref, hb_ref):
        n = pl.program_id(0)

        @pl.when(n == 0)
        def _():
            y, _, _ = _ln_rows(meta_ref[...], g_ref[...], b_ref[...])
            h_ref[...] = jnp.zeros_like(h_ref)
            h_ref[PAD:BLK, :] = y

        @pl.when(n > 0)
        def _():
            y, _, _ = _ln_rows(x_ref[...], g_ref[...], b_ref[...])
            h_ref[...] = y

        hb_ref[...] = h_ref[...].astype(BF16)

    return _side_call(
        body, job, name="embed_fwd", grid=(nb,),
        in_specs=[pl.BlockSpec((BLK, D), lambda n: (jnp.maximum(n - 1, 0), 0)),
                  pl.BlockSpec((N_META, D), lambda n: (0, 0)),
                  pl.BlockSpec((1, D), lambda n: (0, 0)),
                  pl.BlockSpec((1, D), lambda n: (0, 0))],
        out_specs=[pl.BlockSpec((BLK, D), lambda n: (n, 0)),
                   pl.BlockSpec((BLK, D), lambda n: (n, 0))],
        out_shape=[jax.ShapeDtypeStruct((nb * BLK, D), F32), jax.ShapeDtypeStruct((nb * BLK, D), BF16)],
        scratch_shapes=[], semantics=("arbitrary",), args=[x, meta, g, b])


def _embed_bwd(dh, x, meta, g, b):
    S = x.shape[0]
    nb = S // BLK + 1

    def body(dh_ref, x_ref, meta_ref, g_ref, b_ref, gx_ref, gm_ref, dg_ref, db_ref):
        n = pl.program_id(0)

        @pl.when(n == 0)
        def _():
            _, xhat, rstd = _ln_rows(meta_ref[...], g_ref[...], b_ref[...])
            dy = dh_ref[PAD:BLK, :]
            gm_ref[...] = _ln_bwd_rows(dy, xhat, rstd, g_ref[...])
            dg_ref[...] = _colsum(dy * xhat)
            db_ref[...] = _colsum(dy)

        @pl.when(n > 0)
        def _():
            _, xhat, rstd = _ln_rows(x_ref[...], g_ref[...], b_ref[...])
            dy = dh_ref[...]
            gx_ref[...] = _ln_bwd_rows(dy, xhat, rstd, g_ref[...])
            dg_ref[...] += _colsum(dy * xhat)
            db_ref[...] += _colsum(dy)

    prev = lambda n: (jnp.maximum(n - 1, 0), 0)
    const = lambda n: (0, 0)
    return pl.pallas_call(
        body, name="embed_bwd", grid=(nb,),
        in_specs=[pl.BlockSpec((BLK, D), lambda n: (n, 0)),
                  pl.BlockSpec((BLK, D), prev),
                  pl.BlockSpec((N_META, D), const),
                  pl.BlockSpec((1, D), const),
                  pl.BlockSpec((1, D), const)],
        out_specs=[pl.BlockSpec((BLK, D), prev),
                   pl.BlockSpec((N_META, D), const),
                   pl.BlockSpec((1, D), const),
                   pl.BlockSpec((1, D), const)],
        out_shape=[jax.ShapeDtypeStruct((S, D), F32), jax.ShapeDtypeStruct((N_META, D), F32),
                   jax.ShapeDtypeStruct((1, D), F32), jax.ShapeDtypeStruct((1, D), F32)],
        compiler_params=_cp("arbitrary"),
    )(dh, x, meta, g, b)


def _proj_fwd(hb, w_in, order, first, count, prev, l, job=None):
    T = hb.shape[0]
    tm = _pick(T, (1056, 384, 128))

    def body(o_sc, a_ref, w_ref, *rest):
        rest[-1][...] = _dot(a_ref[...], w_ref[...])

    return _side_call(
        body, job, name=f"proj_fwd{l}_{first}", grid=(T // tm, count),
        in_specs=[pl.BlockSpec((tm, D), lambda i, j, o: (i, 0)),
                  pl.BlockSpec((None, D, WIN_SH), lambda i, j, o: (o[first + j], 0, 0))]
        + ([] if prev is None else [pl.BlockSpec(memory_space=pl.ANY)]),
        out_specs=[pl.BlockSpec((tm, WIN_SH), lambda i, j, o: (i, o[first + j]))],
        out_shape=[jax.ShapeDtypeStruct((T, IN_TOTAL), F32)],
        scratch_shapes=[], semantics=("parallel", "arbitrary"),
        args=[hb, w_in] + ([] if prev is None else [prev]),
        aliases=None if prev is None else {2: 0}, prefetch=[order])


def _out_fwd(yc, ya, yl, w_out, h, g, b, l, job=None):
    T = h.shape[0]
    tm = _pick(T, (384, 128))

    def body(yc_ref, ya_ref, yl_ref, w_ref, h_ref, g_ref, b_ref, hn_ref, hnb_ref, xh_ref, rs_ref):
        acc = _dot(yc_ref[...], w_ref[0])
        acc += _dot(ya_ref[:, 0:WOUT_SH], w_ref[1])
        acc += _dot(ya_ref[:, WOUT_SH:2 * WOUT_SH], w_ref[2])
        acc += _dot(yl_ref[...], w_ref[3])
        z = ALPHA * h_ref[...] + acc
        y, xhat, rstd = _ln_rows(z, g_ref[...], b_ref[...])
        hn_ref[...] = y
        hnb_ref[...] = y.astype(BF16)
        xh_ref[...] = xhat
        rs_ref[...] = rstd

    row = lambda i: (i, 0)
    return _side_call(
        body, job, name=f"out_fwd{l}", grid=(T // tm,),
        in_specs=[pl.BlockSpec((tm, CW), row), pl.BlockSpec((tm, AW), row), pl.BlockSpec((tm, LW), row),
                  pl.BlockSpec((N_SHARD, WOUT_SH, D), lambda i: (0, 0, 0)),
                  pl.BlockSpec((tm, D), row),
                  pl.BlockSpec((None, 1, D), lambda i: (l, 0, 0)),
                  pl.BlockSpec((None, 1, D), lambda i: (l, 0, 0))],
        out_specs=[pl.BlockSpec((tm, D), row), pl.BlockSpec((tm, D), row), pl.BlockSpec((tm, D), row),
                   pl.BlockSpec((tm, 1), row)],
        out_shape=[jax.ShapeDtypeStruct((T, D), F32), jax.ShapeDtypeStruct((T, D), BF16),
                   jax.ShapeDtypeStruct((T, D), F32), jax.ShapeDtypeStruct((T, 1), F32)],
        scratch_shapes=[], semantics=("parallel",), args=[yc, ya, yl, w_out, h, g, b])


def _post_ln_dcat_bwd(src, target, xhat, rstd, g, w_out, conv, proj, cln_g, cln_b, pw_w, pw_b, l, job=None):
    T = src.shape[0]
    tm = _pick(T, (384, 128))
    per = tm // BLK if target is not None else 0
    last_blk = target.shape[0] // BLK - 1 if target is not None else 0

    def body(s_ref, *refs):
        t_refs = refs[:per]
        (xh_ref, rs_ref, g_ref, w_ref, conv_ref, ct_ref, cg_ref, cb_ref, pw_ref, pb_ref,
         part_ref, dz_ref, dzb_ref, dg_ref, db_ref, da_ref, dl_ref,
         dconv_ref, dct_ref, dpw_ref, dpb_ref, dcg_ref, dcb_ref) = refs[per:]
        i = pl.program_id(0)

        @pl.when(i == 0)
        def _():
            for ref in (part_ref, dg_ref, db_ref, dpw_ref, dpb_ref, dcg_ref, dcb_ref):
                ref[...] = jnp.zeros_like(ref)

        if per:
            tgt = jnp.concatenate([r[...] for r in t_refs], axis=0) if per > 1 else t_refs[0][...]
            real = _row_ids(tm, i * tm) >= BLK
            err = jnp.where(real, s_ref[...] - tgt, 0.0)
            part_ref[...] += _colsum(err * err) * (0.5 / D)
            dy = err * (1.0 / D)
        else:
            dy = s_ref[...]
        xhat = xh_ref[...]
        dz = _ln_bwd_rows(dy, xhat, rs_ref[...], g_ref[...])
        dzb = dz.astype(BF16)
        dz_ref[...] = dz
        dzb_ref[...] = dzb
        dg_ref[...] += _colsum(dy * xhat)
        db_ref[...] += _colsum(dy)
        da_ref[:, 0:WOUT_SH] = _dot_nt(dzb, w_ref[1])
        da_ref[:, WOUT_SH:2 * WOUT_SH] = _dot_nt(dzb, w_ref[2])
        dl_ref[...] = _dot_nt(dzb, w_ref[3])

        d_yc = _dot_nt(dzb, w_ref[0])
        u, chat, crstd = _ln_rows(conv_ref[...], cg_ref[...], cb_ref[...])
        s, ds_du = _silu_and_grad(u)
        sb = s.astype(BF16)
        cpw = _dot(sb, pw_ref[...]) + pb_ref[...]
        gate, dgate = _silu_and_grad(ct_ref[...])
        d_cpw = d_yc * gate
        dct_ref[...] = (d_yc * cpw * dgate).astype(BF16)
        d_cpw_b = d_cpw.astype(BF16)
        dpb_ref[...] += _colsum(d_cpw)
        dpw_ref[...] += _dot_tn(sb, d_cpw_b)
        du = _dot_nt(d_cpw_b, pw_ref[...]) * ds_du
        dconv_ref[...] = _ln_bwd_rows(du, chat, crstd, cg_ref[...])
        dcg_ref[...] += _colsum(du * chat)
        dcb_ref[...] += _colsum(du)

    row = lambda i: (i, 0)
    const = lambda i: (0, 0)
    vec = pl.BlockSpec((None, 1, CW), lambda i: (l, 0, 0))
    t_specs = [pl.BlockSpec((BLK, D), functools.partial(lambda i, q: (jnp.clip(i * per - 1 + q, 0, last_blk), 0), q=q))
               for q in range(per)]
    return _side_call(
        body, job, name=f"post_ln_dcat_bwd{l}", grid=(T // tm,),
        in_specs=[pl.BlockSpec((tm, D), row)] + t_specs + [
            pl.BlockSpec((tm, D), row), pl.BlockSpec((tm, 1), row), pl.BlockSpec((None, 1, D), lambda i: (l, 0, 0)),
            pl.BlockSpec((N_SHARD, WOUT_SH, D), lambda i: (0, 0, 0)),
            pl.BlockSpec((tm, CW), row), pl.BlockSpec((tm, CW), lambda i: (i, 2)), vec, vec,
            pl.BlockSpec((CW, CW), const), vec],
        out_specs=[pl.BlockSpec((1, D), const), pl.BlockSpec((tm, D), row), pl.BlockSpec((tm, D), row),
                   pl.BlockSpec((1, D), const), pl.BlockSpec((1, D), const),
                   pl.BlockSpec((tm, AW), row), pl.BlockSpec((tm, LW), row),
                   pl.BlockSpec((tm, CW), row), pl.BlockSpec((tm, CW), lambda i: (i, 2)),
                   pl.BlockSpec((CW, CW), const), pl.BlockSpec((1, CW), const),
                   pl.BlockSpec((1, CW), const), pl.BlockSpec((1, CW), const)],
        out_shape=[jax.ShapeDtypeStruct((1, D), F32), jax.ShapeDtypeStruct((T, D), F32),
                   jax.ShapeDtypeStruct((T, D), BF16), jax.ShapeDtypeStruct((1, D), F32),
                   jax.ShapeDtypeStruct((1, D), F32),
                   jax.ShapeDtypeStruct((T, AW), F32), jax.ShapeDtypeStruct((T, LW), F32),
                   jax.ShapeDtypeStruct((T, CW), F32), jax.ShapeDtypeStruct((T, IN_TOTAL), BF16),
                   jax.ShapeDtypeStruct((CW, CW), F32), jax.ShapeDtypeStruct((1, CW), F32),
                   jax.ShapeDtypeStruct((1, CW), F32), jax.ShapeDtypeStruct((1, CW), F32)],
        scratch_shapes=[], semantics=("arbitrary",),
        args=[src] + [target] * per + [xhat, rstd, g, w_out, conv, proj, cln_g, cln_b, pw_w, pw_b])


def _dwout_bwd(yc, ya, yl, dzb, l):
    T = dzb.shape[0]
    tm = _pick(T, (384, 128))

    def body(yc_ref, ya_ref, yl_ref, dz_ref, o_ref):
        @pl.when(pl.program_id(0) == 0)
        def _():
            o_ref[...] = jnp.zeros_like(o_ref)

        cat = jnp.concatenate([yc_ref[...], ya_ref[...], yl_ref[...]], axis=1)
        o_ref[...] += _dot_tn(cat, dz_ref[...])

    row = lambda t: (t, 0)
    out = pl.pallas_call(
        body, name=f"dwout_bwd{l}", grid=(T // tm,),
        in_specs=[pl.BlockSpec((tm, CW), row), pl.BlockSpec((tm, AW), row), pl.BlockSpec((tm, LW), row),
                  pl.BlockSpec((tm, D), row)],
        out_specs=pl.BlockSpec((D, D), lambda t: (0, 0)),
        out_shape=jax.ShapeDtypeStruct((D, D), F32),
        compiler_params=_cp("arbitrary"),
    )(yc, ya, yl, dzb)
    return out.reshape(N_SHARD, 2, WOUT_SH // 2, D)


def _dh_bwd(dproj, w_in, dz, l, job=None):
    T = dproj.shape[0]
    tm = _pick(T, (1056, 384, 128))

    n_w = len(w_in)

    def body(dp_ref, *refs):
        w_refs, (dz_ref, o_ref, acc_ref) = refs[:n_w], refs[n_w:]
        j = pl.program_id(1)

        @pl.when(j == 0)
        def _():
            acc_ref[...] = ALPHA * dz_ref[...]

        dp = dp_ref[...]
        off = 0
        for w_ref in w_refs:
            rows = w_ref.shape[0]
            acc_ref[:, off:off + rows] += _dot_nt(dp, w_ref[...])
            off += rows

        @pl.when(j == N_SHARD - 1)
        def _():
            o_ref[...] = acc_ref[...]

    return _side_call(
        body, job, name=f"dh_bwd{l}", grid=(T // tm, N_SHARD),
        in_specs=[pl.BlockSpec((tm, WIN_SH), lambda i, j: (i, j))]
        + [pl.BlockSpec((None, w.shape[1], WIN_SH), lambda i, j: (j, 0, 0)) for w in w_in]
        + [pl.BlockSpec((tm, D), lambda i, j: (i, 0))],
        out_specs=[pl.BlockSpec((tm, D), lambda i, j: (i, 0))],
        out_shape=[jax.ShapeDtypeStruct((T, D), F32)],
        scratch_shapes=[pltpu.VMEM((tm, D), F32)],
        semantics=("parallel", "arbitrary"), args=[dproj, *w_in, dz])


def _dwin_bwd(hb, dproj, l):
    T = hb.shape[0]
    tm = _pick(T, (1056, 384, 128))

    def body(h_ref, dp_ref, o_ref):
        @pl.when(pl.program_id(1) == 0)
        def _():
            o_ref[...] = jnp.zeros_like(o_ref)

        o_ref[...] += _dot_tn(h_ref[...], dp_ref[...])

    out = pl.pallas_call(
        body, name=f"dwin_bwd{l}", grid=(N_SHARD, T // tm),
        in_specs=[pl.BlockSpec((tm, D), lambda j, t: (t, 0)),
                  pl.BlockSpec((tm, WIN_SH), lambda j, t: (t, j))],
        out_specs=pl.BlockSpec((None, D, WIN_SH), lambda j, t: (j, 0, 0)),
        out_shape=jax.ShapeDtypeStruct((N_SHARD, D, WIN_SH), F32),
        compiler_params=_cp("parallel", "arbitrary"),
    )(hb, dproj)
    return out.reshape(N_SHARD, 2, D // 2, WIN_SH)


def _dwin_half(hb, dproj, which, l, tag, job=None):
    T = hb.shape[0]
    tm = _pick(T, (1056, 384, 128))
    hr = D // 2

    def body(w_ref, h_ref, dp_ref, o_ref):
        @pl.when(pl.program_id(1) == 0)
        def _():
            o_ref[...] = jnp.zeros_like(o_ref)

        o_ref[...] += _dot_tn(h_ref[...], dp_ref[...])

    return _side_call(
        body, job, name=f"dwin_{tag}{l}", grid=(N_SHARD, T // tm),
        in_specs=[pl.BlockSpec((tm, hr), lambda j, t, w: (t, w[0])),
                  pl.BlockSpec((tm, WIN_SH), lambda j, t, w: (t, j))],
        out_specs=[pl.BlockSpec((None, hr, WIN_SH), lambda j, t, w: (j, 0, 0))],
        out_shape=[jax.ShapeDtypeStruct((N_SHARD, hr, WIN_SH), F32)],
        scratch_shapes=[], semantics=("parallel", "arbitrary"), args=[hb, dproj],
        prefetch=[jnp.reshape(which, (1,)).astype(jnp.int32)])


def _glu_masked(v, g, base_row):
    rows = _row_ids(v.shape[0], base_row)
    return jnp.where(rows >= PAD, v * _sigmoid(g), 0.0)


def _conv_tile(T):
    return _pick(T, (384, 128))


SUBLANES = 8


def _for_each_shift(buf, rot, tm, offsets, fn):
    for r in range(SUBLANES):
        group = [o for o in offsets if o % SUBLANES == r]
        if not group:
            continue
        if r == 0:
            src = buf
        else:
            n = tm + max(group) - r
            rot[0:n, :] = buf[r:r + n, :]
            src = rot
        for o in group:
            fn(o, src[o - r:o - r + tm, :])


def _conv_fwd(proj, dw_w, dw_b, ln_g, ln_b, pw_w, pw_b, l, job=None):
    T = proj.shape[0]
    tm = _conv_tile(T)
    hb = tm // HALO

    def body(cv_ref, cg_ref, ct_ref, hv_ref, hg_ref, w_ref, b_ref, g_ref, be_ref, pw_ref, pb_ref,
             yc_ref, conv_ref, buf, rot):
        i = pl.program_id(0)
        buf[0:HALO, :] = _glu_masked(hv_ref[...], hg_ref[...], i * tm - HALO)
        buf[HALO:HALO + tm, :] = _glu_masked(cv_ref[...], cg_ref[...], i * tm)
        first = HALO - (CONV_K - 1)
        total = [jnp.zeros((tm, CW), F32) + b_ref[...]]

        def tap(o, tile):
            k = o - first
            total[0] = total[0] + w_ref[k:k + 1, :] * tile

        _for_each_shift(buf, rot, tm, [first + k for k in range(CONV_K)], tap)
        acc = total[0]
        conv_ref[...] = acc
        u, _, _ = _ln_rows(acc, g_ref[...], be_ref[...])
        s = u * _sigmoid(u)
        cpw = _dot(s.astype(BF16), pw_ref[...]) + pb_ref[...]
        gate, _ = _silu_and_grad(ct_ref[...])
        yc_ref[...] = (cpw * gate).astype(BF16)

    vec = pl.BlockSpec((None, 1, CW), lambda i: (l, 0, 0))
    return _side_call(
        body, job, name=f"conv_fwd{l}", grid=(T // tm,),
        in_specs=[pl.BlockSpec((tm, CW), lambda i: (i, 0)),
                  pl.BlockSpec((tm, CW), lambda i: (i, 1)),
                  pl.BlockSpec((tm, CW), lambda i: (i, 2)),
                  pl.BlockSpec((HALO, CW), lambda i: (jnp.maximum(i * hb - 1, 0), 0)),
                  pl.BlockSpec((HALO, CW), lambda i: (jnp.maximum(i * hb - 1, 0), 1)),
                  pl.BlockSpec((None, CONV_K, CW), lambda i: (l, 0, 0)),
                  vec, vec, vec,
                  pl.BlockSpec((CW, CW), lambda i: (0, 0)),
                  vec],
        out_specs=[pl.BlockSpec((tm, CW), lambda i: (i, 0)), pl.BlockSpec((tm, CW), lambda i: (i, 0))],
        out_shape=[jax.ShapeDtypeStruct((T, CW), BF16), jax.ShapeDtypeStruct((T, CW), F32)],
        scratch_shapes=[pltpu.VMEM((tm + HALO, CW), F32), pltpu.VMEM((tm + HALO, CW), F32)],
        semantics=("parallel",), args=[proj, proj, proj, proj, proj, dw_w, dw_b, ln_g, ln_b, pw_w, pw_b])


def _conv_bwd_taps(d_conv, proj, dw_w, dproj, l, job=None):
    T = d_conv.shape[0]
    tm = _conv_tile(T)
    hb = tm // HALO
    nt = T // tm
    last_halo = T // HALO - 1

    def body(dc_ref, dh_ref, cv_ref, cg_ref, hv_ref, hg_ref, w_ref, _, o_ref, dw_ref, dwb_ref, cbuf, dbuf, rot):
        i = pl.program_id(0)

        @pl.when(i == 0)
        def _():
            dw_ref[...] = jnp.zeros_like(dw_ref)
            dwb_ref[...] = jnp.zeros_like(dwb_ref)

        cbuf[0:HALO, :] = _glu_masked(hv_ref[...], hg_ref[...], i * tm - HALO)
        cbuf[HALO:HALO + tm, :] = _glu_masked(cv_ref[...], cg_ref[...], i * tm)
        dmain = dc_ref[...]
        dbuf[0:tm, :] = dmain
        dbuf[tm:tm + HALO, :] = jnp.where(i < nt - 1, dh_ref[...], 0.0)
        total = [jnp.zeros((tm, CW), F32)]

        def tap_back(o, tile):
            k = CONV_K - 1 - o
            total[0] = total[0] + w_ref[k:k + 1, :] * tile

        _for_each_shift(dbuf, rot, tm, list(range(CONV_K)), tap_back)
        acc = total[0]
        first = HALO - (CONV_K - 1)

        def tap_weight(o, tile):
            k = o - first
            dw_ref[k:k + 1, :] += _colsum(dmain * tile)

        _for_each_shift(cbuf, rot, tm, [first + k for k in range(CONV_K)], tap_weight)
        dwb_ref[...] += _colsum(dmain)
        d_c = jnp.where(_row_ids(tm, i * tm) >= PAD, acc, 0.0)
        sig = _sigmoid(cg_ref[...])
        o_ref[:, 0:CW] = (d_c * sig).astype(BF16)
        o_ref[:, CW:2 * CW] = (d_c * cv_ref[...] * sig * (1.0 - sig)).astype(BF16)

    const = lambda i: (0, 0)
    return _side_call(
        body, job, name=f"conv_bwd_taps{l}", grid=(nt,),
        in_specs=[pl.BlockSpec((tm, CW), lambda i: (i, 0)),
                  pl.BlockSpec((HALO, CW), lambda i: (jnp.minimum((i + 1) * hb, last_halo), 0)),
                  pl.BlockSpec((tm, CW), lambda i: (i, 0)),
                  pl.BlockSpec((tm, CW), lambda i: (i, 1)),
                  pl.BlockSpec((HALO, CW), lambda i: (jnp.maximum(i * hb - 1, 0), 0)),
                  pl.BlockSpec((HALO, CW), lambda i: (jnp.maximum(i * hb - 1, 0), 1)),
                  pl.BlockSpec((None, CONV_K, CW), lambda i: (l, 0, 0)),
                  pl.BlockSpec(memory_space=pl.ANY)],
        out_specs=[pl.BlockSpec((tm, 2 * CW), lambda i: (i, 0)),
                   pl.BlockSpec((HALO, CW), const), pl.BlockSpec((1, CW), const)],
        out_shape=[jax.ShapeDtypeStruct(dproj.shape, BF16), jax.ShapeDtypeStruct((HALO, CW), F32),
                   jax.ShapeDtypeStruct((1, CW), F32)],
        scratch_shapes=[pltpu.VMEM((tm + HALO, CW), F32), pltpu.VMEM((tm + HALO, CW), F32),
                        pltpu.VMEM((tm + HALO, CW), F32)],
        semantics=("arbitrary",), aliases={7: 0},
        args=[d_conv, d_conv, proj, proj, proj, proj, dw_w, dproj])


def _log1p_small(e):
    return jnp.where(e < 1e-3, e * (1.0 - e * (0.5 - e * (1.0 / 3.0))), jnp.log(1.0 + e))


def _softplus(z):
    return jnp.maximum(z, 0.0) + _log1p_small(jnp.exp(-jnp.abs(z)))


def _neg_expm1(x):
    series = -x * (1.0 + x * (1.0 / 2.0) * (1.0 + x * (1.0 / 3.0) * (1.0 + x * (1.0 / 4.0) * (
        1.0 + x * (1.0 / 5.0) * (1.0 + x * (1.0 / 6.0) * (1.0 + x * (1.0 / 7.0)))))))
    return jnp.where(x > -0.25, series, 1.0 - jnp.exp(x))


def _lru_gates(rxbuf, tm, base_row, lw_ref, lb_ref, wa_ref, ba_ref, wx_ref, bx_ref, lam_ref):
    rc = jnp.zeros((tm, LW), F32) + lb_ref[...]
    for k in range(LRU_K):
        o = LHALO - (LRU_K - 1) + k
        rc += lw_ref[k:k + 1, :] * rxbuf[o:o + tm, :]
    rcb = rc.astype(BF16)
    r = _sigmoid(_dot(rcb, wa_ref[...]) + ba_ref[...])
    ig = _sigmoid(_dot(rcb, wx_ref[...]) + bx_ref[...])
    sp = _softplus(-lam_ref[...])
    la = -LRU_C * r * sp
    a = jnp.exp(la)
    mult = jnp.sqrt(_neg_expm1(2.0 * la))
    valid = _row_ids(tm, base_row) >= PAD
    return rc, rcb, r, ig, sp, a, mult, valid


def _mask_rows(v, base_row):
    return jnp.where(_row_ids(v.shape[0], base_row) >= PAD, v, 0.0)


def _scan_rows(aa, bb, carry, out_ref, reverse):
    tm = aa.shape[0]
    sub = _row_ids(tm, 0) & (SUBLANES - 1)
    s = 1
    while s < SUBLANES:
        keep = (sub < SUBLANES - s) if reverse else (sub >= s)
        shift = tm - s if reverse else s
        a_s = jnp.where(keep, pltpu.roll(aa, shift, axis=0), 1.0)
        b_s = jnp.where(keep, pltpu.roll(bb, shift, axis=0), 0.0)
        bb = aa * b_s + bb
        aa = aa * a_s
        s *= 2
    groups = range(tm // SUBLANES)
    edge = 0 if reverse else SUBLANES - 1
    for j in (reversed(groups) if reverse else groups):
        rows = slice(SUBLANES * j, SUBLANES * j + SUBLANES)
        x = bb[rows] + aa[rows] * carry
        out_ref[rows, :] = x
        carry = x[edge:edge + 1]


def _lru_tile(T):
    return _pick(T, (384, 128))


def _lru_fwd(proj, lw, lb, wa, ba, wx, bx, lam, l, job=None):
    T = proj.shape[0]
    tm = _lru_tile(T)
    hb = tm // LHALO

    def body(rx_ref, rg_ref, hx_ref, lw_ref, lb_ref, wa_ref, ba_ref, wx_ref, bx_ref, lam_ref,
             yl_ref, hl_ref, rxbuf, carry):
        i = pl.program_id(0)

        @pl.when(i == 0)
        def _():
            carry[...] = jnp.zeros_like(carry)

        rxbuf[0:LHALO, :] = _mask_rows(hx_ref[...], i * tm - LHALO)
        rxbuf[LHALO:LHALO + tm, :] = _mask_rows(rx_ref[...], i * tm)
        rc, _, _, ig, _, a, mult, valid = _lru_gates(rxbuf, tm, i * tm, lw_ref, lb_ref, wa_ref, ba_ref,
                                                     wx_ref, bx_ref, lam_ref)
        bb = jnp.where(valid, mult * (ig * rc), 0.0)
        _scan_rows(a, bb, carry[0:1, :], hl_ref, reverse=False)
        carry[0:1, :] = hl_ref[tm - 1:tm, :]
        gate, _ = _silu_and_grad(rg_ref[...])
        yl_ref[...] = (hl_ref[...] * gate).astype(BF16)

    vec = pl.BlockSpec((None, 1, LW), lambda i: (l, 0, 0))
    mat = pl.BlockSpec((None, LW, LW), lambda i: (l, 0, 0))
    return _side_call(
        body, job, name=f"lru_fwd{l}", grid=(T // tm,),
        in_specs=[pl.BlockSpec((tm, LW), lambda i: (i, 8)),
                  pl.BlockSpec((tm, LW), lambda i: (i, 9)),
                  pl.BlockSpec((LHALO, LW), lambda i: (jnp.maximum(i * hb - 1, 0), 8)),
                  pl.BlockSpec((None, LRU_K, LW), lambda i: (l, 0, 0)),
                  vec, mat, vec, mat, vec, vec],
        out_specs=[pl.BlockSpec((tm, LW), lambda i: (i, 0)), pl.BlockSpec((tm, LW), lambda i: (i, 0))],
        out_shape=[jax.ShapeDtypeStruct((T, LW), BF16), jax.ShapeDtypeStruct((T, LW), F32)],
        scratch_shapes=[pltpu.VMEM((tm + LHALO, LW), F32), pltpu.VMEM((8, LW), F32)],
        semantics=("arbitrary",), args=[proj, proj, proj, lw, lb, wa, ba, wx, bx, lam])


def _lru_bwd(proj, hl, d_yl, lw, lb, wa, ba, wx, bx, lam, dproj, l, job=None):
    T = proj.shape[0]
    tm = _lru_tile(T)
    hb = tm // LHALO
    nt = T // tm

    def body(rx_ref, rg_ref, hx_ref, hl_ref, hh_ref, dy_ref, lw_ref, lb_ref, wa_ref, ba_ref, wx_ref, bx_ref,
             lam_ref, _, o_ref, dlw_ref, dlb_ref, dwa_ref, dba_ref, dwx_ref, dbx_ref, dlam_ref,
             rxbuf, dbuf, carry, head, gbuf):
        step = pl.program_id(0)
        i = nt - 1 - step

        @pl.when(step == 0)
        def _():
            carry[...] = jnp.zeros_like(carry)
            head[...] = jnp.zeros_like(head)
            for ref in (dlw_ref, dlb_ref, dwa_ref, dba_ref, dwx_ref, dbx_ref, dlam_ref):
                ref[...] = jnp.zeros_like(ref)

        rxbuf[0:LHALO, :] = _mask_rows(hx_ref[...], i * tm - LHALO)
        rxbuf[LHALO:LHALO + tm, :] = _mask_rows(rx_ref[...], i * tm)
        rc, rcb, r, ig, sp, a, mult, valid = _lru_gates(rxbuf, tm, i * tm, lw_ref, lb_ref, wa_ref, ba_ref,
                                                        wx_ref, bx_ref, lam_ref)
        rows = _row_ids(tm, 0)
        h = hl_ref[...]
        h_before = jnp.where(i > 0, hh_ref[LHALO - 1:LHALO, :], 0.0)
        hprev = jnp.where(rows == 0, h_before, pltpu.roll(h, 1, axis=0))
        rg = rg_ref[...]
        gate, dgate = _silu_and_grad(rg)
        dy = dy_ref[...]
        o_ref[:, LW:2 * LW] = (dy * h * dgate).astype(BF16)
        bb = dy * gate + jnp.where(rows == tm - 1, carry[0:1, :], 0.0)
        aa = jnp.where(rows == tm - 1, 0.0, pltpu.roll(a, tm - 1, axis=0))
        _scan_rows(aa, bb, jnp.zeros((1, LW), F32), gbuf, reverse=True)
        g = gbuf[...]
        dbuf[0:tm, :] = a * g
        carry[0:1, :] = dbuf[0:1, :]
        du = jnp.where(valid, g, 0.0)
        da = g * hprev
        dix = du * mult
        dmult = du * (ig * rc)
        dla = jnp.where(valid, da * a - dmult * (a * a) / mult, 0.0)
        dr = dla * (-LRU_C * sp)
        dlam_ref[...] += _colsum(dla * (LRU_C * r)) * _sigmoid(-lam_ref[...])
        dpa = dr * r * (1.0 - r)
        dpx = (dix * rc) * ig * (1.0 - ig)
        dpab = dpa.astype(BF16)
        dpxb = dpx.astype(BF16)
        dba_ref[...] += _colsum(dpa)
        dbx_ref[...] += _colsum(dpx)
        dwa_ref[...] += _dot_tn(rcb, dpab)
        dwx_ref[...] += _dot_tn(rcb, dpxb)
        drc = dix * ig + _dot_nt(dpab, wa_ref[...]) + _dot_nt(dpxb, wx_ref[...])
        dbuf[0:tm, :] = drc
        dbuf[tm:tm + LHALO, :] = head[...]
        acc = jnp.zeros((tm, LW), F32)
        for k in range(LRU_K):
            o = LRU_K - 1 - k
            acc += lw_ref[k:k + 1, :] * dbuf[o:o + tm, :]
            oc = LHALO - (LRU_K - 1) + k
            dlw_ref[k:k + 1, :] += _colsum(drc * rxbuf[oc:oc + tm, :])
        dlb_ref[...] += _colsum(drc)
        head[...] = dbuf[0:LHALO, :]
        o_ref[:, 0:LW] = jnp.where(valid, acc, 0.0).astype(BF16)

    rev = lambda s: nt - 1 - s
    vec = pl.BlockSpec((None, 1, LW), lambda s: (l, 0, 0))
    mat = pl.BlockSpec((None, LW, LW), lambda s: (l, 0, 0))
    const = lambda s: (0, 0)
    halo = lambda s: jnp.maximum(rev(s) * hb - 1, 0)
    return _side_call(
        body, job, name=f"lru_bwd{l}", grid=(nt,),
        in_specs=[pl.BlockSpec((tm, LW), lambda s: (rev(s), 8)),
                  pl.BlockSpec((tm, LW), lambda s: (rev(s), 9)),
                  pl.BlockSpec((LHALO, LW), lambda s: (halo(s), 8)),
                  pl.BlockSpec((tm, LW), lambda s: (rev(s), 0)),
                  pl.BlockSpec((LHALO, LW), lambda s: (halo(s), 0)),
                  pl.BlockSpec((tm, LW), lambda s: (rev(s), 0)),
                  pl.BlockSpec((None, LRU_K, LW), lambda s: (l, 0, 0)),
                  vec, mat, vec, mat, vec, vec, pl.BlockSpec(memory_space=pl.ANY)],
        out_specs=[pl.BlockSpec((tm, 2 * LW), lambda s: (rev(s), 4)),
                   pl.BlockSpec((8, LW), const), pl.BlockSpec((1, LW), const),
                   pl.BlockSpec((LW, LW), const), pl.BlockSpec((1, LW), const),
                   pl.BlockSpec((LW, LW), const), pl.BlockSpec((1, LW), const),
                   pl.BlockSpec((1, LW), const)],
        out_shape=[jax.ShapeDtypeStruct(dproj.shape, BF16),
                   jax.ShapeDtypeStruct((8, LW), F32), jax.ShapeDtypeStruct((1, LW), F32),
                   jax.ShapeDtypeStruct((LW, LW), F32), jax.ShapeDtypeStruct((1, LW), F32),
                   jax.ShapeDtypeStruct((LW, LW), F32), jax.ShapeDtypeStruct((1, LW), F32),
                   jax.ShapeDtypeStruct((1, LW), F32)],
        scratch_shapes=[pltpu.VMEM((tm + LHALO, LW), F32), pltpu.VMEM((tm + LHALO, LW), F32),
                        pltpu.VMEM((8, LW), F32), pltpu.VMEM((LHALO, LW), F32), pltpu.VMEM((tm, LW), F32)],
        semantics=("arbitrary",), aliases={13: 0},
        args=[proj, proj, proj, hl, hl, d_yl, lw, lb, wa, ba, wx, bx, lam, dproj])


def _rope_tables(T):
    pos = (lax.broadcasted_iota(jnp.int32, (T, 128), 0) - PAD).astype(F32)
    lane = lax.broadcasted_iota(jnp.int32, (T, 128), 1) % 64
    inv_freq = ROPE_THETA ** (-(lane % ROT_HALF).astype(F32) / ROT_HALF)
    ang = pos * inv_freq
    cos, sin = jnp.cos(ang), jnp.sin(ang)
    c = jnp.where(lane < 2 * ROT_HALF, cos, 1.0)
    s1 = jnp.where(lane < ROT_HALF, -sin, 0.0)
    s2 = jnp.where((lane >= ROT_HALF) & (lane < 2 * ROT_HALF), sin, 0.0)
    return c, s1, s2


def _rot_fwd(x, c, s1, s2):
    return x * c + pltpu.roll(x, 128 - ROT_HALF, axis=1) * s1 + pltpu.roll(x, ROT_HALF, axis=1) * s2


def _rot_bwd(dy, c, s1, s2):
    return dy * c + pltpu.roll(dy * s1, ROT_HALF, axis=1) + pltpu.roll(dy * s2, 128 - ROT_HALF, axis=1)


KV2 = 2 * KVW


def _rope_fwd(proj, tabs, l):
    T = proj.shape[0]

    def both_halves(x, o_ref, pg):
        lane = lax.broadcasted_iota(jnp.int32, (1, 128), 1)
        for off in range(2):
            half = jnp.where((lane < 64) if off == 0 else (lane >= 64), x, 0.0)
            g = 2 * pg + off
            o_ref[:, 128 * g:128 * g + 128] = (half + pltpu.roll(half, 64, axis=1)).astype(BF16)

    def body(ql_ref, qh_ref, k_ref, v_ref, c_ref, s1_ref, s2_ref, qr_ref, kr_ref, vb_ref, kr2_ref, vb2_ref):
        c, s1, s2 = c_ref[...], s1_ref[...], s2_ref[...]
        for gcol in range(AW // 128):
            src = ql_ref if gcol < 4 else qh_ref
            x = src[:, 128 * (gcol % 4):128 * (gcol % 4) + 128]
            qr_ref[:, 128 * gcol:128 * gcol + 128] = (_rot_fwd(x, c, s1, s2) * 0.125).astype(BF16)
        for pg in range(KVW // 128):
            cols = slice(128 * pg, 128 * pg + 128)
            k = _rot_fwd(k_ref[:, cols], c, s1, s2)
            kr_ref[:, cols] = k.astype(BF16)
            both_halves(k, kr2_ref, pg)
            vb_ref[:, cols] = v_ref[:, cols].astype(BF16)
            both_halves(v_ref[:, cols], vb2_ref, pg)

    tr = _pick(T, (384, 128))
    tab = pl.BlockSpec((tr, 128), lambda n: (n, 0))
    return pl.pallas_call(
        body, name=f"rope_fwd{l}", grid=(T // tr,),
        in_specs=[pl.BlockSpec((tr, 512), lambda n: (n, 3)), pl.BlockSpec((tr, 512), lambda n: (n, 4)),
                  pl.BlockSpec((tr, KVW), lambda n: (n, 10)), pl.BlockSpec((tr, KVW), lambda n: (n, 11)),
                  tab, tab, tab],
        out_specs=[pl.BlockSpec((tr, AW), lambda n: (n, 0)), pl.BlockSpec((tr, KVW), lambda n: (n, 0)),
                   pl.BlockSpec((tr, KVW), lambda n: (n, 0)), pl.BlockSpec((tr, KV2), lambda n: (n, 0)),
                   pl.BlockSpec((tr, KV2), lambda n: (n, 0))],
        out_shape=[jax.ShapeDtypeStruct((T, AW), BF16), jax.ShapeDtypeStruct((T, KVW), BF16),
                   jax.ShapeDtypeStruct((T, KVW), BF16), jax.ShapeDtypeStruct((T, KV2), BF16),
                   jax.ShapeDtypeStruct((T, KV2), BF16)],
        compiler_params=_cp("parallel"),
    )(proj, proj, proj, proj, *tabs)


GROUP = 4


def _attn_mask(n, reps):
    qi = lax.broadcasted_iota(jnp.int32, (reps * BLK, BLK), 0) & (BLK - 1)
    kj = lax.broadcasted_iota(jnp.int32, (reps * BLK, BLK), 1)
    m0 = (kj >= PAD) & (n >= 1)
    mp = (kj > qi) & (n >= 2)
    mc = (kj <= qi) & ((n >= 1) | (kj >= PAD))
    return jnp.concatenate([m0, mp, mc], axis=1)


def _kv_both(x0_ref, xp_ref, xc_ref, g):
    if x0_ref.shape[1] == KV2:
        cols = slice(128 * g, 128 * g + 128)
        return jnp.concatenate([x0_ref[:, cols], xp_ref[:, cols], xc_ref[:, cols]], axis=0)
    pg, off = g // 2, g % 2
    cols = slice(128 * pg, 128 * pg + 128)
    x = jnp.concatenate([x0_ref[:, cols], xp_ref[:, cols], xc_ref[:, cols]], axis=0).astype(F32)
    lane = lax.broadcasted_iota(jnp.int32, (1, 128), 1)
    half = jnp.where((lane < 64) if off == 0 else (lane >= 64), x, 0.0)
    return (half + pltpu.roll(half, 64, axis=1)).astype(BF16)


def _stack_heads(a, b):
    lo = lax.broadcasted_iota(jnp.int32, (1, 128), 1) < 64
    a, b = a.astype(F32), b.astype(F32)
    return jnp.concatenate([jnp.where(lo, a, 0.0), jnp.where(lo, 0.0, a),
                            jnp.where(lo, b, 0.0), jnp.where(lo, 0.0, b)], axis=0).astype(BF16)


def _unstack_heads(x):
    lo = lax.broadcasted_iota(jnp.int32, (1, 128), 1) < 64
    return (jnp.where(lo, x[0:BLK], x[BLK:2 * BLK]), jnp.where(lo, x[2 * BLK:3 * BLK], x[3 * BLK:4 * BLK]))


def _per_head_column(values):
    return jnp.concatenate([jnp.zeros((BLK, 1), F32) + v for v in values], axis=0)


def _attn_fwd(qr, kr, vb, proj, sinks, l, job=None):
    T = qr.shape[0]

    def body(sink_ref, q_ref, k0_ref, kp_ref, kc_ref, v0_ref, vp_ref, vc_ref, ag_ref, ya_ref, att_ref, lse_ref):
        n = pl.program_id(0)
        mask = _attn_mask(n, 1)
        lane = lax.broadcasted_iota(jnp.int32, (1, 128), 1)
        lse_acc = jnp.zeros((BLK, 128), F32)
        for g in range(4):
            kx = _kv_both(k0_ref, kp_ref, kc_ref, g)
            vx = _kv_both(v0_ref, vp_ref, vc_ref, g)
            pair_cols = [slice(128 * (2 * g + pp), 128 * (2 * g + pp) + 128) for pp in range(2)]
            s4 = _dot_nt(_stack_heads(q_ref[:, pair_cols[0]], q_ref[:, pair_cols[1]]), kx)
            probs = []
            for r in range(GROUP):
                h = GROUP * g + r
                sink = sink_ref[l, h]
                s = jnp.where(mask, s4[BLK * r:BLK * r + BLK], NEG_INF)
                m = jnp.maximum(jnp.max(s, axis=1, keepdims=True), sink)
                p = jnp.exp(s - m)
                denom = jnp.sum(p, axis=1, keepdims=True) + jnp.exp(sink - m)
                probs.append((p * (1.0 / denom)).astype(BF16))
                lse_acc = jnp.where(lane == h, m + jnp.log(denom), lse_acc)
            outs = _unstack_heads(_dot(jnp.concatenate(probs, axis=0), vx))
            for cols, out in zip(pair_cols, outs):
                att_ref[:, cols] = out
                gate, _ = _silu_and_grad(ag_ref[:, cols])
                ya_ref[:, cols] = (out * gate).astype(BF16)
        lse_ref[...] = lse_acc

    prev = lambda n: (jnp.maximum(n - 1, 0), 0)
    cur = lambda n: (n, 0)
    zero = lambda n: (0, 0)
    kv = lambda f: pl.BlockSpec((BLK, KV2), f)
    return _side_call(
        body, job, name=f"attn_fwd{l}", grid=(T // BLK,),
        in_specs=[pl.BlockSpec(memory_space=pltpu.SMEM),
                  pl.BlockSpec((BLK, AW), cur), kv(zero), kv(prev), kv(cur), kv(zero), kv(prev), kv(cur),
                  pl.BlockSpec((BLK, AW), lambda n: (n, 3))],
        out_specs=[pl.BlockSpec((BLK, AW), cur), pl.BlockSpec((BLK, AW), cur), pl.BlockSpec((BLK, 128), cur)],
        out_shape=[jax.ShapeDtypeStruct((T, AW), BF16), jax.ShapeDtypeStruct((T, AW), F32),
                   jax.ShapeDtypeStruct((T, 128), F32)],
        scratch_shapes=[], semantics=("parallel",), args=[sinks, qr, kr, kr, kr, vb, vb, vb, proj])


def _attn_bwd(qr, kr, vb, proj, att, lse, d_ya, sinks, dproj, l, job=None):
    T = qr.shape[0]
    nb = T // BLK

    def body(sink_ref, q_ref, k0_ref, kp_ref, kc_ref, v0_ref, vp_ref, vc_ref, ag_ref, att_ref, lse_ref, dy_ref, _,
             dq_ref, dk_ref, dv_ref, dk0_ref, dv0_ref, dag_ref, dsink_ref, kcarry, vcarry):
        n = pl.program_id(0)

        @pl.when(n == 0)
        def _():
            dk0_ref[...] = jnp.zeros_like(dk0_ref)
            dv0_ref[...] = jnp.zeros_like(dv0_ref)
            dsink_ref[...] = jnp.zeros_like(dsink_ref)
            kcarry[...] = jnp.zeros_like(kcarry)
            vcarry[...] = jnp.zeros_like(vcarry)

        @pl.when(n == nb)
        def _():
            dk_ref[...] = kcarry[...]
            dv_ref[...] = vcarry[...]

        @pl.when(n < nb)
        def _():
            mask = _attn_mask(n, GROUP)
            lane = lax.broadcasted_iota(jnp.int32, (1, 128), 1)
            lse = lse_ref[...]
            dsink = jnp.zeros((1, 128), F32)
            dk_pg, dv_pg = [], []
            for pg in range(2):
                dk_acc = jnp.zeros((3 * BLK, 128), F32)
                dv_acc = jnp.zeros((3 * BLK, 128), F32)
                for off in range(2):
                    g = 2 * pg + off
                    kx = _kv_both(k0_ref, kp_ref, kc_ref, g)
                    vx = _kv_both(v0_ref, vp_ref, vc_ref, g)
                    pair_cols = [slice(128 * (2 * g + pp), 128 * (2 * g + pp) + 128) for pp in range(2)]
                    q4 = _stack_heads(q_ref[:, pair_cols[0]], q_ref[:, pair_cols[1]])
                    d_out = []
                    for cols in pair_cols:
                        gate, dgate = _silu_and_grad(ag_ref[:, cols])
                        dy = dy_ref[:, cols]
                        dag_ref[:, cols] = (dy * att_ref[:, cols] * dgate).astype(BF16)
                        d_out.append(dy * gate)
                    do4 = _stack_heads(d_out[0], d_out[1])
                    heads = [GROUP * g + r for r in range(GROUP)]
                    sink = _per_head_column([sink_ref[l, h] for h in heads])
                    lse4 = _per_head_column(
                        [jnp.sum(jnp.where(lane == h, lse, 0.0), axis=1, keepdims=True) for h in heads])
                    p = jnp.where(mask, jnp.exp(_dot_nt(q4, kx) - lse4), 0.0)
                    dp = _dot_nt(do4, vx)
                    delta = jnp.sum(p * dp, axis=1, keepdims=True)
                    ds = (p * (dp - delta)).astype(BF16)
                    sink_term = jnp.exp(sink - lse4) * delta
                    for r, h in enumerate(heads):
                        dsink += jnp.where(lane == h, -jnp.sum(sink_term[BLK * r:BLK * r + BLK]), 0.0)
                    for cols, dq in zip(pair_cols, _unstack_heads(_dot(ds, kx))):
                        dq_ref[:, cols] = dq
                    dkg = _dot_tn(ds, q4)
                    dvg = _dot_tn(p.astype(BF16), do4)
                    own = (lane < 64) if off == 0 else (lane >= 64)
                    dk_acc += jnp.where(own, dkg + pltpu.roll(dkg, 64, axis=1), 0.0)
                    dv_acc += jnp.where(own, dvg + pltpu.roll(dvg, 64, axis=1), 0.0)
                dk_pg.append(dk_acc)
                dv_pg.append(dv_acc)
            dsink_ref[...] += dsink
            for pg in range(2):
                cols = slice(128 * pg, 128 * pg + 128)
                dk0_ref[:, cols] += dk_pg[pg][0:BLK]
                dv0_ref[:, cols] += dv_pg[pg][0:BLK]
                dk_ref[:, cols] = kcarry[:, cols] + dk_pg[pg][BLK:2 * BLK]
                dv_ref[:, cols] = vcarry[:, cols] + dv_pg[pg][BLK:2 * BLK]
                kcarry[:, cols] = dk_pg[pg][2 * BLK:3 * BLK]
                vcarry[:, cols] = dv_pg[pg][2 * BLK:3 * BLK]

    last = nb - 1
    cur = lambda n: (jnp.minimum(n, last), 0)
    prev = lambda n: (jnp.clip(n - 1, 0, last), 0)
    zero = lambda n: (0, 0)
    kv = lambda f: pl.BlockSpec((BLK, KVW), f)
    wide = lambda f: pl.BlockSpec((BLK, AW), f)
    return _side_call(
        body, job, name=f"attn_bwd{l}", grid=(nb + 1,),
        in_specs=[pl.BlockSpec(memory_space=pltpu.SMEM),
                  wide(cur), kv(zero), kv(prev), kv(cur), kv(zero), kv(prev), kv(cur),
                  pl.BlockSpec((BLK, AW), lambda n: (jnp.minimum(n, last), 3)),
                  wide(cur), pl.BlockSpec((BLK, 128), cur), wide(cur), pl.BlockSpec(memory_space=pl.ANY)],
        out_specs=[wide(cur), kv(prev), kv(prev), kv(zero), kv(zero),
                   pl.BlockSpec((BLK, AW), lambda n: (jnp.minimum(n, last), 3)),
                   pl.BlockSpec((1, 128), zero)],
        out_shape=[jax.ShapeDtypeStruct((T, AW), F32), jax.ShapeDtypeStruct((T, KVW), F32),
                   jax.ShapeDtypeStruct((T, KVW), F32), jax.ShapeDtypeStruct((BLK, KVW), F32),
                   jax.ShapeDtypeStruct((BLK, KVW), F32), jax.ShapeDtypeStruct(dproj.shape, BF16),
                   jax.ShapeDtypeStruct((1, 128), F32)],
        scratch_shapes=[pltpu.VMEM((BLK, KVW), F32), pltpu.VMEM((BLK, KVW), F32)],
        semantics=("arbitrary",), aliases={12: 5},
        args=[sinks, qr, kr, kr, kr, vb, vb, vb, proj, att, lse, d_ya, dproj])


def _rope_bwd(dqr, dk, dv, dk0, dv0, tabs, dproj, l):
    T = dqr.shape[0]

    def body(dq_ref, dk_ref, dv_ref, dk0_ref, dv0_ref, c_ref, s1_ref, s2_ref, _, o_ref):
        n = pl.program_id(0)
        c, s1, s2 = c_ref[...], s1_ref[...], s2_ref[...]
        for gcol in range(AW // 128):
            cols = slice(128 * gcol, 128 * gcol + 128)
            o_ref[:, cols] = (_rot_bwd(dq_ref[:, cols], c, s1, s2) * 0.125).astype(BF16)
        for gcol in range(KVW // 128):
            cols = slice(128 * gcol, 128 * gcol + 128)
            kcols = slice(AW + 128 * gcol, AW + 128 * gcol + 128)
            vcols = slice(AW + KVW + 128 * gcol, AW + KVW + 128 * gcol + 128)
            o_ref[:, kcols] = _rot_bwd(dk_ref[:, cols], c, s1, s2).astype(BF16)
            o_ref[:, vcols] = dv_ref[:, cols].astype(BF16)

            @pl.when(n == 0)
            def _():
                dkk = dk_ref[0:BLK, cols] + dk0_ref[:, cols]
                o_ref[0:BLK, kcols] = _rot_bwd(dkk, c[0:BLK], s1[0:BLK], s2[0:BLK]).astype(BF16)
                o_ref[0:BLK, vcols] = (dv_ref[0:BLK, cols] + dv0_ref[:, cols]).astype(BF16)

    tr = _pick(T, (384, 128))
    cur = lambda n: (n, 0)
    zero = lambda n: (0, 0)
    tab = pl.BlockSpec((tr, 128), cur)
    return pl.pallas_call(
        body, name=f"rope_bwd{l}", grid=(T // tr,),
        in_specs=[pl.BlockSpec((tr, AW), cur), pl.BlockSpec((tr, KVW), cur), pl.BlockSpec((tr, KVW), cur),
                  pl.BlockSpec((BLK, KVW), zero), pl.BlockSpec((BLK, KVW), zero), tab, tab, tab,
                  pl.BlockSpec(memory_space=pl.ANY)],
        out_specs=pl.BlockSpec((tr, AW + 2 * KVW), lambda n: (n, 1)),
        out_shape=jax.ShapeDtypeStruct(dproj.shape, BF16),
        input_output_aliases={8: 0},
        compiler_params=_cp("parallel"),
    )(dqr, dk, dv, dk0, dv0, *tabs, dproj)


def _block_diag(w):
    nl, nh, hd, _ = w.shape
    eye = jnp.eye(nh, dtype=w.dtype)
    return jnp.einsum("lhij,hg->lhigj", w, eye).reshape(nl, nh * hd, nh * hd)


def _diag_blocks(m):
    nh, hd = 8, 64
    return jnp.einsum("hihj->hij", m.reshape(nh, hd, nh, hd))


def _device_step(x, target, p, dist=None):
    vec = lambda a: a.reshape(DEPTH, 1, a.shape[-1])
    ln_in_g, ln_in_b = p["ln_in_g"].reshape(1, D), p["ln_in_b"].reshape(1, D)
    conv_dw_b, conv_ln_g, conv_ln_b, conv_pw_b = map(vec, (p["conv_dw_b"], p["conv_ln_g"], p["conv_ln_b"], p["conv_pw_b"]))
    lru_conv_b, lru_ba, lru_bx, lru_lambda = map(vec, (p["lru_conv_b"], p["lru_ba"], p["lru_bx"], p["lru_lambda"]))
    ln_post_g, ln_post_b = vec(p["ln_post_g"]), vec(p["ln_post_b"])
    wa_bd = _block_diag(p["lru_wa"]).astype(BF16)
    wx_bd = _block_diag(p["lru_wx"]).astype(BF16)
    w_in, w_out, pw_w = list(p["w_in"]), list(p["w_out"]), list(p["conv_pw_w"])
    sinks = p["attn_sinks"]
    big_names = ("w_in", "w_out", "conv_pw_w")

    order = dist[4] if dist else jnp.arange(N_SHARD, dtype=jnp.int32)
    (h, hb), got = _embed_fwd(x, p["meta_tokens"], ln_in_g, ln_in_b,
                              job=_gather_job([w_in[0]], peers=(0, 1)) if dist else None)
    if dist:
        w_in[0] = got[0]
    T = h.shape[0]
    tabs = _rope_tables(T)
    saved = []
    for l in range(DEPTH):
        if l == 0:
            job = _join_jobs(_gather_job([w_in[0]], peers=(2,)), _gather_job([pw_w[0]])) if dist else None
            (proj,), got = _proj_fwd(hb, w_in[0], order, 0, N_SHARD - 1, None, l, job=job)
            if dist:
                w_in[0], pw_w[0] = got
            (proj,), _ = _proj_fwd(hb, w_in[0], order, N_SHARD - 1, 1, proj, l)
        else:
            (proj,), _ = _proj_fwd(hb, w_in[l], order, 0, N_SHARD, None, l)
        pw_l = pw_w[l].reshape(CW, CW)
        (yc, conv), got = _conv_fwd(proj, p["conv_dw_w"], conv_dw_b, conv_ln_g, conv_ln_b, pw_l, conv_pw_b, l,
                                    job=_gather_job([w_out[0]]) if dist and l == 0 else None)
        if got:
            w_out[0] = got[0]
        qr, kr, vb, kr2, vb2 = _rope_fwd(proj, tabs, l)
        (ya, att, lse), got = _attn_fwd(
            qr, kr2, vb2, proj, sinks, l, job=_gather_job([w_in[1]]) if dist and l == 0 else None)
        if got:
            w_in[1] = got[0]
        (yl, hl), _ = _lru_fwd(proj, p["lru_conv_w"], lru_conv_b, wa_bd, lru_ba, wx_bd, lru_bx, lru_lambda, l)
        (hn, hnb, xhat, rstd), got = _out_fwd(
            yc, ya, yl, w_out[l], h, ln_post_g, ln_post_b, l,
            job=_gather_job([w_out[1], pw_w[1]]) if dist and l == 0 else None)
        if got:
            w_out[1], pw_w[1] = got
        saved.append((hb, proj, yc, conv, qr, kr, vb, ya, att, lse, yl, hl, xhat, rstd, pw_l))
        h, hb = hn, hnb

    dh = None
    g = {}
    later = None
    early, last = ("w_out", "conv_pw_w"), ("w_in",)
    own = {}
    for l in reversed(range(DEPTH)):
        hb_l, proj, yc, conv, qr, kr, vb, ya, att, lse, yl, hl, xhat, rstd, pw_l = saved[l]
        tail = dist is not None and l == 0
        top = l == DEPTH - 1
        (part, dz, dzb, g["ln_post_g", l], g["ln_post_b", l], d_ya, d_yl, d_conv, dproj, dpw, g["conv_pw_b", l],
         g["conv_ln_g", l], g["conv_ln_b", l]), recv = _post_ln_dcat_bwd(
            h if top else dh, target if top else None, xhat, rstd, ln_post_g, w_out[l],
            conv, proj, conv_ln_g, conv_ln_b, pw_l, conv_pw_b, l,
            job=_swap_job(later["grads"]) if later else None)
        if top:
            loss_part = part
        if later:
            later["parts"], later["owns"] = _chip_partials(big_names, later["grads"], recv, dist, later["l"])
        g["w_out", l] = _dwout_bwd(yc, ya, yl, dzb, l)
        g["conv_pw_w", l] = dpw.reshape(N_SHARD, 2, PW_SH // 2, CW)
        if tail:
            own["early"] = dict(l=0, grads=[g[name, 0] for name in early])
        job = None
        if tail:
            job = _join_jobs(_swap_job(own["early"]["grads"]), _scatter_job(later["parts"][1:]))
        (dproj, ddw, g["conv_dw_b", l]), got = _conv_bwd_taps(d_conv, proj, p["conv_dw_w"], dproj, l, job=job)
        if tail:
            n_early = len(early)
            own["early"]["parts"], own["early"]["owns"] = _chip_partials(
                early, own["early"]["grads"], got[:n_early], dist, 0)
            later["z"] = got[n_early:]
        g["conv_dw_w", l] = ddw[:CONV_K]
        (dqr, dk, dv, dk0, dv0, dproj, dsink), z = _attn_bwd(
            qr, kr, vb, proj, att, lse, d_ya, sinks, dproj, l,
            job=_scatter_job(later["parts"][:1]) if later else None)
        if later:
            later["z"] = z + later["z"]
        g["attn_sinks", l] = dsink[0, :N_HEADS]
        dproj = _rope_bwd(dqr, dk, dv, dk0, dv0, tabs, dproj, l)
        (dproj, dlw, g["lru_conv_b", l], dwa, g["lru_ba", l], dwx, g["lru_bx", l], g["lru_lambda", l]), z = _lru_bwd(
            proj, hl, d_yl, p["lru_conv_w"], lru_conv_b, wa_bd, lru_ba, wx_bd, lru_bx, lru_lambda, dproj, l,
            job=_scatter_job(own["early"]["parts"]) if tail else None)
        if tail:
            own["early"]["z"] = z
        g["lru_conv_w", l] = dlw[:LRU_K]
        g["lru_wa", l] = _diag_blocks(dwa)
        g["lru_wx", l] = _diag_blocks(dwx)
        job = None
        if l > 0:
            g["w_in", l] = _dwin_bwd(hb_l, dproj, l)
        else:
            c = dist[0] if dist else jnp.int32(0)
            job = None
            if dist:
                pack_a = _pack_rows([_layer_stack(g, name) for name in _SMALL_LAYERED])
                totals = _shard_totals(big_names, later, dist)
                job = _join_jobs(_share_job(totals), _spread_job(pack_a))
            (give,), got = _dwin_half(hb_l, dproj, 1 - c, l, "give", job=job)
            (keep,), recv = _dwin_half(hb_l, dproj, c, l, "keep", job=_send_job([give]) if dist else None)
            job = None
            if dist:
                _store_reduced(big_names, later["l"], got[:-1], g)
                later = None
                g["pack_layered", -1] = _sum_slots(pack_a, got[-1], dist[3], "layered")
                own["last"] = dict(l=0)
                own["last"]["parts"], own["last"]["owns"] = _chip_partials(
                    last, [keep.reshape(N_SHARD, 1, D // 2, WIN_SH)], recv, (jnp.int32(0),) + tuple(dist[1:]), 0)
                job = _scatter_job(own["last"]["parts"])
            else:
                g["w_in", l] = jnp.stack([keep, give], axis=1)
        (dh,), got = _dh_bwd(dproj, [w_in[l]], dz, l, job=job)
        if tail:
            own["last"]["z"] = got
        if dist and l > 0:
            later = dict(l=l, grads=[g[name, l] for name in big_names])
    grad_x, g["meta_tokens", -1], g["ln_in_g", -1], g["ln_in_b", -1] = _embed_bwd(
        dh, x, p["meta_tokens"], ln_in_g, ln_in_b)
    if dist:
        pack_b = _pack_rows([g[name, -1] for name in _SMALL_EMBED])
        state = dict(l=0, owns=own["last"]["owns"] + own["early"]["owns"], z=own["last"]["z"] + own["early"]["z"])
        totals = _shard_totals(last + early, state, dist)
        got = _run_job(_join_jobs(_share_job(totals), _spread_job(pack_b)), "share_and_spread")
        _store_reduced(last + early, 0, got[:-1], g)
        g["pack_embed", -1] = _sum_slots(pack_b, got[-1], dist[3], "embed")
    return loss_part, grad_x, g


_SMALL_EMBED = ("meta_tokens", "ln_in_g", "ln_in_b")
_SMALL_LAYERED = ("conv_dw_w", "conv_dw_b", "conv_ln_g", "conv_ln_b", "conv_pw_b", "attn_sinks", "lru_conv_w",
                  "lru_conv_b", "lru_wa", "lru_ba", "lru_wx", "lru_bx", "lru_lambda", "ln_post_g", "ln_post_b")


def _layer_stack(g, name):
    return jnp.stack([g[name, l] for l in range(DEPTH)], axis=0)


def _chip_partials(names, grads, recv, dist, l):
    outs = [_chip_partial(a, r, dist[0], dist[1], f"{name}{l}") for name, a, r in zip(names, grads, recv)]
    return [o[0] for o in outs], [o[1] for o in outs]


def _shard_totals(names, state, dist):
    l = state["l"]
    return [_shard_total(po, zz, dist[2], f"{name}{l}") for name, po, zz in zip(names, state["owns"], state["z"])]


def _store_reduced(names, l, full, g):
    for name, f in zip(names, full):
        g[name, l] = f.reshape(2 * f.shape[1], f.shape[2])


MESH = pl.DeviceIdType.MESH
HBM_SPEC = pl.BlockSpec(memory_space=pltpu.HBM)
N_DEV = 8


def _position():
    x, y, c = lax.axis_index("x"), lax.axis_index("y"), lax.axis_index("c")
    return x, y, c


def _other_chips(x, y):
    return [(1 - x, y), (x, 1 - y), (1 - x, 1 - y)]


def _cast_into_slot(a, l, j, tag):
    _, R, C = a.shape
    tb = _pick(R, (512, 128))

    def body(s_ref, a_ref, o_ref):
        o_ref[...] = a_ref[...].astype(BF16)

    grid_spec = pltpu.PrefetchScalarGridSpec(
        num_scalar_prefetch=1, grid=(R // tb,),
        in_specs=[pl.BlockSpec((None, tb, C), lambda t, sc: (l, t, 0))],
        out_specs=pl.BlockSpec((None, tb, C), lambda t, sc: (sc[0], t, 0)))
    return pl.pallas_call(
        body, name=f"cast_into_slot_{tag}{l}", grid_spec=grid_spec,
        out_shape=jax.ShapeDtypeStruct((N_SHARD, R, C), BF16),
        compiler_params=_cp("arbitrary"),
    )(jnp.reshape(j, (1,)).astype(jnp.int32), a)


class _Job:
    def __init__(self, inputs, aliased, extra_out, sems, start, mid, finish):
        self.inputs, self.extra_out, self.sems = list(inputs), list(extra_out), list(sems)
        self.n_aliased = len(self.inputs) if aliased is True else int(aliased)
        self.start, self.mid, self.finish = start, mid, finish

    def out_shapes(self):
        return [jax.ShapeDtypeStruct(a.shape, a.dtype) for a in self.inputs[:self.n_aliased]] + self.extra_out


def _side_call(body, job, *, name, grid, in_specs, out_specs, out_shape, scratch_shapes, semantics, args,
               aliases=None, prefetch=()):
    aliases = dict(aliases or {})
    n_pre = len(prefetch)

    def call(fn, ins, outs, shapes, scratch, sem, operands):
        if n_pre:
            spec = pltpu.PrefetchScalarGridSpec(num_scalar_prefetch=n_pre, grid=grid, in_specs=ins, out_specs=outs,
                                                scratch_shapes=scratch)
            return pl.pallas_call(fn, name=name, grid_spec=spec, out_shape=shapes,
                                  input_output_aliases={k + n_pre: v for k, v in aliases.items()},
                                  compiler_params=_cp(*sem))(*prefetch, *operands)
        return pl.pallas_call(fn, name=name, grid=grid, in_specs=ins, out_specs=outs, out_shape=shapes,
                              scratch_shapes=scratch, input_output_aliases=aliases,
                              compiler_params=_cp(*sem))(*operands)

    if job is None:
        return list(call(body, list(in_specs), list(out_specs), list(out_shape), list(scratch_shapes),
                         semantics, args)), []
    n_in, n_out, n_scr = len(in_specs), len(out_specs), len(scratch_shapes)
    j_in, j_out = len(job.inputs), len(job.out_shapes())
    steps = 1
    for gsize in grid:
        steps *= gsize

    def wrapped(*refs):
        pre, refs = refs[:n_pre], refs[n_pre:]
        host_in, job_in = refs[:n_in], refs[n_in:n_in + j_in]
        o0 = n_in + j_in
        host_out, job_out = refs[o0:o0 + n_out], refs[o0 + n_out:o0 + n_out + j_out]
        s0 = o0 + n_out + j_out
        host_scr, sems = refs[s0:s0 + n_scr], refs[s0 + n_scr:]
        step = pl.program_id(0)
        for d in range(1, len(grid)):
            step = step * grid[d] + pl.program_id(d)

        @pl.when(step == 0)
        def _():
            job.start(job_in, job_out, sems)

        @pl.when(step == max(steps - 2, 0))
        def _():
            job.mid(job_in, job_out, sems)

        body(*pre, *host_in, *host_out, *host_scr)

        @pl.when(step == steps - 1)
        def _():
            job.finish(job_in, job_out, sems)

    aliases.update({n_in + k: n_out + k for k in range(job.n_aliased)})
    outs = call(wrapped, list(in_specs) + [HBM_SPEC] * j_in, list(out_specs) + [HBM_SPEC] * j_out,
                list(out_shape) + job.out_shapes(), list(scratch_shapes) + job.sems,
                ["arbitrary"] * len(grid), [*args, *job.inputs])
    return list(outs[:n_out]), list(outs[n_out:])


def _run_job(job, name):
    return _side_call(lambda: None, job, name=name, grid=(1,), in_specs=[], out_specs=[], out_shape=[],
                      scratch_shapes=[], semantics=("arbitrary",), args=[])[1]


def _gather_job(slots, peers=(0, 1, 2)):
    n = len(slots)

    def copies(buf, sems):
        ici_send, ici_recv, d2d_send, d2d_recv = sems
        x, y, c = _position()
        chips = _other_chips(x, y)

        def half(k, slot, which):
            hr = buf[k].shape[1] // 2
            return buf[k].at[slot, pl.ds(pl.multiple_of(which * hr, hr), hr)]

        def over_ici(k, p, slot):
            px, py = chips[p]
            return pltpu.make_async_remote_copy(
                src_ref=half(k, slot, c), dst_ref=half(k, slot, c),
                send_sem=ici_send.at[k * 3 + p], recv_sem=ici_recv.at[k * 3 + p],
                device_id=(px, py, c), device_id_type=MESH)

        def over_d2d(k, p, which):
            px, py = chips[p]
            return pltpu.make_async_remote_copy(
                src_ref=half(k, 2 * px + py, which), dst_ref=half(k, 2 * px + py, which),
                send_sem=d2d_send.at[k * 3 + p], recv_sem=d2d_recv.at[k * 3 + p],
                device_id=(x, y, 1 - c), device_id_type=MESH)

        return over_ici, over_d2d, 2 * x + y, chips, c

    pairs = [(k, p) for k in range(n) for p in peers]

    def start(_, buf, sems):
        over_ici, _, mine, _, _ = copies(buf, sems)
        for k, p in pairs:
            over_ici(k, p, mine).start()

    def mid(_, buf, sems):
        over_ici, over_d2d, _, chips, c = copies(buf, sems)
        for k, p in pairs:
            px, py = chips[p]
            over_ici(k, p, 2 * px + py).wait_recv()
            over_d2d(k, p, c).start()

    def finish(_, buf, sems):
        over_ici, over_d2d, mine, _, c = copies(buf, sems)
        for k, p in pairs:
            over_d2d(k, p, 1 - c).wait_recv()
        for k, p in pairs:
            over_ici(k, p, mine).wait_send()
            over_d2d(k, p, c).wait_send()

    return _Job(slots, True, [], [pltpu.SemaphoreType.DMA((3 * n,))] * 4, start, mid, finish)


def _gather_shards(shards):
    n = len(shards)

    def body(*refs):
        src, dst = refs[:n], refs[n:2 * n]
        send_sems, recv_sems, local_sems = refs[2 * n:]
        x, y, c = _position()
        mine = 2 * x + y
        chips = _other_chips(x, y)

        def copy(k, p):
            return pltpu.make_async_remote_copy(
                src_ref=src[k], dst_ref=dst[k].at[mine],
                send_sem=send_sems.at[k * 3 + p], recv_sem=recv_sems.at[k * 3 + p],
                device_id=(*chips[p], c), device_id_type=MESH)

        def arrival(k, p):
            px, py = chips[p]
            return pltpu.make_async_remote_copy(
                src_ref=src[k], dst_ref=dst[k].at[2 * px + py],
                send_sem=send_sems.at[k * 3 + p], recv_sem=recv_sems.at[k * 3 + p],
                device_id=(px, py, c), device_id_type=MESH)

        local = [pltpu.make_async_copy(src[k], dst[k].at[mine], local_sems.at[k]) for k in range(n)]
        for cp in local:
            cp.start()
        for k in range(n):
            for p in range(3):
                copy(k, p).start()
        for k in range(n):
            for p in range(3):
                arrival(k, p).wait_recv()
        for k in range(n):
            for p in range(3):
                copy(k, p).wait_send()
        for cp in local:
            cp.wait()

    return pl.pallas_call(
        body, name="gather_shards",
        in_specs=[HBM_SPEC] * n, out_specs=[HBM_SPEC] * n,
        out_shape=[jax.ShapeDtypeStruct((N_SHARD,) + s.shape, s.dtype) for s in shards],
        scratch_shapes=[pltpu.SemaphoreType.DMA((3 * n,)), pltpu.SemaphoreType.DMA((3 * n,)),
                        pltpu.SemaphoreType.DMA((n,))],
    )(*shards)


def _swap_job(grads):
    n = len(grads)

    def copies(src, dst, sems):
        x, y, c = _position()
        return [pltpu.make_async_remote_copy(
            src_ref=src[k].at[:, 1 - c], dst_ref=dst[k],
            send_sem=sems[0].at[k], recv_sem=sems[1].at[k],
            device_id=(x, y, 1 - c), device_id_type=MESH) for k in range(n)]

    def start(src, dst, sems):
        for cp in copies(src, dst, sems):
            cp.start()

    def finish(src, dst, sems):
        for cp in copies(src, dst, sems):
            cp.wait()

    return _Job(grads, False, [jax.ShapeDtypeStruct((N_SHARD,) + g.shape[2:], F32) for g in grads],
                [pltpu.SemaphoreType.DMA((n,))] * 2, start, lambda *_: None, finish)


def _send_job(arrays):
    n = len(arrays)

    def copies(src, dst, sems):
        x, y, c = _position()
        return [pltpu.make_async_remote_copy(
            src_ref=src[k], dst_ref=dst[k], send_sem=sems[0].at[k], recv_sem=sems[1].at[k],
            device_id=(x, y, 1 - c), device_id_type=MESH) for k in range(n)]

    def start(src, dst, sems):
        for cp in copies(src, dst, sems):
            cp.start()

    def finish(src, dst, sems):
        for cp in copies(src, dst, sems):
            cp.wait()

    return _Job(arrays, False, [jax.ShapeDtypeStruct(a.shape, a.dtype) for a in arrays],
                [pltpu.SemaphoreType.DMA((n,))] * 2, start, lambda *_: None, finish)


def _chip_partial(a, y, c, j, tag):
    _, _, R, C = a.shape
    tr = _pick(R, (256, 64))

    def body(s_ref, a_ref, y_ref, pb_ref, po_ref):
        total = a_ref[...] + y_ref[...]
        pb_ref[...] = total.astype(BF16)

        @pl.when(pl.program_id(1) == s_ref[1])
        def _():
            po_ref[...] = total

    grid_spec = pltpu.PrefetchScalarGridSpec(
        num_scalar_prefetch=1, grid=(R // tr, N_SHARD),
        in_specs=[pl.BlockSpec((None, None, tr, C), lambda t, s, sc: (s, sc[0], t, 0)),
                  pl.BlockSpec((None, tr, C), lambda t, s, sc: (s, t, 0))],
        out_specs=[pl.BlockSpec((None, tr, C), lambda t, s, sc: (s, t, 0)),
                   pl.BlockSpec((tr, C), lambda t, s, sc: (t, 0))])
    return pl.pallas_call(
        body, name=f"chip_partial_{tag}", grid_spec=grid_spec,
        out_shape=[jax.ShapeDtypeStruct((N_SHARD, R, C), BF16), jax.ShapeDtypeStruct((R, C), F32)],
        compiler_params=_cp("arbitrary", "arbitrary"),
    )(jnp.stack([c, j]).astype(jnp.int32), a, y)


def _scatter_job(parts):
    n = len(parts)
    pairs = [(k, p) for k in range(n) for p in range(3)]

    def copy(src, dst, sems, k, p, outgoing):
        x, y, c = _position()
        mine = 2 * x + y
        px, py = _other_chips(x, y)[p]
        theirs = 2 * px + py
        return pltpu.make_async_remote_copy(
            src_ref=src[k].at[theirs if outgoing else mine], dst_ref=dst[k].at[mine if outgoing else theirs],
            send_sem=sems[0].at[k * 3 + p], recv_sem=sems[1].at[k * 3 + p],
            device_id=(px, py, c), device_id_type=MESH)

    def start(src, dst, sems):
        for k, p in pairs:
            copy(src, dst, sems, k, p, True).start()

    def finish(src, dst, sems):
        for k, p in pairs:
            copy(src, dst, sems, k, p, False).wait_recv()
        for k, p in pairs:
            copy(src, dst, sems, k, p, True).wait_send()

    return _Job(parts, False, [jax.ShapeDtypeStruct(pb.shape, BF16) for pb in parts],
                [pltpu.SemaphoreType.DMA((3 * n,))] * 2, start, lambda *_: None, finish)


def _shard_total(own, z, others_c, tag):
    R, C = own.shape
    tr = _pick(R, (256, 64))

    def body(s_ref, o_ref, z0_ref, z1_ref, z2_ref, h_ref):
        h_ref[...] = ((o_ref[...] + z0_ref[...].astype(F32)) + z1_ref[...].astype(F32)) + z2_ref[...].astype(F32)

    zspec = lambda q: pl.BlockSpec((None, tr, C), lambda t, sc: (sc[q], t, 0))
    grid_spec = pltpu.PrefetchScalarGridSpec(
        num_scalar_prefetch=1, grid=(R // tr,),
        in_specs=[pl.BlockSpec((tr, C), lambda t, sc: (t, 0)), zspec(0), zspec(1), zspec(2)],
        out_specs=pl.BlockSpec((None, tr, C), lambda t, sc: (sc[3], t, 0)))
    return pl.pallas_call(
        body, name=f"shard_total_{tag}", grid_spec=grid_spec,
        out_shape=jax.ShapeDtypeStruct((2, R, C), F32),
        compiler_params=_cp("arbitrary"),
    )(others_c, own, z, z, z)


def _share_job(totals):
    n = len(totals)

    def copy(buf, sems, k, which):
        x, y, c = _position()
        return pltpu.make_async_remote_copy(
            src_ref=buf[k].at[which], dst_ref=buf[k].at[which],
            send_sem=sems[0].at[k], recv_sem=sems[1].at[k],
            device_id=(x, y, 1 - c), device_id_type=MESH)

    def start(_, buf, sems):
        c = lax.axis_index("c")
        for k in range(n):
            copy(buf, sems, k, c).start()

    def finish(_, buf, sems):
        c = lax.axis_index("c")
        for k in range(n):
            copy(buf, sems, k, 1 - c).wait_recv()
        for k in range(n):
            copy(buf, sems, k, c).wait_send()

    return _Job(totals, True, [], [pltpu.SemaphoreType.DMA((n,))] * 2, start, lambda *_: None, finish)


def _spread_job(pack):
    def copy(src, dst, sems, m, outgoing):
        x, y, c = _position()
        peer = (x ^ (m >> 2), y ^ ((m >> 1) & 1), c ^ (m & 1))
        slot = 4 * x + 2 * y + c if outgoing else 4 * peer[0] + 2 * peer[1] + peer[2]
        return pltpu.make_async_remote_copy(
            src_ref=src[0], dst_ref=dst[0].at[slot], send_sem=sems[0].at[m - 1], recv_sem=sems[1].at[m - 1],
            device_id=peer, device_id_type=MESH)

    def start(src, dst, sems):
        for m in range(1, N_DEV):
            copy(src, dst, sems, m, True).start()

    def finish(src, dst, sems):
        for m in range(1, N_DEV):
            copy(src, dst, sems, m, False).wait_recv()
        for m in range(1, N_DEV):
            copy(src, dst, sems, m, True).wait_send()

    return _Job([pack], False, [jax.ShapeDtypeStruct((N_DEV,) + pack.shape, F32)],
                [pltpu.SemaphoreType.DMA((N_DEV - 1,))] * 2, start, lambda *_: None, finish)


def _join_jobs(a, b):
    for job in (a, b):
        assert job.n_aliased in (0, len(job.inputs)) and not (job.n_aliased and job.extra_out)
    assert a.n_aliased or not b.n_aliased
    n_in, n_out, n_sem = len(a.inputs), len(a.out_shapes()), len(a.sems)

    def phase(name):
        def run(ins, outs, sems):
            getattr(a, name)(ins[:n_in], outs[:n_out], sems[:n_sem])
            getattr(b, name)(ins[n_in:], outs[n_out:], sems[n_sem:])
        return run

    return _Job(a.inputs + b.inputs, a.n_aliased + b.n_aliased, a.extra_out + b.extra_out, a.sems + b.sems,
                phase("start"), phase("mid"), phase("finish"))


def _sum_slots(pack, slots, me, tag):
    def body(me_ref, p_ref, s_ref, o_ref):
        acc = None
        for d in range(N_DEV):
            term = jnp.where(me_ref[0] == d, p_ref[...], s_ref[d])
            acc = term if acc is None else acc + term
        o_ref[...] = acc

    vm = pl.BlockSpec(memory_space=pltpu.VMEM)
    return pl.pallas_call(
        body, name=f"sum_slots_{tag}",
        in_specs=[pl.BlockSpec(memory_space=pltpu.SMEM), vm, vm], out_specs=vm,
        out_shape=jax.ShapeDtypeStruct(pack.shape, F32),
        compiler_params=pltpu.CompilerParams(vmem_limit_bytes=V7X_VMEM_LIMIT),
    )(jnp.reshape(me, (1,)).astype(jnp.int32), pack, slots)


def _pack_rows(arrays):
    total = sum(a.size for a in arrays)
    rows = -(-total // 128)
    rows = -(-rows // PACK_ROWS_ALIGN) * PACK_ROWS_ALIGN
    flat = [a.reshape(-1) for a in arrays] + [jnp.zeros((rows * 128 - total,), F32)]
    return jnp.concatenate(flat).reshape(rows, 128)


def _adamw_math(w, g, m, v):
    m = ADAM_B1 * m + (1.0 - ADAM_B1) * g
    v = ADAM_B2 * v + (1.0 - ADAM_B2) * (g * g)
    m_hat = m / (1.0 - ADAM_B1 ** ADAM_STEP)
    v_hat = v / (1.0 - ADAM_B2 ** ADAM_STEP)
    delta = -ADAM_LR * (m_hat / (jnp.sqrt(v_hat) + ADAM_EPS) + ADAM_WD * w)
    return delta, m, v


def _adamw_big(w, g0, g1, m, v, tag):
    _, R, C = w.shape
    tr = _pick(R, (256, 128))

    def body(w_ref, g0_ref, g1_ref, m_ref, v_ref, go_ref, d_ref, mo_ref, vo_ref):
        g = jnp.where(pl.program_id(0) == 0, g0_ref[...], g1_ref[...])
        delta, mn, vn = _adamw_math(w_ref[...], g, m_ref[...], v_ref[...])
        go_ref[...] = g
        d_ref[...] = delta
        mo_ref[...] = mn
        vo_ref[...] = vn

    s3 = pl.BlockSpec((None, tr, C), lambda l, t: (l, t, 0))
    g_spec = lambda layer: pl.BlockSpec((tr, C), lambda l, t: (jnp.where(l == layer, t, 0), 0))
    shp = jax.ShapeDtypeStruct(w.shape, F32)
    return pl.pallas_call(
        body, name=f"adamw_{tag}", grid=(2, R // tr),
        in_specs=[s3, g_spec(0), g_spec(1), s3, s3], out_specs=[s3, s3, s3, s3],
        out_shape=[shp, shp, shp, shp],
        compiler_params=_cp("parallel", "parallel"),
    )(w, g0, g1, m, v)


def _adamw_small(ws, gs, ms, vs):
    n = len(ws)

    def body(*refs):
        w_r, g_r, m_r, v_r = refs[:n], refs[n:2 * n], refs[2 * n:3 * n], refs[3 * n:4 * n]
        d_o, m_o, v_o = refs[4 * n:5 * n], refs[5 * n:6 * n], refs[6 * n:7 * n]
        for k in range(n):
            delta, mn, vn = _adamw_math(w_r[k][...], g_r[k][...], m_r[k][...], v_r[k][...])
            d_o[k][...] = delta
            m_o[k][...] = mn
            v_o[k][...] = vn

    vm = pl.BlockSpec(memory_space=pltpu.VMEM)
    shapes = [jax.ShapeDtypeStruct(w.shape, F32) for w in ws]
    outs = pl.pallas_call(
        body, name="adamw_small",
        in_specs=[vm] * (4 * n), out_specs=[vm] * (3 * n),
        out_shape=shapes * 3,
    )(*ws, *gs, *ms, *vs)
    return outs[:n], outs[n:2 * n], outs[2 * n:]


_WEIGHTS = ["meta_tokens", "ln_in_g", "ln_in_b", "w_in", "conv_dw_w", "conv_dw_b", "conv_ln_g", "conv_ln_b",
            "conv_pw_w", "conv_pw_b", "attn_sinks", "lru_conv_w", "lru_conv_b", "lru_wa", "lru_ba", "lru_wx",
            "lru_bx", "lru_lambda", "w_out", "ln_post_g", "ln_post_b"]
_BIG = ("w_in", "w_out", "conv_pw_w")
_SMALL_SHARDED = {"meta_tokens": 1, "conv_dw_w": 2, "lru_conv_w": 2}
PACK_ROWS_ALIGN = 8


def _as2d(a):
    return a.reshape(1, -1) if a.ndim == 1 else a.reshape(-1, a.shape[-1])


def kernel(x, meta_tokens, ln_in_g, ln_in_b, w_in, conv_dw_w, conv_dw_b, conv_ln_g, conv_ln_b, conv_pw_w, conv_pw_b, attn_sinks, lru_conv_w, lru_conv_b, lru_wa, lru_ba, lru_wx, lru_bx, lru_lambda, w_out, ln_post_g, ln_post_b, loss_target, m_meta_tokens, m_ln_in_g, m_ln_in_b, m_w_in, m_conv_dw_w, m_conv_dw_b, m_conv_ln_g, m_conv_ln_b, m_conv_pw_w, m_conv_pw_b, m_attn_sinks, m_lru_conv_w, m_lru_conv_b, m_lru_wa, m_lru_ba, m_lru_wx, m_lru_bx, m_lru_lambda, m_w_out, m_ln_post_g, m_ln_post_b, v_meta_tokens, v_ln_in_g, v_ln_in_b, v_w_in, v_conv_dw_w, v_conv_dw_b, v_conv_ln_g, v_conv_ln_b, v_conv_pw_w, v_conv_pw_b, v_attn_sinks, v_lru_conv_w, v_lru_conv_b, v_lru_wa, v_lru_ba, v_lru_wx, v_lru_bx, v_lru_lambda, v_w_out, v_ln_post_g, v_ln_post_b):
    w = dict(meta_tokens=meta_tokens, ln_in_g=ln_in_g, ln_in_b=ln_in_b, w_in=w_in, conv_dw_w=conv_dw_w,
             conv_dw_b=conv_dw_b, conv_ln_g=conv_ln_g, conv_ln_b=conv_ln_b, conv_pw_w=conv_pw_w,
             conv_pw_b=conv_pw_b, attn_sinks=attn_sinks, lru_conv_w=lru_conv_w, lru_conv_b=lru_conv_b,
             lru_wa=lru_wa, lru_ba=lru_ba, lru_wx=lru_wx, lru_bx=lru_bx, lru_lambda=lru_lambda, w_out=w_out,
             ln_post_g=ln_post_g, ln_post_b=ln_post_b)
    mom_m = dict(zip(_WEIGHTS, (m_meta_tokens, m_ln_in_g, m_ln_in_b, m_w_in, m_conv_dw_w, m_conv_dw_b, m_conv_ln_g,
                                m_conv_ln_b, m_conv_pw_w, m_conv_pw_b, m_attn_sinks, m_lru_conv_w, m_lru_conv_b,
                                m_lru_wa, m_lru_ba, m_lru_wx, m_lru_bx, m_lru_lambda, m_w_out, m_ln_post_g,
                                m_ln_post_b)))
    mom_v = dict(zip(_WEIGHTS, (v_meta_tokens, v_ln_in_g, v_ln_in_b, v_w_in, v_conv_dw_w, v_conv_dw_b, v_conv_ln_g,
                                v_conv_ln_b, v_conv_pw_w, v_conv_pw_b, v_attn_sinks, v_lru_conv_w, v_lru_conv_b,
                                v_lru_wa, v_lru_ba, v_lru_wx, v_lru_bx, v_lru_lambda, v_w_out, v_ln_post_g,
                                v_ln_post_b)))
    xi, yi, ci = _position()
    j = 2 * xi + yi

    g_meta, g_dw, g_lc = _gather_shards([meta_tokens, conv_dw_w, lru_conv_w])
    p = dict(w)
    p["w_in"] = [_cast_into_slot(w_in, l, j, "w_in") for l in range(DEPTH)]
    p["w_out"] = [_cast_into_slot(w_out, l, j, "w_out") for l in range(DEPTH)]
    p["conv_pw_w"] = [_cast_into_slot(conv_pw_w, l, j, "conv_pw_w") for l in range(DEPTH)]
    p["meta_tokens"] = g_meta.transpose(1, 0, 2).reshape(N_META, D)
    p["conv_dw_w"] = g_dw.transpose(1, 2, 0, 3).reshape(DEPTH, CONV_K, CW)
    p["lru_conv_w"] = g_lc.transpose(1, 2, 0, 3).reshape(DEPTH, LRU_K, LW)

    others = jnp.stack([jnp.where(j <= 0, 1, 0), jnp.where(j <= 1, 2, 1), jnp.where(j <= 2, 3, 2), ci]).astype(jnp.int32)
    me = 4 * xi + 2 * yi + ci
    order = jnp.stack([j, 2 * (1 - xi) + yi, 2 * xi + (1 - yi), 2 * (1 - xi) + (1 - yi)]).astype(jnp.int32)
    loss_part, grad_x, g = _device_step(x[0], loss_target[0], p, dist=(ci, j, others, me, order))
    loss = lax.psum(jnp.sum(loss_part), ("x", "y", "c"))
    big = {(name, l): g[name, l] for name in _BIG for l in range(DEPTH)}

    small_names = [n for n in _WEIGHTS if n not in _BIG]
    small_g = {}
    for names, red in ((_SMALL_LAYERED, g["pack_layered", -1]), (_SMALL_EMBED, g["pack_embed", -1])):
        red = red.reshape(-1)
        off = 0
        for n in names:
            fshape = list(w[n].shape)
            if n in _SMALL_SHARDED:
                fshape[_SMALL_SHARDED[n]] *= N_SHARD
            sz = 1
            for dim in fshape:
                sz *= dim
            full = red[off:off + sz].reshape(fshape)
            off += sz
            if n in _SMALL_SHARDED:
                ax = _SMALL_SHARDED[n]
                full = lax.dynamic_slice_in_dim(full, j * w[n].shape[ax], w[n].shape[ax], axis=ax)
            small_g[n] = full

    out_g, out_d, out_m, out_v = {}, {}, {}, {}
    for name in _BIG:
        shp = w[name].shape
        to3 = lambda a: a.reshape(DEPTH, -1, shp[-1])
        go, do, mo, vo = _adamw_big(to3(w[name]), big[name, 0], big[name, 1], to3(mom_m[name]), to3(mom_v[name]), name)
        out_g[name], out_d[name], out_m[name], out_v[name] = (a.reshape(shp) for a in (go, do, mo, vo))
    ds, ms, vs = _adamw_small([_as2d(w[n]) for n in small_names], [_as2d(small_g[n]) for n in small_names],
                              [_as2d(mom_m[n]) for n in small_names], [_as2d(mom_v[n]) for n in small_names])
    for n, d_, m_, v_ in zip(small_names, ds, ms, vs):
        out_g[n] = small_g[n]
        out_d[n], out_m[n], out_v[n] = d_.reshape(w[n].shape), m_.reshape(w[n].shape), v_.reshape(w[n].shape)

    return (loss, grad_x[None], *[out_g[n] for n in _WEIGHTS], *[out_d[n] for n in _WEIGHTS],
            *[out_m[n] for n in _WEIGHTS], *[out_v[n] for n in _WEIGHTS])
```

```python
import functools

import jax
import jax.numpy as jnp
from jax import lax
from jax.experimental import pallas as pl
from jax.experimental.pallas import tpu as pltpu

F32 = jnp.float32
BF16 = jnp.bfloat16

D = 2048
N_META = 16
CW = 512
CONV_K = 31
AW = 1024
KVW = 256
N_HEADS = 16
LW = 512
LRU_K = 4
LRU_C = 8.0
IN_TOTAL = 5120
ROT_HALF = 8
ROPE_THETA = 500000.0
LN_EPS = 1e-5
DEPTH = 2
ALPHA = (2.0 * DEPTH) ** 0.25
NEG_INF = -1e30
ADAM_LR, ADAM_B1, ADAM_B2, ADAM_EPS, ADAM_WD, ADAM_STEP = 0.001, 0.9, 0.999, 1e-08, 0.01, 10

BLK = 128
PAD = BLK - N_META
N_SHARD = 4
WIN_SH = IN_TOTAL // N_SHARD
WOUT_SH = D // N_SHARD
PW_SH = CW // N_SHARD
HALO = 32
LHALO = 8
V7X_VMEM_LIMIT = 60 * 1024 * 1024


def _cp(*sem):
    return pltpu.CompilerParams(dimension_semantics=sem if sem else None, vmem_limit_bytes=V7X_VMEM_LIMIT)


def _pick(total, prefs):
    for p in prefs:
        if total % p == 0:
            return p
    raise ValueError(f"no tile for {total}")


def _dot(a, b):
    return jnp.dot(a, b, preferred_element_type=F32)


def _dot_nt(a, b):
    return lax.dot_general(a, b, (((1,), (1,)), ((), ())), preferred_element_type=F32)


def _dot_tn(a, b):
    return lax.dot_general(a, b, (((0,), (0,)), ((), ())), preferred_element_type=F32)


def _sigmoid(x):
    return 1.0 / (1.0 + jnp.exp(-x))


def _silu_and_grad(x):
    s = _sigmoid(x)
    return x * s, s * (1.0 + x * (1.0 - s))


def _ln_rows(x, g, b):
    mu = jnp.mean(x, axis=-1, keepdims=True)
    xc = x - mu
    var = jnp.mean(xc * xc, axis=-1, keepdims=True)
    rstd = lax.rsqrt(var + LN_EPS)
    xhat = xc * rstd
    return xhat * g + b, xhat, rstd


def _ln_bwd_rows(dy, xhat, rstd, g):
    dxh = dy * g
    m1 = jnp.mean(dxh, axis=-1, keepdims=True)
    m2 = jnp.mean(dxh * xhat, axis=-1, keepdims=True)
    return rstd * (dxh - m1 - xhat * m2)


def _row_ids(n, base):
    return base + lax.broadcasted_iota(jnp.int32, (n, 1), 0)


def _colsum(x):
    return jnp.sum(x, axis=0, keepdims=True)


def _embed_fwd(x, meta, g, b, job=None):
    S = x.shape[0]
    nb = S // BLK + 1

    def body(x_ref, meta_ref, g_ref, b_ref, h_ref, hb_ref):
        n = pl.program_id(0)

        @pl.when(n == 0)
        def _():
            y, _, _ = _ln_rows(meta_ref[...], g_ref[...], b_ref[...])
            h_ref[...] = jnp.zeros_like(h_ref)
            h_ref[PAD:BLK, :] = y

        @pl.when(n > 0)
        def _():
            y, _, _ = _ln_rows(x_ref[...], g_ref[...], b_ref[...])
            h_ref[...] = y

        hb_ref[...] = h_ref[...].astype(BF16)

    return _side_call(
        body, job, name="embed_fwd", grid=(nb,),
        in_specs=[pl.BlockSpec((BLK, D), lambda n: (jnp.maximum(n - 1, 0), 0)),
                  pl.BlockSpec((N_META, D), lambda n: (0, 0)),
                  pl.BlockSpec((1, D), lambda n: (0, 0)),
                  pl.BlockSpec((1, D), lambda n: (0, 0))],
        out_specs=[pl.BlockSpec((BLK, D), lambda n: (n, 0)),
                   pl.BlockSpec((BLK, D), lambda n: (n, 0))],
        out_shape=[jax.ShapeDtypeStruct((nb * BLK, D), F32), jax.ShapeDtypeStruct((nb * BLK, D), BF16)],
        scratch_shapes=[], semantics=("arbitrary",), args=[x, meta, g, b])


def _embed_bwd(dh, x, meta, g, b):
    S = x.shape[0]
    nb = S // BLK + 1

    def body(dh_ref, x_ref, meta_ref, g_ref, b_ref, gx_ref, gm_ref, dg_ref, db_ref):
        n = pl.program_id(0)

        @pl.when(n == 0)
        def _():
            _, xhat, rstd = _ln_rows(meta_ref[...], g_ref[...], b_ref[...])
            dy = dh_ref[PAD:BLK, :]
            gm_ref[...] = _ln_bwd_rows(dy, xhat, rstd, g_ref[...])
            dg_ref[...] = _colsum(dy * xhat)
            db_ref[...] = _colsum(dy)

        @pl.when(n > 0)
        def _():
            _, xhat, rstd = _ln_rows(x_ref[...], g_ref[...], b_ref[...])
            dy = dh_ref[...]
            gx_ref[...] = _ln_bwd_rows(dy, xhat, rstd, g_ref[...])
            dg_ref[...] += _colsum(dy * xhat)
            db_ref[...] += _colsum(dy)

    prev = lambda n: (jnp.maximum(n - 1, 0), 0)
    const = lambda n: (0, 0)
    return pl.pallas_call(
        body, name="embed_bwd", grid=(nb,),
        in_specs=[pl.BlockSpec((BLK, D), lambda n: (n, 0)),
                  pl.BlockSpec((BLK, D), prev),
                  pl.BlockSpec((N_META, D), const),
                  pl.BlockSpec((1, D), const),
                  pl.BlockSpec((1, D), const)],
        out_specs=[pl.BlockSpec((BLK, D), prev),
                   pl.BlockSpec((N_META, D), const),
                   pl.BlockSpec((1, D), const),
                   pl.BlockSpec((1, D), const)],
        out_shape=[jax.ShapeDtypeStruct((S, D), F32), jax.ShapeDtypeStruct((N_META, D), F32),
                   jax.ShapeDtypeStruct((1, D), F32), jax.ShapeDtypeStruct((1, D), F32)],
        compiler_params=_cp("arbitrary"),
    )(dh, x, meta, g, b)


def _proj_fwd(hb, w_in, order, first, count, prev, l, job=None):
    T = hb.shape[0]
    tm = _pick(T, (1056, 384, 128))

    def body(o_sc, a_ref, w_ref, *rest):
        rest[-1][...] = _dot(a_ref[...], w_ref[...])

    return _side_call(
        body, job, name=f"proj_fwd{l}_{first}", grid=(T // tm, count),
        in_specs=[pl.BlockSpec((tm, D), lambda i, j, o: (i, 0)),
                  pl.BlockSpec((None, D, WIN_SH), lambda i, j, o: (o[first + j], 0, 0))]
        + ([] if prev is None else [pl.BlockSpec(memory_space=pl.ANY)]),
        out_specs=[pl.BlockSpec((tm, WIN_SH), lambda i, j, o: (i, o[first + j]))],
        out_shape=[jax.ShapeDtypeStruct((T, IN_TOTAL), F32)],
        scratch_shapes=[], semantics=("parallel", "arbitrary"),
        args=[hb, w_in] + ([] if prev is None else [prev]),
        aliases=None if prev is None else {2: 0}, prefetch=[order])


def _out_fwd(yc, ya, yl, w_out, h, g, b, l, job=None):
    T = h.shape[0]
    tm = _pick(T, (384, 128))

    def body(yc_ref, ya_ref, yl_ref, w_ref, h_ref, g_ref, b_ref, hn_ref, hnb_ref, xh_ref, rs_ref):
        acc = _dot(yc_ref[...], w_ref[0])
        acc += _dot(ya_ref[:, 0:WOUT_SH], w_ref[1])
        acc += _dot(ya_ref[:, WOUT_SH:2 * WOUT_SH], w_ref[2])
        acc += _dot(yl_ref[...], w_ref[3])
        z = ALPHA * h_ref[...] + acc
        y, xhat, rstd = _ln_rows(z, g_ref[...], b_ref[...])
        hn_ref[...] = y
        hnb_ref[...] = y.astype(BF16)
        xh_ref[...] = xhat
        rs_ref[...] = rstd

    row = lambda i: (i, 0)
    return _side_call(
        body, job, name=f"out_fwd{l}", grid=(T // tm,),
        in_specs=[pl.BlockSpec((tm, CW), row), pl.BlockSpec((tm, AW), row), pl.BlockSpec((tm, LW), row),
                  pl.BlockSpec((N_SHARD, WOUT_SH, D), lambda i: (0, 0, 0)),
                  pl.BlockSpec((tm, D), row),
                  pl.BlockSpec((None, 1, D), lambda i: (l, 0, 0)),
                  pl.BlockSpec((None, 1, D), lambda i: (l, 0, 0))],
        out_specs=[pl.BlockSpec((tm, D), row), pl.BlockSpec((tm, D), row), pl.BlockSpec((tm, D), row),
                   pl.BlockSpec((tm, 1), row)],
        out_shape=[jax.ShapeDtypeStruct((T, D), F32), jax.ShapeDtypeStruct((T, D), BF16),
                   jax.ShapeDtypeStruct((T, D), F32), jax.ShapeDtypeStruct((T, 1), F32)],
        scratch_shapes=[], semantics=("parallel",), args=[yc, ya, yl, w_out, h, g, b])


def _post_ln_dcat_bwd(src, target, xhat, rstd, g, w_out, conv, proj, cln_g, cln_b, pw_w, pw_b, l, job=None):
    T = src.shape[0]
    tm = _pick(T, (384, 128))
    per = tm // BLK if target is not None else 0
    last_blk = target.shape[0] // BLK - 1 if target is not None else 0

    def body(s_ref, *refs):
        t_refs = refs[:per]
        (xh_ref, rs_ref, g_ref, w_ref, conv_ref, ct_ref, cg_ref, cb_ref, pw_ref, pb_ref,
         part_ref, dz_ref, dzb_ref, dg_ref, db_ref, da_ref, dl_ref,
         dconv_ref, dct_ref, dpw_ref, dpb_ref, dcg_ref, dcb_ref) = refs[per:]
        i = pl.program_id(0)

        @pl.when(i == 0)
        def _():
            for ref in (part_ref, dg_ref, db_ref, dpw_ref, dpb_ref, dcg_ref, dcb_ref):
                ref[...] = jnp.zeros_like(ref)

        if per:
            tgt = jnp.concatenate([r[...] for r in t_refs], axis=0) if per > 1 else t_refs[0][...]
            real = _row_ids(tm, i * tm) >= BLK
            err = jnp.where(real, s_ref[...] - tgt, 0.0)
            part_ref[...] += _colsum(err * err) * (0.5 / D)
            dy = err * (1.0 / D)
        else:
            dy = s_ref[...]
        xhat = xh_ref[...]
        dz = _ln_bwd_rows(dy, xhat, rs_ref[...], g_ref[...])
        dzb = dz.astype(BF16)
        dz_ref[...] = dz
        dzb_ref[...] = dzb
        dg_ref[...] += _colsum(dy * xhat)
        db_ref[...] += _colsum(dy)
        da_ref[:, 0:WOUT_SH] = _dot_nt(dzb, w_ref[1])
        da_ref[:, WOUT_SH:2 * WOUT_SH] = _dot_nt(dzb, w_ref[2])
        dl_ref[...] = _dot_nt(dzb, w_ref[3])

        d_yc = _dot_nt(dzb, w_ref[0])
        u, chat, crstd = _ln_rows(conv_ref[...], cg_ref[...], cb_ref[...])
        s, ds_du = _silu_and_grad(u)
        sb = s.astype(BF16)
        cpw = _dot(sb, pw_ref[...]) + pb_ref[...]
        gate, dgate = _silu_and_grad(ct_ref[...])
        d_cpw = d_yc * gate
        dct_ref[...] = (d_yc * cpw * dgate).astype(BF16)
        d_cpw_b = d_cpw.astype(BF16)
        dpb_ref[...] += _colsum(d_cpw)
        dpw_ref[...] += _dot_tn(sb, d_cpw_b)
        du = _dot_nt(d_cpw_b, pw_ref[...]) * ds_du
        dconv_ref[...] = _ln_bwd_rows(du, chat, crstd, cg_ref[...])
        dcg_ref[...] += _colsum(du * chat)
        dcb_ref[...] += _colsum(du)

    row = lambda i: (i, 0)
    const = lambda i: (0, 0)
    vec = pl.BlockSpec((None, 1, CW), lambda i: (l, 0, 0))
    t_specs = [pl.BlockSpec((BLK, D), functools.partial(lambda i, q: (jnp.clip(i * per - 1 + q, 0, last_blk), 0), q=q))
               for q in range(per)]
    return _side_call(
        body, job, name=f"post_ln_dcat_bwd{l}", grid=(T // tm,),
        in_specs=[pl.BlockSpec((tm, D), row)] + t_specs + [
            pl.BlockSpec((tm, D), row), pl.BlockSpec((tm, 1), row), pl.BlockSpec((None, 1, D), lambda i: (l, 0, 0)),
            pl.BlockSpec((N_SHARD, WOUT_SH, D), lambda i: (0, 0, 0)),
            pl.BlockSpec((tm, CW), row), pl.BlockSpec((tm, CW), lambda i: (i, 2)), vec, vec,
            pl.BlockSpec((CW, CW), const), vec],
        out_specs=[pl.BlockSpec((1, D), const), pl.BlockSpec((tm, D), row), pl.BlockSpec((tm, D), row),
                   pl.BlockSpec((1, D), const), pl.BlockSpec((1, D), const),
                   pl.BlockSpec((tm, AW), row), pl.BlockSpec((tm, LW), row),
                   pl.BlockSpec((tm, CW), row), pl.BlockSpec((tm, CW), lambda i: (i, 2)),
                   pl.BlockSpec((CW, CW), const), pl.BlockSpec((1, CW), const),
                   pl.BlockSpec((1, CW), const), pl.BlockSpec((1, CW), const)],
        out_shape=[jax.ShapeDtypeStruct((1, D), F32), jax.ShapeDtypeStruct((T, D), F32),
                   jax.ShapeDtypeStruct((T, D), BF16), jax.ShapeDtypeStruct((1, D), F32),
                   jax.ShapeDtypeStruct((1, D), F32),
                   jax.ShapeDtypeStruct((T, AW), F32), jax.ShapeDtypeStruct((T, LW), F32),
                   jax.ShapeDtypeStruct((T, CW), F32), jax.ShapeDtypeStruct((T, IN_TOTAL), BF16),
                   jax.ShapeDtypeStruct((CW, CW), F32), jax.ShapeDtypeStruct((1, CW), F32),
                   jax.ShapeDtypeStruct((1, CW), F32), jax.ShapeDtypeStruct((1, CW), F32)],
        scratch_shapes=[], semantics=("arbitrary",),
        args=[src] + [target] * per + [xhat, rstd, g, w_out, conv, proj, cln_g, cln_b, pw_w, pw_b])


def _dwout_bwd(yc, ya, yl, dzb, l):
    T = dzb.shape[0]
    tm = _pick(T, (384, 128))

    def body(yc_ref, ya_ref, yl_ref, dz_ref, o_ref):
        @pl.when(pl.program_id(0) == 0)
        def _():
            o_ref[...] = jnp.zeros_like(o_ref)

        cat = jnp.concatenate([yc_ref[...], ya_ref[...], yl_ref[...]], axis=1)
        o_ref[...] += _dot_tn(cat, dz_ref[...])

    row = lambda t: (t, 0)
    out = pl.pallas_call(
        body, name=f"dwout_bwd{l}", grid=(T // tm,),
        in_specs=[pl.BlockSpec((tm, CW), row), pl.BlockSpec((tm, AW), row), pl.BlockSpec((tm, LW), row),
                  pl.BlockSpec((tm, D), row)],
        out_specs=pl.BlockSpec((D, D), lambda t: (0, 0)),
        out_shape=jax.ShapeDtypeStruct((D, D), F32),
        compiler_params=_cp("arbitrary"),
    )(yc, ya, yl, dzb)
    return out.reshape(N_SHARD, 2, WOUT_SH // 2, D)


def _dh_bwd(dproj, w_in, dz, l, job=None):
    T = dproj.shape[0]
    tm = _pick(T, (1056, 384, 128))

    n_w = len(w_in)

    def body(dp_ref, *refs):
        w_refs, (dz_ref, o_ref, acc_ref) = refs[:n_w], refs[n_w:]
        j = pl.program_id(1)

        @pl.when(j == 0)
        def _():
            acc_ref[...] = ALPHA * dz_ref[...]

        dp = dp_ref[...]
        off = 0
        for w_ref in w_refs:
            rows = w_ref.shape[0]
            acc_ref[:, off:off + rows] += _dot_nt(dp, w_ref[...])
            off += rows

        @pl.when(j == N_SHARD - 1)
        def _():
            o_ref[...] = acc_ref[...]

    return _side_call(
        body, job, name=f"dh_bwd{l}", grid=(T // tm, N_SHARD),
        in_specs=[pl.BlockSpec((tm, WIN_SH), lambda i, j: (i, j))]
        + [pl.BlockSpec((None, w.shape[1], WIN_SH), lambda i, j: (j, 0, 0)) for w in w_in]
        + [pl.BlockSpec((tm, D), lambda i, j: (i, 0))],
        out_specs=[pl.BlockSpec((tm, D), lambda i, j: (i, 0))],
        out_shape=[jax.ShapeDtypeStruct((T, D), F32)],
        scratch_shapes=[pltpu.VMEM((tm, D), F32)],
        semantics=("parallel", "arbitrary"), args=[dproj, *w_in, dz])


def _dwin_bwd(hb, dproj, l):
    T = hb.shape[0]
    tm = _pick(T, (1056, 384, 128))

    def body(h_ref, dp_ref, o_ref):
        @pl.when(pl.program_id(1) == 0)
        def _():
            o_ref[...] = jnp.zeros_like(o_ref)

        o_ref[...] += _dot_tn(h_ref[...], dp_ref[...])

    out = pl.pallas_call(
        body, name=f"dwin_bwd{l}", grid=(N_SHARD, T // tm),
        in_specs=[pl.BlockSpec((tm, D), lambda j, t: (t, 0)),
                  pl.BlockSpec((tm, WIN_SH), lambda j, t: (t, j))],
        out_specs=pl.BlockSpec((None, D, WIN_SH), lambda j, t: (j, 0, 0)),
        out_shape=jax.ShapeDtypeStruct((N_SHARD, D, WIN_SH), F32),
        compiler_params=_cp("parallel", "arbitrary"),
    )(hb, dproj)
    return out.reshape(N_SHARD, 2, D // 2, WIN_SH)


def _dwin_half(hb, dproj, which, l, tag, job=None):
    T = hb.shape[0]
    tm = _pick(T, (1056, 384, 128))
    hr = D // 2

    def body(w_ref, h_ref, dp_ref, o_ref):
        @pl.when(pl.program_id(1) == 0)
        def _():
            o_ref[...] = jnp.zeros_like(o_ref)

        o_ref[...] += _dot_tn(h_ref[...], dp_ref[...])

    return _side_call(
        body, job, name=f"dwin_{tag}{l}", grid=(N_SHARD, T // tm),
        in_specs=[pl.BlockSpec((tm, hr), lambda j, t, w: (t, w[0])),
                  pl.BlockSpec((tm, WIN_SH), lambda j, t, w: (t, j))],
        out_specs=[pl.BlockSpec((None, hr, WIN_SH), lambda j, t, w: (j, 0, 0))],
        out_shape=[jax.ShapeDtypeStruct((N_SHARD, hr, WIN_SH), F32)],
        scratch_shapes=[], semantics=("parallel", "arbitrary"), args=[hb, dproj],
        prefetch=[jnp.reshape(which, (1,)).astype(jnp.int32)])


def _glu_masked(v, g, base_row):
    rows = _row_ids(v.shape[0], base_row)
    return jnp.where(rows >= PAD, v * _sigmoid(g), 0.0)


def _conv_tile(T):
    return _pick(T, (384, 128))


SUBLANES = 8


def _for_each_shift(buf, rot, tm, offsets, fn):
    for r in range(SUBLANES):
        group = [o for o in offsets if o % SUBLANES == r]
        if not group:
            continue
        if r == 0:
            src = buf
        else:
            n = tm + max(group) - r
            rot[0:n, :] = buf[r:r + n, :]
            src = rot
        for o in group:
            fn(o, src[o - r:o - r + tm, :])


def _conv_fwd(proj, dw_w, dw_b, ln_g, ln_b, pw_w, pw_b, l, job=None):
    T = proj.shape[0]
    tm = _conv_tile(T)
    hb = tm // HALO

    def body(cv_ref, cg_ref, ct_ref, hv_ref, hg_ref, w_ref, b_ref, g_ref, be_ref, pw_ref, pb_ref,
             yc_ref, conv_ref, buf, rot):
        i = pl.program_id(0)
        buf[0:HALO, :] = _glu_masked(hv_ref[...], hg_ref[...], i * tm - HALO)
        buf[HALO:HALO + tm, :] = _glu_masked(cv_ref[...], cg_ref[...], i * tm)
        first = HALO - (CONV_K - 1)
        total = [jnp.zeros((tm, CW), F32) + b_ref[...]]

        def tap(o, tile):
            k = o - first
            total[0] = total[0] + w_ref[k:k + 1, :] * tile

        _for_each_shift(buf, rot, tm, [first + k for k in range(CONV_K)], tap)
        acc = total[0]
        conv_ref[...] = acc
        u, _, _ = _ln_rows(acc, g_ref[...], be_ref[...])
        s = u * _sigmoid(u)
        cpw = _dot(s.astype(BF16), pw_ref[...]) + pb_ref[...]
        gate, _ = _silu_and_grad(ct_ref[...])
        yc_ref[...] = (cpw * gate).astype(BF16)

    vec = pl.BlockSpec((None, 1, CW), lambda i: (l, 0, 0))
    return _side_call(
        body, job, name=f"conv_fwd{l}", grid=(T // tm,),
        in_specs=[pl.BlockSpec((tm, CW), lambda i: (i, 0)),
                  pl.BlockSpec((tm, CW), lambda i: (i, 1)),
                  pl.BlockSpec((tm, CW), lambda i: (i, 2)),
                  pl.BlockSpec((HALO, CW), lambda i: (jnp.maximum(i * hb - 1, 0), 0)),
                  pl.BlockSpec((HALO, CW), lambda i: (jnp.maximum(i * hb - 1, 0), 1)),
                  pl.BlockSpec((None, CONV_K, CW), lambda i: (l, 0, 0)),
                  vec, vec, vec,
                  pl.BlockSpec((CW, CW), lambda i: (0, 0)),
                  vec],
        out_specs=[pl.BlockSpec((tm, CW), lambda i: (i, 0)), pl.BlockSpec((tm, CW), lambda i: (i, 0))],
        out_shape=[jax.ShapeDtypeStruct((T, CW), BF16), jax.ShapeDtypeStruct((T, CW), F32)],
        scratch_shapes=[pltpu.VMEM((tm + HALO, CW), F32), pltpu.VMEM((tm + HALO, CW), F32)],
        semantics=("parallel",), args=[proj, proj, proj, proj, proj, dw_w, dw_b, ln_g, ln_b, pw_w, pw_b])


def _conv_bwd_taps(d_conv, proj, dw_w, dproj, l, job=None):
    T = d_conv.shape[0]
    tm = _conv_tile(T)
    hb = tm // HALO
    nt = T // tm
    last_halo = T // HALO - 1

    def body(dc_ref, dh_ref, cv_ref, cg_ref, hv_ref, hg_ref, w_ref, _, o_ref, dw_ref, dwb_ref, cbuf, dbuf, rot):
        i = pl.program_id(0)

        @pl.when(i == 0)
        def _():
            dw_ref[...] = jnp.zeros_like(dw_ref)
            dwb_ref[...] = jnp.zeros_like(dwb_ref)

        cbuf[0:HALO, :] = _glu_masked(hv_ref[...], hg_ref[...], i * tm - HALO)
        cbuf[HALO:HALO + tm, :] = _glu_masked(cv_ref[...], cg_ref[...], i * tm)
        dmain = dc_ref[...]
        dbuf[0:tm, :] = dmain
        dbuf[tm:tm + HALO, :] = jnp.where(i < nt - 1, dh_ref[...], 0.0)
        total = [jnp.zeros((tm, CW), F32)]

        def tap_back(o, tile):
            k = CONV_K - 1 - o
            total[0] = total[0] + w_ref[k:k + 1, :] * tile

        _for_each_shift(dbuf, rot, tm, list(range(CONV_K)), tap_back)
        acc = total[0]
        first = HALO - (CONV_K - 1)

        def tap_weight(o, tile):
            k = o - first
            dw_ref[k:k + 1, :] += _colsum(dmain * tile)

        _for_each_shift(cbuf, rot, tm, [first + k for k in range(CONV_K)], tap_weight)
        dwb_ref[...] += _colsum(dmain)
        d_c = jnp.where(_row_ids(tm, i * tm) >= PAD, acc, 0.0)
        sig = _sigmoid(cg_ref[...])
        o_ref[:, 0:CW] = (d_c * sig).astype(BF16)
        o_ref[:, CW:2 * CW] = (d_c * cv_ref[...] * sig * (1.0 - sig)).astype(BF16)

    const = lambda i: (0, 0)
    return _side_call(
        body, job, name=f"conv_bwd_taps{l}", grid=(nt,),
        in_specs=[pl.BlockSpec((tm, CW), lambda i: (i, 0)),
                  pl.BlockSpec((HALO, CW), lambda i: (jnp.minimum((i + 1) * hb, last_halo), 0)),
                  pl.BlockSpec((tm, CW), lambda i: (i, 0)),
                  pl.BlockSpec((tm, CW), lambda i: (i, 1)),
                  pl.BlockSpec((HALO, CW), lambda i: (jnp.maximum(i * hb - 1, 0), 0)),
                  pl.BlockSpec((HALO, CW), lambda i: (jnp.maximum(i * hb - 1, 0), 1)),
                  pl.BlockSpec((None, CONV_K, CW), lambda i: (l, 0, 0)),
                  pl.BlockSpec(memory_space=pl.ANY)],
        out_specs=[pl.BlockSpec((tm, 2 * CW), lambda i: (i, 0)),
                   pl.BlockSpec((HALO, CW), const), pl.BlockSpec((1, CW), const)],
        out_shape=[jax.ShapeDtypeStruct(dproj.shape, BF16), jax.ShapeDtypeStruct((HALO, CW), F32),
                   jax.ShapeDtypeStruct((1, CW), F32)],
        scratch_shapes=[pltpu.VMEM((tm + HALO, CW), F32), pltpu.VMEM((tm + HALO, CW), F32),
                        pltpu.VMEM((tm + HALO, CW), F32)],
        semantics=("arbitrary",), aliases={7: 0},
        args=[d_conv, d_conv, proj, proj, proj, proj, dw_w, dproj])


def _log1p_small(e):
    return jnp.where(e < 1e-3, e * (1.0 - e * (0.5 - e * (1.0 / 3.0))), jnp.log(1.0 + e))


def _softplus(z):
    return jnp.maximum(z, 0.0) + _log1p_small(jnp.exp(-jnp.abs(z)))


def _neg_expm1(x):
    series = -x * (1.0 + x * (1.0 / 2.0) * (1.0 + x * (1.0 / 3.0) * (1.0 + x * (1.0 / 4.0) * (
        1.0 + x * (1.0 / 5.0) * (1.0 + x * (1.0 / 6.0) * (1.0 + x * (1.0 / 7.0)))))))
    return jnp.where(x > -0.25, series, 1.0 - jnp.exp(x))


def _lru_gates(rxbuf, tm, base_row, lw_ref, lb_ref, wa_ref, ba_ref, wx_ref, bx_ref, lam_ref):
    rc = jnp.zeros((tm, LW), F32) + lb_ref[...]
    for k in range(LRU_K):
        o = LHALO - (LRU_K - 1) + k
        rc += lw_ref[k:k + 1, :] * rxbuf[o:o + tm, :]
    rcb = rc.astype(BF16)
    r = _sigmoid(_dot(rcb, wa_ref[...]) + ba_ref[...])
    ig = _sigmoid(_dot(rcb, wx_ref[...]) + bx_ref[...])
    sp = _softplus(-lam_ref[...])
    la = -LRU_C * r * sp
    a = jnp.exp(la)
    mult = jnp.sqrt(_neg_expm1(2.0 * la))
    valid = _row_ids(tm, base_row) >= PAD
    return rc, rcb, r, ig, sp, a, mult, valid


def _mask_rows(v, base_row):
    return jnp.where(_row_ids(v.shape[0], base_row) >= PAD, v, 0.0)


def _scan_rows(aa, bb, carry, out_ref, reverse):
    tm = aa.shape[0]
    sub = _row_ids(tm, 0) & (SUBLANES - 1)
    s = 1
    while s < SUBLANES:
        keep = (sub < SUBLANES - s) if reverse else (sub >= s)
        shift = tm - s if reverse else s
        a_s = jnp.where(keep, pltpu.roll(aa, shift, axis=0), 1.0)
        b_s = jnp.where(keep, pltpu.roll(bb, shift, axis=0), 0.0)
        bb = aa * b_s + bb
        aa = aa * a_s
        s *= 2
    groups = range(tm // SUBLANES)
    edge = 0 if reverse else SUBLANES - 1
    for j in (reversed(groups) if reverse else groups):
        rows = slice(SUBLANES * j, SUBLANES * j + SUBLANES)
        x = bb[rows] + aa[rows] * carry
        out_ref[rows, :] = x
        carry = x[edge:edge + 1]


def _lru_tile(T):
    return _pick(T, (384, 128))


def _lru_fwd(proj, lw, lb, wa, ba, wx, bx, lam, l, job=None):
    T = proj.shape[0]
    tm = _lru_tile(T)
    hb = tm // LHALO

    def body(rx_ref, rg_ref, hx_ref, lw_ref, lb_ref, wa_ref, ba_ref, wx_ref, bx_ref, lam_ref,
             yl_ref, hl_ref, rxbuf, carry):
        i = pl.program_id(0)

        @pl.when(i == 0)
        def _():
            carry[...] = jnp.zeros_like(carry)

        rxbuf[0:LHALO, :] = _mask_rows(hx_ref[...], i * tm - LHALO)
        rxbuf[LHALO:LHALO + tm, :] = _mask_rows(rx_ref[...], i * tm)
        rc, _, _, ig, _, a, mult, valid = _lru_gates(rxbuf, tm, i * tm, lw_ref, lb_ref, wa_ref, ba_ref,
                                                     wx_ref, bx_ref, lam_ref)
        bb = jnp.where(valid, mult * (ig * rc), 0.0)
        _scan_rows(a, bb, carry[0:1, :], hl_ref, reverse=False)
        carry[0:1, :] = hl_ref[tm - 1:tm, :]
        gate, _ = _silu_and_grad(rg_ref[...])
        yl_ref[...] = (hl_ref[...] * gate).astype(BF16)

    vec = pl.BlockSpec((None, 1, LW), lambda i: (l, 0, 0))
    mat = pl.BlockSpec((None, LW, LW), lambda i: (l, 0, 0))
    return _side_call(
        body, job, name=f"lru_fwd{l}", grid=(T // tm,),
        in_specs=[pl.BlockSpec((tm, LW), lambda i: (i, 8)),
                  pl.BlockSpec((tm, LW), lambda i: (i, 9)),
                  pl.BlockSpec((LHALO, LW), lambda i: (jnp.maximum(i * hb - 1, 0), 8)),
                  pl.BlockSpec((None, LRU_K, LW), lambda i: (l, 0, 0)),
                  vec, mat, vec, mat, vec, vec],
        out_specs=[pl.BlockSpec((tm, LW), lambda i: (i, 0)), pl.BlockSpec((tm, LW), lambda i: (i, 0))],
        out_shape=[jax.ShapeDtypeStruct((T, LW), BF16), jax.ShapeDtypeStruct((T, LW), F32)],
        scratch_shapes=[pltpu.VMEM((tm + LHALO, LW), F32), pltpu.VMEM((8, LW), F32)],
        semantics=("arbitrary",), args=[proj, proj, proj, lw, lb, wa, ba, wx, bx, lam])


def _lru_bwd(proj, hl, d_yl, lw, lb, wa, ba, wx, bx, lam, dproj, l, job=None):
    T = proj.shape[0]
    tm = _lru_tile(T)
    hb = tm // LHALO
    nt = T // tm

    def body(rx_ref, rg_ref, hx_ref, hl_ref, hh_ref, dy_ref, lw_ref, lb_ref, wa_ref, ba_ref, wx_ref, bx_ref,
             lam_ref, _, o_ref, dlw_ref, dlb_ref, dwa_ref, dba_ref, dwx_ref, dbx_ref, dlam_ref,
             rxbuf, dbuf, carry, head, gbuf):
        step = pl.program_id(0)
        i = nt - 1 - step

        @pl.when(step == 0)
        def _():
            carry[...] = jnp.zeros_like(carry)
            head[...] = jnp.zeros_like(head)
            for ref in (dlw_ref, dlb_ref, dwa_ref, dba_ref, dwx_ref, dbx_ref, dlam_ref):
                ref[...] = jnp.zeros_like(ref)

        rxbuf[0:LHALO, :] = _mask_rows(hx_ref[...], i * tm - LHALO)
        rxbuf[LHALO:LHALO + tm, :] = _mask_rows(rx_ref[...], i * tm)
        rc, rcb, r, ig, sp, a, mult, valid = _lru_gates(rxbuf, tm, i * tm, lw_ref, lb_ref, wa_ref, ba_ref,
                                                        wx_ref, bx_ref, lam_ref)
        rows = _row_ids(tm, 0)
        h = hl_ref[...]
        h_before = jnp.where(i > 0, hh_ref[LHALO - 1:LHALO, :], 0.0)
        hprev = jnp.where(rows == 0, h_before, pltpu.roll(h, 1, axis=0))
        rg = rg_ref[...]
        gate, dgate = _silu_and_grad(rg)
        dy = dy_ref[...]
        o_ref[:, LW:2 * LW] = (dy * h * dgate).astype(BF16)
        bb = dy * gate + jnp.where(rows == tm - 1, carry[0:1, :], 0.0)
        aa = jnp.where(rows == tm - 1, 0.0, pltpu.roll(a, tm - 1, axis=0))
        _scan_rows(aa, bb, jnp.zeros((1, LW), F32), gbuf, reverse=True)
        g = gbuf[...]
        dbuf[0:tm, :] = a * g
        carry[0:1, :] = dbuf[0:1, :]
        du = jnp.where(valid, g, 0.0)
        da = g * hprev
        dix = du * mult
        dmult = du * (ig * rc)
        dla = jnp.where(valid, da * a - dmult * (a * a) / mult, 0.0)
        dr = dla * (-LRU_C * sp)
        dlam_ref[...] += _colsum(dla * (LRU_C * r)) * _sigmoid(-lam_ref[...])
        dpa = dr * r * (1.0 - r)
        dpx = (dix * rc) * ig * (1.0 - ig)
        dpab = dpa.astype(BF16)
        dpxb = dpx.astype(BF16)
        dba_ref[...] += _colsum(dpa)
        dbx_ref[...] += _colsum(dpx)
        dwa_ref[...] += _dot_tn(rcb, dpab)
        dwx_ref[...] += _dot_tn(rcb, dpxb)
        drc = dix * ig + _dot_nt(dpab, wa_ref[...]) + _dot_nt(dpxb, wx_ref[...])
        dbuf[0:tm, :] = drc
        dbuf[tm:tm + LHALO, :] = head[...]
        acc = jnp.zeros((tm, LW), F32)
        for k in range(LRU_K):
            o = LRU_K - 1 - k
            acc += lw_ref[k:k + 1, :] * dbuf[o:o + tm, :]
            oc = LHALO - (LRU_K - 1) + k
            dlw_ref[k:k + 1, :] += _colsum(drc * rxbuf[oc:oc + tm, :])
        dlb_ref[...] += _colsum(drc)
        head[...] = dbuf[0:LHALO, :]
        o_ref[:, 0:LW] = jnp.where(valid, acc, 0.0).astype(BF16)

    rev = lambda s: nt - 1 - s
    vec = pl.BlockSpec((None, 1, LW), lambda s: (l, 0, 0))
    mat = pl.BlockSpec((None, LW, LW), lambda s: (l, 0, 0))
    const = lambda s: (0, 0)
    halo = lambda s: jnp.maximum(rev(s) * hb - 1, 0)
    return _side_call(
        body, job, name=f"lru_bwd{l}", grid=(nt,),
        in_specs=[pl.BlockSpec((tm, LW), lambda s: (rev(s), 8)),
                  pl.BlockSpec((tm, LW), lambda s: (rev(s), 9)),
                  pl.BlockSpec((LHALO, LW), lambda s: (halo(s), 8)),
                  pl.BlockSpec((tm, LW), lambda s: (rev(s), 0)),
                  pl.BlockSpec((LHALO, LW), lambda s: (halo(s), 0)),
                  pl.BlockSpec((tm, LW), lambda s: (rev(s), 0)),
                  pl.BlockSpec((None, LRU_K, LW), lambda s: (l, 0, 0)),
                  vec, mat, vec, mat, vec, vec, pl.BlockSpec(memory_space=pl.ANY)],
        out_specs=[pl.BlockSpec((tm, 2 * LW), lambda s: (rev(s), 4)),
                   pl.BlockSpec((8, LW), const), pl.BlockSpec((1, LW), const),
                   pl.BlockSpec((LW, LW), const), pl.BlockSpec((1, LW), const),
                   pl.BlockSpec((LW, LW), const), pl.BlockSpec((1, LW), const),
                   pl.BlockSpec((1, LW), const)],
        out_shape=[jax.ShapeDtypeStruct(dproj.shape, BF16),
                   jax.ShapeDtypeStruct((8, LW), F32), jax.ShapeDtypeStruct((1, LW), F32),
                   jax.ShapeDtypeStruct((LW, LW), F32), jax.ShapeDtypeStruct((1, LW), F32),
                   jax.ShapeDtypeStruct((LW, LW), F32), jax.ShapeDtypeStruct((1, LW), F32),
                   jax.ShapeDtypeStruct((1, LW), F32)],
        scratch_shapes=[pltpu.VMEM((tm + LHALO, LW), F32), pltpu.VMEM((tm + LHALO, LW), F32),
                        pltpu.VMEM((8, LW), F32), pltpu.VMEM((LHALO, LW), F32), pltpu.VMEM((tm, LW), F32)],
        semantics=("arbitrary",), aliases={13: 0},
        args=[proj, proj, proj, hl, hl, d_yl, lw, lb, wa, ba, wx, bx, lam, dproj])


def _rope_tables(T):
    pos = (lax.broadcasted_iota(jnp.int32, (T, 128), 0) - PAD).astype(F32)
    lane = lax.broadcasted_iota(jnp.int32, (T, 128), 1) % 64
    inv_freq = ROPE_THETA ** (-(lane % ROT_HALF).astype(F32) / ROT_HALF)
    ang = pos * inv_freq
    cos, sin = jnp.cos(ang), jnp.sin(ang)
    c = jnp.where(lane < 2 * ROT_HALF, cos, 1.0)
    s1 = jnp.where(lane < ROT_HALF, -sin, 0.0)
    s2 = jnp.where((lane >= ROT_HALF) & (lane < 2 * ROT_HALF), sin, 0.0)
    return c, s1, s2


def _rot_fwd(x, c, s1, s2):
    return x * c + pltpu.roll(x, 128 - ROT_HALF, axis=1) * s1 + pltpu.roll(x, ROT_HALF, axis=1) * s2


def _rot_bwd(dy, c, s1, s2):
    return dy * c + pltpu.roll(dy * s1, ROT_HALF, axis=1) + pltpu.roll(dy * s2, 128 - ROT_HALF, axis=1)


KV2 = 2 * KVW


def _rope_fwd(proj, tabs, l):
    T = proj.shape[0]

    def both_halves(x, o_ref, pg):
        lane = lax.broadcasted_iota(jnp.int32, (1, 128), 1)
        for off in range(2):
            half = jnp.where((lane < 64) if off == 0 else (lane >= 64), x, 0.0)
            g = 2 * pg + off
            o_ref[:, 128 * g:128 * g + 128] = (half + pltpu.roll(half, 64, axis=1)).astype(BF16)

    def body(ql_ref, qh_ref, k_ref, v_ref, c_ref, s1_ref, s2_ref, qr_ref, kr_ref, vb_ref, kr2_ref, vb2_ref):
        c, s1, s2 = c_ref[...], s1_ref[...], s2_ref[...]
        for gcol in range(AW // 128):
            src = ql_ref if gcol < 4 else qh_ref
            x = src[:, 128 * (gcol % 4):128 * (gcol % 4) + 128]
            qr_ref[:, 128 * gcol:128 * gcol + 128] = (_rot_fwd(x, c, s1, s2) * 0.125).astype(BF16)
        for pg in range(KVW // 128):
            cols = slice(128 * pg, 128 * pg + 128)
            k = _rot_fwd(k_ref[:, cols], c, s1, s2)
            kr_ref[:, cols] = k.astype(BF16)
            both_halves(k, kr2_ref, pg)
            vb_ref[:, cols] = v_ref[:, cols].astype(BF16)
            both_halves(v_ref[:, cols], vb2_ref, pg)

    tr = _pick(T, (384, 128))
    tab = pl.BlockSpec((tr, 128), lambda n: (n, 0))
    return pl.pallas_call(
        body, name=f"rope_fwd{l}", grid=(T // tr,),
        in_specs=[pl.BlockSpec((tr, 512), lambda n: (n, 3)), pl.BlockSpec((tr, 512), lambda n: (n, 4)),
                  pl.BlockSpec((tr, KVW), lambda n: (n, 10)), pl.BlockSpec((tr, KVW), lambda n: (n, 11)),
                  tab, tab, tab],
        out_specs=[pl.BlockSpec((tr, AW), lambda n: (n, 0)), pl.BlockSpec((tr, KVW), lambda n: (n, 0)),
                   pl.BlockSpec((tr, KVW), lambda n: (n, 0)), pl.BlockSpec((tr, KV2), lambda n: (n, 0)),
                   pl.BlockSpec((tr, KV2), lambda n: (n, 0))],
        out_shape=[jax.ShapeDtypeStruct((T, AW), BF16), jax.ShapeDtypeStruct((T, KVW), BF16),
                   jax.ShapeDtypeStruct((T, KVW), BF16), jax.ShapeDtypeStruct((T, KV2), BF16),
                   jax.ShapeDtypeStruct((T, KV2), BF16)],
        compiler_params=_cp("parallel"),
    )(proj, proj, proj, proj, *tabs)


GROUP = 4


def _attn_mask(n, reps):
    qi = lax.broadcasted_iota(jnp.int32, (reps * BLK, BLK), 0) & (BLK - 1)
    kj = lax.broadcasted_iota(jnp.int32, (reps * BLK, BLK), 1)
    m0 = (kj >= PAD) & (n >= 1)
    mp = (kj > qi) & (n >= 2)
    mc = (kj <= qi) & ((n >= 1) | (kj >= PAD))
    return jnp.concatenate([m0, mp, mc], axis=1)


def _kv_both(x0_ref, xp_ref, xc_ref, g):
    if x0_ref.shape[1] == KV2:
        cols = slice(128 * g, 128 * g + 128)
        return jnp.concatenate([x0_ref[:, cols], xp_ref[:, cols], xc_ref[:, cols]], axis=0)
    pg, off = g // 2, g % 2
    cols = slice(128 * pg, 128 * pg + 128)
    x = jnp.concatenate([x0_ref[:, cols], xp_ref[:, cols], xc_ref[:, cols]], axis=0).astype(F32)
    lane = lax.broadcasted_iota(jnp.int32, (1, 128), 1)
    half = jnp.where((lane < 64) if off == 0 else (lane >= 64), x, 0.0)
    return (half + pltpu.roll(half, 64, axis=1)).astype(BF16)


def _stack_heads(a, b):
    lo = lax.broadcasted_iota(jnp.int32, (1, 128), 1) < 64
    a, b = a.astype(F32), b.astype(F32)
    return jnp.concatenate([jnp.where(lo, a, 0.0), jnp.where(lo, 0.0, a),
                            jnp.where(lo, b, 0.0), jnp.where(lo, 0.0, b)], axis=0).astype(BF16)


def _unstack_heads(x):
    lo = lax.broadcasted_iota(jnp.int32, (1, 128), 1) < 64
    return (jnp.where(lo, x[0:BLK], x[BLK:2 * BLK]), jnp.where(lo, x[2 * BLK:3 * BLK], x[3 * BLK:4 * BLK]))


def _per_head_column(values):
    return jnp.concatenate([jnp.zeros((BLK, 1), F32) + v for v in values], axis=0)


def _attn_fwd(qr, kr, vb, proj, sinks, l, job=None):
    T = qr.shape[0]

    def body(sink_ref, q_ref, k0_ref, kp_ref, kc_ref, v0_ref, vp_ref, vc_ref, ag_ref, ya_ref, att_ref, lse_ref):
        n = pl.program_id(0)
        mask = _attn_mask(n, 1)
        lane = lax.broadcasted_iota(jnp.int32, (1, 128), 1)
        lse_acc = jnp.zeros((BLK, 128), F32)
        for g in range(4):
            kx = _kv_both(k0_ref, kp_ref, kc_ref, g)
            vx = _kv_both(v0_ref, vp_ref, vc_ref, g)
            pair_cols = [slice(128 * (2 * g + pp), 128 * (2 * g + pp) + 128) for pp in range(2)]
            s4 = _dot_nt(_stack_heads(q_ref[:, pair_cols[0]], q_ref[:, pair_cols[1]]), kx)
            probs = []
            for r in range(GROUP):
                h = GROUP * g + r
                sink = sink_ref[l, h]
                s = jnp.where(mask, s4[BLK * r:BLK * r + BLK], NEG_INF)
                m = jnp.maximum(jnp.max(s, axis=1, keepdims=True), sink)
                p = jnp.exp(s - m)
                denom = jnp.sum(p, axis=1, keepdims=True) + jnp.exp(sink - m)
                probs.append((p * (1.0 / denom)).astype(BF16))
                lse_acc = jnp.where(lane == h, m + jnp.log(denom), lse_acc)
            outs = _unstack_heads(_dot(jnp.concatenate(probs, axis=0), vx))
            for cols, out in zip(pair_cols, outs):
                att_ref[:, cols] = out
                gate, _ = _silu_and_grad(ag_ref[:, cols])
                ya_ref[:, cols] = (out * gate).astype(BF16)
        lse_ref[...] = lse_acc

    prev = lambda n: (jnp.maximum(n - 1, 0), 0)
    cur = lambda n: (n, 0)
    zero = lambda n: (0, 0)
    kv = lambda f: pl.BlockSpec((BLK, KV2), f)
    return _side_call(
        body, job, name=f"attn_fwd{l}", grid=(T // BLK,),
        in_specs=[pl.BlockSpec(memory_space=pltpu.SMEM),
                  pl.BlockSpec((BLK, AW), cur), kv(zero), kv(prev), kv(cur), kv(zero), kv(prev), kv(cur),
                  pl.BlockSpec((BLK, AW), lambda n: (n, 3))],
        out_specs=[pl.BlockSpec((BLK, AW), cur), pl.BlockSpec((BLK, AW), cur), pl.BlockSpec((BLK, 128), cur)],
        out_shape=[jax.ShapeDtypeStruct((T, AW), BF16), jax.ShapeDtypeStruct((T, AW), F32),
                   jax.ShapeDtypeStruct((T, 128), F32)],
        scratch_shapes=[], semantics=("parallel",), args=[sinks, qr, kr, kr, kr, vb, vb, vb, proj])


def _attn_bwd(qr, kr, vb, proj, att, lse, d_ya, sinks, dproj, l, job=None):
    T = qr.shape[0]
    nb = T // BLK

    def body(sink_ref, q_ref, k0_ref, kp_ref, kc_ref, v0_ref, vp_ref, vc_ref, ag_ref, att_ref, lse_ref, dy_ref, _,
             dq_ref, dk_ref, dv_ref, dk0_ref, dv0_ref, dag_ref, dsink_ref, kcarry, vcarry):
        n = pl.program_id(0)

        @pl.when(n == 0)
        def _():
            dk0_ref[...] = jnp.zeros_like(dk0_ref)
            dv0_ref[...] = jnp.zeros_like(dv0_ref)
            dsink_ref[...] = jnp.zeros_like(dsink_ref)
            kcarry[...] = jnp.zeros_like(kcarry)
            vcarry[...] = jnp.zeros_like(vcarry)

        @pl.when(n == nb)
        def _():
            dk_ref[...] = kcarry[...]
            dv_ref[...] = vcarry[...]

        @pl.when(n < nb)
        def _():
            mask = _attn_mask(n, GROUP)
            lane = lax.broadcasted_iota(jnp.int32, (1, 128), 1)
            lse = lse_ref[...]
            dsink = jnp.zeros((1, 128), F32)
            dk_pg, dv_pg = [], []
            for pg in range(2):
                dk_acc = jnp.zeros((3 * BLK, 128), F32)
                dv_acc = jnp.zeros((3 * BLK, 128), F32)
                for off in range(2):
                    g = 2 * pg + off
                    kx = _kv_both(k0_ref, kp_ref, kc_ref, g)
                    vx = _kv_both(v0_ref, vp_ref, vc_ref, g)
                    pair_cols = [slice(128 * (2 * g + pp), 128 * (2 * g + pp) + 128) for pp in range(2)]
                    q4 = _stack_heads(q_ref[:, pair_cols[0]], q_ref[:, pair_cols[1]])
                    d_out = []
                    for cols in pair_cols:
                        gate, dgate = _silu_and_grad(ag_ref[:, cols])
                        dy = dy_ref[:, cols]
                        dag_ref[:, cols] = (dy * att_ref[:, cols] * dgate).astype(BF16)
                        d_out.append(dy * gate)
                    do4 = _stack_heads(d_out[0], d_out[1])
                    heads = [GROUP * g + r for r in range(GROUP)]
                    sink = _per_head_column([sink_ref[l, h] for h in heads])
                    lse4 = _per_head_column(
                        [jnp.sum(jnp.where(lane == h, lse, 0.0), axis=1, keepdims=True) for h in heads])
                    p = jnp.where(mask, jnp.exp(_dot_nt(q4, kx) - lse4), 0.0)
                    dp = _dot_nt(do4, vx)
                    delta = jnp.sum(p * dp, axis=1, keepdims=True)
                    ds = (p * (dp - delta)).astype(BF16)
                    sink_term = jnp.exp(sink - lse4) * delta
                    for r, h in enumerate(heads):
                        dsink += jnp.where(lane == h, -jnp.sum(sink_term[BLK * r:BLK * r + BLK]), 0.0)
                    for cols, dq in zip(pair_cols, _unstack_heads(_dot(ds, kx))):
                        dq_ref[:, cols] = dq
                    dkg = _dot_tn(ds, q4)
                    dvg = _dot_tn(p.astype(BF16), do4)
                    own = (lane < 64) if off == 0 else (lane >= 64)
                    dk_acc += jnp.where(own, dkg + pltpu.roll(dkg, 64, axis=1), 0.0)
                    dv_acc += jnp.where(own, dvg + pltpu.roll(dvg, 64, axis=1), 0.0)
                dk_pg.append(dk_acc)
                dv_pg.append(dv_acc)
            dsink_ref[...] += dsink
            for pg in range(2):
                cols = slice(128 * pg, 128 * pg + 128)
                dk0_ref[:, cols] += dk_pg[pg][0:BLK]
                dv0_ref[:, cols] += dv_pg[pg][0:BLK]
                dk_ref[:, cols] = kcarry[:, cols] + dk_pg[pg][BLK:2 * BLK]
                dv_ref[:, cols] = vcarry[:, cols] + dv_pg[pg][BLK:2 * BLK]
                kcarry[:, cols] = dk_pg[pg][2 * BLK:3 * BLK]
                vcarry[:, cols] = dv_pg[pg][2 * BLK:3 * BLK]

    last = nb - 1
    cur = lambda n: (jnp.minimum(n, last), 0)
    prev = lambda n: (jnp.clip(n - 1, 0, last), 0)
    zero = lambda n: (0, 0)
    kv = lambda f: pl.BlockSpec((BLK, KVW), f)
    wide = lambda f: pl.BlockSpec((BLK, AW), f)
    return _side_call(
        body, job, name=f"attn_bwd{l}", grid=(nb + 1,),
        in_specs=[pl.BlockSpec(memory_space=pltpu.SMEM),
                  wide(cur), kv(zero), kv(prev), kv(cur), kv(zero), kv(prev), kv(cur),
                  pl.BlockSpec((BLK, AW), lambda n: (jnp.minimum(n, last), 3)),
                  wide(cur), pl.BlockSpec((BLK, 128), cur), wide(cur), pl.BlockSpec(memory_space=pl.ANY)],
        out_specs=[wide(cur), kv(prev), kv(prev), kv(zero), kv(zero),
                   pl.BlockSpec((BLK, AW), lambda n: (jnp.minimum(n, last), 3)),
                   pl.BlockSpec((1, 128), zero)],
        out_shape=[jax.ShapeDtypeStruct((T, AW), F32), jax.ShapeDtypeStruct((T, KVW), F32),
                   jax.ShapeDtypeStruct((T, KVW), F32), jax.ShapeDtypeStruct((BLK, KVW), F32),
                   jax.ShapeDtypeStruct((BLK, KVW), F32), jax.ShapeDtypeStruct(dproj.shape, BF16),
                   jax.ShapeDtypeStruct((1, 128), F32)],
        scratch_shapes=[pltpu.VMEM((BLK, KVW), F32), pltpu.VMEM((BLK, KVW), F32)],
        semantics=("arbitrary",), aliases={12: 5},
        args=[sinks, qr, kr, kr, kr, vb, vb, vb, proj, att, lse, d_ya, dproj])


def _rope_bwd(dqr, dk, dv, dk0, dv0, tabs, dproj, l):
    T = dqr.shape[0]

    def body(dq_ref, dk_ref, dv_ref, dk0_ref, dv0_ref, c_ref, s1_ref, s2_ref, _, o_ref):
        n = pl.program_id(0)
        c, s1, s2 = c_ref[...], s1_ref[...], s2_ref[...]
        for gcol in range(AW // 128):
            cols = slice(128 * gcol, 128 * gcol + 128)
            o_ref[:, cols] = (_rot_bwd(dq_ref[:, cols], c, s1, s2) * 0.125).astype(BF16)
        for gcol in range(KVW // 128):
            cols = slice(128 * gcol, 128 * gcol + 128)
            kcols = slice(AW + 128 * gcol, AW + 128 * gcol + 128)
            vcols = slice(AW + KVW + 128 * gcol, AW + KVW + 128 * gcol + 128)
            o_ref[:, kcols] = _rot_bwd(dk_ref[:, cols], c, s1, s2).astype(BF16)
            o_ref[:, vcols] = dv_ref[:, cols].astype(BF16)

            @pl.when(n == 0)
            def _():
                dkk = dk_ref[0:BLK, cols] + dk0_ref[:, cols]
                o_ref[0:BLK, kcols] = _rot_bwd(dkk, c[0:BLK], s1[0:BLK], s2[0:BLK]).astype(BF16)
                o_ref[0:BLK, vcols] = (dv_ref[0:BLK, cols] + dv0_ref[:, cols]).astype(BF16)

    tr = _pick(T, (384, 128))
    cur = lambda n: (n, 0)
    zero = lambda n: (0, 0)
    tab = pl.BlockSpec((tr, 128), cur)
    return pl.pallas_call(
        body, name=f"rope_bwd{l}", grid=(T // tr,),
        in_specs=[pl.BlockSpec((tr, AW), cur), pl.BlockSpec((tr, KVW), cur), pl.BlockSpec((tr, KVW), cur),
                  pl.BlockSpec((BLK, KVW), zero), pl.BlockSpec((BLK, KVW), zero), tab, tab, tab,
                  pl.BlockSpec(memory_space=pl.ANY)],
        out_specs=pl.BlockSpec((tr, AW + 2 * KVW), lambda n: (n, 1)),
        out_shape=jax.ShapeDtypeStruct(dproj.shape, BF16),
        input_output_aliases={8: 0},
        compiler_params=_cp("parallel"),
    )(dqr, dk, dv, dk0, dv0, *tabs, dproj)


def _block_diag(w):
    nl, nh, hd, _ = w.shape
    eye = jnp.eye(nh, dtype=w.dtype)
    return jnp.einsum("lhij,hg->lhigj", w, eye).reshape(nl, nh * hd, nh * hd)


def _diag_blocks(m):
    nh, hd = 8, 64
    return jnp.einsum("hihj->hij", m.reshape(nh, hd, nh, hd))


def _device_step(x, target, p, dist=None):
    vec = lambda a: a.reshape(DEPTH, 1, a.shape[-1])
    ln_in_g, ln_in_b = p["ln_in_g"].reshape(1, D), p["ln_in_b"].reshape(1, D)
    conv_dw_b, conv_ln_g, conv_ln_b, conv_pw_b = map(vec, (p["conv_dw_b"], p["conv_ln_g"], p["conv_ln_b"], p["conv_pw_b"]))
    lru_conv_b, lru_ba, lru_bx, lru_lambda = map(vec, (p["lru_conv_b"], p["lru_ba"], p["lru_bx"], p["lru_lambda"]))
    ln_post_g, ln_post_b = vec(p["ln_post_g"]), vec(p["ln_post_b"])
    wa_bd = _block_diag(p["lru_wa"]).astype(BF16)
    wx_bd = _block_diag(p["lru_wx"]).astype(BF16)
    w_in, w_out, pw_w = list(p["w_in"]), list(p["w_out"]), list(p["conv_pw_w"])
    sinks = p["attn_sinks"]
    big_names = ("w_in", "w_out", "conv_pw_w")

    order = dist[4] if dist else jnp.arange(N_SHARD, dtype=jnp.int32)
    (h, hb), got = _embed_fwd(x, p["meta_tokens"], ln_in_g, ln_in_b,
                              job=_gather_job([w_in[0]], peers=(0, 1)) if dist else None)
    if dist:
        w_in[0] = got[0]
    T = h.shape[0]
    tabs = _rope_tables(T)
    saved = []
    for l in range(DEPTH):
        if l == 0:
            job = _join_jobs(_gather_job([w_in[0]], peers=(2,)), _gather_job([pw_w[0]])) if dist else None
            (proj,), got = _proj_fwd(hb, w_in[0], order, 0, N_SHARD - 1, None, l, job=job)
            if dist:
                w_in[0], pw_w[0] = got
            (proj,), _ = _proj_fwd(hb, w_in[0], order, N_SHARD - 1, 1, proj, l)
        else:
            (proj,), _ = _proj_fwd(hb, w_in[l], order, 0, N_SHARD, None, l)
        pw_l = pw_w[l].reshape(CW, CW)
        (yc, conv), got = _conv_fwd(proj, p["conv_dw_w"], conv_dw_b, conv_ln_g, conv_ln_b, pw_l, conv_pw_b, l,
                                    job=_gather_job([w_out[0]]) if dist and l == 0 else None)
        if got:
            w_out[0] = got[0]
        qr, kr, vb, kr2, vb2 = _rope_fwd(proj, tabs, l)
        (ya, att, lse), got = _attn_fwd(
            qr, kr2, vb2, proj, sinks, l, job=_gather_job([w_in[1]], peers=(0, 1)) if dist and l == 0 else None)
        if got:
            w_in[1] = got[0]
        (yl, hl), got = _lru_fwd(proj, p["lru_conv_w"], lru_conv_b, wa_bd, lru_ba, wx_bd, lru_bx, lru_lambda, l,
                                 job=_gather_job([w_out[1], pw_w[1]]) if dist and l == 0 else None)
        if got:
            w_out[1], pw_w[1] = got
        (hn, hnb, xhat, rstd), got = _out_fwd(
            yc, ya, yl, w_out[l], h, ln_post_g, ln_post_b, l,
            job=_gather_job([w_in[1]], peers=(2,)) if dist and l == 0 else None)
        if got:
            w_in[1] = got[0]
        saved.append((hb, proj, yc, conv, qr, kr, vb, ya, att, lse, yl, hl, xhat, rstd, pw_l))
        h, hb = hn, hnb

    dh = None
    g = {}
    later = None
    early, last = ("w_out", "conv_pw_w"), ("w_in",)
    own = {}
    for l in reversed(range(DEPTH)):
        hb_l, proj, yc, conv, qr, kr, vb, ya, att, lse, yl, hl, xhat, rstd, pw_l = saved[l]
        tail = dist is not None and l == 0
        top = l == DEPTH - 1
        (part, dz, dzb, g["ln_post_g", l], g["ln_post_b", l], d_ya, d_yl, d_conv, dproj, dpw, g["conv_pw_b", l],
         g["conv_ln_g", l], g["conv_ln_b", l]), recv = _post_ln_dcat_bwd(
            h if top else dh, target if top else None, xhat, rstd, ln_post_g, w_out[l],
            conv, proj, conv_ln_g, conv_ln_b, pw_l, conv_pw_b, l,
            job=_swap_job(later["grads"]) if later else None)
        if top:
            loss_part = part
        if later:
            later["parts"], later["owns"] = _chip_partials(big_names, later["grads"], recv, dist, later["l"])
        g["w_out", l] = _dwout_bwd(yc, ya, yl, dzb, l)
        g["conv_pw_w", l] = dpw.reshape(N_SHARD, 2, PW_SH // 2, CW)
        if tail:
            own["early"] = dict(l=0, grads=[g[name, 0] for name in early])
        job = None
        if tail:
            job = _join_jobs(_swap_job(own["early"]["grads"]), _scatter_job(later["parts"][1:]))
        (dproj, ddw, g["conv_dw_b", l]), got = _conv_bwd_taps(d_conv, proj, p["conv_dw_w"], dproj, l, job=job)
        if tail:
            n_early = len(early)
            own["early"]["parts"], own["early"]["owns"] = _chip_partials(
                early, own["early"]["grads"], got[:n_early], dist, 0)
            later["z"] = got[n_early:]
        g["conv_dw_w", l] = ddw[:CONV_K]
        (dqr, dk, dv, dk0, dv0, dproj, dsink), z = _attn_bwd(
            qr, kr, vb, proj, att, lse, d_ya, sinks, dproj, l,
            job=_scatter_job(later["parts"][:1]) if later else None)
        if later:
            later["z"] = z + later["z"]
        g["attn_sinks", l] = dsink[0, :N_HEADS]
        dproj = _rope_bwd(dqr, dk, dv, dk0, dv0, tabs, dproj, l)
        (dproj, dlw, g["lru_conv_b", l], dwa, g["lru_ba", l], dwx, g["lru_bx", l], g["lru_lambda", l]), z = _lru_bwd(
            proj, hl, d_yl, p["lru_conv_w"], lru_conv_b, wa_bd, lru_ba, wx_bd, lru_bx, lru_lambda, dproj, l,
            job=_scatter_job(own["early"]["parts"]) if tail else None)
        if tail:
            own["early"]["z"] = z
        g["lru_conv_w", l] = dlw[:LRU_K]
        g["lru_wa", l] = _diag_blocks(dwa)
        g["lru_wx", l] = _diag_blocks(dwx)
        job = None
        if l > 0:
            g["w_in", l] = _dwin_bwd(hb_l, dproj, l)
        else:
            c = dist[0] if dist else jnp.int32(0)
            job = None
            if dist:
                pack_a = _pack_rows([_layer_stack(g, name) for name in _SMALL_LAYERED])
                totals = _shard_totals(big_names, later, dist)
                job = _join_jobs(_share_job(totals), _spread_job(pack_a))
            (give,), got = _dwin_half(hb_l, dproj, 1 - c, l, "give", job=job)
            (keep,), recv = _dwin_half(hb_l, dproj, c, l, "keep", job=_send_job([give]) if dist else None)
            job = None
            if dist:
                _store_reduced(big_names, later["l"], got[:-1], g)
                later = None
                g["pack_layered", -1] = _sum_slots(pack_a, got[-1], dist[3], "layered")
                own["last"] = dict(l=0)
                own["last"]["parts"], own["last"]["owns"] = _chip_partials(
                    last, [keep.reshape(N_SHARD, 1, D // 2, WIN_SH)], recv, (jnp.int32(0),) + tuple(dist[1:]), 0)
                job = _scatter_job(own["last"]["parts"])
            else:
                g["w_in", l] = jnp.stack([keep, give], axis=1)
        (dh,), got = _dh_bwd(dproj, [w_in[l]], dz, l, job=job)
        if tail:
            own["last"]["z"] = got
        if dist and l > 0:
            later = dict(l=l, grads=[g[name, l] for name in big_names])
    grad_x, g["meta_tokens", -1], g["ln_in_g", -1], g["ln_in_b", -1] = _embed_bwd(
        dh, x, p["meta_tokens"], ln_in_g, ln_in_b)
    if dist:
        pack_b = _pack_rows([g[name, -1] for name in _SMALL_EMBED])
        state = dict(l=0, owns=own["last"]["owns"] + own["early"]["owns"], z=own["last"]["z"] + own["early"]["z"])
        totals = _shard_totals(last + early, state, dist)
        got = _run_job(_join_jobs(_share_job(totals), _spread_job(pack_b)), "share_and_spread")
        _store_reduced(last + early, 0, got[:-1], g)
        g["pack_embed", -1] = _sum_slots(pack_b, got[-1], dist[3], "embed")
    return loss_part, grad_x, g


_SMALL_EMBED = ("meta_tokens", "ln_in_g", "ln_in_b")
_SMALL_LAYERED = ("conv_dw_w", "conv_dw_b", "conv_ln_g", "conv_ln_b", "conv_pw_b", "attn_sinks", "lru_conv_w",
                  "lru_conv_b", "lru_wa", "lru_ba", "lru_wx", "lru_bx", "lru_lambda", "ln_post_g", "ln_post_b")


def _layer_stack(g, name):
    return jnp.stack([g[name, l] for l in range(DEPTH)], axis=0)


def _chip_partials(names, grads, recv, dist, l):
    outs = [_chip_partial(a, r, dist[0], dist[1], f"{name}{l}") for name, a, r in zip(names, grads, recv)]
    return [o[0] for o in outs], [o[1] for o in outs]


def _shard_totals(names, state, dist):
    l = state["l"]
    return [_shard_total(po, zz, dist[2], f"{name}{l}") for name, po, zz in zip(names, state["owns"], state["z"])]


def _store_reduced(names, l, full, g):
    for name, f in zip(names, full):
        g[name, l] = f.reshape(2 * f.shape[1], f.shape[2])


MESH = pl.DeviceIdType.MESH
HBM_SPEC = pl.BlockSpec(memory_space=pltpu.HBM)
N_DEV = 8


def _position():
    x, y, c = lax.axis_index("x"), lax.axis_index("y"), lax.axis_index("c")
    return x, y, c


def _other_chips(x, y):
    return [(1 - x, y), (x, 1 - y), (1 - x, 1 - y)]


def _cast_into_slot(a, l, j, tag):
    _, R, C = a.shape
    tb = _pick(R, (512, 128))

    def body(s_ref, a_ref, o_ref):
        o_ref[...] = a_ref[...].astype(BF16)

    grid_spec = pltpu.PrefetchScalarGridSpec(
        num_scalar_prefetch=1, grid=(R // tb,),
        in_specs=[pl.BlockSpec((None, tb, C), lambda t, sc: (l, t, 0))],
        out_specs=pl.BlockSpec((None, tb, C), lambda t, sc: (sc[0], t, 0)))
    return pl.pallas_call(
        body, name=f"cast_into_slot_{tag}{l}", grid_spec=grid_spec,
        out_shape=jax.ShapeDtypeStruct((N_SHARD, R, C), BF16),
        compiler_params=_cp("arbitrary"),
    )(jnp.reshape(j, (1,)).astype(jnp.int32), a)


class _Job:
    def __init__(self, inputs, aliased, extra_out, sems, start, mid, finish):
        self.inputs, self.extra_out, self.sems = list(inputs), list(extra_out), list(sems)
        self.n_aliased = len(self.inputs) if aliased is True else int(aliased)
        self.start, self.mid, self.finish = start, mid, finish

    def out_shapes(self):
        return [jax.ShapeDtypeStruct(a.shape, a.dtype) for a in self.inputs[:self.n_aliased]] + self.extra_out


def _side_call(body, job, *, name, grid, in_specs, out_specs, out_shape, scratch_shapes, semantics, args,
               aliases=None, prefetch=()):
    aliases = dict(aliases or {})
    n_pre = len(prefetch)

    def call(fn, ins, outs, shapes, scratch, sem, operands):
        if n_pre:
            spec = pltpu.PrefetchScalarGridSpec(num_scalar_prefetch=n_pre, grid=grid, in_specs=ins, out_specs=outs,
                                                scratch_shapes=scratch)
            return pl.pallas_call(fn, name=name, grid_spec=spec, out_shape=shapes,
                                  input_output_aliases={k + n_pre: v for k, v in aliases.items()},
                                  compiler_params=_cp(*sem))(*prefetch, *operands)
        return pl.pallas_call(fn, name=name, grid=grid, in_specs=ins, out_specs=outs, out_shape=shapes,
                              scratch_shapes=scratch, input_output_aliases=aliases,
                              compiler_params=_cp(*sem))(*operands)

    if job is None:
        return list(call(body, list(in_specs), list(out_specs), list(out_shape), list(scratch_shapes),
                         semantics, args)), []
    n_in, n_out, n_scr = len(in_specs), len(out_specs), len(scratch_shapes)
    j_in, j_out = len(job.inputs), len(job.out_shapes())
    steps = 1
    for gsize in grid:
        steps *= gsize

    def wrapped(*refs):
        pre, refs = refs[:n_pre], refs[n_pre:]
        host_in, job_in = refs[:n_in], refs[n_in:n_in + j_in]
        o0 = n_in + j_in
        host_out, job_out = refs[o0:o0 + n_out], refs[o0 + n_out:o0 + n_out + j_out]
        s0 = o0 + n_out + j_out
        host_scr, sems = refs[s0:s0 + n_scr], refs[s0 + n_scr:]
        step = pl.program_id(0)
        for d in range(1, len(grid)):
            step = step * grid[d] + pl.program_id(d)

        @pl.when(step == 0)
        def _():
            job.start(job_in, job_out, sems)

        @pl.when(step == max(steps - 2, 0))
        def _():
            job.mid(job_in, job_out, sems)

        body(*pre, *host_in, *host_out, *host_scr)

        @pl.when(step == steps - 1)
        def _():
            job.finish(job_in, job_out, sems)

    aliases.update({n_in + k: n_out + k for k in range(job.n_aliased)})
    outs = call(wrapped, list(in_specs) + [HBM_SPEC] * j_in, list(out_specs) + [HBM_SPEC] * j_out,
                list(out_shape) + job.out_shapes(), list(scratch_shapes) + job.sems,
                ["arbitrary"] * len(grid), [*args, *job.inputs])
    return list(outs[:n_out]), list(outs[n_out:])


def _run_job(job, name):
    return _side_call(lambda: None, job, name=name, grid=(1,), in_specs=[], out_specs=[], out_shape=[],
                      scratch_shapes=[], semantics=("arbitrary",), args=[])[1]


def _gather_job(slots, peers=(0, 1, 2)):
    n = len(slots)

    def copies(buf, sems):
        ici_send, ici_recv, d2d_send, d2d_recv = sems
        x, y, c = _position()
        chips = _other_chips(x, y)

        def half(k, slot, which):
            hr = buf[k].shape[1] // 2
            return buf[k].at[slot, pl.ds(pl.multiple_of(which * hr, hr), hr)]

        def over_ici(k, p, slot):
            px, py = chips[p]
            return pltpu.make_async_remote_copy(
                src_ref=half(k, slot, c), dst_ref=half(k, slot, c),
                send_sem=ici_send.at[k * 3 + p], recv_sem=ici_recv.at[k * 3 + p],
                device_id=(px, py, c), device_id_type=MESH)

        def over_d2d(k, p, which):
            px, py = chips[p]
            return pltpu.make_async_remote_copy(
                src_ref=half(k, 2 * px + py, which), dst_ref=half(k, 2 * px + py, which),
                send_sem=d2d_send.at[k * 3 + p], recv_sem=d2d_recv.at[k * 3 + p],
                device_id=(x, y, 1 - c), device_id_type=MESH)

        return over_ici, over_d2d, 2 * x + y, chips, c

    pairs = [(k, p) for k in range(n) for p in peers]

    def start(_, buf, sems):
        over_ici, _, mine, _, _ = copies(buf, sems)
        for k, p in pairs:
            over_ici(k, p, mine).start()

    def mid(_, buf, sems):
        over_ici, over_d2d, _, chips, c = copies(buf, sems)
        for k, p in pairs:
            px, py = chips[p]
            over_ici(k, p, 2 * px + py).wait_recv()
            over_d2d(k, p, c).start()

    def finish(_, buf, sems):
        over_ici, over_d2d, mine, _, c = copies(buf, sems)
        for k, p in pairs:
            over_d2d(k, p, 1 - c).wait_recv()
        for k, p in pairs:
            over_ici(k, p, mine).wait_send()
            over_d2d(k, p, c).wait_send()

    return _Job(slots, True, [], [pltpu.SemaphoreType.DMA((3 * n,))] * 4, start, mid, finish)


def _gather_shards(shards):
    n = len(shards)

    def body(*refs):
        src, dst = refs[:n], refs[n:2 * n]
        send_sems, recv_sems, local_sems = refs[2 * n:]
        x, y, c = _position()
        mine = 2 * x + y
        chips = _other_chips(x, y)

        def copy(k, p):
            return pltpu.make_async_remote_copy(
                src_ref=src[k], dst_ref=dst[k].at[mine],
                send_sem=send_sems.at[k * 3 + p], recv_sem=recv_sems.at[k * 3 + p],
                device_id=(*chips[p], c), device_id_type=MESH)

        def arrival(k, p):
            px, py = chips[p]
            return pltpu.make_async_remote_copy(
                src_ref=src[k], dst_ref=dst[k].at[2 * px + py],
                send_sem=send_sems.at[k * 3 + p], recv_sem=recv_sems.at[k * 3 + p],
                device_id=(px, py, c), device_id_type=MESH)

        local = [pltpu.make_async_copy(src[k], dst[k].at[mine], local_sems.at[k]) for k in range(n)]
        for cp in local:
            cp.start()
        for k in range(n):
            for p in range(3):
                copy(k, p).start()
        for k in range(n):
            for p in range(3):
                arrival(k, p).wait_recv()
        for k in range(n):
            for p in range(3):
                copy(k, p).wait_send()
        for cp in local:
            cp.wait()

    return pl.pallas_call(
        body, name="gather_shards",
        in_specs=[HBM_SPEC] * n, out_specs=[HBM_SPEC] * n,
        out_shape=[jax.ShapeDtypeStruct((N_SHARD,) + s.shape, s.dtype) for s in shards],
        scratch_shapes=[pltpu.SemaphoreType.DMA((3 * n,)), pltpu.SemaphoreType.DMA((3 * n,)),
                        pltpu.SemaphoreType.DMA((n,))],
    )(*shards)


def _swap_job(grads):
    n = len(grads)

    def copies(src, dst, sems):
        x, y, c = _position()
        return [pltpu.make_async_remote_copy(
            src_ref=src[k].at[:, 1 - c], dst_ref=dst[k],
            send_sem=sems[0].at[k], recv_sem=sems[1].at[k],
            device_id=(x, y, 1 - c), device_id_type=MESH) for k in range(n)]

    def start(src, dst, sems):
        for cp in copies(src, dst, sems):
            cp.start()

    def finish(src, dst, sems):
        for cp in copies(src, dst, sems):
            cp.wait()

    return _Job(grads, False, [jax.ShapeDtypeStruct((N_SHARD,) + g.shape[2:], F32) for g in grads],
                [pltpu.SemaphoreType.DMA((n,))] * 2, start, lambda *_: None, finish)


def _send_job(arrays):
    n = len(arrays)

    def copies(src, dst, sems):
        x, y, c = _position()
        return [pltpu.make_async_remote_copy(
            src_ref=src[k], dst_ref=dst[k], send_sem=sems[0].at[k], recv_sem=sems[1].at[k],
            device_id=(x, y, 1 - c), device_id_type=MESH) for k in range(n)]

    def start(src, dst, sems):
        for cp in copies(src, dst, sems):
            cp.start()

    def finish(src, dst, sems):
        for cp in copies(src, dst, sems):
            cp.wait()

    return _Job(arrays, False, [jax.ShapeDtypeStruct(a.shape, a.dtype) for a in arrays],
                [pltpu.SemaphoreType.DMA((n,))] * 2, start, lambda *_: None, finish)


def _chip_partial(a, y, c, j, tag):
    _, _, R, C = a.shape
    tr = _pick(R, (256, 64))

    def body(s_ref, a_ref, y_ref, pb_ref, po_ref):
        total = a_ref[...] + y_ref[...]
        pb_ref[...] = total.astype(BF16)

        @pl.when(pl.program_id(1) == s_ref[1])
        def _():
            po_ref[...] = total

    grid_spec = pltpu.PrefetchScalarGridSpec(
        num_scalar_prefetch=1, grid=(R // tr, N_SHARD),
        in_specs=[pl.BlockSpec((None, None, tr, C), lambda t, s, sc: (s, sc[0], t, 0)),
                  pl.BlockSpec((None, tr, C), lambda t, s, sc: (s, t, 0))],
        out_specs=[pl.BlockSpec((None, tr, C), lambda t, s, sc: (s, t, 0)),
                   pl.BlockSpec((tr, C), lambda t, s, sc: (t, 0))])
    return pl.pallas_call(
        body, name=f"chip_partial_{tag}", grid_spec=grid_spec,
        out_shape=[jax.ShapeDtypeStruct((N_SHARD, R, C), BF16), jax.ShapeDtypeStruct((R, C), F32)],
        compiler_params=_cp("arbitrary", "arbitrary"),
    )(jnp.stack([c, j]).astype(jnp.int32), a, y)


def _scatter_job(parts):
    n = len(parts)
    pairs = [(k, p) for k in range(n) for p in range(3)]

    def copy(src, dst, sems, k, p, outgoing):
        x, y, c = _position()
        mine = 2 * x + y
        px, py = _other_chips(x, y)[p]
        theirs = 2 * px + py
        return pltpu.make_async_remote_copy(
            src_ref=src[k].at[theirs if outgoing else mine], dst_ref=dst[k].at[mine if outgoing else theirs],
            send_sem=sems[0].at[k * 3 + p], recv_sem=sems[1].at[k * 3 + p],
            device_id=(px, py, c), device_id_type=MESH)

    def start(src, dst, sems):
        for k, p in pairs:
            copy(src, dst, sems, k, p, True).start()

    def finish(src, dst, sems):
        for k, p in pairs:
            copy(src, dst, sems, k, p, False).wait_recv()
        for k, p in pairs:
            copy(src, dst, sems, k, p, True).wait_send()

    return _Job(parts, False, [jax.ShapeDtypeStruct(pb.shape, BF16) for pb in parts],
                [pltpu.SemaphoreType.DMA((3 * n,))] * 2, start, lambda *_: None, finish)


def _shard_total(own, z, others_c, tag):
    R, C = own.shape
    tr = _pick(R, (256, 64))

    def body(s_ref, o_ref, z0_ref, z1_ref, z2_ref, h_ref):
        h_ref[...] = ((o_ref[...] + z0_ref[...].astype(F32)) + z1_ref[...].astype(F32)) + z2_ref[...].astype(F32)

    zspec = lambda q: pl.BlockSpec((None, tr, C), lambda t, sc: (sc[q], t, 0))
    grid_spec = pltpu.PrefetchScalarGridSpec(
        num_scalar_prefetch=1, grid=(R // tr,),
        in_specs=[pl.BlockSpec((tr, C), lambda t, sc: (t, 0)), zspec(0), zspec(1), zspec(2)],
        out_specs=pl.BlockSpec((None, tr, C), lambda t, sc: (sc[3], t, 0)))
    return pl.pallas_call(
        body, name=f"shard_total_{tag}", grid_spec=grid_spec,
        out_shape=jax.ShapeDtypeStruct((2, R, C), F32),
        compiler_params=_cp("arbitrary"),
    )(others_c, own, z, z, z)


def _share_job(totals):
    n = len(totals)

    def copy(buf, sems, k, which):
        x, y, c = _position()
        return pltpu.make_async_remote_copy(
            src_ref=buf[k].at[which], dst_ref=buf[k].at[which],
            send_sem=sems[0].at[k], recv_sem=sems[1].at[k],
            device_id=(x, y, 1 - c), device_id_type=MESH)

    def start(_, buf, sems):
        c = lax.axis_index("c")
        for k in range(n):
            copy(buf, sems, k, c).start()

    def finish(_, buf, sems):
        c = lax.axis_index("c")
        for k in range(n):
            copy(buf, sems, k, 1 - c).wait_recv()
        for k in range(n):
            copy(buf, sems, k, c).wait_send()

    return _Job(totals, True, [], [pltpu.SemaphoreType.DMA((n,))] * 2, start, lambda *_: None, finish)


def _spread_job(pack):
    def copy(src, dst, sems, m, outgoing):
        x, y, c = _position()
        peer = (x ^ (m >> 2), y ^ ((m >> 1) & 1), c ^ (m & 1))
        slot = 4 * x + 2 * y + c if outgoing else 4 * peer[0] + 2 * peer[1] + peer[2]
        return pltpu.make_async_remote_copy(
            src_ref=src[0], dst_ref=dst[0].at[slot], send_sem=sems[0].at[m - 1], recv_sem=sems[1].at[m - 1],
            device_id=peer, device_id_type=MESH)

    def start(src, dst, sems):
        for m in range(1, N_DEV):
            copy(src, dst, sems, m, True).start()

    def finish(src, dst, sems):
        for m in range(1, N_DEV):
            copy(src, dst, sems, m, False).wait_recv()
        for m in range(1, N_DEV):
            copy(src, dst, sems, m, True).wait_send()

    return _Job([pack], False, [jax.ShapeDtypeStruct((N_DEV,) + pack.shape, F32)],
                [pltpu.SemaphoreType.DMA((N_DEV - 1,))] * 2, start, lambda *_: None, finish)


def _join_jobs(a, b):
    for job in (a, b):
        assert job.n_aliased in (0, len(job.inputs)) and not (job.n_aliased and job.extra_out)
    assert a.n_aliased or not b.n_aliased
    n_in, n_out, n_sem = len(a.inputs), len(a.out_shapes()), len(a.sems)

    def phase(name):
        def run(ins, outs, sems):
            getattr(a, name)(ins[:n_in], outs[:n_out], sems[:n_sem])
            getattr(b, name)(ins[n_in:], outs[n_out:], sems[n_sem:])
        return run

    return _Job(a.inputs + b.inputs, a.n_aliased + b.n_aliased, a.extra_out + b.extra_out, a.sems + b.sems,
                phase("start"), phase("mid"), phase("finish"))


def _sum_slots(pack, slots, me, tag):
    def body(me_ref, p_ref, s_ref, o_ref):
        acc = None
        for d in range(N_DEV):
            term = jnp.where(me_ref[0] == d, p_ref[...], s_ref[d])
            acc = term if acc is None else acc + term
        o_ref[...] = acc

    vm = pl.BlockSpec(memory_space=pltpu.VMEM)
    return pl.pallas_call(
        body, name=f"sum_slots_{tag}",
        in_specs=[pl.BlockSpec(memory_space=pltpu.SMEM), vm, vm], out_specs=vm,
        out_shape=jax.ShapeDtypeStruct(pack.shape, F32),
        compiler_params=pltpu.CompilerParams(vmem_limit_bytes=V7X_VMEM_LIMIT),
    )(jnp.reshape(me, (1,)).astype(jnp.int32), pack, slots)


def _pack_rows(arrays):
    total = sum(a.size for a in arrays)
    rows = -(-total // 128)
    rows = -(-rows // PACK_ROWS_ALIGN) * PACK_ROWS_ALIGN
    flat = [a.reshape(-1) for a in arrays] + [jnp.zeros((rows * 128 - total,), F32)]
    return jnp.concatenate(flat).reshape(rows, 128)


def _adamw_math(w, g, m, v):
    m = ADAM_B1 * m + (1.0 - ADAM_B1) * g
    v = ADAM_B2 * v + (1.0 - ADAM_B2) * (g * g)
    m_hat = m / (1.0 - ADAM_B1 ** ADAM_STEP)
    v_hat = v / (1.0 - ADAM_B2 ** ADAM_STEP)
    delta = -ADAM_LR * (m_hat / (jnp.sqrt(v_hat) + ADAM_EPS) + ADAM_WD * w)
    return delta, m, v


def _adamw_big(w, g0, g1, m, v, tag):
    _, R, C = w.shape
    tr = _pick(R, (256, 128))

    def body(w_ref, g0_ref, g1_ref, m_ref, v_ref, go_ref, d_ref, mo_ref, vo_ref):
        g = jnp.where(pl.program_id(0) == 0, g0_ref[...], g1_ref[...])
        delta, mn, vn = _adamw_math(w_ref[...], g, m_ref[...], v_ref[...])
        go_ref[...] = g
        d_ref[...] = delta
        mo_ref[...] = mn
        vo_ref[...] = vn

    s3 = pl.BlockSpec((None, tr, C), lambda l, t: (l, t, 0))
    g_spec = lambda layer: pl.BlockSpec((tr, C), lambda l, t: (jnp.where(l == layer, t, 0), 0))
    shp = jax.ShapeDtypeStruct(w.shape, F32)
    return pl.pallas_call(
        body, name=f"adamw_{tag}", grid=(2, R // tr),
        in_specs=[s3, g_spec(0), g_spec(1), s3, s3], out_specs=[s3, s3, s3, s3],
        out_shape=[shp, shp, shp, shp],
        compiler_params=_cp("parallel", "parallel"),
    )(w, g0, g1, m, v)


def _adamw_small(ws, gs, ms, vs):
    n = len(ws)

    def body(*refs):
        w_r, g_r, m_r, v_r = refs[:n], refs[n:2 * n], refs[2 * n:3 * n], refs[3 * n:4 * n]
        d_o, m_o, v_o = refs[4 * n:5 * n], refs[5 * n:6 * n], refs[6 * n:7 * n]
        for k in range(n):
            delta, mn, vn = _adamw_math(w_r[k][...], g_r[k][...], m_r[k][...], v_r[k][...])
            d_o[k][...] = delta
            m_o[k][...] = mn
            v_o[k][...] = vn

    vm = pl.BlockSpec(memory_space=pltpu.VMEM)
    shapes = [jax.ShapeDtypeStruct(w.shape, F32) for w in ws]
    outs = pl.pallas_call(
        body, name="adamw_small",
        in_specs=[vm] * (4 * n), out_specs=[vm] * (3 * n),
        out_shape=shapes * 3,
    )(*ws, *gs, *ms, *vs)
    return outs[:n], outs[n:2 * n], outs[2 * n:]


_WEIGHTS = ["meta_tokens", "ln_in_g", "ln_in_b", "w_in", "conv_dw_w", "conv_dw_b", "conv_ln_g", "conv_ln_b",
            "conv_pw_w", "conv_pw_b", "attn_sinks", "lru_conv_w", "lru_conv_b", "lru_wa", "lru_ba", "lru_wx",
            "lru_bx", "lru_lambda", "w_out", "ln_post_g", "ln_post_b"]
_BIG = ("w_in", "w_out", "conv_pw_w")
_SMALL_SHARDED = {"meta_tokens": 1, "conv_dw_w": 2, "lru_conv_w": 2}
PACK_ROWS_ALIGN = 8


def _as2d(a):
    return a.reshape(1, -1) if a.ndim == 1 else a.reshape(-1, a.shape[-1])


def kernel(x, meta_tokens, ln_in_g, ln_in_b, w_in, conv_dw_w, conv_dw_b, conv_ln_g, conv_ln_b, conv_pw_w, conv_pw_b, attn_sinks, lru_conv_w, lru_conv_b, lru_wa, lru_ba, lru_wx, lru_bx, lru_lambda, w_out, ln_post_g, ln_post_b, loss_target, m_meta_tokens, m_ln_in_g, m_ln_in_b, m_w_in, m_conv_dw_w, m_conv_dw_b, m_conv_ln_g, m_conv_ln_b, m_conv_pw_w, m_conv_pw_b, m_attn_sinks, m_lru_conv_w, m_lru_conv_b, m_lru_wa, m_lru_ba, m_lru_wx, m_lru_bx, m_lru_lambda, m_w_out, m_ln_post_g, m_ln_post_b, v_meta_tokens, v_ln_in_g, v_ln_in_b, v_w_in, v_conv_dw_w, v_conv_dw_b, v_conv_ln_g, v_conv_ln_b, v_conv_pw_w, v_conv_pw_b, v_attn_sinks, v_lru_conv_w, v_lru_conv_b, v_lru_wa, v_lru_ba, v_lru_wx, v_lru_bx, v_lru_lambda, v_w_out, v_ln_post_g, v_ln_post_b):
    w = dict(meta_tokens=meta_tokens, ln_in_g=ln_in_g, ln_in_b=ln_in_b, w_in=w_in, conv_dw_w=conv_dw_w,
             conv_dw_b=conv_dw_b, conv_ln_g=conv_ln_g, conv_ln_b=conv_ln_b, conv_pw_w=conv_pw_w,
             conv_pw_b=conv_pw_b, attn_sinks=attn_sinks, lru_conv_w=lru_conv_w, lru_conv_b=lru_conv_b,
             lru_wa=lru_wa, lru_ba=lru_ba, lru_wx=lru_wx, lru_bx=lru_bx, lru_lambda=lru_lambda, w_out=w_out,
             ln_post_g=ln_post_g, ln_post_b=ln_post_b)
    mom_m = dict(zip(_WEIGHTS, (m_meta_tokens, m_ln_in_g, m_ln_in_b, m_w_in, m_conv_dw_w, m_conv_dw_b, m_conv_ln_g,
                                m_conv_ln_b, m_conv_pw_w, m_conv_pw_b, m_attn_sinks, m_lru_conv_w, m_lru_conv_b,
                                m_lru_wa, m_lru_ba, m_lru_wx, m_lru_bx, m_lru_lambda, m_w_out, m_ln_post_g,
                                m_ln_post_b)))
    mom_v = dict(zip(_WEIGHTS, (v_meta_tokens, v_ln_in_g, v_ln_in_b, v_w_in, v_conv_dw_w, v_conv_dw_b, v_conv_ln_g,
                                v_conv_ln_b, v_conv_pw_w, v_conv_pw_b, v_attn_sinks, v_lru_conv_w, v_lru_conv_b,
                                v_lru_wa, v_lru_ba, v_lru_wx, v_lru_bx, v_lru_lambda, v_w_out, v_ln_post_g,
                                v_ln_post_b)))
    xi, yi, ci = _position()
    j = 2 * xi + yi

    g_meta, g_dw, g_lc = _gather_shards([meta_tokens, conv_dw_w, lru_conv_w])
    p = dict(w)
    p["w_in"] = [_cast_into_slot(w_in, l, j, "w_in") for l in range(DEPTH)]
    p["w_out"] = [_cast_into_slot(w_out, l, j, "w_out") for l in range(DEPTH)]
    p["conv_pw_w"] = [_cast_into_slot(conv_pw_w, l, j, "conv_pw_w") for l in range(DEPTH)]
    p["meta_tokens"] = g_meta.transpose(1, 0, 2).reshape(N_META, D)
    p["conv_dw_w"] = g_dw.transpose(1, 2, 0, 3).reshape(DEPTH, CONV_K, CW)
    p["lru_conv_w"] = g_lc.transpose(1, 2, 0, 3).reshape(DEPTH, LRU_K, LW)

    others = jnp.stack([jnp.where(j <= 0, 1, 0), jnp.where(j <= 1, 2, 1), jnp.where(j <= 2, 3, 2), ci]).astype(jnp.int32)
    me = 4 * xi + 2 * yi + ci
    order = jnp.stack([j, 2 * (1 - xi) + yi, 2 * xi + (1 - yi), 2 * (1 - xi) + (1 - yi)]).astype(jnp.int32)
    loss_part, grad_x, g = _device_step(x[0], loss_target[0], p, dist=(ci, j, others, me, order))
    loss = lax.psum(jnp.sum(loss_part), ("x", "y", "c"))
    big = {(name, l): g[name, l] for name in _BIG for l in range(DEPTH)}

    small_names = [n for n in _WEIGHTS if n not in _BIG]
    small_g = {}
    for names, red in ((_SMALL_LAYERED, g["pack_layered", -1]), (_SMALL_EMBED, g["pack_embed", -1])):
        red = red.reshape(-1)
        off = 0
        for n in names:
            fshape = list(w[n].shape)
            if n in _SMALL_SHARDED:
                fshape[_SMALL_SHARDED[n]] *= N_SHARD
            sz = 1
            for dim in fshape:
                sz *= dim
            full = red[off:off + sz].reshape(fshape)
            off += sz
            if n in _SMALL_SHARDED:
                ax = _SMALL_SHARDED[n]
                full = lax.dynamic_slice_in_dim(full, j * w[n].shape[ax], w[n].shape[ax], axis=ax)
            small_g[n] = full

    out_g, out_d, out_m, out_v = {}, {}, {}, {}
    for name in _BIG:
        shp = w[name].shape
        to3 = lambda a: a.reshape(DEPTH, -1, shp[-1])
        go, do, mo, vo = _adamw_big(to3(w[name]), big[name, 0], big[name, 1], to3(mom_m[name]), to3(mom_v[name]), name)
        out_g[name], out_d[name], out_m[name], out_v[name] = (a.reshape(shp) for a in (go, do, mo, vo))
    ds, ms, vs = _adamw_small([_as2d(w[n]) for n in small_names], [_as2d(small_g[n]) for n in small_names],
                              [_as2d(mom_m[n]) for n in small_names], [_as2d(mom_v[n]) for n in small_names])
    for n, d_, m_, v_ in zip(small_names, ds, ms, vs):
        out_g[n] = small_g[n]
        out_d[n], out_m[n], out_v[n] = d_.reshape(w[n].shape), m_.reshape(w[n].shape), v_.reshape(w[n].shape)

    return (loss, grad_x[None], *[out_g[n] for n in _WEIGHTS], *[out_d[n] for n in _WEIGHTS],
            *[out_m[n] for n in _WEIGHTS], *[out_v[n] for n in _WEIGHTS])
```

```python
import functools

import jax
import jax.numpy as jnp
from jax import lax
from jax.experimental import pallas as pl
from jax.experimental.pallas import tpu as pltpu

F32 = jnp.float32
BF16 = jnp.bfloat16

D = 2048
N_META = 16
CW = 512
CONV_K = 31
AW = 1024
KVW = 256
N_HEADS = 16
LW = 512
LRU_K = 4
LRU_C = 8.0
IN_TOTAL = 5120
ROT_HALF = 8
ROPE_THETA = 500000.0
LN_EPS = 1e-5
DEPTH = 2
ALPHA = (2.0 * DEPTH) ** 0.25
NEG_INF = -1e30
ADAM_LR, ADAM_B1, ADAM_B2, ADAM_EPS, ADAM_WD, ADAM_STEP = 0.001, 0.9, 0.999, 1e-08, 0.01, 10

BLK = 128
PAD = BLK - N_META
N_SHARD = 4
WIN_SH = IN_TOTAL // N_SHARD
WOUT_SH = D // N_SHARD
PW_SH = CW // N_SHARD
HALO = 32
LHALO = 8
V7X_VMEM_LIMIT = 60 * 1024 * 1024


def _cp(*sem):
    return pltpu.CompilerParams(dimension_semantics=sem if sem else None, vmem_limit_bytes=V7X_VMEM_LIMIT)


def _pick(total, prefs):
    for p in prefs:
        if total % p == 0:
            return p
    raise ValueError(f"no tile for {total}")


def _dot(a, b):
    return jnp.dot(a, b, preferred_element_type=F32)


def _dot_nt(a, b):
    return lax.dot_general(a, b, (((1,), (1,)), ((), ())), preferred_element_type=F32)


def _dot_tn(a, b):
    return lax.dot_general(a, b, (((0,), (0,)), ((), ())), preferred_element_type=F32)


def _sigmoid(x):
    return 1.0 / (1.0 + jnp.exp(-x))


def _silu_and_grad(x):
    s = _sigmoid(x)
    return x * s, s * (1.0 + x * (1.0 - s))


def _ln_rows(x, g, b):
    mu = jnp.mean(x, axis=-1, keepdims=True)
    xc = x - mu
    var = jnp.mean(xc * xc, axis=-1, keepdims=True)
    rstd = lax.rsqrt(var + LN_EPS)
    xhat = xc * rstd
    return xhat * g + b, xhat, rstd


def _ln_bwd_rows(dy, xhat, rstd, g):
    dxh = dy * g
    m1 = jnp.mean(dxh, axis=-1, keepdims=True)
    m2 = jnp.mean(dxh * xhat, axis=-1, keepdims=True)
    return rstd * (dxh - m1 - xhat * m2)


def _row_ids(n, base):
    return base + lax.broadcasted_iota(jnp.int32, (n, 1), 0)


def _colsum(x):
    return jnp.sum(x, axis=0, keepdims=True)


def _embed_fwd(x, meta, g, b, job=None):
    S = x.shape[0]
    nb = S // BLK + 1

    def body(x_ref, meta_ref, g_ref, b_ref, h_ref, hb_ref):
        n = pl.program_id(0)

        @pl.when(n == 0)
        def _():
            y, _, _ = _ln_rows(meta_ref[...], g_ref[...], b_ref[...])
            h_ref[...] = jnp.zeros_like(h_ref)
            h_ref[PAD:BLK, :] = y

        @pl.when(n > 0)
        def _():
            y, _, _ = _ln_rows(x_ref[...], g_ref[...], b_ref[...])
            h_ref[...] = y

        hb_ref[...] = h_ref[...].astype(BF16)

    return _side_call(
        body, job, name="embed_fwd", grid=(nb,),
        in_specs=[pl.BlockSpec((BLK, D), lambda n: (jnp.maximum(n - 1, 0), 0)),
                  pl.BlockSpec((N_META, D), lambda n: (0, 0)),
                  pl.BlockSpec((1, D), lambda n: (0, 0)),
                  pl.BlockSpec((1, D), lambda n: (0, 0))],
        out_specs=[pl.BlockSpec((BLK, D), lambda n: (n, 0)),
                   pl.BlockSpec((BLK, D), lambda n: (n, 0))],
        out_shape=[jax.ShapeDtypeStruct((nb * BLK, D), F32), jax.ShapeDtypeStruct((nb * BLK, D), BF16)],
        scratch_shapes=[], semantics=("arbitrary",), args=[x, meta, g, b])


def _embed_bwd(dh, x, meta, g, b):
    S = x.shape[0]
    nb = S // BLK + 1

    def body(dh_ref, x_ref, meta_ref, g_ref, b_ref, gx_ref, gm_ref, dg_ref, db_ref):
        n = pl.program_id(0)

        @pl.when(n == 0)
        def _():
            _, xhat, rstd = _ln_rows(meta_ref[...], g_ref[...], b_ref[...])
            dy = dh_ref[PAD:BLK, :]
            gm_ref[...] = _ln_bwd_rows(dy, xhat, rstd, g_ref[...])
            dg_ref[...] = _colsum(dy * xhat)
            db_ref[...] = _colsum(dy)

        @pl.when(n > 0)
        def _():
            _, xhat, rstd = _ln_rows(x_ref[...], g_ref[...], b_ref[...])
            dy = dh_ref[...]
            gx_ref[...] = _ln_bwd_rows(dy, xhat, rstd, g_ref[...])
            dg_ref[...] += _colsum(dy * xhat)
            db_ref[...] += _colsum(dy)

    prev = lambda n: (jnp.maximum(n - 1, 0), 0)
    const = lambda n: (0, 0)
    return pl.pallas_call(
        body, name="embed_bwd", grid=(nb,),
        in_specs=[pl.BlockSpec((BLK, D), lambda n: (n, 0)),
                  pl.BlockSpec((BLK, D), prev),
                  pl.BlockSpec((N_META, D), const),
                  pl.BlockSpec((1, D), const),
                  pl.BlockSpec((1, D), const)],
        out_specs=[pl.BlockSpec((BLK, D), prev),
                   pl.BlockSpec((N_META, D), const),
                   pl.BlockSpec((1, D), const),
                   pl.BlockSpec((1, D), const)],
        out_shape=[jax.ShapeDtypeStruct((S, D), F32), jax.ShapeDtypeStruct((N_META, D), F32),
                   jax.ShapeDtypeStruct((1, D), F32), jax.ShapeDtypeStruct((1, D), F32)],
        compiler_params=_cp("arbitrary"),
    )(dh, x, meta, g, b)


def _proj_fwd(hb, w_in, order, first, count, prev, l, job=None):
    T = hb.shape[0]
    tm = _pick(T, (1056, 384, 128))

    def body(o_sc, a_ref, w_ref, *rest):
        rest[-1][...] = _dot(a_ref[...], w_ref[...])

    return _side_call(
        body, job, name=f"proj_fwd{l}_{first}", grid=(T // tm, count),
        in_specs=[pl.BlockSpec((tm, D), lambda i, j, o: (i, 0)),
                  pl.BlockSpec((None, D, WIN_SH), lambda i, j, o: (o[first + j], 0, 0))]
        + ([] if prev is None else [pl.BlockSpec(memory_space=pl.ANY)]),
        out_specs=[pl.BlockSpec((tm, WIN_SH), lambda i, j, o: (i, o[first + j]))],
        out_shape=[jax.ShapeDtypeStruct((T, IN_TOTAL), F32)],
        scratch_shapes=[], semantics=("parallel", "arbitrary"),
        args=[hb, w_in] + ([] if prev is None else [prev]),
        aliases=None if prev is None else {2: 0}, prefetch=[order])


def _out_fwd(yc, ya, yl, w_out, h, g, b, l, job=None):
    T = h.shape[0]
    tm = _pick(T, (384, 128))

    def body(yc_ref, ya_ref, yl_ref, w_ref, h_ref, g_ref, b_ref, hn_ref, hnb_ref, xh_ref, rs_ref):
        acc = _dot(yc_ref[...], w_ref[0])
        acc += _dot(ya_ref[:, 0:WOUT_SH], w_ref[1])
        acc += _dot(ya_ref[:, WOUT_SH:2 * WOUT_SH], w_ref[2])
        acc += _dot(yl_ref[...], w_ref[3])
        z = ALPHA * h_ref[...] + acc
        y, xhat, rstd = _ln_rows(z, g_ref[...], b_ref[...])
        hn_ref[...] = y
        hnb_ref[...] = y.astype(BF16)
        xh_ref[...] = xhat
        rs_ref[...] = rstd

    row = lambda i: (i, 0)
    return _side_call(
        body, job, name=f"out_fwd{l}", grid=(T // tm,),
        in_specs=[pl.BlockSpec((tm, CW), row), pl.BlockSpec((tm, AW), row), pl.BlockSpec((tm, LW), row),
                  pl.BlockSpec((N_SHARD, WOUT_SH, D), lambda i: (0, 0, 0)),
                  pl.BlockSpec((tm, D), row),
                  pl.BlockSpec((None, 1, D), lambda i: (l, 0, 0)),
                  pl.BlockSpec((None, 1, D), lambda i: (l, 0, 0))],
        out_specs=[pl.BlockSpec((tm, D), row), pl.BlockSpec((tm, D), row), pl.BlockSpec((tm, D), row),
                   pl.BlockSpec((tm, 1), row)],
        out_shape=[jax.ShapeDtypeStruct((T, D), F32), jax.ShapeDtypeStruct((T, D), BF16),
                   jax.ShapeDtypeStruct((T, D), F32), jax.ShapeDtypeStruct((T, 1), F32)],
        scratch_shapes=[], semantics=("parallel",), args=[yc, ya, yl, w_out, h, g, b])


def _post_ln_dcat_bwd(src, target, xhat, rstd, g, w_out, conv, proj, cln_g, cln_b, pw_w, pw_b, l, job=None):
    T = src.shape[0]
    tm = _pick(T, (384, 128))
    per = tm // BLK if target is not None else 0
    last_blk = target.shape[0] // BLK - 1 if target is not None else 0

    def body(s_ref, *refs):
        t_refs = refs[:per]
        (xh_ref, rs_ref, g_ref, w_ref, conv_ref, ct_ref, cg_ref, cb_ref, pw_ref, pb_ref,
         part_ref, dz_ref, dzb_ref, dg_ref, db_ref, da_ref, dl_ref,
         dconv_ref, dct_ref, dpw_ref, dpb_ref, dcg_ref, dcb_ref) = refs[per:]
        i = pl.program_id(0)

        @pl.when(i == 0)
        def _():
            for ref in (part_ref, dg_ref, db_ref, dpw_ref, dpb_ref, dcg_ref, dcb_ref):
                ref[...] = jnp.zeros_like(ref)

        if per:
            tgt = jnp.concatenate([r[...] for r in t_refs], axis=0) if per > 1 else t_refs[0][...]
            real = _row_ids(tm, i * tm) >= BLK
            err = jnp.where(real, s_ref[...] - tgt, 0.0)
            part_ref[...] += _colsum(err * err) * (0.5 / D)
            dy = err * (1.0 / D)
        else:
            dy = s_ref[...]
        xhat = xh_ref[...]
        dz = _ln_bwd_rows(dy, xhat, rs_ref[...], g_ref[...])
        dzb = dz.astype(BF16)
        dz_ref[...] = dz
        dzb_ref[...] = dzb
        dg_ref[...] += _colsum(dy * xhat)
        db_ref[...] += _colsum(dy)
        da_ref[:, 0:WOUT_SH] = _dot_nt(dzb, w_ref[1])
        da_ref[:, WOUT_SH:2 * WOUT_SH] = _dot_nt(dzb, w_ref[2])
        dl_ref[...] = _dot_nt(dzb, w_ref[3])

        d_yc = _dot_nt(dzb, w_ref[0])
        u, chat, crstd = _ln_rows(conv_ref[...], cg_ref[...], cb_ref[...])
        s, ds_du = _silu_and_grad(u)
        sb = s.astype(BF16)
        cpw = _dot(sb, pw_ref[...]) + pb_ref[...]
        gate, dgate = _silu_and_grad(ct_ref[...])
        d_cpw = d_yc * gate
        dct_ref[...] = (d_yc * cpw * dgate).astype(BF16)
        d_cpw_b = d_cpw.astype(BF16)
        dpb_ref[...] += _colsum(d_cpw)
        dpw_ref[...] += _dot_tn(sb, d_cpw_b)
        du = _dot_nt(d_cpw_b, pw_ref[...]) * ds_du
        dconv_ref[...] = _ln_bwd_rows(du, chat, crstd, cg_ref[...])
        dcg_ref[...] += _colsum(du * chat)
        dcb_ref[...] += _colsum(du)

    row = lambda i: (i, 0)
    const = lambda i: (0, 0)
    vec = pl.BlockSpec((None, 1, CW), lambda i: (l, 0, 0))
    t_specs = [pl.BlockSpec((BLK, D), functools.partial(lambda i, q: (jnp.clip(i * per - 1 + q, 0, last_blk), 0), q=q))
               for q in range(per)]
    return _side_call(
        body, job, name=f"post_ln_dcat_bwd{l}", grid=(T // tm,),
        in_specs=[pl.BlockSpec((tm, D), row)] + t_specs + [
            pl.BlockSpec((tm, D), row), pl.BlockSpec((tm, 1), row), pl.BlockSpec((None, 1, D), lambda i: (l, 0, 0)),
            pl.BlockSpec((N_SHARD, WOUT_SH, D), lambda i: (0, 0, 0)),
            pl.BlockSpec((tm, CW), row), pl.BlockSpec((tm, CW), lambda i: (i, 2)), vec, vec,
            pl.BlockSpec((CW, CW), const), vec],
        out_specs=[pl.BlockSpec((1, D), const), pl.BlockSpec((tm, D), row), pl.BlockSpec((tm, D), row),
                   pl.BlockSpec((1, D), const), pl.BlockSpec((1, D), const),
                   pl.BlockSpec((tm, AW), row), pl.BlockSpec((tm, LW), row),
                   pl.BlockSpec((tm, CW), row), pl.BlockSpec((tm, CW), lambda i: (i, 2)),
                   pl.BlockSpec((CW, CW), const), pl.BlockSpec((1, CW), const),
                   pl.BlockSpec((1, CW), const), pl.BlockSpec((1, CW), const)],
        out_shape=[jax.ShapeDtypeStruct((1, D), F32), jax.ShapeDtypeStruct((T, D), F32),
                   jax.ShapeDtypeStruct((T, D), BF16), jax.ShapeDtypeStruct((1, D), F32),
                   jax.ShapeDtypeStruct((1, D), F32),
                   jax.ShapeDtypeStruct((T, AW), F32), jax.ShapeDtypeStruct((T, LW), F32),
                   jax.ShapeDtypeStruct((T, CW), F32), jax.ShapeDtypeStruct((T, IN_TOTAL), BF16),
                   jax.ShapeDtypeStruct((CW, CW), F32), jax.ShapeDtypeStruct((1, CW), F32),
                   jax.ShapeDtypeStruct((1, CW), F32), jax.ShapeDtypeStruct((1, CW), F32)],
        scratch_shapes=[], semantics=("arbitrary",),
        args=[src] + [target] * per + [xhat, rstd, g, w_out, conv, proj, cln_g, cln_b, pw_w, pw_b])


def _dwout_bwd(yc, ya, yl, dzb, l):
    T = dzb.shape[0]
    tm = _pick(T, (384, 128))

    def body(yc_ref, ya_ref, yl_ref, dz_ref, o_ref):
        @pl.when(pl.program_id(0) == 0)
        def _():
            o_ref[...] = jnp.zeros_like(o_ref)

        cat = jnp.concatenate([yc_ref[...], ya_ref[...], yl_ref[...]], axis=1)
        o_ref[...] += _dot_tn(cat, dz_ref[...])

    row = lambda t: (t, 0)
    out = pl.pallas_call(
        body, name=f"dwout_bwd{l}", grid=(T // tm,),
        in_specs=[pl.BlockSpec((tm, CW), row), pl.BlockSpec((tm, AW), row), pl.BlockSpec((tm, LW), row),
                  pl.BlockSpec((tm, D), row)],
        out_specs=pl.BlockSpec((D, D), lambda t: (0, 0)),
        out_shape=jax.ShapeDtypeStruct((D, D), F32),
        compiler_params=_cp("arbitrary"),
    )(yc, ya, yl, dzb)
    return out.reshape(N_SHARD, 2, WOUT_SH // 2, D)


def _dh_bwd(dproj, w_in, dz, l, job=None):
    T = dproj.shape[0]
    tm = _pick(T, (1056, 384, 128))

    n_w = len(w_in)

    def body(dp_ref, *refs):
        w_refs, (dz_ref, o_ref, acc_ref) = refs[:n_w], refs[n_w:]
        j = pl.program_id(1)

        @pl.when(j == 0)
        def _():
            acc_ref[...] = ALPHA * dz_ref[...]

        dp = dp_ref[...]
        off = 0
        for w_ref in w_refs:
            rows = w_ref.shape[0]
            acc_ref[:, off:off + rows] += _dot_nt(dp, w_ref[...])
            off += rows

        @pl.when(j == N_SHARD - 1)
        def _():
            o_ref[...] = acc_ref[...]

    return _side_call(
        body, job, name=f"dh_bwd{l}", grid=(T // tm, N_SHARD),
        in_specs=[pl.BlockSpec((tm, WIN_SH), lambda i, j: (i, j))]
        + [pl.BlockSpec((None, w.shape[1], WIN_SH), lambda i, j: (j, 0, 0)) for w in w_in]
        + [pl.BlockSpec((tm, D), lambda i, j: (i, 0))],
        out_specs=[pl.BlockSpec((tm, D), lambda i, j: (i, 0))],
        out_shape=[jax.ShapeDtypeStruct((T, D), F32)],
        scratch_shapes=[pltpu.VMEM((tm, D), F32)],
        semantics=("parallel", "arbitrary"), args=[dproj, *w_in, dz])


def _dwin_bwd(hb, dproj, l):
    T = hb.shape[0]
    tm = _pick(T, (1056, 384, 128))

    def body(h_ref, dp_ref, o_ref):
        @pl.when(pl.program_id(1) == 0)
        def _():
            o_ref[...] = jnp.zeros_like(o_ref)

        o_ref[...] += _dot_tn(h_ref[...], dp_ref[...])

    out = pl.pallas_call(
        body, name=f"dwin_bwd{l}", grid=(N_SHARD, T // tm),
        in_specs=[pl.BlockSpec((tm, D), lambda j, t: (t, 0)),
                  pl.BlockSpec((tm, WIN_SH), lambda j, t: (t, j))],
        out_specs=pl.BlockSpec((None, D, WIN_SH), lambda j, t: (j, 0, 0)),
        out_shape=jax.ShapeDtypeStruct((N_SHARD, D, WIN_SH), F32),
        compiler_params=_cp("parallel", "arbitrary"),
    )(hb, dproj)
    return out.reshape(N_SHARD, 2, D // 2, WIN_SH)


def _dwin_half(hb, dproj, which, l, tag, job=None):
    T = hb.shape[0]
    tm = _pick(T, (1056, 384, 128))
    hr = D // 2

    def body(w_ref, h_ref, dp_ref, o_ref):
        @pl.when(pl.program_id(1) == 0)
        def _():
            o_ref[...] = jnp.zeros_like(o_ref)

        o_ref[...] += _dot_tn(h_ref[...], dp_ref[...])

    return _side_call(
        body, job, name=f"dwin_{tag}{l}", grid=(N_SHARD, T // tm),
        in_specs=[pl.BlockSpec((tm, hr), lambda j, t, w: (t, w[0])),
                  pl.BlockSpec((tm, WIN_SH), lambda j, t, w: (t, j))],
        out_specs=[pl.BlockSpec((None, hr, WIN_SH), lambda j, t, w: (j, 0, 0))],
        out_shape=[jax.ShapeDtypeStruct((N_SHARD, hr, WIN_SH), F32)],
        scratch_shapes=[], semantics=("parallel", "arbitrary"), args=[hb, dproj],
        prefetch=[jnp.reshape(which, (1,)).astype(jnp.int32)])


def _glu_masked(v, g, base_row):
    rows = _row_ids(v.shape[0], base_row)
    return jnp.where(rows >= PAD, v * _sigmoid(g), 0.0)


def _conv_tile(T):
    return _pick(T, (384, 128))


SUBLANES = 8


def _for_each_shift(buf, rot, tm, offsets, fn):
    for r in range(SUBLANES):
        group = [o for o in offsets if o % SUBLANES == r]
        if not group:
            continue
        if r == 0:
            src = buf
        else:
            n = tm + max(group) - r
            rot[0:n, :] = buf[r:r + n, :]
            src = rot
        for o in group:
            fn(o, src[o - r:o - r + tm, :])


def _conv_fwd(proj, dw_w, dw_b, ln_g, ln_b, pw_w, pw_b, l, job=None):
    T = proj.shape[0]
    tm = _conv_tile(T)
    hb = tm // HALO

    def body(cv_ref, cg_ref, ct_ref, hv_ref, hg_ref, w_ref, b_ref, g_ref, be_ref, pw_ref, pb_ref,
             yc_ref, conv_ref, buf, rot):
        i = pl.program_id(0)
        buf[0:HALO, :] = _glu_masked(hv_ref[...], hg_ref[...], i * tm - HALO)
        buf[HALO:HALO + tm, :] = _glu_masked(cv_ref[...], cg_ref[...], i * tm)
        first = HALO - (CONV_K - 1)
        total = [jnp.zeros((tm, CW), F32) + b_ref[...]]

        def tap(o, tile):
            k = o - first
            total[0] = total[0] + w_ref[k:k + 1, :] * tile

        _for_each_shift(buf, rot, tm, [first + k for k in range(CONV_K)], tap)
        acc = total[0]
        conv_ref[...] = acc
        u, _, _ = _ln_rows(acc, g_ref[...], be_ref[...])
        s = u * _sigmoid(u)
        cpw = _dot(s.astype(BF16), pw_ref[...]) + pb_ref[...]
        gate, _ = _silu_and_grad(ct_ref[...])
        yc_ref[...] = (cpw * gate).astype(BF16)

    vec = pl.BlockSpec((None, 1, CW), lambda i: (l, 0, 0))
    return _side_call(
        body, job, name=f"conv_fwd{l}", grid=(T // tm,),
        in_specs=[pl.BlockSpec((tm, CW), lambda i: (i, 0)),
                  pl.BlockSpec((tm, CW), lambda i: (i, 1)),
                  pl.BlockSpec((tm, CW), lambda i: (i, 2)),
                  pl.BlockSpec((HALO, CW), lambda i: (jnp.maximum(i * hb - 1, 0), 0)),
                  pl.BlockSpec((HALO, CW), lambda i: (jnp.maximum(i * hb - 1, 0), 1)),
                  pl.BlockSpec((None, CONV_K, CW), lambda i: (l, 0, 0)),
                  vec, vec, vec,
                  pl.BlockSpec((CW, CW), lambda i: (0, 0)),
                  vec],
        out_specs=[pl.BlockSpec((tm, CW), lambda i: (i, 0)), pl.BlockSpec((tm, CW), lambda i: (i, 0))],
        out_shape=[jax.ShapeDtypeStruct((T, CW), BF16), jax.ShapeDtypeStruct((T, CW), F32)],
        scratch_shapes=[pltpu.VMEM((tm + HALO, CW), F32), pltpu.VMEM((tm + HALO, CW), F32)],
        semantics=("parallel",), args=[proj, proj, proj, proj, proj, dw_w, dw_b, ln_g, ln_b, pw_w, pw_b])


def _conv_bwd_taps(d_conv, proj, dw_w, dproj, l, job=None):
    T = d_conv.shape[0]
    tm = _conv_tile(T)
    hb = tm // HALO
    nt = T // tm
    last_halo = T // HALO - 1

    def body(dc_ref, dh_ref, cv_ref, cg_ref, hv_ref, hg_ref, w_ref, _, o_ref, dw_ref, dwb_ref, cbuf, dbuf, rot):
        i = pl.program_id(0)

        @pl.when(i == 0)
        def _():
            dw_ref[...] = jnp.zeros_like(dw_ref)
            dwb_ref[...] = jnp.zeros_like(dwb_ref)

        cbuf[0:HALO, :] = _glu_masked(hv_ref[...], hg_ref[...], i * tm - HALO)
        cbuf[HALO:HALO + tm, :] = _glu_masked(cv_ref[...], cg_ref[...], i * tm)
        dmain = dc_ref[...]
        dbuf[0:tm, :] = dmain
        dbuf[tm:tm + HALO, :] = jnp.where(i < nt - 1, dh_ref[...], 0.0)
        total = [jnp.zeros((tm, CW), F32)]

        def tap_back(o, tile):
            k = CONV_K - 1 - o
            total[0] = total[0] + w_ref[k:k + 1, :] * tile

        _for_each_shift(dbuf, rot, tm, list(range(CONV_K)), tap_back)
        acc = total[0]
        first = HALO - (CONV_K - 1)

        def tap_weight(o, tile):
            k = o - first
            dw_ref[k:k + 1, :] += _colsum(dmain * tile)

        _for_each_shift(cbuf, rot, tm, [first + k for k in range(CONV_K)], tap_weight)
        dwb_ref[...] += _colsum(dmain)
        d_c = jnp.where(_row_ids(tm, i * tm) >= PAD, acc, 0.0)
        sig = _sigmoid(cg_ref[...])
        o_ref[:, 0:CW] = (d_c * sig).astype(BF16)
        o_ref[:, CW:2 * CW] = (d_c * cv_ref[...] * sig * (1.0 - sig)).astype(BF16)

    const = lambda i: (0, 0)
    return _side_call(
        body, job, name=f"conv_bwd_taps{l}", grid=(nt,),
        in_specs=[pl.BlockSpec((tm, CW), lambda i: (i, 0)),
                  pl.BlockSpec((HALO, CW), lambda i: (jnp.minimum((i + 1) * hb, last_halo), 0)),
                  pl.BlockSpec((tm, CW), lambda i: (i, 0)),
                  pl.BlockSpec((tm, CW), lambda i: (i, 1)),
                  pl.BlockSpec((HALO, CW), lambda i: (jnp.maximum(i * hb - 1, 0), 0)),
                  pl.BlockSpec((HALO, CW), lambda i: (jnp.maximum(i * hb - 1, 0), 1)),
                  pl.BlockSpec((None, CONV_K, CW), lambda i: (l, 0, 0)),
                  pl.BlockSpec(memory_space=pl.ANY)],
        out_specs=[pl.BlockSpec((tm, 2 * CW), lambda i: (i, 0)),
                   pl.BlockSpec((HALO, CW), const), pl.BlockSpec((1, CW), const)],
        out_shape=[jax.ShapeDtypeStruct(dproj.shape, BF16), jax.ShapeDtypeStruct((HALO, CW), F32),
                   jax.ShapeDtypeStruct((1, CW), F32)],
        scratch_shapes=[pltpu.VMEM((tm + HALO, CW), F32), pltpu.VMEM((tm + HALO, CW), F32),
                        pltpu.VMEM((tm + HALO, CW), F32)],
        semantics=("arbitrary",), aliases={7: 0},
        args=[d_conv, d_conv, proj, proj, proj, proj, dw_w, dproj])


def _log1p_small(e):
    return jnp.where(e < 1e-3, e * (1.0 - e * (0.5 - e * (1.0 / 3.0))), jnp.log(1.0 + e))


def _softplus(z):
    return jnp.maximum(z, 0.0) + _log1p_small(jnp.exp(-jnp.abs(z)))


def _neg_expm1(x):
    series = -x * (1.0 + x * (1.0 / 2.0) * (1.0 + x * (1.0 / 3.0) * (1.0 + x * (1.0 / 4.0) * (
        1.0 + x * (1.0 / 5.0) * (1.0 + x * (1.0 / 6.0) * (1.0 + x * (1.0 / 7.0)))))))
    return jnp.where(x > -0.25, series, 1.0 - jnp.exp(x))


def _lru_gates(rxbuf, tm, base_row, lw_ref, lb_ref, wa_ref, ba_ref, wx_ref, bx_ref, lam_ref):
    rc = jnp.zeros((tm, LW), F32) + lb_ref[...]
    for k in range(LRU_K):
        o = LHALO - (LRU_K - 1) + k
        rc += lw_ref[k:k + 1, :] * rxbuf[o:o + tm, :]
    rcb = rc.astype(BF16)
    r = _sigmoid(_dot(rcb, wa_ref[...]) + ba_ref[...])
    ig = _sigmoid(_dot(rcb, wx_ref[...]) + bx_ref[...])
    sp = _softplus(-lam_ref[...])
    la = -LRU_C * r * sp
    a = jnp.exp(la)
    mult = jnp.sqrt(_neg_expm1(2.0 * la))
    valid = _row_ids(tm, base_row) >= PAD
    return rc, rcb, r, ig, sp, a, mult, valid


def _mask_rows(v, base_row):
    return jnp.where(_row_ids(v.shape[0], base_row) >= PAD, v, 0.0)


def _scan_rows(aa, bb, carry, out_ref, reverse):
    tm = aa.shape[0]
    sub = _row_ids(tm, 0) & (SUBLANES - 1)
    s = 1
    while s < SUBLANES:
        keep = (sub < SUBLANES - s) if reverse else (sub >= s)
        shift = tm - s if reverse else s
        a_s = jnp.where(keep, pltpu.roll(aa, shift, axis=0), 1.0)
        b_s = jnp.where(keep, pltpu.roll(bb, shift, axis=0), 0.0)
        bb = aa * b_s + bb
        aa = aa * a_s
        s *= 2
    groups = range(tm // SUBLANES)
    edge = 0 if reverse else SUBLANES - 1
    for j in (reversed(groups) if reverse else groups):
        rows = slice(SUBLANES * j, SUBLANES * j + SUBLANES)
        x = bb[rows] + aa[rows] * carry
        out_ref[rows, :] = x
        carry = x[edge:edge + 1]


def _lru_tile(T):
    return _pick(T, (384, 128))


def _lru_fwd(proj, lw, lb, wa, ba, wx, bx, lam, l, job=None):
    T = proj.shape[0]
    tm = _lru_tile(T)
    hb = tm // LHALO

    def body(rx_ref, rg_ref, hx_ref, lw_ref, lb_ref, wa_ref, ba_ref, wx_ref, bx_ref, lam_ref,
             yl_ref, hl_ref, rxbuf, carry):
        i = pl.program_id(0)

        @pl.when(i == 0)
        def _():
            carry[...] = jnp.zeros_like(carry)

        rxbuf[0:LHALO, :] = _mask_rows(hx_ref[...], i * tm - LHALO)
        rxbuf[LHALO:LHALO + tm, :] = _mask_rows(rx_ref[...], i * tm)
        rc, _, _, ig, _, a, mult, valid = _lru_gates(rxbuf, tm, i * tm, lw_ref, lb_ref, wa_ref, ba_ref,
                                                     wx_ref, bx_ref, lam_ref)
        bb = jnp.where(valid, mult * (ig * rc), 0.0)
        _scan_rows(a, bb, carry[0:1, :], hl_ref, reverse=False)
        carry[0:1, :] = hl_ref[tm - 1:tm, :]
        gate, _ = _silu_and_grad(rg_ref[...])
        yl_ref[...] = (hl_ref[...] * gate).astype(BF16)

    vec = pl.BlockSpec((None, 1, LW), lambda i: (l, 0, 0))
    mat = pl.BlockSpec((None, LW, LW), lambda i: (l, 0, 0))
    return _side_call(
        body, job, name=f"lru_fwd{l}", grid=(T // tm,),
        in_specs=[pl.BlockSpec((tm, LW), lambda i: (i, 8)),
                  pl.BlockSpec((tm, LW), lambda i: (i, 9)),
                  pl.BlockSpec((LHALO, LW), lambda i: (jnp.maximum(i * hb - 1, 0), 8)),
                  pl.BlockSpec((None, LRU_K, LW), lambda i: (l, 0, 0)),
                  vec, mat, vec, mat, vec, vec],
        out_specs=[pl.BlockSpec((tm, LW), lambda i: (i, 0)), pl.BlockSpec((tm, LW), lambda i: (i, 0))],
        out_shape=[jax.ShapeDtypeStruct((T, LW), BF16), jax.ShapeDtypeStruct((T, LW), F32)],
        scratch_shapes=[pltpu.VMEM((tm + LHALO, LW), F32), pltpu.VMEM((8, LW), F32)],
        semantics=("arbitrary",), args=[proj, proj, proj, lw, lb, wa, ba, wx, bx, lam])


def _lru_bwd(proj, hl, d_yl, lw, lb, wa, ba, wx, bx, lam, dproj, l, job=None):
    T = proj.shape[0]
    tm = _lru_tile(T)
    hb = tm // LHALO
    nt = T // tm

    def body(rx_ref, rg_ref, hx_ref, hl_ref, hh_ref, dy_ref, lw_ref, lb_ref, wa_ref, ba_ref, wx_ref, bx_ref,
             lam_ref, _, o_ref, dlw_ref, dlb_ref, dwa_ref, dba_ref, dwx_ref, dbx_ref, dlam_ref,
             rxbuf, dbuf, carry, head, gbuf):
        step = pl.program_id(0)
        i = nt - 1 - step

        @pl.when(step == 0)
        def _():
            carry[...] = jnp.zeros_like(carry)
            head[...] = jnp.zeros_like(head)
            for ref in (dlw_ref, dlb_ref, dwa_ref, dba_ref, dwx_ref, dbx_ref, dlam_ref):
                ref[...] = jnp.zeros_like(ref)

        rxbuf[0:LHALO, :] = _mask_rows(hx_ref[...], i * tm - LHALO)
        rxbuf[LHALO:LHALO + tm, :] = _mask_rows(rx_ref[...], i * tm)
        rc, rcb, r, ig, sp, a, mult, valid = _lru_gates(rxbuf, tm, i * tm, lw_ref, lb_ref, wa_ref, ba_ref,
                                                        wx_ref, bx_ref, lam_ref)
        rows = _row_ids(tm, 0)
        h = hl_ref[...]
        h_before = jnp.where(i > 0, hh_ref[LHALO - 1:LHALO, :], 0.0)
        hprev = jnp.where(rows == 0, h_before, pltpu.roll(h, 1, axis=0))
        rg = rg_ref[...]
        gate, dgate = _silu_and_grad(rg)
        dy = dy_ref[...]
        o_ref[:, LW:2 * LW] = (dy * h * dgate).astype(BF16)
        bb = dy * gate + jnp.where(rows == tm - 1, carry[0:1, :], 0.0)
        aa = jnp.where(rows == tm - 1, 0.0, pltpu.roll(a, tm - 1, axis=0))
        _scan_rows(aa, bb, jnp.zeros((1, LW), F32), gbuf, reverse=True)
        g = gbuf[...]
        dbuf[0:tm, :] = a * g
        carry[0:1, :] = dbuf[0:1, :]
        du = jnp.where(valid, g, 0.0)
        da = g * hprev
        dix = du * mult
        dmult = du * (ig * rc)
        dla = jnp.where(valid, da * a - dmult * (a * a) / mult, 0.0)
        dr = dla * (-LRU_C * sp)
        dlam_ref[...] += _colsum(dla * (LRU_C * r)) * _sigmoid(-lam_ref[...])
        dpa = dr * r * (1.0 - r)
        dpx = (dix * rc) * ig * (1.0 - ig)
        dpab = dpa.astype(BF16)
        dpxb = dpx.astype(BF16)
        dba_ref[...] += _colsum(dpa)
        dbx_ref[...] += _colsum(dpx)
        dwa_ref[...] += _dot_tn(rcb, dpab)
        dwx_ref[...] += _dot_tn(rcb, dpxb)
        drc = dix * ig + _dot_nt(dpab, wa_ref[...]) + _dot_nt(dpxb, wx_ref[...])
        dbuf[0:tm, :] = drc
        dbuf[tm:tm + LHALO, :] = head[...]
        acc = jnp.zeros((tm, LW), F32)
        for k in range(LRU_K):
            o = LRU_K - 1 - k
            acc += lw_ref[k:k + 1, :] * dbuf[o:o + tm, :]
            oc = LHALO - (LRU_K - 1) + k
            dlw_ref[k:k + 1, :] += _colsum(drc * rxbuf[oc:oc + tm, :])
        dlb_ref[...] += _colsum(drc)
        head[...] = dbuf[0:LHALO, :]
        o_ref[:, 0:LW] = jnp.where(valid, acc, 0.0).astype(BF16)

    rev = lambda s: nt - 1 - s
    vec = pl.BlockSpec((None, 1, LW), lambda s: (l, 0, 0))
    mat = pl.BlockSpec((None, LW, LW), lambda s: (l, 0, 0))
    const = lambda s: (0, 0)
    halo = lambda s: jnp.maximum(rev(s) * hb - 1, 0)
    return _side_call(
        body, job, name=f"lru_bwd{l}", grid=(nt,),
        in_specs=[pl.BlockSpec((tm, LW), lambda s: (rev(s), 8)),
                  pl.BlockSpec((tm, LW), lambda s: (rev(s), 9)),
                  pl.BlockSpec((LHALO, LW), lambda s: (halo(s), 8)),
                  pl.BlockSpec((tm, LW), lambda s: (rev(s), 0)),
                  pl.BlockSpec((LHALO, LW), lambda s: (halo(s), 0)),
                  pl.BlockSpec((tm, LW), lambda s: (rev(s), 0)),
                  pl.BlockSpec((None, LRU_K, LW), lambda s: (l, 0, 0)),
                  vec, mat, vec, mat, vec, vec, pl.BlockSpec(memory_space=pl.ANY)],
        out_specs=[pl.BlockSpec((tm, 2 * LW), lambda s: (rev(s), 4)),
                   pl.BlockSpec((8, LW), const), pl.BlockSpec((1, LW), const),
                   pl.BlockSpec((LW, LW), const), pl.BlockSpec((1, LW), const),
                   pl.BlockSpec((LW, LW), const), pl.BlockSpec((1, LW), const),
                   pl.BlockSpec((1, LW), const)],
        out_shape=[jax.ShapeDtypeStruct(dproj.shape, BF16),
                   jax.ShapeDtypeStruct((8, LW), F32), jax.ShapeDtypeStruct((1, LW), F32),
                   jax.ShapeDtypeStruct((LW, LW), F32), jax.ShapeDtypeStruct((1, LW), F32),
                   jax.ShapeDtypeStruct((LW, LW), F32), jax.ShapeDtypeStruct((1, LW), F32),
                   jax.ShapeDtypeStruct((1, LW), F32)],
        scratch_shapes=[pltpu.VMEM((tm + LHALO, LW), F32), pltpu.VMEM((tm + LHALO, LW), F32),
                        pltpu.VMEM((8, LW), F32), pltpu.VMEM((LHALO, LW), F32), pltpu.VMEM((tm, LW), F32)],
        semantics=("arbitrary",), aliases={13: 0},
        args=[proj, proj, proj, hl, hl, d_yl, lw, lb, wa, ba, wx, bx, lam, dproj])


def _rope_tables(T):
    pos = (lax.broadcasted_iota(jnp.int32, (T, 128), 0) - PAD).astype(F32)
    lane = lax.broadcasted_iota(jnp.int32, (T, 128), 1) % 64
    inv_freq = ROPE_THETA ** (-(lane % ROT_HALF).astype(F32) / ROT_HALF)
    ang = pos * inv_freq
    cos, sin = jnp.cos(ang), jnp.sin(ang)
    c = jnp.where(lane < 2 * ROT_HALF, cos, 1.0)
    s1 = jnp.where(lane < ROT_HALF, -sin, 0.0)
    s2 = jnp.where((lane >= ROT_HALF) & (lane < 2 * ROT_HALF), sin, 0.0)
    return c, s1, s2


def _rot_fwd(x, c, s1, s2):
    return x * c + pltpu.roll(x, 128 - ROT_HALF, axis=1) * s1 + pltpu.roll(x, ROT_HALF, axis=1) * s2


def _rot_bwd(dy, c, s1, s2):
    return dy * c + pltpu.roll(dy * s1, ROT_HALF, axis=1) + pltpu.roll(dy * s2, 128 - ROT_HALF, axis=1)


KV2 = 2 * KVW


def _rope_fwd(proj, tabs, l):
    T = proj.shape[0]

    def both_halves(x, o_ref, pg):
        lane = lax.broadcasted_iota(jnp.int32, (1, 128), 1)
        for off in range(2):
            half = jnp.where((lane < 64) if off == 0 else (lane >= 64), x, 0.0)
            g = 2 * pg + off
            o_ref[:, 128 * g:128 * g + 128] = (half + pltpu.roll(half, 64, axis=1)).astype(BF16)

    def body(ql_ref, qh_ref, k_ref, v_ref, c_ref, s1_ref, s2_ref, qr_ref, vb_ref, kr2_ref, vb2_ref):
        c, s1, s2 = c_ref[...], s1_ref[...], s2_ref[...]
        for gcol in range(AW // 128):
            src = ql_ref if gcol < 4 else qh_ref
            x = src[:, 128 * (gcol % 4):128 * (gcol % 4) + 128]
            qr_ref[:, 128 * gcol:128 * gcol + 128] = (_rot_fwd(x, c, s1, s2) * 0.125).astype(BF16)
        for pg in range(KVW // 128):
            cols = slice(128 * pg, 128 * pg + 128)
            both_halves(_rot_fwd(k_ref[:, cols], c, s1, s2), kr2_ref, pg)
            vb_ref[:, cols] = v_ref[:, cols].astype(BF16)
            both_halves(v_ref[:, cols], vb2_ref, pg)

    tr = _pick(T, (384, 128))
    tab = pl.BlockSpec((tr, 128), lambda n: (n, 0))
    return pl.pallas_call(
        body, name=f"rope_fwd{l}", grid=(T // tr,),
        in_specs=[pl.BlockSpec((tr, 512), lambda n: (n, 3)), pl.BlockSpec((tr, 512), lambda n: (n, 4)),
                  pl.BlockSpec((tr, KVW), lambda n: (n, 10)), pl.BlockSpec((tr, KVW), lambda n: (n, 11)),
                  tab, tab, tab],
        out_specs=[pl.BlockSpec((tr, AW), lambda n: (n, 0)), pl.BlockSpec((tr, KVW), lambda n: (n, 0)),
                   pl.BlockSpec((tr, KV2), lambda n: (n, 0)), pl.BlockSpec((tr, KV2), lambda n: (n, 0))],
        out_shape=[jax.ShapeDtypeStruct((T, AW), BF16), jax.ShapeDtypeStruct((T, KVW), BF16),
                   jax.ShapeDtypeStruct((T, KV2), BF16), jax.ShapeDtypeStruct((T, KV2), BF16)],
        compiler_params=_cp("parallel"),
    )(proj, proj, proj, proj, *tabs)


GROUP = 4


def _attn_mask(n, reps):
    qi = lax.broadcasted_iota(jnp.int32, (reps * BLK, BLK), 0) & (BLK - 1)
    kj = lax.broadcasted_iota(jnp.int32, (reps * BLK, BLK), 1)
    m0 = (kj >= PAD) & (n >= 1)
    mp = (kj > qi) & (n >= 2)
    mc = (kj <= qi) & ((n >= 1) | (kj >= PAD))
    return jnp.concatenate([m0, mp, mc], axis=1)


def _kv_both(x0_ref, xp_ref, xc_ref, g):
    if x0_ref.shape[1] == KV2:
        cols = slice(128 * g, 128 * g + 128)
        return jnp.concatenate([x0_ref[:, cols], xp_ref[:, cols], xc_ref[:, cols]], axis=0)
    pg, off = g // 2, g % 2
    cols = slice(128 * pg, 128 * pg + 128)
    x = jnp.concatenate([x0_ref[:, cols], xp_ref[:, cols], xc_ref[:, cols]], axis=0).astype(F32)
    lane = lax.broadcasted_iota(jnp.int32, (1, 128), 1)
    half = jnp.where((lane < 64) if off == 0 else (lane >= 64), x, 0.0)
    return (half + pltpu.roll(half, 64, axis=1)).astype(BF16)


def _stack_heads(a, b):
    lo = lax.broadcasted_iota(jnp.int32, (1, 128), 1) < 64
    a, b = a.astype(F32), b.astype(F32)
    return jnp.concatenate([jnp.where(lo, a, 0.0), jnp.where(lo, 0.0, a),
                            jnp.where(lo, b, 0.0), jnp.where(lo, 0.0, b)], axis=0).astype(BF16)


def _unstack_heads(x):
    lo = lax.broadcasted_iota(jnp.int32, (1, 128), 1) < 64
    return (jnp.where(lo, x[0:BLK], x[BLK:2 * BLK]), jnp.where(lo, x[2 * BLK:3 * BLK], x[3 * BLK:4 * BLK]))


def _per_head_column(values):
    return jnp.concatenate([jnp.zeros((BLK, 1), F32) + v for v in values], axis=0)


def _attn_fwd(qr, kr, vb, proj, sinks, l, job=None):
    T = qr.shape[0]

    def body(sink_ref, q_ref, k0_ref, kp_ref, kc_ref, v0_ref, vp_ref, vc_ref, ag_ref, ya_ref, att_ref, lse_ref):
        n = pl.program_id(0)
        mask = _attn_mask(n, 1)
        lane = lax.broadcasted_iota(jnp.int32, (1, 128), 1)
        lse_acc = jnp.zeros((BLK, 128), F32)
        for g in range(4):
            kx = _kv_both(k0_ref, kp_ref, kc_ref, g)
            vx = _kv_both(v0_ref, vp_ref, vc_ref, g)
            pair_cols = [slice(128 * (2 * g + pp), 128 * (2 * g + pp) + 128) for pp in range(2)]
            s4 = _dot_nt(_stack_heads(q_ref[:, pair_cols[0]], q_ref[:, pair_cols[1]]), kx)
            probs = []
            for r in range(GROUP):
                h = GROUP * g + r
                sink = sink_ref[l, h]
                s = jnp.where(mask, s4[BLK * r:BLK * r + BLK], NEG_INF)
                m = jnp.maximum(jnp.max(s, axis=1, keepdims=True), sink)
                p = jnp.exp(s - m)
                denom = jnp.sum(p, axis=1, keepdims=True) + jnp.exp(sink - m)
                probs.append((p * (1.0 / denom)).astype(BF16))
                lse_acc = jnp.where(lane == h, m + jnp.log(denom), lse_acc)
            outs = _unstack_heads(_dot(jnp.concatenate(probs, axis=0), vx))
            for cols, out in zip(pair_cols, outs):
                att_ref[:, cols] = out
                gate, _ = _silu_and_grad(ag_ref[:, cols])
                ya_ref[:, cols] = (out * gate).astype(BF16)
        lse_ref[...] = lse_acc

    prev = lambda n: (jnp.maximum(n - 1, 0), 0)
    cur = lambda n: (n, 0)
    zero = lambda n: (0, 0)
    kv = lambda f: pl.BlockSpec((BLK, KV2), f)
    return _side_call(
        body, job, name=f"attn_fwd{l}", grid=(T // BLK,),
        in_specs=[pl.BlockSpec(memory_space=pltpu.SMEM),
                  pl.BlockSpec((BLK, AW), cur), kv(zero), kv(prev), kv(cur), kv(zero), kv(prev), kv(cur),
                  pl.BlockSpec((BLK, AW), lambda n: (n, 3))],
        out_specs=[pl.BlockSpec((BLK, AW), cur), pl.BlockSpec((BLK, AW), cur), pl.BlockSpec((BLK, 128), cur)],
        out_shape=[jax.ShapeDtypeStruct((T, AW), BF16), jax.ShapeDtypeStruct((T, AW), F32),
                   jax.ShapeDtypeStruct((T, 128), F32)],
        scratch_shapes=[], semantics=("parallel",), args=[sinks, qr, kr, kr, kr, vb, vb, vb, proj])


def _attn_bwd(qr, kr, vb, proj, att, lse, d_ya, sinks, dproj, l, job=None):
    T = qr.shape[0]
    nb = T // BLK

    def body(sink_ref, q_ref, k0_ref, kp_ref, kc_ref, v0_ref, vp_ref, vc_ref, ag_ref, att_ref, lse_ref, dy_ref, _,
             dq_ref, dk_ref, dv_ref, dk0_ref, dv0_ref, dag_ref, dsink_ref, kcarry, vcarry):
        n = pl.program_id(0)

        @pl.when(n == 0)
        def _():
            dk0_ref[...] = jnp.zeros_like(dk0_ref)
            dv0_ref[...] = jnp.zeros_like(dv0_ref)
            dsink_ref[...] = jnp.zeros_like(dsink_ref)
            kcarry[...] = jnp.zeros_like(kcarry)
            vcarry[...] = jnp.zeros_like(vcarry)

        @pl.when(n == nb)
        def _():
            dk_ref[...] = kcarry[...]
            dv_ref[...] = vcarry[...]

        @pl.when(n < nb)
        def _():
            mask = _attn_mask(n, GROUP)
            lane = lax.broadcasted_iota(jnp.int32, (1, 128), 1)
            lse = lse_ref[...]
            dsink = jnp.zeros((1, 128), F32)
            dk_pg, dv_pg = [], []
            for pg in range(2):
                dk_acc = jnp.zeros((3 * BLK, 128), F32)
                dv_acc = jnp.zeros((3 * BLK, 128), F32)
                for off in range(2):
                    g = 2 * pg + off
                    kx = _kv_both(k0_ref, kp_ref, kc_ref, g)
                    vx = _kv_both(v0_ref, vp_ref, vc_ref, g)
                    pair_cols = [slice(128 * (2 * g + pp), 128 * (2 * g + pp) + 128) for pp in range(2)]
                    q4 = _stack_heads(q_ref[:, pair_cols[0]], q_ref[:, pair_cols[1]])
                    d_out = []
                    for cols in pair_cols:
                        gate, dgate = _silu_and_grad(ag_ref[:, cols])
                        dy = dy_ref[:, cols]
                        dag_ref[:, cols] = (dy * att_ref[:, cols] * dgate).astype(BF16)
                        d_out.append(dy * gate)
                    do4 = _stack_heads(d_out[0], d_out[1])
                    heads = [GROUP * g + r for r in range(GROUP)]
                    sink = _per_head_column([sink_ref[l, h] for h in heads])
                    lse4 = _per_head_column(
                        [jnp.sum(jnp.where(lane == h, lse, 0.0), axis=1, keepdims=True) for h in heads])
                    p = jnp.where(mask, jnp.exp(_dot_nt(q4, kx) - lse4), 0.0)
                    dp = _dot_nt(do4, vx)
                    delta = jnp.sum(p * dp, axis=1, keepdims=True)
                    ds = (p * (dp - delta)).astype(BF16)
                    sink_term = jnp.exp(sink - lse4) * delta
                    for r, h in enumerate(heads):
                        dsink += jnp.where(lane == h, -jnp.sum(sink_term[BLK * r:BLK * r + BLK]), 0.0)
                    for cols, dq in zip(pair_cols, _unstack_heads(_dot(ds, kx))):
                        dq_ref[:, cols] = dq
                    dkg = _dot_tn(ds, q4)
                    dvg = _dot_tn(p.astype(BF16), do4)
                    own = (lane < 64) if off == 0 else (lane >= 64)
                    dk_acc += jnp.where(own, dkg + pltpu.roll(dkg, 64, axis=1), 0.0)
                    dv_acc += jnp.where(own, dvg + pltpu.roll(dvg, 64, axis=1), 0.0)
                dk_pg.append(dk_acc)
                dv_pg.append(dv_acc)
            dsink_ref[...] += dsink
            for pg in range(2):
                cols = slice(128 * pg, 128 * pg + 128)
                dk0_ref[:, cols] += dk_pg[pg][0:BLK]
                dv0_ref[:, cols] += dv_pg[pg][0:BLK]
                dk_ref[:, cols] = kcarry[:, cols] + dk_pg[pg][BLK:2 * BLK]
                dv_ref[:, cols] = vcarry[:, cols] + dv_pg[pg][BLK:2 * BLK]
                kcarry[:, cols] = dk_pg[pg][2 * BLK:3 * BLK]
                vcarry[:, cols] = dv_pg[pg][2 * BLK:3 * BLK]

    last = nb - 1
    cur = lambda n: (jnp.minimum(n, last), 0)
    prev = lambda n: (jnp.clip(n - 1, 0, last), 0)
    zero = lambda n: (0, 0)
    kv = lambda f: pl.BlockSpec((BLK, KVW), f)
    kin = lambda a, f: pl.BlockSpec((BLK, a.shape[1]), f)
    wide = lambda f: pl.BlockSpec((BLK, AW), f)
    return _side_call(
        body, job, name=f"attn_bwd{l}", grid=(nb + 1,),
        in_specs=[pl.BlockSpec(memory_space=pltpu.SMEM),
                  wide(cur), kin(kr, zero), kin(kr, prev), kin(kr, cur), kin(vb, zero), kin(vb, prev), kin(vb, cur),
                  pl.BlockSpec((BLK, AW), lambda n: (jnp.minimum(n, last), 3)),
                  wide(cur), pl.BlockSpec((BLK, 128), cur), wide(cur), pl.BlockSpec(memory_space=pl.ANY)],
        out_specs=[wide(cur), kv(prev), kv(prev), kv(zero), kv(zero),
                   pl.BlockSpec((BLK, AW), lambda n: (jnp.minimum(n, last), 3)),
                   pl.BlockSpec((1, 128), zero)],
        out_shape=[jax.ShapeDtypeStruct((T, AW), F32), jax.ShapeDtypeStruct((T, KVW), F32),
                   jax.ShapeDtypeStruct((T, KVW), F32), jax.ShapeDtypeStruct((BLK, KVW), F32),
                   jax.ShapeDtypeStruct((BLK, KVW), F32), jax.ShapeDtypeStruct(dproj.shape, BF16),
                   jax.ShapeDtypeStruct((1, 128), F32)],
        scratch_shapes=[pltpu.VMEM((BLK, KVW), F32), pltpu.VMEM((BLK, KVW), F32)],
        semantics=("arbitrary",), aliases={12: 5},
        args=[sinks, qr, kr, kr, kr, vb, vb, vb, proj, att, lse, d_ya, dproj])


def _rope_bwd(dqr, dk, dv, dk0, dv0, tabs, dproj, l):
    T = dqr.shape[0]

    def body(dq_ref, dk_ref, dv_ref, dk0_ref, dv0_ref, c_ref, s1_ref, s2_ref, _, o_ref):
        n = pl.program_id(0)
        c, s1, s2 = c_ref[...], s1_ref[...], s2_ref[...]
        for gcol in range(AW // 128):
            cols = slice(128 * gcol, 128 * gcol + 128)
            o_ref[:, cols] = (_rot_bwd(dq_ref[:, cols], c, s1, s2) * 0.125).astype(BF16)
        for gcol in range(KVW // 128):
            cols = slice(128 * gcol, 128 * gcol + 128)
            kcols = slice(AW + 128 * gcol, AW + 128 * gcol + 128)
            vcols = slice(AW + KVW + 128 * gcol, AW + KVW + 128 * gcol + 128)
            o_ref[:, kcols] = _rot_bwd(dk_ref[:, cols], c, s1, s2).astype(BF16)
            o_ref[:, vcols] = dv_ref[:, cols].astype(BF16)

            @pl.when(n == 0)
            def _():
                dkk = dk_ref[0:BLK, cols] + dk0_ref[:, cols]
                o_ref[0:BLK, kcols] = _rot_bwd(dkk, c[0:BLK], s1[0:BLK], s2[0:BLK]).astype(BF16)
                o_ref[0:BLK, vcols] = (dv_ref[0:BLK, cols] + dv0_ref[:, cols]).astype(BF16)

    tr = _pick(T, (384, 128))
    cur = lambda n: (n, 0)
    zero = lambda n: (0, 0)
    tab = pl.BlockSpec((tr, 128), cur)
    return pl.pallas_call(
        body, name=f"rope_bwd{l}", grid=(T // tr,),
        in_specs=[pl.BlockSpec((tr, AW), cur), pl.BlockSpec((tr, KVW), cur), pl.BlockSpec((tr, KVW), cur),
                  pl.BlockSpec((BLK, KVW), zero), pl.BlockSpec((BLK, KVW), zero), tab, tab, tab,
                  pl.BlockSpec(memory_space=pl.ANY)],
        out_specs=pl.BlockSpec((tr, AW + 2 * KVW), lambda n: (n, 1)),
        out_shape=jax.ShapeDtypeStruct(dproj.shape, BF16),
        input_output_aliases={8: 0},
        compiler_params=_cp("parallel"),
    )(dqr, dk, dv, dk0, dv0, *tabs, dproj)


def _block_diag(w):
    nl, nh, hd, _ = w.shape
    eye = jnp.eye(nh, dtype=w.dtype)
    return jnp.einsum("lhij,hg->lhigj", w, eye).reshape(nl, nh * hd, nh * hd)


def _diag_blocks(m):
    nh, hd = 8, 64
    return jnp.einsum("hihj->hij", m.reshape(nh, hd, nh, hd))


def _device_step(x, target, p, dist=None):
    vec = lambda a: a.reshape(DEPTH, 1, a.shape[-1])
    ln_in_g, ln_in_b = p["ln_in_g"].reshape(1, D), p["ln_in_b"].reshape(1, D)
    conv_dw_b, conv_ln_g, conv_ln_b, conv_pw_b = map(vec, (p["conv_dw_b"], p["conv_ln_g"], p["conv_ln_b"], p["conv_pw_b"]))
    lru_conv_b, lru_ba, lru_bx, lru_lambda = map(vec, (p["lru_conv_b"], p["lru_ba"], p["lru_bx"], p["lru_lambda"]))
    ln_post_g, ln_post_b = vec(p["ln_post_g"]), vec(p["ln_post_b"])
    wa_bd = _block_diag(p["lru_wa"]).astype(BF16)
    wx_bd = _block_diag(p["lru_wx"]).astype(BF16)
    w_in, w_out, pw_w = list(p["w_in"]), list(p["w_out"]), list(p["conv_pw_w"])
    sinks = p["attn_sinks"]
    big_names = ("w_in", "w_out", "conv_pw_w")

    order = dist[4] if dist else jnp.arange(N_SHARD, dtype=jnp.int32)
    (h, hb), got = _embed_fwd(x, p["meta_tokens"], ln_in_g, ln_in_b,
                              job=_gather_job([w_in[0]], peers=(0, 1)) if dist else None)
    if dist:
        w_in[0] = got[0]
    T = h.shape[0]
    tabs = _rope_tables(T)
    saved = []
    for l in range(DEPTH):
        if l == 0:
            job = _join_jobs(_gather_job([w_in[0]], peers=(2,)), _gather_job([pw_w[0]])) if dist else None
            (proj,), got = _proj_fwd(hb, w_in[0], order, 0, N_SHARD - 1, None, l, job=job)
            if dist:
                w_in[0], pw_w[0] = got
            (proj,), _ = _proj_fwd(hb, w_in[0], order, N_SHARD - 1, 1, proj, l)
        else:
            (proj,), _ = _proj_fwd(hb, w_in[l], order, 0, N_SHARD, None, l)
        pw_l = pw_w[l].reshape(CW, CW)
        (yc, conv), got = _conv_fwd(proj, p["conv_dw_w"], conv_dw_b, conv_ln_g, conv_ln_b, pw_l, conv_pw_b, l,
                                    job=_gather_job([w_out[0]]) if dist and l == 0 else None)
        if got:
            w_out[0] = got[0]
        qr, vb, kr2, vb2 = _rope_fwd(proj, tabs, l)
        (ya, att, lse), got = _attn_fwd(
            qr, kr2, vb2, proj, sinks, l, job=_gather_job([w_in[1]]) if dist and l == 0 else None)
        if got:
            w_in[1] = got[0]
        (yl, hl), _ = _lru_fwd(proj, p["lru_conv_w"], lru_conv_b, wa_bd, lru_ba, wx_bd, lru_bx, lru_lambda, l)
        (hn, hnb, xhat, rstd), got = _out_fwd(
            yc, ya, yl, w_out[l], h, ln_post_g, ln_post_b, l,
            job=_gather_job([w_out[1], pw_w[1]]) if dist and l == 0 else None)
        if got:
            w_out[1], pw_w[1] = got
        saved.append((hb, proj, yc, conv, qr, kr2, vb, ya, att, lse, yl, hl, xhat, rstd, pw_l))
        h, hb = hn, hnb

    dh = None
    g = {}
    later = None
    early, last = ("w_out", "conv_pw_w"), ("w_in",)
    own = {}
    for l in reversed(range(DEPTH)):
        hb_l, proj, yc, conv, qr, kr, vb, ya, att, lse, yl, hl, xhat, rstd, pw_l = saved[l]
        tail = dist is not None and l == 0
        top = l == DEPTH - 1
        (part, dz, dzb, g["ln_post_g", l], g["ln_post_b", l], d_ya, d_yl, d_conv, dproj, dpw, g["conv_pw_b", l],
         g["conv_ln_g", l], g["conv_ln_b", l]), recv = _post_ln_dcat_bwd(
            h if top else dh, target if top else None, xhat, rstd, ln_post_g, w_out[l],
            conv, proj, conv_ln_g, conv_ln_b, pw_l, conv_pw_b, l,
            job=_swap_job(later["grads"]) if later else None)
        if top:
            loss_part = part
        if later:
            later["parts"], later["owns"] = _chip_partials(big_names, later["grads"], recv, dist, later["l"])
        g["w_out", l] = _dwout_bwd(yc, ya, yl, dzb, l)
        g["conv_pw_w", l] = dpw.reshape(N_SHARD, 2, PW_SH // 2, CW)
        if tail:
            own["early"] = dict(l=0, grads=[g[name, 0] for name in early])
        job = None
        if tail:
            job = _join_jobs(_swap_job(own["early"]["grads"]), _scatter_job(later["parts"][1:]))
        (dproj, ddw, g["conv_dw_b", l]), got = _conv_bwd_taps(d_conv, proj, p["conv_dw_w"], dproj, l, job=job)
        if tail:
            n_early = len(early)
            own["early"]["parts"], own["early"]["owns"] = _chip_partials(
                early, own["early"]["grads"], got[:n_early], dist, 0)
            later["z"] = got[n_early:]
        g["conv_dw_w", l] = ddw[:CONV_K]
        (dqr, dk, dv, dk0, dv0, dproj, dsink), z = _attn_bwd(
            qr, kr, vb, proj, att, lse, d_ya, sinks, dproj, l,
            job=_scatter_job(later["parts"][:1]) if later else None)
        if later:
            later["z"] = z + later["z"]
        g["attn_sinks", l] = dsink[0, :N_HEADS]
        dproj = _rope_bwd(dqr, dk, dv, dk0, dv0, tabs, dproj, l)
        (dproj, dlw, g["lru_conv_b", l], dwa, g["lru_ba", l], dwx, g["lru_bx", l], g["lru_lambda", l]), z = _lru_bwd(
            proj, hl, d_yl, p["lru_conv_w"], lru_conv_b, wa_bd, lru_ba, wx_bd, lru_bx, lru_lambda, dproj, l,
            job=_scatter_job(own["early"]["parts"]) if tail else None)
        if tail:
            own["early"]["z"] = z
        g["lru_conv_w", l] = dlw[:LRU_K]
        g["lru_wa", l] = _diag_blocks(dwa)
        g["lru_wx", l] = _diag_blocks(dwx)
        job = None
        if l > 0:
            g["w_in", l] = _dwin_bwd(hb_l, dproj, l)
        else:
            c = dist[0] if dist else jnp.int32(0)
            job = None
            if dist:
                pack_a = _pack_rows([_layer_stack(g, name) for name in _SMALL_LAYERED])
                totals = _shard_totals(big_names, later, dist)
                job = _join_jobs(_share_job(totals), _spread_job(pack_a))
            (give,), got = _dwin_half(hb_l, dproj, 1 - c, l, "give", job=job)
            (keep,), recv = _dwin_half(hb_l, dproj, c, l, "keep", job=_send_job([give]) if dist else None)
            job = None
            if dist:
                _store_reduced(big_names, later["l"], got[:-1], g)
                later = None
                g["pack_layered", -1] = _sum_slots(pack_a, got[-1], dist[3], "layered")
                own["last"] = dict(l=0)
                own["last"]["parts"], own["last"]["owns"] = _chip_partials(
                    last, [keep.reshape(N_SHARD, 1, D // 2, WIN_SH)], recv, (jnp.int32(0),) + tuple(dist[1:]), 0)
                job = _scatter_job(own["last"]["parts"])
            else:
                g["w_in", l] = jnp.stack([keep, give], axis=1)
        (dh,), got = _dh_bwd(dproj, [w_in[l]], dz, l, job=job)
        if tail:
            own["last"]["z"] = got
        if dist and l > 0:
            later = dict(l=l, grads=[g[name, l] for name in big_names])
    grad_x, g["meta_tokens", -1], g["ln_in_g", -1], g["ln_in_b", -1] = _embed_bwd(
        dh, x, p["meta_tokens"], ln_in_g, ln_in_b)
    if dist:
        pack_b = _pack_rows([g[name, -1] for name in _SMALL_EMBED])
        state = dict(l=0, owns=own["last"]["owns"] + own["early"]["owns"], z=own["last"]["z"] + own["early"]["z"])
        totals = _shard_totals(last + early, state, dist)
        got = _run_job(_join_jobs(_share_job(totals), _spread_job(pack_b)), "share_and_spread")
        _store_reduced(last + early, 0, got[:-1], g)
        g["pack_embed", -1] = _sum_slots(pack_b, got[-1], dist[3], "embed")
    return loss_part, grad_x, g


_SMALL_EMBED = ("meta_tokens", "ln_in_g", "ln_in_b")
_SMALL_LAYERED = ("conv_dw_w", "conv_dw_b", "conv_ln_g", "conv_ln_b", "conv_pw_b", "attn_sinks", "lru_conv_w",
                  "lru_conv_b", "lru_wa", "lru_ba", "lru_wx", "lru_bx", "lru_lambda", "ln_post_g", "ln_post_b")


def _layer_stack(g, name):
    return jnp.stack([g[name, l] for l in range(DEPTH)], axis=0)


def _chip_partials(names, grads, recv, dist, l):
    outs = [_chip_partial(a, r, dist[0], dist[1], f"{name}{l}") for name, a, r in zip(names, grads, recv)]
    return [o[0] for o in outs], [o[1] for o in outs]


def _shard_totals(names, state, dist):
    l = state["l"]
    return [_shard_total(po, zz, dist[2], f"{name}{l}") for name, po, zz in zip(names, state["owns"], state["z"])]


def _store_reduced(names, l, full, g):
    for name, f in zip(names, full):
        g[name, l] = f.reshape(2 * f.shape[1], f.shape[2])


MESH = pl.DeviceIdType.MESH
HBM_SPEC = pl.BlockSpec(memory_space=pltpu.HBM)
N_DEV = 8


def _position():
    x, y, c = lax.axis_index("x"), lax.axis_index("y"), lax.axis_index("c")
    return x, y, c


def _other_chips(x, y):
    return [(1 - x, y), (x, 1 - y), (1 - x, 1 - y)]


def _cast_into_slot(a, l, j, tag):
    _, R, C = a.shape
    tb = _pick(R, (512, 128))

    def body(s_ref, a_ref, o_ref):
        o_ref[...] = a_ref[...].astype(BF16)

    grid_spec = pltpu.PrefetchScalarGridSpec(
        num_scalar_prefetch=1, grid=(R // tb,),
        in_specs=[pl.BlockSpec((None, tb, C), lambda t, sc: (l, t, 0))],
        out_specs=pl.BlockSpec((None, tb, C), lambda t, sc: (sc[0], t, 0)))
    return pl.pallas_call(
        body, name=f"cast_into_slot_{tag}{l}", grid_spec=grid_spec,
        out_shape=jax.ShapeDtypeStruct((N_SHARD, R, C), BF16),
        compiler_params=_cp("arbitrary"),
    )(jnp.reshape(j, (1,)).astype(jnp.int32), a)


class _Job:
    def __init__(self, inputs, aliased, extra_out, sems, start, mid, finish):
        self.inputs, self.extra_out, self.sems = list(inputs), list(extra_out), list(sems)
        self.n_aliased = len(self.inputs) if aliased is True else int(aliased)
        self.start, self.mid, self.finish = start, mid, finish

    def out_shapes(self):
        return [jax.ShapeDtypeStruct(a.shape, a.dtype) for a in self.inputs[:self.n_aliased]] + self.extra_out


def _side_call(body, job, *, name, grid, in_specs, out_specs, out_shape, scratch_shapes, semantics, args,
               aliases=None, prefetch=()):
    aliases = dict(aliases or {})
    n_pre = len(prefetch)

    def call(fn, ins, outs, shapes, scratch, sem, operands):
        if n_pre:
            spec = pltpu.PrefetchScalarGridSpec(num_scalar_prefetch=n_pre, grid=grid, in_specs=ins, out_specs=outs,
                                                scratch_shapes=scratch)
            return pl.pallas_call(fn, name=name, grid_spec=spec, out_shape=shapes,
                                  input_output_aliases={k + n_pre: v for k, v in aliases.items()},
                                  compiler_params=_cp(*sem))(*prefetch, *operands)
        return pl.pallas_call(fn, name=name, grid=grid, in_specs=ins, out_specs=outs, out_shape=shapes,
                              scratch_shapes=scratch, input_output_aliases=aliases,
                              compiler_params=_cp(*sem))(*operands)

    if job is None:
        return list(call(body, list(in_specs), list(out_specs), list(out_shape), list(scratch_shapes),
                         semantics, args)), []
    n_in, n_out, n_scr = len(in_specs), len(out_specs), len(scratch_shapes)
    j_in, j_out = len(job.inputs), len(job.out_shapes())
    steps = 1
    for gsize in grid:
        steps *= gsize

    def wrapped(*refs):
        pre, refs = refs[:n_pre], refs[n_pre:]
        host_in, job_in = refs[:n_in], refs[n_in:n_in + j_in]
        o0 = n_in + j_in
        host_out, job_out = refs[o0:o0 + n_out], refs[o0 + n_out:o0 + n_out + j_out]
        s0 = o0 + n_out + j_out
        host_scr, sems = refs[s0:s0 + n_scr], refs[s0 + n_scr:]
        step = pl.program_id(0)
        for d in range(1, len(grid)):
            step = step * grid[d] + pl.program_id(d)

        @pl.when(step == 0)
        def _():
            job.start(job_in, job_out, sems)

        @pl.when(step == max(steps - 2, 0))
        def _():
            job.mid(job_in, job_out, sems)

        body(*pre, *host_in, *host_out, *host_scr)

        @pl.when(step == steps - 1)
        def _():
            job.finish(job_in, job_out, sems)

    aliases.update({n_in + k: n_out + k for k in range(job.n_aliased)})
    outs = call(wrapped, list(in_specs) + [HBM_SPEC] * j_in, list(out_specs) + [HBM_SPEC] * j_out,
                list(out_shape) + job.out_shapes(), list(scratch_shapes) + job.sems,
                ["arbitrary"] * len(grid), [*args, *job.inputs])
    return list(outs[:n_out]), list(outs[n_out:])


def _run_job(job, name):
    return _side_call(lambda: None, job, name=name, grid=(1,), in_specs=[], out_specs=[], out_shape=[],
                      scratch_shapes=[], semantics=("arbitrary",), args=[])[1]


def _gather_job(slots, peers=(0, 1, 2)):
    n = len(slots)

    def copies(buf, sems):
        ici_send, ici_recv, d2d_send, d2d_recv = sems
        x, y, c = _position()
        chips = _other_chips(x, y)

        def half(k, slot, which):
            hr = buf[k].shape[1] // 2
            return buf[k].at[slot, pl.ds(pl.multiple_of(which * hr, hr), hr)]

        def over_ici(k, p, slot):
            px, py = chips[p]
            return pltpu.make_async_remote_copy(
                src_ref=half(k, slot, c), dst_ref=half(k, slot, c),
                send_sem=ici_send.at[k * 3 + p], recv_sem=ici_recv.at[k * 3 + p],
                device_id=(px, py, c), device_id_type=MESH)

        def over_d2d(k, p, which):
            px, py = chips[p]
            return pltpu.make_async_remote_copy(
                src_ref=half(k, 2 * px + py, which), dst_ref=half(k, 2 * px + py, which),
                send_sem=d2d_send.at[k * 3 + p], recv_sem=d2d_recv.at[k * 3 + p],
                device_id=(x, y, 1 - c), device_id_type=MESH)

        return over_ici, over_d2d, 2 * x + y, chips, c

    pairs = [(k, p) for k in range(n) for p in peers]

    def start(_, buf, sems):
        over_ici, _, mine, _, _ = copies(buf, sems)
        for k, p in pairs:
            over_ici(k, p, mine).start()

    def mid(_, buf, sems):
        over_ici, over_d2d, _, chips, c = copies(buf, sems)
        for k, p in pairs:
            px, py = chips[p]
            over_ici(k, p, 2 * px + py).wait_recv()
            over_d2d(k, p, c).start()

    def finish(_, buf, sems):
        over_ici, over_d2d, mine, _, c = copies(buf, sems)
        for k, p in pairs:
            over_d2d(k, p, 1 - c).wait_recv()
        for k, p in pairs:
            over_ici(k, p, mine).wait_send()
            over_d2d(k, p, c).wait_send()

    return _Job(slots, True, [], [pltpu.SemaphoreType.DMA((3 * n,))] * 4, start, mid, finish)


def _gather_shards(shards):
    n = len(shards)

    def body(*refs):
        src, dst = refs[:n], refs[n:2 * n]
        send_sems, recv_sems, local_sems = refs[2 * n:]
        x, y, c = _position()
        mine = 2 * x + y
        chips = _other_chips(x, y)

        def copy(k, p):
            return pltpu.make_async_remote_copy(
                src_ref=src[k], dst_ref=dst[k].at[mine],
                send_sem=send_sems.at[k * 3 + p], recv_sem=recv_sems.at[k * 3 + p],
                device_id=(*chips[p], c), device_id_type=MESH)

        def arrival(k, p):
            px, py = chips[p]
            return pltpu.make_async_remote_copy(
                src_ref=src[k], dst_ref=dst[k].at[2 * px + py],
                send_sem=send_sems.at[k * 3 + p], recv_sem=recv_sems.at[k * 3 + p],
                device_id=(px, py, c), device_id_type=MESH)

        local = [pltpu.make_async_copy(src[k], dst[k].at[mine], local_sems.at[k]) for k in range(n)]
        for cp in local:
            cp.start()
        for k in range(n):
            for p in range(3):
                copy(k, p).start()
        for k in range(n):
            for p in range(3):
                arrival(k, p).wait_recv()
        for k in range(n):
            for p in range(3):
                copy(k, p).wait_send()
        for cp in local:
            cp.wait()

    return pl.pallas_call(
        body, name="gather_shards",
        in_specs=[HBM_SPEC] * n, out_specs=[HBM_SPEC] * n,
        out_shape=[jax.ShapeDtypeStruct((N_SHARD,) + s.shape, s.dtype) for s in shards],
        scratch_shapes=[pltpu.SemaphoreType.DMA((3 * n,)), pltpu.SemaphoreType.DMA((3 * n,)),
                        pltpu.SemaphoreType.DMA((n,))],
    )(*shards)


def _swap_job(grads):
    n = len(grads)

    def copies(src, dst, sems):
        x, y, c = _position()
        return [pltpu.make_async_remote_copy(
            src_ref=src[k].at[:, 1 - c], dst_ref=dst[k],
            send_sem=sems[0].at[k], recv_sem=sems[1].at[k],
            device_id=(x, y, 1 - c), device_id_type=MESH) for k in range(n)]

    def start(src, dst, sems):
        for cp in copies(src, dst, sems):
            cp.start()

    def finish(src, dst, sems):
        for cp in copies(src, dst, sems):
            cp.wait()

    return _Job(grads, False, [jax.ShapeDtypeStruct((N_SHARD,) + g.shape[2:], F32) for g in grads],
                [pltpu.SemaphoreType.DMA((n,))] * 2, start, lambda *_: None, finish)


def _send_job(arrays):
    n = len(arrays)

    def copies(src, dst, sems):
        x, y, c = _position()
        return [pltpu.make_async_remote_copy(
            src_ref=src[k], dst_ref=dst[k], send_sem=sems[0].at[k], recv_sem=sems[1].at[k],
            device_id=(x, y, 1 - c), device_id_type=MESH) for k in range(n)]

    def start(src, dst, sems):
        for cp in copies(src, dst, sems):
            cp.start()

    def finish(src, dst, sems):
        for cp in copies(src, dst, sems):
            cp.wait()

    return _Job(arrays, False, [jax.ShapeDtypeStruct(a.shape, a.dtype) for a in arrays],
                [pltpu.SemaphoreType.DMA((n,))] * 2, start, lambda *_: None, finish)


def _chip_partial(a, y, c, j, tag):
    _, _, R, C = a.shape
    tr = _pick(R, (256, 64))

    def body(s_ref, a_ref, y_ref, pb_ref, po_ref):
        total = a_ref[...] + y_ref[...]
        pb_ref[...] = total.astype(BF16)

        @pl.when(pl.program_id(1) == s_ref[1])
        def _():
            po_ref[...] = total

    grid_spec = pltpu.PrefetchScalarGridSpec(
        num_scalar_prefetch=1, grid=(R // tr, N_SHARD),
        in_specs=[pl.BlockSpec((None, None, tr, C), lambda t, s, sc: (s, sc[0], t, 0)),
                  pl.BlockSpec((None, tr, C), lambda t, s, sc: (s, t, 0))],
        out_specs=[pl.BlockSpec((None, tr, C), lambda t, s, sc: (s, t, 0)),
                   pl.BlockSpec((tr, C), lambda t, s, sc: (t, 0))])
    return pl.pallas_call(
        body, name=f"chip_partial_{tag}", grid_spec=grid_spec,
        out_shape=[jax.ShapeDtypeStruct((N_SHARD, R, C), BF16), jax.ShapeDtypeStruct((R, C), F32)],
        compiler_params=_cp("arbitrary", "arbitrary"),
    )(jnp.stack([c, j]).astype(jnp.int32), a, y)


def _scatter_job(parts):
    n = len(parts)
    pairs = [(k, p) for k in range(n) for p in range(3)]

    def copy(src, dst, sems, k, p, outgoing):
        x, y, c = _position()
        mine = 2 * x + y
        px, py = _other_chips(x, y)[p]
        theirs = 2 * px + py
        return pltpu.make_async_remote_copy(
            src_ref=src[k].at[theirs if outgoing else mine], dst_ref=dst[k].at[mine if outgoing else theirs],
            send_sem=sems[0].at[k * 3 + p], recv_sem=sems[1].at[k * 3 + p],
            device_id=(px, py, c), device_id_type=MESH)

    def start(src, dst, sems):
        for k, p in pairs:
            copy(src, dst, sems, k, p, True).start()

    def finish(src, dst, sems):
        for k, p in pairs:
            copy(src, dst, sems, k, p, False).wait_recv()
        for k, p in pairs:
            copy(src, dst, sems, k, p, True).wait_send()

    return _Job(parts, False, [jax.ShapeDtypeStruct(pb.shape, BF16) for pb in parts],
                [pltpu.SemaphoreType.DMA((3 * n,))] * 2, start, lambda *_: None, finish)


def _shard_total(own, z, others_c, tag):
    R, C = own.shape
    tr = _pick(R, (256, 64))

    def body(s_ref, o_ref, z0_ref, z1_ref, z2_ref, h_ref):
        h_ref[...] = ((o_ref[...] + z0_ref[...].astype(F32)) + z1_ref[...].astype(F32)) + z2_ref[...].astype(F32)

    zspec = lambda q: pl.BlockSpec((None, tr, C), lambda t, sc: (sc[q], t, 0))
    grid_spec = pltpu.PrefetchScalarGridSpec(
        num_scalar_prefetch=1, grid=(R // tr,),
        in_specs=[pl.BlockSpec((tr, C), lambda t, sc: (t, 0)), zspec(0), zspec(1), zspec(2)],
        out_specs=pl.BlockSpec((None, tr, C), lambda t, sc: (sc[3], t, 0)))
    return pl.pallas_call(
        body, name=f"shard_total_{tag}", grid_spec=grid_spec,
        out_shape=jax.ShapeDtypeStruct((2, R, C), F32),
        compiler_params=_cp("arbitrary"),
    )(others_c, own, z, z, z)


def _share_job(totals):
    n = len(totals)

    def copy(buf, sems, k, which):
        x, y, c = _position()
        return pltpu.make_async_remote_copy(
            src_ref=buf[k].at[which], dst_ref=buf[k].at[which],
            send_sem=sems[0].at[k], recv_sem=sems[1].at[k],
            device_id=(x, y, 1 - c), device_id_type=MESH)

    def start(_, buf, sems):
        c = lax.axis_index("c")
        for k in range(n):
            copy(buf, sems, k, c).start()

    def finish(_, buf, sems):
        c = lax.axis_index("c")
        for k in range(n):
            copy(buf, sems, k, 1 - c).wait_recv()
        for k in range(n):
            copy(buf, sems, k, c).wait_send()

    return _Job(totals, True, [], [pltpu.SemaphoreType.DMA((n,))] * 2, start, lambda *_: None, finish)


def _spread_job(pack):
    def copy(src, dst, sems, m, outgoing):
        x, y, c = _position()
        peer = (x ^ (m >> 2), y ^ ((m >> 1) & 1), c ^ (m & 1))
        slot = 4 * x + 2 * y + c if outgoing else 4 * peer[0] + 2 * peer[1] + peer[2]
        return pltpu.make_async_remote_copy(
            src_ref=src[0], dst_ref=dst[0].at[slot], send_sem=sems[0].at[m - 1], recv_sem=sems[1].at[m - 1],
            device_id=peer, device_id_type=MESH)

    def start(src, dst, sems):
        for m in range(1, N_DEV):
            copy(src, dst, sems, m, True).start()

    def finish(src, dst, sems):
        for m in range(1, N_DEV):
            copy(src, dst, sems, m, False).wait_recv()
        for m in range(1, N_DEV):
            copy(src, dst, sems, m, True).wait_send()

    return _Job([pack], False, [jax.ShapeDtypeStruct((N_DEV,) + pack.shape, F32)],
                [pltpu.SemaphoreType.DMA((N_DEV - 1,))] * 2, start, lambda *_: None, finish)


def _join_jobs(a, b):
    for job in (a, b):
        assert job.n_aliased in (0, len(job.inputs)) and not (job.n_aliased and job.extra_out)
    assert a.n_aliased or not b.n_aliased
    n_in, n_out, n_sem = len(a.inputs), len(a.out_shapes()), len(a.sems)

    def phase(name):
        def run(ins, outs, sems):
            getattr(a, name)(ins[:n_in], outs[:n_out], sems[:n_sem])
            getattr(b, name)(ins[n_in:], outs[n_out:], sems[n_sem:])
        return run

    return _Job(a.inputs + b.inputs, a.n_aliased + b.n_aliased, a.extra_out + b.extra_out, a.sems + b.sems,
                phase("start"), phase("mid"), phase("finish"))


def _sum_slots(pack, slots, me, tag):
    def body(me_ref, p_ref, s_ref, o_ref):
        acc = None
        for d in range(N_DEV):
            term = jnp.where(me_ref[0] == d, p_ref[...], s_ref[d])
            acc = term if acc is None else acc + term
        o_ref[...] = acc

    vm = pl.BlockSpec(memory_space=pltpu.VMEM)
    return pl.pallas_call(
        body, name=f"sum_slots_{tag}",
        in_specs=[pl.BlockSpec(memory_space=pltpu.SMEM), vm, vm], out_specs=vm,
        out_shape=jax.ShapeDtypeStruct(pack.shape, F32),
        compiler_params=pltpu.CompilerParams(vmem_limit_bytes=V7X_VMEM_LIMIT),
    )(jnp.reshape(me, (1,)).astype(jnp.int32), pack, slots)


def _pack_rows(arrays):
    total = sum(a.size for a in arrays)
    rows = -(-total // 128)
    rows = -(-rows // PACK_ROWS_ALIGN) * PACK_ROWS_ALIGN
    flat = [a.reshape(-1) for a in arrays] + [jnp.zeros((rows * 128 - total,), F32)]
    return jnp.concatenate(flat).reshape(rows, 128)


def _adamw_math(w, g, m, v):
    m = ADAM_B1 * m + (1.0 - ADAM_B1) * g
    v = ADAM_B2 * v + (1.0 - ADAM_B2) * (g * g)
    m_hat = m / (1.0 - ADAM_B1 ** ADAM_STEP)
    v_hat = v / (1.0 - ADAM_B2 ** ADAM_STEP)
    delta = -ADAM_LR * (m_hat / (jnp.sqrt(v_hat) + ADAM_EPS) + ADAM_WD * w)
    return delta, m, v


def _adamw_big(w, g0, g1, m, v, tag):
    _, R, C = w.shape
    tr = _pick(R, (256, 128))

    def body(w_ref, g0_ref, g1_ref, m_ref, v_ref, go_ref, d_ref, mo_ref, vo_ref):
        g = jnp.where(pl.program_id(0) == 0, g0_ref[...], g1_ref[...])
        delta, mn, vn = _adamw_math(w_ref[...], g, m_ref[...], v_ref[...])
        go_ref[...] = g
        d_ref[...] = delta
        mo_ref[...] = mn
        vo_ref[...] = vn

    s3 = pl.BlockSpec((None, tr, C), lambda l, t: (l, t, 0))
    g_spec = lambda layer: pl.BlockSpec((tr, C), lambda l, t: (jnp.where(l == layer, t, 0), 0))
    shp = jax.ShapeDtypeStruct(w.shape, F32)
    return pl.pallas_call(
        body, name=f"adamw_{tag}", grid=(2, R // tr),
        in_specs=[s3, g_spec(0), g_spec(1), s3, s3], out_specs=[s3, s3, s3, s3],
        out_shape=[shp, shp, shp, shp],
        compiler_params=_cp("parallel", "parallel"),
    )(w, g0, g1, m, v)


def _adamw_small(ws, gs, ms, vs):
    n = len(ws)

    def body(*refs):
        w_r, g_r, m_r, v_r = refs[:n], refs[n:2 * n], refs[2 * n:3 * n], refs[3 * n:4 * n]
        d_o, m_o, v_o = refs[4 * n:5 * n], refs[5 * n:6 * n], refs[6 * n:7 * n]
        for k in range(n):
            delta, mn, vn = _adamw_math(w_r[k][...], g_r[k][...], m_r[k][...], v_r[k][...])
            d_o[k][...] = delta
            m_o[k][...] = mn
            v_o[k][...] = vn

    vm = pl.BlockSpec(memory_space=pltpu.VMEM)
    shapes = [jax.ShapeDtypeStruct(w.shape, F32) for w in ws]
    outs = pl.pallas_call(
        body, name="adamw_small",
        in_specs=[vm] * (4 * n), out_specs=[vm] * (3 * n),
        out_shape=shapes * 3,
    )(*ws, *gs, *ms, *vs)
    return outs[:n], outs[n:2 * n], outs[2 * n:]


_WEIGHTS = ["meta_tokens", "ln_in_g", "ln_in_b", "w_in", "conv_dw_w", "conv_dw_b", "conv_ln_g", "conv_ln_b",
            "conv_pw_w", "conv_pw_b", "attn_sinks", "lru_conv_w", "lru_conv_b", "lru_wa", "lru_ba", "lru_wx",
            "lru_bx", "lru_lambda", "w_out", "ln_post_g", "ln_post_b"]
_BIG = ("w_in", "w_out", "conv_pw_w")
_SMALL_SHARDED = {"meta_tokens": 1, "conv_dw_w": 2, "lru_conv_w": 2}
PACK_ROWS_ALIGN = 8


def _as2d(a):
    return a.reshape(1, -1) if a.ndim == 1 else a.reshape(-1, a.shape[-1])


def kernel(x, meta_tokens, ln_in_g, ln_in_b, w_in, conv_dw_w, conv_dw_b, conv_ln_g, conv_ln_b, conv_pw_w, conv_pw_b, attn_sinks, lru_conv_w, lru_conv_b, lru_wa, lru_ba, lru_wx, lru_bx, lru_lambda, w_out, ln_post_g, ln_post_b, loss_target, m_meta_tokens, m_ln_in_g, m_ln_in_b, m_w_in, m_conv_dw_w, m_conv_dw_b, m_conv_ln_g, m_conv_ln_b, m_conv_pw_w, m_conv_pw_b, m_attn_sinks, m_lru_conv_w, m_lru_conv_b, m_lru_wa, m_lru_ba, m_lru_wx, m_lru_bx, m_lru_lambda, m_w_out, m_ln_post_g, m_ln_post_b, v_meta_tokens, v_ln_in_g, v_ln_in_b, v_w_in, v_conv_dw_w, v_conv_dw_b, v_conv_ln_g, v_conv_ln_b, v_conv_pw_w, v_conv_pw_b, v_attn_sinks, v_lru_conv_w, v_lru_conv_b, v_lru_wa, v_lru_ba, v_lru_wx, v_lru_bx, v_lru_lambda, v_w_out, v_ln_post_g, v_ln_post_b):
    w = dict(meta_tokens=meta_tokens, ln_in_g=ln_in_g, ln_in_b=ln_in_b, w_in=w_in, conv_dw_w=conv_dw_w,
             conv_dw_b=conv_dw_b, conv_ln_g=conv_ln_g, conv_ln_b=conv_ln_b, conv_pw_w=conv_pw_w,
             conv_pw_b=conv_pw_b, attn_sinks=attn_sinks, lru_conv_w=lru_conv_w, lru_conv_b=lru_conv_b,
             lru_wa=lru_wa, lru_ba=lru_ba, lru_wx=lru_wx, lru_bx=lru_bx, lru_lambda=lru_lambda, w_out=w_out,
             ln_post_g=ln_post_g, ln_post_b=ln_post_b)
    mom_m = dict(zip(_WEIGHTS, (m_meta_tokens, m_ln_in_g, m_ln_in_b, m_w_in, m_conv_dw_w, m_conv_dw_b, m_conv_ln_g,
                                m_conv_ln_b, m_conv_pw_w, m_conv_pw_b, m_attn_sinks, m_lru_conv_w, m_lru_conv_b,
                                m_lru_wa, m_lru_ba, m_lru_wx, m_lru_bx, m_lru_lambda, m_w_out, m_ln_post_g,
                                m_ln_post_b)))
    mom_v = dict(zip(_WEIGHTS, (v_meta_tokens, v_ln_in_g, v_ln_in_b, v_w_in, v_conv_dw_w, v_conv_dw_b, v_conv_ln_g,
                                v_conv_ln_b, v_conv_pw_w, v_conv_pw_b, v_attn_sinks, v_lru_conv_w, v_lru_conv_b,
                                v_lru_wa, v_lru_ba, v_lru_wx, v_lru_bx, v_lru_lambda, v_w_out, v_ln_post_g,
                                v_ln_post_b)))
    xi, yi, ci = _position()
    j = 2 * xi + yi

    g_meta, g_dw, g_lc = _gather_shards([meta_tokens, conv_dw_w, lru_conv_w])
    p = dict(w)
    p["w_in"] = [_cast_into_slot(w_in, l, j, "w_in") for l in range(DEPTH)]
    p["w_out"] = [_cast_into_slot(w_out, l, j, "w_out") for l in range(DEPTH)]
    p["conv_pw_w"] = [_cast_into_slot(conv_pw_w, l, j, "conv_pw_w") for l in range(DEPTH)]
    p["meta_tokens"] = g_meta.transpose(1, 0, 2).reshape(N_META, D)
    p["conv_dw_w"] = g_dw.transpose(1, 2, 0, 3).reshape(DEPTH, CONV_K, CW)
    p["lru_conv_w"] = g_lc.transpose(1, 2, 0, 3).reshape(DEPTH, LRU_K, LW)

    others = jnp.stack([jnp.where(j <= 0, 1, 0), jnp.where(j <= 1, 2, 1), jnp.where(j <= 2, 3, 2), ci]).astype(jnp.int32)
    me = 4 * xi + 2 * yi + ci
    order = jnp.stack([j, 2 * (1 - xi) + yi, 2 * xi + (1 - yi), 2 * (1 - xi) + (1 - yi)]).astype(jnp.int32)
    loss_part, grad_x, g = _device_step(x[0], loss_target[0], p, dist=(ci, j, others, me, order))
    loss = lax.psum(jnp.sum(loss_part), ("x", "y", "c"))
    big = {(name, l): g[name, l] for name in _BIG for l in range(DEPTH)}

    small_names = [n for n in _WEIGHTS if n not in _BIG]
    small_g = {}
    for names, red in ((_SMALL_LAYERED, g["pack_layered", -1]), (_SMALL_EMBED, g["pack_embed", -1])):
        red = red.reshape(-1)
        off = 0
        for n in names:
            fshape = list(w[n].shape)
            if n in _SMALL_SHARDED:
                fshape[_SMALL_SHARDED[n]] *= N_SHARD
            sz = 1
            for dim in fshape:
                sz *= dim
            full = red[off:off + sz].reshape(fshape)
            off += sz
            if n in _SMALL_SHARDED:
                ax = _SMALL_SHARDED[n]
                full = lax.dynamic_slice_in_dim(full, j * w[n].shape[ax], w[n].shape[ax], axis=ax)
            small_g[n] = full

    out_g, out_d, out_m, out_v = {}, {}, {}, {}
    for name in _BIG:
        shp = w[name].shape
        to3 = lambda a: a.reshape(DEPTH, -1, shp[-1])
        go, do, mo, vo = _adamw_big(to3(w[name]), big[name, 0], big[name, 1], to3(mom_m[name]), to3(mom_v[name]), name)
        out_g[name], out_d[name], out_m[name], out_v[name] = (a.reshape(shp) for a in (go, do, mo, vo))
    ds, ms, vs = _adamw_small([_as2d(w[n]) for n in small_names], [_as2d(small_g[n]) for n in small_names],
                              [_as2d(mom_m[n]) for n in small_names], [_as2d(mom_v[n]) for n in small_names])
    for n, d_, m_, v_ in zip(small_names, ds, ms, vs):
        out_g[n] = small_g[n]
        out_d[n], out_m[n], out_v[n] = d_.reshape(w[n].shape), m_.reshape(w[n].shape), v_.reshape(w[n].shape)

    return (loss, grad_x[None], *[out_g[n] for n in _WEIGHTS], *[out_d[n] for n in _WEIGHTS],
            *[out_m[n] for n in _WEIGHTS], *[out_v[n] for n in _WEIGHTS])
```

```python
import functools

import jax
import jax.numpy as jnp
from jax import lax
from jax.experimental import pallas as pl
from jax.experimental.pallas import tpu as pltpu

F32 = jnp.float32
BF16 = jnp.bfloat16

D = 2048
N_META = 16
CW = 512
CONV_K = 31
AW = 1024
KVW = 256
N_HEADS = 16
LW = 512
LRU_K = 4
LRU_C = 8.0
IN_TOTAL = 5120
ROT_HALF = 8
ROPE_THETA = 500000.0
LN_EPS = 1e-5
DEPTH = 2
ALPHA = (2.0 * DEPTH) ** 0.25
NEG_INF = -1e30
ADAM_LR, ADAM_B1, ADAM_B2, ADAM_EPS, ADAM_WD, ADAM_STEP = 0.001, 0.9, 0.999, 1e-08, 0.01, 10

BLK = 128
PAD = BLK - N_META
N_SHARD = 4
WIN_SH = IN_TOTAL // N_SHARD
WOUT_SH = D // N_SHARD
PW_SH = CW // N_SHARD
HALO = 32
LHALO = 8
V7X_VMEM_LIMIT = 60 * 1024 * 1024


def _cp(*sem):
    return pltpu.CompilerParams(dimension_semantics=sem if sem else None, vmem_limit_bytes=V7X_VMEM_LIMIT)


def _pick(total, prefs):
    for p in prefs:
        if total % p == 0:
            return p
    raise ValueError(f"no tile for {total}")


def _dot(a, b):
    return jnp.dot(a, b, preferred_element_type=F32)


def _dot_nt(a, b):
    return lax.dot_general(a, b, (((1,), (1,)), ((), ())), preferred_element_type=F32)


def _dot_tn(a, b):
    return lax.dot_general(a, b, (((0,), (0,)), ((), ())), preferred_element_type=F32)


def _sigmoid(x):
    return 1.0 / (1.0 + jnp.exp(-x))


def _silu_and_grad(x):
    s = _sigmoid(x)
    return x * s, s * (1.0 + x * (1.0 - s))


def _ln_rows(x, g, b):
    mu = jnp.mean(x, axis=-1, keepdims=True)
    xc = x - mu
    var = jnp.mean(xc * xc, axis=-1, keepdims=True)
    rstd = lax.rsqrt(var + LN_EPS)
    xhat = xc * rstd
    return xhat * g + b, xhat, rstd


def _ln_bwd_rows(dy, xhat, rstd, g):
    dxh = dy * g
    m1 = jnp.mean(dxh, axis=-1, keepdims=True)
    m2 = jnp.mean(dxh * xhat, axis=-1, keepdims=True)
    return rstd * (dxh - m1 - xhat * m2)


def _row_ids(n, base):
    return base + lax.broadcasted_iota(jnp.int32, (n, 1), 0)


def _colsum(x):
    return jnp.sum(x, axis=0, keepdims=True)


def _embed_fwd(x, meta, g, b, job=None):
    S = x.shape[0]
    nb = S // BLK + 1

    def body(x_ref, meta_ref, g_ref, b_ref, h_ref, hb_ref):
        n = pl.program_id(0)

        @pl.when(n == 0)
        def _():
            y, _, _ = _ln_rows(meta_ref[...], g_ref[...], b_ref[...])
            h_ref[...] = jnp.zeros_like(h_ref)
            h_ref[PAD:BLK, :] = y

        @pl.when(n > 0)
        def _():
            y, _, _ = _ln_rows(x_ref[...], g_ref[...], b_ref[...])
            h_ref[...] = y

        hb_ref[...] = h_ref[...].astype(BF16)

    return _side_call(
        body, job, name="embed_fwd", grid=(nb,),
        in_specs=[pl.BlockSpec((BLK, D), lambda n: (jnp.maximum(n - 1, 0), 0)),
                  pl.BlockSpec((N_META, D), lambda n: (0, 0)),
                  pl.BlockSpec((1, D), lambda n: (0, 0)),
                  pl.BlockSpec((1, D), lambda n: (0, 0))],
        out_specs=[pl.BlockSpec((BLK, D), lambda n: (n, 0)),
                   pl.BlockSpec((BLK, D), lambda n: (n, 0))],
        out_shape=[jax.ShapeDtypeStruct((nb * BLK, D), F32), jax.ShapeDtypeStruct((nb * BLK, D), BF16)],
        scratch_shapes=[], semantics=("arbitrary",), args=[x, meta, g, b])


def _embed_bwd(dh, x, meta, g, b):
    S = x.shape[0]
    nb = S // BLK + 1

    def body(dh_ref, x_ref, meta_ref, g_ref, b_ref, gx_ref, gm_ref, dg_ref, db_ref):
        n = pl.program_id(0)

        @pl.when(n == 0)
        def _():
            _, xhat, rstd = _ln_rows(meta_ref[...], g_ref[...], b_ref[...])
            dy = dh_ref[PAD:BLK, :]
            gm_ref[...] = _ln_bwd_rows(dy, xhat, rstd, g_ref[...])
            dg_ref[...] = _colsum(dy * xhat)
            db_ref[...] = _colsum(dy)

        @pl.when(n > 0)
        def _():
            _, xhat, rstd = _ln_rows(x_ref[...], g_ref[...], b_ref[...])
            dy = dh_ref[...]
            gx_ref[...] = _ln_bwd_rows(dy, xhat, rstd, g_ref[...])
            dg_ref[...] += _colsum(dy * xhat)
            db_ref[...] += _colsum(dy)

    prev = lambda n: (jnp.maximum(n - 1, 0), 0)
    const = lambda n: (0, 0)
    return pl.pallas_call(
        body, name="embed_bwd", grid=(nb,),
        in_specs=[pl.BlockSpec((BLK, D), lambda n: (n, 0)),
                  pl.BlockSpec((BLK, D), prev),
                  pl.BlockSpec((N_META, D), const),
                  pl.BlockSpec((1, D), const),
                  pl.BlockSpec((1, D), const)],
        out_specs=[pl.BlockSpec((BLK, D), prev),
                   pl.BlockSpec((N_META, D), const),
                   pl.BlockSpec((1, D), const),
                   pl.BlockSpec((1, D), const)],
        out_shape=[jax.ShapeDtypeStruct((S, D), F32), jax.ShapeDtypeStruct((N_META, D), F32),
                   jax.ShapeDtypeStruct((1, D), F32), jax.ShapeDtypeStruct((1, D), F32)],
        compiler_params=_cp("arbitrary"),
    )(dh, x, meta, g, b)


def _proj_fwd(hb, w_in, order, first, count, prev, l, job=None):
    T = hb.shape[0]
    tm = _pick(T, (1056, 384, 128))

    def body(o_sc, a_ref, w_ref, *rest):
        rest[-1][...] = _dot(a_ref[...], w_ref[...])

    return _side_call(
        body, job, name=f"proj_fwd{l}_{first}", grid=(T // tm, count),
        in_specs=[pl.BlockSpec((tm, D), lambda i, j, o: (i, 0)),
                  pl.BlockSpec((None, D, WIN_SH), lambda i, j, o: (o[first + j], 0, 0))]
        + ([] if prev is None else [pl.BlockSpec(memory_space=pl.ANY)]),
        out_specs=[pl.BlockSpec((tm, WIN_SH), lambda i, j, o: (i, o[first + j]))],
        out_shape=[jax.ShapeDtypeStruct((T, IN_TOTAL), F32)],
        scratch_shapes=[], semantics=("parallel", "arbitrary"),
        args=[hb, w_in] + ([] if prev is None else [prev]),
        aliases=None if prev is None else {2: 0}, prefetch=[order])


def _out_fwd(yc, ya, yl, w_out, h, g, b, l, job=None):
    T = h.shape[0]
    tm = _pick(T, (384, 128))

    def body(yc_ref, ya_ref, yl_ref, w_ref, h_ref, g_ref, b_ref, hn_ref, hnb_ref, xh_ref, rs_ref):
        acc = _dot(yc_ref[...], w_ref[0])
        acc += _dot(ya_ref[:, 0:WOUT_SH], w_ref[1])
        acc += _dot(ya_ref[:, WOUT_SH:2 * WOUT_SH], w_ref[2])
        acc += _dot(yl_ref[...], w_ref[3])
        z = ALPHA * h_ref[...] + acc
        y, xhat, rstd = _ln_rows(z, g_ref[...], b_ref[...])
        hn_ref[...] = y
        hnb_ref[...] = y.astype(BF16)
        xh_ref[...] = xhat
        rs_ref[...] = rstd

    row = lambda i: (i, 0)
    return _side_call(
        body, job, name=f"out_fwd{l}", grid=(T // tm,),
        in_specs=[pl.BlockSpec((tm, CW), row), pl.BlockSpec((tm, AW), row), pl.BlockSpec((tm, LW), row),
                  pl.BlockSpec((N_SHARD, WOUT_SH, D), lambda i: (0, 0, 0)),
                  pl.BlockSpec((tm, D), row),
                  pl.BlockSpec((None, 1, D), lambda i: (l, 0, 0)),
                  pl.BlockSpec((None, 1, D), lambda i: (l, 0, 0))],
        out_specs=[pl.BlockSpec((tm, D), row), pl.BlockSpec((tm, D), row), pl.BlockSpec((tm, D), row),
                   pl.BlockSpec((tm, 1), row)],
        out_shape=[jax.ShapeDtypeStruct((T, D), F32), jax.ShapeDtypeStruct((T, D), BF16),
                   jax.ShapeDtypeStruct((T, D), F32), jax.ShapeDtypeStruct((T, 1), F32)],
        scratch_shapes=[], semantics=("parallel",), args=[yc, ya, yl, w_out, h, g, b])


def _post_ln_dcat_bwd(src, target, xhat, rstd, g, w_out, conv, proj, cln_g, cln_b, pw_w, pw_b, l, job=None):
    T = src.shape[0]
    tm = _pick(T, (384, 128))
    per = tm // BLK if target is not None else 0
    last_blk = target.shape[0] // BLK - 1 if target is not None else 0

    def body(s_ref, *refs):
        t_refs = refs[:per]
        (xh_ref, rs_ref, g_ref, w_ref, conv_ref, ct_ref, cg_ref, cb_ref, pw_ref, pb_ref,
         part_ref, dz_ref, dzb_ref, dg_ref, db_ref, da_ref, dl_ref,
         dconv_ref, dct_ref, dpw_ref, dpb_ref, dcg_ref, dcb_ref) = refs[per:]
        i = pl.program_id(0)

        @pl.when(i == 0)
        def _():
            for ref in (part_ref, dg_ref, db_ref, dpw_ref, dpb_ref, dcg_ref, dcb_ref):
                ref[...] = jnp.zeros_like(ref)

        if per:
            tgt = jnp.concatenate([r[...] for r in t_refs], axis=0) if per > 1 else t_refs[0][...]
            real = _row_ids(tm, i * tm) >= BLK
            err = jnp.where(real, s_ref[...] - tgt, 0.0)
            part_ref[...] += _colsum(err * err) * (0.5 / D)
            dy = err * (1.0 / D)
        else:
            dy = s_ref[...]
        xhat = xh_ref[...]
        dz = _ln_bwd_rows(dy, xhat, rs_ref[...], g_ref[...])
        dzb = dz.astype(BF16)
        dz_ref[...] = dz
        dzb_ref[...] = dzb
        dg_ref[...] += _colsum(dy * xhat)
        db_ref[...] += _colsum(dy)
        da_ref[:, 0:WOUT_SH] = _dot_nt(dzb, w_ref[1])
        da_ref[:, WOUT_SH:2 * WOUT_SH] = _dot_nt(dzb, w_ref[2])
        dl_ref[...] = _dot_nt(dzb, w_ref[3])

        d_yc = _dot_nt(dzb, w_ref[0])
        u, chat, crstd = _ln_rows(conv_ref[...], cg_ref[...], cb_ref[...])
        s, ds_du = _silu_and_grad(u)
        sb = s.astype(BF16)
        cpw = _dot(sb, pw_ref[...]) + pb_ref[...]
        gate, dgate = _silu_and_grad(ct_ref[...])
        d_cpw = d_yc * gate
        dct_ref[...] = (d_yc * cpw * dgate).astype(BF16)
        d_cpw_b = d_cpw.astype(BF16)
        dpb_ref[...] += _colsum(d_cpw)
        dpw_ref[...] += _dot_tn(sb, d_cpw_b)
        du = _dot_nt(d_cpw_b, pw_ref[...]) * ds_du
        dconv_ref[...] = _ln_bwd_rows(du, chat, crstd, cg_ref[...])
        dcg_ref[...] += _colsum(du * chat)
        dcb_ref[...] += _colsum(du)

    row = lambda i: (i, 0)
    const = lambda i: (0, 0)
    vec = pl.BlockSpec((None, 1, CW), lambda i: (l, 0, 0))
    t_specs = [pl.BlockSpec((BLK, D), functools.partial(lambda i, q: (jnp.clip(i * per - 1 + q, 0, last_blk), 0), q=q))
               for q in range(per)]
    return _side_call(
        body, job, name=f"post_ln_dcat_bwd{l}", grid=(T // tm,),
        in_specs=[pl.BlockSpec((tm, D), row)] + t_specs + [
            pl.BlockSpec((tm, D), row), pl.BlockSpec((tm, 1), row), pl.BlockSpec((None, 1, D), lambda i: (l, 0, 0)),
            pl.BlockSpec((N_SHARD, WOUT_SH, D), lambda i: (0, 0, 0)),
            pl.BlockSpec((tm, CW), row), pl.BlockSpec((tm, CW), lambda i: (i, 2)), vec, vec,
            pl.BlockSpec((CW, CW), const), vec],
        out_specs=[pl.BlockSpec((1, D), const), pl.BlockSpec((tm, D), row), pl.BlockSpec((tm, D), row),
                   pl.BlockSpec((1, D), const), pl.BlockSpec((1, D), const),
                   pl.BlockSpec((tm, AW), row), pl.BlockSpec((tm, LW), row),
                   pl.BlockSpec((tm, CW), row), pl.BlockSpec((tm, CW), lambda i: (i, 2)),
                   pl.BlockSpec((CW, CW), const), pl.BlockSpec((1, CW), const),
                   pl.BlockSpec((1, CW), const), pl.BlockSpec((1, CW), const)],
        out_shape=[jax.ShapeDtypeStruct((1, D), F32), jax.ShapeDtypeStruct((T, D), F32),
                   jax.ShapeDtypeStruct((T, D), BF16), jax.ShapeDtypeStruct((1, D), F32),
                   jax.ShapeDtypeStruct((1, D), F32),
                   jax.ShapeDtypeStruct((T, AW), F32), jax.ShapeDtypeStruct((T, LW), F32),
                   jax.ShapeDtypeStruct((T, CW), F32), jax.ShapeDtypeStruct((T, IN_TOTAL), BF16),
                   jax.ShapeDtypeStruct((CW, CW), F32), jax.ShapeDtypeStruct((1, CW), F32),
                   jax.ShapeDtypeStruct((1, CW), F32), jax.ShapeDtypeStruct((1, CW), F32)],
        scratch_shapes=[], semantics=("arbitrary",),
        args=[src] + [target] * per + [xhat, rstd, g, w_out, conv, proj, cln_g, cln_b, pw_w, pw_b])


def _dwout_bwd(yc, ya, yl, dzb, l):
    T = dzb.shape[0]
    tm = _pick(T, (384, 128))

    def body(yc_ref, ya_ref, yl_ref, dz_ref, o_ref):
        @pl.when(pl.program_id(0) == 0)
        def _():
            o_ref[...] = jnp.zeros_like(o_ref)

        cat = jnp.concatenate([yc_ref[...], ya_ref[...], yl_ref[...]], axis=1)
        o_ref[...] += _dot_tn(cat, dz_ref[...])

    row = lambda t: (t, 0)
    out = pl.pallas_call(
        body, name=f"dwout_bwd{l}", grid=(T // tm,),
        in_specs=[pl.BlockSpec((tm, CW), row), pl.BlockSpec((tm, AW), row), pl.BlockSpec((tm, LW), row),
                  pl.BlockSpec((tm, D), row)],
        out_specs=pl.BlockSpec((D, D), lambda t: (0, 0)),
        out_shape=jax.ShapeDtypeStruct((D, D), F32),
        compiler_params=_cp("arbitrary"),
    )(yc, ya, yl, dzb)
    return out.reshape(N_SHARD, 2, WOUT_SH // 2, D)


def _dh_bwd(dproj, w_in, dz, l, job=None):
    T = dproj.shape[0]
    tm = _pick(T, (1056, 384, 128))

    n_w = len(w_in)

    def body(dp_ref, *refs):
        w_refs, (dz_ref, o_ref, acc_ref) = refs[:n_w], refs[n_w:]
        j = pl.program_id(1)

        @pl.when(j == 0)
        def _():
            acc_ref[...] = ALPHA * dz_ref[...]

        dp = dp_ref[...]
        off = 0
        for w_ref in w_refs:
            rows = w_ref.shape[0]
            acc_ref[:, off:off + rows] += _dot_nt(dp, w_ref[...])
            off += rows

        @pl.when(j == N_SHARD - 1)
        def _():
            o_ref[...] = acc_ref[...]

    return _side_call(
        body, job, name=f"dh_bwd{l}", grid=(T // tm, N_SHARD),
        in_specs=[pl.BlockSpec((tm, WIN_SH), lambda i, j: (i, j))]
        + [pl.BlockSpec((None, w.shape[1], WIN_SH), lambda i, j: (j, 0, 0)) for w in w_in]
        + [pl.BlockSpec((tm, D), lambda i, j: (i, 0))],
        out_specs=[pl.BlockSpec((tm, D), lambda i, j: (i, 0))],
        out_shape=[jax.ShapeDtypeStruct((T, D), F32)],
        scratch_shapes=[pltpu.VMEM((tm, D), F32)],
        semantics=("parallel", "arbitrary"), args=[dproj, *w_in, dz])


def _dwin_bwd(hb, dproj, l):
    T = hb.shape[0]
    tm = _pick(T, (1056, 384, 128))

    def body(h_ref, dp_ref, o_ref):
        @pl.when(pl.program_id(1) == 0)
        def _():
            o_ref[...] = jnp.zeros_like(o_ref)

        o_ref[...] += _dot_tn(h_ref[...], dp_ref[...])

    out = pl.pallas_call(
        body, name=f"dwin_bwd{l}", grid=(N_SHARD, T // tm),
        in_specs=[pl.BlockSpec((tm, D), lambda j, t: (t, 0)),
                  pl.BlockSpec((tm, WIN_SH), lambda j, t: (t, j))],
        out_specs=pl.BlockSpec((None, D, WIN_SH), lambda j, t: (j, 0, 0)),
        out_shape=jax.ShapeDtypeStruct((N_SHARD, D, WIN_SH), F32),
        compiler_params=_cp("parallel", "arbitrary"),
    )(hb, dproj)
    return out.reshape(N_SHARD, 2, D // 2, WIN_SH)


def _dwin_half(hb, dproj, which, l, tag, job=None):
    T = hb.shape[0]
    tm = _pick(T, (1056, 384, 128))
    hr = D // 2

    def body(w_ref, h_ref, dp_ref, o_ref):
        @pl.when(pl.program_id(1) == 0)
        def _():
            o_ref[...] = jnp.zeros_like(o_ref)

        o_ref[...] += _dot_tn(h_ref[...], dp_ref[...])

    return _side_call(
        body, job, name=f"dwin_{tag}{l}", grid=(N_SHARD, T // tm),
        in_specs=[pl.BlockSpec((tm, hr), lambda j, t, w: (t, w[0])),
                  pl.BlockSpec((tm, WIN_SH), lambda j, t, w: (t, j))],
        out_specs=[pl.BlockSpec((None, hr, WIN_SH), lambda j, t, w: (j, 0, 0))],
        out_shape=[jax.ShapeDtypeStruct((N_SHARD, hr, WIN_SH), F32)],
        scratch_shapes=[], semantics=("parallel", "arbitrary"), args=[hb, dproj],
        prefetch=[jnp.reshape(which, (1,)).astype(jnp.int32)])


def _glu_masked(v, g, base_row):
    rows = _row_ids(v.shape[0], base_row)
    return jnp.where(rows >= PAD, v * _sigmoid(g), 0.0)


def _conv_tile(T):
    return _pick(T, (384, 128))


SUBLANES = 8


def _for_each_shift(buf, rot, tm, offsets, fn):
    for r in range(SUBLANES):
        group = [o for o in offsets if o % SUBLANES == r]
        if not group:
            continue
        if r == 0:
            src = buf
        else:
            n = tm + max(group) - r
            rot[0:n, :] = buf[r:r + n, :]
            src = rot
        for o in group:
            fn(o, src[o - r:o - r + tm, :])


def _conv_fwd(proj, dw_w, dw_b, ln_g, ln_b, pw_w, pw_b, l, job=None):
    T = proj.shape[0]
    tm = _conv_tile(T)
    hb = tm // HALO

    def body(cv_ref, cg_ref, ct_ref, hv_ref, hg_ref, w_ref, b_ref, g_ref, be_ref, pw_ref, pb_ref,
             yc_ref, conv_ref, buf, rot):
        i = pl.program_id(0)
        buf[0:HALO, :] = _glu_masked(hv_ref[...], hg_ref[...], i * tm - HALO)
        buf[HALO:HALO + tm, :] = _glu_masked(cv_ref[...], cg_ref[...], i * tm)
        first = HALO - (CONV_K - 1)
        total = [jnp.zeros((tm, CW), F32) + b_ref[...]]

        def tap(o, tile):
            k = o - first
            total[0] = total[0] + w_ref[k:k + 1, :] * tile

        _for_each_shift(buf, rot, tm, [first + k for k in range(CONV_K)], tap)
        acc = total[0]
        conv_ref[...] = acc
        u, _, _ = _ln_rows(acc, g_ref[...], be_ref[...])
        s = u * _sigmoid(u)
        cpw = _dot(s.astype(BF16), pw_ref[...]) + pb_ref[...]
        gate, _ = _silu_and_grad(ct_ref[...])
        yc_ref[...] = (cpw * gate).astype(BF16)

    vec = pl.BlockSpec((None, 1, CW), lambda i: (l, 0, 0))
    return _side_call(
        body, job, name=f"conv_fwd{l}", grid=(T // tm,),
        in_specs=[pl.BlockSpec((tm, CW), lambda i: (i, 0)),
                  pl.BlockSpec((tm, CW), lambda i: (i, 1)),
                  pl.BlockSpec((tm, CW), lambda i: (i, 2)),
                  pl.BlockSpec((HALO, CW), lambda i: (jnp.maximum(i * hb - 1, 0), 0)),
                  pl.BlockSpec((HALO, CW), lambda i: (jnp.maximum(i * hb - 1, 0), 1)),
                  pl.BlockSpec((None, CONV_K, CW), lambda i: (l, 0, 0)),
                  vec, vec, vec,
                  pl.BlockSpec((CW, CW), lambda i: (0, 0)),
                  vec],
        out_specs=[pl.BlockSpec((tm, CW), lambda i: (i, 0)), pl.BlockSpec((tm, CW), lambda i: (i, 0))],
        out_shape=[jax.ShapeDtypeStruct((T, CW), BF16), jax.ShapeDtypeStruct((T, CW), F32)],
        scratch_shapes=[pltpu.VMEM((tm + HALO, CW), F32), pltpu.VMEM((tm + HALO, CW), F32)],
        semantics=("parallel",), args=[proj, proj, proj, proj, proj, dw_w, dw_b, ln_g, ln_b, pw_w, pw_b])


def _conv_bwd_taps(d_conv, proj, dw_w, dproj, l, job=None):
    T = d_conv.shape[0]
    tm = _conv_tile(T)
    hb = tm // HALO
    nt = T // tm
    last_halo = T // HALO - 1

    def body(dc_ref, dh_ref, cv_ref, cg_ref, hv_ref, hg_ref, w_ref, _, o_ref, dw_ref, dwb_ref, cbuf, dbuf, rot):
        i = pl.program_id(0)

        @pl.when(i == 0)
        def _():
            dw_ref[...] = jnp.zeros_like(dw_ref)
            dwb_ref[...] = jnp.zeros_like(dwb_ref)

        cbuf[0:HALO, :] = _glu_masked(hv_ref[...], hg_ref[...], i * tm - HALO)
        cbuf[HALO:HALO + tm, :] = _glu_masked(cv_ref[...], cg_ref[...], i * tm)
        dmain = dc_ref[...]
        dbuf[0:tm, :] = dmain
        dbuf[tm:tm + HALO, :] = jnp.where(i < nt - 1, dh_ref[...], 0.0)
        total = [jnp.zeros((tm, CW), F32)]

        def tap_back(o, tile):
            k = CONV_K - 1 - o
            total[0] = total[0] + w_ref[k:k + 1, :] * tile

        _for_each_shift(dbuf, rot, tm, list(range(CONV_K)), tap_back)
        acc = total[0]
        first = HALO - (CONV_K - 1)

        def tap_weight(o, tile):
            k = o - first
            dw_ref[k:k + 1, :] += _colsum(dmain * tile)

        _for_each_shift(cbuf, rot, tm, [first + k for k in range(CONV_K)], tap_weight)
        dwb_ref[...] += _colsum(dmain)
        d_c = jnp.where(_row_ids(tm, i * tm) >= PAD, acc, 0.0)
        sig = _sigmoid(cg_ref[...])
        o_ref[:, 0:CW] = (d_c * sig).astype(BF16)
        o_ref[:, CW:2 * CW] = (d_c * cv_ref[...] * sig * (1.0 - sig)).astype(BF16)

    const = lambda i: (0, 0)
    return _side_call(
        body, job, name=f"conv_bwd_taps{l}", grid=(nt,),
        in_specs=[pl.BlockSpec((tm, CW), lambda i: (i, 0)),
                  pl.BlockSpec((HALO, CW), lambda i: (jnp.minimum((i + 1) * hb, last_halo), 0)),
                  pl.BlockSpec((tm, CW), lambda i: (i, 0)),
                  pl.BlockSpec((tm, CW), lambda i: (i, 1)),
                  pl.BlockSpec((HALO, CW), lambda i: (jnp.maximum(i * hb - 1, 0), 0)),
                  pl.BlockSpec((HALO, CW), lambda i: (jnp.maximum(i * hb - 1, 0), 1)),
                  pl.BlockSpec((None, CONV_K, CW), lambda i: (l, 0, 0)),
                  pl.BlockSpec(memory_space=pl.ANY)],
        out_specs=[pl.BlockSpec((tm, 2 * CW), lambda i: (i, 0)),
                   pl.BlockSpec((HALO, CW), const), pl.BlockSpec((1, CW), const)],
        out_shape=[jax.ShapeDtypeStruct(dproj.shape, BF16), jax.ShapeDtypeStruct((HALO, CW), F32),
                   jax.ShapeDtypeStruct((1, CW), F32)],
        scratch_shapes=[pltpu.VMEM((tm + HALO, CW), F32), pltpu.VMEM((tm + HALO, CW), F32),
                        pltpu.VMEM((tm + HALO, CW), F32)],
        semantics=("arbitrary",), aliases={7: 0},
        args=[d_conv, d_conv, proj, proj, proj, proj, dw_w, dproj])


def _log1p_small(e):
    return jnp.where(e < 1e-3, e * (1.0 - e * (0.5 - e * (1.0 / 3.0))), jnp.log(1.0 + e))


def _softplus(z):
    return jnp.maximum(z, 0.0) + _log1p_small(jnp.exp(-jnp.abs(z)))


def _neg_expm1(x):
    series = -x * (1.0 + x * (1.0 / 2.0) * (1.0 + x * (1.0 / 3.0) * (1.0 + x * (1.0 / 4.0) * (
        1.0 + x * (1.0 / 5.0) * (1.0 + x * (1.0 / 6.0) * (1.0 + x * (1.0 / 7.0)))))))
    return jnp.where(x > -0.25, series, 1.0 - jnp.exp(x))


def _lru_gates(rxbuf, tm, base_row, lw_ref, lb_ref, wa_ref, ba_ref, wx_ref, bx_ref, lam_ref):
    rc = jnp.zeros((tm, LW), F32) + lb_ref[...]
    for k in range(LRU_K):
        o = LHALO - (LRU_K - 1) + k
        rc += lw_ref[k:k + 1, :] * rxbuf[o:o + tm, :]
    rcb = rc.astype(BF16)
    r = _sigmoid(_dot(rcb, wa_ref[...]) + ba_ref[...])
    ig = _sigmoid(_dot(rcb, wx_ref[...]) + bx_ref[...])
    sp = _softplus(-lam_ref[...])
    la = -LRU_C * r * sp
    a = jnp.exp(la)
    mult = jnp.sqrt(_neg_expm1(2.0 * la))
    valid = _row_ids(tm, base_row) >= PAD
    return rc, rcb, r, ig, sp, a, mult, valid


def _mask_rows(v, base_row):
    return jnp.where(_row_ids(v.shape[0], base_row) >= PAD, v, 0.0)


def _scan_rows(aa, bb, carry, out_ref, reverse):
    tm = aa.shape[0]
    sub = _row_ids(tm, 0) & (SUBLANES - 1)
    s = 1
    while s < SUBLANES:
        keep = (sub < SUBLANES - s) if reverse else (sub >= s)
        shift = tm - s if reverse else s
        a_s = jnp.where(keep, pltpu.roll(aa, shift, axis=0), 1.0)
        b_s = jnp.where(keep, pltpu.roll(bb, shift, axis=0), 0.0)
        bb = aa * b_s + bb
        aa = aa * a_s
        s *= 2
    groups = range(tm // SUBLANES)
    edge = 0 if reverse else SUBLANES - 1
    for j in (reversed(groups) if reverse else groups):
        rows = slice(SUBLANES * j, SUBLANES * j + SUBLANES)
        x = bb[rows] + aa[rows] * carry
        out_ref[rows, :] = x
        carry = x[edge:edge + 1]


def _lru_tile(T):
    return _pick(T, (384, 128))


def _lru_fwd(proj, lw, lb, wa, ba, wx, bx, lam, l, job=None):
    T = proj.shape[0]
    tm = _lru_tile(T)
    hb = tm // LHALO

    def body(rx_ref, rg_ref, hx_ref, lw_ref, lb_ref, wa_ref, ba_ref, wx_ref, bx_ref, lam_ref,
             yl_ref, hl_ref, rxbuf, carry):
        i = pl.program_id(0)

        @pl.when(i == 0)
        def _():
            carry[...] = jnp.zeros_like(carry)

        rxbuf[0:LHALO, :] = _mask_rows(hx_ref[...], i * tm - LHALO)
        rxbuf[LHALO:LHALO + tm, :] = _mask_rows(rx_ref[...], i * tm)
        rc, _, _, ig, _, a, mult, valid = _lru_gates(rxbuf, tm, i * tm, lw_ref, lb_ref, wa_ref, ba_ref,
                                                     wx_ref, bx_ref, lam_ref)
        bb = jnp.where(valid, mult * (ig * rc), 0.0)
        _scan_rows(a, bb, carry[0:1, :], hl_ref, reverse=False)
        carry[0:1, :] = hl_ref[tm - 1:tm, :]
        gate, _ = _silu_and_grad(rg_ref[...])
        yl_ref[...] = (hl_ref[...] * gate).astype(BF16)

    vec = pl.BlockSpec((None, 1, LW), lambda i: (l, 0, 0))
    mat = pl.BlockSpec((None, LW, LW), lambda i: (l, 0, 0))
    return _side_call(
        body, job, name=f"lru_fwd{l}", grid=(T // tm,),
        in_specs=[pl.BlockSpec((tm, LW), lambda i: (i, 8)),
                  pl.BlockSpec((tm, LW), lambda i: (i, 9)),
                  pl.BlockSpec((LHALO, LW), lambda i: (jnp.maximum(i * hb - 1, 0), 8)),
                  pl.BlockSpec((None, LRU_K, LW), lambda i: (l, 0, 0)),
                  vec, mat, vec, mat, vec, vec],
        out_specs=[pl.BlockSpec((tm, LW), lambda i: (i, 0)), pl.BlockSpec((tm, LW), lambda i: (i, 0))],
        out_shape=[jax.ShapeDtypeStruct((T, LW), BF16), jax.ShapeDtypeStruct((T, LW), F32)],
        scratch_shapes=[pltpu.VMEM((tm + LHALO, LW), F32), pltpu.VMEM((8, LW), F32)],
        semantics=("arbitrary",), args=[proj, proj, proj, lw, lb, wa, ba, wx, bx, lam])


def _lru_bwd(proj, hl, d_yl, lw, lb, wa, ba, wx, bx, lam, dproj, l, job=None):
    T = proj.shape[0]
    tm = _lru_tile(T)
    hb = tm // LHALO
    nt = T // tm

    def body(rx_ref, rg_ref, hx_ref, hl_ref, hh_ref, dy_ref, lw_ref, lb_ref, wa_ref, ba_ref, wx_ref, bx_ref,
             lam_ref, _, o_ref, dlw_ref, dlb_ref, dwa_ref, dba_ref, dwx_ref, dbx_ref, dlam_ref,
             rxbuf, dbuf, carry, head, gbuf):
        step = pl.program_id(0)
        i = nt - 1 - step

        @pl.when(step == 0)
        def _():
            carry[...] = jnp.zeros_like(carry)
            head[...] = jnp.zeros_like(head)
            for ref in (dlw_ref, dlb_ref, dwa_ref, dba_ref, dwx_ref, dbx_ref, dlam_ref):
                ref[...] = jnp.zeros_like(ref)

        rxbuf[0:LHALO, :] = _mask_rows(hx_ref[...], i * tm - LHALO)
        rxbuf[LHALO:LHALO + tm, :] = _mask_rows(rx_ref[...], i * tm)
        rc, rcb, r, ig, sp, a, mult, valid = _lru_gates(rxbuf, tm, i * tm, lw_ref, lb_ref, wa_ref, ba_ref,
                                                        wx_ref, bx_ref, lam_ref)
        rows = _row_ids(tm, 0)
        h = hl_ref[...]
        h_before = jnp.where(i > 0, hh_ref[LHALO - 1:LHALO, :], 0.0)
        hprev = jnp.where(rows == 0, h_before, pltpu.roll(h, 1, axis=0))
        rg = rg_ref[...]
        gate, dgate = _silu_and_grad(rg)
        dy = dy_ref[...]
        o_ref[:, LW:2 * LW] = (dy * h * dgate).astype(BF16)
        bb = dy * gate + jnp.where(rows == tm - 1, carry[0:1, :], 0.0)
        aa = jnp.where(rows == tm - 1, 0.0, pltpu.roll(a, tm - 1, axis=0))
        _scan_rows(aa, bb, jnp.zeros((1, LW), F32), gbuf, reverse=True)
        g = gbuf[...]
        dbuf[0:tm, :] = a * g
        carry[0:1, :] = dbuf[0:1, :]
        du = jnp.where(valid, g, 0.0)
        da = g * hprev
        dix = du * mult
        dmult = du * (ig * rc)
        dla = jnp.where(valid, da * a - dmult * (a * a) / mult, 0.0)
        dr = dla * (-LRU_C * sp)
        dlam_ref[...] += _colsum(dla * (LRU_C * r)) * _sigmoid(-lam_ref[...])
        dpa = dr * r * (1.0 - r)
        dpx = (dix * rc) * ig * (1.0 - ig)
        dpab = dpa.astype(BF16)
        dpxb = dpx.astype(BF16)
        dba_ref[...] += _colsum(dpa)
        dbx_ref[...] += _colsum(dpx)
        dwa_ref[...] += _dot_tn(rcb, dpab)
        dwx_ref[...] += _dot_tn(rcb, dpxb)
        drc = dix * ig + _dot_nt(dpab, wa_ref[...]) + _dot_nt(dpxb, wx_ref[...])
        dbuf[0:tm, :] = drc
        dbuf[tm:tm + LHALO, :] = head[...]
        acc = jnp.zeros((tm, LW), F32)
        for k in range(LRU_K):
            o = LRU_K - 1 - k
            acc += lw_ref[k:k + 1, :] * dbuf[o:o + tm, :]
            oc = LHALO - (LRU_K - 1) + k
            dlw_ref[k:k + 1, :] += _colsum(drc * rxbuf[oc:oc + tm, :])
        dlb_ref[...] += _colsum(drc)
        head[...] = dbuf[0:LHALO, :]
        o_ref[:, 0:LW] = jnp.where(valid, acc, 0.0).astype(BF16)

    rev = lambda s: nt - 1 - s
    vec = pl.BlockSpec((None, 1, LW), lambda s: (l, 0, 0))
    mat = pl.BlockSpec((None, LW, LW), lambda s: (l, 0, 0))
    const = lambda s: (0, 0)
    halo = lambda s: jnp.maximum(rev(s) * hb - 1, 0)
    return _side_call(
        body, job, name=f"lru_bwd{l}", grid=(nt,),
        in_specs=[pl.BlockSpec((tm, LW), lambda s: (rev(s), 8)),
                  pl.BlockSpec((tm, LW), lambda s: (rev(s), 9)),
                  pl.BlockSpec((LHALO, LW), lambda s: (halo(s), 8)),
                  pl.BlockSpec((tm, LW), lambda s: (rev(s), 0)),
                  pl.BlockSpec((LHALO, LW), lambda s: (halo(s), 0)),
                  pl.BlockSpec((tm, LW), lambda s: (rev(s), 0)),
                  pl.BlockSpec((None, LRU_K, LW), lambda s: (l, 0, 0)),
                  vec, mat, vec, mat, vec, vec, pl.BlockSpec(memory_space=pl.ANY)],
        out_specs=[pl.BlockSpec((tm, 2 * LW), lambda s: (rev(s), 4)),
                   pl.BlockSpec((8, LW), const), pl.BlockSpec((1, LW), const),
                   pl.BlockSpec((LW, LW), const), pl.BlockSpec((1, LW), const),
                   pl.BlockSpec((LW, LW), const), pl.BlockSpec((1, LW), const),
                   pl.BlockSpec((1, LW), const)],
        out_shape=[jax.ShapeDtypeStruct(dproj.shape, BF16),
                   jax.ShapeDtypeStruct((8, LW), F32), jax.ShapeDtypeStruct((1, LW), F32),
                   jax.ShapeDtypeStruct((LW, LW), F32), jax.ShapeDtypeStruct((1, LW), F32),
                   jax.ShapeDtypeStruct((LW, LW), F32), jax.ShapeDtypeStruct((1, LW), F32),
                   jax.ShapeDtypeStruct((1, LW), F32)],
        scratch_shapes=[pltpu.VMEM((tm + LHALO, LW), F32), pltpu.VMEM((tm + LHALO, LW), F32),
                        pltpu.VMEM((8, LW), F32), pltpu.VMEM((LHALO, LW), F32), pltpu.VMEM((tm, LW), F32)],
        semantics=("arbitrary",), aliases={13: 0},
        args=[proj, proj, proj, hl, hl, d_yl, lw, lb, wa, ba, wx, bx, lam, dproj])


def _rope_tables(T):
    pos = (lax.broadcasted_iota(jnp.int32, (T, 128), 0) - PAD).astype(F32)
    lane = lax.broadcasted_iota(jnp.int32, (T, 128), 1) % 64
    inv_freq = ROPE_THETA ** (-(lane % ROT_HALF).astype(F32) / ROT_HALF)
    ang = pos * inv_freq
    cos, sin = jnp.cos(ang), jnp.sin(ang)
    c = jnp.where(lane < 2 * ROT_HALF, cos, 1.0)
    s1 = jnp.where(lane < ROT_HALF, -sin, 0.0)
    s2 = jnp.where((lane >= ROT_HALF) & (lane < 2 * ROT_HALF), sin, 0.0)
    return c, s1, s2


def _rot_fwd(x, c, s1, s2):
    return x * c + pltpu.roll(x, 128 - ROT_HALF, axis=1) * s1 + pltpu.roll(x, ROT_HALF, axis=1) * s2


def _rot_bwd(dy, c, s1, s2):
    return dy * c + pltpu.roll(dy * s1, ROT_HALF, axis=1) + pltpu.roll(dy * s2, 128 - ROT_HALF, axis=1)


KV2 = 2 * KVW


def _rope_fwd(proj, tabs, l):
    T = proj.shape[0]

    def both_halves(x, o_ref, pg):
        lane = lax.broadcasted_iota(jnp.int32, (1, 128), 1)
        for off in range(2):
            half = jnp.where((lane < 64) if off == 0 else (lane >= 64), x, 0.0)
            g = 2 * pg + off
            o_ref[:, 128 * g:128 * g + 128] = (half + pltpu.roll(half, 64, axis=1)).astype(BF16)

    def body(ql_ref, qh_ref, k_ref, v_ref, c_ref, s1_ref, s2_ref, qr_ref, vb_ref, kr2_ref, vb2_ref):
        c, s1, s2 = c_ref[...], s1_ref[...], s2_ref[...]
        for gcol in range(AW // 128):
            src = ql_ref if gcol < 4 else qh_ref
            x = src[:, 128 * (gcol % 4):128 * (gcol % 4) + 128]
            qr_ref[:, 128 * gcol:128 * gcol + 128] = (_rot_fwd(x, c, s1, s2) * 0.125).astype(BF16)
        for pg in range(KVW // 128):
            cols = slice(128 * pg, 128 * pg + 128)
            both_halves(_rot_fwd(k_ref[:, cols], c, s1, s2), kr2_ref, pg)
            vb_ref[:, cols] = v_ref[:, cols].astype(BF16)
            both_halves(v_ref[:, cols], vb2_ref, pg)

    tr = _pick(T, (384, 128))
    tab = pl.BlockSpec((tr, 128), lambda n: (n, 0))
    return pl.pallas_call(
        body, name=f"rope_fwd{l}", grid=(T // tr,),
        in_specs=[pl.BlockSpec((tr, 512), lambda n: (n, 3)), pl.BlockSpec((tr, 512), lambda n: (n, 4)),
                  pl.BlockSpec((tr, KVW), lambda n: (n, 10)), pl.BlockSpec((tr, KVW), lambda n: (n, 11)),
                  tab, tab, tab],
        out_specs=[pl.BlockSpec((tr, AW), lambda n: (n, 0)), pl.BlockSpec((tr, KVW), lambda n: (n, 0)),
                   pl.BlockSpec((tr, KV2), lambda n: (n, 0)), pl.BlockSpec((tr, KV2), lambda n: (n, 0))],
        out_shape=[jax.ShapeDtypeStruct((T, AW), BF16), jax.ShapeDtypeStruct((T, KVW), BF16),
                   jax.ShapeDtypeStruct((T, KV2), BF16), jax.ShapeDtypeStruct((T, KV2), BF16)],
        compiler_params=_cp("parallel"),
    )(proj, proj, proj, proj, *tabs)


GROUP = 4


def _attn_mask(n, reps):
    qi = lax.broadcasted_iota(jnp.int32, (reps * BLK, BLK), 0) & (BLK - 1)
    kj = lax.broadcasted_iota(jnp.int32, (reps * BLK, BLK), 1)
    m0 = (kj >= PAD) & (n >= 1)
    mp = (kj > qi) & (n >= 2)
    mc = (kj <= qi) & ((n >= 1) | (kj >= PAD))
    return jnp.concatenate([m0, mp, mc], axis=1)


def _kv_both(x0_ref, xp_ref, xc_ref, g):
    if x0_ref.shape[1] == KV2:
        cols = slice(128 * g, 128 * g + 128)
        return jnp.concatenate([x0_ref[:, cols], xp_ref[:, cols], xc_ref[:, cols]], axis=0)
    pg, off = g // 2, g % 2
    cols = slice(128 * pg, 128 * pg + 128)
    x = jnp.concatenate([x0_ref[:, cols], xp_ref[:, cols], xc_ref[:, cols]], axis=0).astype(F32)
    lane = lax.broadcasted_iota(jnp.int32, (1, 128), 1)
    half = jnp.where((lane < 64) if off == 0 else (lane >= 64), x, 0.0)
    return (half + pltpu.roll(half, 64, axis=1)).astype(BF16)


def _stack_heads(a, b):
    lo = lax.broadcasted_iota(jnp.int32, (1, 128), 1) < 64
    a, b = a.astype(F32), b.astype(F32)
    return jnp.concatenate([jnp.where(lo, a, 0.0), jnp.where(lo, 0.0, a),
                            jnp.where(lo, b, 0.0), jnp.where(lo, 0.0, b)], axis=0).astype(BF16)


def _unstack_heads(x):
    lo = lax.broadcasted_iota(jnp.int32, (1, 128), 1) < 64
    return (jnp.where(lo, x[0:BLK], x[BLK:2 * BLK]), jnp.where(lo, x[2 * BLK:3 * BLK], x[3 * BLK:4 * BLK]))


def _per_head_column(values):
    return jnp.concatenate([jnp.zeros((BLK, 1), F32) + v for v in values], axis=0)


def _attn_fwd(qr, kr, vb, proj, sinks, l, job=None):
    T = qr.shape[0]

    def body(sink_ref, q_ref, k0_ref, kp_ref, kc_ref, v0_ref, vp_ref, vc_ref, ag_ref, ya_ref, att_ref, lse_ref):
        n = pl.program_id(0)
        mask = _attn_mask(n, 1)
        lane = lax.broadcasted_iota(jnp.int32, (1, 128), 1)
        lse_acc = jnp.zeros((BLK, 128), F32)
        for g in range(4):
            kx = _kv_both(k0_ref, kp_ref, kc_ref, g)
            vx = _kv_both(v0_ref, vp_ref, vc_ref, g)
            pair_cols = [slice(128 * (2 * g + pp), 128 * (2 * g + pp) + 128) for pp in range(2)]
            s4 = _dot_nt(_stack_heads(q_ref[:, pair_cols[0]], q_ref[:, pair_cols[1]]), kx)
            probs = []
            for r in range(GROUP):
                h = GROUP * g + r
                sink = sink_ref[l, h]
                s = jnp.where(mask, s4[BLK * r:BLK * r + BLK], NEG_INF)
                m = jnp.maximum(jnp.max(s, axis=1, keepdims=True), sink)
                p = jnp.exp(s - m)
                denom = jnp.sum(p, axis=1, keepdims=True) + jnp.exp(sink - m)
                probs.append((p * (1.0 / denom)).astype(BF16))
                lse_acc = jnp.where(lane == h, m + jnp.log(denom), lse_acc)
            outs = _unstack_heads(_dot(jnp.concatenate(probs, axis=0), vx))
            for cols, out in zip(pair_cols, outs):
                att_ref[:, cols] = out
                gate, _ = _silu_and_grad(ag_ref[:, cols])
                ya_ref[:, cols] = (out * gate).astype(BF16)
        lse_ref[...] = lse_acc

    prev = lambda n: (jnp.maximum(n - 1, 0), 0)
    cur = lambda n: (n, 0)
    zero = lambda n: (0, 0)
    kv = lambda f: pl.BlockSpec((BLK, KV2), f)
    return _side_call(
        body, job, name=f"attn_fwd{l}", grid=(T // BLK,),
        in_specs=[pl.BlockSpec(memory_space=pltpu.SMEM),
                  pl.BlockSpec((BLK, AW), cur), kv(zero), kv(prev), kv(cur), kv(zero), kv(prev), kv(cur),
                  pl.BlockSpec((BLK, AW), lambda n: (n, 3))],
        out_specs=[pl.BlockSpec((BLK, AW), cur), pl.BlockSpec((BLK, AW), cur), pl.BlockSpec((BLK, 128), cur)],
        out_shape=[jax.ShapeDtypeStruct((T, AW), BF16), jax.ShapeDtypeStruct((T, AW), F32),
                   jax.ShapeDtypeStruct((T, 128), F32)],
        scratch_shapes=[], semantics=("parallel",), args=[sinks, qr, kr, kr, kr, vb, vb, vb, proj])


def _attn_bwd(qr, kr, vb, proj, att, lse, d_ya, sinks, dproj, l, job=None):
    T = qr.shape[0]
    nb = T // BLK

    def body(sink_ref, q_ref, k0_ref, kp_ref, kc_ref, v0_ref, vp_ref, vc_ref, ag_ref, att_ref, lse_ref, dy_ref, _,
             dq_ref, dk_ref, dv_ref, dk0_ref, dv0_ref, dag_ref, dsink_ref, kcarry, vcarry):
        n = pl.program_id(0)

        @pl.when(n == 0)
        def _():
            dk0_ref[...] = jnp.zeros_like(dk0_ref)
            dv0_ref[...] = jnp.zeros_like(dv0_ref)
            dsink_ref[...] = jnp.zeros_like(dsink_ref)
            kcarry[...] = jnp.zeros_like(kcarry)
            vcarry[...] = jnp.zeros_like(vcarry)

        @pl.when(n == nb)
        def _():
            dk_ref[...] = kcarry[...]
            dv_ref[...] = vcarry[...]

        @pl.when(n < nb)
        def _():
            mask = _attn_mask(n, GROUP)
            lane = lax.broadcasted_iota(jnp.int32, (1, 128), 1)
            lse = lse_ref[...]
            dsink = jnp.zeros((1, 128), F32)
            dk_pg, dv_pg = [], []
            for pg in range(2):
                dk_acc = jnp.zeros((3 * BLK, 128), F32)
                dv_acc = jnp.zeros((3 * BLK, 128), F32)
                for off in range(2):
                    g = 2 * pg + off
                    kx = _kv_both(k0_ref, kp_ref, kc_ref, g)
                    vx = _kv_both(v0_ref, vp_ref, vc_ref, g)
                    pair_cols = [slice(128 * (2 * g + pp), 128 * (2 * g + pp) + 128) for pp in range(2)]
                    q4 = _stack_heads(q_ref[:, pair_cols[0]], q_ref[:, pair_cols[1]])
                    d_out = []
                    for cols in pair_cols:
                        gate, dgate = _silu_and_grad(ag_ref[:, cols])
                        dy = dy_ref[:, cols]
                        dag_ref[:, cols] = (dy * att_ref[:, cols] * dgate).astype(BF16)
                        d_out.append(dy * gate)
                    do4 = _stack_heads(d_out[0], d_out[1])
                    heads = [GROUP * g + r for r in range(GROUP)]
                    sink = _per_head_column([sink_ref[l, h] for h in heads])
                    lse4 = _per_head_column(
                        [jnp.sum(jnp.where(lane == h, lse, 0.0), axis=1, keepdims=True) for h in heads])
                    p = jnp.where(mask, jnp.exp(_dot_nt(q4, kx) - lse4), 0.0)
                    dp = _dot_nt(do4, vx)
                    delta = jnp.sum(p * dp, axis=1, keepdims=True)
                    ds = (p * (dp - delta)).astype(BF16)
                    sink_term = jnp.exp(sink - lse4) * delta
                    for r, h in enumerate(heads):
                        dsink += jnp.where(lane == h, -jnp.sum(sink_term[BLK * r:BLK * r + BLK]), 0.0)
                    for cols, dq in zip(pair_cols, _unstack_heads(_dot(ds, kx))):
                        dq_ref[:, cols] = dq
                    dkg = _dot_tn(ds, q4)
                    dvg = _dot_tn(p.astype(BF16), do4)
                    own = (lane < 64) if off == 0 else (lane >= 64)
                    dk_acc += jnp.where(own, dkg + pltpu.roll(dkg, 64, axis=1), 0.0)
                    dv_acc += jnp.where(own, dvg + pltpu.roll(dvg, 64, axis=1), 0.0)
                dk_pg.append(dk_acc)
                dv_pg.append(dv_acc)
            dsink_ref[...] += dsink
            for pg in range(2):
                cols = slice(128 * pg, 128 * pg + 128)
                dk0_ref[:, cols] += dk_pg[pg][0:BLK]
                dv0_ref[:, cols] += dv_pg[pg][0:BLK]
                dk_ref[:, cols] = kcarry[:, cols] + dk_pg[pg][BLK:2 * BLK]
                dv_ref[:, cols] = vcarry[:, cols] + dv_pg[pg][BLK:2 * BLK]
                kcarry[:, cols] = dk_pg[pg][2 * BLK:3 * BLK]
                vcarry[:, cols] = dv_pg[pg][2 * BLK:3 * BLK]

    last = nb - 1
    cur = lambda n: (jnp.minimum(n, last), 0)
    prev = lambda n: (jnp.clip(n - 1, 0, last), 0)
    zero = lambda n: (0, 0)
    kv = lambda f: pl.BlockSpec((BLK, KVW), f)
    kin = lambda a, f: pl.BlockSpec((BLK, a.shape[1]), f)
    wide = lambda f: pl.BlockSpec((BLK, AW), f)
    return _side_call(
        body, job, name=f"attn_bwd{l}", grid=(nb + 1,),
        in_specs=[pl.BlockSpec(memory_space=pltpu.SMEM),
                  wide(cur), kin(kr, zero), kin(kr, prev), kin(kr, cur), kin(vb, zero), kin(vb, prev), kin(vb, cur),
                  pl.BlockSpec((BLK, AW), lambda n: (jnp.minimum(n, last), 3)),
                  wide(cur), pl.BlockSpec((BLK, 128), cur), wide(cur), pl.BlockSpec(memory_space=pl.ANY)],
        out_specs=[wide(cur), kv(prev), kv(prev), kv(zero), kv(zero),
                   pl.BlockSpec((BLK, AW), lambda n: (jnp.minimum(n, last), 3)),
                   pl.BlockSpec((1, 128), zero)],
        out_shape=[jax.ShapeDtypeStruct((T, AW), F32), jax.ShapeDtypeStruct((T, KVW), F32),
                   jax.ShapeDtypeStruct((T, KVW), F32), jax.ShapeDtypeStruct((BLK, KVW), F32),
                   jax.ShapeDtypeStruct((BLK, KVW), F32), jax.ShapeDtypeStruct(dproj.shape, BF16),
                   jax.ShapeDtypeStruct((1, 128), F32)],
        scratch_shapes=[pltpu.VMEM((BLK, KVW), F32), pltpu.VMEM((BLK, KVW), F32)],
        semantics=("arbitrary",), aliases={12: 5},
        args=[sinks, qr, kr, kr, kr, vb, vb, vb, proj, att, lse, d_ya, dproj])


def _rope_bwd(dqr, dk, dv, dk0, dv0, tabs, dproj, l):
    T = dqr.shape[0]

    def body(dq_ref, dk_ref, dv_ref, dk0_ref, dv0_ref, c_ref, s1_ref, s2_ref, _, o_ref):
        n = pl.program_id(0)
        c, s1, s2 = c_ref[...], s1_ref[...], s2_ref[...]
        for gcol in range(AW // 128):
            cols = slice(128 * gcol, 128 * gcol + 128)
            o_ref[:, cols] = (_rot_bwd(dq_ref[:, cols], c, s1, s2) * 0.125).astype(BF16)
        for gcol in range(KVW // 128):
            cols = slice(128 * gcol, 128 * gcol + 128)
            kcols = slice(AW + 128 * gcol, AW + 128 * gcol + 128)
            vcols = slice(AW + KVW + 128 * gcol, AW + KVW + 128 * gcol + 128)
            o_ref[:, kcols] = _rot_bwd(dk_ref[:, cols], c, s1, s2).astype(BF16)
            o_ref[:, vcols] = dv_ref[:, cols].astype(BF16)

            @pl.when(n == 0)
            def _():
                dkk = dk_ref[0:BLK, cols] + dk0_ref[:, cols]
                o_ref[0:BLK, kcols] = _rot_bwd(dkk, c[0:BLK], s1[0:BLK], s2[0:BLK]).astype(BF16)
                o_ref[0:BLK, vcols] = (dv_ref[0:BLK, cols] + dv0_ref[:, cols]).astype(BF16)

    tr = _pick(T, (384, 128))
    cur = lambda n: (n, 0)
    zero = lambda n: (0, 0)
    tab = pl.BlockSpec((tr, 128), cur)
    return pl.pallas_call(
        body, name=f"rope_bwd{l}", grid=(T // tr,),
        in_specs=[pl.BlockSpec((tr, AW), cur), pl.BlockSpec((tr, KVW), cur), pl.BlockSpec((tr, KVW), cur),
                  pl.BlockSpec((BLK, KVW), zero), pl.BlockSpec((BLK, KVW), zero), tab, tab, tab,
                  pl.BlockSpec(memory_space=pl.ANY)],
        out_specs=pl.BlockSpec((tr, AW + 2 * KVW), lambda n: (n, 1)),
        out_shape=jax.ShapeDtypeStruct(dproj.shape, BF16),
        input_output_aliases={8: 0},
        compiler_params=_cp("parallel"),
    )(dqr, dk, dv, dk0, dv0, *tabs, dproj)


def _block_diag(w):
    nl, nh, hd, _ = w.shape
    eye = jnp.eye(nh, dtype=w.dtype)
    return jnp.einsum("lhij,hg->lhigj", w, eye).reshape(nl, nh * hd, nh * hd)


def _diag_blocks(m):
    nh, hd = 8, 64
    return jnp.einsum("hihj->hij", m.reshape(nh, hd, nh, hd))


def _device_step(x, target, p, dist=None):
    vec = lambda a: a.reshape(DEPTH, 1, a.shape[-1])
    ln_in_g, ln_in_b = p["ln_in_g"].reshape(1, D), p["ln_in_b"].reshape(1, D)
    conv_dw_b, conv_ln_g, conv_ln_b, conv_pw_b = map(vec, (p["conv_dw_b"], p["conv_ln_g"], p["conv_ln_b"], p["conv_pw_b"]))
    lru_conv_b, lru_ba, lru_bx, lru_lambda = map(vec, (p["lru_conv_b"], p["lru_ba"], p["lru_bx"], p["lru_lambda"]))
    ln_post_g, ln_post_b = vec(p["ln_post_g"]), vec(p["ln_post_b"])
    wa_bd = _block_diag(p["lru_wa"]).astype(BF16)
    wx_bd = _block_diag(p["lru_wx"]).astype(BF16)
    w_in, w_out, pw_w = list(p["w_in"]), list(p["w_out"]), list(p["conv_pw_w"])
    sinks = p["attn_sinks"]
    big_names = ("w_in", "w_out", "conv_pw_w")

    order = dist[4] if dist else jnp.arange(N_SHARD, dtype=jnp.int32)
    (h, hb), got = _embed_fwd(x, p["meta_tokens"], ln_in_g, ln_in_b,
                              job=_gather_job([w_in[0]], peers=(0, 1)) if dist else None)
    if dist:
        w_in[0] = got[0]
    T = h.shape[0]
    tabs = _rope_tables(T)
    saved = []
    for l in range(DEPTH):
        if l == 0:
            job = _join_jobs(_gather_job([w_in[0]], peers=(2,)), _gather_job([pw_w[0]])) if dist else None
            (proj,), got = _proj_fwd(hb, w_in[0], order, 0, N_SHARD - 1, None, l, job=job)
            if dist:
                w_in[0], pw_w[0] = got
            (proj,), _ = _proj_fwd(hb, w_in[0], order, N_SHARD - 1, 1, proj, l)
        else:
            (proj,), _ = _proj_fwd(hb, w_in[l], order, 0, N_SHARD, None, l)
        pw_l = pw_w[l].reshape(CW, CW)
        (yc, conv), got = _conv_fwd(proj, p["conv_dw_w"], conv_dw_b, conv_ln_g, conv_ln_b, pw_l, conv_pw_b, l,
                                    job=_gather_job([w_out[0]]) if dist and l == 0 else None)
        if got:
            w_out[0] = got[0]
        qr, vb, kr2, vb2 = _rope_fwd(proj, tabs, l)
        job = None
        if dist and l == 0:
            job = _join_jobs(_gather_job([w_in[1]], peers=(2,)), _gather_job([w_out[1], pw_w[1]]))
        (ya, att, lse), got = _attn_fwd(qr, kr2, vb2, proj, sinks, l, job=job)
        if got:
            w_in[1], w_out[1], pw_w[1] = got
        (yl, hl), _ = _lru_fwd(proj, p["lru_conv_w"], lru_conv_b, wa_bd, lru_ba, wx_bd, lru_bx, lru_lambda, l)
        (hn, hnb, xhat, rstd), got = _out_fwd(
            yc, ya, yl, w_out[l], h, ln_post_g, ln_post_b, l,
            job=_gather_job([w_in[1]], peers=(0, 1)) if dist and l == 0 else None)
        if got:
            w_in[1] = got[0]
        saved.append((hb, proj, yc, conv, qr, kr2, vb, ya, att, lse, yl, hl, xhat, rstd, pw_l))
        h, hb = hn, hnb

    dh = None
    g = {}
    later = None
    early, last = ("w_out", "conv_pw_w"), ("w_in",)
    own = {}
    for l in reversed(range(DEPTH)):
        hb_l, proj, yc, conv, qr, kr, vb, ya, att, lse, yl, hl, xhat, rstd, pw_l = saved[l]
        tail = dist is not None and l == 0
        top = l == DEPTH - 1
        (part, dz, dzb, g["ln_post_g", l], g["ln_post_b", l], d_ya, d_yl, d_conv, dproj, dpw, g["conv_pw_b", l],
         g["conv_ln_g", l], g["conv_ln_b", l]), recv = _post_ln_dcat_bwd(
            h if top else dh, target if top else None, xhat, rstd, ln_post_g, w_out[l],
            conv, proj, conv_ln_g, conv_ln_b, pw_l, conv_pw_b, l,
            job=_swap_job(later["grads"]) if later else None)
        if top:
            loss_part = part
        if later:
            later["parts"], later["owns"] = _chip_partials(big_names, later["grads"], recv, dist, later["l"])
        g["w_out", l] = _dwout_bwd(yc, ya, yl, dzb, l)
        g["conv_pw_w", l] = dpw.reshape(N_SHARD, 2, PW_SH // 2, CW)
        if tail:
            own["early"] = dict(l=0, grads=[g[name, 0] for name in early])
        job = None
        if tail:
            job = _join_jobs(_swap_job(own["early"]["grads"]), _scatter_job(later["parts"][1:]))
        (dproj, ddw, g["conv_dw_b", l]), got = _conv_bwd_taps(d_conv, proj, p["conv_dw_w"], dproj, l, job=job)
        if tail:
            n_early = len(early)
            own["early"]["parts"], own["early"]["owns"] = _chip_partials(
                early, own["early"]["grads"], got[:n_early], dist, 0)
            later["z"] = got[n_early:]
        g["conv_dw_w", l] = ddw[:CONV_K]
        (dqr, dk, dv, dk0, dv0, dproj, dsink), z = _attn_bwd(
            qr, kr, vb, proj, att, lse, d_ya, sinks, dproj, l,
            job=_scatter_job(later["parts"][:1]) if later else None)
        if later:
            later["z"] = z + later["z"]
        g["attn_sinks", l] = dsink[0, :N_HEADS]
        dproj = _rope_bwd(dqr, dk, dv, dk0, dv0, tabs, dproj, l)
        (dproj, dlw, g["lru_conv_b", l], dwa, g["lru_ba", l], dwx, g["lru_bx", l], g["lru_lambda", l]), z = _lru_bwd(
            proj, hl, d_yl, p["lru_conv_w"], lru_conv_b, wa_bd, lru_ba, wx_bd, lru_bx, lru_lambda, dproj, l,
            job=_scatter_job(own["early"]["parts"]) if tail else None)
        if tail:
            own["early"]["z"] = z
        g["lru_conv_w", l] = dlw[:LRU_K]
        g["lru_wa", l] = _diag_blocks(dwa)
        g["lru_wx", l] = _diag_blocks(dwx)
        job = None
        if l > 0:
            g["w_in", l] = _dwin_bwd(hb_l, dproj, l)
        else:
            c = dist[0] if dist else jnp.int32(0)
            job = None
            if dist:
                pack_a = _pack_rows([_layer_stack(g, name) for name in _SMALL_LAYERED])
                totals = _shard_totals(big_names, later, dist)
                job = _join_jobs(_share_job(totals), _spread_job(pack_a))
            (give,), got = _dwin_half(hb_l, dproj, 1 - c, l, "give", job=job)
            (keep,), recv = _dwin_half(hb_l, dproj, c, l, "keep", job=_send_job([give]) if dist else None)
            job = None
            if dist:
                _store_reduced(big_names, later["l"], got[:-1], g)
                later = None
                g["pack_layered", -1] = _sum_slots(pack_a, got[-1], dist[3], "layered")
                own["last"] = dict(l=0)
                own["last"]["parts"], own["last"]["owns"] = _chip_partials(
                    last, [keep.reshape(N_SHARD, 1, D // 2, WIN_SH)], recv, (jnp.int32(0),) + tuple(dist[1:]), 0)
                job = _scatter_job(own["last"]["parts"])
            else:
                g["w_in", l] = jnp.stack([keep, give], axis=1)
        (dh,), got = _dh_bwd(dproj, [w_in[l]], dz, l, job=job)
        if tail:
            own["last"]["z"] = got
        if dist and l > 0:
            later = dict(l=l, grads=[g[name, l] for name in big_names])
    grad_x, g["meta_tokens", -1], g["ln_in_g", -1], g["ln_in_b", -1] = _embed_bwd(
        dh, x, p["meta_tokens"], ln_in_g, ln_in_b)
    if dist:
        pack_b = _pack_rows([g[name, -1] for name in _SMALL_EMBED])
        state = dict(l=0, owns=own["last"]["owns"] + own["early"]["owns"], z=own["last"]["z"] + own["early"]["z"])
        totals = _shard_totals(last + early, state, dist)
        got = _run_job(_join_jobs(_share_job(totals), _spread_job(pack_b)), "share_and_spread")
        _store_reduced(last + early, 0, got[:-1], g)
        g["pack_embed", -1] = _sum_slots(pack_b, got[-1], dist[3], "embed")
    return loss_part, grad_x, g


_SMALL_EMBED = ("meta_tokens", "ln_in_g", "ln_in_b")
_SMALL_LAYERED = ("conv_dw_w", "conv_dw_b", "conv_ln_g", "conv_ln_b", "conv_pw_b", "attn_sinks", "lru_conv_w",
                  "lru_conv_b", "lru_wa", "lru_ba", "lru_wx", "lru_bx", "lru_lambda", "ln_post_g", "ln_post_b")


def _layer_stack(g, name):
    return jnp.stack([g[name, l] for l in range(DEPTH)], axis=0)


def _chip_partials(names, grads, recv, dist, l):
    outs = [_chip_partial(a, r, dist[0], dist[1], f"{name}{l}") for name, a, r in zip(names, grads, recv)]
    return [o[0] for o in outs], [o[1] for o in outs]


def _shard_totals(names, state, dist):
    l = state["l"]
    return [_shard_total(po, zz, dist[2], f"{name}{l}") for name, po, zz in zip(names, state["owns"], state["z"])]


def _store_reduced(names, l, full, g):
    for name, f in zip(names, full):
        g[name, l] = f.reshape(2 * f.shape[1], f.shape[2])


MESH = pl.DeviceIdType.MESH
HBM_SPEC = pl.BlockSpec(memory_space=pltpu.HBM)
N_DEV = 8


def _position():
    x, y, c = lax.axis_index("x"), lax.axis_index("y"), lax.axis_index("c")
    return x, y, c


def _other_chips(x, y):
    return [(1 - x, y), (x, 1 - y), (1 - x, 1 - y)]


def _cast_into_slot(a, l, j, tag):
    _, R, C = a.shape
    tb = _pick(R, (512, 128))

    def body(s_ref, a_ref, o_ref):
        o_ref[...] = a_ref[...].astype(BF16)

    grid_spec = pltpu.PrefetchScalarGridSpec(
        num_scalar_prefetch=1, grid=(R // tb,),
        in_specs=[pl.BlockSpec((None, tb, C), lambda t, sc: (l, t, 0))],
        out_specs=pl.BlockSpec((None, tb, C), lambda t, sc: (sc[0], t, 0)))
    return pl.pallas_call(
        body, name=f"cast_into_slot_{tag}{l}", grid_spec=grid_spec,
        out_shape=jax.ShapeDtypeStruct((N_SHARD, R, C), BF16),
        compiler_params=_cp("arbitrary"),
    )(jnp.reshape(j, (1,)).astype(jnp.int32), a)


class _Job:
    def __init__(self, inputs, aliased, extra_out, sems, start, mid, finish):
        self.inputs, self.extra_out, self.sems = list(inputs), list(extra_out), list(sems)
        self.n_aliased = len(self.inputs) if aliased is True else int(aliased)
        self.start, self.mid, self.finish = start, mid, finish

    def out_shapes(self):
        return [jax.ShapeDtypeStruct(a.shape, a.dtype) for a in self.inputs[:self.n_aliased]] + self.extra_out


def _side_call(body, job, *, name, grid, in_specs, out_specs, out_shape, scratch_shapes, semantics, args,
               aliases=None, prefetch=()):
    aliases = dict(aliases or {})
    n_pre = len(prefetch)

    def call(fn, ins, outs, shapes, scratch, sem, operands):
        if n_pre:
            spec = pltpu.PrefetchScalarGridSpec(num_scalar_prefetch=n_pre, grid=grid, in_specs=ins, out_specs=outs,
                                                scratch_shapes=scratch)
            return pl.pallas_call(fn, name=name, grid_spec=spec, out_shape=shapes,
                                  input_output_aliases={k + n_pre: v for k, v in aliases.items()},
                                  compiler_params=_cp(*sem))(*prefetch, *operands)
        return pl.pallas_call(fn, name=name, grid=grid, in_specs=ins, out_specs=outs, out_shape=shapes,
                              scratch_shapes=scratch, input_output_aliases=aliases,
                              compiler_params=_cp(*sem))(*operands)

    if job is None:
        return list(call(body, list(in_specs), list(out_specs), list(out_shape), list(scratch_shapes),
                         semantics, args)), []
    n_in, n_out, n_scr = len(in_specs), len(out_specs), len(scratch_shapes)
    j_in, j_out = len(job.inputs), len(job.out_shapes())
    steps = 1
    for gsize in grid:
        steps *= gsize

    def wrapped(*refs):
        pre, refs = refs[:n_pre], refs[n_pre:]
        host_in, job_in = refs[:n_in], refs[n_in:n_in + j_in]
        o0 = n_in + j_in
        host_out, job_out = refs[o0:o0 + n_out], refs[o0 + n_out:o0 + n_out + j_out]
        s0 = o0 + n_out + j_out
        host_scr, sems = refs[s0:s0 + n_scr], refs[s0 + n_scr:]
        step = pl.program_id(0)
        for d in range(1, len(grid)):
            step = step * grid[d] + pl.program_id(d)

        @pl.when(step == 0)
        def _():
            job.start(job_in, job_out, sems)

        @pl.when(step == max(steps - 2, 0))
        def _():
            job.mid(job_in, job_out, sems)

        body(*pre, *host_in, *host_out, *host_scr)

        @pl.when(step == steps - 1)
        def _():
            job.finish(job_in, job_out, sems)

    aliases.update({n_in + k: n_out + k for k in range(job.n_aliased)})
    outs = call(wrapped, list(in_specs) + [HBM_SPEC] * j_in, list(out_specs) + [HBM_SPEC] * j_out,
                list(out_shape) + job.out_shapes(), list(scratch_shapes) + job.sems,
                ["arbitrary"] * len(grid), [*args, *job.inputs])
    return list(outs[:n_out]), list(outs[n_out:])


def _run_job(job, name):
    return _side_call(lambda: None, job, name=name, grid=(1,), in_specs=[], out_specs=[], out_shape=[],
                      scratch_shapes=[], semantics=("arbitrary",), args=[])[1]


def _gather_job(slots, peers=(0, 1, 2)):
    n = len(slots)

    def copies(buf, sems):
        ici_send, ici_recv, d2d_send, d2d_recv = sems
        x, y, c = _position()
        chips = _other_chips(x, y)

        def half(k, slot, which):
            hr = buf[k].shape[1] // 2
            return buf[k].at[slot, pl.ds(pl.multiple_of(which * hr, hr), hr)]

        def over_ici(k, p, slot):
            px, py = chips[p]
            return pltpu.make_async_remote_copy(
                src_ref=half(k, slot, c), dst_ref=half(k, slot, c),
                send_sem=ici_send.at[k * 3 + p], recv_sem=ici_recv.at[k * 3 + p],
                device_id=(px, py, c), device_id_type=MESH)

        def over_d2d(k, p, which):
            px, py = chips[p]
            return pltpu.make_async_remote_copy(
                src_ref=half(k, 2 * px + py, which), dst_ref=half(k, 2 * px + py, which),
                send_sem=d2d_send.at[k * 3 + p], recv_sem=d2d_recv.at[k * 3 + p],
                device_id=(x, y, 1 - c), device_id_type=MESH)

        return over_ici, over_d2d, 2 * x + y, chips, c

    pairs = [(k, p) for k in range(n) for p in peers]

    def start(_, buf, sems):
        over_ici, _, mine, _, _ = copies(buf, sems)
        for k, p in pairs:
            over_ici(k, p, mine).start()

    def mid(_, buf, sems):
        over_ici, over_d2d, _, chips, c = copies(buf, sems)
        for k, p in pairs:
            px, py = chips[p]
            over_ici(k, p, 2 * px + py).wait_recv()
            over_d2d(k, p, c).start()

    def finish(_, buf, sems):
        over_ici, over_d2d, mine, _, c = copies(buf, sems)
        for k, p in pairs:
            over_d2d(k, p, 1 - c).wait_recv()
        for k, p in pairs:
            over_ici(k, p, mine).wait_send()
            over_d2d(k, p, c).wait_send()

    return _Job(slots, True, [], [pltpu.SemaphoreType.DMA((3 * n,))] * 4, start, mid, finish)


def _gather_shards(shards):
    n = len(shards)

    def body(*refs):
        src, dst = refs[:n], refs[n:2 * n]
        send_sems, recv_sems, local_sems = refs[2 * n:]
        x, y, c = _position()
        mine = 2 * x + y
        chips = _other_chips(x, y)

        def copy(k, p):
            return pltpu.make_async_remote_copy(
                src_ref=src[k], dst_ref=dst[k].at[mine],
                send_sem=send_sems.at[k * 3 + p], recv_sem=recv_sems.at[k * 3 + p],
                device_id=(*chips[p], c), device_id_type=MESH)

        def arrival(k, p):
            px, py = chips[p]
            return pltpu.make_async_remote_copy(
                src_ref=src[k], dst_ref=dst[k].at[2 * px + py],
                send_sem=send_sems.at[k * 3 + p], recv_sem=recv_sems.at[k * 3 + p],
                device_id=(px, py, c), device_id_type=MESH)

        local = [pltpu.make_async_copy(src[k], dst[k].at[mine], local_sems.at[k]) for k in range(n)]
        for cp in local:
            cp.start()
        for k in range(n):
            for p in range(3):
                copy(k, p).start()
        for k in range(n):
            for p in range(3):
                arrival(k, p).wait_recv()
        for k in range(n):
            for p in range(3):
                copy(k, p).wait_send()
        for cp in local:
            cp.wait()

    return pl.pallas_call(
        body, name="gather_shards",
        in_specs=[HBM_SPEC] * n, out_specs=[HBM_SPEC] * n,
        out_shape=[jax.ShapeDtypeStruct((N_SHARD,) + s.shape, s.dtype) for s in shards],
        scratch_shapes=[pltpu.SemaphoreType.DMA((3 * n,)), pltpu.SemaphoreType.DMA((3 * n,)),
                        pltpu.SemaphoreType.DMA((n,))],
    )(*shards)


def _swap_job(grads):
    n = len(grads)

    def copies(src, dst, sems):
        x, y, c = _position()
        return [pltpu.make_async_remote_copy(
            src_ref=src[k].at[:, 1 - c], dst_ref=dst[k],
            send_sem=sems[0].at[k], recv_sem=sems[1].at[k],
            device_id=(x, y, 1 - c), device_id_type=MESH) for k in range(n)]

    def start(src, dst, sems):
        for cp in copies(src, dst, sems):
            cp.start()

    def finish(src, dst, sems):
        for cp in copies(src, dst, sems):
            cp.wait()

    return _Job(grads, False, [jax.ShapeDtypeStruct((N_SHARD,) + g.shape[2:], F32) for g in grads],
                [pltpu.SemaphoreType.DMA((n,))] * 2, start, lambda *_: None, finish)


def _send_job(arrays):
    n = len(arrays)

    def copies(src, dst, sems):
        x, y, c = _position()
        return [pltpu.make_async_remote_copy(
            src_ref=src[k], dst_ref=dst[k], send_sem=sems[0].at[k], recv_sem=sems[1].at[k],
            device_id=(x, y, 1 - c), device_id_type=MESH) for k in range(n)]

    def start(src, dst, sems):
        for cp in copies(src, dst, sems):
            cp.start()

    def finish(src, dst, sems):
        for cp in copies(src, dst, sems):
            cp.wait()

    return _Job(arrays, False, [jax.ShapeDtypeStruct(a.shape, a.dtype) for a in arrays],
                [pltpu.SemaphoreType.DMA((n,))] * 2, start, lambda *_: None, finish)


def _chip_partial(a, y, c, j, tag):
    _, _, R, C = a.shape
    tr = _pick(R, (256, 64))

    def body(s_ref, a_ref, y_ref, pb_ref, po_ref):
        total = a_ref[...] + y_ref[...]
        pb_ref[...] = total.astype(BF16)

        @pl.when(pl.program_id(1) == s_ref[1])
        def _():
            po_ref[...] = total

    grid_spec = pltpu.PrefetchScalarGridSpec(
        num_scalar_prefetch=1, grid=(R // tr, N_SHARD),
        in_specs=[pl.BlockSpec((None, None, tr, C), lambda t, s, sc: (s, sc[0], t, 0)),
                  pl.BlockSpec((None, tr, C), lambda t, s, sc: (s, t, 0))],
        out_specs=[pl.BlockSpec((None, tr, C), lambda t, s, sc: (s, t, 0)),
                   pl.BlockSpec((tr, C), lambda t, s, sc: (t, 0))])
    return pl.pallas_call(
        body, name=f"chip_partial_{tag}", grid_spec=grid_spec,
        out_shape=[jax.ShapeDtypeStruct((N_SHARD, R, C), BF16), jax.ShapeDtypeStruct((R, C), F32)],
        compiler_params=_cp("arbitrary", "arbitrary"),
    )(jnp.stack([c, j]).astype(jnp.int32), a, y)


def _scatter_job(parts):
    n = len(parts)
    pairs = [(k, p) for k in range(n) for p in range(3)]

    def copy(src, dst, sems, k, p, outgoing):
        x, y, c = _position()
        mine = 2 * x + y
        px, py = _other_chips(x, y)[p]
        theirs = 2 * px + py
        return pltpu.make_async_remote_copy(
            src_ref=src[k].at[theirs if outgoing else mine], dst_ref=dst[k].at[mine if outgoing else theirs],
            send_sem=sems[0].at[k * 3 + p], recv_sem=sems[1].at[k * 3 + p],
            device_id=(px, py, c), device_id_type=MESH)

    def start(src, dst, sems):
        for k, p in pairs:
            copy(src, dst, sems, k, p, True).start()

    def finish(src, dst, sems):
        for k, p in pairs:
            copy(src, dst, sems, k, p, False).wait_recv()
        for k, p in pairs:
            copy(src, dst, sems, k, p, True).wait_send()

    return _Job(parts, False, [jax.ShapeDtypeStruct(pb.shape, BF16) for pb in parts],
                [pltpu.SemaphoreType.DMA((3 * n,))] * 2, start, lambda *_: None, finish)


def _shard_total(own, z, others_c, tag):
    R, C = own.shape
    tr = _pick(R, (256, 64))

    def body(s_ref, o_ref, z0_ref, z1_ref, z2_ref, h_ref):
        h_ref[...] = ((o_ref[...] + z0_ref[...].astype(F32)) + z1_ref[...].astype(F32)) + z2_ref[...].astype(F32)

    zspec = lambda q: pl.BlockSpec((None, tr, C), lambda t, sc: (sc[q], t, 0))
    grid_spec = pltpu.PrefetchScalarGridSpec(
        num_scalar_prefetch=1, grid=(R // tr,),
        in_specs=[pl.BlockSpec((tr, C), lambda t, sc: (t, 0)), zspec(0), zspec(1), zspec(2)],
        out_specs=pl.BlockSpec((None, tr, C), lambda t, sc: (sc[3], t, 0)))
    return pl.pallas_call(
        body, name=f"shard_total_{tag}", grid_spec=grid_spec,
        out_shape=jax.ShapeDtypeStruct((2, R, C), F32),
        compiler_params=_cp("arbitrary"),
    )(others_c, own, z, z, z)


def _share_job(totals):
    n = len(totals)

    def copy(buf, sems, k, which):
        x, y, c = _position()
        return pltpu.make_async_remote_copy(
            src_ref=buf[k].at[which], dst_ref=buf[k].at[which],
            send_sem=sems[0].at[k], recv_sem=sems[1].at[k],
            device_id=(x, y, 1 - c), device_id_type=MESH)

    def start(_, buf, sems):
        c = lax.axis_index("c")
        for k in range(n):
            copy(buf, sems, k, c).start()

    def finish(_, buf, sems):
        c = lax.axis_index("c")
        for k in range(n):
            copy(buf, sems, k, 1 - c).wait_recv()
        for k in range(n):
            copy(buf, sems, k, c).wait_send()

    return _Job(totals, True, [], [pltpu.SemaphoreType.DMA((n,))] * 2, start, lambda *_: None, finish)


def _spread_job(pack):
    def copy(src, dst, sems, m, outgoing):
        x, y, c = _position()
        peer = (x ^ (m >> 2), y ^ ((m >> 1) & 1), c ^ (m & 1))
        slot = 4 * x + 2 * y + c if outgoing else 4 * peer[0] + 2 * peer[1] + peer[2]
        return pltpu.make_async_remote_copy(
            src_ref=src[0], dst_ref=dst[0].at[slot], send_sem=sems[0].at[m - 1], recv_sem=sems[1].at[m - 1],
            device_id=peer, device_id_type=MESH)

    def start(src, dst, sems):
        for m in range(1, N_DEV):
            copy(src, dst, sems, m, True).start()

    def finish(src, dst, sems):
        for m in range(1, N_DEV):
            copy(src, dst, sems, m, False).wait_recv()
        for m in range(1, N_DEV):
            copy(src, dst, sems, m, True).wait_send()

    return _Job([pack], False, [jax.ShapeDtypeStruct((N_DEV,) + pack.shape, F32)],
                [pltpu.SemaphoreType.DMA((N_DEV - 1,))] * 2, start, lambda *_: None, finish)


def _join_jobs(a, b):
    for job in (a, b):
        assert job.n_aliased in (0, len(job.inputs)) and not (job.n_aliased and job.extra_out)
    assert a.n_aliased or not b.n_aliased
    n_in, n_out, n_sem = len(a.inputs), len(a.out_shapes()), len(a.sems)

    def phase(name):
        def run(ins, outs, sems):
            getattr(a, name)(ins[:n_in], outs[:n_out], sems[:n_sem])
            getattr(b, name)(ins[n_in:], outs[n_out:], sems[n_sem:])
        return run

    return _Job(a.inputs + b.inputs, a.n_aliased + b.n_aliased, a.extra_out + b.extra_out, a.sems + b.sems,
                phase("start"), phase("mid"), phase("finish"))


def _sum_slots(pack, slots, me, tag):
    def body(me_ref, p_ref, s_ref, o_ref):
        acc = None
        for d in range(N_DEV):
            term = jnp.where(me_ref[0] == d, p_ref[...], s_ref[d])
            acc = term if acc is None else acc + term
        o_ref[...] = acc

    vm = pl.BlockSpec(memory_space=pltpu.VMEM)
    return pl.pallas_call(
        body, name=f"sum_slots_{tag}",
        in_specs=[pl.BlockSpec(memory_space=pltpu.SMEM), vm, vm], out_specs=vm,
        out_shape=jax.ShapeDtypeStruct(pack.shape, F32),
        compiler_params=pltpu.CompilerParams(vmem_limit_bytes=V7X_VMEM_LIMIT),
    )(jnp.reshape(me, (1,)).astype(jnp.int32), pack, slots)


def _pack_rows(arrays):
    total = sum(a.size for a in arrays)
    rows = -(-total // 128)
    rows = -(-rows // PACK_ROWS_ALIGN) * PACK_ROWS_ALIGN
    flat = [a.reshape(-1) for a in arrays] + [jnp.zeros((rows * 128 - total,), F32)]
    return jnp.concatenate(flat).reshape(rows, 128)


def _adamw_math(w, g, m, v):
    m = ADAM_B1 * m + (1.0 - ADAM_B1) * g
    v = ADAM_B2 * v + (1.0 - ADAM_B2) * (g * g)
    m_hat = m / (1.0 - ADAM_B1 ** ADAM_STEP)
    v_hat = v / (1.0 - ADAM_B2 ** ADAM_STEP)
    delta = -ADAM_LR * (m_hat / (jnp.sqrt(v_hat) + ADAM_EPS) + ADAM_WD * w)
    return delta, m, v


def _adamw_big(w, g0, g1, m, v, tag):
    _, R, C = w.shape
    tr = _pick(R, (256, 128))

    def body(w_ref, g0_ref, g1_ref, m_ref, v_ref, go_ref, d_ref, mo_ref, vo_ref):
        g = jnp.where(pl.program_id(0) == 0, g0_ref[...], g1_ref[...])
        delta, mn, vn = _adamw_math(w_ref[...], g, m_ref[...], v_ref[...])
        go_ref[...] = g
        d_ref[...] = delta
        mo_ref[...] = mn
        vo_ref[...] = vn

    s3 = pl.BlockSpec((None, tr, C), lambda l, t: (l, t, 0))
    g_spec = lambda layer: pl.BlockSpec((tr, C), lambda l, t: (jnp.where(l == layer, t, 0), 0))
    shp = jax.ShapeDtypeStruct(w.shape, F32)
    return pl.pallas_call(
        body, name=f"adamw_{tag}", grid=(2, R // tr),
        in_specs=[s3, g_spec(0), g_spec(1), s3, s3], out_specs=[s3, s3, s3, s3],
        out_shape=[shp, shp, shp, shp],
        compiler_params=_cp("parallel", "parallel"),
    )(w, g0, g1, m, v)


def _adamw_small(ws, gs, ms, vs):
    n = len(ws)

    def body(*refs):
        w_r, g_r, m_r, v_r = refs[:n], refs[n:2 * n], refs[2 * n:3 * n], refs[3 * n:4 * n]
        d_o, m_o, v_o = refs[4 * n:5 * n], refs[5 * n:6 * n], refs[6 * n:7 * n]
        for k in range(n):
            delta, mn, vn = _adamw_math(w_r[k][...], g_r[k][...], m_r[k][...], v_r[k][...])
            d_o[k][...] = delta
            m_o[k][...] = mn
            v_o[k][...] = vn

    vm = pl.BlockSpec(memory_space=pltpu.VMEM)
    shapes = [jax.ShapeDtypeStruct(w.shape, F32) for w in ws]
    outs = pl.pallas_call(
        body, name="adamw_small",
        in_specs=[vm] * (4 * n), out_specs=[vm] * (3 * n),
        out_shape=shapes * 3,
    )(*ws, *gs, *ms, *vs)
    return outs[:n], outs[n:2 * n], outs[2 * n:]


_WEIGHTS = ["meta_tokens", "ln_in_g", "ln_in_b", "w_in", "conv_dw_w", "conv_dw_b", "conv_ln_g", "conv_ln_b",
            "conv_pw_w", "conv_pw_b", "attn_sinks", "lru_conv_w", "lru_conv_b", "lru_wa", "lru_ba", "lru_wx",
            "lru_bx", "lru_lambda", "w_out", "ln_post_g", "ln_post_b"]
_BIG = ("w_in", "w_out", "conv_pw_w")
_SMALL_SHARDED = {"meta_tokens": 1, "conv_dw_w": 2, "lru_conv_w": 2}
PACK_ROWS_ALIGN = 8


def _as2d(a):
    return a.reshape(1, -1) if a.ndim == 1 else a.reshape(-1, a.shape[-1])


def kernel(x, meta_tokens, ln_in_g, ln_in_b, w_in, conv_dw_w, conv_dw_b, conv_ln_g, conv_ln_b, conv_pw_w, conv_pw_b, attn_sinks, lru_conv_w, lru_conv_b, lru_wa, lru_ba, lru_wx, lru_bx, lru_lambda, w_out, ln_post_g, ln_post_b, loss_target, m_meta_tokens, m_ln_in_g, m_ln_in_b, m_w_in, m_conv_dw_w, m_conv_dw_b, m_conv_ln_g, m_conv_ln_b, m_conv_pw_w, m_conv_pw_b, m_attn_sinks, m_lru_conv_w, m_lru_conv_b, m_lru_wa, m_lru_ba, m_lru_wx, m_lru_bx, m_lru_lambda, m_w_out, m_ln_post_g, m_ln_post_b, v_meta_tokens, v_ln_in_g, v_ln_in_b, v_w_in, v_conv_dw_w, v_conv_dw_b, v_conv_ln_g, v_conv_ln_b, v_conv_pw_w, v_conv_pw_b, v_attn_sinks, v_lru_conv_w, v_lru_conv_b, v_lru_wa, v_lru_ba, v_lru_wx, v_lru_bx, v_lru_lambda, v_w_out, v_ln_post_g, v_ln_post_b):
    w = dict(meta_tokens=meta_tokens, ln_in_g=ln_in_g, ln_in_b=ln_in_b, w_in=w_in, conv_dw_w=conv_dw_w,
             conv_dw_b=conv_dw_b, conv_ln_g=conv_ln_g, conv_ln_b=conv_ln_b, conv_pw_w=conv_pw_w,
             conv_pw_b=conv_pw_b, attn_sinks=attn_sinks, lru_conv_w=lru_conv_w, lru_conv_b=lru_conv_b,
             lru_wa=lru_wa, lru_ba=lru_ba, lru_wx=lru_wx, lru_bx=lru_bx, lru_lambda=lru_lambda, w_out=w_out,
             ln_post_g=ln_post_g, ln_post_b=ln_post_b)
    mom_m = dict(zip(_WEIGHTS, (m_meta_tokens, m_ln_in_g, m_ln_in_b, m_w_in, m_conv_dw_w, m_conv_dw_b, m_conv_ln_g,
                                m_conv_ln_b, m_conv_pw_w, m_conv_pw_b, m_attn_sinks, m_lru_conv_w, m_lru_conv_b,
                                m_lru_wa, m_lru_ba, m_lru_wx, m_lru_bx, m_lru_lambda, m_w_out, m_ln_post_g,
                                m_ln_post_b)))
    mom_v = dict(zip(_WEIGHTS, (v_meta_tokens, v_ln_in_g, v_ln_in_b, v_w_in, v_conv_dw_w, v_conv_dw_b, v_conv_ln_g,
                                v_conv_ln_b, v_conv_pw_w, v_conv_pw_b, v_attn_sinks, v_lru_conv_w, v_lru_conv_b,
                                v_lru_wa, v_lru_ba, v_lru_wx, v_lru_bx, v_lru_lambda, v_w_out, v_ln_post_g,
                                v_ln_post_b)))
    xi, yi, ci = _position()
    j = 2 * xi + yi

    g_meta, g_dw, g_lc = _gather_shards([meta_tokens, conv_dw_w, lru_conv_w])
    p = dict(w)
    p["w_in"] = [_cast_into_slot(w_in, l, j, "w_in") for l in range(DEPTH)]
    p["w_out"] = [_cast_into_slot(w_out, l, j, "w_out") for l in range(DEPTH)]
    p["conv_pw_w"] = [_cast_into_slot(conv_pw_w, l, j, "conv_pw_w") for l in range(DEPTH)]
    p["meta_tokens"] = g_meta.transpose(1, 0, 2).reshape(N_META, D)
    p["conv_dw_w"] = g_dw.transpose(1, 2, 0, 3).reshape(DEPTH, CONV_K, CW)
    p["lru_conv_w"] = g_lc.transpose(1, 2, 0, 3).reshape(DEPTH, LRU_K, LW)

    others = jnp.stack([jnp.where(j <= 0, 1, 0), jnp.where(j <= 1, 2, 1), jnp.where(j <= 2, 3, 2), ci]).astype(jnp.int32)
    me = 4 * xi + 2 * yi + ci
    order = jnp.stack([j, 2 * (1 - xi) + yi, 2 * xi + (1 - yi), 2 * (1 - xi) + (1 - yi)]).astype(jnp.int32)
    loss_part, grad_x, g = _device_step(x[0], loss_target[0], p, dist=(ci, j, others, me, order))
    loss = lax.psum(jnp.sum(loss_part), ("x", "y", "c"))
    big = {(name, l): g[name, l] for name in _BIG for l in range(DEPTH)}

    small_names = [n for n in _WEIGHTS if n not in _BIG]
    small_g = {}
    for names, red in ((_SMALL_LAYERED, g["pack_layered", -1]), (_SMALL_EMBED, g["pack_embed", -1])):
        red = red.reshape(-1)
        off = 0
        for n in names:
            fshape = list(w[n].shape)
            if n in _SMALL_SHARDED:
                fshape[_SMALL_SHARDED[n]] *= N_SHARD
            sz = 1
            for dim in fshape:
                sz *= dim
            full = red[off:off + sz].reshape(fshape)
            off += sz
            if n in _SMALL_SHARDED:
                ax = _SMALL_SHARDED[n]
                full = lax.dynamic_slice_in_dim(full, j * w[n].shape[ax], w[n].shape[ax], axis=ax)
            small_g[n] = full

    out_g, out_d, out_m, out_v = {}, {}, {}, {}
    for name in _BIG:
        shp = w[name].shape
        to3 = lambda a: a.reshape(DEPTH, -1, shp[-1])
        go, do, mo, vo = _adamw_big(to3(w[name]), big[name, 0], big[name, 1], to3(mom_m[name]), to3(mom_v[name]), name)
        out_g[name], out_d[name], out_m[name], out_v[name] = (a.reshape(shp) for a in (go, do, mo, vo))
    ds, ms, vs = _adamw_small([_as2d(w[n]) for n in small_names], [_as2d(small_g[n]) for n in small_names],
                              [_as2d(mom_m[n]) for n in small_names], [_as2d(mom_v[n]) for n in small_names])
    for n, d_, m_, v_ in zip(small_names, ds, ms, vs):
        out_g[n] = small_g[n]
        out_d[n], out_m[n], out_v[n] = d_.reshape(w[n].shape), m_.reshape(w[n].shape), v_.reshape(w[n].shape)

    return (loss, grad_x[None], *[out_g[n] for n in _WEIGHTS], *[out_d[n] for n in _WEIGHTS],
            *[out_m[n] for n in _WEIGHTS], *[out_v[n] for n in _WEIGHTS])
```
